```python
import jax, jax.numpy as jnp
from jax import lax
import numpy as np

D_MODEL = 1024
BATCH = 8
SEQ = 4096
DEPTH = 2

N_MIXERS = 2
N_ATTN_LAYERS = (DEPTH + 1) // 2
N_SGU_LAYERS = DEPTH // 2

HEAD_DIM = 64
N_Q_HEADS = D_MODEL // HEAD_DIM
N_KV_HEADS = N_Q_HEADS // 4
GQA_GROUP = N_Q_HEADS // N_KV_HEADS
WINDOW = 128
Q_BLOCK = WINDOW
ROPE_THETA = 10000.0
Q_WIDTH = N_Q_HEADS * HEAD_DIM
KV_WIDTH = N_KV_HEADS * HEAD_DIM
QKV_WIDTH = Q_WIDTH + 2 * KV_WIDTH

SGU_WIDTH = D_MODEL
SGU_GROUPS = 8
SGU_GROUP_DIM = SGU_WIDTH // SGU_GROUPS
SGU_CHUNK = 128

D_FF = ((8 * D_MODEL + 3 * 256 - 1) // (3 * 256)) * 256

EPS = 1e-6

kernel_name = "hybrid_swa_sink_gqa_chunked_sgu_swiglu"


def rmsnorm(x, g):
    xf = x.astype(jnp.float32)
    y = xf * lax.rsqrt(jnp.mean(xf * xf, axis=-1, keepdims=True) + EPS)
    return (y * g.astype(jnp.float32)).astype(x.dtype)


def layernorm(x, g, b):
    xf = x.astype(jnp.float32)
    mu = jnp.mean(xf, axis=-1, keepdims=True)
    var = jnp.mean(jnp.square(xf - mu), axis=-1, keepdims=True)
    y = (xf - mu) * lax.rsqrt(var + EPS)
    return (y * g.astype(jnp.float32) + b.astype(jnp.float32)).astype(x.dtype)


def rope(x, pos):
    half = HEAD_DIM // 2
    inv_freq = ROPE_THETA ** (-(jnp.arange(half, dtype=jnp.float32) * 2.0) / HEAD_DIM)
    ang = pos.astype(jnp.float32)[:, None] * inv_freq[None, :]
    cos = jnp.cos(ang)[None, :, None, :].astype(x.dtype)
    sin = jnp.sin(ang)[None, :, None, :].astype(x.dtype)
    x1, x2 = x[..., :half], x[..., half:]
    return jnp.concatenate([x1 * cos - x2 * sin, x2 * cos + x1 * sin], axis=-1)


def swa_sink_attention(h, w_qkv, b_qkv, sinks, w_o, b_o):
    B, S, _ = h.shape
    nb = S // Q_BLOCK
    T = Q_BLOCK
    qkv = h @ w_qkv + b_qkv
    q = qkv[..., :Q_WIDTH].reshape(B, S, N_Q_HEADS, HEAD_DIM)
    k = qkv[..., Q_WIDTH:Q_WIDTH + KV_WIDTH].reshape(B, S, N_KV_HEADS, HEAD_DIM)
    v = qkv[..., Q_WIDTH + KV_WIDTH:].reshape(B, S, N_KV_HEADS, HEAD_DIM)
    pos = jnp.arange(S, dtype=jnp.int32)
    q = rope(q, pos)
    k = rope(k, pos)
    qb = q.reshape(B, nb, T, N_KV_HEADS, GQA_GROUP, HEAD_DIM)
    kb = k.reshape(B, nb, T, N_KV_HEADS, HEAD_DIM)
    vb = v.reshape(B, nb, T, N_KV_HEADS, HEAD_DIM)
    pad = jnp.zeros_like(kb[:, :1])
    kk = jnp.concatenate([jnp.concatenate([pad, kb[:, :-1]], axis=1), kb], axis=2)
    vv = jnp.concatenate([jnp.concatenate([pad, vb[:, :-1]], axis=1), vb], axis=2)
    scale = HEAD_DIM ** -0.5
    s = jnp.einsum('bnqhgd,bnkhd->bnhgqk', qb, kk).astype(jnp.float32) * scale
    qpos = jnp.arange(T)[:, None] + T
    kpos = jnp.arange(2 * T)[None, :]
    band = (kpos <= qpos) & (qpos - kpos < WINDOW)
    blk = jnp.arange(nb)[:, None, None]
    valid = band[None] & ((blk * T + kpos[None] - T) >= 0)
    s = jnp.where(valid[None, :, None, None], s, -jnp.inf)
    sink = sinks.astype(jnp.float32).reshape(1, 1, N_KV_HEADS, GQA_GROUP, 1, 1)
    m = jnp.maximum(jnp.max(s, axis=-1, keepdims=True), sink)
    p = jnp.exp(s - m)
    denom = jnp.sum(p, axis=-1, keepdims=True) + jnp.exp(sink - m)
    probs = (p / denom).astype(vv.dtype)
    o = jnp.einsum('bnhgqk,bnkhd->bnqhgd', probs, vv).reshape(B, S, Q_WIDTH)
    return o @ w_o + b_o


def chunked_sgu(h, w_in, ln_g, ln_b, w_spatial, b_spatial, w_out):
    B, S, _ = h.shape
    nc = S // SGU_CHUNK
    z = jax.nn.gelu(h @ w_in)
    u, v = z[..., :SGU_WIDTH], z[..., SGU_WIDTH:]
    v = layernorm(v, ln_g, ln_b)
    vg = v.reshape(B, nc, SGU_CHUNK, SGU_GROUPS, SGU_GROUP_DIM)
    causal = jnp.tril(jnp.ones((SGU_CHUNK, SGU_CHUNK), dtype=w_spatial.dtype))
    ws = w_spatial * causal[None]
    mixed = jnp.einsum('gts,bnsgd->bntgd', ws, vg) + b_spatial.T[None, None, :, :, None]
    y = u * mixed.reshape(B, S, SGU_WIDTH)
    return y @ w_out


def swiglu(h, w_gate_up, w_down):
    gu = h @ w_gate_up
    return (jax.nn.silu(gu[..., :D_FF]) * gu[..., D_FF:]) @ w_down


def _fwd_setup_inputs(seed: int = 0) -> dict:
    key = jax.random.key(seed)
    ks = jax.random.split(key, 20)
    f32 = jnp.float32

    def nrm(k, shape, scale):
        return jax.random.normal(k, shape, f32) * scale

    def gain(k, shape):
        return 1.0 + 0.02 * jax.random.normal(k, shape, f32)

    NA, NS = N_ATTN_LAYERS, N_SGU_LAYERS
    return {
        "x": jax.random.normal(ks[0], (BATCH, SEQ, D_MODEL), f32),
        "norm_mix_pre": gain(ks[1], (DEPTH, D_MODEL)),
        "norm_mix_post": gain(ks[2], (DEPTH, D_MODEL)),
        "norm_ffn_pre": gain(ks[3], (DEPTH, D_MODEL)),
        "norm_ffn_post": gain(ks[4], (DEPTH, D_MODEL)),
        "attn_w_qkv": nrm(ks[5], (NA, D_MODEL, QKV_WIDTH), D_MODEL ** -0.5),
        "attn_b_qkv": nrm(ks[6], (NA, QKV_WIDTH), 0.02),
        "attn_sinks": nrm(ks[7], (NA, N_Q_HEADS), 1.0),
        "attn_w_o": nrm(ks[8], (NA, Q_WIDTH, D_MODEL), Q_WIDTH ** -0.5),
        "attn_b_o": nrm(ks[9], (NA, D_MODEL), 0.02),
        "sgu_w_in": nrm(ks[10], (NS, D_MODEL, 2 * SGU_WIDTH), D_MODEL ** -0.5),
        "sgu_ln_g": gain(ks[11], (NS, SGU_WIDTH)),
        "sgu_ln_b": nrm(ks[12], (NS, SGU_WIDTH), 0.02),
        "sgu_w_spatial": nrm(ks[13], (NS, SGU_GROUPS, SGU_CHUNK, SGU_CHUNK), SGU_CHUNK ** -0.5),
        "sgu_b_spatial": 1.0 + nrm(ks[14], (NS, SGU_GROUPS, SGU_CHUNK), 0.02),
        "sgu_w_out": nrm(ks[15], (NS, SGU_WIDTH, D_MODEL), SGU_WIDTH ** -0.5),
        "ffn_w_gate_up": nrm(ks[16], (DEPTH, D_MODEL, 2 * D_FF), D_MODEL ** -0.5),
        "ffn_w_down": nrm(ks[17], (DEPTH, D_FF, D_MODEL), D_FF ** -0.5),
    }


def _fwd_reference(x, norm_mix_pre, norm_mix_post, norm_ffn_pre, norm_ffn_post,
              attn_w_qkv, attn_b_qkv, attn_sinks, attn_w_o, attn_b_o,
              sgu_w_in, sgu_ln_g, sgu_ln_b, sgu_w_spatial, sgu_b_spatial, sgu_w_out,
              ffn_w_gate_up, ffn_w_down):
    for i in range(DEPTH):
        h = rmsnorm(x, norm_mix_pre[i])
        j = i // N_MIXERS
        if i % N_MIXERS == 0:
            m = swa_sink_attention(h, attn_w_qkv[j], attn_b_qkv[j], attn_sinks[j],
                                   attn_w_o[j], attn_b_o[j])
        else:
            m = chunked_sgu(h, sgu_w_in[j], sgu_ln_g[j], sgu_ln_b[j],
                            sgu_w_spatial[j], sgu_b_spatial[j], sgu_w_out[j])
        x = x + rmsnorm(m, norm_mix_post[i])
        h = rmsnorm(x, norm_ffn_pre[i])
        x = x + rmsnorm(swiglu(h, ffn_w_gate_up[i], ffn_w_down[i]), norm_ffn_post[i])
    return x


import jax as _jax
import jax.numpy as _jnp

TWIN_FORMAT = 'train_step'
FWD_PARAMS = ['x', 'norm_mix_pre', 'norm_mix_post', 'norm_ffn_pre', 'norm_ffn_post', 'attn_w_qkv', 'attn_b_qkv', 'attn_sinks', 'attn_w_o', 'attn_b_o', 'sgu_w_in', 'sgu_ln_g', 'sgu_ln_b', 'sgu_w_spatial', 'sgu_b_spatial', 'sgu_w_out', 'ffn_w_gate_up', 'ffn_w_down']
TWIN_WEIGHTS = ['norm_mix_pre', 'norm_mix_post', 'norm_ffn_pre', 'norm_ffn_post', 'attn_w_qkv', 'attn_b_qkv', 'attn_sinks', 'attn_w_o', 'attn_b_o', 'sgu_w_in', 'sgu_ln_g', 'sgu_ln_b', 'sgu_w_spatial', 'sgu_b_spatial', 'sgu_w_out', 'ffn_w_gate_up', 'ffn_w_down']
TWIN_DIFF_INPUT = 'x'
TWIN_INPUTS = ['x', 'norm_mix_pre', 'norm_mix_post', 'norm_ffn_pre', 'norm_ffn_post', 'attn_w_qkv', 'attn_b_qkv', 'attn_sinks', 'attn_w_o', 'attn_b_o', 'sgu_w_in', 'sgu_ln_g', 'sgu_ln_b', 'sgu_w_spatial', 'sgu_b_spatial', 'sgu_w_out', 'ffn_w_gate_up', 'ffn_w_down', 'loss_target', 'm_norm_mix_pre', 'm_norm_mix_post', 'm_norm_ffn_pre', 'm_norm_ffn_post', 'm_attn_w_qkv', 'm_attn_b_qkv', 'm_attn_sinks', 'm_attn_w_o', 'm_attn_b_o', 'm_sgu_w_in', 'm_sgu_ln_g', 'm_sgu_ln_b', 'm_sgu_w_spatial', 'm_sgu_b_spatial', 'm_sgu_w_out', 'm_ffn_w_gate_up', 'm_ffn_w_down', 'v_norm_mix_pre', 'v_norm_mix_post', 'v_norm_ffn_pre', 'v_norm_ffn_post', 'v_attn_w_qkv', 'v_attn_b_qkv', 'v_attn_sinks', 'v_attn_w_o', 'v_attn_b_o', 'v_sgu_w_in', 'v_sgu_ln_g', 'v_sgu_ln_b', 'v_sgu_w_spatial', 'v_sgu_b_spatial', 'v_sgu_w_out', 'v_ffn_w_gate_up', 'v_ffn_w_down']
TWIN_OUTPUTS = ['loss', 'grad_x', 'grad_norm_mix_pre', 'grad_norm_mix_post', 'grad_norm_ffn_pre', 'grad_norm_ffn_post', 'grad_attn_w_qkv', 'grad_attn_b_qkv', 'grad_attn_sinks', 'grad_attn_w_o', 'grad_attn_b_o', 'grad_sgu_w_in', 'grad_sgu_ln_g', 'grad_sgu_ln_b', 'grad_sgu_w_spatial', 'grad_sgu_b_spatial', 'grad_sgu_w_out', 'grad_ffn_w_gate_up', 'grad_ffn_w_down', 'delta_norm_mix_pre', 'delta_norm_mix_post', 'delta_norm_ffn_pre', 'delta_norm_ffn_post', 'delta_attn_w_qkv', 'delta_attn_b_qkv', 'delta_attn_sinks', 'delta_attn_w_o', 'delta_attn_b_o', 'delta_sgu_w_in', 'delta_sgu_ln_g', 'delta_sgu_ln_b', 'delta_sgu_w_spatial', 'delta_sgu_b_spatial', 'delta_sgu_w_out', 'delta_ffn_w_gate_up', 'delta_ffn_w_down', 'new_m_norm_mix_pre', 'new_m_norm_mix_post', 'new_m_norm_ffn_pre', 'new_m_norm_ffn_post', 'new_m_attn_w_qkv', 'new_m_attn_b_qkv', 'new_m_attn_sinks', 'new_m_attn_w_o', 'new_m_attn_b_o', 'new_m_sgu_w_in', 'new_m_sgu_ln_g', 'new_m_sgu_ln_b', 'new_m_sgu_w_spatial', 'new_m_sgu_b_spatial', 'new_m_sgu_w_out', 'new_m_ffn_w_gate_up', 'new_m_ffn_w_down', 'new_v_norm_mix_pre', 'new_v_norm_mix_post', 'new_v_norm_ffn_pre', 'new_v_norm_ffn_post', 'new_v_attn_w_qkv', 'new_v_attn_b_qkv', 'new_v_attn_sinks', 'new_v_attn_w_o', 'new_v_attn_b_o', 'new_v_sgu_w_in', 'new_v_sgu_ln_g', 'new_v_sgu_ln_b', 'new_v_sgu_w_spatial', 'new_v_sgu_b_spatial', 'new_v_sgu_w_out', 'new_v_ffn_w_gate_up', 'new_v_ffn_w_down']
TWIN_LEAF_KINDS = {'loss': 'loss', 'grad_x': 'grad_x', 'grad_norm_mix_pre': 'grad_w', 'grad_norm_mix_post': 'grad_w', 'grad_norm_ffn_pre': 'grad_w', 'grad_norm_ffn_post': 'grad_w', 'grad_attn_w_qkv': 'grad_w', 'grad_attn_b_qkv': 'grad_w', 'grad_attn_sinks': 'grad_w', 'grad_attn_w_o': 'grad_w', 'grad_attn_b_o': 'grad_w', 'grad_sgu_w_in': 'grad_w', 'grad_sgu_ln_g': 'grad_w', 'grad_sgu_ln_b': 'grad_w', 'grad_sgu_w_spatial': 'grad_w', 'grad_sgu_b_spatial': 'grad_w', 'grad_sgu_w_out': 'grad_w', 'grad_ffn_w_gate_up': 'grad_w', 'grad_ffn_w_down': 'grad_w', 'delta_norm_mix_pre': 'delta_w', 'delta_norm_mix_post': 'delta_w', 'delta_norm_ffn_pre': 'delta_w', 'delta_norm_ffn_post': 'delta_w', 'delta_attn_w_qkv': 'delta_w', 'delta_attn_b_qkv': 'delta_w', 'delta_attn_sinks': 'delta_w', 'delta_attn_w_o': 'delta_w', 'delta_attn_b_o': 'delta_w', 'delta_sgu_w_in': 'delta_w', 'delta_sgu_ln_g': 'delta_w', 'delta_sgu_ln_b': 'delta_w', 'delta_sgu_w_spatial': 'delta_w', 'delta_sgu_b_spatial': 'delta_w', 'delta_sgu_w_out': 'delta_w', 'delta_ffn_w_gate_up': 'delta_w', 'delta_ffn_w_down': 'delta_w', 'new_m_norm_mix_pre': 'new_m', 'new_m_norm_mix_post': 'new_m', 'new_m_norm_ffn_pre': 'new_m', 'new_m_norm_ffn_post': 'new_m', 'new_m_attn_w_qkv': 'new_m', 'new_m_attn_b_qkv': 'new_m', 'new_m_attn_sinks': 'new_m', 'new_m_attn_w_o': 'new_m', 'new_m_attn_b_o': 'new_m', 'new_m_sgu_w_in': 'new_m', 'new_m_sgu_ln_g': 'new_m', 'new_m_sgu_ln_b': 'new_m', 'new_m_sgu_w_spatial': 'new_m', 'new_m_sgu_b_spatial': 'new_m', 'new_m_sgu_w_out': 'new_m', 'new_m_ffn_w_gate_up': 'new_m', 'new_m_ffn_w_down': 'new_m', 'new_v_norm_mix_pre': 'new_v', 'new_v_norm_mix_post': 'new_v', 'new_v_norm_ffn_pre': 'new_v', 'new_v_norm_ffn_post': 'new_v', 'new_v_attn_w_qkv': 'new_v', 'new_v_attn_b_qkv': 'new_v', 'new_v_attn_sinks': 'new_v', 'new_v_attn_w_o': 'new_v', 'new_v_attn_b_o': 'new_v', 'new_v_sgu_w_in': 'new_v', 'new_v_sgu_ln_g': 'new_v', 'new_v_sgu_ln_b': 'new_v', 'new_v_sgu_w_spatial': 'new_v', 'new_v_sgu_b_spatial': 'new_v', 'new_v_sgu_w_out': 'new_v', 'new_v_ffn_w_gate_up': 'new_v', 'new_v_ffn_w_down': 'new_v'}


def _forward(args):
    return _fwd_reference(*[args[k] for k in FWD_PARAMS])


def _output_shape():
    out = _jax.eval_shape(lambda: _forward(_fwd_setup_inputs(0)))
    return out.shape, out.dtype

N_MICROBATCH = 1
ADAM_LR = 0.001
ADAM_B1 = 0.9
ADAM_B2 = 0.999
ADAM_EPS = 1e-08
ADAM_WD = 0.01
ADAM_STEP = 10
PER_EXAMPLE_BATCH_AXIS = {'x': 0, 'loss_target': 0}
SHARED_INPUTS = []
_WEIGHT_DTYPES = {'norm_mix_pre': _jnp.float32, 'norm_mix_post': _jnp.float32, 'norm_ffn_pre': _jnp.float32, 'norm_ffn_post': _jnp.float32, 'attn_w_qkv': _jnp.float32, 'attn_b_qkv': _jnp.float32, 'attn_sinks': _jnp.float32, 'attn_w_o': _jnp.float32, 'attn_b_o': _jnp.float32, 'sgu_w_in': _jnp.float32, 'sgu_ln_g': _jnp.float32, 'sgu_ln_b': _jnp.float32, 'sgu_w_spatial': _jnp.float32, 'sgu_b_spatial': _jnp.float32, 'sgu_w_out': _jnp.float32, 'ffn_w_gate_up': _jnp.float32, 'ffn_w_down': _jnp.float32}
MOMENT_SCALE = {'norm_mix_pre': 1.465604e+00, 'norm_mix_post': 3.239841e+01, 'norm_ffn_pre': 1.334425e+00, 'norm_ffn_post': 3.185005e+01, 'attn_w_qkv': 1.381761e+00, 'attn_b_qkv': 7.548039e+01, 'attn_sinks': 3.560787e-01, 'attn_w_o': 1.695147e+00, 'attn_b_o': 9.257379e+01, 'sgu_w_in': 7.802341e-01, 'sgu_ln_g': 2.785417e-01, 'sgu_ln_b': 2.861572e-01, 'sgu_w_spatial': 2.730813e-01, 'sgu_b_spatial': 4.179603e-01, 'sgu_w_out': 2.510838e+00, 'ffn_w_gate_up': 5.326565e-01, 'ffn_w_down': 1.055254e+00}


def _to_microbatches(a, axis):
    t = _jnp.moveaxis(a, axis, 0)
    t = t.reshape((N_MICROBATCH, t.shape[0] // N_MICROBATCH) + t.shape[1:])
    return _jnp.moveaxis(t, 1, axis + 1)


def setup_inputs(seed: int = 0) -> dict:
    inp = _fwd_setup_inputs(seed)
    key = _jax.random.fold_in(_jax.random.key(seed), 7919)
    shape, _ = _output_shape()
    out = dict(inp)
    out["loss_target"] = _jax.random.normal(_jax.random.fold_in(key, 0), shape, _jnp.float32)
    for i, name in enumerate(TWIN_WEIGHTS):
        w = inp[name].astype(_jnp.float32)
        if MOMENT_SCALE is None:
            s = _jnp.sqrt(_jnp.mean(_jnp.square(w)) + 1e-30)
        else:
            s = MOMENT_SCALE[name]
        km, kv = _jax.random.split(_jax.random.fold_in(key, i + 1))
        out[name] = w
        out["m_" + name] = s * _jax.random.normal(km, w.shape, _jnp.float32)
        out["v_" + name] = (s * s) * _jax.random.uniform(kv, w.shape, _jnp.float32, 0.5, 1.5)
    if N_MICROBATCH > 1:
        for name, axis in PER_EXAMPLE_BATCH_AXIS.items():
            out[name] = _to_microbatches(out[name], axis)
    return {'x': out['x'], 'norm_mix_pre': out['norm_mix_pre'], 'norm_mix_post': out['norm_mix_post'], 'norm_ffn_pre': out['norm_ffn_pre'], 'norm_ffn_post': out['norm_ffn_post'], 'attn_w_qkv': out['attn_w_qkv'], 'attn_b_qkv': out['attn_b_qkv'], 'attn_sinks': out['attn_sinks'], 'attn_w_o': out['attn_w_o'], 'attn_b_o': out['attn_b_o'], 'sgu_w_in': out['sgu_w_in'], 'sgu_ln_g': out['sgu_ln_g'], 'sgu_ln_b': out['sgu_ln_b'], 'sgu_w_spatial': out['sgu_w_spatial'], 'sgu_b_spatial': out['sgu_b_spatial'], 'sgu_w_out': out['sgu_w_out'], 'ffn_w_gate_up': out['ffn_w_gate_up'], 'ffn_w_down': out['ffn_w_down'], 'loss_target': out['loss_target'], 'm_norm_mix_pre': out['m_norm_mix_pre'], 'm_norm_mix_post': out['m_norm_mix_post'], 'm_norm_ffn_pre': out['m_norm_ffn_pre'], 'm_norm_ffn_post': out['m_norm_ffn_post'], 'm_attn_w_qkv': out['m_attn_w_qkv'], 'm_attn_b_qkv': out['m_attn_b_qkv'], 'm_attn_sinks': out['m_attn_sinks'], 'm_attn_w_o': out['m_attn_w_o'], 'm_attn_b_o': out['m_attn_b_o'], 'm_sgu_w_in': out['m_sgu_w_in'], 'm_sgu_ln_g': out['m_sgu_ln_g'], 'm_sgu_ln_b': out['m_sgu_ln_b'], 'm_sgu_w_spatial': out['m_sgu_w_spatial'], 'm_sgu_b_spatial': out['m_sgu_b_spatial'], 'm_sgu_w_out': out['m_sgu_w_out'], 'm_ffn_w_gate_up': out['m_ffn_w_gate_up'], 'm_ffn_w_down': out['m_ffn_w_down'], 'v_norm_mix_pre': out['v_norm_mix_pre'], 'v_norm_mix_post': out['v_norm_mix_post'], 'v_norm_ffn_pre': out['v_norm_ffn_pre'], 'v_norm_ffn_post': out['v_norm_ffn_post'], 'v_attn_w_qkv': out['v_attn_w_qkv'], 'v_attn_b_qkv': out['v_attn_b_qkv'], 'v_attn_sinks': out['v_attn_sinks'], 'v_attn_w_o': out['v_attn_w_o'], 'v_attn_b_o': out['v_attn_b_o'], 'v_sgu_w_in': out['v_sgu_w_in'], 'v_sgu_ln_g': out['v_sgu_ln_g'], 'v_sgu_ln_b': out['v_sgu_ln_b'], 'v_sgu_w_spatial': out['v_sgu_w_spatial'], 'v_sgu_b_spatial': out['v_sgu_b_spatial'], 'v_sgu_w_out': out['v_sgu_w_out'], 'v_ffn_w_gate_up': out['v_ffn_w_gate_up'], 'v_ffn_w_down': out['v_ffn_w_down']}


def _loss(weights, diff, rest, loss_target):
    with _jax.named_scope("forward"):
        args = {**rest, TWIN_DIFF_INPUT: diff, **{k: w.astype(_WEIGHT_DTYPES[k]) for k, w in weights.items()}}
        y = _forward(args)
    with _jax.named_scope("loss_head"):
        err = _jnp.square(y.astype(_jnp.float32) - loss_target)
        return 0.5 * _jnp.sum(_jnp.mean(err, axis=-1)) if err.ndim else 0.5 * err


def _adamw(w, g, m, v):
    m = ADAM_B1 * m + (1.0 - ADAM_B1) * g
    v = ADAM_B2 * v + (1.0 - ADAM_B2) * _jnp.square(g)
    m_hat = m / (1.0 - ADAM_B1 ** ADAM_STEP)
    v_hat = v / (1.0 - ADAM_B2 ** ADAM_STEP)
    delta = -ADAM_LR * (m_hat / (_jnp.sqrt(v_hat) + ADAM_EPS) + ADAM_WD * w)
    return delta, m, v


def reference(x, norm_mix_pre, norm_mix_post, norm_ffn_pre, norm_ffn_post, attn_w_qkv, attn_b_qkv, attn_sinks, attn_w_o, attn_b_o, sgu_w_in, sgu_ln_g, sgu_ln_b, sgu_w_spatial, sgu_b_spatial, sgu_w_out, ffn_w_gate_up, ffn_w_down, loss_target, m_norm_mix_pre, m_norm_mix_post, m_norm_ffn_pre, m_norm_ffn_post, m_attn_w_qkv, m_attn_b_qkv, m_attn_sinks, m_attn_w_o, m_attn_b_o, m_sgu_w_in, m_sgu_ln_g, m_sgu_ln_b, m_sgu_w_spatial, m_sgu_b_spatial, m_sgu_w_out, m_ffn_w_gate_up, m_ffn_w_down, v_norm_mix_pre, v_norm_mix_post, v_norm_ffn_pre, v_norm_ffn_post, v_attn_w_qkv, v_attn_b_qkv, v_attn_sinks, v_attn_w_o, v_attn_b_o, v_sgu_w_in, v_sgu_ln_g, v_sgu_ln_b, v_sgu_w_spatial, v_sgu_b_spatial, v_sgu_w_out, v_ffn_w_gate_up, v_ffn_w_down):
    given = dict(x=x, norm_mix_pre=norm_mix_pre, norm_mix_post=norm_mix_post, norm_ffn_pre=norm_ffn_pre, norm_ffn_post=norm_ffn_post, attn_w_qkv=attn_w_qkv, attn_b_qkv=attn_b_qkv, attn_sinks=attn_sinks, attn_w_o=attn_w_o, attn_b_o=attn_b_o, sgu_w_in=sgu_w_in, sgu_ln_g=sgu_ln_g, sgu_ln_b=sgu_ln_b, sgu_w_spatial=sgu_w_spatial, sgu_b_spatial=sgu_b_spatial, sgu_w_out=sgu_w_out, ffn_w_gate_up=ffn_w_gate_up, ffn_w_down=ffn_w_down, loss_target=loss_target, m_norm_mix_pre=m_norm_mix_pre, m_norm_mix_post=m_norm_mix_post, m_norm_ffn_pre=m_norm_ffn_pre, m_norm_ffn_post=m_norm_ffn_post, m_attn_w_qkv=m_attn_w_qkv, m_attn_b_qkv=m_attn_b_qkv, m_attn_sinks=m_attn_sinks, m_attn_w_o=m_attn_w_o, m_attn_b_o=m_attn_b_o, m_sgu_w_in=m_sgu_w_in, m_sgu_ln_g=m_sgu_ln_g, m_sgu_ln_b=m_sgu_ln_b, m_sgu_w_spatial=m_sgu_w_spatial, m_sgu_b_spatial=m_sgu_b_spatial, m_sgu_w_out=m_sgu_w_out, m_ffn_w_gate_up=m_ffn_w_gate_up, m_ffn_w_down=m_ffn_w_down, v_norm_mix_pre=v_norm_mix_pre, v_norm_mix_post=v_norm_mix_post, v_norm_ffn_pre=v_norm_ffn_pre, v_norm_ffn_post=v_norm_ffn_post, v_attn_w_qkv=v_attn_w_qkv, v_attn_b_qkv=v_attn_b_qkv, v_attn_sinks=v_attn_sinks, v_attn_w_o=v_attn_w_o, v_attn_b_o=v_attn_b_o, v_sgu_w_in=v_sgu_w_in, v_sgu_ln_g=v_sgu_ln_g, v_sgu_ln_b=v_sgu_ln_b, v_sgu_w_spatial=v_sgu_w_spatial, v_sgu_b_spatial=v_sgu_b_spatial, v_sgu_w_out=v_sgu_w_out, v_ffn_w_gate_up=v_ffn_w_gate_up, v_ffn_w_down=v_ffn_w_down)
    weights = {n: given[n] for n in TWIN_WEIGHTS}
    shared = {n: given[n] for n in SHARED_INPUTS}
    per_example = {n: given[n] for n in ['x']}
    grad_fn = _jax.value_and_grad(_loss, argnums=(0, 1))

    def one_microbatch(ex, loss_target):
        ex = dict(ex)
        diff = ex.pop(TWIN_DIFF_INPUT)
        return grad_fn(weights, diff, {**shared, **ex}, loss_target)

    if N_MICROBATCH == 1:
        loss, (grad_w, grad_x) = one_microbatch(per_example, given["loss_target"])
    else:
        def body(carry, xs):
            loss_sum, grad_sum = carry
            l_k, (gw_k, gx_k) = one_microbatch(xs[0], xs[1])
            with _jax.named_scope("update"):
                return (loss_sum + l_k, _jax.tree.map(_jnp.add, grad_sum, gw_k)), gx_k

        init = (_jnp.zeros((), _jnp.float32), _jax.tree.map(_jnp.zeros_like, weights))
        (loss, grad_w), grad_x = _jax.lax.scan(body, init, (per_example, given["loss_target"]))
    with _jax.named_scope("update"):
        delta_w, new_m, new_v = {}, {}, {}
        for n in TWIN_WEIGHTS:
            delta_w[n], new_m[n], new_v[n] = _adamw(weights[n], grad_w[n], given["m_" + n], given["v_" + n])
    return (loss, grad_x, *[grad_w[n] for n in TWIN_WEIGHTS], *[delta_w[n] for n in TWIN_WEIGHTS],
            *[new_m[n] for n in TWIN_WEIGHTS], *[new_v[n] for n in TWIN_WEIGHTS])
```

```python
import functools
import math

import jax
import jax.numpy as jnp
from jax import lax
from jax.experimental import pallas as pl
from jax.experimental.pallas import tpu as pltpu

F32 = jnp.float32
ACT = jnp.bfloat16

EPS = 1e-6
HEAD_DIM = 64
PAIR = 2 * HEAD_DIM
GQA_GROUP = 4
WINDOW = 128
ROPE_THETA = 10000.0
SCALE = HEAD_DIM ** -0.5
MASKED = -1e30
LANES = 128

ADAM_LR, ADAM_B1, ADAM_B2, ADAM_EPS, ADAM_WD, ADAM_STEP = 0.001, 0.9, 0.999, 1e-08, 0.01, 10

N_DEV = 8
MESH_AXES = ("x", "y", "c")
VMEM_LIMIT_BYTES = 56 * 1024 * 1024
MESH = pl.DeviceIdType.MESH


def _pick(n, candidates):
    for c in candidates:
        if n % c == 0:
            return c
    return n


def _row_tile(t):
    return _pick(t, (512,))


def _params(*sem):
    return pltpu.CompilerParams(dimension_semantics=sem, vmem_limit_bytes=VMEM_LIMIT_BYTES)


def _full(shape):
    return pl.BlockSpec(shape, lambda *_: (0,) * len(shape))


def _rows(tm, n):
    return pl.BlockSpec((tm, n), lambda i: (i, 0))


def _sds(shape, dtype):
    return jax.ShapeDtypeStruct(shape, dtype)


def _rstd(x):
    return lax.rsqrt(jnp.mean(x * x, axis=-1, keepdims=True) + EPS)


def _rms_bwd(xhat, r, g, dy):
    dxh = dy * g
    return r * (dxh - xhat * jnp.mean(dxh * xhat, axis=-1, keepdims=True))


def _first_half(rows):
    lane = lax.broadcasted_iota(jnp.int32, (rows, PAIR), 1)
    return (lane % HEAD_DIM) < (HEAD_DIM // 2)


def _rot_half(v, first):
    return jnp.where(first, pltpu.roll(v, PAIR - HEAD_DIM // 2, 1), pltpu.roll(v, HEAD_DIM // 2, 1))


def _rope(v, cos, sin, first):
    return v * cos + _rot_half(v, first) * sin


def _rope_t(dv, cos, sin, first):
    return dv * cos + _rot_half(dv * sin, first)


def _dot(a, b):
    return jnp.dot(a, b, preferred_element_type=F32)


def _dot_nt(a, b):
    return lax.dot_general(a, b, (((1,), (1,)), ((), ())), preferred_element_type=F32)


def _dot_tn(a, b):
    return lax.dot_general(a, b, (((0,), (0,)), ((), ())), preferred_element_type=F32)


def _sigmoid(v):
    return 1.0 / (1.0 + jnp.exp(-v))


_GELU_K = math.sqrt(2.0 / math.pi)
_GELU_C = 0.044715


def _gelu(v):
    return v * (0.5 * (1.0 + jnp.tanh(_GELU_K * (v + _GELU_C * (v * v * v)))))


def _gelu_grad(v):
    t = jnp.tanh(_GELU_K * (v + _GELU_C * (v * v * v)))
    return 0.5 * (1.0 + t) + 0.5 * v * (1.0 - t * t) * (_GELU_K * (1.0 + 3.0 * _GELU_C * (v * v)))


def _attn_qkv_fwd(x, g, w, b, cos, sin):
    t, d = x.shape
    n = w.shape[1]
    kd = (n - d) // 2
    tm = _row_tile(t)
    ch = _pick(d, (512,))

    def body(x_ref, g_ref, w_ref, b_ref, cos_ref, sin_ref, h_ref, q_ref, k_ref, v_ref):
        xv = x_ref[...]
        hb = (xv * _rstd(xv) * g_ref[...]).astype(ACT)
        h_ref[...] = hb
        cosv, sinv = cos_ref[...], sin_ref[...]
        first = _first_half(tm)

        def proj(lo, width):
            return _dot(hb, w_ref[:, lo:lo + width]) + b_ref[:, lo:lo + width]

        for c in range(0, d, ch):
            y = proj(c, ch)
            for s in range(0, ch, PAIR):
                q_ref[:, c + s:c + s + PAIR] = (_rope(y[:, s:s + PAIR], cosv, sinv, first) * SCALE).astype(ACT)
        y = proj(d, kd)
        for s in range(0, kd, PAIR):
            k_ref[:, s:s + PAIR] = _rope(y[:, s:s + PAIR], cosv, sinv, first).astype(ACT)
        v_ref[...] = proj(d + kd, kd).astype(ACT)

    return pl.pallas_call(
        body, name="attn_qkv_fwd", grid=(t // tm,),
        in_specs=[_rows(tm, d), _full((1, d)), _full((d, n)), _full((1, n)), _rows(tm, PAIR), _rows(tm, PAIR)],
        out_specs=[_rows(tm, d), _rows(tm, d), _rows(tm, kd), _rows(tm, kd)],
        out_shape=[_sds((t, d), ACT), _sds((t, d), ACT), _sds((t, kd), ACT), _sds((t, kd), ACT)],
        compiler_params=_params("parallel"),
    )(x, g, w, b, cos, sin)


def _band_mask(n):
    row = lax.broadcasted_iota(jnp.int32, (WINDOW, 2 * WINDOW), 0)
    col = lax.broadcasted_iota(jnp.int32, (WINDOW, 2 * WINDOW), 1)
    return ((col < WINDOW) & (col > row) & (n > 0)) | ((col >= WINDOW) & (col - WINDOW <= row))


def _lane_halves():
    lane = lax.broadcasted_iota(jnp.int32, (1, PAIR), 1)
    return lane < HEAD_DIM, lane >= HEAD_DIM


def _probs(qm, k2, valid, sink):
    s = jnp.where(valid, _dot_nt(qm, k2), MASKED)
    m = jnp.maximum(jnp.max(s, axis=-1, keepdims=True), sink)
    p = jnp.exp(s - m)
    psink = jnp.exp(sink - m)
    inv = 1.0 / (jnp.sum(p, axis=-1, keepdims=True) + psink)
    return p * inv, psink * inv


def _attn_specs(d, kd):
    cur = lambda n: (n, 0)
    prev = lambda n: (jnp.maximum(n - 1, 0), 0)
    return cur, prev, [pl.BlockSpec((WINDOW, d), cur), pl.BlockSpec((WINDOW, kd), cur), pl.BlockSpec((WINDOW, kd), prev),
                       pl.BlockSpec((WINDOW, kd), cur), pl.BlockSpec((WINDOW, kd), prev)]


def _attn_fwd(q, kd_, vd, sinks):
    t, d = q.shape
    kd = kd_.shape[1]
    _, _, specs = _attn_specs(d, kd)

    def body(sink_ref, q_ref, kc_ref, kp_ref, vc_ref, vp_ref, o_ref):
        valid = _band_mask(pl.program_id(0))
        halves = _lane_halves()
        for h in range(kd // PAIR):
            hs = slice(h * PAIR, (h + 1) * PAIR)
            k2 = jnp.concatenate([kp_ref[:, hs], kc_ref[:, hs]], axis=0)
            v2 = jnp.concatenate([vp_ref[:, hs], vc_ref[:, hs]], axis=0)
            for p in range(2):
                ps = slice((2 * h + p) * PAIR, (2 * h + p + 1) * PAIR)
                qp = q_ref[:, ps]
                acc = jnp.zeros((WINDOW, PAIR), F32)
                for e in range(2):
                    qm = jnp.where(halves[e], qp, jnp.zeros_like(qp))
                    pr, _ = _probs(qm, k2, valid, sink_ref[0, GQA_GROUP * h + 2 * p + e])
                    acc = acc + _dot(pr.astype(ACT), jnp.where(halves[e], v2, jnp.zeros_like(v2)))
                o_ref[:, ps] = acc.astype(ACT)

    return pl.pallas_call(
        body, name="attn_fwd", grid=(t // WINDOW,),
        in_specs=[pl.BlockSpec(memory_space=pltpu.SMEM)] + specs,
        out_specs=pl.BlockSpec((WINDOW, d), lambda n: (n, 0)),
        out_shape=_sds((t, d), ACT),
        compiler_params=_params("parallel"),
    )(sinks, q, kd_, kd_, vd, vd)


def _attn_bwd(q, kd_, vd, do, sinks, cos, sin):
    t, d = q.shape
    kd = kd_.shape[1]
    cur, prev, specs = _attn_specs(d, kd)
    n_heads = d // HEAD_DIM

    def body(sink_ref, q_ref, kc_ref, kp_ref, vc_ref, vp_ref, do_ref, cosc_ref, sinc_ref, cosp_ref, sinp_ref,
             dq_ref, dkc_ref, dkp_ref, dvc_ref, dvp_ref, dsink_ref):
        n = pl.program_id(0)

        @pl.when(n == 0)
        def _():
            dsink_ref[...] = jnp.zeros_like(dsink_ref)

        valid = _band_mask(n)
        halves = _lane_halves()
        first = _first_half(WINDOW)
        slot = lax.broadcasted_iota(jnp.int32, dsink_ref.shape, 1)
        top = lax.broadcasted_iota(jnp.int32, dsink_ref.shape, 0) == 0
        dsink = jnp.zeros(dsink_ref.shape, F32)
        for h in range(kd // PAIR):
            hs = slice(h * PAIR, (h + 1) * PAIR)
            k2 = jnp.concatenate([kp_ref[:, hs], kc_ref[:, hs]], axis=0)
            v2 = jnp.concatenate([vp_ref[:, hs], vc_ref[:, hs]], axis=0)
            dk2 = jnp.zeros((2 * WINDOW, PAIR), F32)
            dv2 = jnp.zeros((2 * WINDOW, PAIR), F32)
            for p in range(2):
                ps = slice((2 * h + p) * PAIR, (2 * h + p + 1) * PAIR)
                qp, dop = q_ref[:, ps], do_ref[:, ps]
                dqp = jnp.zeros((WINDOW, PAIR), F32)
                for e in range(2):
                    head = GQA_GROUP * h + 2 * p + e
                    qm = jnp.where(halves[e], qp, jnp.zeros_like(qp))
                    dom = jnp.where(halves[e], dop, jnp.zeros_like(dop))
                    pr, psink = _probs(qm, k2, valid, sink_ref[0, head])
                    dp = _dot_nt(dom, v2)
                    delta = jnp.sum(pr * dp, axis=-1, keepdims=True)
                    ds = (pr * (dp - delta)).astype(ACT)
                    dsink = dsink - jnp.where(top & (slot == head), jnp.sum(psink * delta, axis=0, keepdims=True), 0.0)
                    dqp = dqp + jnp.where(halves[e], _dot(ds, k2), 0.0)
                    dk2 = dk2 + _dot_tn(ds, qm)
                    dv2 = dv2 + _dot_tn(pr.astype(ACT), dom)
                dq_ref[:, ps] = (_rope_t(dqp, cosc_ref[...], sinc_ref[...], first) * SCALE).astype(ACT)
            dkp_ref[:, hs] = _rope_t(dk2[:WINDOW], cosp_ref[...], sinp_ref[...], first)
            dkc_ref[:, hs] = _rope_t(dk2[WINDOW:], cosc_ref[...], sinc_ref[...], first)
            dvp_ref[:, hs] = dv2[:WINDOW]
            dvc_ref[:, hs] = dv2[WINDOW:]
        dsink_ref[...] += dsink

    del n_heads
    tab_c, tab_p = pl.BlockSpec((WINDOW, PAIR), cur), pl.BlockSpec((WINDOW, PAIR), prev)
    kv_out = pl.BlockSpec((WINDOW, kd), cur)
    return pl.pallas_call(
        body, name="attn_bwd", grid=(t // WINDOW,),
        in_specs=[pl.BlockSpec(memory_space=pltpu.SMEM)] + specs + [pl.BlockSpec((WINDOW, d), cur), tab_c, tab_c, tab_p, tab_p],
        out_specs=[pl.BlockSpec((WINDOW, d), cur), kv_out, kv_out, kv_out, kv_out, _full((8, LANES))],
        out_shape=[_sds((t, d), ACT)] + [_sds((t, kd), F32)] * 4 + [_sds((8, LANES), F32)],
        compiler_params=_params("arbitrary"),
    )(sinks, q, kd_, kd_, vd, vd, do, cos, sin, cos, sin)


def _attn_dqkv(dq, dkc, dkp, dvc, dvp):
    t, d = dq.shape
    kd = dkc.shape[1]
    nb = t // WINDOW
    n_out = d + 2 * kd
    cur = lambda n: (n, 0)
    nxt = lambda n: (jnp.minimum(n + 1, nb - 1), 0)

    def body(dq_ref, dkc_ref, dkp_ref, dvc_ref, dvp_ref, out_ref, db_ref):
        n = pl.program_id(0)

        @pl.when(n == 0)
        def _():
            db_ref[...] = jnp.zeros_like(db_ref)

        more = n < nb - 1
        dqv = dq_ref[...]
        dk = dkc_ref[...] + jnp.where(more, dkp_ref[...], 0.0)
        dv = dvc_ref[...] + jnp.where(more, dvp_ref[...], 0.0)
        out_ref[:, :d] = dqv
        out_ref[:, d:d + kd] = dk.astype(ACT)
        out_ref[:, d + kd:] = dv.astype(ACT)
        db_ref[:, :d] += jnp.sum(dqv.astype(F32), axis=0, keepdims=True)
        db_ref[:, d:d + kd] += jnp.sum(dk, axis=0, keepdims=True)
        db_ref[:, d + kd:] += jnp.sum(dv, axis=0, keepdims=True)

    kv_c, kv_n = pl.BlockSpec((WINDOW, kd), cur), pl.BlockSpec((WINDOW, kd), nxt)
    return pl.pallas_call(
        body, name="attn_dqkv", grid=(nb,),
        in_specs=[pl.BlockSpec((WINDOW, d), cur), kv_c, kv_n, kv_c, kv_n],
        out_specs=[pl.BlockSpec((WINDOW, n_out), cur), _full((1, n_out))],
        out_shape=[_sds((t, n_out), ACT), _sds((1, n_out), F32)],
        compiler_params=_params("arbitrary"),
    )(dq, dkc, dkp, dvc, dvp)


def _proj_res(a, w, b, g, x, name):
    t, k = a.shape
    d = w.shape[1]
    tm = _row_tile(t)
    has_bias = b is not None

    def body(*refs):
        a_ref, w_ref = refs[:2]
        b_ref = refs[2] if has_bias else None
        g_ref, x_ref, m_ref, xo_ref = refs[2 + has_bias:]
        m = _dot(a_ref[...], w_ref[...])
        if has_bias:
            m = m + b_ref[...]
        m_ref[...] = m
        xo_ref[...] = x_ref[...] + (m * _rstd(m)) * g_ref[...]

    ins = [a, w] + ([b] if has_bias else []) + [g, x]
    specs = [_rows(tm, k), _full((k, d))] + ([_full((1, d))] if has_bias else []) + [_full((1, d)), _rows(tm, d)]
    return pl.pallas_call(
        body, name=name, grid=(t // tm,), in_specs=specs,
        out_specs=[_rows(tm, d), _rows(tm, d)], out_shape=[_sds((t, d), F32)] * 2,
        compiler_params=_params("parallel"),
    )(*ins)


def _post_bwd(dres, m, g, w, mode, name, gu=None):
    t, d = dres.shape
    k = w.shape[0]
    tm = _row_tile(t)
    kc = _pick(k, (1408, 1024, 512))

    def body(*refs):
        d_ref, m_ref, g_ref, w_ref = refs[:4]
        gu_ref = refs[4] if mode == "ffn" else None
        outs = refs[4 + (mode == "ffn"):]
        dm_ref, da_ref, dg_ref = outs[:3]
        i = pl.program_id(0)

        @pl.when(i == 0)
        def _():
            dg_ref[...] = jnp.zeros_like(dg_ref)
            if mode == "attn":
                outs[3][...] = jnp.zeros_like(outs[3])

        dv, mv = d_ref[...], m_ref[...]
        r = _rstd(mv)
        mh = mv * r
        dg_ref[...] += jnp.sum(dv * mh, axis=0, keepdims=True)
        dm = _rms_bwd(mh, r, g_ref[...], dv)
        dmb = dm.astype(ACT)
        dm_ref[...] = dmb
        if mode == "attn":
            outs[3][...] += jnp.sum(dm, axis=0, keepdims=True)
        for c in range(0, k, kc):
            da = _dot_nt(dmb, w_ref[c:c + kc, :])
            if mode == "attn":
                da_ref[:, c:c + kc] = da.astype(ACT)
            elif mode == "sgu":
                da_ref[:, c:c + kc] = da
            else:
                gate = gu_ref[:, c:c + kc].astype(F32)
                up = gu_ref[:, k + c:k + c + kc].astype(F32)
                sg = _sigmoid(gate)
                da_ref[:, c:c + kc] = (da * up * (sg * (1.0 + gate * (1.0 - sg)))).astype(ACT)
                da_ref[:, k + c:k + c + kc] = (da * (gate * sg)).astype(ACT)

    ins = [dres, m, g, w]
    specs = [_rows(tm, d), _rows(tm, d), _full((1, d)), _full((k, d))]
    if mode == "ffn":
        ins.append(gu)
        specs.append(_rows(tm, 2 * k))
    da_cols, da_dtype = (2 * k, ACT) if mode == "ffn" else (k, ACT if mode == "attn" else F32)
    out_specs = [_rows(tm, d), _rows(tm, da_cols), _full((1, d))]
    out_shape = [_sds((t, d), ACT), _sds((t, da_cols), da_dtype), _sds((1, d), F32)]
    if mode == "attn":
        out_specs.append(_full((1, d)))
        out_shape.append(_sds((1, d), F32))
    return pl.pallas_call(
        body, name=name, grid=(t // tm,), in_specs=specs, out_specs=out_specs, out_shape=out_shape,
        compiler_params=_params("arbitrary"),
    )(*ins)


def _pre_bwd(dy, w, x, g, dres, name):
    t, n = dy.shape
    d = w.shape[0]
    tm = _row_tile(t)
    nc = _pick(n, (1408, 1024, 512))

    def body(dy_ref, w_ref, x_ref, g_ref, dres_ref, dx_ref, dg_ref):
        @pl.when(pl.program_id(0) == 0)
        def _():
            dg_ref[...] = jnp.zeros_like(dg_ref)

        dh = jnp.zeros((tm, d), F32)
        for c in range(0, n, nc):
            dh = dh + _dot_nt(dy_ref[:, c:c + nc], w_ref[:, c:c + nc])
        xv = x_ref[...]
        r = _rstd(xv)
        xh = xv * r
        dg_ref[...] += jnp.sum(dh * xh, axis=0, keepdims=True)
        dx_ref[...] = dres_ref[...] + _rms_bwd(xh, r, g_ref[...], dh)

    return pl.pallas_call(
        body, name=name, grid=(t // tm,),
        in_specs=[_rows(tm, n), _full((d, n)), _rows(tm, d), _full((1, d)), _rows(tm, d)],
        out_specs=[_rows(tm, d), _full((1, d))], out_shape=[_sds((t, d), F32), _sds((1, d), F32)],
        compiler_params=_params("arbitrary"),
    )(dy, w, x, g, dres)


def _wgrad(a, b, name):
    t, k = a.shape
    n = b.shape[1]
    tk = _pick(k, (1024, 1408))
    tn = _pick(n, (1024, 1408))
    tt = _pick(t, (1024,))
    steps = t // tt

    def body(a_ref, b_ref, o_ref, acc_ref):
        s = pl.program_id(2)

        @pl.when(s == 0)
        def _():
            acc_ref[...] = jnp.zeros_like(acc_ref)

        acc_ref[...] += _dot_tn(a_ref[...], b_ref[...])

        @pl.when(s == steps - 1)
        def _():
            o_ref[...] = acc_ref[...]

    return pl.pallas_call(
        body, name=name, grid=(k // tk, n // tn, steps),
        in_specs=[pl.BlockSpec((tt, tk), lambda i, j, s: (s, i)), pl.BlockSpec((tt, tn), lambda i, j, s: (s, j))],
        out_specs=pl.BlockSpec((tk, tn), lambda i, j, s: (i, j)), out_shape=_sds((k, n), F32),
        scratch_shapes=[pltpu.VMEM((tk, tn), F32)],
        compiler_params=_params("parallel", "parallel", "arbitrary"),
    )(a, b)


def _ffn_up_fwd(x, g, w):
    t, d = x.shape
    f = w.shape[1] // 2
    tm = _row_tile(t)
    fc = _pick(f, (1408, 1024, 512))

    def body(x_ref, g_ref, w_ref, h_ref, gu_ref, a_ref):
        xv = x_ref[...]
        hb = (xv * _rstd(xv) * g_ref[...]).astype(ACT)
        h_ref[...] = hb
        for c in range(0, f, fc):
            gate = _dot(hb, w_ref[:, c:c + fc])
            up = _dot(hb, w_ref[:, f + c:f + c + fc])
            gu_ref[:, c:c + fc] = gate.astype(ACT)
            gu_ref[:, f + c:f + c + fc] = up.astype(ACT)
            a_ref[:, c:c + fc] = (gate * _sigmoid(gate) * up).astype(ACT)

    return pl.pallas_call(
        body, name="ffn_up_fwd", grid=(t // tm,),
        in_specs=[_rows(tm, d), _full((1, d)), _full((d, 2 * f))],
        out_specs=[_rows(tm, d), _rows(tm, 2 * f), _rows(tm, f)],
        out_shape=[_sds((t, d), ACT), _sds((t, 2 * f), ACT), _sds((t, f), ACT)],
        compiler_params=_params("parallel"),
    )(x, g, w)


def _sgu_in_fwd(x, g, w):
    t, d = x.shape
    n = w.shape[1]
    tm = _row_tile(t)
    nc = _pick(n, (1024, 512))

    def body(x_ref, g_ref, w_ref, h_ref, z_ref):
        xv = x_ref[...]
        hb = (xv * _rstd(xv) * g_ref[...]).astype(ACT)
        h_ref[...] = hb
        for c in range(0, n, nc):
            z_ref[:, c:c + nc] = _dot(hb, w_ref[:, c:c + nc])

    return pl.pallas_call(
        body, name="sgu_in_fwd", grid=(t // tm,),
        in_specs=[_rows(tm, d), _full((1, d)), _full((d, n))],
        out_specs=[_rows(tm, d), _rows(tm, n)], out_shape=[_sds((t, d), ACT), _sds((t, n), F32)],
        compiler_params=_params("parallel"),
    )(x, g, w)


def _causal(ws):
    row = lax.broadcasted_iota(jnp.int32, ws.shape, 0)
    col = lax.broadcasted_iota(jnp.int32, ws.shape, 1)
    return jnp.where(col <= row, ws, 0.0)


def _sgu_ln(v, lg, lb):
    mu = jnp.mean(v, axis=-1, keepdims=True)
    vc = v - mu
    rs = lax.rsqrt(jnp.mean(vc * vc, axis=-1, keepdims=True) + EPS)
    vh = vc * rs
    return vh, rs, vh * lg + lb


def _sgu_mix_fwd(z, lg, lb, ws, bs_t):
    t, n = z.shape
    d = n // 2
    groups = d // WINDOW
    tm = _row_tile(t)

    def body(z_ref, lg_ref, lb_ref, ws_ref, bs_ref, y_ref):
        for c in range(0, tm, WINDOW):
            u = _gelu(z_ref[c:c + WINDOW, :d])
            _, _, vn = _sgu_ln(_gelu(z_ref[c:c + WINDOW, d:]), lg_ref[...], lb_ref[...])
            for gi in range(groups):
                gs = slice(gi * WINDOW, (gi + 1) * WINDOW)
                mixed = _dot(_causal(ws_ref[gi]).astype(ACT), vn[:, gs].astype(ACT)) + bs_ref[:, gi:gi + 1]
                y_ref[c:c + WINDOW, gs] = (u[:, gs] * mixed).astype(ACT)

    return pl.pallas_call(
        body, name="sgu_mix_fwd", grid=(t // tm,),
        in_specs=[_rows(tm, n), _full((1, d)), _full((1, d)), _full((groups, WINDOW, WINDOW)), _full((WINDOW, groups))],
        out_specs=_rows(tm, d), out_shape=_sds((t, d), ACT),
        compiler_params=_params("parallel"),
    )(z, lg, lb, ws, bs_t)


def _sgu_mix_bwd(z, dy, lg, lb, ws, bs_t):
    t, n = z.shape
    d = n // 2
    groups = d // WINDOW
    tm = _row_tile(t)

    def body(z_ref, dy_ref, lg_ref, lb_ref, ws_ref, bs_ref, dz_ref, dlg_ref, dlb_ref, dws_ref, dbs_ref):
        @pl.when(pl.program_id(0) == 0)
        def _():
            for ref in (dlg_ref, dlb_ref, dws_ref, dbs_ref):
                ref[...] = jnp.zeros_like(ref)

        lane = lax.broadcasted_iota(jnp.int32, (WINDOW, groups), 1)
        for c in range(0, tm, WINDOW):
            rows = slice(c, c + WINDOW)
            zu, zv = z_ref[rows, :d], z_ref[rows, d:]
            u = _gelu(zu)
            vh, rs, vn = _sgu_ln(_gelu(zv), lg_ref[...], lb_ref[...])
            dyv = dy_ref[rows, :]
            dvn_parts = []
            for gi in range(groups):
                gs = slice(gi * WINDOW, (gi + 1) * WINDOW)
                wsg = _causal(ws_ref[gi]).astype(ACT)
                vng = vn[:, gs].astype(ACT)
                mixed = _dot(wsg, vng) + bs_ref[:, gi:gi + 1]
                dz_ref[rows, gs] = (dyv[:, gs] * mixed * _gelu_grad(zu[:, gs])).astype(ACT)
                dmix = dyv[:, gs] * u[:, gs]
                dmb = dmix.astype(ACT)
                dvn_parts.append(_dot_tn(wsg, dmb))
                dws_ref[gi] += _dot_nt(dmb, vng)
                dbs_ref[...] += jnp.where(lane == gi, jnp.sum(dmix, axis=-1, keepdims=True), 0.0)
            dvn = jnp.concatenate(dvn_parts, axis=-1)
            dlg_ref[...] += jnp.sum(dvn * vh, axis=0, keepdims=True)
            dlb_ref[...] += jnp.sum(dvn, axis=0, keepdims=True)
            dvh = dvn * lg_ref[...]
            dv = rs * (dvh - jnp.mean(dvh, axis=-1, keepdims=True) - vh * jnp.mean(dvh * vh, axis=-1, keepdims=True))
            dz_ref[rows, d:] = (dv * _gelu_grad(zv)).astype(ACT)

    vec = _full((1, d))
    return pl.pallas_call(
        body, name="sgu_mix_bwd", grid=(t // tm,),
        in_specs=[_rows(tm, n), _rows(tm, d), vec, vec, _full((groups, WINDOW, WINDOW)), _full((WINDOW, groups))],
        out_specs=[_rows(tm, n), vec, vec, _full((groups, WINDOW, WINDOW)), _full((WINDOW, groups))],
        out_shape=[_sds((t, n), ACT), _sds((1, d), F32), _sds((1, d), F32), _sds((groups, WINDOW, WINDOW), F32),
                   _sds((WINDOW, groups), F32)],
        compiler_params=_params("arbitrary"),
    )(z, dy, lg, lb, ws, bs_t)


def _loss_head(y, target):
    t, d = y.shape
    tm = _row_tile(t)

    def body(y_ref, t_ref, dy_ref, loss_ref):
        @pl.when(pl.program_id(0) == 0)
        def _():
            loss_ref[...] = jnp.zeros_like(loss_ref)

        err = y_ref[...] - t_ref[...]
        dy_ref[...] = err * (1.0 / d)
        per_token = jnp.mean(err * err, axis=-1, keepdims=True)
        loss_ref[...] += 0.5 * jnp.sum(per_token, axis=0, keepdims=True)

    return pl.pallas_call(
        body, name="loss_head", grid=(t // tm,),
        in_specs=[_rows(tm, d), _rows(tm, d)],
        out_specs=[_rows(tm, d), _full((8, LANES))], out_shape=[_sds((t, d), F32), _sds((8, LANES), F32)],
        compiler_params=_params("arbitrary"),
    )(y, target)


def _rope_tables(t):
    half = HEAD_DIM // 2
    inv_freq = ROPE_THETA ** (-(jnp.arange(half, dtype=F32) * 2.0) / HEAD_DIM)
    ang = jnp.arange(t, dtype=jnp.int32).astype(F32)[:, None] * inv_freq[None, :]
    cos, sin = jnp.cos(ang), jnp.sin(ang)
    return jnp.tile(jnp.concatenate([cos, cos], axis=1), (1, 2)), jnp.tile(jnp.concatenate([-sin, sin], axis=1), (1, 2))


def _dup_heads(w, d):
    kv = (w.shape[-1] - d) // 2
    lead = w.shape[:-1]

    def dup(part):
        heads = part.reshape(lead + (kv // HEAD_DIM, 1, HEAD_DIM))
        return jnp.broadcast_to(heads, lead + (kv // HEAD_DIM, 2, HEAD_DIM)).reshape(lead + (2 * kv,))

    return jnp.concatenate([w[..., :d], dup(w[..., d:d + kv]), dup(w[..., d + kv:])], axis=-1)


def _fold_heads(dw, d):
    kv = (dw.shape[-1] - d) // 4
    lead = dw.shape[:-1]

    def fold(part):
        return part.reshape(lead + (kv // HEAD_DIM, 2, HEAD_DIM)).sum(axis=-2).reshape(lead + (kv,))

    return jnp.concatenate([dw[..., :d], fold(dw[..., d:d + 2 * kv]), fold(dw[..., d + 2 * kv:])], axis=-1)


def _local_step(x, target, p):
    t, d = x.shape
    row = lambda v: v.reshape(1, -1).astype(F32)
    cos, sin = _rope_tables(t)
    w_qkv = _dup_heads(p["attn_w_qkv"], d)
    b_qkv = _dup_heads(row(p["attn_b_qkv"]), d)
    bs_t = p["sgu_b_spatial"].astype(F32).T
    g_mix_pre, g_mix_post = p["norm_mix_pre"], p["norm_mix_post"]
    g_ffn_pre, g_ffn_post = p["norm_ffn_pre"], p["norm_ffn_post"]

    h0, q, kdup, vdup = _attn_qkv_fwd(x, row(g_mix_pre[0]), w_qkv, b_qkv, cos, sin)
    o = _attn_fwd(q, kdup, vdup, p["attn_sinks"].reshape(1, -1).astype(F32))
    m0, x1 = _proj_res(o, p["attn_w_o"], row(p["attn_b_o"]), row(g_mix_post[0]), x, "attn_out_fwd")
    h1, gu0, a0 = _ffn_up_fwd(x1, row(g_ffn_pre[0]), p["ffn_w_gate_up"][0])
    f0, x2 = _proj_res(a0, p["ffn_w_down"][0], None, row(g_ffn_post[0]), x1, "ffn_down_fwd0")
    h2, z = _sgu_in_fwd(x2, row(g_mix_pre[1]), p["sgu_w_in"])
    lg, lb = row(p["sgu_ln_g"]), row(p["sgu_ln_b"])
    ws = p["sgu_w_spatial"].astype(F32)
    y = _sgu_mix_fwd(z, lg, lb, ws, bs_t)
    m1, x3 = _proj_res(y, p["sgu_w_out"], None, row(g_mix_post[1]), x2, "sgu_out_fwd")
    h3, gu1, a1 = _ffn_up_fwd(x3, row(g_ffn_pre[1]), p["ffn_w_gate_up"][1])
    f1, x4 = _proj_res(a1, p["ffn_w_down"][1], None, row(g_ffn_post[1]), x3, "ffn_down_fwd1")
    dx4, loss_tile = _loss_head(x4, target)

    df1, dgu1, dg_ffn_post1 = _post_bwd(dx4, f1, row(g_ffn_post[1]), p["ffn_w_down"][1], "ffn", "ffn_down_bwd1", gu1)
    dw_down1 = _wgrad(a1, df1, "ffn_down_wgrad1")
    dw_gu1 = _wgrad(h3, dgu1, "ffn_up_wgrad1")
    dx3, dg_ffn_pre1 = _pre_bwd(dgu1, p["ffn_w_gate_up"][1], x3, row(g_ffn_pre[1]), dx4, "ffn_up_bwd1")
    dm1, dy, dg_mix_post1 = _post_bwd(dx3, m1, row(g_mix_post[1]), p["sgu_w_out"], "sgu", "sgu_out_bwd")
    dw_out = _wgrad(y, dm1, "sgu_out_wgrad")
    dz, dlg, dlb, dws, dbs_t = _sgu_mix_bwd(z, dy, lg, lb, ws, bs_t)
    dw_in = _wgrad(h2, dz, "sgu_in_wgrad")
    dx2, dg_mix_pre1 = _pre_bwd(dz, p["sgu_w_in"], x2, row(g_mix_pre[1]), dx3, "sgu_in_bwd")
    df0, dgu0, dg_ffn_post0 = _post_bwd(dx2, f0, row(g_ffn_post[0]), p["ffn_w_down"][0], "ffn", "ffn_down_bwd0", gu0)
    dw_down0 = _wgrad(a0, df0, "ffn_down_wgrad0")
    dw_gu0 = _wgrad(h1, dgu0, "ffn_up_wgrad0")
    dx1, dg_ffn_pre0 = _pre_bwd(dgu0, p["ffn_w_gate_up"][0], x1, row(g_ffn_pre[0]), dx2, "ffn_up_bwd0")
    dm0, do, dg_mix_post0, db_o = _post_bwd(dx1, m0, row(g_mix_post[0]), p["attn_w_o"], "attn", "attn_out_bwd")
    dw_o = _wgrad(o, dm0, "attn_out_wgrad")
    dq, dkc, dkp, dvc, dvp, dsink = _attn_bwd(q, kdup, vdup, do, p["attn_sinks"].reshape(1, -1).astype(F32), cos, sin)
    dqkv, db_qkv = _attn_dqkv(dq, dkc, dkp, dvc, dvp)
    dw_qkv = _wgrad(h0, dqkv, "attn_qkv_wgrad")
    dx0, dg_mix_pre0 = _pre_bwd(dqkv, w_qkv, x, row(g_mix_pre[0]), dx1, "attn_qkv_bwd")

    causal = jnp.tril(jnp.ones((WINDOW, WINDOW), F32))
    grads = {
        "norm_mix_pre": jnp.concatenate([dg_mix_pre0, dg_mix_pre1], axis=0),
        "norm_mix_post": jnp.concatenate([dg_mix_post0, dg_mix_post1], axis=0),
        "norm_ffn_pre": jnp.concatenate([dg_ffn_pre0, dg_ffn_pre1], axis=0),
        "norm_ffn_post": jnp.concatenate([dg_ffn_post0, dg_ffn_post1], axis=0),
        "attn_w_qkv": _fold_heads(dw_qkv, d),
        "attn_b_qkv": _fold_heads(db_qkv, d),
        "attn_sinks": dsink[:1, :d // HEAD_DIM],
        "attn_w_o": dw_o,
        "attn_b_o": db_o,
        "sgu_w_in": dw_in,
        "sgu_ln_g": dlg,
        "sgu_ln_b": dlb,
        "sgu_w_spatial": dws * causal[None],
        "sgu_b_spatial": dbs_t.T,
        "sgu_w_out": dw_out,
        "ffn_w_gate_up": jnp.stack([dw_gu0, dw_gu1]),
        "ffn_w_down": jnp.stack([dw_down0, dw_down1]),
    }
    return loss_tile[0, 0], dx0, grads


ANY = pl.BlockSpec(memory_space=pl.ANY)
IN_VMEM = pl.BlockSpec(memory_space=pltpu.VMEM)
AG_COPIES = 7


def _place():
    return lax.axis_index("x"), lax.axis_index("y"), lax.axis_index("c")


def _other_chips(x, y):
    return [(1 - x, y), (x, 1 - y), (1 - x, 1 - y)]


def _all_gather(blocks, dtypes, name):
    n = len(blocks)

    def body(*refs):
        ins, outs, stage = refs[:n], refs[n:2 * n], refs[2 * n:3 * n]
        send, recv, local_sem = refs[3 * n:]
        x, y, c = _place()
        sibling = (x, y, 1 - c)
        chips = _other_chips(x, y)
        mine = 4 * x + 2 * y + c

        def copy(a, k, block, to, src=None):
            dst = outs[a].at[block]
            return pltpu.make_async_remote_copy(
                src_ref=dst if src is None else src, dst_ref=dst, send_sem=send.at[a * AG_COPIES + k],
                recv_sem=recv.at[a * AG_COPIES + k], device_id=to, device_id_type=MESH)

        started = []
        for a in range(n):
            stage[a][...] = ins[a][...].astype(dtypes[a])
            own = pltpu.make_async_copy(stage[a], outs[a].at[mine], local_sem.at[a])
            own.start()
            started.append(own)
        sends = []
        for a in range(n):
            sends.append(copy(a, 0, mine, sibling, src=stage[a]))
            sends += [copy(a, 1 + j, mine, (*chip, c), src=stage[a]) for j, chip in enumerate(chips)]
        for cp in sends:
            cp.start()
        for j, (px, py) in enumerate(chips):
            block = 4 * px + 2 * py + c
            for a in range(n):
                copy(a, 1 + j, block, (x, y, c)).wait_recv()
                forward = copy(a, 4 + j, block, sibling)
                forward.start()
                sends.append(forward)
        for a in range(n):
            copy(a, 0, 4 * x + 2 * y + (1 - c), (x, y, c)).wait_recv()
            for j, (px, py) in enumerate(chips):
                copy(a, 4 + j, 4 * px + 2 * py + (1 - c), (x, y, c)).wait_recv()
        for cp in sends:
            cp.wait_send()
        for own in started:
            own.wait()

    return pl.pallas_call(
        body, name=name,
        in_specs=[IN_VMEM] * n, out_specs=[ANY] * n,
        out_shape=[_sds((N_DEV,) + b.shape, dt) for b, dt in zip(blocks, dtypes)],
        scratch_shapes=[pltpu.VMEM(b.shape, dt) for b, dt in zip(blocks, dtypes)]
        + [pltpu.SemaphoreType.DMA((n * AG_COPIES,)), pltpu.SemaphoreType.DMA((n * AG_COPIES,)), pltpu.SemaphoreType.DMA((n,))],
        compiler_params=pltpu.CompilerParams(vmem_limit_bytes=VMEM_LIMIT_BYTES),
    )(*blocks)


def _exchange(arrays, n_slots, route, name):
    n = len(arrays)

    def body(*refs):
        ins, outs = refs[:n], refs[n:2 * n]
        send, recv = refs[2 * n:]
        x, y, c = _place()
        copies = []
        for a in range(n):
            for s in range(n_slots):
                block, to = route(s, x, y, c)
                copies.append(pltpu.make_async_remote_copy(
                    src_ref=ins[a].at[block], dst_ref=outs[a].at[s], send_sem=send.at[a * n_slots + s],
                    recv_sem=recv.at[a * n_slots + s], device_id=to, device_id_type=MESH))
        for cp in copies:
            cp.start()
        for cp in copies:
            cp.wait()

    return pl.pallas_call(
        body, name=name, in_specs=[ANY] * n, out_specs=[ANY] * n,
        out_shape=[_sds((n_slots,) + v.shape[1:], v.dtype) for v in arrays],
        scratch_shapes=[pltpu.SemaphoreType.DMA((n * n_slots,)), pltpu.SemaphoreType.DMA((n * n_slots,))],
    )(*arrays)


def _to_sibling(s, x, y, c):
    return 2 * s + (1 - c), (x, y, 1 - c)


def _to_owner_chip(s, x, y, c):
    px, py = _other_chips(x, y)[s]
    return 2 * px + py, (px, py, c)


def _pair_sum(g, r, core, name):
    _, _, rows, cols = g.shape
    tr = _pick(rows, (512, 256, 128))

    def body(core_ref, g_ref, r_ref, p_ref):
        del core_ref
        p_ref[...] = (g_ref[...].astype(F32) + r_ref[...].astype(F32)).astype(p_ref.dtype)

    return pl.pallas_call(
        body, name=name,
        grid_spec=pltpu.PrefetchScalarGridSpec(
            num_scalar_prefetch=1, grid=(4, rows // tr),
            in_specs=[pl.BlockSpec((None, None, tr, cols), lambda q, i, core_ref: (q, core_ref[0], i, 0)),
                      pl.BlockSpec((None, tr, cols), lambda q, i, core_ref: (q, i, 0))],
            out_specs=pl.BlockSpec((None, tr, cols), lambda q, i, core_ref: (q, i, 0))),
        out_shape=_sds((4, rows, cols), g.dtype),
        compiler_params=_params("parallel", "parallel"),
    )(core, g, r)


def _adam(w, g, m, v):
    m2 = ADAM_B1 * m + (1.0 - ADAM_B1) * g
    v2 = ADAM_B2 * v + (1.0 - ADAM_B2) * (g * g)
    m_hat = m2 / (1.0 - ADAM_B1 ** ADAM_STEP)
    v_hat = v2 / (1.0 - ADAM_B2 ** ADAM_STEP)
    return -ADAM_LR * (m_hat / (jnp.sqrt(v_hat) + ADAM_EPS) + ADAM_WD * w), m2, v2


def _shard_update(p, qsum, chip, w, m, v, name):
    rows, cols = w.shape
    tr = _pick(rows, (512, 256, 128))

    def body(chip_ref, p_ref, q_ref, w_ref, m_ref, v_ref, g_out, d_out, m_out, v_out):
        del chip_ref
        g = p_ref[...].astype(F32)
        for s in range(3):
            g = g + q_ref[s].astype(F32)
        g_out[...] = g
        d_out[...], m_out[...], v_out[...] = _adam(w_ref[...], g, m_ref[...], v_ref[...])

    tile = pl.BlockSpec((tr, cols), lambda i, chip_ref: (i, 0))
    return pl.pallas_call(
        body, name=name,
        grid_spec=pltpu.PrefetchScalarGridSpec(
            num_scalar_prefetch=1, grid=(rows // tr,),
            in_specs=[pl.BlockSpec((None, tr, cols), lambda i, chip_ref: (chip_ref[0], i, 0)),
                      pl.BlockSpec((3, tr, cols), lambda i, chip_ref: (0, i, 0)), tile, tile, tile],
            out_specs=[tile] * 4),
        out_shape=[_sds((rows, cols), F32)] * 4,
        compiler_params=_params("parallel"),
    )(chip, p, qsum, w, m, v)


def _replicated_update(gathered, w, m, v):
    rows, cols = w.shape

    def body(s_ref, w_ref, m_ref, v_ref, g_out, d_out, m_out, v_out):
        g = s_ref[0]
        for dev in range(1, N_DEV):
            g = g + s_ref[dev]
        g_out[...] = g
        d_out[...], m_out[...], v_out[...] = _adam(w_ref[...], g, m_ref[...], v_ref[...])

    return pl.pallas_call(
        body, name="replicated_update", out_shape=[_sds((rows, cols), F32)] * 4,
        compiler_params=pltpu.CompilerParams(vmem_limit_bytes=VMEM_LIMIT_BYTES),
    )(gathered, w, m, v)


COLUMN_SHARDED = ("attn_w_qkv", "sgu_w_in", "ffn_w_gate_up")
ROW_SHARDED = ("attn_w_o", "sgu_w_out", "ffn_w_down")
SHARDED = ("attn_w_qkv", "attn_w_o", "sgu_w_in", "sgu_ln", "sgu_w_out", "ffn_w_gate_up", "ffn_w_down")
REPLICATED = ("norm_mix_pre", "norm_mix_post", "norm_ffn_pre", "norm_ffn_post", "attn_b_qkv", "attn_sinks", "attn_b_o",
              "sgu_w_spatial", "sgu_b_spatial")
WEIGHTS = ("norm_mix_pre", "norm_mix_post", "norm_ffn_pre", "norm_ffn_post", "attn_w_qkv", "attn_b_qkv", "attn_sinks", "attn_w_o",
           "attn_b_o", "sgu_w_in", "sgu_ln_g", "sgu_ln_b", "sgu_w_spatial", "sgu_b_spatial", "sgu_w_out", "ffn_w_gate_up", "ffn_w_down")


def _as_block(name, shard):
    return shard.reshape(-1, shard.shape[-1])


def _unblock_full(name, gathered, layers):
    _, rc, c = gathered.shape
    r = rc // layers
    g = gathered.reshape(N_DEV, layers, r, c)
    if name in COLUMN_SHARDED:
        full = jnp.transpose(g, (1, 2, 0, 3)).reshape(layers, r, N_DEV * c)
    else:
        full = jnp.transpose(g, (1, 0, 2, 3)).reshape(layers, N_DEV * r, c)
    return full


def _block_full(name, full, dtype):
    layers, rows, cols = full.shape
    if name in COLUMN_SHARDED:
        c = cols // N_DEV
        g = jnp.transpose(full.reshape(layers, rows, N_DEV, c), (2, 0, 1, 3)).reshape(N_DEV, layers * rows, c)
    else:
        r = rows // N_DEV
        g = jnp.transpose(full.reshape(layers, N_DEV, r, cols), (1, 0, 2, 3)).reshape(N_DEV, layers * r, cols)
    return g.astype(dtype)


def _pack_rows(parts, width):
    rows = []
    for v in parts:
        flat = v.reshape(-1).astype(F32)
        n_rows = -(-flat.shape[0] // width)
        rows.append(jnp.pad(flat, (0, n_rows * width - flat.shape[0])).reshape(n_rows, width))
    total = sum(r.shape[0] for r in rows)
    pad = -total % 8
    if pad:
        rows.append(jnp.zeros((pad, width), F32))
    return jnp.concatenate(rows, axis=0)


def _unpack_rows(packed, like, width):
    out, at = [], 0
    for v in like:
        size = math.prod(v.shape)
        n_rows = -(-size // width)
        out.append(packed[at:at + n_rows].reshape(-1)[:size].reshape(v.shape))
        at += n_rows
    return out


def kernel(x, norm_mix_pre, norm_mix_post, norm_ffn_pre, norm_ffn_post, attn_w_qkv, attn_b_qkv, attn_sinks, attn_w_o, attn_b_o, sgu_w_in, sgu_ln_g, sgu_ln_b, sgu_w_spatial, sgu_b_spatial, sgu_w_out, ffn_w_gate_up, ffn_w_down, loss_target, m_norm_mix_pre, m_norm_mix_post, m_norm_ffn_pre, m_norm_ffn_post, m_attn_w_qkv, m_attn_b_qkv, m_attn_sinks, m_attn_w_o, m_attn_b_o, m_sgu_w_in, m_sgu_ln_g, m_sgu_ln_b, m_sgu_w_spatial, m_sgu_b_spatial, m_sgu_w_out, m_ffn_w_gate_up, m_ffn_w_down, v_norm_mix_pre, v_norm_mix_post, v_norm_ffn_pre, v_norm_ffn_post, v_attn_w_qkv, v_attn_b_qkv, v_attn_sinks, v_attn_w_o, v_attn_b_o, v_sgu_w_in, v_sgu_ln_g, v_sgu_ln_b, v_sgu_w_spatial, v_sgu_b_spatial, v_sgu_w_out, v_ffn_w_gate_up, v_ffn_w_down):
    w = dict(norm_mix_pre=norm_mix_pre, norm_mix_post=norm_mix_post, norm_ffn_pre=norm_ffn_pre, norm_ffn_post=norm_ffn_post,
             attn_w_qkv=attn_w_qkv, attn_b_qkv=attn_b_qkv, attn_sinks=attn_sinks, attn_w_o=attn_w_o, attn_b_o=attn_b_o,
             sgu_w_in=sgu_w_in, sgu_ln_g=sgu_ln_g, sgu_ln_b=sgu_ln_b, sgu_w_spatial=sgu_w_spatial, sgu_b_spatial=sgu_b_spatial,
             sgu_w_out=sgu_w_out, ffn_w_gate_up=ffn_w_gate_up, ffn_w_down=ffn_w_down)
    mom = dict(norm_mix_pre=m_norm_mix_pre, norm_mix_post=m_norm_mix_post, norm_ffn_pre=m_norm_ffn_pre, norm_ffn_post=m_norm_ffn_post,
               attn_w_qkv=m_attn_w_qkv, attn_b_qkv=m_attn_b_qkv, attn_sinks=m_attn_sinks, attn_w_o=m_attn_w_o, attn_b_o=m_attn_b_o,
               sgu_w_in=m_sgu_w_in, sgu_ln_g=m_sgu_ln_g, sgu_ln_b=m_sgu_ln_b, sgu_w_spatial=m_sgu_w_spatial,
               sgu_b_spatial=m_sgu_b_spatial, sgu_w_out=m_sgu_w_out, ffn_w_gate_up=m_ffn_w_gate_up, ffn_w_down=m_ffn_w_down)
    var = dict(norm_mix_pre=v_norm_mix_pre, norm_mix_post=v_norm_mix_post, norm_ffn_pre=v_norm_ffn_pre, norm_ffn_post=v_norm_ffn_post,
               attn_w_qkv=v_attn_w_qkv, attn_b_qkv=v_attn_b_qkv, attn_sinks=v_attn_sinks, attn_w_o=v_attn_w_o, attn_b_o=v_attn_b_o,
               sgu_w_in=v_sgu_w_in, sgu_ln_g=v_sgu_ln_g, sgu_ln_b=v_sgu_ln_b, sgu_w_spatial=v_sgu_w_spatial,
               sgu_b_spatial=v_sgu_b_spatial, sgu_w_out=v_sgu_w_out, ffn_w_gate_up=v_ffn_w_gate_up, ffn_w_down=v_ffn_w_down)

    def shard_of(tree, name):
        if name == "sgu_ln":
            return jnp.concatenate([tree["sgu_ln_g"], tree["sgu_ln_b"]], axis=0)
        return tree[name]

    dtypes = [F32 if name == "sgu_ln" else ACT for name in SHARDED]
    gathered = _all_gather([_as_block(name, shard_of(w, name)) for name in SHARDED], dtypes, "weights_all_gather")
    full = {name: w[name] for name in REPLICATED}
    for name, g in zip(SHARDED, gathered):
        if name == "sgu_ln":
            full["sgu_ln_g"], full["sgu_ln_b"] = g[:, 0, :].reshape(-1), g[:, 1, :].reshape(-1)
        else:
            layers = w[name].shape[0]
            weight = _unblock_full(name, g, layers)
            full[name] = weight if name.startswith("ffn") else weight[0]
    full["sgu_w_spatial"], full["sgu_b_spatial"] = sgu_w_spatial[0], sgu_b_spatial[0]
    full["attn_b_qkv"], full["attn_b_o"], full["attn_sinks"] = attn_b_qkv[0], attn_b_o[0], attn_sinks[0]

    loss_share, dx, grads = _local_step(x[0], loss_target[0], full)
    loss = lax.psum(loss_share, MESH_AXES)

    blocked = []
    for name in SHARDED:
        if name == "sgu_ln":
            blocked.append(jnp.stack([grads["sgu_ln_g"].reshape(N_DEV, -1), grads["sgu_ln_b"].reshape(N_DEV, -1)], axis=1))
        else:
            g = grads[name]
            blocked.append(_block_full(name, g if g.ndim == 3 else g[None], ACT))
    core = lax.axis_index("c").astype(jnp.int32).reshape(1)
    chip = (2 * lax.axis_index("x") + lax.axis_index("y")).astype(jnp.int32).reshape(1)
    from_sibling = _exchange(blocked, 4, _to_sibling, "grads_to_sibling")
    partial = [_pair_sum(g.reshape((4, 2) + g.shape[1:]), r, core, "grads_pair_sum_" + name)
               for name, g, r in zip(SHARDED, blocked, from_sibling)]
    from_chips = _exchange(partial, 3, _to_owner_chip, "grads_to_owner_chip")

    out_g, out_d, out_m, out_v = {}, {}, {}, {}
    for name, p, qsum in zip(SHARDED, partial, from_chips):
        shards = [_as_block(name, shard_of(tree, name)) for tree in (w, mom, var)]
        res = _shard_update(p, qsum, chip, *shards, "shard_update_" + name)
        for out, val in zip((out_g, out_d, out_m, out_v), res):
            if name == "sgu_ln":
                out["sgu_ln_g"], out["sgu_ln_b"] = val[0:1], val[1:2]
            else:
                out[name] = val.reshape(w[name].shape)

    width = x.shape[-1]
    small = [grads[name].reshape(w[name].shape) for name in REPLICATED]
    shares, = _all_gather([_pack_rows(small, width)], [F32], "replicated_grads_all_gather")
    packed = [_pack_rows([tree[name] for name in REPLICATED], width) for tree in (w, mom, var)]
    res = _replicated_update(shares, *packed)
    for out, val in zip((out_g, out_d, out_m, out_v), res):
        for name, leaf in zip(REPLICATED, _unpack_rows(val, [w[name] for name in REPLICATED], width)):
            out[name] = leaf

    return (loss, dx[None], *[out_g[n] for n in WEIGHTS], *[out_d[n] for n in WEIGHTS],
            *[out_m[n] for n in WEIGHTS], *[out_v[n] for n in WEIGHTS])
```

```python
import functools
import math

import jax
import jax.numpy as jnp
from jax import lax
from jax.experimental import pallas as pl
from jax.experimental.pallas import tpu as pltpu

F32 = jnp.float32
ACT = jnp.bfloat16

EPS = 1e-6
HEAD_DIM = 64
PAIR = 2 * HEAD_DIM
GQA_GROUP = 4
WINDOW = 128
ROPE_THETA = 10000.0
SCALE = HEAD_DIM ** -0.5
MASKED = -1e30
LANES = 128

ADAM_LR, ADAM_B1, ADAM_B2, ADAM_EPS, ADAM_WD, ADAM_STEP = 0.001, 0.9, 0.999, 1e-08, 0.01, 10

N_DEV = 8
MESH_AXES = ("x", "y", "c")
VMEM_LIMIT_BYTES = 56 * 1024 * 1024
MESH = pl.DeviceIdType.MESH


def _pick(n, candidates):
    for c in candidates:
        if n % c == 0:
            return c
    return n


def _row_tile(t):
    return _pick(t, (512,))


def _params(*sem):
    return pltpu.CompilerParams(dimension_semantics=sem, vmem_limit_bytes=VMEM_LIMIT_BYTES)


def _full(shape):
    return pl.BlockSpec(shape, lambda *_: (0,) * len(shape))


def _rows(tm, n):
    return pl.BlockSpec((tm, n), lambda i: (i, 0))


def _sds(shape, dtype):
    return jax.ShapeDtypeStruct(shape, dtype)


def _rstd(x):
    return lax.rsqrt(jnp.mean(x * x, axis=-1, keepdims=True) + EPS)


def _rms_bwd(xhat, r, g, dy):
    dxh = dy * g
    return r * (dxh - xhat * jnp.mean(dxh * xhat, axis=-1, keepdims=True))


def _first_half(rows):
    lane = lax.broadcasted_iota(jnp.int32, (rows, PAIR), 1)
    return (lane % HEAD_DIM) < (HEAD_DIM // 2)


def _rot_half(v, first):
    return jnp.where(first, pltpu.roll(v, PAIR - HEAD_DIM // 2, 1), pltpu.roll(v, HEAD_DIM // 2, 1))


def _rope(v, cos, sin, first):
    return v * cos + _rot_half(v, first) * sin


def _rope_t(dv, cos, sin, first):
    return dv * cos + _rot_half(dv * sin, first)


def _dot(a, b):
    return jnp.dot(a, b, preferred_element_type=F32)


def _dot_nt(a, b):
    return lax.dot_general(a, b, (((1,), (1,)), ((), ())), preferred_element_type=F32)


def _dot_tn(a, b):
    return lax.dot_general(a, b, (((0,), (0,)), ((), ())), preferred_element_type=F32)


def _sigmoid(v):
    return 1.0 / (1.0 + jnp.exp(-v))


_GELU_K = math.sqrt(2.0 / math.pi)
_GELU_C = 0.044715


def _gelu(v):
    return v * (0.5 * (1.0 + jnp.tanh(_GELU_K * (v + _GELU_C * (v * v * v)))))


def _gelu_grad(v):
    t = jnp.tanh(_GELU_K * (v + _GELU_C * (v * v * v)))
    return 0.5 * (1.0 + t) + 0.5 * v * (1.0 - t * t) * (_GELU_K * (1.0 + 3.0 * _GELU_C * (v * v)))


def _attn_qkv_fwd(x, g, w, b, cos, sin):
    t, d = x.shape
    n = w.shape[1]
    kd = (n - d) // 2
    tm = _row_tile(t)
    ch = _pick(d, (512,))

    def body(x_ref, g_ref, w_ref, b_ref, cos_ref, sin_ref, h_ref, q_ref, k_ref, v_ref):
        xv = x_ref[...]
        hb = (xv * _rstd(xv) * g_ref[...]).astype(ACT)
        h_ref[...] = hb
        cosv, sinv = cos_ref[...], sin_ref[...]
        first = _first_half(tm)

        def proj(lo, width):
            return _dot(hb, w_ref[:, lo:lo + width]) + b_ref[:, lo:lo + width]

        for c in range(0, d, ch):
            y = proj(c, ch)
            for s in range(0, ch, PAIR):
                q_ref[:, c + s:c + s + PAIR] = (_rope(y[:, s:s + PAIR], cosv, sinv, first) * SCALE).astype(ACT)
        y = proj(d, kd)
        for s in range(0, kd, PAIR):
            k_ref[:, s:s + PAIR] = _rope(y[:, s:s + PAIR], cosv, sinv, first).astype(ACT)
        v_ref[...] = proj(d + kd, kd).astype(ACT)

    return pl.pallas_call(
        body, name="attn_qkv_fwd", grid=(t // tm,),
        in_specs=[_rows(tm, d), _full((1, d)), _full((d, n)), _full((1, n)), _rows(tm, PAIR), _rows(tm, PAIR)],
        out_specs=[_rows(tm, d), _rows(tm, d), _rows(tm, kd), _rows(tm, kd)],
        out_shape=[_sds((t, d), ACT), _sds((t, d), ACT), _sds((t, kd), ACT), _sds((t, kd), ACT)],
        compiler_params=_params("parallel"),
    )(x, g, w, b, cos, sin)


STACK = GQA_GROUP * WINDOW


def _band_mask(has_prev):
    row = lax.broadcasted_iota(jnp.int32, (STACK, 2 * WINDOW), 0) % WINDOW
    col = lax.broadcasted_iota(jnp.int32, (STACK, 2 * WINDOW), 1)
    return ((col < WINDOW) & (col > row) & has_prev) | ((col >= WINDOW) & (col - WINDOW <= row))


def _lane_halves():
    lane = lax.broadcasted_iota(jnp.int32, (1, PAIR), 1)
    return lane < HEAD_DIM, lane >= HEAD_DIM


def _stack_heads(ref, h, halves):
    parts = []
    for p in range(2):
        pair = ref[:, (2 * h + p) * PAIR:(2 * h + p + 1) * PAIR]
        parts += [jnp.where(half, pair, jnp.zeros_like(pair)) for half in halves]
    return jnp.concatenate(parts, axis=0)


def _sink_column(sink_ref, h):
    row = lax.broadcasted_iota(jnp.int32, (STACK, 1), 0)
    col = jnp.full((STACK, 1), sink_ref[0, GQA_GROUP * h + GQA_GROUP - 1], F32)
    for j in range(GQA_GROUP - 2, -1, -1):
        col = jnp.where(row < (j + 1) * WINDOW, sink_ref[0, GQA_GROUP * h + j], col)
    return col


def _probs(qs, k2, valid, sink):
    s = jnp.where(valid, _dot_nt(qs, k2), MASKED)
    m = jnp.maximum(jnp.max(s, axis=-1, keepdims=True), sink)
    p = jnp.exp(s - m)
    psink = jnp.exp(sink - m)
    inv = 1.0 / (jnp.sum(p, axis=-1, keepdims=True) + psink)
    return p * inv, psink * inv


def _attn_fwd(q, kd_, vd, sinks):
    t, d = q.shape
    kd = kd_.shape[1]
    cur = lambda n: (n, 0)
    prev = lambda n: (jnp.maximum(n - 1, 0), 0)

    def body(sink_ref, q_ref, kc_ref, kp_ref, vc_ref, vp_ref, o_ref):
        valid = _band_mask(pl.program_id(0) > 0)
        halves = _lane_halves()
        for h in range(kd // PAIR):
            hs = slice(h * PAIR, (h + 1) * PAIR)
            k2 = jnp.concatenate([kp_ref[:, hs], kc_ref[:, hs]], axis=0)
            v2 = jnp.concatenate([vp_ref[:, hs], vc_ref[:, hs]], axis=0)
            vs = jnp.concatenate([jnp.where(half, v2, jnp.zeros_like(v2)) for half in halves], axis=0)
            pr, _ = _probs(_stack_heads(q_ref, h, halves), k2, valid, _sink_column(sink_ref, h))
            pr = pr.astype(ACT)
            for p in range(2):
                both = jnp.concatenate([pr[2 * p * WINDOW:(2 * p + 1) * WINDOW], pr[(2 * p + 1) * WINDOW:(2 * p + 2) * WINDOW]], axis=1)
                o_ref[:, (2 * h + p) * PAIR:(2 * h + p + 1) * PAIR] = _dot(both, vs).astype(ACT)

    kv_c, kv_p = pl.BlockSpec((WINDOW, kd), cur), pl.BlockSpec((WINDOW, kd), prev)
    return pl.pallas_call(
        body, name="attn_fwd", grid=(t // WINDOW,),
        in_specs=[pl.BlockSpec(memory_space=pltpu.SMEM), pl.BlockSpec((WINDOW, d), cur), kv_c, kv_p, kv_c, kv_p],
        out_specs=pl.BlockSpec((WINDOW, d), cur),
        out_shape=_sds((t, d), ACT),
        compiler_params=_params("parallel"),
    )(sinks, q, kd_, kd_, vd, vd)


def _attn_bwd(q, kd_, vd, do, sinks, cos, sin):
    t, d = q.shape
    kd = kd_.shape[1]
    nb = t // WINDOW
    n_out = d + 2 * kd
    cur = lambda n: (jnp.minimum(n, nb - 1), 0)
    prev = lambda n: (jnp.maximum(jnp.minimum(n, nb - 1) - 1, 0), 0)
    late = lambda n: (jnp.maximum(n - 1, 0), 0)

    def body(sink_ref, q_ref, kc_ref, kp_ref, vc_ref, vp_ref, do_ref, cosc_ref, sinc_ref, cosp_ref, sinp_ref,
             out_ref, db_ref, dsink_ref, dq_keep, dk_keep, dv_keep):
        n = pl.program_id(0)

        @pl.when(n == 0)
        def _():
            for ref in (db_ref, dsink_ref, dq_keep, dk_keep, dv_keep):
                ref[...] = jnp.zeros_like(ref)

        valid = _band_mask(n > 0) & (n < nb)
        halves = _lane_halves()
        first = _first_half(WINDOW)
        slot = lax.broadcasted_iota(jnp.int32, dsink_ref.shape, 1)
        top = lax.broadcasted_iota(jnp.int32, dsink_ref.shape, 0) == 0
        dsink = jnp.zeros(dsink_ref.shape, F32)

        def emit(cols, done):
            out_ref[:, cols] = done.astype(ACT)
            db_ref[:, cols] += jnp.sum(done.astype(F32), axis=0, keepdims=True)

        for h in range(kd // PAIR):
            hs = slice(h * PAIR, (h + 1) * PAIR)
            k2 = jnp.concatenate([kp_ref[:, hs], kc_ref[:, hs]], axis=0)
            v2 = jnp.concatenate([vp_ref[:, hs], vc_ref[:, hs]], axis=0)
            qs, dos = _stack_heads(q_ref, h, halves), _stack_heads(do_ref, h, halves)
            pr, psink = _probs(qs, k2, valid, _sink_column(sink_ref, h))
            dp = _dot_nt(dos, v2)
            delta = jnp.sum(pr * dp, axis=-1, keepdims=True)
            ds = (pr * (dp - delta)).astype(ACT)
            leak = psink * delta
            for j in range(GQA_GROUP):
                share = jnp.sum(leak[j * WINDOW:(j + 1) * WINDOW], axis=0, keepdims=True)
                dsink = dsink - jnp.where(top & (slot == GQA_GROUP * h + j), share, 0.0)
            dqs = _dot(ds, k2)
            for p in range(2):
                ps = slice((2 * h + p) * PAIR, (2 * h + p + 1) * PAIR)
                dqp = jnp.where(halves[0], dqs[2 * p * WINDOW:(2 * p + 1) * WINDOW], dqs[(2 * p + 1) * WINDOW:(2 * p + 2) * WINDOW])
                emit(ps, dq_keep[:, ps])
                dq_keep[:, ps] = (_rope_t(dqp, cosc_ref[...], sinc_ref[...], first) * SCALE).astype(ACT)
            dk2 = _dot_tn(ds, qs)
            dv2 = _dot_tn(pr.astype(ACT), dos)
            ks = slice(d + h * PAIR, d + (h + 1) * PAIR)
            vs = slice(d + kd + h * PAIR, d + kd + (h + 1) * PAIR)
            emit(ks, dk_keep[:, hs] + _rope_t(dk2[:WINDOW], cosp_ref[...], sinp_ref[...], first))
            dk_keep[:, hs] = _rope_t(dk2[WINDOW:], cosc_ref[...], sinc_ref[...], first)
            emit(vs, dv_keep[:, hs] + dv2[:WINDOW])
            dv_keep[:, hs] = dv2[WINDOW:]
        dsink_ref[...] += dsink

    kv_c, kv_p = pl.BlockSpec((WINDOW, kd), cur), pl.BlockSpec((WINDOW, kd), prev)
    tab_c, tab_p = pl.BlockSpec((WINDOW, PAIR), cur), pl.BlockSpec((WINDOW, PAIR), prev)
    return pl.pallas_call(
        body, name="attn_bwd", grid=(nb + 1,),
        in_specs=[pl.BlockSpec(memory_space=pltpu.SMEM), pl.BlockSpec((WINDOW, d), cur), kv_c, kv_p, kv_c, kv_p,
                  pl.BlockSpec((WINDOW, d), cur), tab_c, tab_c, tab_p, tab_p],
        out_specs=[pl.BlockSpec((WINDOW, n_out), late), _full((1, n_out)), _full((8, LANES))],
        out_shape=[_sds((t, n_out), ACT), _sds((1, n_out), F32), _sds((8, LANES), F32)],
        scratch_shapes=[pltpu.VMEM((WINDOW, d), ACT), pltpu.VMEM((WINDOW, kd), F32), pltpu.VMEM((WINDOW, kd), F32)],
        compiler_params=_params("arbitrary"),
    )(sinks, q, kd_, kd_, vd, vd, do, cos, sin, cos, sin)


def _proj_res(a, w, b, g, x, name):
    t, k = a.shape
    d = w.shape[1]
    tm = _row_tile(t)
    has_bias = b is not None

    def body(*refs):
        a_ref, w_ref = refs[:2]
        b_ref = refs[2] if has_bias else None
        g_ref, x_ref, m_ref, xo_ref = refs[2 + has_bias:]
        m = _dot(a_ref[...], w_ref[...])
        if has_bias:
            m = m + b_ref[...]
        m_ref[...] = m
        xo_ref[...] = x_ref[...] + (m * _rstd(m)) * g_ref[...]

    ins = [a, w] + ([b] if has_bias else []) + [g, x]
    specs = [_rows(tm, k), _full((k, d))] + ([_full((1, d))] if has_bias else []) + [_full((1, d)), _rows(tm, d)]
    return pl.pallas_call(
        body, name=name, grid=(t // tm,), in_specs=specs,
        out_specs=[_rows(tm, d), _rows(tm, d)], out_shape=[_sds((t, d), F32)] * 2,
        compiler_params=_params("parallel"),
    )(*ins)


def _post_bwd(dres, m, g, w, mode, name, gu=None):
    t, d = dres.shape
    k = w.shape[0]
    tm = _row_tile(t)
    kc = _pick(k, (1408, 1024, 512))

    def body(*refs):
        d_ref, m_ref, g_ref, w_ref = refs[:4]
        gu_ref = refs[4] if mode == "ffn" else None
        outs = refs[4 + (mode == "ffn"):]
        dm_ref, da_ref, dg_ref = outs[:3]
        i = pl.program_id(0)

        @pl.when(i == 0)
        def _():
            dg_ref[...] = jnp.zeros_like(dg_ref)
            if mode == "attn":
                outs[3][...] = jnp.zeros_like(outs[3])

        dv, mv = d_ref[...], m_ref[...]
        r = _rstd(mv)
        mh = mv * r
        dg_ref[...] += jnp.sum(dv * mh, axis=0, keepdims=True)
        dm = _rms_bwd(mh, r, g_ref[...], dv)
        dmb = dm.astype(ACT)
        dm_ref[...] = dmb
        if mode == "attn":
            outs[3][...] += jnp.sum(dm, axis=0, keepdims=True)
        for c in range(0, k, kc):
            da = _dot_nt(dmb, w_ref[c:c + kc, :])
            if mode == "attn":
                da_ref[:, c:c + kc] = da.astype(ACT)
            elif mode == "sgu":
                da_ref[:, c:c + kc] = da
            else:
                gate = gu_ref[:, c:c + kc].astype(F32)
                up = gu_ref[:, k + c:k + c + kc].astype(F32)
                sg = _sigmoid(gate)
                da_ref[:, c:c + kc] = (da * up * (sg * (1.0 + gate * (1.0 - sg)))).astype(ACT)
                da_ref[:, k + c:k + c + kc] = (da * (gate * sg)).astype(ACT)

    ins = [dres, m, g, w]
    specs = [_rows(tm, d), _rows(tm, d), _full((1, d)), _full((k, d))]
    if mode == "ffn":
        ins.append(gu)
        specs.append(_rows(tm, 2 * k))
    da_cols, da_dtype = (2 * k, ACT) if mode == "ffn" else (k, ACT if mode == "attn" else F32)
    out_specs = [_rows(tm, d), _rows(tm, da_cols), _full((1, d))]
    out_shape = [_sds((t, d), ACT), _sds((t, da_cols), da_dtype), _sds((1, d), F32)]
    if mode == "attn":
        out_specs.append(_full((1, d)))
        out_shape.append(_sds((1, d), F32))
    return pl.pallas_call(
        body, name=name, grid=(t // tm,), in_specs=specs, out_specs=out_specs, out_shape=out_shape,
        compiler_params=_params("arbitrary"),
    )(*ins)


def _pre_bwd(dy, w, x, g, dres, name):
    t, n = dy.shape
    d = w.shape[0]
    tm = _row_tile(t)
    nc = _pick(n, (1408, 1024, 512))

    def body(dy_ref, w_ref, x_ref, g_ref, dres_ref, dx_ref, dg_ref):
        @pl.when(pl.program_id(0) == 0)
        def _():
            dg_ref[...] = jnp.zeros_like(dg_ref)

        dh = jnp.zeros((tm, d), F32)
        for c in range(0, n, nc):
            dh = dh + _dot_nt(dy_ref[:, c:c + nc], w_ref[:, c:c + nc])
        xv = x_ref[...]
        r = _rstd(xv)
        xh = xv * r
        dg_ref[...] += jnp.sum(dh * xh, axis=0, keepdims=True)
        dx_ref[...] = dres_ref[...] + _rms_bwd(xh, r, g_ref[...], dh)

    return pl.pallas_call(
        body, name=name, grid=(t // tm,),
        in_specs=[_rows(tm, n), _full((d, n)), _rows(tm, d), _full((1, d)), _rows(tm, d)],
        out_specs=[_rows(tm, d), _full((1, d))], out_shape=[_sds((t, d), F32), _sds((1, d), F32)],
        compiler_params=_params("arbitrary"),
    )(dy, w, x, g, dres)


def _wgrad(a, b, name):
    t, k = a.shape
    n = b.shape[1]
    tk = _pick(k, (1024, 1408))
    tn = _pick(n, (1024, 1408))
    tt = _pick(t, (1024,))
    steps = t // tt

    def body(a_ref, b_ref, o_ref, acc_ref):
        s = pl.program_id(2)

        @pl.when(s == 0)
        def _():
            acc_ref[...] = jnp.zeros_like(acc_ref)

        acc_ref[...] += _dot_tn(a_ref[...], b_ref[...])

        @pl.when(s == steps - 1)
        def _():
            o_ref[...] = acc_ref[...]

    return pl.pallas_call(
        body, name=name, grid=(k // tk, n // tn, steps),
        in_specs=[pl.BlockSpec((tt, tk), lambda i, j, s: (s, i)), pl.BlockSpec((tt, tn), lambda i, j, s: (s, j))],
        out_specs=pl.BlockSpec((tk, tn), lambda i, j, s: (i, j)), out_shape=_sds((k, n), F32),
        scratch_shapes=[pltpu.VMEM((tk, tn), F32)],
        compiler_params=_params("parallel", "parallel", "arbitrary"),
    )(a, b)


def _ffn_up_fwd(x, g, w):
    t, d = x.shape
    f = w.shape[1] // 2
    tm = _row_tile(t)
    fc = _pick(f, (1408, 1024, 512))

    def body(x_ref, g_ref, w_ref, h_ref, gu_ref, a_ref):
        xv = x_ref[...]
        hb = (xv * _rstd(xv) * g_ref[...]).astype(ACT)
        h_ref[...] = hb
        for c in range(0, f, fc):
            gate = _dot(hb, w_ref[:, c:c + fc])
            up = _dot(hb, w_ref[:, f + c:f + c + fc])
            gu_ref[:, c:c + fc] = gate.astype(ACT)
            gu_ref[:, f + c:f + c + fc] = up.astype(ACT)
            a_ref[:, c:c + fc] = (gate * _sigmoid(gate) * up).astype(ACT)

    return pl.pallas_call(
        body, name="ffn_up_fwd", grid=(t // tm,),
        in_specs=[_rows(tm, d), _full((1, d)), _full((d, 2 * f))],
        out_specs=[_rows(tm, d), _rows(tm, 2 * f), _rows(tm, f)],
        out_shape=[_sds((t, d), ACT), _sds((t, 2 * f), ACT), _sds((t, f), ACT)],
        compiler_params=_params("parallel"),
    )(x, g, w)


def _sgu_in_fwd(x, g, w):
    t, d = x.shape
    n = w.shape[1]
    tm = _row_tile(t)
    nc = _pick(n, (1024, 512))

    def body(x_ref, g_ref, w_ref, h_ref, z_ref):
        xv = x_ref[...]
        hb = (xv * _rstd(xv) * g_ref[...]).astype(ACT)
        h_ref[...] = hb
        for c in range(0, n, nc):
            z_ref[:, c:c + nc] = _dot(hb, w_ref[:, c:c + nc])

    return pl.pallas_call(
        body, name="sgu_in_fwd", grid=(t // tm,),
        in_specs=[_rows(tm, d), _full((1, d)), _full((d, n))],
        out_specs=[_rows(tm, d), _rows(tm, n)], out_shape=[_sds((t, d), ACT), _sds((t, n), F32)],
        compiler_params=_params("parallel"),
    )(x, g, w)


def _causal(ws):
    row = lax.broadcasted_iota(jnp.int32, ws.shape, 0)
    col = lax.broadcasted_iota(jnp.int32, ws.shape, 1)
    return jnp.where(col <= row, ws, 0.0)


def _sgu_ln(v, lg, lb):
    mu = jnp.mean(v, axis=-1, keepdims=True)
    vc = v - mu
    rs = lax.rsqrt(jnp.mean(vc * vc, axis=-1, keepdims=True) + EPS)
    vh = vc * rs
    return vh, rs, vh * lg + lb


def _sgu_mix_fwd(z, lg, lb, ws, bs_t):
    t, n = z.shape
    d = n // 2
    groups = d // WINDOW
    tm = _row_tile(t)

    def body(z_ref, lg_ref, lb_ref, ws_ref, bs_ref, y_ref):
        for c in range(0, tm, WINDOW):
            u = _gelu(z_ref[c:c + WINDOW, :d])
            _, _, vn = _sgu_ln(_gelu(z_ref[c:c + WINDOW, d:]), lg_ref[...], lb_ref[...])
            for gi in range(groups):
                gs = slice(gi * WINDOW, (gi + 1) * WINDOW)
                mixed = _dot(_causal(ws_ref[gi]).astype(ACT), vn[:, gs].astype(ACT)) + bs_ref[:, gi:gi + 1]
                y_ref[c:c + WINDOW, gs] = (u[:, gs] * mixed).astype(ACT)

    return pl.pallas_call(
        body, name="sgu_mix_fwd", grid=(t // tm,),
        in_specs=[_rows(tm, n), _full((1, d)), _full((1, d)), _full((groups, WINDOW, WINDOW)), _full((WINDOW, groups))],
        out_specs=_rows(tm, d), out_shape=_sds((t, d), ACT),
        compiler_params=_params("parallel"),
    )(z, lg, lb, ws, bs_t)


def _sgu_mix_bwd(z, dy, lg, lb, ws, bs_t):
    t, n = z.shape
    d = n // 2
    groups = d // WINDOW
    tm = _row_tile(t)

    def body(z_ref, dy_ref, lg_ref, lb_ref, ws_ref, bs_ref, dz_ref, dlg_ref, dlb_ref, dws_ref, dbs_ref):
        @pl.when(pl.program_id(0) == 0)
        def _():
            for ref in (dlg_ref, dlb_ref, dws_ref, dbs_ref):
                ref[...] = jnp.zeros_like(ref)

        lane = lax.broadcasted_iota(jnp.int32, (WINDOW, groups), 1)
        for c in range(0, tm, WINDOW):
            rows = slice(c, c + WINDOW)
            zu, zv = z_ref[rows, :d], z_ref[rows, d:]
            u = _gelu(zu)
            vh, rs, vn = _sgu_ln(_gelu(zv), lg_ref[...], lb_ref[...])
            dyv = dy_ref[rows, :]
            dvn_parts = []
            for gi in range(groups):
                gs = slice(gi * WINDOW, (gi + 1) * WINDOW)
                wsg = _causal(ws_ref[gi]).astype(ACT)
                vng = vn[:, gs].astype(ACT)
                mixed = _dot(wsg, vng) + bs_ref[:, gi:gi + 1]
                dz_ref[rows, gs] = (dyv[:, gs] * mixed * _gelu_grad(zu[:, gs])).astype(ACT)
                dmix = dyv[:, gs] * u[:, gs]
                dmb = dmix.astype(ACT)
                dvn_parts.append(_dot_tn(wsg, dmb))
                dws_ref[gi] += _dot_nt(dmb, vng)
                dbs_ref[...] += jnp.where(lane == gi, jnp.sum(dmix, axis=-1, keepdims=True), 0.0)
            dvn = jnp.concatenate(dvn_parts, axis=-1)
            dlg_ref[...] += jnp.sum(dvn * vh, axis=0, keepdims=True)
            dlb_ref[...] += jnp.sum(dvn, axis=0, keepdims=True)
            dvh = dvn * lg_ref[...]
            dv = rs * (dvh - jnp.mean(dvh, axis=-1, keepdims=True) - vh * jnp.mean(dvh * vh, axis=-1, keepdims=True))
            dz_ref[rows, d:] = (dv * _gelu_grad(zv)).astype(ACT)

    vec = _full((1, d))
    return pl.pallas_call(
        body, name="sgu_mix_bwd", grid=(t // tm,),
        in_specs=[_rows(tm, n), _rows(tm, d), vec, vec, _full((groups, WINDOW, WINDOW)), _full((WINDOW, groups))],
        out_specs=[_rows(tm, n), vec, vec, _full((groups, WINDOW, WINDOW)), _full((WINDOW, groups))],
        out_shape=[_sds((t, n), ACT), _sds((1, d), F32), _sds((1, d), F32), _sds((groups, WINDOW, WINDOW), F32),
                   _sds((WINDOW, groups), F32)],
        compiler_params=_params("arbitrary"),
    )(z, dy, lg, lb, ws, bs_t)


def _loss_head(y, target):
    t, d = y.shape
    tm = _row_tile(t)

    def body(y_ref, t_ref, dy_ref, loss_ref):
        @pl.when(pl.program_id(0) == 0)
        def _():
            loss_ref[...] = jnp.zeros_like(loss_ref)

        err = y_ref[...] - t_ref[...]
        dy_ref[...] = err * (1.0 / d)
        per_token = jnp.mean(err * err, axis=-1, keepdims=True)
        loss_ref[...] += 0.5 * jnp.sum(per_token, axis=0, keepdims=True)

    return pl.pallas_call(
        body, name="loss_head", grid=(t // tm,),
        in_specs=[_rows(tm, d), _rows(tm, d)],
        out_specs=[_rows(tm, d), _full((8, LANES))], out_shape=[_sds((t, d), F32), _sds((8, LANES), F32)],
        compiler_params=_params("arbitrary"),
    )(y, target)


def _rope_tables(t):
    half = HEAD_DIM // 2
    inv_freq = ROPE_THETA ** (-(jnp.arange(half, dtype=F32) * 2.0) / HEAD_DIM)
    ang = jnp.arange(t, dtype=jnp.int32).astype(F32)[:, None] * inv_freq[None, :]
    cos, sin = jnp.cos(ang), jnp.sin(ang)
    return jnp.tile(jnp.concatenate([cos, cos], axis=1), (1, 2)), jnp.tile(jnp.concatenate([-sin, sin], axis=1), (1, 2))


def _dup_heads(w, d):
    kv = (w.shape[-1] - d) // 2
    lead = w.shape[:-1]

    def dup(part):
        heads = part.reshape(lead + (kv // HEAD_DIM, 1, HEAD_DIM))
        return jnp.broadcast_to(heads, lead + (kv // HEAD_DIM, 2, HEAD_DIM)).reshape(lead + (2 * kv,))

    return jnp.concatenate([w[..., :d], dup(w[..., d:d + kv]), dup(w[..., d + kv:])], axis=-1)


def _fold_heads(dw, d):
    kv = (dw.shape[-1] - d) // 4
    lead = dw.shape[:-1]

    def fold(part):
        return part.reshape(lead + (kv // HEAD_DIM, 2, HEAD_DIM)).sum(axis=-2).reshape(lead + (kv,))

    return jnp.concatenate([dw[..., :d], fold(dw[..., d:d + 2 * kv]), fold(dw[..., d + 2 * kv:])], axis=-1)


def _local_step(x, target, p):
    t, d = x.shape
    row = lambda v: v.reshape(1, -1).astype(F32)
    cos, sin = _rope_tables(t)
    w_qkv = _dup_heads(p["attn_w_qkv"], d)
    b_qkv = _dup_heads(row(p["attn_b_qkv"]), d)
    bs_t = p["sgu_b_spatial"].astype(F32).T
    g_mix_pre, g_mix_post = p["norm_mix_pre"], p["norm_mix_post"]
    g_ffn_pre, g_ffn_post = p["norm_ffn_pre"], p["norm_ffn_post"]

    h0, q, kdup, vdup = _attn_qkv_fwd(x, row(g_mix_pre[0]), w_qkv, b_qkv, cos, sin)
    o = _attn_fwd(q, kdup, vdup, p["attn_sinks"].reshape(1, -1).astype(F32))
    m0, x1 = _proj_res(o, p["attn_w_o"], row(p["attn_b_o"]), row(g_mix_post[0]), x, "attn_out_fwd")
    h1, gu0, a0 = _ffn_up_fwd(x1, row(g_ffn_pre[0]), p["ffn_w_gate_up"][0])
    f0, x2 = _proj_res(a0, p["ffn_w_down"][0], None, row(g_ffn_post[0]), x1, "ffn_down_fwd0")
    h2, z = _sgu_in_fwd(x2, row(g_mix_pre[1]), p["sgu_w_in"])
    lg, lb = row(p["sgu_ln_g"]), row(p["sgu_ln_b"])
    ws = p["sgu_w_spatial"].astype(F32)
    y = _sgu_mix_fwd(z, lg, lb, ws, bs_t)
    m1, x3 = _proj_res(y, p["sgu_w_out"], None, row(g_mix_post[1]), x2, "sgu_out_fwd")
    h3, gu1, a1 = _ffn_up_fwd(x3, row(g_ffn_pre[1]), p["ffn_w_gate_up"][1])
    f1, x4 = _proj_res(a1, p["ffn_w_down"][1], None, row(g_ffn_post[1]), x3, "ffn_down_fwd1")
    dx4, loss_tile = _loss_head(x4, target)

    df1, dgu1, dg_ffn_post1 = _post_bwd(dx4, f1, row(g_ffn_post[1]), p["ffn_w_down"][1], "ffn", "ffn_down_bwd1", gu1)
    dw_down1 = _wgrad(a1, df1, "ffn_down_wgrad1")
    dw_gu1 = _wgrad(h3, dgu1, "ffn_up_wgrad1")
    dx3, dg_ffn_pre1 = _pre_bwd(dgu1, p["ffn_w_gate_up"][1], x3, row(g_ffn_pre[1]), dx4, "ffn_up_bwd1")
    dm1, dy, dg_mix_post1 = _post_bwd(dx3, m1, row(g_mix_post[1]), p["sgu_w_out"], "sgu", "sgu_out_bwd")
    dw_out = _wgrad(y, dm1, "sgu_out_wgrad")
    dz, dlg, dlb, dws, dbs_t = _sgu_mix_bwd(z, dy, lg, lb, ws, bs_t)
    dw_in = _wgrad(h2, dz, "sgu_in_wgrad")
    dx2, dg_mix_pre1 = _pre_bwd(dz, p["sgu_w_in"], x2, row(g_mix_pre[1]), dx3, "sgu_in_bwd")
    df0, dgu0, dg_ffn_post0 = _post_bwd(dx2, f0, row(g_ffn_post[0]), p["ffn_w_down"][0], "ffn", "ffn_down_bwd0", gu0)
    dw_down0 = _wgrad(a0, df0, "ffn_down_wgrad0")
    dw_gu0 = _wgrad(h1, dgu0, "ffn_up_wgrad0")
    dx1, dg_ffn_pre0 = _pre_bwd(dgu0, p["ffn_w_gate_up"][0], x1, row(g_ffn_pre[0]), dx2, "ffn_up_bwd0")
    dm0, do, dg_mix_post0, db_o = _post_bwd(dx1, m0, row(g_mix_post[0]), p["attn_w_o"], "attn", "attn_out_bwd")
    dw_o = _wgrad(o, dm0, "attn_out_wgrad")
    dqkv, db_qkv, dsink = _attn_bwd(q, kdup, vdup, do, p["attn_sinks"].reshape(1, -1).astype(F32), cos, sin)
    dw_qkv = _wgrad(h0, dqkv, "attn_qkv_wgrad")
    dx0, dg_mix_pre0 = _pre_bwd(dqkv, w_qkv, x, row(g_mix_pre[0]), dx1, "attn_qkv_bwd")

    causal = jnp.tril(jnp.ones((WINDOW, WINDOW), F32))
    grads = {
        "norm_mix_pre": jnp.concatenate([dg_mix_pre0, dg_mix_pre1], axis=0),
        "norm_mix_post": jnp.concatenate([dg_mix_post0, dg_mix_post1], axis=0),
        "norm_ffn_pre": jnp.concatenate([dg_ffn_pre0, dg_ffn_pre1], axis=0),
        "norm_ffn_post": jnp.concatenate([dg_ffn_post0, dg_ffn_post1], axis=0),
        "attn_w_qkv": _fold_heads(dw_qkv, d),
        "attn_b_qkv": _fold_heads(db_qkv, d),
        "attn_sinks": dsink[:1, :d // HEAD_DIM],
        "attn_w_o": dw_o,
        "attn_b_o": db_o,
        "sgu_w_in": dw_in,
        "sgu_ln_g": dlg,
        "sgu_ln_b": dlb,
        "sgu_w_spatial": dws * causal[None],
        "sgu_b_spatial": dbs_t.T,
        "sgu_w_out": dw_out,
        "ffn_w_gate_up": jnp.stack([dw_gu0, dw_gu1]),
        "ffn_w_down": jnp.stack([dw_down0, dw_down1]),
    }
    return loss_tile[0, 0], dx0, grads


ANY = pl.BlockSpec(memory_space=pl.ANY)
IN_VMEM = pl.BlockSpec(memory_space=pltpu.VMEM)
AG_COPIES = 7


def _place():
    return lax.axis_index("x"), lax.axis_index("y"), lax.axis_index("c")


def _other_chips(x, y):
    return [(1 - x, y), (x, 1 - y), (1 - x, 1 - y)]


def _all_gather(blocks, dtypes, name):
    n = len(blocks)

    def body(*refs):
        ins, outs, stage = refs[:n], refs[n:2 * n], refs[2 * n:3 * n]
        send, recv, local_sem = refs[3 * n:]
        x, y, c = _place()
        sibling = (x, y, 1 - c)
        chips = _other_chips(x, y)
        mine = 4 * x + 2 * y + c

        def copy(a, k, block, to, src=None):
            dst = outs[a].at[block]
            return pltpu.make_async_remote_copy(
                src_ref=dst if src is None else src, dst_ref=dst, send_sem=send.at[a * AG_COPIES + k],
                recv_sem=recv.at[a * AG_COPIES + k], device_id=to, device_id_type=MESH)

        started = []
        for a in range(n):
            stage[a][...] = ins[a][...].astype(dtypes[a])
            own = pltpu.make_async_copy(stage[a], outs[a].at[mine], local_sem.at[a])
            own.start()
            started.append(own)
        sends = []
        for a in range(n):
            sends.append(copy(a, 0, mine, sibling, src=stage[a]))
            sends += [copy(a, 1 + j, mine, (*chip, c), src=stage[a]) for j, chip in enumerate(chips)]
        for cp in sends:
            cp.start()
        for j, (px, py) in enumerate(chips):
            block = 4 * px + 2 * py + c
            for a in range(n):
                copy(a, 1 + j, block, (x, y, c)).wait_recv()
                forward = copy(a, 4 + j, block, sibling)
                forward.start()
                sends.append(forward)
        for a in range(n):
            copy(a, 0, 4 * x + 2 * y + (1 - c), (x, y, c)).wait_recv()
            for j, (px, py) in enumerate(chips):
                copy(a, 4 + j, 4 * px + 2 * py + (1 - c), (x, y, c)).wait_recv()
        for cp in sends:
            cp.wait_send()
        for own in started:
            own.wait()

    return pl.pallas_call(
        body, name=name,
        in_specs=[IN_VMEM] * n, out_specs=[ANY] * n,
        out_shape=[_sds((N_DEV,) + b.shape, dt) for b, dt in zip(blocks, dtypes)],
        scratch_shapes=[pltpu.VMEM(b.shape, dt) for b, dt in zip(blocks, dtypes)]
        + [pltpu.SemaphoreType.DMA((n * AG_COPIES,)), pltpu.SemaphoreType.DMA((n * AG_COPIES,)), pltpu.SemaphoreType.DMA((n,))],
        compiler_params=pltpu.CompilerParams(vmem_limit_bytes=VMEM_LIMIT_BYTES),
    )(*blocks)


def _exchange(arrays, n_slots, route, name):
    n = len(arrays)

    def body(*refs):
        ins, outs = refs[:n], refs[n:2 * n]
        send, recv = refs[2 * n:]
        x, y, c = _place()
        copies = []
        for a in range(n):
            for s in range(n_slots):
                block, to = route(s, x, y, c)
                copies.append(pltpu.make_async_remote_copy(
                    src_ref=ins[a].at[block], dst_ref=outs[a].at[s], send_sem=send.at[a * n_slots + s],
                    recv_sem=recv.at[a * n_slots + s], device_id=to, device_id_type=MESH))
        for cp in copies:
            cp.start()
        for cp in copies:
            cp.wait()

    return pl.pallas_call(
        body, name=name, in_specs=[ANY] * n, out_specs=[ANY] * n,
        out_shape=[_sds((n_slots,) + v.shape[1:], v.dtype) for v in arrays],
        scratch_shapes=[pltpu.SemaphoreType.DMA((n * n_slots,)), pltpu.SemaphoreType.DMA((n * n_slots,))],
    )(*arrays)


def _to_sibling(s, x, y, c):
    return 2 * s + (1 - c), (x, y, 1 - c)


def _to_owner_chip(s, x, y, c):
    px, py = _other_chips(x, y)[s]
    return 2 * px + py, (px, py, c)


def _pair_sum(g, r, core, name):
    _, _, rows, cols = g.shape
    tr = _pick(rows, (512, 256, 128))

    def body(core_ref, g_ref, r_ref, p_ref):
        del core_ref
        p_ref[...] = (g_ref[...].astype(F32) + r_ref[...].astype(F32)).astype(p_ref.dtype)

    return pl.pallas_call(
        body, name=name,
        grid_spec=pltpu.PrefetchScalarGridSpec(
            num_scalar_prefetch=1, grid=(4, rows // tr),
            in_specs=[pl.BlockSpec((None, None, tr, cols), lambda q, i, core_ref: (q, core_ref[0], i, 0)),
                      pl.BlockSpec((None, tr, cols), lambda q, i, core_ref: (q, i, 0))],
            out_specs=pl.BlockSpec((None, tr, cols), lambda q, i, core_ref: (q, i, 0))),
        out_shape=_sds((4, rows, cols), g.dtype),
        compiler_params=_params("parallel", "parallel"),
    )(core, g, r)


def _adam(w, g, m, v):
    m2 = ADAM_B1 * m + (1.0 - ADAM_B1) * g
    v2 = ADAM_B2 * v + (1.0 - ADAM_B2) * (g * g)
    m_hat = m2 / (1.0 - ADAM_B1 ** ADAM_STEP)
    v_hat = v2 / (1.0 - ADAM_B2 ** ADAM_STEP)
    return -ADAM_LR * (m_hat / (jnp.sqrt(v_hat) + ADAM_EPS) + ADAM_WD * w), m2, v2


def _shard_update(p, qsum, chip, w, m, v, name):
    rows, cols = w.shape
    tr = _pick(rows, (512, 256, 128))

    def body(chip_ref, p_ref, q_ref, w_ref, m_ref, v_ref, g_out, d_out, m_out, v_out):
        del chip_ref
        g = p_ref[...].astype(F32)
        for s in range(3):
            g = g + q_ref[s].astype(F32)
        g_out[...] = g
        d_out[...], m_out[...], v_out[...] = _adam(w_ref[...], g, m_ref[...], v_ref[...])

    tile = pl.BlockSpec((tr, cols), lambda i, chip_ref: (i, 0))
    return pl.pallas_call(
        body, name=name,
        grid_spec=pltpu.PrefetchScalarGridSpec(
            num_scalar_prefetch=1, grid=(rows // tr,),
            in_specs=[pl.BlockSpec((None, tr, cols), lambda i, chip_ref: (chip_ref[0], i, 0)),
                      pl.BlockSpec((3, tr, cols), lambda i, chip_ref: (0, i, 0)), tile, tile, tile],
            out_specs=[tile] * 4),
        out_shape=[_sds((rows, cols), F32)] * 4,
        compiler_params=_params("parallel"),
    )(chip, p, qsum, w, m, v)


def _replicated_update(gathered, w, m, v):
    rows, cols = w.shape

    def body(s_ref, w_ref, m_ref, v_ref, g_out, d_out, m_out, v_out):
        g = s_ref[0]
        for dev in range(1, N_DEV):
            g = g + s_ref[dev]
        g_out[...] = g
        d_out[...], m_out[...], v_out[...] = _adam(w_ref[...], g, m_ref[...], v_ref[...])

    return pl.pallas_call(
        body, name="replicated_update", out_shape=[_sds((rows, cols), F32)] * 4,
        compiler_params=pltpu.CompilerParams(vmem_limit_bytes=VMEM_LIMIT_BYTES),
    )(gathered, w, m, v)


COLUMN_SHARDED = ("attn_w_qkv", "sgu_w_in", "ffn_w_gate_up")
ROW_SHARDED = ("attn_w_o", "sgu_w_out", "ffn_w_down")
SHARDED = ("attn_w_qkv", "attn_w_o", "sgu_w_in", "sgu_ln", "sgu_w_out", "ffn_w_gate_up", "ffn_w_down")
REPLICATED = ("norm_mix_pre", "norm_mix_post", "norm_ffn_pre", "norm_ffn_post", "attn_b_qkv", "attn_sinks", "attn_b_o",
              "sgu_w_spatial", "sgu_b_spatial")
WEIGHTS = ("norm_mix_pre", "norm_mix_post", "norm_ffn_pre", "norm_ffn_post", "attn_w_qkv", "attn_b_qkv", "attn_sinks", "attn_w_o",
           "attn_b_o", "sgu_w_in", "sgu_ln_g", "sgu_ln_b", "sgu_w_spatial", "sgu_b_spatial", "sgu_w_out", "ffn_w_gate_up", "ffn_w_down")


def _as_block(name, shard):
    return shard.reshape(-1, shard.shape[-1])


def _unblock_full(name, gathered, layers):
    _, rc, c = gathered.shape
    r = rc // layers
    g = gathered.reshape(N_DEV, layers, r, c)
    if name in COLUMN_SHARDED:
        full = jnp.transpose(g, (1, 2, 0, 3)).reshape(layers, r, N_DEV * c)
    else:
        full = jnp.transpose(g, (1, 0, 2, 3)).reshape(layers, N_DEV * r, c)
    return full


def _block_full(name, full, dtype):
    layers, rows, cols = full.shape
    if name in COLUMN_SHARDED:
        c = cols // N_DEV
        g = jnp.transpose(full.reshape(layers, rows, N_DEV, c), (2, 0, 1, 3)).reshape(N_DEV, layers * rows, c)
    else:
        r = rows // N_DEV
        g = jnp.transpose(full.reshape(layers, N_DEV, r, cols), (1, 0, 2, 3)).reshape(N_DEV, layers * r, cols)
    return g.astype(dtype)


def _packed_rows(size, width):
    return -(-size // (8 * width)) * 8


def _pack_rows(parts, width):
    rows = []
    for v in parts:
        flat = v.reshape(-1).astype(F32)
        n_rows = _packed_rows(flat.shape[0], width)
        rows.append(jnp.pad(flat, (0, n_rows * width - flat.shape[0])).reshape(n_rows, width))
    return jnp.concatenate(rows, axis=0)


def _unpack_rows(packed, like, width):
    out, at = [], 0
    for v in like:
        size = math.prod(v.shape)
        out.append(packed[at:at + -(-size // width)].reshape(-1)[:size].reshape(v.shape))
        at += _packed_rows(size, width)
    return out


def kernel(x, norm_mix_pre, norm_mix_post, norm_ffn_pre, norm_ffn_post, attn_w_qkv, attn_b_qkv, attn_sinks, attn_w_o, attn_b_o, sgu_w_in, sgu_ln_g, sgu_ln_b, sgu_w_spatial, sgu_b_spatial, sgu_w_out, ffn_w_gate_up, ffn_w_down, loss_target, m_norm_mix_pre, m_norm_mix_post, m_norm_ffn_pre, m_norm_ffn_post, m_attn_w_qkv, m_attn_b_qkv, m_attn_sinks, m_attn_w_o, m_attn_b_o, m_sgu_w_in, m_sgu_ln_g, m_sgu_ln_b, m_sgu_w_spatial, m_sgu_b_spatial, m_sgu_w_out, m_ffn_w_gate_up, m_ffn_w_down, v_norm_mix_pre, v_norm_mix_post, v_norm_ffn_pre, v_norm_ffn_post, v_attn_w_qkv, v_attn_b_qkv, v_attn_sinks, v_attn_w_o, v_attn_b_o, v_sgu_w_in, v_sgu_ln_g, v_sgu_ln_b, v_sgu_w_spatial, v_sgu_b_spatial, v_sgu_w_out, v_ffn_w_gate_up, v_ffn_w_down):
    w = dict(norm_mix_pre=norm_mix_pre, norm_mix_post=norm_mix_post, norm_ffn_pre=norm_ffn_pre, norm_ffn_post=norm_ffn_post,
             attn_w_qkv=attn_w_qkv, attn_b_qkv=attn_b_qkv, attn_sinks=attn_sinks, attn_w_o=attn_w_o, attn_b_o=attn_b_o,
             sgu_w_in=sgu_w_in, sgu_ln_g=sgu_ln_g, sgu_ln_b=sgu_ln_b, sgu_w_spatial=sgu_w_spatial, sgu_b_spatial=sgu_b_spatial,
             sgu_w_out=sgu_w_out, ffn_w_gate_up=ffn_w_gate_up, ffn_w_down=ffn_w_down)
    mom = dict(norm_mix_pre=m_norm_mix_pre, norm_mix_post=m_norm_mix_post, norm_ffn_pre=m_norm_ffn_pre, norm_ffn_post=m_norm_ffn_post,
               attn_w_qkv=m_attn_w_qkv, attn_b_qkv=m_attn_b_qkv, attn_sinks=m_attn_sinks, attn_w_o=m_attn_w_o, attn_b_o=m_attn_b_o,
               sgu_w_in=m_sgu_w_in, sgu_ln_g=m_sgu_ln_g, sgu_ln_b=m_sgu_ln_b, sgu_w_spatial=m_sgu_w_spatial,
               sgu_b_spatial=m_sgu_b_spatial, sgu_w_out=m_sgu_w_out, ffn_w_gate_up=m_ffn_w_gate_up, ffn_w_down=m_ffn_w_down)
    var = dict(norm_mix_pre=v_norm_mix_pre, norm_mix_post=v_norm_mix_post, norm_ffn_pre=v_norm_ffn_pre, norm_ffn_post=v_norm_ffn_post,
               attn_w_qkv=v_attn_w_qkv, attn_b_qkv=v_attn_b_qkv, attn_sinks=v_attn_sinks, attn_w_o=v_attn_w_o, attn_b_o=v_attn_b_o,
               sgu_w_in=v_sgu_w_in, sgu_ln_g=v_sgu_ln_g, sgu_ln_b=v_sgu_ln_b, sgu_w_spatial=v_sgu_w_spatial,
               sgu_b_spatial=v_sgu_b_spatial, sgu_w_out=v_sgu_w_out, ffn_w_gate_up=v_ffn_w_gate_up, ffn_w_down=v_ffn_w_down)

    def shard_of(tree, name):
        if name == "sgu_ln":
            return jnp.concatenate([tree["sgu_ln_g"], tree["sgu_ln_b"]], axis=0)
        return tree[name]

    dtypes = [F32 if name == "sgu_ln" else ACT for name in SHARDED]
    gathered = _all_gather([_as_block(name, shard_of(w, name)) for name in SHARDED], dtypes, "weights_all_gather")
    full = {name: w[name] for name in REPLICATED}
    for name, g in zip(SHARDED, gathered):
        if name == "sgu_ln":
            full["sgu_ln_g"], full["sgu_ln_b"] = g[:, 0, :].reshape(-1), g[:, 1, :].reshape(-1)
        else:
            layers = w[name].shape[0]
            weight = _unblock_full(name, g, layers)
            full[name] = weight if name.startswith("ffn") else weight[0]
    full["sgu_w_spatial"], full["sgu_b_spatial"] = sgu_w_spatial[0], sgu_b_spatial[0]
    full["attn_b_qkv"], full["attn_b_o"], full["attn_sinks"] = attn_b_qkv[0], attn_b_o[0], attn_sinks[0]

    loss_share, dx, grads = _local_step(x[0], loss_target[0], full)

    blocked = []
    for name in SHARDED:
        if name == "sgu_ln":
            blocked.append(jnp.stack([grads["sgu_ln_g"].reshape(N_DEV, -1), grads["sgu_ln_b"].reshape(N_DEV, -1)], axis=1))
        else:
            g = grads[name]
            blocked.append(_block_full(name, g if g.ndim == 3 else g[None], ACT))
    core = lax.axis_index("c").astype(jnp.int32).reshape(1)
    chip = (2 * lax.axis_index("x") + lax.axis_index("y")).astype(jnp.int32).reshape(1)
    from_sibling = _exchange(blocked, 4, _to_sibling, "grads_to_sibling")
    partial = [_pair_sum(g.reshape((4, 2) + g.shape[1:]), r, core, "grads_pair_sum_" + name)
               for name, g, r in zip(SHARDED, blocked, from_sibling)]
    from_chips = _exchange(partial, 3, _to_owner_chip, "grads_to_owner_chip")

    out_g, out_d, out_m, out_v = {}, {}, {}, {}
    for name, p, qsum in zip(SHARDED, partial, from_chips):
        shards = [_as_block(name, shard_of(tree, name)) for tree in (w, mom, var)]
        res = _shard_update(p, qsum, chip, *shards, "shard_update_" + name)
        for out, val in zip((out_g, out_d, out_m, out_v), res):
            if name == "sgu_ln":
                out["sgu_ln_g"], out["sgu_ln_b"] = val[0:1], val[1:2]
            else:
                out[name] = val.reshape(w[name].shape)

    width = x.shape[-1]
    like = [w[name] for name in REPLICATED] + [loss_share.reshape(1)]
    shares, = _all_gather([_pack_rows([grads[name] for name in REPLICATED] + [loss_share], width)], [F32],
                          "replicated_grads_all_gather")
    packed = [_pack_rows([tree[name] for name in REPLICATED] + [jnp.zeros((1,), F32)], width) for tree in (w, mom, var)]
    res = _replicated_update(shares, *packed)
    loss = _unpack_rows(res[0], like, width)[-1][0]
    for out, val in zip((out_g, out_d, out_m, out_v), res):
        for name, leaf in zip(REPLICATED, _unpack_rows(val, like, width)):
            out[name] = leaf

    return (loss, dx[None], *[out_g[n] for n in WEIGHTS], *[out_d[n] for n in WEIGHTS],
            *[out_m[n] for n in WEIGHTS], *[out_v[n] for n in WEIGHTS])
```

```python
import functools
import math
import operator

import jax
import jax.numpy as jnp
from jax import lax
from jax.experimental import pallas as pl
from jax.experimental.pallas import tpu as pltpu

F32 = jnp.float32
ACT = jnp.bfloat16

EPS = 1e-6
HEAD_DIM = 64
PAIR = 2 * HEAD_DIM
GQA_GROUP = 4
WINDOW = 128
ROPE_THETA = 10000.0
SCALE = HEAD_DIM ** -0.5
MASKED = -1e30
LANES = 128

ADAM_LR, ADAM_B1, ADAM_B2, ADAM_EPS, ADAM_WD, ADAM_STEP = 0.001, 0.9, 0.999, 1e-08, 0.01, 10

N_DEV = 8
VMEM_LIMIT_BYTES = 56 * 1024 * 1024
MESH = pl.DeviceIdType.MESH
ANY = pl.BlockSpec(memory_space=pl.ANY)
IN_VMEM = pl.BlockSpec(memory_space=pltpu.VMEM)


def _pick(n, candidates):
    for c in candidates:
        if n % c == 0:
            return c
    return n


def _row_tile(t):
    return _pick(t, (512,))


def _params(*sem):
    return pltpu.CompilerParams(dimension_semantics=sem, vmem_limit_bytes=VMEM_LIMIT_BYTES)


def _full(shape):
    return pl.BlockSpec(shape, lambda *_: (0,) * len(shape))


def _rows(tm, n):
    return pl.BlockSpec((tm, n), lambda i: (i, 0))


def _sds(shape, dtype):
    return jax.ShapeDtypeStruct(shape, dtype)


def _place():
    return lax.axis_index("x"), lax.axis_index("y"), lax.axis_index("c")


def _other_chips(x, y):
    return [(1 - x, y), (x, 1 - y), (1 - x, 1 - y)]


class _Job:
    def __init__(self, inputs, out_shape, aliases, emit, n_remote, n_local=0):
        self.inputs, self.out_shape, self.aliases, self.emit = list(inputs), list(out_shape), dict(aliases), emit
        self.n_remote, self.n_local = n_remote, n_local


def _call(body, name, grid, in_specs, out_specs, out_shape, args, sem, scratch=(), jobs=()):
    n_in, n_out, n_scr = len(args), len(out_shape), len(scratch)
    j_in = [a for job in jobs for a in job.inputs]
    j_out = [s for job in jobs for s in job.out_shape]
    aliases, at_in, at_out = {}, n_in, n_out
    for job in jobs:
        aliases.update({at_in + i: at_out + o for i, o in job.aliases.items()})
        at_in, at_out = at_in + len(job.inputs), at_out + len(job.out_shape)
    n_remote = sum(job.n_remote for job in jobs)
    n_local = sum(job.n_local for job in jobs)

    def wrapped(*refs):
        ins, jin = refs[:n_in], refs[n_in:n_in + len(j_in)]
        rest = refs[n_in + len(j_in):]
        outs, jout = rest[:n_out], rest[n_out:n_out + len(j_out)]
        scr = rest[n_out + len(j_out):n_out + len(j_out) + n_scr]
        if not jobs:
            body(*ins, *outs, *scr)
            return
        send, recv, local = rest[n_out + len(j_out) + n_scr:]
        ids = [pl.program_id(a) for a in range(len(grid))]
        first = functools.reduce(operator.and_, [i == 0 for i in ids])
        last = functools.reduce(operator.and_, [i == g - 1 for i, g in zip(ids, grid)])

        def descriptors():
            place, found, i, o, r, l = _place(), [], 0, 0, 0, 0
            for job in jobs:
                for src, dst, to in job.emit(jin[i:i + len(job.inputs)], jout[o:o + len(job.out_shape)], place):
                    if to is None:
                        found.append(pltpu.make_async_copy(src, dst, local.at[l]))
                        l += 1
                    else:
                        found.append(pltpu.make_async_remote_copy(src_ref=src, dst_ref=dst, send_sem=send.at[r], recv_sem=recv.at[r],
                                                                  device_id=to, device_id_type=MESH))
                        r += 1
                i, o = i + len(job.inputs), o + len(job.out_shape)
            return found

        @pl.when(first)
        def _():
            for cp in descriptors():
                cp.start()

        body(*ins, *outs, *scr)

        @pl.when(last)
        def _():
            for cp in descriptors():
                cp.wait()

    sems = [pltpu.SemaphoreType.DMA((n_remote,)), pltpu.SemaphoreType.DMA((n_remote,)),
            pltpu.SemaphoreType.DMA((max(n_local, 1),))] if jobs else []
    res = pl.pallas_call(
        wrapped, name=name, grid=grid,
        in_specs=list(in_specs) + [ANY] * len(j_in), out_specs=list(out_specs) + [ANY] * len(j_out),
        out_shape=list(out_shape) + j_out, scratch_shapes=list(scratch) + sems, input_output_aliases=aliases,
        compiler_params=_params(*(("arbitrary",) * len(grid) if jobs else sem)),
    )(*args, *j_in)
    job_res, at = [], n_out
    for job in jobs:
        job_res.append(list(res[at:at + len(job.out_shape)]))
        at += len(job.out_shape)
    return list(res[:n_out]), job_res


def _gather_job(stage, gathered, sends=(), forwards=()):
    shape = _sds((N_DEV,) + stage.shape, stage.dtype)
    inputs = ([stage] if sends else []) + ([gathered] if gathered is not None else [])
    aliases = {len(inputs) - 1: 0} if gathered is not None else {}

    def emit(ins, outs, place):
        x, y, c = place
        sibling, chips, mine, g = (x, y, 1 - c), _other_chips(x, y), 4 * x + 2 * y + c, outs[0]
        found = []
        for r0, nr in sends:
            src, dst = ins[0].at[pl.ds(r0, nr)], g.at[mine, pl.ds(r0, nr)]
            found += [(src, dst, None), (src, dst, sibling)] + [(src, dst, (px, py, c)) for px, py in chips]
        for r0, nr in forwards:
            for px, py in chips:
                block = 4 * px + 2 * py + c
                found.append((ins[-1].at[block, pl.ds(r0, nr)], g.at[block, pl.ds(r0, nr)], sibling))
        return found

    return _Job(inputs, [shape], aliases, emit, 4 * len(sends) + 3 * len(forwards), len(sends))


def _scatter_job(arrays, n_slots, route, prev=None, piece=None):
    shapes = [_sds((n_slots,) + v.shape[1:], v.dtype) for v in arrays]
    inputs = list(arrays) + (list(prev) if prev else [])
    aliases = {len(arrays) + i: i for i in range(len(arrays))} if prev else {}

    def emit(ins, outs, place):
        found = []
        for a in range(len(arrays)):
            for s in range(n_slots):
                block, to = route(s, *place)
                if piece is None:
                    found.append((ins[a].at[block], outs[a].at[s], to))
                else:
                    found.append((ins[a].at[block, pl.ds(*piece)], outs[a].at[s, pl.ds(*piece)], to))
        return found

    return _Job(inputs, shapes, aliases, emit, len(arrays) * n_slots)


def _to_sibling(s, x, y, c):
    return 2 * s + (1 - c), (x, y, 1 - c)


def _to_owner_chip(s, x, y, c):
    px, py = _other_chips(x, y)[s]
    return 2 * px + py, (px, py, c)


def _to_owner(s, x, y, c):
    if s == 0:
        px, py, pc = x, y, 1 - c
    else:
        px, py = _other_chips(x, y)[(s - 1) % 3]
        pc = c if s <= 3 else 1 - c
    return 4 * px + 2 * py + pc, (px, py, pc)


def _rstd(x):
    return lax.rsqrt(jnp.mean(x * x, axis=-1, keepdims=True) + EPS)


def _rms_bwd(xhat, r, g, dy):
    dxh = dy * g
    return r * (dxh - xhat * jnp.mean(dxh * xhat, axis=-1, keepdims=True))


def _first_half(rows):
    lane = lax.broadcasted_iota(jnp.int32, (rows, PAIR), 1)
    return (lane % HEAD_DIM) < (HEAD_DIM // 2)


def _rot_half(v, first):
    return jnp.where(first, pltpu.roll(v, PAIR - HEAD_DIM // 2, 1), pltpu.roll(v, HEAD_DIM // 2, 1))


def _rope(v, cos, sin, first):
    return v * cos + _rot_half(v, first) * sin


def _rope_t(dv, cos, sin, first):
    return dv * cos + _rot_half(dv * sin, first)


def _dot(a, b):
    return jnp.dot(a, b, preferred_element_type=F32)


def _dot_nt(a, b):
    return lax.dot_general(a, b, (((1,), (1,)), ((), ())), preferred_element_type=F32)


def _dot_tn(a, b):
    return lax.dot_general(a, b, (((0,), (0,)), ((), ())), preferred_element_type=F32)


def _sigmoid(v):
    return 1.0 / (1.0 + jnp.exp(-v))


_GELU_K = math.sqrt(2.0 / math.pi)
_GELU_C = 0.044715


def _gelu(v):
    return v * (0.5 * (1.0 + jnp.tanh(_GELU_K * (v + _GELU_C * (v * v * v)))))


def _gelu_grad(v):
    t = jnp.tanh(_GELU_K * (v + _GELU_C * (v * v * v)))
    return 0.5 * (1.0 + t) + 0.5 * v * (1.0 - t * t) * (_GELU_K * (1.0 + 3.0 * _GELU_C * (v * v)))


def _attn_qkv_fwd(x, g, w, b, cos, sin, jobs=()):
    t, d = x.shape
    n = w.shape[1]
    kd = (n - d) // 2
    tm = _row_tile(t)
    ch = _pick(d, (512,))

    def body(x_ref, g_ref, w_ref, b_ref, cos_ref, sin_ref, h_ref, q_ref, k_ref, v_ref):
        xv = x_ref[...]
        hb = (xv * _rstd(xv) * g_ref[...]).astype(ACT)
        h_ref[...] = hb
        cosv, sinv = cos_ref[...], sin_ref[...]
        first = _first_half(tm)

        def proj(lo, width):
            return _dot(hb, w_ref[:, lo:lo + width]) + b_ref[:, lo:lo + width]

        for c in range(0, d, ch):
            y = proj(c, ch)
            for s in range(0, ch, PAIR):
                q_ref[:, c + s:c + s + PAIR] = (_rope(y[:, s:s + PAIR], cosv, sinv, first) * SCALE).astype(ACT)
        y = proj(d, kd)
        for s in range(0, kd, PAIR):
            k_ref[:, s:s + PAIR] = _rope(y[:, s:s + PAIR], cosv, sinv, first).astype(ACT)
        v_ref[...] = proj(d + kd, kd).astype(ACT)

    return _call(
        body, "attn_qkv_fwd", (t // tm,),
        [_rows(tm, d), _full((1, d)), _full((d, n)), _full((1, n)), _rows(tm, PAIR), _rows(tm, PAIR)],
        [_rows(tm, d), _rows(tm, d), _rows(tm, kd), _rows(tm, kd)],
        [_sds((t, d), ACT), _sds((t, d), ACT), _sds((t, kd), ACT), _sds((t, kd), ACT)],
        (x, g, w, b, cos, sin), ("parallel",), jobs=jobs)


STACK = GQA_GROUP * WINDOW


def _band_mask(has_prev):
    row = lax.broadcasted_iota(jnp.int32, (STACK, 2 * WINDOW), 0) % WINDOW
    col = lax.broadcasted_iota(jnp.int32, (STACK, 2 * WINDOW), 1)
    return ((col < WINDOW) & (col > row) & has_prev) | ((col >= WINDOW) & (col - WINDOW <= row))


def _lane_halves():
    lane = lax.broadcasted_iota(jnp.int32, (1, PAIR), 1)
    return lane < HEAD_DIM, lane >= HEAD_DIM


def _stack_heads(ref, h, halves):
    parts = []
    for p in range(2):
        pair = ref[:, (2 * h + p) * PAIR:(2 * h + p + 1) * PAIR]
        parts += [jnp.where(half, pair, jnp.zeros_like(pair)) for half in halves]
    return jnp.concatenate(parts, axis=0)


def _sink_column(sink_ref, h):
    row = lax.broadcasted_iota(jnp.int32, (STACK, 1), 0)
    col = jnp.full((STACK, 1), sink_ref[0, GQA_GROUP * h + GQA_GROUP - 1], F32)
    for j in range(GQA_GROUP - 2, -1, -1):
        col = jnp.where(row < (j + 1) * WINDOW, sink_ref[0, GQA_GROUP * h + j], col)
    return col


def _probs(qs, k2, valid, sink):
    s = jnp.where(valid, _dot_nt(qs, k2), MASKED)
    m = jnp.maximum(jnp.max(s, axis=-1, keepdims=True), sink)
    p = jnp.exp(s - m)
    psink = jnp.exp(sink - m)
    inv = 1.0 / (jnp.sum(p, axis=-1, keepdims=True) + psink)
    return p * inv, psink * inv


def _attn_fwd(q, kd_, vd, sinks, jobs=()):
    t, d = q.shape
    kd = kd_.shape[1]
    cur = lambda n: (n, 0)
    prev = lambda n: (jnp.maximum(n - 1, 0), 0)

    def body(sink_ref, q_ref, kc_ref, kp_ref, vc_ref, vp_ref, o_ref):
        valid = _band_mask(pl.program_id(0) > 0)
        halves = _lane_halves()
        for h in range(kd // PAIR):
            hs = slice(h * PAIR, (h + 1) * PAIR)
            k2 = jnp.concatenate([kp_ref[:, hs], kc_ref[:, hs]], axis=0)
            v2 = jnp.concatenate([vp_ref[:, hs], vc_ref[:, hs]], axis=0)
            vs = jnp.concatenate([jnp.where(half, v2, jnp.zeros_like(v2)) for half in halves], axis=0)
            pr, _ = _probs(_stack_heads(q_ref, h, halves), k2, valid, _sink_column(sink_ref, h))
            pr = pr.astype(ACT)
            for p in range(2):
                both = jnp.concatenate([pr[2 * p * WINDOW:(2 * p + 1) * WINDOW], pr[(2 * p + 1) * WINDOW:(2 * p + 2) * WINDOW]], axis=1)
                o_ref[:, (2 * h + p) * PAIR:(2 * h + p + 1) * PAIR] = _dot(both, vs).astype(ACT)

    kv_c, kv_p = pl.BlockSpec((WINDOW, kd), cur), pl.BlockSpec((WINDOW, kd), prev)
    return _call(
        body, "attn_fwd", (t // WINDOW,),
        [pl.BlockSpec(memory_space=pltpu.SMEM), pl.BlockSpec((WINDOW, d), cur), kv_c, kv_p, kv_c, kv_p],
        [pl.BlockSpec((WINDOW, d), cur)], [_sds((t, d), ACT)],
        (sinks, q, kd_, kd_, vd, vd), ("parallel",), jobs=jobs)


def _attn_bwd(q, kd_, vd, do, sinks, cos, sin, jobs=()):
    t, d = q.shape
    kd = kd_.shape[1]
    nb = t // WINDOW
    n_out = d + 2 * kd
    cur = lambda n: (jnp.minimum(n, nb - 1), 0)
    prev = lambda n: (jnp.maximum(jnp.minimum(n, nb - 1) - 1, 0), 0)
    late = lambda n: (jnp.maximum(n - 1, 0), 0)

    def body(sink_ref, q_ref, kc_ref, kp_ref, vc_ref, vp_ref, do_ref, cosc_ref, sinc_ref, cosp_ref, sinp_ref,
             out_ref, db_ref, dsink_ref, dq_keep, dk_keep, dv_keep):
        n = pl.program_id(0)

        @pl.when(n == 0)
        def _():
            for ref in (db_ref, dsink_ref, dq_keep, dk_keep, dv_keep):
                ref[...] = jnp.zeros_like(ref)

        valid = _band_mask(n > 0) & (n < nb)
        halves = _lane_halves()
        first = _first_half(WINDOW)
        slot = lax.broadcasted_iota(jnp.int32, dsink_ref.shape, 1)
        top = lax.broadcasted_iota(jnp.int32, dsink_ref.shape, 0) == 0
        dsink = jnp.zeros(dsink_ref.shape, F32)

        def emit(cols, done):
            out_ref[:, cols] = done.astype(ACT)
            db_ref[:, cols] += jnp.sum(done.astype(F32), axis=0, keepdims=True)

        for h in range(kd // PAIR):
            hs = slice(h * PAIR, (h + 1) * PAIR)
            k2 = jnp.concatenate([kp_ref[:, hs], kc_ref[:, hs]], axis=0)
            v2 = jnp.concatenate([vp_ref[:, hs], vc_ref[:, hs]], axis=0)
            qs, dos = _stack_heads(q_ref, h, halves), _stack_heads(do_ref, h, halves)
            pr, psink = _probs(qs, k2, valid, _sink_column(sink_ref, h))
            dp = _dot_nt(dos, v2)
            delta = jnp.sum(pr * dp, axis=-1, keepdims=True)
            ds = (pr * (dp - delta)).astype(ACT)
            leak = psink * delta
            for j in range(GQA_GROUP):
                share = jnp.sum(leak[j * WINDOW:(j + 1) * WINDOW], axis=0, keepdims=True)
                dsink = dsink - jnp.where(top & (slot == GQA_GROUP * h + j), share, 0.0)
            dqs = _dot(ds, k2)
            for p in range(2):
                ps = slice((2 * h + p) * PAIR, (2 * h + p + 1) * PAIR)
                dqp = jnp.where(halves[0], dqs[2 * p * WINDOW:(2 * p + 1) * WINDOW], dqs[(2 * p + 1) * WINDOW:(2 * p + 2) * WINDOW])
                emit(ps, dq_keep[:, ps])
                dq_keep[:, ps] = (_rope_t(dqp, cosc_ref[...], sinc_ref[...], first) * SCALE).astype(ACT)
            dk2 = _dot_tn(ds, qs)
            dv2 = _dot_tn(pr.astype(ACT), dos)
            ks = slice(d + h * PAIR, d + (h + 1) * PAIR)
            vs = slice(d + kd + h * PAIR, d + kd + (h + 1) * PAIR)
            emit(ks, dk_keep[:, hs] + _rope_t(dk2[:WINDOW], cosp_ref[...], sinp_ref[...], first))
            dk_keep[:, hs] = _rope_t(dk2[WINDOW:], cosc_ref[...], sinc_ref[...], first)
            emit(vs, dv_keep[:, hs] + dv2[:WINDOW])
            dv_keep[:, hs] = dv2[WINDOW:]
        dsink_ref[...] += dsink

    kv_c, kv_p = pl.BlockSpec((WINDOW, kd), cur), pl.BlockSpec((WINDOW, kd), prev)
    tab_c, tab_p = pl.BlockSpec((WINDOW, PAIR), cur), pl.BlockSpec((WINDOW, PAIR), prev)
    return _call(
        body, "attn_bwd", (nb + 1,),
        [pl.BlockSpec(memory_space=pltpu.SMEM), pl.BlockSpec((WINDOW, d), cur), kv_c, kv_p, kv_c, kv_p,
         pl.BlockSpec((WINDOW, d), cur), tab_c, tab_c, tab_p, tab_p],
        [pl.BlockSpec((WINDOW, n_out), late), _full((1, n_out)), _full((8, LANES))],
        [_sds((t, n_out), ACT), _sds((1, n_out), F32), _sds((8, LANES), F32)],
        (sinks, q, kd_, kd_, vd, vd, do, cos, sin, cos, sin), ("arbitrary",),
        scratch=[pltpu.VMEM((WINDOW, d), ACT), pltpu.VMEM((WINDOW, kd), F32), pltpu.VMEM((WINDOW, kd), F32)], jobs=jobs)


def _proj_res(a, w, b, g, x, name, jobs=()):
    t, k = a.shape
    d = w.shape[1]
    tm = _row_tile(t)
    has_bias = b is not None

    def body(*refs):
        a_ref, w_ref = refs[:2]
        b_ref = refs[2] if has_bias else None
        g_ref, x_ref, m_ref, xo_ref = refs[2 + has_bias:]
        m = _dot(a_ref[...], w_ref[...])
        if has_bias:
            m = m + b_ref[...]
        m_ref[...] = m
        xo_ref[...] = x_ref[...] + (m * _rstd(m)) * g_ref[...]

    ins = [a, w] + ([b] if has_bias else []) + [g, x]
    specs = [_rows(tm, k), _full((k, d))] + ([_full((1, d))] if has_bias else []) + [_full((1, d)), _rows(tm, d)]
    return _call(body, name, (t // tm,), specs, [_rows(tm, d), _rows(tm, d)], [_sds((t, d), F32)] * 2, ins, ("parallel",), jobs=jobs)


def _post_bwd(dres, m, g, w, mode, name, gu=None, jobs=()):
    t, d = dres.shape
    k = w.shape[0]
    tm = _row_tile(t)
    kc = _pick(k, (1408, 1024, 512))

    def body(*refs):
        d_ref, m_ref, g_ref, w_ref = refs[:4]
        gu_ref = refs[4] if mode == "ffn" else None
        outs = refs[4 + (mode == "ffn"):]
        dm_ref, da_ref, dg_ref = outs[:3]
        i = pl.program_id(0)

        @pl.when(i == 0)
        def _():
            dg_ref[...] = jnp.zeros_like(dg_ref)
            if mode == "attn":
                outs[3][...] = jnp.zeros_like(outs[3])

        dv, mv = d_ref[...], m_ref[...]
        r = _rstd(mv)
        mh = mv * r
        dg_ref[...] += jnp.sum(dv * mh, axis=0, keepdims=True)
        dm = _rms_bwd(mh, r, g_ref[...], dv)
        dmb = dm.astype(ACT)
        dm_ref[...] = dmb
        if mode == "attn":
            outs[3][...] += jnp.sum(dm, axis=0, keepdims=True)
        for c in range(0, k, kc):
            da = _dot_nt(dmb, w_ref[c:c + kc, :])
            if mode == "attn":
                da_ref[:, c:c + kc] = da.astype(ACT)
            elif mode == "sgu":
                da_ref[:, c:c + kc] = da
            else:
                gate = gu_ref[:, c:c + kc].astype(F32)
                up = gu_ref[:, k + c:k + c + kc].astype(F32)
                sg = _sigmoid(gate)
                da_ref[:, c:c + kc] = (da * up * (sg * (1.0 + gate * (1.0 - sg)))).astype(ACT)
                da_ref[:, k + c:k + c + kc] = (da * (gate * sg)).astype(ACT)

    ins = [dres, m, g, w]
    specs = [_rows(tm, d), _rows(tm, d), _full((1, d)), _full((k, d))]
    if mode == "ffn":
        ins.append(gu)
        specs.append(_rows(tm, 2 * k))
    da_cols, da_dtype = (2 * k, ACT) if mode == "ffn" else (k, ACT if mode == "attn" else F32)
    out_specs = [_rows(tm, d), _rows(tm, da_cols), _full((1, d))]
    out_shape = [_sds((t, d), ACT), _sds((t, da_cols), da_dtype), _sds((1, d), F32)]
    if mode == "attn":
        out_specs.append(_full((1, d)))
        out_shape.append(_sds((1, d), F32))
    return _call(body, name, (t // tm,), specs, out_specs, out_shape, ins, ("arbitrary",), jobs=jobs)


def _pre_bwd(dy, w, x, g, dres, name, jobs=()):
    t, n = dy.shape
    d = w.shape[0]
    tm = _row_tile(t)
    nc = _pick(n, (1408, 1024, 512))

    def body(dy_ref, w_ref, x_ref, g_ref, dres_ref, dx_ref, dg_ref):
        @pl.when(pl.program_id(0) == 0)
        def _():
            dg_ref[...] = jnp.zeros_like(dg_ref)

        dh = jnp.zeros((tm, d), F32)
        for c in range(0, n, nc):
            dh = dh + _dot_nt(dy_ref[:, c:c + nc], w_ref[:, c:c + nc])
        xv = x_ref[...]
        r = _rstd(xv)
        xh = xv * r
        dg_ref[...] += jnp.sum(dh * xh, axis=0, keepdims=True)
        dx_ref[...] = dres_ref[...] + _rms_bwd(xh, r, g_ref[...], dh)

    return _call(
        body, name, (t // tm,),
        [_rows(tm, n), _full((d, n)), _rows(tm, d), _full((1, d)), _rows(tm, d)],
        [_rows(tm, d), _full((1, d))], [_sds((t, d), F32), _sds((1, d), F32)],
        (dy, w, x, g, dres), ("arbitrary",), jobs=jobs)


def _wgrad(a, b, name, out_dtype=F32, jobs=()):
    t, k = a.shape
    n = b.shape[1]
    tk = _pick(k, (1024, 1408))
    tn = _pick(n, (1024, 1408))
    tt = _pick(t, (1024,))
    steps = t // tt

    def body(a_ref, b_ref, o_ref, acc_ref):
        s = pl.program_id(2)

        @pl.when(s == 0)
        def _():
            acc_ref[...] = jnp.zeros_like(acc_ref)

        acc_ref[...] += _dot_tn(a_ref[...], b_ref[...])

        @pl.when(s == steps - 1)
        def _():
            o_ref[...] = acc_ref[...].astype(out_dtype)

    res, job_res = _call(
        body, name, (k // tk, n // tn, steps),
        [pl.BlockSpec((tt, tk), lambda i, j, s: (s, i)), pl.BlockSpec((tt, tn), lambda i, j, s: (s, j))],
        [pl.BlockSpec((tk, tn), lambda i, j, s: (i, j))], [_sds((k, n), out_dtype)],
        (a, b), ("parallel", "parallel", "arbitrary"), scratch=[pltpu.VMEM((tk, tn), F32)], jobs=jobs)
    return res[0], job_res


def _ffn_up_fwd(x, g, w, jobs=()):
    t, d = x.shape
    f = w.shape[1] // 2
    tm = _row_tile(t)
    fc = _pick(f, (1408, 1024, 512))

    def body(x_ref, g_ref, w_ref, h_ref, gu_ref, a_ref):
        xv = x_ref[...]
        hb = (xv * _rstd(xv) * g_ref[...]).astype(ACT)
        h_ref[...] = hb
        for c in range(0, f, fc):
            gate = _dot(hb, w_ref[:, c:c + fc])
            up = _dot(hb, w_ref[:, f + c:f + c + fc])
            gu_ref[:, c:c + fc] = gate.astype(ACT)
            gu_ref[:, f + c:f + c + fc] = up.astype(ACT)
            a_ref[:, c:c + fc] = (gate * _sigmoid(gate) * up).astype(ACT)

    return _call(
        body, "ffn_up_fwd", (t // tm,),
        [_rows(tm, d), _full((1, d)), _full((d, 2 * f))],
        [_rows(tm, d), _rows(tm, 2 * f), _rows(tm, f)],
        [_sds((t, d), ACT), _sds((t, 2 * f), ACT), _sds((t, f), ACT)],
        (x, g, w), ("parallel",), jobs=jobs)


def _sgu_in_fwd(x, g, w, jobs=()):
    t, d = x.shape
    n = w.shape[1]
    tm = _row_tile(t)
    nc = _pick(n, (1024, 512))

    def body(x_ref, g_ref, w_ref, h_ref, z_ref):
        xv = x_ref[...]
        hb = (xv * _rstd(xv) * g_ref[...]).astype(ACT)
        h_ref[...] = hb
        for c in range(0, n, nc):
            z_ref[:, c:c + nc] = _dot(hb, w_ref[:, c:c + nc])

    return _call(
        body, "sgu_in_fwd", (t // tm,), [_rows(tm, d), _full((1, d)), _full((d, n))],
        [_rows(tm, d), _rows(tm, n)], [_sds((t, d), ACT), _sds((t, n), F32)], (x, g, w), ("parallel",), jobs=jobs)


def _causal(ws):
    row = lax.broadcasted_iota(jnp.int32, ws.shape, 0)
    col = lax.broadcasted_iota(jnp.int32, ws.shape, 1)
    return jnp.where(col <= row, ws, 0.0)


def _sgu_ln(v, lg, lb):
    mu = jnp.mean(v, axis=-1, keepdims=True)
    vc = v - mu
    rs = lax.rsqrt(jnp.mean(vc * vc, axis=-1, keepdims=True) + EPS)
    vh = vc * rs
    return vh, rs, vh * lg + lb


def _sgu_mix_fwd(z, lg, lb, ws, bs_t, jobs=()):
    t, n = z.shape
    d = n // 2
    groups = d // WINDOW
    tm = _row_tile(t)

    def body(z_ref, lg_ref, lb_ref, ws_ref, bs_ref, y_ref):
        for c in range(0, tm, WINDOW):
            u = _gelu(z_ref[c:c + WINDOW, :d])
            _, _, vn = _sgu_ln(_gelu(z_ref[c:c + WINDOW, d:]), lg_ref[...], lb_ref[...])
            for gi in range(groups):
                gs = slice(gi * WINDOW, (gi + 1) * WINDOW)
                mixed = _dot(_causal(ws_ref[gi]).astype(ACT), vn[:, gs].astype(ACT)) + bs_ref[:, gi:gi + 1]
                y_ref[c:c + WINDOW, gs] = (u[:, gs] * mixed).astype(ACT)

    return _call(
        body, "sgu_mix_fwd", (t // tm,),
        [_rows(tm, n), _full((1, d)), _full((1, d)), _full((groups, WINDOW, WINDOW)), _full((WINDOW, groups))],
        [_rows(tm, d)], [_sds((t, d), ACT)], (z, lg, lb, ws, bs_t), ("parallel",), jobs=jobs)


def _sgu_mix_bwd(z, dy, lg, lb, ws, bs_t, jobs=()):
    t, n = z.shape
    d = n // 2
    groups = d // WINDOW
    tm = _row_tile(t)

    def body(z_ref, dy_ref, lg_ref, lb_ref, ws_ref, bs_ref, dz_ref, dlg_ref, dlb_ref, dws_ref, dbs_ref):
        @pl.when(pl.program_id(0) == 0)
        def _():
            for ref in (dlg_ref, dlb_ref, dws_ref, dbs_ref):
                ref[...] = jnp.zeros_like(ref)

        lane = lax.broadcasted_iota(jnp.int32, (WINDOW, groups), 1)
        for c in range(0, tm, WINDOW):
            rows = slice(c, c + WINDOW)
            zu, zv = z_ref[rows, :d], z_ref[rows, d:]
            u = _gelu(zu)
            vh, rs, vn = _sgu_ln(_gelu(zv), lg_ref[...], lb_ref[...])
            dyv = dy_ref[rows, :]
            dvn_parts = []
            for gi in range(groups):
                gs = slice(gi * WINDOW, (gi + 1) * WINDOW)
                wsg = _causal(ws_ref[gi]).astype(ACT)
                vng = vn[:, gs].astype(ACT)
                mixed = _dot(wsg, vng) + bs_ref[:, gi:gi + 1]
                dz_ref[rows, gs] = (dyv[:, gs] * mixed * _gelu_grad(zu[:, gs])).astype(ACT)
                dmix = dyv[:, gs] * u[:, gs]
                dmb = dmix.astype(ACT)
                dvn_parts.append(_dot_tn(wsg, dmb))
                dws_ref[gi] += _dot_nt(dmb, vng)
                dbs_ref[...] += jnp.where(lane == gi, jnp.sum(dmix, axis=-1, keepdims=True), 0.0)
            dvn = jnp.concatenate(dvn_parts, axis=-1)
            dlg_ref[...] += jnp.sum(dvn * vh, axis=0, keepdims=True)
            dlb_ref[...] += jnp.sum(dvn, axis=0, keepdims=True)
            dvh = dvn * lg_ref[...]
            dv = rs * (dvh - jnp.mean(dvh, axis=-1, keepdims=True) - vh * jnp.mean(dvh * vh, axis=-1, keepdims=True))
            dz_ref[rows, d:] = (dv * _gelu_grad(zv)).astype(ACT)

    vec = _full((1, d))
    return _call(
        body, "sgu_mix_bwd", (t // tm,),
        [_rows(tm, n), _rows(tm, d), vec, vec, _full((groups, WINDOW, WINDOW)), _full((WINDOW, groups))],
        [_rows(tm, n), vec, vec, _full((groups, WINDOW, WINDOW)), _full((WINDOW, groups))],
        [_sds((t, n), ACT), _sds((1, d), F32), _sds((1, d), F32), _sds((groups, WINDOW, WINDOW), F32), _sds((WINDOW, groups), F32)],
        (z, dy, lg, lb, ws, bs_t), ("arbitrary",), jobs=jobs)


def _loss_head(y, target):
    t, d = y.shape
    tm = _row_tile(t)

    def body(y_ref, t_ref, dy_ref, loss_ref):
        @pl.when(pl.program_id(0) == 0)
        def _():
            loss_ref[...] = jnp.zeros_like(loss_ref)

        err = y_ref[...] - t_ref[...]
        dy_ref[...] = err * (1.0 / d)
        per_token = jnp.mean(err * err, axis=-1, keepdims=True)
        loss_ref[...] += 0.5 * jnp.sum(per_token, axis=0, keepdims=True)

    res, _ = _call(body, "loss_head", (t // tm,), [_rows(tm, d), _rows(tm, d)], [_rows(tm, d), _full((8, LANES))],
                   [_sds((t, d), F32), _sds((8, LANES), F32)], (y, target), ("arbitrary",))
    return res


AG_COPIES = 7


def _prepare(blocks, dtypes, n_gather, name):
    n = len(blocks)

    def body(*refs):
        ins, stage, outs = refs[:n], refs[n:2 * n], refs[2 * n:2 * n + n_gather]
        send, recv, local_sem = refs[2 * n + n_gather:]
        x, y, c = _place()
        sibling = (x, y, 1 - c)
        chips = _other_chips(x, y)
        mine = 4 * x + 2 * y + c

        def copy(a, k, block, to, src=None):
            dst = outs[a].at[block]
            return pltpu.make_async_remote_copy(
                src_ref=dst if src is None else src, dst_ref=dst, send_sem=send.at[a * AG_COPIES + k],
                recv_sem=recv.at[a * AG_COPIES + k], device_id=to, device_id_type=MESH)

        for a in range(n):
            stage[a][...] = ins[a][...].astype(dtypes[a])
        started = []
        for a in range(n_gather):
            own = pltpu.make_async_copy(stage[a], outs[a].at[mine], local_sem.at[a])
            own.start()
            started.append(own)
        sends = []
        for a in range(n_gather):
            sends.append(copy(a, 0, mine, sibling, src=stage[a]))
            sends += [copy(a, 1 + j, mine, (*chip, c), src=stage[a]) for j, chip in enumerate(chips)]
        for cp in sends:
            cp.start()
        for j, (px, py) in enumerate(chips):
            block = 4 * px + 2 * py + c
            for a in range(n_gather):
                copy(a, 1 + j, block, (x, y, c)).wait_recv()
                forward = copy(a, 4 + j, block, sibling)
                forward.start()
                sends.append(forward)
        for a in range(n_gather):
            copy(a, 0, 4 * x + 2 * y + (1 - c), (x, y, c)).wait_recv()
            for j, (px, py) in enumerate(chips):
                copy(a, 4 + j, 4 * px + 2 * py + (1 - c), (x, y, c)).wait_recv()
        for cp in sends:
            cp.wait_send()
        for own in started:
            own.wait()

    res = pl.pallas_call(
        body, name=name,
        in_specs=[IN_VMEM] * n, out_specs=[IN_VMEM] * n + [ANY] * n_gather,
        out_shape=[_sds(b.shape, dt) for b, dt in zip(blocks, dtypes)]
        + [_sds((N_DEV,) + b.shape, dt) for b, dt in zip(blocks[:n_gather], dtypes)],
        scratch_shapes=[pltpu.SemaphoreType.DMA((n_gather * AG_COPIES,)), pltpu.SemaphoreType.DMA((n_gather * AG_COPIES,)),
                        pltpu.SemaphoreType.DMA((n_gather,))],
        compiler_params=pltpu.CompilerParams(vmem_limit_bytes=VMEM_LIMIT_BYTES),
    )(*blocks)
    return list(res[:n]), list(res[n:])


def _pair_sum(g, r, core, name):
    _, _, rows, cols = g.shape
    tr = _pick(rows, (512, 256, 128))

    def body(core_ref, g_ref, r_ref, p_ref):
        del core_ref
        p_ref[...] = (g_ref[...].astype(F32) + r_ref[...].astype(F32)).astype(p_ref.dtype)

    return pl.pallas_call(
        body, name=name,
        grid_spec=pltpu.PrefetchScalarGridSpec(
            num_scalar_prefetch=1, grid=(4, rows // tr),
            in_specs=[pl.BlockSpec((None, None, tr, cols), lambda q, i, core_ref: (q, core_ref[0], i, 0)),
                      pl.BlockSpec((None, tr, cols), lambda q, i, core_ref: (q, i, 0))],
            out_specs=pl.BlockSpec((None, tr, cols), lambda q, i, core_ref: (q, i, 0))),
        out_shape=_sds((4, rows, cols), g.dtype),
        compiler_params=_params("parallel", "parallel"),
    )(core, g, r)


def _adam(w, g, m, v):
    m2 = ADAM_B1 * m + (1.0 - ADAM_B1) * g
    v2 = ADAM_B2 * v + (1.0 - ADAM_B2) * (g * g)
    m_hat = m2 / (1.0 - ADAM_B1 ** ADAM_STEP)
    v_hat = v2 / (1.0 - ADAM_B2 ** ADAM_STEP)
    return -ADAM_LR * (m_hat / (jnp.sqrt(v_hat) + ADAM_EPS) + ADAM_WD * w), m2, v2


def _shard_update(own, index, received, w, m, v, name):
    rows, cols = w.shape
    n_recv = received.shape[0]
    tr = _pick(rows, (512, 256, 128))

    def body(index_ref, p_ref, q_ref, w_ref, m_ref, v_ref, g_out, d_out, m_out, v_out):
        del index_ref
        g = p_ref[...].astype(F32)
        for s in range(n_recv):
            g = g + q_ref[s].astype(F32)
        g_out[...] = g
        d_out[...], m_out[...], v_out[...] = _adam(w_ref[...], g, m_ref[...], v_ref[...])

    tile = pl.BlockSpec((tr, cols), lambda i, index_ref: (i, 0))
    return pl.pallas_call(
        body, name=name,
        grid_spec=pltpu.PrefetchScalarGridSpec(
            num_scalar_prefetch=1, grid=(rows // tr,),
            in_specs=[pl.BlockSpec((None, tr, cols), lambda i, index_ref: (index_ref[0], i, 0)),
                      pl.BlockSpec((n_recv, tr, cols), lambda i, index_ref: (0, i, 0)), tile, tile, tile],
            out_specs=[tile] * 4),
        out_shape=[_sds((rows, cols), F32)] * 4,
        compiler_params=_params("parallel"),
    )(index, own, received, w, m, v)


def _replicated_update(shares, w, m, v):
    rows, cols = w.shape

    def body(s_ref, w_ref, m_ref, v_ref, g_out, d_out, m_out, v_out):
        g = s_ref[0]
        for dev in range(1, N_DEV):
            g = g + s_ref[dev]
        g_out[...] = g
        d_out[...], m_out[...], v_out[...] = _adam(w_ref[...], g, m_ref[...], v_ref[...])

    return pl.pallas_call(
        body, name="replicated_update_%d" % rows, out_shape=[_sds((rows, cols), F32)] * 4,
        compiler_params=pltpu.CompilerParams(vmem_limit_bytes=VMEM_LIMIT_BYTES),
    )(shares, w, m, v)


def _rope_tables(t):
    half = HEAD_DIM // 2
    inv_freq = ROPE_THETA ** (-(jnp.arange(half, dtype=F32) * 2.0) / HEAD_DIM)
    ang = jnp.arange(t, dtype=jnp.int32).astype(F32)[:, None] * inv_freq[None, :]
    cos, sin = jnp.cos(ang), jnp.sin(ang)
    return jnp.tile(jnp.concatenate([cos, cos], axis=1), (1, 2)), jnp.tile(jnp.concatenate([-sin, sin], axis=1), (1, 2))


def _dup_heads(w, d):
    kv = (w.shape[-1] - d) // 2
    lead = w.shape[:-1]

    def dup(part):
        heads = part.reshape(lead + (kv // HEAD_DIM, 1, HEAD_DIM))
        return jnp.broadcast_to(heads, lead + (kv // HEAD_DIM, 2, HEAD_DIM)).reshape(lead + (2 * kv,))

    return jnp.concatenate([w[..., :d], dup(w[..., d:d + kv]), dup(w[..., d + kv:])], axis=-1)


def _fold_heads(dw, d):
    kv = (dw.shape[-1] - d) // 4
    lead = dw.shape[:-1]

    def fold(part):
        return part.reshape(lead + (kv // HEAD_DIM, 2, HEAD_DIM)).sum(axis=-2).reshape(lead + (kv,))

    return jnp.concatenate([dw[..., :d], fold(dw[..., d:d + 2 * kv]), fold(dw[..., d + 2 * kv:])], axis=-1)


def _columns_full(gathered):
    _, rows, c = gathered.shape
    return jnp.transpose(gathered, (1, 0, 2)).reshape(rows, N_DEV * c)


def _rows_full(gathered):
    return gathered.reshape(-1, gathered.shape[-1])


def _columns_blocked(full):
    rows, cols = full.shape
    return jnp.transpose(full.reshape(rows, N_DEV, cols // N_DEV), (1, 0, 2)).astype(ACT)


def _rows_blocked(full):
    return full.reshape(N_DEV, full.shape[0] // N_DEV, full.shape[1]).astype(ACT)


def _packed_rows(size, width):
    return -(-size // (8 * width)) * 8


def _pack_rows(parts, width):
    rows = []
    for v in parts:
        flat = v.reshape(-1).astype(F32)
        n_rows = _packed_rows(flat.shape[0], width)
        rows.append(jnp.pad(flat, (0, n_rows * width - flat.shape[0])).reshape(n_rows, width))
    return jnp.concatenate(rows, axis=0)


def _unpack_rows(packed, like, width):
    out, at = [], 0
    for v in like:
        size = math.prod(v.shape)
        out.append(packed[at:at + -(-size // width)].reshape(-1)[:size].reshape(v.shape))
        at += _packed_rows(size, width)
    return out


def _halves(block):
    half = block.shape[0] // 2
    return (0, half), (half, half)


def _whole(block):
    return (0, block.shape[0])


EARLY = ("attn_w_qkv", "sgu_ln", "attn_w_o")
LATE = ("sgu_w_in", "sgu_w_out", "gu0", "gu1", "dn0", "dn1")
COLUMN_SHARDED = ("attn_w_qkv", "sgu_w_in", "gu0", "gu1")
LAST = ("norm_mix_pre",)
REPLICATED = ("norm_mix_post", "norm_ffn_pre", "norm_ffn_post", "attn_b_qkv", "attn_sinks", "attn_b_o", "sgu_w_spatial", "sgu_b_spatial")
WEIGHTS = ("norm_mix_pre", "norm_mix_post", "norm_ffn_pre", "norm_ffn_post", "attn_w_qkv", "attn_b_qkv", "attn_sinks", "attn_w_o",
           "attn_b_o", "sgu_w_in", "sgu_ln_g", "sgu_ln_b", "sgu_w_spatial", "sgu_b_spatial", "sgu_w_out", "ffn_w_gate_up", "ffn_w_down")


def _blocks_of(tree):
    return {
        "attn_w_qkv": tree["attn_w_qkv"][0], "attn_w_o": tree["attn_w_o"][0],
        "sgu_ln": jnp.concatenate([tree["sgu_ln_g"], tree["sgu_ln_b"]], axis=0),
        "sgu_w_in": tree["sgu_w_in"][0], "sgu_w_out": tree["sgu_w_out"][0],
        "gu0": tree["ffn_w_gate_up"][0], "gu1": tree["ffn_w_gate_up"][1],
        "dn0": tree["ffn_w_down"][0], "dn1": tree["ffn_w_down"][1],
    }


def kernel(x, norm_mix_pre, norm_mix_post, norm_ffn_pre, norm_ffn_post, attn_w_qkv, attn_b_qkv, attn_sinks, attn_w_o, attn_b_o, sgu_w_in, sgu_ln_g, sgu_ln_b, sgu_w_spatial, sgu_b_spatial, sgu_w_out, ffn_w_gate_up, ffn_w_down, loss_target, m_norm_mix_pre, m_norm_mix_post, m_norm_ffn_pre, m_norm_ffn_post, m_attn_w_qkv, m_attn_b_qkv, m_attn_sinks, m_attn_w_o, m_attn_b_o, m_sgu_w_in, m_sgu_ln_g, m_sgu_ln_b, m_sgu_w_spatial, m_sgu_b_spatial, m_sgu_w_out, m_ffn_w_gate_up, m_ffn_w_down, v_norm_mix_pre, v_norm_mix_post, v_norm_ffn_pre, v_norm_ffn_post, v_attn_w_qkv, v_attn_b_qkv, v_attn_sinks, v_attn_w_o, v_attn_b_o, v_sgu_w_in, v_sgu_ln_g, v_sgu_ln_b, v_sgu_w_spatial, v_sgu_b_spatial, v_sgu_w_out, v_ffn_w_gate_up, v_ffn_w_down):
    w = dict(norm_mix_pre=norm_mix_pre, norm_mix_post=norm_mix_post, norm_ffn_pre=norm_ffn_pre, norm_ffn_post=norm_ffn_post,
             attn_w_qkv=attn_w_qkv, attn_b_qkv=attn_b_qkv, attn_sinks=attn_sinks, attn_w_o=attn_w_o, attn_b_o=attn_b_o,
             sgu_w_in=sgu_w_in, sgu_ln_g=sgu_ln_g, sgu_ln_b=sgu_ln_b, sgu_w_spatial=sgu_w_spatial, sgu_b_spatial=sgu_b_spatial,
             sgu_w_out=sgu_w_out, ffn_w_gate_up=ffn_w_gate_up, ffn_w_down=ffn_w_down)
    mom = dict(norm_mix_pre=m_norm_mix_pre, norm_mix_post=m_norm_mix_post, norm_ffn_pre=m_norm_ffn_pre, norm_ffn_post=m_norm_ffn_post,
               attn_w_qkv=m_attn_w_qkv, attn_b_qkv=m_attn_b_qkv, attn_sinks=m_attn_sinks, attn_w_o=m_attn_w_o, attn_b_o=m_attn_b_o,
               sgu_w_in=m_sgu_w_in, sgu_ln_g=m_sgu_ln_g, sgu_ln_b=m_sgu_ln_b, sgu_w_spatial=m_sgu_w_spatial,
               sgu_b_spatial=m_sgu_b_spatial, sgu_w_out=m_sgu_w_out, ffn_w_gate_up=m_ffn_w_gate_up, ffn_w_down=m_ffn_w_down)
    var = dict(norm_mix_pre=v_norm_mix_pre, norm_mix_post=v_norm_mix_post, norm_ffn_pre=v_norm_ffn_pre, norm_ffn_post=v_norm_ffn_post,
               attn_w_qkv=v_attn_w_qkv, attn_b_qkv=v_attn_b_qkv, attn_sinks=v_attn_sinks, attn_w_o=v_attn_w_o, attn_b_o=v_attn_b_o,
               sgu_w_in=v_sgu_w_in, sgu_ln_g=v_sgu_ln_g, sgu_ln_b=v_sgu_ln_b, sgu_w_spatial=v_sgu_w_spatial,
               sgu_b_spatial=v_sgu_b_spatial, sgu_w_out=v_sgu_w_out, ffn_w_gate_up=v_ffn_w_gate_up, ffn_w_down=v_ffn_w_down)
    xs, target = x[0], loss_target[0]
    t, d = xs.shape
    row = lambda v: v.reshape(1, -1).astype(F32)
    core = lax.axis_index("c").astype(jnp.int32).reshape(1)
    chip = (2 * lax.axis_index("x") + lax.axis_index("y")).astype(jnp.int32).reshape(1)
    me = (4 * lax.axis_index("x") + 2 * lax.axis_index("y") + lax.axis_index("c")).astype(jnp.int32).reshape(1)
    blocks = {tree_name: _blocks_of(tree) for tree_name, tree in (("w", w), ("m", mom), ("v", var))}

    names = EARLY + LATE
    staged, early = _prepare([blocks["w"][n] for n in names], [F32 if n == "sgu_ln" else ACT for n in names], len(EARLY),
                             "weights_prepare")
    stage = dict(zip(names, staged))
    w_qkv = _dup_heads(_columns_full(early[0]), d)
    lg, lb = row(early[1][:, 0, :]), row(early[1][:, 1, :])
    w_o = _rows_full(early[2])
    b_qkv = _dup_heads(row(attn_b_qkv), d)
    sinks = attn_sinks.reshape(1, -1).astype(F32)
    ws = sgu_w_spatial[0].astype(F32)
    bs_t = sgu_b_spatial[0].astype(F32).T
    cos, sin = _rope_tables(t)
    gather = lambda name, so_far, **pieces: _gather_job(stage[name], so_far, **pieces)
    a_half = lambda name: _halves(stage[name])[0]
    b_half = lambda name: _halves(stage[name])[1]
    whole = lambda name: _whole(stage[name])

    (h0, q, kdup, vdup), ((g_gu0,),) = _attn_qkv_fwd(
        xs, row(norm_mix_pre[0]), w_qkv, b_qkv, cos, sin, jobs=[gather("gu0", None, sends=[a_half("gu0")])])
    (o,), ((g_gu0,), (g_dn0,)) = _attn_fwd(q, kdup, vdup, sinks, jobs=[
        gather("gu0", g_gu0, sends=[b_half("gu0")], forwards=[a_half("gu0")]), gather("dn0", None, sends=[a_half("dn0")])])
    (m0, x1), ((g_gu0,), (g_dn0,)) = _proj_res(o, w_o, row(attn_b_o), row(norm_mix_post[0]), xs, "attn_out_fwd", jobs=[
        gather("gu0", g_gu0, forwards=[b_half("gu0")]),
        gather("dn0", g_dn0, sends=[b_half("dn0")], forwards=[a_half("dn0")])])
    w_gu0 = _columns_full(g_gu0)
    (h1, gu0, a0), ((g_dn0,), (g_in,), (g_out,), (g_gu1,)) = _ffn_up_fwd(x1, row(norm_ffn_pre[0]), w_gu0, jobs=[
        gather("dn0", g_dn0, forwards=[b_half("dn0")]), gather("sgu_w_in", None, sends=[whole("sgu_w_in")]),
        gather("sgu_w_out", None, sends=[whole("sgu_w_out")]), gather("gu1", None, sends=[a_half("gu1")])])
    w_dn0 = _rows_full(g_dn0)
    (f0, x2), ((g_in,), (g_out,), (g_gu1,)) = _proj_res(a0, w_dn0, None, row(norm_ffn_post[0]), x1, "ffn_down_fwd0", jobs=[
        gather("sgu_w_in", g_in, forwards=[whole("sgu_w_in")]), gather("sgu_w_out", g_out, forwards=[whole("sgu_w_out")]),
        gather("gu1", g_gu1, sends=[b_half("gu1")], forwards=[a_half("gu1")])])
    w_in = _columns_full(g_in)
    (h2, z), ((g_gu1,), (g_dn1,)) = _sgu_in_fwd(x2, row(norm_mix_pre[1]), w_in, jobs=[
        gather("gu1", g_gu1, forwards=[b_half("gu1")]), gather("dn1", None, sends=[a_half("dn1")])])
    (y,), ((g_dn1,),) = _sgu_mix_fwd(z, lg, lb, ws, bs_t, jobs=[
        gather("dn1", g_dn1, sends=[b_half("dn1")], forwards=[a_half("dn1")])])
    w_out = _rows_full(g_out)
    (m1, x3), ((g_dn1,),) = _proj_res(y, w_out, None, row(norm_mix_post[1]), x2, "sgu_out_fwd", jobs=[
        gather("dn1", g_dn1, forwards=[b_half("dn1")])])
    w_gu1 = _columns_full(g_gu1)
    (h3, gu1, a1), _ = _ffn_up_fwd(x3, row(norm_ffn_pre[1]), w_gu1)
    w_dn1 = _rows_full(g_dn1)
    (f1, x4), _ = _proj_res(a1, w_dn1, None, row(norm_ffn_post[1]), x3, "ffn_down_fwd1")
    dx4, loss_tile = _loss_head(x4, target)

    to_sibling = lambda g: _scatter_job([g], 4, _to_sibling)
    pair = lambda g, r, name: _pair_sum(g.reshape((4, 2) + g.shape[1:]), r, core, "pair_sum_" + name)
    to_chips = lambda p, prev=None, piece=None: _scatter_job([p], 3, _to_owner_chip, prev=prev, piece=piece)
    out_g, out_d, out_m, out_v = {}, {}, {}, {}

    def update(name, own, index, received):
        res = _shard_update(own, index, received, blocks["w"][name], blocks["m"][name], blocks["v"][name], "update_" + name)
        for out, val in zip((out_g, out_d, out_m, out_v), res):
            out[name] = val

    (df1, dgu1, dg_ffn_post1), _ = _post_bwd(dx4, f1, row(norm_ffn_post[1]), w_dn1, "ffn", "ffn_down_bwd1", gu1)
    dw_dn1, _ = _wgrad(a1, df1, "ffn_down_wgrad1", ACT)
    b_dn1 = _rows_blocked(dw_dn1)
    dw_gu1, ((r_dn1,),) = _wgrad(h3, dgu1, "ffn_up_wgrad1", ACT, jobs=[to_sibling(b_dn1)])
    b_gu1 = _columns_blocked(dw_gu1)
    p_dn1 = pair(b_dn1, r_dn1, "dn1")
    (dx3, dg_ffn_pre1), ((q_dn1,), (r_gu1,)) = _pre_bwd(dgu1, w_gu1, x3, row(norm_ffn_pre[1]), dx4, "ffn_up_bwd1",
                                                         jobs=[to_chips(p_dn1), to_sibling(b_gu1)])
    update("dn1", p_dn1, chip, q_dn1)
    p_gu1 = pair(b_gu1, r_gu1, "gu1")
    (dm1, dy, dg_mix_post1), ((q_gu1,),) = _post_bwd(dx3, m1, row(norm_mix_post[1]), w_out, "sgu", "sgu_out_bwd",
                                                      jobs=[to_chips(p_gu1, piece=_halves(p_gu1[0])[0])])
    dw_out, _ = _wgrad(y, dm1, "sgu_out_wgrad", ACT)
    b_out = _rows_blocked(dw_out)
    (dz, dlg, dlb, dws, dbs_t), ((q_gu1,), (r_out,)) = _sgu_mix_bwd(z, dy, lg, lb, ws, bs_t, jobs=[
        to_chips(p_gu1, prev=[q_gu1], piece=_halves(p_gu1[0])[1]), to_sibling(b_out)])
    update("gu1", p_gu1, chip, q_gu1)
    p_out = pair(b_out, r_out, "sgu_w_out")
    dw_in, ((q_out,),) = _wgrad(h2, dz, "sgu_in_wgrad", ACT, jobs=[to_chips(p_out)])
    update("sgu_w_out", p_out, chip, q_out)
    b_in = _columns_blocked(dw_in)
    b_ln = jnp.stack([dlg.reshape(N_DEV, -1), dlb.reshape(N_DEV, -1)], axis=1)
    (dx2, dg_mix_pre1), ((r_in,), (r_ln,)) = _pre_bwd(dz, w_in, x2, row(norm_mix_pre[1]), dx3, "sgu_in_bwd",
                                                      jobs=[to_sibling(b_in), to_sibling(b_ln)])
    p_in, p_ln = pair(b_in, r_in, "sgu_w_in"), pair(b_ln, r_ln, "sgu_ln")
    (df0, dgu0, dg_ffn_post0), ((q_in,), (q_ln,)) = _post_bwd(dx2, f0, row(norm_ffn_post[0]), w_dn0, "ffn", "ffn_down_bwd0", gu0,
                                                              jobs=[to_chips(p_in), to_chips(p_ln)])
    update("sgu_w_in", p_in, chip, q_in)
    update("sgu_ln", p_ln, chip, q_ln)
    dw_dn0, _ = _wgrad(a0, df0, "ffn_down_wgrad0", ACT)
    b_dn0 = _rows_blocked(dw_dn0)
    dw_gu0, ((r_dn0,),) = _wgrad(h1, dgu0, "ffn_up_wgrad0", ACT, jobs=[to_sibling(b_dn0)])
    b_gu0 = _columns_blocked(dw_gu0)
    p_dn0 = pair(b_dn0, r_dn0, "dn0")
    (dx1, dg_ffn_pre0), ((q_dn0,), (r_gu0,)) = _pre_bwd(dgu0, w_gu0, x1, row(norm_ffn_pre[0]), dx2, "ffn_up_bwd0",
                                                         jobs=[to_chips(p_dn0), to_sibling(b_gu0)])
    update("dn0", p_dn0, chip, q_dn0)
    p_gu0 = pair(b_gu0, r_gu0, "gu0")
    (dm0, do, dg_mix_post0, db_o), ((q_gu0,),) = _post_bwd(dx1, m0, row(norm_mix_post[0]), w_o, "attn", "attn_out_bwd",
                                                           jobs=[to_chips(p_gu0, piece=_halves(p_gu0[0])[0])])
    dw_o, _ = _wgrad(o, dm0, "attn_out_wgrad", ACT)
    b_o = _rows_blocked(dw_o)
    (dqkv, db_qkv, dsink), ((q_gu0,), (r_o,)) = _attn_bwd(q, kdup, vdup, do, sinks, cos, sin, jobs=[
        to_chips(p_gu0, prev=[q_gu0], piece=_halves(p_gu0[0])[1]), to_sibling(b_o)])
    update("gu0", p_gu0, chip, q_gu0)
    p_o = pair(b_o, r_o, "attn_w_o")

    grads = {
        "norm_mix_post": jnp.concatenate([dg_mix_post0, dg_mix_post1], axis=0),
        "norm_ffn_pre": jnp.concatenate([dg_ffn_pre0, dg_ffn_pre1], axis=0),
        "norm_ffn_post": jnp.concatenate([dg_ffn_post0, dg_ffn_post1], axis=0),
        "attn_b_qkv": _fold_heads(db_qkv, d),
        "attn_sinks": dsink[:1, :d // HEAD_DIM],
        "attn_b_o": db_o,
        "sgu_w_spatial": dws * jnp.tril(jnp.ones((WINDOW, WINDOW), F32))[None],
        "sgu_b_spatial": dbs_t.T,
    }
    like = [w[name] for name in REPLICATED] + [loss_tile[:1, :1]]
    shares = _pack_rows([grads[name] for name in REPLICATED] + [loss_tile[:1, :1]], d)
    dw_qkv, ((q_o,), (g_shares,)) = _wgrad(h0, dqkv, "attn_qkv_wgrad", jobs=[
        to_chips(p_o), _gather_job(shares, None, sends=[_whole(shares)])])
    update("attn_w_o", p_o, chip, q_o)
    b_qkv_grad = _columns_blocked(_fold_heads(dw_qkv, d))
    (dx0, dg_mix_pre0), ((q_qkv,), (g_shares,)) = _pre_bwd(dqkv, w_qkv, xs, row(norm_mix_pre[0]), dx1, "attn_qkv_bwd", jobs=[
        _scatter_job([b_qkv_grad], N_DEV - 1, _to_owner), _gather_job(shares, g_shares, forwards=[_whole(shares)])])
    update("attn_w_qkv", b_qkv_grad, me, q_qkv)

    packed = [_pack_rows([tree[name] for name in REPLICATED] + [jnp.zeros((1, 1), F32)], d) for tree in (w, mom, var)]
    res = _replicated_update(g_shares, *packed)
    loss = _unpack_rows(res[0], like, d)[-1][0, 0]
    for out, val in zip((out_g, out_d, out_m, out_v), res):
        for name, leaf in zip(REPLICATED, _unpack_rows(val, like, d)):
            out[name] = leaf

    last_grads = {"norm_mix_pre": jnp.concatenate([dg_mix_pre0, dg_mix_pre1], axis=0)}
    last_like = [w[name] for name in LAST]
    _, (last_shares,) = _prepare([_pack_rows([last_grads[name] for name in LAST], d)], [F32], 1, "last_grads_all_gather")
    res = _replicated_update(last_shares, *[_pack_rows([tree[name] for name in LAST], d) for tree in (w, mom, var)])
    for out, val in zip((out_g, out_d, out_m, out_v), res):
        for name, leaf in zip(LAST, _unpack_rows(val, last_like, d)):
            out[name] = leaf

    for out in (out_g, out_d, out_m, out_v):
        out["sgu_ln_g"], out["sgu_ln_b"] = out["sgu_ln"][0:1], out["sgu_ln"][1:2]
        out["ffn_w_gate_up"] = jnp.stack([out["gu0"], out["gu1"]])
        out["ffn_w_down"] = jnp.stack([out["dn0"], out["dn1"]])
        for name in ("attn_w_qkv", "attn_w_o", "sgu_w_in", "sgu_w_out"):
            out[name] = out[name][None]

    return (loss, dx0[None], *[out_g[n] for n in WEIGHTS], *[out_d[n] for n in WEIGHTS],
            *[out_m[n] for n in WEIGHTS], *[out_v[n] for n in WEIGHTS])
```

```python
import functools
import math
import operator

import jax
import jax.numpy as jnp
from jax import lax
from jax.experimental import pallas as pl
from jax.experimental.pallas import tpu as pltpu

F32 = jnp.float32
ACT = jnp.bfloat16

EPS = 1e-6
HEAD_DIM = 64
PAIR = 2 * HEAD_DIM
GQA_GROUP = 4
WINDOW = 128
ROPE_THETA = 10000.0
SCALE = HEAD_DIM ** -0.5
MASKED = -1e30
LANES = 128

ADAM_LR, ADAM_B1, ADAM_B2, ADAM_EPS, ADAM_WD, ADAM_STEP = 0.001, 0.9, 0.999, 1e-08, 0.01, 10

N_DEV = 8
VMEM_LIMIT_BYTES = 56 * 1024 * 1024
MESH = pl.DeviceIdType.MESH
ANY = pl.BlockSpec(memory_space=pl.ANY)
IN_VMEM = pl.BlockSpec(memory_space=pltpu.VMEM)


def _pick(n, candidates):
    for c in candidates:
        if n % c == 0:
            return c
    return n


def _row_tile(t):
    return _pick(t, (512,))


def _params(*sem):
    return pltpu.CompilerParams(dimension_semantics=sem, vmem_limit_bytes=VMEM_LIMIT_BYTES)


def _full(shape):
    return pl.BlockSpec(shape, lambda *_: (0,) * len(shape))


def _rows(tm, n):
    return pl.BlockSpec((tm, n), lambda i: (i, 0))


def _sds(shape, dtype):
    return jax.ShapeDtypeStruct(shape, dtype)


def _place():
    return lax.axis_index("x"), lax.axis_index("y"), lax.axis_index("c")


def _other_chips(x, y):
    return [(1 - x, y), (x, 1 - y), (1 - x, 1 - y)]


class _Job:
    def __init__(self, inputs, out_shape, aliases, emit, n_remote, n_local=0):
        self.inputs, self.out_shape, self.aliases, self.emit = list(inputs), list(out_shape), dict(aliases), emit
        self.n_remote, self.n_local = n_remote, n_local


def _call(body, name, grid, in_specs, out_specs, out_shape, args, sem, scratch=(), jobs=()):
    n_in, n_out, n_scr = len(args), len(out_shape), len(scratch)
    j_in = [a for job in jobs for a in job.inputs]
    j_out = [s for job in jobs for s in job.out_shape]
    aliases, at_in, at_out = {}, n_in, n_out
    for job in jobs:
        aliases.update({at_in + i: at_out + o for i, o in job.aliases.items()})
        at_in, at_out = at_in + len(job.inputs), at_out + len(job.out_shape)
    n_remote = sum(job.n_remote for job in jobs)
    n_local = sum(job.n_local for job in jobs)

    def wrapped(*refs):
        ins, jin = refs[:n_in], refs[n_in:n_in + len(j_in)]
        rest = refs[n_in + len(j_in):]
        outs, jout = rest[:n_out], rest[n_out:n_out + len(j_out)]
        scr = rest[n_out + len(j_out):n_out + len(j_out) + n_scr]
        if not jobs:
            body(*ins, *outs, *scr)
            return
        send, recv, local = rest[n_out + len(j_out) + n_scr:]
        ids = [pl.program_id(a) for a in range(len(grid))]
        first = functools.reduce(operator.and_, [i == 0 for i in ids])
        last = functools.reduce(operator.and_, [i == g - 1 for i, g in zip(ids, grid)])

        def descriptors():
            place, found, i, o, r, l = _place(), [], 0, 0, 0, 0
            for job in jobs:
                for src, dst, to in job.emit(jin[i:i + len(job.inputs)], jout[o:o + len(job.out_shape)], place):
                    if to is None:
                        found.append(pltpu.make_async_copy(src, dst, local.at[l]))
                        l += 1
                    else:
                        found.append(pltpu.make_async_remote_copy(src_ref=src, dst_ref=dst, send_sem=send.at[r], recv_sem=recv.at[r],
                                                                  device_id=to, device_id_type=MESH))
                        r += 1
                i, o = i + len(job.inputs), o + len(job.out_shape)
            return found

        @pl.when(first)
        def _():
            for cp in descriptors():
                cp.start()

        body(*ins, *outs, *scr)

        @pl.when(last)
        def _():
            for cp in descriptors():
                cp.wait()

    sems = [pltpu.SemaphoreType.DMA((n_remote,)), pltpu.SemaphoreType.DMA((n_remote,)),
            pltpu.SemaphoreType.DMA((max(n_local, 1),))] if jobs else []
    res = pl.pallas_call(
        wrapped, name=name, grid=grid,
        in_specs=list(in_specs) + [ANY] * len(j_in), out_specs=list(out_specs) + [ANY] * len(j_out),
        out_shape=list(out_shape) + j_out, scratch_shapes=list(scratch) + sems, input_output_aliases=aliases,
        compiler_params=_params(*(("arbitrary",) * len(grid) if jobs else sem)),
    )(*args, *j_in)
    job_res, at = [], n_out
    for job in jobs:
        job_res.append(list(res[at:at + len(job.out_shape)]))
        at += len(job.out_shape)
    return list(res[:n_out]), job_res


def _gather_job(stage, gathered, sends=(), forwards=()):
    shape = _sds((N_DEV,) + stage.shape, stage.dtype)
    inputs = ([stage] if sends else []) + ([gathered] if gathered is not None else [])
    aliases = {len(inputs) - 1: 0} if gathered is not None else {}

    def emit(ins, outs, place):
        x, y, c = place
        sibling, chips, mine, g = (x, y, 1 - c), _other_chips(x, y), 4 * x + 2 * y + c, outs[0]
        found = []
        for r0, nr in sends:
            src, dst = ins[0].at[pl.ds(r0, nr)], g.at[mine, pl.ds(r0, nr)]
            found += [(src, dst, None), (src, dst, sibling)] + [(src, dst, (px, py, c)) for px, py in chips]
        for r0, nr in forwards:
            for px, py in chips:
                block = 4 * px + 2 * py + c
                found.append((ins[-1].at[block, pl.ds(r0, nr)], g.at[block, pl.ds(r0, nr)], sibling))
        return found

    return _Job(inputs, [shape], aliases, emit, 4 * len(sends) + 3 * len(forwards), len(sends))


def _scatter_job(arrays, n_slots, route, prev=None, piece=None):
    shapes = [_sds((n_slots,) + v.shape[1:], v.dtype) for v in arrays]
    inputs = list(arrays) + (list(prev) if prev else [])
    aliases = {len(arrays) + i: i for i in range(len(arrays))} if prev else {}

    def emit(ins, outs, place):
        found = []
        for a in range(len(arrays)):
            for s in range(n_slots):
                block, to = route(s, *place)
                if piece is None:
                    found.append((ins[a].at[block], outs[a].at[s], to))
                else:
                    found.append((ins[a].at[block, pl.ds(*piece)], outs[a].at[s, pl.ds(*piece)], to))
        return found

    return _Job(inputs, shapes, aliases, emit, len(arrays) * n_slots)


def _to_sibling(s, x, y, c):
    return 2 * s + (1 - c), (x, y, 1 - c)


def _to_owner_chip(s, x, y, c):
    px, py = _other_chips(x, y)[s]
    return 2 * px + py, (px, py, c)


def _to_owner(s, x, y, c):
    if s == 0:
        px, py, pc = x, y, 1 - c
    else:
        px, py = _other_chips(x, y)[(s - 1) % 3]
        pc = c if s <= 3 else 1 - c
    return 4 * px + 2 * py + pc, (px, py, pc)


def _rstd(x):
    return lax.rsqrt(jnp.mean(x * x, axis=-1, keepdims=True) + EPS)


def _rms_bwd(xhat, r, g, dy):
    dxh = dy * g
    return r * (dxh - xhat * jnp.mean(dxh * xhat, axis=-1, keepdims=True))


def _first_half(rows):
    lane = lax.broadcasted_iota(jnp.int32, (rows, PAIR), 1)
    return (lane % HEAD_DIM) < (HEAD_DIM // 2)


def _rot_half(v, first):
    return jnp.where(first, pltpu.roll(v, PAIR - HEAD_DIM // 2, 1), pltpu.roll(v, HEAD_DIM // 2, 1))


def _rope(v, cos, sin, first):
    return v * cos + _rot_half(v, first) * sin


def _rope_t(dv, cos, sin, first):
    return dv * cos + _rot_half(dv * sin, first)


def _dot(a, b):
    return jnp.dot(a, b, preferred_element_type=F32)


def _dot_nt(a, b):
    return lax.dot_general(a, b, (((1,), (1,)), ((), ())), preferred_element_type=F32)


def _dot_tn(a, b):
    return lax.dot_general(a, b, (((0,), (0,)), ((), ())), preferred_element_type=F32)


def _sigmoid(v):
    return 1.0 / (1.0 + jnp.exp(-v))


_GELU_K = math.sqrt(2.0 / math.pi)
_GELU_C = 0.044715


def _gelu(v):
    return v * (0.5 * (1.0 + jnp.tanh(_GELU_K * (v + _GELU_C * (v * v * v)))))


def _gelu_grad(v):
    t = jnp.tanh(_GELU_K * (v + _GELU_C * (v * v * v)))
    return 0.5 * (1.0 + t) + 0.5 * v * (1.0 - t * t) * (_GELU_K * (1.0 + 3.0 * _GELU_C * (v * v)))


def _attn_qkv_fwd(x, g, w, b, cos, sin, jobs=()):
    t, d = x.shape
    n = w.shape[1]
    kd = (n - d) // 2
    tm = _row_tile(t)
    ch = _pick(d, (512,))

    def body(x_ref, g_ref, w_ref, b_ref, cos_ref, sin_ref, h_ref, q_ref, k_ref, v_ref):
        xv = x_ref[...]
        hb = (xv * _rstd(xv) * g_ref[...]).astype(ACT)
        h_ref[...] = hb
        cosv, sinv = cos_ref[...], sin_ref[...]
        first = _first_half(tm)

        def proj(lo, width):
            return _dot(hb, w_ref[:, lo:lo + width]) + b_ref[:, lo:lo + width]

        for c in range(0, d, ch):
            y = proj(c, ch)
            for s in range(0, ch, PAIR):
                q_ref[:, c + s:c + s + PAIR] = (_rope(y[:, s:s + PAIR], cosv, sinv, first) * SCALE).astype(ACT)
        y = proj(d, kd)
        for s in range(0, kd, PAIR):
            k_ref[:, s:s + PAIR] = _rope(y[:, s:s + PAIR], cosv, sinv, first).astype(ACT)
        v_ref[...] = proj(d + kd, kd).astype(ACT)

    return _call(
        body, "attn_qkv_fwd", (t // tm,),
        [_rows(tm, d), _full((1, d)), _full((d, n)), _full((1, n)), _rows(tm, PAIR), _rows(tm, PAIR)],
        [_rows(tm, d), _rows(tm, d), _rows(tm, kd), _rows(tm, kd)],
        [_sds((t, d), ACT), _sds((t, d), ACT), _sds((t, kd), ACT), _sds((t, kd), ACT)],
        (x, g, w, b, cos, sin), ("parallel",), jobs=jobs)


STACK = GQA_GROUP * WINDOW


def _band_mask(has_prev):
    row = lax.broadcasted_iota(jnp.int32, (STACK, 2 * WINDOW), 0) % WINDOW
    col = lax.broadcasted_iota(jnp.int32, (STACK, 2 * WINDOW), 1)
    return ((col < WINDOW) & (col > row) & has_prev) | ((col >= WINDOW) & (col - WINDOW <= row))


def _lane_halves():
    lane = lax.broadcasted_iota(jnp.int32, (1, PAIR), 1)
    return lane < HEAD_DIM, lane >= HEAD_DIM


def _stack_heads(ref, h, halves):
    parts = []
    for p in range(2):
        pair = ref[:, (2 * h + p) * PAIR:(2 * h + p + 1) * PAIR]
        parts += [jnp.where(half, pair, jnp.zeros_like(pair)) for half in halves]
    return jnp.concatenate(parts, axis=0)


def _sink_column(sink_ref, h):
    row = lax.broadcasted_iota(jnp.int32, (STACK, 1), 0)
    col = jnp.full((STACK, 1), sink_ref[0, GQA_GROUP * h + GQA_GROUP - 1], F32)
    for j in range(GQA_GROUP - 2, -1, -1):
        col = jnp.where(row < (j + 1) * WINDOW, sink_ref[0, GQA_GROUP * h + j], col)
    return col


def _probs(qs, k2, valid, sink):
    s = jnp.where(valid, _dot_nt(qs, k2), MASKED)
    m = jnp.maximum(jnp.max(s, axis=-1, keepdims=True), sink)
    p = jnp.exp(s - m)
    psink = jnp.exp(sink - m)
    inv = 1.0 / (jnp.sum(p, axis=-1, keepdims=True) + psink)
    return p * inv, psink * inv


def _attn_fwd(q, kd_, vd, sinks, jobs=()):
    t, d = q.shape
    kd = kd_.shape[1]
    cur = lambda n: (n, 0)
    prev = lambda n: (jnp.maximum(n - 1, 0), 0)

    def body(sink_ref, q_ref, kc_ref, kp_ref, vc_ref, vp_ref, o_ref):
        valid = _band_mask(pl.program_id(0) > 0)
        halves = _lane_halves()
        for h in range(kd // PAIR):
            hs = slice(h * PAIR, (h + 1) * PAIR)
            k2 = jnp.concatenate([kp_ref[:, hs], kc_ref[:, hs]], axis=0)
            v2 = jnp.concatenate([vp_ref[:, hs], vc_ref[:, hs]], axis=0)
            vs = jnp.concatenate([jnp.where(half, v2, jnp.zeros_like(v2)) for half in halves], axis=0)
            pr, _ = _probs(_stack_heads(q_ref, h, halves), k2, valid, _sink_column(sink_ref, h))
            pr = pr.astype(ACT)
            for p in range(2):
                both = jnp.concatenate([pr[2 * p * WINDOW:(2 * p + 1) * WINDOW], pr[(2 * p + 1) * WINDOW:(2 * p + 2) * WINDOW]], axis=1)
                o_ref[:, (2 * h + p) * PAIR:(2 * h + p + 1) * PAIR] = _dot(both, vs).astype(ACT)

    kv_c, kv_p = pl.BlockSpec((WINDOW, kd), cur), pl.BlockSpec((WINDOW, kd), prev)
    return _call(
        body, "attn_fwd", (t // WINDOW,),
        [pl.BlockSpec(memory_space=pltpu.SMEM), pl.BlockSpec((WINDOW, d), cur), kv_c, kv_p, kv_c, kv_p],
        [pl.BlockSpec((WINDOW, d), cur)], [_sds((t, d), ACT)],
        (sinks, q, kd_, kd_, vd, vd), ("parallel",), jobs=jobs)


def _attn_bwd(q, kd_, vd, do, sinks, cos, sin, jobs=()):
    t, d = q.shape
    kd = kd_.shape[1]
    nb = t // WINDOW
    n_out = d + 2 * kd
    cur = lambda n: (jnp.minimum(n, nb - 1), 0)
    prev = lambda n: (jnp.maximum(jnp.minimum(n, nb - 1) - 1, 0), 0)
    late = lambda n: (jnp.maximum(n - 1, 0), 0)

    def body(sink_ref, q_ref, kc_ref, kp_ref, vc_ref, vp_ref, do_ref, cosc_ref, sinc_ref, cosp_ref, sinp_ref,
             out_ref, db_ref, dsink_ref, dq_keep, dk_keep, dv_keep):
        n = pl.program_id(0)

        @pl.when(n == 0)
        def _():
            for ref in (db_ref, dsink_ref, dq_keep, dk_keep, dv_keep):
                ref[...] = jnp.zeros_like(ref)

        valid = _band_mask(n > 0) & (n < nb)
        halves = _lane_halves()
        first = _first_half(WINDOW)
        slot = lax.broadcasted_iota(jnp.int32, dsink_ref.shape, 1)
        top = lax.broadcasted_iota(jnp.int32, dsink_ref.shape, 0) == 0
        dsink = jnp.zeros(dsink_ref.shape, F32)

        def emit(cols, done):
            out_ref[:, cols] = done.astype(ACT)
            db_ref[:, cols] += jnp.sum(done.astype(F32), axis=0, keepdims=True)

        for h in range(kd // PAIR):
            hs = slice(h * PAIR, (h + 1) * PAIR)
            k2 = jnp.concatenate([kp_ref[:, hs], kc_ref[:, hs]], axis=0)
            v2 = jnp.concatenate([vp_ref[:, hs], vc_ref[:, hs]], axis=0)
            qs, dos = _stack_heads(q_ref, h, halves), _stack_heads(do_ref, h, halves)
            pr, psink = _probs(qs, k2, valid, _sink_column(sink_ref, h))
            dp = _dot_nt(dos, v2)
            delta = jnp.sum(pr * dp, axis=-1, keepdims=True)
            ds = (pr * (dp - delta)).astype(ACT)
            leak = psink * delta
            for j in range(GQA_GROUP):
                share = jnp.sum(leak[j * WINDOW:(j + 1) * WINDOW], axis=0, keepdims=True)
                dsink = dsink - jnp.where(top & (slot == GQA_GROUP * h + j), share, 0.0)
            dqs = _dot(ds, k2)
            for p in range(2):
                ps = slice((2 * h + p) * PAIR, (2 * h + p + 1) * PAIR)
                dqp = jnp.where(halves[0], dqs[2 * p * WINDOW:(2 * p + 1) * WINDOW], dqs[(2 * p + 1) * WINDOW:(2 * p + 2) * WINDOW])
                emit(ps, dq_keep[:, ps])
                dq_keep[:, ps] = (_rope_t(dqp, cosc_ref[...], sinc_ref[...], first) * SCALE).astype(ACT)
            dk2 = _dot_tn(ds, qs)
            dv2 = _dot_tn(pr.astype(ACT), dos)
            ks = slice(d + h * PAIR, d + (h + 1) * PAIR)
            vs = slice(d + kd + h * PAIR, d + kd + (h + 1) * PAIR)
            emit(ks, dk_keep[:, hs] + _rope_t(dk2[:WINDOW], cosp_ref[...], sinp_ref[...], first))
            dk_keep[:, hs] = _rope_t(dk2[WINDOW:], cosc_ref[...], sinc_ref[...], first)
            emit(vs, dv_keep[:, hs] + dv2[:WINDOW])
            dv_keep[:, hs] = dv2[WINDOW:]
        dsink_ref[...] += dsink

    kv_c, kv_p = pl.BlockSpec((WINDOW, kd), cur), pl.BlockSpec((WINDOW, kd), prev)
    tab_c, tab_p = pl.BlockSpec((WINDOW, PAIR), cur), pl.BlockSpec((WINDOW, PAIR), prev)
    return _call(
        body, "attn_bwd", (nb + 1,),
        [pl.BlockSpec(memory_space=pltpu.SMEM), pl.BlockSpec((WINDOW, d), cur), kv_c, kv_p, kv_c, kv_p,
         pl.BlockSpec((WINDOW, d), cur), tab_c, tab_c, tab_p, tab_p],
        [pl.BlockSpec((WINDOW, n_out), late), _full((1, n_out)), _full((8, LANES))],
        [_sds((t, n_out), ACT), _sds((1, n_out), F32), _sds((8, LANES), F32)],
        (sinks, q, kd_, kd_, vd, vd, do, cos, sin, cos, sin), ("arbitrary",),
        scratch=[pltpu.VMEM((WINDOW, d), ACT), pltpu.VMEM((WINDOW, kd), F32), pltpu.VMEM((WINDOW, kd), F32)], jobs=jobs)


def _proj_res(a, w, b, g, x, name, jobs=()):
    blocked = a.ndim == 3
    t = a.shape[-2]
    k, d = w.shape
    tm = _row_tile(t)
    has_bias = b is not None

    def body(*refs):
        a_ref, w_ref = refs[:2]
        b_ref = refs[2] if has_bias else None
        g_ref, x_ref, m_ref, xo_ref = refs[2 + has_bias:]
        if blocked:
            c = a.shape[2]
            m = _dot(a_ref[0], w_ref[:c, :])
            for j in range(1, a.shape[0]):
                m = m + _dot(a_ref[j], w_ref[j * c:(j + 1) * c, :])
        else:
            m = _dot(a_ref[...], w_ref[...])
        if has_bias:
            m = m + b_ref[...]
        m_ref[...] = m
        xo_ref[...] = x_ref[...] + (m * _rstd(m)) * g_ref[...]

    ins = [a, w] + ([b] if has_bias else []) + [g, x]
    a_spec = pl.BlockSpec((a.shape[0], tm, a.shape[2]), lambda i: (0, i, 0)) if blocked else _rows(tm, k)
    specs = [a_spec, _full((k, d))] + ([_full((1, d))] if has_bias else []) + [_full((1, d)), _rows(tm, d)]
    return _call(body, name, (t // tm,), specs, [_rows(tm, d), _rows(tm, d)], [_sds((t, d), F32)] * 2, ins, ("parallel",), jobs=jobs)


def _post_bwd(dres, m, g, w, mode, name, gu=None, jobs=()):
    t, d = dres.shape
    k = w.shape[0]
    tm = _row_tile(t)
    kc = _pick(k, (1408, 1024, 512))

    def body(*refs):
        d_ref, m_ref, g_ref, w_ref = refs[:4]
        gu_ref = refs[4] if mode == "ffn" else None
        outs = refs[4 + (mode == "ffn"):]
        dm_ref, da_ref, dg_ref = outs[:3]
        i = pl.program_id(0)

        @pl.when(i == 0)
        def _():
            dg_ref[...] = jnp.zeros_like(dg_ref)
            if mode == "attn":
                outs[3][...] = jnp.zeros_like(outs[3])

        dv, mv = d_ref[...], m_ref[...]
        r = _rstd(mv)
        mh = mv * r
        dg_ref[...] += jnp.sum(dv * mh, axis=0, keepdims=True)
        dm = _rms_bwd(mh, r, g_ref[...], dv)
        dmb = dm.astype(ACT)
        dm_ref[...] = dmb
        if mode == "attn":
            outs[3][...] += jnp.sum(dm, axis=0, keepdims=True)
        if mode == "ffn":
            nb, _, cb = gu.shape
            for j in range(nb // 2):
                da = _dot_nt(dmb, w_ref[j * cb:(j + 1) * cb, :])
                gate = gu_ref[j].astype(F32)
                up = gu_ref[nb // 2 + j].astype(F32)
                sg = _sigmoid(gate)
                da_ref[j] = (da * up * (sg * (1.0 + gate * (1.0 - sg)))).astype(ACT)
                da_ref[nb // 2 + j] = (da * (gate * sg)).astype(ACT)
        else:
            for c in range(0, k, kc):
                da = _dot_nt(dmb, w_ref[c:c + kc, :])
                da_ref[:, c:c + kc] = da.astype(ACT) if mode == "attn" else da

    ins = [dres, m, g, w]
    specs = [_rows(tm, d), _rows(tm, d), _full((1, d)), _full((k, d))]
    if mode == "ffn":
        slab = pl.BlockSpec((gu.shape[0], tm, gu.shape[2]), lambda i: (0, i, 0))
        ins.append(gu)
        specs.append(slab)
        out_specs = [_rows(tm, d), slab, _full((1, d))]
        out_shape = [_sds((t, d), ACT), _sds(gu.shape, ACT), _sds((1, d), F32)]
    else:
        out_specs = [_rows(tm, d), _rows(tm, k), _full((1, d))]
        out_shape = [_sds((t, d), ACT), _sds((t, k), ACT if mode == "attn" else F32), _sds((1, d), F32)]
    if mode == "attn":
        out_specs.append(_full((1, d)))
        out_shape.append(_sds((1, d), F32))
    return _call(body, name, (t // tm,), specs, out_specs, out_shape, ins, ("arbitrary",), jobs=jobs)


def _pre_bwd(dy, w, x, g, dres, name, jobs=()):
    t, d = x.shape
    tm = _row_tile(t)

    def body(dy_ref, w_ref, x_ref, g_ref, dres_ref, dx_ref, dg_ref):
        @pl.when(pl.program_id(0) == 0)
        def _():
            dg_ref[...] = jnp.zeros_like(dg_ref)

        dh = jnp.zeros((tm, d), F32)
        if w.ndim == 3:
            c = w.shape[2]
            for j in range(w.shape[0]):
                dh = dh + _dot_nt(dy_ref[j] if dy.ndim == 3 else dy_ref[:, j * c:(j + 1) * c], w_ref[j])
        else:
            n = w.shape[1]
            nc = _pick(n, (1408, 1024, 512))
            for c in range(0, n, nc):
                dh = dh + _dot_nt(dy_ref[:, c:c + nc], w_ref[:, c:c + nc])
        xv = x_ref[...]
        r = _rstd(xv)
        xh = xv * r
        dg_ref[...] += jnp.sum(dh * xh, axis=0, keepdims=True)
        dx_ref[...] = dres_ref[...] + _rms_bwd(xh, r, g_ref[...], dh)

    dy_spec = pl.BlockSpec((dy.shape[0], tm, dy.shape[2]), lambda i: (0, i, 0)) if dy.ndim == 3 else _rows(tm, dy.shape[1])
    return _call(
        body, name, (t // tm,),
        [dy_spec, _full(w.shape), _rows(tm, d), _full((1, d)), _rows(tm, d)],
        [_rows(tm, d), _full((1, d))], [_sds((t, d), F32), _sds((1, d), F32)],
        (dy, w, x, g, dres), ("arbitrary",), jobs=jobs)


def _wgrad(a, b, name, out_dtype=F32, out_block=None, jobs=()):
    t = a.shape[-2]
    tt = _pick(t, (1024,))
    steps = t // tt
    if a.ndim == 3:
        k_steps, _, tk = a.shape
        a_spec = pl.BlockSpec((None, tt, tk), lambda i, j, s: (i, s, 0))
    else:
        tk = _pick(a.shape[1], (1024, 1408))
        k_steps = a.shape[1] // tk
        a_spec = pl.BlockSpec((tt, tk), lambda i, j, s: (s, i))
    k = k_steps * tk
    if b.ndim == 3:
        n_steps, _, tn = b.shape
        per = 1
        b_spec = pl.BlockSpec((None, tt, tn), lambda i, j, s: (j, s, 0))
        out_spec, out_shape = pl.BlockSpec((None, tk, tn), lambda i, j, s: (j, i, 0)), _sds((n_steps, k, tn), out_dtype)
    else:
        n = b.shape[1]
        tn = _pick(n, (1024, 1408))
        n_steps = n // tn
        b_spec = pl.BlockSpec((tt, tn), lambda i, j, s: (s, j))
        if out_block is None:
            per = 0
            out_spec, out_shape = pl.BlockSpec((tk, tn), lambda i, j, s: (i, j)), _sds((k, n), out_dtype)
        else:
            per = tn // out_block
            out_spec = pl.BlockSpec((per, tk, out_block), lambda i, j, s: (j, i, 0))
            out_shape = _sds((n // out_block, k, out_block), out_dtype)

    def body(a_ref, b_ref, o_ref, acc_ref):
        s = pl.program_id(2)

        @pl.when(s == 0)
        def _():
            acc_ref[...] = jnp.zeros_like(acc_ref)

        acc_ref[...] += _dot_tn(a_ref[...], b_ref[...])

        @pl.when(s == steps - 1)
        def _():
            if per > 1 or (per == 1 and b.ndim == 2):
                for p in range(per):
                    o_ref[p] = acc_ref[:, p * out_block:(p + 1) * out_block].astype(out_dtype)
            else:
                o_ref[...] = acc_ref[...].astype(out_dtype)

    res, job_res = _call(
        body, name, (k_steps, n_steps, steps), [a_spec, b_spec], [out_spec], [out_shape],
        (a, b), ("parallel", "parallel", "arbitrary"), scratch=[pltpu.VMEM((tk, tn), F32)], jobs=jobs)
    return res[0], job_res


def _ffn_up_fwd(x, g, w, jobs=()):
    t, d = x.shape
    nb, _, c = w.shape
    tm = _row_tile(t)

    def body(x_ref, g_ref, w_ref, h_ref, gu_ref, a_ref):
        xv = x_ref[...]
        hb = (xv * _rstd(xv) * g_ref[...]).astype(ACT)
        h_ref[...] = hb
        for j in range(nb // 2):
            gate = _dot(hb, w_ref[j])
            up = _dot(hb, w_ref[nb // 2 + j])
            gu_ref[j] = gate.astype(ACT)
            gu_ref[nb // 2 + j] = up.astype(ACT)
            a_ref[j] = (gate * _sigmoid(gate) * up).astype(ACT)

    slab = lambda n: pl.BlockSpec((n, tm, c), lambda i: (0, i, 0))
    return _call(
        body, "ffn_up_fwd", (t // tm,),
        [_rows(tm, d), _full((1, d)), _full((nb, d, c))],
        [_rows(tm, d), slab(nb), slab(nb // 2)],
        [_sds((t, d), ACT), _sds((nb, t, c), ACT), _sds((nb // 2, t, c), ACT)],
        (x, g, w), ("parallel",), jobs=jobs)


def _sgu_in_fwd(x, g, w, jobs=()):
    t, d = x.shape
    nb, _, c = w.shape
    n = nb * c
    tm = _row_tile(t)

    def body(x_ref, g_ref, w_ref, h_ref, z_ref):
        xv = x_ref[...]
        hb = (xv * _rstd(xv) * g_ref[...]).astype(ACT)
        h_ref[...] = hb
        for j in range(nb):
            z_ref[:, j * c:(j + 1) * c] = _dot(hb, w_ref[j])

    return _call(
        body, "sgu_in_fwd", (t // tm,), [_rows(tm, d), _full((1, d)), _full((nb, d, c))],
        [_rows(tm, d), _rows(tm, n)], [_sds((t, d), ACT), _sds((t, n), F32)], (x, g, w), ("parallel",), jobs=jobs)


def _causal(ws):
    row = lax.broadcasted_iota(jnp.int32, ws.shape, 0)
    col = lax.broadcasted_iota(jnp.int32, ws.shape, 1)
    return jnp.where(col <= row, ws, 0.0)


def _sgu_ln(v, lg, lb):
    mu = jnp.mean(v, axis=-1, keepdims=True)
    vc = v - mu
    rs = lax.rsqrt(jnp.mean(vc * vc, axis=-1, keepdims=True) + EPS)
    vh = vc * rs
    return vh, rs, vh * lg + lb


def _sgu_mix_fwd(z, lg, lb, ws, bs_t, jobs=()):
    t, n = z.shape
    d = n // 2
    groups = d // WINDOW
    tm = _row_tile(t)

    def body(z_ref, lg_ref, lb_ref, ws_ref, bs_ref, y_ref):
        for c in range(0, tm, WINDOW):
            u = _gelu(z_ref[c:c + WINDOW, :d])
            _, _, vn = _sgu_ln(_gelu(z_ref[c:c + WINDOW, d:]), lg_ref[...], lb_ref[...])
            for gi in range(groups):
                gs = slice(gi * WINDOW, (gi + 1) * WINDOW)
                mixed = _dot(_causal(ws_ref[gi]).astype(ACT), vn[:, gs].astype(ACT)) + bs_ref[:, gi:gi + 1]
                y_ref[c:c + WINDOW, gs] = (u[:, gs] * mixed).astype(ACT)

    return _call(
        body, "sgu_mix_fwd", (t // tm,),
        [_rows(tm, n), _full((1, d)), _full((1, d)), _full((groups, WINDOW, WINDOW)), _full((WINDOW, groups))],
        [_rows(tm, d)], [_sds((t, d), ACT)], (z, lg, lb, ws, bs_t), ("parallel",), jobs=jobs)


def _sgu_mix_bwd(z, dy, lg, lb, ws, bs_t, jobs=()):
    t, n = z.shape
    d = n // 2
    groups = d // WINDOW
    tm = _row_tile(t)

    def body(z_ref, dy_ref, lg_ref, lb_ref, ws_ref, bs_ref, dz_ref, dlg_ref, dlb_ref, dws_ref, dbs_ref):
        @pl.when(pl.program_id(0) == 0)
        def _():
            for ref in (dlg_ref, dlb_ref, dws_ref, dbs_ref):
                ref[...] = jnp.zeros_like(ref)

        lane = lax.broadcasted_iota(jnp.int32, (WINDOW, groups), 1)
        for c in range(0, tm, WINDOW):
            rows = slice(c, c + WINDOW)
            zu, zv = z_ref[rows, :d], z_ref[rows, d:]
            u = _gelu(zu)
            vh, rs, vn = _sgu_ln(_gelu(zv), lg_ref[...], lb_ref[...])
            dyv = dy_ref[rows, :]
            dvn_parts = []
            for gi in range(groups):
                gs = slice(gi * WINDOW, (gi + 1) * WINDOW)
                wsg = _causal(ws_ref[gi]).astype(ACT)
                vng = vn[:, gs].astype(ACT)
                mixed = _dot(wsg, vng) + bs_ref[:, gi:gi + 1]
                dz_ref[rows, gs] = (dyv[:, gs] * mixed * _gelu_grad(zu[:, gs])).astype(ACT)
                dmix = dyv[:, gs] * u[:, gs]
                dmb = dmix.astype(ACT)
                dvn_parts.append(_dot_tn(wsg, dmb))
                dws_ref[gi] += _dot_nt(dmb, vng)
                dbs_ref[...] += jnp.where(lane == gi, jnp.sum(dmix, axis=-1, keepdims=True), 0.0)
            dvn = jnp.concatenate(dvn_parts, axis=-1)
            dlg_ref[...] += jnp.sum(dvn * vh, axis=0, keepdims=True)
            dlb_ref[...] += jnp.sum(dvn, axis=0, keepdims=True)
            dvh = dvn * lg_ref[...]
            dv = rs * (dvh - jnp.mean(dvh, axis=-1, keepdims=True) - vh * jnp.mean(dvh * vh, axis=-1, keepdims=True))
            dz_ref[rows, d:] = (dv * _gelu_grad(zv)).astype(ACT)

    vec = _full((1, d))
    return _call(
        body, "sgu_mix_bwd", (t // tm,),
        [_rows(tm, n), _rows(tm, d), vec, vec, _full((groups, WINDOW, WINDOW)), _full((WINDOW, groups))],
        [_rows(tm, n), vec, vec, _full((groups, WINDOW, WINDOW)), _full((WINDOW, groups))],
        [_sds((t, n), ACT), _sds((1, d), F32), _sds((1, d), F32), _sds((groups, WINDOW, WINDOW), F32), _sds((WINDOW, groups), F32)],
        (z, dy, lg, lb, ws, bs_t), ("arbitrary",), jobs=jobs)


def _loss_head(y, target):
    t, d = y.shape
    tm = _row_tile(t)

    def body(y_ref, t_ref, dy_ref, loss_ref):
        @pl.when(pl.program_id(0) == 0)
        def _():
            loss_ref[...] = jnp.zeros_like(loss_ref)

        err = y_ref[...] - t_ref[...]
        dy_ref[...] = err * (1.0 / d)
        per_token = jnp.mean(err * err, axis=-1, keepdims=True)
        loss_ref[...] += 0.5 * jnp.sum(per_token, axis=0, keepdims=True)

    res, _ = _call(body, "loss_head", (t // tm,), [_rows(tm, d), _rows(tm, d)], [_rows(tm, d), _full((8, LANES))],
                   [_sds((t, d), F32), _sds((8, LANES), F32)], (y, target), ("arbitrary",))
    return res


AG_COPIES = 7


def _prepare(sources, dtypes, n_gather, name):
    n = len(sources)
    arrays, where = [], []
    for parts, layer in sources:
        found = []
        for part in parts:
            known = [i for i, v in enumerate(arrays) if v is part]
            if not known:
                arrays.append(part)
            found.append(known[0] if known else len(arrays) - 1)
        where.append((found, layer))
    shapes = [(sum(p.shape[1] for p in parts), parts[0].shape[2]) for parts, _ in sources]

    def body(*refs):
        ins, refs = refs[:len(arrays)], refs[len(arrays):]
        stage, outs = refs[:n], refs[n:n + n_gather]
        send, recv, local_sem = refs[n + n_gather:]
        x, y, c = _place()
        sibling = (x, y, 1 - c)
        chips = _other_chips(x, y)
        mine = 4 * x + 2 * y + c

        def copy(a, k, block, to, src=None):
            dst = outs[a].at[block]
            return pltpu.make_async_remote_copy(
                src_ref=dst if src is None else src, dst_ref=dst, send_sem=send.at[a * AG_COPIES + k],
                recv_sem=recv.at[a * AG_COPIES + k], device_id=to, device_id_type=MESH)

        for a, (found, layer) in enumerate(where):
            at = 0
            for i in found:
                rows = arrays[i].shape[1]
                stage[a][at:at + rows, :] = ins[i][layer].astype(dtypes[a])
                at += rows
        started = []
        for a in range(n_gather):
            own = pltpu.make_async_copy(stage[a], outs[a].at[mine], local_sem.at[a])
            own.start()
            started.append(own)
        sends = []
        for a in range(n_gather):
            sends.append(copy(a, 0, mine, sibling, src=stage[a]))
            sends += [copy(a, 1 + j, mine, (*chip, c), src=stage[a]) for j, chip in enumerate(chips)]
        for cp in sends:
            cp.start()
        for j, (px, py) in enumerate(chips):
            block = 4 * px + 2 * py + c
            for a in range(n_gather):
                copy(a, 1 + j, block, (x, y, c)).wait_recv()
                forward = copy(a, 4 + j, block, sibling)
                forward.start()
                sends.append(forward)
        for a in range(n_gather):
            copy(a, 0, 4 * x + 2 * y + (1 - c), (x, y, c)).wait_recv()
            for j, (px, py) in enumerate(chips):
                copy(a, 4 + j, 4 * px + 2 * py + (1 - c), (x, y, c)).wait_recv()
        for cp in sends:
            cp.wait_send()
        for own in started:
            own.wait()

    res = pl.pallas_call(
        body, name=name,
        in_specs=[IN_VMEM] * len(arrays), out_specs=[IN_VMEM] * n + [ANY] * n_gather,
        out_shape=[_sds(s, dt) for s, dt in zip(shapes, dtypes)]
        + [_sds((N_DEV,) + s, dt) for s, dt in zip(shapes[:n_gather], dtypes)],
        scratch_shapes=[pltpu.SemaphoreType.DMA((n_gather * AG_COPIES,)), pltpu.SemaphoreType.DMA((n_gather * AG_COPIES,)),
                        pltpu.SemaphoreType.DMA((n_gather,))],
        compiler_params=pltpu.CompilerParams(vmem_limit_bytes=VMEM_LIMIT_BYTES),
    )(*arrays)
    return list(res[:n]), list(res[n:])


def _pair_sum(g, r, core, name):
    _, _, rows, cols = g.shape
    tr = _pick(rows, (512, 256, 128))

    def body(core_ref, g_ref, r_ref, p_ref):
        del core_ref
        p_ref[...] = (g_ref[...].astype(F32) + r_ref[...].astype(F32)).astype(p_ref.dtype)

    return pl.pallas_call(
        body, name=name,
        grid_spec=pltpu.PrefetchScalarGridSpec(
            num_scalar_prefetch=1, grid=(4, rows // tr),
            in_specs=[pl.BlockSpec((None, None, tr, cols), lambda q, i, core_ref: (q, core_ref[0], i, 0)),
                      pl.BlockSpec((None, tr, cols), lambda q, i, core_ref: (q, i, 0))],
            out_specs=pl.BlockSpec((None, tr, cols), lambda q, i, core_ref: (q, i, 0))),
        out_shape=_sds((4, rows, cols), g.dtype),
        compiler_params=_params("parallel", "parallel"),
    )(core, g, r)


def _adam(w, g, m, v):
    m2 = ADAM_B1 * m + (1.0 - ADAM_B1) * g
    v2 = ADAM_B2 * v + (1.0 - ADAM_B2) * (g * g)
    m_hat = m2 / (1.0 - ADAM_B1 ** ADAM_STEP)
    v_hat = v2 / (1.0 - ADAM_B2 ** ADAM_STEP)
    return -ADAM_LR * (m_hat / (jnp.sqrt(v_hat) + ADAM_EPS) + ADAM_WD * w), m2, v2


def _shard_update(own, index, received, w, m, v, layer, so_far, name):
    _, rows, cols = w.shape
    n_recv = received.shape[0]
    tr = _pick(rows, (512, 256, 128))

    def body(index_ref, p_ref, q_ref, w_ref, m_ref, v_ref, *rest):
        del index_ref
        g_out, d_out, m_out, v_out = rest[-4:]
        g = p_ref[...].astype(F32)
        for s in range(n_recv):
            g = g + q_ref[s].astype(F32)
        g_out[...] = g
        d_out[...], m_out[...], v_out[...] = _adam(w_ref[...], g, m_ref[...], v_ref[...])

    tile = pl.BlockSpec((None, tr, cols), lambda i, index_ref: (layer, i, 0))
    kept = list(so_far) if so_far is not None else []
    return pl.pallas_call(
        body, name=name,
        grid_spec=pltpu.PrefetchScalarGridSpec(
            num_scalar_prefetch=1, grid=(rows // tr,),
            in_specs=[pl.BlockSpec((None, tr, cols), lambda i, index_ref: (index_ref[0], i, 0)),
                      pl.BlockSpec((n_recv, tr, cols), lambda i, index_ref: (0, i, 0)), tile, tile, tile] + [ANY] * len(kept),
            out_specs=[tile] * 4),
        out_shape=[_sds(w.shape, F32)] * 4,
        input_output_aliases={6 + i: i for i in range(len(kept))},
        compiler_params=_params("parallel"),
    )(index, own, received, w, m, v, *kept)


def _replicated_update(shares, w, m, v):
    rows, cols = w.shape

    def body(s_ref, w_ref, m_ref, v_ref, g_out, d_out, m_out, v_out):
        g = s_ref[0]
        for dev in range(1, N_DEV):
            g = g + s_ref[dev]
        g_out[...] = g
        d_out[...], m_out[...], v_out[...] = _adam(w_ref[...], g, m_ref[...], v_ref[...])

    return pl.pallas_call(
        body, name="replicated_update_%d" % rows, out_shape=[_sds((rows, cols), F32)] * 4,
        compiler_params=pltpu.CompilerParams(vmem_limit_bytes=VMEM_LIMIT_BYTES),
    )(shares, w, m, v)


def _rope_tables(t):
    half = HEAD_DIM // 2
    inv_freq = ROPE_THETA ** (-(jnp.arange(half, dtype=F32) * 2.0) / HEAD_DIM)
    ang = jnp.arange(t, dtype=jnp.int32).astype(F32)[:, None] * inv_freq[None, :]
    cos, sin = jnp.cos(ang), jnp.sin(ang)
    return jnp.tile(jnp.concatenate([cos, cos], axis=1), (1, 2)), jnp.tile(jnp.concatenate([-sin, sin], axis=1), (1, 2))


def _dup_heads(w, d):
    kv = (w.shape[-1] - d) // 2
    lead = w.shape[:-1]

    def dup(part):
        heads = part.reshape(lead + (kv // HEAD_DIM, 1, HEAD_DIM))
        return jnp.broadcast_to(heads, lead + (kv // HEAD_DIM, 2, HEAD_DIM)).reshape(lead + (2 * kv,))

    return jnp.concatenate([w[..., :d], dup(w[..., d:d + kv]), dup(w[..., d + kv:])], axis=-1)


def _fold_heads(dw, d):
    kv = (dw.shape[-1] - d) // 4
    lead = dw.shape[:-1]

    def fold(part):
        return part.reshape(lead + (kv // HEAD_DIM, 2, HEAD_DIM)).sum(axis=-2).reshape(lead + (kv,))

    return jnp.concatenate([dw[..., :d], fold(dw[..., d:d + 2 * kv]), fold(dw[..., d + 2 * kv:])], axis=-1)


def _columns_full(gathered):
    _, rows, c = gathered.shape
    return jnp.transpose(gathered, (1, 0, 2)).reshape(rows, N_DEV * c)


def _rows_full(gathered):
    return gathered.reshape(-1, gathered.shape[-1])


def _columns_blocked(full):
    rows, cols = full.shape
    return jnp.transpose(full.reshape(rows, N_DEV, cols // N_DEV), (1, 0, 2)).astype(ACT)


def _rows_blocked(full):
    return full.reshape(N_DEV, full.shape[0] // N_DEV, full.shape[1]).astype(ACT)


def _packed_rows(size, width):
    return -(-size // (8 * width)) * 8


def _pack_rows(parts, width):
    rows = []
    for v in parts:
        flat = v.reshape(-1).astype(F32)
        n_rows = _packed_rows(flat.shape[0], width)
        rows.append(jnp.pad(flat, (0, n_rows * width - flat.shape[0])).reshape(n_rows, width))
    return jnp.concatenate(rows, axis=0)


def _unpack_rows(packed, like, width):
    out, at = [], 0
    for v in like:
        size = math.prod(v.shape)
        out.append(packed[at:at + -(-size // width)].reshape(-1)[:size].reshape(v.shape))
        at += _packed_rows(size, width)
    return out


def _halves(block):
    half = block.shape[0] // 2
    return (0, half), (half, half)


def _whole(block):
    return (0, block.shape[0])


EARLY = ("attn_w_qkv", "sgu_ln", "attn_w_o")
LATE = ("sgu_w_in", "sgu_w_out", "gu0", "gu1", "dn0", "dn1")
COLUMN_SHARDED = ("attn_w_qkv", "sgu_w_in", "gu0", "gu1")
LAST = ("norm_mix_pre",)
REPLICATED = ("norm_mix_post", "norm_ffn_pre", "norm_ffn_post", "attn_b_qkv", "attn_sinks", "attn_b_o", "sgu_w_spatial", "sgu_b_spatial")
WEIGHTS = ("norm_mix_pre", "norm_mix_post", "norm_ffn_pre", "norm_ffn_post", "attn_w_qkv", "attn_b_qkv", "attn_sinks", "attn_w_o",
           "attn_b_o", "sgu_w_in", "sgu_ln_g", "sgu_ln_b", "sgu_w_spatial", "sgu_b_spatial", "sgu_w_out", "ffn_w_gate_up", "ffn_w_down")


LAYER_OF = {"gu0": ("ffn_w_gate_up", 0), "gu1": ("ffn_w_gate_up", 1), "dn0": ("ffn_w_down", 0), "dn1": ("ffn_w_down", 1)}


def _source(tree, name):
    if name == "sgu_ln":
        return [tree["sgu_ln_g"][None], tree["sgu_ln_b"][None]], 0
    leaf, layer = LAYER_OF.get(name, (name, 0))
    return [tree[leaf]], layer


def kernel(x, norm_mix_pre, norm_mix_post, norm_ffn_pre, norm_ffn_post, attn_w_qkv, attn_b_qkv, attn_sinks, attn_w_o, attn_b_o, sgu_w_in, sgu_ln_g, sgu_ln_b, sgu_w_spatial, sgu_b_spatial, sgu_w_out, ffn_w_gate_up, ffn_w_down, loss_target, m_norm_mix_pre, m_norm_mix_post, m_norm_ffn_pre, m_norm_ffn_post, m_attn_w_qkv, m_attn_b_qkv, m_attn_sinks, m_attn_w_o, m_attn_b_o, m_sgu_w_in, m_sgu_ln_g, m_sgu_ln_b, m_sgu_w_spatial, m_sgu_b_spatial, m_sgu_w_out, m_ffn_w_gate_up, m_ffn_w_down, v_norm_mix_pre, v_norm_mix_post, v_norm_ffn_pre, v_norm_ffn_post, v_attn_w_qkv, v_attn_b_qkv, v_attn_sinks, v_attn_w_o, v_attn_b_o, v_sgu_w_in, v_sgu_ln_g, v_sgu_ln_b, v_sgu_w_spatial, v_sgu_b_spatial, v_sgu_w_out, v_ffn_w_gate_up, v_ffn_w_down):
    w = dict(norm_mix_pre=norm_mix_pre, norm_mix_post=norm_mix_post, norm_ffn_pre=norm_ffn_pre, norm_ffn_post=norm_ffn_post,
             attn_w_qkv=attn_w_qkv, attn_b_qkv=attn_b_qkv, attn_sinks=attn_sinks, attn_w_o=attn_w_o, attn_b_o=attn_b_o,
             sgu_w_in=sgu_w_in, sgu_ln_g=sgu_ln_g, sgu_ln_b=sgu_ln_b, sgu_w_spatial=sgu_w_spatial, sgu_b_spatial=sgu_b_spatial,
             sgu_w_out=sgu_w_out, ffn_w_gate_up=ffn_w_gate_up, ffn_w_down=ffn_w_down)
    mom = dict(norm_mix_pre=m_norm_mix_pre, norm_mix_post=m_norm_mix_post, norm_ffn_pre=m_norm_ffn_pre, norm_ffn_post=m_norm_ffn_post,
               attn_w_qkv=m_attn_w_qkv, attn_b_qkv=m_attn_b_qkv, attn_sinks=m_attn_sinks, attn_w_o=m_attn_w_o, attn_b_o=m_attn_b_o,
               sgu_w_in=m_sgu_w_in, sgu_ln_g=m_sgu_ln_g, sgu_ln_b=m_sgu_ln_b, sgu_w_spatial=m_sgu_w_spatial,
               sgu_b_spatial=m_sgu_b_spatial, sgu_w_out=m_sgu_w_out, ffn_w_gate_up=m_ffn_w_gate_up, ffn_w_down=m_ffn_w_down)
    var = dict(norm_mix_pre=v_norm_mix_pre, norm_mix_post=v_norm_mix_post, norm_ffn_pre=v_norm_ffn_pre, norm_ffn_post=v_norm_ffn_post,
               attn_w_qkv=v_attn_w_qkv, attn_b_qkv=v_attn_b_qkv, attn_sinks=v_attn_sinks, attn_w_o=v_attn_w_o, attn_b_o=v_attn_b_o,
               sgu_w_in=v_sgu_w_in, sgu_ln_g=v_sgu_ln_g, sgu_ln_b=v_sgu_ln_b, sgu_w_spatial=v_sgu_w_spatial,
               sgu_b_spatial=v_sgu_b_spatial, sgu_w_out=v_sgu_w_out, ffn_w_gate_up=v_ffn_w_gate_up, ffn_w_down=v_ffn_w_down)
    xs, target = x[0], loss_target[0]
    t, d = xs.shape
    row = lambda v: v.reshape(1, -1).astype(F32)
    core = lax.axis_index("c").astype(jnp.int32).reshape(1)
    chip = (2 * lax.axis_index("x") + lax.axis_index("y")).astype(jnp.int32).reshape(1)
    me = (4 * lax.axis_index("x") + 2 * lax.axis_index("y") + lax.axis_index("c")).astype(jnp.int32).reshape(1)
    names = EARLY + LATE
    staged, early = _prepare([_source(w, n) for n in names], [F32 if n == "sgu_ln" else ACT for n in names], len(EARLY),
                             "weights_prepare")
    stage = dict(zip(names, staged))
    w_qkv = _dup_heads(_columns_full(early[0]), d)
    lg, lb = row(early[1][:, 0, :]), row(early[1][:, 1, :])
    w_o = _rows_full(early[2])
    b_qkv = _dup_heads(row(attn_b_qkv), d)
    sinks = attn_sinks.reshape(1, -1).astype(F32)
    ws = sgu_w_spatial[0].astype(F32)
    bs_t = sgu_b_spatial[0].astype(F32).T
    cos, sin = _rope_tables(t)
    gather = lambda name, so_far, **pieces: _gather_job(stage[name], so_far, **pieces)
    a_half = lambda name: _halves(stage[name])[0]
    b_half = lambda name: _halves(stage[name])[1]
    whole = lambda name: _whole(stage[name])

    (h0, q, kdup, vdup), ((g_gu0,),) = _attn_qkv_fwd(
        xs, row(norm_mix_pre[0]), w_qkv, b_qkv, cos, sin, jobs=[gather("gu0", None, sends=[a_half("gu0")])])
    (o,), ((g_gu0,), (g_dn0,)) = _attn_fwd(q, kdup, vdup, sinks, jobs=[
        gather("gu0", g_gu0, sends=[b_half("gu0")], forwards=[a_half("gu0")]), gather("dn0", None, sends=[a_half("dn0")])])
    (m0, x1), ((g_gu0,), (g_dn0,)) = _proj_res(o, w_o, row(attn_b_o), row(norm_mix_post[0]), xs, "attn_out_fwd", jobs=[
        gather("gu0", g_gu0, forwards=[b_half("gu0")]),
        gather("dn0", g_dn0, sends=[b_half("dn0")], forwards=[a_half("dn0")])])
    w_gu0 = g_gu0
    (h1, gu0, a0), ((g_dn0,), (g_in,), (g_out,), (g_gu1,)) = _ffn_up_fwd(x1, row(norm_ffn_pre[0]), w_gu0, jobs=[
        gather("dn0", g_dn0, forwards=[b_half("dn0")]), gather("sgu_w_in", None, sends=[whole("sgu_w_in")]),
        gather("sgu_w_out", None, sends=[whole("sgu_w_out")]), gather("gu1", None, sends=[a_half("gu1")])])
    w_dn0 = _rows_full(g_dn0)
    (f0, x2), ((g_in,), (g_out,), (g_gu1,)) = _proj_res(a0, w_dn0, None, row(norm_ffn_post[0]), x1, "ffn_down_fwd0", jobs=[
        gather("sgu_w_in", g_in, forwards=[whole("sgu_w_in")]), gather("sgu_w_out", g_out, forwards=[whole("sgu_w_out")]),
        gather("gu1", g_gu1, sends=[b_half("gu1")], forwards=[a_half("gu1")])])
    w_in = g_in
    (h2, z), ((g_gu1,), (g_dn1,)) = _sgu_in_fwd(x2, row(norm_mix_pre[1]), w_in, jobs=[
        gather("gu1", g_gu1, forwards=[b_half("gu1")]), gather("dn1", None, sends=[a_half("dn1")])])
    (y,), ((g_dn1,),) = _sgu_mix_fwd(z, lg, lb, ws, bs_t, jobs=[
        gather("dn1", g_dn1, sends=[b_half("dn1")], forwards=[a_half("dn1")])])
    w_out = _rows_full(g_out)
    (m1, x3), ((g_dn1,),) = _proj_res(y, w_out, None, row(norm_mix_post[1]), x2, "sgu_out_fwd", jobs=[
        gather("dn1", g_dn1, forwards=[b_half("dn1")])])
    w_gu1 = g_gu1
    (h3, gu1, a1), _ = _ffn_up_fwd(x3, row(norm_ffn_pre[1]), w_gu1)
    w_dn1 = _rows_full(g_dn1)
    (f1, x4), _ = _proj_res(a1, w_dn1, None, row(norm_ffn_post[1]), x3, "ffn_down_fwd1")
    dx4, loss_tile = _loss_head(x4, target)

    to_sibling = lambda g: _scatter_job([g], 4, _to_sibling)
    pair = lambda g, r, name: _pair_sum(g.reshape((4, 2) + g.shape[1:]), r, core, "pair_sum_" + name)
    to_chips = lambda p, prev=None, piece=None: _scatter_job([p], 3, _to_owner_chip, prev=prev, piece=piece)
    out_g, out_d, out_m, out_v = {}, {}, {}, {}

    trees = [{**tree, "sgu_ln": jnp.stack([tree["sgu_ln_g"], tree["sgu_ln_b"]], axis=1)} for tree in (w, mom, var)]
    so_far = {}

    def update(name, own, index, received):
        leaf, layer = LAYER_OF.get(name, (name, 0))
        so_far[leaf] = _shard_update(own, index, received, *[tree[leaf] for tree in trees], layer, so_far.get(leaf), "update_" + name)
        for out, val in zip((out_g, out_d, out_m, out_v), so_far[leaf]):
            out[leaf] = val

    (df1, dgu1, dg_ffn_post1), _ = _post_bwd(dx4, f1, row(norm_ffn_post[1]), w_dn1, "ffn", "ffn_down_bwd1", gu1)
    dw_dn1, _ = _wgrad(a1, df1, "ffn_down_wgrad1", ACT)
    b_dn1 = _rows_blocked(dw_dn1)
    dw_gu1, ((r_dn1,),) = _wgrad(h3, dgu1, "ffn_up_wgrad1", ACT, jobs=[to_sibling(b_dn1)])
    b_gu1 = dw_gu1
    p_dn1 = pair(b_dn1, r_dn1, "dn1")
    (dx3, dg_ffn_pre1), ((q_dn1,), (r_gu1,)) = _pre_bwd(dgu1, w_gu1, x3, row(norm_ffn_pre[1]), dx4, "ffn_up_bwd1",
                                                         jobs=[to_chips(p_dn1), to_sibling(b_gu1)])
    update("dn1", p_dn1, chip, q_dn1)
    p_gu1 = pair(b_gu1, r_gu1, "gu1")
    (dm1, dy, dg_mix_post1), ((q_gu1,),) = _post_bwd(dx3, m1, row(norm_mix_post[1]), w_out, "sgu", "sgu_out_bwd",
                                                      jobs=[to_chips(p_gu1, piece=_halves(p_gu1[0])[0])])
    dw_out, _ = _wgrad(y, dm1, "sgu_out_wgrad", ACT)
    b_out = _rows_blocked(dw_out)
    (dz, dlg, dlb, dws, dbs_t), ((q_gu1,), (r_out,)) = _sgu_mix_bwd(z, dy, lg, lb, ws, bs_t, jobs=[
        to_chips(p_gu1, prev=[q_gu1], piece=_halves(p_gu1[0])[1]), to_sibling(b_out)])
    update("gu1", p_gu1, chip, q_gu1)
    p_out = pair(b_out, r_out, "sgu_w_out")
    b_in, ((q_out,),) = _wgrad(h2, dz, "sgu_in_wgrad", ACT, out_block=stage["sgu_w_in"].shape[1], jobs=[to_chips(p_out)])
    update("sgu_w_out", p_out, chip, q_out)
    b_ln = jnp.stack([dlg.reshape(N_DEV, -1), dlb.reshape(N_DEV, -1)], axis=1)
    (dx2, dg_mix_pre1), ((r_in,), (r_ln,)) = _pre_bwd(dz, w_in, x2, row(norm_mix_pre[1]), dx3, "sgu_in_bwd",
                                                      jobs=[to_sibling(b_in), to_sibling(b_ln)])
    p_in, p_ln = pair(b_in, r_in, "sgu_w_in"), pair(b_ln, r_ln, "sgu_ln")
    (df0, dgu0, dg_ffn_post0), ((q_in,), (q_ln,)) = _post_bwd(dx2, f0, row(norm_ffn_post[0]), w_dn0, "ffn", "ffn_down_bwd0", gu0,
                                                              jobs=[to_chips(p_in), to_chips(p_ln)])
    update("sgu_w_in", p_in, chip, q_in)
    update("sgu_ln", p_ln, chip, q_ln)
    dw_dn0, _ = _wgrad(a0, df0, "ffn_down_wgrad0", ACT)
    b_dn0 = _rows_blocked(dw_dn0)
    dw_gu0, ((r_dn0,),) = _wgrad(h1, dgu0, "ffn_up_wgrad0", ACT, jobs=[to_sibling(b_dn0)])
    b_gu0 = dw_gu0
    p_dn0 = pair(b_dn0, r_dn0, "dn0")
    (dx1, dg_ffn_pre0), ((q_dn0,), (r_gu0,)) = _pre_bwd(dgu0, w_gu0, x1, row(norm_ffn_pre[0]), dx2, "ffn_up_bwd0",
                                                         jobs=[to_chips(p_dn0), to_sibling(b_gu0)])
    update("dn0", p_dn0, chip, q_dn0)
    p_gu0 = pair(b_gu0, r_gu0, "gu0")
    (dm0, do, dg_mix_post0, db_o), ((q_gu0,),) = _post_bwd(dx1, m0, row(norm_mix_post[0]), w_o, "attn", "attn_out_bwd",
                                                           jobs=[to_chips(p_gu0, piece=_halves(p_gu0[0])[0])])
    dw_o, _ = _wgrad(o, dm0, "attn_out_wgrad", ACT)
    b_o = _rows_blocked(dw_o)
    (dqkv, db_qkv, dsink), ((q_gu0,), (r_o,)) = _attn_bwd(q, kdup, vdup, do, sinks, cos, sin, jobs=[
        to_chips(p_gu0, prev=[q_gu0], piece=_halves(p_gu0[0])[1]), to_sibling(b_o)])
    update("gu0", p_gu0, chip, q_gu0)
    p_o = pair(b_o, r_o, "attn_w_o")

    grads = {
        "norm_mix_post": jnp.concatenate([dg_mix_post0, dg_mix_post1], axis=0),
        "norm_ffn_pre": jnp.concatenate([dg_ffn_pre0, dg_ffn_pre1], axis=0),
        "norm_ffn_post": jnp.concatenate([dg_ffn_post0, dg_ffn_post1], axis=0),
        "attn_b_qkv": _fold_heads(db_qkv, d),
        "attn_sinks": dsink[:1, :d // HEAD_DIM],
        "attn_b_o": db_o,
        "sgu_w_spatial": dws * jnp.tril(jnp.ones((WINDOW, WINDOW), F32))[None],
        "sgu_b_spatial": dbs_t.T,
    }
    like = [w[name] for name in REPLICATED] + [loss_tile[:1, :1]]
    shares = _pack_rows([grads[name] for name in REPLICATED] + [loss_tile[:1, :1]], d)
    dw_qkv, ((q_o,), (g_shares,)) = _wgrad(h0, dqkv, "attn_qkv_wgrad", jobs=[
        to_chips(p_o), _gather_job(shares, None, sends=[_whole(shares)])])
    update("attn_w_o", p_o, chip, q_o)
    b_qkv_grad = _columns_blocked(_fold_heads(dw_qkv, d))
    (dx0, dg_mix_pre0), ((q_qkv,), (g_shares,)) = _pre_bwd(dqkv, w_qkv, xs, row(norm_mix_pre[0]), dx1, "attn_qkv_bwd", jobs=[
        _scatter_job([b_qkv_grad], N_DEV - 1, _to_owner), _gather_job(shares, g_shares, forwards=[_whole(shares)])])
    update("attn_w_qkv", b_qkv_grad, me, q_qkv)

    packed = [_pack_rows([tree[name] for name in REPLICATED] + [jnp.zeros((1, 1), F32)], d) for tree in (w, mom, var)]
    res = _replicated_update(g_shares, *packed)
    loss = _unpack_rows(res[0], like, d)[-1][0, 0]
    for out, val in zip((out_g, out_d, out_m, out_v), res):
        for name, leaf in zip(REPLICATED, _unpack_rows(val, like, d)):
            out[name] = leaf

    last_grads = {"norm_mix_pre": jnp.concatenate([dg_mix_pre0, dg_mix_pre1], axis=0)}
    last_like = [w[name] for name in LAST]
    _, (last_shares,) = _prepare([([_pack_rows([last_grads[name] for name in LAST], d)[None]], 0)], [F32], 1,
                                 "last_grads_all_gather")
    res = _replicated_update(last_shares, *[_pack_rows([tree[name] for name in LAST], d) for tree in (w, mom, var)])
    for out, val in zip((out_g, out_d, out_m, out_v), res):
        for name, leaf in zip(LAST, _unpack_rows(val, last_like, d)):
            out[name] = leaf

    for out in (out_g, out_d, out_m, out_v):
        out["sgu_ln_g"], out["sgu_ln_b"] = out["sgu_ln"][:, 0, :], out["sgu_ln"][:, 1, :]

    return (loss, dx0[None], *[out_g[n] for n in WEIGHTS], *[out_d[n] for n in WEIGHTS],
            *[out_m[n] for n in WEIGHTS], *[out_v[n] for n in WEIGHTS])
```

```python
import functools
import math
import operator

import jax
import jax.numpy as jnp
from jax import lax
from jax.experimental import pallas as pl
from jax.experimental.pallas import tpu as pltpu

F32 = jnp.float32
ACT = jnp.bfloat16

EPS = 1e-6
HEAD_DIM = 64
PAIR = 2 * HEAD_DIM
GQA_GROUP = 4
WINDOW = 128
ROPE_THETA = 10000.0
SCALE = HEAD_DIM ** -0.5
MASKED = -1e30
LANES = 128

ADAM_LR, ADAM_B1, ADAM_B2, ADAM_EPS, ADAM_WD, ADAM_STEP = 0.001, 0.9, 0.999, 1e-08, 0.01, 10

N_DEV = 8
VMEM_LIMIT_BYTES = 56 * 1024 * 1024
MESH = pl.DeviceIdType.MESH
ANY = pl.BlockSpec(memory_space=pl.ANY)
IN_VMEM = pl.BlockSpec(memory_space=pltpu.VMEM)


def _pick(n, candidates):
    for c in candidates:
        if n % c == 0:
            return c
    return n


def _row_tile(t):
    return _pick(t, (512,))


def _params(*sem):
    return pltpu.CompilerParams(dimension_semantics=sem, vmem_limit_bytes=VMEM_LIMIT_BYTES)


def _full(shape):
    return pl.BlockSpec(shape, lambda *_: (0,) * len(shape))


def _rows(tm, n):
    return pl.BlockSpec((tm, n), lambda i: (i, 0))


def _sds(shape, dtype):
    return jax.ShapeDtypeStruct(shape, dtype)


def _place():
    return lax.axis_index("x"), lax.axis_index("y"), lax.axis_index("c")


def _other_chips(x, y):
    return [(1 - x, y), (x, 1 - y), (1 - x, 1 - y)]


class _Job:
    def __init__(self, inputs, out_shape, aliases, emit, n_remote, n_local=0):
        self.inputs, self.out_shape, self.aliases, self.emit = list(inputs), list(out_shape), dict(aliases), emit
        self.n_remote, self.n_local = n_remote, n_local


def _call(body, name, grid, in_specs, out_specs, out_shape, args, sem, scratch=(), jobs=()):
    n_in, n_out, n_scr = len(args), len(out_shape), len(scratch)
    j_in = [a for job in jobs for a in job.inputs]
    j_out = [s for job in jobs for s in job.out_shape]
    aliases, at_in, at_out = {}, n_in, n_out
    for job in jobs:
        aliases.update({at_in + i: at_out + o for i, o in job.aliases.items()})
        at_in, at_out = at_in + len(job.inputs), at_out + len(job.out_shape)
    n_remote = sum(job.n_remote for job in jobs)
    n_local = sum(job.n_local for job in jobs)

    def wrapped(*refs):
        ins, jin = refs[:n_in], refs[n_in:n_in + len(j_in)]
        rest = refs[n_in + len(j_in):]
        outs, jout = rest[:n_out], rest[n_out:n_out + len(j_out)]
        scr = rest[n_out + len(j_out):n_out + len(j_out) + n_scr]
        if not jobs:
            body(*ins, *outs, *scr)
            return
        send, recv, local = rest[n_out + len(j_out) + n_scr:]
        ids = [pl.program_id(a) for a in range(len(grid))]
        first = functools.reduce(operator.and_, [i == 0 for i in ids])
        last = functools.reduce(operator.and_, [i == g - 1 for i, g in zip(ids, grid)])

        def descriptors():
            place, found, i, o, r, l = _place(), [], 0, 0, 0, 0
            for job in jobs:
                for src, dst, to in job.emit(jin[i:i + len(job.inputs)], jout[o:o + len(job.out_shape)], place):
                    if to is None:
                        found.append(pltpu.make_async_copy(src, dst, local.at[l]))
                        l += 1
                    else:
                        found.append(pltpu.make_async_remote_copy(src_ref=src, dst_ref=dst, send_sem=send.at[r], recv_sem=recv.at[r],
                                                                  device_id=to, device_id_type=MESH))
                        r += 1
                i, o = i + len(job.inputs), o + len(job.out_shape)
            return found

        @pl.when(first)
        def _():
            for cp in descriptors():
                cp.start()

        body(*ins, *outs, *scr)

        @pl.when(last)
        def _():
            for cp in descriptors():
                cp.wait()

    sems = [pltpu.SemaphoreType.DMA((n_remote,)), pltpu.SemaphoreType.DMA((n_remote,)),
            pltpu.SemaphoreType.DMA((max(n_local, 1),))] if jobs else []
    res = pl.pallas_call(
        wrapped, name=name, grid=grid,
        in_specs=list(in_specs) + [ANY] * len(j_in), out_specs=list(out_specs) + [ANY] * len(j_out),
        out_shape=list(out_shape) + j_out, scratch_shapes=list(scratch) + sems, input_output_aliases=aliases,
        compiler_params=_params(*(("arbitrary",) * len(grid) if jobs else sem)),
    )(*args, *j_in)
    job_res, at = [], n_out
    for job in jobs:
        job_res.append(list(res[at:at + len(job.out_shape)]))
        at += len(job.out_shape)
    return list(res[:n_out]), job_res


def _gather_job(stage, gathered, sends=(), forwards=()):
    shape = _sds((N_DEV,) + stage.shape, stage.dtype)
    inputs = ([stage] if sends else []) + ([gathered] if gathered is not None else [])
    aliases = {len(inputs) - 1: 0} if gathered is not None else {}

    def emit(ins, outs, place):
        x, y, c = place
        sibling, chips, mine, g = (x, y, 1 - c), _other_chips(x, y), 4 * x + 2 * y + c, outs[0]
        found = []
        for r0, nr in sends:
            src, dst = ins[0].at[pl.ds(r0, nr)], g.at[mine, pl.ds(r0, nr)]
            found += [(src, dst, None), (src, dst, sibling)] + [(src, dst, (px, py, c)) for px, py in chips]
        for r0, nr in forwards:
            for px, py in chips:
                block = 4 * px + 2 * py + c
                found.append((ins[-1].at[block, pl.ds(r0, nr)], g.at[block, pl.ds(r0, nr)], sibling))
        return found

    return _Job(inputs, [shape], aliases, emit, 4 * len(sends) + 3 * len(forwards), len(sends))


def _scatter_job(arrays, n_slots, route, prev=None, piece=None):
    shapes = [_sds((n_slots,) + v.shape[1:], v.dtype) for v in arrays]
    inputs = list(arrays) + (list(prev) if prev else [])
    aliases = {len(arrays) + i: i for i in range(len(arrays))} if prev else {}

    def emit(ins, outs, place):
        found = []
        for a in range(len(arrays)):
            for s in range(n_slots):
                block, to = route(s, *place)
                if piece is None:
                    found.append((ins[a].at[block], outs[a].at[s], to))
                else:
                    found.append((ins[a].at[block, pl.ds(*piece)], outs[a].at[s, pl.ds(*piece)], to))
        return found

    return _Job(inputs, shapes, aliases, emit, len(arrays) * n_slots)


def _to_sibling(s, x, y, c):
    return 2 * s + (1 - c), (x, y, 1 - c)


def _to_owner_chip(s, x, y, c):
    px, py = _other_chips(x, y)[s]
    return 2 * px + py, (px, py, c)


def _to_owner(s, x, y, c):
    if s == 0:
        px, py, pc = x, y, 1 - c
    else:
        px, py = _other_chips(x, y)[(s - 1) % 3]
        pc = c if s <= 3 else 1 - c
    return 4 * px + 2 * py + pc, (px, py, pc)


def _rstd(x):
    return lax.rsqrt(jnp.mean(x * x, axis=-1, keepdims=True) + EPS)


def _rms_bwd(xhat, r, g, dy):
    dxh = dy * g
    return r * (dxh - xhat * jnp.mean(dxh * xhat, axis=-1, keepdims=True))


def _first_half(rows):
    lane = lax.broadcasted_iota(jnp.int32, (rows, PAIR), 1)
    return (lane % HEAD_DIM) < (HEAD_DIM // 2)


def _rot_half(v, first):
    return jnp.where(first, pltpu.roll(v, PAIR - HEAD_DIM // 2, 1), pltpu.roll(v, HEAD_DIM // 2, 1))


def _rope(v, cos, sin, first):
    return v * cos + _rot_half(v, first) * sin


def _rope_t(dv, cos, sin, first):
    return dv * cos + _rot_half(dv * sin, first)


def _dot(a, b):
    return jnp.dot(a, b, preferred_element_type=F32)


def _dot_nt(a, b):
    return lax.dot_general(a, b, (((1,), (1,)), ((), ())), preferred_element_type=F32)


def _dot_tn(a, b):
    return lax.dot_general(a, b, (((0,), (0,)), ((), ())), preferred_element_type=F32)


def _sigmoid(v):
    return 1.0 / (1.0 + jnp.exp(-v))


_GELU_K = math.sqrt(2.0 / math.pi)
_GELU_C = 0.044715


def _gelu(v):
    return v * (0.5 * (1.0 + jnp.tanh(_GELU_K * (v + _GELU_C * (v * v * v)))))


def _gelu_grad(v):
    t = jnp.tanh(_GELU_K * (v + _GELU_C * (v * v * v)))
    return 0.5 * (1.0 + t) + 0.5 * v * (1.0 - t * t) * (_GELU_K * (1.0 + 3.0 * _GELU_C * (v * v)))


def _attn_qkv_fwd(x, g, w, b, cos, sin, jobs=()):
    t, d = x.shape
    n = w.shape[1]
    kd = (n - d) // 2
    tm = _row_tile(t)
    ch = _pick(d, (512,))

    def body(x_ref, g_ref, w_ref, b_ref, cos_ref, sin_ref, h_ref, q_ref, k_ref, v_ref):
        xv = x_ref[...]
        hb = (xv * _rstd(xv) * g_ref[...]).astype(ACT)
        h_ref[...] = hb
        cosv, sinv = cos_ref[...], sin_ref[...]
        first = _first_half(tm)

        def proj(lo, width):
            return _dot(hb, w_ref[:, lo:lo + width]) + b_ref[:, lo:lo + width]

        for c in range(0, d, ch):
            y = proj(c, ch)
            for s in range(0, ch, PAIR):
                q_ref[:, c + s:c + s + PAIR] = (_rope(y[:, s:s + PAIR], cosv, sinv, first) * SCALE).astype(ACT)
        y = proj(d, kd)
        for s in range(0, kd, PAIR):
            k_ref[:, s:s + PAIR] = _rope(y[:, s:s + PAIR], cosv, sinv, first).astype(ACT)
        v_ref[...] = proj(d + kd, kd).astype(ACT)

    return _call(
        body, "attn_qkv_fwd", (t // tm,),
        [_rows(tm, d), _full((1, d)), _full((d, n)), _full((1, n)), _rows(tm, PAIR), _rows(tm, PAIR)],
        [_rows(tm, d), _rows(tm, d), _rows(tm, kd), _rows(tm, kd)],
        [_sds((t, d), ACT), _sds((t, d), ACT), _sds((t, kd), ACT), _sds((t, kd), ACT)],
        (x, g, w, b, cos, sin), ("parallel",), jobs=jobs)


STACK = GQA_GROUP * WINDOW


def _band_mask(has_prev):
    row = lax.broadcasted_iota(jnp.int32, (STACK, 2 * WINDOW), 0) % WINDOW
    col = lax.broadcasted_iota(jnp.int32, (STACK, 2 * WINDOW), 1)
    return ((col < WINDOW) & (col > row) & has_prev) | ((col >= WINDOW) & (col - WINDOW <= row))


def _lane_halves():
    lane = lax.broadcasted_iota(jnp.int32, (1, PAIR), 1)
    return lane < HEAD_DIM, lane >= HEAD_DIM


def _stack_heads(ref, h, halves):
    parts = []
    for p in range(2):
        pair = ref[:, (2 * h + p) * PAIR:(2 * h + p + 1) * PAIR]
        parts += [jnp.where(half, pair, jnp.zeros_like(pair)) for half in halves]
    return jnp.concatenate(parts, axis=0)


def _sink_column(sink_ref, h):
    row = lax.broadcasted_iota(jnp.int32, (STACK, 1), 0)
    col = jnp.full((STACK, 1), sink_ref[0, GQA_GROUP * h + GQA_GROUP - 1], F32)
    for j in range(GQA_GROUP - 2, -1, -1):
        col = jnp.where(row < (j + 1) * WINDOW, sink_ref[0, GQA_GROUP * h + j], col)
    return col


def _probs(qs, k2, valid, sink):
    s = jnp.where(valid, _dot_nt(qs, k2), MASKED)
    m = jnp.maximum(jnp.max(s, axis=-1, keepdims=True), sink)
    p = jnp.exp(s - m)
    psink = jnp.exp(sink - m)
    inv = 1.0 / (jnp.sum(p, axis=-1, keepdims=True) + psink)
    return p * inv, psink * inv


def _attn_fwd(q, kd_, vd, sinks, jobs=()):
    t, d = q.shape
    kd = kd_.shape[1]
    cur = lambda n: (n, 0)
    prev = lambda n: (jnp.maximum(n - 1, 0), 0)

    def body(sink_ref, q_ref, kc_ref, kp_ref, vc_ref, vp_ref, o_ref):
        valid = _band_mask(pl.program_id(0) > 0)
        halves = _lane_halves()
        for h in range(kd // PAIR):
            hs = slice(h * PAIR, (h + 1) * PAIR)
            k2 = jnp.concatenate([kp_ref[:, hs], kc_ref[:, hs]], axis=0)
            v2 = jnp.concatenate([vp_ref[:, hs], vc_ref[:, hs]], axis=0)
            vs = jnp.concatenate([jnp.where(half, v2, jnp.zeros_like(v2)) for half in halves], axis=0)
            pr, _ = _probs(_stack_heads(q_ref, h, halves), k2, valid, _sink_column(sink_ref, h))
            pr = pr.astype(ACT)
            for p in range(2):
                both = jnp.concatenate([pr[2 * p * WINDOW:(2 * p + 1) * WINDOW], pr[(2 * p + 1) * WINDOW:(2 * p + 2) * WINDOW]], axis=1)
                o_ref[:, (2 * h + p) * PAIR:(2 * h + p + 1) * PAIR] = _dot(both, vs).astype(ACT)

    kv_c, kv_p = pl.BlockSpec((WINDOW, kd), cur), pl.BlockSpec((WINDOW, kd), prev)
    return _call(
        body, "attn_fwd", (t // WINDOW,),
        [pl.BlockSpec(memory_space=pltpu.SMEM), pl.BlockSpec((WINDOW, d), cur), kv_c, kv_p, kv_c, kv_p],
        [pl.BlockSpec((WINDOW, d), cur)], [_sds((t, d), ACT)],
        (sinks, q, kd_, kd_, vd, vd), ("parallel",), jobs=jobs)


def _attn_bwd(q, kd_, vd, do, sinks, cos, sin, jobs=()):
    t, d = q.shape
    kd = kd_.shape[1]
    nb = t // WINDOW
    n_out = d + 2 * kd
    cur = lambda n: (jnp.minimum(n, nb - 1), 0)
    prev = lambda n: (jnp.maximum(jnp.minimum(n, nb - 1) - 1, 0), 0)
    late = lambda n: (jnp.maximum(n - 1, 0), 0)

    def body(sink_ref, q_ref, kc_ref, kp_ref, vc_ref, vp_ref, do_ref, cosc_ref, sinc_ref, cosp_ref, sinp_ref,
             out_ref, db_ref, dsink_ref, dq_keep, dk_keep, dv_keep):
        n = pl.program_id(0)

        @pl.when(n == 0)
        def _():
            for ref in (db_ref, dsink_ref, dq_keep, dk_keep, dv_keep):
                ref[...] = jnp.zeros_like(ref)

        valid = _band_mask(n > 0) & (n < nb)
        halves = _lane_halves()
        first = _first_half(WINDOW)
        slot = lax.broadcasted_iota(jnp.int32, dsink_ref.shape, 1)
        top = lax.broadcasted_iota(jnp.int32, dsink_ref.shape, 0) == 0
        dsink = jnp.zeros(dsink_ref.shape, F32)

        def emit(cols, done):
            out_ref[:, cols] = done.astype(ACT)
            db_ref[:, cols] += jnp.sum(done.astype(F32), axis=0, keepdims=True)

        for h in range(kd // PAIR):
            hs = slice(h * PAIR, (h + 1) * PAIR)
            k2 = jnp.concatenate([kp_ref[:, hs], kc_ref[:, hs]], axis=0)
            v2 = jnp.concatenate([vp_ref[:, hs], vc_ref[:, hs]], axis=0)
            qs, dos = _stack_heads(q_ref, h, halves), _stack_heads(do_ref, h, halves)
            pr, psink = _probs(qs, k2, valid, _sink_column(sink_ref, h))
            dp = _dot_nt(dos, v2)
            delta = jnp.sum(pr * dp, axis=-1, keepdims=True)
            ds = (pr * (dp - delta)).astype(ACT)
            leak = psink * delta
            for j in range(GQA_GROUP):
                share = jnp.sum(leak[j * WINDOW:(j + 1) * WINDOW], axis=0, keepdims=True)
                dsink = dsink - jnp.where(top & (slot == GQA_GROUP * h + j), share, 0.0)
            dqs = _dot(ds, k2)
            for p in range(2):
                ps = slice((2 * h + p) * PAIR, (2 * h + p + 1) * PAIR)
                dqp = jnp.where(halves[0], dqs[2 * p * WINDOW:(2 * p + 1) * WINDOW], dqs[(2 * p + 1) * WINDOW:(2 * p + 2) * WINDOW])
                emit(ps, dq_keep[:, ps])
                dq_keep[:, ps] = (_rope_t(dqp, cosc_ref[...], sinc_ref[...], first) * SCALE).astype(ACT)
            dk2 = _dot_tn(ds, qs)
            dv2 = _dot_tn(pr.astype(ACT), dos)
            ks = slice(d + h * PAIR, d + (h + 1) * PAIR)
            vs = slice(d + kd + h * PAIR, d + kd + (h + 1) * PAIR)
            emit(ks, dk_keep[:, hs] + _rope_t(dk2[:WINDOW], cosp_ref[...], sinp_ref[...], first))
            dk_keep[:, hs] = _rope_t(dk2[WINDOW:], cosc_ref[...], sinc_ref[...], first)
            emit(vs, dv_keep[:, hs] + dv2[:WINDOW])
            dv_keep[:, hs] = dv2[WINDOW:]
        dsink_ref[...] += dsink

    kv_c, kv_p = pl.BlockSpec((WINDOW, kd), cur), pl.BlockSpec((WINDOW, kd), prev)
    tab_c, tab_p = pl.BlockSpec((WINDOW, PAIR), cur), pl.BlockSpec((WINDOW, PAIR), prev)
    return _call(
        body, "attn_bwd", (nb + 1,),
        [pl.BlockSpec(memory_space=pltpu.SMEM), pl.BlockSpec((WINDOW, d), cur), kv_c, kv_p, kv_c, kv_p,
         pl.BlockSpec((WINDOW, d), cur), tab_c, tab_c, tab_p, tab_p],
        [pl.BlockSpec((WINDOW, n_out), late), _full((1, n_out)), _full((8, LANES))],
        [_sds((t, n_out), ACT), _sds((1, n_out), F32), _sds((8, LANES), F32)],
        (sinks, q, kd_, kd_, vd, vd, do, cos, sin, cos, sin), ("arbitrary",),
        scratch=[pltpu.VMEM((WINDOW, d), ACT), pltpu.VMEM((WINDOW, kd), F32), pltpu.VMEM((WINDOW, kd), F32)], jobs=jobs)


def _proj_res(a, w, b, g, x, name, jobs=()):
    blocked = a.ndim == 3
    t = a.shape[-2]
    k, d = w.shape
    tm = _row_tile(t)
    has_bias = b is not None

    def body(*refs):
        a_ref, w_ref = refs[:2]
        b_ref = refs[2] if has_bias else None
        g_ref, x_ref, m_ref, xo_ref = refs[2 + has_bias:]
        if blocked:
            c = a.shape[2]
            m = _dot(a_ref[0], w_ref[:c, :])
            for j in range(1, a.shape[0]):
                m = m + _dot(a_ref[j], w_ref[j * c:(j + 1) * c, :])
        else:
            m = _dot(a_ref[...], w_ref[...])
        if has_bias:
            m = m + b_ref[...]
        m_ref[...] = m
        xo_ref[...] = x_ref[...] + (m * _rstd(m)) * g_ref[...]

    ins = [a, w] + ([b] if has_bias else []) + [g, x]
    a_spec = pl.BlockSpec((a.shape[0], tm, a.shape[2]), lambda i: (0, i, 0)) if blocked else _rows(tm, k)
    specs = [a_spec, _full((k, d))] + ([_full((1, d))] if has_bias else []) + [_full((1, d)), _rows(tm, d)]
    return _call(body, name, (t // tm,), specs, [_rows(tm, d), _rows(tm, d)], [_sds((t, d), F32)] * 2, ins, ("parallel",), jobs=jobs)


def _post_bwd(dres, m, g, w, mode, name, gu=None, jobs=()):
    t, d = dres.shape
    k = w.shape[0]
    tm = _row_tile(t)
    kc = _pick(k, (1408, 1024, 512))

    def body(*refs):
        d_ref, m_ref, g_ref, w_ref = refs[:4]
        gu_ref = refs[4] if mode == "ffn" else None
        outs = refs[4 + (mode == "ffn"):]
        dm_ref, da_ref, dg_ref = outs[:3]
        i = pl.program_id(0)

        @pl.when(i == 0)
        def _():
            dg_ref[...] = jnp.zeros_like(dg_ref)
            if mode == "attn":
                outs[3][...] = jnp.zeros_like(outs[3])

        dv, mv = d_ref[...], m_ref[...]
        r = _rstd(mv)
        mh = mv * r
        dg_ref[...] += jnp.sum(dv * mh, axis=0, keepdims=True)
        dm = _rms_bwd(mh, r, g_ref[...], dv)
        dmb = dm.astype(ACT)
        dm_ref[...] = dmb
        if mode == "attn":
            outs[3][...] += jnp.sum(dm, axis=0, keepdims=True)
        if mode == "ffn":
            nb, _, cb = gu.shape
            for j in range(nb // 2):
                da = _dot_nt(dmb, w_ref[j * cb:(j + 1) * cb, :])
                gate = gu_ref[j].astype(F32)
                up = gu_ref[nb // 2 + j].astype(F32)
                sg = _sigmoid(gate)
                da_ref[j] = (da * up * (sg * (1.0 + gate * (1.0 - sg)))).astype(ACT)
                da_ref[nb // 2 + j] = (da * (gate * sg)).astype(ACT)
        else:
            for c in range(0, k, kc):
                da = _dot_nt(dmb, w_ref[c:c + kc, :])
                da_ref[:, c:c + kc] = da.astype(ACT) if mode == "attn" else da

    ins = [dres, m, g, w]
    specs = [_rows(tm, d), _rows(tm, d), _full((1, d)), _full((k, d))]
    if mode == "ffn":
        slab = pl.BlockSpec((gu.shape[0], tm, gu.shape[2]), lambda i: (0, i, 0))
        ins.append(gu)
        specs.append(slab)
        out_specs = [_rows(tm, d), slab, _full((1, d))]
        out_shape = [_sds((t, d), ACT), _sds(gu.shape, ACT), _sds((1, d), F32)]
    else:
        out_specs = [_rows(tm, d), _rows(tm, k), _full((1, d))]
        out_shape = [_sds((t, d), ACT), _sds((t, k), ACT if mode == "attn" else F32), _sds((1, d), F32)]
    if mode == "attn":
        out_specs.append(_full((1, d)))
        out_shape.append(_sds((1, d), F32))
    return _call(body, name, (t // tm,), specs, out_specs, out_shape, ins, ("arbitrary",), jobs=jobs)


def _pre_bwd(dy, w, x, g, dres, name, jobs=()):
    t, d = x.shape
    tm = _row_tile(t)

    def body(dy_ref, w_ref, x_ref, g_ref, dres_ref, dx_ref, dg_ref):
        @pl.when(pl.program_id(0) == 0)
        def _():
            dg_ref[...] = jnp.zeros_like(dg_ref)

        dh = jnp.zeros((tm, d), F32)
        if w.ndim == 3:
            c = w.shape[2]
            for j in range(w.shape[0]):
                dh = dh + _dot_nt(dy_ref[j] if dy.ndim == 3 else dy_ref[:, j * c:(j + 1) * c], w_ref[j])
        else:
            n = w.shape[1]
            nc = _pick(n, (1408, 1024, 512))
            for c in range(0, n, nc):
                dh = dh + _dot_nt(dy_ref[:, c:c + nc], w_ref[:, c:c + nc])
        xv = x_ref[...]
        r = _rstd(xv)
        xh = xv * r
        dg_ref[...] += jnp.sum(dh * xh, axis=0, keepdims=True)
        dx_ref[...] = dres_ref[...] + _rms_bwd(xh, r, g_ref[...], dh)

    dy_spec = pl.BlockSpec((dy.shape[0], tm, dy.shape[2]), lambda i: (0, i, 0)) if dy.ndim == 3 else _rows(tm, dy.shape[1])
    return _call(
        body, name, (t // tm,),
        [dy_spec, _full(w.shape), _rows(tm, d), _full((1, d)), _rows(tm, d)],
        [_rows(tm, d), _full((1, d))], [_sds((t, d), F32), _sds((1, d), F32)],
        (dy, w, x, g, dres), ("arbitrary",), jobs=jobs)


def _wgrad(a, b, name, out_dtype=F32, out_block=None, jobs=()):
    t = a.shape[-2]
    tt = _pick(t, (1024,))
    steps = t // tt
    if a.ndim == 3:
        k_steps, _, tk = a.shape
        a_spec = pl.BlockSpec((None, tt, tk), lambda i, j, s: (i, s, 0))
    else:
        tk = _pick(a.shape[1], (1024, 1408))
        k_steps = a.shape[1] // tk
        a_spec = pl.BlockSpec((tt, tk), lambda i, j, s: (s, i))
    k = k_steps * tk
    if b.ndim == 3:
        n_steps, _, tn = b.shape
        per = 1
        b_spec = pl.BlockSpec((None, tt, tn), lambda i, j, s: (j, s, 0))
        out_spec, out_shape = pl.BlockSpec((None, tk, tn), lambda i, j, s: (j, i, 0)), _sds((n_steps, k, tn), out_dtype)
    else:
        n = b.shape[1]
        tn = _pick(n, (1024, 1408))
        n_steps = n // tn
        b_spec = pl.BlockSpec((tt, tn), lambda i, j, s: (s, j))
        if out_block is None:
            per = 0
            out_spec, out_shape = pl.BlockSpec((tk, tn), lambda i, j, s: (i, j)), _sds((k, n), out_dtype)
        else:
            per = tn // out_block
            out_spec = pl.BlockSpec((per, tk, out_block), lambda i, j, s: (j, i, 0))
            out_shape = _sds((n // out_block, k, out_block), out_dtype)

    def body(a_ref, b_ref, o_ref, acc_ref):
        s = pl.program_id(2)

        @pl.when(s == 0)
        def _():
            acc_ref[...] = jnp.zeros_like(acc_ref)

        acc_ref[...] += _dot_tn(a_ref[...], b_ref[...])

        @pl.when(s == steps - 1)
        def _():
            if per > 1 or (per == 1 and b.ndim == 2):
                for p in range(per):
                    o_ref[p] = acc_ref[:, p * out_block:(p + 1) * out_block].astype(out_dtype)
            else:
                o_ref[...] = acc_ref[...].astype(out_dtype)

    res, job_res = _call(
        body, name, (k_steps, n_steps, steps), [a_spec, b_spec], [out_spec], [out_shape],
        (a, b), ("parallel", "parallel", "arbitrary"), scratch=[pltpu.VMEM((tk, tn), F32)], jobs=jobs)
    return res[0], job_res


def _ffn_up_fwd(x, g, w, jobs=()):
    t, d = x.shape
    nb, _, c = w.shape
    tm = _row_tile(t)

    def body(x_ref, g_ref, w_ref, h_ref, gu_ref, a_ref):
        xv = x_ref[...]
        hb = (xv * _rstd(xv) * g_ref[...]).astype(ACT)
        h_ref[...] = hb
        for j in range(nb // 2):
            gate = _dot(hb, w_ref[j])
            up = _dot(hb, w_ref[nb // 2 + j])
            gu_ref[j] = gate.astype(ACT)
            gu_ref[nb // 2 + j] = up.astype(ACT)
            a_ref[j] = (gate * _sigmoid(gate) * up).astype(ACT)

    slab = lambda n: pl.BlockSpec((n, tm, c), lambda i: (0, i, 0))
    return _call(
        body, "ffn_up_fwd", (t // tm,),
        [_rows(tm, d), _full((1, d)), _full((nb, d, c))],
        [_rows(tm, d), slab(nb), slab(nb // 2)],
        [_sds((t, d), ACT), _sds((nb, t, c), ACT), _sds((nb // 2, t, c), ACT)],
        (x, g, w), ("parallel",), jobs=jobs)


def _sgu_in_fwd(x, g, w, jobs=()):
    t, d = x.shape
    nb, _, c = w.shape
    n = nb * c
    tm = _row_tile(t)

    def body(x_ref, g_ref, w_ref, h_ref, z_ref):
        xv = x_ref[...]
        hb = (xv * _rstd(xv) * g_ref[...]).astype(ACT)
        h_ref[...] = hb
        for j in range(nb):
            z_ref[:, j * c:(j + 1) * c] = _dot(hb, w_ref[j])

    return _call(
        body, "sgu_in_fwd", (t // tm,), [_rows(tm, d), _full((1, d)), _full((nb, d, c))],
        [_rows(tm, d), _rows(tm, n)], [_sds((t, d), ACT), _sds((t, n), F32)], (x, g, w), ("parallel",), jobs=jobs)


def _causal(ws):
    row = lax.broadcasted_iota(jnp.int32, ws.shape, 0)
    col = lax.broadcasted_iota(jnp.int32, ws.shape, 1)
    return jnp.where(col <= row, ws, 0.0)


def _sgu_ln(v, lg, lb):
    mu = jnp.mean(v, axis=-1, keepdims=True)
    vc = v - mu
    rs = lax.rsqrt(jnp.mean(vc * vc, axis=-1, keepdims=True) + EPS)
    vh = vc * rs
    return vh, rs, vh * lg + lb


def _sgu_mix_fwd(z, lg, lb, ws, bs_t, jobs=()):
    t, n = z.shape
    d = n // 2
    groups = d // WINDOW
    tm = _row_tile(t)

    def body(z_ref, lg_ref, lb_ref, ws_ref, bs_ref, y_ref):
        for c in range(0, tm, WINDOW):
            u = _gelu(z_ref[c:c + WINDOW, :d])
            _, _, vn = _sgu_ln(_gelu(z_ref[c:c + WINDOW, d:]), lg_ref[...], lb_ref[...])
            for gi in range(groups):
                gs = slice(gi * WINDOW, (gi + 1) * WINDOW)
                mixed = _dot(_causal(ws_ref[gi]).astype(ACT), vn[:, gs].astype(ACT)) + bs_ref[:, gi:gi + 1]
                y_ref[c:c + WINDOW, gs] = (u[:, gs] * mixed).astype(ACT)

    return _call(
        body, "sgu_mix_fwd", (t // tm,),
        [_rows(tm, n), _full((1, d)), _full((1, d)), _full((groups, WINDOW, WINDOW)), _full((WINDOW, groups))],
        [_rows(tm, d)], [_sds((t, d), ACT)], (z, lg, lb, ws, bs_t), ("parallel",), jobs=jobs)


def _sgu_mix_bwd(z, dy, lg, lb, ws, bs_t, jobs=()):
    t, n = z.shape
    d = n // 2
    groups = d // WINDOW
    tm = _row_tile(t)

    def body(z_ref, dy_ref, lg_ref, lb_ref, ws_ref, bs_ref, dz_ref, dlg_ref, dlb_ref, dws_ref, dbs_ref):
        @pl.when(pl.program_id(0) == 0)
        def _():
            for ref in (dlg_ref, dlb_ref, dws_ref, dbs_ref):
                ref[...] = jnp.zeros_like(ref)

        lane = lax.broadcasted_iota(jnp.int32, (WINDOW, groups), 1)
        for c in range(0, tm, WINDOW):
            rows = slice(c, c + WINDOW)
            zu, zv = z_ref[rows, :d], z_ref[rows, d:]
            u = _gelu(zu)
            vh, rs, vn = _sgu_ln(_gelu(zv), lg_ref[...], lb_ref[...])
            dyv = dy_ref[rows, :]
            dvn_parts = []
            for gi in range(groups):
                gs = slice(gi * WINDOW, (gi + 1) * WINDOW)
                wsg = _causal(ws_ref[gi]).astype(ACT)
                vng = vn[:, gs].astype(ACT)
                mixed = _dot(wsg, vng) + bs_ref[:, gi:gi + 1]
                dz_ref[rows, gs] = (dyv[:, gs] * mixed * _gelu_grad(zu[:, gs])).astype(ACT)
                dmix = dyv[:, gs] * u[:, gs]
                dmb = dmix.astype(ACT)
                dvn_parts.append(_dot_tn(wsg, dmb))
                dws_ref[gi] += _dot_nt(dmb, vng)
                dbs_ref[...] += jnp.where(lane == gi, jnp.sum(dmix, axis=-1, keepdims=True), 0.0)
            dvn = jnp.concatenate(dvn_parts, axis=-1)
            dlg_ref[...] += jnp.sum(dvn * vh, axis=0, keepdims=True)
            dlb_ref[...] += jnp.sum(dvn, axis=0, keepdims=True)
            dvh = dvn * lg_ref[...]
            dv = rs * (dvh - jnp.mean(dvh, axis=-1, keepdims=True) - vh * jnp.mean(dvh * vh, axis=-1, keepdims=True))
            dz_ref[rows, d:] = (dv * _gelu_grad(zv)).astype(ACT)

    vec = _full((1, d))
    return _call(
        body, "sgu_mix_bwd", (t // tm,),
        [_rows(tm, n), _rows(tm, d), vec, vec, _full((groups, WINDOW, WINDOW)), _full((WINDOW, groups))],
        [_rows(tm, n), vec, vec, _full((groups, WINDOW, WINDOW)), _full((WINDOW, groups))],
        [_sds((t, n), ACT), _sds((1, d), F32), _sds((1, d), F32), _sds((groups, WINDOW, WINDOW), F32), _sds((WINDOW, groups), F32)],
        (z, dy, lg, lb, ws, bs_t), ("arbitrary",), jobs=jobs)


def _loss_head(y, target):
    t, d = y.shape
    tm = _row_tile(t)

    def body(y_ref, t_ref, dy_ref, loss_ref):
        @pl.when(pl.program_id(0) == 0)
        def _():
            loss_ref[...] = jnp.zeros_like(loss_ref)

        err = y_ref[...] - t_ref[...]
        dy_ref[...] = err * (1.0 / d)
        per_token = jnp.mean(err * err, axis=-1, keepdims=True)
        loss_ref[...] += 0.5 * jnp.sum(per_token, axis=0, keepdims=True)

    res, _ = _call(body, "loss_head", (t // tm,), [_rows(tm, d), _rows(tm, d)], [_rows(tm, d), _full((8, LANES))],
                   [_sds((t, d), F32), _sds((8, LANES), F32)], (y, target), ("arbitrary",))
    return res


AG_COPIES = 7


def _prepare(sources, dtypes, n_gather, name):
    n = len(sources)
    arrays, where = [], []
    for parts, layer in sources:
        found = []
        for part in parts:
            known = [i for i, v in enumerate(arrays) if v is part]
            if not known:
                arrays.append(part)
            found.append(known[0] if known else len(arrays) - 1)
        where.append((found, layer))
    shapes = [(sum(p.shape[1] for p in parts), parts[0].shape[2]) for parts, _ in sources]

    def body(*refs):
        ins, refs = refs[:len(arrays)], refs[len(arrays):]
        stage, outs = refs[:n], refs[n:n + n_gather]
        send, recv, local_sem = refs[n + n_gather:]
        x, y, c = _place()
        sibling = (x, y, 1 - c)
        chips = _other_chips(x, y)
        mine = 4 * x + 2 * y + c

        def copy(a, k, block, to, src=None):
            dst = outs[a].at[block]
            return pltpu.make_async_remote_copy(
                src_ref=dst if src is None else src, dst_ref=dst, send_sem=send.at[a * AG_COPIES + k],
                recv_sem=recv.at[a * AG_COPIES + k], device_id=to, device_id_type=MESH)

        for a, (found, layer) in enumerate(where):
            at = 0
            for i in found:
                rows = arrays[i].shape[1]
                stage[a][at:at + rows, :] = ins[i][layer].astype(dtypes[a])
                at += rows
        started = []
        for a in range(n_gather):
            own = pltpu.make_async_copy(stage[a], outs[a].at[mine], local_sem.at[a])
            own.start()
            started.append(own)
        sends = []
        for a in range(n_gather):
            sends.append(copy(a, 0, mine, sibling, src=stage[a]))
            sends += [copy(a, 1 + j, mine, (*chip, c), src=stage[a]) for j, chip in enumerate(chips)]
        for cp in sends:
            cp.start()
        for j, (px, py) in enumerate(chips):
            block = 4 * px + 2 * py + c
            for a in range(n_gather):
                copy(a, 1 + j, block, (x, y, c)).wait_recv()
                forward = copy(a, 4 + j, block, sibling)
                forward.start()
                sends.append(forward)
        for a in range(n_gather):
            copy(a, 0, 4 * x + 2 * y + (1 - c), (x, y, c)).wait_recv()
            for j, (px, py) in enumerate(chips):
                copy(a, 4 + j, 4 * px + 2 * py + (1 - c), (x, y, c)).wait_recv()
        for cp in sends:
            cp.wait_send()
        for own in started:
            own.wait()

    res = pl.pallas_call(
        body, name=name,
        in_specs=[IN_VMEM] * len(arrays), out_specs=[IN_VMEM] * n + [ANY] * n_gather,
        out_shape=[_sds(s, dt) for s, dt in zip(shapes, dtypes)]
        + [_sds((N_DEV,) + s, dt) for s, dt in zip(shapes[:n_gather], dtypes)],
        scratch_shapes=[pltpu.SemaphoreType.DMA((n_gather * AG_COPIES,)), pltpu.SemaphoreType.DMA((n_gather * AG_COPIES,)),
                        pltpu.SemaphoreType.DMA((n_gather,))],
        compiler_params=pltpu.CompilerParams(vmem_limit_bytes=VMEM_LIMIT_BYTES),
    )(*arrays)
    return list(res[:n]), list(res[n:])


def _pair_sum(g, r, core, name):
    _, _, rows, cols = g.shape
    tr = _pick(rows, (512, 256, 128))

    def body(core_ref, g_ref, r_ref, p_ref):
        del core_ref
        p_ref[...] = (g_ref[...].astype(F32) + r_ref[...].astype(F32)).astype(p_ref.dtype)

    return pl.pallas_call(
        body, name=name,
        grid_spec=pltpu.PrefetchScalarGridSpec(
            num_scalar_prefetch=1, grid=(4, rows // tr),
            in_specs=[pl.BlockSpec((None, None, tr, cols), lambda q, i, core_ref: (q, core_ref[0], i, 0)),
                      pl.BlockSpec((None, tr, cols), lambda q, i, core_ref: (q, i, 0))],
            out_specs=pl.BlockSpec((None, tr, cols), lambda q, i, core_ref: (q, i, 0))),
        out_shape=_sds((4, rows, cols), g.dtype),
        compiler_params=_params("parallel", "parallel"),
    )(core, g, r)


def _adam(w, g, m, v):
    m2 = ADAM_B1 * m + (1.0 - ADAM_B1) * g
    v2 = ADAM_B2 * v + (1.0 - ADAM_B2) * (g * g)
    m_hat = m2 / (1.0 - ADAM_B1 ** ADAM_STEP)
    v_hat = v2 / (1.0 - ADAM_B2 ** ADAM_STEP)
    return -ADAM_LR * (m_hat / (jnp.sqrt(v_hat) + ADAM_EPS) + ADAM_WD * w), m2, v2


def _shard_update(own, index, received, w, m, v, layer, so_far, name):
    _, rows, cols = w.shape
    n_recv = received.shape[0]
    tr = _pick(rows, (512, 256, 128))

    def body(index_ref, p_ref, q_ref, w_ref, m_ref, v_ref, *rest):
        del index_ref
        g_out, d_out, m_out, v_out = rest[-4:]
        g = p_ref[...].astype(F32)
        for s in range(n_recv):
            g = g + q_ref[s].astype(F32)
        g_out[...] = g
        d_out[...], m_out[...], v_out[...] = _adam(w_ref[...], g, m_ref[...], v_ref[...])

    tile = pl.BlockSpec((None, tr, cols), lambda i, index_ref: (layer, i, 0))
    kept = list(so_far) if so_far is not None else []
    return pl.pallas_call(
        body, name=name,
        grid_spec=pltpu.PrefetchScalarGridSpec(
            num_scalar_prefetch=1, grid=(rows // tr,),
            in_specs=[pl.BlockSpec((None, tr, cols), lambda i, index_ref: (index_ref[0], i, 0)),
                      pl.BlockSpec((n_recv, tr, cols), lambda i, index_ref: (0, i, 0)), tile, tile, tile] + [ANY] * len(kept),
            out_specs=[tile] * 4),
        out_shape=[_sds(w.shape, F32)] * 4,
        input_output_aliases={6 + i: i for i in range(len(kept))},
        compiler_params=_params("parallel"),
    )(index, own, received, w, m, v, *kept)


def _replicated_update(shares, w, m, v):
    rows, cols = w.shape

    def body(s_ref, w_ref, m_ref, v_ref, g_out, d_out, m_out, v_out):
        g = s_ref[0]
        for dev in range(1, N_DEV):
            g = g + s_ref[dev]
        g_out[...] = g
        d_out[...], m_out[...], v_out[...] = _adam(w_ref[...], g, m_ref[...], v_ref[...])

    return pl.pallas_call(
        body, name="replicated_update_%d" % rows, out_shape=[_sds((rows, cols), F32)] * 4,
        compiler_params=pltpu.CompilerParams(vmem_limit_bytes=VMEM_LIMIT_BYTES),
    )(shares, w, m, v)


def _rope_tables(t):
    half = HEAD_DIM // 2
    inv_freq = ROPE_THETA ** (-(jnp.arange(half, dtype=F32) * 2.0) / HEAD_DIM)
    ang = jnp.arange(t, dtype=jnp.int32).astype(F32)[:, None] * inv_freq[None, :]
    cos, sin = jnp.cos(ang), jnp.sin(ang)
    return jnp.tile(jnp.concatenate([cos, cos], axis=1), (1, 2)), jnp.tile(jnp.concatenate([-sin, sin], axis=1), (1, 2))


def _dup_heads(w, d):
    kv = (w.shape[-1] - d) // 2
    lead = w.shape[:-1]

    def dup(part):
        heads = part.reshape(lead + (kv // HEAD_DIM, 1, HEAD_DIM))
        return jnp.broadcast_to(heads, lead + (kv // HEAD_DIM, 2, HEAD_DIM)).reshape(lead + (2 * kv,))

    return jnp.concatenate([w[..., :d], dup(w[..., d:d + kv]), dup(w[..., d + kv:])], axis=-1)


def _fold_heads(dw, d):
    kv = (dw.shape[-1] - d) // 4
    lead = dw.shape[:-1]

    def fold(part):
        return part.reshape(lead + (kv // HEAD_DIM, 2, HEAD_DIM)).sum(axis=-2).reshape(lead + (kv,))

    return jnp.concatenate([dw[..., :d], fold(dw[..., d:d + 2 * kv]), fold(dw[..., d + 2 * kv:])], axis=-1)


def _columns_full(gathered):
    _, rows, c = gathered.shape
    return jnp.transpose(gathered, (1, 0, 2)).reshape(rows, N_DEV * c)


def _rows_full(gathered):
    return gathered.reshape(-1, gathered.shape[-1])


def _columns_blocked(full):
    rows, cols = full.shape
    return jnp.transpose(full.reshape(rows, N_DEV, cols // N_DEV), (1, 0, 2)).astype(ACT)


def _rows_blocked(full):
    return full.reshape(N_DEV, full.shape[0] // N_DEV, full.shape[1]).astype(ACT)


def _packed_rows(size, width):
    return -(-size // (8 * width)) * 8


def _pack_rows(parts, width):
    rows = []
    for v in parts:
        flat = v.reshape(-1).astype(F32)
        n_rows = _packed_rows(flat.shape[0], width)
        rows.append(jnp.pad(flat, (0, n_rows * width - flat.shape[0])).reshape(n_rows, width))
    return jnp.concatenate(rows, axis=0)


def _unpack_rows(packed, like, width):
    out, at = [], 0
    for v in like:
        size = math.prod(v.shape)
        out.append(packed[at:at + -(-size // width)].reshape(-1)[:size].reshape(v.shape))
        at += _packed_rows(size, width)
    return out


def _halves(block):
    half = block.shape[0] // 2
    return (0, half), (half, half)


def _whole(block):
    return (0, block.shape[0])


EARLY = ("attn_w_qkv", "sgu_ln", "attn_w_o")
LATE = ("sgu_w_in", "sgu_w_out", "gu0", "gu1", "dn0", "dn1")
COLUMN_SHARDED = ("attn_w_qkv", "sgu_w_in", "gu0", "gu1")
BEFORE_ATTN = ("norm_mix_post", "norm_ffn_pre", "norm_ffn_post", "attn_b_o", "sgu_w_spatial", "sgu_b_spatial")
AFTER_ATTN = ("attn_b_qkv", "attn_sinks")
LAST = ("norm_mix_pre",)
WEIGHTS = ("norm_mix_pre", "norm_mix_post", "norm_ffn_pre", "norm_ffn_post", "attn_w_qkv", "attn_b_qkv", "attn_sinks", "attn_w_o",
           "attn_b_o", "sgu_w_in", "sgu_ln_g", "sgu_ln_b", "sgu_w_spatial", "sgu_b_spatial", "sgu_w_out", "ffn_w_gate_up", "ffn_w_down")


LAYER_OF = {"gu0": ("ffn_w_gate_up", 0), "gu1": ("ffn_w_gate_up", 1), "dn0": ("ffn_w_down", 0), "dn1": ("ffn_w_down", 1)}


def _source(tree, name):
    if name == "sgu_ln":
        return [tree["sgu_ln_g"][None], tree["sgu_ln_b"][None]], 0
    leaf, layer = LAYER_OF.get(name, (name, 0))
    return [tree[leaf]], layer


def kernel(x, norm_mix_pre, norm_mix_post, norm_ffn_pre, norm_ffn_post, attn_w_qkv, attn_b_qkv, attn_sinks, attn_w_o, attn_b_o, sgu_w_in, sgu_ln_g, sgu_ln_b, sgu_w_spatial, sgu_b_spatial, sgu_w_out, ffn_w_gate_up, ffn_w_down, loss_target, m_norm_mix_pre, m_norm_mix_post, m_norm_ffn_pre, m_norm_ffn_post, m_attn_w_qkv, m_attn_b_qkv, m_attn_sinks, m_attn_w_o, m_attn_b_o, m_sgu_w_in, m_sgu_ln_g, m_sgu_ln_b, m_sgu_w_spatial, m_sgu_b_spatial, m_sgu_w_out, m_ffn_w_gate_up, m_ffn_w_down, v_norm_mix_pre, v_norm_mix_post, v_norm_ffn_pre, v_norm_ffn_post, v_attn_w_qkv, v_attn_b_qkv, v_attn_sinks, v_attn_w_o, v_attn_b_o, v_sgu_w_in, v_sgu_ln_g, v_sgu_ln_b, v_sgu_w_spatial, v_sgu_b_spatial, v_sgu_w_out, v_ffn_w_gate_up, v_ffn_w_down):
    w = dict(norm_mix_pre=norm_mix_pre, norm_mix_post=norm_mix_post, norm_ffn_pre=norm_ffn_pre, norm_ffn_post=norm_ffn_post,
             attn_w_qkv=attn_w_qkv, attn_b_qkv=attn_b_qkv, attn_sinks=attn_sinks, attn_w_o=attn_w_o, attn_b_o=attn_b_o,
             sgu_w_in=sgu_w_in, sgu_ln_g=sgu_ln_g, sgu_ln_b=sgu_ln_b, sgu_w_spatial=sgu_w_spatial, sgu_b_spatial=sgu_b_spatial,
             sgu_w_out=sgu_w_out, ffn_w_gate_up=ffn_w_gate_up, ffn_w_down=ffn_w_down)
    mom = dict(norm_mix_pre=m_norm_mix_pre, norm_mix_post=m_norm_mix_post, norm_ffn_pre=m_norm_ffn_pre, norm_ffn_post=m_norm_ffn_post,
               attn_w_qkv=m_attn_w_qkv, attn_b_qkv=m_attn_b_qkv, attn_sinks=m_attn_sinks, attn_w_o=m_attn_w_o, attn_b_o=m_attn_b_o,
               sgu_w_in=m_sgu_w_in, sgu_ln_g=m_sgu_ln_g, sgu_ln_b=m_sgu_ln_b, sgu_w_spatial=m_sgu_w_spatial,
               sgu_b_spatial=m_sgu_b_spatial, sgu_w_out=m_sgu_w_out, ffn_w_gate_up=m_ffn_w_gate_up, ffn_w_down=m_ffn_w_down)
    var = dict(norm_mix_pre=v_norm_mix_pre, norm_mix_post=v_norm_mix_post, norm_ffn_pre=v_norm_ffn_pre, norm_ffn_post=v_norm_ffn_post,
               attn_w_qkv=v_attn_w_qkv, attn_b_qkv=v_attn_b_qkv, attn_sinks=v_attn_sinks, attn_w_o=v_attn_w_o, attn_b_o=v_attn_b_o,
               sgu_w_in=v_sgu_w_in, sgu_ln_g=v_sgu_ln_g, sgu_ln_b=v_sgu_ln_b, sgu_w_spatial=v_sgu_w_spatial,
               sgu_b_spatial=v_sgu_b_spatial, sgu_w_out=v_sgu_w_out, ffn_w_gate_up=v_ffn_w_gate_up, ffn_w_down=v_ffn_w_down)
    xs, target = x[0], loss_target[0]
    t, d = xs.shape
    row = lambda v: v.reshape(1, -1).astype(F32)
    core = lax.axis_index("c").astype(jnp.int32).reshape(1)
    chip = (2 * lax.axis_index("x") + lax.axis_index("y")).astype(jnp.int32).reshape(1)
    me = (4 * lax.axis_index("x") + 2 * lax.axis_index("y") + lax.axis_index("c")).astype(jnp.int32).reshape(1)
    names = EARLY + LATE
    staged, early = _prepare([_source(w, n) for n in names], [F32 if n == "sgu_ln" else ACT for n in names], len(EARLY),
                             "weights_prepare")
    stage = dict(zip(names, staged))
    w_qkv = _dup_heads(_columns_full(early[0]), d)
    lg, lb = row(early[1][:, 0, :]), row(early[1][:, 1, :])
    w_o = _rows_full(early[2])
    b_qkv = _dup_heads(row(attn_b_qkv), d)
    sinks = attn_sinks.reshape(1, -1).astype(F32)
    ws = sgu_w_spatial[0].astype(F32)
    bs_t = sgu_b_spatial[0].astype(F32).T
    cos, sin = _rope_tables(t)
    gather = lambda name, so_far, **pieces: _gather_job(stage[name], so_far, **pieces)
    a_half = lambda name: _halves(stage[name])[0]
    b_half = lambda name: _halves(stage[name])[1]
    whole = lambda name: _whole(stage[name])

    quarter = lambda name, i: (i * stage[name].shape[0] // 4, stage[name].shape[0] // 4)
    (h0, q, kdup, vdup), ((g_gu0,),) = _attn_qkv_fwd(
        xs, row(norm_mix_pre[0]), w_qkv, b_qkv, cos, sin, jobs=[gather("gu0", None, sends=[quarter("gu0", 0)])])
    (o,), ((g_gu0,),) = _attn_fwd(q, kdup, vdup, sinks, jobs=[
        gather("gu0", g_gu0, sends=[quarter("gu0", i) for i in (1, 2, 3)], forwards=[quarter("gu0", 0)])])
    (m0, x1), ((g_gu0,), (g_dn0,)) = _proj_res(o, w_o, row(attn_b_o), row(norm_mix_post[0]), xs, "attn_out_fwd", jobs=[
        gather("gu0", g_gu0, forwards=[quarter("gu0", i) for i in (1, 2, 3)]), gather("dn0", None, sends=[whole("dn0")])])
    w_gu0 = g_gu0
    (h1, gu0, a0), ((g_dn0,), (g_in,), (g_gu1,)) = _ffn_up_fwd(x1, row(norm_ffn_pre[0]), w_gu0, jobs=[
        gather("dn0", g_dn0, forwards=[whole("dn0")]), gather("sgu_w_in", None, sends=[whole("sgu_w_in")]),
        gather("gu1", None, sends=[quarter("gu1", 0), quarter("gu1", 1)])])
    w_dn0 = _rows_full(g_dn0)
    (f0, x2), ((g_in,), (g_out,), (g_gu1,)) = _proj_res(a0, w_dn0, None, row(norm_ffn_post[0]), x1, "ffn_down_fwd0", jobs=[
        gather("sgu_w_in", g_in, forwards=[whole("sgu_w_in")]), gather("sgu_w_out", None, sends=[whole("sgu_w_out")]),
        gather("gu1", g_gu1, sends=[quarter("gu1", 2)], forwards=[quarter("gu1", 0), quarter("gu1", 1)])])
    w_in = g_in
    (h2, z), ((g_out,), (g_gu1,)) = _sgu_in_fwd(x2, row(norm_mix_pre[1]), w_in, jobs=[
        gather("sgu_w_out", g_out, forwards=[whole("sgu_w_out")]),
        gather("gu1", g_gu1, sends=[quarter("gu1", 3)], forwards=[quarter("gu1", 2)])])
    (y,), ((g_gu1,), (g_dn1,)) = _sgu_mix_fwd(z, lg, lb, ws, bs_t, jobs=[
        gather("gu1", g_gu1, forwards=[quarter("gu1", 3)]), gather("dn1", None, sends=[a_half("dn1")])])
    w_out = _rows_full(g_out)
    (m1, x3), ((g_dn1,),) = _proj_res(y, w_out, None, row(norm_mix_post[1]), x2, "sgu_out_fwd", jobs=[
        gather("dn1", g_dn1, sends=[b_half("dn1")], forwards=[a_half("dn1")])])
    w_gu1 = g_gu1
    (h3, gu1, a1), ((g_dn1,),) = _ffn_up_fwd(x3, row(norm_ffn_pre[1]), w_gu1, jobs=[gather("dn1", g_dn1, forwards=[b_half("dn1")])])
    w_dn1 = _rows_full(g_dn1)
    (f1, x4), _ = _proj_res(a1, w_dn1, None, row(norm_ffn_post[1]), x3, "ffn_down_fwd1")
    dx4, loss_tile = _loss_head(x4, target)

    to_sibling = lambda g: _scatter_job([g], 4, _to_sibling)
    pair = lambda g, r, name: _pair_sum(g.reshape((4, 2) + g.shape[1:]), r, core, "pair_sum_" + name)
    to_chips = lambda p, prev=None, piece=None: _scatter_job([p], 3, _to_owner_chip, prev=prev, piece=piece)
    out_g, out_d, out_m, out_v = {}, {}, {}, {}

    trees = [{**tree, "sgu_ln": jnp.stack([tree["sgu_ln_g"], tree["sgu_ln_b"]], axis=1)} for tree in (w, mom, var)]
    so_far = {}

    def update(name, own, index, received):
        leaf, layer = LAYER_OF.get(name, (name, 0))
        so_far[leaf] = _shard_update(own, index, received, *[tree[leaf] for tree in trees], layer, so_far.get(leaf), "update_" + name)
        for out, val in zip((out_g, out_d, out_m, out_v), so_far[leaf]):
            out[leaf] = val

    def finish(names, extra, shares):
        like = [w[name] for name in names] + extra
        packed = [_pack_rows([tree[name] for name in names] + [jnp.zeros_like(e) for e in extra], d) for tree in (w, mom, var)]
        res = _replicated_update(shares, *packed)
        for out, val in zip((out_g, out_d, out_m, out_v), res):
            for name, leaf in zip(names, _unpack_rows(val, like, d)):
                out[name] = leaf
        return _unpack_rows(res[0], like, d)[len(names):]

    first_half = lambda p: _halves(p[0])[0]
    second_half = lambda p: _halves(p[0])[1]
    (df1, dgu1, dg_ffn_post1), _ = _post_bwd(dx4, f1, row(norm_ffn_post[1]), w_dn1, "ffn", "ffn_down_bwd1", gu1)
    b_gu1, _ = _wgrad(h3, dgu1, "ffn_up_wgrad1", ACT)
    dw_dn1, ((r_gu1,),) = _wgrad(a1, df1, "ffn_down_wgrad1", ACT, jobs=[to_sibling(b_gu1)])
    b_dn1 = _rows_blocked(dw_dn1)
    p_gu1 = pair(b_gu1, r_gu1, "gu1")
    (dx3, dg_ffn_pre1), ((q_gu1,), (r_dn1,)) = _pre_bwd(dgu1, w_gu1, x3, row(norm_ffn_pre[1]), dx4, "ffn_up_bwd1", jobs=[
        to_chips(p_gu1, piece=first_half(p_gu1)), to_sibling(b_dn1)])
    p_dn1 = pair(b_dn1, r_dn1, "dn1")
    (dm1, dy, dg_mix_post1), _ = _post_bwd(dx3, m1, row(norm_mix_post[1]), w_out, "sgu", "sgu_out_bwd")
    dw_out, _ = _wgrad(y, dm1, "sgu_out_wgrad", ACT)
    b_out = _rows_blocked(dw_out)
    (dz, dlg, dlb, dws, dbs_t), ((q_gu1,), (r_out,)) = _sgu_mix_bwd(z, dy, lg, lb, ws, bs_t, jobs=[
        to_chips(p_gu1, prev=[q_gu1], piece=second_half(p_gu1)), to_sibling(b_out)])
    update("gu1", p_gu1, chip, q_gu1)
    p_out = pair(b_out, r_out, "sgu_w_out")
    b_in, ((q_out,),) = _wgrad(h2, dz, "sgu_in_wgrad", ACT, out_block=stage["sgu_w_in"].shape[1], jobs=[to_chips(p_out)])
    update("sgu_w_out", p_out, chip, q_out)
    b_ln = jnp.stack([dlg.reshape(N_DEV, -1), dlb.reshape(N_DEV, -1)], axis=1)
    (dx2, dg_mix_pre1), ((r_in,), (r_ln,)) = _pre_bwd(dz, w_in, x2, row(norm_mix_pre[1]), dx3, "sgu_in_bwd",
                                                      jobs=[to_sibling(b_in), to_sibling(b_ln)])
    p_in, p_ln = pair(b_in, r_in, "sgu_w_in"), pair(b_ln, r_ln, "sgu_ln")
    (df0, dgu0, dg_ffn_post0), ((q_dn1,),) = _post_bwd(dx2, f0, row(norm_ffn_post[0]), w_dn0, "ffn", "ffn_down_bwd0", gu0,
                                                      jobs=[to_chips(p_dn1)])
    update("dn1", p_dn1, chip, q_dn1)
    b_gu0, ((q_in,), (q_ln,)) = _wgrad(h1, dgu0, "ffn_up_wgrad0", ACT, jobs=[to_chips(p_in), to_chips(p_ln)])
    update("sgu_w_in", p_in, chip, q_in)
    update("sgu_ln", p_ln, chip, q_ln)
    dw_dn0, ((r_gu0,),) = _wgrad(a0, df0, "ffn_down_wgrad0", ACT, jobs=[to_sibling(b_gu0)])
    b_dn0 = _rows_blocked(dw_dn0)
    p_gu0 = pair(b_gu0, r_gu0, "gu0")
    (dx1, dg_ffn_pre0), ((q_gu0,), (r_dn0,)) = _pre_bwd(dgu0, w_gu0, x1, row(norm_ffn_pre[0]), dx2, "ffn_up_bwd0", jobs=[
        to_chips(p_gu0, piece=first_half(p_gu0)), to_sibling(b_dn0)])
    p_dn0 = pair(b_dn0, r_dn0, "dn0")
    (dm0, do, dg_mix_post0, db_o), _ = _post_bwd(dx1, m0, row(norm_mix_post[0]), w_o, "attn", "attn_out_bwd")
    dw_o, _ = _wgrad(o, dm0, "attn_out_wgrad", ACT)
    b_o = _rows_blocked(dw_o)
    grads = {
        "norm_mix_post": jnp.concatenate([dg_mix_post0, dg_mix_post1], axis=0),
        "norm_ffn_pre": jnp.concatenate([dg_ffn_pre0, dg_ffn_pre1], axis=0),
        "norm_ffn_post": jnp.concatenate([dg_ffn_post0, dg_ffn_post1], axis=0),
        "attn_b_o": db_o,
        "sgu_w_spatial": dws * jnp.tril(jnp.ones((WINDOW, WINDOW), F32))[None],
        "sgu_b_spatial": dbs_t.T,
    }
    shares1 = _pack_rows([grads[name] for name in BEFORE_ATTN], d)
    (dqkv, db_qkv, dsink), ((q_gu0,), (q_dn0,), (r_o,), (g_shares1,)) = _attn_bwd(q, kdup, vdup, do, sinks, cos, sin, jobs=[
        to_chips(p_gu0, prev=[q_gu0], piece=second_half(p_gu0)), to_chips(p_dn0), to_sibling(b_o),
        _gather_job(shares1, None, sends=[_whole(shares1)])])
    update("gu0", p_gu0, chip, q_gu0)
    update("dn0", p_dn0, chip, q_dn0)
    p_o = pair(b_o, r_o, "attn_w_o")
    grads.update({"attn_b_qkv": _fold_heads(db_qkv, d), "attn_sinks": dsink[:1, :d // HEAD_DIM]})
    shares2 = _pack_rows([grads[name] for name in AFTER_ATTN] + [loss_tile[:1, :1]], d)
    dw_qkv, ((q_o,), (g_shares1,), (g_shares2,)) = _wgrad(h0, dqkv, "attn_qkv_wgrad", jobs=[
        to_chips(p_o), _gather_job(shares1, g_shares1, forwards=[_whole(shares1)]),
        _gather_job(shares2, None, sends=[_whole(shares2)])])
    update("attn_w_o", p_o, chip, q_o)
    b_qkv_grad = _columns_blocked(_fold_heads(dw_qkv, d))
    (dx0, dg_mix_pre0), ((q_qkv,), (g_shares2,)) = _pre_bwd(dqkv, w_qkv, xs, row(norm_mix_pre[0]), dx1, "attn_qkv_bwd", jobs=[
        _scatter_job([b_qkv_grad], N_DEV - 1, _to_owner), _gather_job(shares2, g_shares2, forwards=[_whole(shares2)])])
    update("attn_w_qkv", b_qkv_grad, me, q_qkv)

    finish(BEFORE_ATTN, [], g_shares1)
    loss = finish(AFTER_ATTN, [loss_tile[:1, :1]], g_shares2)[0][0, 0]
    grads["norm_mix_pre"] = jnp.concatenate([dg_mix_pre0, dg_mix_pre1], axis=0)
    _, (last_shares,) = _prepare([([_pack_rows([grads[name] for name in LAST], d)[None]], 0)], [F32], 1, "last_grads_all_gather")
    finish(LAST, [], last_shares)

    for out in (out_g, out_d, out_m, out_v):
        out["sgu_ln_g"], out["sgu_ln_b"] = out["sgu_ln"][:, 0, :], out["sgu_ln"][:, 1, :]

    return (loss, dx0[None], *[out_g[n] for n in WEIGHTS], *[out_d[n] for n in WEIGHTS],
            *[out_m[n] for n in WEIGHTS], *[out_v[n] for n in WEIGHTS])
```

```python
import functools
import math
import operator

import jax
import jax.numpy as jnp
from jax import lax
from jax.experimental import pallas as pl
from jax.experimental.pallas import tpu as pltpu

F32 = jnp.float32
ACT = jnp.bfloat16

EPS = 1e-6
HEAD_DIM = 64
PAIR = 2 * HEAD_DIM
GQA_GROUP = 4
WINDOW = 128
ROPE_THETA = 10000.0
SCALE = HEAD_DIM ** -0.5
MASKED = -1e30
LANES = 128

ADAM_LR, ADAM_B1, ADAM_B2, ADAM_EPS, ADAM_WD, ADAM_STEP = 0.001, 0.9, 0.999, 1e-08, 0.01, 10

N_DEV = 8
VMEM_LIMIT_BYTES = 56 * 1024 * 1024
MESH = pl.DeviceIdType.MESH
ANY = pl.BlockSpec(memory_space=pl.ANY)
IN_VMEM = pl.BlockSpec(memory_space=pltpu.VMEM)


def _pick(n, candidates):
    for c in candidates:
        if n % c == 0:
            return c
    return n


def _row_tile(t):
    return _pick(t, (512,))


def _params(*sem):
    return pltpu.CompilerParams(dimension_semantics=sem, vmem_limit_bytes=VMEM_LIMIT_BYTES)


def _full(shape):
    return pl.BlockSpec(shape, lambda *_: (0,) * len(shape))


def _rows(tm, n):
    return pl.BlockSpec((tm, n), lambda i: (i, 0))


def _sds(shape, dtype):
    return jax.ShapeDtypeStruct(shape, dtype)


def _place():
    return lax.axis_index("x"), lax.axis_index("y"), lax.axis_index("c")


def _other_chips(x, y):
    return [(1 - x, y), (x, 1 - y), (1 - x, 1 - y)]


class _Job:
    def __init__(self, inputs, out_shape, aliases, emit, n_remote, n_local=0):
        self.inputs, self.out_shape, self.aliases, self.emit = list(inputs), list(out_shape), dict(aliases), emit
        self.n_remote, self.n_local = n_remote, n_local


def _call(body, name, grid, in_specs, out_specs, out_shape, args, sem, scratch=(), jobs=()):
    n_in, n_out, n_scr = len(args), len(out_shape), len(scratch)
    j_in = [a for job in jobs for a in job.inputs]
    j_out = [s for job in jobs for s in job.out_shape]
    aliases, at_in, at_out = {}, n_in, n_out
    for job in jobs:
        aliases.update({at_in + i: at_out + o for i, o in job.aliases.items()})
        at_in, at_out = at_in + len(job.inputs), at_out + len(job.out_shape)
    n_remote = sum(job.n_remote for job in jobs)
    n_local = sum(job.n_local for job in jobs)

    def wrapped(*refs):
        ins, jin = refs[:n_in], refs[n_in:n_in + len(j_in)]
        rest = refs[n_in + len(j_in):]
        outs, jout = rest[:n_out], rest[n_out:n_out + len(j_out)]
        scr = rest[n_out + len(j_out):n_out + len(j_out) + n_scr]
        if not jobs:
            body(*ins, *outs, *scr)
            return
        send, recv, local = rest[n_out + len(j_out) + n_scr:]
        ids = [pl.program_id(a) for a in range(len(grid))]
        first = functools.reduce(operator.and_, [i == 0 for i in ids])
        last = functools.reduce(operator.and_, [i == g - 1 for i, g in zip(ids, grid)])

        def descriptors():
            place, found, i, o, r, l = _place(), [], 0, 0, 0, 0
            for job in jobs:
                for src, dst, to in job.emit(jin[i:i + len(job.inputs)], jout[o:o + len(job.out_shape)], place):
                    if to is None:
                        found.append(pltpu.make_async_copy(src, dst, local.at[l]))
                        l += 1
                    else:
                        found.append(pltpu.make_async_remote_copy(src_ref=src, dst_ref=dst, send_sem=send.at[r], recv_sem=recv.at[r],
                                                                  device_id=to, device_id_type=MESH))
                        r += 1
                i, o = i + len(job.inputs), o + len(job.out_shape)
            return found

        @pl.when(first)
        def _():
            for cp in descriptors():
                cp.start()

        body(*ins, *outs, *scr)

        @pl.when(last)
        def _():
            for cp in descriptors():
                cp.wait()

    sems = [pltpu.SemaphoreType.DMA((n_remote,)), pltpu.SemaphoreType.DMA((n_remote,)),
            pltpu.SemaphoreType.DMA((max(n_local, 1),))] if jobs else []
    res = pl.pallas_call(
        wrapped, name=name, grid=grid,
        in_specs=list(in_specs) + [ANY] * len(j_in), out_specs=list(out_specs) + [ANY] * len(j_out),
        out_shape=list(out_shape) + j_out, scratch_shapes=list(scratch) + sems, input_output_aliases=aliases,
        compiler_params=_params(*(("arbitrary",) * len(grid) if jobs else sem)),
    )(*args, *j_in)
    job_res, at = [], n_out
    for job in jobs:
        job_res.append(list(res[at:at + len(job.out_shape)]))
        at += len(job.out_shape)
    return list(res[:n_out]), job_res


def _gather_job(stage, gathered, sends=(), forwards=()):
    shape = _sds((N_DEV,) + stage.shape, stage.dtype)
    inputs = ([stage] if sends else []) + ([gathered] if gathered is not None else [])
    aliases = {len(inputs) - 1: 0} if gathered is not None else {}

    def emit(ins, outs, place):
        x, y, c = place
        sibling, chips, mine, g = (x, y, 1 - c), _other_chips(x, y), 4 * x + 2 * y + c, outs[0]
        found = []
        for r0, nr in sends:
            src, dst = ins[0].at[pl.ds(r0, nr)], g.at[mine, pl.ds(r0, nr)]
            found += [(src, dst, None), (src, dst, sibling)] + [(src, dst, (px, py, c)) for px, py in chips]
        for r0, nr in forwards:
            for px, py in chips:
                block = 4 * px + 2 * py + c
                found.append((ins[-1].at[block, pl.ds(r0, nr)], g.at[block, pl.ds(r0, nr)], sibling))
        return found

    return _Job(inputs, [shape], aliases, emit, 4 * len(sends) + 3 * len(forwards), len(sends))


def _scatter_job(arrays, n_slots, route, prev=None, piece=None):
    shapes = [_sds((n_slots,) + v.shape[1:], v.dtype) for v in arrays]
    inputs = list(arrays) + (list(prev) if prev else [])
    aliases = {len(arrays) + i: i for i in range(len(arrays))} if prev else {}

    def emit(ins, outs, place):
        found = []
        for a in range(len(arrays)):
            for s in range(n_slots):
                block, to = route(s, *place)
                if piece is None:
                    found.append((ins[a].at[block], outs[a].at[s], to))
                else:
                    found.append((ins[a].at[block, pl.ds(*piece)], outs[a].at[s, pl.ds(*piece)], to))
        return found

    return _Job(inputs, shapes, aliases, emit, len(arrays) * n_slots)


def _to_sibling(s, x, y, c):
    return 2 * s + (1 - c), (x, y, 1 - c)


def _to_owner_chip(s, x, y, c):
    px, py = _other_chips(x, y)[s]
    return 2 * px + py, (px, py, c)


def _to_owner(s, x, y, c):
    if s == 0:
        px, py, pc = x, y, 1 - c
    else:
        px, py = _other_chips(x, y)[(s - 1) % 3]
        pc = c if s <= 3 else 1 - c
    return 4 * px + 2 * py + pc, (px, py, pc)


def _rstd(x):
    return lax.rsqrt(jnp.mean(x * x, axis=-1, keepdims=True) + EPS)


def _rms_bwd(xhat, r, g, dy):
    dxh = dy * g
    return r * (dxh - xhat * jnp.mean(dxh * xhat, axis=-1, keepdims=True))


def _first_half(rows):
    lane = lax.broadcasted_iota(jnp.int32, (rows, PAIR), 1)
    return (lane % HEAD_DIM) < (HEAD_DIM // 2)


def _rot_half(v, first):
    return jnp.where(first, pltpu.roll(v, PAIR - HEAD_DIM // 2, 1), pltpu.roll(v, HEAD_DIM // 2, 1))


def _rope(v, cos, sin, first):
    return v * cos + _rot_half(v, first) * sin


def _rope_t(dv, cos, sin, first):
    return dv * cos + _rot_half(dv * sin, first)


def _dot(a, b):
    return jnp.dot(a, b, preferred_element_type=F32)


def _dot_nt(a, b):
    return lax.dot_general(a, b, (((1,), (1,)), ((), ())), preferred_element_type=F32)


def _dot_tn(a, b):
    return lax.dot_general(a, b, (((0,), (0,)), ((), ())), preferred_element_type=F32)


def _sigmoid(v):
    return 1.0 / (1.0 + jnp.exp(-v))


_GELU_K = math.sqrt(2.0 / math.pi)
_GELU_C = 0.044715


def _gelu(v):
    return v * (0.5 * (1.0 + jnp.tanh(_GELU_K * (v + _GELU_C * (v * v * v)))))


def _gelu_grad(v):
    t = jnp.tanh(_GELU_K * (v + _GELU_C * (v * v * v)))
    return 0.5 * (1.0 + t) + 0.5 * v * (1.0 - t * t) * (_GELU_K * (1.0 + 3.0 * _GELU_C * (v * v)))


def _attn_qkv_fwd(x, g, w, b, cos, sin, jobs=()):
    t, d = x.shape
    n = w.shape[1]
    kd = (n - d) // 2
    tm = _row_tile(t)
    ch = _pick(d, (512,))

    def body(x_ref, g_ref, w_ref, b_ref, cos_ref, sin_ref, h_ref, q_ref, k_ref, v_ref):
        xv = x_ref[...]
        hb = (xv * _rstd(xv) * g_ref[...]).astype(ACT)
        h_ref[...] = hb
        cosv, sinv = cos_ref[...], sin_ref[...]
        first = _first_half(tm)

        def proj(lo, width):
            return _dot(hb, w_ref[:, lo:lo + width]) + b_ref[:, lo:lo + width]

        for c in range(0, d, ch):
            y = proj(c, ch)
            for s in range(0, ch, PAIR):
                q_ref[:, c + s:c + s + PAIR] = (_rope(y[:, s:s + PAIR], cosv, sinv, first) * SCALE).astype(ACT)
        y = proj(d, kd)
        for s in range(0, kd, PAIR):
            k_ref[:, s:s + PAIR] = _rope(y[:, s:s + PAIR], cosv, sinv, first).astype(ACT)
        v_ref[...] = proj(d + kd, kd).astype(ACT)

    return _call(
        body, "attn_qkv_fwd", (t // tm,),
        [_rows(tm, d), _full((1, d)), _full((d, n)), _full((1, n)), _rows(tm, PAIR), _rows(tm, PAIR)],
        [_rows(tm, d), _rows(tm, d), _rows(tm, kd), _rows(tm, kd)],
        [_sds((t, d), ACT), _sds((t, d), ACT), _sds((t, kd), ACT), _sds((t, kd), ACT)],
        (x, g, w, b, cos, sin), ("parallel",), jobs=jobs)


STACK = GQA_GROUP * WINDOW


def _band_mask(has_prev):
    row = lax.broadcasted_iota(jnp.int32, (STACK, 2 * WINDOW), 0) % WINDOW
    col = lax.broadcasted_iota(jnp.int32, (STACK, 2 * WINDOW), 1)
    return ((col < WINDOW) & (col > row) & has_prev) | ((col >= WINDOW) & (col - WINDOW <= row))


def _lane_halves():
    lane = lax.broadcasted_iota(jnp.int32, (1, PAIR), 1)
    return lane < HEAD_DIM, lane >= HEAD_DIM


def _stack_heads(ref, h, halves):
    parts = []
    for p in range(2):
        pair = ref[:, (2 * h + p) * PAIR:(2 * h + p + 1) * PAIR]
        parts += [jnp.where(half, pair, jnp.zeros_like(pair)) for half in halves]
    return jnp.concatenate(parts, axis=0)


def _sink_column(sink_ref, h):
    row = lax.broadcasted_iota(jnp.int32, (STACK, 1), 0)
    col = jnp.full((STACK, 1), sink_ref[0, GQA_GROUP * h + GQA_GROUP - 1], F32)
    for j in range(GQA_GROUP - 2, -1, -1):
        col = jnp.where(row < (j + 1) * WINDOW, sink_ref[0, GQA_GROUP * h + j], col)
    return col


def _probs(qs, k2, valid, sink):
    s = jnp.where(valid, _dot_nt(qs, k2), MASKED)
    m = jnp.maximum(jnp.max(s, axis=-1, keepdims=True), sink)
    p = jnp.exp(s - m)
    psink = jnp.exp(sink - m)
    inv = 1.0 / (jnp.sum(p, axis=-1, keepdims=True) + psink)
    return p * inv, psink * inv


def _attn_fwd(q, kd_, vd, sinks, jobs=()):
    t, d = q.shape
    kd = kd_.shape[1]
    cur = lambda n: (n, 0)
    prev = lambda n: (jnp.maximum(n - 1, 0), 0)

    def body(sink_ref, q_ref, kc_ref, kp_ref, vc_ref, vp_ref, o_ref):
        valid = _band_mask(pl.program_id(0) > 0)
        halves = _lane_halves()
        for h in range(kd // PAIR):
            hs = slice(h * PAIR, (h + 1) * PAIR)
            k2 = jnp.concatenate([kp_ref[:, hs], kc_ref[:, hs]], axis=0)
            v2 = jnp.concatenate([vp_ref[:, hs], vc_ref[:, hs]], axis=0)
            vs = jnp.concatenate([jnp.where(half, v2, jnp.zeros_like(v2)) for half in halves], axis=0)
            pr, _ = _probs(_stack_heads(q_ref, h, halves), k2, valid, _sink_column(sink_ref, h))
            pr = pr.astype(ACT)
            for p in range(2):
                both = jnp.concatenate([pr[2 * p * WINDOW:(2 * p + 1) * WINDOW], pr[(2 * p + 1) * WINDOW:(2 * p + 2) * WINDOW]], axis=1)
                o_ref[:, (2 * h + p) * PAIR:(2 * h + p + 1) * PAIR] = _dot(both, vs).astype(ACT)

    kv_c, kv_p = pl.BlockSpec((WINDOW, kd), cur), pl.BlockSpec((WINDOW, kd), prev)
    return _call(
        body, "attn_fwd", (t // WINDOW,),
        [pl.BlockSpec(memory_space=pltpu.SMEM), pl.BlockSpec((WINDOW, d), cur), kv_c, kv_p, kv_c, kv_p],
        [pl.BlockSpec((WINDOW, d), cur)], [_sds((t, d), ACT)],
        (sinks, q, kd_, kd_, vd, vd), ("parallel",), jobs=jobs)


def _attn_bwd(q, kd_, vd, do, sinks, cos, sin, jobs=()):
    t, d = q.shape
    kd = kd_.shape[1]
    nb = t // WINDOW
    n_out = d + 2 * kd
    cur = lambda n: (jnp.minimum(n, nb - 1), 0)
    prev = lambda n: (jnp.maximum(jnp.minimum(n, nb - 1) - 1, 0), 0)
    late = lambda n: (jnp.maximum(n - 1, 0), 0)

    def body(sink_ref, q_ref, kc_ref, kp_ref, vc_ref, vp_ref, do_ref, cosc_ref, sinc_ref, cosp_ref, sinp_ref,
             out_ref, db_ref, dsink_ref, dq_keep, dk_keep, dv_keep):
        n = pl.program_id(0)

        @pl.when(n == 0)
        def _():
            for ref in (db_ref, dsink_ref, dq_keep, dk_keep, dv_keep):
                ref[...] = jnp.zeros_like(ref)

        valid = _band_mask(n > 0) & (n < nb)
        halves = _lane_halves()
        first = _first_half(WINDOW)
        slot = lax.broadcasted_iota(jnp.int32, dsink_ref.shape, 1)
        top = lax.broadcasted_iota(jnp.int32, dsink_ref.shape, 0) == 0
        dsink = jnp.zeros(dsink_ref.shape, F32)

        def emit(cols, done):
            out_ref[:, cols] = done.astype(ACT)
            db_ref[:, cols] += jnp.sum(done.astype(F32), axis=0, keepdims=True)

        for h in range(kd // PAIR):
            hs = slice(h * PAIR, (h + 1) * PAIR)
            k2 = jnp.concatenate([kp_ref[:, hs], kc_ref[:, hs]], axis=0)
            v2 = jnp.concatenate([vp_ref[:, hs], vc_ref[:, hs]], axis=0)
            qs, dos = _stack_heads(q_ref, h, halves), _stack_heads(do_ref, h, halves)
            pr, psink = _probs(qs, k2, valid, _sink_column(sink_ref, h))
            dp = _dot_nt(dos, v2)
            delta = jnp.sum(pr * dp, axis=-1, keepdims=True)
            ds = (pr * (dp - delta)).astype(ACT)
            leak = psink * delta
            for j in range(GQA_GROUP):
                share = jnp.sum(leak[j * WINDOW:(j + 1) * WINDOW], axis=0, keepdims=True)
                dsink = dsink - jnp.where(top & (slot == GQA_GROUP * h + j), share, 0.0)
            dqs = _dot(ds, k2)
            for p in range(2):
                ps = slice((2 * h + p) * PAIR, (2 * h + p + 1) * PAIR)
                dqp = jnp.where(halves[0], dqs[2 * p * WINDOW:(2 * p + 1) * WINDOW], dqs[(2 * p + 1) * WINDOW:(2 * p + 2) * WINDOW])
                emit(ps, dq_keep[:, ps])
                dq_keep[:, ps] = (_rope_t(dqp, cosc_ref[...], sinc_ref[...], first) * SCALE).astype(ACT)
            dk2 = _dot_tn(ds, qs)
            dv2 = _dot_tn(pr.astype(ACT), dos)
            ks = slice(d + h * PAIR, d + (h + 1) * PAIR)
            vs = slice(d + kd + h * PAIR, d + kd + (h + 1) * PAIR)
            emit(ks, dk_keep[:, hs] + _rope_t(dk2[:WINDOW], cosp_ref[...], sinp_ref[...], first))
            dk_keep[:, hs] = _rope_t(dk2[WINDOW:], cosc_ref[...], sinc_ref[...], first)
            emit(vs, dv_keep[:, hs] + dv2[:WINDOW])
            dv_keep[:, hs] = dv2[WINDOW:]
        dsink_ref[...] += dsink

    kv_c, kv_p = pl.BlockSpec((WINDOW, kd), cur), pl.BlockSpec((WINDOW, kd), prev)
    tab_c, tab_p = pl.BlockSpec((WINDOW, PAIR), cur), pl.BlockSpec((WINDOW, PAIR), prev)
    return _call(
        body, "attn_bwd", (nb + 1,),
        [pl.BlockSpec(memory_space=pltpu.SMEM), pl.BlockSpec((WINDOW, d), cur), kv_c, kv_p, kv_c, kv_p,
         pl.BlockSpec((WINDOW, d), cur), tab_c, tab_c, tab_p, tab_p],
        [pl.BlockSpec((WINDOW, n_out), late), _full((1, n_out)), _full((8, LANES))],
        [_sds((t, n_out), ACT), _sds((1, n_out), F32), _sds((8, LANES), F32)],
        (sinks, q, kd_, kd_, vd, vd, do, cos, sin, cos, sin), ("arbitrary",),
        scratch=[pltpu.VMEM((WINDOW, d), ACT), pltpu.VMEM((WINDOW, kd), F32), pltpu.VMEM((WINDOW, kd), F32)], jobs=jobs)


def _proj_res(a, w, b, g, x, name, target=None, jobs=()):
    blocked = a.ndim == 3
    t = a.shape[-2]
    k, d = w.shape
    tm = _row_tile(t)
    has_bias = b is not None

    def body(*refs):
        a_ref, w_ref = refs[:2]
        b_ref = refs[2] if has_bias else None
        g_ref, x_ref, m_ref, xo_ref = refs[2 + has_bias:]
        if blocked:
            c = a.shape[2]
            m = _dot(a_ref[0], w_ref[:c, :])
            for j in range(1, a.shape[0]):
                m = m + _dot(a_ref[j], w_ref[j * c:(j + 1) * c, :])
        else:
            m = _dot(a_ref[...], w_ref[...])
        if has_bias:
            m = m + b_ref[...]
        m_ref[...] = m
        xo_ref[...] = x_ref[...] + (m * _rstd(m)) * g_ref[...]

    def body_with_loss(a_ref, w_ref, g_ref, x_ref, t_ref, m_ref, dy_ref, loss_ref):
        @pl.when(pl.program_id(0) == 0)
        def _():
            loss_ref[...] = jnp.zeros_like(loss_ref)

        body(a_ref, w_ref, g_ref, x_ref, m_ref, dy_ref)
        err = dy_ref[...] - t_ref[...]
        dy_ref[...] = err * (1.0 / d)
        loss_ref[...] += 0.5 * jnp.sum(jnp.mean(err * err, axis=-1, keepdims=True), axis=0, keepdims=True)

    ins = [a, w] + ([b] if has_bias else []) + [g, x]
    a_spec = pl.BlockSpec((a.shape[0], tm, a.shape[2]), lambda i: (0, i, 0)) if blocked else _rows(tm, k)
    specs = [a_spec, _full((k, d))] + ([_full((1, d))] if has_bias else []) + [_full((1, d)), _rows(tm, d)]
    if target is not None:
        return _call(body_with_loss, name, (t // tm,), specs + [_rows(tm, d)], [_rows(tm, d), _rows(tm, d), _full((8, LANES))],
                     [_sds((t, d), F32), _sds((t, d), F32), _sds((8, LANES), F32)], ins + [target], ("arbitrary",), jobs=jobs)
    return _call(body, name, (t // tm,), specs, [_rows(tm, d), _rows(tm, d)], [_sds((t, d), F32)] * 2, ins, ("parallel",), jobs=jobs)


def _post_bwd(dres, m, g, w, mode, name, gu=None, jobs=()):
    t, d = dres.shape
    k = w.shape[0]
    tm = _row_tile(t)
    kc = _pick(k, (1408, 1024, 512))

    def body(*refs):
        d_ref, m_ref, g_ref, w_ref = refs[:4]
        gu_ref = refs[4] if mode == "ffn" else None
        outs = refs[4 + (mode == "ffn"):]
        dm_ref, da_ref, dg_ref = outs[:3]
        i = pl.program_id(0)

        @pl.when(i == 0)
        def _():
            dg_ref[...] = jnp.zeros_like(dg_ref)
            if mode == "attn":
                outs[3][...] = jnp.zeros_like(outs[3])

        dv, mv = d_ref[...], m_ref[...]
        r = _rstd(mv)
        mh = mv * r
        dg_ref[...] += jnp.sum(dv * mh, axis=0, keepdims=True)
        dm = _rms_bwd(mh, r, g_ref[...], dv)
        dmb = dm.astype(ACT)
        dm_ref[...] = dmb
        if mode == "attn":
            outs[3][...] += jnp.sum(dm, axis=0, keepdims=True)
        if mode == "ffn":
            nb, _, cb = gu.shape
            for j in range(nb // 2):
                da = _dot_nt(dmb, w_ref[j * cb:(j + 1) * cb, :]).astype(ACT)
                gate, up = gu_ref[j], gu_ref[nb // 2 + j]
                sg = _sigmoid(gate.astype(F32)).astype(ACT)
                gs = gate * sg
                da_ref[j] = da * up * (sg + gs - gs * sg)
                da_ref[nb // 2 + j] = da * gs
        else:
            for c in range(0, k, kc):
                da = _dot_nt(dmb, w_ref[c:c + kc, :])
                da_ref[:, c:c + kc] = da.astype(ACT) if mode == "attn" else da

    ins = [dres, m, g, w]
    specs = [_rows(tm, d), _rows(tm, d), _full((1, d)), _full((k, d))]
    if mode == "ffn":
        slab = pl.BlockSpec((gu.shape[0], tm, gu.shape[2]), lambda i: (0, i, 0))
        ins.append(gu)
        specs.append(slab)
        out_specs = [_rows(tm, d), slab, _full((1, d))]
        out_shape = [_sds((t, d), ACT), _sds(gu.shape, ACT), _sds((1, d), F32)]
    else:
        out_specs = [_rows(tm, d), _rows(tm, k), _full((1, d))]
        out_shape = [_sds((t, d), ACT), _sds((t, k), ACT if mode == "attn" else F32), _sds((1, d), F32)]
    if mode == "attn":
        out_specs.append(_full((1, d)))
        out_shape.append(_sds((1, d), F32))
    return _call(body, name, (t // tm,), specs, out_specs, out_shape, ins, ("arbitrary",), jobs=jobs)


def _pre_bwd(dy, w, x, g, dres, name, transposed=False, jobs=()):
    t, d = x.shape
    tm = _row_tile(t)

    def body(dy_ref, w_ref, x_ref, g_ref, dres_ref, dx_ref, dg_ref):
        @pl.when(pl.program_id(0) == 0)
        def _():
            dg_ref[...] = jnp.zeros_like(dg_ref)

        dh = jnp.zeros((tm, d), F32)
        if w.ndim == 3 and transposed:
            for j in range(w.shape[0]):
                dh = dh + _dot(dy_ref[j], w_ref[j])
        elif w.ndim == 3:
            c = w.shape[2]
            for j in range(w.shape[0]):
                dh = dh + _dot_nt(dy_ref[j] if dy.ndim == 3 else dy_ref[:, j * c:(j + 1) * c], w_ref[j])
        else:
            n = w.shape[1]
            nc = _pick(n, (1408, 1024, 512))
            for c in range(0, n, nc):
                dh = dh + _dot_nt(dy_ref[:, c:c + nc], w_ref[:, c:c + nc])
        xv = x_ref[...]
        r = _rstd(xv)
        xh = xv * r
        dg_ref[...] += jnp.sum(dh * xh, axis=0, keepdims=True)
        dx_ref[...] = dres_ref[...] + _rms_bwd(xh, r, g_ref[...], dh)

    dy_spec = pl.BlockSpec((dy.shape[0], tm, dy.shape[2]), lambda i: (0, i, 0)) if dy.ndim == 3 else _rows(tm, dy.shape[1])
    return _call(
        body, name, (t // tm,),
        [dy_spec, _full(w.shape), _rows(tm, d), _full((1, d)), _rows(tm, d)],
        [_rows(tm, d), _full((1, d))], [_sds((t, d), F32), _sds((1, d), F32)],
        (dy, w, x, g, dres), ("arbitrary",), jobs=jobs)


def _wgrad(a, b, name, out_dtype=F32, out_block=None, transposed=False, jobs=()):
    t = a.shape[-2]
    tt = _pick(t, (1024,))
    steps = t // tt
    if a.ndim == 3:
        k_steps, _, tk = a.shape
        a_spec = pl.BlockSpec((None, tt, tk), lambda i, j, s: (i, s, 0))
    else:
        tk = _pick(a.shape[1], (1024, 1408))
        k_steps = a.shape[1] // tk
        a_spec = pl.BlockSpec((tt, tk), lambda i, j, s: (s, i))
    k = k_steps * tk
    group = 1
    if b.ndim == 3:
        n_blocks, _, tn = b.shape
        group = 2 if n_blocks % 2 == 0 else 1
        n_steps, per = n_blocks // group, 1
        b_spec = pl.BlockSpec((group, tt, tn), lambda i, j, s: (j, s, 0))
        if transposed:
            out_spec, out_shape = pl.BlockSpec((group, tn, tk), lambda i, j, s: (j, 0, i)), _sds((n_blocks, tn, k), out_dtype)
        else:
            out_spec, out_shape = pl.BlockSpec((group, tk, tn), lambda i, j, s: (j, i, 0)), _sds((n_blocks, k, tn), out_dtype)
    else:
        n = b.shape[1]
        tn = _pick(n, (1024, 1408))
        n_steps = n // tn
        b_spec = pl.BlockSpec((tt, tn), lambda i, j, s: (s, j))
        if out_block is None:
            per = 0
            out_spec, out_shape = pl.BlockSpec((tk, tn), lambda i, j, s: (i, j)), _sds((k, n), out_dtype)
        else:
            per = tn // out_block
            out_spec = pl.BlockSpec((per, tk, out_block), lambda i, j, s: (j, i, 0))
            out_shape = _sds((n // out_block, k, out_block), out_dtype)

    def body(a_ref, b_ref, o_ref, acc_ref):
        s = pl.program_id(2)

        @pl.when(s == 0)
        def _():
            acc_ref[...] = jnp.zeros_like(acc_ref)

        if b.ndim == 3:
            av = a_ref[...]
            for p in range(group):
                acc_ref[p] += _dot_tn(b_ref[p], av) if transposed else _dot_tn(av, b_ref[p])
        else:
            acc_ref[...] += _dot_tn(a_ref[...], b_ref[...])

        @pl.when(s == steps - 1)
        def _():
            if b.ndim == 2 and per >= 1:
                for p in range(per):
                    o_ref[p] = acc_ref[:, p * out_block:(p + 1) * out_block].astype(out_dtype)
            else:
                o_ref[...] = acc_ref[...].astype(out_dtype)

    acc_shape = ((group, tn, tk) if transposed else (group, tk, tn)) if b.ndim == 3 else (tk, tn)
    res, job_res = _call(
        body, name, (k_steps, n_steps, steps), [a_spec, b_spec], [out_spec], [out_shape],
        (a, b), ("parallel", "parallel", "arbitrary"), scratch=[pltpu.VMEM(acc_shape, F32)], jobs=jobs)
    return res[0], job_res


def _ffn_up_fwd(x, g, w, jobs=()):
    t, d = x.shape
    nb, c, _ = w.shape
    tm = _row_tile(t)

    def body(x_ref, g_ref, w_ref, h_ref, gu_ref, a_ref):
        xv = x_ref[...]
        hb = (xv * _rstd(xv) * g_ref[...]).astype(ACT)
        h_ref[...] = hb
        for j in range(nb // 2):
            gate = _dot_nt(hb, w_ref[j])
            up = _dot_nt(hb, w_ref[nb // 2 + j])
            gu_ref[j] = gate.astype(ACT)
            gu_ref[nb // 2 + j] = up.astype(ACT)
            a_ref[j] = (gate * _sigmoid(gate) * up).astype(ACT)

    slab = lambda n: pl.BlockSpec((n, tm, c), lambda i: (0, i, 0))
    return _call(
        body, "ffn_up_fwd", (t // tm,),
        [_rows(tm, d), _full((1, d)), _full(w.shape)],
        [_rows(tm, d), slab(nb), slab(nb // 2)],
        [_sds((t, d), ACT), _sds((nb, t, c), ACT), _sds((nb // 2, t, c), ACT)],
        (x, g, w), ("parallel",), jobs=jobs)


def _sgu_in_fwd(x, g, w, jobs=()):
    t, d = x.shape
    nb, _, c = w.shape
    n = nb * c
    tm = _row_tile(t)

    def body(x_ref, g_ref, w_ref, h_ref, z_ref):
        xv = x_ref[...]
        hb = (xv * _rstd(xv) * g_ref[...]).astype(ACT)
        h_ref[...] = hb
        for j in range(nb):
            z_ref[:, j * c:(j + 1) * c] = _dot(hb, w_ref[j])

    return _call(
        body, "sgu_in_fwd", (t // tm,), [_rows(tm, d), _full((1, d)), _full((nb, d, c))],
        [_rows(tm, d), _rows(tm, n)], [_sds((t, d), ACT), _sds((t, n), F32)], (x, g, w), ("parallel",), jobs=jobs)


def _causal(ws):
    row = lax.broadcasted_iota(jnp.int32, ws.shape, 0)
    col = lax.broadcasted_iota(jnp.int32, ws.shape, 1)
    return jnp.where(col <= row, ws, 0.0)


def _sgu_ln(v, lg, lb):
    mu = jnp.mean(v, axis=-1, keepdims=True)
    vc = v - mu
    rs = lax.rsqrt(jnp.mean(vc * vc, axis=-1, keepdims=True) + EPS)
    vh = vc * rs
    return vh, rs, vh * lg + lb


def _sgu_mix_fwd(z, lg, lb, ws, bs_t, jobs=()):
    t, n = z.shape
    d = n // 2
    groups = d // WINDOW
    tm = _row_tile(t)

    def body(z_ref, lg_ref, lb_ref, ws_ref, bs_ref, y_ref):
        for c in range(0, tm, WINDOW):
            u = _gelu(z_ref[c:c + WINDOW, :d])
            _, _, vn = _sgu_ln(_gelu(z_ref[c:c + WINDOW, d:]), lg_ref[...], lb_ref[...])
            for gi in range(groups):
                gs = slice(gi * WINDOW, (gi + 1) * WINDOW)
                mixed = _dot(_causal(ws_ref[gi]).astype(ACT), vn[:, gs].astype(ACT)) + bs_ref[:, gi:gi + 1]
                y_ref[c:c + WINDOW, gs] = (u[:, gs] * mixed).astype(ACT)

    return _call(
        body, "sgu_mix_fwd", (t // tm,),
        [_rows(tm, n), _full((1, d)), _full((1, d)), _full((groups, WINDOW, WINDOW)), _full((WINDOW, groups))],
        [_rows(tm, d)], [_sds((t, d), ACT)], (z, lg, lb, ws, bs_t), ("parallel",), jobs=jobs)


def _sgu_mix_bwd(z, dy, lg, lb, ws, bs_t, jobs=()):
    t, n = z.shape
    d = n // 2
    groups = d // WINDOW
    tm = _row_tile(t)

    def body(z_ref, dy_ref, lg_ref, lb_ref, ws_ref, bs_ref, dz_ref, dlg_ref, dlb_ref, dws_ref, dbs_ref):
        @pl.when(pl.program_id(0) == 0)
        def _():
            for ref in (dlg_ref, dlb_ref, dws_ref, dbs_ref):
                ref[...] = jnp.zeros_like(ref)

        lane = lax.broadcasted_iota(jnp.int32, (WINDOW, groups), 1)
        for c in range(0, tm, WINDOW):
            rows = slice(c, c + WINDOW)
            zu, zv = z_ref[rows, :d], z_ref[rows, d:]
            u = _gelu(zu)
            vh, rs, vn = _sgu_ln(_gelu(zv), lg_ref[...], lb_ref[...])
            dyv = dy_ref[rows, :]
            dvn_parts = []
            for gi in range(groups):
                gs = slice(gi * WINDOW, (gi + 1) * WINDOW)
                wsg = _causal(ws_ref[gi]).astype(ACT)
                vng = vn[:, gs].astype(ACT)
                mixed = _dot(wsg, vng) + bs_ref[:, gi:gi + 1]
                dz_ref[rows, gs] = (dyv[:, gs] * mixed * _gelu_grad(zu[:, gs])).astype(ACT)
                dmix = dyv[:, gs] * u[:, gs]
                dmb = dmix.astype(ACT)
                dvn_parts.append(_dot_tn(wsg, dmb))
                dws_ref[gi] += _dot_nt(dmb, vng)
                dbs_ref[...] += jnp.where(lane == gi, jnp.sum(dmix, axis=-1, keepdims=True), 0.0)
            dvn = jnp.concatenate(dvn_parts, axis=-1)
            dlg_ref[...] += jnp.sum(dvn * vh, axis=0, keepdims=True)
            dlb_ref[...] += jnp.sum(dvn, axis=0, keepdims=True)
            dvh = dvn * lg_ref[...]
            dv = rs * (dvh - jnp.mean(dvh, axis=-1, keepdims=True) - vh * jnp.mean(dvh * vh, axis=-1, keepdims=True))
            dz_ref[rows, d:] = (dv * _gelu_grad(zv)).astype(ACT)

    vec = _full((1, d))
    return _call(
        body, "sgu_mix_bwd", (t // tm,),
        [_rows(tm, n), _rows(tm, d), vec, vec, _full((groups, WINDOW, WINDOW)), _full((WINDOW, groups))],
        [_rows(tm, n), vec, vec, _full((groups, WINDOW, WINDOW)), _full((WINDOW, groups))],
        [_sds((t, n), ACT), _sds((1, d), F32), _sds((1, d), F32), _sds((groups, WINDOW, WINDOW), F32), _sds((WINDOW, groups), F32)],
        (z, dy, lg, lb, ws, bs_t), ("arbitrary",), jobs=jobs)


AG_COPIES = 7


def _prepare(sources, dtypes, n_gather, name):
    n = len(sources)
    arrays, where = [], []
    for parts, layer in sources:
        found = []
        for part in parts:
            known = [i for i, v in enumerate(arrays) if v is part]
            if not known:
                arrays.append(part)
            found.append(known[0] if known else len(arrays) - 1)
        where.append((found, layer))
    shapes = [(sum(p.shape[1] for p in parts), parts[0].shape[2]) for parts, _ in sources]

    def body(*refs):
        ins, refs = refs[:len(arrays)], refs[len(arrays):]
        stage, outs = refs[:n], refs[n:n + n_gather]
        send, recv, local_sem = refs[n + n_gather:]
        x, y, c = _place()
        sibling = (x, y, 1 - c)
        chips = _other_chips(x, y)
        mine = 4 * x + 2 * y + c

        def copy(a, k, block, to, src=None):
            dst = outs[a].at[block]
            return pltpu.make_async_remote_copy(
                src_ref=dst if src is None else src, dst_ref=dst, send_sem=send.at[a * AG_COPIES + k],
                recv_sem=recv.at[a * AG_COPIES + k], device_id=to, device_id_type=MESH)

        for a, (found, layer) in enumerate(where):
            at = 0
            for i in found:
                rows = arrays[i].shape[1]
                stage[a][at:at + rows, :] = ins[i][layer].astype(dtypes[a])
                at += rows
        started = []
        for a in range(n_gather):
            own = pltpu.make_async_copy(stage[a], outs[a].at[mine], local_sem.at[a])
            own.start()
            started.append(own)
        sends = []
        for a in range(n_gather):
            sends.append(copy(a, 0, mine, sibling, src=stage[a]))
            sends += [copy(a, 1 + j, mine, (*chip, c), src=stage[a]) for j, chip in enumerate(chips)]
        for cp in sends:
            cp.start()
        for j, (px, py) in enumerate(chips):
            block = 4 * px + 2 * py + c
            for a in range(n_gather):
                copy(a, 1 + j, block, (x, y, c)).wait_recv()
                forward = copy(a, 4 + j, block, sibling)
                forward.start()
                sends.append(forward)
        for a in range(n_gather):
            copy(a, 0, 4 * x + 2 * y + (1 - c), (x, y, c)).wait_recv()
            for j, (px, py) in enumerate(chips):
                copy(a, 4 + j, 4 * px + 2 * py + (1 - c), (x, y, c)).wait_recv()
        for cp in sends:
            cp.wait_send()
        for own in started:
            own.wait()

    res = pl.pallas_call(
        body, name=name,
        in_specs=[IN_VMEM] * len(arrays), out_specs=[IN_VMEM] * n + [ANY] * n_gather,
        out_shape=[_sds(s, dt) for s, dt in zip(shapes, dtypes)]
        + [_sds((N_DEV,) + s, dt) for s, dt in zip(shapes[:n_gather], dtypes)],
        scratch_shapes=[pltpu.SemaphoreType.DMA((n_gather * AG_COPIES,)), pltpu.SemaphoreType.DMA((n_gather * AG_COPIES,)),
                        pltpu.SemaphoreType.DMA((n_gather,))],
        compiler_params=pltpu.CompilerParams(vmem_limit_bytes=VMEM_LIMIT_BYTES),
    )(*arrays)
    return list(res[:n]), list(res[n:])


def _pair_sum(g, r, core, name):
    _, _, rows, cols = g.shape
    tr = _pick(rows, (512, 256, 128))

    def body(core_ref, g_ref, r_ref, p_ref):
        del core_ref
        p_ref[...] = (g_ref[...].astype(F32) + r_ref[...].astype(F32)).astype(p_ref.dtype)

    return pl.pallas_call(
        body, name=name,
        grid_spec=pltpu.PrefetchScalarGridSpec(
            num_scalar_prefetch=1, grid=(4, rows // tr),
            in_specs=[pl.BlockSpec((None, None, tr, cols), lambda q, i, core_ref: (q, core_ref[0], i, 0)),
                      pl.BlockSpec((None, tr, cols), lambda q, i, core_ref: (q, i, 0))],
            out_specs=pl.BlockSpec((None, tr, cols), lambda q, i, core_ref: (q, i, 0))),
        out_shape=_sds((4, rows, cols), g.dtype),
        compiler_params=_params("parallel", "parallel"),
    )(core, g, r)


def _adam(w, g, m, v):
    m2 = ADAM_B1 * m + (1.0 - ADAM_B1) * g
    v2 = ADAM_B2 * v + (1.0 - ADAM_B2) * (g * g)
    m_hat = m2 / (1.0 - ADAM_B1 ** ADAM_STEP)
    v_hat = v2 / (1.0 - ADAM_B2 ** ADAM_STEP)
    return -ADAM_LR * (m_hat / (jnp.sqrt(v_hat) + ADAM_EPS) + ADAM_WD * w), m2, v2


def _shard_update(own, index, received, w, m, v, layer, so_far, name):
    _, rows, cols = w.shape
    n_recv = received.shape[0]
    tr = _pick(rows, (512, 256, 128))

    def body(index_ref, p_ref, q_ref, w_ref, m_ref, v_ref, *rest):
        del index_ref
        g_out, d_out, m_out, v_out = rest[-4:]
        g = p_ref[...].astype(F32)
        for s in range(n_recv):
            g = g + q_ref[s].astype(F32)
        g_out[...] = g
        d_out[...], m_out[...], v_out[...] = _adam(w_ref[...], g, m_ref[...], v_ref[...])

    tile = pl.BlockSpec((None, tr, cols), lambda i, index_ref: (layer, i, 0))
    kept = list(so_far) if so_far is not None else []
    return pl.pallas_call(
        body, name=name,
        grid_spec=pltpu.PrefetchScalarGridSpec(
            num_scalar_prefetch=1, grid=(rows // tr,),
            in_specs=[pl.BlockSpec((None, tr, cols), lambda i, index_ref: (index_ref[0], i, 0)),
                      pl.BlockSpec((n_recv, tr, cols), lambda i, index_ref: (0, i, 0)), tile, tile, tile] + [ANY] * len(kept),
            out_specs=[tile] * 4),
        out_shape=[_sds(w.shape, F32)] * 4,
        input_output_aliases={6 + i: i for i in range(len(kept))},
        compiler_params=_params("parallel"),
    )(index, own, received, w, m, v, *kept)


def _replicated_update(shares, w, m, v):
    rows, cols = w.shape

    def body(s_ref, w_ref, m_ref, v_ref, g_out, d_out, m_out, v_out):
        g = s_ref[0]
        for dev in range(1, N_DEV):
            g = g + s_ref[dev]
        g_out[...] = g
        d_out[...], m_out[...], v_out[...] = _adam(w_ref[...], g, m_ref[...], v_ref[...])

    return pl.pallas_call(
        body, name="replicated_update_%d" % rows, out_shape=[_sds((rows, cols), F32)] * 4,
        compiler_params=pltpu.CompilerParams(vmem_limit_bytes=VMEM_LIMIT_BYTES),
    )(shares, w, m, v)


def _rope_tables(t):
    half = HEAD_DIM // 2
    inv_freq = ROPE_THETA ** (-(jnp.arange(half, dtype=F32) * 2.0) / HEAD_DIM)
    ang = jnp.arange(t, dtype=jnp.int32).astype(F32)[:, None] * inv_freq[None, :]
    cos, sin = jnp.cos(ang), jnp.sin(ang)
    return jnp.tile(jnp.concatenate([cos, cos], axis=1), (1, 2)), jnp.tile(jnp.concatenate([-sin, sin], axis=1), (1, 2))


def _dup_heads(w, d):
    kv = (w.shape[-1] - d) // 2
    lead = w.shape[:-1]

    def dup(part):
        heads = part.reshape(lead + (kv // HEAD_DIM, 1, HEAD_DIM))
        return jnp.broadcast_to(heads, lead + (kv // HEAD_DIM, 2, HEAD_DIM)).reshape(lead + (2 * kv,))

    return jnp.concatenate([w[..., :d], dup(w[..., d:d + kv]), dup(w[..., d + kv:])], axis=-1)


def _fold_heads(dw, d):
    kv = (dw.shape[-1] - d) // 4
    lead = dw.shape[:-1]

    def fold(part):
        return part.reshape(lead + (kv // HEAD_DIM, 2, HEAD_DIM)).sum(axis=-2).reshape(lead + (kv,))

    return jnp.concatenate([dw[..., :d], fold(dw[..., d:d + 2 * kv]), fold(dw[..., d + 2 * kv:])], axis=-1)


def _columns_full(gathered):
    _, rows, c = gathered.shape
    return jnp.transpose(gathered, (1, 0, 2)).reshape(rows, N_DEV * c)


def _rows_full(gathered):
    return gathered.reshape(-1, gathered.shape[-1])


def _columns_blocked(full):
    rows, cols = full.shape
    return jnp.transpose(full.reshape(rows, N_DEV, cols // N_DEV), (1, 0, 2)).astype(ACT)


def _rows_blocked(full):
    return full.reshape(N_DEV, full.shape[0] // N_DEV, full.shape[1]).astype(ACT)


def _packed_rows(size, width):
    return -(-size // (8 * width)) * 8


def _pack_rows(parts, width):
    rows = []
    for v in parts:
        flat = v.reshape(-1).astype(F32)
        n_rows = _packed_rows(flat.shape[0], width)
        rows.append(jnp.pad(flat, (0, n_rows * width - flat.shape[0])).reshape(n_rows, width))
    return jnp.concatenate(rows, axis=0)


def _unpack_rows(packed, like, width):
    out, at = [], 0
    for v in like:
        size = math.prod(v.shape)
        out.append(packed[at:at + -(-size // width)].reshape(-1)[:size].reshape(v.shape))
        at += _packed_rows(size, width)
    return out


def _halves(block):
    half = block.shape[0] // 2
    return (0, half), (half, half)


def _whole(block):
    return (0, block.shape[0])


EARLY = ("attn_w_qkv", "sgu_ln", "attn_w_o")
LATE = ("sgu_w_in", "sgu_w_out", "gu0", "gu1", "dn0", "dn1")
COLUMN_SHARDED = ("attn_w_qkv", "sgu_w_in", "gu0", "gu1")
BEFORE_ATTN = ("norm_mix_post", "norm_ffn_pre", "norm_ffn_post", "attn_b_o", "sgu_w_spatial", "sgu_b_spatial")
AFTER_ATTN = ("attn_b_qkv", "attn_sinks")
LAST = ("norm_mix_pre",)
WEIGHTS = ("norm_mix_pre", "norm_mix_post", "norm_ffn_pre", "norm_ffn_post", "attn_w_qkv", "attn_b_qkv", "attn_sinks", "attn_w_o",
           "attn_b_o", "sgu_w_in", "sgu_ln_g", "sgu_ln_b", "sgu_w_spatial", "sgu_b_spatial", "sgu_w_out", "ffn_w_gate_up", "ffn_w_down")


LAYER_OF = {"gu0": ("ffn_w_gate_up", 0), "gu1": ("ffn_w_gate_up", 1), "dn0": ("ffn_w_down", 0), "dn1": ("ffn_w_down", 1)}


def _source(tree, name):
    if name == "sgu_ln":
        return [tree["sgu_ln_g"][None], tree["sgu_ln_b"][None]], 0
    leaf, layer = LAYER_OF.get(name, (name, 0))
    return [tree[leaf]], layer


def kernel(x, norm_mix_pre, norm_mix_post, norm_ffn_pre, norm_ffn_post, attn_w_qkv, attn_b_qkv, attn_sinks, attn_w_o, attn_b_o, sgu_w_in, sgu_ln_g, sgu_ln_b, sgu_w_spatial, sgu_b_spatial, sgu_w_out, ffn_w_gate_up, ffn_w_down, loss_target, m_norm_mix_pre, m_norm_mix_post, m_norm_ffn_pre, m_norm_ffn_post, m_attn_w_qkv, m_attn_b_qkv, m_attn_sinks, m_attn_w_o, m_attn_b_o, m_sgu_w_in, m_sgu_ln_g, m_sgu_ln_b, m_sgu_w_spatial, m_sgu_b_spatial, m_sgu_w_out, m_ffn_w_gate_up, m_ffn_w_down, v_norm_mix_pre, v_norm_mix_post, v_norm_ffn_pre, v_norm_ffn_post, v_attn_w_qkv, v_attn_b_qkv, v_attn_sinks, v_attn_w_o, v_attn_b_o, v_sgu_w_in, v_sgu_ln_g, v_sgu_ln_b, v_sgu_w_spatial, v_sgu_b_spatial, v_sgu_w_out, v_ffn_w_gate_up, v_ffn_w_down):
    w = dict(norm_mix_pre=norm_mix_pre, norm_mix_post=norm_mix_post, norm_ffn_pre=norm_ffn_pre, norm_ffn_post=norm_ffn_post,
             attn_w_qkv=attn_w_qkv, attn_b_qkv=attn_b_qkv, attn_sinks=attn_sinks, attn_w_o=attn_w_o, attn_b_o=attn_b_o,
             sgu_w_in=sgu_w_in, sgu_ln_g=sgu_ln_g, sgu_ln_b=sgu_ln_b, sgu_w_spatial=sgu_w_spatial, sgu_b_spatial=sgu_b_spatial,
             sgu_w_out=sgu_w_out, ffn_w_gate_up=ffn_w_gate_up, ffn_w_down=ffn_w_down)
    mom = dict(norm_mix_pre=m_norm_mix_pre, norm_mix_post=m_norm_mix_post, norm_ffn_pre=m_norm_ffn_pre, norm_ffn_post=m_norm_ffn_post,
               attn_w_qkv=m_attn_w_qkv, attn_b_qkv=m_attn_b_qkv, attn_sinks=m_attn_sinks, attn_w_o=m_attn_w_o, attn_b_o=m_attn_b_o,
               sgu_w_in=m_sgu_w_in, sgu_ln_g=m_sgu_ln_g, sgu_ln_b=m_sgu_ln_b, sgu_w_spatial=m_sgu_w_spatial,
               sgu_b_spatial=m_sgu_b_spatial, sgu_w_out=m_sgu_w_out, ffn_w_gate_up=m_ffn_w_gate_up, ffn_w_down=m_ffn_w_down)
    var = dict(norm_mix_pre=v_norm_mix_pre, norm_mix_post=v_norm_mix_post, norm_ffn_pre=v_norm_ffn_pre, norm_ffn_post=v_norm_ffn_post,
               attn_w_qkv=v_attn_w_qkv, attn_b_qkv=v_attn_b_qkv, attn_sinks=v_attn_sinks, attn_w_o=v_attn_w_o, attn_b_o=v_attn_b_o,
               sgu_w_in=v_sgu_w_in, sgu_ln_g=v_sgu_ln_g, sgu_ln_b=v_sgu_ln_b, sgu_w_spatial=v_sgu_w_spatial,
               sgu_b_spatial=v_sgu_b_spatial, sgu_w_out=v_sgu_w_out, ffn_w_gate_up=v_ffn_w_gate_up, ffn_w_down=v_ffn_w_down)
    w, mom, var = [{**tree, "ffn_w_gate_up": jnp.swapaxes(tree["ffn_w_gate_up"], 1, 2)} for tree in (w, mom, var)]
    xs, target = x[0], loss_target[0]
    t, d = xs.shape
    row = lambda v: v.reshape(1, -1).astype(F32)
    core = lax.axis_index("c").astype(jnp.int32).reshape(1)
    chip = (2 * lax.axis_index("x") + lax.axis_index("y")).astype(jnp.int32).reshape(1)
    me = (4 * lax.axis_index("x") + 2 * lax.axis_index("y") + lax.axis_index("c")).astype(jnp.int32).reshape(1)
    names = EARLY + LATE
    staged, early = _prepare([_source(w, n) for n in names], [F32 if n == "sgu_ln" else ACT for n in names], len(EARLY),
                             "weights_prepare")
    stage = dict(zip(names, staged))
    w_qkv = _dup_heads(_columns_full(early[0]), d)
    lg, lb = row(early[1][:, 0, :]), row(early[1][:, 1, :])
    w_o = _rows_full(early[2])
    b_qkv = _dup_heads(row(attn_b_qkv), d)
    sinks = attn_sinks.reshape(1, -1).astype(F32)
    ws = sgu_w_spatial[0].astype(F32)
    bs_t = sgu_b_spatial[0].astype(F32).T
    cos, sin = _rope_tables(t)
    gather = lambda name, so_far, **pieces: _gather_job(stage[name], so_far, **pieces)
    a_half = lambda name: _halves(stage[name])[0]
    b_half = lambda name: _halves(stage[name])[1]
    whole = lambda name: _whole(stage[name])

    quarter = lambda name, i: (i * stage[name].shape[0] // 4, stage[name].shape[0] // 4)
    (h0, q, kdup, vdup), ((g_gu0,),) = _attn_qkv_fwd(
        xs, row(norm_mix_pre[0]), w_qkv, b_qkv, cos, sin, jobs=[gather("gu0", None, sends=[quarter("gu0", 0)])])
    (o,), ((g_gu0,),) = _attn_fwd(q, kdup, vdup, sinks, jobs=[
        gather("gu0", g_gu0, sends=[quarter("gu0", i) for i in (1, 2, 3)], forwards=[quarter("gu0", 0)])])
    (m0, x1), ((g_gu0,), (g_dn0,)) = _proj_res(o, w_o, row(attn_b_o), row(norm_mix_post[0]), xs, "attn_out_fwd", jobs=[
        gather("gu0", g_gu0, forwards=[quarter("gu0", i) for i in (1, 2, 3)]), gather("dn0", None, sends=[whole("dn0")])])
    w_gu0 = g_gu0
    (h1, gu0, a0), ((g_dn0,), (g_in,), (g_gu1,)) = _ffn_up_fwd(x1, row(norm_ffn_pre[0]), w_gu0, jobs=[
        gather("dn0", g_dn0, forwards=[whole("dn0")]), gather("sgu_w_in", None, sends=[whole("sgu_w_in")]),
        gather("gu1", None, sends=[quarter("gu1", 0), quarter("gu1", 1)])])
    w_dn0 = _rows_full(g_dn0)
    (f0, x2), ((g_in,), (g_out,), (g_gu1,)) = _proj_res(a0, w_dn0, None, row(norm_ffn_post[0]), x1, "ffn_down_fwd0", jobs=[
        gather("sgu_w_in", g_in, forwards=[whole("sgu_w_in")]), gather("sgu_w_out", None, sends=[whole("sgu_w_out")]),
        gather("gu1", g_gu1, sends=[quarter("gu1", 2)], forwards=[quarter("gu1", 0), quarter("gu1", 1)])])
    w_in = g_in
    (h2, z), ((g_out,), (g_gu1,)) = _sgu_in_fwd(x2, row(norm_mix_pre[1]), w_in, jobs=[
        gather("sgu_w_out", g_out, forwards=[whole("sgu_w_out")]),
        gather("gu1", g_gu1, sends=[quarter("gu1", 3)], forwards=[quarter("gu1", 2)])])
    (y,), ((g_gu1,), (g_dn1,)) = _sgu_mix_fwd(z, lg, lb, ws, bs_t, jobs=[
        gather("gu1", g_gu1, forwards=[quarter("gu1", 3)]), gather("dn1", None, sends=[a_half("dn1")])])
    w_out = _rows_full(g_out)
    (m1, x3), ((g_dn1,),) = _proj_res(y, w_out, None, row(norm_mix_post[1]), x2, "sgu_out_fwd", jobs=[
        gather("dn1", g_dn1, sends=[b_half("dn1")], forwards=[a_half("dn1")])])
    w_gu1 = g_gu1
    (h3, gu1, a1), ((g_dn1,),) = _ffn_up_fwd(x3, row(norm_ffn_pre[1]), w_gu1, jobs=[gather("dn1", g_dn1, forwards=[b_half("dn1")])])
    w_dn1 = _rows_full(g_dn1)
    (f1, dx4, loss_tile), _ = _proj_res(a1, w_dn1, None, row(norm_ffn_post[1]), x3, "ffn_down_fwd1", target=target)

    to_sibling = lambda g: _scatter_job([g], 4, _to_sibling)
    pair = lambda g, r, name: _pair_sum(g.reshape((4, 2) + g.shape[1:]), r, core, "pair_sum_" + name)
    to_chips = lambda p, prev=None, piece=None: _scatter_job([p], 3, _to_owner_chip, prev=prev, piece=piece)
    out_g, out_d, out_m, out_v = {}, {}, {}, {}

    trees = [{**tree, "sgu_ln": jnp.stack([tree["sgu_ln_g"], tree["sgu_ln_b"]], axis=1)} for tree in (w, mom, var)]
    so_far = {}

    def update(name, own, index, received):
        leaf, layer = LAYER_OF.get(name, (name, 0))
        so_far[leaf] = _shard_update(own, index, received, *[tree[leaf] for tree in trees], layer, so_far.get(leaf), "update_" + name)
        for out, val in zip((out_g, out_d, out_m, out_v), so_far[leaf]):
            out[leaf] = val

    def finish(names, extra, shares):
        like = [w[name] for name in names] + extra
        packed = [_pack_rows([tree[name] for name in names] + [jnp.zeros_like(e) for e in extra], d) for tree in (w, mom, var)]
        res = _replicated_update(shares, *packed)
        for out, val in zip((out_g, out_d, out_m, out_v), res):
            for name, leaf in zip(names, _unpack_rows(val, like, d)):
                out[name] = leaf
        return _unpack_rows(res[0], like, d)[len(names):]

    first_half = lambda p: _halves(p[0])[0]
    second_half = lambda p: _halves(p[0])[1]
    (df1, dgu1, dg_ffn_post1), _ = _post_bwd(dx4, f1, row(norm_ffn_post[1]), w_dn1, "ffn", "ffn_down_bwd1", gu1)
    b_gu1, _ = _wgrad(h3, dgu1, "ffn_up_wgrad1", ACT, transposed=True)
    dw_dn1, ((r_gu1,),) = _wgrad(a1, df1, "ffn_down_wgrad1", ACT, jobs=[to_sibling(b_gu1)])
    b_dn1 = _rows_blocked(dw_dn1)
    p_gu1 = pair(b_gu1, r_gu1, "gu1")
    (dx3, dg_ffn_pre1), ((q_gu1,), (r_dn1,)) = _pre_bwd(dgu1, w_gu1, x3, row(norm_ffn_pre[1]), dx4, "ffn_up_bwd1", True, jobs=[
        to_chips(p_gu1, piece=first_half(p_gu1)), to_sibling(b_dn1)])
    p_dn1 = pair(b_dn1, r_dn1, "dn1")
    (dm1, dy, dg_mix_post1), _ = _post_bwd(dx3, m1, row(norm_mix_post[1]), w_out, "sgu", "sgu_out_bwd")
    dw_out, _ = _wgrad(y, dm1, "sgu_out_wgrad", ACT)
    b_out = _rows_blocked(dw_out)
    (dz, dlg, dlb, dws, dbs_t), ((q_gu1,), (r_out,)) = _sgu_mix_bwd(z, dy, lg, lb, ws, bs_t, jobs=[
        to_chips(p_gu1, prev=[q_gu1], piece=second_half(p_gu1)), to_sibling(b_out)])
    update("gu1", p_gu1, chip, q_gu1)
    p_out = pair(b_out, r_out, "sgu_w_out")
    b_in, ((q_out,),) = _wgrad(h2, dz, "sgu_in_wgrad", ACT, out_block=stage["sgu_w_in"].shape[1], jobs=[to_chips(p_out)])
    update("sgu_w_out", p_out, chip, q_out)
    b_ln = jnp.stack([dlg.reshape(N_DEV, -1), dlb.reshape(N_DEV, -1)], axis=1)
    (dx2, dg_mix_pre1), ((r_in,), (r_ln,)) = _pre_bwd(dz, w_in, x2, row(norm_mix_pre[1]), dx3, "sgu_in_bwd",
                                                      jobs=[to_sibling(b_in), to_sibling(b_ln)])
    p_in, p_ln = pair(b_in, r_in, "sgu_w_in"), pair(b_ln, r_ln, "sgu_ln")
    (df0, dgu0, dg_ffn_post0), ((q_dn1,),) = _post_bwd(dx2, f0, row(norm_ffn_post[0]), w_dn0, "ffn", "ffn_down_bwd0", gu0,
                                                      jobs=[to_chips(p_dn1)])
    update("dn1", p_dn1, chip, q_dn1)
    b_gu0, ((q_in,), (q_ln,)) = _wgrad(h1, dgu0, "ffn_up_wgrad0", ACT, transposed=True, jobs=[to_chips(p_in), to_chips(p_ln)])
    update("sgu_w_in", p_in, chip, q_in)
    update("sgu_ln", p_ln, chip, q_ln)
    dw_dn0, ((r_gu0,),) = _wgrad(a0, df0, "ffn_down_wgrad0", ACT, jobs=[to_sibling(b_gu0)])
    b_dn0 = _rows_blocked(dw_dn0)
    p_gu0 = pair(b_gu0, r_gu0, "gu0")
    (dx1, dg_ffn_pre0), ((q_gu0,), (r_dn0,)) = _pre_bwd(dgu0, w_gu0, x1, row(norm_ffn_pre[0]), dx2, "ffn_up_bwd0", True, jobs=[
        to_chips(p_gu0, piece=first_half(p_gu0)), to_sibling(b_dn0)])
    p_dn0 = pair(b_dn0, r_dn0, "dn0")
    (dm0, do, dg_mix_post0, db_o), _ = _post_bwd(dx1, m0, row(norm_mix_post[0]), w_o, "attn", "attn_out_bwd")
    dw_o, _ = _wgrad(o, dm0, "attn_out_wgrad", ACT)
    b_o = _rows_blocked(dw_o)
    grads = {
        "norm_mix_post": jnp.concatenate([dg_mix_post0, dg_mix_post1], axis=0),
        "norm_ffn_pre": jnp.concatenate([dg_ffn_pre0, dg_ffn_pre1], axis=0),
        "norm_ffn_post": jnp.concatenate([dg_ffn_post0, dg_ffn_post1], axis=0),
        "attn_b_o": db_o,
        "sgu_w_spatial": dws * jnp.tril(jnp.ones((WINDOW, WINDOW), F32))[None],
        "sgu_b_spatial": dbs_t.T,
    }
    shares1 = _pack_rows([grads[name] for name in BEFORE_ATTN], d)
    (dqkv, db_qkv, dsink), ((q_gu0,), (q_dn0,), (r_o,), (g_shares1,)) = _attn_bwd(q, kdup, vdup, do, sinks, cos, sin, jobs=[
        to_chips(p_gu0, prev=[q_gu0], piece=second_half(p_gu0)), to_chips(p_dn0), to_sibling(b_o),
        _gather_job(shares1, None, sends=[_whole(shares1)])])
    update("gu0", p_gu0, chip, q_gu0)
    update("dn0", p_dn0, chip, q_dn0)
    p_o = pair(b_o, r_o, "attn_w_o")
    grads.update({"attn_b_qkv": _fold_heads(db_qkv, d), "attn_sinks": dsink[:1, :d // HEAD_DIM]})
    shares2 = _pack_rows([grads[name] for name in AFTER_ATTN] + [loss_tile[:1, :1]], d)
    dw_qkv, ((q_o,), (g_shares1,), (g_shares2,)) = _wgrad(h0, dqkv, "attn_qkv_wgrad", jobs=[
        to_chips(p_o), _gather_job(shares1, g_shares1, forwards=[_whole(shares1)]),
        _gather_job(shares2, None, sends=[_whole(shares2)])])
    update("attn_w_o", p_o, chip, q_o)
    b_qkv_grad = _columns_blocked(_fold_heads(dw_qkv, d))
    (dx0, dg_mix_pre0), ((q_qkv,), (g_shares2,)) = _pre_bwd(dqkv, w_qkv, xs, row(norm_mix_pre[0]), dx1, "attn_qkv_bwd", jobs=[
        _scatter_job([b_qkv_grad], N_DEV - 1, _to_owner), _gather_job(shares2, g_shares2, forwards=[_whole(shares2)])])
    update("attn_w_qkv", b_qkv_grad, me, q_qkv)

    finish(BEFORE_ATTN, [], g_shares1)
    loss = finish(AFTER_ATTN, [loss_tile[:1, :1]], g_shares2)[0][0, 0]
    grads["norm_mix_pre"] = jnp.concatenate([dg_mix_pre0, dg_mix_pre1], axis=0)
    _, (last_shares,) = _prepare([([_pack_rows([grads[name] for name in LAST], d)[None]], 0)], [F32], 1, "last_grads_all_gather")
    finish(LAST, [], last_shares)

    for out in (out_g, out_d, out_m, out_v):
        out["sgu_ln_g"], out["sgu_ln_b"] = out["sgu_ln"][:, 0, :], out["sgu_ln"][:, 1, :]
        out["ffn_w_gate_up"] = jnp.swapaxes(out["ffn_w_gate_up"], 1, 2)

    return (loss, dx0[None], *[out_g[n] for n in WEIGHTS], *[out_d[n] for n in WEIGHTS],
            *[out_m[n] for n in WEIGHTS], *[out_v[n] for n in WEIGHTS])
```

```python
import functools
import math
import operator

import jax
import jax.numpy as jnp
from jax import lax
from jax.experimental import pallas as pl
from jax.experimental.pallas import tpu as pltpu

F32 = jnp.float32
ACT = jnp.bfloat16

EPS = 1e-6
HEAD_DIM = 64
PAIR = 2 * HEAD_DIM
GQA_GROUP = 4
WINDOW = 128
ROPE_THETA = 10000.0
SCALE = HEAD_DIM ** -0.5
MASKED = -1e30
LANES = 128

ADAM_LR, ADAM_B1, ADAM_B2, ADAM_EPS, ADAM_WD, ADAM_STEP = 0.001, 0.9, 0.999, 1e-08, 0.01, 10

N_DEV = 8
VMEM_LIMIT_BYTES = 56 * 1024 * 1024
MESH = pl.DeviceIdType.MESH
ANY = pl.BlockSpec(memory_space=pl.ANY)
IN_VMEM = pl.BlockSpec(memory_space=pltpu.VMEM)


def _pick(n, candidates):
    for c in candidates:
        if n % c == 0:
            return c
    return n


def _row_tile(t):
    return _pick(t, (512,))


def _params(*sem):
    return pltpu.CompilerParams(dimension_semantics=sem, vmem_limit_bytes=VMEM_LIMIT_BYTES)


def _full(shape):
    return pl.BlockSpec(shape, lambda *_: (0,) * len(shape))


def _rows(tm, n):
    return pl.BlockSpec((tm, n), lambda i: (i, 0))


def _sds(shape, dtype):
    return jax.ShapeDtypeStruct(shape, dtype)


def _place():
    return lax.axis_index("x"), lax.axis_index("y"), lax.axis_index("c")


def _other_chips(x, y):
    return [(1 - x, y), (x, 1 - y), (1 - x, 1 - y)]


class _Job:
    def __init__(self, inputs, out_shape, aliases, emit, n_remote, n_local=0):
        self.inputs, self.out_shape, self.aliases, self.emit = list(inputs), list(out_shape), dict(aliases), emit
        self.n_remote, self.n_local = n_remote, n_local


def _call(body, name, grid, in_specs, out_specs, out_shape, args, sem, scratch=(), jobs=()):
    n_in, n_out, n_scr = len(args), len(out_shape), len(scratch)
    j_in = [a for job in jobs for a in job.inputs]
    j_out = [s for job in jobs for s in job.out_shape]
    aliases, at_in, at_out = {}, n_in, n_out
    for job in jobs:
        aliases.update({at_in + i: at_out + o for i, o in job.aliases.items()})
        at_in, at_out = at_in + len(job.inputs), at_out + len(job.out_shape)
    n_remote = sum(job.n_remote for job in jobs)
    n_local = sum(job.n_local for job in jobs)

    def wrapped(*refs):
        ins, jin = refs[:n_in], refs[n_in:n_in + len(j_in)]
        rest = refs[n_in + len(j_in):]
        outs, jout = rest[:n_out], rest[n_out:n_out + len(j_out)]
        scr = rest[n_out + len(j_out):n_out + len(j_out) + n_scr]
        if not jobs:
            body(*ins, *outs, *scr)
            return
        send, recv, local = rest[n_out + len(j_out) + n_scr:]
        ids = [pl.program_id(a) for a in range(len(grid))]
        first = functools.reduce(operator.and_, [i == 0 for i in ids])
        last = functools.reduce(operator.and_, [i == g - 1 for i, g in zip(ids, grid)])

        def descriptors():
            place, found, i, o, r, l = _place(), [], 0, 0, 0, 0
            for job in jobs:
                for src, dst, to in job.emit(jin[i:i + len(job.inputs)], jout[o:o + len(job.out_shape)], place):
                    if to is None:
                        found.append(pltpu.make_async_copy(src, dst, local.at[l]))
                        l += 1
                    else:
                        found.append(pltpu.make_async_remote_copy(src_ref=src, dst_ref=dst, send_sem=send.at[r], recv_sem=recv.at[r],
                                                                  device_id=to, device_id_type=MESH))
                        r += 1
                i, o = i + len(job.inputs), o + len(job.out_shape)
            return found

        @pl.when(first)
        def _():
            for cp in descriptors():
                cp.start()

        body(*ins, *outs, *scr)

        @pl.when(last)
        def _():
            for cp in descriptors():
                cp.wait()

    sems = [pltpu.SemaphoreType.DMA((n_remote,)), pltpu.SemaphoreType.DMA((n_remote,)),
            pltpu.SemaphoreType.DMA((max(n_local, 1),))] if jobs else []
    res = pl.pallas_call(
        wrapped, name=name, grid=grid,
        in_specs=list(in_specs) + [ANY] * len(j_in), out_specs=list(out_specs) + [ANY] * len(j_out),
        out_shape=list(out_shape) + j_out, scratch_shapes=list(scratch) + sems, input_output_aliases=aliases,
        compiler_params=_params(*(("arbitrary",) * len(grid) if jobs else sem)),
    )(*args, *j_in)
    job_res, at = [], n_out
    for job in jobs:
        job_res.append(list(res[at:at + len(job.out_shape)]))
        at += len(job.out_shape)
    return list(res[:n_out]), job_res


def _gather_job(stage, gathered, sends=(), forwards=()):
    shape = _sds((N_DEV,) + stage.shape, stage.dtype)
    inputs = ([stage] if sends else []) + ([gathered] if gathered is not None else [])
    aliases = {len(inputs) - 1: 0} if gathered is not None else {}

    def emit(ins, outs, place):
        x, y, c = place
        sibling, chips, mine, g = (x, y, 1 - c), _other_chips(x, y), 4 * x + 2 * y + c, outs[0]
        found = []
        for r0, nr in sends:
            src, dst = ins[0].at[pl.ds(r0, nr)], g.at[mine, pl.ds(r0, nr)]
            found += [(src, dst, None), (src, dst, sibling)] + [(src, dst, (px, py, c)) for px, py in chips]
        for r0, nr in forwards:
            for px, py in chips:
                block = 4 * px + 2 * py + c
                found.append((ins[-1].at[block, pl.ds(r0, nr)], g.at[block, pl.ds(r0, nr)], sibling))
        return found

    return _Job(inputs, [shape], aliases, emit, 4 * len(sends) + 3 * len(forwards), len(sends))


def _scatter_job(arrays, n_slots, route, prev=None, piece=None):
    shapes = [_sds((n_slots,) + v.shape[1:], v.dtype) for v in arrays]
    inputs = list(arrays) + (list(prev) if prev else [])
    aliases = {len(arrays) + i: i for i in range(len(arrays))} if prev else {}

    def emit(ins, outs, place):
        found = []
        for a in range(len(arrays)):
            for s in range(n_slots):
                block, to = route(s, *place)
                if piece is None:
                    found.append((ins[a].at[block], outs[a].at[s], to))
                else:
                    found.append((ins[a].at[block, pl.ds(*piece)], outs[a].at[s, pl.ds(*piece)], to))
        return found

    return _Job(inputs, shapes, aliases, emit, len(arrays) * n_slots)


def _to_sibling(s, x, y, c):
    return 2 * s + (1 - c), (x, y, 1 - c)


def _to_owner_chip(s, x, y, c):
    px, py = _other_chips(x, y)[s]
    return 2 * px + py, (px, py, c)


def _to_owner(s, x, y, c):
    if s == 0:
        px, py, pc = x, y, 1 - c
    else:
        px, py = _other_chips(x, y)[(s - 1) % 3]
        pc = c if s <= 3 else 1 - c
    return 4 * px + 2 * py + pc, (px, py, pc)


def _rstd(x):
    return lax.rsqrt(jnp.mean(x * x, axis=-1, keepdims=True) + EPS)


def _rms_bwd(xhat, r, g, dy):
    dxh = dy * g
    return r * (dxh - xhat * jnp.mean(dxh * xhat, axis=-1, keepdims=True))


def _first_half(rows):
    lane = lax.broadcasted_iota(jnp.int32, (rows, PAIR), 1)
    return (lane % HEAD_DIM) < (HEAD_DIM // 2)


def _rot_half(v, first):
    return jnp.where(first, pltpu.roll(v, PAIR - HEAD_DIM // 2, 1), pltpu.roll(v, HEAD_DIM // 2, 1))


def _rope(v, cos, sin, first):
    return v * cos + _rot_half(v, first) * sin


def _rope_t(dv, cos, sin, first):
    return dv * cos + _rot_half(dv * sin, first)


def _dot(a, b):
    return jnp.dot(a, b, preferred_element_type=F32)


def _dot_nt(a, b):
    return lax.dot_general(a, b, (((1,), (1,)), ((), ())), preferred_element_type=F32)


def _dot_tn(a, b):
    return lax.dot_general(a, b, (((0,), (0,)), ((), ())), preferred_element_type=F32)


def _sigmoid(v):
    return 1.0 / (1.0 + jnp.exp(-v))


_GELU_K = math.sqrt(2.0 / math.pi)
_GELU_C = 0.044715


def _gelu(v):
    return v * (0.5 * (1.0 + jnp.tanh(_GELU_K * (v + _GELU_C * (v * v * v)))))


def _gelu_grad(v):
    t = jnp.tanh(_GELU_K * (v + _GELU_C * (v * v * v)))
    return 0.5 * (1.0 + t) + 0.5 * v * (1.0 - t * t) * (_GELU_K * (1.0 + 3.0 * _GELU_C * (v * v)))


def _attn_qkv_fwd(x, g, w, b, cos, sin, jobs=()):
    t, d = x.shape
    n = w.shape[1]
    kd = (n - d) // 2
    tm = _row_tile(t)
    ch = _pick(d, (512,))

    def body(x_ref, g_ref, w_ref, b_ref, cos_ref, sin_ref, h_ref, q_ref, k_ref, v_ref):
        xv = x_ref[...]
        hb = (xv * _rstd(xv) * g_ref[...]).astype(ACT)
        h_ref[...] = hb
        cosv, sinv = cos_ref[...], sin_ref[...]
        first = _first_half(tm)

        def proj(lo, width):
            return _dot(hb, w_ref[:, lo:lo + width]) + b_ref[:, lo:lo + width]

        for c in range(0, d, ch):
            y = proj(c, ch)
            for s in range(0, ch, PAIR):
                q_ref[:, c + s:c + s + PAIR] = (_rope(y[:, s:s + PAIR], cosv, sinv, first) * SCALE).astype(ACT)
        y = proj(d, kd)
        for s in range(0, kd, PAIR):
            k_ref[:, s:s + PAIR] = _rope(y[:, s:s + PAIR], cosv, sinv, first).astype(ACT)
        v_ref[...] = proj(d + kd, kd).astype(ACT)

    return _call(
        body, "attn_qkv_fwd", (t // tm,),
        [_rows(tm, d), _full((1, d)), _full((d, n)), _full((1, n)), _rows(tm, PAIR), _rows(tm, PAIR)],
        [_rows(tm, d), _rows(tm, d), _rows(tm, kd), _rows(tm, kd)],
        [_sds((t, d), ACT), _sds((t, d), ACT), _sds((t, kd), ACT), _sds((t, kd), ACT)],
        (x, g, w, b, cos, sin), ("parallel",), jobs=jobs)


STACK = GQA_GROUP * WINDOW


def _band_mask(has_prev):
    row = lax.broadcasted_iota(jnp.int32, (STACK, 2 * WINDOW), 0) % WINDOW
    col = lax.broadcasted_iota(jnp.int32, (STACK, 2 * WINDOW), 1)
    return ((col < WINDOW) & (col > row) & has_prev) | ((col >= WINDOW) & (col - WINDOW <= row))


def _lane_halves():
    lane = lax.broadcasted_iota(jnp.int32, (1, PAIR), 1)
    return lane < HEAD_DIM, lane >= HEAD_DIM


def _stack_heads(ref, h, halves):
    parts = []
    for p in range(2):
        pair = ref[:, (2 * h + p) * PAIR:(2 * h + p + 1) * PAIR]
        parts += [jnp.where(half, pair, jnp.zeros_like(pair)) for half in halves]
    return jnp.concatenate(parts, axis=0)


def _sink_column(sink_ref, h):
    row = lax.broadcasted_iota(jnp.int32, (STACK, 1), 0)
    col = jnp.full((STACK, 1), sink_ref[0, GQA_GROUP * h + GQA_GROUP - 1], F32)
    for j in range(GQA_GROUP - 2, -1, -1):
        col = jnp.where(row < (j + 1) * WINDOW, sink_ref[0, GQA_GROUP * h + j], col)
    return col


def _probs(qs, k2, valid, sink):
    s = jnp.where(valid, _dot_nt(qs, k2), MASKED)
    m = jnp.maximum(jnp.max(s, axis=-1, keepdims=True), sink)
    p = jnp.exp(s - m)
    psink = jnp.exp(sink - m)
    inv = 1.0 / (jnp.sum(p, axis=-1, keepdims=True) + psink)
    return p * inv, psink * inv


def _attn_fwd(q, kd_, vd, sinks, jobs=()):
    t, d = q.shape
    kd = kd_.shape[1]
    cur = lambda n: (n, 0)
    prev = lambda n: (jnp.maximum(n - 1, 0), 0)

    def body(sink_ref, q_ref, kc_ref, kp_ref, vc_ref, vp_ref, o_ref):
        valid = _band_mask(pl.program_id(0) > 0)
        halves = _lane_halves()
        for h in range(kd // PAIR):
            hs = slice(h * PAIR, (h + 1) * PAIR)
            k2 = jnp.concatenate([kp_ref[:, hs], kc_ref[:, hs]], axis=0)
            v2 = jnp.concatenate([vp_ref[:, hs], vc_ref[:, hs]], axis=0)
            vs = jnp.concatenate([jnp.where(half, v2, jnp.zeros_like(v2)) for half in halves], axis=0)
            pr, _ = _probs(_stack_heads(q_ref, h, halves), k2, valid, _sink_column(sink_ref, h))
            pr = pr.astype(ACT)
            for p in range(2):
                both = jnp.concatenate([pr[2 * p * WINDOW:(2 * p + 1) * WINDOW], pr[(2 * p + 1) * WINDOW:(2 * p + 2) * WINDOW]], axis=1)
                o_ref[:, (2 * h + p) * PAIR:(2 * h + p + 1) * PAIR] = _dot(both, vs).astype(ACT)

    kv_c, kv_p = pl.BlockSpec((WINDOW, kd), cur), pl.BlockSpec((WINDOW, kd), prev)
    return _call(
        body, "attn_fwd", (t // WINDOW,),
        [pl.BlockSpec(memory_space=pltpu.SMEM), pl.BlockSpec((WINDOW, d), cur), kv_c, kv_p, kv_c, kv_p],
        [pl.BlockSpec((WINDOW, d), cur)], [_sds((t, d), ACT)],
        (sinks, q, kd_, kd_, vd, vd), ("parallel",), jobs=jobs)


def _attn_bwd(q, kd_, vd, do, sinks, cos, sin, jobs=()):
    t, d = q.shape
    kd = kd_.shape[1]
    nb = t // WINDOW
    n_out = d + 2 * kd
    cur = lambda n: (jnp.minimum(n, nb - 1), 0)
    prev = lambda n: (jnp.maximum(jnp.minimum(n, nb - 1) - 1, 0), 0)
    late = lambda n: (jnp.maximum(n - 1, 0), 0)

    def body(sink_ref, q_ref, kc_ref, kp_ref, vc_ref, vp_ref, do_ref, cosc_ref, sinc_ref, cosp_ref, sinp_ref,
             out_ref, db_ref, dsink_ref, dq_keep, dk_keep, dv_keep):
        n = pl.program_id(0)

        @pl.when(n == 0)
        def _():
            for ref in (db_ref, dsink_ref, dq_keep, dk_keep, dv_keep):
                ref[...] = jnp.zeros_like(ref)

        valid = _band_mask(n > 0) & (n < nb)
        halves = _lane_halves()
        first = _first_half(WINDOW)
        slot = lax.broadcasted_iota(jnp.int32, dsink_ref.shape, 1)
        top = lax.broadcasted_iota(jnp.int32, dsink_ref.shape, 0) == 0
        dsink = jnp.zeros(dsink_ref.shape, F32)

        def emit(cols, done):
            out_ref[:, cols] = done.astype(ACT)
            db_ref[:, cols] += jnp.sum(done.astype(F32), axis=0, keepdims=True)

        for h in range(kd // PAIR):
            hs = slice(h * PAIR, (h + 1) * PAIR)
            k2 = jnp.concatenate([kp_ref[:, hs], kc_ref[:, hs]], axis=0)
            v2 = jnp.concatenate([vp_ref[:, hs], vc_ref[:, hs]], axis=0)
            qs, dos = _stack_heads(q_ref, h, halves), _stack_heads(do_ref, h, halves)
            pr, psink = _probs(qs, k2, valid, _sink_column(sink_ref, h))
            dp = _dot_nt(dos, v2)
            delta = jnp.sum(pr * dp, axis=-1, keepdims=True)
            ds = (pr * (dp - delta)).astype(ACT)
            leak = psink * delta
            for j in range(GQA_GROUP):
                share = jnp.sum(leak[j * WINDOW:(j + 1) * WINDOW], axis=0, keepdims=True)
                dsink = dsink - jnp.where(top & (slot == GQA_GROUP * h + j), share, 0.0)
            dqs = _dot(ds, k2)
            for p in range(2):
                ps = slice((2 * h + p) * PAIR, (2 * h + p + 1) * PAIR)
                dqp = jnp.where(halves[0], dqs[2 * p * WINDOW:(2 * p + 1) * WINDOW], dqs[(2 * p + 1) * WINDOW:(2 * p + 2) * WINDOW])
                emit(ps, dq_keep[:, ps])
                dq_keep[:, ps] = (_rope_t(dqp, cosc_ref[...], sinc_ref[...], first) * SCALE).astype(ACT)
            dk2 = _dot_tn(ds, qs)
            dv2 = _dot_tn(pr.astype(ACT), dos)
            ks = slice(d + h * PAIR, d + (h + 1) * PAIR)
            vs = slice(d + kd + h * PAIR, d + kd + (h + 1) * PAIR)
            emit(ks, dk_keep[:, hs] + _rope_t(dk2[:WINDOW], cosp_ref[...], sinp_ref[...], first))
            dk_keep[:, hs] = _rope_t(dk2[WINDOW:], cosc_ref[...], sinc_ref[...], first)
            emit(vs, dv_keep[:, hs] + dv2[:WINDOW])
            dv_keep[:, hs] = dv2[WINDOW:]
        dsink_ref[...] += dsink

    kv_c, kv_p = pl.BlockSpec((WINDOW, kd), cur), pl.BlockSpec((WINDOW, kd), prev)
    tab_c, tab_p = pl.BlockSpec((WINDOW, PAIR), cur), pl.BlockSpec((WINDOW, PAIR), prev)
    return _call(
        body, "attn_bwd", (nb + 1,),
        [pl.BlockSpec(memory_space=pltpu.SMEM), pl.BlockSpec((WINDOW, d), cur), kv_c, kv_p, kv_c, kv_p,
         pl.BlockSpec((WINDOW, d), cur), tab_c, tab_c, tab_p, tab_p],
        [pl.BlockSpec((WINDOW, n_out), late), _full((1, n_out)), _full((8, LANES))],
        [_sds((t, n_out), ACT), _sds((1, n_out), F32), _sds((8, LANES), F32)],
        (sinks, q, kd_, kd_, vd, vd, do, cos, sin, cos, sin), ("arbitrary",),
        scratch=[pltpu.VMEM((WINDOW, d), ACT), pltpu.VMEM((WINDOW, kd), F32), pltpu.VMEM((WINDOW, kd), F32)], jobs=jobs)


def _proj_res(a, w, b, g, x, name, target=None, jobs=()):
    blocked = a.ndim == 3
    t = a.shape[-2]
    k, d = w.shape
    tm = _row_tile(t)
    has_bias = b is not None

    def body(*refs):
        a_ref, w_ref = refs[:2]
        b_ref = refs[2] if has_bias else None
        g_ref, x_ref, m_ref, xo_ref = refs[2 + has_bias:]
        if blocked:
            c = a.shape[2]
            m = _dot(a_ref[0], w_ref[:c, :])
            for j in range(1, a.shape[0]):
                m = m + _dot(a_ref[j], w_ref[j * c:(j + 1) * c, :])
        else:
            m = _dot(a_ref[...], w_ref[...])
        if has_bias:
            m = m + b_ref[...]
        m_ref[...] = m.astype(ACT)
        xo_ref[...] = x_ref[...] + (m * _rstd(m)) * g_ref[...]

    def body_with_loss(a_ref, w_ref, g_ref, x_ref, t_ref, m_ref, dy_ref, loss_ref):
        @pl.when(pl.program_id(0) == 0)
        def _():
            loss_ref[...] = jnp.zeros_like(loss_ref)

        body(a_ref, w_ref, g_ref, x_ref, m_ref, dy_ref)
        err = dy_ref[...] - t_ref[...]
        dy_ref[...] = err * (1.0 / d)
        loss_ref[...] += 0.5 * jnp.sum(jnp.mean(err * err, axis=-1, keepdims=True), axis=0, keepdims=True)

    ins = [a, w] + ([b] if has_bias else []) + [g, x]
    a_spec = pl.BlockSpec((a.shape[0], tm, a.shape[2]), lambda i: (0, i, 0)) if blocked else _rows(tm, k)
    specs = [a_spec, _full((k, d))] + ([_full((1, d))] if has_bias else []) + [_full((1, d)), _rows(tm, d)]
    if target is not None:
        return _call(body_with_loss, name, (t // tm,), specs + [_rows(tm, d)], [_rows(tm, d), _rows(tm, d), _full((8, LANES))],
                     [_sds((t, d), ACT), _sds((t, d), F32), _sds((8, LANES), F32)], ins + [target], ("arbitrary",), jobs=jobs)
    return _call(body, name, (t // tm,), specs, [_rows(tm, d), _rows(tm, d)], [_sds((t, d), ACT), _sds((t, d), F32)], ins,
                 ("parallel",), jobs=jobs)


def _post_bwd(dres, m, g, w, mode, name, gu=None, jobs=()):
    t, d = dres.shape
    k = w.shape[0]
    tm = _row_tile(t)
    kc = _pick(k, (1408, 1024, 512))

    def body(*refs):
        d_ref, m_ref, g_ref, w_ref = refs[:4]
        gu_ref = refs[4] if mode == "ffn" else None
        outs = refs[4 + (mode == "ffn"):]
        dm_ref, da_ref, dg_ref = outs[:3]
        i = pl.program_id(0)

        @pl.when(i == 0)
        def _():
            dg_ref[...] = jnp.zeros_like(dg_ref)
            if mode == "attn":
                outs[3][...] = jnp.zeros_like(outs[3])

        dv, mv = d_ref[...], m_ref[...].astype(F32)
        r = _rstd(mv)
        mh = mv * r
        dg_ref[...] += jnp.sum(dv * mh, axis=0, keepdims=True)
        dm = _rms_bwd(mh, r, g_ref[...], dv)
        dmb = dm.astype(ACT)
        dm_ref[...] = dmb
        if mode == "attn":
            outs[3][...] += jnp.sum(dm, axis=0, keepdims=True)
        if mode == "ffn":
            nb, _, cb = gu.shape
            for j in range(nb // 2):
                da = _dot_nt(dmb, w_ref[j * cb:(j + 1) * cb, :]).astype(ACT)
                gate, up = gu_ref[j], gu_ref[nb // 2 + j]
                sg = _sigmoid(gate.astype(F32)).astype(ACT)
                gs = gate * sg
                da_ref[j] = da * up * (sg + gs - gs * sg)
                da_ref[nb // 2 + j] = da * gs
        else:
            for c in range(0, k, kc):
                da = _dot_nt(dmb, w_ref[c:c + kc, :])
                da_ref[:, c:c + kc] = da.astype(ACT)

    ins = [dres, m, g, w]
    specs = [_rows(tm, d), _rows(tm, d), _full((1, d)), _full((k, d))]
    if mode == "ffn":
        slab = pl.BlockSpec((gu.shape[0], tm, gu.shape[2]), lambda i: (0, i, 0))
        ins.append(gu)
        specs.append(slab)
        out_specs = [_rows(tm, d), slab, _full((1, d))]
        out_shape = [_sds((t, d), ACT), _sds(gu.shape, ACT), _sds((1, d), F32)]
    else:
        out_specs = [_rows(tm, d), _rows(tm, k), _full((1, d))]
        out_shape = [_sds((t, d), ACT), _sds((t, k), ACT), _sds((1, d), F32)]
    if mode == "attn":
        out_specs.append(_full((1, d)))
        out_shape.append(_sds((1, d), F32))
    return _call(body, name, (t // tm,), specs, out_specs, out_shape, ins, ("arbitrary",), jobs=jobs)


def _pre_bwd(dy, w, x, g, dres, name, transposed=False, jobs=()):
    t, d = x.shape
    tm = _row_tile(t)

    def body(dy_ref, w_ref, x_ref, g_ref, dres_ref, dx_ref, dg_ref):
        @pl.when(pl.program_id(0) == 0)
        def _():
            dg_ref[...] = jnp.zeros_like(dg_ref)

        dh = jnp.zeros((tm, d), F32)
        if w.ndim == 3 and transposed:
            for j in range(w.shape[0]):
                dh = dh + _dot(dy_ref[j], w_ref[j])
        elif w.ndim == 3:
            c = w.shape[2]
            for j in range(w.shape[0]):
                dh = dh + _dot_nt(dy_ref[j] if dy.ndim == 3 else dy_ref[:, j * c:(j + 1) * c], w_ref[j])
        else:
            n = w.shape[1]
            nc = _pick(n, (1408, 1024, 512))
            for c in range(0, n, nc):
                dh = dh + _dot_nt(dy_ref[:, c:c + nc], w_ref[:, c:c + nc])
        xv = x_ref[...]
        r = _rstd(xv)
        xh = xv * r
        dg_ref[...] += jnp.sum(dh * xh, axis=0, keepdims=True)
        dx_ref[...] = dres_ref[...] + _rms_bwd(xh, r, g_ref[...], dh)

    dy_spec = pl.BlockSpec((dy.shape[0], tm, dy.shape[2]), lambda i: (0, i, 0)) if dy.ndim == 3 else _rows(tm, dy.shape[1])
    return _call(
        body, name, (t // tm,),
        [dy_spec, _full(w.shape), _rows(tm, d), _full((1, d)), _rows(tm, d)],
        [_rows(tm, d), _full((1, d))], [_sds((t, d), F32), _sds((1, d), F32)],
        (dy, w, x, g, dres), ("arbitrary",), jobs=jobs)


def _wgrad(a, b, name, out_dtype=F32, out_block=None, transposed=False, jobs=()):
    t = a.shape[-2]
    tt = _pick(t, (1024,))
    steps = t // tt
    if a.ndim == 3:
        k_steps, _, tk = a.shape
        a_spec = pl.BlockSpec((None, tt, tk), lambda i, j, s: (i, s, 0))
    else:
        tk = _pick(a.shape[1], (1024, 1408))
        k_steps = a.shape[1] // tk
        a_spec = pl.BlockSpec((tt, tk), lambda i, j, s: (s, i))
    k = k_steps * tk
    group = 1
    if b.ndim == 3:
        n_blocks, _, tn = b.shape
        group = 2 if n_blocks % 2 == 0 else 1
        n_steps, per = n_blocks // group, 1
        b_spec = pl.BlockSpec((group, tt, tn), lambda i, j, s: (j, s, 0))
        if transposed:
            out_spec, out_shape = pl.BlockSpec((group, tn, tk), lambda i, j, s: (j, 0, i)), _sds((n_blocks, tn, k), out_dtype)
        else:
            out_spec, out_shape = pl.BlockSpec((group, tk, tn), lambda i, j, s: (j, i, 0)), _sds((n_blocks, k, tn), out_dtype)
    else:
        n = b.shape[1]
        tn = _pick(n, (1024, 1408))
        n_steps = n // tn
        b_spec = pl.BlockSpec((tt, tn), lambda i, j, s: (s, j))
        if out_block is None:
            per = 0
            out_spec, out_shape = pl.BlockSpec((tk, tn), lambda i, j, s: (i, j)), _sds((k, n), out_dtype)
        else:
            per = tn // out_block
            out_spec = pl.BlockSpec((per, tk, out_block), lambda i, j, s: (j, i, 0))
            out_shape = _sds((n // out_block, k, out_block), out_dtype)

    def body(a_ref, b_ref, o_ref, acc_ref):
        s = pl.program_id(2)

        @pl.when(s == 0)
        def _():
            acc_ref[...] = jnp.zeros_like(acc_ref)

        if b.ndim == 3:
            av = a_ref[...]
            for p in range(group):
                acc_ref[p] += _dot_tn(b_ref[p], av) if transposed else _dot_tn(av, b_ref[p])
        else:
            acc_ref[...] += _dot_tn(a_ref[...], b_ref[...])

        @pl.when(s == steps - 1)
        def _():
            if b.ndim == 2 and per >= 1:
                for p in range(per):
                    o_ref[p] = acc_ref[:, p * out_block:(p + 1) * out_block].astype(out_dtype)
            else:
                o_ref[...] = acc_ref[...].astype(out_dtype)

    acc_shape = ((group, tn, tk) if transposed else (group, tk, tn)) if b.ndim == 3 else (tk, tn)
    res, job_res = _call(
        body, name, (k_steps, n_steps, steps), [a_spec, b_spec], [out_spec], [out_shape],
        (a, b), ("parallel", "parallel", "arbitrary"), scratch=[pltpu.VMEM(acc_shape, F32)], jobs=jobs)
    return res[0], job_res


def _ffn_up_fwd(x, g, w, jobs=()):
    t, d = x.shape
    nb, c, _ = w.shape
    tm = _row_tile(t)

    def body(x_ref, g_ref, w_ref, h_ref, gu_ref, a_ref):
        xv = x_ref[...]
        hb = (xv * _rstd(xv) * g_ref[...]).astype(ACT)
        h_ref[...] = hb
        for j in range(nb // 2):
            gate = _dot_nt(hb, w_ref[j])
            up = _dot_nt(hb, w_ref[nb // 2 + j])
            gu_ref[j] = gate.astype(ACT)
            gu_ref[nb // 2 + j] = up.astype(ACT)
            a_ref[j] = (gate * _sigmoid(gate) * up).astype(ACT)

    slab = lambda n: pl.BlockSpec((n, tm, c), lambda i: (0, i, 0))
    return _call(
        body, "ffn_up_fwd", (t // tm,),
        [_rows(tm, d), _full((1, d)), _full(w.shape)],
        [_rows(tm, d), slab(nb), slab(nb // 2)],
        [_sds((t, d), ACT), _sds((nb, t, c), ACT), _sds((nb // 2, t, c), ACT)],
        (x, g, w), ("parallel",), jobs=jobs)


def _causal(ws):
    row = lax.broadcasted_iota(jnp.int32, ws.shape, 0)
    col = lax.broadcasted_iota(jnp.int32, ws.shape, 1)
    return jnp.where(col <= row, ws, 0.0)


def _sgu_ln(v, lg, lb):
    mu = jnp.mean(v, axis=-1, keepdims=True)
    vc = v - mu
    rs = lax.rsqrt(jnp.mean(vc * vc, axis=-1, keepdims=True) + EPS)
    vh = vc * rs
    return vh, rs, vh * lg + lb


def _sgu_fwd(x, g, w, lg, lb, ws, bs_t, jobs=()):
    t, d = x.shape
    nb, _, c = w.shape
    n = nb * c
    groups = d // WINDOW
    tm = _row_tile(t)

    def body(x_ref, g_ref, w_ref, lg_ref, lb_ref, ws_ref, bs_ref, h_ref, z_ref, y_ref, zf_ref):
        xv = x_ref[...]
        hb = (xv * _rstd(xv) * g_ref[...]).astype(ACT)
        h_ref[...] = hb
        for j in range(nb):
            zf_ref[:, j * c:(j + 1) * c] = _dot(hb, w_ref[j])
        z_ref[...] = zf_ref[...].astype(ACT)
        for r in range(0, tm, WINDOW):
            u = _gelu(zf_ref[r:r + WINDOW, :d])
            _, _, vn = _sgu_ln(_gelu(zf_ref[r:r + WINDOW, d:]), lg_ref[...], lb_ref[...])
            for gi in range(groups):
                gs = slice(gi * WINDOW, (gi + 1) * WINDOW)
                mixed = _dot(_causal(ws_ref[gi]).astype(ACT), vn[:, gs].astype(ACT)) + bs_ref[:, gi:gi + 1]
                y_ref[r:r + WINDOW, gs] = (u[:, gs] * mixed).astype(ACT)

    vec = _full((1, d))
    return _call(
        body, "sgu_fwd", (t // tm,),
        [_rows(tm, d), vec, _full((nb, d, c)), vec, vec, _full((groups, WINDOW, WINDOW)), _full((WINDOW, groups))],
        [_rows(tm, d), _rows(tm, n), _rows(tm, d)], [_sds((t, d), ACT), _sds((t, n), ACT), _sds((t, d), ACT)],
        (x, g, w, lg, lb, ws, bs_t), ("parallel",), scratch=[pltpu.VMEM((tm, n), F32)], jobs=jobs)


def _sgu_mix_bwd(z, dy, lg, lb, ws, bs_t, jobs=()):
    t, n = z.shape
    d = n // 2
    groups = d // WINDOW
    tm = _row_tile(t)

    def body(z_ref, dy_ref, lg_ref, lb_ref, ws_ref, bs_ref, dz_ref, dlg_ref, dlb_ref, dws_ref, dbs_ref):
        @pl.when(pl.program_id(0) == 0)
        def _():
            for ref in (dlg_ref, dlb_ref, dws_ref, dbs_ref):
                ref[...] = jnp.zeros_like(ref)

        lane = lax.broadcasted_iota(jnp.int32, (WINDOW, groups), 1)
        for c in range(0, tm, WINDOW):
            rows = slice(c, c + WINDOW)
            zu, zv = z_ref[rows, :d].astype(F32), z_ref[rows, d:].astype(F32)
            u = _gelu(zu)
            vh, rs, vn = _sgu_ln(_gelu(zv), lg_ref[...], lb_ref[...])
            dyv = dy_ref[rows, :].astype(F32)
            dvn_parts = []
            for gi in range(groups):
                gs = slice(gi * WINDOW, (gi + 1) * WINDOW)
                wsg = _causal(ws_ref[gi]).astype(ACT)
                vng = vn[:, gs].astype(ACT)
                mixed = _dot(wsg, vng) + bs_ref[:, gi:gi + 1]
                dz_ref[rows, gs] = (dyv[:, gs] * mixed * _gelu_grad(zu[:, gs])).astype(ACT)
                dmix = dyv[:, gs] * u[:, gs]
                dmb = dmix.astype(ACT)
                dvn_parts.append(_dot_tn(wsg, dmb))
                dws_ref[gi] += _dot_nt(dmb, vng)
                dbs_ref[...] += jnp.where(lane == gi, jnp.sum(dmix, axis=-1, keepdims=True), 0.0)
            dvn = jnp.concatenate(dvn_parts, axis=-1)
            dlg_ref[...] += jnp.sum(dvn * vh, axis=0, keepdims=True)
            dlb_ref[...] += jnp.sum(dvn, axis=0, keepdims=True)
            dvh = dvn * lg_ref[...]
            dv = rs * (dvh - jnp.mean(dvh, axis=-1, keepdims=True) - vh * jnp.mean(dvh * vh, axis=-1, keepdims=True))
            dz_ref[rows, d:] = (dv * _gelu_grad(zv)).astype(ACT)

    vec = _full((1, d))
    return _call(
        body, "sgu_mix_bwd", (t // tm,),
        [_rows(tm, n), _rows(tm, d), vec, vec, _full((groups, WINDOW, WINDOW)), _full((WINDOW, groups))],
        [_rows(tm, n), vec, vec, _full((groups, WINDOW, WINDOW)), _full((WINDOW, groups))],
        [_sds((t, n), ACT), _sds((1, d), F32), _sds((1, d), F32), _sds((groups, WINDOW, WINDOW), F32), _sds((WINDOW, groups), F32)],
        (z, dy, lg, lb, ws, bs_t), ("arbitrary",), jobs=jobs)


AG_COPIES = 7


def _prepare(sources, dtypes, n_gather, name):
    n = len(sources)
    arrays, where = [], []
    for parts, layer in sources:
        found = []
        for part in parts:
            known = [i for i, v in enumerate(arrays) if v is part]
            if not known:
                arrays.append(part)
            found.append(known[0] if known else len(arrays) - 1)
        where.append((found, layer))
    shapes = [(sum(p.shape[1] for p in parts), parts[0].shape[2]) for parts, _ in sources]

    def body(*refs):
        ins, refs = refs[:len(arrays)], refs[len(arrays):]
        stage, outs = refs[:n], refs[n:n + n_gather]
        send, recv, local_sem = refs[n + n_gather:]
        x, y, c = _place()
        sibling = (x, y, 1 - c)
        chips = _other_chips(x, y)
        mine = 4 * x + 2 * y + c

        def copy(a, k, block, to, src=None):
            dst = outs[a].at[block]
            return pltpu.make_async_remote_copy(
                src_ref=dst if src is None else src, dst_ref=dst, send_sem=send.at[a * AG_COPIES + k],
                recv_sem=recv.at[a * AG_COPIES + k], device_id=to, device_id_type=MESH)

        def cast(a):
            found, layer = where[a]
            at = 0
            for i in found:
                rows = arrays[i].shape[1]
                stage[a][at:at + rows, :] = ins[i][layer].astype(dtypes[a])
                at += rows

        for a in range(n_gather):
            cast(a)
        started = []
        for a in range(n_gather):
            own = pltpu.make_async_copy(stage[a], outs[a].at[mine], local_sem.at[a])
            own.start()
            started.append(own)
        sends = []
        for a in range(n_gather):
            sends.append(copy(a, 0, mine, sibling, src=stage[a]))
            sends += [copy(a, 1 + j, mine, (*chip, c), src=stage[a]) for j, chip in enumerate(chips)]
        for cp in sends:
            cp.start()
        for a in range(n_gather, n):
            cast(a)
        for j, (px, py) in enumerate(chips):
            block = 4 * px + 2 * py + c
            for a in range(n_gather):
                copy(a, 1 + j, block, (x, y, c)).wait_recv()
                forward = copy(a, 4 + j, block, sibling)
                forward.start()
                sends.append(forward)
        for a in range(n_gather):
            copy(a, 0, 4 * x + 2 * y + (1 - c), (x, y, c)).wait_recv()
            for j, (px, py) in enumerate(chips):
                copy(a, 4 + j, 4 * px + 2 * py + (1 - c), (x, y, c)).wait_recv()
        for cp in sends:
            cp.wait_send()
        for own in started:
            own.wait()

    res = pl.pallas_call(
        body, name=name,
        in_specs=[IN_VMEM] * len(arrays), out_specs=[IN_VMEM] * n + [ANY] * n_gather,
        out_shape=[_sds(s, dt) for s, dt in zip(shapes, dtypes)]
        + [_sds((N_DEV,) + s, dt) for s, dt in zip(shapes[:n_gather], dtypes)],
        scratch_shapes=[pltpu.SemaphoreType.DMA((n_gather * AG_COPIES,)), pltpu.SemaphoreType.DMA((n_gather * AG_COPIES,)),
                        pltpu.SemaphoreType.DMA((n_gather,))],
        compiler_params=pltpu.CompilerParams(vmem_limit_bytes=VMEM_LIMIT_BYTES),
    )(*arrays)
    return list(res[:n]), list(res[n:])


def _pair_sum(g, r, core, name):
    _, _, rows, cols = g.shape
    tr = _pick(rows, (512, 256, 128))

    def body(core_ref, g_ref, r_ref, p_ref):
        del core_ref
        p_ref[...] = (g_ref[...].astype(F32) + r_ref[...].astype(F32)).astype(p_ref.dtype)

    return pl.pallas_call(
        body, name=name,
        grid_spec=pltpu.PrefetchScalarGridSpec(
            num_scalar_prefetch=1, grid=(4, rows // tr),
            in_specs=[pl.BlockSpec((None, None, tr, cols), lambda q, i, core_ref: (q, core_ref[0], i, 0)),
                      pl.BlockSpec((None, tr, cols), lambda q, i, core_ref: (q, i, 0))],
            out_specs=pl.BlockSpec((None, tr, cols), lambda q, i, core_ref: (q, i, 0))),
        out_shape=_sds((4, rows, cols), g.dtype),
        compiler_params=_params("parallel", "parallel"),
    )(core, g, r)


def _adam(w, g, m, v):
    m2 = ADAM_B1 * m + (1.0 - ADAM_B1) * g
    v2 = ADAM_B2 * v + (1.0 - ADAM_B2) * (g * g)
    m_hat = m2 / (1.0 - ADAM_B1 ** ADAM_STEP)
    v_hat = v2 / (1.0 - ADAM_B2 ** ADAM_STEP)
    return -ADAM_LR * (m_hat / (jnp.sqrt(v_hat) + ADAM_EPS) + ADAM_WD * w), m2, v2


def _shard_update(own, index, received, w, m, v, layer, so_far, name):
    _, rows, cols = w.shape
    n_recv = received.shape[0]
    tr = _pick(rows, (512, 256, 128))

    def body(index_ref, p_ref, q_ref, w_ref, m_ref, v_ref, *rest):
        del index_ref
        g_out, d_out, m_out, v_out = rest[-4:]
        g = p_ref[...].astype(F32)
        for s in range(n_recv):
            g = g + q_ref[s].astype(F32)
        g_out[...] = g
        d_out[...], m_out[...], v_out[...] = _adam(w_ref[...], g, m_ref[...], v_ref[...])

    tile = pl.BlockSpec((None, tr, cols), lambda i, index_ref: (layer, i, 0))
    kept = list(so_far) if so_far is not None else []
    return pl.pallas_call(
        body, name=name,
        grid_spec=pltpu.PrefetchScalarGridSpec(
            num_scalar_prefetch=1, grid=(rows // tr,),
            in_specs=[pl.BlockSpec((None, tr, cols), lambda i, index_ref: (index_ref[0], i, 0)),
                      pl.BlockSpec((n_recv, tr, cols), lambda i, index_ref: (0, i, 0)), tile, tile, tile] + [ANY] * len(kept),
            out_specs=[tile] * 4),
        out_shape=[_sds(w.shape, F32)] * 4,
        input_output_aliases={6 + i: i for i in range(len(kept))},
        compiler_params=_params("parallel"),
    )(index, own, received, w, m, v, *kept)


def _replicated_update(shares, w, m, v):
    rows, cols = w.shape

    def body(s_ref, w_ref, m_ref, v_ref, g_out, d_out, m_out, v_out):
        g = s_ref[0]
        for dev in range(1, N_DEV):
            g = g + s_ref[dev]
        g_out[...] = g
        d_out[...], m_out[...], v_out[...] = _adam(w_ref[...], g, m_ref[...], v_ref[...])

    return pl.pallas_call(
        body, name="replicated_update_%d" % rows, out_shape=[_sds((rows, cols), F32)] * 4,
        compiler_params=pltpu.CompilerParams(vmem_limit_bytes=VMEM_LIMIT_BYTES),
    )(shares, w, m, v)


def _rope_tables(t):
    half = HEAD_DIM // 2
    inv_freq = ROPE_THETA ** (-(jnp.arange(half, dtype=F32) * 2.0) / HEAD_DIM)
    ang = jnp.arange(t, dtype=jnp.int32).astype(F32)[:, None] * inv_freq[None, :]
    cos, sin = jnp.cos(ang), jnp.sin(ang)
    return jnp.tile(jnp.concatenate([cos, cos], axis=1), (1, 2)), jnp.tile(jnp.concatenate([-sin, sin], axis=1), (1, 2))


def _dup_heads(w, d):
    kv = (w.shape[-1] - d) // 2
    lead = w.shape[:-1]

    def dup(part):
        heads = part.reshape(lead + (kv // HEAD_DIM, 1, HEAD_DIM))
        return jnp.broadcast_to(heads, lead + (kv // HEAD_DIM, 2, HEAD_DIM)).reshape(lead + (2 * kv,))

    return jnp.concatenate([w[..., :d], dup(w[..., d:d + kv]), dup(w[..., d + kv:])], axis=-1)


def _fold_heads(dw, d):
    kv = (dw.shape[-1] - d) // 4
    lead = dw.shape[:-1]

    def fold(part):
        return part.reshape(lead + (kv // HEAD_DIM, 2, HEAD_DIM)).sum(axis=-2).reshape(lead + (kv,))

    return jnp.concatenate([dw[..., :d], fold(dw[..., d:d + 2 * kv]), fold(dw[..., d + 2 * kv:])], axis=-1)


def _columns_full(gathered):
    _, rows, c = gathered.shape
    return jnp.transpose(gathered, (1, 0, 2)).reshape(rows, N_DEV * c)


def _rows_full(gathered):
    return gathered.reshape(-1, gathered.shape[-1])


def _columns_blocked(full):
    rows, cols = full.shape
    return jnp.transpose(full.reshape(rows, N_DEV, cols // N_DEV), (1, 0, 2)).astype(ACT)


def _rows_blocked(full):
    return full.reshape(N_DEV, full.shape[0] // N_DEV, full.shape[1]).astype(ACT)


def _packed_rows(size, width):
    return -(-size // (8 * width)) * 8


def _pack_rows(parts, width):
    rows = []
    for v in parts:
        flat = v.reshape(-1).astype(F32)
        n_rows = _packed_rows(flat.shape[0], width)
        rows.append(jnp.pad(flat, (0, n_rows * width - flat.shape[0])).reshape(n_rows, width))
    return jnp.concatenate(rows, axis=0)


def _unpack_rows(packed, like, width):
    out, at = [], 0
    for v in like:
        size = math.prod(v.shape)
        out.append(packed[at:at + -(-size // width)].reshape(-1)[:size].reshape(v.shape))
        at += _packed_rows(size, width)
    return out


def _halves(block):
    half = block.shape[0] // 2
    return (0, half), (half, half)


def _whole(block):
    return (0, block.shape[0])


EARLY = ("attn_w_qkv", "sgu_ln", "attn_w_o")
LATE = ("sgu_w_in", "sgu_w_out", "gu0", "gu1", "dn0", "dn1")
COLUMN_SHARDED = ("attn_w_qkv", "sgu_w_in", "gu0", "gu1")
BEFORE_ATTN = ("norm_mix_post", "norm_ffn_pre", "norm_ffn_post", "attn_b_o", "sgu_w_spatial", "sgu_b_spatial")
AFTER_ATTN = ("attn_b_qkv", "attn_sinks")
LAST = ("norm_mix_pre",)
WEIGHTS = ("norm_mix_pre", "norm_mix_post", "norm_ffn_pre", "norm_ffn_post", "attn_w_qkv", "attn_b_qkv", "attn_sinks", "attn_w_o",
           "attn_b_o", "sgu_w_in", "sgu_ln_g", "sgu_ln_b", "sgu_w_spatial", "sgu_b_spatial", "sgu_w_out", "ffn_w_gate_up", "ffn_w_down")


LAYER_OF = {"gu0": ("ffn_w_gate_up", 0), "gu1": ("ffn_w_gate_up", 1), "dn0": ("ffn_w_down", 0), "dn1": ("ffn_w_down", 1)}


def _source(tree, name):
    if name == "sgu_ln":
        return [tree["sgu_ln_g"][None], tree["sgu_ln_b"][None]], 0
    leaf, layer = LAYER_OF.get(name, (name, 0))
    return [tree[leaf]], layer


def kernel(x, norm_mix_pre, norm_mix_post, norm_ffn_pre, norm_ffn_post, attn_w_qkv, attn_b_qkv, attn_sinks, attn_w_o, attn_b_o, sgu_w_in, sgu_ln_g, sgu_ln_b, sgu_w_spatial, sgu_b_spatial, sgu_w_out, ffn_w_gate_up, ffn_w_down, loss_target, m_norm_mix_pre, m_norm_mix_post, m_norm_ffn_pre, m_norm_ffn_post, m_attn_w_qkv, m_attn_b_qkv, m_attn_sinks, m_attn_w_o, m_attn_b_o, m_sgu_w_in, m_sgu_ln_g, m_sgu_ln_b, m_sgu_w_spatial, m_sgu_b_spatial, m_sgu_w_out, m_ffn_w_gate_up, m_ffn_w_down, v_norm_mix_pre, v_norm_mix_post, v_norm_ffn_pre, v_norm_ffn_post, v_attn_w_qkv, v_attn_b_qkv, v_attn_sinks, v_attn_w_o, v_attn_b_o, v_sgu_w_in, v_sgu_ln_g, v_sgu_ln_b, v_sgu_w_spatial, v_sgu_b_spatial, v_sgu_w_out, v_ffn_w_gate_up, v_ffn_w_down):
    w = dict(norm_mix_pre=norm_mix_pre, norm_mix_post=norm_mix_post, norm_ffn_pre=norm_ffn_pre, norm_ffn_post=norm_ffn_post,
             attn_w_qkv=attn_w_qkv, attn_b_qkv=attn_b_qkv, attn_sinks=attn_sinks, attn_w_o=attn_w_o, attn_b_o=attn_b_o,
             sgu_w_in=sgu_w_in, sgu_ln_g=sgu_ln_g, sgu_ln_b=sgu_ln_b, sgu_w_spatial=sgu_w_spatial, sgu_b_spatial=sgu_b_spatial,
             sgu_w_out=sgu_w_out, ffn_w_gate_up=ffn_w_gate_up, ffn_w_down=ffn_w_down)
    mom = dict(norm_mix_pre=m_norm_mix_pre, norm_mix_post=m_norm_mix_post, norm_ffn_pre=m_norm_ffn_pre, norm_ffn_post=m_norm_ffn_post,
               attn_w_qkv=m_attn_w_qkv, attn_b_qkv=m_attn_b_qkv, attn_sinks=m_attn_sinks, attn_w_o=m_attn_w_o, attn_b_o=m_attn_b_o,
               sgu_w_in=m_sgu_w_in, sgu_ln_g=m_sgu_ln_g, sgu_ln_b=m_sgu_ln_b, sgu_w_spatial=m_sgu_w_spatial,
               sgu_b_spatial=m_sgu_b_spatial, sgu_w_out=m_sgu_w_out, ffn_w_gate_up=m_ffn_w_gate_up, ffn_w_down=m_ffn_w_down)
    var = dict(norm_mix_pre=v_norm_mix_pre, norm_mix_post=v_norm_mix_post, norm_ffn_pre=v_norm_ffn_pre, norm_ffn_post=v_norm_ffn_post,
               attn_w_qkv=v_attn_w_qkv, attn_b_qkv=v_attn_b_qkv, attn_sinks=v_attn_sinks, attn_w_o=v_attn_w_o, attn_b_o=v_attn_b_o,
               sgu_w_in=v_sgu_w_in, sgu_ln_g=v_sgu_ln_g, sgu_ln_b=v_sgu_ln_b, sgu_w_spatial=v_sgu_w_spatial,
               sgu_b_spatial=v_sgu_b_spatial, sgu_w_out=v_sgu_w_out, ffn_w_gate_up=v_ffn_w_gate_up, ffn_w_down=v_ffn_w_down)
    w, mom, var = [{**tree, "ffn_w_gate_up": jnp.swapaxes(tree["ffn_w_gate_up"], 1, 2)} for tree in (w, mom, var)]
    xs, target = x[0], loss_target[0]
    t, d = xs.shape
    row = lambda v: v.reshape(1, -1).astype(F32)
    core = lax.axis_index("c").astype(jnp.int32).reshape(1)
    chip = (2 * lax.axis_index("x") + lax.axis_index("y")).astype(jnp.int32).reshape(1)
    me = (4 * lax.axis_index("x") + 2 * lax.axis_index("y") + lax.axis_index("c")).astype(jnp.int32).reshape(1)
    names = EARLY + LATE
    staged, early = _prepare([_source(w, n) for n in names], [F32 if n == "sgu_ln" else ACT for n in names], len(EARLY),
                             "weights_prepare")
    stage = dict(zip(names, staged))
    w_qkv = _dup_heads(_columns_full(early[0]), d)
    lg, lb = row(early[1][:, 0, :]), row(early[1][:, 1, :])
    w_o = _rows_full(early[2])
    b_qkv = _dup_heads(row(attn_b_qkv), d)
    sinks = attn_sinks.reshape(1, -1).astype(F32)
    ws = sgu_w_spatial[0].astype(F32)
    bs_t = sgu_b_spatial[0].astype(F32).T
    cos, sin = _rope_tables(t)
    gather = lambda name, so_far, **pieces: _gather_job(stage[name], so_far, **pieces)
    a_half = lambda name: _halves(stage[name])[0]
    b_half = lambda name: _halves(stage[name])[1]
    whole = lambda name: _whole(stage[name])

    quarter = lambda name, i: (i * stage[name].shape[0] // 4, stage[name].shape[0] // 4)
    (h0, q, kdup, vdup), ((g_gu0,),) = _attn_qkv_fwd(
        xs, row(norm_mix_pre[0]), w_qkv, b_qkv, cos, sin, jobs=[gather("gu0", None, sends=[quarter("gu0", 0)])])
    (o,), ((g_gu0,),) = _attn_fwd(q, kdup, vdup, sinks, jobs=[
        gather("gu0", g_gu0, sends=[quarter("gu0", i) for i in (1, 2, 3)], forwards=[quarter("gu0", 0)])])
    (m0, x1), ((g_gu0,), (g_dn0,)) = _proj_res(o, w_o, row(attn_b_o), row(norm_mix_post[0]), xs, "attn_out_fwd", jobs=[
        gather("gu0", g_gu0, forwards=[quarter("gu0", i) for i in (1, 2, 3)]), gather("dn0", None, sends=[whole("dn0")])])
    w_gu0 = g_gu0
    (h1, gu0, a0), ((g_dn0,), (g_in,), (g_gu1,)) = _ffn_up_fwd(x1, row(norm_ffn_pre[0]), w_gu0, jobs=[
        gather("dn0", g_dn0, forwards=[whole("dn0")]), gather("sgu_w_in", None, sends=[whole("sgu_w_in")]),
        gather("gu1", None, sends=[quarter("gu1", 0), quarter("gu1", 1)])])
    w_dn0 = _rows_full(g_dn0)
    (f0, x2), ((g_in,), (g_out,), (g_gu1,)) = _proj_res(a0, w_dn0, None, row(norm_ffn_post[0]), x1, "ffn_down_fwd0", jobs=[
        gather("sgu_w_in", g_in, forwards=[whole("sgu_w_in")]), gather("sgu_w_out", None, sends=[whole("sgu_w_out")]),
        gather("gu1", g_gu1, sends=[quarter("gu1", 2)], forwards=[quarter("gu1", 0), quarter("gu1", 1)])])
    w_in = g_in
    (h2, z, y), ((g_out,), (g_gu1,), (g_dn1,)) = _sgu_fwd(x2, row(norm_mix_pre[1]), w_in, lg, lb, ws, bs_t, jobs=[
        gather("sgu_w_out", g_out, forwards=[whole("sgu_w_out")]),
        gather("gu1", g_gu1, sends=[quarter("gu1", 3)], forwards=[quarter("gu1", 2)]), gather("dn1", None, sends=[a_half("dn1")])])
    w_out = _rows_full(g_out)
    (m1, x3), ((g_gu1,), (g_dn1,)) = _proj_res(y, w_out, None, row(norm_mix_post[1]), x2, "sgu_out_fwd", jobs=[
        gather("gu1", g_gu1, forwards=[quarter("gu1", 3)]), gather("dn1", g_dn1, sends=[b_half("dn1")], forwards=[a_half("dn1")])])
    w_gu1 = g_gu1
    (h3, gu1, a1), ((g_dn1,),) = _ffn_up_fwd(x3, row(norm_ffn_pre[1]), w_gu1, jobs=[gather("dn1", g_dn1, forwards=[b_half("dn1")])])
    w_dn1 = _rows_full(g_dn1)
    (f1, dx4, loss_tile), _ = _proj_res(a1, w_dn1, None, row(norm_ffn_post[1]), x3, "ffn_down_fwd1", target=target)

    to_sibling = lambda g: _scatter_job([g], 4, _to_sibling)
    pair = lambda g, r, name: _pair_sum(g.reshape((4, 2) + g.shape[1:]), r, core, "pair_sum_" + name)
    to_chips = lambda p, prev=None, piece=None: _scatter_job([p], 3, _to_owner_chip, prev=prev, piece=piece)
    out_g, out_d, out_m, out_v = {}, {}, {}, {}

    trees = [{**tree, "sgu_ln": jnp.stack([tree["sgu_ln_g"], tree["sgu_ln_b"]], axis=1)} for tree in (w, mom, var)]
    so_far = {}

    def update(name, own, index, received):
        leaf, layer = LAYER_OF.get(name, (name, 0))
        so_far[leaf] = _shard_update(own, index, received, *[tree[leaf] for tree in trees], layer, so_far.get(leaf), "update_" + name)
        for out, val in zip((out_g, out_d, out_m, out_v), so_far[leaf]):
            out[leaf] = val

    def finish(names, extra, shares):
        like = [w[name] for name in names] + extra
        packed = [_pack_rows([tree[name] for name in names] + [jnp.zeros_like(e) for e in extra], d) for tree in (w, mom, var)]
        res = _replicated_update(shares, *packed)
        for out, val in zip((out_g, out_d, out_m, out_v), res):
            for name, leaf in zip(names, _unpack_rows(val, like, d)):
                out[name] = leaf
        return _unpack_rows(res[0], like, d)[len(names):]

    first_half = lambda p: _halves(p[0])[0]
    second_half = lambda p: _halves(p[0])[1]
    (df1, dgu1, dg_ffn_post1), _ = _post_bwd(dx4, f1, row(norm_ffn_post[1]), w_dn1, "ffn", "ffn_down_bwd1", gu1)
    b_gu1, _ = _wgrad(h3, dgu1, "ffn_up_wgrad1", ACT, transposed=True)
    dw_dn1, ((r_gu1,),) = _wgrad(a1, df1, "ffn_down_wgrad1", ACT, jobs=[to_sibling(b_gu1)])
    b_dn1 = _rows_blocked(dw_dn1)
    p_gu1 = pair(b_gu1, r_gu1, "gu1")
    (dx3, dg_ffn_pre1), ((q_gu1,), (r_dn1,)) = _pre_bwd(dgu1, w_gu1, x3, row(norm_ffn_pre[1]), dx4, "ffn_up_bwd1", True, jobs=[
        to_chips(p_gu1, piece=first_half(p_gu1)), to_sibling(b_dn1)])
    p_dn1 = pair(b_dn1, r_dn1, "dn1")
    (dm1, dy, dg_mix_post1), _ = _post_bwd(dx3, m1, row(norm_mix_post[1]), w_out, "sgu", "sgu_out_bwd")
    dw_out, _ = _wgrad(y, dm1, "sgu_out_wgrad", ACT)
    b_out = _rows_blocked(dw_out)
    (dz, dlg, dlb, dws, dbs_t), ((q_gu1,), (r_out,)) = _sgu_mix_bwd(z, dy, lg, lb, ws, bs_t, jobs=[
        to_chips(p_gu1, prev=[q_gu1], piece=second_half(p_gu1)), to_sibling(b_out)])
    update("gu1", p_gu1, chip, q_gu1)
    p_out = pair(b_out, r_out, "sgu_w_out")
    b_in, ((q_out,),) = _wgrad(h2, dz, "sgu_in_wgrad", ACT, out_block=stage["sgu_w_in"].shape[1], jobs=[to_chips(p_out)])
    update("sgu_w_out", p_out, chip, q_out)
    b_ln = jnp.stack([dlg.reshape(N_DEV, -1), dlb.reshape(N_DEV, -1)], axis=1)
    (dx2, dg_mix_pre1), ((r_in,), (r_ln,)) = _pre_bwd(dz, w_in, x2, row(norm_mix_pre[1]), dx3, "sgu_in_bwd",
                                                      jobs=[to_sibling(b_in), to_sibling(b_ln)])
    p_in, p_ln = pair(b_in, r_in, "sgu_w_in"), pair(b_ln, r_ln, "sgu_ln")
    (df0, dgu0, dg_ffn_post0), ((q_dn1,),) = _post_bwd(dx2, f0, row(norm_ffn_post[0]), w_dn0, "ffn", "ffn_down_bwd0", gu0,
                                                      jobs=[to_chips(p_dn1)])
    update("dn1", p_dn1, chip, q_dn1)
    b_gu0, ((q_in,), (q_ln,)) = _wgrad(h1, dgu0, "ffn_up_wgrad0", ACT, transposed=True, jobs=[to_chips(p_in), to_chips(p_ln)])
    update("sgu_w_in", p_in, chip, q_in)
    update("sgu_ln", p_ln, chip, q_ln)
    dw_dn0, ((r_gu0,),) = _wgrad(a0, df0, "ffn_down_wgrad0", ACT, jobs=[to_sibling(b_gu0)])
    b_dn0 = _rows_blocked(dw_dn0)
    p_gu0 = pair(b_gu0, r_gu0, "gu0")
    (dx1, dg_ffn_pre0), ((q_gu0,), (r_dn0,)) = _pre_bwd(dgu0, w_gu0, x1, row(norm_ffn_pre[0]), dx2, "ffn_up_bwd0", True, jobs=[
        to_chips(p_gu0, piece=first_half(p_gu0)), to_sibling(b_dn0)])
    p_dn0 = pair(b_dn0, r_dn0, "dn0")
    (dm0, do, dg_mix_post0, db_o), _ = _post_bwd(dx1, m0, row(norm_mix_post[0]), w_o, "attn", "attn_out_bwd")
    dw_o, _ = _wgrad(o, dm0, "attn_out_wgrad", ACT)
    b_o = _rows_blocked(dw_o)
    grads = {
        "norm_mix_post": jnp.concatenate([dg_mix_post0, dg_mix_post1], axis=0),
        "norm_ffn_pre": jnp.concatenate([dg_ffn_pre0, dg_ffn_pre1], axis=0),
        "norm_ffn_post": jnp.concatenate([dg_ffn_post0, dg_ffn_post1], axis=0),
        "attn_b_o": db_o,
        "sgu_w_spatial": dws * jnp.tril(jnp.ones((WINDOW, WINDOW), F32))[None],
        "sgu_b_spatial": dbs_t.T,
    }
    shares1 = _pack_rows([grads[name] for name in BEFORE_ATTN], d)
    (dqkv, db_qkv, dsink), ((q_gu0,), (q_dn0,), (r_o,), (g_shares1,)) = _attn_bwd(q, kdup, vdup, do, sinks, cos, sin, jobs=[
        to_chips(p_gu0, prev=[q_gu0], piece=second_half(p_gu0)), to_chips(p_dn0), to_sibling(b_o),
        _gather_job(shares1, None, sends=[_whole(shares1)])])
    update("gu0", p_gu0, chip, q_gu0)
    update("dn0", p_dn0, chip, q_dn0)
    p_o = pair(b_o, r_o, "attn_w_o")
    grads.update({"attn_b_qkv": _fold_heads(db_qkv, d), "attn_sinks": dsink[:1, :d // HEAD_DIM]})
    shares2 = _pack_rows([grads[name] for name in AFTER_ATTN] + [loss_tile[:1, :1]], d)
    dw_qkv, ((q_o,), (g_shares1,), (g_shares2,)) = _wgrad(h0, dqkv, "attn_qkv_wgrad", jobs=[
        to_chips(p_o), _gather_job(shares1, g_shares1, forwards=[_whole(shares1)]),
        _gather_job(shares2, None, sends=[_whole(shares2)])])
    update("attn_w_o", p_o, chip, q_o)
    b_qkv_grad = _columns_blocked(_fold_heads(dw_qkv, d))
    (dx0, dg_mix_pre0), ((q_qkv,), (g_shares2,)) = _pre_bwd(dqkv, w_qkv, xs, row(norm_mix_pre[0]), dx1, "attn_qkv_bwd", jobs=[
        _scatter_job([b_qkv_grad], N_DEV - 1, _to_owner), _gather_job(shares2, g_shares2, forwards=[_whole(shares2)])])
    update("attn_w_qkv", b_qkv_grad, me, q_qkv)

    finish(BEFORE_ATTN, [], g_shares1)
    loss = finish(AFTER_ATTN, [loss_tile[:1, :1]], g_shares2)[0][0, 0]
    grads["norm_mix_pre"] = jnp.concatenate([dg_mix_pre0, dg_mix_pre1], axis=0)
    _, (last_shares,) = _prepare([([_pack_rows([grads[name] for name in LAST], d)[None]], 0)], [F32], 1, "last_grads_all_gather")
    finish(LAST, [], last_shares)

    for out in (out_g, out_d, out_m, out_v):
        out["sgu_ln_g"], out["sgu_ln_b"] = out["sgu_ln"][:, 0, :], out["sgu_ln"][:, 1, :]
        out["ffn_w_gate_up"] = jnp.swapaxes(out["ffn_w_gate_up"], 1, 2)

    return (loss, dx0[None], *[out_g[n] for n in WEIGHTS], *[out_d[n] for n in WEIGHTS],
            *[out_m[n] for n in WEIGHTS], *[out_v[n] for n in WEIGHTS])
```

```python
import functools
import math
import operator

import jax
import jax.numpy as jnp
from jax import lax
from jax.experimental import pallas as pl
from jax.experimental.pallas import tpu as pltpu

F32 = jnp.float32
ACT = jnp.bfloat16

EPS = 1e-6
HEAD_DIM = 64
PAIR = 2 * HEAD_DIM
GQA_GROUP = 4
WINDOW = 128
ROPE_THETA = 10000.0
SCALE = HEAD_DIM ** -0.5
MASKED = -1e30
LANES = 128

ADAM_LR, ADAM_B1, ADAM_B2, ADAM_EPS, ADAM_WD, ADAM_STEP = 0.001, 0.9, 0.999, 1e-08, 0.01, 10

N_DEV = 8
VMEM_LIMIT_BYTES = 56 * 1024 * 1024
MESH = pl.DeviceIdType.MESH
ANY = pl.BlockSpec(memory_space=pl.ANY)
IN_VMEM = pl.BlockSpec(memory_space=pltpu.VMEM)


def _pick(n, candidates):
    for c in candidates:
        if n % c == 0:
            return c
    return n


def _row_tile(t):
    return _pick(t, (512,))


def _params(*sem):
    return pltpu.CompilerParams(dimension_semantics=sem, vmem_limit_bytes=VMEM_LIMIT_BYTES)


def _full(shape):
    return pl.BlockSpec(shape, lambda *_: (0,) * len(shape))


def _rows(tm, n):
    return pl.BlockSpec((tm, n), lambda i: (i, 0))


def _sds(shape, dtype):
    return jax.ShapeDtypeStruct(shape, dtype)


def _place():
    return lax.axis_index("x"), lax.axis_index("y"), lax.axis_index("c")


def _other_chips(x, y):
    return [(1 - x, y), (x, 1 - y), (1 - x, 1 - y)]


class _Job:
    def __init__(self, inputs, out_shape, aliases, emit, n_remote, n_local=0):
        self.inputs, self.out_shape, self.aliases, self.emit = list(inputs), list(out_shape), dict(aliases), emit
        self.n_remote, self.n_local = n_remote, n_local


def _call(body, name, grid, in_specs, out_specs, out_shape, args, sem, scratch=(), jobs=()):
    n_in, n_out, n_scr = len(args), len(out_shape), len(scratch)
    j_in = [a for job in jobs for a in job.inputs]
    j_out = [s for job in jobs for s in job.out_shape]
    aliases, at_in, at_out = {}, n_in, n_out
    for job in jobs:
        aliases.update({at_in + i: at_out + o for i, o in job.aliases.items()})
        at_in, at_out = at_in + len(job.inputs), at_out + len(job.out_shape)
    n_remote = sum(job.n_remote for job in jobs)
    n_local = sum(job.n_local for job in jobs)

    def wrapped(*refs):
        ins, jin = refs[:n_in], refs[n_in:n_in + len(j_in)]
        rest = refs[n_in + len(j_in):]
        outs, jout = rest[:n_out], rest[n_out:n_out + len(j_out)]
        scr = rest[n_out + len(j_out):n_out + len(j_out) + n_scr]
        if not jobs:
            body(*ins, *outs, *scr)
            return
        send, recv, local = rest[n_out + len(j_out) + n_scr:]
        ids = [pl.program_id(a) for a in range(len(grid))]
        first = functools.reduce(operator.and_, [i == 0 for i in ids])
        last = functools.reduce(operator.and_, [i == g - 1 for i, g in zip(ids, grid)])

        def descriptors():
            place, found, i, o, r, l = _place(), [], 0, 0, 0, 0
            for job in jobs:
                for src, dst, to in job.emit(jin[i:i + len(job.inputs)], jout[o:o + len(job.out_shape)], place):
                    if to is None:
                        found.append(pltpu.make_async_copy(src, dst, local.at[l]))
                        l += 1
                    else:
                        found.append(pltpu.make_async_remote_copy(src_ref=src, dst_ref=dst, send_sem=send.at[r], recv_sem=recv.at[r],
                                                                  device_id=to, device_id_type=MESH))
                        r += 1
                i, o = i + len(job.inputs), o + len(job.out_shape)
            return found

        @pl.when(first)
        def _():
            for cp in descriptors():
                cp.start()

        body(*ins, *outs, *scr)

        @pl.when(last)
        def _():
            for cp in descriptors():
                cp.wait()

    sems = [pltpu.SemaphoreType.DMA((n_remote,)), pltpu.SemaphoreType.DMA((n_remote,)),
            pltpu.SemaphoreType.DMA((max(n_local, 1),))] if jobs else []
    res = pl.pallas_call(
        wrapped, name=name, grid=grid,
        in_specs=list(in_specs) + [ANY] * len(j_in), out_specs=list(out_specs) + [ANY] * len(j_out),
        out_shape=list(out_shape) + j_out, scratch_shapes=list(scratch) + sems, input_output_aliases=aliases,
        compiler_params=_params(*(("arbitrary",) * len(grid) if jobs else sem)),
    )(*args, *j_in)
    job_res, at = [], n_out
    for job in jobs:
        job_res.append(list(res[at:at + len(job.out_shape)]))
        at += len(job.out_shape)
    return list(res[:n_out]), job_res


def _gather_job(stage, gathered, sends=(), forwards=()):
    shape = _sds((N_DEV,) + stage.shape, stage.dtype)
    inputs = ([stage] if sends else []) + ([gathered] if gathered is not None else [])
    aliases = {len(inputs) - 1: 0} if gathered is not None else {}

    def emit(ins, outs, place):
        x, y, c = place
        sibling, chips, mine, g = (x, y, 1 - c), _other_chips(x, y), 4 * x + 2 * y + c, outs[0]
        found = []
        for r0, nr in sends:
            src, dst = ins[0].at[pl.ds(r0, nr)], g.at[mine, pl.ds(r0, nr)]
            found += [(src, dst, None), (src, dst, sibling)] + [(src, dst, (px, py, c)) for px, py in chips]
        for r0, nr in forwards:
            for px, py in chips:
                block = 4 * px + 2 * py + c
                found.append((ins[-1].at[block, pl.ds(r0, nr)], g.at[block, pl.ds(r0, nr)], sibling))
        return found

    return _Job(inputs, [shape], aliases, emit, 4 * len(sends) + 3 * len(forwards), len(sends))


def _scatter_job(arrays, n_slots, route, prev=None, piece=None, into=None):
    shapes = [_sds((n_slots,) + ((into[1],) + v.shape[2:] if into else v.shape[1:]), v.dtype) for v in arrays]
    inputs = list(arrays) + (list(prev) if prev else [])
    aliases = {len(arrays) + i: i for i in range(len(arrays))} if prev else {}

    def emit(ins, outs, place):
        found = []
        for a in range(len(arrays)):
            for s in range(n_slots):
                block, to = route(s, *place)
                if into is not None:
                    found.append((ins[a].at[block], outs[a].at[s, pl.ds(into[0], ins[a].shape[1])], to))
                elif piece is None:
                    found.append((ins[a].at[block], outs[a].at[s], to))
                else:
                    found.append((ins[a].at[block, pl.ds(*piece)], outs[a].at[s, pl.ds(*piece)], to))
        return found

    return _Job(inputs, shapes, aliases, emit, len(arrays) * n_slots)


def _to_sibling(s, x, y, c):
    return 2 * s + (1 - c), (x, y, 1 - c)


def _to_owner_chip(s, x, y, c):
    px, py = _other_chips(x, y)[s]
    return 2 * px + py, (px, py, c)


def _to_owner(s, x, y, c):
    if s == 0:
        px, py, pc = x, y, 1 - c
    else:
        px, py = _other_chips(x, y)[(s - 1) % 3]
        pc = c if s <= 3 else 1 - c
    return 4 * px + 2 * py + pc, (px, py, pc)


def _rstd(x):
    return lax.rsqrt(jnp.mean(x * x, axis=-1, keepdims=True) + EPS)


def _rms_bwd(xhat, r, g, dy):
    dxh = dy * g
    return r * (dxh - xhat * jnp.mean(dxh * xhat, axis=-1, keepdims=True))


def _first_half(rows):
    lane = lax.broadcasted_iota(jnp.int32, (rows, PAIR), 1)
    return (lane % HEAD_DIM) < (HEAD_DIM // 2)


def _rot_half(v, first):
    return jnp.where(first, pltpu.roll(v, PAIR - HEAD_DIM // 2, 1), pltpu.roll(v, HEAD_DIM // 2, 1))


def _rope(v, cos, sin, first):
    return v * cos + _rot_half(v, first) * sin


def _rope_t(dv, cos, sin, first):
    return dv * cos + _rot_half(dv * sin, first)


def _dot(a, b):
    return jnp.dot(a, b, preferred_element_type=F32)


def _dot_nt(a, b):
    return lax.dot_general(a, b, (((1,), (1,)), ((), ())), preferred_element_type=F32)


def _dot_tn(a, b):
    return lax.dot_general(a, b, (((0,), (0,)), ((), ())), preferred_element_type=F32)


def _sigmoid(v):
    return 1.0 / (1.0 + jnp.exp(-v))


_GELU_K = math.sqrt(2.0 / math.pi)
_GELU_C = 0.044715


def _gelu(v):
    return v * (0.5 * (1.0 + jnp.tanh(_GELU_K * (v + _GELU_C * (v * v * v)))))


def _gelu_grad(v):
    t = jnp.tanh(_GELU_K * (v + _GELU_C * (v * v * v)))
    return 0.5 * (1.0 + t) + 0.5 * v * (1.0 - t * t) * (_GELU_K * (1.0 + 3.0 * _GELU_C * (v * v)))


def _attn_qkv_fwd(x, g, w, b, cos, sin, jobs=()):
    t, d = x.shape
    n = w.shape[1]
    kd = (n - d) // 2
    tm = _row_tile(t)
    ch = _pick(d, (512,))

    def body(x_ref, g_ref, w_ref, b_ref, cos_ref, sin_ref, h_ref, q_ref, k_ref, v_ref):
        xv = x_ref[...]
        hb = (xv * _rstd(xv) * g_ref[...]).astype(ACT)
        h_ref[...] = hb
        cosv, sinv = cos_ref[...], sin_ref[...]
        first = _first_half(tm)

        def proj(lo, width):
            return _dot(hb, w_ref[:, lo:lo + width]) + b_ref[:, lo:lo + width]

        for c in range(0, d, ch):
            y = proj(c, ch)
            for s in range(0, ch, PAIR):
                q_ref[:, c + s:c + s + PAIR] = (_rope(y[:, s:s + PAIR], cosv, sinv, first) * SCALE).astype(ACT)
        y = proj(d, kd)
        for s in range(0, kd, PAIR):
            k_ref[:, s:s + PAIR] = _rope(y[:, s:s + PAIR], cosv, sinv, first).astype(ACT)
        v_ref[...] = proj(d + kd, kd).astype(ACT)

    return _call(
        body, "attn_qkv_fwd", (t // tm,),
        [_rows(tm, d), _full((1, d)), _full((d, n)), _full((1, n)), _rows(tm, PAIR), _rows(tm, PAIR)],
        [_rows(tm, d), _rows(tm, d), _rows(tm, kd), _rows(tm, kd)],
        [_sds((t, d), ACT), _sds((t, d), ACT), _sds((t, kd), ACT), _sds((t, kd), ACT)],
        (x, g, w, b, cos, sin), ("parallel",), jobs=jobs)


STACK = GQA_GROUP * WINDOW


def _band_mask(has_prev):
    row = lax.broadcasted_iota(jnp.int32, (STACK, 2 * WINDOW), 0) % WINDOW
    col = lax.broadcasted_iota(jnp.int32, (STACK, 2 * WINDOW), 1)
    return ((col < WINDOW) & (col > row) & has_prev) | ((col >= WINDOW) & (col - WINDOW <= row))


def _lane_halves():
    lane = lax.broadcasted_iota(jnp.int32, (1, PAIR), 1)
    return lane < HEAD_DIM, lane >= HEAD_DIM


def _stack_heads(ref, h, halves):
    parts = []
    for p in range(2):
        pair = ref[:, (2 * h + p) * PAIR:(2 * h + p + 1) * PAIR]
        parts += [jnp.where(half, pair, jnp.zeros_like(pair)) for half in halves]
    return jnp.concatenate(parts, axis=0)


def _sink_column(sink_ref, h):
    row = lax.broadcasted_iota(jnp.int32, (STACK, 1), 0)
    col = jnp.full((STACK, 1), sink_ref[0, GQA_GROUP * h + GQA_GROUP - 1], F32)
    for j in range(GQA_GROUP - 2, -1, -1):
        col = jnp.where(row < (j + 1) * WINDOW, sink_ref[0, GQA_GROUP * h + j], col)
    return col


def _probs(qs, k2, valid, sink):
    s = jnp.where(valid, _dot_nt(qs, k2), MASKED)
    m = jnp.maximum(jnp.max(s, axis=-1, keepdims=True), sink)
    p = jnp.exp(s - m)
    psink = jnp.exp(sink - m)
    inv = 1.0 / (jnp.sum(p, axis=-1, keepdims=True) + psink)
    return p * inv, psink * inv


def _attn_fwd(q, kd_, vd, sinks, jobs=()):
    t, d = q.shape
    kd = kd_.shape[1]
    cur = lambda n: (n, 0)
    prev = lambda n: (jnp.maximum(n - 1, 0), 0)

    def body(sink_ref, q_ref, kc_ref, kp_ref, vc_ref, vp_ref, o_ref):
        valid = _band_mask(pl.program_id(0) > 0)
        halves = _lane_halves()
        for h in range(kd // PAIR):
            hs = slice(h * PAIR, (h + 1) * PAIR)
            k2 = jnp.concatenate([kp_ref[:, hs], kc_ref[:, hs]], axis=0)
            v2 = jnp.concatenate([vp_ref[:, hs], vc_ref[:, hs]], axis=0)
            vs = jnp.concatenate([jnp.where(half, v2, jnp.zeros_like(v2)) for half in halves], axis=0)
            pr, _ = _probs(_stack_heads(q_ref, h, halves), k2, valid, _sink_column(sink_ref, h))
            pr = pr.astype(ACT)
            for p in range(2):
                both = jnp.concatenate([pr[2 * p * WINDOW:(2 * p + 1) * WINDOW], pr[(2 * p + 1) * WINDOW:(2 * p + 2) * WINDOW]], axis=1)
                o_ref[:, (2 * h + p) * PAIR:(2 * h + p + 1) * PAIR] = _dot(both, vs).astype(ACT)

    kv_c, kv_p = pl.BlockSpec((WINDOW, kd), cur), pl.BlockSpec((WINDOW, kd), prev)
    return _call(
        body, "attn_fwd", (t // WINDOW,),
        [pl.BlockSpec(memory_space=pltpu.SMEM), pl.BlockSpec((WINDOW, d), cur), kv_c, kv_p, kv_c, kv_p],
        [pl.BlockSpec((WINDOW, d), cur)], [_sds((t, d), ACT)],
        (sinks, q, kd_, kd_, vd, vd), ("parallel",), jobs=jobs)


def _attn_bwd(q, kd_, vd, do, sinks, cos, sin, jobs=()):
    t, d = q.shape
    kd = kd_.shape[1]
    nb = t // WINDOW
    n_out = d + 2 * kd
    cur = lambda n: (jnp.minimum(n, nb - 1), 0)
    prev = lambda n: (jnp.maximum(jnp.minimum(n, nb - 1) - 1, 0), 0)
    late = lambda n: (jnp.maximum(n - 1, 0), 0)

    def body(sink_ref, q_ref, kc_ref, kp_ref, vc_ref, vp_ref, do_ref, cosc_ref, sinc_ref, cosp_ref, sinp_ref,
             out_ref, db_ref, dsink_ref, dq_keep, dk_keep, dv_keep):
        n = pl.program_id(0)

        @pl.when(n == 0)
        def _():
            for ref in (db_ref, dsink_ref, dq_keep, dk_keep, dv_keep):
                ref[...] = jnp.zeros_like(ref)

        valid = _band_mask(n > 0) & (n < nb)
        halves = _lane_halves()
        first = _first_half(WINDOW)
        slot = lax.broadcasted_iota(jnp.int32, dsink_ref.shape, 1)
        top = lax.broadcasted_iota(jnp.int32, dsink_ref.shape, 0) == 0
        dsink = jnp.zeros(dsink_ref.shape, F32)

        def emit(cols, done):
            out_ref[:, cols] = done.astype(ACT)
            db_ref[:, cols] += jnp.sum(done.astype(F32), axis=0, keepdims=True)

        for h in range(kd // PAIR):
            hs = slice(h * PAIR, (h + 1) * PAIR)
            k2 = jnp.concatenate([kp_ref[:, hs], kc_ref[:, hs]], axis=0)
            v2 = jnp.concatenate([vp_ref[:, hs], vc_ref[:, hs]], axis=0)
            qs, dos = _stack_heads(q_ref, h, halves), _stack_heads(do_ref, h, halves)
            pr, psink = _probs(qs, k2, valid, _sink_column(sink_ref, h))
            dp = _dot_nt(dos, v2)
            delta = jnp.sum(pr * dp, axis=-1, keepdims=True)
            ds = (pr * (dp - delta)).astype(ACT)
            leak = psink * delta
            for j in range(GQA_GROUP):
                share = jnp.sum(leak[j * WINDOW:(j + 1) * WINDOW], axis=0, keepdims=True)
                dsink = dsink - jnp.where(top & (slot == GQA_GROUP * h + j), share, 0.0)
            dqs = _dot(ds, k2)
            for p in range(2):
                ps = slice((2 * h + p) * PAIR, (2 * h + p + 1) * PAIR)
                dqp = jnp.where(halves[0], dqs[2 * p * WINDOW:(2 * p + 1) * WINDOW], dqs[(2 * p + 1) * WINDOW:(2 * p + 2) * WINDOW])
                emit(ps, dq_keep[:, ps])
                dq_keep[:, ps] = (_rope_t(dqp, cosc_ref[...], sinc_ref[...], first) * SCALE).astype(ACT)
            dk2 = _dot_tn(ds, qs)
            dv2 = _dot_tn(pr.astype(ACT), dos)
            ks = slice(d + h * PAIR, d + (h + 1) * PAIR)
            vs = slice(d + kd + h * PAIR, d + kd + (h + 1) * PAIR)
            emit(ks, dk_keep[:, hs] + _rope_t(dk2[:WINDOW], cosp_ref[...], sinp_ref[...], first))
            dk_keep[:, hs] = _rope_t(dk2[WINDOW:], cosc_ref[...], sinc_ref[...], first)
            emit(vs, dv_keep[:, hs] + dv2[:WINDOW])
            dv_keep[:, hs] = dv2[WINDOW:]
        dsink_ref[...] += dsink

    kv_c, kv_p = pl.BlockSpec((WINDOW, kd), cur), pl.BlockSpec((WINDOW, kd), prev)
    tab_c, tab_p = pl.BlockSpec((WINDOW, PAIR), cur), pl.BlockSpec((WINDOW, PAIR), prev)
    return _call(
        body, "attn_bwd", (nb + 1,),
        [pl.BlockSpec(memory_space=pltpu.SMEM), pl.BlockSpec((WINDOW, d), cur), kv_c, kv_p, kv_c, kv_p,
         pl.BlockSpec((WINDOW, d), cur), tab_c, tab_c, tab_p, tab_p],
        [pl.BlockSpec((WINDOW, n_out), late), _full((1, n_out)), _full((8, LANES))],
        [_sds((t, n_out), ACT), _sds((1, n_out), F32), _sds((8, LANES), F32)],
        (sinks, q, kd_, kd_, vd, vd, do, cos, sin, cos, sin), ("arbitrary",),
        scratch=[pltpu.VMEM((WINDOW, d), ACT), pltpu.VMEM((WINDOW, kd), F32), pltpu.VMEM((WINDOW, kd), F32)], jobs=jobs)


def _proj_res(a, w, b, g, x, name, target=None, jobs=()):
    blocked = a.ndim == 3
    t = a.shape[-2]
    k, d = w.shape
    tm = _row_tile(t)
    has_bias = b is not None

    def body(*refs):
        a_ref, w_ref = refs[:2]
        b_ref = refs[2] if has_bias else None
        g_ref, x_ref, m_ref, xo_ref = refs[2 + has_bias:]
        if blocked:
            c = a.shape[2]
            m = _dot(a_ref[0], w_ref[:c, :])
            for j in range(1, a.shape[0]):
                m = m + _dot(a_ref[j], w_ref[j * c:(j + 1) * c, :])
        else:
            m = _dot(a_ref[...], w_ref[...])
        if has_bias:
            m = m + b_ref[...]
        m_ref[...] = m.astype(ACT)
        xo_ref[...] = x_ref[...] + (m * _rstd(m)) * g_ref[...]

    def body_with_loss(a_ref, w_ref, g_ref, x_ref, t_ref, m_ref, dy_ref, loss_ref):
        @pl.when(pl.program_id(0) == 0)
        def _():
            loss_ref[...] = jnp.zeros_like(loss_ref)

        body(a_ref, w_ref, g_ref, x_ref, m_ref, dy_ref)
        err = dy_ref[...] - t_ref[...]
        dy_ref[...] = err * (1.0 / d)
        loss_ref[...] += 0.5 * jnp.sum(jnp.mean(err * err, axis=-1, keepdims=True), axis=0, keepdims=True)

    ins = [a, w] + ([b] if has_bias else []) + [g, x]
    a_spec = pl.BlockSpec((a.shape[0], tm, a.shape[2]), lambda i: (0, i, 0)) if blocked else _rows(tm, k)
    specs = [a_spec, _full((k, d))] + ([_full((1, d))] if has_bias else []) + [_full((1, d)), _rows(tm, d)]
    if target is not None:
        return _call(body_with_loss, name, (t // tm,), specs + [_rows(tm, d)], [_rows(tm, d), _rows(tm, d), _full((8, LANES))],
                     [_sds((t, d), ACT), _sds((t, d), F32), _sds((8, LANES), F32)], ins + [target], ("arbitrary",), jobs=jobs)
    return _call(body, name, (t // tm,), specs, [_rows(tm, d), _rows(tm, d)], [_sds((t, d), ACT), _sds((t, d), F32)], ins,
                 ("parallel",), jobs=jobs)


def _post_bwd(dres, m, g, w, mode, name, gu=None, jobs=()):
    t, d = dres.shape
    k = w.shape[0]
    tm = _row_tile(t)
    kc = _pick(k, (1408, 1024, 512))

    def body(*refs):
        d_ref, m_ref, g_ref, w_ref = refs[:4]
        gu_ref = refs[4] if mode == "ffn" else None
        outs = refs[4 + (mode == "ffn"):]
        dm_ref, da_ref, dg_ref = outs[:3]
        i = pl.program_id(0)

        @pl.when(i == 0)
        def _():
            dg_ref[...] = jnp.zeros_like(dg_ref)
            if mode == "attn":
                outs[3][...] = jnp.zeros_like(outs[3])

        dv, mv = d_ref[...], m_ref[...].astype(F32)
        r = _rstd(mv)
        mh = mv * r
        dg_ref[...] += jnp.sum(dv * mh, axis=0, keepdims=True)
        dm = _rms_bwd(mh, r, g_ref[...], dv)
        dmb = dm.astype(ACT)
        dm_ref[...] = dmb
        if mode == "attn":
            outs[3][...] += jnp.sum(dm, axis=0, keepdims=True)
        if mode == "ffn":
            nb, _, cb = gu.shape
            for j in range(nb // 2):
                da = _dot_nt(dmb, w_ref[j * cb:(j + 1) * cb, :]).astype(ACT)
                gate, up = gu_ref[j], gu_ref[nb // 2 + j]
                sg = _sigmoid(gate.astype(F32)).astype(ACT)
                gs = gate * sg
                da_ref[j] = da * up * (sg + gs - gs * sg)
                da_ref[nb // 2 + j] = da * gs
        else:
            for c in range(0, k, kc):
                da = _dot_nt(dmb, w_ref[c:c + kc, :])
                da_ref[:, c:c + kc] = da.astype(ACT)

    ins = [dres, m, g, w]
    specs = [_rows(tm, d), _rows(tm, d), _full((1, d)), _full((k, d))]
    if mode == "ffn":
        slab = pl.BlockSpec((gu.shape[0], tm, gu.shape[2]), lambda i: (0, i, 0))
        ins.append(gu)
        specs.append(slab)
        out_specs = [_rows(tm, d), slab, _full((1, d))]
        out_shape = [_sds((t, d), ACT), _sds(gu.shape, ACT), _sds((1, d), F32)]
    else:
        out_specs = [_rows(tm, d), _rows(tm, k), _full((1, d))]
        out_shape = [_sds((t, d), ACT), _sds((t, k), ACT), _sds((1, d), F32)]
    if mode == "attn":
        out_specs.append(_full((1, d)))
        out_shape.append(_sds((1, d), F32))
    return _call(body, name, (t // tm,), specs, out_specs, out_shape, ins, ("arbitrary",), jobs=jobs)


def _pre_bwd(dy, w, x, g, dres, name, transposed=False, jobs=()):
    t, d = x.shape
    tm = _row_tile(t)

    def body(dy_ref, w_ref, x_ref, g_ref, dres_ref, dx_ref, dg_ref):
        @pl.when(pl.program_id(0) == 0)
        def _():
            dg_ref[...] = jnp.zeros_like(dg_ref)

        dh = jnp.zeros((tm, d), F32)
        if w.ndim == 3 and transposed:
            for j in range(w.shape[0]):
                dh = dh + _dot(dy_ref[j], w_ref[j])
        elif w.ndim == 3:
            c = w.shape[2]
            for j in range(w.shape[0]):
                dh = dh + _dot_nt(dy_ref[j] if dy.ndim == 3 else dy_ref[:, j * c:(j + 1) * c], w_ref[j])
        else:
            n = w.shape[1]
            nc = _pick(n, (1408, 1024, 512))
            for c in range(0, n, nc):
                dh = dh + _dot_nt(dy_ref[:, c:c + nc], w_ref[:, c:c + nc])
        xv = x_ref[...]
        r = _rstd(xv)
        xh = xv * r
        dg_ref[...] += jnp.sum(dh * xh, axis=0, keepdims=True)
        dx_ref[...] = dres_ref[...] + _rms_bwd(xh, r, g_ref[...], dh)

    dy_spec = pl.BlockSpec((dy.shape[0], tm, dy.shape[2]), lambda i: (0, i, 0)) if dy.ndim == 3 else _rows(tm, dy.shape[1])
    return _call(
        body, name, (t // tm,),
        [dy_spec, _full(w.shape), _rows(tm, d), _full((1, d)), _rows(tm, d)],
        [_rows(tm, d), _full((1, d))], [_sds((t, d), F32), _sds((1, d), F32)],
        (dy, w, x, g, dres), ("arbitrary",), jobs=jobs)


def _wgrad(a, b, name, out_dtype=F32, out_block=None, transposed=False, k_part=None, jobs=()):
    t = a.shape[-2]
    tt = _pick(t, (1024,))
    steps = t // tt
    if a.ndim == 3:
        k_steps, _, tk = a.shape
        a_spec = pl.BlockSpec((None, tt, tk), lambda i, j, s: (i, s, 0))
    elif k_part is not None:
        part, parts = k_part
        tk, k_steps = a.shape[1] // parts, 1
        a_spec = pl.BlockSpec((tt, tk), lambda i, j, s: (s, part))
    else:
        tk = _pick(a.shape[1], (1024, 1408))
        k_steps = a.shape[1] // tk
        a_spec = pl.BlockSpec((tt, tk), lambda i, j, s: (s, i))
    k = k_steps * tk
    group = 1
    if b.ndim == 3:
        n_blocks, _, tn = b.shape
        group = 2 if n_blocks % 2 == 0 else 1
        n_steps, per = n_blocks // group, 1
        b_spec = pl.BlockSpec((group, tt, tn), lambda i, j, s: (j, s, 0))
        if transposed:
            out_spec, out_shape = pl.BlockSpec((group, tn, tk), lambda i, j, s: (j, 0, i)), _sds((n_blocks, tn, k), out_dtype)
        else:
            out_spec, out_shape = pl.BlockSpec((group, tk, tn), lambda i, j, s: (j, i, 0)), _sds((n_blocks, k, tn), out_dtype)
    else:
        n = b.shape[1]
        tn = _pick(n, (1024, 1408))
        n_steps = n // tn
        b_spec = pl.BlockSpec((tt, tn), lambda i, j, s: (s, j))
        if out_block is None:
            per = 0
            out_spec, out_shape = pl.BlockSpec((tk, tn), lambda i, j, s: (i, j)), _sds((k, n), out_dtype)
        else:
            per = tn // out_block
            out_spec = pl.BlockSpec((per, tk, out_block), lambda i, j, s: (j, i, 0))
            out_shape = _sds((n // out_block, k, out_block), out_dtype)

    def body(a_ref, b_ref, o_ref, acc_ref):
        s = pl.program_id(2)

        @pl.when(s == 0)
        def _():
            acc_ref[...] = jnp.zeros_like(acc_ref)

        if b.ndim == 3:
            av = a_ref[...]
            for p in range(group):
                acc_ref[p] += _dot_tn(b_ref[p], av) if transposed else _dot_tn(av, b_ref[p])
        else:
            acc_ref[...] += _dot_tn(a_ref[...], b_ref[...])

        @pl.when(s == steps - 1)
        def _():
            if b.ndim == 2 and per >= 1:
                for p in range(per):
                    o_ref[p] = acc_ref[:, p * out_block:(p + 1) * out_block].astype(out_dtype)
            else:
                o_ref[...] = acc_ref[...].astype(out_dtype)

    acc_shape = ((group, tn, tk) if transposed else (group, tk, tn)) if b.ndim == 3 else (tk, tn)
    res, job_res = _call(
        body, name, (k_steps, n_steps, steps), [a_spec, b_spec], [out_spec], [out_shape],
        (a, b), ("parallel", "parallel", "arbitrary"), scratch=[pltpu.VMEM(acc_shape, F32)], jobs=jobs)
    return res[0], job_res


def _ffn_up_fwd(x, g, w, jobs=()):
    t, d = x.shape
    nb, c, _ = w.shape
    tm = _row_tile(t)

    def body(x_ref, g_ref, w_ref, h_ref, gu_ref, a_ref):
        xv = x_ref[...]
        hb = (xv * _rstd(xv) * g_ref[...]).astype(ACT)
        h_ref[...] = hb
        for j in range(nb // 2):
            gate = _dot_nt(hb, w_ref[j])
            up = _dot_nt(hb, w_ref[nb // 2 + j])
            gu_ref[j] = gate.astype(ACT)
            gu_ref[nb // 2 + j] = up.astype(ACT)
            a_ref[j] = (gate * _sigmoid(gate) * up).astype(ACT)

    slab = lambda n: pl.BlockSpec((n, tm, c), lambda i: (0, i, 0))
    return _call(
        body, "ffn_up_fwd", (t // tm,),
        [_rows(tm, d), _full((1, d)), _full(w.shape)],
        [_rows(tm, d), slab(nb), slab(nb // 2)],
        [_sds((t, d), ACT), _sds((nb, t, c), ACT), _sds((nb // 2, t, c), ACT)],
        (x, g, w), ("parallel",), jobs=jobs)


def _causal(ws):
    row = lax.broadcasted_iota(jnp.int32, ws.shape, 0)
    col = lax.broadcasted_iota(jnp.int32, ws.shape, 1)
    return jnp.where(col <= row, ws, 0.0)


def _sgu_ln(v, lg, lb):
    mu = jnp.mean(v, axis=-1, keepdims=True)
    vc = v - mu
    rs = lax.rsqrt(jnp.mean(vc * vc, axis=-1, keepdims=True) + EPS)
    vh = vc * rs
    return vh, rs, vh * lg + lb


def _sgu_fwd(x, g, w, lg, lb, ws, bs_t, jobs=()):
    t, d = x.shape
    nb, _, c = w.shape
    n = nb * c
    groups = d // WINDOW
    tm = _row_tile(t)

    def body(x_ref, g_ref, w_ref, lg_ref, lb_ref, ws_ref, bs_ref, h_ref, z_ref, y_ref, zf_ref):
        xv = x_ref[...]
        hb = (xv * _rstd(xv) * g_ref[...]).astype(ACT)
        h_ref[...] = hb
        for j in range(nb):
            zf_ref[:, j * c:(j + 1) * c] = _dot(hb, w_ref[j])
        z_ref[...] = zf_ref[...].astype(ACT)
        for r in range(0, tm, WINDOW):
            u = _gelu(zf_ref[r:r + WINDOW, :d])
            _, _, vn = _sgu_ln(_gelu(zf_ref[r:r + WINDOW, d:]), lg_ref[...], lb_ref[...])
            for gi in range(groups):
                gs = slice(gi * WINDOW, (gi + 1) * WINDOW)
                mixed = _dot(_causal(ws_ref[gi]).astype(ACT), vn[:, gs].astype(ACT)) + bs_ref[:, gi:gi + 1]
                y_ref[r:r + WINDOW, gs] = (u[:, gs] * mixed).astype(ACT)

    vec = _full((1, d))
    return _call(
        body, "sgu_fwd", (t // tm,),
        [_rows(tm, d), vec, _full((nb, d, c)), vec, vec, _full((groups, WINDOW, WINDOW)), _full((WINDOW, groups))],
        [_rows(tm, d), _rows(tm, n), _rows(tm, d)], [_sds((t, d), ACT), _sds((t, n), ACT), _sds((t, d), ACT)],
        (x, g, w, lg, lb, ws, bs_t), ("parallel",), scratch=[pltpu.VMEM((tm, n), F32)], jobs=jobs)


def _sgu_mix_bwd(z, dy, lg, lb, ws, bs_t, jobs=()):
    t, n = z.shape
    d = n // 2
    groups = d // WINDOW
    tm = _row_tile(t)

    def body(z_ref, dy_ref, lg_ref, lb_ref, ws_ref, bs_ref, dz_ref, dlg_ref, dlb_ref, dws_ref, dbs_ref):
        @pl.when(pl.program_id(0) == 0)
        def _():
            for ref in (dlg_ref, dlb_ref, dws_ref, dbs_ref):
                ref[...] = jnp.zeros_like(ref)

        lane = lax.broadcasted_iota(jnp.int32, (WINDOW, groups), 1)
        for c in range(0, tm, WINDOW):
            rows = slice(c, c + WINDOW)
            zu, zv = z_ref[rows, :d].astype(F32), z_ref[rows, d:].astype(F32)
            u = _gelu(zu)
            vh, rs, vn = _sgu_ln(_gelu(zv), lg_ref[...], lb_ref[...])
            dyv = dy_ref[rows, :].astype(F32)
            dvn_parts = []
            for gi in range(groups):
                gs = slice(gi * WINDOW, (gi + 1) * WINDOW)
                wsg = _causal(ws_ref[gi]).astype(ACT)
                vng = vn[:, gs].astype(ACT)
                mixed = _dot(wsg, vng) + bs_ref[:, gi:gi + 1]
                dz_ref[rows, gs] = (dyv[:, gs] * mixed * _gelu_grad(zu[:, gs])).astype(ACT)
                dmix = dyv[:, gs] * u[:, gs]
                dmb = dmix.astype(ACT)
                dvn_parts.append(_dot_tn(wsg, dmb))
                dws_ref[:, gs] += _causal(_dot_nt(dmb, vng))
                dbs_ref[...] += jnp.where(lane == gi, jnp.sum(dmix, axis=-1, keepdims=True), 0.0)
            dvn = jnp.concatenate(dvn_parts, axis=-1)
            dlg_ref[...] += jnp.sum(dvn * vh, axis=0, keepdims=True)
            dlb_ref[...] += jnp.sum(dvn, axis=0, keepdims=True)
            dvh = dvn * lg_ref[...]
            dv = rs * (dvh - jnp.mean(dvh, axis=-1, keepdims=True) - vh * jnp.mean(dvh * vh, axis=-1, keepdims=True))
            dz_ref[rows, d:] = (dv * _gelu_grad(zv)).astype(ACT)

    vec = _full((1, d))
    return _call(
        body, "sgu_mix_bwd", (t // tm,),
        [_rows(tm, n), _rows(tm, d), vec, vec, _full((groups, WINDOW, WINDOW)), _full((WINDOW, groups))],
        [_rows(tm, n), vec, vec, _full((WINDOW, d)), _full((WINDOW, groups))],
        [_sds((t, n), ACT), _sds((1, d), F32), _sds((1, d), F32), _sds((WINDOW, d), F32), _sds((WINDOW, groups), F32)],
        (z, dy, lg, lb, ws, bs_t), ("arbitrary",), jobs=jobs)


AG_COPIES = 7


def _prepare(sources, dtypes, n_gather, name):
    n = len(sources)
    arrays, where = [], []
    for parts, layer in sources:
        found = []
        for part in parts:
            known = [i for i, v in enumerate(arrays) if v is part]
            if not known:
                arrays.append(part)
            found.append(known[0] if known else len(arrays) - 1)
        where.append((found, layer))
    shapes = [(sum(p.shape[1] for p in parts), parts[0].shape[2]) for parts, _ in sources]

    def body(*refs):
        ins, refs = refs[:len(arrays)], refs[len(arrays):]
        stage, outs = refs[:n], refs[n:n + n_gather]
        send, recv, local_sem = refs[n + n_gather:]
        x, y, c = _place()
        sibling = (x, y, 1 - c)
        chips = _other_chips(x, y)
        mine = 4 * x + 2 * y + c

        def copy(a, k, block, to, src=None):
            dst = outs[a].at[block]
            return pltpu.make_async_remote_copy(
                src_ref=dst if src is None else src, dst_ref=dst, send_sem=send.at[a * AG_COPIES + k],
                recv_sem=recv.at[a * AG_COPIES + k], device_id=to, device_id_type=MESH)

        def cast(a):
            found, layer = where[a]
            at = 0
            for i in found:
                rows = arrays[i].shape[1]
                stage[a][at:at + rows, :] = ins[i][layer].astype(dtypes[a])
                at += rows

        for a in range(n_gather):
            cast(a)
        started = []
        for a in range(n_gather):
            own = pltpu.make_async_copy(stage[a], outs[a].at[mine], local_sem.at[a])
            own.start()
            started.append(own)
        sends = []
        for a in range(n_gather):
            sends.append(copy(a, 0, mine, sibling, src=stage[a]))
            sends += [copy(a, 1 + j, mine, (*chip, c), src=stage[a]) for j, chip in enumerate(chips)]
        for cp in sends:
            cp.start()
        for a in range(n_gather, n):
            cast(a)
        for j, (px, py) in enumerate(chips):
            block = 4 * px + 2 * py + c
            for a in range(n_gather):
                copy(a, 1 + j, block, (x, y, c)).wait_recv()
                forward = copy(a, 4 + j, block, sibling)
                forward.start()
                sends.append(forward)
        for a in range(n_gather):
            copy(a, 0, 4 * x + 2 * y + (1 - c), (x, y, c)).wait_recv()
            for j, (px, py) in enumerate(chips):
                copy(a, 4 + j, 4 * px + 2 * py + (1 - c), (x, y, c)).wait_recv()
        for cp in sends:
            cp.wait_send()
        for own in started:
            own.wait()

    res = pl.pallas_call(
        body, name=name,
        in_specs=[IN_VMEM] * len(arrays), out_specs=[IN_VMEM] * n + [ANY] * n_gather,
        out_shape=[_sds(s, dt) for s, dt in zip(shapes, dtypes)]
        + [_sds((N_DEV,) + s, dt) for s, dt in zip(shapes[:n_gather], dtypes)],
        scratch_shapes=[pltpu.SemaphoreType.DMA((n_gather * AG_COPIES,)), pltpu.SemaphoreType.DMA((n_gather * AG_COPIES,)),
                        pltpu.SemaphoreType.DMA((n_gather,))],
        compiler_params=pltpu.CompilerParams(vmem_limit_bytes=VMEM_LIMIT_BYTES),
    )(*arrays)
    return list(res[:n]), list(res[n:])


def _pair_sum(g, r, core, name):
    _, _, rows, cols = g.shape
    tr = _pick(rows, (512, 256, 128))

    def body(core_ref, g_ref, r_ref, p_ref):
        del core_ref
        p_ref[...] = (g_ref[...].astype(F32) + r_ref[...].astype(F32)).astype(p_ref.dtype)

    return pl.pallas_call(
        body, name=name,
        grid_spec=pltpu.PrefetchScalarGridSpec(
            num_scalar_prefetch=1, grid=(4, rows // tr),
            in_specs=[pl.BlockSpec((None, None, tr, cols), lambda q, i, core_ref: (q, core_ref[0], i, 0)),
                      pl.BlockSpec((None, tr, cols), lambda q, i, core_ref: (q, i, 0))],
            out_specs=pl.BlockSpec((None, tr, cols), lambda q, i, core_ref: (q, i, 0))),
        out_shape=_sds((4, rows, cols), g.dtype),
        compiler_params=_params("parallel", "parallel"),
    )(core, g, r)


def _adam(w, g, m, v):
    m2 = ADAM_B1 * m + (1.0 - ADAM_B1) * g
    v2 = ADAM_B2 * v + (1.0 - ADAM_B2) * (g * g)
    m_hat = m2 / (1.0 - ADAM_B1 ** ADAM_STEP)
    v_hat = v2 / (1.0 - ADAM_B2 ** ADAM_STEP)
    return -ADAM_LR * (m_hat / (jnp.sqrt(v_hat) + ADAM_EPS) + ADAM_WD * w), m2, v2


def _shard_update(own, index, received, w, m, v, layer, so_far, name):
    _, rows, cols = w.shape
    n_recv = received.shape[0]
    tr = _pick(rows, (512, 256, 128))

    def body(index_ref, p_ref, q_ref, w_ref, m_ref, v_ref, *rest):
        del index_ref
        g_out, d_out, m_out, v_out = rest[-4:]
        g = p_ref[...].astype(F32)
        for s in range(n_recv):
            g = g + q_ref[s].astype(F32)
        g_out[...] = g
        d_out[...], m_out[...], v_out[...] = _adam(w_ref[...], g, m_ref[...], v_ref[...])

    tile = pl.BlockSpec((None, tr, cols), lambda i, index_ref: (layer, i, 0))
    kept = list(so_far) if so_far is not None else []
    return pl.pallas_call(
        body, name=name,
        grid_spec=pltpu.PrefetchScalarGridSpec(
            num_scalar_prefetch=1, grid=(rows // tr,),
            in_specs=[pl.BlockSpec((None, tr, cols), lambda i, index_ref: (index_ref[0], i, 0)),
                      pl.BlockSpec((n_recv, tr, cols), lambda i, index_ref: (0, i, 0)), tile, tile, tile] + [ANY] * len(kept),
            out_specs=[tile] * 4),
        out_shape=[_sds(w.shape, F32)] * 4,
        input_output_aliases={6 + i: i for i in range(len(kept))},
        compiler_params=_params("parallel"),
    )(index, own, received, w, m, v, *kept)


def _natural_pieces(shape, r, c):
    if tuple(shape[-2:]) == (r, c) and all(n == 1 for n in shape[:-2]):
        return [((0,) * (len(shape) - 2), (0, c))]
    groups, width = shape[1], shape[3]
    assert len(shape) == 4 and shape[0] == 1 and shape[2] == r and groups * width == c, (shape, r, c)
    return [((0, g), (g * width, width)) for g in range(groups)]


def _replicated_update(shares, where, params, name):
    flat = [a for p in params if p is not None for a in p]

    def body(s_ref, *refs):
        ins, outs = refs[:len(flat)], refs[len(flat):]
        total = s_ref[0]
        for dev in range(1, N_DEV):
            total = total + s_ref[dev]
        i_at = o_at = 0
        for ranges, p in zip(where, params):
            chunks = [total[r0:r0 + r, :c] for r0, r, c in ranges]
            g2d = chunks[0] if len(chunks) == 1 else jnp.concatenate(chunks, axis=1)
            if p is None:
                outs[o_at][...] = g2d
                o_at += 1
                continue
            w_ref, m_ref, v_ref = ins[i_at:i_at + 3]
            for idx, (c0, cols) in _natural_pieces(p[0].shape, *g2d.shape):
                at = idx + (slice(None), slice(None))
                g = g2d[:, c0:c0 + cols]
                outs[o_at][at] = g
                outs[o_at + 1][at], outs[o_at + 2][at], outs[o_at + 3][at] = _adam(w_ref[at], g, m_ref[at], v_ref[at])
            i_at, o_at = i_at + 3, o_at + 4

    out_shape = []
    for ranges, p in zip(where, params):
        out_shape += [_sds((ranges[0][1], sum(c for _, _, c in ranges)), F32)] if p is None else [_sds(p[0].shape, F32)] * 4
    res = pl.pallas_call(
        body, name=name, out_shape=out_shape, compiler_params=pltpu.CompilerParams(vmem_limit_bytes=VMEM_LIMIT_BYTES),
    )(shares, *flat)
    out, at = [], 0
    for p in params:
        out.append(list(res[at:at + (1 if p is None else 4)]))
        at += 1 if p is None else 4
    return out


def _rope_tables(t):
    half = HEAD_DIM // 2
    inv_freq = ROPE_THETA ** (-(jnp.arange(half, dtype=F32) * 2.0) / HEAD_DIM)
    ang = jnp.arange(t, dtype=jnp.int32).astype(F32)[:, None] * inv_freq[None, :]
    cos, sin = jnp.cos(ang), jnp.sin(ang)
    return jnp.tile(jnp.concatenate([cos, cos], axis=1), (1, 2)), jnp.tile(jnp.concatenate([-sin, sin], axis=1), (1, 2))


def _dup_heads(w, d):
    kv = (w.shape[-1] - d) // 2
    lead = w.shape[:-1]

    def dup(part):
        heads = part.reshape(lead + (kv // HEAD_DIM, 1, HEAD_DIM))
        return jnp.broadcast_to(heads, lead + (kv // HEAD_DIM, 2, HEAD_DIM)).reshape(lead + (2 * kv,))

    return jnp.concatenate([w[..., :d], dup(w[..., d:d + kv]), dup(w[..., d + kv:])], axis=-1)


def _fold_heads(dw, d):
    kv = (dw.shape[-1] - d) // 4
    lead = dw.shape[:-1]

    def fold(part):
        return part.reshape(lead + (kv // HEAD_DIM, 2, HEAD_DIM)).sum(axis=-2).reshape(lead + (kv,))

    return jnp.concatenate([dw[..., :d], fold(dw[..., d:d + 2 * kv]), fold(dw[..., d + 2 * kv:])], axis=-1)


def _columns_full(gathered):
    _, rows, c = gathered.shape
    return jnp.transpose(gathered, (1, 0, 2)).reshape(rows, N_DEV * c)


def _rows_full(gathered):
    return gathered.reshape(-1, gathered.shape[-1])


def _columns_blocked(full):
    rows, cols = full.shape
    return jnp.transpose(full.reshape(rows, N_DEV, cols // N_DEV), (1, 0, 2)).astype(ACT)


def _rows_blocked(full):
    return full.reshape(N_DEV, full.shape[0] // N_DEV, full.shape[1]).astype(ACT)


def _pack_rows(parts, width):
    rows, where, at = [], [], 0
    for v in parts:
        r, c = v.shape
        n_rows = -(-r // 8) * 8
        chunks = []
        for c0 in range(0, c, width):
            chunk = v[:, c0:c0 + width].astype(F32)
            rows.append(jnp.pad(chunk, ((0, n_rows - r), (0, width - chunk.shape[1]))))
            chunks.append((at, r, chunk.shape[1]))
            at += n_rows
        where.append(chunks)
    return jnp.concatenate(rows, axis=0), where


def _halves(block):
    half = block.shape[0] // 2
    return (0, half), (half, half)


def _whole(block):
    return (0, block.shape[0])


EARLY = ("attn_w_qkv", "sgu_ln", "attn_w_o")
LATE = ("sgu_w_in", "sgu_w_out", "gu0", "gu1", "dn0", "dn1")
COLUMN_SHARDED = ("attn_w_qkv", "sgu_w_in", "gu0", "gu1")
BEFORE_ATTN = ("norm_mix_post", "norm_ffn_pre", "norm_ffn_post", "attn_b_o", "sgu_w_spatial", "sgu_b_spatial")
AFTER_ATTN = ("attn_b_qkv", "attn_sinks")
LAST = ("norm_mix_pre",)
WEIGHTS = ("norm_mix_pre", "norm_mix_post", "norm_ffn_pre", "norm_ffn_post", "attn_w_qkv", "attn_b_qkv", "attn_sinks", "attn_w_o",
           "attn_b_o", "sgu_w_in", "sgu_ln_g", "sgu_ln_b", "sgu_w_spatial", "sgu_b_spatial", "sgu_w_out", "ffn_w_gate_up", "ffn_w_down")


LAYER_OF = {"gu0": ("ffn_w_gate_up", 0), "gu1": ("ffn_w_gate_up", 1), "dn0": ("ffn_w_down", 0), "dn1": ("ffn_w_down", 1)}


def _source(tree, name):
    if name == "sgu_ln":
        return [tree["sgu_ln_g"][None], tree["sgu_ln_b"][None]], 0
    leaf, layer = LAYER_OF.get(name, (name, 0))
    return [tree[leaf]], layer


def kernel(x, norm_mix_pre, norm_mix_post, norm_ffn_pre, norm_ffn_post, attn_w_qkv, attn_b_qkv, attn_sinks, attn_w_o, attn_b_o, sgu_w_in, sgu_ln_g, sgu_ln_b, sgu_w_spatial, sgu_b_spatial, sgu_w_out, ffn_w_gate_up, ffn_w_down, loss_target, m_norm_mix_pre, m_norm_mix_post, m_norm_ffn_pre, m_norm_ffn_post, m_attn_w_qkv, m_attn_b_qkv, m_attn_sinks, m_attn_w_o, m_attn_b_o, m_sgu_w_in, m_sgu_ln_g, m_sgu_ln_b, m_sgu_w_spatial, m_sgu_b_spatial, m_sgu_w_out, m_ffn_w_gate_up, m_ffn_w_down, v_norm_mix_pre, v_norm_mix_post, v_norm_ffn_pre, v_norm_ffn_post, v_attn_w_qkv, v_attn_b_qkv, v_attn_sinks, v_attn_w_o, v_attn_b_o, v_sgu_w_in, v_sgu_ln_g, v_sgu_ln_b, v_sgu_w_spatial, v_sgu_b_spatial, v_sgu_w_out, v_ffn_w_gate_up, v_ffn_w_down):
    w = dict(norm_mix_pre=norm_mix_pre, norm_mix_post=norm_mix_post, norm_ffn_pre=norm_ffn_pre, norm_ffn_post=norm_ffn_post,
             attn_w_qkv=attn_w_qkv, attn_b_qkv=attn_b_qkv, attn_sinks=attn_sinks, attn_w_o=attn_w_o, attn_b_o=attn_b_o,
             sgu_w_in=sgu_w_in, sgu_ln_g=sgu_ln_g, sgu_ln_b=sgu_ln_b, sgu_w_spatial=sgu_w_spatial, sgu_b_spatial=sgu_b_spatial,
             sgu_w_out=sgu_w_out, ffn_w_gate_up=ffn_w_gate_up, ffn_w_down=ffn_w_down)
    mom = dict(norm_mix_pre=m_norm_mix_pre, norm_mix_post=m_norm_mix_post, norm_ffn_pre=m_norm_ffn_pre, norm_ffn_post=m_norm_ffn_post,
               attn_w_qkv=m_attn_w_qkv, attn_b_qkv=m_attn_b_qkv, attn_sinks=m_attn_sinks, attn_w_o=m_attn_w_o, attn_b_o=m_attn_b_o,
               sgu_w_in=m_sgu_w_in, sgu_ln_g=m_sgu_ln_g, sgu_ln_b=m_sgu_ln_b, sgu_w_spatial=m_sgu_w_spatial,
               sgu_b_spatial=m_sgu_b_spatial, sgu_w_out=m_sgu_w_out, ffn_w_gate_up=m_ffn_w_gate_up, ffn_w_down=m_ffn_w_down)
    var = dict(norm_mix_pre=v_norm_mix_pre, norm_mix_post=v_norm_mix_post, norm_ffn_pre=v_norm_ffn_pre, norm_ffn_post=v_norm_ffn_post,
               attn_w_qkv=v_attn_w_qkv, attn_b_qkv=v_attn_b_qkv, attn_sinks=v_attn_sinks, attn_w_o=v_attn_w_o, attn_b_o=v_attn_b_o,
               sgu_w_in=v_sgu_w_in, sgu_ln_g=v_sgu_ln_g, sgu_ln_b=v_sgu_ln_b, sgu_w_spatial=v_sgu_w_spatial,
               sgu_b_spatial=v_sgu_b_spatial, sgu_w_out=v_sgu_w_out, ffn_w_gate_up=v_ffn_w_gate_up, ffn_w_down=v_ffn_w_down)
    w, mom, var = [{**tree, "ffn_w_gate_up": jnp.swapaxes(tree["ffn_w_gate_up"], 1, 2)} for tree in (w, mom, var)]
    xs, target = x[0], loss_target[0]
    t, d = xs.shape
    row = lambda v: v.reshape(1, -1).astype(F32)
    core = lax.axis_index("c").astype(jnp.int32).reshape(1)
    chip = (2 * lax.axis_index("x") + lax.axis_index("y")).astype(jnp.int32).reshape(1)
    me = (4 * lax.axis_index("x") + 2 * lax.axis_index("y") + lax.axis_index("c")).astype(jnp.int32).reshape(1)
    names = EARLY + LATE
    staged, early = _prepare([_source(w, n) for n in names], [F32 if n == "sgu_ln" else ACT for n in names], len(EARLY),
                             "weights_prepare")
    stage = dict(zip(names, staged))
    w_qkv = _dup_heads(_columns_full(early[0]), d)
    lg, lb = row(early[1][:, 0, :]), row(early[1][:, 1, :])
    w_o = _rows_full(early[2])
    b_qkv = _dup_heads(row(attn_b_qkv), d)
    sinks = attn_sinks.reshape(1, -1).astype(F32)
    ws = sgu_w_spatial[0].astype(F32)
    bs_t = sgu_b_spatial[0].astype(F32).T
    cos, sin = _rope_tables(t)
    gather = lambda name, so_far, **pieces: _gather_job(stage[name], so_far, **pieces)
    a_half = lambda name: _halves(stage[name])[0]
    b_half = lambda name: _halves(stage[name])[1]
    whole = lambda name: _whole(stage[name])

    quarter = lambda name, i: (i * stage[name].shape[0] // 4, stage[name].shape[0] // 4)
    (h0, q, kdup, vdup), ((g_gu0,),) = _attn_qkv_fwd(
        xs, row(norm_mix_pre[0]), w_qkv, b_qkv, cos, sin, jobs=[gather("gu0", None, sends=[quarter("gu0", 0)])])
    (o,), ((g_gu0,),) = _attn_fwd(q, kdup, vdup, sinks, jobs=[
        gather("gu0", g_gu0, sends=[quarter("gu0", i) for i in (1, 2, 3)], forwards=[quarter("gu0", 0)])])
    (m0, x1), ((g_gu0,), (g_dn0,)) = _proj_res(o, w_o, row(attn_b_o), row(norm_mix_post[0]), xs, "attn_out_fwd", jobs=[
        gather("gu0", g_gu0, forwards=[quarter("gu0", i) for i in (1, 2, 3)]), gather("dn0", None, sends=[whole("dn0")])])
    w_gu0 = g_gu0
    (h1, gu0, a0), ((g_dn0,), (g_in,), (g_gu1,)) = _ffn_up_fwd(x1, row(norm_ffn_pre[0]), w_gu0, jobs=[
        gather("dn0", g_dn0, forwards=[whole("dn0")]), gather("sgu_w_in", None, sends=[whole("sgu_w_in")]),
        gather("gu1", None, sends=[quarter("gu1", 0), quarter("gu1", 1)])])
    w_dn0 = _rows_full(g_dn0)
    (f0, x2), ((g_in,), (g_out,), (g_gu1,)) = _proj_res(a0, w_dn0, None, row(norm_ffn_post[0]), x1, "ffn_down_fwd0", jobs=[
        gather("sgu_w_in", g_in, forwards=[whole("sgu_w_in")]), gather("sgu_w_out", None, sends=[whole("sgu_w_out")]),
        gather("gu1", g_gu1, sends=[quarter("gu1", 2)], forwards=[quarter("gu1", 0), quarter("gu1", 1)])])
    w_in = g_in
    (h2, z, y), ((g_out,), (g_gu1,), (g_dn1,)) = _sgu_fwd(x2, row(norm_mix_pre[1]), w_in, lg, lb, ws, bs_t, jobs=[
        gather("sgu_w_out", g_out, forwards=[whole("sgu_w_out")]),
        gather("gu1", g_gu1, sends=[quarter("gu1", 3)], forwards=[quarter("gu1", 2)]), gather("dn1", None, sends=[a_half("dn1")])])
    w_out = _rows_full(g_out)
    (m1, x3), ((g_gu1,), (g_dn1,)) = _proj_res(y, w_out, None, row(norm_mix_post[1]), x2, "sgu_out_fwd", jobs=[
        gather("gu1", g_gu1, forwards=[quarter("gu1", 3)]), gather("dn1", g_dn1, sends=[b_half("dn1")], forwards=[a_half("dn1")])])
    w_gu1 = g_gu1
    (h3, gu1, a1), ((g_dn1,),) = _ffn_up_fwd(x3, row(norm_ffn_pre[1]), w_gu1, jobs=[gather("dn1", g_dn1, forwards=[b_half("dn1")])])
    w_dn1 = _rows_full(g_dn1)
    (f1, dx4, loss_tile), _ = _proj_res(a1, w_dn1, None, row(norm_ffn_post[1]), x3, "ffn_down_fwd1", target=target)

    to_sibling = lambda g: _scatter_job([g], 4, _to_sibling)
    pair = lambda g, r, name: _pair_sum(g.reshape((4, 2) + g.shape[1:]), r, core, "pair_sum_" + name)
    to_chips = lambda p, prev=None, piece=None: _scatter_job([p], 3, _to_owner_chip, prev=prev, piece=piece)
    out_g, out_d, out_m, out_v = {}, {}, {}, {}

    trees = [{**tree, "sgu_ln": jnp.stack([tree["sgu_ln_g"], tree["sgu_ln_b"]], axis=1)} for tree in (w, mom, var)]
    so_far = {}

    def update(name, own, index, received):
        leaf, layer = LAYER_OF.get(name, (name, 0))
        so_far[leaf] = _shard_update(own, index, received, *[tree[leaf] for tree in trees], layer, so_far.get(leaf), "update_" + name)
        for out, val in zip((out_g, out_d, out_m, out_v), so_far[leaf]):
            out[leaf] = val

    def finish(names, n_extra, shares, where, name):
        res = _replicated_update(shares, where, [(w[n], mom[n], var[n]) for n in names] + [None] * n_extra, name)
        for n, vals in zip(names, res):
            out_g[n], out_d[n], out_m[n], out_v[n] = vals
        return [vals[0] for vals in res[len(names):]]

    first_half = lambda p: _halves(p[0])[0]
    second_half = lambda p: _halves(p[0])[1]
    (df1, dgu1, dg_ffn_post1), _ = _post_bwd(dx4, f1, row(norm_ffn_post[1]), w_dn1, "ffn", "ffn_down_bwd1", gu1)
    b_gu1, _ = _wgrad(h3, dgu1, "ffn_up_wgrad1", ACT, transposed=True)
    dw_dn1, ((r_gu1,),) = _wgrad(a1, df1, "ffn_down_wgrad1", ACT, jobs=[to_sibling(b_gu1)])
    b_dn1 = _rows_blocked(dw_dn1)
    p_gu1 = pair(b_gu1, r_gu1, "gu1")
    (dx3, dg_ffn_pre1), ((q_gu1,), (r_dn1,)) = _pre_bwd(dgu1, w_gu1, x3, row(norm_ffn_pre[1]), dx4, "ffn_up_bwd1", True, jobs=[
        to_chips(p_gu1, piece=first_half(p_gu1)), to_sibling(b_dn1)])
    p_dn1 = pair(b_dn1, r_dn1, "dn1")
    (dm1, dy, dg_mix_post1), _ = _post_bwd(dx3, m1, row(norm_mix_post[1]), w_out, "sgu", "sgu_out_bwd")
    dw_out, _ = _wgrad(y, dm1, "sgu_out_wgrad", ACT)
    b_out = _rows_blocked(dw_out)
    (dz, dlg, dlb, dws, dbs_t), ((q_gu1,), (r_out,)) = _sgu_mix_bwd(z, dy, lg, lb, ws, bs_t, jobs=[
        to_chips(p_gu1, prev=[q_gu1], piece=second_half(p_gu1)), to_sibling(b_out)])
    update("gu1", p_gu1, chip, q_gu1)
    p_out = pair(b_out, r_out, "sgu_w_out")
    b_in, ((q_out,),) = _wgrad(h2, dz, "sgu_in_wgrad", ACT, out_block=stage["sgu_w_in"].shape[1], jobs=[to_chips(p_out)])
    update("sgu_w_out", p_out, chip, q_out)
    b_ln = jnp.stack([dlg.reshape(N_DEV, -1), dlb.reshape(N_DEV, -1)], axis=1)
    (dx2, dg_mix_pre1), ((r_in,), (r_ln,)) = _pre_bwd(dz, w_in, x2, row(norm_mix_pre[1]), dx3, "sgu_in_bwd",
                                                      jobs=[to_sibling(b_in), to_sibling(b_ln)])
    p_in, p_ln = pair(b_in, r_in, "sgu_w_in"), pair(b_ln, r_ln, "sgu_ln")
    (df0, dgu0, dg_ffn_post0), ((q_dn1,),) = _post_bwd(dx2, f0, row(norm_ffn_post[0]), w_dn0, "ffn", "ffn_down_bwd0", gu0,
                                                      jobs=[to_chips(p_dn1)])
    update("dn1", p_dn1, chip, q_dn1)
    b_gu0, ((q_in,), (q_ln,)) = _wgrad(h1, dgu0, "ffn_up_wgrad0", ACT, transposed=True, jobs=[to_chips(p_in), to_chips(p_ln)])
    update("sgu_w_in", p_in, chip, q_in)
    update("sgu_ln", p_ln, chip, q_ln)
    dw_dn0, ((r_gu0,),) = _wgrad(a0, df0, "ffn_down_wgrad0", ACT, jobs=[to_sibling(b_gu0)])
    b_dn0 = _rows_blocked(dw_dn0)
    p_gu0 = pair(b_gu0, r_gu0, "gu0")
    (dx1, dg_ffn_pre0), ((q_gu0,), (r_dn0,)) = _pre_bwd(dgu0, w_gu0, x1, row(norm_ffn_pre[0]), dx2, "ffn_up_bwd0", True, jobs=[
        to_chips(p_gu0, piece=first_half(p_gu0)), to_sibling(b_dn0)])
    p_dn0 = pair(b_dn0, r_dn0, "dn0")
    (dm0, do, dg_mix_post0, db_o), _ = _post_bwd(dx1, m0, row(norm_mix_post[0]), w_o, "attn", "attn_out_bwd")
    dw_o, _ = _wgrad(o, dm0, "attn_out_wgrad", ACT)
    b_o = _rows_blocked(dw_o)
    grads = {
        "norm_mix_post": jnp.concatenate([dg_mix_post0, dg_mix_post1], axis=0),
        "norm_ffn_pre": jnp.concatenate([dg_ffn_pre0, dg_ffn_pre1], axis=0),
        "norm_ffn_post": jnp.concatenate([dg_ffn_post0, dg_ffn_post1], axis=0),
        "attn_b_o": db_o,
        "sgu_w_spatial": dws,
        "sgu_b_spatial": dbs_t.T,
    }
    shares1, where1 = _pack_rows([grads[name] for name in BEFORE_ATTN], d)
    (dqkv, db_qkv, dsink), ((q_gu0,), (q_dn0,), (r_o,), (g_shares1,)) = _attn_bwd(q, kdup, vdup, do, sinks, cos, sin, jobs=[
        to_chips(p_gu0, prev=[q_gu0], piece=second_half(p_gu0)), to_chips(p_dn0), to_sibling(b_o),
        _gather_job(shares1, None, sends=[_whole(shares1)])])
    update("gu0", p_gu0, chip, q_gu0)
    update("dn0", p_dn0, chip, q_dn0)
    p_o = pair(b_o, r_o, "attn_w_o")
    grads.update({"attn_b_qkv": _fold_heads(db_qkv, d), "attn_sinks": dsink[:1, :d // HEAD_DIM]})
    shares2, where2 = _pack_rows([grads[name] for name in AFTER_ATTN] + [loss_tile[:1, :1]], d)
    blocked_half = lambda dw: _columns_blocked(_fold_heads(dw, d))
    dw_a, ((q_o,),) = _wgrad(h0, dqkv, "attn_qkv_wgrad_a", k_part=(0, 2), jobs=[to_chips(p_o)])
    update("attn_w_o", p_o, chip, q_o)
    b_qkv_a = blocked_half(dw_a)
    dw_b, ((q_qkv,), (g_shares1,), (g_shares2,)) = _wgrad(h0, dqkv, "attn_qkv_wgrad_b", k_part=(1, 2), jobs=[
        _scatter_job([b_qkv_a], N_DEV - 1, _to_owner, into=(0, d)), _gather_job(shares1, g_shares1, forwards=[_whole(shares1)]),
        _gather_job(shares2, None, sends=[_whole(shares2)])])
    b_qkv_b = blocked_half(dw_b)
    (dx0, dg_mix_pre0), ((q_qkv,), (g_shares2,)) = _pre_bwd(dqkv, w_qkv, xs, row(norm_mix_pre[0]), dx1, "attn_qkv_bwd", jobs=[
        _scatter_job([b_qkv_b], N_DEV - 1, _to_owner, prev=[q_qkv], into=(d // 2, d)),
        _gather_job(shares2, g_shares2, forwards=[_whole(shares2)])])
    b_qkv_grad = jnp.concatenate([b_qkv_a, b_qkv_b], axis=1)
    update("attn_w_qkv", b_qkv_grad, me, q_qkv)

    finish(BEFORE_ATTN, 0, g_shares1, where1, "replicated_update_before_attn")
    loss = finish(AFTER_ATTN, 1, g_shares2, where2, "replicated_update_after_attn")[0][0, 0]
    grads["norm_mix_pre"] = jnp.concatenate([dg_mix_pre0, dg_mix_pre1], axis=0)
    shares3, where3 = _pack_rows([grads[name] for name in LAST], d)
    _, (last_shares,) = _prepare([([shares3[None]], 0)], [F32], 1, "last_grads_all_gather")
    finish(LAST, 0, last_shares, where3, "replicated_update_last")

    for out in (out_g, out_d, out_m, out_v):
        out["sgu_ln_g"], out["sgu_ln_b"] = out["sgu_ln"][:, 0, :], out["sgu_ln"][:, 1, :]
        out["ffn_w_gate_up"] = jnp.swapaxes(out["ffn_w_gate_up"], 1, 2)

    return (loss, dx0[None], *[out_g[n] for n in WEIGHTS], *[out_d[n] for n in WEIGHTS],
            *[out_m[n] for n in WEIGHTS], *[out_v[n] for n in WEIGHTS])
```

```python
import functools
import math
import operator

import jax
import jax.numpy as jnp
from jax import lax
from jax.experimental import pallas as pl
from jax.experimental.pallas import tpu as pltpu

F32 = jnp.float32
ACT = jnp.bfloat16

EPS = 1e-6
HEAD_DIM = 64
PAIR = 2 * HEAD_DIM
GQA_GROUP = 4
WINDOW = 128
ROPE_THETA = 10000.0
SCALE = HEAD_DIM ** -0.5
MASKED = -1e30
LANES = 128

ADAM_LR, ADAM_B1, ADAM_B2, ADAM_EPS, ADAM_WD, ADAM_STEP = 0.001, 0.9, 0.999, 1e-08, 0.01, 10

N_DEV = 8
VMEM_LIMIT_BYTES = 56 * 1024 * 1024
MESH = pl.DeviceIdType.MESH
ANY = pl.BlockSpec(memory_space=pl.ANY)
IN_VMEM = pl.BlockSpec(memory_space=pltpu.VMEM)


def _pick(n, candidates):
    for c in candidates:
        if n % c == 0:
            return c
    return n


def _row_tile(t):
    return _pick(t, (512,))


def _params(*sem):
    return pltpu.CompilerParams(dimension_semantics=sem, vmem_limit_bytes=VMEM_LIMIT_BYTES)


def _full(shape):
    return pl.BlockSpec(shape, lambda *_: (0,) * len(shape))


def _rows(tm, n):
    return pl.BlockSpec((tm, n), lambda i: (i, 0))


def _sds(shape, dtype):
    return jax.ShapeDtypeStruct(shape, dtype)


def _place():
    return lax.axis_index("x"), lax.axis_index("y"), lax.axis_index("c")


def _other_chips(x, y):
    return [(1 - x, y), (x, 1 - y), (1 - x, 1 - y)]


class _Job:
    def __init__(self, inputs, out_shape, aliases, emit, n_remote, n_local=0):
        self.inputs, self.out_shape, self.aliases, self.emit = list(inputs), list(out_shape), dict(aliases), emit
        self.n_remote, self.n_local = n_remote, n_local


def _call(body, name, grid, in_specs, out_specs, out_shape, args, sem, scratch=(), jobs=()):
    n_in, n_out, n_scr = len(args), len(out_shape), len(scratch)
    j_in = [a for job in jobs for a in job.inputs]
    j_out = [s for job in jobs for s in job.out_shape]
    aliases, at_in, at_out = {}, n_in, n_out
    for job in jobs:
        aliases.update({at_in + i: at_out + o for i, o in job.aliases.items()})
        at_in, at_out = at_in + len(job.inputs), at_out + len(job.out_shape)
    n_remote = sum(job.n_remote for job in jobs)
    n_local = sum(job.n_local for job in jobs)

    def wrapped(*refs):
        ins, jin = refs[:n_in], refs[n_in:n_in + len(j_in)]
        rest = refs[n_in + len(j_in):]
        outs, jout = rest[:n_out], rest[n_out:n_out + len(j_out)]
        scr = rest[n_out + len(j_out):n_out + len(j_out) + n_scr]
        if not jobs:
            body(*ins, *outs, *scr)
            return
        send, recv, local = rest[n_out + len(j_out) + n_scr:]
        ids = [pl.program_id(a) for a in range(len(grid))]
        first = functools.reduce(operator.and_, [i == 0 for i in ids])
        last = functools.reduce(operator.and_, [i == g - 1 for i, g in zip(ids, grid)])

        def descriptors():
            place, found, i, o, r, l = _place(), [], 0, 0, 0, 0
            for job in jobs:
                for src, dst, to in job.emit(jin[i:i + len(job.inputs)], jout[o:o + len(job.out_shape)], place):
                    if to is None:
                        found.append(pltpu.make_async_copy(src, dst, local.at[l]))
                        l += 1
                    else:
                        found.append(pltpu.make_async_remote_copy(src_ref=src, dst_ref=dst, send_sem=send.at[r], recv_sem=recv.at[r],
                                                                  device_id=to, device_id_type=MESH))
                        r += 1
                i, o = i + len(job.inputs), o + len(job.out_shape)
            return found

        @pl.when(first)
        def _():
            for cp in descriptors():
                cp.start()

        body(*ins, *outs, *scr)

        @pl.when(last)
        def _():
            for cp in descriptors():
                cp.wait()

    sems = [pltpu.SemaphoreType.DMA((n_remote,)), pltpu.SemaphoreType.DMA((n_remote,)),
            pltpu.SemaphoreType.DMA((max(n_local, 1),))] if jobs else []
    res = pl.pallas_call(
        wrapped, name=name, grid=grid,
        in_specs=list(in_specs) + [ANY] * len(j_in), out_specs=list(out_specs) + [ANY] * len(j_out),
        out_shape=list(out_shape) + j_out, scratch_shapes=list(scratch) + sems, input_output_aliases=aliases,
        compiler_params=_params(*(("arbitrary",) * len(grid) if jobs else sem)),
    )(*args, *j_in)
    job_res, at = [], n_out
    for job in jobs:
        job_res.append(list(res[at:at + len(job.out_shape)]))
        at += len(job.out_shape)
    return list(res[:n_out]), job_res


def _gather_job(stage, gathered, sends=(), forwards=()):
    shape = _sds((N_DEV,) + stage.shape, stage.dtype)
    inputs = ([stage] if sends else []) + ([gathered] if gathered is not None else [])
    aliases = {len(inputs) - 1: 0} if gathered is not None else {}

    def emit(ins, outs, place):
        x, y, c = place
        sibling, chips, mine, g = (x, y, 1 - c), _other_chips(x, y), 4 * x + 2 * y + c, outs[0]
        found = []
        for r0, nr in sends:
            src, dst = ins[0].at[pl.ds(r0, nr)], g.at[mine, pl.ds(r0, nr)]
            found += [(src, dst, None), (src, dst, sibling)] + [(src, dst, (px, py, c)) for px, py in chips]
        for r0, nr in forwards:
            for px, py in chips:
                block = 4 * px + 2 * py + c
                found.append((ins[-1].at[block, pl.ds(r0, nr)], g.at[block, pl.ds(r0, nr)], sibling))
        return found

    return _Job(inputs, [shape], aliases, emit, 4 * len(sends) + 3 * len(forwards), len(sends))


def _scatter_job(arrays, n_slots, route, prev=None, piece=None, into=None):
    shapes = [_sds((n_slots,) + ((into[1],) + v.shape[2:] if into else v.shape[1:]), v.dtype) for v in arrays]
    inputs = list(arrays) + (list(prev) if prev else [])
    aliases = {len(arrays) + i: i for i in range(len(arrays))} if prev else {}

    def emit(ins, outs, place):
        found = []
        for a in range(len(arrays)):
            for s in range(n_slots):
                block, to = route(s, *place)
                if into is not None:
                    found.append((ins[a].at[block], outs[a].at[s, pl.ds(into[0], ins[a].shape[1])], to))
                elif piece is None:
                    found.append((ins[a].at[block], outs[a].at[s], to))
                else:
                    found.append((ins[a].at[block, pl.ds(*piece)], outs[a].at[s, pl.ds(*piece)], to))
        return found

    return _Job(inputs, shapes, aliases, emit, len(arrays) * n_slots)


def _to_sibling(s, x, y, c):
    return 2 * s + (1 - c), (x, y, 1 - c)


def _to_owner_chip(s, x, y, c):
    px, py = _other_chips(x, y)[s]
    return 2 * px + py, (px, py, c)


def _to_owner(s, x, y, c):
    if s == 0:
        px, py, pc = x, y, 1 - c
    else:
        px, py = _other_chips(x, y)[(s - 1) % 3]
        pc = c if s <= 3 else 1 - c
    return 4 * px + 2 * py + pc, (px, py, pc)


def _rstd(x):
    return lax.rsqrt(jnp.mean(x * x, axis=-1, keepdims=True) + EPS)


def _rms_bwd(xhat, r, g, dy):
    dxh = dy * g
    return r * (dxh - xhat * jnp.mean(dxh * xhat, axis=-1, keepdims=True))


def _first_half(rows):
    lane = lax.broadcasted_iota(jnp.int32, (rows, PAIR), 1)
    return (lane % HEAD_DIM) < (HEAD_DIM // 2)


def _rot_half(v, first):
    return jnp.where(first, pltpu.roll(v, PAIR - HEAD_DIM // 2, 1), pltpu.roll(v, HEAD_DIM // 2, 1))


def _rope(v, cos, sin, first):
    return v * cos + _rot_half(v, first) * sin


def _rope_t(dv, cos, sin, first):
    return dv * cos + _rot_half(dv * sin, first)


def _dot(a, b):
    return jnp.dot(a, b, preferred_element_type=F32)


def _dot_nt(a, b):
    return lax.dot_general(a, b, (((1,), (1,)), ((), ())), preferred_element_type=F32)


def _dot_tn(a, b):
    return lax.dot_general(a, b, (((0,), (0,)), ((), ())), preferred_element_type=F32)


def _sigmoid(v):
    return 1.0 / (1.0 + jnp.exp(-v))


_GELU_K = math.sqrt(2.0 / math.pi)
_GELU_C = 0.044715


def _gelu(v):
    return v * (0.5 * (1.0 + jnp.tanh(_GELU_K * (v + _GELU_C * (v * v * v)))))


def _gelu_grad(v):
    t = jnp.tanh(_GELU_K * (v + _GELU_C * (v * v * v)))
    return 0.5 * (1.0 + t) + 0.5 * v * (1.0 - t * t) * (_GELU_K * (1.0 + 3.0 * _GELU_C * (v * v)))


def _attn_qkv_fwd(x, g, w, b, cos, sin, jobs=()):
    t, d = x.shape
    n = w.shape[1]
    kd = (n - d) // 2
    tm = _row_tile(t)
    ch = _pick(d, (512,))

    def body(x_ref, g_ref, w_ref, b_ref, cos_ref, sin_ref, h_ref, q_ref, k_ref, v_ref):
        xv = x_ref[...]
        hb = (xv * _rstd(xv) * g_ref[...]).astype(ACT)
        h_ref[...] = hb
        cosv, sinv = cos_ref[...], sin_ref[...]
        first = _first_half(tm)

        def proj(lo, width):
            return _dot(hb, w_ref[:, lo:lo + width]) + b_ref[:, lo:lo + width]

        for c in range(0, d, ch):
            y = proj(c, ch)
            for s in range(0, ch, PAIR):
                q_ref[:, c + s:c + s + PAIR] = (_rope(y[:, s:s + PAIR], cosv, sinv, first) * SCALE).astype(ACT)
        y = proj(d, kd)
        for s in range(0, kd, PAIR):
            k_ref[:, s:s + PAIR] = _rope(y[:, s:s + PAIR], cosv, sinv, first).astype(ACT)
        v_ref[...] = proj(d + kd, kd).astype(ACT)

    return _call(
        body, "attn_qkv_fwd", (t // tm,),
        [_rows(tm, d), _full((1, d)), _full((d, n)), _full((1, n)), _rows(tm, PAIR), _rows(tm, PAIR)],
        [_rows(tm, d), _rows(tm, d), _rows(tm, kd), _rows(tm, kd)],
        [_sds((t, d), ACT), _sds((t, d), ACT), _sds((t, kd), ACT), _sds((t, kd), ACT)],
        (x, g, w, b, cos, sin), ("parallel",), jobs=jobs)


STACK = GQA_GROUP * WINDOW


def _band_mask(has_prev):
    row = lax.broadcasted_iota(jnp.int32, (STACK, 2 * WINDOW), 0) % WINDOW
    col = lax.broadcasted_iota(jnp.int32, (STACK, 2 * WINDOW), 1)
    return ((col < WINDOW) & (col > row) & has_prev) | ((col >= WINDOW) & (col - WINDOW <= row))


def _lane_halves():
    lane = lax.broadcasted_iota(jnp.int32, (1, PAIR), 1)
    return lane < HEAD_DIM, lane >= HEAD_DIM


def _stack_heads(ref, h, halves):
    parts = []
    for p in range(2):
        pair = ref[:, (2 * h + p) * PAIR:(2 * h + p + 1) * PAIR]
        parts += [jnp.where(half, pair, jnp.zeros_like(pair)) for half in halves]
    return jnp.concatenate(parts, axis=0)


def _sink_column(sink_ref, h):
    row = lax.broadcasted_iota(jnp.int32, (STACK, 1), 0)
    col = jnp.full((STACK, 1), sink_ref[0, GQA_GROUP * h + GQA_GROUP - 1], F32)
    for j in range(GQA_GROUP - 2, -1, -1):
        col = jnp.where(row < (j + 1) * WINDOW, sink_ref[0, GQA_GROUP * h + j], col)
    return col


def _probs(scores, valid, sink):
    s = jnp.where(valid, scores, MASKED)
    m = jnp.maximum(jnp.max(s, axis=-1, keepdims=True), sink)
    p = jnp.exp(s - m)
    psink = jnp.exp(sink - m)
    inv = 1.0 / (jnp.sum(p, axis=-1, keepdims=True) + psink)
    return p * inv, psink * inv


def _attn_fwd(q, kd_, vd, sinks, jobs=()):
    t, d = q.shape
    kd = kd_.shape[1]
    cur = lambda n: (n, 0)
    prev = lambda n: (jnp.maximum(n - 1, 0), 0)

    def body(sink_ref, q_ref, kc_ref, kp_ref, vc_ref, vp_ref, o_ref):
        valid = _band_mask(pl.program_id(0) > 0)
        halves = _lane_halves()
        heads = range(kd // PAIR)
        hs = [slice(h * PAIR, (h + 1) * PAIR) for h in heads]
        scores = [_dot_nt(_stack_heads(q_ref, h, halves), jnp.concatenate([kp_ref[:, hs[h]], kc_ref[:, hs[h]]], axis=0)) for h in heads]
        probs = [_probs(scores[h], valid, _sink_column(sink_ref, h))[0].astype(ACT) for h in heads]
        for h in heads:
            v2 = jnp.concatenate([vp_ref[:, hs[h]], vc_ref[:, hs[h]]], axis=0)
            vs = jnp.concatenate([jnp.where(half, v2, jnp.zeros_like(v2)) for half in halves], axis=0)
            pr = probs[h]
            for p in range(2):
                both = jnp.concatenate([pr[2 * p * WINDOW:(2 * p + 1) * WINDOW], pr[(2 * p + 1) * WINDOW:(2 * p + 2) * WINDOW]], axis=1)
                o_ref[:, (2 * h + p) * PAIR:(2 * h + p + 1) * PAIR] = _dot(both, vs).astype(ACT)

    kv_c, kv_p = pl.BlockSpec((WINDOW, kd), cur), pl.BlockSpec((WINDOW, kd), prev)
    return _call(
        body, "attn_fwd", (t // WINDOW,),
        [pl.BlockSpec(memory_space=pltpu.SMEM), pl.BlockSpec((WINDOW, d), cur), kv_c, kv_p, kv_c, kv_p],
        [pl.BlockSpec((WINDOW, d), cur)], [_sds((t, d), ACT)],
        (sinks, q, kd_, kd_, vd, vd), ("parallel",), jobs=jobs)


def _attn_bwd(q, kd_, vd, do, sinks, cos, sin, jobs=()):
    t, d = q.shape
    kd = kd_.shape[1]
    nb = t // WINDOW
    n_out = d + 2 * kd
    cur = lambda n: (jnp.minimum(n, nb - 1), 0)
    prev = lambda n: (jnp.maximum(jnp.minimum(n, nb - 1) - 1, 0), 0)
    late = lambda n: (jnp.maximum(n - 1, 0), 0)

    def body(sink_ref, q_ref, kc_ref, kp_ref, vc_ref, vp_ref, do_ref, cosc_ref, sinc_ref, cosp_ref, sinp_ref,
             out_ref, db_ref, dsink_ref, dq_keep, dk_keep, dv_keep):
        n = pl.program_id(0)

        @pl.when(n == 0)
        def _():
            for ref in (db_ref, dsink_ref, dq_keep, dk_keep, dv_keep):
                ref[...] = jnp.zeros_like(ref)

        valid = _band_mask(n > 0) & (n < nb)
        halves = _lane_halves()
        first = _first_half(WINDOW)
        slot = lax.broadcasted_iota(jnp.int32, dsink_ref.shape, 1)
        top = lax.broadcasted_iota(jnp.int32, dsink_ref.shape, 0) == 0
        dsink = jnp.zeros(dsink_ref.shape, F32)

        def emit(cols, done):
            out_ref[:, cols] = done.astype(ACT)
            db_ref[:, cols] += jnp.sum(done.astype(F32), axis=0, keepdims=True)

        heads = range(kd // PAIR)
        hs = [slice(h * PAIR, (h + 1) * PAIR) for h in heads]
        k2 = [jnp.concatenate([kp_ref[:, hs[h]], kc_ref[:, hs[h]]], axis=0) for h in heads]
        v2 = [jnp.concatenate([vp_ref[:, hs[h]], vc_ref[:, hs[h]]], axis=0) for h in heads]
        qs = [_stack_heads(q_ref, h, halves) for h in heads]
        dos = [_stack_heads(do_ref, h, halves) for h in heads]
        scores = [_dot_nt(qs[h], k2[h]) for h in heads]
        dps = [_dot_nt(dos[h], v2[h]) for h in heads]
        prs, dss = [], []
        for h in heads:
            pr, psink = _probs(scores[h], valid, _sink_column(sink_ref, h))
            delta = jnp.sum(pr * dps[h], axis=-1, keepdims=True)
            dss.append((pr * (dps[h] - delta)).astype(ACT))
            prs.append(pr.astype(ACT))
            leak = psink * delta
            for j in range(GQA_GROUP):
                share = jnp.sum(leak[j * WINDOW:(j + 1) * WINDOW], axis=0, keepdims=True)
                dsink = dsink - jnp.where(top & (slot == GQA_GROUP * h + j), share, 0.0)
        dqs = [_dot(dss[h], k2[h]) for h in heads]
        dk2 = [_dot_tn(dss[h], qs[h]) for h in heads]
        dv2 = [_dot_tn(prs[h], dos[h]) for h in heads]
        for h in heads:
            for p in range(2):
                ps = slice((2 * h + p) * PAIR, (2 * h + p + 1) * PAIR)
                dqp = jnp.where(halves[0], dqs[h][2 * p * WINDOW:(2 * p + 1) * WINDOW], dqs[h][(2 * p + 1) * WINDOW:(2 * p + 2) * WINDOW])
                emit(ps, dq_keep[:, ps])
                dq_keep[:, ps] = (_rope_t(dqp, cosc_ref[...], sinc_ref[...], first) * SCALE).astype(ACT)
            ks = slice(d + h * PAIR, d + (h + 1) * PAIR)
            vs = slice(d + kd + h * PAIR, d + kd + (h + 1) * PAIR)
            emit(ks, dk_keep[:, hs[h]] + _rope_t(dk2[h][:WINDOW], cosp_ref[...], sinp_ref[...], first))
            dk_keep[:, hs[h]] = _rope_t(dk2[h][WINDOW:], cosc_ref[...], sinc_ref[...], first)
            emit(vs, dv_keep[:, hs[h]] + dv2[h][:WINDOW])
            dv_keep[:, hs[h]] = dv2[h][WINDOW:]
        dsink_ref[...] += dsink

    kv_c, kv_p = pl.BlockSpec((WINDOW, kd), cur), pl.BlockSpec((WINDOW, kd), prev)
    tab_c, tab_p = pl.BlockSpec((WINDOW, PAIR), cur), pl.BlockSpec((WINDOW, PAIR), prev)
    return _call(
        body, "attn_bwd", (nb + 1,),
        [pl.BlockSpec(memory_space=pltpu.SMEM), pl.BlockSpec((WINDOW, d), cur), kv_c, kv_p, kv_c, kv_p,
         pl.BlockSpec((WINDOW, d), cur), tab_c, tab_c, tab_p, tab_p],
        [pl.BlockSpec((WINDOW, n_out), late), _full((1, n_out)), _full((8, LANES))],
        [_sds((t, n_out), ACT), _sds((1, n_out), F32), _sds((8, LANES), F32)],
        (sinks, q, kd_, kd_, vd, vd, do, cos, sin, cos, sin), ("arbitrary",),
        scratch=[pltpu.VMEM((WINDOW, d), ACT), pltpu.VMEM((WINDOW, kd), F32), pltpu.VMEM((WINDOW, kd), F32)], jobs=jobs)


def _proj_res(a, w, b, g, x, name, target=None, jobs=()):
    blocked = a.ndim == 3
    t = a.shape[-2]
    k, d = w.shape
    tm = _row_tile(t)
    has_bias = b is not None

    def body(*refs):
        a_ref, w_ref = refs[:2]
        b_ref = refs[2] if has_bias else None
        g_ref, x_ref, m_ref, xo_ref = refs[2 + has_bias:]
        if blocked:
            c = a.shape[2]
            m = _dot(a_ref[0], w_ref[:c, :])
            for j in range(1, a.shape[0]):
                m = m + _dot(a_ref[j], w_ref[j * c:(j + 1) * c, :])
        else:
            m = _dot(a_ref[...], w_ref[...])
        if has_bias:
            m = m + b_ref[...]
        m_ref[...] = m.astype(ACT)
        xo_ref[...] = x_ref[...] + (m * _rstd(m)) * g_ref[...]

    def body_with_loss(a_ref, w_ref, g_ref, x_ref, t_ref, m_ref, dy_ref, loss_ref):
        @pl.when(pl.program_id(0) == 0)
        def _():
            loss_ref[...] = jnp.zeros_like(loss_ref)

        body(a_ref, w_ref, g_ref, x_ref, m_ref, dy_ref)
        err = dy_ref[...] - t_ref[...]
        dy_ref[...] = err * (1.0 / d)
        loss_ref[...] += 0.5 * jnp.sum(jnp.mean(err * err, axis=-1, keepdims=True), axis=0, keepdims=True)

    ins = [a, w] + ([b] if has_bias else []) + [g, x]
    a_spec = pl.BlockSpec((a.shape[0], tm, a.shape[2]), lambda i: (0, i, 0)) if blocked else _rows(tm, k)
    specs = [a_spec, _full((k, d))] + ([_full((1, d))] if has_bias else []) + [_full((1, d)), _rows(tm, d)]
    if target is not None:
        return _call(body_with_loss, name, (t // tm,), specs + [_rows(tm, d)], [_rows(tm, d), _rows(tm, d), _full((8, LANES))],
                     [_sds((t, d), ACT), _sds((t, d), F32), _sds((8, LANES), F32)], ins + [target], ("arbitrary",), jobs=jobs)
    return _call(body, name, (t // tm,), specs, [_rows(tm, d), _rows(tm, d)], [_sds((t, d), ACT), _sds((t, d), F32)], ins,
                 ("parallel",), jobs=jobs)


def _post_bwd(dres, m, g, w, mode, name, gu=None, jobs=()):
    t, d = dres.shape
    k = w.shape[0]
    tm = _row_tile(t)
    kc = _pick(k, (1408, 1024, 512))

    def body(*refs):
        d_ref, m_ref, g_ref, w_ref = refs[:4]
        gu_ref = refs[4] if mode == "ffn" else None
        outs = refs[4 + (mode == "ffn"):]
        dm_ref, da_ref, dg_ref = outs[:3]
        i = pl.program_id(0)

        @pl.when(i == 0)
        def _():
            dg_ref[...] = jnp.zeros_like(dg_ref)
            if mode == "attn":
                outs[3][...] = jnp.zeros_like(outs[3])

        dv, mv = d_ref[...], m_ref[...].astype(F32)
        r = _rstd(mv)
        mh = mv * r
        dg_ref[...] += jnp.sum(dv * mh, axis=0, keepdims=True)
        dm = _rms_bwd(mh, r, g_ref[...], dv)
        dmb = dm.astype(ACT)
        dm_ref[...] = dmb
        if mode == "attn":
            outs[3][...] += jnp.sum(dm, axis=0, keepdims=True)
        if mode == "ffn":
            nb, _, cb = gu.shape
            for j in range(nb // 2):
                da = _dot_nt(dmb, w_ref[j * cb:(j + 1) * cb, :]).astype(ACT)
                gate, up = gu_ref[j], gu_ref[nb // 2 + j]
                sg = _sigmoid(gate.astype(F32)).astype(ACT)
                gs = gate * sg
                da_ref[j] = da * up * (sg + gs - gs * sg)
                da_ref[nb // 2 + j] = da * gs
        else:
            for c in range(0, k, kc):
                da = _dot_nt(dmb, w_ref[c:c + kc, :])
                da_ref[:, c:c + kc] = da.astype(ACT)

    ins = [dres, m, g, w]
    specs = [_rows(tm, d), _rows(tm, d), _full((1, d)), _full((k, d))]
    if mode == "ffn":
        slab = pl.BlockSpec((gu.shape[0], tm, gu.shape[2]), lambda i: (0, i, 0))
        ins.append(gu)
        specs.append(slab)
        out_specs = [_rows(tm, d), slab, _full((1, d))]
        out_shape = [_sds((t, d), ACT), _sds(gu.shape, ACT), _sds((1, d), F32)]
    else:
        out_specs = [_rows(tm, d), _rows(tm, k), _full((1, d))]
        out_shape = [_sds((t, d), ACT), _sds((t, k), ACT), _sds((1, d), F32)]
    if mode == "attn":
        out_specs.append(_full((1, d)))
        out_shape.append(_sds((1, d), F32))
    return _call(body, name, (t // tm,), specs, out_specs, out_shape, ins, ("arbitrary",), jobs=jobs)


def _pre_bwd(dy, w, x, g, dres, name, transposed=False, jobs=()):
    t, d = x.shape
    tm = _row_tile(t)

    def body(dy_ref, w_ref, x_ref, g_ref, dres_ref, dx_ref, dg_ref):
        @pl.when(pl.program_id(0) == 0)
        def _():
            dg_ref[...] = jnp.zeros_like(dg_ref)

        dh = jnp.zeros((tm, d), F32)
        if w.ndim == 3 and transposed:
            for j in range(w.shape[0]):
                dh = dh + _dot(dy_ref[j], w_ref[j])
        elif w.ndim == 3:
            c = w.shape[2]
            for j in range(w.shape[0]):
                dh = dh + _dot_nt(dy_ref[j] if dy.ndim == 3 else dy_ref[:, j * c:(j + 1) * c], w_ref[j])
        else:
            n = w.shape[1]
            nc = _pick(n, (1408, 1024, 512))
            for c in range(0, n, nc):
                dh = dh + _dot_nt(dy_ref[:, c:c + nc], w_ref[:, c:c + nc])
        xv = x_ref[...]
        r = _rstd(xv)
        xh = xv * r
        dg_ref[...] += jnp.sum(dh * xh, axis=0, keepdims=True)
        dx_ref[...] = dres_ref[...] + _rms_bwd(xh, r, g_ref[...], dh)

    dy_spec = pl.BlockSpec((dy.shape[0], tm, dy.shape[2]), lambda i: (0, i, 0)) if dy.ndim == 3 else _rows(tm, dy.shape[1])
    return _call(
        body, name, (t // tm,),
        [dy_spec, _full(w.shape), _rows(tm, d), _full((1, d)), _rows(tm, d)],
        [_rows(tm, d), _full((1, d))], [_sds((t, d), F32), _sds((1, d), F32)],
        (dy, w, x, g, dres), ("arbitrary",), jobs=jobs)


def _wgrad(a, b, name, out_dtype=F32, out_block=None, transposed=False, k_part=None, jobs=()):
    t = a.shape[-2]
    tt = _pick(t, (1024,))
    steps = t // tt
    if a.ndim == 3:
        k_steps, _, tk = a.shape
        a_spec = pl.BlockSpec((None, tt, tk), lambda i, j, s: (i, s, 0))
    elif k_part is not None:
        part, parts = k_part
        tk, k_steps = a.shape[1] // parts, 1
        a_spec = pl.BlockSpec((tt, tk), lambda i, j, s: (s, part))
    else:
        tk = _pick(a.shape[1], (1024, 1408))
        k_steps = a.shape[1] // tk
        a_spec = pl.BlockSpec((tt, tk), lambda i, j, s: (s, i))
    k = k_steps * tk
    group = 1
    if b.ndim == 3:
        n_blocks, _, tn = b.shape
        group = 2 if n_blocks % 2 == 0 else 1
        n_steps, per = n_blocks // group, 1
        b_spec = pl.BlockSpec((group, tt, tn), lambda i, j, s: (j, s, 0))
        if transposed:
            out_spec, out_shape = pl.BlockSpec((group, tn, tk), lambda i, j, s: (j, 0, i)), _sds((n_blocks, tn, k), out_dtype)
        else:
            out_spec, out_shape = pl.BlockSpec((group, tk, tn), lambda i, j, s: (j, i, 0)), _sds((n_blocks, k, tn), out_dtype)
    else:
        n = b.shape[1]
        tn = _pick(n, (1024, 1408))
        n_steps = n // tn
        b_spec = pl.BlockSpec((tt, tn), lambda i, j, s: (s, j))
        if out_block is None:
            per = 0
            out_spec, out_shape = pl.BlockSpec((tk, tn), lambda i, j, s: (i, j)), _sds((k, n), out_dtype)
        else:
            per = tn // out_block
            out_spec = pl.BlockSpec((per, tk, out_block), lambda i, j, s: (j, i, 0))
            out_shape = _sds((n // out_block, k, out_block), out_dtype)

    def body(a_ref, b_ref, o_ref, acc_ref):
        s = pl.program_id(2)

        @pl.when(s == 0)
        def _():
            acc_ref[...] = jnp.zeros_like(acc_ref)

        if b.ndim == 3:
            av = a_ref[...]
            for p in range(group):
                acc_ref[p] += _dot_tn(b_ref[p], av) if transposed else _dot_tn(av, b_ref[p])
        else:
            acc_ref[...] += _dot_tn(a_ref[...], b_ref[...])

        @pl.when(s == steps - 1)
        def _():
            if b.ndim == 2 and per >= 1:
                for p in range(per):
                    o_ref[p] = acc_ref[:, p * out_block:(p + 1) * out_block].astype(out_dtype)
            else:
                o_ref[...] = acc_ref[...].astype(out_dtype)

    acc_shape = ((group, tn, tk) if transposed else (group, tk, tn)) if b.ndim == 3 else (tk, tn)
    res, job_res = _call(
        body, name, (k_steps, n_steps, steps), [a_spec, b_spec], [out_spec], [out_shape],
        (a, b), ("parallel", "parallel", "arbitrary"), scratch=[pltpu.VMEM(acc_shape, F32)], jobs=jobs)
    return res[0], job_res


def _ffn_up_fwd(x, g, w, jobs=()):
    t, d = x.shape
    nb, c, _ = w.shape
    tm = _row_tile(t)

    def body(x_ref, g_ref, w_ref, h_ref, gu_ref, a_ref):
        xv = x_ref[...]
        hb = (xv * _rstd(xv) * g_ref[...]).astype(ACT)
        h_ref[...] = hb
        for j in range(nb // 2):
            gate = _dot_nt(hb, w_ref[j])
            up = _dot_nt(hb, w_ref[nb // 2 + j])
            gu_ref[j] = gate.astype(ACT)
            gu_ref[nb // 2 + j] = up.astype(ACT)
            a_ref[j] = (gate * _sigmoid(gate) * up).astype(ACT)

    slab = lambda n: pl.BlockSpec((n, tm, c), lambda i: (0, i, 0))
    return _call(
        body, "ffn_up_fwd", (t // tm,),
        [_rows(tm, d), _full((1, d)), _full(w.shape)],
        [_rows(tm, d), slab(nb), slab(nb // 2)],
        [_sds((t, d), ACT), _sds((nb, t, c), ACT), _sds((nb // 2, t, c), ACT)],
        (x, g, w), ("parallel",), jobs=jobs)


def _causal(ws):
    row = lax.broadcasted_iota(jnp.int32, ws.shape, 0)
    col = lax.broadcasted_iota(jnp.int32, ws.shape, 1)
    return jnp.where(col <= row, ws, 0.0)


def _sgu_ln(v, lg, lb):
    mu = jnp.mean(v, axis=-1, keepdims=True)
    vc = v - mu
    rs = lax.rsqrt(jnp.mean(vc * vc, axis=-1, keepdims=True) + EPS)
    vh = vc * rs
    return vh, rs, vh * lg + lb


def _sgu_fwd(x, g, w, lg, lb, ws, bs_t, jobs=()):
    t, d = x.shape
    nb, _, c = w.shape
    n = nb * c
    groups = d // WINDOW
    tm = _row_tile(t)

    def body(x_ref, g_ref, w_ref, lg_ref, lb_ref, ws_ref, bs_ref, h_ref, z_ref, y_ref, zf_ref):
        xv = x_ref[...]
        hb = (xv * _rstd(xv) * g_ref[...]).astype(ACT)
        h_ref[...] = hb
        for j in range(nb):
            zf_ref[:, j * c:(j + 1) * c] = _dot(hb, w_ref[j])
        z_ref[...] = zf_ref[...].astype(ACT)
        gs = [slice(gi * WINDOW, (gi + 1) * WINDOW) for gi in range(groups)]
        wsg = [_causal(ws_ref[gi]).astype(ACT) for gi in range(groups)]
        for r in range(0, tm, WINDOW):
            u = _gelu(zf_ref[r:r + WINDOW, :d])
            _, _, vn = _sgu_ln(_gelu(zf_ref[r:r + WINDOW, d:]), lg_ref[...], lb_ref[...])
            mixed = [_dot(wsg[gi], vn[:, gs[gi]].astype(ACT)) for gi in range(groups)]
            for gi in range(groups):
                y_ref[r:r + WINDOW, gs[gi]] = (u[:, gs[gi]] * (mixed[gi] + bs_ref[:, gi:gi + 1])).astype(ACT)

    vec = _full((1, d))
    return _call(
        body, "sgu_fwd", (t // tm,),
        [_rows(tm, d), vec, _full((nb, d, c)), vec, vec, _full((groups, WINDOW, WINDOW)), _full((WINDOW, groups))],
        [_rows(tm, d), _rows(tm, n), _rows(tm, d)], [_sds((t, d), ACT), _sds((t, n), ACT), _sds((t, d), ACT)],
        (x, g, w, lg, lb, ws, bs_t), ("parallel",), scratch=[pltpu.VMEM((tm, n), F32)], jobs=jobs)


def _sgu_mix_bwd(z, dy, lg, lb, ws, bs_t, jobs=()):
    t, n = z.shape
    d = n // 2
    groups = d // WINDOW
    tm = _row_tile(t)

    def body(z_ref, dy_ref, lg_ref, lb_ref, ws_ref, bs_ref, dz_ref, dlg_ref, dlb_ref, dws_ref, dbs_ref):
        @pl.when(pl.program_id(0) == 0)
        def _():
            for ref in (dlg_ref, dlb_ref, dws_ref, dbs_ref):
                ref[...] = jnp.zeros_like(ref)

        lane = lax.broadcasted_iota(jnp.int32, (WINDOW, groups), 1)
        gs = [slice(gi * WINDOW, (gi + 1) * WINDOW) for gi in range(groups)]
        wsg = [_causal(ws_ref[gi]).astype(ACT) for gi in range(groups)]
        for c in range(0, tm, WINDOW):
            rows = slice(c, c + WINDOW)
            zu, zv = z_ref[rows, :d].astype(F32), z_ref[rows, d:].astype(F32)
            u = _gelu(zu)
            vh, rs, vn = _sgu_ln(_gelu(zv), lg_ref[...], lb_ref[...])
            dyv = dy_ref[rows, :].astype(F32)
            vng = [vn[:, gs[gi]].astype(ACT) for gi in range(groups)]
            dmix = dyv * u
            dmb = [dmix[:, gs[gi]].astype(ACT) for gi in range(groups)]
            mixed = [_dot(wsg[gi], vng[gi]) for gi in range(groups)]
            dvn_parts = [_dot_tn(wsg[gi], dmb[gi]) for gi in range(groups)]
            dws_parts = [_dot_nt(dmb[gi], vng[gi]) for gi in range(groups)]
            for gi in range(groups):
                dz_ref[rows, gs[gi]] = (dyv[:, gs[gi]] * (mixed[gi] + bs_ref[:, gi:gi + 1]) * _gelu_grad(zu[:, gs[gi]])).astype(ACT)
                dws_ref[:, gs[gi]] += _causal(dws_parts[gi])
                dbs_ref[...] += jnp.where(lane == gi, jnp.sum(dmix[:, gs[gi]], axis=-1, keepdims=True), 0.0)
            dvn = jnp.concatenate(dvn_parts, axis=-1)
            dlg_ref[...] += jnp.sum(dvn * vh, axis=0, keepdims=True)
            dlb_ref[...] += jnp.sum(dvn, axis=0, keepdims=True)
            dvh = dvn * lg_ref[...]
            dv = rs * (dvh - jnp.mean(dvh, axis=-1, keepdims=True) - vh * jnp.mean(dvh * vh, axis=-1, keepdims=True))
            dz_ref[rows, d:] = (dv * _gelu_grad(zv)).astype(ACT)

    vec = _full((1, d))
    return _call(
        body, "sgu_mix_bwd", (t // tm,),
        [_rows(tm, n), _rows(tm, d), vec, vec, _full((groups, WINDOW, WINDOW)), _full((WINDOW, groups))],
        [_rows(tm, n), vec, vec, _full((WINDOW, d)), _full((WINDOW, groups))],
        [_sds((t, n), ACT), _sds((1, d), F32), _sds((1, d), F32), _sds((WINDOW, d), F32), _sds((WINDOW, groups), F32)],
        (z, dy, lg, lb, ws, bs_t), ("arbitrary",), jobs=jobs)


AG_COPIES = 7


def _prepare(sources, dtypes, n_gather, name):
    n = len(sources)
    arrays, where = [], []
    for parts, layer in sources:
        found = []
        for part in parts:
            known = [i for i, v in enumerate(arrays) if v is part]
            if not known:
                arrays.append(part)
            found.append(known[0] if known else len(arrays) - 1)
        where.append((found, layer))
    shapes = [(sum(p.shape[1] for p in parts), parts[0].shape[2]) for parts, _ in sources]

    def body(*refs):
        ins, refs = refs[:len(arrays)], refs[len(arrays):]
        stage, outs = refs[:n], refs[n:n + n_gather]
        send, recv, local_sem = refs[n + n_gather:]
        x, y, c = _place()
        sibling = (x, y, 1 - c)
        chips = _other_chips(x, y)
        mine = 4 * x + 2 * y + c

        def copy(a, k, block, to, src=None):
            dst = outs[a].at[block]
            return pltpu.make_async_remote_copy(
                src_ref=dst if src is None else src, dst_ref=dst, send_sem=send.at[a * AG_COPIES + k],
                recv_sem=recv.at[a * AG_COPIES + k], device_id=to, device_id_type=MESH)

        def cast(a):
            found, layer = where[a]
            at = 0
            for i in found:
                rows = arrays[i].shape[1]
                stage[a][at:at + rows, :] = ins[i][layer].astype(dtypes[a])
                at += rows

        for a in range(n_gather):
            cast(a)
        started = []
        for a in range(n_gather):
            own = pltpu.make_async_copy(stage[a], outs[a].at[mine], local_sem.at[a])
            own.start()
            started.append(own)
        sends = []
        for a in range(n_gather):
            sends.append(copy(a, 0, mine, sibling, src=stage[a]))
            sends += [copy(a, 1 + j, mine, (*chip, c), src=stage[a]) for j, chip in enumerate(chips)]
        for cp in sends:
            cp.start()
        for a in range(n_gather, n):
            cast(a)
        for j, (px, py) in enumerate(chips):
            block = 4 * px + 2 * py + c
            for a in range(n_gather):
                copy(a, 1 + j, block, (x, y, c)).wait_recv()
                forward = copy(a, 4 + j, block, sibling)
                forward.start()
                sends.append(forward)
        for a in range(n_gather):
            copy(a, 0, 4 * x + 2 * y + (1 - c), (x, y, c)).wait_recv()
            for j, (px, py) in enumerate(chips):
                copy(a, 4 + j, 4 * px + 2 * py + (1 - c), (x, y, c)).wait_recv()
        for cp in sends:
            cp.wait_send()
        for own in started:
            own.wait()

    res = pl.pallas_call(
        body, name=name,
        in_specs=[IN_VMEM] * len(arrays), out_specs=[IN_VMEM] * n + [ANY] * n_gather,
        out_shape=[_sds(s, dt) for s, dt in zip(shapes, dtypes)]
        + [_sds((N_DEV,) + s, dt) for s, dt in zip(shapes[:n_gather], dtypes)],
        scratch_shapes=[pltpu.SemaphoreType.DMA((n_gather * AG_COPIES,)), pltpu.SemaphoreType.DMA((n_gather * AG_COPIES,)),
                        pltpu.SemaphoreType.DMA((n_gather,))],
        compiler_params=pltpu.CompilerParams(vmem_limit_bytes=VMEM_LIMIT_BYTES),
    )(*arrays)
    return list(res[:n]), list(res[n:])


def _pair_sum(g, r, core, name):
    _, _, rows, cols = g.shape
    tr = _pick(rows, (512, 256, 128))

    def body(core_ref, g_ref, r_ref, p_ref):
        del core_ref
        p_ref[...] = (g_ref[...].astype(F32) + r_ref[...].astype(F32)).astype(p_ref.dtype)

    return pl.pallas_call(
        body, name=name,
        grid_spec=pltpu.PrefetchScalarGridSpec(
            num_scalar_prefetch=1, grid=(4, rows // tr),
            in_specs=[pl.BlockSpec((None, None, tr, cols), lambda q, i, core_ref: (q, core_ref[0], i, 0)),
                      pl.BlockSpec((None, tr, cols), lambda q, i, core_ref: (q, i, 0))],
            out_specs=pl.BlockSpec((None, tr, cols), lambda q, i, core_ref: (q, i, 0))),
        out_shape=_sds((4, rows, cols), g.dtype),
        compiler_params=_params("parallel", "parallel"),
    )(core, g, r)


def _adam(w, g, m, v):
    m2 = ADAM_B1 * m + (1.0 - ADAM_B1) * g
    v2 = ADAM_B2 * v + (1.0 - ADAM_B2) * (g * g)
    m_hat = m2 / (1.0 - ADAM_B1 ** ADAM_STEP)
    v_hat = v2 / (1.0 - ADAM_B2 ** ADAM_STEP)
    return -ADAM_LR * (m_hat / (jnp.sqrt(v_hat) + ADAM_EPS) + ADAM_WD * w), m2, v2


def _shard_update(own, index, received, w, m, v, layer, so_far, name):
    _, rows, cols = w.shape
    n_recv = received.shape[0]
    tr = _pick(rows, (512, 256, 128))

    def body(index_ref, p_ref, q_ref, w_ref, m_ref, v_ref, *rest):
        del index_ref
        g_out, d_out, m_out, v_out = rest[-4:]
        g = p_ref[...].astype(F32)
        for s in range(n_recv):
            g = g + q_ref[s].astype(F32)
        g_out[...] = g
        d_out[...], m_out[...], v_out[...] = _adam(w_ref[...], g, m_ref[...], v_ref[...])

    tile = pl.BlockSpec((None, tr, cols), lambda i, index_ref: (layer, i, 0))
    kept = list(so_far) if so_far is not None else []
    return pl.pallas_call(
        body, name=name,
        grid_spec=pltpu.PrefetchScalarGridSpec(
            num_scalar_prefetch=1, grid=(rows // tr,),
            in_specs=[pl.BlockSpec((None, tr, cols), lambda i, index_ref: (index_ref[0], i, 0)),
                      pl.BlockSpec((n_recv, tr, cols), lambda i, index_ref: (0, i, 0)), tile, tile, tile] + [ANY] * len(kept),
            out_specs=[tile] * 4),
        out_shape=[_sds(w.shape, F32)] * 4,
        input_output_aliases={6 + i: i for i in range(len(kept))},
        compiler_params=_params("parallel"),
    )(index, own, received, w, m, v, *kept)


def _natural_pieces(shape, r, c):
    if tuple(shape[-2:]) == (r, c) and all(n == 1 for n in shape[:-2]):
        return [((0,) * (len(shape) - 2), (0, c))]
    groups, width = shape[1], shape[3]
    assert len(shape) == 4 and shape[0] == 1 and shape[2] == r and groups * width == c, (shape, r, c)
    return [((0, g), (g * width, width)) for g in range(groups)]


def _replicated_update(shares, where, params, name):
    flat = [a for p in params if p is not None for a in p]

    def body(s_ref, *refs):
        ins, outs = refs[:len(flat)], refs[len(flat):]
        total = s_ref[0]
        for dev in range(1, N_DEV):
            total = total + s_ref[dev]
        i_at = o_at = 0
        for ranges, p in zip(where, params):
            chunks = [total[r0:r0 + r, :c] for r0, r, c in ranges]
            g2d = chunks[0] if len(chunks) == 1 else jnp.concatenate(chunks, axis=1)
            if p is None:
                outs[o_at][...] = g2d
                o_at += 1
                continue
            w_ref, m_ref, v_ref = ins[i_at:i_at + 3]
            for idx, (c0, cols) in _natural_pieces(p[0].shape, *g2d.shape):
                at = idx + (slice(None), slice(None))
                g = g2d[:, c0:c0 + cols]
                outs[o_at][at] = g
                outs[o_at + 1][at], outs[o_at + 2][at], outs[o_at + 3][at] = _adam(w_ref[at], g, m_ref[at], v_ref[at])
            i_at, o_at = i_at + 3, o_at + 4

    out_shape = []
    for ranges, p in zip(where, params):
        out_shape += [_sds((ranges[0][1], sum(c for _, _, c in ranges)), F32)] if p is None else [_sds(p[0].shape, F32)] * 4
    res = pl.pallas_call(
        body, name=name, out_shape=out_shape, compiler_params=pltpu.CompilerParams(vmem_limit_bytes=VMEM_LIMIT_BYTES),
    )(shares, *flat)
    out, at = [], 0
    for p in params:
        out.append(list(res[at:at + (1 if p is None else 4)]))
        at += 1 if p is None else 4
    return out


def _rope_tables(t):
    half = HEAD_DIM // 2
    inv_freq = ROPE_THETA ** (-(jnp.arange(half, dtype=F32) * 2.0) / HEAD_DIM)
    ang = jnp.arange(t, dtype=jnp.int32).astype(F32)[:, None] * inv_freq[None, :]
    cos, sin = jnp.cos(ang), jnp.sin(ang)
    return jnp.tile(jnp.concatenate([cos, cos], axis=1), (1, 2)), jnp.tile(jnp.concatenate([-sin, sin], axis=1), (1, 2))


def _dup_heads(w, d):
    kv = (w.shape[-1] - d) // 2
    lead = w.shape[:-1]

    def dup(part):
        heads = part.reshape(lead + (kv // HEAD_DIM, 1, HEAD_DIM))
        return jnp.broadcast_to(heads, lead + (kv // HEAD_DIM, 2, HEAD_DIM)).reshape(lead + (2 * kv,))

    return jnp.concatenate([w[..., :d], dup(w[..., d:d + kv]), dup(w[..., d + kv:])], axis=-1)


def _fold_heads(dw, d):
    kv = (dw.shape[-1] - d) // 4
    lead = dw.shape[:-1]

    def fold(part):
        return part.reshape(lead + (kv // HEAD_DIM, 2, HEAD_DIM)).sum(axis=-2).reshape(lead + (kv,))

    return jnp.concatenate([dw[..., :d], fold(dw[..., d:d + 2 * kv]), fold(dw[..., d + 2 * kv:])], axis=-1)


def _columns_full(gathered):
    _, rows, c = gathered.shape
    return jnp.transpose(gathered, (1, 0, 2)).reshape(rows, N_DEV * c)


def _rows_full(gathered):
    return gathered.reshape(-1, gathered.shape[-1])


def _columns_blocked(full):
    rows, cols = full.shape
    return jnp.transpose(full.reshape(rows, N_DEV, cols // N_DEV), (1, 0, 2)).astype(ACT)


def _rows_blocked(full):
    return full.reshape(N_DEV, full.shape[0] // N_DEV, full.shape[1]).astype(ACT)


def _pack_rows(parts, width):
    rows, where, at = [], [], 0
    for v in parts:
        r, c = v.shape
        n_rows = -(-r // 8) * 8
        chunks = []
        for c0 in range(0, c, width):
            chunk = v[:, c0:c0 + width].astype(F32)
            rows.append(jnp.pad(chunk, ((0, n_rows - r), (0, width - chunk.shape[1]))))
            chunks.append((at, r, chunk.shape[1]))
            at += n_rows
        where.append(chunks)
    return jnp.concatenate(rows, axis=0), where


def _halves(block):
    half = block.shape[0] // 2
    return (0, half), (half, half)


def _whole(block):
    return (0, block.shape[0])


EARLY = ("attn_w_qkv", "sgu_ln", "attn_w_o")
LATE = ("sgu_w_in", "sgu_w_out", "gu0", "gu1", "dn0", "dn1")
COLUMN_SHARDED = ("attn_w_qkv", "sgu_w_in", "gu0", "gu1")
BEFORE_ATTN = ("norm_mix_post", "norm_ffn_pre", "norm_ffn_post", "attn_b_o", "sgu_w_spatial", "sgu_b_spatial")
AFTER_ATTN = ("attn_b_qkv", "attn_sinks")
LAST = ("norm_mix_pre",)
WEIGHTS = ("norm_mix_pre", "norm_mix_post", "norm_ffn_pre", "norm_ffn_post", "attn_w_qkv", "attn_b_qkv", "attn_sinks", "attn_w_o",
           "attn_b_o", "sgu_w_in", "sgu_ln_g", "sgu_ln_b", "sgu_w_spatial", "sgu_b_spatial", "sgu_w_out", "ffn_w_gate_up", "ffn_w_down")


LAYER_OF = {"gu0": ("ffn_w_gate_up", 0), "gu1": ("ffn_w_gate_up", 1), "dn0": ("ffn_w_down", 0), "dn1": ("ffn_w_down", 1)}


def _source(tree, name):
    if name == "sgu_ln":
        return [tree["sgu_ln_g"][None], tree["sgu_ln_b"][None]], 0
    leaf, layer = LAYER_OF.get(name, (name, 0))
    return [tree[leaf]], layer


def kernel(x, norm_mix_pre, norm_mix_post, norm_ffn_pre, norm_ffn_post, attn_w_qkv, attn_b_qkv, attn_sinks, attn_w_o, attn_b_o, sgu_w_in, sgu_ln_g, sgu_ln_b, sgu_w_spatial, sgu_b_spatial, sgu_w_out, ffn_w_gate_up, ffn_w_down, loss_target, m_norm_mix_pre, m_norm_mix_post, m_norm_ffn_pre, m_norm_ffn_post, m_attn_w_qkv, m_attn_b_qkv, m_attn_sinks, m_attn_w_o, m_attn_b_o, m_sgu_w_in, m_sgu_ln_g, m_sgu_ln_b, m_sgu_w_spatial, m_sgu_b_spatial, m_sgu_w_out, m_ffn_w_gate_up, m_ffn_w_down, v_norm_mix_pre, v_norm_mix_post, v_norm_ffn_pre, v_norm_ffn_post, v_attn_w_qkv, v_attn_b_qkv, v_attn_sinks, v_attn_w_o, v_attn_b_o, v_sgu_w_in, v_sgu_ln_g, v_sgu_ln_b, v_sgu_w_spatial, v_sgu_b_spatial, v_sgu_w_out, v_ffn_w_gate_up, v_ffn_w_down):
    w = dict(norm_mix_pre=norm_mix_pre, norm_mix_post=norm_mix_post, norm_ffn_pre=norm_ffn_pre, norm_ffn_post=norm_ffn_post,
             attn_w_qkv=attn_w_qkv, attn_b_qkv=attn_b_qkv, attn_sinks=attn_sinks, attn_w_o=attn_w_o, attn_b_o=attn_b_o,
             sgu_w_in=sgu_w_in, sgu_ln_g=sgu_ln_g, sgu_ln_b=sgu_ln_b, sgu_w_spatial=sgu_w_spatial, sgu_b_spatial=sgu_b_spatial,
             sgu_w_out=sgu_w_out, ffn_w_gate_up=ffn_w_gate_up, ffn_w_down=ffn_w_down)
    mom = dict(norm_mix_pre=m_norm_mix_pre, norm_mix_post=m_norm_mix_post, norm_ffn_pre=m_norm_ffn_pre, norm_ffn_post=m_norm_ffn_post,
               attn_w_qkv=m_attn_w_qkv, attn_b_qkv=m_attn_b_qkv, attn_sinks=m_attn_sinks, attn_w_o=m_attn_w_o, attn_b_o=m_attn_b_o,
               sgu_w_in=m_sgu_w_in, sgu_ln_g=m_sgu_ln_g, sgu_ln_b=m_sgu_ln_b, sgu_w_spatial=m_sgu_w_spatial,
               sgu_b_spatial=m_sgu_b_spatial, sgu_w_out=m_sgu_w_out, ffn_w_gate_up=m_ffn_w_gate_up, ffn_w_down=m_ffn_w_down)
    var = dict(norm_mix_pre=v_norm_mix_pre, norm_mix_post=v_norm_mix_post, norm_ffn_pre=v_norm_ffn_pre, norm_ffn_post=v_norm_ffn_post,
               attn_w_qkv=v_attn_w_qkv, attn_b_qkv=v_attn_b_qkv, attn_sinks=v_attn_sinks, attn_w_o=v_attn_w_o, attn_b_o=v_attn_b_o,
               sgu_w_in=v_sgu_w_in, sgu_ln_g=v_sgu_ln_g, sgu_ln_b=v_sgu_ln_b, sgu_w_spatial=v_sgu_w_spatial,
               sgu_b_spatial=v_sgu_b_spatial, sgu_w_out=v_sgu_w_out, ffn_w_gate_up=v_ffn_w_gate_up, ffn_w_down=v_ffn_w_down)
    w, mom, var = [{**tree, "ffn_w_gate_up": jnp.swapaxes(tree["ffn_w_gate_up"], 1, 2)} for tree in (w, mom, var)]
    xs, target = x[0], loss_target[0]
    t, d = xs.shape
    row = lambda v: v.reshape(1, -1).astype(F32)
    core = lax.axis_index("c").astype(jnp.int32).reshape(1)
    chip = (2 * lax.axis_index("x") + lax.axis_index("y")).astype(jnp.int32).reshape(1)
    me = (4 * lax.axis_index("x") + 2 * lax.axis_index("y") + lax.axis_index("c")).astype(jnp.int32).reshape(1)
    names = EARLY + LATE
    staged, early = _prepare([_source(w, n) for n in names], [F32 if n == "sgu_ln" else ACT for n in names], len(EARLY),
                             "weights_prepare")
    stage = dict(zip(names, staged))
    w_qkv = _dup_heads(_columns_full(early[0]), d)
    lg, lb = row(early[1][:, 0, :]), row(early[1][:, 1, :])
    w_o = _rows_full(early[2])
    b_qkv = _dup_heads(row(attn_b_qkv), d)
    sinks = attn_sinks.reshape(1, -1).astype(F32)
    ws = sgu_w_spatial[0].astype(F32)
    bs_t = sgu_b_spatial[0].astype(F32).T
    cos, sin = _rope_tables(t)
    gather = lambda name, so_far, **pieces: _gather_job(stage[name], so_far, **pieces)
    a_half = lambda name: _halves(stage[name])[0]
    b_half = lambda name: _halves(stage[name])[1]
    whole = lambda name: _whole(stage[name])

    quarter = lambda name, i: (i * stage[name].shape[0] // 4, stage[name].shape[0] // 4)
    (h0, q, kdup, vdup), ((g_gu0,),) = _attn_qkv_fwd(
        xs, row(norm_mix_pre[0]), w_qkv, b_qkv, cos, sin, jobs=[gather("gu0", None, sends=[quarter("gu0", 0)])])
    (o,), ((g_gu0,),) = _attn_fwd(q, kdup, vdup, sinks, jobs=[
        gather("gu0", g_gu0, sends=[quarter("gu0", i) for i in (1, 2, 3)], forwards=[quarter("gu0", 0)])])
    (m0, x1), ((g_gu0,), (g_dn0,)) = _proj_res(o, w_o, row(attn_b_o), row(norm_mix_post[0]), xs, "attn_out_fwd", jobs=[
        gather("gu0", g_gu0, forwards=[quarter("gu0", i) for i in (1, 2, 3)]), gather("dn0", None, sends=[whole("dn0")])])
    w_gu0 = g_gu0
    (h1, gu0, a0), ((g_dn0,), (g_in,), (g_gu1,)) = _ffn_up_fwd(x1, row(norm_ffn_pre[0]), w_gu0, jobs=[
        gather("dn0", g_dn0, forwards=[whole("dn0")]), gather("sgu_w_in", None, sends=[whole("sgu_w_in")]),
        gather("gu1", None, sends=[quarter("gu1", 0), quarter("gu1", 1)])])
    w_dn0 = _rows_full(g_dn0)
    (f0, x2), ((g_in,), (g_out,), (g_gu1,)) = _proj_res(a0, w_dn0, None, row(norm_ffn_post[0]), x1, "ffn_down_fwd0", jobs=[
        gather("sgu_w_in", g_in, forwards=[whole("sgu_w_in")]), gather("sgu_w_out", None, sends=[whole("sgu_w_out")]),
        gather("gu1", g_gu1, sends=[quarter("gu1", 2)], forwards=[quarter("gu1", 0), quarter("gu1", 1)])])
    w_in = g_in
    (h2, z, y), ((g_out,), (g_gu1,), (g_dn1,)) = _sgu_fwd(x2, row(norm_mix_pre[1]), w_in, lg, lb, ws, bs_t, jobs=[
        gather("sgu_w_out", g_out, forwards=[whole("sgu_w_out")]),
        gather("gu1", g_gu1, sends=[quarter("gu1", 3)], forwards=[quarter("gu1", 2)]), gather("dn1", None, sends=[a_half("dn1")])])
    w_out = _rows_full(g_out)
    (m1, x3), ((g_gu1,), (g_dn1,)) = _proj_res(y, w_out, None, row(norm_mix_post[1]), x2, "sgu_out_fwd", jobs=[
        gather("gu1", g_gu1, forwards=[quarter("gu1", 3)]), gather("dn1", g_dn1, sends=[b_half("dn1")], forwards=[a_half("dn1")])])
    w_gu1 = g_gu1
    (h3, gu1, a1), ((g_dn1,),) = _ffn_up_fwd(x3, row(norm_ffn_pre[1]), w_gu1, jobs=[gather("dn1", g_dn1, forwards=[b_half("dn1")])])
    w_dn1 = _rows_full(g_dn1)
    (f1, dx4, loss_tile), _ = _proj_res(a1, w_dn1, None, row(norm_ffn_post[1]), x3, "ffn_down_fwd1", target=target)

    to_sibling = lambda g: _scatter_job([g], 4, _to_sibling)
    pair = lambda g, r, name: _pair_sum(g.reshape((4, 2) + g.shape[1:]), r, core, "pair_sum_" + name)
    to_chips = lambda p, prev=None, piece=None: _scatter_job([p], 3, _to_owner_chip, prev=prev, piece=piece)
    out_g, out_d, out_m, out_v = {}, {}, {}, {}

    trees = [{**tree, "sgu_ln": jnp.stack([tree["sgu_ln_g"], tree["sgu_ln_b"]], axis=1)} for tree in (w, mom, var)]
    so_far = {}

    def update(name, own, index, received):
        leaf, layer = LAYER_OF.get(name, (name, 0))
        so_far[leaf] = _shard_update(own, index, received, *[tree[leaf] for tree in trees], layer, so_far.get(leaf), "update_" + name)
        for out, val in zip((out_g, out_d, out_m, out_v), so_far[leaf]):
            out[leaf] = val

    def finish(names, n_extra, shares, where, name):
        res = _replicated_update(shares, where, [(w[n], mom[n], var[n]) for n in names] + [None] * n_extra, name)
        for n, vals in zip(names, res):
            out_g[n], out_d[n], out_m[n], out_v[n] = vals
        return [vals[0] for vals in res[len(names):]]

    first_half = lambda p: _halves(p[0])[0]
    second_half = lambda p: _halves(p[0])[1]
    (df1, dgu1, dg_ffn_post1), _ = _post_bwd(dx4, f1, row(norm_ffn_post[1]), w_dn1, "ffn", "ffn_down_bwd1", gu1)
    b_gu1, _ = _wgrad(h3, dgu1, "ffn_up_wgrad1", ACT, transposed=True)
    dw_dn1, ((r_gu1,),) = _wgrad(a1, df1, "ffn_down_wgrad1", ACT, jobs=[to_sibling(b_gu1)])
    b_dn1 = _rows_blocked(dw_dn1)
    p_gu1 = pair(b_gu1, r_gu1, "gu1")
    (dx3, dg_ffn_pre1), ((q_gu1,), (r_dn1,)) = _pre_bwd(dgu1, w_gu1, x3, row(norm_ffn_pre[1]), dx4, "ffn_up_bwd1", True, jobs=[
        to_chips(p_gu1, piece=first_half(p_gu1)), to_sibling(b_dn1)])
    p_dn1 = pair(b_dn1, r_dn1, "dn1")
    (dm1, dy, dg_mix_post1), _ = _post_bwd(dx3, m1, row(norm_mix_post[1]), w_out, "sgu", "sgu_out_bwd")
    dw_out, _ = _wgrad(y, dm1, "sgu_out_wgrad", ACT)
    b_out = _rows_blocked(dw_out)
    (dz, dlg, dlb, dws, dbs_t), ((q_gu1,), (r_out,)) = _sgu_mix_bwd(z, dy, lg, lb, ws, bs_t, jobs=[
        to_chips(p_gu1, prev=[q_gu1], piece=second_half(p_gu1)), to_sibling(b_out)])
    update("gu1", p_gu1, chip, q_gu1)
    p_out = pair(b_out, r_out, "sgu_w_out")
    b_in, ((q_out,),) = _wgrad(h2, dz, "sgu_in_wgrad", ACT, out_block=stage["sgu_w_in"].shape[1], jobs=[to_chips(p_out)])
    update("sgu_w_out", p_out, chip, q_out)
    b_ln = jnp.stack([dlg.reshape(N_DEV, -1), dlb.reshape(N_DEV, -1)], axis=1)
    (dx2, dg_mix_pre1), ((r_in,), (r_ln,)) = _pre_bwd(dz, w_in, x2, row(norm_mix_pre[1]), dx3, "sgu_in_bwd",
                                                      jobs=[to_sibling(b_in), to_sibling(b_ln)])
    p_in, p_ln = pair(b_in, r_in, "sgu_w_in"), pair(b_ln, r_ln, "sgu_ln")
    (df0, dgu0, dg_ffn_post0), ((q_dn1,),) = _post_bwd(dx2, f0, row(norm_ffn_post[0]), w_dn0, "ffn", "ffn_down_bwd0", gu0,
                                                      jobs=[to_chips(p_dn1)])
    update("dn1", p_dn1, chip, q_dn1)
    b_gu0, ((q_in,), (q_ln,)) = _wgrad(h1, dgu0, "ffn_up_wgrad0", ACT, transposed=True, jobs=[to_chips(p_in), to_chips(p_ln)])
    update("sgu_w_in", p_in, chip, q_in)
    update("sgu_ln", p_ln, chip, q_ln)
    dw_dn0, ((r_gu0,),) = _wgrad(a0, df0, "ffn_down_wgrad0", ACT, jobs=[to_sibling(b_gu0)])
    b_dn0 = _rows_blocked(dw_dn0)
    p_gu0 = pair(b_gu0, r_gu0, "gu0")
    (dx1, dg_ffn_pre0), ((q_gu0,), (r_dn0,)) = _pre_bwd(dgu0, w_gu0, x1, row(norm_ffn_pre[0]), dx2, "ffn_up_bwd0", True, jobs=[
        to_chips(p_gu0, piece=first_half(p_gu0)), to_sibling(b_dn0)])
    p_dn0 = pair(b_dn0, r_dn0, "dn0")
    (dm0, do, dg_mix_post0, db_o), _ = _post_bwd(dx1, m0, row(norm_mix_post[0]), w_o, "attn", "attn_out_bwd")
    dw_o, _ = _wgrad(o, dm0, "attn_out_wgrad", ACT)
    b_o = _rows_blocked(dw_o)
    grads = {
        "norm_mix_post": jnp.concatenate([dg_mix_post0, dg_mix_post1], axis=0),
        "norm_ffn_pre": jnp.concatenate([dg_ffn_pre0, dg_ffn_pre1], axis=0),
        "norm_ffn_post": jnp.concatenate([dg_ffn_post0, dg_ffn_post1], axis=0),
        "attn_b_o": db_o,
        "sgu_w_spatial": dws,
        "sgu_b_spatial": dbs_t.T,
    }
    shares1, where1 = _pack_rows([grads[name] for name in BEFORE_ATTN], d)
    (dqkv, db_qkv, dsink), ((q_gu0,), (q_dn0,), (r_o,), (g_shares1,)) = _attn_bwd(q, kdup, vdup, do, sinks, cos, sin, jobs=[
        to_chips(p_gu0, prev=[q_gu0], piece=second_half(p_gu0)), to_chips(p_dn0), to_sibling(b_o),
        _gather_job(shares1, None, sends=[_whole(shares1)])])
    update("gu0", p_gu0, chip, q_gu0)
    update("dn0", p_dn0, chip, q_dn0)
    p_o = pair(b_o, r_o, "attn_w_o")
    grads.update({"attn_b_qkv": _fold_heads(db_qkv, d), "attn_sinks": dsink[:1, :d // HEAD_DIM]})
    shares2, where2 = _pack_rows([grads[name] for name in AFTER_ATTN] + [loss_tile[:1, :1]], d)
    blocked_half = lambda dw: _columns_blocked(_fold_heads(dw, d))
    dw_a, ((q_o,),) = _wgrad(h0, dqkv, "attn_qkv_wgrad_a", k_part=(0, 2), jobs=[to_chips(p_o)])
    update("attn_w_o", p_o, chip, q_o)
    b_qkv_a = blocked_half(dw_a)
    dw_b, ((q_qkv,), (g_shares1,), (g_shares2,)) = _wgrad(h0, dqkv, "attn_qkv_wgrad_b", k_part=(1, 2), jobs=[
        _scatter_job([b_qkv_a], N_DEV - 1, _to_owner, into=(0, d)), _gather_job(shares1, g_shares1, forwards=[_whole(shares1)]),
        _gather_job(shares2, None, sends=[_whole(shares2)])])
    b_qkv_b = blocked_half(dw_b)
    (dx0, dg_mix_pre0), ((q_qkv,), (g_shares2,)) = _pre_bwd(dqkv, w_qkv, xs, row(norm_mix_pre[0]), dx1, "attn_qkv_bwd", jobs=[
        _scatter_job([b_qkv_b], N_DEV - 1, _to_owner, prev=[q_qkv], into=(d // 2, d)),
        _gather_job(shares2, g_shares2, forwards=[_whole(shares2)])])
    b_qkv_grad = jnp.concatenate([b_qkv_a, b_qkv_b], axis=1)
    update("attn_w_qkv", b_qkv_grad, me, q_qkv)

    finish(BEFORE_ATTN, 0, g_shares1, where1, "replicated_update_before_attn")
    loss = finish(AFTER_ATTN, 1, g_shares2, where2, "replicated_update_after_attn")[0][0, 0]
    grads["norm_mix_pre"] = jnp.concatenate([dg_mix_pre0, dg_mix_pre1], axis=0)
    shares3, where3 = _pack_rows([grads[name] for name in LAST], d)
    _, (last_shares,) = _prepare([([shares3[None]], 0)], [F32], 1, "last_grads_all_gather")
    finish(LAST, 0, last_shares, where3, "replicated_update_last")

    for out in (out_g, out_d, out_m, out_v):
        out["sgu_ln_g"], out["sgu_ln_b"] = out["sgu_ln"][:, 0, :], out["sgu_ln"][:, 1, :]
        out["ffn_w_gate_up"] = jnp.swapaxes(out["ffn_w_gate_up"], 1, 2)

    return (loss, dx0[None], *[out_g[n] for n in WEIGHTS], *[out_d[n] for n in WEIGHTS],
            *[out_m[n] for n in WEIGHTS], *[out_v[n] for n in WEIGHTS])
```

```python
import functools
import math
import operator

import jax
import jax.numpy as jnp
from jax import lax
from jax.experimental import pallas as pl
from jax.experimental.pallas import tpu as pltpu

F32 = jnp.float32
ACT = jnp.bfloat16

EPS = 1e-6
HEAD_DIM = 64
PAIR = 2 * HEAD_DIM
GQA_GROUP = 4
WINDOW = 128
ROPE_THETA = 10000.0
SCALE = HEAD_DIM ** -0.5
MASKED = -1e30
LANES = 128

ADAM_LR, ADAM_B1, ADAM_B2, ADAM_EPS, ADAM_WD, ADAM_STEP = 0.001, 0.9, 0.999, 1e-08, 0.01, 10

N_DEV = 8
VMEM_LIMIT_BYTES = 56 * 1024 * 1024
MESH = pl.DeviceIdType.MESH
ANY = pl.BlockSpec(memory_space=pl.ANY)
IN_VMEM = pl.BlockSpec(memory_space=pltpu.VMEM)


def _pick(n, candidates):
    for c in candidates:
        if n % c == 0:
            return c
    return n


def _row_tile(t):
    return _pick(t, (512,))


def _params(*sem):
    return pltpu.CompilerParams(dimension_semantics=sem, vmem_limit_bytes=VMEM_LIMIT_BYTES)


def _full(shape):
    return pl.BlockSpec(shape, lambda *_: (0,) * len(shape))


def _rows(tm, n):
    return pl.BlockSpec((tm, n), lambda i: (i, 0))


def _sds(shape, dtype):
    return jax.ShapeDtypeStruct(shape, dtype)


def _place():
    return lax.axis_index("x"), lax.axis_index("y"), lax.axis_index("c")


def _other_chips(x, y):
    return [(1 - x, y), (x, 1 - y), (1 - x, 1 - y)]


class _Job:
    def __init__(self, inputs, out_shape, aliases, emit, n_remote, n_local=0):
        self.inputs, self.out_shape, self.aliases, self.emit = list(inputs), list(out_shape), dict(aliases), emit
        self.n_remote, self.n_local = n_remote, n_local


def _call(body, name, grid, in_specs, out_specs, out_shape, args, sem, scratch=(), jobs=()):
    n_in, n_out, n_scr = len(args), len(out_shape), len(scratch)
    j_in = [a for job in jobs for a in job.inputs]
    j_out = [s for job in jobs for s in job.out_shape]
    aliases, at_in, at_out = {}, n_in, n_out
    for job in jobs:
        aliases.update({at_in + i: at_out + o for i, o in job.aliases.items()})
        at_in, at_out = at_in + len(job.inputs), at_out + len(job.out_shape)
    n_remote = sum(job.n_remote for job in jobs)
    n_local = sum(job.n_local for job in jobs)

    def wrapped(*refs):
        ins, jin = refs[:n_in], refs[n_in:n_in + len(j_in)]
        rest = refs[n_in + len(j_in):]
        outs, jout = rest[:n_out], rest[n_out:n_out + len(j_out)]
        scr = rest[n_out + len(j_out):n_out + len(j_out) + n_scr]
        if not jobs:
            body(*ins, *outs, *scr)
            return
        send, recv, local = rest[n_out + len(j_out) + n_scr:]
        ids = [pl.program_id(a) for a in range(len(grid))]
        first = functools.reduce(operator.and_, [i == 0 for i in ids])
        last = functools.reduce(operator.and_, [i == g - 1 for i, g in zip(ids, grid)])

        def descriptors():
            place, found, i, o, r, l = _place(), [], 0, 0, 0, 0
            for job in jobs:
                for src, dst, to in job.emit(jin[i:i + len(job.inputs)], jout[o:o + len(job.out_shape)], place):
                    if to is None:
                        found.append(pltpu.make_async_copy(src, dst, local.at[l]))
                        l += 1
                    else:
                        found.append(pltpu.make_async_remote_copy(src_ref=src, dst_ref=dst, send_sem=send.at[r], recv_sem=recv.at[r],
                                                                  device_id=to, device_id_type=MESH))
                        r += 1
                i, o = i + len(job.inputs), o + len(job.out_shape)
            return found

        @pl.when(first)
        def _():
            for cp in descriptors():
                cp.start()

        body(*ins, *outs, *scr)

        @pl.when(last)
        def _():
            for cp in descriptors():
                cp.wait()

    sems = [pltpu.SemaphoreType.DMA((n_remote,)), pltpu.SemaphoreType.DMA((n_remote,)),
            pltpu.SemaphoreType.DMA((max(n_local, 1),))] if jobs else []
    res = pl.pallas_call(
        wrapped, name=name, grid=grid,
        in_specs=list(in_specs) + [ANY] * len(j_in), out_specs=list(out_specs) + [ANY] * len(j_out),
        out_shape=list(out_shape) + j_out, scratch_shapes=list(scratch) + sems, input_output_aliases=aliases,
        compiler_params=_params(*(("arbitrary",) * len(grid) if jobs else sem)),
    )(*args, *j_in)
    job_res, at = [], n_out
    for job in jobs:
        job_res.append(list(res[at:at + len(job.out_shape)]))
        at += len(job.out_shape)
    return list(res[:n_out]), job_res


def _gather_job(stage, gathered, sends=(), forwards=()):
    shape = _sds((N_DEV,) + stage.shape, stage.dtype)
    inputs = ([stage] if sends else []) + ([gathered] if gathered is not None else [])
    aliases = {len(inputs) - 1: 0} if gathered is not None else {}

    def emit(ins, outs, place):
        x, y, c = place
        sibling, chips, mine, g = (x, y, 1 - c), _other_chips(x, y), 4 * x + 2 * y + c, outs[0]
        found = []
        for r0, nr in sends:
            src, dst = ins[0].at[pl.ds(r0, nr)], g.at[mine, pl.ds(r0, nr)]
            found += [(src, dst, None), (src, dst, sibling)] + [(src, dst, (px, py, c)) for px, py in chips]
        for r0, nr in forwards:
            for px, py in chips:
                block = 4 * px + 2 * py + c
                found.append((ins[-1].at[block, pl.ds(r0, nr)], g.at[block, pl.ds(r0, nr)], sibling))
        return found

    return _Job(inputs, [shape], aliases, emit, 4 * len(sends) + 3 * len(forwards), len(sends))


def _relay_gather_job(stage, gathered, sends=(), relays=(), forwards=(), far=()):
    shape = _sds((N_DEV,) + stage.shape, stage.dtype)
    inputs = ([stage] if sends else []) + ([gathered] if gathered is not None else [])
    aliases = {len(inputs) - 1: 0} if gathered is not None else {}

    def emit(ins, outs, place):
        x, y, c = place
        sibling, beside_x, beside_y, g = (x, y, 1 - c), (1 - x, y, c), (x, 1 - y, c), outs[0]
        slot = lambda dev: 4 * dev[0] + 2 * dev[1] + dev[2]
        found = []
        for r0, nr in sends:
            src, dst = ins[0].at[pl.ds(r0, nr)], g.at[slot((x, y, c)), pl.ds(r0, nr)]
            found += [(src, dst, None), (src, dst, sibling), (src, dst, beside_x), (src, dst, beside_y)]

        def passed_on(block, piece, to):
            return ins[-1].at[block, pl.ds(*piece)], g.at[block, pl.ds(*piece)], to

        for piece, way in relays:
            found.append(passed_on(slot(beside_x), piece, beside_y) if way == 0 else passed_on(slot(beside_y), piece, beside_x))
        for piece in forwards:
            found += [passed_on(slot(beside_x), piece, sibling), passed_on(slot(beside_y), piece, sibling)]
        for piece in far:
            found.append(passed_on(slot((1 - x, 1 - y, c)), piece, sibling))
        return found

    return _Job(inputs, [shape], aliases, emit, 3 * len(sends) + len(relays) + 2 * len(forwards) + len(far), len(sends))


def _scatter_job(arrays, n_slots, route, prev=None, piece=None, into=None):
    shapes = [_sds((n_slots,) + ((into[1],) + v.shape[2:] if into else v.shape[1:]), v.dtype) for v in arrays]
    inputs = list(arrays) + (list(prev) if prev else [])
    aliases = {len(arrays) + i: i for i in range(len(arrays))} if prev else {}

    def emit(ins, outs, place):
        found = []
        for a in range(len(arrays)):
            for s in range(n_slots):
                block, to = route(s, *place)
                if into is not None:
                    found.append((ins[a].at[block], outs[a].at[s, pl.ds(into[0], ins[a].shape[1])], to))
                elif piece is None:
                    found.append((ins[a].at[block], outs[a].at[s], to))
                else:
                    found.append((ins[a].at[block, pl.ds(*piece)], outs[a].at[s, pl.ds(*piece)], to))
        return found

    return _Job(inputs, shapes, aliases, emit, len(arrays) * n_slots)


def _to_sibling(s, x, y, c):
    return 2 * s + (1 - c), (x, y, 1 - c)


def _to_owner_chip(s, x, y, c):
    px, py = _other_chips(x, y)[s]
    return 2 * px + py, (px, py, c)


def _to_owner(s, x, y, c):
    if s == 0:
        px, py, pc = x, y, 1 - c
    else:
        px, py = _other_chips(x, y)[(s - 1) % 3]
        pc = c if s <= 3 else 1 - c
    return 4 * px + 2 * py + pc, (px, py, pc)


def _rstd(x):
    return lax.rsqrt(jnp.mean(x * x, axis=-1, keepdims=True) + EPS)


def _rms_bwd(xhat, r, g, dy):
    dxh = dy * g
    return r * (dxh - xhat * jnp.mean(dxh * xhat, axis=-1, keepdims=True))


def _first_half(rows):
    lane = lax.broadcasted_iota(jnp.int32, (rows, PAIR), 1)
    return (lane % HEAD_DIM) < (HEAD_DIM // 2)


def _rot_half(v, first):
    return jnp.where(first, pltpu.roll(v, PAIR - HEAD_DIM // 2, 1), pltpu.roll(v, HEAD_DIM // 2, 1))


def _rope(v, cos, sin, first):
    return v * cos + _rot_half(v, first) * sin


def _rope_t(dv, cos, sin, first):
    return dv * cos + _rot_half(dv * sin, first)


def _dot(a, b):
    return jnp.dot(a, b, preferred_element_type=F32)


def _dot_nt(a, b):
    return lax.dot_general(a, b, (((1,), (1,)), ((), ())), preferred_element_type=F32)


def _dot_tn(a, b):
    return lax.dot_general(a, b, (((0,), (0,)), ((), ())), preferred_element_type=F32)


def _sigmoid(v):
    return 1.0 / (1.0 + jnp.exp(-v))


_GELU_K = math.sqrt(2.0 / math.pi)
_GELU_C = 0.044715


def _gelu(v):
    return v * (0.5 * (1.0 + jnp.tanh(_GELU_K * (v + _GELU_C * (v * v * v)))))


def _gelu_grad(v):
    t = jnp.tanh(_GELU_K * (v + _GELU_C * (v * v * v)))
    return 0.5 * (1.0 + t) + 0.5 * v * (1.0 - t * t) * (_GELU_K * (1.0 + 3.0 * _GELU_C * (v * v)))


def _attn_qkv_fwd(x, g, w, b, cos, sin, jobs=()):
    t, d = x.shape
    n = w.shape[1]
    kd = (n - d) // 2
    tm = _row_tile(t)
    ch = _pick(d, (512,))

    def body(x_ref, g_ref, w_ref, b_ref, cos_ref, sin_ref, h_ref, q_ref, k_ref, v_ref):
        xv = x_ref[...]
        hb = (xv * _rstd(xv) * g_ref[...]).astype(ACT)
        h_ref[...] = hb
        cosv, sinv = cos_ref[...], sin_ref[...]
        first = _first_half(tm)

        def proj(lo, width):
            return _dot(hb, w_ref[:, lo:lo + width]) + b_ref[:, lo:lo + width]

        for c in range(0, d, ch):
            y = proj(c, ch)
            for s in range(0, ch, PAIR):
                q_ref[:, c + s:c + s + PAIR] = (_rope(y[:, s:s + PAIR], cosv, sinv, first) * SCALE).astype(ACT)
        y = proj(d, kd)
        for s in range(0, kd, PAIR):
            k_ref[:, s:s + PAIR] = _rope(y[:, s:s + PAIR], cosv, sinv, first).astype(ACT)
        v_ref[...] = proj(d + kd, kd).astype(ACT)

    return _call(
        body, "attn_qkv_fwd", (t // tm,),
        [_rows(tm, d), _full((1, d)), _full((d, n)), _full((1, n)), _rows(tm, PAIR), _rows(tm, PAIR)],
        [_rows(tm, d), _rows(tm, d), _rows(tm, kd), _rows(tm, kd)],
        [_sds((t, d), ACT), _sds((t, d), ACT), _sds((t, kd), ACT), _sds((t, kd), ACT)],
        (x, g, w, b, cos, sin), ("parallel",), jobs=jobs)


STACK = GQA_GROUP * WINDOW


def _band_mask(has_prev):
    row = lax.broadcasted_iota(jnp.int32, (STACK, 2 * WINDOW), 0) % WINDOW
    col = lax.broadcasted_iota(jnp.int32, (STACK, 2 * WINDOW), 1)
    return ((col < WINDOW) & (col > row) & has_prev) | ((col >= WINDOW) & (col - WINDOW <= row))


def _lane_halves():
    lane = lax.broadcasted_iota(jnp.int32, (1, PAIR), 1)
    return lane < HEAD_DIM, lane >= HEAD_DIM


def _stack_heads(ref, h, halves):
    parts = []
    for p in range(2):
        pair = ref[:, (2 * h + p) * PAIR:(2 * h + p + 1) * PAIR]
        parts += [jnp.where(half, pair, jnp.zeros_like(pair)) for half in halves]
    return jnp.concatenate(parts, axis=0)


def _sink_column(sink_ref, h):
    row = lax.broadcasted_iota(jnp.int32, (STACK, 1), 0)
    col = jnp.full((STACK, 1), sink_ref[0, GQA_GROUP * h + GQA_GROUP - 1], F32)
    for j in range(GQA_GROUP - 2, -1, -1):
        col = jnp.where(row < (j + 1) * WINDOW, sink_ref[0, GQA_GROUP * h + j], col)
    return col


def _probs(scores, valid, sink):
    s = jnp.where(valid, scores, MASKED)
    m = jnp.maximum(jnp.max(s, axis=-1, keepdims=True), sink)
    p = jnp.exp(s - m)
    psink = jnp.exp(sink - m)
    inv = 1.0 / (jnp.sum(p, axis=-1, keepdims=True) + psink)
    return p * inv, psink * inv


def _attn_fwd(q, kd_, vd, sinks, jobs=()):
    t, d = q.shape
    kd = kd_.shape[1]
    cur = lambda n: (n, 0)
    prev = lambda n: (jnp.maximum(n - 1, 0), 0)

    def body(sink_ref, q_ref, kc_ref, kp_ref, vc_ref, vp_ref, o_ref):
        valid = _band_mask(pl.program_id(0) > 0)
        halves = _lane_halves()
        heads = range(kd // PAIR)
        hs = [slice(h * PAIR, (h + 1) * PAIR) for h in heads]
        scores = [_dot_nt(_stack_heads(q_ref, h, halves), jnp.concatenate([kp_ref[:, hs[h]], kc_ref[:, hs[h]]], axis=0)) for h in heads]
        probs = [_probs(scores[h], valid, _sink_column(sink_ref, h))[0].astype(ACT) for h in heads]
        for h in heads:
            v2 = jnp.concatenate([vp_ref[:, hs[h]], vc_ref[:, hs[h]]], axis=0)
            vs = jnp.concatenate([jnp.where(half, v2, jnp.zeros_like(v2)) for half in halves], axis=0)
            pr = probs[h]
            for p in range(2):
                both = jnp.concatenate([pr[2 * p * WINDOW:(2 * p + 1) * WINDOW], pr[(2 * p + 1) * WINDOW:(2 * p + 2) * WINDOW]], axis=1)
                o_ref[:, (2 * h + p) * PAIR:(2 * h + p + 1) * PAIR] = _dot(both, vs).astype(ACT)

    kv_c, kv_p = pl.BlockSpec((WINDOW, kd), cur), pl.BlockSpec((WINDOW, kd), prev)
    return _call(
        body, "attn_fwd", (t // WINDOW,),
        [pl.BlockSpec(memory_space=pltpu.SMEM), pl.BlockSpec((WINDOW, d), cur), kv_c, kv_p, kv_c, kv_p],
        [pl.BlockSpec((WINDOW, d), cur)], [_sds((t, d), ACT)],
        (sinks, q, kd_, kd_, vd, vd), ("parallel",), jobs=jobs)


def _attn_bwd(q, kd_, vd, do, sinks, cos, sin, jobs=()):
    t, d = q.shape
    kd = kd_.shape[1]
    nb = t // WINDOW
    n_out = d + 2 * kd
    cur = lambda n: (jnp.minimum(n, nb - 1), 0)
    prev = lambda n: (jnp.maximum(jnp.minimum(n, nb - 1) - 1, 0), 0)
    late = lambda n: (jnp.maximum(n - 1, 0), 0)

    def body(sink_ref, q_ref, kc_ref, kp_ref, vc_ref, vp_ref, do_ref, cosc_ref, sinc_ref, cosp_ref, sinp_ref,
             out_ref, db_ref, dsink_ref, dq_keep, dk_keep, dv_keep):
        n = pl.program_id(0)

        @pl.when(n == 0)
        def _():
            for ref in (db_ref, dsink_ref, dq_keep, dk_keep, dv_keep):
                ref[...] = jnp.zeros_like(ref)

        valid = _band_mask(n > 0) & (n < nb)
        halves = _lane_halves()
        first = _first_half(WINDOW)
        slot = lax.broadcasted_iota(jnp.int32, dsink_ref.shape, 1)
        top = lax.broadcasted_iota(jnp.int32, dsink_ref.shape, 0) == 0
        dsink = jnp.zeros(dsink_ref.shape, F32)

        def emit(cols, done):
            out_ref[:, cols] = done.astype(ACT)
            db_ref[:, cols] += jnp.sum(done.astype(F32), axis=0, keepdims=True)

        heads = range(kd // PAIR)
        hs = [slice(h * PAIR, (h + 1) * PAIR) for h in heads]
        k2 = [jnp.concatenate([kp_ref[:, hs[h]], kc_ref[:, hs[h]]], axis=0) for h in heads]
        v2 = [jnp.concatenate([vp_ref[:, hs[h]], vc_ref[:, hs[h]]], axis=0) for h in heads]
        qs = [_stack_heads(q_ref, h, halves) for h in heads]
        dos = [_stack_heads(do_ref, h, halves) for h in heads]
        scores = [_dot_nt(qs[h], k2[h]) for h in heads]
        dps = [_dot_nt(dos[h], v2[h]) for h in heads]
        prs, dss = [], []
        for h in heads:
            pr, psink = _probs(scores[h], valid, _sink_column(sink_ref, h))
            delta = jnp.sum(pr * dps[h], axis=-1, keepdims=True)
            dss.append((pr * (dps[h] - delta)).astype(ACT))
            prs.append(pr.astype(ACT))
            leak = psink * delta
            for j in range(GQA_GROUP):
                share = jnp.sum(leak[j * WINDOW:(j + 1) * WINDOW], axis=0, keepdims=True)
                dsink = dsink - jnp.where(top & (slot == GQA_GROUP * h + j), share, 0.0)
        dqs = [_dot(dss[h], k2[h]) for h in heads]
        dk2 = [_dot_tn(dss[h], qs[h]) for h in heads]
        dv2 = [_dot_tn(prs[h], dos[h]) for h in heads]
        for h in heads:
            for p in range(2):
                ps = slice((2 * h + p) * PAIR, (2 * h + p + 1) * PAIR)
                dqp = jnp.where(halves[0], dqs[h][2 * p * WINDOW:(2 * p + 1) * WINDOW], dqs[h][(2 * p + 1) * WINDOW:(2 * p + 2) * WINDOW])
                emit(ps, dq_keep[:, ps])
                dq_keep[:, ps] = (_rope_t(dqp, cosc_ref[...], sinc_ref[...], first) * SCALE).astype(ACT)
            ks = slice(d + h * PAIR, d + (h + 1) * PAIR)
            vs = slice(d + kd + h * PAIR, d + kd + (h + 1) * PAIR)
            emit(ks, dk_keep[:, hs[h]] + _rope_t(dk2[h][:WINDOW], cosp_ref[...], sinp_ref[...], first))
            dk_keep[:, hs[h]] = _rope_t(dk2[h][WINDOW:], cosc_ref[...], sinc_ref[...], first)
            emit(vs, dv_keep[:, hs[h]] + dv2[h][:WINDOW])
            dv_keep[:, hs[h]] = dv2[h][WINDOW:]
        dsink_ref[...] += dsink

    kv_c, kv_p = pl.BlockSpec((WINDOW, kd), cur), pl.BlockSpec((WINDOW, kd), prev)
    tab_c, tab_p = pl.BlockSpec((WINDOW, PAIR), cur), pl.BlockSpec((WINDOW, PAIR), prev)
    return _call(
        body, "attn_bwd", (nb + 1,),
        [pl.BlockSpec(memory_space=pltpu.SMEM), pl.BlockSpec((WINDOW, d), cur), kv_c, kv_p, kv_c, kv_p,
         pl.BlockSpec((WINDOW, d), cur), tab_c, tab_c, tab_p, tab_p],
        [pl.BlockSpec((WINDOW, n_out), late), _full((1, n_out)), _full((8, LANES))],
        [_sds((t, n_out), ACT), _sds((1, n_out), F32), _sds((8, LANES), F32)],
        (sinks, q, kd_, kd_, vd, vd, do, cos, sin, cos, sin), ("arbitrary",),
        scratch=[pltpu.VMEM((WINDOW, d), ACT), pltpu.VMEM((WINDOW, kd), F32), pltpu.VMEM((WINDOW, kd), F32)], jobs=jobs)


def _proj_res(a, w, b, g, x, name, target=None, jobs=()):
    blocked = a.ndim == 3
    t = a.shape[-2]
    k, d = w.shape
    tm = _row_tile(t)
    has_bias = b is not None

    def body(*refs):
        a_ref, w_ref = refs[:2]
        b_ref = refs[2] if has_bias else None
        g_ref, x_ref, m_ref, xo_ref = refs[2 + has_bias:]
        if blocked:
            c = a.shape[2]
            m = _dot(a_ref[0], w_ref[:c, :])
            for j in range(1, a.shape[0]):
                m = m + _dot(a_ref[j], w_ref[j * c:(j + 1) * c, :])
        else:
            m = _dot(a_ref[...], w_ref[...])
        if has_bias:
            m = m + b_ref[...]
        m_ref[...] = m.astype(ACT)
        xo_ref[...] = x_ref[...] + (m * _rstd(m)) * g_ref[...]

    def body_with_loss(a_ref, w_ref, g_ref, x_ref, t_ref, m_ref, dy_ref, loss_ref):
        @pl.when(pl.program_id(0) == 0)
        def _():
            loss_ref[...] = jnp.zeros_like(loss_ref)

        body(a_ref, w_ref, g_ref, x_ref, m_ref, dy_ref)
        err = dy_ref[...] - t_ref[...]
        dy_ref[...] = err * (1.0 / d)
        loss_ref[...] += 0.5 * jnp.sum(jnp.mean(err * err, axis=-1, keepdims=True), axis=0, keepdims=True)

    ins = [a, w] + ([b] if has_bias else []) + [g, x]
    a_spec = pl.BlockSpec((a.shape[0], tm, a.shape[2]), lambda i: (0, i, 0)) if blocked else _rows(tm, k)
    specs = [a_spec, _full((k, d))] + ([_full((1, d))] if has_bias else []) + [_full((1, d)), _rows(tm, d)]
    if target is not None:
        return _call(body_with_loss, name, (t // tm,), specs + [_rows(tm, d)], [_rows(tm, d), _rows(tm, d), _full((8, LANES))],
                     [_sds((t, d), ACT), _sds((t, d), F32), _sds((8, LANES), F32)], ins + [target], ("arbitrary",), jobs=jobs)
    return _call(body, name, (t // tm,), specs, [_rows(tm, d), _rows(tm, d)], [_sds((t, d), ACT), _sds((t, d), F32)], ins,
                 ("parallel",), jobs=jobs)


def _post_bwd(dres, m, g, w, mode, name, gu=None, jobs=()):
    t, d = dres.shape
    k = w.shape[0]
    tm = _row_tile(t)
    kc = _pick(k, (1408, 1024, 512))

    def body(*refs):
        d_ref, m_ref, g_ref, w_ref = refs[:4]
        gu_ref = refs[4] if mode == "ffn" else None
        outs = refs[4 + (mode == "ffn"):]
        dm_ref, da_ref, dg_ref = outs[:3]
        i = pl.program_id(0)

        @pl.when(i == 0)
        def _():
            dg_ref[...] = jnp.zeros_like(dg_ref)
            if mode == "attn":
                outs[3][...] = jnp.zeros_like(outs[3])

        dv, mv = d_ref[...], m_ref[...].astype(F32)
        r = _rstd(mv)
        mh = mv * r
        dg_ref[...] += jnp.sum(dv * mh, axis=0, keepdims=True)
        dm = _rms_bwd(mh, r, g_ref[...], dv)
        dmb = dm.astype(ACT)
        dm_ref[...] = dmb
        if mode == "attn":
            outs[3][...] += jnp.sum(dm, axis=0, keepdims=True)
        if mode == "ffn":
            nb, _, cb = gu.shape
            for j in range(nb // 2):
                da = _dot_nt(dmb, w_ref[j * cb:(j + 1) * cb, :]).astype(ACT)
                gate, up = gu_ref[j], gu_ref[nb // 2 + j]
                sg = _sigmoid(gate.astype(F32)).astype(ACT)
                gs = gate * sg
                da_ref[j] = da * up * (sg + gs - gs * sg)
                da_ref[nb // 2 + j] = da * gs
        else:
            for c in range(0, k, kc):
                da = _dot_nt(dmb, w_ref[c:c + kc, :])
                da_ref[:, c:c + kc] = da.astype(ACT)

    ins = [dres, m, g, w]
    specs = [_rows(tm, d), _rows(tm, d), _full((1, d)), _full((k, d))]
    if mode == "ffn":
        slab = pl.BlockSpec((gu.shape[0], tm, gu.shape[2]), lambda i: (0, i, 0))
        ins.append(gu)
        specs.append(slab)
        out_specs = [_rows(tm, d), slab, _full((1, d))]
        out_shape = [_sds((t, d), ACT), _sds(gu.shape, ACT), _sds((1, d), F32)]
    else:
        out_specs = [_rows(tm, d), _rows(tm, k), _full((1, d))]
        out_shape = [_sds((t, d), ACT), _sds((t, k), ACT), _sds((1, d), F32)]
    if mode == "attn":
        out_specs.append(_full((1, d)))
        out_shape.append(_sds((1, d), F32))
    return _call(body, name, (t // tm,), specs, out_specs, out_shape, ins, ("arbitrary",), jobs=jobs)


def _pre_bwd(dy, w, x, g, dres, name, transposed=False, jobs=()):
    t, d = x.shape
    tm = _row_tile(t)

    def body(dy_ref, w_ref, x_ref, g_ref, dres_ref, dx_ref, dg_ref):
        @pl.when(pl.program_id(0) == 0)
        def _():
            dg_ref[...] = jnp.zeros_like(dg_ref)

        dh = jnp.zeros((tm, d), F32)
        if w.ndim == 3 and transposed:
            for j in range(w.shape[0]):
                dh = dh + _dot(dy_ref[j], w_ref[j])
        elif w.ndim == 3:
            c = w.shape[2]
            for j in range(w.shape[0]):
                dh = dh + _dot_nt(dy_ref[j] if dy.ndim == 3 else dy_ref[:, j * c:(j + 1) * c], w_ref[j])
        else:
            n = w.shape[1]
            nc = _pick(n, (1408, 1024, 512))
            for c in range(0, n, nc):
                dh = dh + _dot_nt(dy_ref[:, c:c + nc], w_ref[:, c:c + nc])
        xv = x_ref[...]
        r = _rstd(xv)
        xh = xv * r
        dg_ref[...] += jnp.sum(dh * xh, axis=0, keepdims=True)
        dx_ref[...] = dres_ref[...] + _rms_bwd(xh, r, g_ref[...], dh)

    dy_spec = pl.BlockSpec((dy.shape[0], tm, dy.shape[2]), lambda i: (0, i, 0)) if dy.ndim == 3 else _rows(tm, dy.shape[1])
    return _call(
        body, name, (t // tm,),
        [dy_spec, _full(w.shape), _rows(tm, d), _full((1, d)), _rows(tm, d)],
        [_rows(tm, d), _full((1, d))], [_sds((t, d), F32), _sds((1, d), F32)],
        (dy, w, x, g, dres), ("arbitrary",), jobs=jobs)


def _wgrad(a, b, name, out_dtype=F32, out_block=None, transposed=False, k_part=None, jobs=()):
    t = a.shape[-2]
    tt = _pick(t, (1024,))
    steps = t // tt
    if a.ndim == 3:
        k_steps, _, tk = a.shape
        a_spec = pl.BlockSpec((None, tt, tk), lambda i, j, s: (i, s, 0))
    elif k_part is not None:
        part, parts = k_part
        tk, k_steps = a.shape[1] // parts, 1
        a_spec = pl.BlockSpec((tt, tk), lambda i, j, s: (s, part))
    else:
        tk = _pick(a.shape[1], (1024, 1408))
        k_steps = a.shape[1] // tk
        a_spec = pl.BlockSpec((tt, tk), lambda i, j, s: (s, i))
    k = k_steps * tk
    group = 1
    if b.ndim == 3:
        n_blocks, _, tn = b.shape
        group = 2 if n_blocks % 2 == 0 else 1
        n_steps, per = n_blocks // group, 1
        b_spec = pl.BlockSpec((group, tt, tn), lambda i, j, s: (j, s, 0))
        if transposed:
            out_spec, out_shape = pl.BlockSpec((group, tn, tk), lambda i, j, s: (j, 0, i)), _sds((n_blocks, tn, k), out_dtype)
        else:
            out_spec, out_shape = pl.BlockSpec((group, tk, tn), lambda i, j, s: (j, i, 0)), _sds((n_blocks, k, tn), out_dtype)
    else:
        n = b.shape[1]
        tn = _pick(n, (1024, 1408))
        n_steps = n // tn
        b_spec = pl.BlockSpec((tt, tn), lambda i, j, s: (s, j))
        if out_block is None:
            per = 0
            out_spec, out_shape = pl.BlockSpec((tk, tn), lambda i, j, s: (i, j)), _sds((k, n), out_dtype)
        else:
            per = tn // out_block
            out_spec = pl.BlockSpec((per, tk, out_block), lambda i, j, s: (j, i, 0))
            out_shape = _sds((n // out_block, k, out_block), out_dtype)

    def body(a_ref, b_ref, o_ref, acc_ref):
        s = pl.program_id(2)

        @pl.when(s == 0)
        def _():
            acc_ref[...] = jnp.zeros_like(acc_ref)

        if b.ndim == 3:
            av = a_ref[...]
            for p in range(group):
                acc_ref[p] += _dot_tn(b_ref[p], av) if transposed else _dot_tn(av, b_ref[p])
        else:
            acc_ref[...] += _dot_tn(a_ref[...], b_ref[...])

        @pl.when(s == steps - 1)
        def _():
            if b.ndim == 2 and per >= 1:
                for p in range(per):
                    o_ref[p] = acc_ref[:, p * out_block:(p + 1) * out_block].astype(out_dtype)
            else:
                o_ref[...] = acc_ref[...].astype(out_dtype)

    acc_shape = ((group, tn, tk) if transposed else (group, tk, tn)) if b.ndim == 3 else (tk, tn)
    res, job_res = _call(
        body, name, (k_steps, n_steps, steps), [a_spec, b_spec], [out_spec], [out_shape],
        (a, b), ("parallel", "parallel", "arbitrary"), scratch=[pltpu.VMEM(acc_shape, F32)], jobs=jobs)
    return res[0], job_res


def _ffn_up_fwd(x, g, w, jobs=()):
    t, d = x.shape
    nb, c, _ = w.shape
    tm = _row_tile(t)

    def body(x_ref, g_ref, w_ref, h_ref, gu_ref, a_ref):
        xv = x_ref[...]
        hb = (xv * _rstd(xv) * g_ref[...]).astype(ACT)
        h_ref[...] = hb
        for j in range(nb // 2):
            gate = _dot_nt(hb, w_ref[j])
            up = _dot_nt(hb, w_ref[nb // 2 + j])
            gu_ref[j] = gate.astype(ACT)
            gu_ref[nb // 2 + j] = up.astype(ACT)
            a_ref[j] = (gate * _sigmoid(gate) * up).astype(ACT)

    slab = lambda n: pl.BlockSpec((n, tm, c), lambda i: (0, i, 0))
    return _call(
        body, "ffn_up_fwd", (t // tm,),
        [_rows(tm, d), _full((1, d)), _full(w.shape)],
        [_rows(tm, d), slab(nb), slab(nb // 2)],
        [_sds((t, d), ACT), _sds((nb, t, c), ACT), _sds((nb // 2, t, c), ACT)],
        (x, g, w), ("parallel",), jobs=jobs)


def _causal(ws):
    row = lax.broadcasted_iota(jnp.int32, ws.shape, 0)
    col = lax.broadcasted_iota(jnp.int32, ws.shape, 1)
    return jnp.where(col <= row, ws, 0.0)


def _sgu_ln(v, lg, lb):
    mu = jnp.mean(v, axis=-1, keepdims=True)
    vc = v - mu
    rs = lax.rsqrt(jnp.mean(vc * vc, axis=-1, keepdims=True) + EPS)
    vh = vc * rs
    return vh, rs, vh * lg + lb


def _sgu_fwd(x, g, w, lg, lb, ws, bs_t, jobs=()):
    t, d = x.shape
    nb, _, c = w.shape
    n = nb * c
    groups = d // WINDOW
    tm = _row_tile(t)

    def body(x_ref, g_ref, w_ref, lg_ref, lb_ref, ws_ref, bs_ref, h_ref, z_ref, y_ref, zf_ref):
        xv = x_ref[...]
        hb = (xv * _rstd(xv) * g_ref[...]).astype(ACT)
        h_ref[...] = hb
        for j in range(nb):
            zf_ref[:, j * c:(j + 1) * c] = _dot(hb, w_ref[j])
        z_ref[...] = zf_ref[...].astype(ACT)
        gs = [slice(gi * WINDOW, (gi + 1) * WINDOW) for gi in range(groups)]
        wsg = [_causal(ws_ref[gi]).astype(ACT) for gi in range(groups)]
        for r in range(0, tm, WINDOW):
            u = _gelu(zf_ref[r:r + WINDOW, :d])
            _, _, vn = _sgu_ln(_gelu(zf_ref[r:r + WINDOW, d:]), lg_ref[...], lb_ref[...])
            mixed = [_dot(wsg[gi], vn[:, gs[gi]].astype(ACT)) for gi in range(groups)]
            for gi in range(groups):
                y_ref[r:r + WINDOW, gs[gi]] = (u[:, gs[gi]] * (mixed[gi] + bs_ref[:, gi:gi + 1])).astype(ACT)

    vec = _full((1, d))
    return _call(
        body, "sgu_fwd", (t // tm,),
        [_rows(tm, d), vec, _full((nb, d, c)), vec, vec, _full((groups, WINDOW, WINDOW)), _full((WINDOW, groups))],
        [_rows(tm, d), _rows(tm, n), _rows(tm, d)], [_sds((t, d), ACT), _sds((t, n), ACT), _sds((t, d), ACT)],
        (x, g, w, lg, lb, ws, bs_t), ("parallel",), scratch=[pltpu.VMEM((tm, n), F32)], jobs=jobs)


def _sgu_mix_bwd(z, dy, lg, lb, ws, bs_t, jobs=()):
    t, n = z.shape
    d = n // 2
    groups = d // WINDOW
    tm = _row_tile(t)

    def body(z_ref, dy_ref, lg_ref, lb_ref, ws_ref, bs_ref, dz_ref, dlg_ref, dlb_ref, dws_ref, dbs_ref):
        @pl.when(pl.program_id(0) == 0)
        def _():
            for ref in (dlg_ref, dlb_ref, dws_ref, dbs_ref):
                ref[...] = jnp.zeros_like(ref)

        lane = lax.broadcasted_iota(jnp.int32, (WINDOW, groups), 1)
        gs = [slice(gi * WINDOW, (gi + 1) * WINDOW) for gi in range(groups)]
        wsg = [_causal(ws_ref[gi]).astype(ACT) for gi in range(groups)]
        for c in range(0, tm, WINDOW):
            rows = slice(c, c + WINDOW)
            zu, zv = z_ref[rows, :d].astype(F32), z_ref[rows, d:].astype(F32)
            u = _gelu(zu)
            vh, rs, vn = _sgu_ln(_gelu(zv), lg_ref[...], lb_ref[...])
            dyv = dy_ref[rows, :].astype(F32)
            vng = [vn[:, gs[gi]].astype(ACT) for gi in range(groups)]
            dmix = dyv * u
            dmb = [dmix[:, gs[gi]].astype(ACT) for gi in range(groups)]
            mixed = [_dot(wsg[gi], vng[gi]) for gi in range(groups)]
            dvn_parts = [_dot_tn(wsg[gi], dmb[gi]) for gi in range(groups)]
            dws_parts = [_dot_nt(dmb[gi], vng[gi]) for gi in range(groups)]
            for gi in range(groups):
                dz_ref[rows, gs[gi]] = (dyv[:, gs[gi]] * (mixed[gi] + bs_ref[:, gi:gi + 1]) * _gelu_grad(zu[:, gs[gi]])).astype(ACT)
                dws_ref[:, gs[gi]] += _causal(dws_parts[gi])
                dbs_ref[...] += jnp.where(lane == gi, jnp.sum(dmix[:, gs[gi]], axis=-1, keepdims=True), 0.0)
            dvn = jnp.concatenate(dvn_parts, axis=-1)
            dlg_ref[...] += jnp.sum(dvn * vh, axis=0, keepdims=True)
            dlb_ref[...] += jnp.sum(dvn, axis=0, keepdims=True)
            dvh = dvn * lg_ref[...]
            dv = rs * (dvh - jnp.mean(dvh, axis=-1, keepdims=True) - vh * jnp.mean(dvh * vh, axis=-1, keepdims=True))
            dz_ref[rows, d:] = (dv * _gelu_grad(zv)).astype(ACT)

    vec = _full((1, d))
    return _call(
        body, "sgu_mix_bwd", (t // tm,),
        [_rows(tm, n), _rows(tm, d), vec, vec, _full((groups, WINDOW, WINDOW)), _full((WINDOW, groups))],
        [_rows(tm, n), vec, vec, _full((WINDOW, d)), _full((WINDOW, groups))],
        [_sds((t, n), ACT), _sds((1, d), F32), _sds((1, d), F32), _sds((WINDOW, d), F32), _sds((WINDOW, groups), F32)],
        (z, dy, lg, lb, ws, bs_t), ("arbitrary",), jobs=jobs)


AG_COPIES = 8


def _prepare(sources, dtypes, n_gather, name):
    n = len(sources)
    arrays, where = [], []
    for parts, layer in sources:
        found = []
        for part in parts:
            known = [i for i, v in enumerate(arrays) if v is part]
            if not known:
                arrays.append(part)
            found.append(known[0] if known else len(arrays) - 1)
        where.append((found, layer))
    shapes = [(sum(p.shape[1] for p in parts), parts[0].shape[2]) for parts, _ in sources]

    def body(*refs):
        ins, refs = refs[:len(arrays)], refs[len(arrays):]
        stage, outs = refs[:n], refs[n:n + n_gather]
        send, recv, local_sem = refs[n + n_gather:]
        x, y, c = _place()
        me, sibling, beside_x, beside_y, far = (x, y, c), (x, y, 1 - c), (1 - x, y, c), (x, 1 - y, c), (1 - x, 1 - y, c)
        slot = lambda dev: 4 * dev[0] + 2 * dev[1] + dev[2]
        other = lambda dev: (dev[0], dev[1], 1 - c)
        whole = lambda a: (0, shapes[a][0])
        cuts = [rows // 2 if rows % 32 == 0 else rows for rows, _ in shapes]
        first = lambda a: (0, cuts[a])
        rest = lambda a: (cuts[a], shapes[a][0] - cuts[a])

        def copy(a, k, block, rows, to, src=None):
            dst = outs[a].at[block, pl.ds(*rows)]
            return pltpu.make_async_remote_copy(
                src_ref=dst if src is None else src, dst_ref=dst, send_sem=send.at[a * AG_COPIES + k],
                recv_sem=recv.at[a * AG_COPIES + k], device_id=to, device_id_type=MESH)

        def cast(a):
            found, layer = where[a]
            at = 0
            for i in found:
                rows = arrays[i].shape[1]
                stage[a][at:at + rows, :] = ins[i][layer].astype(dtypes[a])
                at += rows

        for a in range(n_gather):
            cast(a)
        started = []
        for a in range(n_gather):
            own = pltpu.make_async_copy(stage[a], outs[a].at[slot(me)], local_sem.at[a])
            own.start()
            started.append(own)
        sends = []
        for a in range(n_gather):
            sends += [copy(a, k, slot(me), whole(a), to, src=stage[a]) for k, to in enumerate((sibling, beside_x, beside_y))]
        for cp in sends:
            cp.start()
        for a in range(n_gather, n):
            cast(a)
        for a in range(n_gather):
            copy(a, 1, slot(beside_x), whole(a), me).wait_recv()
            copy(a, 2, slot(beside_y), whole(a), me).wait_recv()
            passed = [copy(a, 3, slot(beside_x), first(a), beside_y), copy(a, 5, slot(beside_x), whole(a), sibling),
                      copy(a, 6, slot(beside_y), whole(a), sibling)]
            if rest(a)[1]:
                passed.append(copy(a, 4, slot(beside_y), rest(a), beside_x))
            for cp in passed:
                cp.start()
            sends += passed
        for a in range(n_gather):
            copy(a, 3, slot(far), first(a), me).wait_recv()
            if rest(a)[1]:
                copy(a, 4, slot(far), rest(a), me).wait_recv()
            passed = copy(a, 7, slot(far), whole(a), sibling)
            passed.start()
            sends.append(passed)
        for a in range(n_gather):
            for k, source in ((0, me), (5, beside_x), (6, beside_y), (7, far)):
                copy(a, k, slot(other(source)), whole(a), me).wait_recv()
        for cp in sends:
            cp.wait_send()
        for own in started:
            own.wait()

    res = pl.pallas_call(
        body, name=name,
        in_specs=[IN_VMEM] * len(arrays), out_specs=[IN_VMEM] * n + [ANY] * n_gather,
        out_shape=[_sds(s, dt) for s, dt in zip(shapes, dtypes)]
        + [_sds((N_DEV,) + s, dt) for s, dt in zip(shapes[:n_gather], dtypes)],
        scratch_shapes=[pltpu.SemaphoreType.DMA((n_gather * AG_COPIES,)), pltpu.SemaphoreType.DMA((n_gather * AG_COPIES,)),
                        pltpu.SemaphoreType.DMA((n_gather,))],
        compiler_params=pltpu.CompilerParams(vmem_limit_bytes=VMEM_LIMIT_BYTES),
    )(*arrays)
    return list(res[:n]), list(res[n:])


def _pair_sum(g, r, core, name):
    _, _, rows, cols = g.shape
    tr = _pick(rows, (512, 256, 128))

    def body(core_ref, g_ref, r_ref, p_ref):
        del core_ref
        p_ref[...] = (g_ref[...].astype(F32) + r_ref[...].astype(F32)).astype(p_ref.dtype)

    return pl.pallas_call(
        body, name=name,
        grid_spec=pltpu.PrefetchScalarGridSpec(
            num_scalar_prefetch=1, grid=(4, rows // tr),
            in_specs=[pl.BlockSpec((None, None, tr, cols), lambda q, i, core_ref: (q, core_ref[0], i, 0)),
                      pl.BlockSpec((None, tr, cols), lambda q, i, core_ref: (q, i, 0))],
            out_specs=pl.BlockSpec((None, tr, cols), lambda q, i, core_ref: (q, i, 0))),
        out_shape=_sds((4, rows, cols), g.dtype),
        compiler_params=_params("parallel", "parallel"),
    )(core, g, r)


def _adam(w, g, m, v):
    m2 = ADAM_B1 * m + (1.0 - ADAM_B1) * g
    v2 = ADAM_B2 * v + (1.0 - ADAM_B2) * (g * g)
    m_hat = m2 / (1.0 - ADAM_B1 ** ADAM_STEP)
    v_hat = v2 / (1.0 - ADAM_B2 ** ADAM_STEP)
    return -ADAM_LR * (m_hat / (jnp.sqrt(v_hat) + ADAM_EPS) + ADAM_WD * w), m2, v2


def _shard_update(own, index, received, w, m, v, layer, so_far, name):
    _, rows, cols = w.shape
    n_recv = received.shape[0]
    tr = _pick(rows, (512, 256, 128))

    def body(index_ref, p_ref, q_ref, w_ref, m_ref, v_ref, *rest):
        del index_ref
        g_out, d_out, m_out, v_out = rest[-4:]
        g = p_ref[...].astype(F32)
        for s in range(n_recv):
            g = g + q_ref[s].astype(F32)
        g_out[...] = g
        d_out[...], m_out[...], v_out[...] = _adam(w_ref[...], g, m_ref[...], v_ref[...])

    tile = pl.BlockSpec((None, tr, cols), lambda i, index_ref: (layer, i, 0))
    kept = list(so_far) if so_far is not None else []
    return pl.pallas_call(
        body, name=name,
        grid_spec=pltpu.PrefetchScalarGridSpec(
            num_scalar_prefetch=1, grid=(rows // tr,),
            in_specs=[pl.BlockSpec((None, tr, cols), lambda i, index_ref: (index_ref[0], i, 0)),
                      pl.BlockSpec((n_recv, tr, cols), lambda i, index_ref: (0, i, 0)), tile, tile, tile] + [ANY] * len(kept),
            out_specs=[tile] * 4),
        out_shape=[_sds(w.shape, F32)] * 4,
        input_output_aliases={6 + i: i for i in range(len(kept))},
        compiler_params=_params("parallel"),
    )(index, own, received, w, m, v, *kept)


def _natural_pieces(shape, r, c):
    if tuple(shape[-2:]) == (r, c) and all(n == 1 for n in shape[:-2]):
        return [((0,) * (len(shape) - 2), (0, c))]
    groups, width = shape[1], shape[3]
    assert len(shape) == 4 and shape[0] == 1 and shape[2] == r and groups * width == c, (shape, r, c)
    return [((0, g), (g * width, width)) for g in range(groups)]


def _replicated_update(shares, where, params, name):
    flat = [a for p in params if p is not None for a in p]

    def body(s_ref, *refs):
        ins, outs = refs[:len(flat)], refs[len(flat):]
        total = s_ref[0]
        for dev in range(1, N_DEV):
            total = total + s_ref[dev]
        i_at = o_at = 0
        for ranges, p in zip(where, params):
            chunks = [total[r0:r0 + r, :c] for r0, r, c in ranges]
            g2d = chunks[0] if len(chunks) == 1 else jnp.concatenate(chunks, axis=1)
            if p is None:
                outs[o_at][...] = g2d
                o_at += 1
                continue
            w_ref, m_ref, v_ref = ins[i_at:i_at + 3]
            for idx, (c0, cols) in _natural_pieces(p[0].shape, *g2d.shape):
                at = idx + (slice(None), slice(None))
                g = g2d[:, c0:c0 + cols]
                outs[o_at][at] = g
                outs[o_at + 1][at], outs[o_at + 2][at], outs[o_at + 3][at] = _adam(w_ref[at], g, m_ref[at], v_ref[at])
            i_at, o_at = i_at + 3, o_at + 4

    out_shape = []
    for ranges, p in zip(where, params):
        out_shape += [_sds((ranges[0][1], sum(c for _, _, c in ranges)), F32)] if p is None else [_sds(p[0].shape, F32)] * 4
    res = pl.pallas_call(
        body, name=name, out_shape=out_shape, compiler_params=pltpu.CompilerParams(vmem_limit_bytes=VMEM_LIMIT_BYTES),
    )(shares, *flat)
    out, at = [], 0
    for p in params:
        out.append(list(res[at:at + (1 if p is None else 4)]))
        at += 1 if p is None else 4
    return out


def _rope_tables(t):
    half = HEAD_DIM // 2
    inv_freq = ROPE_THETA ** (-(jnp.arange(half, dtype=F32) * 2.0) / HEAD_DIM)
    ang = jnp.arange(t, dtype=jnp.int32).astype(F32)[:, None] * inv_freq[None, :]
    cos, sin = jnp.cos(ang), jnp.sin(ang)
    return jnp.tile(jnp.concatenate([cos, cos], axis=1), (1, 2)), jnp.tile(jnp.concatenate([-sin, sin], axis=1), (1, 2))


def _dup_heads(w, d):
    kv = (w.shape[-1] - d) // 2
    lead = w.shape[:-1]

    def dup(part):
        heads = part.reshape(lead + (kv // HEAD_DIM, 1, HEAD_DIM))
        return jnp.broadcast_to(heads, lead + (kv // HEAD_DIM, 2, HEAD_DIM)).reshape(lead + (2 * kv,))

    return jnp.concatenate([w[..., :d], dup(w[..., d:d + kv]), dup(w[..., d + kv:])], axis=-1)


def _fold_heads(dw, d):
    kv = (dw.shape[-1] - d) // 4
    lead = dw.shape[:-1]

    def fold(part):
        return part.reshape(lead + (kv // HEAD_DIM, 2, HEAD_DIM)).sum(axis=-2).reshape(lead + (kv,))

    return jnp.concatenate([dw[..., :d], fold(dw[..., d:d + 2 * kv]), fold(dw[..., d + 2 * kv:])], axis=-1)


def _columns_full(gathered):
    _, rows, c = gathered.shape
    return jnp.transpose(gathered, (1, 0, 2)).reshape(rows, N_DEV * c)


def _rows_full(gathered):
    return gathered.reshape(-1, gathered.shape[-1])


def _columns_blocked(full):
    rows, cols = full.shape
    return jnp.transpose(full.reshape(rows, N_DEV, cols // N_DEV), (1, 0, 2)).astype(ACT)


def _rows_blocked(full):
    return full.reshape(N_DEV, full.shape[0] // N_DEV, full.shape[1]).astype(ACT)


def _pack_rows(parts, width):
    rows, where, at = [], [], 0
    for v in parts:
        r, c = v.shape
        n_rows = -(-r // 8) * 8
        chunks = []
        for c0 in range(0, c, width):
            chunk = v[:, c0:c0 + width].astype(F32)
            rows.append(jnp.pad(chunk, ((0, n_rows - r), (0, width - chunk.shape[1]))))
            chunks.append((at, r, chunk.shape[1]))
            at += n_rows
        where.append(chunks)
    return jnp.concatenate(rows, axis=0), where


def _halves(block):
    half = block.shape[0] // 2
    return (0, half), (half, half)


def _whole(block):
    return (0, block.shape[0])


EARLY = ("attn_w_qkv", "sgu_ln", "attn_w_o")
LATE = ("sgu_w_in", "sgu_w_out", "gu0", "gu1", "dn0", "dn1")
COLUMN_SHARDED = ("attn_w_qkv", "sgu_w_in", "gu0", "gu1")
BEFORE_ATTN = ("norm_mix_post", "norm_ffn_pre", "norm_ffn_post", "attn_b_o", "sgu_w_spatial", "sgu_b_spatial")
AFTER_ATTN = ("attn_b_qkv", "attn_sinks")
LAST = ("norm_mix_pre",)
WEIGHTS = ("norm_mix_pre", "norm_mix_post", "norm_ffn_pre", "norm_ffn_post", "attn_w_qkv", "attn_b_qkv", "attn_sinks", "attn_w_o",
           "attn_b_o", "sgu_w_in", "sgu_ln_g", "sgu_ln_b", "sgu_w_spatial", "sgu_b_spatial", "sgu_w_out", "ffn_w_gate_up", "ffn_w_down")


LAYER_OF = {"gu0": ("ffn_w_gate_up", 0), "gu1": ("ffn_w_gate_up", 1), "dn0": ("ffn_w_down", 0), "dn1": ("ffn_w_down", 1)}


def _source(tree, name):
    if name == "sgu_ln":
        return [tree["sgu_ln_g"][None], tree["sgu_ln_b"][None]], 0
    leaf, layer = LAYER_OF.get(name, (name, 0))
    return [tree[leaf]], layer


def kernel(x, norm_mix_pre, norm_mix_post, norm_ffn_pre, norm_ffn_post, attn_w_qkv, attn_b_qkv, attn_sinks, attn_w_o, attn_b_o, sgu_w_in, sgu_ln_g, sgu_ln_b, sgu_w_spatial, sgu_b_spatial, sgu_w_out, ffn_w_gate_up, ffn_w_down, loss_target, m_norm_mix_pre, m_norm_mix_post, m_norm_ffn_pre, m_norm_ffn_post, m_attn_w_qkv, m_attn_b_qkv, m_attn_sinks, m_attn_w_o, m_attn_b_o, m_sgu_w_in, m_sgu_ln_g, m_sgu_ln_b, m_sgu_w_spatial, m_sgu_b_spatial, m_sgu_w_out, m_ffn_w_gate_up, m_ffn_w_down, v_norm_mix_pre, v_norm_mix_post, v_norm_ffn_pre, v_norm_ffn_post, v_attn_w_qkv, v_attn_b_qkv, v_attn_sinks, v_attn_w_o, v_attn_b_o, v_sgu_w_in, v_sgu_ln_g, v_sgu_ln_b, v_sgu_w_spatial, v_sgu_b_spatial, v_sgu_w_out, v_ffn_w_gate_up, v_ffn_w_down):
    w = dict(norm_mix_pre=norm_mix_pre, norm_mix_post=norm_mix_post, norm_ffn_pre=norm_ffn_pre, norm_ffn_post=norm_ffn_post,
             attn_w_qkv=attn_w_qkv, attn_b_qkv=attn_b_qkv, attn_sinks=attn_sinks, attn_w_o=attn_w_o, attn_b_o=attn_b_o,
             sgu_w_in=sgu_w_in, sgu_ln_g=sgu_ln_g, sgu_ln_b=sgu_ln_b, sgu_w_spatial=sgu_w_spatial, sgu_b_spatial=sgu_b_spatial,
             sgu_w_out=sgu_w_out, ffn_w_gate_up=ffn_w_gate_up, ffn_w_down=ffn_w_down)
    mom = dict(norm_mix_pre=m_norm_mix_pre, norm_mix_post=m_norm_mix_post, norm_ffn_pre=m_norm_ffn_pre, norm_ffn_post=m_norm_ffn_post,
               attn_w_qkv=m_attn_w_qkv, attn_b_qkv=m_attn_b_qkv, attn_sinks=m_attn_sinks, attn_w_o=m_attn_w_o, attn_b_o=m_attn_b_o,
               sgu_w_in=m_sgu_w_in, sgu_ln_g=m_sgu_ln_g, sgu_ln_b=m_sgu_ln_b, sgu_w_spatial=m_sgu_w_spatial,
               sgu_b_spatial=m_sgu_b_spatial, sgu_w_out=m_sgu_w_out, ffn_w_gate_up=m_ffn_w_gate_up, ffn_w_down=m_ffn_w_down)
    var = dict(norm_mix_pre=v_norm_mix_pre, norm_mix_post=v_norm_mix_post, norm_ffn_pre=v_norm_ffn_pre, norm_ffn_post=v_norm_ffn_post,
               attn_w_qkv=v_attn_w_qkv, attn_b_qkv=v_attn_b_qkv, attn_sinks=v_attn_sinks, attn_w_o=v_attn_w_o, attn_b_o=v_attn_b_o,
               sgu_w_in=v_sgu_w_in, sgu_ln_g=v_sgu_ln_g, sgu_ln_b=v_sgu_ln_b, sgu_w_spatial=v_sgu_w_spatial,
               sgu_b_spatial=v_sgu_b_spatial, sgu_w_out=v_sgu_w_out, ffn_w_gate_up=v_ffn_w_gate_up, ffn_w_down=v_ffn_w_down)
    w, mom, var = [{**tree, "ffn_w_gate_up": jnp.swapaxes(tree["ffn_w_gate_up"], 1, 2)} for tree in (w, mom, var)]
    xs, target = x[0], loss_target[0]
    t, d = xs.shape
    row = lambda v: v.reshape(1, -1).astype(F32)
    core = lax.axis_index("c").astype(jnp.int32).reshape(1)
    chip = (2 * lax.axis_index("x") + lax.axis_index("y")).astype(jnp.int32).reshape(1)
    me = (4 * lax.axis_index("x") + 2 * lax.axis_index("y") + lax.axis_index("c")).astype(jnp.int32).reshape(1)
    names = EARLY + LATE
    staged, early = _prepare([_source(w, n) for n in names], [F32 if n == "sgu_ln" else ACT for n in names], len(EARLY),
                             "weights_prepare")
    stage = dict(zip(names, staged))
    w_qkv = _dup_heads(_columns_full(early[0]), d)
    lg, lb = row(early[1][:, 0, :]), row(early[1][:, 1, :])
    w_o = _rows_full(early[2])
    b_qkv = _dup_heads(row(attn_b_qkv), d)
    sinks = attn_sinks.reshape(1, -1).astype(F32)
    ws = sgu_w_spatial[0].astype(F32)
    bs_t = sgu_b_spatial[0].astype(F32).T
    cos, sin = _rope_tables(t)
    gather = lambda name, so_far, **pieces: _gather_job(stage[name], so_far, **pieces)
    a_half = lambda name: _halves(stage[name])[0]
    b_half = lambda name: _halves(stage[name])[1]
    whole = lambda name: _whole(stage[name])

    def pieces(name, n):
        rows = stage[name].shape[0] // n
        return [(i * rows, rows) for i in range(n)]

    ways = lambda parts: [(piece, i % 2) for i, piece in enumerate(parts)]
    relay = lambda name, so_far, **steps: _relay_gather_job(stage[name], so_far, **steps)
    gu0_parts, gu1_parts = pieces("gu0", 4), pieces("gu1", 4)
    dn0_parts, dn1_parts, in_parts, out_parts = pieces("dn0", 2), pieces("dn1", 2), pieces("sgu_w_in", 2), pieces("sgu_w_out", 2)
    (h0, q, kdup, vdup), ((g_gu0,),) = _attn_qkv_fwd(
        xs, row(norm_mix_pre[0]), w_qkv, b_qkv, cos, sin, jobs=[relay("gu0", None, sends=gu0_parts)])
    (o,), ((g_gu0,), (g_dn0,)) = _attn_fwd(q, kdup, vdup, sinks, jobs=[
        relay("gu0", g_gu0, relays=ways(gu0_parts), forwards=gu0_parts), relay("dn0", None, sends=dn0_parts)])
    (m0, x1), ((g_gu0,), (g_dn0,), (g_in,)) = _proj_res(o, w_o, row(attn_b_o), row(norm_mix_post[0]), xs, "attn_out_fwd", jobs=[
        relay("gu0", g_gu0, far=gu0_parts), relay("dn0", g_dn0, relays=ways(dn0_parts), forwards=dn0_parts),
        relay("sgu_w_in", None, sends=in_parts)])
    w_gu0 = g_gu0
    (h1, gu0, a0), ((g_dn0,), (g_in,), (g_out,), (g_gu1,)) = _ffn_up_fwd(x1, row(norm_ffn_pre[0]), w_gu0, jobs=[
        relay("dn0", g_dn0, far=dn0_parts), relay("sgu_w_in", g_in, relays=ways(in_parts), forwards=in_parts),
        relay("sgu_w_out", None, sends=out_parts), relay("gu1", None, sends=gu1_parts[:2])])
    w_dn0 = _rows_full(g_dn0)
    (f0, x2), ((g_in,), (g_out,), (g_gu1,)) = _proj_res(a0, w_dn0, None, row(norm_ffn_post[0]), x1, "ffn_down_fwd0", jobs=[
        relay("sgu_w_in", g_in, far=in_parts), relay("sgu_w_out", g_out, relays=ways(out_parts), forwards=out_parts),
        relay("gu1", g_gu1, sends=gu1_parts[2:], relays=ways(gu1_parts[:2]), forwards=gu1_parts[:2])])
    w_in = g_in
    (h2, z, y), ((g_out,), (g_gu1,), (g_dn1,)) = _sgu_fwd(x2, row(norm_mix_pre[1]), w_in, lg, lb, ws, bs_t, jobs=[
        relay("sgu_w_out", g_out, far=out_parts),
        relay("gu1", g_gu1, relays=ways(gu1_parts[2:]), forwards=gu1_parts[2:], far=gu1_parts[:2]),
        relay("dn1", None, sends=dn1_parts)])
    w_out = _rows_full(g_out)
    (m1, x3), ((g_gu1,), (g_dn1,)) = _proj_res(y, w_out, None, row(norm_mix_post[1]), x2, "sgu_out_fwd", jobs=[
        relay("gu1", g_gu1, far=gu1_parts[2:]), relay("dn1", g_dn1, relays=ways(dn1_parts), forwards=dn1_parts)])
    w_gu1 = g_gu1
    (h3, gu1, a1), ((g_dn1,),) = _ffn_up_fwd(x3, row(norm_ffn_pre[1]), w_gu1, jobs=[relay("dn1", g_dn1, far=dn1_parts)])
    w_dn1 = _rows_full(g_dn1)
    (f1, dx4, loss_tile), _ = _proj_res(a1, w_dn1, None, row(norm_ffn_post[1]), x3, "ffn_down_fwd1", target=target)

    to_sibling = lambda g: _scatter_job([g], 4, _to_sibling)
    pair = lambda g, r, name: _pair_sum(g.reshape((4, 2) + g.shape[1:]), r, core, "pair_sum_" + name)
    to_chips = lambda p, prev=None, piece=None: _scatter_job([p], 3, _to_owner_chip, prev=prev, piece=piece)
    out_g, out_d, out_m, out_v = {}, {}, {}, {}

    trees = [{**tree, "sgu_ln": jnp.stack([tree["sgu_ln_g"], tree["sgu_ln_b"]], axis=1)} for tree in (w, mom, var)]
    so_far = {}

    def update(name, own, index, received):
        leaf, layer = LAYER_OF.get(name, (name, 0))
        so_far[leaf] = _shard_update(own, index, received, *[tree[leaf] for tree in trees], layer, so_far.get(leaf), "update_" + name)
        for out, val in zip((out_g, out_d, out_m, out_v), so_far[leaf]):
            out[leaf] = val

    def finish(names, n_extra, shares, where, name):
        res = _replicated_update(shares, where, [(w[n], mom[n], var[n]) for n in names] + [None] * n_extra, name)
        for n, vals in zip(names, res):
            out_g[n], out_d[n], out_m[n], out_v[n] = vals
        return [vals[0] for vals in res[len(names):]]

    first_half = lambda p: _halves(p[0])[0]
    second_half = lambda p: _halves(p[0])[1]
    (df1, dgu1, dg_ffn_post1), _ = _post_bwd(dx4, f1, row(norm_ffn_post[1]), w_dn1, "ffn", "ffn_down_bwd1", gu1)
    b_gu1, _ = _wgrad(h3, dgu1, "ffn_up_wgrad1", ACT, transposed=True)
    dw_dn1, ((r_gu1,),) = _wgrad(a1, df1, "ffn_down_wgrad1", ACT, jobs=[to_sibling(b_gu1)])
    b_dn1 = _rows_blocked(dw_dn1)
    p_gu1 = pair(b_gu1, r_gu1, "gu1")
    (dx3, dg_ffn_pre1), ((q_gu1,), (r_dn1,)) = _pre_bwd(dgu1, w_gu1, x3, row(norm_ffn_pre[1]), dx4, "ffn_up_bwd1", True, jobs=[
        to_chips(p_gu1, piece=first_half(p_gu1)), to_sibling(b_dn1)])
    p_dn1 = pair(b_dn1, r_dn1, "dn1")
    (dm1, dy, dg_mix_post1), _ = _post_bwd(dx3, m1, row(norm_mix_post[1]), w_out, "sgu", "sgu_out_bwd")
    dw_out, _ = _wgrad(y, dm1, "sgu_out_wgrad", ACT)
    b_out = _rows_blocked(dw_out)
    (dz, dlg, dlb, dws, dbs_t), ((q_gu1,), (r_out,)) = _sgu_mix_bwd(z, dy, lg, lb, ws, bs_t, jobs=[
        to_chips(p_gu1, prev=[q_gu1], piece=second_half(p_gu1)), to_sibling(b_out)])
    update("gu1", p_gu1, chip, q_gu1)
    p_out = pair(b_out, r_out, "sgu_w_out")
    b_in, ((q_out,),) = _wgrad(h2, dz, "sgu_in_wgrad", ACT, out_block=stage["sgu_w_in"].shape[1], jobs=[to_chips(p_out)])
    update("sgu_w_out", p_out, chip, q_out)
    b_ln = jnp.stack([dlg.reshape(N_DEV, -1), dlb.reshape(N_DEV, -1)], axis=1)
    (dx2, dg_mix_pre1), ((r_in,), (r_ln,)) = _pre_bwd(dz, w_in, x2, row(norm_mix_pre[1]), dx3, "sgu_in_bwd",
                                                      jobs=[to_sibling(b_in), to_sibling(b_ln)])
    p_in, p_ln = pair(b_in, r_in, "sgu_w_in"), pair(b_ln, r_ln, "sgu_ln")
    (df0, dgu0, dg_ffn_post0), ((q_dn1,),) = _post_bwd(dx2, f0, row(norm_ffn_post[0]), w_dn0, "ffn", "ffn_down_bwd0", gu0,
                                                      jobs=[to_chips(p_dn1)])
    update("dn1", p_dn1, chip, q_dn1)
    b_gu0, ((q_in,), (q_ln,)) = _wgrad(h1, dgu0, "ffn_up_wgrad0", ACT, transposed=True, jobs=[to_chips(p_in), to_chips(p_ln)])
    update("sgu_w_in", p_in, chip, q_in)
    update("sgu_ln", p_ln, chip, q_ln)
    dw_dn0, ((r_gu0,),) = _wgrad(a0, df0, "ffn_down_wgrad0", ACT, jobs=[to_sibling(b_gu0)])
    b_dn0 = _rows_blocked(dw_dn0)
    p_gu0 = pair(b_gu0, r_gu0, "gu0")
    (dx1, dg_ffn_pre0), ((q_gu0,), (r_dn0,)) = _pre_bwd(dgu0, w_gu0, x1, row(norm_ffn_pre[0]), dx2, "ffn_up_bwd0", True, jobs=[
        to_chips(p_gu0, piece=first_half(p_gu0)), to_sibling(b_dn0)])
    p_dn0 = pair(b_dn0, r_dn0, "dn0")
    (dm0, do, dg_mix_post0, db_o), _ = _post_bwd(dx1, m0, row(norm_mix_post[0]), w_o, "attn", "attn_out_bwd")
    dw_o, _ = _wgrad(o, dm0, "attn_out_wgrad", ACT)
    b_o = _rows_blocked(dw_o)
    grads = {
        "norm_mix_post": jnp.concatenate([dg_mix_post0, dg_mix_post1], axis=0),
        "norm_ffn_pre": jnp.concatenate([dg_ffn_pre0, dg_ffn_pre1], axis=0),
        "norm_ffn_post": jnp.concatenate([dg_ffn_post0, dg_ffn_post1], axis=0),
        "attn_b_o": db_o,
        "sgu_w_spatial": dws,
        "sgu_b_spatial": dbs_t.T,
    }
    shares1, where1 = _pack_rows([grads[name] for name in BEFORE_ATTN], d)
    (dqkv, db_qkv, dsink), ((q_gu0,), (q_dn0,), (r_o,), (g_shares1,)) = _attn_bwd(q, kdup, vdup, do, sinks, cos, sin, jobs=[
        to_chips(p_gu0, prev=[q_gu0], piece=second_half(p_gu0)), to_chips(p_dn0), to_sibling(b_o),
        _gather_job(shares1, None, sends=[_whole(shares1)])])
    update("gu0", p_gu0, chip, q_gu0)
    update("dn0", p_dn0, chip, q_dn0)
    p_o = pair(b_o, r_o, "attn_w_o")
    grads.update({"attn_b_qkv": _fold_heads(db_qkv, d), "attn_sinks": dsink[:1, :d // HEAD_DIM]})
    shares2, where2 = _pack_rows([grads[name] for name in AFTER_ATTN] + [loss_tile[:1, :1]], d)
    blocked_half = lambda dw: _columns_blocked(_fold_heads(dw, d))
    dw_a, ((q_o,),) = _wgrad(h0, dqkv, "attn_qkv_wgrad_a", k_part=(0, 2), jobs=[to_chips(p_o)])
    update("attn_w_o", p_o, chip, q_o)
    b_qkv_a = blocked_half(dw_a)
    dw_b, ((q_qkv,), (g_shares1,), (g_shares2,)) = _wgrad(h0, dqkv, "attn_qkv_wgrad_b", k_part=(1, 2), jobs=[
        _scatter_job([b_qkv_a], N_DEV - 1, _to_owner, into=(0, d)), _gather_job(shares1, g_shares1, forwards=[_whole(shares1)]),
        _gather_job(shares2, None, sends=[_whole(shares2)])])
    b_qkv_b = blocked_half(dw_b)
    (dx0, dg_mix_pre0), ((q_qkv,), (g_shares2,)) = _pre_bwd(dqkv, w_qkv, xs, row(norm_mix_pre[0]), dx1, "attn_qkv_bwd", jobs=[
        _scatter_job([b_qkv_b], N_DEV - 1, _to_owner, prev=[q_qkv], into=(d // 2, d)),
        _gather_job(shares2, g_shares2, forwards=[_whole(shares2)])])
    b_qkv_grad = jnp.concatenate([b_qkv_a, b_qkv_b], axis=1)
    update("attn_w_qkv", b_qkv_grad, me, q_qkv)

    finish(BEFORE_ATTN, 0, g_shares1, where1, "replicated_update_before_attn")
    loss = finish(AFTER_ATTN, 1, g_shares2, where2, "replicated_update_after_attn")[0][0, 0]
    grads["norm_mix_pre"] = jnp.concatenate([dg_mix_pre0, dg_mix_pre1], axis=0)
    shares3, where3 = _pack_rows([grads[name] for name in LAST], d)
    _, (last_shares,) = _prepare([([shares3[None]], 0)], [F32], 1, "last_grads_all_gather")
    finish(LAST, 0, last_shares, where3, "replicated_update_last")

    for out in (out_g, out_d, out_m, out_v):
        out["sgu_ln_g"], out["sgu_ln_b"] = out["sgu_ln"][:, 0, :], out["sgu_ln"][:, 1, :]
        out["ffn_w_gate_up"] = jnp.swapaxes(out["ffn_w_gate_up"], 1, 2)

    return (loss, dx0[None], *[out_g[n] for n in WEIGHTS], *[out_d[n] for n in WEIGHTS],
            *[out_m[n] for n in WEIGHTS], *[out_v[n] for n in WEIGHTS])
```

```python
import functools
import math
import operator

import jax
import jax.numpy as jnp
from jax import lax
from jax.experimental import pallas as pl
from jax.experimental.pallas import tpu as pltpu

F32 = jnp.float32
ACT = jnp.bfloat16

EPS = 1e-6
HEAD_DIM = 64
PAIR = 2 * HEAD_DIM
GQA_GROUP = 4
WINDOW = 128
ROPE_THETA = 10000.0
SCALE = HEAD_DIM ** -0.5
MASKED = -1e30
LANES = 128

ADAM_LR, ADAM_B1, ADAM_B2, ADAM_EPS, ADAM_WD, ADAM_STEP = 0.001, 0.9, 0.999, 1e-08, 0.01, 10

N_DEV = 8
VMEM_LIMIT_BYTES = 56 * 1024 * 1024
MESH = pl.DeviceIdType.MESH
ANY = pl.BlockSpec(memory_space=pl.ANY)
IN_VMEM = pl.BlockSpec(memory_space=pltpu.VMEM)


def _pick(n, candidates):
    for c in candidates:
        if n % c == 0:
            return c
    return n


def _row_tile(t):
    return _pick(t, (512,))


def _params(*sem):
    return pltpu.CompilerParams(dimension_semantics=sem, vmem_limit_bytes=VMEM_LIMIT_BYTES)


def _full(shape):
    return pl.BlockSpec(shape, lambda *_: (0,) * len(shape))


def _rows(tm, n):
    return pl.BlockSpec((tm, n), lambda i: (i, 0))


def _sds(shape, dtype):
    return jax.ShapeDtypeStruct(shape, dtype)


def _place():
    return lax.axis_index("x"), lax.axis_index("y"), lax.axis_index("c")


def _other_chips(x, y):
    return [(1 - x, y), (x, 1 - y), (1 - x, 1 - y)]


class _Job:
    def __init__(self, inputs, out_shape, aliases, emit, n_remote, n_local=0):
        self.inputs, self.out_shape, self.aliases, self.emit = list(inputs), list(out_shape), dict(aliases), emit
        self.n_remote, self.n_local = n_remote, n_local


def _call(body, name, grid, in_specs, out_specs, out_shape, args, sem, scratch=(), jobs=()):
    n_in, n_out, n_scr = len(args), len(out_shape), len(scratch)
    j_in = [a for job in jobs for a in job.inputs]
    j_out = [s for job in jobs for s in job.out_shape]
    aliases, at_in, at_out = {}, n_in, n_out
    for job in jobs:
        aliases.update({at_in + i: at_out + o for i, o in job.aliases.items()})
        at_in, at_out = at_in + len(job.inputs), at_out + len(job.out_shape)
    n_remote = sum(job.n_remote for job in jobs)
    n_local = sum(job.n_local for job in jobs)

    def wrapped(*refs):
        ins, jin = refs[:n_in], refs[n_in:n_in + len(j_in)]
        rest = refs[n_in + len(j_in):]
        outs, jout = rest[:n_out], rest[n_out:n_out + len(j_out)]
        scr = rest[n_out + len(j_out):n_out + len(j_out) + n_scr]
        if not jobs:
            body(*ins, *outs, *scr)
            return
        send, recv, local = rest[n_out + len(j_out) + n_scr:]
        ids = [pl.program_id(a) for a in range(len(grid))]
        first = functools.reduce(operator.and_, [i == 0 for i in ids])
        last = functools.reduce(operator.and_, [i == g - 1 for i, g in zip(ids, grid)])

        def descriptors():
            place, found, i, o, r, l = _place(), [], 0, 0, 0, 0
            for job in jobs:
                for src, dst, to in job.emit(jin[i:i + len(job.inputs)], jout[o:o + len(job.out_shape)], place):
                    if to is None:
                        found.append(pltpu.make_async_copy(src, dst, local.at[l]))
                        l += 1
                    else:
                        found.append(pltpu.make_async_remote_copy(src_ref=src, dst_ref=dst, send_sem=send.at[r], recv_sem=recv.at[r],
                                                                  device_id=to, device_id_type=MESH))
                        r += 1
                i, o = i + len(job.inputs), o + len(job.out_shape)
            return found

        @pl.when(first)
        def _():
            for cp in descriptors():
                cp.start()

        body(*ins, *outs, *scr)

        @pl.when(last)
        def _():
            for cp in descriptors():
                cp.wait()

    sems = [pltpu.SemaphoreType.DMA((n_remote,)), pltpu.SemaphoreType.DMA((n_remote,)),
            pltpu.SemaphoreType.DMA((max(n_local, 1),))] if jobs else []
    res = pl.pallas_call(
        wrapped, name=name, grid=grid,
        in_specs=list(in_specs) + [ANY] * len(j_in), out_specs=list(out_specs) + [ANY] * len(j_out),
        out_shape=list(out_shape) + j_out, scratch_shapes=list(scratch) + sems, input_output_aliases=aliases,
        compiler_params=_params(*(("arbitrary",) * len(grid) if jobs else sem)),
    )(*args, *j_in)
    job_res, at = [], n_out
    for job in jobs:
        job_res.append(list(res[at:at + len(job.out_shape)]))
        at += len(job.out_shape)
    return list(res[:n_out]), job_res


def _copies_only(jobs, name):
    def body(o_ref):
        o_ref[...] = jnp.zeros_like(o_ref)

    return _call(body, name, (1,), [], [_full((8, LANES))], [_sds((8, LANES), F32)], (), ("arbitrary",), jobs=jobs)[1]


def _gather_job(stage, gathered, sends=(), forwards=()):
    shape = _sds((N_DEV,) + stage.shape, stage.dtype)
    inputs = ([stage] if sends else []) + ([gathered] if gathered is not None else [])
    aliases = {len(inputs) - 1: 0} if gathered is not None else {}

    def emit(ins, outs, place):
        x, y, c = place
        sibling, chips, mine, g = (x, y, 1 - c), _other_chips(x, y), 4 * x + 2 * y + c, outs[0]
        found = []
        for r0, nr in sends:
            src, dst = ins[0].at[pl.ds(r0, nr)], g.at[mine, pl.ds(r0, nr)]
            found += [(src, dst, None), (src, dst, sibling)] + [(src, dst, (px, py, c)) for px, py in chips]
        for r0, nr in forwards:
            for px, py in chips:
                block = 4 * px + 2 * py + c
                found.append((ins[-1].at[block, pl.ds(r0, nr)], g.at[block, pl.ds(r0, nr)], sibling))
        return found

    return _Job(inputs, [shape], aliases, emit, 4 * len(sends) + 3 * len(forwards), len(sends))


def _relay_gather_job(stage, gathered, sends=(), relays=(), forwards=(), far=()):
    shape = _sds((N_DEV,) + stage.shape, stage.dtype)
    inputs = ([stage] if sends else []) + ([gathered] if gathered is not None else [])
    aliases = {len(inputs) - 1: 0} if gathered is not None else {}

    def emit(ins, outs, place):
        x, y, c = place
        sibling, beside_x, beside_y, g = (x, y, 1 - c), (1 - x, y, c), (x, 1 - y, c), outs[0]
        slot = lambda dev: 4 * dev[0] + 2 * dev[1] + dev[2]
        found = []
        for r0, nr in sends:
            src, dst = ins[0].at[pl.ds(r0, nr)], g.at[slot((x, y, c)), pl.ds(r0, nr)]
            found += [(src, dst, None), (src, dst, sibling), (src, dst, beside_x), (src, dst, beside_y)]

        def passed_on(block, piece, to):
            return ins[-1].at[block, pl.ds(*piece)], g.at[block, pl.ds(*piece)], to

        for piece, way in relays:
            found.append(passed_on(slot(beside_x), piece, beside_y) if way == 0 else passed_on(slot(beside_y), piece, beside_x))
        for piece in forwards:
            found += [passed_on(slot(beside_x), piece, sibling), passed_on(slot(beside_y), piece, sibling)]
        for piece in far:
            found.append(passed_on(slot((1 - x, 1 - y, c)), piece, sibling))
        return found

    return _Job(inputs, [shape], aliases, emit, 3 * len(sends) + len(relays) + 2 * len(forwards) + len(far), len(sends))


def _scatter_job(arrays, n_slots, route, prev=None, piece=None):
    shapes = [_sds((n_slots,) + v.shape[1:], v.dtype) for v in arrays]
    inputs = list(arrays) + (list(prev) if prev else [])
    aliases = {len(arrays) + i: i for i in range(len(arrays))} if prev else {}

    def emit(ins, outs, place):
        found = []
        for a in range(len(arrays)):
            for s in range(n_slots):
                block, to = route(s, *place)
                if piece is None:
                    found.append((ins[a].at[block], outs[a].at[s], to))
                else:
                    found.append((ins[a].at[block, pl.ds(*piece)], outs[a].at[s, pl.ds(*piece)], to))
        return found

    return _Job(inputs, shapes, aliases, emit, len(arrays) * n_slots)


def _to_sibling(s, x, y, c):
    return 2 * s + (1 - c), (x, y, 1 - c)


def _to_owner_chip(s, x, y, c):
    px, py = _other_chips(x, y)[s]
    return 2 * px + py, (px, py, c)


def _rstd(x):
    return lax.rsqrt(jnp.mean(x * x, axis=-1, keepdims=True) + EPS)


def _rms_bwd(xhat, r, g, dy):
    dxh = dy * g
    return r * (dxh - xhat * jnp.mean(dxh * xhat, axis=-1, keepdims=True))


def _first_half(rows):
    lane = lax.broadcasted_iota(jnp.int32, (rows, PAIR), 1)
    return (lane % HEAD_DIM) < (HEAD_DIM // 2)


def _rot_half(v, first):
    return jnp.where(first, pltpu.roll(v, PAIR - HEAD_DIM // 2, 1), pltpu.roll(v, HEAD_DIM // 2, 1))


def _rope(v, cos, sin, first):
    return v * cos + _rot_half(v, first) * sin


def _rope_t(dv, cos, sin, first):
    return dv * cos + _rot_half(dv * sin, first)


def _dot(a, b):
    return jnp.dot(a, b, preferred_element_type=F32)


def _dot_nt(a, b):
    return lax.dot_general(a, b, (((1,), (1,)), ((), ())), preferred_element_type=F32)


def _dot_tn(a, b):
    return lax.dot_general(a, b, (((0,), (0,)), ((), ())), preferred_element_type=F32)


def _sigmoid(v):
    return 1.0 / (1.0 + jnp.exp(-v))


_GELU_K = math.sqrt(2.0 / math.pi)
_GELU_C = 0.044715


def _gelu(v):
    return v * (0.5 * (1.0 + jnp.tanh(_GELU_K * (v + _GELU_C * (v * v * v)))))


def _gelu_grad(v):
    t = jnp.tanh(_GELU_K * (v + _GELU_C * (v * v * v)))
    return 0.5 * (1.0 + t) + 0.5 * v * (1.0 - t * t) * (_GELU_K * (1.0 + 3.0 * _GELU_C * (v * v)))


def _attn_qkv_fwd(x, g, w, b, cos, sin, jobs=()):
    t, d = x.shape
    n = w.shape[1]
    kd = (n - d) // 2
    tm = _row_tile(t)
    ch = _pick(d, (512,))

    def body(x_ref, g_ref, w_ref, b_ref, cos_ref, sin_ref, h_ref, q_ref, k_ref, v_ref):
        xv = x_ref[...]
        hb = (xv * _rstd(xv) * g_ref[...]).astype(ACT)
        h_ref[...] = hb
        cosv, sinv = cos_ref[...], sin_ref[...]
        first = _first_half(tm)

        def proj(lo, width):
            return _dot(hb, w_ref[:, lo:lo + width]) + b_ref[:, lo:lo + width]

        for c in range(0, d, ch):
            y = proj(c, ch)
            for s in range(0, ch, PAIR):
                q_ref[:, c + s:c + s + PAIR] = (_rope(y[:, s:s + PAIR], cosv, sinv, first) * SCALE).astype(ACT)
        y = proj(d, kd)
        for s in range(0, kd, PAIR):
            k_ref[:, s:s + PAIR] = _rope(y[:, s:s + PAIR], cosv, sinv, first).astype(ACT)
        v_ref[...] = proj(d + kd, kd).astype(ACT)

    return _call(
        body, "attn_qkv_fwd", (t // tm,),
        [_rows(tm, d), _full((1, d)), _full((d, n)), _full((1, n)), _rows(tm, PAIR), _rows(tm, PAIR)],
        [_rows(tm, d), _rows(tm, d), _rows(tm, kd), _rows(tm, kd)],
        [_sds((t, d), ACT), _sds((t, d), ACT), _sds((t, kd), ACT), _sds((t, kd), ACT)],
        (x, g, w, b, cos, sin), ("parallel",), jobs=jobs)


STACK = GQA_GROUP * WINDOW


def _band_mask(has_prev):
    row = lax.broadcasted_iota(jnp.int32, (STACK, 2 * WINDOW), 0) % WINDOW
    col = lax.broadcasted_iota(jnp.int32, (STACK, 2 * WINDOW), 1)
    return ((col < WINDOW) & (col > row) & has_prev) | ((col >= WINDOW) & (col - WINDOW <= row))


def _lane_halves():
    lane = lax.broadcasted_iota(jnp.int32, (1, PAIR), 1)
    return lane < HEAD_DIM, lane >= HEAD_DIM


def _stack_heads(ref, h, halves):
    parts = []
    for p in range(2):
        pair = ref[:, (2 * h + p) * PAIR:(2 * h + p + 1) * PAIR]
        parts += [jnp.where(half, pair, jnp.zeros_like(pair)) for half in halves]
    return jnp.concatenate(parts, axis=0)


def _sink_column(sink_ref, h):
    row = lax.broadcasted_iota(jnp.int32, (STACK, 1), 0)
    col = jnp.full((STACK, 1), sink_ref[0, GQA_GROUP * h + GQA_GROUP - 1], F32)
    for j in range(GQA_GROUP - 2, -1, -1):
        col = jnp.where(row < (j + 1) * WINDOW, sink_ref[0, GQA_GROUP * h + j], col)
    return col


def _probs(scores, valid, sink):
    s = jnp.where(valid, scores, MASKED)
    m = jnp.maximum(jnp.max(s, axis=-1, keepdims=True), sink)
    p = jnp.exp(s - m)
    psink = jnp.exp(sink - m)
    inv = 1.0 / (jnp.sum(p, axis=-1, keepdims=True) + psink)
    return p * inv, psink * inv


def _attn_fwd(q, kd_, vd, sinks, jobs=()):
    t, d = q.shape
    kd = kd_.shape[1]
    cur = lambda n: (n, 0)
    prev = lambda n: (jnp.maximum(n - 1, 0), 0)

    def body(sink_ref, q_ref, kc_ref, kp_ref, vc_ref, vp_ref, o_ref):
        valid = _band_mask(pl.program_id(0) > 0)
        halves = _lane_halves()
        heads = range(kd // PAIR)
        hs = [slice(h * PAIR, (h + 1) * PAIR) for h in heads]
        scores = [_dot_nt(_stack_heads(q_ref, h, halves), jnp.concatenate([kp_ref[:, hs[h]], kc_ref[:, hs[h]]], axis=0)) for h in heads]
        probs = [_probs(scores[h], valid, _sink_column(sink_ref, h))[0].astype(ACT) for h in heads]
        for h in heads:
            v2 = jnp.concatenate([vp_ref[:, hs[h]], vc_ref[:, hs[h]]], axis=0)
            vs = jnp.concatenate([jnp.where(half, v2, jnp.zeros_like(v2)) for half in halves], axis=0)
            pr = probs[h]
            for p in range(2):
                both = jnp.concatenate([pr[2 * p * WINDOW:(2 * p + 1) * WINDOW], pr[(2 * p + 1) * WINDOW:(2 * p + 2) * WINDOW]], axis=1)
                o_ref[:, (2 * h + p) * PAIR:(2 * h + p + 1) * PAIR] = _dot(both, vs).astype(ACT)

    kv_c, kv_p = pl.BlockSpec((WINDOW, kd), cur), pl.BlockSpec((WINDOW, kd), prev)
    return _call(
        body, "attn_fwd", (t // WINDOW,),
        [pl.BlockSpec(memory_space=pltpu.SMEM), pl.BlockSpec((WINDOW, d), cur), kv_c, kv_p, kv_c, kv_p],
        [pl.BlockSpec((WINDOW, d), cur)], [_sds((t, d), ACT)],
        (sinks, q, kd_, kd_, vd, vd), ("parallel",), jobs=jobs)


def _attn_bwd(q, kd_, vd, do, sinks, cos, sin, jobs=()):
    t, d = q.shape
    kd = kd_.shape[1]
    nb = t // WINDOW
    n_out = d + 2 * kd
    cur = lambda n: (jnp.minimum(n, nb - 1), 0)
    prev = lambda n: (jnp.maximum(jnp.minimum(n, nb - 1) - 1, 0), 0)
    late = lambda n: (jnp.maximum(n - 1, 0), 0)

    def body(sink_ref, q_ref, kc_ref, kp_ref, vc_ref, vp_ref, do_ref, cosc_ref, sinc_ref, cosp_ref, sinp_ref,
             out_ref, db_ref, dsink_ref, dq_keep, dk_keep, dv_keep):
        n = pl.program_id(0)

        @pl.when(n == 0)
        def _():
            for ref in (db_ref, dsink_ref, dq_keep, dk_keep, dv_keep):
                ref[...] = jnp.zeros_like(ref)

        valid = _band_mask(n > 0) & (n < nb)
        halves = _lane_halves()
        first = _first_half(WINDOW)
        slot = lax.broadcasted_iota(jnp.int32, dsink_ref.shape, 1)
        top = lax.broadcasted_iota(jnp.int32, dsink_ref.shape, 0) == 0
        dsink = jnp.zeros(dsink_ref.shape, F32)

        def emit(cols, done):
            out_ref[:, cols] = done.astype(ACT)
            db_ref[:, cols] += jnp.sum(done.astype(F32), axis=0, keepdims=True)

        heads = range(kd // PAIR)
        hs = [slice(h * PAIR, (h + 1) * PAIR) for h in heads]
        k2 = [jnp.concatenate([kp_ref[:, hs[h]], kc_ref[:, hs[h]]], axis=0) for h in heads]
        v2 = [jnp.concatenate([vp_ref[:, hs[h]], vc_ref[:, hs[h]]], axis=0) for h in heads]
        qs = [_stack_heads(q_ref, h, halves) for h in heads]
        dos = [_stack_heads(do_ref, h, halves) for h in heads]
        scores = [_dot_nt(qs[h], k2[h]) for h in heads]
        dps = [_dot_nt(dos[h], v2[h]) for h in heads]
        prs, dss = [], []
        for h in heads:
            pr, psink = _probs(scores[h], valid, _sink_column(sink_ref, h))
            delta = jnp.sum(pr * dps[h], axis=-1, keepdims=True)
            dss.append((pr * (dps[h] - delta)).astype(ACT))
            prs.append(pr.astype(ACT))
            leak = psink * delta
            for j in range(GQA_GROUP):
                share = jnp.sum(leak[j * WINDOW:(j + 1) * WINDOW], axis=0, keepdims=True)
                dsink = dsink - jnp.where(top & (slot == GQA_GROUP * h + j), share, 0.0)
        dqs = [_dot(dss[h], k2[h]) for h in heads]
        dk2 = [_dot_tn(dss[h], qs[h]) for h in heads]
        dv2 = [_dot_tn(prs[h], dos[h]) for h in heads]
        for h in heads:
            for p in range(2):
                ps = slice((2 * h + p) * PAIR, (2 * h + p + 1) * PAIR)
                dqp = jnp.where(halves[0], dqs[h][2 * p * WINDOW:(2 * p + 1) * WINDOW], dqs[h][(2 * p + 1) * WINDOW:(2 * p + 2) * WINDOW])
                emit(ps, dq_keep[:, ps])
                dq_keep[:, ps] = (_rope_t(dqp, cosc_ref[...], sinc_ref[...], first) * SCALE).astype(ACT)
            ks = slice(d + h * PAIR, d + (h + 1) * PAIR)
            vs = slice(d + kd + h * PAIR, d + kd + (h + 1) * PAIR)
            emit(ks, dk_keep[:, hs[h]] + _rope_t(dk2[h][:WINDOW], cosp_ref[...], sinp_ref[...], first))
            dk_keep[:, hs[h]] = _rope_t(dk2[h][WINDOW:], cosc_ref[...], sinc_ref[...], first)
            emit(vs, dv_keep[:, hs[h]] + dv2[h][:WINDOW])
            dv_keep[:, hs[h]] = dv2[h][WINDOW:]
        dsink_ref[...] += dsink

    kv_c, kv_p = pl.BlockSpec((WINDOW, kd), cur), pl.BlockSpec((WINDOW, kd), prev)
    tab_c, tab_p = pl.BlockSpec((WINDOW, PAIR), cur), pl.BlockSpec((WINDOW, PAIR), prev)
    return _call(
        body, "attn_bwd", (nb + 1,),
        [pl.BlockSpec(memory_space=pltpu.SMEM), pl.BlockSpec((WINDOW, d), cur), kv_c, kv_p, kv_c, kv_p,
         pl.BlockSpec((WINDOW, d), cur), tab_c, tab_c, tab_p, tab_p],
        [pl.BlockSpec((WINDOW, n_out), late), _full((1, n_out)), _full((8, LANES))],
        [_sds((t, n_out), ACT), _sds((1, n_out), F32), _sds((8, LANES), F32)],
        (sinks, q, kd_, kd_, vd, vd, do, cos, sin, cos, sin), ("arbitrary",),
        scratch=[pltpu.VMEM((WINDOW, d), ACT), pltpu.VMEM((WINDOW, kd), F32), pltpu.VMEM((WINDOW, kd), F32)], jobs=jobs)


def _proj_res(a, w, b, g, x, name, target=None, jobs=()):
    blocked = a.ndim == 3
    t = a.shape[-2]
    k, d = w.shape
    tm = _row_tile(t)
    has_bias = b is not None

    def body(*refs):
        a_ref, w_ref = refs[:2]
        b_ref = refs[2] if has_bias else None
        g_ref, x_ref, m_ref, xo_ref = refs[2 + has_bias:]
        if blocked:
            c = a.shape[2]
            m = _dot(a_ref[0], w_ref[:c, :])
            for j in range(1, a.shape[0]):
                m = m + _dot(a_ref[j], w_ref[j * c:(j + 1) * c, :])
        else:
            m = _dot(a_ref[...], w_ref[...])
        if has_bias:
            m = m + b_ref[...]
        m_ref[...] = m.astype(ACT)
        xo_ref[...] = x_ref[...] + (m * _rstd(m)) * g_ref[...]

    def body_with_loss(a_ref, w_ref, g_ref, x_ref, t_ref, m_ref, dy_ref, loss_ref):
        @pl.when(pl.program_id(0) == 0)
        def _():
            loss_ref[...] = jnp.zeros_like(loss_ref)

        body(a_ref, w_ref, g_ref, x_ref, m_ref, dy_ref)
        err = dy_ref[...] - t_ref[...]
        dy_ref[...] = err * (1.0 / d)
        loss_ref[...] += 0.5 * jnp.sum(jnp.mean(err * err, axis=-1, keepdims=True), axis=0, keepdims=True)

    ins = [a, w] + ([b] if has_bias else []) + [g, x]
    a_spec = pl.BlockSpec((a.shape[0], tm, a.shape[2]), lambda i: (0, i, 0)) if blocked else _rows(tm, k)
    specs = [a_spec, _full((k, d))] + ([_full((1, d))] if has_bias else []) + [_full((1, d)), _rows(tm, d)]
    if target is not None:
        return _call(body_with_loss, name, (t // tm,), specs + [_rows(tm, d)], [_rows(tm, d), _rows(tm, d), _full((8, LANES))],
                     [_sds((t, d), ACT), _sds((t, d), F32), _sds((8, LANES), F32)], ins + [target], ("arbitrary",), jobs=jobs)
    return _call(body, name, (t // tm,), specs, [_rows(tm, d), _rows(tm, d)], [_sds((t, d), ACT), _sds((t, d), F32)], ins,
                 ("parallel",), jobs=jobs)


def _post_bwd(dres, m, g, w, mode, name, gu=None, jobs=()):
    t, d = dres.shape
    k = w.shape[0]
    tm = _row_tile(t)
    kc = _pick(k, (1408, 1024, 512))

    def body(*refs):
        d_ref, m_ref, g_ref, w_ref = refs[:4]
        gu_ref = refs[4] if mode == "ffn" else None
        outs = refs[4 + (mode == "ffn"):]
        dm_ref, da_ref, dg_ref = outs[:3]
        i = pl.program_id(0)

        @pl.when(i == 0)
        def _():
            dg_ref[...] = jnp.zeros_like(dg_ref)
            if mode == "attn":
                outs[3][...] = jnp.zeros_like(outs[3])

        dv, mv = d_ref[...], m_ref[...].astype(F32)
        r = _rstd(mv)
        mh = mv * r
        dg_ref[...] += jnp.sum(dv * mh, axis=0, keepdims=True)
        dm = _rms_bwd(mh, r, g_ref[...], dv)
        dmb = dm.astype(ACT)
        dm_ref[...] = dmb
        if mode == "attn":
            outs[3][...] += jnp.sum(dm, axis=0, keepdims=True)
        if mode == "ffn":
            nb, _, cb = gu.shape
            for j in range(nb // 2):
                da = _dot_nt(dmb, w_ref[j * cb:(j + 1) * cb, :]).astype(ACT)
                gate, up = gu_ref[j], gu_ref[nb // 2 + j]
                sg = _sigmoid(gate.astype(F32)).astype(ACT)
                gs = gate * sg
                da_ref[j] = da * up * (sg + gs - gs * sg)
                da_ref[nb // 2 + j] = da * gs
        else:
            for c in range(0, k, kc):
                da = _dot_nt(dmb, w_ref[c:c + kc, :])
                da_ref[:, c:c + kc] = da.astype(ACT)

    ins = [dres, m, g, w]
    specs = [_rows(tm, d), _rows(tm, d), _full((1, d)), _full((k, d))]
    if mode == "ffn":
        slab = pl.BlockSpec((gu.shape[0], tm, gu.shape[2]), lambda i: (0, i, 0))
        ins.append(gu)
        specs.append(slab)
        out_specs = [_rows(tm, d), slab, _full((1, d))]
        out_shape = [_sds((t, d), ACT), _sds(gu.shape, ACT), _sds((1, d), F32)]
    else:
        out_specs = [_rows(tm, d), _rows(tm, k), _full((1, d))]
        out_shape = [_sds((t, d), ACT), _sds((t, k), ACT), _sds((1, d), F32)]
    if mode == "attn":
        out_specs.append(_full((1, d)))
        out_shape.append(_sds((1, d), F32))
    return _call(body, name, (t // tm,), specs, out_specs, out_shape, ins, ("arbitrary",), jobs=jobs)


def _pre_bwd(dy, w, x, g, dres, name, transposed=False, jobs=()):
    t, d = x.shape
    tm = _row_tile(t)

    def body(dy_ref, w_ref, x_ref, g_ref, dres_ref, dx_ref, dg_ref):
        @pl.when(pl.program_id(0) == 0)
        def _():
            dg_ref[...] = jnp.zeros_like(dg_ref)

        dh = jnp.zeros((tm, d), F32)
        if w.ndim == 3 and transposed:
            for j in range(w.shape[0]):
                dh = dh + _dot(dy_ref[j], w_ref[j])
        elif w.ndim == 3:
            c = w.shape[2]
            for j in range(w.shape[0]):
                dh = dh + _dot_nt(dy_ref[j] if dy.ndim == 3 else dy_ref[:, j * c:(j + 1) * c], w_ref[j])
        else:
            n = w.shape[1]
            nc = _pick(n, (1408, 1024, 512))
            for c in range(0, n, nc):
                dh = dh + _dot_nt(dy_ref[:, c:c + nc], w_ref[:, c:c + nc])
        xv = x_ref[...]
        r = _rstd(xv)
        xh = xv * r
        dg_ref[...] += jnp.sum(dh * xh, axis=0, keepdims=True)
        dx_ref[...] = dres_ref[...] + _rms_bwd(xh, r, g_ref[...], dh)

    dy_spec = pl.BlockSpec((dy.shape[0], tm, dy.shape[2]), lambda i: (0, i, 0)) if dy.ndim == 3 else _rows(tm, dy.shape[1])
    return _call(
        body, name, (t // tm,),
        [dy_spec, _full(w.shape), _rows(tm, d), _full((1, d)), _rows(tm, d)],
        [_rows(tm, d), _full((1, d))], [_sds((t, d), F32), _sds((1, d), F32)],
        (dy, w, x, g, dres), ("arbitrary",), jobs=jobs)


def _wgrad(a, b, name, out_dtype=F32, out_block=None, transposed=False, jobs=()):
    t = a.shape[-2]
    tt = _pick(t, (1024,))
    steps = t // tt
    if a.ndim == 3:
        k_steps, _, tk = a.shape
        a_spec = pl.BlockSpec((None, tt, tk), lambda i, j, s: (i, s, 0))
    else:
        tk = _pick(a.shape[1], (1024, 1408))
        k_steps = a.shape[1] // tk
        a_spec = pl.BlockSpec((tt, tk), lambda i, j, s: (s, i))
    k = k_steps * tk
    group = 1
    if b.ndim == 3:
        n_blocks, _, tn = b.shape
        group = 2 if n_blocks % 2 == 0 else 1
        n_steps, per = n_blocks // group, 1
        b_spec = pl.BlockSpec((group, tt, tn), lambda i, j, s: (j, s, 0))
        if transposed:
            out_spec, out_shape = pl.BlockSpec((group, tn, tk), lambda i, j, s: (j, 0, i)), _sds((n_blocks, tn, k), out_dtype)
        else:
            out_spec, out_shape = pl.BlockSpec((group, tk, tn), lambda i, j, s: (j, i, 0)), _sds((n_blocks, k, tn), out_dtype)
    else:
        n = b.shape[1]
        tn = _pick(n, (1024, 1408))
        n_steps = n // tn
        b_spec = pl.BlockSpec((tt, tn), lambda i, j, s: (s, j))
        if out_block is None:
            per = 0
            out_spec, out_shape = pl.BlockSpec((tk, tn), lambda i, j, s: (i, j)), _sds((k, n), out_dtype)
        else:
            per = tn // out_block
            out_spec = pl.BlockSpec((per, tk, out_block), lambda i, j, s: (j, i, 0))
            out_shape = _sds((n // out_block, k, out_block), out_dtype)

    def body(a_ref, b_ref, o_ref, acc_ref):
        s = pl.program_id(2)

        @pl.when(s == 0)
        def _():
            acc_ref[...] = jnp.zeros_like(acc_ref)

        if b.ndim == 3:
            av = a_ref[...]
            for p in range(group):
                acc_ref[p] += _dot_tn(b_ref[p], av) if transposed else _dot_tn(av, b_ref[p])
        else:
            acc_ref[...] += _dot_tn(a_ref[...], b_ref[...])

        @pl.when(s == steps - 1)
        def _():
            if b.ndim == 2 and per >= 1:
                for p in range(per):
                    o_ref[p] = acc_ref[:, p * out_block:(p + 1) * out_block].astype(out_dtype)
            else:
                o_ref[...] = acc_ref[...].astype(out_dtype)

    acc_shape = ((group, tn, tk) if transposed else (group, tk, tn)) if b.ndim == 3 else (tk, tn)
    res, job_res = _call(
        body, name, (k_steps, n_steps, steps), [a_spec, b_spec], [out_spec], [out_shape],
        (a, b), ("parallel", "parallel", "arbitrary"), scratch=[pltpu.VMEM(acc_shape, F32)], jobs=jobs)
    return res[0], job_res


def _ffn_up_fwd(x, g, w, jobs=()):
    t, d = x.shape
    nb, c, _ = w.shape
    tm = _row_tile(t)

    def body(x_ref, g_ref, w_ref, h_ref, gu_ref, a_ref):
        xv = x_ref[...]
        hb = (xv * _rstd(xv) * g_ref[...]).astype(ACT)
        h_ref[...] = hb
        for j in range(nb // 2):
            gate = _dot_nt(hb, w_ref[j])
            up = _dot_nt(hb, w_ref[nb // 2 + j])
            gu_ref[j] = gate.astype(ACT)
            gu_ref[nb // 2 + j] = up.astype(ACT)
            a_ref[j] = (gate * _sigmoid(gate) * up).astype(ACT)

    slab = lambda n: pl.BlockSpec((n, tm, c), lambda i: (0, i, 0))
    return _call(
        body, "ffn_up_fwd", (t // tm,),
        [_rows(tm, d), _full((1, d)), _full(w.shape)],
        [_rows(tm, d), slab(nb), slab(nb // 2)],
        [_sds((t, d), ACT), _sds((nb, t, c), ACT), _sds((nb // 2, t, c), ACT)],
        (x, g, w), ("parallel",), jobs=jobs)


def _causal(ws):
    row = lax.broadcasted_iota(jnp.int32, ws.shape, 0)
    col = lax.broadcasted_iota(jnp.int32, ws.shape, 1)
    return jnp.where(col <= row, ws, 0.0)


def _sgu_ln(v, lg, lb):
    mu = jnp.mean(v, axis=-1, keepdims=True)
    vc = v - mu
    rs = lax.rsqrt(jnp.mean(vc * vc, axis=-1, keepdims=True) + EPS)
    vh = vc * rs
    return vh, rs, vh * lg + lb


def _sgu_fwd(x, g, w, lg, lb, ws, bs_t, jobs=()):
    t, d = x.shape
    nb, _, c = w.shape
    n = nb * c
    groups = d // WINDOW
    tm = _row_tile(t)

    def body(x_ref, g_ref, w_ref, lg_ref, lb_ref, ws_ref, bs_ref, h_ref, z_ref, y_ref, zf_ref):
        xv = x_ref[...]
        hb = (xv * _rstd(xv) * g_ref[...]).astype(ACT)
        h_ref[...] = hb
        for j in range(nb):
            zf_ref[:, j * c:(j + 1) * c] = _dot(hb, w_ref[j])
        z_ref[...] = zf_ref[...].astype(ACT)
        gs = [slice(gi * WINDOW, (gi + 1) * WINDOW) for gi in range(groups)]
        wsg = [_causal(ws_ref[gi]).astype(ACT) for gi in range(groups)]
        for r in range(0, tm, WINDOW):
            u = _gelu(zf_ref[r:r + WINDOW, :d])
            _, _, vn = _sgu_ln(_gelu(zf_ref[r:r + WINDOW, d:]), lg_ref[...], lb_ref[...])
            mixed = [_dot(wsg[gi], vn[:, gs[gi]].astype(ACT)) for gi in range(groups)]
            for gi in range(groups):
                y_ref[r:r + WINDOW, gs[gi]] = (u[:, gs[gi]] * (mixed[gi] + bs_ref[:, gi:gi + 1])).astype(ACT)

    vec = _full((1, d))
    return _call(
        body, "sgu_fwd", (t // tm,),
        [_rows(tm, d), vec, _full((nb, d, c)), vec, vec, _full((groups, WINDOW, WINDOW)), _full((WINDOW, groups))],
        [_rows(tm, d), _rows(tm, n), _rows(tm, d)], [_sds((t, d), ACT), _sds((t, n), ACT), _sds((t, d), ACT)],
        (x, g, w, lg, lb, ws, bs_t), ("parallel",), scratch=[pltpu.VMEM((tm, n), F32)], jobs=jobs)


def _sgu_mix_bwd(z, dy, lg, lb, ws, bs_t, jobs=()):
    t, n = z.shape
    d = n // 2
    groups = d // WINDOW
    tm = _row_tile(t)

    def body(z_ref, dy_ref, lg_ref, lb_ref, ws_ref, bs_ref, dz_ref, dlg_ref, dlb_ref, dws_ref, dbs_ref):
        @pl.when(pl.program_id(0) == 0)
        def _():
            for ref in (dlg_ref, dlb_ref, dws_ref, dbs_ref):
                ref[...] = jnp.zeros_like(ref)

        lane = lax.broadcasted_iota(jnp.int32, (WINDOW, groups), 1)
        gs = [slice(gi * WINDOW, (gi + 1) * WINDOW) for gi in range(groups)]
        wsg = [_causal(ws_ref[gi]).astype(ACT) for gi in range(groups)]
        for c in range(0, tm, WINDOW):
            rows = slice(c, c + WINDOW)
            zu, zv = z_ref[rows, :d].astype(F32), z_ref[rows, d:].astype(F32)
            u = _gelu(zu)
            vh, rs, vn = _sgu_ln(_gelu(zv), lg_ref[...], lb_ref[...])
            dyv = dy_ref[rows, :].astype(F32)
            vng = [vn[:, gs[gi]].astype(ACT) for gi in range(groups)]
            dmix = dyv * u
            dmb = [dmix[:, gs[gi]].astype(ACT) for gi in range(groups)]
            mixed = [_dot(wsg[gi], vng[gi]) for gi in range(groups)]
            dvn_parts = [_dot_tn(wsg[gi], dmb[gi]) for gi in range(groups)]
            dws_parts = [_dot_nt(dmb[gi], vng[gi]) for gi in range(groups)]
            for gi in range(groups):
                dz_ref[rows, gs[gi]] = (dyv[:, gs[gi]] * (mixed[gi] + bs_ref[:, gi:gi + 1]) * _gelu_grad(zu[:, gs[gi]])).astype(ACT)
                dws_ref[:, gs[gi]] += _causal(dws_parts[gi])
                dbs_ref[...] += jnp.where(lane == gi, jnp.sum(dmix[:, gs[gi]], axis=-1, keepdims=True), 0.0)
            dvn = jnp.concatenate(dvn_parts, axis=-1)
            dlg_ref[...] += jnp.sum(dvn * vh, axis=0, keepdims=True)
            dlb_ref[...] += jnp.sum(dvn, axis=0, keepdims=True)
            dvh = dvn * lg_ref[...]
            dv = rs * (dvh - jnp.mean(dvh, axis=-1, keepdims=True) - vh * jnp.mean(dvh * vh, axis=-1, keepdims=True))
            dz_ref[rows, d:] = (dv * _gelu_grad(zv)).astype(ACT)

    vec = _full((1, d))
    return _call(
        body, "sgu_mix_bwd", (t // tm,),
        [_rows(tm, n), _rows(tm, d), vec, vec, _full((groups, WINDOW, WINDOW)), _full((WINDOW, groups))],
        [_rows(tm, n), vec, vec, _full((WINDOW, d)), _full((WINDOW, groups))],
        [_sds((t, n), ACT), _sds((1, d), F32), _sds((1, d), F32), _sds((WINDOW, d), F32), _sds((WINDOW, groups), F32)],
        (z, dy, lg, lb, ws, bs_t), ("arbitrary",), jobs=jobs)


AG_COPIES = 8


def _prepare(sources, dtypes, n_gather, name):
    n = len(sources)
    arrays, where = [], []
    for parts, layer in sources:
        found = []
        for part in parts:
            known = [i for i, v in enumerate(arrays) if v is part]
            if not known:
                arrays.append(part)
            found.append(known[0] if known else len(arrays) - 1)
        where.append((found, layer))
    shapes = [(sum(p.shape[1] for p in parts), parts[0].shape[2]) for parts, _ in sources]

    def body(*refs):
        ins, refs = refs[:len(arrays)], refs[len(arrays):]
        stage, outs = refs[:n], refs[n:n + n_gather]
        send, recv, local_sem = refs[n + n_gather:]
        x, y, c = _place()
        me, sibling, beside_x, beside_y, far = (x, y, c), (x, y, 1 - c), (1 - x, y, c), (x, 1 - y, c), (1 - x, 1 - y, c)
        slot = lambda dev: 4 * dev[0] + 2 * dev[1] + dev[2]
        other = lambda dev: (dev[0], dev[1], 1 - c)
        whole = lambda a: (0, shapes[a][0])
        cuts = [rows // 2 if rows % 32 == 0 else rows for rows, _ in shapes]
        first = lambda a: (0, cuts[a])
        rest = lambda a: (cuts[a], shapes[a][0] - cuts[a])

        def copy(a, k, block, rows, to, src=None):
            dst = outs[a].at[block, pl.ds(*rows)]
            return pltpu.make_async_remote_copy(
                src_ref=dst if src is None else src, dst_ref=dst, send_sem=send.at[a * AG_COPIES + k],
                recv_sem=recv.at[a * AG_COPIES + k], device_id=to, device_id_type=MESH)

        def cast(a):
            found, layer = where[a]
            at = 0
            for i in found:
                rows = arrays[i].shape[1]
                stage[a][at:at + rows, :] = ins[i][layer].astype(dtypes[a])
                at += rows

        for a in range(n_gather):
            cast(a)
        started = []
        for a in range(n_gather):
            own = pltpu.make_async_copy(stage[a], outs[a].at[slot(me)], local_sem.at[a])
            own.start()
            started.append(own)
        sends = []
        for a in range(n_gather):
            sends += [copy(a, k, slot(me), whole(a), to, src=stage[a]) for k, to in enumerate((sibling, beside_x, beside_y))]
        for cp in sends:
            cp.start()
        for a in range(n_gather, n):
            cast(a)
        for a in range(n_gather):
            copy(a, 1, slot(beside_x), whole(a), me).wait_recv()
            copy(a, 2, slot(beside_y), whole(a), me).wait_recv()
            passed = [copy(a, 3, slot(beside_x), first(a), beside_y), copy(a, 5, slot(beside_x), whole(a), sibling),
                      copy(a, 6, slot(beside_y), whole(a), sibling)]
            if rest(a)[1]:
                passed.append(copy(a, 4, slot(beside_y), rest(a), beside_x))
            for cp in passed:
                cp.start()
            sends += passed
        for a in range(n_gather):
            copy(a, 3, slot(far), first(a), me).wait_recv()
            if rest(a)[1]:
                copy(a, 4, slot(far), rest(a), me).wait_recv()
            passed = copy(a, 7, slot(far), whole(a), sibling)
            passed.start()
            sends.append(passed)
        for a in range(n_gather):
            for k, source in ((0, me), (5, beside_x), (6, beside_y), (7, far)):
                copy(a, k, slot(other(source)), whole(a), me).wait_recv()
        for cp in sends:
            cp.wait_send()
        for own in started:
            own.wait()

    res = pl.pallas_call(
        body, name=name,
        in_specs=[IN_VMEM] * len(arrays), out_specs=[IN_VMEM] * n + [ANY] * n_gather,
        out_shape=[_sds(s, dt) for s, dt in zip(shapes, dtypes)]
        + [_sds((N_DEV,) + s, dt) for s, dt in zip(shapes[:n_gather], dtypes)],
        scratch_shapes=[pltpu.SemaphoreType.DMA((n_gather * AG_COPIES,)), pltpu.SemaphoreType.DMA((n_gather * AG_COPIES,)),
                        pltpu.SemaphoreType.DMA((n_gather,))],
        compiler_params=pltpu.CompilerParams(vmem_limit_bytes=VMEM_LIMIT_BYTES),
    )(*arrays)
    return list(res[:n]), list(res[n:])


def _pair_sum(g, r, core, name):
    _, _, rows, cols = g.shape
    tr = _pick(rows, (512, 256, 128))

    def body(core_ref, g_ref, r_ref, p_ref):
        del core_ref
        p_ref[...] = (g_ref[...].astype(F32) + r_ref[...].astype(F32)).astype(p_ref.dtype)

    return pl.pallas_call(
        body, name=name,
        grid_spec=pltpu.PrefetchScalarGridSpec(
            num_scalar_prefetch=1, grid=(4, rows // tr),
            in_specs=[pl.BlockSpec((None, None, tr, cols), lambda q, i, core_ref: (q, core_ref[0], i, 0)),
                      pl.BlockSpec((None, tr, cols), lambda q, i, core_ref: (q, i, 0))],
            out_specs=pl.BlockSpec((None, tr, cols), lambda q, i, core_ref: (q, i, 0))),
        out_shape=_sds((4, rows, cols), g.dtype),
        compiler_params=_params("parallel", "parallel"),
    )(core, g, r)


def _adam(w, g, m, v):
    m2 = ADAM_B1 * m + (1.0 - ADAM_B1) * g
    v2 = ADAM_B2 * v + (1.0 - ADAM_B2) * (g * g)
    m_hat = m2 / (1.0 - ADAM_B1 ** ADAM_STEP)
    v_hat = v2 / (1.0 - ADAM_B2 ** ADAM_STEP)
    return -ADAM_LR * (m_hat / (jnp.sqrt(v_hat) + ADAM_EPS) + ADAM_WD * w), m2, v2


def _shard_update(own, index, received, w, m, v, layer, so_far, name):
    _, rows, cols = w.shape
    n_recv = received.shape[0]
    tr = _pick(rows, (512, 256, 128))

    def body(index_ref, p_ref, q_ref, w_ref, m_ref, v_ref, *rest):
        del index_ref
        g_out, d_out, m_out, v_out = rest[-4:]
        g = p_ref[...].astype(F32)
        for s in range(n_recv):
            g = g + q_ref[s].astype(F32)
        g_out[...] = g
        d_out[...], m_out[...], v_out[...] = _adam(w_ref[...], g, m_ref[...], v_ref[...])

    tile = pl.BlockSpec((None, tr, cols), lambda i, index_ref: (layer, i, 0))
    kept = list(so_far) if so_far is not None else []
    return pl.pallas_call(
        body, name=name,
        grid_spec=pltpu.PrefetchScalarGridSpec(
            num_scalar_prefetch=1, grid=(rows // tr,),
            in_specs=[pl.BlockSpec((None, tr, cols), lambda i, index_ref: (index_ref[0], i, 0)),
                      pl.BlockSpec((n_recv, tr, cols), lambda i, index_ref: (0, i, 0)), tile, tile, tile] + [ANY] * len(kept),
            out_specs=[tile] * 4),
        out_shape=[_sds(w.shape, F32)] * 4,
        input_output_aliases={6 + i: i for i in range(len(kept))},
        compiler_params=_params("parallel"),
    )(index, own, received, w, m, v, *kept)


def _natural_pieces(shape, r, c):
    if tuple(shape[-2:]) == (r, c) and all(n == 1 for n in shape[:-2]):
        return [((0,) * (len(shape) - 2), (0, c))]
    groups, width = shape[1], shape[3]
    assert len(shape) == 4 and shape[0] == 1 and shape[2] == r and groups * width == c, (shape, r, c)
    return [((0, g), (g * width, width)) for g in range(groups)]


def _replicated_update(shares, where, params, name):
    flat = [a for p in params if p is not None for a in p]

    def body(s_ref, *refs):
        ins, outs = refs[:len(flat)], refs[len(flat):]
        total = s_ref[0]
        for dev in range(1, N_DEV):
            total = total + s_ref[dev]
        i_at = o_at = 0
        for ranges, p in zip(where, params):
            chunks = [total[r0:r0 + r, :c] for r0, r, c in ranges]
            g2d = chunks[0] if len(chunks) == 1 else jnp.concatenate(chunks, axis=1)
            if p is None:
                outs[o_at][...] = g2d
                o_at += 1
                continue
            w_ref, m_ref, v_ref = ins[i_at:i_at + 3]
            for idx, (c0, cols) in _natural_pieces(p[0].shape, *g2d.shape):
                at = idx + (slice(None), slice(None))
                g = g2d[:, c0:c0 + cols]
                outs[o_at][at] = g
                outs[o_at + 1][at], outs[o_at + 2][at], outs[o_at + 3][at] = _adam(w_ref[at], g, m_ref[at], v_ref[at])
            i_at, o_at = i_at + 3, o_at + 4

    out_shape = []
    for ranges, p in zip(where, params):
        out_shape += [_sds((ranges[0][1], sum(c for _, _, c in ranges)), F32)] if p is None else [_sds(p[0].shape, F32)] * 4
    res = pl.pallas_call(
        body, name=name, out_shape=out_shape, compiler_params=pltpu.CompilerParams(vmem_limit_bytes=VMEM_LIMIT_BYTES),
    )(shares, *flat)
    out, at = [], 0
    for p in params:
        out.append(list(res[at:at + (1 if p is None else 4)]))
        at += 1 if p is None else 4
    return out


def _rope_tables(t):
    half = HEAD_DIM // 2
    inv_freq = ROPE_THETA ** (-(jnp.arange(half, dtype=F32) * 2.0) / HEAD_DIM)
    ang = jnp.arange(t, dtype=jnp.int32).astype(F32)[:, None] * inv_freq[None, :]
    cos, sin = jnp.cos(ang), jnp.sin(ang)
    return jnp.tile(jnp.concatenate([cos, cos], axis=1), (1, 2)), jnp.tile(jnp.concatenate([-sin, sin], axis=1), (1, 2))


def _dup_heads(w, d):
    kv = (w.shape[-1] - d) // 2
    lead = w.shape[:-1]

    def dup(part):
        heads = part.reshape(lead + (kv // HEAD_DIM, 1, HEAD_DIM))
        return jnp.broadcast_to(heads, lead + (kv // HEAD_DIM, 2, HEAD_DIM)).reshape(lead + (2 * kv,))

    return jnp.concatenate([w[..., :d], dup(w[..., d:d + kv]), dup(w[..., d + kv:])], axis=-1)


def _fold_heads(dw, d):
    kv = (dw.shape[-1] - d) // 4
    lead = dw.shape[:-1]

    def fold(part):
        return part.reshape(lead + (kv // HEAD_DIM, 2, HEAD_DIM)).sum(axis=-2).reshape(lead + (kv,))

    return jnp.concatenate([dw[..., :d], fold(dw[..., d:d + 2 * kv]), fold(dw[..., d + 2 * kv:])], axis=-1)


def _columns_full(gathered):
    _, rows, c = gathered.shape
    return jnp.transpose(gathered, (1, 0, 2)).reshape(rows, N_DEV * c)


def _rows_full(gathered):
    return gathered.reshape(-1, gathered.shape[-1])


def _columns_blocked(full):
    rows, cols = full.shape
    return jnp.transpose(full.reshape(rows, N_DEV, cols // N_DEV), (1, 0, 2)).astype(ACT)


def _rows_blocked(full):
    return full.reshape(N_DEV, full.shape[0] // N_DEV, full.shape[1]).astype(ACT)


def _pack_rows(parts, width):
    rows, where, at = [], [], 0
    for v in parts:
        r, c = v.shape
        n_rows = -(-r // 8) * 8
        chunks = []
        for c0 in range(0, c, width):
            chunk = v[:, c0:c0 + width].astype(F32)
            rows.append(jnp.pad(chunk, ((0, n_rows - r), (0, width - chunk.shape[1]))))
            chunks.append((at, r, chunk.shape[1]))
            at += n_rows
        where.append(chunks)
    return jnp.concatenate(rows, axis=0), where


def _halves(block):
    half = block.shape[0] // 2
    return (0, half), (half, half)


def _whole(block):
    return (0, block.shape[0])


EARLY = ("attn_w_qkv", "sgu_ln", "attn_w_o")
LATE = ("sgu_w_in", "sgu_w_out", "gu0", "gu1", "dn0", "dn1")
COLUMN_SHARDED = ("attn_w_qkv", "sgu_w_in", "gu0", "gu1")
BEFORE_ATTN = ("norm_mix_post", "norm_ffn_pre", "norm_ffn_post", "attn_b_o", "sgu_w_spatial", "sgu_b_spatial")
AFTER_ATTN = ("attn_b_qkv", "attn_sinks")
LAST = ("norm_mix_pre",)
WEIGHTS = ("norm_mix_pre", "norm_mix_post", "norm_ffn_pre", "norm_ffn_post", "attn_w_qkv", "attn_b_qkv", "attn_sinks", "attn_w_o",
           "attn_b_o", "sgu_w_in", "sgu_ln_g", "sgu_ln_b", "sgu_w_spatial", "sgu_b_spatial", "sgu_w_out", "ffn_w_gate_up", "ffn_w_down")


LAYER_OF = {"gu0": ("ffn_w_gate_up", 0), "gu1": ("ffn_w_gate_up", 1), "dn0": ("ffn_w_down", 0), "dn1": ("ffn_w_down", 1)}


def _source(tree, name):
    if name == "sgu_ln":
        return [tree["sgu_ln_g"][None], tree["sgu_ln_b"][None]], 0
    leaf, layer = LAYER_OF.get(name, (name, 0))
    return [tree[leaf]], layer


def kernel(x, norm_mix_pre, norm_mix_post, norm_ffn_pre, norm_ffn_post, attn_w_qkv, attn_b_qkv, attn_sinks, attn_w_o, attn_b_o, sgu_w_in, sgu_ln_g, sgu_ln_b, sgu_w_spatial, sgu_b_spatial, sgu_w_out, ffn_w_gate_up, ffn_w_down, loss_target, m_norm_mix_pre, m_norm_mix_post, m_norm_ffn_pre, m_norm_ffn_post, m_attn_w_qkv, m_attn_b_qkv, m_attn_sinks, m_attn_w_o, m_attn_b_o, m_sgu_w_in, m_sgu_ln_g, m_sgu_ln_b, m_sgu_w_spatial, m_sgu_b_spatial, m_sgu_w_out, m_ffn_w_gate_up, m_ffn_w_down, v_norm_mix_pre, v_norm_mix_post, v_norm_ffn_pre, v_norm_ffn_post, v_attn_w_qkv, v_attn_b_qkv, v_attn_sinks, v_attn_w_o, v_attn_b_o, v_sgu_w_in, v_sgu_ln_g, v_sgu_ln_b, v_sgu_w_spatial, v_sgu_b_spatial, v_sgu_w_out, v_ffn_w_gate_up, v_ffn_w_down):
    w = dict(norm_mix_pre=norm_mix_pre, norm_mix_post=norm_mix_post, norm_ffn_pre=norm_ffn_pre, norm_ffn_post=norm_ffn_post,
             attn_w_qkv=attn_w_qkv, attn_b_qkv=attn_b_qkv, attn_sinks=attn_sinks, attn_w_o=attn_w_o, attn_b_o=attn_b_o,
             sgu_w_in=sgu_w_in, sgu_ln_g=sgu_ln_g, sgu_ln_b=sgu_ln_b, sgu_w_spatial=sgu_w_spatial, sgu_b_spatial=sgu_b_spatial,
             sgu_w_out=sgu_w_out, ffn_w_gate_up=ffn_w_gate_up, ffn_w_down=ffn_w_down)
    mom = dict(norm_mix_pre=m_norm_mix_pre, norm_mix_post=m_norm_mix_post, norm_ffn_pre=m_norm_ffn_pre, norm_ffn_post=m_norm_ffn_post,
               attn_w_qkv=m_attn_w_qkv, attn_b_qkv=m_attn_b_qkv, attn_sinks=m_attn_sinks, attn_w_o=m_attn_w_o, attn_b_o=m_attn_b_o,
               sgu_w_in=m_sgu_w_in, sgu_ln_g=m_sgu_ln_g, sgu_ln_b=m_sgu_ln_b, sgu_w_spatial=m_sgu_w_spatial,
               sgu_b_spatial=m_sgu_b_spatial, sgu_w_out=m_sgu_w_out, ffn_w_gate_up=m_ffn_w_gate_up, ffn_w_down=m_ffn_w_down)
    var = dict(norm_mix_pre=v_norm_mix_pre, norm_mix_post=v_norm_mix_post, norm_ffn_pre=v_norm_ffn_pre, norm_ffn_post=v_norm_ffn_post,
               attn_w_qkv=v_attn_w_qkv, attn_b_qkv=v_attn_b_qkv, attn_sinks=v_attn_sinks, attn_w_o=v_attn_w_o, attn_b_o=v_attn_b_o,
               sgu_w_in=v_sgu_w_in, sgu_ln_g=v_sgu_ln_g, sgu_ln_b=v_sgu_ln_b, sgu_w_spatial=v_sgu_w_spatial,
               sgu_b_spatial=v_sgu_b_spatial, sgu_w_out=v_sgu_w_out, ffn_w_gate_up=v_ffn_w_gate_up, ffn_w_down=v_ffn_w_down)
    w, mom, var = [{**tree, "ffn_w_gate_up": jnp.swapaxes(tree["ffn_w_gate_up"], 1, 2)} for tree in (w, mom, var)]
    xs, target = x[0], loss_target[0]
    t, d = xs.shape
    row = lambda v: v.reshape(1, -1).astype(F32)
    core = lax.axis_index("c").astype(jnp.int32).reshape(1)
    chip = (2 * lax.axis_index("x") + lax.axis_index("y")).astype(jnp.int32).reshape(1)
    names = EARLY + LATE
    staged, early = _prepare([_source(w, n) for n in names], [F32 if n == "sgu_ln" else ACT for n in names], len(EARLY),
                             "weights_prepare")
    stage = dict(zip(names, staged))
    w_qkv = _dup_heads(_columns_full(early[0]), d)
    lg, lb = row(early[1][:, 0, :]), row(early[1][:, 1, :])
    w_o = _rows_full(early[2])
    b_qkv = _dup_heads(row(attn_b_qkv), d)
    sinks = attn_sinks.reshape(1, -1).astype(F32)
    ws = sgu_w_spatial[0].astype(F32)
    bs_t = sgu_b_spatial[0].astype(F32).T
    cos, sin = _rope_tables(t)
    gather = lambda name, so_far, **pieces: _gather_job(stage[name], so_far, **pieces)
    a_half = lambda name: _halves(stage[name])[0]
    b_half = lambda name: _halves(stage[name])[1]
    whole = lambda name: _whole(stage[name])

    def pieces(name, n):
        rows = stage[name].shape[0] // n
        return [(i * rows, rows) for i in range(n)]

    ways = lambda parts: [(piece, i % 2) for i, piece in enumerate(parts)]
    relay = lambda name, so_far, **steps: _relay_gather_job(stage[name], so_far, **steps)
    gu0_parts, gu1_parts = pieces("gu0", 4), pieces("gu1", 4)
    dn0_parts, dn1_parts, in_parts, out_parts = pieces("dn0", 2), pieces("dn1", 2), pieces("sgu_w_in", 2), pieces("sgu_w_out", 2)
    (h0, q, kdup, vdup), ((g_gu0,),) = _attn_qkv_fwd(
        xs, row(norm_mix_pre[0]), w_qkv, b_qkv, cos, sin, jobs=[relay("gu0", None, sends=gu0_parts)])
    (o,), ((g_gu0,), (g_dn0,)) = _attn_fwd(q, kdup, vdup, sinks, jobs=[
        relay("gu0", g_gu0, relays=ways(gu0_parts), forwards=gu0_parts), relay("dn0", None, sends=dn0_parts)])
    (m0, x1), ((g_gu0,), (g_dn0,), (g_in,)) = _proj_res(o, w_o, row(attn_b_o), row(norm_mix_post[0]), xs, "attn_out_fwd", jobs=[
        relay("gu0", g_gu0, far=gu0_parts), relay("dn0", g_dn0, relays=ways(dn0_parts), forwards=dn0_parts),
        relay("sgu_w_in", None, sends=in_parts)])
    w_gu0 = g_gu0
    (h1, gu0, a0), ((g_dn0,), (g_in,), (g_out,), (g_gu1,)) = _ffn_up_fwd(x1, row(norm_ffn_pre[0]), w_gu0, jobs=[
        relay("dn0", g_dn0, far=dn0_parts), relay("sgu_w_in", g_in, relays=ways(in_parts), forwards=in_parts),
        relay("sgu_w_out", None, sends=out_parts), relay("gu1", None, sends=gu1_parts[:2])])
    w_dn0 = _rows_full(g_dn0)
    (f0, x2), ((g_in,), (g_out,), (g_gu1,)) = _proj_res(a0, w_dn0, None, row(norm_ffn_post[0]), x1, "ffn_down_fwd0", jobs=[
        relay("sgu_w_in", g_in, far=in_parts), relay("sgu_w_out", g_out, relays=ways(out_parts), forwards=out_parts),
        relay("gu1", g_gu1, sends=gu1_parts[2:], relays=ways(gu1_parts[:2]), forwards=gu1_parts[:2])])
    w_in = g_in
    (h2, z, y), ((g_out,), (g_gu1,), (g_dn1,)) = _sgu_fwd(x2, row(norm_mix_pre[1]), w_in, lg, lb, ws, bs_t, jobs=[
        relay("sgu_w_out", g_out, far=out_parts),
        relay("gu1", g_gu1, relays=ways(gu1_parts[2:]), forwards=gu1_parts[2:], far=gu1_parts[:2]),
        relay("dn1", None, sends=dn1_parts)])
    w_out = _rows_full(g_out)
    (m1, x3), ((g_gu1,), (g_dn1,)) = _proj_res(y, w_out, None, row(norm_mix_post[1]), x2, "sgu_out_fwd", jobs=[
        relay("gu1", g_gu1, far=gu1_parts[2:]), relay("dn1", g_dn1, relays=ways(dn1_parts), forwards=dn1_parts)])
    w_gu1 = g_gu1
    (h3, gu1, a1), ((g_dn1,),) = _ffn_up_fwd(x3, row(norm_ffn_pre[1]), w_gu1, jobs=[relay("dn1", g_dn1, far=dn1_parts)])
    w_dn1 = _rows_full(g_dn1)
    (f1, dx4, loss_tile), _ = _proj_res(a1, w_dn1, None, row(norm_ffn_post[1]), x3, "ffn_down_fwd1", target=target)

    to_sibling = lambda g: _scatter_job([g], 4, _to_sibling)
    pair = lambda g, r, name: _pair_sum(g.reshape((4, 2) + g.shape[1:]), r, core, "pair_sum_" + name)
    to_chips = lambda p, prev=None, piece=None: _scatter_job([p], 3, _to_owner_chip, prev=prev, piece=piece)
    out_g, out_d, out_m, out_v = {}, {}, {}, {}

    trees = [{**tree, "sgu_ln": jnp.stack([tree["sgu_ln_g"], tree["sgu_ln_b"]], axis=1)} for tree in (w, mom, var)]
    so_far = {}

    def update(name, own, index, received):
        leaf, layer = LAYER_OF.get(name, (name, 0))
        so_far[leaf] = _shard_update(own, index, received, *[tree[leaf] for tree in trees], layer, so_far.get(leaf), "update_" + name)
        for out, val in zip((out_g, out_d, out_m, out_v), so_far[leaf]):
            out[leaf] = val

    def finish(names, n_extra, shares, where, name):
        res = _replicated_update(shares, where, [(w[n], mom[n], var[n]) for n in names] + [None] * n_extra, name)
        for n, vals in zip(names, res):
            out_g[n], out_d[n], out_m[n], out_v[n] = vals
        return [vals[0] for vals in res[len(names):]]

    first_half = lambda p: _halves(p[0])[0]
    second_half = lambda p: _halves(p[0])[1]
    (df1, dgu1, dg_ffn_post1), _ = _post_bwd(dx4, f1, row(norm_ffn_post[1]), w_dn1, "ffn", "ffn_down_bwd1", gu1)
    b_gu1, _ = _wgrad(h3, dgu1, "ffn_up_wgrad1", ACT, transposed=True)
    dw_dn1, ((r_gu1,),) = _wgrad(a1, df1, "ffn_down_wgrad1", ACT, jobs=[to_sibling(b_gu1)])
    b_dn1 = _rows_blocked(dw_dn1)
    p_gu1 = pair(b_gu1, r_gu1, "gu1")
    (dx3, dg_ffn_pre1), ((q_gu1,), (r_dn1,)) = _pre_bwd(dgu1, w_gu1, x3, row(norm_ffn_pre[1]), dx4, "ffn_up_bwd1", True, jobs=[
        to_chips(p_gu1, piece=first_half(p_gu1)), to_sibling(b_dn1)])
    p_dn1 = pair(b_dn1, r_dn1, "dn1")
    (dm1, dy, dg_mix_post1), _ = _post_bwd(dx3, m1, row(norm_mix_post[1]), w_out, "sgu", "sgu_out_bwd")
    dw_out, _ = _wgrad(y, dm1, "sgu_out_wgrad", ACT)
    b_out = _rows_blocked(dw_out)
    (dz, dlg, dlb, dws, dbs_t), ((q_gu1,), (r_out,)) = _sgu_mix_bwd(z, dy, lg, lb, ws, bs_t, jobs=[
        to_chips(p_gu1, prev=[q_gu1], piece=second_half(p_gu1)), to_sibling(b_out)])
    update("gu1", p_gu1, chip, q_gu1)
    p_out = pair(b_out, r_out, "sgu_w_out")
    b_in, _ = _wgrad(h2, dz, "sgu_in_wgrad", ACT, out_block=stage["sgu_w_in"].shape[1])
    b_ln = jnp.stack([dlg.reshape(N_DEV, -1), dlb.reshape(N_DEV, -1)], axis=1)
    (dx2, dg_mix_pre1), _ = _pre_bwd(dz, w_in, x2, row(norm_mix_pre[1]), dx3, "sgu_in_bwd")
    (df0, dgu0, dg_ffn_post0), ((q_dn1,), (q_out,), (r_in,), (r_ln,)) = _post_bwd(
        dx2, f0, row(norm_ffn_post[0]), w_dn0, "ffn", "ffn_down_bwd0", gu0,
        jobs=[to_chips(p_dn1), to_chips(p_out), to_sibling(b_in), to_sibling(b_ln)])
    update("dn1", p_dn1, chip, q_dn1)
    update("sgu_w_out", p_out, chip, q_out)
    p_in, p_ln = pair(b_in, r_in, "sgu_w_in"), pair(b_ln, r_ln, "sgu_ln")
    b_gu0, ((q_in,), (q_ln,)) = _wgrad(h1, dgu0, "ffn_up_wgrad0", ACT, transposed=True, jobs=[to_chips(p_in), to_chips(p_ln)])
    update("sgu_w_in", p_in, chip, q_in)
    update("sgu_ln", p_ln, chip, q_ln)
    dw_dn0, ((r_gu0,),) = _wgrad(a0, df0, "ffn_down_wgrad0", ACT, jobs=[to_sibling(b_gu0)])
    b_dn0 = _rows_blocked(dw_dn0)
    p_gu0 = pair(b_gu0, r_gu0, "gu0")
    (dx1, dg_ffn_pre0), ((q_gu0,), (r_dn0,)) = _pre_bwd(dgu0, w_gu0, x1, row(norm_ffn_pre[0]), dx2, "ffn_up_bwd0", True, jobs=[
        to_chips(p_gu0, piece=first_half(p_gu0)), to_sibling(b_dn0)])
    p_dn0 = pair(b_dn0, r_dn0, "dn0")
    (dm0, do, dg_mix_post0, db_o), _ = _post_bwd(dx1, m0, row(norm_mix_post[0]), w_o, "attn", "attn_out_bwd")
    dw_o, _ = _wgrad(o, dm0, "attn_out_wgrad", ACT)
    b_o = _rows_blocked(dw_o)
    grads = {
        "norm_mix_post": jnp.concatenate([dg_mix_post0, dg_mix_post1], axis=0),
        "norm_ffn_pre": jnp.concatenate([dg_ffn_pre0, dg_ffn_pre1], axis=0),
        "norm_ffn_post": jnp.concatenate([dg_ffn_post0, dg_ffn_post1], axis=0),
        "attn_b_o": db_o,
        "sgu_w_spatial": dws,
        "sgu_b_spatial": dbs_t.T,
    }
    shares1, where1 = _pack_rows([grads[name] for name in BEFORE_ATTN], d)
    (dqkv, db_qkv, dsink), ((q_gu0,), (q_dn0,), (r_o,), (g_shares1,)) = _attn_bwd(q, kdup, vdup, do, sinks, cos, sin, jobs=[
        to_chips(p_gu0, prev=[q_gu0], piece=second_half(p_gu0)), to_chips(p_dn0), to_sibling(b_o),
        _gather_job(shares1, None, sends=[_whole(shares1)])])
    update("gu0", p_gu0, chip, q_gu0)
    update("dn0", p_dn0, chip, q_dn0)
    p_o = pair(b_o, r_o, "attn_w_o")
    grads.update({"attn_b_qkv": _fold_heads(db_qkv, d), "attn_sinks": dsink[:1, :d // HEAD_DIM]})
    shares2, where2 = _pack_rows([grads[name] for name in AFTER_ATTN] + [loss_tile[:1, :1]], d)
    (dx0, dg_mix_pre0), _ = _pre_bwd(dqkv, w_qkv, xs, row(norm_mix_pre[0]), dx1, "attn_qkv_bwd")
    grads["norm_mix_pre"] = jnp.concatenate([dg_mix_pre0, dg_mix_pre1], axis=0)
    shares3, where3 = _pack_rows([grads[name] for name in LAST], d)
    dw_qkv, ((q_o,), (g_shares1,), (g_shares2,), (g_shares3,)) = _wgrad(h0, dqkv, "attn_qkv_wgrad", jobs=[
        to_chips(p_o), _gather_job(shares1, g_shares1, forwards=[_whole(shares1)]),
        _gather_job(shares2, None, sends=[_whole(shares2)]), _gather_job(shares3, None, sends=[_whole(shares3)])])
    update("attn_w_o", p_o, chip, q_o)
    b_qkv_grad = _columns_blocked(_fold_heads(dw_qkv, d))
    (r_qkv,), (g_shares2,), (g_shares3,) = _copies_only([
        to_sibling(b_qkv_grad), _gather_job(shares2, g_shares2, forwards=[_whole(shares2)]),
        _gather_job(shares3, g_shares3, forwards=[_whole(shares3)])], "grads_tail_to_sibling")
    p_qkv = pair(b_qkv_grad, r_qkv, "attn_w_qkv")
    (q_qkv,), = _copies_only([to_chips(p_qkv)], "grads_tail_to_owner_chip")
    update("attn_w_qkv", p_qkv, chip, q_qkv)

    finish(BEFORE_ATTN, 0, g_shares1, where1, "replicated_update_before_attn")
    loss = finish(AFTER_ATTN, 1, g_shares2, where2, "replicated_update_after_attn")[0][0, 0]
    finish(LAST, 0, g_shares3, where3, "replicated_update_last")

    for out in (out_g, out_d, out_m, out_v):
        out["sgu_ln_g"], out["sgu_ln_b"] = out["sgu_ln"][:, 0, :], out["sgu_ln"][:, 1, :]
        out["ffn_w_gate_up"] = jnp.swapaxes(out["ffn_w_gate_up"], 1, 2)

    return (loss, dx0[None], *[out_g[n] for n in WEIGHTS], *[out_d[n] for n in WEIGHTS],
            *[out_m[n] for n in WEIGHTS], *[out_v[n] for n in WEIGHTS])
```

```python
import functools
import math
import operator

import jax
import jax.numpy as jnp
from jax import lax
from jax.experimental import pallas as pl
from jax.experimental.pallas import tpu as pltpu

F32 = jnp.float32
ACT = jnp.bfloat16

EPS = 1e-6
HEAD_DIM = 64
PAIR = 2 * HEAD_DIM
GQA_GROUP = 4
WINDOW = 128
ROPE_THETA = 10000.0
SCALE = HEAD_DIM ** -0.5
MASKED = -1e30
LANES = 128
MXU_WIDTH = 256

ADAM_LR, ADAM_B1, ADAM_B2, ADAM_EPS, ADAM_WD, ADAM_STEP = 0.001, 0.9, 0.999, 1e-08, 0.01, 10

N_DEV = 8
VMEM_LIMIT_BYTES = 56 * 1024 * 1024
MESH = pl.DeviceIdType.MESH
ANY = pl.BlockSpec(memory_space=pl.ANY)
IN_VMEM = pl.BlockSpec(memory_space=pltpu.VMEM)


def _pick(n, candidates):
    for c in candidates:
        if n % c == 0:
            return c
    return n


def _row_tile(t):
    return _pick(t, (512,))


def _mxu_chunks(n, most=4 * MXU_WIDTH):
    assert n % MXU_WIDTH == 0 and most % MXU_WIDTH == 0
    return [(c, min(most, n - c)) for c in range(0, n, most)]


def _params(*sem):
    return pltpu.CompilerParams(dimension_semantics=sem, vmem_limit_bytes=VMEM_LIMIT_BYTES)


def _full(shape):
    return pl.BlockSpec(shape, lambda *_: (0,) * len(shape))


def _rows(tm, n):
    return pl.BlockSpec((tm, n), lambda i: (i, 0))


def _sds(shape, dtype):
    return jax.ShapeDtypeStruct(shape, dtype)


def _place():
    return lax.axis_index("x"), lax.axis_index("y"), lax.axis_index("c")


def _other_chips(x, y):
    return [(1 - x, y), (x, 1 - y), (1 - x, 1 - y)]


class _Job:
    def __init__(self, inputs, out_shape, aliases, emit, n_remote, n_local=0):
        self.inputs, self.out_shape, self.aliases, self.emit = list(inputs), list(out_shape), dict(aliases), emit
        self.n_remote, self.n_local = n_remote, n_local


def _call(body, name, grid, in_specs, out_specs, out_shape, args, sem, scratch=(), jobs=()):
    n_in, n_out, n_scr = len(args), len(out_shape), len(scratch)
    j_in = [a for job in jobs for a in job.inputs]
    j_out = [s for job in jobs for s in job.out_shape]
    aliases, at_in, at_out = {}, n_in, n_out
    for job in jobs:
        aliases.update({at_in + i: at_out + o for i, o in job.aliases.items()})
        at_in, at_out = at_in + len(job.inputs), at_out + len(job.out_shape)
    n_remote = sum(job.n_remote for job in jobs)
    n_local = sum(job.n_local for job in jobs)

    def wrapped(*refs):
        ins, jin = refs[:n_in], refs[n_in:n_in + len(j_in)]
        rest = refs[n_in + len(j_in):]
        outs, jout = rest[:n_out], rest[n_out:n_out + len(j_out)]
        scr = rest[n_out + len(j_out):n_out + len(j_out) + n_scr]
        if not jobs:
            body(*ins, *outs, *scr)
            return
        send, recv, local = rest[n_out + len(j_out) + n_scr:]
        ids = [pl.program_id(a) for a in range(len(grid))]
        first = functools.reduce(operator.and_, [i == 0 for i in ids])
        last = functools.reduce(operator.and_, [i == g - 1 for i, g in zip(ids, grid)])

        def descriptors():
            place, found, i, o, r, l = _place(), [], 0, 0, 0, 0
            for job in jobs:
                for src, dst, to in job.emit(jin[i:i + len(job.inputs)], jout[o:o + len(job.out_shape)], place):
                    if to is None:
                        found.append(pltpu.make_async_copy(src, dst, local.at[l]))
                        l += 1
                    else:
                        found.append(pltpu.make_async_remote_copy(src_ref=src, dst_ref=dst, send_sem=send.at[r], recv_sem=recv.at[r],
                                                                  device_id=to, device_id_type=MESH))
                        r += 1
                i, o = i + len(job.inputs), o + len(job.out_shape)
            return found

        @pl.when(first)
        def _():
            for cp in descriptors():
                cp.start()

        body(*ins, *outs, *scr)

        @pl.when(last)
        def _():
            for cp in descriptors():
                cp.wait()

    sems = [pltpu.SemaphoreType.DMA((n_remote,)), pltpu.SemaphoreType.DMA((n_remote,)),
            pltpu.SemaphoreType.DMA((max(n_local, 1),))] if jobs else []
    res = pl.pallas_call(
        wrapped, name=name, grid=grid,
        in_specs=list(in_specs) + [ANY] * len(j_in), out_specs=list(out_specs) + [ANY] * len(j_out),
        out_shape=list(out_shape) + j_out, scratch_shapes=list(scratch) + sems, input_output_aliases=aliases,
        compiler_params=_params(*(("arbitrary",) * len(grid) if jobs else sem)),
    )(*args, *j_in)
    job_res, at = [], n_out
    for job in jobs:
        job_res.append(list(res[at:at + len(job.out_shape)]))
        at += len(job.out_shape)
    return list(res[:n_out]), job_res


def _copies_only(jobs, name):
    def body(o_ref):
        o_ref[...] = jnp.zeros_like(o_ref)

    return _call(body, name, (1,), [], [_full((8, LANES))], [_sds((8, LANES), F32)], (), ("arbitrary",), jobs=jobs)[1]


def _gather_job(stage, gathered, sends=(), forwards=()):
    shape = _sds((N_DEV,) + stage.shape, stage.dtype)
    inputs = ([stage] if sends else []) + ([gathered] if gathered is not None else [])
    aliases = {len(inputs) - 1: 0} if gathered is not None else {}

    def emit(ins, outs, place):
        x, y, c = place
        sibling, chips, mine, g = (x, y, 1 - c), _other_chips(x, y), 4 * x + 2 * y + c, outs[0]
        found = []
        for r0, nr in sends:
            src, dst = ins[0].at[pl.ds(r0, nr)], g.at[mine, pl.ds(r0, nr)]
            found += [(src, dst, None), (src, dst, sibling)] + [(src, dst, (px, py, c)) for px, py in chips]
        for r0, nr in forwards:
            for px, py in chips:
                block = 4 * px + 2 * py + c
                found.append((ins[-1].at[block, pl.ds(r0, nr)], g.at[block, pl.ds(r0, nr)], sibling))
        return found

    return _Job(inputs, [shape], aliases, emit, 4 * len(sends) + 3 * len(forwards), len(sends))


def _relay_gather_job(stage, gathered, sends=(), relays=(), forwards=(), far=()):
    shape = _sds((N_DEV,) + stage.shape, stage.dtype)
    inputs = ([stage] if sends else []) + ([gathered] if gathered is not None else [])
    aliases = {len(inputs) - 1: 0} if gathered is not None else {}

    def emit(ins, outs, place):
        x, y, c = place
        sibling, beside_x, beside_y, g = (x, y, 1 - c), (1 - x, y, c), (x, 1 - y, c), outs[0]
        slot = lambda dev: 4 * dev[0] + 2 * dev[1] + dev[2]
        found = []
        for r0, nr in sends:
            src, dst = ins[0].at[pl.ds(r0, nr)], g.at[slot((x, y, c)), pl.ds(r0, nr)]
            found += [(src, dst, None), (src, dst, sibling), (src, dst, beside_x), (src, dst, beside_y)]

        def passed_on(block, piece, to):
            return ins[-1].at[block, pl.ds(*piece)], g.at[block, pl.ds(*piece)], to

        for piece, way in relays:
            found.append(passed_on(slot(beside_x), piece, beside_y) if way == 0 else passed_on(slot(beside_y), piece, beside_x))
        for piece in forwards:
            found += [passed_on(slot(beside_x), piece, sibling), passed_on(slot(beside_y), piece, sibling)]
        for piece in far:
            found.append(passed_on(slot((1 - x, 1 - y, c)), piece, sibling))
        return found

    return _Job(inputs, [shape], aliases, emit, 3 * len(sends) + len(relays) + 2 * len(forwards) + len(far), len(sends))


def _scatter_job(arrays, n_slots, route, prev=None, piece=None):
    shapes = [_sds((n_slots,) + v.shape[1:], v.dtype) for v in arrays]
    inputs = list(arrays) + (list(prev) if prev else [])
    aliases = {len(arrays) + i: i for i in range(len(arrays))} if prev else {}

    def emit(ins, outs, place):
        found = []
        for a in range(len(arrays)):
            for s in range(n_slots):
                block, to = route(s, *place)
                if piece is None:
                    found.append((ins[a].at[block], outs[a].at[s], to))
                else:
                    found.append((ins[a].at[block, pl.ds(*piece)], outs[a].at[s, pl.ds(*piece)], to))
        return found

    return _Job(inputs, shapes, aliases, emit, len(arrays) * n_slots)


def _to_sibling(s, x, y, c):
    return 2 * s + (1 - c), (x, y, 1 - c)


def _to_owner_chip(s, x, y, c):
    px, py = _other_chips(x, y)[s]
    return 2 * px + py, (px, py, c)


def _rstd(x):
    return lax.rsqrt(jnp.mean(x * x, axis=-1, keepdims=True) + EPS)


def _rms_bwd(xhat, r, g, dy):
    dxh = dy * g
    return r * (dxh - xhat * jnp.mean(dxh * xhat, axis=-1, keepdims=True))


def _first_half(rows):
    lane = lax.broadcasted_iota(jnp.int32, (rows, PAIR), 1)
    return (lane % HEAD_DIM) < (HEAD_DIM // 2)


def _rot_half(v, first):
    return jnp.where(first, pltpu.roll(v, PAIR - HEAD_DIM // 2, 1), pltpu.roll(v, HEAD_DIM // 2, 1))


def _rope(v, cos, sin, first):
    return v * cos + _rot_half(v, first) * sin


def _rope_t(dv, cos, sin, first):
    return dv * cos + _rot_half(dv * sin, first)


def _dot(a, b):
    return jnp.dot(a, b, preferred_element_type=F32)


def _dot_nt(a, b):
    return lax.dot_general(a, b, (((1,), (1,)), ((), ())), preferred_element_type=F32)


def _dot_tn(a, b):
    return lax.dot_general(a, b, (((0,), (0,)), ((), ())), preferred_element_type=F32)


def _sigmoid(v):
    return 1.0 / (1.0 + jnp.exp(-v))


_GELU_K = math.sqrt(2.0 / math.pi)
_GELU_C = 0.044715


def _gelu(v):
    return v * (0.5 * (1.0 + jnp.tanh(_GELU_K * (v + _GELU_C * (v * v * v)))))


def _gelu_grad(v):
    t = jnp.tanh(_GELU_K * (v + _GELU_C * (v * v * v)))
    return 0.5 * (1.0 + t) + 0.5 * v * (1.0 - t * t) * (_GELU_K * (1.0 + 3.0 * _GELU_C * (v * v)))


def _attn_qkv_fwd(x, g, w, b, cos, sin, jobs=()):
    t, d = x.shape
    n = w.shape[1]
    kd = (n - d) // 2
    tm = _row_tile(t)
    ch = _pick(d, (512,))

    def body(x_ref, g_ref, w_ref, b_ref, cos_ref, sin_ref, h_ref, q_ref, k_ref, v_ref):
        xv = x_ref[...]
        hb = (xv * _rstd(xv) * g_ref[...]).astype(ACT)
        h_ref[...] = hb
        cosv, sinv = cos_ref[...], sin_ref[...]
        first = _first_half(tm)

        def proj(lo, width):
            return _dot(hb, w_ref[:, lo:lo + width]) + b_ref[:, lo:lo + width]

        for c in range(0, d, ch):
            y = proj(c, ch)
            for s in range(0, ch, PAIR):
                q_ref[:, c + s:c + s + PAIR] = (_rope(y[:, s:s + PAIR], cosv, sinv, first) * SCALE).astype(ACT)
        y = proj(d, kd)
        for s in range(0, kd, PAIR):
            k_ref[:, s:s + PAIR] = _rope(y[:, s:s + PAIR], cosv, sinv, first).astype(ACT)
        v_ref[...] = proj(d + kd, kd).astype(ACT)

    return _call(
        body, "attn_qkv_fwd", (t // tm,),
        [_rows(tm, d), _full((1, d)), _full((d, n)), _full((1, n)), _rows(tm, PAIR), _rows(tm, PAIR)],
        [_rows(tm, d), _rows(tm, d), _rows(tm, kd), _rows(tm, kd)],
        [_sds((t, d), ACT), _sds((t, d), ACT), _sds((t, kd), ACT), _sds((t, kd), ACT)],
        (x, g, w, b, cos, sin), ("parallel",), jobs=jobs)


STACK = GQA_GROUP * WINDOW


def _band_mask(has_prev):
    row = lax.broadcasted_iota(jnp.int32, (STACK, 2 * WINDOW), 0) % WINDOW
    col = lax.broadcasted_iota(jnp.int32, (STACK, 2 * WINDOW), 1)
    return ((col < WINDOW) & (col > row) & has_prev) | ((col >= WINDOW) & (col - WINDOW <= row))


def _lane_halves():
    lane = lax.broadcasted_iota(jnp.int32, (1, PAIR), 1)
    return lane < HEAD_DIM, lane >= HEAD_DIM


def _stack_heads(ref, h, halves):
    parts = []
    for p in range(2):
        pair = ref[:, (2 * h + p) * PAIR:(2 * h + p + 1) * PAIR]
        parts += [jnp.where(half, pair, jnp.zeros_like(pair)) for half in halves]
    return jnp.concatenate(parts, axis=0)


def _sink_column(sink_ref, h):
    row = lax.broadcasted_iota(jnp.int32, (STACK, 1), 0)
    col = jnp.full((STACK, 1), sink_ref[0, GQA_GROUP * h + GQA_GROUP - 1], F32)
    for j in range(GQA_GROUP - 2, -1, -1):
        col = jnp.where(row < (j + 1) * WINDOW, sink_ref[0, GQA_GROUP * h + j], col)
    return col


def _probs(scores, valid, sink):
    s = jnp.where(valid, scores, MASKED)
    m = jnp.maximum(jnp.max(s, axis=-1, keepdims=True), sink)
    p = jnp.exp(s - m)
    psink = jnp.exp(sink - m)
    inv = 1.0 / (jnp.sum(p, axis=-1, keepdims=True) + psink)
    return p * inv, psink * inv


def _attn_fwd(q, kd_, vd, sinks, jobs=()):
    t, d = q.shape
    kd = kd_.shape[1]
    cur = lambda n: (n, 0)
    prev = lambda n: (jnp.maximum(n - 1, 0), 0)

    def body(sink_ref, q_ref, kc_ref, kp_ref, vc_ref, vp_ref, o_ref):
        valid = _band_mask(pl.program_id(0) > 0)
        halves = _lane_halves()
        heads = range(kd // PAIR)
        hs = [slice(h * PAIR, (h + 1) * PAIR) for h in heads]
        scores = [_dot_nt(_stack_heads(q_ref, h, halves), jnp.concatenate([kp_ref[:, hs[h]], kc_ref[:, hs[h]]], axis=0)) for h in heads]
        probs = [_probs(scores[h], valid, _sink_column(sink_ref, h))[0].astype(ACT) for h in heads]
        for h in heads:
            v2 = jnp.concatenate([vp_ref[:, hs[h]], vc_ref[:, hs[h]]], axis=0)
            vs = jnp.concatenate([jnp.where(half, v2, jnp.zeros_like(v2)) for half in halves], axis=0)
            pr = probs[h]
            for p in range(2):
                both = jnp.concatenate([pr[2 * p * WINDOW:(2 * p + 1) * WINDOW], pr[(2 * p + 1) * WINDOW:(2 * p + 2) * WINDOW]], axis=1)
                o_ref[:, (2 * h + p) * PAIR:(2 * h + p + 1) * PAIR] = _dot(both, vs).astype(ACT)

    kv_c, kv_p = pl.BlockSpec((WINDOW, kd), cur), pl.BlockSpec((WINDOW, kd), prev)
    return _call(
        body, "attn_fwd", (t // WINDOW,),
        [pl.BlockSpec(memory_space=pltpu.SMEM), pl.BlockSpec((WINDOW, d), cur), kv_c, kv_p, kv_c, kv_p],
        [pl.BlockSpec((WINDOW, d), cur)], [_sds((t, d), ACT)],
        (sinks, q, kd_, kd_, vd, vd), ("parallel",), jobs=jobs)


def _attn_bwd(q, kd_, vd, do, sinks, cos, sin, jobs=()):
    t, d = q.shape
    kd = kd_.shape[1]
    nb = t // WINDOW
    n_out = d + 2 * kd
    cur = lambda n: (jnp.minimum(n, nb - 1), 0)
    prev = lambda n: (jnp.maximum(jnp.minimum(n, nb - 1) - 1, 0), 0)
    late = lambda n: (jnp.maximum(n - 1, 0), 0)

    def body(sink_ref, q_ref, kc_ref, kp_ref, vc_ref, vp_ref, do_ref, cosc_ref, sinc_ref, cosp_ref, sinp_ref,
             out_ref, db_ref, dsink_ref, dq_keep, dk_keep, dv_keep):
        n = pl.program_id(0)

        @pl.when(n == 0)
        def _():
            for ref in (db_ref, dsink_ref, dq_keep, dk_keep, dv_keep):
                ref[...] = jnp.zeros_like(ref)

        valid = _band_mask(n > 0) & (n < nb)
        halves = _lane_halves()
        first = _first_half(WINDOW)
        slot = lax.broadcasted_iota(jnp.int32, dsink_ref.shape, 1)
        top = lax.broadcasted_iota(jnp.int32, dsink_ref.shape, 0) == 0
        dsink = jnp.zeros(dsink_ref.shape, F32)

        def emit(cols, done):
            out_ref[:, cols] = done.astype(ACT)
            db_ref[:, cols] += jnp.sum(done.astype(F32), axis=0, keepdims=True)

        heads = range(kd // PAIR)
        hs = [slice(h * PAIR, (h + 1) * PAIR) for h in heads]
        k2 = [jnp.concatenate([kp_ref[:, hs[h]], kc_ref[:, hs[h]]], axis=0) for h in heads]
        v2 = [jnp.concatenate([vp_ref[:, hs[h]], vc_ref[:, hs[h]]], axis=0) for h in heads]
        qs = [_stack_heads(q_ref, h, halves) for h in heads]
        dos = [_stack_heads(do_ref, h, halves) for h in heads]
        scores = [_dot_nt(qs[h], k2[h]) for h in heads]
        dps = [_dot_nt(dos[h], v2[h]) for h in heads]
        prs, dss = [], []
        for h in heads:
            pr, psink = _probs(scores[h], valid, _sink_column(sink_ref, h))
            delta = jnp.sum(pr * dps[h], axis=-1, keepdims=True)
            dss.append((pr * (dps[h] - delta)).astype(ACT))
            prs.append(pr.astype(ACT))
            leak = psink * delta
            for j in range(GQA_GROUP):
                share = jnp.sum(leak[j * WINDOW:(j + 1) * WINDOW], axis=0, keepdims=True)
                dsink = dsink - jnp.where(top & (slot == GQA_GROUP * h + j), share, 0.0)
        dqs = [_dot(dss[h], k2[h]) for h in heads]
        dk2 = [_dot_tn(dss[h], qs[h]) for h in heads]
        dv2 = [_dot_tn(prs[h], dos[h]) for h in heads]
        for h in heads:
            for p in range(2):
                ps = slice((2 * h + p) * PAIR, (2 * h + p + 1) * PAIR)
                dqp = jnp.where(halves[0], dqs[h][2 * p * WINDOW:(2 * p + 1) * WINDOW], dqs[h][(2 * p + 1) * WINDOW:(2 * p + 2) * WINDOW])
                emit(ps, dq_keep[:, ps])
                dq_keep[:, ps] = (_rope_t(dqp, cosc_ref[...], sinc_ref[...], first) * SCALE).astype(ACT)
            ks = slice(d + h * PAIR, d + (h + 1) * PAIR)
            vs = slice(d + kd + h * PAIR, d + kd + (h + 1) * PAIR)
            emit(ks, dk_keep[:, hs[h]] + _rope_t(dk2[h][:WINDOW], cosp_ref[...], sinp_ref[...], first))
            dk_keep[:, hs[h]] = _rope_t(dk2[h][WINDOW:], cosc_ref[...], sinc_ref[...], first)
            emit(vs, dv_keep[:, hs[h]] + dv2[h][:WINDOW])
            dv_keep[:, hs[h]] = dv2[h][WINDOW:]
        dsink_ref[...] += dsink

    kv_c, kv_p = pl.BlockSpec((WINDOW, kd), cur), pl.BlockSpec((WINDOW, kd), prev)
    tab_c, tab_p = pl.BlockSpec((WINDOW, PAIR), cur), pl.BlockSpec((WINDOW, PAIR), prev)
    return _call(
        body, "attn_bwd", (nb + 1,),
        [pl.BlockSpec(memory_space=pltpu.SMEM), pl.BlockSpec((WINDOW, d), cur), kv_c, kv_p, kv_c, kv_p,
         pl.BlockSpec((WINDOW, d), cur), tab_c, tab_c, tab_p, tab_p],
        [pl.BlockSpec((WINDOW, n_out), late), _full((1, n_out)), _full((8, LANES))],
        [_sds((t, n_out), ACT), _sds((1, n_out), F32), _sds((8, LANES), F32)],
        (sinks, q, kd_, kd_, vd, vd, do, cos, sin, cos, sin), ("arbitrary",),
        scratch=[pltpu.VMEM((WINDOW, d), ACT), pltpu.VMEM((WINDOW, kd), F32), pltpu.VMEM((WINDOW, kd), F32)], jobs=jobs)


def _proj_res(a, w, b, g, x, name, target=None, jobs=()):
    t = a.shape[0]
    k, d = w.shape
    tm = _row_tile(t)
    has_bias = b is not None

    def body(*refs):
        a_ref, w_ref = refs[:2]
        b_ref = refs[2] if has_bias else None
        g_ref, x_ref, m_ref, xo_ref = refs[2 + has_bias:]
        m = _dot(a_ref[...], w_ref[...])
        if has_bias:
            m = m + b_ref[...]
        m_ref[...] = m.astype(ACT)
        xo_ref[...] = x_ref[...] + (m * _rstd(m)) * g_ref[...]

    def body_with_loss(a_ref, w_ref, g_ref, x_ref, t_ref, m_ref, dy_ref, loss_ref):
        @pl.when(pl.program_id(0) == 0)
        def _():
            loss_ref[...] = jnp.zeros_like(loss_ref)

        body(a_ref, w_ref, g_ref, x_ref, m_ref, dy_ref)
        err = dy_ref[...] - t_ref[...]
        dy_ref[...] = err * (1.0 / d)
        loss_ref[...] += 0.5 * jnp.sum(jnp.mean(err * err, axis=-1, keepdims=True), axis=0, keepdims=True)

    ins = [a, w] + ([b] if has_bias else []) + [g, x]
    specs = [_rows(tm, k), _full((k, d))] + ([_full((1, d))] if has_bias else []) + [_full((1, d)), _rows(tm, d)]
    if target is not None:
        return _call(body_with_loss, name, (t // tm,), specs + [_rows(tm, d)], [_rows(tm, d), _rows(tm, d), _full((8, LANES))],
                     [_sds((t, d), ACT), _sds((t, d), F32), _sds((8, LANES), F32)], ins + [target], ("arbitrary",), jobs=jobs)
    return _call(body, name, (t // tm,), specs, [_rows(tm, d), _rows(tm, d)], [_sds((t, d), ACT), _sds((t, d), F32)], ins,
                 ("parallel",), jobs=jobs)


def _post_bwd(dres, m, g, w, mode, name, gu=None, jobs=()):
    t, d = dres.shape
    k = w.shape[0]
    tm = _row_tile(t)
    kc = _pick(k, (1408, 1024, 512))

    def body(*refs):
        d_ref, m_ref, g_ref, w_ref = refs[:4]
        gu_ref = refs[4] if mode == "ffn" else None
        outs = refs[4 + (mode == "ffn"):]
        dm_ref, da_ref, dg_ref = outs[:3]
        i = pl.program_id(0)

        @pl.when(i == 0)
        def _():
            dg_ref[...] = jnp.zeros_like(dg_ref)
            if mode == "attn":
                outs[3][...] = jnp.zeros_like(outs[3])

        dv, mv = d_ref[...], m_ref[...].astype(F32)
        r = _rstd(mv)
        mh = mv * r
        dg_ref[...] += jnp.sum(dv * mh, axis=0, keepdims=True)
        dm = _rms_bwd(mh, r, g_ref[...], dv)
        dmb = dm.astype(ACT)
        dm_ref[...] = dmb
        if mode == "attn":
            outs[3][...] += jnp.sum(dm, axis=0, keepdims=True)
        if mode == "ffn":
            for c, size in _mxu_chunks(k):
                da = _dot_nt(dmb, w_ref[c:c + size, :]).astype(ACT)
                gate, up = gu_ref[:, c:c + size], gu_ref[:, k + c:k + c + size]
                sg = _sigmoid(gate.astype(F32)).astype(ACT)
                gs = gate * sg
                da_ref[:, c:c + size] = da * up * (sg + gs - gs * sg)
                da_ref[:, k + c:k + c + size] = da * gs
        else:
            for c in range(0, k, kc):
                da = _dot_nt(dmb, w_ref[c:c + kc, :])
                da_ref[:, c:c + kc] = da.astype(ACT)

    ins = [dres, m, g, w]
    specs = [_rows(tm, d), _rows(tm, d), _full((1, d)), _full((k, d))]
    if mode == "ffn":
        ins.append(gu)
        specs.append(_rows(tm, 2 * k))
        out_specs = [_rows(tm, d), _rows(tm, 2 * k), _full((1, d))]
        out_shape = [_sds((t, d), ACT), _sds(gu.shape, ACT), _sds((1, d), F32)]
    else:
        out_specs = [_rows(tm, d), _rows(tm, k), _full((1, d))]
        out_shape = [_sds((t, d), ACT), _sds((t, k), ACT), _sds((1, d), F32)]
    if mode == "attn":
        out_specs.append(_full((1, d)))
        out_shape.append(_sds((1, d), F32))
    return _call(body, name, (t // tm,), specs, out_specs, out_shape, ins, ("arbitrary",), jobs=jobs)


def _pre_bwd(dy, w, x, g, dres, name, transposed=False, jobs=()):
    t, d = x.shape
    tm = _row_tile(t)

    def body(dy_ref, w_ref, x_ref, g_ref, dres_ref, dx_ref, dg_ref):
        @pl.when(pl.program_id(0) == 0)
        def _():
            dg_ref[...] = jnp.zeros_like(dg_ref)

        dh = jnp.zeros((tm, d), F32)
        if transposed:
            dh = dh + _dot(dy_ref[...], w_ref[...])
        elif w.ndim == 3:
            c = w.shape[2]
            for j in range(w.shape[0]):
                dh = dh + _dot_nt(dy_ref[:, j * c:(j + 1) * c], w_ref[j])
        else:
            n = w.shape[1]
            nc = _pick(n, (1408, 1024, 512))
            for c in range(0, n, nc):
                dh = dh + _dot_nt(dy_ref[:, c:c + nc], w_ref[:, c:c + nc])
        xv = x_ref[...]
        r = _rstd(xv)
        xh = xv * r
        dg_ref[...] += jnp.sum(dh * xh, axis=0, keepdims=True)
        dx_ref[...] = dres_ref[...] + _rms_bwd(xh, r, g_ref[...], dh)

    return _call(
        body, name, (t // tm,),
        [_rows(tm, dy.shape[1]), _full(w.shape), _rows(tm, d), _full((1, d)), _rows(tm, d)],
        [_rows(tm, d), _full((1, d))], [_sds((t, d), F32), _sds((1, d), F32)],
        (dy, w, x, g, dres), ("arbitrary",), jobs=jobs)


def _wgrad(a, b, name, out_dtype=F32, out_block=None, transposed=False, jobs=()):
    t = a.shape[0]
    tt = _pick(t, (1024,))
    steps = t // tt
    tk = _pick(a.shape[1], (1024, 1408))
    k, n = a.shape[1], b.shape[1]
    tn = _pick(n, (1024, 1408))
    per = 0
    if transposed:
        out_spec, out_shape, acc_shape = pl.BlockSpec((tn, tk), lambda i, j, s: (j, i)), _sds((n, k), out_dtype), (tn, tk)
    elif out_block is None:
        out_spec, out_shape, acc_shape = pl.BlockSpec((tk, tn), lambda i, j, s: (i, j)), _sds((k, n), out_dtype), (tk, tn)
    else:
        per = tn // out_block
        out_spec = pl.BlockSpec((per, tk, out_block), lambda i, j, s: (j, i, 0))
        out_shape, acc_shape = _sds((n // out_block, k, out_block), out_dtype), (tk, tn)

    def body(a_ref, b_ref, o_ref, acc_ref):
        s = pl.program_id(2)

        @pl.when(s == 0)
        def _():
            acc_ref[...] = jnp.zeros_like(acc_ref)

        acc_ref[...] += _dot_tn(b_ref[...], a_ref[...]) if transposed else _dot_tn(a_ref[...], b_ref[...])

        @pl.when(s == steps - 1)
        def _():
            if per >= 1:
                for p in range(per):
                    o_ref[p] = acc_ref[:, p * out_block:(p + 1) * out_block].astype(out_dtype)
            else:
                o_ref[...] = acc_ref[...].astype(out_dtype)

    res, job_res = _call(
        body, name, (k // tk, n // tn, steps),
        [pl.BlockSpec((tt, tk), lambda i, j, s: (s, i)), pl.BlockSpec((tt, tn), lambda i, j, s: (s, j))], [out_spec], [out_shape],
        (a, b), ("parallel", "parallel", "arbitrary"), scratch=[pltpu.VMEM(acc_shape, F32)], jobs=jobs)
    return res[0], job_res


def _ffn_up_fwd(x, g, w, jobs=()):
    t, d = x.shape
    n = w.shape[0] // 2
    tm = _row_tile(t)

    def body(x_ref, g_ref, w_ref, h_ref, gu_ref, a_ref):
        xv = x_ref[...]
        hb = (xv * _rstd(xv) * g_ref[...]).astype(ACT)
        h_ref[...] = hb
        for c, size in _mxu_chunks(n):
            gate = _dot_nt(hb, w_ref[c:c + size, :])
            up = _dot_nt(hb, w_ref[n + c:n + c + size, :])
            gu_ref[:, c:c + size] = gate.astype(ACT)
            gu_ref[:, n + c:n + c + size] = up.astype(ACT)
            a_ref[:, c:c + size] = (gate * _sigmoid(gate) * up).astype(ACT)

    return _call(
        body, "ffn_up_fwd", (t // tm,),
        [_rows(tm, d), _full((1, d)), _full(w.shape)],
        [_rows(tm, d), _rows(tm, 2 * n), _rows(tm, n)],
        [_sds((t, d), ACT), _sds((t, 2 * n), ACT), _sds((t, n), ACT)],
        (x, g, w), ("parallel",), jobs=jobs)


def _causal(ws):
    row = lax.broadcasted_iota(jnp.int32, ws.shape, 0)
    col = lax.broadcasted_iota(jnp.int32, ws.shape, 1)
    return jnp.where(col <= row, ws, 0.0)


def _sgu_ln(v, lg, lb):
    mu = jnp.mean(v, axis=-1, keepdims=True)
    vc = v - mu
    rs = lax.rsqrt(jnp.mean(vc * vc, axis=-1, keepdims=True) + EPS)
    vh = vc * rs
    return vh, rs, vh * lg + lb


def _sgu_fwd(x, g, w, lg, lb, ws, bs_t, jobs=()):
    t, d = x.shape
    nb, _, c = w.shape
    n = nb * c
    groups = d // WINDOW
    tm = _row_tile(t)

    def body(x_ref, g_ref, w_ref, lg_ref, lb_ref, ws_ref, bs_ref, h_ref, z_ref, y_ref, zf_ref):
        xv = x_ref[...]
        hb = (xv * _rstd(xv) * g_ref[...]).astype(ACT)
        h_ref[...] = hb
        for j in range(nb):
            zf_ref[:, j * c:(j + 1) * c] = _dot(hb, w_ref[j])
        z_ref[...] = zf_ref[...].astype(ACT)
        gs = [slice(gi * WINDOW, (gi + 1) * WINDOW) for gi in range(groups)]
        wsg = [_causal(ws_ref[gi]).astype(ACT) for gi in range(groups)]
        for r in range(0, tm, WINDOW):
            u = _gelu(zf_ref[r:r + WINDOW, :d])
            _, _, vn = _sgu_ln(_gelu(zf_ref[r:r + WINDOW, d:]), lg_ref[...], lb_ref[...])
            mixed = [_dot(wsg[gi], vn[:, gs[gi]].astype(ACT)) for gi in range(groups)]
            for gi in range(groups):
                y_ref[r:r + WINDOW, gs[gi]] = (u[:, gs[gi]] * (mixed[gi] + bs_ref[:, gi:gi + 1])).astype(ACT)

    vec = _full((1, d))
    return _call(
        body, "sgu_fwd", (t // tm,),
        [_rows(tm, d), vec, _full((nb, d, c)), vec, vec, _full((groups, WINDOW, WINDOW)), _full((WINDOW, groups))],
        [_rows(tm, d), _rows(tm, n), _rows(tm, d)], [_sds((t, d), ACT), _sds((t, n), ACT), _sds((t, d), ACT)],
        (x, g, w, lg, lb, ws, bs_t), ("parallel",), scratch=[pltpu.VMEM((tm, n), F32)], jobs=jobs)


def _sgu_mix_bwd(z, dy, lg, lb, ws, bs_t, jobs=()):
    t, n = z.shape
    d = n // 2
    groups = d // WINDOW
    tm = _row_tile(t)

    def body(z_ref, dy_ref, lg_ref, lb_ref, ws_ref, bs_ref, dz_ref, dlg_ref, dlb_ref, dws_ref, dbs_ref):
        @pl.when(pl.program_id(0) == 0)
        def _():
            for ref in (dlg_ref, dlb_ref, dws_ref, dbs_ref):
                ref[...] = jnp.zeros_like(ref)

        lane = lax.broadcasted_iota(jnp.int32, (WINDOW, groups), 1)
        gs = [slice(gi * WINDOW, (gi + 1) * WINDOW) for gi in range(groups)]
        wsg = [_causal(ws_ref[gi]).astype(ACT) for gi in range(groups)]
        for c in range(0, tm, WINDOW):
            rows = slice(c, c + WINDOW)
            zu, zv = z_ref[rows, :d].astype(F32), z_ref[rows, d:].astype(F32)
            u = _gelu(zu)
            vh, rs, vn = _sgu_ln(_gelu(zv), lg_ref[...], lb_ref[...])
            dyv = dy_ref[rows, :].astype(F32)
            vng = [vn[:, gs[gi]].astype(ACT) for gi in range(groups)]
            dmix = dyv * u
            dmb = [dmix[:, gs[gi]].astype(ACT) for gi in range(groups)]
            mixed = [_dot(wsg[gi], vng[gi]) for gi in range(groups)]
            dvn_parts = [_dot_tn(wsg[gi], dmb[gi]) for gi in range(groups)]
            dws_parts = [_dot_nt(dmb[gi], vng[gi]) for gi in range(groups)]
            for gi in range(groups):
                dz_ref[rows, gs[gi]] = (dyv[:, gs[gi]] * (mixed[gi] + bs_ref[:, gi:gi + 1]) * _gelu_grad(zu[:, gs[gi]])).astype(ACT)
                dws_ref[:, gs[gi]] += _causal(dws_parts[gi])
                dbs_ref[...] += jnp.where(lane == gi, jnp.sum(dmix[:, gs[gi]], axis=-1, keepdims=True), 0.0)
            dvn = jnp.concatenate(dvn_parts, axis=-1)
            dlg_ref[...] += jnp.sum(dvn * vh, axis=0, keepdims=True)
            dlb_ref[...] += jnp.sum(dvn, axis=0, keepdims=True)
            dvh = dvn * lg_ref[...]
            dv = rs * (dvh - jnp.mean(dvh, axis=-1, keepdims=True) - vh * jnp.mean(dvh * vh, axis=-1, keepdims=True))
            dz_ref[rows, d:] = (dv * _gelu_grad(zv)).astype(ACT)

    vec = _full((1, d))
    return _call(
        body, "sgu_mix_bwd", (t // tm,),
        [_rows(tm, n), _rows(tm, d), vec, vec, _full((groups, WINDOW, WINDOW)), _full((WINDOW, groups))],
        [_rows(tm, n), vec, vec, _full((WINDOW, d)), _full((WINDOW, groups))],
        [_sds((t, n), ACT), _sds((1, d), F32), _sds((1, d), F32), _sds((WINDOW, d), F32), _sds((WINDOW, groups), F32)],
        (z, dy, lg, lb, ws, bs_t), ("arbitrary",), jobs=jobs)


AG_COPIES = 8


def _prepare(sources, dtypes, n_gather, name):
    n = len(sources)
    arrays, where = [], []
    for parts, layer in sources:
        found = []
        for part in parts:
            known = [i for i, v in enumerate(arrays) if v is part]
            if not known:
                arrays.append(part)
            found.append(known[0] if known else len(arrays) - 1)
        where.append((found, layer))
    shapes = [(sum(p.shape[1] for p in parts), parts[0].shape[2]) for parts, _ in sources]

    def body(*refs):
        ins, refs = refs[:len(arrays)], refs[len(arrays):]
        stage, outs = refs[:n], refs[n:n + n_gather]
        send, recv, local_sem = refs[n + n_gather:]
        x, y, c = _place()
        me, sibling, beside_x, beside_y, far = (x, y, c), (x, y, 1 - c), (1 - x, y, c), (x, 1 - y, c), (1 - x, 1 - y, c)
        slot = lambda dev: 4 * dev[0] + 2 * dev[1] + dev[2]
        other = lambda dev: (dev[0], dev[1], 1 - c)
        whole = lambda a: (0, shapes[a][0])
        cuts = [rows // 2 if rows % 32 == 0 else rows for rows, _ in shapes]
        first = lambda a: (0, cuts[a])
        rest = lambda a: (cuts[a], shapes[a][0] - cuts[a])

        def copy(a, k, block, rows, to, src=None):
            dst = outs[a].at[block, pl.ds(*rows)]
            return pltpu.make_async_remote_copy(
                src_ref=dst if src is None else src, dst_ref=dst, send_sem=send.at[a * AG_COPIES + k],
                recv_sem=recv.at[a * AG_COPIES + k], device_id=to, device_id_type=MESH)

        def cast(a):
            found, layer = where[a]
            at = 0
            for i in found:
                rows = arrays[i].shape[1]
                stage[a][at:at + rows, :] = ins[i][layer].astype(dtypes[a])
                at += rows

        for a in range(n_gather):
            cast(a)
        started = []
        for a in range(n_gather):
            own = pltpu.make_async_copy(stage[a], outs[a].at[slot(me)], local_sem.at[a])
            own.start()
            started.append(own)
        sends = []
        for a in range(n_gather):
            sends += [copy(a, k, slot(me), whole(a), to, src=stage[a]) for k, to in enumerate((sibling, beside_x, beside_y))]
        for cp in sends:
            cp.start()
        for a in range(n_gather, n):
            cast(a)
        for a in range(n_gather):
            copy(a, 1, slot(beside_x), whole(a), me).wait_recv()
            copy(a, 2, slot(beside_y), whole(a), me).wait_recv()
            passed = [copy(a, 3, slot(beside_x), first(a), beside_y), copy(a, 5, slot(beside_x), whole(a), sibling),
                      copy(a, 6, slot(beside_y), whole(a), sibling)]
            if rest(a)[1]:
                passed.append(copy(a, 4, slot(beside_y), rest(a), beside_x))
            for cp in passed:
                cp.start()
            sends += passed
        for a in range(n_gather):
            copy(a, 3, slot(far), first(a), me).wait_recv()
            if rest(a)[1]:
                copy(a, 4, slot(far), rest(a), me).wait_recv()
            passed = copy(a, 7, slot(far), whole(a), sibling)
            passed.start()
            sends.append(passed)
        for a in range(n_gather):
            for k, source in ((0, me), (5, beside_x), (6, beside_y), (7, far)):
                copy(a, k, slot(other(source)), whole(a), me).wait_recv()
        for cp in sends:
            cp.wait_send()
        for own in started:
            own.wait()

    res = pl.pallas_call(
        body, name=name,
        in_specs=[IN_VMEM] * len(arrays), out_specs=[IN_VMEM] * n + [ANY] * n_gather,
        out_shape=[_sds(s, dt) for s, dt in zip(shapes, dtypes)]
        + [_sds((N_DEV,) + s, dt) for s, dt in zip(shapes[:n_gather], dtypes)],
        scratch_shapes=[pltpu.SemaphoreType.DMA((n_gather * AG_COPIES,)), pltpu.SemaphoreType.DMA((n_gather * AG_COPIES,)),
                        pltpu.SemaphoreType.DMA((n_gather,))],
        compiler_params=pltpu.CompilerParams(vmem_limit_bytes=VMEM_LIMIT_BYTES),
    )(*arrays)
    return list(res[:n]), list(res[n:])


def _pair_sum(g, r, core, name):
    _, _, rows, cols = g.shape
    tr = _pick(rows, (512, 256, 128))

    def body(core_ref, g_ref, r_ref, p_ref):
        del core_ref
        p_ref[...] = (g_ref[...].astype(F32) + r_ref[...].astype(F32)).astype(p_ref.dtype)

    return pl.pallas_call(
        body, name=name,
        grid_spec=pltpu.PrefetchScalarGridSpec(
            num_scalar_prefetch=1, grid=(4, rows // tr),
            in_specs=[pl.BlockSpec((None, None, tr, cols), lambda q, i, core_ref: (q, core_ref[0], i, 0)),
                      pl.BlockSpec((None, tr, cols), lambda q, i, core_ref: (q, i, 0))],
            out_specs=pl.BlockSpec((None, tr, cols), lambda q, i, core_ref: (q, i, 0))),
        out_shape=_sds((4, rows, cols), g.dtype),
        compiler_params=_params("parallel", "parallel"),
    )(core, g, r)


def _adam(w, g, m, v):
    m2 = ADAM_B1 * m + (1.0 - ADAM_B1) * g
    v2 = ADAM_B2 * v + (1.0 - ADAM_B2) * (g * g)
    m_hat = m2 / (1.0 - ADAM_B1 ** ADAM_STEP)
    v_hat = v2 / (1.0 - ADAM_B2 ** ADAM_STEP)
    return -ADAM_LR * (m_hat / (jnp.sqrt(v_hat) + ADAM_EPS) + ADAM_WD * w), m2, v2


def _shard_update(own, index, received, w, m, v, layer, so_far, name):
    _, rows, cols = w.shape
    n_recv = received.shape[0]
    tr = _pick(rows, (512, 256, 128))

    def body(index_ref, p_ref, q_ref, w_ref, m_ref, v_ref, *rest):
        del index_ref
        g_out, d_out, m_out, v_out = rest[-4:]
        g = p_ref[...].astype(F32)
        for s in range(n_recv):
            g = g + q_ref[s].astype(F32)
        g_out[...] = g
        d_out[...], m_out[...], v_out[...] = _adam(w_ref[...], g, m_ref[...], v_ref[...])

    tile = pl.BlockSpec((None, tr, cols), lambda i, index_ref: (layer, i, 0))
    kept = list(so_far) if so_far is not None else []
    return pl.pallas_call(
        body, name=name,
        grid_spec=pltpu.PrefetchScalarGridSpec(
            num_scalar_prefetch=1, grid=(rows // tr,),
            in_specs=[pl.BlockSpec((None, tr, cols), lambda i, index_ref: (index_ref[0], i, 0)),
                      pl.BlockSpec((n_recv, tr, cols), lambda i, index_ref: (0, i, 0)), tile, tile, tile] + [ANY] * len(kept),
            out_specs=[tile] * 4),
        out_shape=[_sds(w.shape, F32)] * 4,
        input_output_aliases={6 + i: i for i in range(len(kept))},
        compiler_params=_params("parallel"),
    )(index, own, received, w, m, v, *kept)


def _natural_pieces(shape, r, c):
    if tuple(shape[-2:]) == (r, c) and all(n == 1 for n in shape[:-2]):
        return [((0,) * (len(shape) - 2), (0, c))]
    groups, width = shape[1], shape[3]
    assert len(shape) == 4 and shape[0] == 1 and shape[2] == r and groups * width == c, (shape, r, c)
    return [((0, g), (g * width, width)) for g in range(groups)]


def _replicated_update(shares, where, params, name):
    flat = [a for p in params if p is not None for a in p]

    def body(s_ref, *refs):
        ins, outs = refs[:len(flat)], refs[len(flat):]
        total = s_ref[0]
        for dev in range(1, N_DEV):
            total = total + s_ref[dev]
        i_at = o_at = 0
        for ranges, p in zip(where, params):
            chunks = [total[r0:r0 + r, :c] for r0, r, c in ranges]
            g2d = chunks[0] if len(chunks) == 1 else jnp.concatenate(chunks, axis=1)
            if p is None:
                outs[o_at][...] = g2d
                o_at += 1
                continue
            w_ref, m_ref, v_ref = ins[i_at:i_at + 3]
            for idx, (c0, cols) in _natural_pieces(p[0].shape, *g2d.shape):
                at = idx + (slice(None), slice(None))
                g = g2d[:, c0:c0 + cols]
                outs[o_at][at] = g
                outs[o_at + 1][at], outs[o_at + 2][at], outs[o_at + 3][at] = _adam(w_ref[at], g, m_ref[at], v_ref[at])
            i_at, o_at = i_at + 3, o_at + 4

    out_shape = []
    for ranges, p in zip(where, params):
        out_shape += [_sds((ranges[0][1], sum(c for _, _, c in ranges)), F32)] if p is None else [_sds(p[0].shape, F32)] * 4
    res = pl.pallas_call(
        body, name=name, out_shape=out_shape, compiler_params=pltpu.CompilerParams(vmem_limit_bytes=VMEM_LIMIT_BYTES),
    )(shares, *flat)
    out, at = [], 0
    for p in params:
        out.append(list(res[at:at + (1 if p is None else 4)]))
        at += 1 if p is None else 4
    return out


def _rope_tables(t):
    half = HEAD_DIM // 2
    inv_freq = ROPE_THETA ** (-(jnp.arange(half, dtype=F32) * 2.0) / HEAD_DIM)
    ang = jnp.arange(t, dtype=jnp.int32).astype(F32)[:, None] * inv_freq[None, :]
    cos, sin = jnp.cos(ang), jnp.sin(ang)
    return jnp.tile(jnp.concatenate([cos, cos], axis=1), (1, 2)), jnp.tile(jnp.concatenate([-sin, sin], axis=1), (1, 2))


def _dup_heads(w, d):
    kv = (w.shape[-1] - d) // 2
    lead = w.shape[:-1]

    def dup(part):
        heads = part.reshape(lead + (kv // HEAD_DIM, 1, HEAD_DIM))
        return jnp.broadcast_to(heads, lead + (kv // HEAD_DIM, 2, HEAD_DIM)).reshape(lead + (2 * kv,))

    return jnp.concatenate([w[..., :d], dup(w[..., d:d + kv]), dup(w[..., d + kv:])], axis=-1)


def _fold_heads(dw, d):
    kv = (dw.shape[-1] - d) // 4
    lead = dw.shape[:-1]

    def fold(part):
        return part.reshape(lead + (kv // HEAD_DIM, 2, HEAD_DIM)).sum(axis=-2).reshape(lead + (kv,))

    return jnp.concatenate([dw[..., :d], fold(dw[..., d:d + 2 * kv]), fold(dw[..., d + 2 * kv:])], axis=-1)


def _columns_full(gathered):
    _, rows, c = gathered.shape
    return jnp.transpose(gathered, (1, 0, 2)).reshape(rows, N_DEV * c)


def _rows_full(gathered):
    return gathered.reshape(-1, gathered.shape[-1])


def _columns_blocked(full):
    rows, cols = full.shape
    return jnp.transpose(full.reshape(rows, N_DEV, cols // N_DEV), (1, 0, 2)).astype(ACT)


def _rows_blocked(full):
    return full.reshape(N_DEV, full.shape[0] // N_DEV, full.shape[1]).astype(ACT)


def _pack_rows(parts, width):
    rows, where, at = [], [], 0
    for v in parts:
        r, c = v.shape
        n_rows = -(-r // 8) * 8
        chunks = []
        for c0 in range(0, c, width):
            chunk = v[:, c0:c0 + width].astype(F32)
            rows.append(jnp.pad(chunk, ((0, n_rows - r), (0, width - chunk.shape[1]))))
            chunks.append((at, r, chunk.shape[1]))
            at += n_rows
        where.append(chunks)
    return jnp.concatenate(rows, axis=0), where


def _halves(block):
    half = block.shape[0] // 2
    return (0, half), (half, half)


def _whole(block):
    return (0, block.shape[0])


EARLY = ("attn_w_qkv", "sgu_ln", "attn_w_o")
LATE = ("sgu_w_in", "sgu_w_out", "gu0", "gu1", "dn0", "dn1")
COLUMN_SHARDED = ("attn_w_qkv", "sgu_w_in", "gu0", "gu1")
BEFORE_ATTN = ("norm_mix_post", "norm_ffn_pre", "norm_ffn_post", "attn_b_o", "sgu_w_spatial", "sgu_b_spatial")
AFTER_ATTN = ("attn_b_qkv", "attn_sinks")
LAST = ("norm_mix_pre",)
WEIGHTS = ("norm_mix_pre", "norm_mix_post", "norm_ffn_pre", "norm_ffn_post", "attn_w_qkv", "attn_b_qkv", "attn_sinks", "attn_w_o",
           "attn_b_o", "sgu_w_in", "sgu_ln_g", "sgu_ln_b", "sgu_w_spatial", "sgu_b_spatial", "sgu_w_out", "ffn_w_gate_up", "ffn_w_down")


LAYER_OF = {"gu0": ("ffn_w_gate_up", 0), "gu1": ("ffn_w_gate_up", 1), "dn0": ("ffn_w_down", 0), "dn1": ("ffn_w_down", 1)}


def _source(tree, name):
    if name == "sgu_ln":
        return [tree["sgu_ln_g"][None], tree["sgu_ln_b"][None]], 0
    leaf, layer = LAYER_OF.get(name, (name, 0))
    return [tree[leaf]], layer


def kernel(x, norm_mix_pre, norm_mix_post, norm_ffn_pre, norm_ffn_post, attn_w_qkv, attn_b_qkv, attn_sinks, attn_w_o, attn_b_o, sgu_w_in, sgu_ln_g, sgu_ln_b, sgu_w_spatial, sgu_b_spatial, sgu_w_out, ffn_w_gate_up, ffn_w_down, loss_target, m_norm_mix_pre, m_norm_mix_post, m_norm_ffn_pre, m_norm_ffn_post, m_attn_w_qkv, m_attn_b_qkv, m_attn_sinks, m_attn_w_o, m_attn_b_o, m_sgu_w_in, m_sgu_ln_g, m_sgu_ln_b, m_sgu_w_spatial, m_sgu_b_spatial, m_sgu_w_out, m_ffn_w_gate_up, m_ffn_w_down, v_norm_mix_pre, v_norm_mix_post, v_norm_ffn_pre, v_norm_ffn_post, v_attn_w_qkv, v_attn_b_qkv, v_attn_sinks, v_attn_w_o, v_attn_b_o, v_sgu_w_in, v_sgu_ln_g, v_sgu_ln_b, v_sgu_w_spatial, v_sgu_b_spatial, v_sgu_w_out, v_ffn_w_gate_up, v_ffn_w_down):
    w = dict(norm_mix_pre=norm_mix_pre, norm_mix_post=norm_mix_post, norm_ffn_pre=norm_ffn_pre, norm_ffn_post=norm_ffn_post,
             attn_w_qkv=attn_w_qkv, attn_b_qkv=attn_b_qkv, attn_sinks=attn_sinks, attn_w_o=attn_w_o, attn_b_o=attn_b_o,
             sgu_w_in=sgu_w_in, sgu_ln_g=sgu_ln_g, sgu_ln_b=sgu_ln_b, sgu_w_spatial=sgu_w_spatial, sgu_b_spatial=sgu_b_spatial,
             sgu_w_out=sgu_w_out, ffn_w_gate_up=ffn_w_gate_up, ffn_w_down=ffn_w_down)
    mom = dict(norm_mix_pre=m_norm_mix_pre, norm_mix_post=m_norm_mix_post, norm_ffn_pre=m_norm_ffn_pre, norm_ffn_post=m_norm_ffn_post,
               attn_w_qkv=m_attn_w_qkv, attn_b_qkv=m_attn_b_qkv, attn_sinks=m_attn_sinks, attn_w_o=m_attn_w_o, attn_b_o=m_attn_b_o,
               sgu_w_in=m_sgu_w_in, sgu_ln_g=m_sgu_ln_g, sgu_ln_b=m_sgu_ln_b, sgu_w_spatial=m_sgu_w_spatial,
               sgu_b_spatial=m_sgu_b_spatial, sgu_w_out=m_sgu_w_out, ffn_w_gate_up=m_ffn_w_gate_up, ffn_w_down=m_ffn_w_down)
    var = dict(norm_mix_pre=v_norm_mix_pre, norm_mix_post=v_norm_mix_post, norm_ffn_pre=v_norm_ffn_pre, norm_ffn_post=v_norm_ffn_post,
               attn_w_qkv=v_attn_w_qkv, attn_b_qkv=v_attn_b_qkv, attn_sinks=v_attn_sinks, attn_w_o=v_attn_w_o, attn_b_o=v_attn_b_o,
               sgu_w_in=v_sgu_w_in, sgu_ln_g=v_sgu_ln_g, sgu_ln_b=v_sgu_ln_b, sgu_w_spatial=v_sgu_w_spatial,
               sgu_b_spatial=v_sgu_b_spatial, sgu_w_out=v_sgu_w_out, ffn_w_gate_up=v_ffn_w_gate_up, ffn_w_down=v_ffn_w_down)
    w, mom, var = [{**tree, "ffn_w_gate_up": jnp.swapaxes(tree["ffn_w_gate_up"], 1, 2)} for tree in (w, mom, var)]
    xs, target = x[0], loss_target[0]
    t, d = xs.shape
    row = lambda v: v.reshape(1, -1).astype(F32)
    core = lax.axis_index("c").astype(jnp.int32).reshape(1)
    chip = (2 * lax.axis_index("x") + lax.axis_index("y")).astype(jnp.int32).reshape(1)
    names = EARLY + LATE
    staged, early = _prepare([_source(w, n) for n in names], [F32 if n == "sgu_ln" else ACT for n in names], len(EARLY),
                             "weights_prepare")
    stage = dict(zip(names, staged))
    w_qkv = _dup_heads(_columns_full(early[0]), d)
    lg, lb = row(early[1][:, 0, :]), row(early[1][:, 1, :])
    w_o = _rows_full(early[2])
    b_qkv = _dup_heads(row(attn_b_qkv), d)
    sinks = attn_sinks.reshape(1, -1).astype(F32)
    ws = sgu_w_spatial[0].astype(F32)
    bs_t = sgu_b_spatial[0].astype(F32).T
    cos, sin = _rope_tables(t)
    gather = lambda name, so_far, **pieces: _gather_job(stage[name], so_far, **pieces)
    a_half = lambda name: _halves(stage[name])[0]
    b_half = lambda name: _halves(stage[name])[1]
    whole = lambda name: _whole(stage[name])

    def pieces(name, n):
        rows = stage[name].shape[0] // n
        return [(i * rows, rows) for i in range(n)]

    ways = lambda parts: [(piece, i % 2) for i, piece in enumerate(parts)]
    relay = lambda name, so_far, **steps: _relay_gather_job(stage[name], so_far, **steps)
    gu0_parts, gu1_parts = pieces("gu0", 4), pieces("gu1", 4)
    dn0_parts, dn1_parts, in_parts, out_parts = pieces("dn0", 2), pieces("dn1", 2), pieces("sgu_w_in", 2), pieces("sgu_w_out", 2)
    (h0, q, kdup, vdup), ((g_gu0,),) = _attn_qkv_fwd(
        xs, row(norm_mix_pre[0]), w_qkv, b_qkv, cos, sin, jobs=[relay("gu0", None, sends=gu0_parts)])
    (o,), ((g_gu0,), (g_dn0,)) = _attn_fwd(q, kdup, vdup, sinks, jobs=[
        relay("gu0", g_gu0, relays=ways(gu0_parts), forwards=gu0_parts), relay("dn0", None, sends=dn0_parts)])
    (m0, x1), ((g_gu0,), (g_dn0,), (g_in,)) = _proj_res(o, w_o, row(attn_b_o), row(norm_mix_post[0]), xs, "attn_out_fwd", jobs=[
        relay("gu0", g_gu0, far=gu0_parts), relay("dn0", g_dn0, relays=ways(dn0_parts), forwards=dn0_parts),
        relay("sgu_w_in", None, sends=in_parts)])
    w_gu0 = _rows_full(g_gu0)
    (h1, gu0, a0), ((g_dn0,), (g_in,), (g_out,), (g_gu1,)) = _ffn_up_fwd(x1, row(norm_ffn_pre[0]), w_gu0, jobs=[
        relay("dn0", g_dn0, far=dn0_parts), relay("sgu_w_in", g_in, relays=ways(in_parts), forwards=in_parts),
        relay("sgu_w_out", None, sends=out_parts), relay("gu1", None, sends=gu1_parts[:2])])
    w_dn0 = _rows_full(g_dn0)
    (f0, x2), ((g_in,), (g_out,), (g_gu1,)) = _proj_res(a0, w_dn0, None, row(norm_ffn_post[0]), x1, "ffn_down_fwd0", jobs=[
        relay("sgu_w_in", g_in, far=in_parts), relay("sgu_w_out", g_out, relays=ways(out_parts), forwards=out_parts),
        relay("gu1", g_gu1, sends=gu1_parts[2:], relays=ways(gu1_parts[:2]), forwards=gu1_parts[:2])])
    w_in = g_in
    (h2, z, y), ((g_out,), (g_gu1,), (g_dn1,)) = _sgu_fwd(x2, row(norm_mix_pre[1]), w_in, lg, lb, ws, bs_t, jobs=[
        relay("sgu_w_out", g_out, far=out_parts),
        relay("gu1", g_gu1, relays=ways(gu1_parts[2:]), forwards=gu1_parts[2:], far=gu1_parts[:2]),
        relay("dn1", None, sends=dn1_parts)])
    w_out = _rows_full(g_out)
    (m1, x3), ((g_gu1,), (g_dn1,)) = _proj_res(y, w_out, None, row(norm_mix_post[1]), x2, "sgu_out_fwd", jobs=[
        relay("gu1", g_gu1, far=gu1_parts[2:]), relay("dn1", g_dn1, relays=ways(dn1_parts), forwards=dn1_parts)])
    w_gu1 = _rows_full(g_gu1)
    (h3, gu1, a1), ((g_dn1,),) = _ffn_up_fwd(x3, row(norm_ffn_pre[1]), w_gu1, jobs=[relay("dn1", g_dn1, far=dn1_parts)])
    w_dn1 = _rows_full(g_dn1)
    (f1, dx4, loss_tile), _ = _proj_res(a1, w_dn1, None, row(norm_ffn_post[1]), x3, "ffn_down_fwd1", target=target)

    to_sibling = lambda g: _scatter_job([g], 4, _to_sibling)
    pair = lambda g, r, name: _pair_sum(g.reshape((4, 2) + g.shape[1:]), r, core, "pair_sum_" + name)
    to_chips = lambda p, prev=None, piece=None: _scatter_job([p], 3, _to_owner_chip, prev=prev, piece=piece)
    out_g, out_d, out_m, out_v = {}, {}, {}, {}

    trees = [{**tree, "sgu_ln": jnp.stack([tree["sgu_ln_g"], tree["sgu_ln_b"]], axis=1)} for tree in (w, mom, var)]
    so_far = {}

    def update(name, own, index, received):
        leaf, layer = LAYER_OF.get(name, (name, 0))
        so_far[leaf] = _shard_update(own, index, received, *[tree[leaf] for tree in trees], layer, so_far.get(leaf), "update_" + name)
        for out, val in zip((out_g, out_d, out_m, out_v), so_far[leaf]):
            out[leaf] = val

    def finish(names, n_extra, shares, where, name):
        res = _replicated_update(shares, where, [(w[n], mom[n], var[n]) for n in names] + [None] * n_extra, name)
        for n, vals in zip(names, res):
            out_g[n], out_d[n], out_m[n], out_v[n] = vals
        return [vals[0] for vals in res[len(names):]]

    first_half = lambda p: _halves(p[0])[0]
    second_half = lambda p: _halves(p[0])[1]
    (df1, dgu1, dg_ffn_post1), _ = _post_bwd(dx4, f1, row(norm_ffn_post[1]), w_dn1, "ffn", "ffn_down_bwd1", gu1)
    b_gu1, _ = _wgrad(h3, dgu1, "ffn_up_wgrad1", ACT, transposed=True)
    b_gu1 = _rows_blocked(b_gu1)
    dw_dn1, ((r_gu1,),) = _wgrad(a1, df1, "ffn_down_wgrad1", ACT, jobs=[to_sibling(b_gu1)])
    b_dn1 = _rows_blocked(dw_dn1)
    p_gu1 = pair(b_gu1, r_gu1, "gu1")
    (dx3, dg_ffn_pre1), ((q_gu1,), (r_dn1,)) = _pre_bwd(dgu1, w_gu1, x3, row(norm_ffn_pre[1]), dx4, "ffn_up_bwd1", True, jobs=[
        to_chips(p_gu1, piece=first_half(p_gu1)), to_sibling(b_dn1)])
    p_dn1 = pair(b_dn1, r_dn1, "dn1")
    (dm1, dy, dg_mix_post1), _ = _post_bwd(dx3, m1, row(norm_mix_post[1]), w_out, "sgu", "sgu_out_bwd")
    dw_out, _ = _wgrad(y, dm1, "sgu_out_wgrad", ACT)
    b_out = _rows_blocked(dw_out)
    (dz, dlg, dlb, dws, dbs_t), ((q_gu1,), (r_out,)) = _sgu_mix_bwd(z, dy, lg, lb, ws, bs_t, jobs=[
        to_chips(p_gu1, prev=[q_gu1], piece=second_half(p_gu1)), to_sibling(b_out)])
    update("gu1", p_gu1, chip, q_gu1)
    p_out = pair(b_out, r_out, "sgu_w_out")
    b_in, _ = _wgrad(h2, dz, "sgu_in_wgrad", ACT, out_block=stage["sgu_w_in"].shape[1])
    b_ln = jnp.stack([dlg.reshape(N_DEV, -1), dlb.reshape(N_DEV, -1)], axis=1)
    (dx2, dg_mix_pre1), _ = _pre_bwd(dz, w_in, x2, row(norm_mix_pre[1]), dx3, "sgu_in_bwd")
    (df0, dgu0, dg_ffn_post0), ((q_dn1,), (q_out,), (r_in,), (r_ln,)) = _post_bwd(
        dx2, f0, row(norm_ffn_post[0]), w_dn0, "ffn", "ffn_down_bwd0", gu0,
        jobs=[to_chips(p_dn1), to_chips(p_out), to_sibling(b_in), to_sibling(b_ln)])
    update("dn1", p_dn1, chip, q_dn1)
    update("sgu_w_out", p_out, chip, q_out)
    p_in, p_ln = pair(b_in, r_in, "sgu_w_in"), pair(b_ln, r_ln, "sgu_ln")
    b_gu0, ((q_in,), (q_ln,)) = _wgrad(h1, dgu0, "ffn_up_wgrad0", ACT, transposed=True, jobs=[to_chips(p_in), to_chips(p_ln)])
    update("sgu_w_in", p_in, chip, q_in)
    update("sgu_ln", p_ln, chip, q_ln)
    b_gu0 = _rows_blocked(b_gu0)
    dw_dn0, ((r_gu0,),) = _wgrad(a0, df0, "ffn_down_wgrad0", ACT, jobs=[to_sibling(b_gu0)])
    b_dn0 = _rows_blocked(dw_dn0)
    p_gu0 = pair(b_gu0, r_gu0, "gu0")
    (dx1, dg_ffn_pre0), ((q_gu0,), (r_dn0,)) = _pre_bwd(dgu0, w_gu0, x1, row(norm_ffn_pre[0]), dx2, "ffn_up_bwd0", True, jobs=[
        to_chips(p_gu0, piece=first_half(p_gu0)), to_sibling(b_dn0)])
    p_dn0 = pair(b_dn0, r_dn0, "dn0")
    (dm0, do, dg_mix_post0, db_o), _ = _post_bwd(dx1, m0, row(norm_mix_post[0]), w_o, "attn", "attn_out_bwd")
    dw_o, _ = _wgrad(o, dm0, "attn_out_wgrad", ACT)
    b_o = _rows_blocked(dw_o)
    grads = {
        "norm_mix_post": jnp.concatenate([dg_mix_post0, dg_mix_post1], axis=0),
        "norm_ffn_pre": jnp.concatenate([dg_ffn_pre0, dg_ffn_pre1], axis=0),
        "norm_ffn_post": jnp.concatenate([dg_ffn_post0, dg_ffn_post1], axis=0),
        "attn_b_o": db_o,
        "sgu_w_spatial": dws,
        "sgu_b_spatial": dbs_t.T,
    }
    shares1, where1 = _pack_rows([grads[name] for name in BEFORE_ATTN], d)
    (dqkv, db_qkv, dsink), ((q_gu0,), (q_dn0,), (r_o,), (g_shares1,)) = _attn_bwd(q, kdup, vdup, do, sinks, cos, sin, jobs=[
        to_chips(p_gu0, prev=[q_gu0], piece=second_half(p_gu0)), to_chips(p_dn0), to_sibling(b_o),
        _gather_job(shares1, None, sends=[_whole(shares1)])])
    update("gu0", p_gu0, chip, q_gu0)
    update("dn0", p_dn0, chip, q_dn0)
    p_o = pair(b_o, r_o, "attn_w_o")
    grads.update({"attn_b_qkv": _fold_heads(db_qkv, d), "attn_sinks": dsink[:1, :d // HEAD_DIM]})
    shares2, where2 = _pack_rows([grads[name] for name in AFTER_ATTN] + [loss_tile[:1, :1]], d)
    (dx0, dg_mix_pre0), _ = _pre_bwd(dqkv, w_qkv, xs, row(norm_mix_pre[0]), dx1, "attn_qkv_bwd")
    grads["norm_mix_pre"] = jnp.concatenate([dg_mix_pre0, dg_mix_pre1], axis=0)
    shares3, where3 = _pack_rows([grads[name] for name in LAST], d)
    dw_qkv, ((q_o,), (g_shares1,), (g_shares2,), (g_shares3,)) = _wgrad(h0, dqkv, "attn_qkv_wgrad", jobs=[
        to_chips(p_o), _gather_job(shares1, g_shares1, forwards=[_whole(shares1)]),
        _gather_job(shares2, None, sends=[_whole(shares2)]), _gather_job(shares3, None, sends=[_whole(shares3)])])
    update("attn_w_o", p_o, chip, q_o)
    b_qkv_grad = _columns_blocked(_fold_heads(dw_qkv, d))
    (r_qkv,), (g_shares2,), (g_shares3,) = _copies_only([
        to_sibling(b_qkv_grad), _gather_job(shares2, g_shares2, forwards=[_whole(shares2)]),
        _gather_job(shares3, g_shares3, forwards=[_whole(shares3)])], "grads_tail_to_sibling")
    p_qkv = pair(b_qkv_grad, r_qkv, "attn_w_qkv")
    (q_qkv,), = _copies_only([to_chips(p_qkv)], "grads_tail_to_owner_chip")
    update("attn_w_qkv", p_qkv, chip, q_qkv)

    finish(BEFORE_ATTN, 0, g_shares1, where1, "replicated_update_before_attn")
    loss = finish(AFTER_ATTN, 1, g_shares2, where2, "replicated_update_after_attn")[0][0, 0]
    finish(LAST, 0, g_shares3, where3, "replicated_update_last")

    for out in (out_g, out_d, out_m, out_v):
        out["sgu_ln_g"], out["sgu_ln_b"] = out["sgu_ln"][:, 0, :], out["sgu_ln"][:, 1, :]
        out["ffn_w_gate_up"] = jnp.swapaxes(out["ffn_w_gate_up"], 1, 2)

    return (loss, dx0[None], *[out_g[n] for n in WEIGHTS], *[out_d[n] for n in WEIGHTS],
            *[out_m[n] for n in WEIGHTS], *[out_v[n] for n in WEIGHTS])
```

```python
import functools
import math
import operator

import jax
import jax.numpy as jnp
from jax import lax
from jax.experimental import pallas as pl
from jax.experimental.pallas import tpu as pltpu

F32 = jnp.float32
ACT = jnp.bfloat16

EPS = 1e-6
HEAD_DIM = 64
PAIR = 2 * HEAD_DIM
GQA_GROUP = 4
WINDOW = 128
ROPE_THETA = 10000.0
SCALE = HEAD_DIM ** -0.5
MASKED = -1e30
LANES = 128
MXU_WIDTH = 256

ADAM_LR, ADAM_B1, ADAM_B2, ADAM_EPS, ADAM_WD, ADAM_STEP = 0.001, 0.9, 0.999, 1e-08, 0.01, 10

N_DEV = 8
VMEM_LIMIT_BYTES = 56 * 1024 * 1024
MESH = pl.DeviceIdType.MESH
ANY = pl.BlockSpec(memory_space=pl.ANY)
IN_VMEM = pl.BlockSpec(memory_space=pltpu.VMEM)


def _pick(n, candidates):
    for c in candidates:
        if n % c == 0:
            return c
    return n


def _row_tile(t, most=512):
    return _pick(t, (most,))


def _mxu_chunks(n, most=4 * MXU_WIDTH):
    assert n % MXU_WIDTH == 0 and most % MXU_WIDTH == 0
    return [(c, min(most, n - c)) for c in range(0, n, most)]


def _params(*sem):
    return pltpu.CompilerParams(dimension_semantics=sem, vmem_limit_bytes=VMEM_LIMIT_BYTES)


def _full(shape):
    return pl.BlockSpec(shape, lambda *_: (0,) * len(shape))


def _resident(shape):
    return pl.BlockSpec(shape, lambda *_: (0,) * len(shape), pipeline_mode=pl.Buffered(1))


def _rows(tm, n):
    return pl.BlockSpec((tm, n), lambda i: (i, 0))


def _sds(shape, dtype):
    return jax.ShapeDtypeStruct(shape, dtype)


def _place():
    return lax.axis_index("x"), lax.axis_index("y"), lax.axis_index("c")


def _other_chips(x, y):
    return [(1 - x, y), (x, 1 - y), (1 - x, 1 - y)]


class _Job:
    def __init__(self, inputs, out_shape, aliases, emit, n_remote, n_local=0):
        self.inputs, self.out_shape, self.aliases, self.emit = list(inputs), list(out_shape), dict(aliases), emit
        self.n_remote, self.n_local = n_remote, n_local


def _call(body, name, grid, in_specs, out_specs, out_shape, args, sem, scratch=(), jobs=()):
    n_in, n_out, n_scr = len(args), len(out_shape), len(scratch)
    j_in = [a for job in jobs for a in job.inputs]
    j_out = [s for job in jobs for s in job.out_shape]
    aliases, at_in, at_out = {}, n_in, n_out
    for job in jobs:
        aliases.update({at_in + i: at_out + o for i, o in job.aliases.items()})
        at_in, at_out = at_in + len(job.inputs), at_out + len(job.out_shape)
    n_remote = sum(job.n_remote for job in jobs)
    n_local = sum(job.n_local for job in jobs)

    def wrapped(*refs):
        ins, jin = refs[:n_in], refs[n_in:n_in + len(j_in)]
        rest = refs[n_in + len(j_in):]
        outs, jout = rest[:n_out], rest[n_out:n_out + len(j_out)]
        scr = rest[n_out + len(j_out):n_out + len(j_out) + n_scr]
        if not jobs:
            body(*ins, *outs, *scr)
            return
        send, recv, local = rest[n_out + len(j_out) + n_scr:]
        ids = [pl.program_id(a) for a in range(len(grid))]
        first = functools.reduce(operator.and_, [i == 0 for i in ids])
        last = functools.reduce(operator.and_, [i == g - 1 for i, g in zip(ids, grid)])

        def descriptors():
            place, found, i, o, r, l = _place(), [], 0, 0, 0, 0
            for job in jobs:
                for src, dst, to in job.emit(jin[i:i + len(job.inputs)], jout[o:o + len(job.out_shape)], place):
                    if to is None:
                        found.append(pltpu.make_async_copy(src, dst, local.at[l]))
                        l += 1
                    else:
                        found.append(pltpu.make_async_remote_copy(src_ref=src, dst_ref=dst, send_sem=send.at[r], recv_sem=recv.at[r],
                                                                  device_id=to, device_id_type=MESH))
                        r += 1
                i, o = i + len(job.inputs), o + len(job.out_shape)
            return found

        @pl.when(first)
        def _():
            for cp in descriptors():
                cp.start()

        body(*ins, *outs, *scr)

        @pl.when(last)
        def _():
            for cp in descriptors():
                cp.wait()

    sems = [pltpu.SemaphoreType.DMA((n_remote,)), pltpu.SemaphoreType.DMA((n_remote,)),
            pltpu.SemaphoreType.DMA((max(n_local, 1),))] if jobs else []
    res = pl.pallas_call(
        wrapped, name=name, grid=grid,
        in_specs=list(in_specs) + [ANY] * len(j_in), out_specs=list(out_specs) + [ANY] * len(j_out),
        out_shape=list(out_shape) + j_out, scratch_shapes=list(scratch) + sems, input_output_aliases=aliases,
        compiler_params=_params(*(("arbitrary",) * len(grid) if jobs else sem)),
    )(*args, *j_in)
    job_res, at = [], n_out
    for job in jobs:
        job_res.append(list(res[at:at + len(job.out_shape)]))
        at += len(job.out_shape)
    return list(res[:n_out]), job_res


def _copies_only(jobs, name):
    def body(o_ref):
        o_ref[...] = jnp.zeros_like(o_ref)

    return _call(body, name, (1,), [], [_full((8, LANES))], [_sds((8, LANES), F32)], (), ("arbitrary",), jobs=jobs)[1]


def _gather_job(stage, gathered, sends=(), forwards=()):
    shape = _sds((N_DEV,) + stage.shape, stage.dtype)
    inputs = ([stage] if sends else []) + ([gathered] if gathered is not None else [])
    aliases = {len(inputs) - 1: 0} if gathered is not None else {}

    def emit(ins, outs, place):
        x, y, c = place
        sibling, chips, mine, g = (x, y, 1 - c), _other_chips(x, y), 4 * x + 2 * y + c, outs[0]
        found = []
        for r0, nr in sends:
            src, dst = ins[0].at[pl.ds(r0, nr)], g.at[mine, pl.ds(r0, nr)]
            found += [(src, dst, None), (src, dst, sibling)] + [(src, dst, (px, py, c)) for px, py in chips]
        for r0, nr in forwards:
            for px, py in chips:
                block = 4 * px + 2 * py + c
                found.append((ins[-1].at[block, pl.ds(r0, nr)], g.at[block, pl.ds(r0, nr)], sibling))
        return found

    return _Job(inputs, [shape], aliases, emit, 4 * len(sends) + 3 * len(forwards), len(sends))


def _relay_gather_job(stage, gathered, sends=(), relays=(), forwards=(), far=()):
    shape = _sds((N_DEV,) + stage.shape, stage.dtype)
    inputs = ([stage] if sends else []) + ([gathered] if gathered is not None else [])
    aliases = {len(inputs) - 1: 0} if gathered is not None else {}

    def emit(ins, outs, place):
        x, y, c = place
        sibling, beside_x, beside_y, g = (x, y, 1 - c), (1 - x, y, c), (x, 1 - y, c), outs[0]
        slot = lambda dev: 4 * dev[0] + 2 * dev[1] + dev[2]
        found = []
        for r0, nr in sends:
            src, dst = ins[0].at[pl.ds(r0, nr)], g.at[slot((x, y, c)), pl.ds(r0, nr)]
            found += [(src, dst, None), (src, dst, sibling), (src, dst, beside_x), (src, dst, beside_y)]

        def passed_on(block, piece, to):
            return ins[-1].at[block, pl.ds(*piece)], g.at[block, pl.ds(*piece)], to

        for piece, way in relays:
            found.append(passed_on(slot(beside_x), piece, beside_y) if way == 0 else passed_on(slot(beside_y), piece, beside_x))
        for piece in forwards:
            found += [passed_on(slot(beside_x), piece, sibling), passed_on(slot(beside_y), piece, sibling)]
        for piece in far:
            found.append(passed_on(slot((1 - x, 1 - y, c)), piece, sibling))
        return found

    return _Job(inputs, [shape], aliases, emit, 3 * len(sends) + len(relays) + 2 * len(forwards) + len(far), len(sends))


def _scatter_job(arrays, n_slots, route, prev=None, piece=None):
    shapes = [_sds((n_slots,) + v.shape[1:], v.dtype) for v in arrays]
    inputs = list(arrays) + (list(prev) if prev else [])
    aliases = {len(arrays) + i: i for i in range(len(arrays))} if prev else {}

    def emit(ins, outs, place):
        found = []
        for a in range(len(arrays)):
            for s in range(n_slots):
                block, to = route(s, *place)
                if piece is None:
                    found.append((ins[a].at[block], outs[a].at[s], to))
                else:
                    found.append((ins[a].at[block, pl.ds(*piece)], outs[a].at[s, pl.ds(*piece)], to))
        return found

    return _Job(inputs, shapes, aliases, emit, len(arrays) * n_slots)


def _to_sibling(s, x, y, c):
    return 2 * s + (1 - c), (x, y, 1 - c)


def _to_owner_chip(s, x, y, c):
    px, py = _other_chips(x, y)[s]
    return 2 * px + py, (px, py, c)


def _rstd(x):
    return lax.rsqrt(jnp.mean(x * x, axis=-1, keepdims=True) + EPS)


def _rms_bwd(xhat, r, g, dy):
    dxh = dy * g
    return r * (dxh - xhat * jnp.mean(dxh * xhat, axis=-1, keepdims=True))


def _first_half(rows):
    lane = lax.broadcasted_iota(jnp.int32, (rows, PAIR), 1)
    return (lane % HEAD_DIM) < (HEAD_DIM // 2)


def _rot_half(v, first):
    return jnp.where(first, pltpu.roll(v, PAIR - HEAD_DIM // 2, 1), pltpu.roll(v, HEAD_DIM // 2, 1))


def _rope(v, cos, sin, first):
    return v * cos + _rot_half(v, first) * sin


def _rope_t(dv, cos, sin, first):
    return dv * cos + _rot_half(dv * sin, first)


def _dot(a, b):
    return jnp.dot(a, b, preferred_element_type=F32)


def _dot_nt(a, b):
    return lax.dot_general(a, b, (((1,), (1,)), ((), ())), preferred_element_type=F32)


def _dot_tn(a, b):
    return lax.dot_general(a, b, (((0,), (0,)), ((), ())), preferred_element_type=F32)


def _sigmoid(v):
    return 1.0 / (1.0 + jnp.exp(-v))


_GELU_K = math.sqrt(2.0 / math.pi)
_GELU_C = 0.044715


def _gelu(v):
    return v * (0.5 * (1.0 + jnp.tanh(_GELU_K * (v + _GELU_C * (v * v * v)))))


def _gelu_grad(v):
    t = jnp.tanh(_GELU_K * (v + _GELU_C * (v * v * v)))
    return 0.5 * (1.0 + t) + 0.5 * v * (1.0 - t * t) * (_GELU_K * (1.0 + 3.0 * _GELU_C * (v * v)))


def _attn_qkv_fwd(x, g, w, b, cos, sin, jobs=()):
    t, d = x.shape
    n = w.shape[1]
    kd = (n - d) // 2
    tm = _row_tile(t)
    ch = _pick(d, (512,))

    def body(x_ref, g_ref, w_ref, b_ref, cos_ref, sin_ref, h_ref, q_ref, k_ref, v_ref):
        xv = x_ref[...]
        hb = (xv * _rstd(xv) * g_ref[...]).astype(ACT)
        h_ref[...] = hb
        cosv, sinv = cos_ref[...], sin_ref[...]
        first = _first_half(tm)

        def proj(lo, width):
            return _dot(hb, w_ref[:, lo:lo + width]) + b_ref[:, lo:lo + width]

        for c in range(0, d, ch):
            y = proj(c, ch)
            for s in range(0, ch, PAIR):
                q_ref[:, c + s:c + s + PAIR] = (_rope(y[:, s:s + PAIR], cosv, sinv, first) * SCALE).astype(ACT)
        y = proj(d, kd)
        for s in range(0, kd, PAIR):
            k_ref[:, s:s + PAIR] = _rope(y[:, s:s + PAIR], cosv, sinv, first).astype(ACT)
        v_ref[...] = proj(d + kd, kd).astype(ACT)

    return _call(
        body, "attn_qkv_fwd", (t // tm,),
        [_rows(tm, d), _full((1, d)), _full((d, n)), _full((1, n)), _rows(tm, PAIR), _rows(tm, PAIR)],
        [_rows(tm, d), _rows(tm, d), _rows(tm, kd), _rows(tm, kd)],
        [_sds((t, d), ACT), _sds((t, d), ACT), _sds((t, kd), ACT), _sds((t, kd), ACT)],
        (x, g, w, b, cos, sin), ("parallel",), jobs=jobs)


STACK = GQA_GROUP * WINDOW


def _band_mask(has_prev):
    row = lax.broadcasted_iota(jnp.int32, (STACK, 2 * WINDOW), 0) % WINDOW
    col = lax.broadcasted_iota(jnp.int32, (STACK, 2 * WINDOW), 1)
    return ((col < WINDOW) & (col > row) & has_prev) | ((col >= WINDOW) & (col - WINDOW <= row))


def _lane_halves():
    lane = lax.broadcasted_iota(jnp.int32, (1, PAIR), 1)
    return lane < HEAD_DIM, lane >= HEAD_DIM


def _stack_heads(ref, h, halves):
    parts = []
    for p in range(2):
        pair = ref[:, (2 * h + p) * PAIR:(2 * h + p + 1) * PAIR]
        parts += [jnp.where(half, pair, jnp.zeros_like(pair)) for half in halves]
    return jnp.concatenate(parts, axis=0)


def _sink_column(sink_ref, h):
    row = lax.broadcasted_iota(jnp.int32, (STACK, 1), 0)
    col = jnp.full((STACK, 1), sink_ref[0, GQA_GROUP * h + GQA_GROUP - 1], F32)
    for j in range(GQA_GROUP - 2, -1, -1):
        col = jnp.where(row < (j + 1) * WINDOW, sink_ref[0, GQA_GROUP * h + j], col)
    return col


def _probs(scores, valid, sink):
    s = jnp.where(valid, scores, MASKED)
    m = jnp.maximum(jnp.max(s, axis=-1, keepdims=True), sink)
    p = jnp.exp(s - m)
    psink = jnp.exp(sink - m)
    inv = 1.0 / (jnp.sum(p, axis=-1, keepdims=True) + psink)
    return p * inv, psink * inv


def _attn_fwd(q, kd_, vd, sinks, jobs=()):
    t, d = q.shape
    kd = kd_.shape[1]
    cur = lambda n: (n, 0)
    prev = lambda n: (jnp.maximum(n - 1, 0), 0)

    def body(sink_ref, q_ref, kc_ref, kp_ref, vc_ref, vp_ref, o_ref):
        valid = _band_mask(pl.program_id(0) > 0)
        halves = _lane_halves()
        heads = range(kd // PAIR)
        hs = [slice(h * PAIR, (h + 1) * PAIR) for h in heads]
        scores = [_dot_nt(_stack_heads(q_ref, h, halves), jnp.concatenate([kp_ref[:, hs[h]], kc_ref[:, hs[h]]], axis=0)) for h in heads]
        probs = [_probs(scores[h], valid, _sink_column(sink_ref, h))[0].astype(ACT) for h in heads]
        for h in heads:
            v2 = jnp.concatenate([vp_ref[:, hs[h]], vc_ref[:, hs[h]]], axis=0)
            vs = jnp.concatenate([jnp.where(half, v2, jnp.zeros_like(v2)) for half in halves], axis=0)
            pr = probs[h]
            for p in range(2):
                both = jnp.concatenate([pr[2 * p * WINDOW:(2 * p + 1) * WINDOW], pr[(2 * p + 1) * WINDOW:(2 * p + 2) * WINDOW]], axis=1)
                o_ref[:, (2 * h + p) * PAIR:(2 * h + p + 1) * PAIR] = _dot(both, vs).astype(ACT)

    kv_c, kv_p = pl.BlockSpec((WINDOW, kd), cur), pl.BlockSpec((WINDOW, kd), prev)
    return _call(
        body, "attn_fwd", (t // WINDOW,),
        [pl.BlockSpec(memory_space=pltpu.SMEM), pl.BlockSpec((WINDOW, d), cur), kv_c, kv_p, kv_c, kv_p],
        [pl.BlockSpec((WINDOW, d), cur)], [_sds((t, d), ACT)],
        (sinks, q, kd_, kd_, vd, vd), ("parallel",), jobs=jobs)


def _attn_bwd(q, kd_, vd, do, sinks, cos, sin, jobs=()):
    t, d = q.shape
    kd = kd_.shape[1]
    nb = t // WINDOW
    n_out = d + 2 * kd
    cur = lambda n: (jnp.minimum(n, nb - 1), 0)
    prev = lambda n: (jnp.maximum(jnp.minimum(n, nb - 1) - 1, 0), 0)
    late = lambda n: (jnp.maximum(n - 1, 0), 0)

    def body(sink_ref, q_ref, kc_ref, kp_ref, vc_ref, vp_ref, do_ref, cosc_ref, sinc_ref, cosp_ref, sinp_ref,
             out_ref, db_ref, dsink_ref, dq_keep, dk_keep, dv_keep):
        n = pl.program_id(0)

        @pl.when(n == 0)
        def _():
            for ref in (db_ref, dsink_ref, dq_keep, dk_keep, dv_keep):
                ref[...] = jnp.zeros_like(ref)

        valid = _band_mask(n > 0) & (n < nb)
        halves = _lane_halves()
        first = _first_half(WINDOW)
        slot = lax.broadcasted_iota(jnp.int32, dsink_ref.shape, 1)
        top = lax.broadcasted_iota(jnp.int32, dsink_ref.shape, 0) == 0
        dsink = jnp.zeros(dsink_ref.shape, F32)

        def emit(cols, done):
            out_ref[:, cols] = done.astype(ACT)
            db_ref[:, cols] += jnp.sum(done.astype(F32), axis=0, keepdims=True)

        heads = range(kd // PAIR)
        hs = [slice(h * PAIR, (h + 1) * PAIR) for h in heads]
        k2 = [jnp.concatenate([kp_ref[:, hs[h]], kc_ref[:, hs[h]]], axis=0) for h in heads]
        v2 = [jnp.concatenate([vp_ref[:, hs[h]], vc_ref[:, hs[h]]], axis=0) for h in heads]
        qs = [_stack_heads(q_ref, h, halves) for h in heads]
        dos = [_stack_heads(do_ref, h, halves) for h in heads]
        scores = [_dot_nt(qs[h], k2[h]) for h in heads]
        dps = [_dot_nt(dos[h], v2[h]) for h in heads]
        prs, dss = [], []
        for h in heads:
            pr, psink = _probs(scores[h], valid, _sink_column(sink_ref, h))
            delta = jnp.sum(pr * dps[h], axis=-1, keepdims=True)
            dss.append((pr * (dps[h] - delta)).astype(ACT))
            prs.append(pr.astype(ACT))
            leak = psink * delta
            for j in range(GQA_GROUP):
                share = jnp.sum(leak[j * WINDOW:(j + 1) * WINDOW], axis=0, keepdims=True)
                dsink = dsink - jnp.where(top & (slot == GQA_GROUP * h + j), share, 0.0)
        dqs = [_dot(dss[h], k2[h]) for h in heads]
        dk2 = [_dot_tn(dss[h], qs[h]) for h in heads]
        dv2 = [_dot_tn(prs[h], dos[h]) for h in heads]
        for h in heads:
            for p in range(2):
                ps = slice((2 * h + p) * PAIR, (2 * h + p + 1) * PAIR)
                dqp = jnp.where(halves[0], dqs[h][2 * p * WINDOW:(2 * p + 1) * WINDOW], dqs[h][(2 * p + 1) * WINDOW:(2 * p + 2) * WINDOW])
                emit(ps, dq_keep[:, ps])
                dq_keep[:, ps] = (_rope_t(dqp, cosc_ref[...], sinc_ref[...], first) * SCALE).astype(ACT)
            ks = slice(d + h * PAIR, d + (h + 1) * PAIR)
            vs = slice(d + kd + h * PAIR, d + kd + (h + 1) * PAIR)
            emit(ks, dk_keep[:, hs[h]] + _rope_t(dk2[h][:WINDOW], cosp_ref[...], sinp_ref[...], first))
            dk_keep[:, hs[h]] = _rope_t(dk2[h][WINDOW:], cosc_ref[...], sinc_ref[...], first)
            emit(vs, dv_keep[:, hs[h]] + dv2[h][:WINDOW])
            dv_keep[:, hs[h]] = dv2[h][WINDOW:]
        dsink_ref[...] += dsink

    kv_c, kv_p = pl.BlockSpec((WINDOW, kd), cur), pl.BlockSpec((WINDOW, kd), prev)
    tab_c, tab_p = pl.BlockSpec((WINDOW, PAIR), cur), pl.BlockSpec((WINDOW, PAIR), prev)
    return _call(
        body, "attn_bwd", (nb + 1,),
        [pl.BlockSpec(memory_space=pltpu.SMEM), pl.BlockSpec((WINDOW, d), cur), kv_c, kv_p, kv_c, kv_p,
         pl.BlockSpec((WINDOW, d), cur), tab_c, tab_c, tab_p, tab_p],
        [pl.BlockSpec((WINDOW, n_out), late), _full((1, n_out)), _full((8, LANES))],
        [_sds((t, n_out), ACT), _sds((1, n_out), F32), _sds((8, LANES), F32)],
        (sinks, q, kd_, kd_, vd, vd, do, cos, sin, cos, sin), ("arbitrary",),
        scratch=[pltpu.VMEM((WINDOW, d), ACT), pltpu.VMEM((WINDOW, kd), F32), pltpu.VMEM((WINDOW, kd), F32)], jobs=jobs)


def _proj_res(a, w, b, g, x, name, target=None, jobs=()):
    t = a.shape[0]
    k, d = w.shape
    tm = _row_tile(t)
    has_bias = b is not None

    def body(*refs):
        a_ref, w_ref = refs[:2]
        b_ref = refs[2] if has_bias else None
        g_ref, x_ref, m_ref, xo_ref = refs[2 + has_bias:]
        m = _dot(a_ref[...], w_ref[...])
        if has_bias:
            m = m + b_ref[...]
        m_ref[...] = m.astype(ACT)
        xo_ref[...] = x_ref[...] + (m * _rstd(m)) * g_ref[...]

    def body_with_loss(a_ref, w_ref, g_ref, x_ref, t_ref, m_ref, dy_ref, loss_ref):
        @pl.when(pl.program_id(0) == 0)
        def _():
            loss_ref[...] = jnp.zeros_like(loss_ref)

        body(a_ref, w_ref, g_ref, x_ref, m_ref, dy_ref)
        err = dy_ref[...] - t_ref[...]
        dy_ref[...] = err * (1.0 / d)
        loss_ref[...] += 0.5 * jnp.sum(jnp.mean(err * err, axis=-1, keepdims=True), axis=0, keepdims=True)

    ins = [a, w] + ([b] if has_bias else []) + [g, x]
    specs = [_rows(tm, k), _full((k, d))] + ([_full((1, d))] if has_bias else []) + [_full((1, d)), _rows(tm, d)]
    if target is not None:
        return _call(body_with_loss, name, (t // tm,), specs + [_rows(tm, d)], [_rows(tm, d), _rows(tm, d), _full((8, LANES))],
                     [_sds((t, d), ACT), _sds((t, d), F32), _sds((8, LANES), F32)], ins + [target], ("arbitrary",), jobs=jobs)
    return _call(body, name, (t // tm,), specs, [_rows(tm, d), _rows(tm, d)], [_sds((t, d), ACT), _sds((t, d), F32)], ins,
                 ("parallel",), jobs=jobs)


def _post_bwd(dres, m, g, w, mode, name, jobs=()):
    t, d = dres.shape
    k = w.shape[0]
    tm = _row_tile(t)
    kc = _pick(k, (1408, 1024, 512))

    def body(*refs):
        d_ref, m_ref, g_ref, w_ref = refs[:4]
        outs = refs[4:]
        dm_ref, da_ref, dg_ref = outs[:3]
        i = pl.program_id(0)

        @pl.when(i == 0)
        def _():
            dg_ref[...] = jnp.zeros_like(dg_ref)
            if mode == "attn":
                outs[3][...] = jnp.zeros_like(outs[3])

        dv, mv = d_ref[...], m_ref[...].astype(F32)
        r = _rstd(mv)
        mh = mv * r
        dg_ref[...] += jnp.sum(dv * mh, axis=0, keepdims=True)
        dm = _rms_bwd(mh, r, g_ref[...], dv)
        dmb = dm.astype(ACT)
        dm_ref[...] = dmb
        if mode == "attn":
            outs[3][...] += jnp.sum(dm, axis=0, keepdims=True)
        for c in range(0, k, kc):
            da = _dot_nt(dmb, w_ref[c:c + kc, :])
            da_ref[:, c:c + kc] = da.astype(ACT)

    ins = [dres, m, g, w]
    specs = [_rows(tm, d), _rows(tm, d), _full((1, d)), _full((k, d))]
    out_specs = [_rows(tm, d), _rows(tm, k), _full((1, d))]
    out_shape = [_sds((t, d), ACT), _sds((t, k), ACT), _sds((1, d), F32)]
    if mode == "attn":
        out_specs.append(_full((1, d)))
        out_shape.append(_sds((1, d), F32))
    return _call(body, name, (t // tm,), specs, out_specs, out_shape, ins, ("arbitrary",), jobs=jobs)


def _pre_bwd(dy, w, x, g, dres, name, jobs=()):
    t, d = x.shape
    tm = _row_tile(t)

    def body(dy_ref, w_ref, x_ref, g_ref, dres_ref, dx_ref, dg_ref):
        @pl.when(pl.program_id(0) == 0)
        def _():
            dg_ref[...] = jnp.zeros_like(dg_ref)

        dh = jnp.zeros((tm, d), F32)
        if w.ndim == 3:
            c = w.shape[2]
            for j in range(w.shape[0]):
                dh = dh + _dot_nt(dy_ref[:, j * c:(j + 1) * c], w_ref[j])
        else:
            n = w.shape[1]
            nc = _pick(n, (1408, 1024, 512))
            for c in range(0, n, nc):
                dh = dh + _dot_nt(dy_ref[:, c:c + nc], w_ref[:, c:c + nc])
        xv = x_ref[...]
        r = _rstd(xv)
        xh = xv * r
        dg_ref[...] += jnp.sum(dh * xh, axis=0, keepdims=True)
        dx_ref[...] = dres_ref[...] + _rms_bwd(xh, r, g_ref[...], dh)

    return _call(
        body, name, (t // tm,),
        [_rows(tm, dy.shape[1]), _full(w.shape), _rows(tm, d), _full((1, d)), _rows(tm, d)],
        [_rows(tm, d), _full((1, d))], [_sds((t, d), F32), _sds((1, d), F32)],
        (dy, w, x, g, dres), ("arbitrary",), jobs=jobs)


def _ffn_bwd(dres, f, g_post, w_dn, gu, w_gu, x, g_pre, name, jobs=()):
    t, d = dres.shape
    n = w_dn.shape[0]
    tm = _row_tile(t, 256)

    def body(d_ref, f_ref, gp_ref, wd_ref, gu_ref, wu_ref, x_ref, g_ref, df_ref, dgu_ref, dx_ref, dgp_ref, dg_ref):
        @pl.when(pl.program_id(0) == 0)
        def _():
            dgp_ref[...] = jnp.zeros_like(dgp_ref)
            dg_ref[...] = jnp.zeros_like(dg_ref)

        dv, fv = d_ref[...], f_ref[...].astype(F32)
        r = _rstd(fv)
        fh = fv * r
        dgp_ref[...] += jnp.sum(dv * fh, axis=0, keepdims=True)
        dfb = _rms_bwd(fh, r, gp_ref[...], dv).astype(ACT)
        df_ref[...] = dfb
        for c, size in _mxu_chunks(n):
            da = _dot_nt(dfb, wd_ref[c:c + size, :]).astype(ACT)
            gate, up = gu_ref[:, c:c + size], gu_ref[:, n + c:n + c + size]
            sg = _sigmoid(gate.astype(F32)).astype(ACT)
            gs = gate * sg
            dgu_ref[:, c:c + size] = da * up * (sg + gs - gs * sg)
            dgu_ref[:, n + c:n + c + size] = da * gs
        dh = _dot(dgu_ref[...], wu_ref[...])
        xv = x_ref[...]
        rx = _rstd(xv)
        xh = xv * rx
        dg_ref[...] += jnp.sum(dh * xh, axis=0, keepdims=True)
        dx_ref[...] = dv + _rms_bwd(xh, rx, g_ref[...], dh)

    return _call(
        body, name, (t // tm,),
        [_rows(tm, d), _rows(tm, d), _full((1, d)), _resident(w_dn.shape), _rows(tm, 2 * n), _resident(w_gu.shape), _rows(tm, d),
         _full((1, d))],
        [_rows(tm, d), _rows(tm, 2 * n), _rows(tm, d), _full((1, d)), _full((1, d))],
        [_sds((t, d), ACT), _sds((t, 2 * n), ACT), _sds((t, d), F32), _sds((1, d), F32), _sds((1, d), F32)],
        (dres, f, g_post, w_dn, gu, w_gu, x, g_pre), ("arbitrary",), jobs=jobs)


def _wgrad(a, b, name, out_dtype=F32, out_block=None, transposed=False, jobs=()):
    t = a.shape[0]
    tt = _pick(t, (1024,))
    steps = t // tt
    tk = _pick(a.shape[1], (1024, 1408))
    k, n = a.shape[1], b.shape[1]
    tn = _pick(n, (1024, 1408))
    per = 0
    if transposed:
        out_spec, out_shape, acc_shape = pl.BlockSpec((tn, tk), lambda i, j, s: (j, i)), _sds((n, k), out_dtype), (tn, tk)
    elif out_block is None:
        out_spec, out_shape, acc_shape = pl.BlockSpec((tk, tn), lambda i, j, s: (i, j)), _sds((k, n), out_dtype), (tk, tn)
    else:
        per = tn // out_block
        out_spec = pl.BlockSpec((per, tk, out_block), lambda i, j, s: (j, i, 0))
        out_shape, acc_shape = _sds((n // out_block, k, out_block), out_dtype), (tk, tn)

    def body(a_ref, b_ref, o_ref, acc_ref):
        s = pl.program_id(2)

        @pl.when(s == 0)
        def _():
            acc_ref[...] = jnp.zeros_like(acc_ref)

        acc_ref[...] += _dot_tn(b_ref[...], a_ref[...]) if transposed else _dot_tn(a_ref[...], b_ref[...])

        @pl.when(s == steps - 1)
        def _():
            if per >= 1:
                for p in range(per):
                    o_ref[p] = acc_ref[:, p * out_block:(p + 1) * out_block].astype(out_dtype)
            else:
                o_ref[...] = acc_ref[...].astype(out_dtype)

    res, job_res = _call(
        body, name, (k // tk, n // tn, steps),
        [pl.BlockSpec((tt, tk), lambda i, j, s: (s, i)), pl.BlockSpec((tt, tn), lambda i, j, s: (s, j))], [out_spec], [out_shape],
        (a, b), ("parallel", "parallel", "arbitrary"), scratch=[pltpu.VMEM(acc_shape, F32)], jobs=jobs)
    return res[0], job_res


def _ffn_up_fwd(x, g, w, jobs=()):
    t, d = x.shape
    n = w.shape[0] // 2
    tm = _row_tile(t)

    def body(x_ref, g_ref, w_ref, h_ref, gu_ref, a_ref):
        xv = x_ref[...]
        hb = (xv * _rstd(xv) * g_ref[...]).astype(ACT)
        h_ref[...] = hb
        for c, size in _mxu_chunks(n):
            gate = _dot_nt(hb, w_ref[c:c + size, :])
            up = _dot_nt(hb, w_ref[n + c:n + c + size, :])
            gu_ref[:, c:c + size] = gate.astype(ACT)
            gu_ref[:, n + c:n + c + size] = up.astype(ACT)
            a_ref[:, c:c + size] = (gate * _sigmoid(gate) * up).astype(ACT)

    return _call(
        body, "ffn_up_fwd", (t // tm,),
        [_rows(tm, d), _full((1, d)), _full(w.shape)],
        [_rows(tm, d), _rows(tm, 2 * n), _rows(tm, n)],
        [_sds((t, d), ACT), _sds((t, 2 * n), ACT), _sds((t, n), ACT)],
        (x, g, w), ("parallel",), jobs=jobs)


def _causal(ws):
    row = lax.broadcasted_iota(jnp.int32, ws.shape, 0)
    col = lax.broadcasted_iota(jnp.int32, ws.shape, 1)
    return jnp.where(col <= row, ws, 0.0)


def _sgu_ln(v, lg, lb):
    mu = jnp.mean(v, axis=-1, keepdims=True)
    vc = v - mu
    rs = lax.rsqrt(jnp.mean(vc * vc, axis=-1, keepdims=True) + EPS)
    vh = vc * rs
    return vh, rs, vh * lg + lb


def _sgu_fwd(x, g, w, lg, lb, ws, bs_t, jobs=()):
    t, d = x.shape
    nb, _, c = w.shape
    n = nb * c
    groups = d // WINDOW
    tm = _row_tile(t)

    def body(x_ref, g_ref, w_ref, lg_ref, lb_ref, ws_ref, bs_ref, h_ref, z_ref, y_ref, zf_ref):
        xv = x_ref[...]
        hb = (xv * _rstd(xv) * g_ref[...]).astype(ACT)
        h_ref[...] = hb
        for j in range(nb):
            zf_ref[:, j * c:(j + 1) * c] = _dot(hb, w_ref[j])
        z_ref[...] = zf_ref[...].astype(ACT)
        gs = [slice(gi * WINDOW, (gi + 1) * WINDOW) for gi in range(groups)]
        wsg = [_causal(ws_ref[gi]).astype(ACT) for gi in range(groups)]
        for r in range(0, tm, WINDOW):
            u = _gelu(zf_ref[r:r + WINDOW, :d])
            _, _, vn = _sgu_ln(_gelu(zf_ref[r:r + WINDOW, d:]), lg_ref[...], lb_ref[...])
            mixed = [_dot(wsg[gi], vn[:, gs[gi]].astype(ACT)) for gi in range(groups)]
            for gi in range(groups):
                y_ref[r:r + WINDOW, gs[gi]] = (u[:, gs[gi]] * (mixed[gi] + bs_ref[:, gi:gi + 1])).astype(ACT)

    vec = _full((1, d))
    return _call(
        body, "sgu_fwd", (t // tm,),
        [_rows(tm, d), vec, _full((nb, d, c)), vec, vec, _full((groups, WINDOW, WINDOW)), _full((WINDOW, groups))],
        [_rows(tm, d), _rows(tm, n), _rows(tm, d)], [_sds((t, d), ACT), _sds((t, n), ACT), _sds((t, d), ACT)],
        (x, g, w, lg, lb, ws, bs_t), ("parallel",), scratch=[pltpu.VMEM((tm, n), F32)], jobs=jobs)


def _sgu_mix_bwd(z, dy, lg, lb, ws, bs_t, jobs=()):
    t, n = z.shape
    d = n // 2
    groups = d // WINDOW
    tm = _row_tile(t)

    def body(z_ref, dy_ref, lg_ref, lb_ref, ws_ref, bs_ref, dz_ref, dlg_ref, dlb_ref, dws_ref, dbs_ref):
        @pl.when(pl.program_id(0) == 0)
        def _():
            for ref in (dlg_ref, dlb_ref, dws_ref, dbs_ref):
                ref[...] = jnp.zeros_like(ref)

        lane = lax.broadcasted_iota(jnp.int32, (WINDOW, groups), 1)
        gs = [slice(gi * WINDOW, (gi + 1) * WINDOW) for gi in range(groups)]
        wsg = [_causal(ws_ref[gi]).astype(ACT) for gi in range(groups)]
        for c in range(0, tm, WINDOW):
            rows = slice(c, c + WINDOW)
            zu, zv = z_ref[rows, :d].astype(F32), z_ref[rows, d:].astype(F32)
            u = _gelu(zu)
            vh, rs, vn = _sgu_ln(_gelu(zv), lg_ref[...], lb_ref[...])
            dyv = dy_ref[rows, :].astype(F32)
            vng = [vn[:, gs[gi]].astype(ACT) for gi in range(groups)]
            dmix = dyv * u
            dmb = [dmix[:, gs[gi]].astype(ACT) for gi in range(groups)]
            mixed = [_dot(wsg[gi], vng[gi]) for gi in range(groups)]
            dvn_parts = [_dot_tn(wsg[gi], dmb[gi]) for gi in range(groups)]
            dws_parts = [_dot_nt(dmb[gi], vng[gi]) for gi in range(groups)]
            for gi in range(groups):
                dz_ref[rows, gs[gi]] = (dyv[:, gs[gi]] * (mixed[gi] + bs_ref[:, gi:gi + 1]) * _gelu_grad(zu[:, gs[gi]])).astype(ACT)
                dws_ref[:, gs[gi]] += _causal(dws_parts[gi])
                dbs_ref[...] += jnp.where(lane == gi, jnp.sum(dmix[:, gs[gi]], axis=-1, keepdims=True), 0.0)
            dvn = jnp.concatenate(dvn_parts, axis=-1)
            dlg_ref[...] += jnp.sum(dvn * vh, axis=0, keepdims=True)
            dlb_ref[...] += jnp.sum(dvn, axis=0, keepdims=True)
            dvh = dvn * lg_ref[...]
            dv = rs * (dvh - jnp.mean(dvh, axis=-1, keepdims=True) - vh * jnp.mean(dvh * vh, axis=-1, keepdims=True))
            dz_ref[rows, d:] = (dv * _gelu_grad(zv)).astype(ACT)

    vec = _full((1, d))
    return _call(
        body, "sgu_mix_bwd", (t // tm,),
        [_rows(tm, n), _rows(tm, d), vec, vec, _full((groups, WINDOW, WINDOW)), _full((WINDOW, groups))],
        [_rows(tm, n), vec, vec, _full((WINDOW, d)), _full((WINDOW, groups))],
        [_sds((t, n), ACT), _sds((1, d), F32), _sds((1, d), F32), _sds((WINDOW, d), F32), _sds((WINDOW, groups), F32)],
        (z, dy, lg, lb, ws, bs_t), ("arbitrary",), jobs=jobs)


AG_COPIES = 8


def _prepare(sources, dtypes, n_gather, name):
    n = len(sources)
    arrays, where = [], []
    for parts, layer in sources:
        found = []
        for part in parts:
            known = [i for i, v in enumerate(arrays) if v is part]
            if not known:
                arrays.append(part)
            found.append(known[0] if known else len(arrays) - 1)
        where.append((found, layer))
    shapes = [(sum(p.shape[1] for p in parts), parts[0].shape[2]) for parts, _ in sources]

    def body(*refs):
        ins, refs = refs[:len(arrays)], refs[len(arrays):]
        stage, outs = refs[:n], refs[n:n + n_gather]
        send, recv, local_sem = refs[n + n_gather:]
        x, y, c = _place()
        me, sibling, beside_x, beside_y, far = (x, y, c), (x, y, 1 - c), (1 - x, y, c), (x, 1 - y, c), (1 - x, 1 - y, c)
        slot = lambda dev: 4 * dev[0] + 2 * dev[1] + dev[2]
        other = lambda dev: (dev[0], dev[1], 1 - c)
        whole = lambda a: (0, shapes[a][0])
        cuts = [rows // 2 if rows % 32 == 0 else rows for rows, _ in shapes]
        first = lambda a: (0, cuts[a])
        rest = lambda a: (cuts[a], shapes[a][0] - cuts[a])

        def copy(a, k, block, rows, to, src=None):
            dst = outs[a].at[block, pl.ds(*rows)]
            return pltpu.make_async_remote_copy(
                src_ref=dst if src is None else src, dst_ref=dst, send_sem=send.at[a * AG_COPIES + k],
                recv_sem=recv.at[a * AG_COPIES + k], device_id=to, device_id_type=MESH)

        def cast(a):
            found, layer = where[a]
            at = 0
            for i in found:
                rows = arrays[i].shape[1]
                stage[a][at:at + rows, :] = ins[i][layer].astype(dtypes[a])
                at += rows

        for a in range(n_gather):
            cast(a)
        started = []
        for a in range(n_gather):
            own = pltpu.make_async_copy(stage[a], outs[a].at[slot(me)], local_sem.at[a])
            own.start()
            started.append(own)
        sends = []
        for a in range(n_gather):
            sends += [copy(a, k, slot(me), whole(a), to, src=stage[a]) for k, to in enumerate((sibling, beside_x, beside_y))]
        for cp in sends:
            cp.start()
        for a in range(n_gather, n):
            cast(a)
        for a in range(n_gather):
            copy(a, 1, slot(beside_x), whole(a), me).wait_recv()
            copy(a, 2, slot(beside_y), whole(a), me).wait_recv()
            passed = [copy(a, 3, slot(beside_x), first(a), beside_y), copy(a, 5, slot(beside_x), whole(a), sibling),
                      copy(a, 6, slot(beside_y), whole(a), sibling)]
            if rest(a)[1]:
                passed.append(copy(a, 4, slot(beside_y), rest(a), beside_x))
            for cp in passed:
                cp.start()
            sends += passed
        for a in range(n_gather):
            copy(a, 3, slot(far), first(a), me).wait_recv()
            if rest(a)[1]:
                copy(a, 4, slot(far), rest(a), me).wait_recv()
            passed = copy(a, 7, slot(far), whole(a), sibling)
            passed.start()
            sends.append(passed)
        for a in range(n_gather):
            for k, source in ((0, me), (5, beside_x), (6, beside_y), (7, far)):
                copy(a, k, slot(other(source)), whole(a), me).wait_recv()
        for cp in sends:
            cp.wait_send()
        for own in started:
            own.wait()

    res = pl.pallas_call(
        body, name=name,
        in_specs=[IN_VMEM] * len(arrays), out_specs=[IN_VMEM] * n + [ANY] * n_gather,
        out_shape=[_sds(s, dt) for s, dt in zip(shapes, dtypes)]
        + [_sds((N_DEV,) + s, dt) for s, dt in zip(shapes[:n_gather], dtypes)],
        scratch_shapes=[pltpu.SemaphoreType.DMA((n_gather * AG_COPIES,)), pltpu.SemaphoreType.DMA((n_gather * AG_COPIES,)),
                        pltpu.SemaphoreType.DMA((n_gather,))],
        compiler_params=pltpu.CompilerParams(vmem_limit_bytes=VMEM_LIMIT_BYTES),
    )(*arrays)
    return list(res[:n]), list(res[n:])


def _pair_sum(g, r, core, name):
    _, _, rows, cols = g.shape
    tr = _pick(rows, (512, 256, 128))

    def body(core_ref, g_ref, r_ref, p_ref):
        del core_ref
        p_ref[...] = (g_ref[...].astype(F32) + r_ref[...].astype(F32)).astype(p_ref.dtype)

    return pl.pallas_call(
        body, name=name,
        grid_spec=pltpu.PrefetchScalarGridSpec(
            num_scalar_prefetch=1, grid=(4, rows // tr),
            in_specs=[pl.BlockSpec((None, None, tr, cols), lambda q, i, core_ref: (q, core_ref[0], i, 0)),
                      pl.BlockSpec((None, tr, cols), lambda q, i, core_ref: (q, i, 0))],
            out_specs=pl.BlockSpec((None, tr, cols), lambda q, i, core_ref: (q, i, 0))),
        out_shape=_sds((4, rows, cols), g.dtype),
        compiler_params=_params("parallel", "parallel"),
    )(core, g, r)


def _adam(w, g, m, v):
    m2 = ADAM_B1 * m + (1.0 - ADAM_B1) * g
    v2 = ADAM_B2 * v + (1.0 - ADAM_B2) * (g * g)
    m_hat = m2 / (1.0 - ADAM_B1 ** ADAM_STEP)
    v_hat = v2 / (1.0 - ADAM_B2 ** ADAM_STEP)
    return -ADAM_LR * (m_hat / (jnp.sqrt(v_hat) + ADAM_EPS) + ADAM_WD * w), m2, v2


def _shard_update(own, index, received, w, m, v, layer, so_far, name):
    _, rows, cols = w.shape
    n_recv = received.shape[0]
    tr = _pick(rows, (512, 256, 128))

    def body(index_ref, p_ref, q_ref, w_ref, m_ref, v_ref, *rest):
        del index_ref
        g_out, d_out, m_out, v_out = rest[-4:]
        g = p_ref[...].astype(F32)
        for s in range(n_recv):
            g = g + q_ref[s].astype(F32)
        g_out[...] = g
        d_out[...], m_out[...], v_out[...] = _adam(w_ref[...], g, m_ref[...], v_ref[...])

    tile = pl.BlockSpec((None, tr, cols), lambda i, index_ref: (layer, i, 0))
    kept = list(so_far) if so_far is not None else []
    return pl.pallas_call(
        body, name=name,
        grid_spec=pltpu.PrefetchScalarGridSpec(
            num_scalar_prefetch=1, grid=(rows // tr,),
            in_specs=[pl.BlockSpec((None, tr, cols), lambda i, index_ref: (index_ref[0], i, 0)),
                      pl.BlockSpec((n_recv, tr, cols), lambda i, index_ref: (0, i, 0)), tile, tile, tile] + [ANY] * len(kept),
            out_specs=[tile] * 4),
        out_shape=[_sds(w.shape, F32)] * 4,
        input_output_aliases={6 + i: i for i in range(len(kept))},
        compiler_params=_params("parallel"),
    )(index, own, received, w, m, v, *kept)


def _natural_pieces(shape, r, c):
    if tuple(shape[-2:]) == (r, c) and all(n == 1 for n in shape[:-2]):
        return [((0,) * (len(shape) - 2), (0, c))]
    groups, width = shape[1], shape[3]
    assert len(shape) == 4 and shape[0] == 1 and shape[2] == r and groups * width == c, (shape, r, c)
    return [((0, g), (g * width, width)) for g in range(groups)]


def _replicated_update(shares, where, params, name):
    flat = [a for p in params if p is not None for a in p]

    def body(s_ref, *refs):
        ins, outs = refs[:len(flat)], refs[len(flat):]
        total = s_ref[0]
        for dev in range(1, N_DEV):
            total = total + s_ref[dev]
        i_at = o_at = 0
        for ranges, p in zip(where, params):
            chunks = [total[r0:r0 + r, :c] for r0, r, c in ranges]
            g2d = chunks[0] if len(chunks) == 1 else jnp.concatenate(chunks, axis=1)
            if p is None:
                outs[o_at][...] = g2d
                o_at += 1
                continue
            w_ref, m_ref, v_ref = ins[i_at:i_at + 3]
            for idx, (c0, cols) in _natural_pieces(p[0].shape, *g2d.shape):
                at = idx + (slice(None), slice(None))
                g = g2d[:, c0:c0 + cols]
                outs[o_at][at] = g
                outs[o_at + 1][at], outs[o_at + 2][at], outs[o_at + 3][at] = _adam(w_ref[at], g, m_ref[at], v_ref[at])
            i_at, o_at = i_at + 3, o_at + 4

    out_shape = []
    for ranges, p in zip(where, params):
        out_shape += [_sds((ranges[0][1], sum(c for _, _, c in ranges)), F32)] if p is None else [_sds(p[0].shape, F32)] * 4
    res = pl.pallas_call(
        body, name=name, out_shape=out_shape, compiler_params=pltpu.CompilerParams(vmem_limit_bytes=VMEM_LIMIT_BYTES),
    )(shares, *flat)
    out, at = [], 0
    for p in params:
        out.append(list(res[at:at + (1 if p is None else 4)]))
        at += 1 if p is None else 4
    return out


def _rope_tables(t):
    half = HEAD_DIM // 2
    inv_freq = ROPE_THETA ** (-(jnp.arange(half, dtype=F32) * 2.0) / HEAD_DIM)
    ang = jnp.arange(t, dtype=jnp.int32).astype(F32)[:, None] * inv_freq[None, :]
    cos, sin = jnp.cos(ang), jnp.sin(ang)
    return jnp.tile(jnp.concatenate([cos, cos], axis=1), (1, 2)), jnp.tile(jnp.concatenate([-sin, sin], axis=1), (1, 2))


def _dup_heads(w, d):
    kv = (w.shape[-1] - d) // 2
    lead = w.shape[:-1]

    def dup(part):
        heads = part.reshape(lead + (kv // HEAD_DIM, 1, HEAD_DIM))
        return jnp.broadcast_to(heads, lead + (kv // HEAD_DIM, 2, HEAD_DIM)).reshape(lead + (2 * kv,))

    return jnp.concatenate([w[..., :d], dup(w[..., d:d + kv]), dup(w[..., d + kv:])], axis=-1)


def _fold_heads(dw, d):
    kv = (dw.shape[-1] - d) // 4
    lead = dw.shape[:-1]

    def fold(part):
        return part.reshape(lead + (kv // HEAD_DIM, 2, HEAD_DIM)).sum(axis=-2).reshape(lead + (kv,))

    return jnp.concatenate([dw[..., :d], fold(dw[..., d:d + 2 * kv]), fold(dw[..., d + 2 * kv:])], axis=-1)


def _columns_full(gathered):
    _, rows, c = gathered.shape
    return jnp.transpose(gathered, (1, 0, 2)).reshape(rows, N_DEV * c)


def _rows_full(gathered):
    return gathered.reshape(-1, gathered.shape[-1])


def _columns_blocked(full):
    rows, cols = full.shape
    return jnp.transpose(full.reshape(rows, N_DEV, cols // N_DEV), (1, 0, 2)).astype(ACT)


def _rows_blocked(full):
    return full.reshape(N_DEV, full.shape[0] // N_DEV, full.shape[1]).astype(ACT)


def _pack_rows(parts, width):
    rows, where, at = [], [], 0
    for v in parts:
        r, c = v.shape
        n_rows = -(-r // 8) * 8
        chunks = []
        for c0 in range(0, c, width):
            chunk = v[:, c0:c0 + width].astype(F32)
            rows.append(jnp.pad(chunk, ((0, n_rows - r), (0, width - chunk.shape[1]))))
            chunks.append((at, r, chunk.shape[1]))
            at += n_rows
        where.append(chunks)
    return jnp.concatenate(rows, axis=0), where


def _halves(block):
    half = block.shape[0] // 2
    return (0, half), (half, half)


def _whole(block):
    return (0, block.shape[0])


EARLY = ("attn_w_qkv", "sgu_ln", "attn_w_o")
LATE = ("sgu_w_in", "sgu_w_out", "gu0", "gu1", "dn0", "dn1")
COLUMN_SHARDED = ("attn_w_qkv", "sgu_w_in", "gu0", "gu1")
BEFORE_ATTN = ("norm_mix_post", "norm_ffn_pre", "norm_ffn_post", "attn_b_o", "sgu_w_spatial", "sgu_b_spatial")
AFTER_ATTN = ("attn_b_qkv", "attn_sinks")
LAST = ("norm_mix_pre",)
WEIGHTS = ("norm_mix_pre", "norm_mix_post", "norm_ffn_pre", "norm_ffn_post", "attn_w_qkv", "attn_b_qkv", "attn_sinks", "attn_w_o",
           "attn_b_o", "sgu_w_in", "sgu_ln_g", "sgu_ln_b", "sgu_w_spatial", "sgu_b_spatial", "sgu_w_out", "ffn_w_gate_up", "ffn_w_down")


LAYER_OF = {"gu0": ("ffn_w_gate_up", 0), "gu1": ("ffn_w_gate_up", 1), "dn0": ("ffn_w_down", 0), "dn1": ("ffn_w_down", 1)}


def _source(tree, name):
    if name == "sgu_ln":
        return [tree["sgu_ln_g"][None], tree["sgu_ln_b"][None]], 0
    leaf, layer = LAYER_OF.get(name, (name, 0))
    return [tree[leaf]], layer


def kernel(x, norm_mix_pre, norm_mix_post, norm_ffn_pre, norm_ffn_post, attn_w_qkv, attn_b_qkv, attn_sinks, attn_w_o, attn_b_o, sgu_w_in, sgu_ln_g, sgu_ln_b, sgu_w_spatial, sgu_b_spatial, sgu_w_out, ffn_w_gate_up, ffn_w_down, loss_target, m_norm_mix_pre, m_norm_mix_post, m_norm_ffn_pre, m_norm_ffn_post, m_attn_w_qkv, m_attn_b_qkv, m_attn_sinks, m_attn_w_o, m_attn_b_o, m_sgu_w_in, m_sgu_ln_g, m_sgu_ln_b, m_sgu_w_spatial, m_sgu_b_spatial, m_sgu_w_out, m_ffn_w_gate_up, m_ffn_w_down, v_norm_mix_pre, v_norm_mix_post, v_norm_ffn_pre, v_norm_ffn_post, v_attn_w_qkv, v_attn_b_qkv, v_attn_sinks, v_attn_w_o, v_attn_b_o, v_sgu_w_in, v_sgu_ln_g, v_sgu_ln_b, v_sgu_w_spatial, v_sgu_b_spatial, v_sgu_w_out, v_ffn_w_gate_up, v_ffn_w_down):
    w = dict(norm_mix_pre=norm_mix_pre, norm_mix_post=norm_mix_post, norm_ffn_pre=norm_ffn_pre, norm_ffn_post=norm_ffn_post,
             attn_w_qkv=attn_w_qkv, attn_b_qkv=attn_b_qkv, attn_sinks=attn_sinks, attn_w_o=attn_w_o, attn_b_o=attn_b_o,
             sgu_w_in=sgu_w_in, sgu_ln_g=sgu_ln_g, sgu_ln_b=sgu_ln_b, sgu_w_spatial=sgu_w_spatial, sgu_b_spatial=sgu_b_spatial,
             sgu_w_out=sgu_w_out, ffn_w_gate_up=ffn_w_gate_up, ffn_w_down=ffn_w_down)
    mom = dict(norm_mix_pre=m_norm_mix_pre, norm_mix_post=m_norm_mix_post, norm_ffn_pre=m_norm_ffn_pre, norm_ffn_post=m_norm_ffn_post,
               attn_w_qkv=m_attn_w_qkv, attn_b_qkv=m_attn_b_qkv, attn_sinks=m_attn_sinks, attn_w_o=m_attn_w_o, attn_b_o=m_attn_b_o,
               sgu_w_in=m_sgu_w_in, sgu_ln_g=m_sgu_ln_g, sgu_ln_b=m_sgu_ln_b, sgu_w_spatial=m_sgu_w_spatial,
               sgu_b_spatial=m_sgu_b_spatial, sgu_w_out=m_sgu_w_out, ffn_w_gate_up=m_ffn_w_gate_up, ffn_w_down=m_ffn_w_down)
    var = dict(norm_mix_pre=v_norm_mix_pre, norm_mix_post=v_norm_mix_post, norm_ffn_pre=v_norm_ffn_pre, norm_ffn_post=v_norm_ffn_post,
               attn_w_qkv=v_attn_w_qkv, attn_b_qkv=v_attn_b_qkv, attn_sinks=v_attn_sinks, attn_w_o=v_attn_w_o, attn_b_o=v_attn_b_o,
               sgu_w_in=v_sgu_w_in, sgu_ln_g=v_sgu_ln_g, sgu_ln_b=v_sgu_ln_b, sgu_w_spatial=v_sgu_w_spatial,
               sgu_b_spatial=v_sgu_b_spatial, sgu_w_out=v_sgu_w_out, ffn_w_gate_up=v_ffn_w_gate_up, ffn_w_down=v_ffn_w_down)
    w, mom, var = [{**tree, "ffn_w_gate_up": jnp.swapaxes(tree["ffn_w_gate_up"], 1, 2)} for tree in (w, mom, var)]
    xs, target = x[0], loss_target[0]
    t, d = xs.shape
    row = lambda v: v.reshape(1, -1).astype(F32)
    core = lax.axis_index("c").astype(jnp.int32).reshape(1)
    chip = (2 * lax.axis_index("x") + lax.axis_index("y")).astype(jnp.int32).reshape(1)
    names = EARLY + LATE
    staged, early = _prepare([_source(w, n) for n in names], [F32 if n == "sgu_ln" else ACT for n in names], len(EARLY),
                             "weights_prepare")
    stage = dict(zip(names, staged))
    w_qkv = _dup_heads(_columns_full(early[0]), d)
    lg, lb = row(early[1][:, 0, :]), row(early[1][:, 1, :])
    w_o = _rows_full(early[2])
    b_qkv = _dup_heads(row(attn_b_qkv), d)
    sinks = attn_sinks.reshape(1, -1).astype(F32)
    ws = sgu_w_spatial[0].astype(F32)
    bs_t = sgu_b_spatial[0].astype(F32).T
    cos, sin = _rope_tables(t)
    gather = lambda name, so_far, **pieces: _gather_job(stage[name], so_far, **pieces)
    a_half = lambda name: _halves(stage[name])[0]
    b_half = lambda name: _halves(stage[name])[1]
    whole = lambda name: _whole(stage[name])

    def pieces(name, n):
        rows = stage[name].shape[0] // n
        return [(i * rows, rows) for i in range(n)]

    ways = lambda parts: [(piece, i % 2) for i, piece in enumerate(parts)]
    relay = lambda name, so_far, **steps: _relay_gather_job(stage[name], so_far, **steps)
    gu0_parts, gu1_parts = pieces("gu0", 4), pieces("gu1", 4)
    dn0_parts, dn1_parts, in_parts, out_parts = pieces("dn0", 2), pieces("dn1", 2), pieces("sgu_w_in", 2), pieces("sgu_w_out", 2)
    (h0, q, kdup, vdup), ((g_gu0,),) = _attn_qkv_fwd(
        xs, row(norm_mix_pre[0]), w_qkv, b_qkv, cos, sin, jobs=[relay("gu0", None, sends=gu0_parts)])
    (o,), ((g_gu0,), (g_dn0,)) = _attn_fwd(q, kdup, vdup, sinks, jobs=[
        relay("gu0", g_gu0, relays=ways(gu0_parts), forwards=gu0_parts), relay("dn0", None, sends=dn0_parts)])
    (m0, x1), ((g_gu0,), (g_dn0,), (g_in,)) = _proj_res(o, w_o, row(attn_b_o), row(norm_mix_post[0]), xs, "attn_out_fwd", jobs=[
        relay("gu0", g_gu0, far=gu0_parts), relay("dn0", g_dn0, relays=ways(dn0_parts), forwards=dn0_parts),
        relay("sgu_w_in", None, sends=in_parts)])
    w_gu0 = _rows_full(g_gu0)
    (h1, gu0, a0), ((g_dn0,), (g_in,), (g_out,), (g_gu1,)) = _ffn_up_fwd(x1, row(norm_ffn_pre[0]), w_gu0, jobs=[
        relay("dn0", g_dn0, far=dn0_parts), relay("sgu_w_in", g_in, relays=ways(in_parts), forwards=in_parts),
        relay("sgu_w_out", None, sends=out_parts), relay("gu1", None, sends=gu1_parts[:2])])
    w_dn0 = _rows_full(g_dn0)
    (f0, x2), ((g_in,), (g_out,), (g_gu1,)) = _proj_res(a0, w_dn0, None, row(norm_ffn_post[0]), x1, "ffn_down_fwd0", jobs=[
        relay("sgu_w_in", g_in, far=in_parts), relay("sgu_w_out", g_out, relays=ways(out_parts), forwards=out_parts),
        relay("gu1", g_gu1, sends=gu1_parts[2:], relays=ways(gu1_parts[:2]), forwards=gu1_parts[:2])])
    w_in = g_in
    (h2, z, y), ((g_out,), (g_gu1,), (g_dn1,)) = _sgu_fwd(x2, row(norm_mix_pre[1]), w_in, lg, lb, ws, bs_t, jobs=[
        relay("sgu_w_out", g_out, far=out_parts),
        relay("gu1", g_gu1, relays=ways(gu1_parts[2:]), forwards=gu1_parts[2:], far=gu1_parts[:2]),
        relay("dn1", None, sends=dn1_parts)])
    w_out = _rows_full(g_out)
    (m1, x3), ((g_gu1,), (g_dn1,)) = _proj_res(y, w_out, None, row(norm_mix_post[1]), x2, "sgu_out_fwd", jobs=[
        relay("gu1", g_gu1, far=gu1_parts[2:]), relay("dn1", g_dn1, relays=ways(dn1_parts), forwards=dn1_parts)])
    w_gu1 = _rows_full(g_gu1)
    (h3, gu1, a1), ((g_dn1,),) = _ffn_up_fwd(x3, row(norm_ffn_pre[1]), w_gu1, jobs=[relay("dn1", g_dn1, far=dn1_parts)])
    w_dn1 = _rows_full(g_dn1)
    (f1, dx4, loss_tile), _ = _proj_res(a1, w_dn1, None, row(norm_ffn_post[1]), x3, "ffn_down_fwd1", target=target)

    to_sibling = lambda g: _scatter_job([g], 4, _to_sibling)
    pair = lambda g, r, name: _pair_sum(g.reshape((4, 2) + g.shape[1:]), r, core, "pair_sum_" + name)
    to_chips = lambda p, prev=None, piece=None: _scatter_job([p], 3, _to_owner_chip, prev=prev, piece=piece)
    out_g, out_d, out_m, out_v = {}, {}, {}, {}

    trees = [{**tree, "sgu_ln": jnp.stack([tree["sgu_ln_g"], tree["sgu_ln_b"]], axis=1)} for tree in (w, mom, var)]
    so_far = {}

    def update(name, own, index, received):
        leaf, layer = LAYER_OF.get(name, (name, 0))
        so_far[leaf] = _shard_update(own, index, received, *[tree[leaf] for tree in trees], layer, so_far.get(leaf), "update_" + name)
        for out, val in zip((out_g, out_d, out_m, out_v), so_far[leaf]):
            out[leaf] = val

    def finish(names, n_extra, shares, where, name):
        res = _replicated_update(shares, where, [(w[n], mom[n], var[n]) for n in names] + [None] * n_extra, name)
        for n, vals in zip(names, res):
            out_g[n], out_d[n], out_m[n], out_v[n] = vals
        return [vals[0] for vals in res[len(names):]]

    first_half = lambda p: _halves(p[0])[0]
    second_half = lambda p: _halves(p[0])[1]
    (df1, dgu1, dx3, dg_ffn_post1, dg_ffn_pre1), _ = _ffn_bwd(
        dx4, f1, row(norm_ffn_post[1]), w_dn1, gu1, w_gu1, x3, row(norm_ffn_pre[1]), "ffn_bwd1")
    b_gu1, _ = _wgrad(h3, dgu1, "ffn_up_wgrad1", ACT, transposed=True)
    b_gu1 = _rows_blocked(b_gu1)
    (dm1, dy, dg_mix_post1), ((r_gu1,),) = _post_bwd(
        dx3, m1, row(norm_mix_post[1]), w_out, "sgu", "sgu_out_bwd", jobs=[to_sibling(b_gu1)])
    p_gu1 = pair(b_gu1, r_gu1, "gu1")
    dw_dn1, ((q_gu1,),) = _wgrad(a1, df1, "ffn_down_wgrad1", ACT, jobs=[to_chips(p_gu1, piece=first_half(p_gu1))])
    b_dn1 = _rows_blocked(dw_dn1)
    dw_out, ((r_dn1,),) = _wgrad(y, dm1, "sgu_out_wgrad", ACT, jobs=[to_sibling(b_dn1)])
    p_dn1 = pair(b_dn1, r_dn1, "dn1")
    b_out = _rows_blocked(dw_out)
    (dz, dlg, dlb, dws, dbs_t), ((q_gu1,), (r_out,)) = _sgu_mix_bwd(z, dy, lg, lb, ws, bs_t, jobs=[
        to_chips(p_gu1, prev=[q_gu1], piece=second_half(p_gu1)), to_sibling(b_out)])
    update("gu1", p_gu1, chip, q_gu1)
    p_out = pair(b_out, r_out, "sgu_w_out")
    b_in, _ = _wgrad(h2, dz, "sgu_in_wgrad", ACT, out_block=stage["sgu_w_in"].shape[1])
    b_ln = jnp.stack([dlg.reshape(N_DEV, -1), dlb.reshape(N_DEV, -1)], axis=1)
    (dx2, dg_mix_pre1), _ = _pre_bwd(dz, w_in, x2, row(norm_mix_pre[1]), dx3, "sgu_in_bwd")
    (df0, dgu0, dx1, dg_ffn_post0, dg_ffn_pre0), ((q_dn1,), (q_out,), (r_in,), (r_ln,)) = _ffn_bwd(
        dx2, f0, row(norm_ffn_post[0]), w_dn0, gu0, w_gu0, x1, row(norm_ffn_pre[0]), "ffn_bwd0",
        jobs=[to_chips(p_dn1), to_chips(p_out), to_sibling(b_in), to_sibling(b_ln)])
    update("dn1", p_dn1, chip, q_dn1)
    update("sgu_w_out", p_out, chip, q_out)
    p_in, p_ln = pair(b_in, r_in, "sgu_w_in"), pair(b_ln, r_ln, "sgu_ln")
    b_gu0, ((q_in,), (q_ln,)) = _wgrad(h1, dgu0, "ffn_up_wgrad0", ACT, transposed=True, jobs=[to_chips(p_in), to_chips(p_ln)])
    update("sgu_w_in", p_in, chip, q_in)
    update("sgu_ln", p_ln, chip, q_ln)
    b_gu0 = _rows_blocked(b_gu0)
    (dm0, do, dg_mix_post0, db_o), ((r_gu0,),) = _post_bwd(
        dx1, m0, row(norm_mix_post[0]), w_o, "attn", "attn_out_bwd", jobs=[to_sibling(b_gu0)])
    p_gu0 = pair(b_gu0, r_gu0, "gu0")
    dw_dn0, ((q_gu0,),) = _wgrad(a0, df0, "ffn_down_wgrad0", ACT, jobs=[to_chips(p_gu0, piece=first_half(p_gu0))])
    b_dn0 = _rows_blocked(dw_dn0)
    dw_o, ((r_dn0,),) = _wgrad(o, dm0, "attn_out_wgrad", ACT, jobs=[to_sibling(b_dn0)])
    p_dn0 = pair(b_dn0, r_dn0, "dn0")
    b_o = _rows_blocked(dw_o)
    grads = {
        "norm_mix_post": jnp.concatenate([dg_mix_post0, dg_mix_post1], axis=0),
        "norm_ffn_pre": jnp.concatenate([dg_ffn_pre0, dg_ffn_pre1], axis=0),
        "norm_ffn_post": jnp.concatenate([dg_ffn_post0, dg_ffn_post1], axis=0),
        "attn_b_o": db_o,
        "sgu_w_spatial": dws,
        "sgu_b_spatial": dbs_t.T,
    }
    shares1, where1 = _pack_rows([grads[name] for name in BEFORE_ATTN], d)
    (dqkv, db_qkv, dsink), ((q_gu0,), (q_dn0,), (r_o,), (g_shares1,)) = _attn_bwd(q, kdup, vdup, do, sinks, cos, sin, jobs=[
        to_chips(p_gu0, prev=[q_gu0], piece=second_half(p_gu0)), to_chips(p_dn0), to_sibling(b_o),
        _gather_job(shares1, None, sends=[_whole(shares1)])])
    update("gu0", p_gu0, chip, q_gu0)
    update("dn0", p_dn0, chip, q_dn0)
    p_o = pair(b_o, r_o, "attn_w_o")
    grads.update({"attn_b_qkv": _fold_heads(db_qkv, d), "attn_sinks": dsink[:1, :d // HEAD_DIM]})
    shares2, where2 = _pack_rows([grads[name] for name in AFTER_ATTN] + [loss_tile[:1, :1]], d)
    (dx0, dg_mix_pre0), _ = _pre_bwd(dqkv, w_qkv, xs, row(norm_mix_pre[0]), dx1, "attn_qkv_bwd")
    grads["norm_mix_pre"] = jnp.concatenate([dg_mix_pre0, dg_mix_pre1], axis=0)
    shares3, where3 = _pack_rows([grads[name] for name in LAST], d)
    dw_qkv, ((q_o,), (g_shares1,), (g_shares2,), (g_shares3,)) = _wgrad(h0, dqkv, "attn_qkv_wgrad", jobs=[
        to_chips(p_o), _gather_job(shares1, g_shares1, forwards=[_whole(shares1)]),
        _gather_job(shares2, None, sends=[_whole(shares2)]), _gather_job(shares3, None, sends=[_whole(shares3)])])
    update("attn_w_o", p_o, chip, q_o)
    b_qkv_grad = _columns_blocked(_fold_heads(dw_qkv, d))
    (r_qkv,), (g_shares2,), (g_shares3,) = _copies_only([
        to_sibling(b_qkv_grad), _gather_job(shares2, g_shares2, forwards=[_whole(shares2)]),
        _gather_job(shares3, g_shares3, forwards=[_whole(shares3)])], "grads_tail_to_sibling")
    p_qkv = pair(b_qkv_grad, r_qkv, "attn_w_qkv")
    (q_qkv,), = _copies_only([to_chips(p_qkv)], "grads_tail_to_owner_chip")
    update("attn_w_qkv", p_qkv, chip, q_qkv)

    finish(BEFORE_ATTN, 0, g_shares1, where1, "replicated_update_before_attn")
    loss = finish(AFTER_ATTN, 1, g_shares2, where2, "replicated_update_after_attn")[0][0, 0]
    finish(LAST, 0, g_shares3, where3, "replicated_update_last")

    for out in (out_g, out_d, out_m, out_v):
        out["sgu_ln_g"], out["sgu_ln_b"] = out["sgu_ln"][:, 0, :], out["sgu_ln"][:, 1, :]
        out["ffn_w_gate_up"] = jnp.swapaxes(out["ffn_w_gate_up"], 1, 2)

    return (loss, dx0[None], *[out_g[n] for n in WEIGHTS], *[out_d[n] for n in WEIGHTS],
            *[out_m[n] for n in WEIGHTS], *[out_v[n] for n in WEIGHTS])
```

```python
import functools
import math
import operator

import jax
import jax.numpy as jnp
from jax import lax
from jax.experimental import pallas as pl
from jax.experimental.pallas import tpu as pltpu

F32 = jnp.float32
ACT = jnp.bfloat16

EPS = 1e-6
HEAD_DIM = 64
PAIR = 2 * HEAD_DIM
GQA_GROUP = 4
WINDOW = 128
ROPE_THETA = 10000.0
SCALE = HEAD_DIM ** -0.5
MASKED = -1e30
LANES = 128
MXU_WIDTH = 256

ADAM_LR, ADAM_B1, ADAM_B2, ADAM_EPS, ADAM_WD, ADAM_STEP = 0.001, 0.9, 0.999, 1e-08, 0.01, 10

N_DEV = 8
VMEM_LIMIT_BYTES = 56 * 1024 * 1024
MESH = pl.DeviceIdType.MESH
ANY = pl.BlockSpec(memory_space=pl.ANY)
IN_VMEM = pl.BlockSpec(memory_space=pltpu.VMEM)


def _pick(n, candidates):
    for c in candidates:
        if n % c == 0:
            return c
    return n


def _row_tile(t, most=512):
    return _pick(t, (most,))


def _mxu_chunks(n, most=4 * MXU_WIDTH):
    assert n % MXU_WIDTH == 0 and most % MXU_WIDTH == 0
    return [(c, min(most, n - c)) for c in range(0, n, most)]


def _params(*sem):
    return pltpu.CompilerParams(dimension_semantics=sem, vmem_limit_bytes=VMEM_LIMIT_BYTES)


def _full(shape):
    return pl.BlockSpec(shape, lambda *_: (0,) * len(shape))


def _resident(shape):
    return pl.BlockSpec(shape, lambda *_: (0,) * len(shape), pipeline_mode=pl.Buffered(1))


def _rows(tm, n):
    return pl.BlockSpec((tm, n), lambda i: (i, 0))


def _sds(shape, dtype):
    return jax.ShapeDtypeStruct(shape, dtype)


def _place():
    return lax.axis_index("x"), lax.axis_index("y"), lax.axis_index("c")


def _other_chips(x, y):
    return [(1 - x, y), (x, 1 - y), (1 - x, 1 - y)]


class _Job:
    def __init__(self, inputs, out_shape, aliases, emit, n_remote, n_local=0):
        self.inputs, self.out_shape, self.aliases, self.emit = list(inputs), list(out_shape), dict(aliases), emit
        self.n_remote, self.n_local = n_remote, n_local


def _call(body, name, grid, in_specs, out_specs, out_shape, args, sem, scratch=(), jobs=()):
    n_in, n_out, n_scr = len(args), len(out_shape), len(scratch)
    j_in = [a for job in jobs for a in job.inputs]
    j_out = [s for job in jobs for s in job.out_shape]
    aliases, at_in, at_out = {}, n_in, n_out
    for job in jobs:
        aliases.update({at_in + i: at_out + o for i, o in job.aliases.items()})
        at_in, at_out = at_in + len(job.inputs), at_out + len(job.out_shape)
    n_remote = sum(job.n_remote for job in jobs)
    n_local = sum(job.n_local for job in jobs)

    def wrapped(*refs):
        ins, jin = refs[:n_in], refs[n_in:n_in + len(j_in)]
        rest = refs[n_in + len(j_in):]
        outs, jout = rest[:n_out], rest[n_out:n_out + len(j_out)]
        scr = rest[n_out + len(j_out):n_out + len(j_out) + n_scr]
        if not jobs:
            body(*ins, *outs, *scr)
            return
        send, recv, local = rest[n_out + len(j_out) + n_scr:]
        ids = [pl.program_id(a) for a in range(len(grid))]
        first = functools.reduce(operator.and_, [i == 0 for i in ids])
        last = functools.reduce(operator.and_, [i == g - 1 for i, g in zip(ids, grid)])

        def descriptors():
            place, found, i, o, r, l = _place(), [], 0, 0, 0, 0
            for job in jobs:
                for src, dst, to in job.emit(jin[i:i + len(job.inputs)], jout[o:o + len(job.out_shape)], place):
                    if to is None:
                        found.append(pltpu.make_async_copy(src, dst, local.at[l]))
                        l += 1
                    else:
                        found.append(pltpu.make_async_remote_copy(src_ref=src, dst_ref=dst, send_sem=send.at[r], recv_sem=recv.at[r],
                                                                  device_id=to, device_id_type=MESH))
                        r += 1
                i, o = i + len(job.inputs), o + len(job.out_shape)
            return found

        @pl.when(first)
        def _():
            for cp in descriptors():
                cp.start()

        body(*ins, *outs, *scr)

        @pl.when(last)
        def _():
            for cp in descriptors():
                cp.wait()

    sems = [pltpu.SemaphoreType.DMA((n_remote,)), pltpu.SemaphoreType.DMA((n_remote,)),
            pltpu.SemaphoreType.DMA((max(n_local, 1),))] if jobs else []
    res = pl.pallas_call(
        wrapped, name=name, grid=grid,
        in_specs=list(in_specs) + [ANY] * len(j_in), out_specs=list(out_specs) + [ANY] * len(j_out),
        out_shape=list(out_shape) + j_out, scratch_shapes=list(scratch) + sems, input_output_aliases=aliases,
        compiler_params=_params(*(("arbitrary",) * len(grid) if jobs else sem)),
    )(*args, *j_in)
    job_res, at = [], n_out
    for job in jobs:
        job_res.append(list(res[at:at + len(job.out_shape)]))
        at += len(job.out_shape)
    return list(res[:n_out]), job_res


def _copies_only(jobs, name):
    def body(o_ref):
        o_ref[...] = jnp.zeros_like(o_ref)

    return _call(body, name, (1,), [], [_full((8, LANES))], [_sds((8, LANES), F32)], (), ("arbitrary",), jobs=jobs)[1]


def _gather_job(stage, gathered, sends=(), forwards=()):
    shape = _sds((N_DEV,) + stage.shape, stage.dtype)
    inputs = ([stage] if sends else []) + ([gathered] if gathered is not None else [])
    aliases = {len(inputs) - 1: 0} if gathered is not None else {}

    def emit(ins, outs, place):
        x, y, c = place
        sibling, chips, mine, g = (x, y, 1 - c), _other_chips(x, y), 4 * x + 2 * y + c, outs[0]
        found = []
        for r0, nr in sends:
            src, dst = ins[0].at[pl.ds(r0, nr)], g.at[mine, pl.ds(r0, nr)]
            found += [(src, dst, None), (src, dst, sibling)] + [(src, dst, (px, py, c)) for px, py in chips]
        for r0, nr in forwards:
            for px, py in chips:
                block = 4 * px + 2 * py + c
                found.append((ins[-1].at[block, pl.ds(r0, nr)], g.at[block, pl.ds(r0, nr)], sibling))
        return found

    return _Job(inputs, [shape], aliases, emit, 4 * len(sends) + 3 * len(forwards), len(sends))


def _relay_gather_job(stage, gathered, sends=(), relays=(), forwards=(), far=()):
    shape = _sds((N_DEV,) + stage.shape, stage.dtype)
    inputs = ([stage] if sends else []) + ([gathered] if gathered is not None else [])
    aliases = {len(inputs) - 1: 0} if gathered is not None else {}

    def emit(ins, outs, place):
        x, y, c = place
        sibling, beside_x, beside_y, g = (x, y, 1 - c), (1 - x, y, c), (x, 1 - y, c), outs[0]
        slot = lambda dev: 4 * dev[0] + 2 * dev[1] + dev[2]
        found = []
        for r0, nr in sends:
            src, dst = ins[0].at[pl.ds(r0, nr)], g.at[slot((x, y, c)), pl.ds(r0, nr)]
            found += [(src, dst, None), (src, dst, sibling), (src, dst, beside_x), (src, dst, beside_y)]

        def passed_on(block, piece, to):
            return ins[-1].at[block, pl.ds(*piece)], g.at[block, pl.ds(*piece)], to

        for piece, way in relays:
            found.append(passed_on(slot(beside_x), piece, beside_y) if way == 0 else passed_on(slot(beside_y), piece, beside_x))
        for piece in forwards:
            found += [passed_on(slot(beside_x), piece, sibling), passed_on(slot(beside_y), piece, sibling)]
        for piece in far:
            found.append(passed_on(slot((1 - x, 1 - y, c)), piece, sibling))
        return found

    return _Job(inputs, [shape], aliases, emit, 3 * len(sends) + len(relays) + 2 * len(forwards) + len(far), len(sends))


def _scatter_job(arrays, n_slots, route, prev=None, piece=None):
    shapes = [_sds((n_slots,) + v.shape[1:], v.dtype) for v in arrays]
    inputs = list(arrays) + (list(prev) if prev else [])
    aliases = {len(arrays) + i: i for i in range(len(arrays))} if prev else {}

    def emit(ins, outs, place):
        found = []
        for a in range(len(arrays)):
            for s in range(n_slots):
                block, to = route(s, *place)
                if piece is None:
                    found.append((ins[a].at[block], outs[a].at[s], to))
                else:
                    found.append((ins[a].at[block, pl.ds(*piece)], outs[a].at[s, pl.ds(*piece)], to))
        return found

    return _Job(inputs, shapes, aliases, emit, len(arrays) * n_slots)


def _to_sibling(s, x, y, c):
    return 2 * s + (1 - c), (x, y, 1 - c)


def _to_owner_chip(s, x, y, c):
    px, py = _other_chips(x, y)[s]
    return 2 * px + py, (px, py, c)


def _rstd(x):
    return lax.rsqrt(jnp.mean(x * x, axis=-1, keepdims=True) + EPS)


def _rms_bwd(xhat, r, g, dy):
    dxh = dy * g
    return r * (dxh - xhat * jnp.mean(dxh * xhat, axis=-1, keepdims=True))


def _first_half(rows):
    lane = lax.broadcasted_iota(jnp.int32, (rows, PAIR), 1)
    return (lane % HEAD_DIM) < (HEAD_DIM // 2)


def _rot_half(v, first):
    return jnp.where(first, pltpu.roll(v, PAIR - HEAD_DIM // 2, 1), pltpu.roll(v, HEAD_DIM // 2, 1))


def _rope(v, cos, sin, first):
    return v * cos + _rot_half(v, first) * sin


def _rope_t(dv, cos, sin, first):
    return dv * cos + _rot_half(dv * sin, first)


def _dot(a, b):
    return jnp.dot(a, b, preferred_element_type=F32)


def _dot_nt(a, b):
    return lax.dot_general(a, b, (((1,), (1,)), ((), ())), preferred_element_type=F32)


def _dot_tn(a, b):
    return lax.dot_general(a, b, (((0,), (0,)), ((), ())), preferred_element_type=F32)


def _sigmoid(v):
    return 1.0 / (1.0 + jnp.exp(-v))


_GELU_K = math.sqrt(2.0 / math.pi)
_GELU_C = 0.044715


def _gelu(v):
    return v * (0.5 * (1.0 + jnp.tanh(_GELU_K * (v + _GELU_C * (v * v * v)))))


def _gelu_grad(v):
    t = jnp.tanh(_GELU_K * (v + _GELU_C * (v * v * v)))
    return 0.5 * (1.0 + t) + 0.5 * v * (1.0 - t * t) * (_GELU_K * (1.0 + 3.0 * _GELU_C * (v * v)))


def _attn_qkv_fwd(x, g, w, b, cos, sin, jobs=()):
    t, d = x.shape
    n = w.shape[1]
    kd = (n - d) // 2
    tm = _row_tile(t)
    ch = _pick(d, (512,))

    def body(x_ref, g_ref, w_ref, b_ref, cos_ref, sin_ref, h_ref, q_ref, k_ref, v_ref):
        xv = x_ref[...]
        hb = (xv * _rstd(xv) * g_ref[...]).astype(ACT)
        h_ref[...] = hb
        cosv, sinv = cos_ref[...], sin_ref[...]
        first = _first_half(tm)

        def proj(lo, width):
            return _dot(hb, w_ref[:, lo:lo + width]) + b_ref[:, lo:lo + width]

        for c in range(0, d, ch):
            y = proj(c, ch)
            for s in range(0, ch, PAIR):
                q_ref[:, c + s:c + s + PAIR] = (_rope(y[:, s:s + PAIR], cosv, sinv, first) * SCALE).astype(ACT)
        y = proj(d, kd)
        for s in range(0, kd, PAIR):
            k_ref[:, s:s + PAIR] = _rope(y[:, s:s + PAIR], cosv, sinv, first).astype(ACT)
        v_ref[...] = proj(d + kd, kd).astype(ACT)

    return _call(
        body, "attn_qkv_fwd", (t // tm,),
        [_rows(tm, d), _full((1, d)), _full((d, n)), _full((1, n)), _rows(tm, PAIR), _rows(tm, PAIR)],
        [_rows(tm, d), _rows(tm, d), _rows(tm, kd), _rows(tm, kd)],
        [_sds((t, d), ACT), _sds((t, d), ACT), _sds((t, kd), ACT), _sds((t, kd), ACT)],
        (x, g, w, b, cos, sin), ("parallel",), jobs=jobs)


STACK = GQA_GROUP * WINDOW


def _band_mask(has_prev):
    row = lax.broadcasted_iota(jnp.int32, (STACK, 2 * WINDOW), 0) % WINDOW
    col = lax.broadcasted_iota(jnp.int32, (STACK, 2 * WINDOW), 1)
    return ((col < WINDOW) & (col > row) & has_prev) | ((col >= WINDOW) & (col - WINDOW <= row))


def _lane_halves():
    lane = lax.broadcasted_iota(jnp.int32, (1, PAIR), 1)
    return lane < HEAD_DIM, lane >= HEAD_DIM


def _stack_heads(ref, h, halves):
    parts = []
    for p in range(2):
        pair = ref[:, (2 * h + p) * PAIR:(2 * h + p + 1) * PAIR]
        parts += [jnp.where(half, pair, jnp.zeros_like(pair)) for half in halves]
    return jnp.concatenate(parts, axis=0)


def _sink_column(sink_ref, h):
    row = lax.broadcasted_iota(jnp.int32, (STACK, 1), 0)
    col = jnp.full((STACK, 1), sink_ref[0, GQA_GROUP * h + GQA_GROUP - 1], F32)
    for j in range(GQA_GROUP - 2, -1, -1):
        col = jnp.where(row < (j + 1) * WINDOW, sink_ref[0, GQA_GROUP * h + j], col)
    return col


def _probs(scores, valid, sink):
    s = jnp.where(valid, scores, MASKED)
    m = jnp.maximum(jnp.max(s, axis=-1, keepdims=True), sink)
    p = jnp.exp(s - m)
    psink = jnp.exp(sink - m)
    inv = 1.0 / (jnp.sum(p, axis=-1, keepdims=True) + psink)
    return p * inv, psink * inv


def _attn_fwd(q, kd_, vd, sinks, jobs=()):
    t, d = q.shape
    kd = kd_.shape[1]
    cur = lambda n: (n, 0)
    prev = lambda n: (jnp.maximum(n - 1, 0), 0)

    def body(sink_ref, q_ref, kc_ref, kp_ref, vc_ref, vp_ref, o_ref):
        valid = _band_mask(pl.program_id(0) > 0)
        halves = _lane_halves()
        heads = range(kd // PAIR)
        hs = [slice(h * PAIR, (h + 1) * PAIR) for h in heads]
        scores = [_dot_nt(_stack_heads(q_ref, h, halves), jnp.concatenate([kp_ref[:, hs[h]], kc_ref[:, hs[h]]], axis=0)) for h in heads]
        probs = [_probs(scores[h], valid, _sink_column(sink_ref, h))[0].astype(ACT) for h in heads]
        for h in heads:
            v2 = jnp.concatenate([vp_ref[:, hs[h]], vc_ref[:, hs[h]]], axis=0)
            vs = jnp.concatenate([jnp.where(half, v2, jnp.zeros_like(v2)) for half in halves], axis=0)
            pr = probs[h]
            for p in range(2):
                both = jnp.concatenate([pr[2 * p * WINDOW:(2 * p + 1) * WINDOW], pr[(2 * p + 1) * WINDOW:(2 * p + 2) * WINDOW]], axis=1)
                o_ref[:, (2 * h + p) * PAIR:(2 * h + p + 1) * PAIR] = _dot(both, vs).astype(ACT)

    kv_c, kv_p = pl.BlockSpec((WINDOW, kd), cur), pl.BlockSpec((WINDOW, kd), prev)
    return _call(
        body, "attn_fwd", (t // WINDOW,),
        [pl.BlockSpec(memory_space=pltpu.SMEM), pl.BlockSpec((WINDOW, d), cur), kv_c, kv_p, kv_c, kv_p],
        [pl.BlockSpec((WINDOW, d), cur)], [_sds((t, d), ACT)],
        (sinks, q, kd_, kd_, vd, vd), ("parallel",), jobs=jobs)


def _attn_bwd(q, kd_, vd, do, sinks, cos, sin, jobs=()):
    t, d = q.shape
    kd = kd_.shape[1]
    nb = t // WINDOW
    n_out = d + 2 * kd
    cur = lambda n: (jnp.minimum(n, nb - 1), 0)
    prev = lambda n: (jnp.maximum(jnp.minimum(n, nb - 1) - 1, 0), 0)
    late = lambda n: (jnp.maximum(n - 1, 0), 0)

    def body(sink_ref, q_ref, kc_ref, kp_ref, vc_ref, vp_ref, do_ref, cosc_ref, sinc_ref, cosp_ref, sinp_ref,
             out_ref, db_ref, dsink_ref, dq_keep, dk_keep, dv_keep):
        n = pl.program_id(0)

        @pl.when(n == 0)
        def _():
            for ref in (db_ref, dsink_ref, dq_keep, dk_keep, dv_keep):
                ref[...] = jnp.zeros_like(ref)

        valid = _band_mask(n > 0) & (n < nb)
        halves = _lane_halves()
        first = _first_half(WINDOW)
        slot = lax.broadcasted_iota(jnp.int32, dsink_ref.shape, 1)
        top = lax.broadcasted_iota(jnp.int32, dsink_ref.shape, 0) == 0
        dsink = jnp.zeros(dsink_ref.shape, F32)

        def emit(cols, done):
            out_ref[:, cols] = done.astype(ACT)
            db_ref[:, cols] += jnp.sum(done.astype(F32), axis=0, keepdims=True)

        heads = range(kd // PAIR)
        hs = [slice(h * PAIR, (h + 1) * PAIR) for h in heads]
        k2 = [jnp.concatenate([kp_ref[:, hs[h]], kc_ref[:, hs[h]]], axis=0) for h in heads]
        v2 = [jnp.concatenate([vp_ref[:, hs[h]], vc_ref[:, hs[h]]], axis=0) for h in heads]
        qs = [_stack_heads(q_ref, h, halves) for h in heads]
        dos = [_stack_heads(do_ref, h, halves) for h in heads]
        scores = [_dot_nt(qs[h], k2[h]) for h in heads]
        dps = [_dot_nt(dos[h], v2[h]) for h in heads]
        prs, dss = [], []
        for h in heads:
            pr, psink = _probs(scores[h], valid, _sink_column(sink_ref, h))
            delta = jnp.sum(pr * dps[h], axis=-1, keepdims=True)
            dss.append((pr * (dps[h] - delta)).astype(ACT))
            prs.append(pr.astype(ACT))
            leak = psink * delta
            for j in range(GQA_GROUP):
                share = jnp.sum(leak[j * WINDOW:(j + 1) * WINDOW], axis=0, keepdims=True)
                dsink = dsink - jnp.where(top & (slot == GQA_GROUP * h + j), share, 0.0)
        dqs = [_dot(dss[h], k2[h]) for h in heads]
        dk2 = [_dot_tn(dss[h], qs[h]) for h in heads]
        dv2 = [_dot_tn(prs[h], dos[h]) for h in heads]
        for h in heads:
            for p in range(2):
                ps = slice((2 * h + p) * PAIR, (2 * h + p + 1) * PAIR)
                dqp = jnp.where(halves[0], dqs[h][2 * p * WINDOW:(2 * p + 1) * WINDOW], dqs[h][(2 * p + 1) * WINDOW:(2 * p + 2) * WINDOW])
                emit(ps, dq_keep[:, ps])
                dq_keep[:, ps] = (_rope_t(dqp, cosc_ref[...], sinc_ref[...], first) * SCALE).astype(ACT)
            ks = slice(d + h * PAIR, d + (h + 1) * PAIR)
            vs = slice(d + kd + h * PAIR, d + kd + (h + 1) * PAIR)
            emit(ks, dk_keep[:, hs[h]] + _rope_t(dk2[h][:WINDOW], cosp_ref[...], sinp_ref[...], first))
            dk_keep[:, hs[h]] = _rope_t(dk2[h][WINDOW:], cosc_ref[...], sinc_ref[...], first)
            emit(vs, dv_keep[:, hs[h]] + dv2[h][:WINDOW])
            dv_keep[:, hs[h]] = dv2[h][WINDOW:]
        dsink_ref[...] += dsink

    kv_c, kv_p = pl.BlockSpec((WINDOW, kd), cur), pl.BlockSpec((WINDOW, kd), prev)
    tab_c, tab_p = pl.BlockSpec((WINDOW, PAIR), cur), pl.BlockSpec((WINDOW, PAIR), prev)
    return _call(
        body, "attn_bwd", (nb + 1,),
        [pl.BlockSpec(memory_space=pltpu.SMEM), pl.BlockSpec((WINDOW, d), cur), kv_c, kv_p, kv_c, kv_p,
         pl.BlockSpec((WINDOW, d), cur), tab_c, tab_c, tab_p, tab_p],
        [pl.BlockSpec((WINDOW, n_out), late), _full((1, n_out)), _full((8, LANES))],
        [_sds((t, n_out), ACT), _sds((1, n_out), F32), _sds((8, LANES), F32)],
        (sinks, q, kd_, kd_, vd, vd, do, cos, sin, cos, sin), ("arbitrary",),
        scratch=[pltpu.VMEM((WINDOW, d), ACT), pltpu.VMEM((WINDOW, kd), F32), pltpu.VMEM((WINDOW, kd), F32)], jobs=jobs)


def _proj_res(a, w, b, g, x, name, target=None, jobs=()):
    t = a.shape[0]
    k, d = w.shape
    tm = _row_tile(t)
    has_bias = b is not None

    def body(*refs):
        a_ref, w_ref = refs[:2]
        b_ref = refs[2] if has_bias else None
        g_ref, x_ref, m_ref, xo_ref = refs[2 + has_bias:]
        m = _dot(a_ref[...], w_ref[...])
        if has_bias:
            m = m + b_ref[...]
        m_ref[...] = m.astype(ACT)
        xo_ref[...] = x_ref[...] + (m * _rstd(m)) * g_ref[...]

    def body_with_loss(a_ref, w_ref, g_ref, x_ref, t_ref, m_ref, dy_ref, loss_ref):
        @pl.when(pl.program_id(0) == 0)
        def _():
            loss_ref[...] = jnp.zeros_like(loss_ref)

        body(a_ref, w_ref, g_ref, x_ref, m_ref, dy_ref)
        err = dy_ref[...] - t_ref[...]
        dy_ref[...] = err * (1.0 / d)
        loss_ref[...] += 0.5 * jnp.sum(jnp.mean(err * err, axis=-1, keepdims=True), axis=0, keepdims=True)

    ins = [a, w] + ([b] if has_bias else []) + [g, x]
    specs = [_rows(tm, k), _full((k, d))] + ([_full((1, d))] if has_bias else []) + [_full((1, d)), _rows(tm, d)]
    if target is not None:
        return _call(body_with_loss, name, (t // tm,), specs + [_rows(tm, d)], [_rows(tm, d), _rows(tm, d), _full((8, LANES))],
                     [_sds((t, d), ACT), _sds((t, d), F32), _sds((8, LANES), F32)], ins + [target], ("arbitrary",), jobs=jobs)
    return _call(body, name, (t // tm,), specs, [_rows(tm, d), _rows(tm, d)], [_sds((t, d), ACT), _sds((t, d), F32)], ins,
                 ("parallel",), jobs=jobs)


def _post_bwd(dres, m, g, w, mode, name, jobs=()):
    t, d = dres.shape
    k = w.shape[0]
    tm = _row_tile(t)
    kc = _pick(k, (1408, 1024, 512))

    def body(*refs):
        d_ref, m_ref, g_ref, w_ref = refs[:4]
        outs = refs[4:]
        dm_ref, da_ref, dg_ref = outs[:3]
        i = pl.program_id(0)

        @pl.when(i == 0)
        def _():
            dg_ref[...] = jnp.zeros_like(dg_ref)
            if mode == "attn":
                outs[3][...] = jnp.zeros_like(outs[3])

        dv, mv = d_ref[...], m_ref[...].astype(F32)
        r = _rstd(mv)
        mh = mv * r
        dg_ref[...] += jnp.sum(dv * mh, axis=0, keepdims=True)
        dm = _rms_bwd(mh, r, g_ref[...], dv)
        dmb = dm.astype(ACT)
        dm_ref[...] = dmb
        if mode == "attn":
            outs[3][...] += jnp.sum(dm, axis=0, keepdims=True)
        for c in range(0, k, kc):
            da = _dot_nt(dmb, w_ref[c:c + kc, :])
            da_ref[:, c:c + kc] = da.astype(ACT)

    ins = [dres, m, g, w]
    specs = [_rows(tm, d), _rows(tm, d), _full((1, d)), _full((k, d))]
    out_specs = [_rows(tm, d), _rows(tm, k), _full((1, d))]
    out_shape = [_sds((t, d), ACT), _sds((t, k), ACT), _sds((1, d), F32)]
    if mode == "attn":
        out_specs.append(_full((1, d)))
        out_shape.append(_sds((1, d), F32))
    return _call(body, name, (t // tm,), specs, out_specs, out_shape, ins, ("arbitrary",), jobs=jobs)


def _pre_bwd(dy, w, x, g, dres, name, jobs=()):
    t, d = x.shape
    tm = _row_tile(t)

    def body(dy_ref, w_ref, x_ref, g_ref, dres_ref, dx_ref, dg_ref):
        @pl.when(pl.program_id(0) == 0)
        def _():
            dg_ref[...] = jnp.zeros_like(dg_ref)

        dh = jnp.zeros((tm, d), F32)
        if w.ndim == 3:
            c = w.shape[2]
            for j in range(w.shape[0]):
                dh = dh + _dot_nt(dy_ref[:, j * c:(j + 1) * c], w_ref[j])
        else:
            n = w.shape[1]
            nc = _pick(n, (1408, 1024, 512))
            for c in range(0, n, nc):
                dh = dh + _dot_nt(dy_ref[:, c:c + nc], w_ref[:, c:c + nc])
        xv = x_ref[...]
        r = _rstd(xv)
        xh = xv * r
        dg_ref[...] += jnp.sum(dh * xh, axis=0, keepdims=True)
        dx_ref[...] = dres_ref[...] + _rms_bwd(xh, r, g_ref[...], dh)

    return _call(
        body, name, (t // tm,),
        [_rows(tm, dy.shape[1]), _full(w.shape), _rows(tm, d), _full((1, d)), _rows(tm, d)],
        [_rows(tm, d), _full((1, d))], [_sds((t, d), F32), _sds((1, d), F32)],
        (dy, w, x, g, dres), ("arbitrary",), jobs=jobs)


def _ffn_bwd(dres, f, g_post, w_dn, gu, w_gu, x, g_pre, name, jobs=()):
    t, d = dres.shape
    n = w_dn.shape[0]
    tm = _row_tile(t, 256)

    def body(d_ref, f_ref, gp_ref, wd_ref, gu_ref, wu_ref, x_ref, g_ref, df_ref, dgu_ref, dx_ref, dgp_ref, dg_ref):
        @pl.when(pl.program_id(0) == 0)
        def _():
            dgp_ref[...] = jnp.zeros_like(dgp_ref)
            dg_ref[...] = jnp.zeros_like(dg_ref)

        dv, fv = d_ref[...], f_ref[...].astype(F32)
        r = _rstd(fv)
        fh = fv * r
        dgp_ref[...] += jnp.sum(dv * fh, axis=0, keepdims=True)
        dfb = _rms_bwd(fh, r, gp_ref[...], dv).astype(ACT)
        df_ref[...] = dfb
        for c, size in _mxu_chunks(n):
            da = _dot_nt(dfb, wd_ref[c:c + size, :]).astype(ACT)
            gate, up = gu_ref[:, c:c + size], gu_ref[:, n + c:n + c + size]
            sg = _sigmoid(gate.astype(F32)).astype(ACT)
            gs = gate * sg
            dgu_ref[:, c:c + size] = da * up * (sg + gs - gs * sg)
            dgu_ref[:, n + c:n + c + size] = da * gs
        dh = _dot(dgu_ref[...], wu_ref[...])
        xv = x_ref[...]
        rx = _rstd(xv)
        xh = xv * rx
        dg_ref[...] += jnp.sum(dh * xh, axis=0, keepdims=True)
        dx_ref[...] = dv + _rms_bwd(xh, rx, g_ref[...], dh)

    return _call(
        body, name, (t // tm,),
        [_rows(tm, d), _rows(tm, d), _full((1, d)), _resident(w_dn.shape), _rows(tm, 2 * n), _resident(w_gu.shape), _rows(tm, d),
         _full((1, d))],
        [_rows(tm, d), _rows(tm, 2 * n), _rows(tm, d), _full((1, d)), _full((1, d))],
        [_sds((t, d), ACT), _sds((t, 2 * n), ACT), _sds((t, d), F32), _sds((1, d), F32), _sds((1, d), F32)],
        (dres, f, g_post, w_dn, gu, w_gu, x, g_pre), ("arbitrary",), jobs=jobs)


def _wgrad(a, b, name, out_dtype=F32, out_block=None, transposed=False, jobs=()):
    t = a.shape[0]
    tt = _pick(t, (1024,))
    steps = t // tt
    tk = _pick(a.shape[1], (1024, 1408))
    k, n = a.shape[1], b.shape[1]
    tn = _pick(n, (1024, 1408))
    per = 0
    if transposed:
        out_spec, out_shape, acc_shape = pl.BlockSpec((tn, tk), lambda i, j, s: (j, i)), _sds((n, k), out_dtype), (tn, tk)
    elif out_block is None:
        out_spec, out_shape, acc_shape = pl.BlockSpec((tk, tn), lambda i, j, s: (i, j)), _sds((k, n), out_dtype), (tk, tn)
    else:
        per = tn // out_block
        out_spec = pl.BlockSpec((per, tk, out_block), lambda i, j, s: (j, i, 0))
        out_shape, acc_shape = _sds((n // out_block, k, out_block), out_dtype), (tk, tn)

    def body(a_ref, b_ref, o_ref, acc_ref):
        s = pl.program_id(2)

        @pl.when(s == 0)
        def _():
            acc_ref[...] = jnp.zeros_like(acc_ref)

        acc_ref[...] += _dot_tn(b_ref[...], a_ref[...]) if transposed else _dot_tn(a_ref[...], b_ref[...])

        @pl.when(s == steps - 1)
        def _():
            if per >= 1:
                for p in range(per):
                    o_ref[p] = acc_ref[:, p * out_block:(p + 1) * out_block].astype(out_dtype)
            else:
                o_ref[...] = acc_ref[...].astype(out_dtype)

    res, job_res = _call(
        body, name, (k // tk, n // tn, steps),
        [pl.BlockSpec((tt, tk), lambda i, j, s: (s, i)), pl.BlockSpec((tt, tn), lambda i, j, s: (s, j))], [out_spec], [out_shape],
        (a, b), ("parallel", "parallel", "arbitrary"), scratch=[pltpu.VMEM(acc_shape, F32)], jobs=jobs)
    return res[0], job_res


def _ffn_up_fwd(x, g, w, jobs=()):
    t, d = x.shape
    n = w.shape[0] // 2
    tm = _row_tile(t)

    def body(x_ref, g_ref, w_ref, h_ref, gu_ref, a_ref):
        xv = x_ref[...]
        hb = (xv * _rstd(xv) * g_ref[...]).astype(ACT)
        h_ref[...] = hb
        for c, size in _mxu_chunks(n):
            gate = _dot_nt(hb, w_ref[c:c + size, :])
            up = _dot_nt(hb, w_ref[n + c:n + c + size, :])
            gu_ref[:, c:c + size] = gate.astype(ACT)
            gu_ref[:, n + c:n + c + size] = up.astype(ACT)
            a_ref[:, c:c + size] = (gate * _sigmoid(gate) * up).astype(ACT)

    return _call(
        body, "ffn_up_fwd", (t // tm,),
        [_rows(tm, d), _full((1, d)), _full(w.shape)],
        [_rows(tm, d), _rows(tm, 2 * n), _rows(tm, n)],
        [_sds((t, d), ACT), _sds((t, 2 * n), ACT), _sds((t, n), ACT)],
        (x, g, w), ("parallel",), jobs=jobs)


def _causal(ws):
    row = lax.broadcasted_iota(jnp.int32, ws.shape, 0)
    col = lax.broadcasted_iota(jnp.int32, ws.shape, 1)
    return jnp.where(col <= row, ws, 0.0)


def _sgu_ln(v, lg, lb):
    mu = jnp.mean(v, axis=-1, keepdims=True)
    vc = v - mu
    rs = lax.rsqrt(jnp.mean(vc * vc, axis=-1, keepdims=True) + EPS)
    vh = vc * rs
    return vh, rs, vh * lg + lb


def _sgu_fwd(x, g, w, lg, lb, ws, bs_t, jobs=()):
    t, d = x.shape
    nb, _, c = w.shape
    n = nb * c
    groups = d // WINDOW
    tm = _row_tile(t)

    def body(x_ref, g_ref, w_ref, lg_ref, lb_ref, ws_ref, bs_ref, h_ref, z_ref, y_ref, zf_ref):
        xv = x_ref[...]
        hb = (xv * _rstd(xv) * g_ref[...]).astype(ACT)
        h_ref[...] = hb
        for j in range(nb):
            zf_ref[:, j * c:(j + 1) * c] = _dot(hb, w_ref[j])
        z_ref[...] = zf_ref[...].astype(ACT)
        gs = [slice(gi * WINDOW, (gi + 1) * WINDOW) for gi in range(groups)]
        wsg = [_causal(ws_ref[gi]).astype(ACT) for gi in range(groups)]
        for r in range(0, tm, WINDOW):
            u = _gelu(zf_ref[r:r + WINDOW, :d])
            _, _, vn = _sgu_ln(_gelu(zf_ref[r:r + WINDOW, d:]), lg_ref[...], lb_ref[...])
            mixed = [_dot(wsg[gi], vn[:, gs[gi]].astype(ACT)) for gi in range(groups)]
            for gi in range(groups):
                y_ref[r:r + WINDOW, gs[gi]] = (u[:, gs[gi]] * (mixed[gi] + bs_ref[:, gi:gi + 1])).astype(ACT)

    vec = _full((1, d))
    return _call(
        body, "sgu_fwd", (t // tm,),
        [_rows(tm, d), vec, _full((nb, d, c)), vec, vec, _full((groups, WINDOW, WINDOW)), _full((WINDOW, groups))],
        [_rows(tm, d), _rows(tm, n), _rows(tm, d)], [_sds((t, d), ACT), _sds((t, n), ACT), _sds((t, d), ACT)],
        (x, g, w, lg, lb, ws, bs_t), ("parallel",), scratch=[pltpu.VMEM((tm, n), F32)], jobs=jobs)


def _sgu_mix_bwd(z, dy, lg, lb, ws, bs_t, jobs=()):
    t, n = z.shape
    d = n // 2
    groups = d // WINDOW
    tm = _row_tile(t)

    def body(z_ref, dy_ref, lg_ref, lb_ref, ws_ref, bs_ref, dz_ref, dlg_ref, dlb_ref, dws_ref, dbs_ref):
        @pl.when(pl.program_id(0) == 0)
        def _():
            for ref in (dlg_ref, dlb_ref, dws_ref, dbs_ref):
                ref[...] = jnp.zeros_like(ref)

        lane = lax.broadcasted_iota(jnp.int32, (WINDOW, groups), 1)
        gs = [slice(gi * WINDOW, (gi + 1) * WINDOW) for gi in range(groups)]
        wsg = [_causal(ws_ref[gi]).astype(ACT) for gi in range(groups)]
        for c in range(0, tm, WINDOW):
            rows = slice(c, c + WINDOW)
            zu, zv = z_ref[rows, :d].astype(F32), z_ref[rows, d:].astype(F32)
            u = _gelu(zu)
            vh, rs, vn = _sgu_ln(_gelu(zv), lg_ref[...], lb_ref[...])
            dyv = dy_ref[rows, :].astype(F32)
            vng = [vn[:, gs[gi]].astype(ACT) for gi in range(groups)]
            dmix = dyv * u
            dmb = [dmix[:, gs[gi]].astype(ACT) for gi in range(groups)]
            mixed = [_dot(wsg[gi], vng[gi]) for gi in range(groups)]
            dvn_parts = [_dot_tn(wsg[gi], dmb[gi]) for gi in range(groups)]
            dws_parts = [_dot_nt(dmb[gi], vng[gi]) for gi in range(groups)]
            for gi in range(groups):
                dz_ref[rows, gs[gi]] = (dyv[:, gs[gi]] * (mixed[gi] + bs_ref[:, gi:gi + 1]) * _gelu_grad(zu[:, gs[gi]])).astype(ACT)
                dws_ref[:, gs[gi]] += _causal(dws_parts[gi])
                dbs_ref[...] += jnp.where(lane == gi, jnp.sum(dmix[:, gs[gi]], axis=-1, keepdims=True), 0.0)
            dvn = jnp.concatenate(dvn_parts, axis=-1)
            dlg_ref[...] += jnp.sum(dvn * vh, axis=0, keepdims=True)
            dlb_ref[...] += jnp.sum(dvn, axis=0, keepdims=True)
            dvh = dvn * lg_ref[...]
            dv = rs * (dvh - jnp.mean(dvh, axis=-1, keepdims=True) - vh * jnp.mean(dvh * vh, axis=-1, keepdims=True))
            dz_ref[rows, d:] = (dv * _gelu_grad(zv)).astype(ACT)

    vec = _full((1, d))
    return _call(
        body, "sgu_mix_bwd", (t // tm,),
        [_rows(tm, n), _rows(tm, d), vec, vec, _full((groups, WINDOW, WINDOW)), _full((WINDOW, groups))],
        [_rows(tm, n), vec, vec, _full((WINDOW, d)), _full((WINDOW, groups))],
        [_sds((t, n), ACT), _sds((1, d), F32), _sds((1, d), F32), _sds((WINDOW, d), F32), _sds((WINDOW, groups), F32)],
        (z, dy, lg, lb, ws, bs_t), ("arbitrary",), jobs=jobs)


AG_COPIES = 8


def _prepare(sources, dtypes, n_gather, name):
    n = len(sources)
    arrays, where = [], []
    for parts, layer in sources:
        found = []
        for part in parts:
            known = [i for i, v in enumerate(arrays) if v is part]
            if not known:
                arrays.append(part)
            found.append(known[0] if known else len(arrays) - 1)
        where.append((found, layer))
    shapes = [(sum(p.shape[1] for p in parts), parts[0].shape[2]) for parts, _ in sources]

    def body(*refs):
        ins, refs = refs[:len(arrays)], refs[len(arrays):]
        stage, outs = refs[:n], refs[n:n + n_gather]
        send, recv, local_sem = refs[n + n_gather:]
        x, y, c = _place()
        me, sibling, beside_x, beside_y, far = (x, y, c), (x, y, 1 - c), (1 - x, y, c), (x, 1 - y, c), (1 - x, 1 - y, c)
        slot = lambda dev: 4 * dev[0] + 2 * dev[1] + dev[2]
        other = lambda dev: (dev[0], dev[1], 1 - c)
        whole = lambda a: (0, shapes[a][0])
        cuts = [rows // 2 if rows % 32 == 0 else rows for rows, _ in shapes]
        first = lambda a: (0, cuts[a])
        rest = lambda a: (cuts[a], shapes[a][0] - cuts[a])

        def copy(a, k, block, rows, to, src=None):
            dst = outs[a].at[block, pl.ds(*rows)]
            return pltpu.make_async_remote_copy(
                src_ref=dst if src is None else src, dst_ref=dst, send_sem=send.at[a * AG_COPIES + k],
                recv_sem=recv.at[a * AG_COPIES + k], device_id=to, device_id_type=MESH)

        def cast(a):
            found, layer = where[a]
            at = 0
            for i in found:
                rows = arrays[i].shape[1]
                stage[a][at:at + rows, :] = ins[i][layer].astype(dtypes[a])
                at += rows

        for a in range(n_gather):
            cast(a)
        started = []
        for a in range(n_gather):
            own = pltpu.make_async_copy(stage[a], outs[a].at[slot(me)], local_sem.at[a])
            own.start()
            started.append(own)
        sends = []
        for a in range(n_gather):
            sends += [copy(a, k, slot(me), whole(a), to, src=stage[a]) for k, to in enumerate((sibling, beside_x, beside_y))]
        for cp in sends:
            cp.start()
        for a in range(n_gather, n):
            cast(a)
        for a in range(n_gather):
            copy(a, 1, slot(beside_x), whole(a), me).wait_recv()
            copy(a, 2, slot(beside_y), whole(a), me).wait_recv()
            passed = [copy(a, 3, slot(beside_x), first(a), beside_y), copy(a, 5, slot(beside_x), whole(a), sibling),
                      copy(a, 6, slot(beside_y), whole(a), sibling)]
            if rest(a)[1]:
                passed.append(copy(a, 4, slot(beside_y), rest(a), beside_x))
            for cp in passed:
                cp.start()
            sends += passed
        for a in range(n_gather):
            copy(a, 3, slot(far), first(a), me).wait_recv()
            if rest(a)[1]:
                copy(a, 4, slot(far), rest(a), me).wait_recv()
            passed = copy(a, 7, slot(far), whole(a), sibling)
            passed.start()
            sends.append(passed)
        for a in range(n_gather):
            for k, source in ((0, me), (5, beside_x), (6, beside_y), (7, far)):
                copy(a, k, slot(other(source)), whole(a), me).wait_recv()
        for cp in sends:
            cp.wait_send()
        for own in started:
            own.wait()

    res = pl.pallas_call(
        body, name=name,
        in_specs=[IN_VMEM] * len(arrays), out_specs=[IN_VMEM] * n + [ANY] * n_gather,
        out_shape=[_sds(s, dt) for s, dt in zip(shapes, dtypes)]
        + [_sds((N_DEV,) + s, dt) for s, dt in zip(shapes[:n_gather], dtypes)],
        scratch_shapes=[pltpu.SemaphoreType.DMA((n_gather * AG_COPIES,)), pltpu.SemaphoreType.DMA((n_gather * AG_COPIES,)),
                        pltpu.SemaphoreType.DMA((n_gather,))],
        compiler_params=pltpu.CompilerParams(vmem_limit_bytes=VMEM_LIMIT_BYTES),
    )(*arrays)
    return list(res[:n]), list(res[n:])


def _pair_sum(g, r, core, name):
    _, _, rows, cols = g.shape
    tr = _pick(rows, (512, 256, 128))

    def body(core_ref, g_ref, r_ref, p_ref):
        del core_ref
        p_ref[...] = (g_ref[...].astype(F32) + r_ref[...].astype(F32)).astype(p_ref.dtype)

    return pl.pallas_call(
        body, name=name,
        grid_spec=pltpu.PrefetchScalarGridSpec(
            num_scalar_prefetch=1, grid=(4, rows // tr),
            in_specs=[pl.BlockSpec((None, None, tr, cols), lambda q, i, core_ref: (q, core_ref[0], i, 0)),
                      pl.BlockSpec((None, tr, cols), lambda q, i, core_ref: (q, i, 0))],
            out_specs=pl.BlockSpec((None, tr, cols), lambda q, i, core_ref: (q, i, 0))),
        out_shape=_sds((4, rows, cols), g.dtype),
        compiler_params=_params("parallel", "parallel"),
    )(core, g, r)


def _adam(w, g, m, v):
    m2 = ADAM_B1 * m + (1.0 - ADAM_B1) * g
    v2 = ADAM_B2 * v + (1.0 - ADAM_B2) * (g * g)
    m_hat = m2 / (1.0 - ADAM_B1 ** ADAM_STEP)
    v_hat = v2 / (1.0 - ADAM_B2 ** ADAM_STEP)
    return -ADAM_LR * (m_hat / (jnp.sqrt(v_hat) + ADAM_EPS) + ADAM_WD * w), m2, v2


def _shard_update(own, index, received, w, m, v, layer, so_far, name):
    _, rows, cols = w.shape
    n_recv = received.shape[0]
    tr = _pick(rows, (512, 256, 128))

    def body(index_ref, p_ref, q_ref, w_ref, m_ref, v_ref, *rest):
        del index_ref
        g_out, d_out, m_out, v_out = rest[-4:]
        g = p_ref[...].astype(F32)
        for s in range(n_recv):
            g = g + q_ref[s].astype(F32)
        g_out[...] = g
        d_out[...], m_out[...], v_out[...] = _adam(w_ref[...], g, m_ref[...], v_ref[...])

    tile = pl.BlockSpec((None, tr, cols), lambda i, index_ref: (layer, i, 0))
    kept = list(so_far) if so_far is not None else []
    return pl.pallas_call(
        body, name=name,
        grid_spec=pltpu.PrefetchScalarGridSpec(
            num_scalar_prefetch=1, grid=(rows // tr,),
            in_specs=[pl.BlockSpec((None, tr, cols), lambda i, index_ref: (index_ref[0], i, 0)),
                      pl.BlockSpec((n_recv, tr, cols), lambda i, index_ref: (0, i, 0)), tile, tile, tile] + [ANY] * len(kept),
            out_specs=[tile] * 4),
        out_shape=[_sds(w.shape, F32)] * 4,
        input_output_aliases={6 + i: i for i in range(len(kept))},
        compiler_params=_params("parallel"),
    )(index, own, received, w, m, v, *kept)


def _natural_pieces(shape, r, c):
    if tuple(shape[-2:]) == (r, c) and all(n == 1 for n in shape[:-2]):
        return [((0,) * (len(shape) - 2), (0, c))]
    groups, width = shape[1], shape[3]
    assert len(shape) == 4 and shape[0] == 1 and shape[2] == r and groups * width == c, (shape, r, c)
    return [((0, g), (g * width, width)) for g in range(groups)]


def _replicated_update(shares, where, params, name):
    flat = [a for p in params if p is not None for a in p]

    def body(s_ref, *refs):
        ins, outs = refs[:len(flat)], refs[len(flat):]
        total = s_ref[0]
        for dev in range(1, N_DEV):
            total = total + s_ref[dev]
        i_at = o_at = 0
        for ranges, p in zip(where, params):
            chunks = [total[r0:r0 + r, :c] for r0, r, c in ranges]
            g2d = chunks[0] if len(chunks) == 1 else jnp.concatenate(chunks, axis=1)
            if p is None:
                outs[o_at][...] = g2d
                o_at += 1
                continue
            w_ref, m_ref, v_ref = ins[i_at:i_at + 3]
            for idx, (c0, cols) in _natural_pieces(p[0].shape, *g2d.shape):
                at = idx + (slice(None), slice(None))
                g = g2d[:, c0:c0 + cols]
                outs[o_at][at] = g
                outs[o_at + 1][at], outs[o_at + 2][at], outs[o_at + 3][at] = _adam(w_ref[at], g, m_ref[at], v_ref[at])
            i_at, o_at = i_at + 3, o_at + 4

    out_shape = []
    for ranges, p in zip(where, params):
        out_shape += [_sds((ranges[0][1], sum(c for _, _, c in ranges)), F32)] if p is None else [_sds(p[0].shape, F32)] * 4
    res = pl.pallas_call(
        body, name=name, out_shape=out_shape, compiler_params=pltpu.CompilerParams(vmem_limit_bytes=VMEM_LIMIT_BYTES),
    )(shares, *flat)
    out, at = [], 0
    for p in params:
        out.append(list(res[at:at + (1 if p is None else 4)]))
        at += 1 if p is None else 4
    return out


def _rope_tables(t):
    half = HEAD_DIM // 2
    inv_freq = ROPE_THETA ** (-(jnp.arange(half, dtype=F32) * 2.0) / HEAD_DIM)
    ang = jnp.arange(t, dtype=jnp.int32).astype(F32)[:, None] * inv_freq[None, :]
    cos, sin = jnp.cos(ang), jnp.sin(ang)
    return jnp.tile(jnp.concatenate([cos, cos], axis=1), (1, 2)), jnp.tile(jnp.concatenate([-sin, sin], axis=1), (1, 2))


def _dup_heads(w, d):
    kv = (w.shape[-1] - d) // 2
    lead = w.shape[:-1]

    def dup(part):
        heads = part.reshape(lead + (kv // HEAD_DIM, 1, HEAD_DIM))
        return jnp.broadcast_to(heads, lead + (kv // HEAD_DIM, 2, HEAD_DIM)).reshape(lead + (2 * kv,))

    return jnp.concatenate([w[..., :d], dup(w[..., d:d + kv]), dup(w[..., d + kv:])], axis=-1)


def _fold_heads(dw, d):
    kv = (dw.shape[-1] - d) // 4
    lead = dw.shape[:-1]

    def fold(part):
        return part.reshape(lead + (kv // HEAD_DIM, 2, HEAD_DIM)).sum(axis=-2).reshape(lead + (kv,))

    return jnp.concatenate([dw[..., :d], fold(dw[..., d:d + 2 * kv]), fold(dw[..., d + 2 * kv:])], axis=-1)


def _columns_full(gathered):
    _, rows, c = gathered.shape
    return jnp.transpose(gathered, (1, 0, 2)).reshape(rows, N_DEV * c)


def _rows_full(gathered):
    return gathered.reshape(-1, gathered.shape[-1])


def _columns_blocked(full):
    rows, cols = full.shape
    return jnp.transpose(full.reshape(rows, N_DEV, cols // N_DEV), (1, 0, 2)).astype(ACT)


def _rows_blocked(full):
    return full.reshape(N_DEV, full.shape[0] // N_DEV, full.shape[1]).astype(ACT)


def _pack_rows(parts, width):
    rows, where, at = [], [], 0
    for v in parts:
        r, c = v.shape
        n_rows = -(-r // 8) * 8
        chunks = []
        for c0 in range(0, c, width):
            chunk = v[:, c0:c0 + width].astype(F32)
            rows.append(jnp.pad(chunk, ((0, n_rows - r), (0, width - chunk.shape[1]))))
            chunks.append((at, r, chunk.shape[1]))
            at += n_rows
        where.append(chunks)
    return jnp.concatenate(rows, axis=0), where


def _halves(block):
    half = block.shape[0] // 2
    return (0, half), (half, half)


def _whole(block):
    return (0, block.shape[0])


EARLY = ("attn_w_qkv", "sgu_ln", "attn_w_o")
LATE = ("sgu_w_in", "sgu_w_out", "gu0", "gu1", "dn0", "dn1")
COLUMN_SHARDED = ("attn_w_qkv", "sgu_w_in", "gu0", "gu1")
BEFORE_ATTN = ("norm_mix_post", "norm_ffn_pre", "norm_ffn_post", "attn_b_o", "sgu_w_spatial", "sgu_b_spatial")
AFTER_ATTN = ("attn_b_qkv", "attn_sinks")
LAST = ("norm_mix_pre",)
WEIGHTS = ("norm_mix_pre", "norm_mix_post", "norm_ffn_pre", "norm_ffn_post", "attn_w_qkv", "attn_b_qkv", "attn_sinks", "attn_w_o",
           "attn_b_o", "sgu_w_in", "sgu_ln_g", "sgu_ln_b", "sgu_w_spatial", "sgu_b_spatial", "sgu_w_out", "ffn_w_gate_up", "ffn_w_down")


LAYER_OF = {"gu0": ("ffn_w_gate_up", 0), "gu1": ("ffn_w_gate_up", 1), "dn0": ("ffn_w_down", 0), "dn1": ("ffn_w_down", 1)}


def _source(tree, name):
    if name == "sgu_ln":
        return [tree["sgu_ln_g"][None], tree["sgu_ln_b"][None]], 0
    leaf, layer = LAYER_OF.get(name, (name, 0))
    return [tree[leaf]], layer


def kernel(x, norm_mix_pre, norm_mix_post, norm_ffn_pre, norm_ffn_post, attn_w_qkv, attn_b_qkv, attn_sinks, attn_w_o, attn_b_o, sgu_w_in, sgu_ln_g, sgu_ln_b, sgu_w_spatial, sgu_b_spatial, sgu_w_out, ffn_w_gate_up, ffn_w_down, loss_target, m_norm_mix_pre, m_norm_mix_post, m_norm_ffn_pre, m_norm_ffn_post, m_attn_w_qkv, m_attn_b_qkv, m_attn_sinks, m_attn_w_o, m_attn_b_o, m_sgu_w_in, m_sgu_ln_g, m_sgu_ln_b, m_sgu_w_spatial, m_sgu_b_spatial, m_sgu_w_out, m_ffn_w_gate_up, m_ffn_w_down, v_norm_mix_pre, v_norm_mix_post, v_norm_ffn_pre, v_norm_ffn_post, v_attn_w_qkv, v_attn_b_qkv, v_attn_sinks, v_attn_w_o, v_attn_b_o, v_sgu_w_in, v_sgu_ln_g, v_sgu_ln_b, v_sgu_w_spatial, v_sgu_b_spatial, v_sgu_w_out, v_ffn_w_gate_up, v_ffn_w_down):
    w = dict(norm_mix_pre=norm_mix_pre, norm_mix_post=norm_mix_post, norm_ffn_pre=norm_ffn_pre, norm_ffn_post=norm_ffn_post,
             attn_w_qkv=attn_w_qkv, attn_b_qkv=attn_b_qkv, attn_sinks=attn_sinks, attn_w_o=attn_w_o, attn_b_o=attn_b_o,
             sgu_w_in=sgu_w_in, sgu_ln_g=sgu_ln_g, sgu_ln_b=sgu_ln_b, sgu_w_spatial=sgu_w_spatial, sgu_b_spatial=sgu_b_spatial,
             sgu_w_out=sgu_w_out, ffn_w_gate_up=ffn_w_gate_up, ffn_w_down=ffn_w_down)
    mom = dict(norm_mix_pre=m_norm_mix_pre, norm_mix_post=m_norm_mix_post, norm_ffn_pre=m_norm_ffn_pre, norm_ffn_post=m_norm_ffn_post,
               attn_w_qkv=m_attn_w_qkv, attn_b_qkv=m_attn_b_qkv, attn_sinks=m_attn_sinks, attn_w_o=m_attn_w_o, attn_b_o=m_attn_b_o,
               sgu_w_in=m_sgu_w_in, sgu_ln_g=m_sgu_ln_g, sgu_ln_b=m_sgu_ln_b, sgu_w_spatial=m_sgu_w_spatial,
               sgu_b_spatial=m_sgu_b_spatial, sgu_w_out=m_sgu_w_out, ffn_w_gate_up=m_ffn_w_gate_up, ffn_w_down=m_ffn_w_down)
    var = dict(norm_mix_pre=v_norm_mix_pre, norm_mix_post=v_norm_mix_post, norm_ffn_pre=v_norm_ffn_pre, norm_ffn_post=v_norm_ffn_post,
               attn_w_qkv=v_attn_w_qkv, attn_b_qkv=v_attn_b_qkv, attn_sinks=v_attn_sinks, attn_w_o=v_attn_w_o, attn_b_o=v_attn_b_o,
               sgu_w_in=v_sgu_w_in, sgu_ln_g=v_sgu_ln_g, sgu_ln_b=v_sgu_ln_b, sgu_w_spatial=v_sgu_w_spatial,
               sgu_b_spatial=v_sgu_b_spatial, sgu_w_out=v_sgu_w_out, ffn_w_gate_up=v_ffn_w_gate_up, ffn_w_down=v_ffn_w_down)
    w, mom, var = [{**tree, "ffn_w_gate_up": jnp.swapaxes(tree["ffn_w_gate_up"], 1, 2)} for tree in (w, mom, var)]
    xs, target = x[0], loss_target[0]
    t, d = xs.shape
    row = lambda v: v.reshape(1, -1).astype(F32)
    core = lax.axis_index("c").astype(jnp.int32).reshape(1)
    chip = (2 * lax.axis_index("x") + lax.axis_index("y")).astype(jnp.int32).reshape(1)
    names = EARLY + LATE
    staged, early = _prepare([_source(w, n) for n in names], [F32 if n == "sgu_ln" else ACT for n in names], len(EARLY),
                             "weights_prepare")
    stage = dict(zip(names, staged))
    w_qkv = _dup_heads(_columns_full(early[0]), d)
    lg, lb = row(early[1][:, 0, :]), row(early[1][:, 1, :])
    w_o = _rows_full(early[2])
    b_qkv = _dup_heads(row(attn_b_qkv), d)
    sinks = attn_sinks.reshape(1, -1).astype(F32)
    ws = sgu_w_spatial[0].astype(F32)
    bs_t = sgu_b_spatial[0].astype(F32).T
    cos, sin = _rope_tables(t)
    gather = lambda name, so_far, **pieces: _gather_job(stage[name], so_far, **pieces)
    a_half = lambda name: _halves(stage[name])[0]
    b_half = lambda name: _halves(stage[name])[1]
    whole = lambda name: _whole(stage[name])

    def pieces(name, n):
        rows = stage[name].shape[0] // n
        return [(i * rows, rows) for i in range(n)]

    ways = lambda parts: [(piece, i % 2) for i, piece in enumerate(parts)]
    relay = lambda name, so_far, **steps: _relay_gather_job(stage[name], so_far, **steps)
    gu0_parts, gu1_parts = pieces("gu0", 4), pieces("gu1", 4)
    dn0_parts, dn1_parts, in_parts, out_parts = pieces("dn0", 2), pieces("dn1", 2), pieces("sgu_w_in", 2), pieces("sgu_w_out", 2)
    (h0, q, kdup, vdup), ((g_gu0,),) = _attn_qkv_fwd(
        xs, row(norm_mix_pre[0]), w_qkv, b_qkv, cos, sin, jobs=[relay("gu0", None, sends=gu0_parts)])
    (o,), ((g_gu0,), (g_dn0,)) = _attn_fwd(q, kdup, vdup, sinks, jobs=[
        relay("gu0", g_gu0, relays=ways(gu0_parts), forwards=gu0_parts), relay("dn0", None, sends=dn0_parts)])
    (m0, x1), ((g_gu0,), (g_dn0,), (g_in,)) = _proj_res(o, w_o, row(attn_b_o), row(norm_mix_post[0]), xs, "attn_out_fwd", jobs=[
        relay("gu0", g_gu0, far=gu0_parts), relay("dn0", g_dn0, relays=ways(dn0_parts), forwards=dn0_parts),
        relay("sgu_w_in", None, sends=in_parts)])
    w_gu0 = _rows_full(g_gu0)
    (h1, gu0, a0), ((g_dn0,), (g_in,), (g_out,), (g_gu1,)) = _ffn_up_fwd(x1, row(norm_ffn_pre[0]), w_gu0, jobs=[
        relay("dn0", g_dn0, far=dn0_parts), relay("sgu_w_in", g_in, relays=ways(in_parts), forwards=in_parts),
        relay("sgu_w_out", None, sends=out_parts), relay("gu1", None, sends=gu1_parts[:2])])
    w_dn0 = _rows_full(g_dn0)
    (f0, x2), ((g_in,), (g_out,), (g_gu1,)) = _proj_res(a0, w_dn0, None, row(norm_ffn_post[0]), x1, "ffn_down_fwd0", jobs=[
        relay("sgu_w_in", g_in, far=in_parts), relay("sgu_w_out", g_out, relays=ways(out_parts), forwards=out_parts),
        relay("gu1", g_gu1, sends=gu1_parts[2:], relays=ways(gu1_parts[:2]), forwards=gu1_parts[:2])])
    w_in = g_in
    (h2, z, y), ((g_out,), (g_gu1,), (g_dn1,)) = _sgu_fwd(x2, row(norm_mix_pre[1]), w_in, lg, lb, ws, bs_t, jobs=[
        relay("sgu_w_out", g_out, far=out_parts),
        relay("gu1", g_gu1, relays=ways(gu1_parts[2:]), forwards=gu1_parts[2:], far=gu1_parts[:2]),
        relay("dn1", None, sends=dn1_parts)])
    w_out = _rows_full(g_out)
    (m1, x3), ((g_gu1,), (g_dn1,)) = _proj_res(y, w_out, None, row(norm_mix_post[1]), x2, "sgu_out_fwd", jobs=[
        relay("gu1", g_gu1, far=gu1_parts[2:]), relay("dn1", g_dn1, relays=ways(dn1_parts), forwards=dn1_parts)])
    w_gu1 = _rows_full(g_gu1)
    (h3, gu1, a1), ((g_dn1,),) = _ffn_up_fwd(x3, row(norm_ffn_pre[1]), w_gu1, jobs=[relay("dn1", g_dn1, far=dn1_parts)])
    w_dn1 = _rows_full(g_dn1)
    (f1, dx4, loss_tile), _ = _proj_res(a1, w_dn1, None, row(norm_ffn_post[1]), x3, "ffn_down_fwd1", target=target)

    to_sibling = lambda g: _scatter_job([g], 4, _to_sibling)
    pair = lambda g, r, name: _pair_sum(g.reshape((4, 2) + g.shape[1:]), r, core, "pair_sum_" + name)
    to_chips = lambda p, prev=None, piece=None: _scatter_job([p], 3, _to_owner_chip, prev=prev, piece=piece)
    out_g, out_d, out_m, out_v = {}, {}, {}, {}

    trees = [{**tree, "sgu_ln": jnp.stack([tree["sgu_ln_g"], tree["sgu_ln_b"]], axis=1)} for tree in (w, mom, var)]
    so_far = {}

    def update(name, own, index, received):
        leaf, layer = LAYER_OF.get(name, (name, 0))
        so_far[leaf] = _shard_update(own, index, received, *[tree[leaf] for tree in trees], layer, so_far.get(leaf), "update_" + name)
        for out, val in zip((out_g, out_d, out_m, out_v), so_far[leaf]):
            out[leaf] = val

    def finish(names, n_extra, shares, where, name):
        res = _replicated_update(shares, where, [(w[n], mom[n], var[n]) for n in names] + [None] * n_extra, name)
        for n, vals in zip(names, res):
            out_g[n], out_d[n], out_m[n], out_v[n] = vals
        return [vals[0] for vals in res[len(names):]]

    first_half = lambda p: _halves(p[0])[0]
    second_half = lambda p: _halves(p[0])[1]
    (df1, dgu1, dx3, dg_ffn_post1, dg_ffn_pre1), _ = _ffn_bwd(
        dx4, f1, row(norm_ffn_post[1]), w_dn1, gu1, w_gu1, x3, row(norm_ffn_pre[1]), "ffn_bwd1")
    b_gu1, _ = _wgrad(h3, dgu1, "ffn_up_wgrad1", ACT, transposed=True)
    b_gu1 = _rows_blocked(b_gu1)
    dw_dn1, _ = _wgrad(a1, df1, "ffn_down_wgrad1", ACT)
    b_dn1 = _rows_blocked(dw_dn1)
    (dm1, dy, dg_mix_post1), ((r_gu1,), (r_dn1,)) = _post_bwd(
        dx3, m1, row(norm_mix_post[1]), w_out, "sgu", "sgu_out_bwd", jobs=[to_sibling(b_gu1), to_sibling(b_dn1)])
    p_gu1, p_dn1 = pair(b_gu1, r_gu1, "gu1"), pair(b_dn1, r_dn1, "dn1")
    dw_out, _ = _wgrad(y, dm1, "sgu_out_wgrad", ACT)
    b_out = _rows_blocked(dw_out)
    (dz, dlg, dlb, dws, dbs_t), ((q_gu1,), (r_out,)) = _sgu_mix_bwd(z, dy, lg, lb, ws, bs_t, jobs=[
        to_chips(p_gu1, piece=first_half(p_gu1)), to_sibling(b_out)])
    p_out = pair(b_out, r_out, "sgu_w_out")
    b_in, _ = _wgrad(h2, dz, "sgu_in_wgrad", ACT, out_block=stage["sgu_w_in"].shape[1])
    b_ln = jnp.stack([dlg.reshape(N_DEV, -1), dlb.reshape(N_DEV, -1)], axis=1)
    (dx2, dg_mix_pre1), _ = _pre_bwd(dz, w_in, x2, row(norm_mix_pre[1]), dx3, "sgu_in_bwd")
    (df0, dgu0, dx1, dg_ffn_post0, dg_ffn_pre0), ((q_gu1,), (q_dn1,), (q_out,), (r_in,), (r_ln,)) = _ffn_bwd(
        dx2, f0, row(norm_ffn_post[0]), w_dn0, gu0, w_gu0, x1, row(norm_ffn_pre[0]), "ffn_bwd0",
        jobs=[to_chips(p_gu1, prev=[q_gu1], piece=second_half(p_gu1)), to_chips(p_dn1), to_chips(p_out), to_sibling(b_in),
              to_sibling(b_ln)])
    update("gu1", p_gu1, chip, q_gu1)
    update("dn1", p_dn1, chip, q_dn1)
    update("sgu_w_out", p_out, chip, q_out)
    p_in, p_ln = pair(b_in, r_in, "sgu_w_in"), pair(b_ln, r_ln, "sgu_ln")
    b_gu0, ((q_in,), (q_ln,)) = _wgrad(h1, dgu0, "ffn_up_wgrad0", ACT, transposed=True, jobs=[to_chips(p_in), to_chips(p_ln)])
    update("sgu_w_in", p_in, chip, q_in)
    update("sgu_ln", p_ln, chip, q_ln)
    b_gu0 = _rows_blocked(b_gu0)
    dw_dn0, _ = _wgrad(a0, df0, "ffn_down_wgrad0", ACT)
    b_dn0 = _rows_blocked(dw_dn0)
    (dm0, do, dg_mix_post0, db_o), ((r_gu0,), (r_dn0,)) = _post_bwd(
        dx1, m0, row(norm_mix_post[0]), w_o, "attn", "attn_out_bwd", jobs=[to_sibling(b_gu0), to_sibling(b_dn0)])
    p_gu0, p_dn0 = pair(b_gu0, r_gu0, "gu0"), pair(b_dn0, r_dn0, "dn0")
    dw_o, _ = _wgrad(o, dm0, "attn_out_wgrad", ACT)
    b_o = _rows_blocked(dw_o)
    grads = {
        "norm_mix_post": jnp.concatenate([dg_mix_post0, dg_mix_post1], axis=0),
        "norm_ffn_pre": jnp.concatenate([dg_ffn_pre0, dg_ffn_pre1], axis=0),
        "norm_ffn_post": jnp.concatenate([dg_ffn_post0, dg_ffn_post1], axis=0),
        "attn_b_o": db_o,
        "sgu_w_spatial": dws,
        "sgu_b_spatial": dbs_t.T,
    }
    shares1, where1 = _pack_rows([grads[name] for name in BEFORE_ATTN], d)
    (dqkv, db_qkv, dsink), ((q_gu0,), (q_dn0,), (r_o,), (g_shares1,)) = _attn_bwd(q, kdup, vdup, do, sinks, cos, sin, jobs=[
        to_chips(p_gu0), to_chips(p_dn0), to_sibling(b_o),
        _gather_job(shares1, None, sends=[_whole(shares1)])])
    update("gu0", p_gu0, chip, q_gu0)
    update("dn0", p_dn0, chip, q_dn0)
    p_o = pair(b_o, r_o, "attn_w_o")
    grads.update({"attn_b_qkv": _fold_heads(db_qkv, d), "attn_sinks": dsink[:1, :d // HEAD_DIM]})
    shares2, where2 = _pack_rows([grads[name] for name in AFTER_ATTN] + [loss_tile[:1, :1]], d)
    (dx0, dg_mix_pre0), _ = _pre_bwd(dqkv, w_qkv, xs, row(norm_mix_pre[0]), dx1, "attn_qkv_bwd")
    grads["norm_mix_pre"] = jnp.concatenate([dg_mix_pre0, dg_mix_pre1], axis=0)
    shares3, where3 = _pack_rows([grads[name] for name in LAST], d)
    dw_qkv, ((q_o,), (g_shares1,), (g_shares2,), (g_shares3,)) = _wgrad(h0, dqkv, "attn_qkv_wgrad", jobs=[
        to_chips(p_o), _gather_job(shares1, g_shares1, forwards=[_whole(shares1)]),
        _gather_job(shares2, None, sends=[_whole(shares2)]), _gather_job(shares3, None, sends=[_whole(shares3)])])
    update("attn_w_o", p_o, chip, q_o)
    b_qkv_grad = _columns_blocked(_fold_heads(dw_qkv, d))
    (r_qkv,), (g_shares2,), (g_shares3,) = _copies_only([
        to_sibling(b_qkv_grad), _gather_job(shares2, g_shares2, forwards=[_whole(shares2)]),
        _gather_job(shares3, g_shares3, forwards=[_whole(shares3)])], "grads_tail_to_sibling")
    p_qkv = pair(b_qkv_grad, r_qkv, "attn_w_qkv")
    (q_qkv,), = _copies_only([to_chips(p_qkv)], "grads_tail_to_owner_chip")
    update("attn_w_qkv", p_qkv, chip, q_qkv)

    finish(BEFORE_ATTN, 0, g_shares1, where1, "replicated_update_before_attn")
    loss = finish(AFTER_ATTN, 1, g_shares2, where2, "replicated_update_after_attn")[0][0, 0]
    finish(LAST, 0, g_shares3, where3, "replicated_update_last")

    for out in (out_g, out_d, out_m, out_v):
        out["sgu_ln_g"], out["sgu_ln_b"] = out["sgu_ln"][:, 0, :], out["sgu_ln"][:, 1, :]
        out["ffn_w_gate_up"] = jnp.swapaxes(out["ffn_w_gate_up"], 1, 2)

    return (loss, dx0[None], *[out_g[n] for n in WEIGHTS], *[out_d[n] for n in WEIGHTS],
            *[out_m[n] for n in WEIGHTS], *[out_v[n] for n in WEIGHTS])
```

```python
import functools
import math
import operator

import jax
import jax.numpy as jnp
from jax import lax
from jax.experimental import pallas as pl
from jax.experimental.pallas import tpu as pltpu

F32 = jnp.float32
ACT = jnp.bfloat16

EPS = 1e-6
HEAD_DIM = 64
PAIR = 2 * HEAD_DIM
GQA_GROUP = 4
WINDOW = 128
ROPE_THETA = 10000.0
SCALE = HEAD_DIM ** -0.5
MASKED = -1e30
LANES = 128
MXU_WIDTH = 256

ADAM_LR, ADAM_B1, ADAM_B2, ADAM_EPS, ADAM_WD, ADAM_STEP = 0.001, 0.9, 0.999, 1e-08, 0.01, 10

N_DEV = 8
VMEM_LIMIT_BYTES = 56 * 1024 * 1024
MESH = pl.DeviceIdType.MESH
ANY = pl.BlockSpec(memory_space=pl.ANY)
IN_VMEM = pl.BlockSpec(memory_space=pltpu.VMEM)


def _pick(n, candidates):
    for c in candidates:
        if n % c == 0:
            return c
    return n


def _row_tile(t, most=512):
    return _pick(t, (most,))


def _mxu_chunks(n, most=4 * MXU_WIDTH):
    assert n % MXU_WIDTH == 0 and most % MXU_WIDTH == 0
    return [(c, min(most, n - c)) for c in range(0, n, most)]


def _params(*sem):
    return pltpu.CompilerParams(dimension_semantics=sem, vmem_limit_bytes=VMEM_LIMIT_BYTES)


def _full(shape):
    return pl.BlockSpec(shape, lambda *_: (0,) * len(shape))


def _resident(shape):
    return pl.BlockSpec(shape, lambda *_: (0,) * len(shape), pipeline_mode=pl.Buffered(1))


def _rows(tm, n):
    return pl.BlockSpec((tm, n), lambda i: (i, 0))


def _sds(shape, dtype):
    return jax.ShapeDtypeStruct(shape, dtype)


def _place():
    return lax.axis_index("x"), lax.axis_index("y"), lax.axis_index("c")


def _other_chips(x, y):
    return [(1 - x, y), (x, 1 - y), (1 - x, 1 - y)]


class _Job:
    def __init__(self, inputs, out_shape, aliases, emit, n_remote, n_local=0):
        self.inputs, self.out_shape, self.aliases, self.emit = list(inputs), list(out_shape), dict(aliases), emit
        self.n_remote, self.n_local = n_remote, n_local


def _call(body, name, grid, in_specs, out_specs, out_shape, args, sem, scratch=(), jobs=()):
    n_in, n_out, n_scr = len(args), len(out_shape), len(scratch)
    j_in = [a for job in jobs for a in job.inputs]
    j_out = [s for job in jobs for s in job.out_shape]
    aliases, at_in, at_out = {}, n_in, n_out
    for job in jobs:
        aliases.update({at_in + i: at_out + o for i, o in job.aliases.items()})
        at_in, at_out = at_in + len(job.inputs), at_out + len(job.out_shape)
    n_remote = sum(job.n_remote for job in jobs)
    n_local = sum(job.n_local for job in jobs)

    def wrapped(*refs):
        ins, jin = refs[:n_in], refs[n_in:n_in + len(j_in)]
        rest = refs[n_in + len(j_in):]
        outs, jout = rest[:n_out], rest[n_out:n_out + len(j_out)]
        scr = rest[n_out + len(j_out):n_out + len(j_out) + n_scr]
        if not jobs:
            body(*ins, *outs, *scr)
            return
        send, recv, local = rest[n_out + len(j_out) + n_scr:]
        ids = [pl.program_id(a) for a in range(len(grid))]
        first = functools.reduce(operator.and_, [i == 0 for i in ids])
        last = functools.reduce(operator.and_, [i == g - 1 for i, g in zip(ids, grid)])

        def descriptors():
            place, found, i, o, r, l = _place(), [], 0, 0, 0, 0
            for job in jobs:
                for src, dst, to in job.emit(jin[i:i + len(job.inputs)], jout[o:o + len(job.out_shape)], place):
                    if to is None:
                        found.append(pltpu.make_async_copy(src, dst, local.at[l]))
                        l += 1
                    else:
                        found.append(pltpu.make_async_remote_copy(src_ref=src, dst_ref=dst, send_sem=send.at[r], recv_sem=recv.at[r],
                                                                  device_id=to, device_id_type=MESH))
                        r += 1
                i, o = i + len(job.inputs), o + len(job.out_shape)
            return found

        @pl.when(first)
        def _():
            for cp in descriptors():
                cp.start()

        body(*ins, *outs, *scr)

        @pl.when(last)
        def _():
            for cp in descriptors():
                cp.wait()

    sems = [pltpu.SemaphoreType.DMA((n_remote,)), pltpu.SemaphoreType.DMA((n_remote,)),
            pltpu.SemaphoreType.DMA((max(n_local, 1),))] if jobs else []
    res = pl.pallas_call(
        wrapped, name=name, grid=grid,
        in_specs=list(in_specs) + [ANY] * len(j_in), out_specs=list(out_specs) + [ANY] * len(j_out),
        out_shape=list(out_shape) + j_out, scratch_shapes=list(scratch) + sems, input_output_aliases=aliases,
        compiler_params=_params(*(("arbitrary",) * len(grid) if jobs else sem)),
    )(*args, *j_in)
    job_res, at = [], n_out
    for job in jobs:
        job_res.append(list(res[at:at + len(job.out_shape)]))
        at += len(job.out_shape)
    return list(res[:n_out]), job_res


def _copies_only(jobs, name):
    def body(o_ref):
        o_ref[...] = jnp.zeros_like(o_ref)

    return _call(body, name, (1,), [], [_full((8, LANES))], [_sds((8, LANES), F32)], (), ("arbitrary",), jobs=jobs)[1]


def _gather_job(stage, gathered, sends=(), forwards=()):
    shape = _sds((N_DEV,) + stage.shape, stage.dtype)
    inputs = ([stage] if sends else []) + ([gathered] if gathered is not None else [])
    aliases = {len(inputs) - 1: 0} if gathered is not None else {}

    def emit(ins, outs, place):
        x, y, c = place
        sibling, chips, mine, g = (x, y, 1 - c), _other_chips(x, y), 4 * x + 2 * y + c, outs[0]
        found = []
        for r0, nr in sends:
            src, dst = ins[0].at[pl.ds(r0, nr)], g.at[mine, pl.ds(r0, nr)]
            found += [(src, dst, None), (src, dst, sibling)] + [(src, dst, (px, py, c)) for px, py in chips]
        for r0, nr in forwards:
            for px, py in chips:
                block = 4 * px + 2 * py + c
                found.append((ins[-1].at[block, pl.ds(r0, nr)], g.at[block, pl.ds(r0, nr)], sibling))
        return found

    return _Job(inputs, [shape], aliases, emit, 4 * len(sends) + 3 * len(forwards), len(sends))


def _relay_gather_job(stage, gathered, sends=(), relays=(), forwards=(), far=()):
    shape = _sds((N_DEV,) + stage.shape, stage.dtype)
    inputs = ([stage] if sends else []) + ([gathered] if gathered is not None else [])
    aliases = {len(inputs) - 1: 0} if gathered is not None else {}

    def emit(ins, outs, place):
        x, y, c = place
        sibling, beside_x, beside_y, g = (x, y, 1 - c), (1 - x, y, c), (x, 1 - y, c), outs[0]
        slot = lambda dev: 4 * dev[0] + 2 * dev[1] + dev[2]
        found = []
        for r0, nr in sends:
            src, dst = ins[0].at[pl.ds(r0, nr)], g.at[slot((x, y, c)), pl.ds(r0, nr)]
            found += [(src, dst, None), (src, dst, sibling), (src, dst, beside_x), (src, dst, beside_y)]

        def passed_on(block, piece, to):
            return ins[-1].at[block, pl.ds(*piece)], g.at[block, pl.ds(*piece)], to

        for piece, way in relays:
            found.append(passed_on(slot(beside_x), piece, beside_y) if way == 0 else passed_on(slot(beside_y), piece, beside_x))
        for piece in forwards:
            found += [passed_on(slot(beside_x), piece, sibling), passed_on(slot(beside_y), piece, sibling)]
        for piece in far:
            found.append(passed_on(slot((1 - x, 1 - y, c)), piece, sibling))
        return found

    return _Job(inputs, [shape], aliases, emit, 3 * len(sends) + len(relays) + 2 * len(forwards) + len(far), len(sends))


def _scatter_job(arrays, n_slots, route, prev=None, piece=None):
    shapes = [_sds((n_slots,) + v.shape[1:], v.dtype) for v in arrays]
    inputs = list(arrays) + (list(prev) if prev else [])
    aliases = {len(arrays) + i: i for i in range(len(arrays))} if prev else {}

    def emit(ins, outs, place):
        found = []
        for a in range(len(arrays)):
            for s in range(n_slots):
                block, to = route(s, *place)
                if piece is None:
                    found.append((ins[a].at[block], outs[a].at[s], to))
                else:
                    found.append((ins[a].at[block, pl.ds(*piece)], outs[a].at[s, pl.ds(*piece)], to))
        return found

    return _Job(inputs, shapes, aliases, emit, len(arrays) * n_slots)


def _to_sibling(s, x, y, c):
    return 2 * s + (1 - c), (x, y, 1 - c)


def _to_owner_chip(s, x, y, c):
    px, py = _other_chips(x, y)[s]
    return 2 * px + py, (px, py, c)


def _rstd(x):
    return lax.rsqrt(jnp.mean(x * x, axis=-1, keepdims=True) + EPS)


def _rms_bwd(xhat, r, g, dy):
    dxh = dy * g
    return r * (dxh - xhat * jnp.mean(dxh * xhat, axis=-1, keepdims=True))


def _first_half(rows):
    lane = lax.broadcasted_iota(jnp.int32, (rows, PAIR), 1)
    return (lane % HEAD_DIM) < (HEAD_DIM // 2)


def _rot_half(v, first):
    return jnp.where(first, pltpu.roll(v, PAIR - HEAD_DIM // 2, 1), pltpu.roll(v, HEAD_DIM // 2, 1))


def _rope(v, cos, sin, first):
    return v * cos + _rot_half(v, first) * sin


def _rope_t(dv, cos, sin, first):
    return dv * cos + _rot_half(dv * sin, first)


def _dot(a, b):
    return jnp.dot(a, b, preferred_element_type=F32)


def _dot_nt(a, b):
    return lax.dot_general(a, b, (((1,), (1,)), ((), ())), preferred_element_type=F32)


def _dot_tn(a, b):
    return lax.dot_general(a, b, (((0,), (0,)), ((), ())), preferred_element_type=F32)


def _sigmoid(v):
    return 1.0 / (1.0 + jnp.exp(-v))


_GELU_K = math.sqrt(2.0 / math.pi)
_GELU_C = 0.044715


def _gelu(v):
    return v * (0.5 * (1.0 + jnp.tanh(_GELU_K * (v + _GELU_C * (v * v * v)))))


def _gelu_grad(v):
    t = jnp.tanh(_GELU_K * (v + _GELU_C * (v * v * v)))
    return 0.5 * (1.0 + t) + 0.5 * v * (1.0 - t * t) * (_GELU_K * (1.0 + 3.0 * _GELU_C * (v * v)))


def _attn_qkv_fwd(x, g, w, b, cos, sin, jobs=()):
    t, d = x.shape
    n = w.shape[1]
    kd = (n - d) // 2
    tm = _row_tile(t)
    ch = _pick(d, (512,))

    def body(x_ref, g_ref, w_ref, b_ref, cos_ref, sin_ref, h_ref, q_ref, k_ref, v_ref):
        xv = x_ref[...]
        hb = (xv * _rstd(xv) * g_ref[...]).astype(ACT)
        h_ref[...] = hb
        cosv, sinv = cos_ref[...], sin_ref[...]
        first = _first_half(tm)

        def proj(lo, width):
            return _dot(hb, w_ref[:, lo:lo + width]) + b_ref[:, lo:lo + width]

        for c in range(0, d, ch):
            y = proj(c, ch)
            for s in range(0, ch, PAIR):
                q_ref[:, c + s:c + s + PAIR] = (_rope(y[:, s:s + PAIR], cosv, sinv, first) * SCALE).astype(ACT)
        y = proj(d, kd)
        for s in range(0, kd, PAIR):
            k_ref[:, s:s + PAIR] = _rope(y[:, s:s + PAIR], cosv, sinv, first).astype(ACT)
        v_ref[...] = proj(d + kd, kd).astype(ACT)

    return _call(
        body, "attn_qkv_fwd", (t // tm,),
        [_rows(tm, d), _full((1, d)), _full((d, n)), _full((1, n)), _rows(tm, PAIR), _rows(tm, PAIR)],
        [_rows(tm, d), _rows(tm, d), _rows(tm, kd), _rows(tm, kd)],
        [_sds((t, d), ACT), _sds((t, d), ACT), _sds((t, kd), ACT), _sds((t, kd), ACT)],
        (x, g, w, b, cos, sin), ("parallel",), jobs=jobs)


STACK = GQA_GROUP * WINDOW


def _band_mask(has_prev):
    row = lax.broadcasted_iota(jnp.int32, (STACK, 2 * WINDOW), 0) % WINDOW
    col = lax.broadcasted_iota(jnp.int32, (STACK, 2 * WINDOW), 1)
    return ((col < WINDOW) & (col > row) & has_prev) | ((col >= WINDOW) & (col - WINDOW <= row))


def _lane_halves():
    lane = lax.broadcasted_iota(jnp.int32, (1, PAIR), 1)
    return lane < HEAD_DIM, lane >= HEAD_DIM


def _stack_heads(ref, h, halves):
    parts = []
    for p in range(2):
        pair = ref[:, (2 * h + p) * PAIR:(2 * h + p + 1) * PAIR]
        parts += [jnp.where(half, pair, jnp.zeros_like(pair)) for half in halves]
    return jnp.concatenate(parts, axis=0)


def _sink_column(sink_ref, h):
    row = lax.broadcasted_iota(jnp.int32, (STACK, 1), 0)
    col = jnp.full((STACK, 1), sink_ref[0, GQA_GROUP * h + GQA_GROUP - 1], F32)
    for j in range(GQA_GROUP - 2, -1, -1):
        col = jnp.where(row < (j + 1) * WINDOW, sink_ref[0, GQA_GROUP * h + j], col)
    return col


def _probs(scores, valid, sink):
    s = jnp.where(valid, scores, MASKED)
    m = jnp.maximum(jnp.max(s, axis=-1, keepdims=True), sink)
    p = jnp.exp(s - m)
    psink = jnp.exp(sink - m)
    inv = 1.0 / (jnp.sum(p, axis=-1, keepdims=True) + psink)
    return p * inv, psink * inv


def _attn_fwd(q, kd_, vd, sinks, jobs=()):
    t, d = q.shape
    kd = kd_.shape[1]
    cur = lambda n: (n, 0)
    prev = lambda n: (jnp.maximum(n - 1, 0), 0)

    def body(sink_ref, q_ref, kc_ref, kp_ref, vc_ref, vp_ref, o_ref):
        valid = _band_mask(pl.program_id(0) > 0)
        halves = _lane_halves()
        heads = range(kd // PAIR)
        hs = [slice(h * PAIR, (h + 1) * PAIR) for h in heads]
        scores = [_dot_nt(_stack_heads(q_ref, h, halves), jnp.concatenate([kp_ref[:, hs[h]], kc_ref[:, hs[h]]], axis=0)) for h in heads]
        probs = [_probs(scores[h], valid, _sink_column(sink_ref, h))[0].astype(ACT) for h in heads]
        for h in heads:
            v2 = jnp.concatenate([vp_ref[:, hs[h]], vc_ref[:, hs[h]]], axis=0)
            vs = jnp.concatenate([jnp.where(half, v2, jnp.zeros_like(v2)) for half in halves], axis=0)
            pr = probs[h]
            for p in range(2):
                both = jnp.concatenate([pr[2 * p * WINDOW:(2 * p + 1) * WINDOW], pr[(2 * p + 1) * WINDOW:(2 * p + 2) * WINDOW]], axis=1)
                o_ref[:, (2 * h + p) * PAIR:(2 * h + p + 1) * PAIR] = _dot(both, vs).astype(ACT)

    kv_c, kv_p = pl.BlockSpec((WINDOW, kd), cur), pl.BlockSpec((WINDOW, kd), prev)
    return _call(
        body, "attn_fwd", (t // WINDOW,),
        [pl.BlockSpec(memory_space=pltpu.SMEM), pl.BlockSpec((WINDOW, d), cur), kv_c, kv_p, kv_c, kv_p],
        [pl.BlockSpec((WINDOW, d), cur)], [_sds((t, d), ACT)],
        (sinks, q, kd_, kd_, vd, vd), ("parallel",), jobs=jobs)


def _attn_bwd(q, kd_, vd, do, sinks, cos, sin, jobs=()):
    t, d = q.shape
    kd = kd_.shape[1]
    nb = t // WINDOW
    n_out = d + 2 * kd
    cur = lambda n: (jnp.minimum(n, nb - 1), 0)
    prev = lambda n: (jnp.maximum(jnp.minimum(n, nb - 1) - 1, 0), 0)
    late = lambda n: (jnp.maximum(n - 1, 0), 0)

    def body(sink_ref, q_ref, kc_ref, kp_ref, vc_ref, vp_ref, do_ref, cosc_ref, sinc_ref, cosp_ref, sinp_ref,
             out_ref, db_ref, dsink_ref, dq_keep, dk_keep, dv_keep):
        n = pl.program_id(0)

        @pl.when(n == 0)
        def _():
            for ref in (db_ref, dsink_ref, dq_keep, dk_keep, dv_keep):
                ref[...] = jnp.zeros_like(ref)

        valid = _band_mask(n > 0) & (n < nb)
        halves = _lane_halves()
        first = _first_half(WINDOW)
        slot = lax.broadcasted_iota(jnp.int32, dsink_ref.shape, 1)
        top = lax.broadcasted_iota(jnp.int32, dsink_ref.shape, 0) == 0
        dsink = jnp.zeros(dsink_ref.shape, F32)

        def emit(cols, done):
            out_ref[:, cols] = done.astype(ACT)
            db_ref[:, cols] += jnp.sum(done.astype(F32), axis=0, keepdims=True)

        heads = range(kd // PAIR)
        hs = [slice(h * PAIR, (h + 1) * PAIR) for h in heads]
        k2 = [jnp.concatenate([kp_ref[:, hs[h]], kc_ref[:, hs[h]]], axis=0) for h in heads]
        v2 = [jnp.concatenate([vp_ref[:, hs[h]], vc_ref[:, hs[h]]], axis=0) for h in heads]
        qs = [_stack_heads(q_ref, h, halves) for h in heads]
        dos = [_stack_heads(do_ref, h, halves) for h in heads]
        scores = [_dot_nt(qs[h], k2[h]) for h in heads]
        dps = [_dot_nt(dos[h], v2[h]) for h in heads]
        prs, dss = [], []
        for h in heads:
            pr, psink = _probs(scores[h], valid, _sink_column(sink_ref, h))
            delta = jnp.sum(pr * dps[h], axis=-1, keepdims=True)
            dss.append((pr * (dps[h] - delta)).astype(ACT))
            prs.append(pr.astype(ACT))
            leak = psink * delta
            for j in range(GQA_GROUP):
                share = jnp.sum(leak[j * WINDOW:(j + 1) * WINDOW], axis=0, keepdims=True)
                dsink = dsink - jnp.where(top & (slot == GQA_GROUP * h + j), share, 0.0)
        dqs = [_dot(dss[h], k2[h]) for h in heads]
        dk2 = [_dot_tn(dss[h], qs[h]) for h in heads]
        dv2 = [_dot_tn(prs[h], dos[h]) for h in heads]
        for h in heads:
            for p in range(2):
                ps = slice((2 * h + p) * PAIR, (2 * h + p + 1) * PAIR)
                dqp = jnp.where(halves[0], dqs[h][2 * p * WINDOW:(2 * p + 1) * WINDOW], dqs[h][(2 * p + 1) * WINDOW:(2 * p + 2) * WINDOW])
                emit(ps, dq_keep[:, ps])
                dq_keep[:, ps] = (_rope_t(dqp, cosc_ref[...], sinc_ref[...], first) * SCALE).astype(ACT)
            ks = slice(d + h * PAIR, d + (h + 1) * PAIR)
            vs = slice(d + kd + h * PAIR, d + kd + (h + 1) * PAIR)
            emit(ks, dk_keep[:, hs[h]] + _rope_t(dk2[h][:WINDOW], cosp_ref[...], sinp_ref[...], first))
            dk_keep[:, hs[h]] = _rope_t(dk2[h][WINDOW:], cosc_ref[...], sinc_ref[...], first)
            emit(vs, dv_keep[:, hs[h]] + dv2[h][:WINDOW])
            dv_keep[:, hs[h]] = dv2[h][WINDOW:]
        dsink_ref[...] += dsink

    kv_c, kv_p = pl.BlockSpec((WINDOW, kd), cur), pl.BlockSpec((WINDOW, kd), prev)
    tab_c, tab_p = pl.BlockSpec((WINDOW, PAIR), cur), pl.BlockSpec((WINDOW, PAIR), prev)
    return _call(
        body, "attn_bwd", (nb + 1,),
        [pl.BlockSpec(memory_space=pltpu.SMEM), pl.BlockSpec((WINDOW, d), cur), kv_c, kv_p, kv_c, kv_p,
         pl.BlockSpec((WINDOW, d), cur), tab_c, tab_c, tab_p, tab_p],
        [pl.BlockSpec((WINDOW, n_out), late), _full((1, n_out)), _full((8, LANES))],
        [_sds((t, n_out), ACT), _sds((1, n_out), F32), _sds((8, LANES), F32)],
        (sinks, q, kd_, kd_, vd, vd, do, cos, sin, cos, sin), ("arbitrary",),
        scratch=[pltpu.VMEM((WINDOW, d), ACT), pltpu.VMEM((WINDOW, kd), F32), pltpu.VMEM((WINDOW, kd), F32)], jobs=jobs)


def _proj_res(a, w, b, g, x, name, target=None, jobs=()):
    t = a.shape[0]
    k, d = w.shape
    tm = _row_tile(t)
    has_bias = b is not None

    def body(*refs):
        a_ref, w_ref = refs[:2]
        b_ref = refs[2] if has_bias else None
        g_ref, x_ref, m_ref, xo_ref = refs[2 + has_bias:]
        m = _dot(a_ref[...], w_ref[...])
        if has_bias:
            m = m + b_ref[...]
        m_ref[...] = m.astype(ACT)
        xo_ref[...] = x_ref[...] + (m * _rstd(m)) * g_ref[...]

    def body_with_loss(a_ref, w_ref, g_ref, x_ref, t_ref, m_ref, dy_ref, loss_ref):
        @pl.when(pl.program_id(0) == 0)
        def _():
            loss_ref[...] = jnp.zeros_like(loss_ref)

        body(a_ref, w_ref, g_ref, x_ref, m_ref, dy_ref)
        err = dy_ref[...] - t_ref[...]
        dy_ref[...] = err * (1.0 / d)
        loss_ref[...] += 0.5 * jnp.sum(jnp.mean(err * err, axis=-1, keepdims=True), axis=0, keepdims=True)

    ins = [a, w] + ([b] if has_bias else []) + [g, x]
    specs = [_rows(tm, k), _full((k, d))] + ([_full((1, d))] if has_bias else []) + [_full((1, d)), _rows(tm, d)]
    if target is not None:
        return _call(body_with_loss, name, (t // tm,), specs + [_rows(tm, d)], [_rows(tm, d), _rows(tm, d), _full((8, LANES))],
                     [_sds((t, d), ACT), _sds((t, d), F32), _sds((8, LANES), F32)], ins + [target], ("arbitrary",), jobs=jobs)
    return _call(body, name, (t // tm,), specs, [_rows(tm, d), _rows(tm, d)], [_sds((t, d), ACT), _sds((t, d), F32)], ins,
                 ("parallel",), jobs=jobs)


def _post_bwd(dres, m, g, w, mode, name, jobs=()):
    t, d = dres.shape
    k = w.shape[0]
    tm = _row_tile(t)
    kc = _pick(k, (1408, 1024, 512))

    def body(*refs):
        d_ref, m_ref, g_ref, w_ref = refs[:4]
        outs = refs[4:]
        dm_ref, da_ref, dg_ref = outs[:3]
        i = pl.program_id(0)

        @pl.when(i == 0)
        def _():
            dg_ref[...] = jnp.zeros_like(dg_ref)
            if mode == "attn":
                outs[3][...] = jnp.zeros_like(outs[3])

        dv, mv = d_ref[...], m_ref[...].astype(F32)
        r = _rstd(mv)
        mh = mv * r
        dg_ref[...] += jnp.sum(dv * mh, axis=0, keepdims=True)
        dm = _rms_bwd(mh, r, g_ref[...], dv)
        dmb = dm.astype(ACT)
        dm_ref[...] = dmb
        if mode == "attn":
            outs[3][...] += jnp.sum(dm, axis=0, keepdims=True)
        for c in range(0, k, kc):
            da = _dot_nt(dmb, w_ref[c:c + kc, :])
            da_ref[:, c:c + kc] = da.astype(ACT)

    ins = [dres, m, g, w]
    specs = [_rows(tm, d), _rows(tm, d), _full((1, d)), _full((k, d))]
    out_specs = [_rows(tm, d), _rows(tm, k), _full((1, d))]
    out_shape = [_sds((t, d), ACT), _sds((t, k), ACT), _sds((1, d), F32)]
    if mode == "attn":
        out_specs.append(_full((1, d)))
        out_shape.append(_sds((1, d), F32))
    return _call(body, name, (t // tm,), specs, out_specs, out_shape, ins, ("arbitrary",), jobs=jobs)


def _pre_bwd(dy, w, x, g, dres, name, jobs=()):
    t, d = x.shape
    tm = _row_tile(t)

    def body(dy_ref, w_ref, x_ref, g_ref, dres_ref, dx_ref, dg_ref):
        @pl.when(pl.program_id(0) == 0)
        def _():
            dg_ref[...] = jnp.zeros_like(dg_ref)

        dh = jnp.zeros((tm, d), F32)
        if w.ndim == 3:
            c = w.shape[2]
            for j in range(w.shape[0]):
                dh = dh + _dot_nt(dy_ref[:, j * c:(j + 1) * c], w_ref[j])
        else:
            n = w.shape[1]
            nc = _pick(n, (1408, 1024, 512))
            for c in range(0, n, nc):
                dh = dh + _dot_nt(dy_ref[:, c:c + nc], w_ref[:, c:c + nc])
        xv = x_ref[...]
        r = _rstd(xv)
        xh = xv * r
        dg_ref[...] += jnp.sum(dh * xh, axis=0, keepdims=True)
        dx_ref[...] = dres_ref[...] + _rms_bwd(xh, r, g_ref[...], dh)

    return _call(
        body, name, (t // tm,),
        [_rows(tm, dy.shape[1]), _full(w.shape), _rows(tm, d), _full((1, d)), _rows(tm, d)],
        [_rows(tm, d), _full((1, d))], [_sds((t, d), F32), _sds((1, d), F32)],
        (dy, w, x, g, dres), ("arbitrary",), jobs=jobs)


def _ffn_bwd(dres, f, g_post, w_dn, gu, w_gu, x, g_pre, name, jobs=()):
    t, d = dres.shape
    n = w_dn.shape[0]
    tm = _row_tile(t, 256)

    def body(d_ref, f_ref, gp_ref, wd_ref, gu_ref, wu_ref, x_ref, g_ref, df_ref, dgu_ref, dx_ref, dgp_ref, dg_ref):
        @pl.when(pl.program_id(0) == 0)
        def _():
            dgp_ref[...] = jnp.zeros_like(dgp_ref)
            dg_ref[...] = jnp.zeros_like(dg_ref)

        dv, fv = d_ref[...], f_ref[...].astype(F32)
        r = _rstd(fv)
        fh = fv * r
        dgp_ref[...] += jnp.sum(dv * fh, axis=0, keepdims=True)
        dfb = _rms_bwd(fh, r, gp_ref[...], dv).astype(ACT)
        df_ref[...] = dfb
        for c, size in _mxu_chunks(n):
            da = _dot_nt(dfb, wd_ref[c:c + size, :]).astype(ACT)
            gate, up = gu_ref[:, c:c + size], gu_ref[:, n + c:n + c + size]
            sg = _sigmoid(gate.astype(F32)).astype(ACT)
            gs = gate * sg
            dgu_ref[:, c:c + size] = da * up * (sg + gs - gs * sg)
            dgu_ref[:, n + c:n + c + size] = da * gs
        dh = _dot(dgu_ref[...], wu_ref[...])
        xv = x_ref[...]
        rx = _rstd(xv)
        xh = xv * rx
        dg_ref[...] += jnp.sum(dh * xh, axis=0, keepdims=True)
        dx_ref[...] = dv + _rms_bwd(xh, rx, g_ref[...], dh)

    return _call(
        body, name, (t // tm,),
        [_rows(tm, d), _rows(tm, d), _full((1, d)), _resident(w_dn.shape), _rows(tm, 2 * n), _resident(w_gu.shape), _rows(tm, d),
         _full((1, d))],
        [_rows(tm, d), _rows(tm, 2 * n), _rows(tm, d), _full((1, d)), _full((1, d))],
        [_sds((t, d), ACT), _sds((t, 2 * n), ACT), _sds((t, d), F32), _sds((1, d), F32), _sds((1, d), F32)],
        (dres, f, g_post, w_dn, gu, w_gu, x, g_pre), ("arbitrary",), jobs=jobs)


def _wgrad(a, b, name, out_dtype=F32, out_block=None, transposed=False, jobs=()):
    t = a.shape[0]
    tt = _pick(t, (1024,))
    steps = t // tt
    tk = _pick(a.shape[1], (1024, 1408))
    k, n = a.shape[1], b.shape[1]
    tn = _pick(n, (1024, 1408))
    per = 0
    if transposed:
        out_spec, out_shape, acc_shape = pl.BlockSpec((tn, tk), lambda i, j, s: (j, i)), _sds((n, k), out_dtype), (tn, tk)
    elif out_block is None:
        out_spec, out_shape, acc_shape = pl.BlockSpec((tk, tn), lambda i, j, s: (i, j)), _sds((k, n), out_dtype), (tk, tn)
    else:
        per = tn // out_block
        out_spec = pl.BlockSpec((per, tk, out_block), lambda i, j, s: (j, i, 0))
        out_shape, acc_shape = _sds((n // out_block, k, out_block), out_dtype), (tk, tn)

    def body(a_ref, b_ref, o_ref, acc_ref):
        s = pl.program_id(2)

        @pl.when(s == 0)
        def _():
            acc_ref[...] = jnp.zeros_like(acc_ref)

        acc_ref[...] += _dot_tn(b_ref[...], a_ref[...]) if transposed else _dot_tn(a_ref[...], b_ref[...])

        @pl.when(s == steps - 1)
        def _():
            if per >= 1:
                for p in range(per):
                    o_ref[p] = acc_ref[:, p * out_block:(p + 1) * out_block].astype(out_dtype)
            else:
                o_ref[...] = acc_ref[...].astype(out_dtype)

    res, job_res = _call(
        body, name, (k // tk, n // tn, steps),
        [pl.BlockSpec((tt, tk), lambda i, j, s: (s, i)), pl.BlockSpec((tt, tn), lambda i, j, s: (s, j))], [out_spec], [out_shape],
        (a, b), ("parallel", "parallel", "arbitrary"), scratch=[pltpu.VMEM(acc_shape, F32)], jobs=jobs)
    return res[0], job_res


def _ffn_up_fwd(x, g, w, jobs=()):
    t, d = x.shape
    n = w.shape[0] // 2
    tm = _row_tile(t)

    def body(x_ref, g_ref, w_ref, h_ref, gu_ref, a_ref):
        xv = x_ref[...]
        hb = (xv * _rstd(xv) * g_ref[...]).astype(ACT)
        h_ref[...] = hb
        for c, size in _mxu_chunks(n):
            gate = _dot_nt(hb, w_ref[c:c + size, :])
            up = _dot_nt(hb, w_ref[n + c:n + c + size, :])
            gu_ref[:, c:c + size] = gate.astype(ACT)
            gu_ref[:, n + c:n + c + size] = up.astype(ACT)
            a_ref[:, c:c + size] = (gate * _sigmoid(gate) * up).astype(ACT)

    return _call(
        body, "ffn_up_fwd", (t // tm,),
        [_rows(tm, d), _full((1, d)), _full(w.shape)],
        [_rows(tm, d), _rows(tm, 2 * n), _rows(tm, n)],
        [_sds((t, d), ACT), _sds((t, 2 * n), ACT), _sds((t, n), ACT)],
        (x, g, w), ("parallel",), jobs=jobs)


def _causal(ws):
    row = lax.broadcasted_iota(jnp.int32, ws.shape, 0)
    col = lax.broadcasted_iota(jnp.int32, ws.shape, 1)
    return jnp.where(col <= row, ws, 0.0)


def _sgu_ln(v, lg, lb):
    mu = jnp.mean(v, axis=-1, keepdims=True)
    vc = v - mu
    rs = lax.rsqrt(jnp.mean(vc * vc, axis=-1, keepdims=True) + EPS)
    vh = vc * rs
    return vh, rs, vh * lg + lb


def _sgu_fwd(x, g, w, lg, lb, ws, bs_t, jobs=()):
    t, d = x.shape
    nb, _, c = w.shape
    n = nb * c
    groups = d // WINDOW
    tm = _row_tile(t)

    def body(x_ref, g_ref, w_ref, lg_ref, lb_ref, ws_ref, bs_ref, h_ref, z_ref, y_ref, zf_ref):
        xv = x_ref[...]
        hb = (xv * _rstd(xv) * g_ref[...]).astype(ACT)
        h_ref[...] = hb
        for j in range(nb):
            zf_ref[:, j * c:(j + 1) * c] = _dot(hb, w_ref[j])
        z_ref[...] = zf_ref[...].astype(ACT)
        gs = [slice(gi * WINDOW, (gi + 1) * WINDOW) for gi in range(groups)]
        wsg = [_causal(ws_ref[gi]).astype(ACT) for gi in range(groups)]
        for r in range(0, tm, WINDOW):
            u = _gelu(zf_ref[r:r + WINDOW, :d])
            _, _, vn = _sgu_ln(_gelu(zf_ref[r:r + WINDOW, d:]), lg_ref[...], lb_ref[...])
            mixed = [_dot(wsg[gi], vn[:, gs[gi]].astype(ACT)) for gi in range(groups)]
            for gi in range(groups):
                y_ref[r:r + WINDOW, gs[gi]] = (u[:, gs[gi]] * (mixed[gi] + bs_ref[:, gi:gi + 1])).astype(ACT)

    vec = _full((1, d))
    return _call(
        body, "sgu_fwd", (t // tm,),
        [_rows(tm, d), vec, _full((nb, d, c)), vec, vec, _full((groups, WINDOW, WINDOW)), _full((WINDOW, groups))],
        [_rows(tm, d), _rows(tm, n), _rows(tm, d)], [_sds((t, d), ACT), _sds((t, n), ACT), _sds((t, d), ACT)],
        (x, g, w, lg, lb, ws, bs_t), ("parallel",), scratch=[pltpu.VMEM((tm, n), F32)], jobs=jobs)


def _sgu_mix_bwd(z, dy, lg, lb, ws, bs_t, jobs=()):
    t, n = z.shape
    d = n // 2
    groups = d // WINDOW
    tm = _row_tile(t)

    def body(z_ref, dy_ref, lg_ref, lb_ref, ws_ref, bs_ref, dz_ref, dlg_ref, dlb_ref, dws_ref, dbs_ref):
        @pl.when(pl.program_id(0) == 0)
        def _():
            for ref in (dlg_ref, dlb_ref, dws_ref, dbs_ref):
                ref[...] = jnp.zeros_like(ref)

        lane = lax.broadcasted_iota(jnp.int32, (WINDOW, groups), 1)
        gs = [slice(gi * WINDOW, (gi + 1) * WINDOW) for gi in range(groups)]
        wsg = [_causal(ws_ref[gi]).astype(ACT) for gi in range(groups)]
        for c in range(0, tm, WINDOW):
            rows = slice(c, c + WINDOW)
            zu, zv = z_ref[rows, :d].astype(F32), z_ref[rows, d:].astype(F32)
            u = _gelu(zu)
            vh, rs, vn = _sgu_ln(_gelu(zv), lg_ref[...], lb_ref[...])
            dyv = dy_ref[rows, :].astype(F32)
            vng = [vn[:, gs[gi]].astype(ACT) for gi in range(groups)]
            dmix = dyv * u
            dmb = [dmix[:, gs[gi]].astype(ACT) for gi in range(groups)]
            mixed = [_dot(wsg[gi], vng[gi]) for gi in range(groups)]
            dvn_parts = [_dot_tn(wsg[gi], dmb[gi]) for gi in range(groups)]
            dws_parts = [_dot_nt(dmb[gi], vng[gi]) for gi in range(groups)]
            for gi in range(groups):
                dz_ref[rows, gs[gi]] = (dyv[:, gs[gi]] * (mixed[gi] + bs_ref[:, gi:gi + 1]) * _gelu_grad(zu[:, gs[gi]])).astype(ACT)
                dws_ref[:, gs[gi]] += _causal(dws_parts[gi])
                dbs_ref[...] += jnp.where(lane == gi, jnp.sum(dmix[:, gs[gi]], axis=-1, keepdims=True), 0.0)
            dvn = jnp.concatenate(dvn_parts, axis=-1)
            dlg_ref[...] += jnp.sum(dvn * vh, axis=0, keepdims=True)
            dlb_ref[...] += jnp.sum(dvn, axis=0, keepdims=True)
            dvh = dvn * lg_ref[...]
            dv = rs * (dvh - jnp.mean(dvh, axis=-1, keepdims=True) - vh * jnp.mean(dvh * vh, axis=-1, keepdims=True))
            dz_ref[rows, d:] = (dv * _gelu_grad(zv)).astype(ACT)

    vec = _full((1, d))
    return _call(
        body, "sgu_mix_bwd", (t // tm,),
        [_rows(tm, n), _rows(tm, d), vec, vec, _full((groups, WINDOW, WINDOW)), _full((WINDOW, groups))],
        [_rows(tm, n), vec, vec, _full((WINDOW, d)), _full((WINDOW, groups))],
        [_sds((t, n), ACT), _sds((1, d), F32), _sds((1, d), F32), _sds((WINDOW, d), F32), _sds((WINDOW, groups), F32)],
        (z, dy, lg, lb, ws, bs_t), ("arbitrary",), jobs=jobs)


AG_COPIES = 8


def _prepare(sources, dtypes, n_gather, name):
    n = len(sources)
    arrays, where = [], []
    for parts, layer in sources:
        found = []
        for part in parts:
            known = [i for i, v in enumerate(arrays) if v is part]
            if not known:
                arrays.append(part)
            found.append(known[0] if known else len(arrays) - 1)
        where.append((found, layer))
    shapes = [(sum(p.shape[1] for p in parts), parts[0].shape[2]) for parts, _ in sources]

    def body(*refs):
        ins, refs = refs[:len(arrays)], refs[len(arrays):]
        stage, outs = refs[:n], refs[n:n + n_gather]
        send, recv, local_sem = refs[n + n_gather:]
        x, y, c = _place()
        me, sibling, beside_x, beside_y, far = (x, y, c), (x, y, 1 - c), (1 - x, y, c), (x, 1 - y, c), (1 - x, 1 - y, c)
        slot = lambda dev: 4 * dev[0] + 2 * dev[1] + dev[2]
        other = lambda dev: (dev[0], dev[1], 1 - c)
        whole = lambda a: (0, shapes[a][0])
        cuts = [rows // 2 if rows % 32 == 0 else rows for rows, _ in shapes]
        first = lambda a: (0, cuts[a])
        rest = lambda a: (cuts[a], shapes[a][0] - cuts[a])

        def copy(a, k, block, rows, to, src=None):
            dst = outs[a].at[block, pl.ds(*rows)]
            return pltpu.make_async_remote_copy(
                src_ref=dst if src is None else src, dst_ref=dst, send_sem=send.at[a * AG_COPIES + k],
                recv_sem=recv.at[a * AG_COPIES + k], device_id=to, device_id_type=MESH)

        def cast(a):
            found, layer = where[a]
            at = 0
            for i in found:
                rows = arrays[i].shape[1]
                stage[a][at:at + rows, :] = ins[i][layer].astype(dtypes[a])
                at += rows

        for a in range(n_gather):
            cast(a)
        started = []
        for a in range(n_gather):
            own = pltpu.make_async_copy(stage[a], outs[a].at[slot(me)], local_sem.at[a])
            own.start()
            started.append(own)
        sends = []
        for a in range(n_gather):
            sends += [copy(a, k, slot(me), whole(a), to, src=stage[a]) for k, to in enumerate((sibling, beside_x, beside_y))]
        for cp in sends:
            cp.start()
        for a in range(n_gather, n):
            cast(a)
        for a in range(n_gather):
            copy(a, 1, slot(beside_x), whole(a), me).wait_recv()
            copy(a, 2, slot(beside_y), whole(a), me).wait_recv()
            passed = [copy(a, 3, slot(beside_x), first(a), beside_y), copy(a, 5, slot(beside_x), whole(a), sibling),
                      copy(a, 6, slot(beside_y), whole(a), sibling)]
            if rest(a)[1]:
                passed.append(copy(a, 4, slot(beside_y), rest(a), beside_x))
            for cp in passed:
                cp.start()
            sends += passed
        for a in range(n_gather):
            copy(a, 3, slot(far), first(a), me).wait_recv()
            if rest(a)[1]:
                copy(a, 4, slot(far), rest(a), me).wait_recv()
            passed = copy(a, 7, slot(far), whole(a), sibling)
            passed.start()
            sends.append(passed)
        for a in range(n_gather):
            for k, source in ((0, me), (5, beside_x), (6, beside_y), (7, far)):
                copy(a, k, slot(other(source)), whole(a), me).wait_recv()
        for cp in sends:
            cp.wait_send()
        for own in started:
            own.wait()

    res = pl.pallas_call(
        body, name=name,
        in_specs=[IN_VMEM] * len(arrays), out_specs=[IN_VMEM] * n + [ANY] * n_gather,
        out_shape=[_sds(s, dt) for s, dt in zip(shapes, dtypes)]
        + [_sds((N_DEV,) + s, dt) for s, dt in zip(shapes[:n_gather], dtypes)],
        scratch_shapes=[pltpu.SemaphoreType.DMA((n_gather * AG_COPIES,)), pltpu.SemaphoreType.DMA((n_gather * AG_COPIES,)),
                        pltpu.SemaphoreType.DMA((n_gather,))],
        compiler_params=pltpu.CompilerParams(vmem_limit_bytes=VMEM_LIMIT_BYTES),
    )(*arrays)
    return list(res[:n]), list(res[n:])


def _pair_sum(g, r, core, name):
    _, _, rows, cols = g.shape
    tr = _pick(rows, (512, 256, 128))

    def body(core_ref, g_ref, r_ref, p_ref):
        del core_ref
        p_ref[...] = (g_ref[...].astype(F32) + r_ref[...].astype(F32)).astype(p_ref.dtype)

    return pl.pallas_call(
        body, name=name,
        grid_spec=pltpu.PrefetchScalarGridSpec(
            num_scalar_prefetch=1, grid=(4, rows // tr),
            in_specs=[pl.BlockSpec((None, None, tr, cols), lambda q, i, core_ref: (q, core_ref[0], i, 0)),
                      pl.BlockSpec((None, tr, cols), lambda q, i, core_ref: (q, i, 0))],
            out_specs=pl.BlockSpec((None, tr, cols), lambda q, i, core_ref: (q, i, 0))),
        out_shape=_sds((4, rows, cols), g.dtype),
        compiler_params=_params("parallel", "parallel"),
    )(core, g, r)


def _adam(w, g, m, v):
    m2 = ADAM_B1 * m + (1.0 - ADAM_B1) * g
    v2 = ADAM_B2 * v + (1.0 - ADAM_B2) * (g * g)
    m_hat = m2 / (1.0 - ADAM_B1 ** ADAM_STEP)
    v_hat = v2 / (1.0 - ADAM_B2 ** ADAM_STEP)
    return -ADAM_LR * (m_hat / (jnp.sqrt(v_hat) + ADAM_EPS) + ADAM_WD * w), m2, v2


def _shard_update(own, index, received, w, m, v, layer, so_far, name):
    _, rows, cols = w.shape
    n_recv = received.shape[0]
    tr = _pick(rows, (512, 256, 128))

    def body(index_ref, p_ref, q_ref, w_ref, m_ref, v_ref, *rest):
        del index_ref
        g_out, d_out, m_out, v_out = rest[-4:]
        g = p_ref[...].astype(F32)
        for s in range(n_recv):
            g = g + q_ref[s].astype(F32)
        g_out[...] = g
        d_out[...], m_out[...], v_out[...] = _adam(w_ref[...], g, m_ref[...], v_ref[...])

    tile = pl.BlockSpec((None, tr, cols), lambda i, index_ref: (layer, i, 0))
    kept = list(so_far) if so_far is not None else []
    return pl.pallas_call(
        body, name=name,
        grid_spec=pltpu.PrefetchScalarGridSpec(
            num_scalar_prefetch=1, grid=(rows // tr,),
            in_specs=[pl.BlockSpec((None, tr, cols), lambda i, index_ref: (index_ref[0], i, 0)),
                      pl.BlockSpec((n_recv, tr, cols), lambda i, index_ref: (0, i, 0)), tile, tile, tile] + [ANY] * len(kept),
            out_specs=[tile] * 4),
        out_shape=[_sds(w.shape, F32)] * 4,
        input_output_aliases={6 + i: i for i in range(len(kept))},
        compiler_params=_params("parallel"),
    )(index, own, received, w, m, v, *kept)


def _natural_pieces(shape, r, c):
    if tuple(shape[-2:]) == (r, c) and all(n == 1 for n in shape[:-2]):
        return [((0,) * (len(shape) - 2), (0, c))]
    groups, width = shape[1], shape[3]
    assert len(shape) == 4 and shape[0] == 1 and shape[2] == r and groups * width == c, (shape, r, c)
    return [((0, g), (g * width, width)) for g in range(groups)]


def _replicated_update(shares, where, params, name):
    flat = [a for p in params if p is not None for a in p]

    def body(s_ref, *refs):
        ins, outs = refs[:len(flat)], refs[len(flat):]
        total = s_ref[0]
        for dev in range(1, N_DEV):
            total = total + s_ref[dev]
        i_at = o_at = 0
        for ranges, p in zip(where, params):
            chunks = [total[r0:r0 + r, :c] for r0, r, c in ranges]
            g2d = chunks[0] if len(chunks) == 1 else jnp.concatenate(chunks, axis=1)
            if p is None:
                outs[o_at][...] = g2d
                o_at += 1
                continue
            w_ref, m_ref, v_ref = ins[i_at:i_at + 3]
            for idx, (c0, cols) in _natural_pieces(p[0].shape, *g2d.shape):
                at = idx + (slice(None), slice(None))
                g = g2d[:, c0:c0 + cols]
                outs[o_at][at] = g
                outs[o_at + 1][at], outs[o_at + 2][at], outs[o_at + 3][at] = _adam(w_ref[at], g, m_ref[at], v_ref[at])
            i_at, o_at = i_at + 3, o_at + 4

    out_shape = []
    for ranges, p in zip(where, params):
        out_shape += [_sds((ranges[0][1], sum(c for _, _, c in ranges)), F32)] if p is None else [_sds(p[0].shape, F32)] * 4
    res = pl.pallas_call(
        body, name=name, out_shape=out_shape, compiler_params=pltpu.CompilerParams(vmem_limit_bytes=VMEM_LIMIT_BYTES),
    )(shares, *flat)
    out, at = [], 0
    for p in params:
        out.append(list(res[at:at + (1 if p is None else 4)]))
        at += 1 if p is None else 4
    return out


def _rope_tables(t):
    half = HEAD_DIM // 2
    inv_freq = ROPE_THETA ** (-(jnp.arange(half, dtype=F32) * 2.0) / HEAD_DIM)
    ang = jnp.arange(t, dtype=jnp.int32).astype(F32)[:, None] * inv_freq[None, :]
    cos, sin = jnp.cos(ang), jnp.sin(ang)
    return jnp.tile(jnp.concatenate([cos, cos], axis=1), (1, 2)), jnp.tile(jnp.concatenate([-sin, sin], axis=1), (1, 2))


def _dup_heads(w, d):
    kv = (w.shape[-1] - d) // 2
    lead = w.shape[:-1]

    def dup(part):
        heads = part.reshape(lead + (kv // HEAD_DIM, 1, HEAD_DIM))
        return jnp.broadcast_to(heads, lead + (kv // HEAD_DIM, 2, HEAD_DIM)).reshape(lead + (2 * kv,))

    return jnp.concatenate([w[..., :d], dup(w[..., d:d + kv]), dup(w[..., d + kv:])], axis=-1)


def _fold_heads(dw, d):
    kv = (dw.shape[-1] - d) // 4
    lead = dw.shape[:-1]

    def fold(part):
        return part.reshape(lead + (kv // HEAD_DIM, 2, HEAD_DIM)).sum(axis=-2).reshape(lead + (kv,))

    return jnp.concatenate([dw[..., :d], fold(dw[..., d:d + 2 * kv]), fold(dw[..., d + 2 * kv:])], axis=-1)


def _columns_full(gathered):
    _, rows, c = gathered.shape
    return jnp.transpose(gathered, (1, 0, 2)).reshape(rows, N_DEV * c)


def _rows_full(gathered):
    return gathered.reshape(-1, gathered.shape[-1])


def _columns_blocked(full):
    rows, cols = full.shape
    return jnp.transpose(full.reshape(rows, N_DEV, cols // N_DEV), (1, 0, 2)).astype(ACT)


def _rows_blocked(full):
    return full.reshape(N_DEV, full.shape[0] // N_DEV, full.shape[1]).astype(ACT)


def _pack_rows(parts, width):
    rows, where, at = [], [], 0
    for v in parts:
        r, c = v.shape
        n_rows = -(-r // 8) * 8
        chunks = []
        for c0 in range(0, c, width):
            chunk = v[:, c0:c0 + width].astype(F32)
            rows.append(jnp.pad(chunk, ((0, n_rows - r), (0, width - chunk.shape[1]))))
            chunks.append((at, r, chunk.shape[1]))
            at += n_rows
        where.append(chunks)
    return jnp.concatenate(rows, axis=0), where


def _halves(block):
    half = block.shape[0] // 2
    return (0, half), (half, half)


def _whole(block):
    return (0, block.shape[0])


EARLY = ("attn_w_qkv", "sgu_ln", "attn_w_o")
LATE = ("sgu_w_in", "sgu_w_out", "gu0", "gu1", "dn0", "dn1")
COLUMN_SHARDED = ("attn_w_qkv", "sgu_w_in", "gu0", "gu1")
BEFORE_ATTN = ("norm_mix_post", "norm_ffn_pre", "norm_ffn_post", "attn_b_o", "sgu_w_spatial", "sgu_b_spatial")
AFTER_ATTN = ("attn_b_qkv", "attn_sinks")
LAST = ("norm_mix_pre",)
WEIGHTS = ("norm_mix_pre", "norm_mix_post", "norm_ffn_pre", "norm_ffn_post", "attn_w_qkv", "attn_b_qkv", "attn_sinks", "attn_w_o",
           "attn_b_o", "sgu_w_in", "sgu_ln_g", "sgu_ln_b", "sgu_w_spatial", "sgu_b_spatial", "sgu_w_out", "ffn_w_gate_up", "ffn_w_down")


LAYER_OF = {"gu0": ("ffn_w_gate_up", 0), "gu1": ("ffn_w_gate_up", 1), "dn0": ("ffn_w_down", 0), "dn1": ("ffn_w_down", 1)}


def _source(tree, name):
    if name == "sgu_ln":
        return [tree["sgu_ln_g"][None], tree["sgu_ln_b"][None]], 0
    leaf, layer = LAYER_OF.get(name, (name, 0))
    return [tree[leaf]], layer


def kernel(x, norm_mix_pre, norm_mix_post, norm_ffn_pre, norm_ffn_post, attn_w_qkv, attn_b_qkv, attn_sinks, attn_w_o, attn_b_o, sgu_w_in, sgu_ln_g, sgu_ln_b, sgu_w_spatial, sgu_b_spatial, sgu_w_out, ffn_w_gate_up, ffn_w_down, loss_target, m_norm_mix_pre, m_norm_mix_post, m_norm_ffn_pre, m_norm_ffn_post, m_attn_w_qkv, m_attn_b_qkv, m_attn_sinks, m_attn_w_o, m_attn_b_o, m_sgu_w_in, m_sgu_ln_g, m_sgu_ln_b, m_sgu_w_spatial, m_sgu_b_spatial, m_sgu_w_out, m_ffn_w_gate_up, m_ffn_w_down, v_norm_mix_pre, v_norm_mix_post, v_norm_ffn_pre, v_norm_ffn_post, v_attn_w_qkv, v_attn_b_qkv, v_attn_sinks, v_attn_w_o, v_attn_b_o, v_sgu_w_in, v_sgu_ln_g, v_sgu_ln_b, v_sgu_w_spatial, v_sgu_b_spatial, v_sgu_w_out, v_ffn_w_gate_up, v_ffn_w_down):
    w = dict(norm_mix_pre=norm_mix_pre, norm_mix_post=norm_mix_post, norm_ffn_pre=norm_ffn_pre, norm_ffn_post=norm_ffn_post,
             attn_w_qkv=attn_w_qkv, attn_b_qkv=attn_b_qkv, attn_sinks=attn_sinks, attn_w_o=attn_w_o, attn_b_o=attn_b_o,
             sgu_w_in=sgu_w_in, sgu_ln_g=sgu_ln_g, sgu_ln_b=sgu_ln_b, sgu_w_spatial=sgu_w_spatial, sgu_b_spatial=sgu_b_spatial,
             sgu_w_out=sgu_w_out, ffn_w_gate_up=ffn_w_gate_up, ffn_w_down=ffn_w_down)
    mom = dict(norm_mix_pre=m_norm_mix_pre, norm_mix_post=m_norm_mix_post, norm_ffn_pre=m_norm_ffn_pre, norm_ffn_post=m_norm_ffn_post,
               attn_w_qkv=m_attn_w_qkv, attn_b_qkv=m_attn_b_qkv, attn_sinks=m_attn_sinks, attn_w_o=m_attn_w_o, attn_b_o=m_attn_b_o,
               sgu_w_in=m_sgu_w_in, sgu_ln_g=m_sgu_ln_g, sgu_ln_b=m_sgu_ln_b, sgu_w_spatial=m_sgu_w_spatial,
               sgu_b_spatial=m_sgu_b_spatial, sgu_w_out=m_sgu_w_out, ffn_w_gate_up=m_ffn_w_gate_up, ffn_w_down=m_ffn_w_down)
    var = dict(norm_mix_pre=v_norm_mix_pre, norm_mix_post=v_norm_mix_post, norm_ffn_pre=v_norm_ffn_pre, norm_ffn_post=v_norm_ffn_post,
               attn_w_qkv=v_attn_w_qkv, attn_b_qkv=v_attn_b_qkv, attn_sinks=v_attn_sinks, attn_w_o=v_attn_w_o, attn_b_o=v_attn_b_o,
               sgu_w_in=v_sgu_w_in, sgu_ln_g=v_sgu_ln_g, sgu_ln_b=v_sgu_ln_b, sgu_w_spatial=v_sgu_w_spatial,
               sgu_b_spatial=v_sgu_b_spatial, sgu_w_out=v_sgu_w_out, ffn_w_gate_up=v_ffn_w_gate_up, ffn_w_down=v_ffn_w_down)
    w, mom, var = [{**tree, "ffn_w_gate_up": jnp.swapaxes(tree["ffn_w_gate_up"], 1, 2)} for tree in (w, mom, var)]
    xs, target = x[0], loss_target[0]
    t, d = xs.shape
    row = lambda v: v.reshape(1, -1).astype(F32)
    core = lax.axis_index("c").astype(jnp.int32).reshape(1)
    chip = (2 * lax.axis_index("x") + lax.axis_index("y")).astype(jnp.int32).reshape(1)
    names = EARLY + LATE
    staged, early = _prepare([_source(w, n) for n in names], [F32 if n == "sgu_ln" else ACT for n in names], len(EARLY),
                             "weights_prepare")
    stage = dict(zip(names, staged))
    w_qkv = _dup_heads(_columns_full(early[0]), d)
    lg, lb = row(early[1][:, 0, :]), row(early[1][:, 1, :])
    w_o = _rows_full(early[2])
    b_qkv = _dup_heads(row(attn_b_qkv), d)
    sinks = attn_sinks.reshape(1, -1).astype(F32)
    ws = sgu_w_spatial[0].astype(F32)
    bs_t = sgu_b_spatial[0].astype(F32).T
    cos, sin = _rope_tables(t)
    gather = lambda name, so_far, **pieces: _gather_job(stage[name], so_far, **pieces)
    a_half = lambda name: _halves(stage[name])[0]
    b_half = lambda name: _halves(stage[name])[1]
    whole = lambda name: _whole(stage[name])

    def pieces(name, n):
        rows = stage[name].shape[0] // n
        return [(i * rows, rows) for i in range(n)]

    ways = lambda parts: [(piece, i % 2) for i, piece in enumerate(parts)]
    relay = lambda name, so_far, **steps: _relay_gather_job(stage[name], so_far, **steps)
    gu0_parts, gu1_parts = pieces("gu0", 4), pieces("gu1", 4)
    dn0_parts, dn1_parts, in_parts, out_parts = pieces("dn0", 2), pieces("dn1", 2), pieces("sgu_w_in", 2), pieces("sgu_w_out", 2)
    (h0, q, kdup, vdup), ((g_gu0,),) = _attn_qkv_fwd(
        xs, row(norm_mix_pre[0]), w_qkv, b_qkv, cos, sin, jobs=[relay("gu0", None, sends=gu0_parts)])
    (o,), ((g_gu0,), (g_dn0,)) = _attn_fwd(q, kdup, vdup, sinks, jobs=[
        relay("gu0", g_gu0, relays=ways(gu0_parts), forwards=gu0_parts), relay("dn0", None, sends=dn0_parts)])
    (m0, x1), ((g_gu0,), (g_dn0,), (g_in,)) = _proj_res(o, w_o, row(attn_b_o), row(norm_mix_post[0]), xs, "attn_out_fwd", jobs=[
        relay("gu0", g_gu0, far=gu0_parts), relay("dn0", g_dn0, relays=ways(dn0_parts), forwards=dn0_parts),
        relay("sgu_w_in", None, sends=in_parts)])
    w_gu0 = _rows_full(g_gu0)
    (h1, gu0, a0), ((g_dn0,), (g_in,), (g_out,), (g_gu1,)) = _ffn_up_fwd(x1, row(norm_ffn_pre[0]), w_gu0, jobs=[
        relay("dn0", g_dn0, far=dn0_parts), relay("sgu_w_in", g_in, relays=ways(in_parts), forwards=in_parts),
        relay("sgu_w_out", None, sends=out_parts), relay("gu1", None, sends=gu1_parts[:2])])
    w_dn0 = _rows_full(g_dn0)
    (f0, x2), ((g_in,), (g_out,), (g_gu1,)) = _proj_res(a0, w_dn0, None, row(norm_ffn_post[0]), x1, "ffn_down_fwd0", jobs=[
        relay("sgu_w_in", g_in, far=in_parts), relay("sgu_w_out", g_out, relays=ways(out_parts), forwards=out_parts),
        relay("gu1", g_gu1, sends=gu1_parts[2:], relays=ways(gu1_parts[:2]), forwards=gu1_parts[:2])])
    w_in = g_in
    (h2, z, y), ((g_out,), (g_gu1,), (g_dn1,)) = _sgu_fwd(x2, row(norm_mix_pre[1]), w_in, lg, lb, ws, bs_t, jobs=[
        relay("sgu_w_out", g_out, far=out_parts),
        relay("gu1", g_gu1, relays=ways(gu1_parts[2:]), forwards=gu1_parts[2:], far=gu1_parts[:2]),
        relay("dn1", None, sends=dn1_parts)])
    w_out = _rows_full(g_out)
    (m1, x3), ((g_gu1,), (g_dn1,)) = _proj_res(y, w_out, None, row(norm_mix_post[1]), x2, "sgu_out_fwd", jobs=[
        relay("gu1", g_gu1, far=gu1_parts[2:]), relay("dn1", g_dn1, relays=ways(dn1_parts), forwards=dn1_parts)])
    w_gu1 = _rows_full(g_gu1)
    (h3, gu1, a1), ((g_dn1,),) = _ffn_up_fwd(x3, row(norm_ffn_pre[1]), w_gu1, jobs=[relay("dn1", g_dn1, far=dn1_parts)])
    w_dn1 = _rows_full(g_dn1)
    (f1, dx4, loss_tile), _ = _proj_res(a1, w_dn1, None, row(norm_ffn_post[1]), x3, "ffn_down_fwd1", target=target)

    to_sibling = lambda g: _scatter_job([g], 4, _to_sibling)
    pair = lambda g, r, name: _pair_sum(g.reshape((4, 2) + g.shape[1:]), r, core, "pair_sum_" + name)
    to_chips = lambda p, prev=None, piece=None: _scatter_job([p], 3, _to_owner_chip, prev=prev, piece=piece)
    out_g, out_d, out_m, out_v = {}, {}, {}, {}

    trees = [{**tree, "sgu_ln": jnp.stack([tree["sgu_ln_g"], tree["sgu_ln_b"]], axis=1)} for tree in (w, mom, var)]
    so_far = {}

    def update(name, own, index, received):
        leaf, layer = LAYER_OF.get(name, (name, 0))
        so_far[leaf] = _shard_update(own, index, received, *[tree[leaf] for tree in trees], layer, so_far.get(leaf), "update_" + name)
        for out, val in zip((out_g, out_d, out_m, out_v), so_far[leaf]):
            out[leaf] = val

    def finish(names, n_extra, shares, where, name):
        res = _replicated_update(shares, where, [(w[n], mom[n], var[n]) for n in names] + [None] * n_extra, name)
        for n, vals in zip(names, res):
            out_g[n], out_d[n], out_m[n], out_v[n] = vals
        return [vals[0] for vals in res[len(names):]]

    first_half = lambda p: _halves(p[0])[0]
    second_half = lambda p: _halves(p[0])[1]
    (df1, dgu1, dx3, dg_ffn_post1, dg_ffn_pre1), _ = _ffn_bwd(
        dx4, f1, row(norm_ffn_post[1]), w_dn1, gu1, w_gu1, x3, row(norm_ffn_pre[1]), "ffn_bwd1")
    b_gu1, _ = _wgrad(h3, dgu1, "ffn_up_wgrad1", ACT, transposed=True)
    b_gu1 = _rows_blocked(b_gu1)
    dw_dn1, _ = _wgrad(a1, df1, "ffn_down_wgrad1", ACT)
    b_dn1 = _rows_blocked(dw_dn1)
    (dm1, dy, dg_mix_post1), ((r_gu1,), (r_dn1,)) = _post_bwd(
        dx3, m1, row(norm_mix_post[1]), w_out, "sgu", "sgu_out_bwd", jobs=[to_sibling(b_gu1), to_sibling(b_dn1)])
    p_gu1, p_dn1 = pair(b_gu1, r_gu1, "gu1"), pair(b_dn1, r_dn1, "dn1")
    dw_out, _ = _wgrad(y, dm1, "sgu_out_wgrad", ACT)
    b_out = _rows_blocked(dw_out)
    (dz, dlg, dlb, dws, dbs_t), ((q_gu1,), (r_out,)) = _sgu_mix_bwd(z, dy, lg, lb, ws, bs_t, jobs=[
        to_chips(p_gu1, piece=first_half(p_gu1)), to_sibling(b_out)])
    p_out = pair(b_out, r_out, "sgu_w_out")
    b_in, _ = _wgrad(h2, dz, "sgu_in_wgrad", ACT, out_block=stage["sgu_w_in"].shape[1])
    b_ln = jnp.stack([dlg.reshape(N_DEV, -1), dlb.reshape(N_DEV, -1)], axis=1)
    (dx2, dg_mix_pre1), _ = _pre_bwd(dz, w_in, x2, row(norm_mix_pre[1]), dx3, "sgu_in_bwd")
    (df0, dgu0, dx1, dg_ffn_post0, dg_ffn_pre0), ((q_gu1,), (q_dn1,), (q_out,), (r_in,), (r_ln,)) = _ffn_bwd(
        dx2, f0, row(norm_ffn_post[0]), w_dn0, gu0, w_gu0, x1, row(norm_ffn_pre[0]), "ffn_bwd0",
        jobs=[to_chips(p_gu1, prev=[q_gu1], piece=second_half(p_gu1)), to_chips(p_dn1), to_chips(p_out), to_sibling(b_in),
              to_sibling(b_ln)])
    update("gu1", p_gu1, chip, q_gu1)
    update("dn1", p_dn1, chip, q_dn1)
    update("sgu_w_out", p_out, chip, q_out)
    p_in, p_ln = pair(b_in, r_in, "sgu_w_in"), pair(b_ln, r_ln, "sgu_ln")
    b_gu0, ((q_in,), (q_ln,)) = _wgrad(h1, dgu0, "ffn_up_wgrad0", ACT, transposed=True, jobs=[to_chips(p_in), to_chips(p_ln)])
    update("sgu_w_in", p_in, chip, q_in)
    update("sgu_ln", p_ln, chip, q_ln)
    b_gu0 = _rows_blocked(b_gu0)
    (dm0, do, dg_mix_post0, db_o), ((r_gu0,),) = _post_bwd(
        dx1, m0, row(norm_mix_post[0]), w_o, "attn", "attn_out_bwd", jobs=[to_sibling(b_gu0)])
    p_gu0 = pair(b_gu0, r_gu0, "gu0")
    dw_dn0, ((q_gu0,),) = _wgrad(a0, df0, "ffn_down_wgrad0", ACT, jobs=[to_chips(p_gu0, piece=first_half(p_gu0))])
    b_dn0 = _rows_blocked(dw_dn0)
    dw_o, ((r_dn0,),) = _wgrad(o, dm0, "attn_out_wgrad", ACT, jobs=[to_sibling(b_dn0)])
    p_dn0 = pair(b_dn0, r_dn0, "dn0")
    b_o = _rows_blocked(dw_o)
    grads = {
        "norm_mix_post": jnp.concatenate([dg_mix_post0, dg_mix_post1], axis=0),
        "norm_ffn_pre": jnp.concatenate([dg_ffn_pre0, dg_ffn_pre1], axis=0),
        "norm_ffn_post": jnp.concatenate([dg_ffn_post0, dg_ffn_post1], axis=0),
        "attn_b_o": db_o,
        "sgu_w_spatial": dws,
        "sgu_b_spatial": dbs_t.T,
    }
    shares1, where1 = _pack_rows([grads[name] for name in BEFORE_ATTN], d)
    (dqkv, db_qkv, dsink), ((q_gu0,), (q_dn0,), (r_o,), (g_shares1,)) = _attn_bwd(q, kdup, vdup, do, sinks, cos, sin, jobs=[
        to_chips(p_gu0, prev=[q_gu0], piece=second_half(p_gu0)), to_chips(p_dn0), to_sibling(b_o),
        _gather_job(shares1, None, sends=[_whole(shares1)])])
    update("gu0", p_gu0, chip, q_gu0)
    update("dn0", p_dn0, chip, q_dn0)
    p_o = pair(b_o, r_o, "attn_w_o")
    grads.update({"attn_b_qkv": _fold_heads(db_qkv, d), "attn_sinks": dsink[:1, :d // HEAD_DIM]})
    shares2, where2 = _pack_rows([grads[name] for name in AFTER_ATTN] + [loss_tile[:1, :1]], d)
    (dx0, dg_mix_pre0), _ = _pre_bwd(dqkv, w_qkv, xs, row(norm_mix_pre[0]), dx1, "attn_qkv_bwd")
    grads["norm_mix_pre"] = jnp.concatenate([dg_mix_pre0, dg_mix_pre1], axis=0)
    shares3, where3 = _pack_rows([grads[name] for name in LAST], d)
    dw_qkv, ((q_o,), (g_shares1,), (g_shares2,), (g_shares3,)) = _wgrad(h0, dqkv, "attn_qkv_wgrad", jobs=[
        to_chips(p_o), _gather_job(shares1, g_shares1, forwards=[_whole(shares1)]),
        _gather_job(shares2, None, sends=[_whole(shares2)]), _gather_job(shares3, None, sends=[_whole(shares3)])])
    update("attn_w_o", p_o, chip, q_o)
    b_qkv_grad = _columns_blocked(_fold_heads(dw_qkv, d))
    (r_qkv,), (g_shares2,), (g_shares3,) = _copies_only([
        to_sibling(b_qkv_grad), _gather_job(shares2, g_shares2, forwards=[_whole(shares2)]),
        _gather_job(shares3, g_shares3, forwards=[_whole(shares3)])], "grads_tail_to_sibling")
    p_qkv = pair(b_qkv_grad, r_qkv, "attn_w_qkv")
    (q_qkv,), = _copies_only([to_chips(p_qkv)], "grads_tail_to_owner_chip")
    update("attn_w_qkv", p_qkv, chip, q_qkv)

    finish(BEFORE_ATTN, 0, g_shares1, where1, "replicated_update_before_attn")
    loss = finish(AFTER_ATTN, 1, g_shares2, where2, "replicated_update_after_attn")[0][0, 0]
    finish(LAST, 0, g_shares3, where3, "replicated_update_last")

    for out in (out_g, out_d, out_m, out_v):
        out["sgu_ln_g"], out["sgu_ln_b"] = out["sgu_ln"][:, 0, :], out["sgu_ln"][:, 1, :]
        out["ffn_w_gate_up"] = jnp.swapaxes(out["ffn_w_gate_up"], 1, 2)

    return (loss, dx0[None], *[out_g[n] for n in WEIGHTS], *[out_d[n] for n in WEIGHTS],
            *[out_m[n] for n in WEIGHTS], *[out_v[n] for n in WEIGHTS])
```

```python
import functools
import math
import operator

import jax
import jax.numpy as jnp
from jax import lax
from jax.experimental import pallas as pl
from jax.experimental.pallas import tpu as pltpu

F32 = jnp.float32
ACT = jnp.bfloat16

EPS = 1e-6
HEAD_DIM = 64
PAIR = 2 * HEAD_DIM
GQA_GROUP = 4
WINDOW = 128
ROPE_THETA = 10000.0
SCALE = HEAD_DIM ** -0.5
MASKED = -1e30
LANES = 128
MXU_WIDTH = 256

ADAM_LR, ADAM_B1, ADAM_B2, ADAM_EPS, ADAM_WD, ADAM_STEP = 0.001, 0.9, 0.999, 1e-08, 0.01, 10

N_DEV = 8
VMEM_LIMIT_BYTES = 56 * 1024 * 1024
MESH = pl.DeviceIdType.MESH
ANY = pl.BlockSpec(memory_space=pl.ANY)
IN_VMEM = pl.BlockSpec(memory_space=pltpu.VMEM)


def _pick(n, candidates):
    for c in candidates:
        if n % c == 0:
            return c
    return n


def _row_tile(t, most=512):
    return _pick(t, (most,))


def _mxu_chunks(n, most=4 * MXU_WIDTH):
    assert n % MXU_WIDTH == 0 and most % MXU_WIDTH == 0
    return [(c, min(most, n - c)) for c in range(0, n, most)]


def _params(*sem):
    return pltpu.CompilerParams(dimension_semantics=sem, vmem_limit_bytes=VMEM_LIMIT_BYTES)


def _full(shape):
    return pl.BlockSpec(shape, lambda *_: (0,) * len(shape))


def _resident(shape):
    return pl.BlockSpec(shape, lambda *_: (0,) * len(shape), pipeline_mode=pl.Buffered(1))


def _rows(tm, n):
    return pl.BlockSpec((tm, n), lambda i: (i, 0))


def _sds(shape, dtype):
    return jax.ShapeDtypeStruct(shape, dtype)


def _place():
    return lax.axis_index("x"), lax.axis_index("y"), lax.axis_index("c")


def _other_chips(x, y):
    return [(1 - x, y), (x, 1 - y), (1 - x, 1 - y)]


class _Job:
    def __init__(self, inputs, out_shape, aliases, emit, n_remote, n_local=0):
        self.inputs, self.out_shape, self.aliases, self.emit = list(inputs), list(out_shape), dict(aliases), emit
        self.n_remote, self.n_local = n_remote, n_local


def _call(body, name, grid, in_specs, out_specs, out_shape, args, sem, scratch=(), jobs=()):
    n_in, n_out, n_scr = len(args), len(out_shape), len(scratch)
    j_in = [a for job in jobs for a in job.inputs]
    j_out = [s for job in jobs for s in job.out_shape]
    aliases, at_in, at_out = {}, n_in, n_out
    for job in jobs:
        aliases.update({at_in + i: at_out + o for i, o in job.aliases.items()})
        at_in, at_out = at_in + len(job.inputs), at_out + len(job.out_shape)
    n_remote = sum(job.n_remote for job in jobs)
    n_local = sum(job.n_local for job in jobs)

    def wrapped(*refs):
        ins, jin = refs[:n_in], refs[n_in:n_in + len(j_in)]
        rest = refs[n_in + len(j_in):]
        outs, jout = rest[:n_out], rest[n_out:n_out + len(j_out)]
        scr = rest[n_out + len(j_out):n_out + len(j_out) + n_scr]
        if not jobs:
            body(*ins, *outs, *scr)
            return
        send, recv, local = rest[n_out + len(j_out) + n_scr:]
        ids = [pl.program_id(a) for a in range(len(grid))]
        first = functools.reduce(operator.and_, [i == 0 for i in ids])
        last = functools.reduce(operator.and_, [i == g - 1 for i, g in zip(ids, grid)])

        def descriptors():
            place, found, i, o, r, l = _place(), [], 0, 0, 0, 0
            for job in jobs:
                for src, dst, to in job.emit(jin[i:i + len(job.inputs)], jout[o:o + len(job.out_shape)], place):
                    if to is None:
                        found.append(pltpu.make_async_copy(src, dst, local.at[l]))
                        l += 1
                    else:
                        found.append(pltpu.make_async_remote_copy(src_ref=src, dst_ref=dst, send_sem=send.at[r], recv_sem=recv.at[r],
                                                                  device_id=to, device_id_type=MESH))
                        r += 1
                i, o = i + len(job.inputs), o + len(job.out_shape)
            return found

        @pl.when(first)
        def _():
            for cp in descriptors():
                cp.start()

        body(*ins, *outs, *scr)

        @pl.when(last)
        def _():
            for cp in descriptors():
                cp.wait()

    sems = [pltpu.SemaphoreType.DMA((n_remote,)), pltpu.SemaphoreType.DMA((n_remote,)),
            pltpu.SemaphoreType.DMA((max(n_local, 1),))] if jobs else []
    res = pl.pallas_call(
        wrapped, name=name, grid=grid,
        in_specs=list(in_specs) + [ANY] * len(j_in), out_specs=list(out_specs) + [ANY] * len(j_out),
        out_shape=list(out_shape) + j_out, scratch_shapes=list(scratch) + sems, input_output_aliases=aliases,
        compiler_params=_params(*(("arbitrary",) * len(grid) if jobs else sem)),
    )(*args, *j_in)
    job_res, at = [], n_out
    for job in jobs:
        job_res.append(list(res[at:at + len(job.out_shape)]))
        at += len(job.out_shape)
    return list(res[:n_out]), job_res


def _copies_only(jobs, name):
    def body(o_ref):
        o_ref[...] = jnp.zeros_like(o_ref)

    return _call(body, name, (1,), [], [_full((8, LANES))], [_sds((8, LANES), F32)], (), ("arbitrary",), jobs=jobs)[1]


def _gather_job(stage, gathered, sends=(), forwards=()):
    shape = _sds((N_DEV,) + stage.shape, stage.dtype)
    inputs = ([stage] if sends else []) + ([gathered] if gathered is not None else [])
    aliases = {len(inputs) - 1: 0} if gathered is not None else {}

    def emit(ins, outs, place):
        x, y, c = place
        sibling, chips, mine, g = (x, y, 1 - c), _other_chips(x, y), 4 * x + 2 * y + c, outs[0]
        found = []
        for r0, nr in sends:
            src, dst = ins[0].at[pl.ds(r0, nr)], g.at[mine, pl.ds(r0, nr)]
            found += [(src, dst, None), (src, dst, sibling)] + [(src, dst, (px, py, c)) for px, py in chips]
        for r0, nr in forwards:
            for px, py in chips:
                block = 4 * px + 2 * py + c
                found.append((ins[-1].at[block, pl.ds(r0, nr)], g.at[block, pl.ds(r0, nr)], sibling))
        return found

    return _Job(inputs, [shape], aliases, emit, 4 * len(sends) + 3 * len(forwards), len(sends))


def _relay_gather_job(stage, gathered, sends=(), relays=(), forwards=(), far=()):
    shape = _sds((N_DEV,) + stage.shape, stage.dtype)
    inputs = ([stage] if sends else []) + ([gathered] if gathered is not None else [])
    aliases = {len(inputs) - 1: 0} if gathered is not None else {}

    def emit(ins, outs, place):
        x, y, c = place
        sibling, beside_x, beside_y, g = (x, y, 1 - c), (1 - x, y, c), (x, 1 - y, c), outs[0]
        slot = lambda dev: 4 * dev[0] + 2 * dev[1] + dev[2]
        found = []
        for r0, nr in sends:
            src, dst = ins[0].at[pl.ds(r0, nr)], g.at[slot((x, y, c)), pl.ds(r0, nr)]
            found += [(src, dst, None), (src, dst, sibling), (src, dst, beside_x), (src, dst, beside_y)]

        def passed_on(block, piece, to):
            return ins[-1].at[block, pl.ds(*piece)], g.at[block, pl.ds(*piece)], to

        for piece, way in relays:
            found.append(passed_on(slot(beside_x), piece, beside_y) if way == 0 else passed_on(slot(beside_y), piece, beside_x))
        for piece in forwards:
            found += [passed_on(slot(beside_x), piece, sibling), passed_on(slot(beside_y), piece, sibling)]
        for piece in far:
            found.append(passed_on(slot((1 - x, 1 - y, c)), piece, sibling))
        return found

    return _Job(inputs, [shape], aliases, emit, 3 * len(sends) + len(relays) + 2 * len(forwards) + len(far), len(sends))


def _scatter_job(arrays, n_slots, route, prev=None, piece=None):
    shapes = [_sds((n_slots,) + v.shape[1:], v.dtype) for v in arrays]
    inputs = list(arrays) + (list(prev) if prev else [])
    aliases = {len(arrays) + i: i for i in range(len(arrays))} if prev else {}

    def emit(ins, outs, place):
        found = []
        for a in range(len(arrays)):
            for s in range(n_slots):
                block, to = route(s, *place)
                if piece is None:
                    found.append((ins[a].at[block], outs[a].at[s], to))
                else:
                    found.append((ins[a].at[block, pl.ds(*piece)], outs[a].at[s, pl.ds(*piece)], to))
        return found

    return _Job(inputs, shapes, aliases, emit, len(arrays) * n_slots)


def _to_sibling(s, x, y, c):
    return 2 * s + (1 - c), (x, y, 1 - c)


def _to_owner_chip(s, x, y, c):
    px, py = _other_chips(x, y)[s]
    return 2 * px + py, (px, py, c)


def _rstd(x):
    return lax.rsqrt(jnp.mean(x * x, axis=-1, keepdims=True) + EPS)


def _rms_bwd(xhat, r, g, dy):
    dxh = dy * g
    return r * (dxh - xhat * jnp.mean(dxh * xhat, axis=-1, keepdims=True))


def _first_half(rows):
    lane = lax.broadcasted_iota(jnp.int32, (rows, PAIR), 1)
    return (lane % HEAD_DIM) < (HEAD_DIM // 2)


def _rot_half(v, first):
    return jnp.where(first, pltpu.roll(v, PAIR - HEAD_DIM // 2, 1), pltpu.roll(v, HEAD_DIM // 2, 1))


def _rope(v, cos, sin, first):
    return v * cos + _rot_half(v, first) * sin


def _rope_t(dv, cos, sin, first):
    return dv * cos + _rot_half(dv * sin, first)


def _dot(a, b):
    return jnp.dot(a, b, preferred_element_type=F32)


def _dot_nt(a, b):
    return lax.dot_general(a, b, (((1,), (1,)), ((), ())), preferred_element_type=F32)


def _dot_tn(a, b):
    return lax.dot_general(a, b, (((0,), (0,)), ((), ())), preferred_element_type=F32)


def _sigmoid(v):
    return 1.0 / (1.0 + jnp.exp(-v))


_GELU_K = math.sqrt(2.0 / math.pi)
_GELU_C = 0.044715


def _gelu(v):
    return v * (0.5 * (1.0 + jnp.tanh(_GELU_K * (v + _GELU_C * (v * v * v)))))


def _gelu_grad(v):
    t = jnp.tanh(_GELU_K * (v + _GELU_C * (v * v * v)))
    return 0.5 * (1.0 + t) + 0.5 * v * (1.0 - t * t) * (_GELU_K * (1.0 + 3.0 * _GELU_C * (v * v)))


def _attn_qkv_fwd(x, g, w, b, cos, sin, jobs=()):
    t, d = x.shape
    n = w.shape[0]
    kd = n - d
    assert (kd // 2) % PAIR == 0
    tm = _row_tile(t)
    ch = _pick(d, (512,))

    def body(x_ref, g_ref, w_ref, b_ref, cos_ref, sin_ref, h_ref, q_ref, k_ref, v_ref):
        xv = x_ref[...]
        hb = (xv * _rstd(xv) * g_ref[...]).astype(ACT)
        h_ref[...] = hb
        cosv, sinv = cos_ref[...], sin_ref[...]
        first = _first_half(tm)
        left = _lane_halves()[0]

        def proj(lo, width):
            return _dot_nt(hb, w_ref[lo:lo + width, :]) + b_ref[:, lo:lo + width]

        def twice(ref, s, two_heads):
            swapped = pltpu.roll(two_heads, HEAD_DIM, 1)
            ref[:, 2 * s:2 * s + PAIR] = jnp.where(left, two_heads, swapped).astype(ACT)
            ref[:, 2 * s + PAIR:2 * s + 2 * PAIR] = jnp.where(left, swapped, two_heads).astype(ACT)

        for c in range(0, d, ch):
            y = proj(c, ch)
            for s in range(0, ch, PAIR):
                q_ref[:, c + s:c + s + PAIR] = (_rope(y[:, s:s + PAIR], cosv, sinv, first) * SCALE).astype(ACT)
        y = proj(d, kd)
        for s in range(0, kd // 2, PAIR):
            twice(k_ref, s, _rope(y[:, s:s + PAIR], cosv, sinv, first))
            twice(v_ref, s, y[:, kd // 2 + s:kd // 2 + s + PAIR])

    return _call(
        body, "attn_qkv_fwd", (t // tm,),
        [_rows(tm, d), _full((1, d)), _full((n, d)), _full((1, n)), _rows(tm, PAIR), _rows(tm, PAIR)],
        [_rows(tm, d), _rows(tm, d), _rows(tm, kd), _rows(tm, kd)],
        [_sds((t, d), ACT), _sds((t, d), ACT), _sds((t, kd), ACT), _sds((t, kd), ACT)],
        (x, g, w, b, cos, sin), ("parallel",), jobs=jobs)


STACK = GQA_GROUP * WINDOW


def _band_mask(has_prev):
    row = lax.broadcasted_iota(jnp.int32, (STACK, 2 * WINDOW), 0) % WINDOW
    col = lax.broadcasted_iota(jnp.int32, (STACK, 2 * WINDOW), 1)
    return ((col < WINDOW) & (col > row) & has_prev) | ((col >= WINDOW) & (col - WINDOW <= row))


def _lane_halves():
    lane = lax.broadcasted_iota(jnp.int32, (1, PAIR), 1)
    return lane < HEAD_DIM, lane >= HEAD_DIM


def _stack_heads(ref, h, halves):
    parts = []
    for p in range(2):
        pair = ref[:, (2 * h + p) * PAIR:(2 * h + p + 1) * PAIR]
        parts += [jnp.where(half, pair, jnp.zeros_like(pair)) for half in halves]
    return jnp.concatenate(parts, axis=0)


def _sink_column(sink_ref, h):
    row = lax.broadcasted_iota(jnp.int32, (STACK, 1), 0)
    col = jnp.full((STACK, 1), sink_ref[0, GQA_GROUP * h + GQA_GROUP - 1], F32)
    for j in range(GQA_GROUP - 2, -1, -1):
        col = jnp.where(row < (j + 1) * WINDOW, sink_ref[0, GQA_GROUP * h + j], col)
    return col


def _probs(scores, valid, sink):
    s = jnp.where(valid, scores, MASKED)
    m = jnp.maximum(jnp.max(s, axis=-1, keepdims=True), sink)
    p = jnp.exp(s - m)
    psink = jnp.exp(sink - m)
    inv = 1.0 / (jnp.sum(p, axis=-1, keepdims=True) + psink)
    return p * inv, psink * inv


def _attn_fwd(q, kd_, vd, sinks, jobs=()):
    t, d = q.shape
    kd = kd_.shape[1]
    cur = lambda n: (n, 0)
    prev = lambda n: (jnp.maximum(n - 1, 0), 0)

    def body(sink_ref, q_ref, kc_ref, kp_ref, vc_ref, vp_ref, o_ref):
        valid = _band_mask(pl.program_id(0) > 0)
        halves = _lane_halves()
        heads = range(kd // PAIR)
        hs = [slice(h * PAIR, (h + 1) * PAIR) for h in heads]
        scores = [_dot_nt(_stack_heads(q_ref, h, halves), jnp.concatenate([kp_ref[:, hs[h]], kc_ref[:, hs[h]]], axis=0)) for h in heads]
        probs = [_probs(scores[h], valid, _sink_column(sink_ref, h))[0].astype(ACT) for h in heads]
        for h in heads:
            v2 = jnp.concatenate([vp_ref[:, hs[h]], vc_ref[:, hs[h]]], axis=0)
            vs = jnp.concatenate([jnp.where(half, v2, jnp.zeros_like(v2)) for half in halves], axis=0)
            pr = probs[h]
            for p in range(2):
                both = jnp.concatenate([pr[2 * p * WINDOW:(2 * p + 1) * WINDOW], pr[(2 * p + 1) * WINDOW:(2 * p + 2) * WINDOW]], axis=1)
                o_ref[:, (2 * h + p) * PAIR:(2 * h + p + 1) * PAIR] = _dot(both, vs).astype(ACT)

    kv_c, kv_p = pl.BlockSpec((WINDOW, kd), cur), pl.BlockSpec((WINDOW, kd), prev)
    return _call(
        body, "attn_fwd", (t // WINDOW,),
        [pl.BlockSpec(memory_space=pltpu.SMEM), pl.BlockSpec((WINDOW, d), cur), kv_c, kv_p, kv_c, kv_p],
        [pl.BlockSpec((WINDOW, d), cur)], [_sds((t, d), ACT)],
        (sinks, q, kd_, kd_, vd, vd), ("parallel",), jobs=jobs)


def _attn_bwd(q, kd_, vd, do, sinks, cos, sin, jobs=()):
    t, d = q.shape
    kd = kd_.shape[1]
    nb = t // WINDOW
    n_out = d + kd
    assert (kd // 2) % PAIR == 0
    cur = lambda n: (jnp.minimum(n, nb - 1), 0)
    prev = lambda n: (jnp.maximum(jnp.minimum(n, nb - 1) - 1, 0), 0)
    late = lambda n: (jnp.maximum(n - 1, 0), 0)

    def body(sink_ref, q_ref, kc_ref, kp_ref, vc_ref, vp_ref, do_ref, cosc_ref, sinc_ref, cosp_ref, sinp_ref,
             out_ref, db_ref, dsink_ref, dq_keep, dk_keep, dv_keep):
        n = pl.program_id(0)

        @pl.when(n == 0)
        def _():
            for ref in (db_ref, dsink_ref, dq_keep, dk_keep, dv_keep):
                ref[...] = jnp.zeros_like(ref)

        valid = _band_mask(n > 0) & (n < nb)
        halves = _lane_halves()
        first = _first_half(WINDOW)
        slot = lax.broadcasted_iota(jnp.int32, dsink_ref.shape, 1)
        top = lax.broadcasted_iota(jnp.int32, dsink_ref.shape, 0) == 0
        dsink = jnp.zeros(dsink_ref.shape, F32)

        def emit(cols, done):
            out_ref[:, cols] = done.astype(ACT)
            db_ref[:, cols] += jnp.sum(done.astype(F32), axis=0, keepdims=True)

        heads = range(kd // PAIR)
        hs = [slice(h * PAIR, (h + 1) * PAIR) for h in heads]
        k2 = [jnp.concatenate([kp_ref[:, hs[h]], kc_ref[:, hs[h]]], axis=0) for h in heads]
        v2 = [jnp.concatenate([vp_ref[:, hs[h]], vc_ref[:, hs[h]]], axis=0) for h in heads]
        qs = [_stack_heads(q_ref, h, halves) for h in heads]
        dos = [_stack_heads(do_ref, h, halves) for h in heads]
        scores = [_dot_nt(qs[h], k2[h]) for h in heads]
        dps = [_dot_nt(dos[h], v2[h]) for h in heads]
        prs, dss = [], []
        for h in heads:
            pr, psink = _probs(scores[h], valid, _sink_column(sink_ref, h))
            delta = jnp.sum(pr * dps[h], axis=-1, keepdims=True)
            dss.append((pr * (dps[h] - delta)).astype(ACT))
            prs.append(pr.astype(ACT))
            leak = psink * delta
            for j in range(GQA_GROUP):
                share = jnp.sum(leak[j * WINDOW:(j + 1) * WINDOW], axis=0, keepdims=True)
                dsink = dsink - jnp.where(top & (slot == GQA_GROUP * h + j), share, 0.0)
        dqs = [_dot(dss[h], k2[h]) for h in heads]
        dk2 = [_dot_tn(dss[h], qs[h]) for h in heads]
        dv2 = [_dot_tn(prs[h], dos[h]) for h in heads]
        for h in heads:
            for p in range(2):
                ps = slice((2 * h + p) * PAIR, (2 * h + p + 1) * PAIR)
                dqp = jnp.where(halves[0], dqs[h][2 * p * WINDOW:(2 * p + 1) * WINDOW], dqs[h][(2 * p + 1) * WINDOW:(2 * p + 2) * WINDOW])
                emit(ps, dq_keep[:, ps])
                dq_keep[:, ps] = (_rope_t(dqp, cosc_ref[...], sinc_ref[...], first) * SCALE).astype(ACT)
        dks, dvs = [], []
        for h in heads:
            dks.append(dk_keep[:, hs[h]] + _rope_t(dk2[h][:WINDOW], cosp_ref[...], sinp_ref[...], first))
            dk_keep[:, hs[h]] = _rope_t(dk2[h][WINDOW:], cosc_ref[...], sinc_ref[...], first)
            dvs.append(dv_keep[:, hs[h]] + dv2[h][:WINDOW])
            dv_keep[:, hs[h]] = dv2[h][WINDOW:]
        both = lambda v: v + pltpu.roll(v, HEAD_DIM, 1)
        for h in range(0, len(heads), 2):
            at = h // 2 * PAIR
            emit(slice(d + at, d + at + PAIR), jnp.where(halves[0], both(dks[h]), both(dks[h + 1])))
            emit(slice(d + kd // 2 + at, d + kd // 2 + at + PAIR), jnp.where(halves[0], both(dvs[h]), both(dvs[h + 1])))
        dsink_ref[...] += dsink

    kv_c, kv_p = pl.BlockSpec((WINDOW, kd), cur), pl.BlockSpec((WINDOW, kd), prev)
    tab_c, tab_p = pl.BlockSpec((WINDOW, PAIR), cur), pl.BlockSpec((WINDOW, PAIR), prev)
    return _call(
        body, "attn_bwd", (nb + 1,),
        [pl.BlockSpec(memory_space=pltpu.SMEM), pl.BlockSpec((WINDOW, d), cur), kv_c, kv_p, kv_c, kv_p,
         pl.BlockSpec((WINDOW, d), cur), tab_c, tab_c, tab_p, tab_p],
        [pl.BlockSpec((WINDOW, n_out), late), _full((1, n_out)), _full((8, LANES))],
        [_sds((t, n_out), ACT), _sds((1, n_out), F32), _sds((8, LANES), F32)],
        (sinks, q, kd_, kd_, vd, vd, do, cos, sin, cos, sin), ("arbitrary",),
        scratch=[pltpu.VMEM((WINDOW, d), ACT), pltpu.VMEM((WINDOW, kd), F32), pltpu.VMEM((WINDOW, kd), F32)], jobs=jobs)


def _proj_res(a, w, b, g, x, name, target=None, jobs=()):
    t = a.shape[0]
    k, d = w.shape
    tm = _row_tile(t)
    has_bias = b is not None

    def body(*refs):
        a_ref, w_ref = refs[:2]
        b_ref = refs[2] if has_bias else None
        g_ref, x_ref, m_ref, xo_ref = refs[2 + has_bias:]
        m = _dot(a_ref[...], w_ref[...])
        if has_bias:
            m = m + b_ref[...]
        m_ref[...] = m.astype(ACT)
        xo_ref[...] = x_ref[...] + (m * _rstd(m)) * g_ref[...]

    def body_with_loss(a_ref, w_ref, g_ref, x_ref, t_ref, m_ref, dy_ref, loss_ref):
        @pl.when(pl.program_id(0) == 0)
        def _():
            loss_ref[...] = jnp.zeros_like(loss_ref)

        body(a_ref, w_ref, g_ref, x_ref, m_ref, dy_ref)
        err = dy_ref[...] - t_ref[...]
        dy_ref[...] = err * (1.0 / d)
        loss_ref[...] += 0.5 * jnp.sum(jnp.mean(err * err, axis=-1, keepdims=True), axis=0, keepdims=True)

    ins = [a, w] + ([b] if has_bias else []) + [g, x]
    specs = [_rows(tm, k), _full((k, d))] + ([_full((1, d))] if has_bias else []) + [_full((1, d)), _rows(tm, d)]
    if target is not None:
        return _call(body_with_loss, name, (t // tm,), specs + [_rows(tm, d)], [_rows(tm, d), _rows(tm, d), _full((8, LANES))],
                     [_sds((t, d), ACT), _sds((t, d), F32), _sds((8, LANES), F32)], ins + [target], ("arbitrary",), jobs=jobs)
    return _call(body, name, (t // tm,), specs, [_rows(tm, d), _rows(tm, d)], [_sds((t, d), ACT), _sds((t, d), F32)], ins,
                 ("parallel",), jobs=jobs)


def _post_bwd(dres, m, g, w, mode, name, jobs=()):
    t, d = dres.shape
    k = w.shape[0]
    tm = _row_tile(t)
    kc = _pick(k, (1408, 1024, 512))

    def body(*refs):
        d_ref, m_ref, g_ref, w_ref = refs[:4]
        outs = refs[4:]
        dm_ref, da_ref, dg_ref = outs[:3]
        i = pl.program_id(0)

        @pl.when(i == 0)
        def _():
            dg_ref[...] = jnp.zeros_like(dg_ref)
            if mode == "attn":
                outs[3][...] = jnp.zeros_like(outs[3])

        dv, mv = d_ref[...], m_ref[...].astype(F32)
        r = _rstd(mv)
        mh = mv * r
        dg_ref[...] += jnp.sum(dv * mh, axis=0, keepdims=True)
        dm = _rms_bwd(mh, r, g_ref[...], dv)
        dmb = dm.astype(ACT)
        dm_ref[...] = dmb
        if mode == "attn":
            outs[3][...] += jnp.sum(dm, axis=0, keepdims=True)
        for c in range(0, k, kc):
            da = _dot_nt(dmb, w_ref[c:c + kc, :])
            da_ref[:, c:c + kc] = da.astype(ACT)

    ins = [dres, m, g, w]
    specs = [_rows(tm, d), _rows(tm, d), _full((1, d)), _full((k, d))]
    out_specs = [_rows(tm, d), _rows(tm, k), _full((1, d))]
    out_shape = [_sds((t, d), ACT), _sds((t, k), ACT), _sds((1, d), F32)]
    if mode == "attn":
        out_specs.append(_full((1, d)))
        out_shape.append(_sds((1, d), F32))
    return _call(body, name, (t // tm,), specs, out_specs, out_shape, ins, ("arbitrary",), jobs=jobs)


def _pre_bwd(dy, w, x, g, dres, name, transposed=False, jobs=()):
    t, d = x.shape
    tm = _row_tile(t)

    def body(dy_ref, w_ref, x_ref, g_ref, dres_ref, dx_ref, dg_ref):
        @pl.when(pl.program_id(0) == 0)
        def _():
            dg_ref[...] = jnp.zeros_like(dg_ref)

        dh = jnp.zeros((tm, d), F32)
        if transposed:
            dh = dh + _dot(dy_ref[...], w_ref[...])
        elif w.ndim == 3:
            c = w.shape[2]
            for j in range(w.shape[0]):
                dh = dh + _dot_nt(dy_ref[:, j * c:(j + 1) * c], w_ref[j])
        else:
            n = w.shape[1]
            nc = _pick(n, (1408, 1024, 512))
            for c in range(0, n, nc):
                dh = dh + _dot_nt(dy_ref[:, c:c + nc], w_ref[:, c:c + nc])
        xv = x_ref[...]
        r = _rstd(xv)
        xh = xv * r
        dg_ref[...] += jnp.sum(dh * xh, axis=0, keepdims=True)
        dx_ref[...] = dres_ref[...] + _rms_bwd(xh, r, g_ref[...], dh)

    return _call(
        body, name, (t // tm,),
        [_rows(tm, dy.shape[1]), _full(w.shape), _rows(tm, d), _full((1, d)), _rows(tm, d)],
        [_rows(tm, d), _full((1, d))], [_sds((t, d), F32), _sds((1, d), F32)],
        (dy, w, x, g, dres), ("arbitrary",), jobs=jobs)


def _ffn_bwd(dres, f, g_post, w_dn, gu, w_gu, x, g_pre, name, jobs=()):
    t, d = dres.shape
    n = w_dn.shape[0]
    tm = _row_tile(t, 256)

    def body(d_ref, f_ref, gp_ref, wd_ref, gu_ref, wu_ref, x_ref, g_ref, df_ref, dgu_ref, dx_ref, dgp_ref, dg_ref):
        @pl.when(pl.program_id(0) == 0)
        def _():
            dgp_ref[...] = jnp.zeros_like(dgp_ref)
            dg_ref[...] = jnp.zeros_like(dg_ref)

        dv, fv = d_ref[...], f_ref[...].astype(F32)
        r = _rstd(fv)
        fh = fv * r
        dgp_ref[...] += jnp.sum(dv * fh, axis=0, keepdims=True)
        dfb = _rms_bwd(fh, r, gp_ref[...], dv).astype(ACT)
        df_ref[...] = dfb
        for c, size in _mxu_chunks(n):
            da = _dot_nt(dfb, wd_ref[c:c + size, :]).astype(ACT)
            gate, up = gu_ref[:, c:c + size], gu_ref[:, n + c:n + c + size]
            sg = _sigmoid(gate.astype(F32)).astype(ACT)
            gs = gate * sg
            dgu_ref[:, c:c + size] = da * up * (sg + gs - gs * sg)
            dgu_ref[:, n + c:n + c + size] = da * gs
        dh = _dot(dgu_ref[...], wu_ref[...])
        xv = x_ref[...]
        rx = _rstd(xv)
        xh = xv * rx
        dg_ref[...] += jnp.sum(dh * xh, axis=0, keepdims=True)
        dx_ref[...] = dv + _rms_bwd(xh, rx, g_ref[...], dh)

    return _call(
        body, name, (t // tm,),
        [_rows(tm, d), _rows(tm, d), _full((1, d)), _resident(w_dn.shape), _rows(tm, 2 * n), _resident(w_gu.shape), _rows(tm, d),
         _full((1, d))],
        [_rows(tm, d), _rows(tm, 2 * n), _rows(tm, d), _full((1, d)), _full((1, d))],
        [_sds((t, d), ACT), _sds((t, 2 * n), ACT), _sds((t, d), F32), _sds((1, d), F32), _sds((1, d), F32)],
        (dres, f, g_post, w_dn, gu, w_gu, x, g_pre), ("arbitrary",), jobs=jobs)


def _wgrad(a, b, name, out_dtype=F32, out_block=None, transposed=False, jobs=()):
    t = a.shape[0]
    tt = _pick(t, (1024,))
    steps = t // tt
    tk = _pick(a.shape[1], (1024, 1408))
    k, n = a.shape[1], b.shape[1]
    tn = _pick(n, (1024, 1408))
    per = 0
    if transposed:
        out_spec, out_shape, acc_shape = pl.BlockSpec((tn, tk), lambda i, j, s: (j, i)), _sds((n, k), out_dtype), (tn, tk)
    elif out_block is None:
        out_spec, out_shape, acc_shape = pl.BlockSpec((tk, tn), lambda i, j, s: (i, j)), _sds((k, n), out_dtype), (tk, tn)
    else:
        per = tn // out_block
        out_spec = pl.BlockSpec((per, tk, out_block), lambda i, j, s: (j, i, 0))
        out_shape, acc_shape = _sds((n // out_block, k, out_block), out_dtype), (tk, tn)

    def body(a_ref, b_ref, o_ref, acc_ref):
        s = pl.program_id(2)

        @pl.when(s == 0)
        def _():
            acc_ref[...] = jnp.zeros_like(acc_ref)

        acc_ref[...] += _dot_tn(b_ref[...], a_ref[...]) if transposed else _dot_tn(a_ref[...], b_ref[...])

        @pl.when(s == steps - 1)
        def _():
            if per >= 1:
                for p in range(per):
                    o_ref[p] = acc_ref[:, p * out_block:(p + 1) * out_block].astype(out_dtype)
            else:
                o_ref[...] = acc_ref[...].astype(out_dtype)

    res, job_res = _call(
        body, name, (k // tk, n // tn, steps),
        [pl.BlockSpec((tt, tk), lambda i, j, s: (s, i)), pl.BlockSpec((tt, tn), lambda i, j, s: (s, j))], [out_spec], [out_shape],
        (a, b), ("parallel", "parallel", "arbitrary"), scratch=[pltpu.VMEM(acc_shape, F32)], jobs=jobs)
    return res[0], job_res


def _ffn_up_fwd(x, g, w, jobs=()):
    t, d = x.shape
    n = w.shape[0] // 2
    tm = _row_tile(t)

    def body(x_ref, g_ref, w_ref, h_ref, gu_ref, a_ref):
        xv = x_ref[...]
        hb = (xv * _rstd(xv) * g_ref[...]).astype(ACT)
        h_ref[...] = hb
        for c, size in _mxu_chunks(n):
            gate = _dot_nt(hb, w_ref[c:c + size, :])
            up = _dot_nt(hb, w_ref[n + c:n + c + size, :])
            gu_ref[:, c:c + size] = gate.astype(ACT)
            gu_ref[:, n + c:n + c + size] = up.astype(ACT)
            a_ref[:, c:c + size] = (gate * _sigmoid(gate) * up).astype(ACT)

    return _call(
        body, "ffn_up_fwd", (t // tm,),
        [_rows(tm, d), _full((1, d)), _full(w.shape)],
        [_rows(tm, d), _rows(tm, 2 * n), _rows(tm, n)],
        [_sds((t, d), ACT), _sds((t, 2 * n), ACT), _sds((t, n), ACT)],
        (x, g, w), ("parallel",), jobs=jobs)


def _causal(ws):
    row = lax.broadcasted_iota(jnp.int32, ws.shape, 0)
    col = lax.broadcasted_iota(jnp.int32, ws.shape, 1)
    return jnp.where(col <= row, ws, 0.0)


def _sgu_ln(v, lg, lb):
    mu = jnp.mean(v, axis=-1, keepdims=True)
    vc = v - mu
    rs = lax.rsqrt(jnp.mean(vc * vc, axis=-1, keepdims=True) + EPS)
    vh = vc * rs
    return vh, rs, vh * lg + lb


def _sgu_fwd(x, g, w, lg, lb, ws, bs_t, jobs=()):
    t, d = x.shape
    nb, _, c = w.shape
    n = nb * c
    groups = d // WINDOW
    tm = _row_tile(t)

    def body(x_ref, g_ref, w_ref, lg_ref, lb_ref, ws_ref, bs_ref, h_ref, z_ref, y_ref, zf_ref):
        xv = x_ref[...]
        hb = (xv * _rstd(xv) * g_ref[...]).astype(ACT)
        h_ref[...] = hb
        for j in range(nb):
            zf_ref[:, j * c:(j + 1) * c] = _dot(hb, w_ref[j])
        z_ref[...] = zf_ref[...].astype(ACT)
        gs = [slice(gi * WINDOW, (gi + 1) * WINDOW) for gi in range(groups)]
        wsg = [_causal(ws_ref[gi]).astype(ACT) for gi in range(groups)]
        for r in range(0, tm, WINDOW):
            u = _gelu(zf_ref[r:r + WINDOW, :d])
            _, _, vn = _sgu_ln(_gelu(zf_ref[r:r + WINDOW, d:]), lg_ref[...], lb_ref[...])
            mixed = [_dot(wsg[gi], vn[:, gs[gi]].astype(ACT)) for gi in range(groups)]
            for gi in range(groups):
                y_ref[r:r + WINDOW, gs[gi]] = (u[:, gs[gi]] * (mixed[gi] + bs_ref[:, gi:gi + 1])).astype(ACT)

    vec = _full((1, d))
    return _call(
        body, "sgu_fwd", (t // tm,),
        [_rows(tm, d), vec, _full((nb, d, c)), vec, vec, _full((groups, WINDOW, WINDOW)), _full((WINDOW, groups))],
        [_rows(tm, d), _rows(tm, n), _rows(tm, d)], [_sds((t, d), ACT), _sds((t, n), ACT), _sds((t, d), ACT)],
        (x, g, w, lg, lb, ws, bs_t), ("parallel",), scratch=[pltpu.VMEM((tm, n), F32)], jobs=jobs)


def _sgu_mix_bwd(z, dy, lg, lb, ws, bs_t, jobs=()):
    t, n = z.shape
    d = n // 2
    groups = d // WINDOW
    tm = _row_tile(t)

    def body(z_ref, dy_ref, lg_ref, lb_ref, ws_ref, bs_ref, dz_ref, dlg_ref, dlb_ref, dws_ref, dbs_ref):
        @pl.when(pl.program_id(0) == 0)
        def _():
            for ref in (dlg_ref, dlb_ref, dws_ref, dbs_ref):
                ref[...] = jnp.zeros_like(ref)

        lane = lax.broadcasted_iota(jnp.int32, (WINDOW, groups), 1)
        gs = [slice(gi * WINDOW, (gi + 1) * WINDOW) for gi in range(groups)]
        wsg = [_causal(ws_ref[gi]).astype(ACT) for gi in range(groups)]
        for c in range(0, tm, WINDOW):
            rows = slice(c, c + WINDOW)
            zu, zv = z_ref[rows, :d].astype(F32), z_ref[rows, d:].astype(F32)
            u = _gelu(zu)
            vh, rs, vn = _sgu_ln(_gelu(zv), lg_ref[...], lb_ref[...])
            dyv = dy_ref[rows, :].astype(F32)
            vng = [vn[:, gs[gi]].astype(ACT) for gi in range(groups)]
            dmix = dyv * u
            dmb = [dmix[:, gs[gi]].astype(ACT) for gi in range(groups)]
            mixed = [_dot(wsg[gi], vng[gi]) for gi in range(groups)]
            dvn_parts = [_dot_tn(wsg[gi], dmb[gi]) for gi in range(groups)]
            dws_parts = [_dot_nt(dmb[gi], vng[gi]) for gi in range(groups)]
            for gi in range(groups):
                dz_ref[rows, gs[gi]] = (dyv[:, gs[gi]] * (mixed[gi] + bs_ref[:, gi:gi + 1]) * _gelu_grad(zu[:, gs[gi]])).astype(ACT)
                dws_ref[:, gs[gi]] += _causal(dws_parts[gi])
                dbs_ref[...] += jnp.where(lane == gi, jnp.sum(dmix[:, gs[gi]], axis=-1, keepdims=True), 0.0)
            dvn = jnp.concatenate(dvn_parts, axis=-1)
            dlg_ref[...] += jnp.sum(dvn * vh, axis=0, keepdims=True)
            dlb_ref[...] += jnp.sum(dvn, axis=0, keepdims=True)
            dvh = dvn * lg_ref[...]
            dv = rs * (dvh - jnp.mean(dvh, axis=-1, keepdims=True) - vh * jnp.mean(dvh * vh, axis=-1, keepdims=True))
            dz_ref[rows, d:] = (dv * _gelu_grad(zv)).astype(ACT)

    vec = _full((1, d))
    return _call(
        body, "sgu_mix_bwd", (t // tm,),
        [_rows(tm, n), _rows(tm, d), vec, vec, _full((groups, WINDOW, WINDOW)), _full((WINDOW, groups))],
        [_rows(tm, n), vec, vec, _full((WINDOW, d)), _full((WINDOW, groups))],
        [_sds((t, n), ACT), _sds((1, d), F32), _sds((1, d), F32), _sds((WINDOW, d), F32), _sds((WINDOW, groups), F32)],
        (z, dy, lg, lb, ws, bs_t), ("arbitrary",), jobs=jobs)


AG_COPIES = 8


def _prepare(sources, dtypes, n_gather, name):
    n = len(sources)
    arrays, where = [], []
    for parts, layer in sources:
        found = []
        for part in parts:
            known = [i for i, v in enumerate(arrays) if v is part]
            if not known:
                arrays.append(part)
            found.append(known[0] if known else len(arrays) - 1)
        where.append((found, layer))
    shapes = [(sum(p.shape[1] for p in parts), parts[0].shape[2]) for parts, _ in sources]

    def body(*refs):
        ins, refs = refs[:len(arrays)], refs[len(arrays):]
        stage, outs = refs[:n], refs[n:n + n_gather]
        send, recv, local_sem = refs[n + n_gather:]
        x, y, c = _place()
        me, sibling, beside_x, beside_y, far = (x, y, c), (x, y, 1 - c), (1 - x, y, c), (x, 1 - y, c), (1 - x, 1 - y, c)
        slot = lambda dev: 4 * dev[0] + 2 * dev[1] + dev[2]
        other = lambda dev: (dev[0], dev[1], 1 - c)
        whole = lambda a: (0, shapes[a][0])
        cuts = [rows // 2 if rows % 32 == 0 else rows for rows, _ in shapes]
        first = lambda a: (0, cuts[a])
        rest = lambda a: (cuts[a], shapes[a][0] - cuts[a])

        def copy(a, k, block, rows, to, src=None):
            dst = outs[a].at[block, pl.ds(*rows)]
            return pltpu.make_async_remote_copy(
                src_ref=dst if src is None else src, dst_ref=dst, send_sem=send.at[a * AG_COPIES + k],
                recv_sem=recv.at[a * AG_COPIES + k], device_id=to, device_id_type=MESH)

        def cast(a):
            found, layer = where[a]
            at = 0
            for i in found:
                rows = arrays[i].shape[1]
                stage[a][at:at + rows, :] = ins[i][layer].astype(dtypes[a])
                at += rows

        for a in range(n_gather):
            cast(a)
        started = []
        for a in range(n_gather):
            own = pltpu.make_async_copy(stage[a], outs[a].at[slot(me)], local_sem.at[a])
            own.start()
            started.append(own)
        sends = []
        for a in range(n_gather):
            sends += [copy(a, k, slot(me), whole(a), to, src=stage[a]) for k, to in enumerate((sibling, beside_x, beside_y))]
        for cp in sends:
            cp.start()
        for a in range(n_gather, n):
            cast(a)
        for a in range(n_gather):
            copy(a, 1, slot(beside_x), whole(a), me).wait_recv()
            copy(a, 2, slot(beside_y), whole(a), me).wait_recv()
            passed = [copy(a, 3, slot(beside_x), first(a), beside_y), copy(a, 5, slot(beside_x), whole(a), sibling),
                      copy(a, 6, slot(beside_y), whole(a), sibling)]
            if rest(a)[1]:
                passed.append(copy(a, 4, slot(beside_y), rest(a), beside_x))
            for cp in passed:
                cp.start()
            sends += passed
        for a in range(n_gather):
            copy(a, 3, slot(far), first(a), me).wait_recv()
            if rest(a)[1]:
                copy(a, 4, slot(far), rest(a), me).wait_recv()
            passed = copy(a, 7, slot(far), whole(a), sibling)
            passed.start()
            sends.append(passed)
        for a in range(n_gather):
            for k, source in ((0, me), (5, beside_x), (6, beside_y), (7, far)):
                copy(a, k, slot(other(source)), whole(a), me).wait_recv()
        for cp in sends:
            cp.wait_send()
        for own in started:
            own.wait()

    res = pl.pallas_call(
        body, name=name,
        in_specs=[IN_VMEM] * len(arrays), out_specs=[IN_VMEM] * n + [ANY] * n_gather,
        out_shape=[_sds(s, dt) for s, dt in zip(shapes, dtypes)]
        + [_sds((N_DEV,) + s, dt) for s, dt in zip(shapes[:n_gather], dtypes)],
        scratch_shapes=[pltpu.SemaphoreType.DMA((n_gather * AG_COPIES,)), pltpu.SemaphoreType.DMA((n_gather * AG_COPIES,)),
                        pltpu.SemaphoreType.DMA((n_gather,))],
        compiler_params=pltpu.CompilerParams(vmem_limit_bytes=VMEM_LIMIT_BYTES),
    )(*arrays)
    return list(res[:n]), list(res[n:])


def _pair_sum(g, r, core, name):
    _, _, rows, cols = g.shape
    tr = _pick(rows, (512, 256, 128))

    def body(core_ref, g_ref, r_ref, p_ref):
        del core_ref
        p_ref[...] = (g_ref[...].astype(F32) + r_ref[...].astype(F32)).astype(p_ref.dtype)

    return pl.pallas_call(
        body, name=name,
        grid_spec=pltpu.PrefetchScalarGridSpec(
            num_scalar_prefetch=1, grid=(4, rows // tr),
            in_specs=[pl.BlockSpec((None, None, tr, cols), lambda q, i, core_ref: (q, core_ref[0], i, 0)),
                      pl.BlockSpec((None, tr, cols), lambda q, i, core_ref: (q, i, 0))],
            out_specs=pl.BlockSpec((None, tr, cols), lambda q, i, core_ref: (q, i, 0))),
        out_shape=_sds((4, rows, cols), g.dtype),
        compiler_params=_params("parallel", "parallel"),
    )(core, g, r)


def _adam(w, g, m, v):
    m2 = ADAM_B1 * m + (1.0 - ADAM_B1) * g
    v2 = ADAM_B2 * v + (1.0 - ADAM_B2) * (g * g)
    m_hat = m2 / (1.0 - ADAM_B1 ** ADAM_STEP)
    v_hat = v2 / (1.0 - ADAM_B2 ** ADAM_STEP)
    return -ADAM_LR * (m_hat / (jnp.sqrt(v_hat) + ADAM_EPS) + ADAM_WD * w), m2, v2


def _shard_update(own, index, received, w, m, v, layer, so_far, name):
    _, rows, cols = w.shape
    n_recv = received.shape[0]
    tr = _pick(rows, (512, 256, 128))

    def body(index_ref, p_ref, q_ref, w_ref, m_ref, v_ref, *rest):
        del index_ref
        g_out, d_out, m_out, v_out = rest[-4:]
        g = p_ref[...].astype(F32)
        for s in range(n_recv):
            g = g + q_ref[s].astype(F32)
        g_out[...] = g
        d_out[...], m_out[...], v_out[...] = _adam(w_ref[...], g, m_ref[...], v_ref[...])

    tile = pl.BlockSpec((None, tr, cols), lambda i, index_ref: (layer, i, 0))
    kept = list(so_far) if so_far is not None else []
    return pl.pallas_call(
        body, name=name,
        grid_spec=pltpu.PrefetchScalarGridSpec(
            num_scalar_prefetch=1, grid=(rows // tr,),
            in_specs=[pl.BlockSpec((None, tr, cols), lambda i, index_ref: (index_ref[0], i, 0)),
                      pl.BlockSpec((n_recv, tr, cols), lambda i, index_ref: (0, i, 0)), tile, tile, tile] + [ANY] * len(kept),
            out_specs=[tile] * 4),
        out_shape=[_sds(w.shape, F32)] * 4,
        input_output_aliases={6 + i: i for i in range(len(kept))},
        compiler_params=_params("parallel"),
    )(index, own, received, w, m, v, *kept)


def _natural_pieces(shape, r, c):
    if tuple(shape[-2:]) == (r, c) and all(n == 1 for n in shape[:-2]):
        return [((0,) * (len(shape) - 2), (0, c))]
    groups, width = shape[1], shape[3]
    assert len(shape) == 4 and shape[0] == 1 and shape[2] == r and groups * width == c, (shape, r, c)
    return [((0, g), (g * width, width)) for g in range(groups)]


def _replicated_update(shares, where, params, name):
    flat = [a for p in params if p is not None for a in p]

    def body(s_ref, *refs):
        ins, outs = refs[:len(flat)], refs[len(flat):]
        total = s_ref[0]
        for dev in range(1, N_DEV):
            total = total + s_ref[dev]
        i_at = o_at = 0
        for ranges, p in zip(where, params):
            chunks = [total[r0:r0 + r, :c] for r0, r, c in ranges]
            g2d = chunks[0] if len(chunks) == 1 else jnp.concatenate(chunks, axis=1)
            if p is None:
                outs[o_at][...] = g2d
                o_at += 1
                continue
            w_ref, m_ref, v_ref = ins[i_at:i_at + 3]
            for idx, (c0, cols) in _natural_pieces(p[0].shape, *g2d.shape):
                at = idx + (slice(None), slice(None))
                g = g2d[:, c0:c0 + cols]
                outs[o_at][at] = g
                outs[o_at + 1][at], outs[o_at + 2][at], outs[o_at + 3][at] = _adam(w_ref[at], g, m_ref[at], v_ref[at])
            i_at, o_at = i_at + 3, o_at + 4

    out_shape = []
    for ranges, p in zip(where, params):
        out_shape += [_sds((ranges[0][1], sum(c for _, _, c in ranges)), F32)] if p is None else [_sds(p[0].shape, F32)] * 4
    res = pl.pallas_call(
        body, name=name, out_shape=out_shape, compiler_params=pltpu.CompilerParams(vmem_limit_bytes=VMEM_LIMIT_BYTES),
    )(shares, *flat)
    out, at = [], 0
    for p in params:
        out.append(list(res[at:at + (1 if p is None else 4)]))
        at += 1 if p is None else 4
    return out


def _rope_tables(t):
    half = HEAD_DIM // 2
    inv_freq = ROPE_THETA ** (-(jnp.arange(half, dtype=F32) * 2.0) / HEAD_DIM)
    ang = jnp.arange(t, dtype=jnp.int32).astype(F32)[:, None] * inv_freq[None, :]
    cos, sin = jnp.cos(ang), jnp.sin(ang)
    return jnp.tile(jnp.concatenate([cos, cos], axis=1), (1, 2)), jnp.tile(jnp.concatenate([-sin, sin], axis=1), (1, 2))


def _rows_full(gathered):
    return gathered.reshape(-1, gathered.shape[-1])


def _rows_blocked(full):
    return full.reshape(N_DEV, full.shape[0] // N_DEV, full.shape[1]).astype(ACT)


def _pack_rows(parts, width):
    rows, where, at = [], [], 0
    for v in parts:
        r, c = v.shape
        n_rows = -(-r // 8) * 8
        chunks = []
        for c0 in range(0, c, width):
            chunk = v[:, c0:c0 + width].astype(F32)
            rows.append(jnp.pad(chunk, ((0, n_rows - r), (0, width - chunk.shape[1]))))
            chunks.append((at, r, chunk.shape[1]))
            at += n_rows
        where.append(chunks)
    return jnp.concatenate(rows, axis=0), where


def _halves(block):
    half = block.shape[0] // 2
    return (0, half), (half, half)


def _whole(block):
    return (0, block.shape[0])


EARLY = ("attn_w_qkv", "sgu_ln", "attn_w_o")
LATE = ("sgu_w_in", "sgu_w_out", "gu0", "gu1", "dn0", "dn1")
SWAPPED = ("attn_w_qkv", "ffn_w_gate_up")
BEFORE_ATTN = ("norm_mix_post", "norm_ffn_pre", "norm_ffn_post", "attn_b_o", "sgu_w_spatial", "sgu_b_spatial")
AFTER_ATTN = ("attn_b_qkv", "attn_sinks")
LAST = ("norm_mix_pre",)
WEIGHTS = ("norm_mix_pre", "norm_mix_post", "norm_ffn_pre", "norm_ffn_post", "attn_w_qkv", "attn_b_qkv", "attn_sinks", "attn_w_o",
           "attn_b_o", "sgu_w_in", "sgu_ln_g", "sgu_ln_b", "sgu_w_spatial", "sgu_b_spatial", "sgu_w_out", "ffn_w_gate_up", "ffn_w_down")


LAYER_OF = {"gu0": ("ffn_w_gate_up", 0), "gu1": ("ffn_w_gate_up", 1), "dn0": ("ffn_w_down", 0), "dn1": ("ffn_w_down", 1)}


def _source(tree, name):
    if name == "sgu_ln":
        return [tree["sgu_ln_g"][None], tree["sgu_ln_b"][None]], 0
    leaf, layer = LAYER_OF.get(name, (name, 0))
    return [tree[leaf]], layer


def kernel(x, norm_mix_pre, norm_mix_post, norm_ffn_pre, norm_ffn_post, attn_w_qkv, attn_b_qkv, attn_sinks, attn_w_o, attn_b_o, sgu_w_in, sgu_ln_g, sgu_ln_b, sgu_w_spatial, sgu_b_spatial, sgu_w_out, ffn_w_gate_up, ffn_w_down, loss_target, m_norm_mix_pre, m_norm_mix_post, m_norm_ffn_pre, m_norm_ffn_post, m_attn_w_qkv, m_attn_b_qkv, m_attn_sinks, m_attn_w_o, m_attn_b_o, m_sgu_w_in, m_sgu_ln_g, m_sgu_ln_b, m_sgu_w_spatial, m_sgu_b_spatial, m_sgu_w_out, m_ffn_w_gate_up, m_ffn_w_down, v_norm_mix_pre, v_norm_mix_post, v_norm_ffn_pre, v_norm_ffn_post, v_attn_w_qkv, v_attn_b_qkv, v_attn_sinks, v_attn_w_o, v_attn_b_o, v_sgu_w_in, v_sgu_ln_g, v_sgu_ln_b, v_sgu_w_spatial, v_sgu_b_spatial, v_sgu_w_out, v_ffn_w_gate_up, v_ffn_w_down):
    w = dict(norm_mix_pre=norm_mix_pre, norm_mix_post=norm_mix_post, norm_ffn_pre=norm_ffn_pre, norm_ffn_post=norm_ffn_post,
             attn_w_qkv=attn_w_qkv, attn_b_qkv=attn_b_qkv, attn_sinks=attn_sinks, attn_w_o=attn_w_o, attn_b_o=attn_b_o,
             sgu_w_in=sgu_w_in, sgu_ln_g=sgu_ln_g, sgu_ln_b=sgu_ln_b, sgu_w_spatial=sgu_w_spatial, sgu_b_spatial=sgu_b_spatial,
             sgu_w_out=sgu_w_out, ffn_w_gate_up=ffn_w_gate_up, ffn_w_down=ffn_w_down)
    mom = dict(norm_mix_pre=m_norm_mix_pre, norm_mix_post=m_norm_mix_post, norm_ffn_pre=m_norm_ffn_pre, norm_ffn_post=m_norm_ffn_post,
               attn_w_qkv=m_attn_w_qkv, attn_b_qkv=m_attn_b_qkv, attn_sinks=m_attn_sinks, attn_w_o=m_attn_w_o, attn_b_o=m_attn_b_o,
               sgu_w_in=m_sgu_w_in, sgu_ln_g=m_sgu_ln_g, sgu_ln_b=m_sgu_ln_b, sgu_w_spatial=m_sgu_w_spatial,
               sgu_b_spatial=m_sgu_b_spatial, sgu_w_out=m_sgu_w_out, ffn_w_gate_up=m_ffn_w_gate_up, ffn_w_down=m_ffn_w_down)
    var = dict(norm_mix_pre=v_norm_mix_pre, norm_mix_post=v_norm_mix_post, norm_ffn_pre=v_norm_ffn_pre, norm_ffn_post=v_norm_ffn_post,
               attn_w_qkv=v_attn_w_qkv, attn_b_qkv=v_attn_b_qkv, attn_sinks=v_attn_sinks, attn_w_o=v_attn_w_o, attn_b_o=v_attn_b_o,
               sgu_w_in=v_sgu_w_in, sgu_ln_g=v_sgu_ln_g, sgu_ln_b=v_sgu_ln_b, sgu_w_spatial=v_sgu_w_spatial,
               sgu_b_spatial=v_sgu_b_spatial, sgu_w_out=v_sgu_w_out, ffn_w_gate_up=v_ffn_w_gate_up, ffn_w_down=v_ffn_w_down)
    w, mom, var = [{**tree, **{n: jnp.swapaxes(tree[n], 1, 2) for n in SWAPPED}} for tree in (w, mom, var)]
    xs, target = x[0], loss_target[0]
    t, d = xs.shape
    row = lambda v: v.reshape(1, -1).astype(F32)
    core = lax.axis_index("c").astype(jnp.int32).reshape(1)
    chip = (2 * lax.axis_index("x") + lax.axis_index("y")).astype(jnp.int32).reshape(1)
    names = EARLY + LATE
    staged, early = _prepare([_source(w, n) for n in names], [F32 if n == "sgu_ln" else ACT for n in names], len(EARLY),
                             "weights_prepare")
    stage = dict(zip(names, staged))
    w_qkv = _rows_full(early[0])
    lg, lb = row(early[1][:, 0, :]), row(early[1][:, 1, :])
    w_o = _rows_full(early[2])
    b_qkv = row(attn_b_qkv)
    sinks = attn_sinks.reshape(1, -1).astype(F32)
    ws = sgu_w_spatial[0].astype(F32)
    bs_t = sgu_b_spatial[0].astype(F32).T
    cos, sin = _rope_tables(t)
    gather = lambda name, so_far, **pieces: _gather_job(stage[name], so_far, **pieces)
    a_half = lambda name: _halves(stage[name])[0]
    b_half = lambda name: _halves(stage[name])[1]
    whole = lambda name: _whole(stage[name])

    def pieces(name, n):
        rows = stage[name].shape[0] // n
        return [(i * rows, rows) for i in range(n)]

    ways = lambda parts: [(piece, i % 2) for i, piece in enumerate(parts)]
    relay = lambda name, so_far, **steps: _relay_gather_job(stage[name], so_far, **steps)
    gu0_parts, gu1_parts = pieces("gu0", 4), pieces("gu1", 4)
    dn0_parts, dn1_parts, in_parts, out_parts = pieces("dn0", 2), pieces("dn1", 2), pieces("sgu_w_in", 2), pieces("sgu_w_out", 2)
    (h0, q, kdup, vdup), ((g_gu0,),) = _attn_qkv_fwd(
        xs, row(norm_mix_pre[0]), w_qkv, b_qkv, cos, sin, jobs=[relay("gu0", None, sends=gu0_parts)])
    (o,), ((g_gu0,), (g_dn0,)) = _attn_fwd(q, kdup, vdup, sinks, jobs=[
        relay("gu0", g_gu0, relays=ways(gu0_parts), forwards=gu0_parts), relay("dn0", None, sends=dn0_parts)])
    (m0, x1), ((g_gu0,), (g_dn0,), (g_in,)) = _proj_res(o, w_o, row(attn_b_o), row(norm_mix_post[0]), xs, "attn_out_fwd", jobs=[
        relay("gu0", g_gu0, far=gu0_parts), relay("dn0", g_dn0, relays=ways(dn0_parts), forwards=dn0_parts),
        relay("sgu_w_in", None, sends=in_parts)])
    w_gu0 = _rows_full(g_gu0)
    (h1, gu0, a0), ((g_dn0,), (g_in,), (g_out,), (g_gu1,)) = _ffn_up_fwd(x1, row(norm_ffn_pre[0]), w_gu0, jobs=[
        relay("dn0", g_dn0, far=dn0_parts), relay("sgu_w_in", g_in, relays=ways(in_parts), forwards=in_parts),
        relay("sgu_w_out", None, sends=out_parts), relay("gu1", None, sends=gu1_parts[:2])])
    w_dn0 = _rows_full(g_dn0)
    (f0, x2), ((g_in,), (g_out,), (g_gu1,)) = _proj_res(a0, w_dn0, None, row(norm_ffn_post[0]), x1, "ffn_down_fwd0", jobs=[
        relay("sgu_w_in", g_in, far=in_parts), relay("sgu_w_out", g_out, relays=ways(out_parts), forwards=out_parts),
        relay("gu1", g_gu1, sends=gu1_parts[2:], relays=ways(gu1_parts[:2]), forwards=gu1_parts[:2])])
    w_in = g_in
    (h2, z, y), ((g_out,), (g_gu1,), (g_dn1,)) = _sgu_fwd(x2, row(norm_mix_pre[1]), w_in, lg, lb, ws, bs_t, jobs=[
        relay("sgu_w_out", g_out, far=out_parts),
        relay("gu1", g_gu1, relays=ways(gu1_parts[2:]), forwards=gu1_parts[2:], far=gu1_parts[:2]),
        relay("dn1", None, sends=dn1_parts)])
    w_out = _rows_full(g_out)
    (m1, x3), ((g_gu1,), (g_dn1,)) = _proj_res(y, w_out, None, row(norm_mix_post[1]), x2, "sgu_out_fwd", jobs=[
        relay("gu1", g_gu1, far=gu1_parts[2:]), relay("dn1", g_dn1, relays=ways(dn1_parts), forwards=dn1_parts)])
    w_gu1 = _rows_full(g_gu1)
    (h3, gu1, a1), ((g_dn1,),) = _ffn_up_fwd(x3, row(norm_ffn_pre[1]), w_gu1, jobs=[relay("dn1", g_dn1, far=dn1_parts)])
    w_dn1 = _rows_full(g_dn1)
    (f1, dx4, loss_tile), _ = _proj_res(a1, w_dn1, None, row(norm_ffn_post[1]), x3, "ffn_down_fwd1", target=target)

    to_sibling = lambda g: _scatter_job([g], 4, _to_sibling)
    pair = lambda g, r, name: _pair_sum(g.reshape((4, 2) + g.shape[1:]), r, core, "pair_sum_" + name)
    to_chips = lambda p, prev=None, piece=None: _scatter_job([p], 3, _to_owner_chip, prev=prev, piece=piece)
    out_g, out_d, out_m, out_v = {}, {}, {}, {}

    trees = [{**tree, "sgu_ln": jnp.stack([tree["sgu_ln_g"], tree["sgu_ln_b"]], axis=1)} for tree in (w, mom, var)]
    so_far = {}

    def update(name, own, index, received):
        leaf, layer = LAYER_OF.get(name, (name, 0))
        so_far[leaf] = _shard_update(own, index, received, *[tree[leaf] for tree in trees], layer, so_far.get(leaf), "update_" + name)
        for out, val in zip((out_g, out_d, out_m, out_v), so_far[leaf]):
            out[leaf] = val

    def finish(names, n_extra, shares, where, name):
        res = _replicated_update(shares, where, [(w[n], mom[n], var[n]) for n in names] + [None] * n_extra, name)
        for n, vals in zip(names, res):
            out_g[n], out_d[n], out_m[n], out_v[n] = vals
        return [vals[0] for vals in res[len(names):]]

    first_half = lambda p: _halves(p[0])[0]
    second_half = lambda p: _halves(p[0])[1]
    (df1, dgu1, dx3, dg_ffn_post1, dg_ffn_pre1), _ = _ffn_bwd(
        dx4, f1, row(norm_ffn_post[1]), w_dn1, gu1, w_gu1, x3, row(norm_ffn_pre[1]), "ffn_bwd1")
    b_gu1, _ = _wgrad(h3, dgu1, "ffn_up_wgrad1", ACT, transposed=True)
    b_gu1 = _rows_blocked(b_gu1)
    dw_dn1, _ = _wgrad(a1, df1, "ffn_down_wgrad1", ACT)
    b_dn1 = _rows_blocked(dw_dn1)
    (dm1, dy, dg_mix_post1), ((r_gu1,), (r_dn1,)) = _post_bwd(
        dx3, m1, row(norm_mix_post[1]), w_out, "sgu", "sgu_out_bwd", jobs=[to_sibling(b_gu1), to_sibling(b_dn1)])
    p_gu1, p_dn1 = pair(b_gu1, r_gu1, "gu1"), pair(b_dn1, r_dn1, "dn1")
    dw_out, _ = _wgrad(y, dm1, "sgu_out_wgrad", ACT)
    b_out = _rows_blocked(dw_out)
    (dz, dlg, dlb, dws, dbs_t), ((q_gu1,), (r_out,)) = _sgu_mix_bwd(z, dy, lg, lb, ws, bs_t, jobs=[
        to_chips(p_gu1, piece=first_half(p_gu1)), to_sibling(b_out)])
    p_out = pair(b_out, r_out, "sgu_w_out")
    b_in, _ = _wgrad(h2, dz, "sgu_in_wgrad", ACT, out_block=stage["sgu_w_in"].shape[1])
    b_ln = jnp.stack([dlg.reshape(N_DEV, -1), dlb.reshape(N_DEV, -1)], axis=1)
    (dx2, dg_mix_pre1), _ = _pre_bwd(dz, w_in, x2, row(norm_mix_pre[1]), dx3, "sgu_in_bwd")
    (df0, dgu0, dx1, dg_ffn_post0, dg_ffn_pre0), ((q_gu1,), (q_dn1,), (q_out,), (r_in,), (r_ln,)) = _ffn_bwd(
        dx2, f0, row(norm_ffn_post[0]), w_dn0, gu0, w_gu0, x1, row(norm_ffn_pre[0]), "ffn_bwd0",
        jobs=[to_chips(p_gu1, prev=[q_gu1], piece=second_half(p_gu1)), to_chips(p_dn1), to_chips(p_out), to_sibling(b_in),
              to_sibling(b_ln)])
    update("gu1", p_gu1, chip, q_gu1)
    update("dn1", p_dn1, chip, q_dn1)
    update("sgu_w_out", p_out, chip, q_out)
    p_in, p_ln = pair(b_in, r_in, "sgu_w_in"), pair(b_ln, r_ln, "sgu_ln")
    b_gu0, ((q_in,), (q_ln,)) = _wgrad(h1, dgu0, "ffn_up_wgrad0", ACT, transposed=True, jobs=[to_chips(p_in), to_chips(p_ln)])
    update("sgu_w_in", p_in, chip, q_in)
    update("sgu_ln", p_ln, chip, q_ln)
    b_gu0 = _rows_blocked(b_gu0)
    (dm0, do, dg_mix_post0, db_o), ((r_gu0,),) = _post_bwd(
        dx1, m0, row(norm_mix_post[0]), w_o, "attn", "attn_out_bwd", jobs=[to_sibling(b_gu0)])
    p_gu0 = pair(b_gu0, r_gu0, "gu0")
    dw_dn0, ((q_gu0,),) = _wgrad(a0, df0, "ffn_down_wgrad0", ACT, jobs=[to_chips(p_gu0, piece=first_half(p_gu0))])
    b_dn0 = _rows_blocked(dw_dn0)
    dw_o, ((r_dn0,),) = _wgrad(o, dm0, "attn_out_wgrad", ACT, jobs=[to_sibling(b_dn0)])
    p_dn0 = pair(b_dn0, r_dn0, "dn0")
    b_o = _rows_blocked(dw_o)
    grads = {
        "norm_mix_post": jnp.concatenate([dg_mix_post0, dg_mix_post1], axis=0),
        "norm_ffn_pre": jnp.concatenate([dg_ffn_pre0, dg_ffn_pre1], axis=0),
        "norm_ffn_post": jnp.concatenate([dg_ffn_post0, dg_ffn_post1], axis=0),
        "attn_b_o": db_o,
        "sgu_w_spatial": dws,
        "sgu_b_spatial": dbs_t.T,
    }
    shares1, where1 = _pack_rows([grads[name] for name in BEFORE_ATTN], d)
    (dqkv, db_qkv, dsink), ((q_gu0,), (q_dn0,), (r_o,), (g_shares1,)) = _attn_bwd(q, kdup, vdup, do, sinks, cos, sin, jobs=[
        to_chips(p_gu0, prev=[q_gu0], piece=second_half(p_gu0)), to_chips(p_dn0), to_sibling(b_o),
        _gather_job(shares1, None, sends=[_whole(shares1)])])
    update("gu0", p_gu0, chip, q_gu0)
    update("dn0", p_dn0, chip, q_dn0)
    p_o = pair(b_o, r_o, "attn_w_o")
    grads.update({"attn_b_qkv": db_qkv, "attn_sinks": dsink[:1, :d // HEAD_DIM]})
    shares2, where2 = _pack_rows([grads[name] for name in AFTER_ATTN] + [loss_tile[:1, :1]], d)
    (dx0, dg_mix_pre0), _ = _pre_bwd(dqkv, w_qkv, xs, row(norm_mix_pre[0]), dx1, "attn_qkv_bwd", True)
    grads["norm_mix_pre"] = jnp.concatenate([dg_mix_pre0, dg_mix_pre1], axis=0)
    shares3, where3 = _pack_rows([grads[name] for name in LAST], d)
    dw_qkv, ((q_o,), (g_shares1,), (g_shares2,), (g_shares3,)) = _wgrad(h0, dqkv, "attn_qkv_wgrad", ACT, transposed=True, jobs=[
        to_chips(p_o), _gather_job(shares1, g_shares1, forwards=[_whole(shares1)]),
        _gather_job(shares2, None, sends=[_whole(shares2)]), _gather_job(shares3, None, sends=[_whole(shares3)])])
    update("attn_w_o", p_o, chip, q_o)
    b_qkv_grad = _rows_blocked(dw_qkv)
    (r_qkv,), (g_shares2,), (g_shares3,) = _copies_only([
        to_sibling(b_qkv_grad), _gather_job(shares2, g_shares2, forwards=[_whole(shares2)]),
        _gather_job(shares3, g_shares3, forwards=[_whole(shares3)])], "grads_tail_to_sibling")
    p_qkv = pair(b_qkv_grad, r_qkv, "attn_w_qkv")
    (q_qkv,), = _copies_only([to_chips(p_qkv)], "grads_tail_to_owner_chip")
    update("attn_w_qkv", p_qkv, chip, q_qkv)

    finish(BEFORE_ATTN, 0, g_shares1, where1, "replicated_update_before_attn")
    loss = finish(AFTER_ATTN, 1, g_shares2, where2, "replicated_update_after_attn")[0][0, 0]
    finish(LAST, 0, g_shares3, where3, "replicated_update_last")

    for out in (out_g, out_d, out_m, out_v):
        out["sgu_ln_g"], out["sgu_ln_b"] = out["sgu_ln"][:, 0, :], out["sgu_ln"][:, 1, :]
        for n in SWAPPED:
            out[n] = jnp.swapaxes(out[n], 1, 2)

    return (loss, dx0[None], *[out_g[n] for n in WEIGHTS], *[out_d[n] for n in WEIGHTS],
            *[out_m[n] for n in WEIGHTS], *[out_v[n] for n in WEIGHTS])
```

```python
import functools
import math
import operator

import jax
import jax.numpy as jnp
from jax import lax
from jax.experimental import pallas as pl
from jax.experimental.pallas import tpu as pltpu

F32 = jnp.float32
ACT = jnp.bfloat16

EPS = 1e-6
HEAD_DIM = 64
PAIR = 2 * HEAD_DIM
GQA_GROUP = 4
WINDOW = 128
ROPE_THETA = 10000.0
SCALE = HEAD_DIM ** -0.5
MASKED = -1e30
LANES = 128
MXU_WIDTH = 256

ADAM_LR, ADAM_B1, ADAM_B2, ADAM_EPS, ADAM_WD, ADAM_STEP = 0.001, 0.9, 0.999, 1e-08, 0.01, 10

N_DEV = 8
VMEM_LIMIT_BYTES = 56 * 1024 * 1024
MESH = pl.DeviceIdType.MESH
ANY = pl.BlockSpec(memory_space=pl.ANY)
IN_VMEM = pl.BlockSpec(memory_space=pltpu.VMEM)


def _pick(n, candidates):
    for c in candidates:
        if n % c == 0:
            return c
    return n


def _row_tile(t, most=512):
    return _pick(t, (most,))


def _shard_tile(rows):
    return next((c for c in (512, 256, 176, 128, 96, 64) if rows % c == 0 and rows >= 2 * c), rows)


def _mxu_chunks(n, most=4 * MXU_WIDTH):
    assert n % MXU_WIDTH == 0 and most % MXU_WIDTH == 0
    return [(c, min(most, n - c)) for c in range(0, n, most)]


def _params(*sem):
    return pltpu.CompilerParams(dimension_semantics=sem, vmem_limit_bytes=VMEM_LIMIT_BYTES)


def _full(shape):
    return pl.BlockSpec(shape, lambda *_: (0,) * len(shape))


def _resident(shape):
    return pl.BlockSpec(shape, lambda *_: (0,) * len(shape), pipeline_mode=pl.Buffered(1))


def _rows(tm, n):
    return pl.BlockSpec((tm, n), lambda i: (i, 0))


def _sds(shape, dtype):
    return jax.ShapeDtypeStruct(shape, dtype)


def _place():
    return lax.axis_index("x"), lax.axis_index("y"), lax.axis_index("c")


def _other_chips(x, y):
    return [(1 - x, y), (x, 1 - y), (1 - x, 1 - y)]


class _Job:
    def __init__(self, inputs, out_shape, aliases, emit, n_remote, n_local=0):
        self.inputs, self.out_shape, self.aliases, self.emit = list(inputs), list(out_shape), dict(aliases), emit
        self.n_remote, self.n_local = n_remote, n_local


def _call(body, name, grid, in_specs, out_specs, out_shape, args, sem, scratch=(), jobs=()):
    n_in, n_out, n_scr = len(args), len(out_shape), len(scratch)
    j_in = [a for job in jobs for a in job.inputs]
    j_out = [s for job in jobs for s in job.out_shape]
    aliases, at_in, at_out = {}, n_in, n_out
    for job in jobs:
        aliases.update({at_in + i: at_out + o for i, o in job.aliases.items()})
        at_in, at_out = at_in + len(job.inputs), at_out + len(job.out_shape)
    n_remote = sum(job.n_remote for job in jobs)
    n_local = sum(job.n_local for job in jobs)

    def wrapped(*refs):
        ins, jin = refs[:n_in], refs[n_in:n_in + len(j_in)]
        rest = refs[n_in + len(j_in):]
        outs, jout = rest[:n_out], rest[n_out:n_out + len(j_out)]
        scr = rest[n_out + len(j_out):n_out + len(j_out) + n_scr]
        if not jobs:
            body(*ins, *outs, *scr)
            return
        send, recv, local = rest[n_out + len(j_out) + n_scr:]
        ids = [pl.program_id(a) for a in range(len(grid))]
        first = functools.reduce(operator.and_, [i == 0 for i in ids])
        last = functools.reduce(operator.and_, [i == g - 1 for i, g in zip(ids, grid)])

        def descriptors():
            place, found, i, o, r, l = _place(), [], 0, 0, 0, 0
            for job in jobs:
                for src, dst, to in job.emit(jin[i:i + len(job.inputs)], jout[o:o + len(job.out_shape)], place):
                    if to is None:
                        found.append(pltpu.make_async_copy(src, dst, local.at[l]))
                        l += 1
                    else:
                        found.append(pltpu.make_async_remote_copy(src_ref=src, dst_ref=dst, send_sem=send.at[r], recv_sem=recv.at[r],
                                                                  device_id=to, device_id_type=MESH))
                        r += 1
                i, o = i + len(job.inputs), o + len(job.out_shape)
            return found

        @pl.when(first)
        def _():
            for cp in descriptors():
                cp.start()

        body(*ins, *outs, *scr)

        @pl.when(last)
        def _():
            for cp in descriptors():
                cp.wait()

    sems = [pltpu.SemaphoreType.DMA((n_remote,)), pltpu.SemaphoreType.DMA((n_remote,)),
            pltpu.SemaphoreType.DMA((max(n_local, 1),))] if jobs else []
    res = pl.pallas_call(
        wrapped, name=name, grid=grid,
        in_specs=list(in_specs) + [ANY] * len(j_in), out_specs=list(out_specs) + [ANY] * len(j_out),
        out_shape=list(out_shape) + j_out, scratch_shapes=list(scratch) + sems, input_output_aliases=aliases,
        compiler_params=_params(*(("arbitrary",) * len(grid) if jobs else sem)),
    )(*args, *j_in)
    job_res, at = [], n_out
    for job in jobs:
        job_res.append(list(res[at:at + len(job.out_shape)]))
        at += len(job.out_shape)
    return list(res[:n_out]), job_res


def _copies_only(jobs, name):
    def body(o_ref):
        o_ref[...] = jnp.zeros_like(o_ref)

    return _call(body, name, (1,), [], [_full((8, LANES))], [_sds((8, LANES), F32)], (), ("arbitrary",), jobs=jobs)[1]


def _gather_job(stage, gathered, sends=(), forwards=()):
    shape = _sds((N_DEV,) + stage.shape, stage.dtype)
    inputs = ([stage] if sends else []) + ([gathered] if gathered is not None else [])
    aliases = {len(inputs) - 1: 0} if gathered is not None else {}

    def emit(ins, outs, place):
        x, y, c = place
        sibling, chips, mine, g = (x, y, 1 - c), _other_chips(x, y), 4 * x + 2 * y + c, outs[0]
        found = []
        for r0, nr in sends:
            src, dst = ins[0].at[pl.ds(r0, nr)], g.at[mine, pl.ds(r0, nr)]
            found += [(src, dst, None), (src, dst, sibling)] + [(src, dst, (px, py, c)) for px, py in chips]
        for r0, nr in forwards:
            for px, py in chips:
                block = 4 * px + 2 * py + c
                found.append((ins[-1].at[block, pl.ds(r0, nr)], g.at[block, pl.ds(r0, nr)], sibling))
        return found

    return _Job(inputs, [shape], aliases, emit, 4 * len(sends) + 3 * len(forwards), len(sends))


def _relay_gather_job(stage, gathered, sends=(), relays=(), forwards=(), far=()):
    shape = _sds((N_DEV,) + stage.shape, stage.dtype)
    inputs = ([stage] if sends else []) + ([gathered] if gathered is not None else [])
    aliases = {len(inputs) - 1: 0} if gathered is not None else {}

    def emit(ins, outs, place):
        x, y, c = place
        sibling, beside_x, beside_y, g = (x, y, 1 - c), (1 - x, y, c), (x, 1 - y, c), outs[0]
        slot = lambda dev: 4 * dev[0] + 2 * dev[1] + dev[2]
        found = []
        for r0, nr in sends:
            src, dst = ins[0].at[pl.ds(r0, nr)], g.at[slot((x, y, c)), pl.ds(r0, nr)]
            found += [(src, dst, None), (src, dst, sibling), (src, dst, beside_x), (src, dst, beside_y)]

        def passed_on(block, piece, to):
            return ins[-1].at[block, pl.ds(*piece)], g.at[block, pl.ds(*piece)], to

        for piece, way in relays:
            found.append(passed_on(slot(beside_x), piece, beside_y) if way == 0 else passed_on(slot(beside_y), piece, beside_x))
        for piece in forwards:
            found += [passed_on(slot(beside_x), piece, sibling), passed_on(slot(beside_y), piece, sibling)]
        for piece in far:
            found.append(passed_on(slot((1 - x, 1 - y, c)), piece, sibling))
        return found

    return _Job(inputs, [shape], aliases, emit, 3 * len(sends) + len(relays) + 2 * len(forwards) + len(far), len(sends))


def _scatter_job(arrays, n_slots, route, prev=None, piece=None):
    shapes = [_sds((n_slots,) + v.shape[1:], v.dtype) for v in arrays]
    inputs = list(arrays) + (list(prev) if prev else [])
    aliases = {len(arrays) + i: i for i in range(len(arrays))} if prev else {}

    def emit(ins, outs, place):
        found = []
        for a in range(len(arrays)):
            for s in range(n_slots):
                block, to = route(s, *place)
                if piece is None:
                    found.append((ins[a].at[block], outs[a].at[s], to))
                else:
                    found.append((ins[a].at[block, pl.ds(*piece)], outs[a].at[s, pl.ds(*piece)], to))
        return found

    return _Job(inputs, shapes, aliases, emit, len(arrays) * n_slots)


def _to_sibling(s, x, y, c):
    return 2 * s + (1 - c), (x, y, 1 - c)


def _to_owner_chip(s, x, y, c):
    px, py = _other_chips(x, y)[s]
    return 2 * px + py, (px, py, c)


def _rstd(x):
    return lax.rsqrt(jnp.mean(x * x, axis=-1, keepdims=True) + EPS)


def _rms_bwd(xhat, r, g, dy):
    dxh = dy * g
    return r * (dxh - xhat * jnp.mean(dxh * xhat, axis=-1, keepdims=True))


def _first_half(rows):
    lane = lax.broadcasted_iota(jnp.int32, (rows, PAIR), 1)
    return (lane % HEAD_DIM) < (HEAD_DIM // 2)


def _rot_half(v, first):
    return jnp.where(first, pltpu.roll(v, PAIR - HEAD_DIM // 2, 1), pltpu.roll(v, HEAD_DIM // 2, 1))


def _rope(v, cos, sin, first):
    return v * cos + _rot_half(v, first) * sin


def _rope_t(dv, cos, sin, first):
    return dv * cos + _rot_half(dv * sin, first)


def _dot(a, b):
    return jnp.dot(a, b, preferred_element_type=F32)


def _dot_nt(a, b):
    return lax.dot_general(a, b, (((1,), (1,)), ((), ())), preferred_element_type=F32)


def _dot_tn(a, b):
    return lax.dot_general(a, b, (((0,), (0,)), ((), ())), preferred_element_type=F32)


def _sigmoid(v):
    return 1.0 / (1.0 + jnp.exp(-v))


_GELU_K = math.sqrt(2.0 / math.pi)
_GELU_C = 0.044715


def _gelu(v):
    return v * (0.5 * (1.0 + jnp.tanh(_GELU_K * (v + _GELU_C * (v * v * v)))))


def _gelu_grad(v):
    t = jnp.tanh(_GELU_K * (v + _GELU_C * (v * v * v)))
    return 0.5 * (1.0 + t) + 0.5 * v * (1.0 - t * t) * (_GELU_K * (1.0 + 3.0 * _GELU_C * (v * v)))


def _attn_qkv_fwd(x, g, w, b, cos, sin, jobs=()):
    t, d = x.shape
    n = w.shape[0]
    kd = n - d
    assert (kd // 2) % PAIR == 0
    tm = _row_tile(t)
    ch = _pick(d, (512,))

    def body(x_ref, g_ref, w_ref, b_ref, cos_ref, sin_ref, h_ref, q_ref, k_ref, v_ref):
        xv = x_ref[...]
        hb = (xv * _rstd(xv) * g_ref[...]).astype(ACT)
        h_ref[...] = hb
        cosv, sinv = cos_ref[...], sin_ref[...]
        first = _first_half(tm)
        left = _lane_halves()[0]

        def proj(lo, width):
            return _dot_nt(hb, w_ref[lo:lo + width, :]) + b_ref[:, lo:lo + width]

        def twice(ref, s, two_heads):
            swapped = pltpu.roll(two_heads, HEAD_DIM, 1)
            ref[:, 2 * s:2 * s + PAIR] = jnp.where(left, two_heads, swapped).astype(ACT)
            ref[:, 2 * s + PAIR:2 * s + 2 * PAIR] = jnp.where(left, swapped, two_heads).astype(ACT)

        for c in range(0, d, ch):
            y = proj(c, ch)
            for s in range(0, ch, PAIR):
                q_ref[:, c + s:c + s + PAIR] = (_rope(y[:, s:s + PAIR], cosv, sinv, first) * SCALE).astype(ACT)
        y = proj(d, kd)
        for s in range(0, kd // 2, PAIR):
            twice(k_ref, s, _rope(y[:, s:s + PAIR], cosv, sinv, first))
            twice(v_ref, s, y[:, kd // 2 + s:kd // 2 + s + PAIR])

    return _call(
        body, "attn_qkv_fwd", (t // tm,),
        [_rows(tm, d), _full((1, d)), _full((n, d)), _full((1, n)), _rows(tm, PAIR), _rows(tm, PAIR)],
        [_rows(tm, d), _rows(tm, d), _rows(tm, kd), _rows(tm, kd)],
        [_sds((t, d), ACT), _sds((t, d), ACT), _sds((t, kd), ACT), _sds((t, kd), ACT)],
        (x, g, w, b, cos, sin), ("parallel",), jobs=jobs)


STACK = GQA_GROUP * WINDOW


def _band_mask(has_prev):
    row = lax.broadcasted_iota(jnp.int32, (STACK, 2 * WINDOW), 0) % WINDOW
    col = lax.broadcasted_iota(jnp.int32, (STACK, 2 * WINDOW), 1)
    return ((col < WINDOW) & (col > row) & has_prev) | ((col >= WINDOW) & (col - WINDOW <= row))


def _lane_halves():
    lane = lax.broadcasted_iota(jnp.int32, (1, PAIR), 1)
    return lane < HEAD_DIM, lane >= HEAD_DIM


def _stack_heads(ref, h, halves):
    parts = []
    for p in range(2):
        pair = ref[:, (2 * h + p) * PAIR:(2 * h + p + 1) * PAIR]
        parts += [jnp.where(half, pair, jnp.zeros_like(pair)) for half in halves]
    return jnp.concatenate(parts, axis=0)


def _sink_column(sink_ref, h):
    row = lax.broadcasted_iota(jnp.int32, (STACK, 1), 0)
    col = jnp.full((STACK, 1), sink_ref[0, GQA_GROUP * h + GQA_GROUP - 1], F32)
    for j in range(GQA_GROUP - 2, -1, -1):
        col = jnp.where(row < (j + 1) * WINDOW, sink_ref[0, GQA_GROUP * h + j], col)
    return col


def _probs(scores, valid, sink):
    s = jnp.where(valid, scores, MASKED)
    m = jnp.maximum(jnp.max(s, axis=-1, keepdims=True), sink)
    p = jnp.exp(s - m)
    psink = jnp.exp(sink - m)
    inv = 1.0 / (jnp.sum(p, axis=-1, keepdims=True) + psink)
    return p * inv, psink * inv


def _attn_fwd(q, kd_, vd, sinks, jobs=()):
    t, d = q.shape
    kd = kd_.shape[1]
    cur = lambda n: (n, 0)
    prev = lambda n: (jnp.maximum(n - 1, 0), 0)

    def body(sink_ref, q_ref, kc_ref, kp_ref, vc_ref, vp_ref, o_ref):
        valid = _band_mask(pl.program_id(0) > 0)
        halves = _lane_halves()
        heads = range(kd // PAIR)
        hs = [slice(h * PAIR, (h + 1) * PAIR) for h in heads]
        scores = [_dot_nt(_stack_heads(q_ref, h, halves), jnp.concatenate([kp_ref[:, hs[h]], kc_ref[:, hs[h]]], axis=0)) for h in heads]
        probs = [_probs(scores[h], valid, _sink_column(sink_ref, h))[0].astype(ACT) for h in heads]
        for h in heads:
            v2 = jnp.concatenate([vp_ref[:, hs[h]], vc_ref[:, hs[h]]], axis=0)
            vs = jnp.concatenate([jnp.where(half, v2, jnp.zeros_like(v2)) for half in halves], axis=0)
            pr = probs[h]
            for p in range(2):
                both = jnp.concatenate([pr[2 * p * WINDOW:(2 * p + 1) * WINDOW], pr[(2 * p + 1) * WINDOW:(2 * p + 2) * WINDOW]], axis=1)
                o_ref[:, (2 * h + p) * PAIR:(2 * h + p + 1) * PAIR] = _dot(both, vs).astype(ACT)

    kv_c, kv_p = pl.BlockSpec((WINDOW, kd), cur), pl.BlockSpec((WINDOW, kd), prev)
    return _call(
        body, "attn_fwd", (t // WINDOW,),
        [pl.BlockSpec(memory_space=pltpu.SMEM), pl.BlockSpec((WINDOW, d), cur), kv_c, kv_p, kv_c, kv_p],
        [pl.BlockSpec((WINDOW, d), cur)], [_sds((t, d), ACT)],
        (sinks, q, kd_, kd_, vd, vd), ("parallel",), jobs=jobs)


def _attn_bwd(q, kd_, vd, do, sinks, cos, sin, jobs=()):
    t, d = q.shape
    kd = kd_.shape[1]
    nb = t // WINDOW
    n_out = d + kd
    assert (kd // 2) % PAIR == 0
    cur = lambda n: (jnp.minimum(n, nb - 1), 0)
    prev = lambda n: (jnp.maximum(jnp.minimum(n, nb - 1) - 1, 0), 0)
    late = lambda n: (jnp.maximum(n - 1, 0), 0)

    def body(sink_ref, q_ref, kc_ref, kp_ref, vc_ref, vp_ref, do_ref, cosc_ref, sinc_ref, cosp_ref, sinp_ref,
             out_ref, db_ref, dsink_ref, dq_keep, dk_keep, dv_keep):
        n = pl.program_id(0)

        @pl.when(n == 0)
        def _():
            for ref in (db_ref, dsink_ref, dq_keep, dk_keep, dv_keep):
                ref[...] = jnp.zeros_like(ref)

        valid = _band_mask(n > 0) & (n < nb)
        halves = _lane_halves()
        first = _first_half(WINDOW)
        slot = lax.broadcasted_iota(jnp.int32, dsink_ref.shape, 1)
        top = lax.broadcasted_iota(jnp.int32, dsink_ref.shape, 0) == 0
        dsink = jnp.zeros(dsink_ref.shape, F32)

        def emit(cols, done):
            out_ref[:, cols] = done.astype(ACT)
            db_ref[:, cols] += jnp.sum(done.astype(F32), axis=0, keepdims=True)

        heads = range(kd // PAIR)
        hs = [slice(h * PAIR, (h + 1) * PAIR) for h in heads]
        k2 = [jnp.concatenate([kp_ref[:, hs[h]], kc_ref[:, hs[h]]], axis=0) for h in heads]
        v2 = [jnp.concatenate([vp_ref[:, hs[h]], vc_ref[:, hs[h]]], axis=0) for h in heads]
        qs = [_stack_heads(q_ref, h, halves) for h in heads]
        dos = [_stack_heads(do_ref, h, halves) for h in heads]
        scores = [_dot_nt(qs[h], k2[h]) for h in heads]
        dps = [_dot_nt(dos[h], v2[h]) for h in heads]
        prs, dss = [], []
        for h in heads:
            pr, psink = _probs(scores[h], valid, _sink_column(sink_ref, h))
            delta = jnp.sum(pr * dps[h], axis=-1, keepdims=True)
            dss.append((pr * (dps[h] - delta)).astype(ACT))
            prs.append(pr.astype(ACT))
            leak = psink * delta
            for j in range(GQA_GROUP):
                share = jnp.sum(leak[j * WINDOW:(j + 1) * WINDOW], axis=0, keepdims=True)
                dsink = dsink - jnp.where(top & (slot == GQA_GROUP * h + j), share, 0.0)
        dqs = [_dot(dss[h], k2[h]) for h in heads]
        dk2 = [_dot_tn(dss[h], qs[h]) for h in heads]
        dv2 = [_dot_tn(prs[h], dos[h]) for h in heads]
        for h in heads:
            for p in range(2):
                ps = slice((2 * h + p) * PAIR, (2 * h + p + 1) * PAIR)
                dqp = jnp.where(halves[0], dqs[h][2 * p * WINDOW:(2 * p + 1) * WINDOW], dqs[h][(2 * p + 1) * WINDOW:(2 * p + 2) * WINDOW])
                emit(ps, dq_keep[:, ps])
                dq_keep[:, ps] = (_rope_t(dqp, cosc_ref[...], sinc_ref[...], first) * SCALE).astype(ACT)
        dks, dvs = [], []
        for h in heads:
            dks.append(dk_keep[:, hs[h]] + _rope_t(dk2[h][:WINDOW], cosp_ref[...], sinp_ref[...], first))
            dk_keep[:, hs[h]] = _rope_t(dk2[h][WINDOW:], cosc_ref[...], sinc_ref[...], first)
            dvs.append(dv_keep[:, hs[h]] + dv2[h][:WINDOW])
            dv_keep[:, hs[h]] = dv2[h][WINDOW:]
        both = lambda v: v + pltpu.roll(v, HEAD_DIM, 1)
        for h in range(0, len(heads), 2):
            at = h // 2 * PAIR
            emit(slice(d + at, d + at + PAIR), jnp.where(halves[0], both(dks[h]), both(dks[h + 1])))
            emit(slice(d + kd // 2 + at, d + kd // 2 + at + PAIR), jnp.where(halves[0], both(dvs[h]), both(dvs[h + 1])))
        dsink_ref[...] += dsink

    kv_c, kv_p = pl.BlockSpec((WINDOW, kd), cur), pl.BlockSpec((WINDOW, kd), prev)
    tab_c, tab_p = pl.BlockSpec((WINDOW, PAIR), cur), pl.BlockSpec((WINDOW, PAIR), prev)
    return _call(
        body, "attn_bwd", (nb + 1,),
        [pl.BlockSpec(memory_space=pltpu.SMEM), pl.BlockSpec((WINDOW, d), cur), kv_c, kv_p, kv_c, kv_p,
         pl.BlockSpec((WINDOW, d), cur), tab_c, tab_c, tab_p, tab_p],
        [pl.BlockSpec((WINDOW, n_out), late), _full((1, n_out)), _full((8, LANES))],
        [_sds((t, n_out), ACT), _sds((1, n_out), F32), _sds((8, LANES), F32)],
        (sinks, q, kd_, kd_, vd, vd, do, cos, sin, cos, sin), ("arbitrary",),
        scratch=[pltpu.VMEM((WINDOW, d), ACT), pltpu.VMEM((WINDOW, kd), F32), pltpu.VMEM((WINDOW, kd), F32)], jobs=jobs)


def _proj_res(a, w, b, g, x, name, target=None, jobs=()):
    t = a.shape[0]
    k, d = w.shape
    tm = _row_tile(t)
    has_bias = b is not None

    def body(*refs):
        a_ref, w_ref = refs[:2]
        b_ref = refs[2] if has_bias else None
        g_ref, x_ref, m_ref, xo_ref = refs[2 + has_bias:]
        m = _dot(a_ref[...], w_ref[...])
        if has_bias:
            m = m + b_ref[...]
        m_ref[...] = m.astype(ACT)
        xo_ref[...] = x_ref[...] + (m * _rstd(m)) * g_ref[...]

    def body_with_loss(a_ref, w_ref, g_ref, x_ref, t_ref, m_ref, dy_ref, loss_ref):
        @pl.when(pl.program_id(0) == 0)
        def _():
            loss_ref[...] = jnp.zeros_like(loss_ref)

        body(a_ref, w_ref, g_ref, x_ref, m_ref, dy_ref)
        err = dy_ref[...] - t_ref[...]
        dy_ref[...] = err * (1.0 / d)
        loss_ref[...] += 0.5 * jnp.sum(jnp.mean(err * err, axis=-1, keepdims=True), axis=0, keepdims=True)

    ins = [a, w] + ([b] if has_bias else []) + [g, x]
    specs = [_rows(tm, k), _full((k, d))] + ([_full((1, d))] if has_bias else []) + [_full((1, d)), _rows(tm, d)]
    if target is not None:
        return _call(body_with_loss, name, (t // tm,), specs + [_rows(tm, d)], [_rows(tm, d), _rows(tm, d), _full((8, LANES))],
                     [_sds((t, d), ACT), _sds((t, d), F32), _sds((8, LANES), F32)], ins + [target], ("arbitrary",), jobs=jobs)
    return _call(body, name, (t // tm,), specs, [_rows(tm, d), _rows(tm, d)], [_sds((t, d), ACT), _sds((t, d), F32)], ins,
                 ("parallel",), jobs=jobs)


def _post_bwd(dres, m, g, w, mode, name, jobs=()):
    t, d = dres.shape
    k = w.shape[0]
    tm = _row_tile(t)
    kc = _pick(k, (1408, 1024, 512))

    def body(*refs):
        d_ref, m_ref, g_ref, w_ref = refs[:4]
        outs = refs[4:]
        dm_ref, da_ref, dg_ref = outs[:3]
        i = pl.program_id(0)

        @pl.when(i == 0)
        def _():
            dg_ref[...] = jnp.zeros_like(dg_ref)
            if mode == "attn":
                outs[3][...] = jnp.zeros_like(outs[3])

        dv, mv = d_ref[...], m_ref[...].astype(F32)
        r = _rstd(mv)
        mh = mv * r
        dg_ref[...] += jnp.sum(dv * mh, axis=0, keepdims=True)
        dm = _rms_bwd(mh, r, g_ref[...], dv)
        dmb = dm.astype(ACT)
        dm_ref[...] = dmb
        if mode == "attn":
            outs[3][...] += jnp.sum(dm, axis=0, keepdims=True)
        for c in range(0, k, kc):
            da = _dot_nt(dmb, w_ref[c:c + kc, :])
            da_ref[:, c:c + kc] = da.astype(ACT)

    ins = [dres, m, g, w]
    specs = [_rows(tm, d), _rows(tm, d), _full((1, d)), _full((k, d))]
    out_specs = [_rows(tm, d), _rows(tm, k), _full((1, d))]
    out_shape = [_sds((t, d), ACT), _sds((t, k), ACT), _sds((1, d), F32)]
    if mode == "attn":
        out_specs.append(_full((1, d)))
        out_shape.append(_sds((1, d), F32))
    return _call(body, name, (t // tm,), specs, out_specs, out_shape, ins, ("arbitrary",), jobs=jobs)


def _pre_bwd(dy, w, x, g, dres, name, transposed=False, jobs=()):
    t, d = x.shape
    tm = _row_tile(t)

    def body(dy_ref, w_ref, x_ref, g_ref, dres_ref, dx_ref, dg_ref):
        @pl.when(pl.program_id(0) == 0)
        def _():
            dg_ref[...] = jnp.zeros_like(dg_ref)

        dh = jnp.zeros((tm, d), F32)
        if transposed:
            dh = dh + _dot(dy_ref[...], w_ref[...])
        elif w.ndim == 3:
            c = w.shape[2]
            for j in range(w.shape[0]):
                dh = dh + _dot_nt(dy_ref[:, j * c:(j + 1) * c], w_ref[j])
        else:
            n = w.shape[1]
            nc = _pick(n, (1408, 1024, 512))
            for c in range(0, n, nc):
                dh = dh + _dot_nt(dy_ref[:, c:c + nc], w_ref[:, c:c + nc])
        xv = x_ref[...]
        r = _rstd(xv)
        xh = xv * r
        dg_ref[...] += jnp.sum(dh * xh, axis=0, keepdims=True)
        dx_ref[...] = dres_ref[...] + _rms_bwd(xh, r, g_ref[...], dh)

    return _call(
        body, name, (t // tm,),
        [_rows(tm, dy.shape[1]), _full(w.shape), _rows(tm, d), _full((1, d)), _rows(tm, d)],
        [_rows(tm, d), _full((1, d))], [_sds((t, d), F32), _sds((1, d), F32)],
        (dy, w, x, g, dres), ("arbitrary",), jobs=jobs)


def _ffn_bwd(dres, f, g_post, w_dn, gu, w_gu, x, g_pre, name, jobs=()):
    t, d = dres.shape
    n = w_dn.shape[0]
    tm = _row_tile(t, 256)

    def body(d_ref, f_ref, gp_ref, wd_ref, gu_ref, wu_ref, x_ref, g_ref, df_ref, dgu_ref, dx_ref, dgp_ref, dg_ref):
        @pl.when(pl.program_id(0) == 0)
        def _():
            dgp_ref[...] = jnp.zeros_like(dgp_ref)
            dg_ref[...] = jnp.zeros_like(dg_ref)

        dv, fv = d_ref[...], f_ref[...].astype(F32)
        r = _rstd(fv)
        fh = fv * r
        dgp_ref[...] += jnp.sum(dv * fh, axis=0, keepdims=True)
        dfb = _rms_bwd(fh, r, gp_ref[...], dv).astype(ACT)
        df_ref[...] = dfb
        for c, size in _mxu_chunks(n):
            da = _dot_nt(dfb, wd_ref[c:c + size, :]).astype(ACT)
            gate, up = gu_ref[:, c:c + size], gu_ref[:, n + c:n + c + size]
            sg = _sigmoid(gate.astype(F32)).astype(ACT)
            gs = gate * sg
            dgu_ref[:, c:c + size] = da * up * (sg + gs - gs * sg)
            dgu_ref[:, n + c:n + c + size] = da * gs
        dh = _dot(dgu_ref[...], wu_ref[...])
        xv = x_ref[...]
        rx = _rstd(xv)
        xh = xv * rx
        dg_ref[...] += jnp.sum(dh * xh, axis=0, keepdims=True)
        dx_ref[...] = dv + _rms_bwd(xh, rx, g_ref[...], dh)

    return _call(
        body, name, (t // tm,),
        [_rows(tm, d), _rows(tm, d), _full((1, d)), _resident(w_dn.shape), _rows(tm, 2 * n), _resident(w_gu.shape), _rows(tm, d),
         _full((1, d))],
        [_rows(tm, d), _rows(tm, 2 * n), _rows(tm, d), _full((1, d)), _full((1, d))],
        [_sds((t, d), ACT), _sds((t, 2 * n), ACT), _sds((t, d), F32), _sds((1, d), F32), _sds((1, d), F32)],
        (dres, f, g_post, w_dn, gu, w_gu, x, g_pre), ("arbitrary",), jobs=jobs)


def _wgrad(a, b, name, out_dtype=F32, out_block=None, transposed=False, jobs=()):
    t = a.shape[0]
    tt = _pick(t, (1024,))
    steps = t // tt
    tk = _pick(a.shape[1], (1024, 1408))
    k, n = a.shape[1], b.shape[1]
    tn = _pick(n, (1024, 1408))
    per = 0
    if transposed:
        out_spec, out_shape, acc_shape = pl.BlockSpec((tn, tk), lambda i, j, s: (j, i)), _sds((n, k), out_dtype), (tn, tk)
    elif out_block is None:
        out_spec, out_shape, acc_shape = pl.BlockSpec((tk, tn), lambda i, j, s: (i, j)), _sds((k, n), out_dtype), (tk, tn)
    else:
        per = tn // out_block
        out_spec = pl.BlockSpec((per, tk, out_block), lambda i, j, s: (j, i, 0))
        out_shape, acc_shape = _sds((n // out_block, k, out_block), out_dtype), (tk, tn)

    def body(a_ref, b_ref, o_ref, acc_ref):
        s = pl.program_id(2)

        @pl.when(s == 0)
        def _():
            acc_ref[...] = jnp.zeros_like(acc_ref)

        acc_ref[...] += _dot_tn(b_ref[...], a_ref[...]) if transposed else _dot_tn(a_ref[...], b_ref[...])

        @pl.when(s == steps - 1)
        def _():
            if per >= 1:
                for p in range(per):
                    o_ref[p] = acc_ref[:, p * out_block:(p + 1) * out_block].astype(out_dtype)
            else:
                o_ref[...] = acc_ref[...].astype(out_dtype)

    res, job_res = _call(
        body, name, (k // tk, n // tn, steps),
        [pl.BlockSpec((tt, tk), lambda i, j, s: (s, i)), pl.BlockSpec((tt, tn), lambda i, j, s: (s, j))], [out_spec], [out_shape],
        (a, b), ("parallel", "parallel", "arbitrary"), scratch=[pltpu.VMEM(acc_shape, F32)], jobs=jobs)
    return res[0], job_res


def _ffn_up_fwd(x, g, w, jobs=()):
    t, d = x.shape
    n = w.shape[0] // 2
    tm = _row_tile(t)

    def body(x_ref, g_ref, w_ref, h_ref, gu_ref, a_ref):
        xv = x_ref[...]
        hb = (xv * _rstd(xv) * g_ref[...]).astype(ACT)
        h_ref[...] = hb
        for c, size in _mxu_chunks(n):
            gate = _dot_nt(hb, w_ref[c:c + size, :])
            up = _dot_nt(hb, w_ref[n + c:n + c + size, :])
            gu_ref[:, c:c + size] = gate.astype(ACT)
            gu_ref[:, n + c:n + c + size] = up.astype(ACT)
            a_ref[:, c:c + size] = (gate * _sigmoid(gate) * up).astype(ACT)

    return _call(
        body, "ffn_up_fwd", (t // tm,),
        [_rows(tm, d), _full((1, d)), _full(w.shape)],
        [_rows(tm, d), _rows(tm, 2 * n), _rows(tm, n)],
        [_sds((t, d), ACT), _sds((t, 2 * n), ACT), _sds((t, n), ACT)],
        (x, g, w), ("parallel",), jobs=jobs)


def _causal(ws):
    row = lax.broadcasted_iota(jnp.int32, ws.shape, 0)
    col = lax.broadcasted_iota(jnp.int32, ws.shape, 1)
    return jnp.where(col <= row, ws, 0.0)


def _sgu_ln(v, lg, lb):
    mu = jnp.mean(v, axis=-1, keepdims=True)
    vc = v - mu
    rs = lax.rsqrt(jnp.mean(vc * vc, axis=-1, keepdims=True) + EPS)
    vh = vc * rs
    return vh, rs, vh * lg + lb


def _sgu_fwd(x, g, w, lg, lb, ws, bs_t, jobs=()):
    t, d = x.shape
    nb, _, c = w.shape
    n = nb * c
    groups = d // WINDOW
    tm = _row_tile(t)

    def body(x_ref, g_ref, w_ref, lg_ref, lb_ref, ws_ref, bs_ref, h_ref, z_ref, y_ref, zf_ref):
        xv = x_ref[...]
        hb = (xv * _rstd(xv) * g_ref[...]).astype(ACT)
        h_ref[...] = hb
        for j in range(nb):
            zf_ref[:, j * c:(j + 1) * c] = _dot(hb, w_ref[j])
        z_ref[...] = zf_ref[...].astype(ACT)
        gs = [slice(gi * WINDOW, (gi + 1) * WINDOW) for gi in range(groups)]
        wsg = [_causal(ws_ref[gi]).astype(ACT) for gi in range(groups)]
        for r in range(0, tm, WINDOW):
            u = _gelu(zf_ref[r:r + WINDOW, :d])
            _, _, vn = _sgu_ln(_gelu(zf_ref[r:r + WINDOW, d:]), lg_ref[...], lb_ref[...])
            mixed = [_dot(wsg[gi], vn[:, gs[gi]].astype(ACT)) for gi in range(groups)]
            for gi in range(groups):
                y_ref[r:r + WINDOW, gs[gi]] = (u[:, gs[gi]] * (mixed[gi] + bs_ref[:, gi:gi + 1])).astype(ACT)

    vec = _full((1, d))
    return _call(
        body, "sgu_fwd", (t // tm,),
        [_rows(tm, d), vec, _full((nb, d, c)), vec, vec, _full((groups, WINDOW, WINDOW)), _full((WINDOW, groups))],
        [_rows(tm, d), _rows(tm, n), _rows(tm, d)], [_sds((t, d), ACT), _sds((t, n), ACT), _sds((t, d), ACT)],
        (x, g, w, lg, lb, ws, bs_t), ("parallel",), scratch=[pltpu.VMEM((tm, n), F32)], jobs=jobs)


def _sgu_mix_bwd(z, dy, lg, lb, ws, bs_t, jobs=()):
    t, n = z.shape
    d = n // 2
    groups = d // WINDOW
    tm = _row_tile(t)

    def body(z_ref, dy_ref, lg_ref, lb_ref, ws_ref, bs_ref, dz_ref, dlg_ref, dlb_ref, dws_ref, dbs_ref):
        @pl.when(pl.program_id(0) == 0)
        def _():
            for ref in (dlg_ref, dlb_ref, dws_ref, dbs_ref):
                ref[...] = jnp.zeros_like(ref)

        lane = lax.broadcasted_iota(jnp.int32, (WINDOW, groups), 1)
        gs = [slice(gi * WINDOW, (gi + 1) * WINDOW) for gi in range(groups)]
        wsg = [_causal(ws_ref[gi]).astype(ACT) for gi in range(groups)]
        for c in range(0, tm, WINDOW):
            rows = slice(c, c + WINDOW)
            zu, zv = z_ref[rows, :d].astype(F32), z_ref[rows, d:].astype(F32)
            u = _gelu(zu)
            vh, rs, vn = _sgu_ln(_gelu(zv), lg_ref[...], lb_ref[...])
            dyv = dy_ref[rows, :].astype(F32)
            vng = [vn[:, gs[gi]].astype(ACT) for gi in range(groups)]
            dmix = dyv * u
            dmb = [dmix[:, gs[gi]].astype(ACT) for gi in range(groups)]
            mixed = [_dot(wsg[gi], vng[gi]) for gi in range(groups)]
            dvn_parts = [_dot_tn(wsg[gi], dmb[gi]) for gi in range(groups)]
            dws_parts = [_dot_nt(dmb[gi], vng[gi]) for gi in range(groups)]
            for gi in range(groups):
                dz_ref[rows, gs[gi]] = (dyv[:, gs[gi]] * (mixed[gi] + bs_ref[:, gi:gi + 1]) * _gelu_grad(zu[:, gs[gi]])).astype(ACT)
                dws_ref[:, gs[gi]] += _causal(dws_parts[gi])
                dbs_ref[...] += jnp.where(lane == gi, jnp.sum(dmix[:, gs[gi]], axis=-1, keepdims=True), 0.0)
            dvn = jnp.concatenate(dvn_parts, axis=-1)
            dlg_ref[...] += jnp.sum(dvn * vh, axis=0, keepdims=True)
            dlb_ref[...] += jnp.sum(dvn, axis=0, keepdims=True)
            dvh = dvn * lg_ref[...]
            dv = rs * (dvh - jnp.mean(dvh, axis=-1, keepdims=True) - vh * jnp.mean(dvh * vh, axis=-1, keepdims=True))
            dz_ref[rows, d:] = (dv * _gelu_grad(zv)).astype(ACT)

    vec = _full((1, d))
    return _call(
        body, "sgu_mix_bwd", (t // tm,),
        [_rows(tm, n), _rows(tm, d), vec, vec, _full((groups, WINDOW, WINDOW)), _full((WINDOW, groups))],
        [_rows(tm, n), vec, vec, _full((WINDOW, d)), _full((WINDOW, groups))],
        [_sds((t, n), ACT), _sds((1, d), F32), _sds((1, d), F32), _sds((WINDOW, d), F32), _sds((WINDOW, groups), F32)],
        (z, dy, lg, lb, ws, bs_t), ("arbitrary",), jobs=jobs)


AG_COPIES = 8


def _prepare(sources, dtypes, n_gather, name):
    n = len(sources)
    arrays, where = [], []
    for parts, layer in sources:
        found = []
        for part in parts:
            known = [i for i, v in enumerate(arrays) if v is part]
            if not known:
                arrays.append(part)
            found.append(known[0] if known else len(arrays) - 1)
        where.append((found, layer))
    shapes = [(sum(p.shape[1] for p in parts), parts[0].shape[2]) for parts, _ in sources]

    def body(*refs):
        ins, refs = refs[:len(arrays)], refs[len(arrays):]
        stage, outs = refs[:n], refs[n:n + n_gather]
        send, recv, local_sem = refs[n + n_gather:]
        x, y, c = _place()
        me, sibling, beside_x, beside_y, far = (x, y, c), (x, y, 1 - c), (1 - x, y, c), (x, 1 - y, c), (1 - x, 1 - y, c)
        slot = lambda dev: 4 * dev[0] + 2 * dev[1] + dev[2]
        other = lambda dev: (dev[0], dev[1], 1 - c)
        whole = lambda a: (0, shapes[a][0])
        cuts = [rows // 2 if rows % 32 == 0 else rows for rows, _ in shapes]
        first = lambda a: (0, cuts[a])
        rest = lambda a: (cuts[a], shapes[a][0] - cuts[a])

        def copy(a, k, block, rows, to, src=None):
            dst = outs[a].at[block, pl.ds(*rows)]
            return pltpu.make_async_remote_copy(
                src_ref=dst if src is None else src, dst_ref=dst, send_sem=send.at[a * AG_COPIES + k],
                recv_sem=recv.at[a * AG_COPIES + k], device_id=to, device_id_type=MESH)

        def cast(a):
            found, layer = where[a]
            at = 0
            for i in found:
                rows = arrays[i].shape[1]
                stage[a][at:at + rows, :] = ins[i][layer].astype(dtypes[a])
                at += rows

        for a in range(n_gather):
            cast(a)
        started = []
        for a in range(n_gather):
            own = pltpu.make_async_copy(stage[a], outs[a].at[slot(me)], local_sem.at[a])
            own.start()
            started.append(own)
        sends = []
        for a in range(n_gather):
            sends += [copy(a, k, slot(me), whole(a), to, src=stage[a]) for k, to in enumerate((sibling, beside_x, beside_y))]
        for cp in sends:
            cp.start()
        for a in range(n_gather, n):
            cast(a)
        for a in range(n_gather):
            copy(a, 1, slot(beside_x), whole(a), me).wait_recv()
            copy(a, 2, slot(beside_y), whole(a), me).wait_recv()
            passed = [copy(a, 3, slot(beside_x), first(a), beside_y), copy(a, 5, slot(beside_x), whole(a), sibling),
                      copy(a, 6, slot(beside_y), whole(a), sibling)]
            if rest(a)[1]:
                passed.append(copy(a, 4, slot(beside_y), rest(a), beside_x))
            for cp in passed:
                cp.start()
            sends += passed
        for a in range(n_gather):
            copy(a, 3, slot(far), first(a), me).wait_recv()
            if rest(a)[1]:
                copy(a, 4, slot(far), rest(a), me).wait_recv()
            passed = copy(a, 7, slot(far), whole(a), sibling)
            passed.start()
            sends.append(passed)
        for a in range(n_gather):
            for k, source in ((0, me), (5, beside_x), (6, beside_y), (7, far)):
                copy(a, k, slot(other(source)), whole(a), me).wait_recv()
        for cp in sends:
            cp.wait_send()
        for own in started:
            own.wait()

    res = pl.pallas_call(
        body, name=name,
        in_specs=[IN_VMEM] * len(arrays), out_specs=[IN_VMEM] * n + [ANY] * n_gather,
        out_shape=[_sds(s, dt) for s, dt in zip(shapes, dtypes)]
        + [_sds((N_DEV,) + s, dt) for s, dt in zip(shapes[:n_gather], dtypes)],
        scratch_shapes=[pltpu.SemaphoreType.DMA((n_gather * AG_COPIES,)), pltpu.SemaphoreType.DMA((n_gather * AG_COPIES,)),
                        pltpu.SemaphoreType.DMA((n_gather,))],
        compiler_params=pltpu.CompilerParams(vmem_limit_bytes=VMEM_LIMIT_BYTES),
    )(*arrays)
    return list(res[:n]), list(res[n:])


def _pair_sum(g, r, core, name):
    _, _, rows, cols = g.shape
    tr = _pick(rows, (512, 256, 128))

    def body(core_ref, g_ref, r_ref, p_ref):
        del core_ref
        p_ref[...] = (g_ref[...].astype(F32) + r_ref[...].astype(F32)).astype(p_ref.dtype)

    return pl.pallas_call(
        body, name=name,
        grid_spec=pltpu.PrefetchScalarGridSpec(
            num_scalar_prefetch=1, grid=(4, rows // tr),
            in_specs=[pl.BlockSpec((None, None, tr, cols), lambda q, i, core_ref: (q, core_ref[0], i, 0)),
                      pl.BlockSpec((None, tr, cols), lambda q, i, core_ref: (q, i, 0))],
            out_specs=pl.BlockSpec((None, tr, cols), lambda q, i, core_ref: (q, i, 0))),
        out_shape=_sds((4, rows, cols), g.dtype),
        compiler_params=_params("parallel", "parallel"),
    )(core, g, r)


def _adam(w, g, m, v):
    m2 = ADAM_B1 * m + (1.0 - ADAM_B1) * g
    v2 = ADAM_B2 * v + (1.0 - ADAM_B2) * (g * g)
    m_hat = m2 / (1.0 - ADAM_B1 ** ADAM_STEP)
    v_hat = v2 / (1.0 - ADAM_B2 ** ADAM_STEP)
    return -ADAM_LR * (m_hat / (jnp.sqrt(v_hat) + ADAM_EPS) + ADAM_WD * w), m2, v2


def _shard_update(own, index, received, w, m, v, layer, so_far, name):
    _, rows, cols = w.shape
    n_recv = received.shape[0]
    tr = _shard_tile(rows)

    def body(index_ref, p_ref, q_ref, w_ref, m_ref, v_ref, *rest):
        del index_ref
        g_out, d_out, m_out, v_out = rest[-4:]
        g = p_ref[...].astype(F32)
        for s in range(n_recv):
            g = g + q_ref[s].astype(F32)
        g_out[...] = g
        d_out[...], m_out[...], v_out[...] = _adam(w_ref[...], g, m_ref[...], v_ref[...])

    tile = pl.BlockSpec((None, tr, cols), lambda i, index_ref: (layer, i, 0))
    kept = list(so_far) if so_far is not None else []
    return pl.pallas_call(
        body, name=name,
        grid_spec=pltpu.PrefetchScalarGridSpec(
            num_scalar_prefetch=1, grid=(rows // tr,),
            in_specs=[pl.BlockSpec((None, tr, cols), lambda i, index_ref: (index_ref[0], i, 0)),
                      pl.BlockSpec((n_recv, tr, cols), lambda i, index_ref: (0, i, 0)), tile, tile, tile] + [ANY] * len(kept),
            out_specs=[tile] * 4),
        out_shape=[_sds(w.shape, F32)] * 4,
        input_output_aliases={6 + i: i for i in range(len(kept))},
        compiler_params=_params("parallel"),
    )(index, own, received, w, m, v, *kept)


def _natural_pieces(shape, r, c):
    if tuple(shape[-2:]) == (r, c) and all(n == 1 for n in shape[:-2]):
        return [((0,) * (len(shape) - 2), (0, c))]
    groups, width = shape[1], shape[3]
    assert len(shape) == 4 and shape[0] == 1 and shape[2] == r and groups * width == c, (shape, r, c)
    return [((0, g), (g * width, width)) for g in range(groups)]


def _replicated_update(shares, where, params, name):
    flat = [a for p in params if p is not None for a in p]

    def body(s_ref, *refs):
        ins, outs = refs[:len(flat)], refs[len(flat):]
        total = s_ref[0]
        for dev in range(1, N_DEV):
            total = total + s_ref[dev]
        i_at = o_at = 0
        for ranges, p in zip(where, params):
            chunks = [total[r0:r0 + r, :c] for r0, r, c in ranges]
            g2d = chunks[0] if len(chunks) == 1 else jnp.concatenate(chunks, axis=1)
            if p is None:
                outs[o_at][...] = g2d
                o_at += 1
                continue
            w_ref, m_ref, v_ref = ins[i_at:i_at + 3]
            for idx, (c0, cols) in _natural_pieces(p[0].shape, *g2d.shape):
                at = idx + (slice(None), slice(None))
                g = g2d[:, c0:c0 + cols]
                outs[o_at][at] = g
                outs[o_at + 1][at], outs[o_at + 2][at], outs[o_at + 3][at] = _adam(w_ref[at], g, m_ref[at], v_ref[at])
            i_at, o_at = i_at + 3, o_at + 4

    out_shape = []
    for ranges, p in zip(where, params):
        out_shape += [_sds((ranges[0][1], sum(c for _, _, c in ranges)), F32)] if p is None else [_sds(p[0].shape, F32)] * 4
    res = pl.pallas_call(
        body, name=name, out_shape=out_shape, compiler_params=pltpu.CompilerParams(vmem_limit_bytes=VMEM_LIMIT_BYTES),
    )(shares, *flat)
    out, at = [], 0
    for p in params:
        out.append(list(res[at:at + (1 if p is None else 4)]))
        at += 1 if p is None else 4
    return out


def _rope_tables(t):
    half = HEAD_DIM // 2
    inv_freq = ROPE_THETA ** (-(jnp.arange(half, dtype=F32) * 2.0) / HEAD_DIM)
    ang = jnp.arange(t, dtype=jnp.int32).astype(F32)[:, None] * inv_freq[None, :]
    cos, sin = jnp.cos(ang), jnp.sin(ang)
    return jnp.tile(jnp.concatenate([cos, cos], axis=1), (1, 2)), jnp.tile(jnp.concatenate([-sin, sin], axis=1), (1, 2))


def _rows_full(gathered):
    return gathered.reshape(-1, gathered.shape[-1])


def _rows_blocked(full):
    return full.reshape(N_DEV, full.shape[0] // N_DEV, full.shape[1]).astype(ACT)


def _pack_rows(parts, width):
    rows, where, at = [], [], 0
    for v in parts:
        r, c = v.shape
        n_rows = -(-r // 8) * 8
        chunks = []
        for c0 in range(0, c, width):
            chunk = v[:, c0:c0 + width].astype(F32)
            rows.append(jnp.pad(chunk, ((0, n_rows - r), (0, width - chunk.shape[1]))))
            chunks.append((at, r, chunk.shape[1]))
            at += n_rows
        where.append(chunks)
    return jnp.concatenate(rows, axis=0), where


def _halves(block):
    half = block.shape[0] // 2
    return (0, half), (half, half)


def _whole(block):
    return (0, block.shape[0])


EARLY = ("attn_w_qkv", "sgu_ln", "attn_w_o")
LATE = ("sgu_w_in", "sgu_w_out", "gu0", "gu1", "dn0", "dn1")
SWAPPED = ("attn_w_qkv", "ffn_w_gate_up")
BEFORE_ATTN = ("norm_mix_post", "norm_ffn_pre", "norm_ffn_post", "attn_b_o", "sgu_w_spatial", "sgu_b_spatial")
AFTER_ATTN = ("attn_b_qkv", "attn_sinks")
LAST = ("norm_mix_pre",)
WEIGHTS = ("norm_mix_pre", "norm_mix_post", "norm_ffn_pre", "norm_ffn_post", "attn_w_qkv", "attn_b_qkv", "attn_sinks", "attn_w_o",
           "attn_b_o", "sgu_w_in", "sgu_ln_g", "sgu_ln_b", "sgu_w_spatial", "sgu_b_spatial", "sgu_w_out", "ffn_w_gate_up", "ffn_w_down")


LAYER_OF = {"gu0": ("ffn_w_gate_up", 0), "gu1": ("ffn_w_gate_up", 1), "dn0": ("ffn_w_down", 0), "dn1": ("ffn_w_down", 1)}


def _source(tree, name):
    if name == "sgu_ln":
        return [tree["sgu_ln_g"][None], tree["sgu_ln_b"][None]], 0
    leaf, layer = LAYER_OF.get(name, (name, 0))
    return [tree[leaf]], layer


def kernel(x, norm_mix_pre, norm_mix_post, norm_ffn_pre, norm_ffn_post, attn_w_qkv, attn_b_qkv, attn_sinks, attn_w_o, attn_b_o, sgu_w_in, sgu_ln_g, sgu_ln_b, sgu_w_spatial, sgu_b_spatial, sgu_w_out, ffn_w_gate_up, ffn_w_down, loss_target, m_norm_mix_pre, m_norm_mix_post, m_norm_ffn_pre, m_norm_ffn_post, m_attn_w_qkv, m_attn_b_qkv, m_attn_sinks, m_attn_w_o, m_attn_b_o, m_sgu_w_in, m_sgu_ln_g, m_sgu_ln_b, m_sgu_w_spatial, m_sgu_b_spatial, m_sgu_w_out, m_ffn_w_gate_up, m_ffn_w_down, v_norm_mix_pre, v_norm_mix_post, v_norm_ffn_pre, v_norm_ffn_post, v_attn_w_qkv, v_attn_b_qkv, v_attn_sinks, v_attn_w_o, v_attn_b_o, v_sgu_w_in, v_sgu_ln_g, v_sgu_ln_b, v_sgu_w_spatial, v_sgu_b_spatial, v_sgu_w_out, v_ffn_w_gate_up, v_ffn_w_down):
    w = dict(norm_mix_pre=norm_mix_pre, norm_mix_post=norm_mix_post, norm_ffn_pre=norm_ffn_pre, norm_ffn_post=norm_ffn_post,
             attn_w_qkv=attn_w_qkv, attn_b_qkv=attn_b_qkv, attn_sinks=attn_sinks, attn_w_o=attn_w_o, attn_b_o=attn_b_o,
             sgu_w_in=sgu_w_in, sgu_ln_g=sgu_ln_g, sgu_ln_b=sgu_ln_b, sgu_w_spatial=sgu_w_spatial, sgu_b_spatial=sgu_b_spatial,
             sgu_w_out=sgu_w_out, ffn_w_gate_up=ffn_w_gate_up, ffn_w_down=ffn_w_down)
    mom = dict(norm_mix_pre=m_norm_mix_pre, norm_mix_post=m_norm_mix_post, norm_ffn_pre=m_norm_ffn_pre, norm_ffn_post=m_norm_ffn_post,
               attn_w_qkv=m_attn_w_qkv, attn_b_qkv=m_attn_b_qkv, attn_sinks=m_attn_sinks, attn_w_o=m_attn_w_o, attn_b_o=m_attn_b_o,
               sgu_w_in=m_sgu_w_in, sgu_ln_g=m_sgu_ln_g, sgu_ln_b=m_sgu_ln_b, sgu_w_spatial=m_sgu_w_spatial,
               sgu_b_spatial=m_sgu_b_spatial, sgu_w_out=m_sgu_w_out, ffn_w_gate_up=m_ffn_w_gate_up, ffn_w_down=m_ffn_w_down)
    var = dict(norm_mix_pre=v_norm_mix_pre, norm_mix_post=v_norm_mix_post, norm_ffn_pre=v_norm_ffn_pre, norm_ffn_post=v_norm_ffn_post,
               attn_w_qkv=v_attn_w_qkv, attn_b_qkv=v_attn_b_qkv, attn_sinks=v_attn_sinks, attn_w_o=v_attn_w_o, attn_b_o=v_attn_b_o,
               sgu_w_in=v_sgu_w_in, sgu_ln_g=v_sgu_ln_g, sgu_ln_b=v_sgu_ln_b, sgu_w_spatial=v_sgu_w_spatial,
               sgu_b_spatial=v_sgu_b_spatial, sgu_w_out=v_sgu_w_out, ffn_w_gate_up=v_ffn_w_gate_up, ffn_w_down=v_ffn_w_down)
    w, mom, var = [{**tree, **{n: jnp.swapaxes(tree[n], 1, 2) for n in SWAPPED}} for tree in (w, mom, var)]
    xs, target = x[0], loss_target[0]
    t, d = xs.shape
    row = lambda v: v.reshape(1, -1).astype(F32)
    core = lax.axis_index("c").astype(jnp.int32).reshape(1)
    chip = (2 * lax.axis_index("x") + lax.axis_index("y")).astype(jnp.int32).reshape(1)
    names = EARLY + LATE
    staged, early = _prepare([_source(w, n) for n in names], [F32 if n == "sgu_ln" else ACT for n in names], len(EARLY),
                             "weights_prepare")
    stage = dict(zip(names, staged))
    w_qkv = _rows_full(early[0])
    lg, lb = row(early[1][:, 0, :]), row(early[1][:, 1, :])
    w_o = _rows_full(early[2])
    b_qkv = row(attn_b_qkv)
    sinks = attn_sinks.reshape(1, -1).astype(F32)
    ws = sgu_w_spatial[0].astype(F32)
    bs_t = sgu_b_spatial[0].astype(F32).T
    cos, sin = _rope_tables(t)
    gather = lambda name, so_far, **pieces: _gather_job(stage[name], so_far, **pieces)
    a_half = lambda name: _halves(stage[name])[0]
    b_half = lambda name: _halves(stage[name])[1]
    whole = lambda name: _whole(stage[name])

    def pieces(name, n):
        rows = stage[name].shape[0] // n
        return [(i * rows, rows) for i in range(n)]

    ways = lambda parts: [(piece, i % 2) for i, piece in enumerate(parts)]
    relay = lambda name, so_far, **steps: _relay_gather_job(stage[name], so_far, **steps)
    gu0_parts, gu1_parts = pieces("gu0", 4), pieces("gu1", 4)
    dn0_parts, dn1_parts, in_parts, out_parts = pieces("dn0", 2), pieces("dn1", 2), pieces("sgu_w_in", 2), pieces("sgu_w_out", 2)
    (h0, q, kdup, vdup), ((g_gu0,),) = _attn_qkv_fwd(
        xs, row(norm_mix_pre[0]), w_qkv, b_qkv, cos, sin, jobs=[relay("gu0", None, sends=gu0_parts)])
    (o,), ((g_gu0,), (g_dn0,)) = _attn_fwd(q, kdup, vdup, sinks, jobs=[
        relay("gu0", g_gu0, relays=ways(gu0_parts), forwards=gu0_parts), relay("dn0", None, sends=dn0_parts)])
    (m0, x1), ((g_gu0,), (g_dn0,), (g_in,)) = _proj_res(o, w_o, row(attn_b_o), row(norm_mix_post[0]), xs, "attn_out_fwd", jobs=[
        relay("gu0", g_gu0, far=gu0_parts), relay("dn0", g_dn0, relays=ways(dn0_parts), forwards=dn0_parts),
        relay("sgu_w_in", None, sends=in_parts)])
    w_gu0 = _rows_full(g_gu0)
    (h1, gu0, a0), ((g_dn0,), (g_in,), (g_out,), (g_gu1,)) = _ffn_up_fwd(x1, row(norm_ffn_pre[0]), w_gu0, jobs=[
        relay("dn0", g_dn0, far=dn0_parts), relay("sgu_w_in", g_in, relays=ways(in_parts), forwards=in_parts),
        relay("sgu_w_out", None, sends=out_parts), relay("gu1", None, sends=gu1_parts[:2])])
    w_dn0 = _rows_full(g_dn0)
    (f0, x2), ((g_in,), (g_out,), (g_gu1,)) = _proj_res(a0, w_dn0, None, row(norm_ffn_post[0]), x1, "ffn_down_fwd0", jobs=[
        relay("sgu_w_in", g_in, far=in_parts), relay("sgu_w_out", g_out, relays=ways(out_parts), forwards=out_parts),
        relay("gu1", g_gu1, sends=gu1_parts[2:], relays=ways(gu1_parts[:2]), forwards=gu1_parts[:2])])
    w_in = g_in
    (h2, z, y), ((g_out,), (g_gu1,), (g_dn1,)) = _sgu_fwd(x2, row(norm_mix_pre[1]), w_in, lg, lb, ws, bs_t, jobs=[
        relay("sgu_w_out", g_out, far=out_parts),
        relay("gu1", g_gu1, relays=ways(gu1_parts[2:]), forwards=gu1_parts[2:], far=gu1_parts[:2]),
        relay("dn1", None, sends=dn1_parts)])
    w_out = _rows_full(g_out)
    (m1, x3), ((g_gu1,), (g_dn1,)) = _proj_res(y, w_out, None, row(norm_mix_post[1]), x2, "sgu_out_fwd", jobs=[
        relay("gu1", g_gu1, far=gu1_parts[2:]), relay("dn1", g_dn1, relays=ways(dn1_parts), forwards=dn1_parts)])
    w_gu1 = _rows_full(g_gu1)
    (h3, gu1, a1), ((g_dn1,),) = _ffn_up_fwd(x3, row(norm_ffn_pre[1]), w_gu1, jobs=[relay("dn1", g_dn1, far=dn1_parts)])
    w_dn1 = _rows_full(g_dn1)
    (f1, dx4, loss_tile), _ = _proj_res(a1, w_dn1, None, row(norm_ffn_post[1]), x3, "ffn_down_fwd1", target=target)

    to_sibling = lambda g: _scatter_job([g], 4, _to_sibling)
    pair = lambda g, r, name: _pair_sum(g.reshape((4, 2) + g.shape[1:]), r, core, "pair_sum_" + name)
    to_chips = lambda p, prev=None, piece=None: _scatter_job([p], 3, _to_owner_chip, prev=prev, piece=piece)
    out_g, out_d, out_m, out_v = {}, {}, {}, {}

    trees = [{**tree, "sgu_ln": jnp.stack([tree["sgu_ln_g"], tree["sgu_ln_b"]], axis=1)} for tree in (w, mom, var)]
    so_far = {}

    def update(name, own, index, received):
        leaf, layer = LAYER_OF.get(name, (name, 0))
        so_far[leaf] = _shard_update(own, index, received, *[tree[leaf] for tree in trees], layer, so_far.get(leaf), "update_" + name)
        for out, val in zip((out_g, out_d, out_m, out_v), so_far[leaf]):
            out[leaf] = val

    def finish(names, n_extra, shares, where, name):
        res = _replicated_update(shares, where, [(w[n], mom[n], var[n]) for n in names] + [None] * n_extra, name)
        for n, vals in zip(names, res):
            out_g[n], out_d[n], out_m[n], out_v[n] = vals
        return [vals[0] for vals in res[len(names):]]

    first_half = lambda p: _halves(p[0])[0]
    second_half = lambda p: _halves(p[0])[1]
    (df1, dgu1, dx3, dg_ffn_post1, dg_ffn_pre1), _ = _ffn_bwd(
        dx4, f1, row(norm_ffn_post[1]), w_dn1, gu1, w_gu1, x3, row(norm_ffn_pre[1]), "ffn_bwd1")
    b_gu1, _ = _wgrad(h3, dgu1, "ffn_up_wgrad1", ACT, transposed=True)
    b_gu1 = _rows_blocked(b_gu1)
    dw_dn1, _ = _wgrad(a1, df1, "ffn_down_wgrad1", ACT)
    b_dn1 = _rows_blocked(dw_dn1)
    (dm1, dy, dg_mix_post1), ((r_gu1,), (r_dn1,)) = _post_bwd(
        dx3, m1, row(norm_mix_post[1]), w_out, "sgu", "sgu_out_bwd", jobs=[to_sibling(b_gu1), to_sibling(b_dn1)])
    p_gu1, p_dn1 = pair(b_gu1, r_gu1, "gu1"), pair(b_dn1, r_dn1, "dn1")
    dw_out, _ = _wgrad(y, dm1, "sgu_out_wgrad", ACT)
    b_out = _rows_blocked(dw_out)
    (dz, dlg, dlb, dws, dbs_t), ((q_gu1,), (r_out,)) = _sgu_mix_bwd(z, dy, lg, lb, ws, bs_t, jobs=[
        to_chips(p_gu1, piece=first_half(p_gu1)), to_sibling(b_out)])
    p_out = pair(b_out, r_out, "sgu_w_out")
    b_in, _ = _wgrad(h2, dz, "sgu_in_wgrad", ACT, out_block=stage["sgu_w_in"].shape[1])
    b_ln = jnp.stack([dlg.reshape(N_DEV, -1), dlb.reshape(N_DEV, -1)], axis=1)
    (dx2, dg_mix_pre1), _ = _pre_bwd(dz, w_in, x2, row(norm_mix_pre[1]), dx3, "sgu_in_bwd")
    (df0, dgu0, dx1, dg_ffn_post0, dg_ffn_pre0), ((q_gu1,), (q_dn1,), (q_out,), (r_in,), (r_ln,)) = _ffn_bwd(
        dx2, f0, row(norm_ffn_post[0]), w_dn0, gu0, w_gu0, x1, row(norm_ffn_pre[0]), "ffn_bwd0",
        jobs=[to_chips(p_gu1, prev=[q_gu1], piece=second_half(p_gu1)), to_chips(p_dn1), to_chips(p_out), to_sibling(b_in),
              to_sibling(b_ln)])
    update("gu1", p_gu1, chip, q_gu1)
    update("dn1", p_dn1, chip, q_dn1)
    update("sgu_w_out", p_out, chip, q_out)
    p_in, p_ln = pair(b_in, r_in, "sgu_w_in"), pair(b_ln, r_ln, "sgu_ln")
    b_gu0, ((q_in,), (q_ln,)) = _wgrad(h1, dgu0, "ffn_up_wgrad0", ACT, transposed=True, jobs=[to_chips(p_in), to_chips(p_ln)])
    update("sgu_w_in", p_in, chip, q_in)
    update("sgu_ln", p_ln, chip, q_ln)
    b_gu0 = _rows_blocked(b_gu0)
    (dm0, do, dg_mix_post0, db_o), ((r_gu0,),) = _post_bwd(
        dx1, m0, row(norm_mix_post[0]), w_o, "attn", "attn_out_bwd", jobs=[to_sibling(b_gu0)])
    p_gu0 = pair(b_gu0, r_gu0, "gu0")
    dw_dn0, ((q_gu0,),) = _wgrad(a0, df0, "ffn_down_wgrad0", ACT, jobs=[to_chips(p_gu0, piece=first_half(p_gu0))])
    b_dn0 = _rows_blocked(dw_dn0)
    dw_o, ((r_dn0,),) = _wgrad(o, dm0, "attn_out_wgrad", ACT, jobs=[to_sibling(b_dn0)])
    p_dn0 = pair(b_dn0, r_dn0, "dn0")
    b_o = _rows_blocked(dw_o)
    grads = {
        "norm_mix_post": jnp.concatenate([dg_mix_post0, dg_mix_post1], axis=0),
        "norm_ffn_pre": jnp.concatenate([dg_ffn_pre0, dg_ffn_pre1], axis=0),
        "norm_ffn_post": jnp.concatenate([dg_ffn_post0, dg_ffn_post1], axis=0),
        "attn_b_o": db_o,
        "sgu_w_spatial": dws,
        "sgu_b_spatial": dbs_t.T,
    }
    shares1, where1 = _pack_rows([grads[name] for name in BEFORE_ATTN], d)
    (dqkv, db_qkv, dsink), ((q_gu0,), (q_dn0,), (r_o,), (g_shares1,)) = _attn_bwd(q, kdup, vdup, do, sinks, cos, sin, jobs=[
        to_chips(p_gu0, prev=[q_gu0], piece=second_half(p_gu0)), to_chips(p_dn0), to_sibling(b_o),
        _gather_job(shares1, None, sends=[_whole(shares1)])])
    update("gu0", p_gu0, chip, q_gu0)
    update("dn0", p_dn0, chip, q_dn0)
    p_o = pair(b_o, r_o, "attn_w_o")
    grads.update({"attn_b_qkv": db_qkv, "attn_sinks": dsink[:1, :d // HEAD_DIM]})
    shares2, where2 = _pack_rows([grads[name] for name in AFTER_ATTN] + [loss_tile[:1, :1]], d)
    (dx0, dg_mix_pre0), _ = _pre_bwd(dqkv, w_qkv, xs, row(norm_mix_pre[0]), dx1, "attn_qkv_bwd", True)
    grads["norm_mix_pre"] = jnp.concatenate([dg_mix_pre0, dg_mix_pre1], axis=0)
    shares3, where3 = _pack_rows([grads[name] for name in LAST], d)
    dw_qkv, ((q_o,), (g_shares1,), (g_shares2,), (g_shares3,)) = _wgrad(h0, dqkv, "attn_qkv_wgrad", ACT, transposed=True, jobs=[
        to_chips(p_o), _gather_job(shares1, g_shares1, forwards=[_whole(shares1)]),
        _gather_job(shares2, None, sends=[_whole(shares2)]), _gather_job(shares3, None, sends=[_whole(shares3)])])
    update("attn_w_o", p_o, chip, q_o)
    b_qkv_grad = _rows_blocked(dw_qkv)
    (r_qkv,), (g_shares2,), (g_shares3,) = _copies_only([
        to_sibling(b_qkv_grad), _gather_job(shares2, g_shares2, forwards=[_whole(shares2)]),
        _gather_job(shares3, g_shares3, forwards=[_whole(shares3)])], "grads_tail_to_sibling")
    p_qkv = pair(b_qkv_grad, r_qkv, "attn_w_qkv")
    (q_qkv,), = _copies_only([to_chips(p_qkv)], "grads_tail_to_owner_chip")
    update("attn_w_qkv", p_qkv, chip, q_qkv)

    finish(BEFORE_ATTN, 0, g_shares1, where1, "replicated_update_before_attn")
    loss = finish(AFTER_ATTN, 1, g_shares2, where2, "replicated_update_after_attn")[0][0, 0]
    finish(LAST, 0, g_shares3, where3, "replicated_update_last")

    for out in (out_g, out_d, out_m, out_v):
        out["sgu_ln_g"], out["sgu_ln_b"] = out["sgu_ln"][:, 0, :], out["sgu_ln"][:, 1, :]
        for n in SWAPPED:
            out[n] = jnp.swapaxes(out[n], 1, 2)

    return (loss, dx0[None], *[out_g[n] for n in WEIGHTS], *[out_d[n] for n in WEIGHTS],
            *[out_m[n] for n in WEIGHTS], *[out_v[n] for n in WEIGHTS])
```

```python
import functools
import math
import operator

import jax
import jax.numpy as jnp
from jax import lax
from jax.experimental import pallas as pl
from jax.experimental.pallas import tpu as pltpu

F32 = jnp.float32
ACT = jnp.bfloat16

EPS = 1e-6
HEAD_DIM = 64
PAIR = 2 * HEAD_DIM
GQA_GROUP = 4
WINDOW = 128
ROPE_THETA = 10000.0
SCALE = HEAD_DIM ** -0.5
MASKED = -1e30
LANES = 128
MXU_WIDTH = 256

ADAM_LR, ADAM_B1, ADAM_B2, ADAM_EPS, ADAM_WD, ADAM_STEP = 0.001, 0.9, 0.999, 1e-08, 0.01, 10

N_DEV = 8
VMEM_LIMIT_BYTES = 56 * 1024 * 1024
MESH = pl.DeviceIdType.MESH
ANY = pl.BlockSpec(memory_space=pl.ANY)
IN_VMEM = pl.BlockSpec(memory_space=pltpu.VMEM)


def _pick(n, candidates):
    for c in candidates:
        if n % c == 0:
            return c
    return n


def _row_tile(t, most=512):
    return _pick(t, (most,))


def _shard_tile(rows):
    return next((c for c in (512, 256, 176, 128, 96, 64) if rows % c == 0 and rows >= 2 * c), rows)


def _mxu_chunks(n, most=4 * MXU_WIDTH):
    assert n % MXU_WIDTH == 0 and most % MXU_WIDTH == 0
    return [(c, min(most, n - c)) for c in range(0, n, most)]


def _params(*sem):
    return pltpu.CompilerParams(dimension_semantics=sem, vmem_limit_bytes=VMEM_LIMIT_BYTES)


def _full(shape):
    return pl.BlockSpec(shape, lambda *_: (0,) * len(shape))


def _resident(shape):
    return pl.BlockSpec(shape, lambda *_: (0,) * len(shape), pipeline_mode=pl.Buffered(1))


def _rows(tm, n):
    return pl.BlockSpec((tm, n), lambda i: (i, 0))


def _sds(shape, dtype):
    return jax.ShapeDtypeStruct(shape, dtype)


def _place():
    return lax.axis_index("x"), lax.axis_index("y"), lax.axis_index("c")


def _other_chips(x, y):
    return [(1 - x, y), (x, 1 - y), (1 - x, 1 - y)]


class _Job:
    def __init__(self, inputs, out_shape, aliases, emit, n_remote, n_local=0):
        self.inputs, self.out_shape, self.aliases, self.emit = list(inputs), list(out_shape), dict(aliases), emit
        self.n_remote, self.n_local = n_remote, n_local


def _call(body, name, grid, in_specs, out_specs, out_shape, args, sem, scratch=(), jobs=()):
    n_in, n_out, n_scr = len(args), len(out_shape), len(scratch)
    j_in = [a for job in jobs for a in job.inputs]
    j_out = [s for job in jobs for s in job.out_shape]
    aliases, at_in, at_out = {}, n_in, n_out
    for job in jobs:
        aliases.update({at_in + i: at_out + o for i, o in job.aliases.items()})
        at_in, at_out = at_in + len(job.inputs), at_out + len(job.out_shape)
    n_remote = sum(job.n_remote for job in jobs)
    n_local = sum(job.n_local for job in jobs)

    def wrapped(*refs):
        ins, jin = refs[:n_in], refs[n_in:n_in + len(j_in)]
        rest = refs[n_in + len(j_in):]
        outs, jout = rest[:n_out], rest[n_out:n_out + len(j_out)]
        scr = rest[n_out + len(j_out):n_out + len(j_out) + n_scr]
        if not jobs:
            body(*ins, *outs, *scr)
            return
        send, recv, local = rest[n_out + len(j_out) + n_scr:]
        ids = [pl.program_id(a) for a in range(len(grid))]
        first = functools.reduce(operator.and_, [i == 0 for i in ids])
        last = functools.reduce(operator.and_, [i == g - 1 for i, g in zip(ids, grid)])

        def descriptors():
            place, found, i, o, r, l = _place(), [], 0, 0, 0, 0
            for job in jobs:
                for src, dst, to in job.emit(jin[i:i + len(job.inputs)], jout[o:o + len(job.out_shape)], place):
                    if to is None:
                        found.append(pltpu.make_async_copy(src, dst, local.at[l]))
                        l += 1
                    else:
                        found.append(pltpu.make_async_remote_copy(src_ref=src, dst_ref=dst, send_sem=send.at[r], recv_sem=recv.at[r],
                                                                  device_id=to, device_id_type=MESH))
                        r += 1
                i, o = i + len(job.inputs), o + len(job.out_shape)
            return found

        @pl.when(first)
        def _():
            for cp in descriptors():
                cp.start()

        body(*ins, *outs, *scr)

        @pl.when(last)
        def _():
            for cp in descriptors():
                cp.wait()

    sems = [pltpu.SemaphoreType.DMA((n_remote,)), pltpu.SemaphoreType.DMA((n_remote,)),
            pltpu.SemaphoreType.DMA((max(n_local, 1),))] if jobs else []
    res = pl.pallas_call(
        wrapped, name=name, grid=grid,
        in_specs=list(in_specs) + [ANY] * len(j_in), out_specs=list(out_specs) + [ANY] * len(j_out),
        out_shape=list(out_shape) + j_out, scratch_shapes=list(scratch) + sems, input_output_aliases=aliases,
        compiler_params=_params(*(("arbitrary",) * len(grid) if jobs else sem)),
    )(*args, *j_in)
    job_res, at = [], n_out
    for job in jobs:
        job_res.append(list(res[at:at + len(job.out_shape)]))
        at += len(job.out_shape)
    return list(res[:n_out]), job_res


def _copies_only(jobs, name):
    def body(o_ref):
        o_ref[...] = jnp.zeros_like(o_ref)

    return _call(body, name, (1,), [], [_full((8, LANES))], [_sds((8, LANES), F32)], (), ("arbitrary",), jobs=jobs)[1]


def _gather_job(stage, gathered, sends=(), forwards=()):
    shape = _sds((N_DEV,) + stage.shape, stage.dtype)
    inputs = ([stage] if sends else []) + ([gathered] if gathered is not None else [])
    aliases = {len(inputs) - 1: 0} if gathered is not None else {}

    def emit(ins, outs, place):
        x, y, c = place
        sibling, chips, mine, g = (x, y, 1 - c), _other_chips(x, y), 4 * x + 2 * y + c, outs[0]
        found = []
        for r0, nr in sends:
            src, dst = ins[0].at[pl.ds(r0, nr)], g.at[mine, pl.ds(r0, nr)]
            found += [(src, dst, None), (src, dst, sibling)] + [(src, dst, (px, py, c)) for px, py in chips]
        for r0, nr in forwards:
            for px, py in chips:
                block = 4 * px + 2 * py + c
                found.append((ins[-1].at[block, pl.ds(r0, nr)], g.at[block, pl.ds(r0, nr)], sibling))
        return found

    return _Job(inputs, [shape], aliases, emit, 4 * len(sends) + 3 * len(forwards), len(sends))


def _relay_gather_job(stage, gathered, sends=(), relays=(), forwards=(), far=()):
    shape = _sds((N_DEV,) + stage.shape, stage.dtype)
    inputs = ([stage] if sends else []) + ([gathered] if gathered is not None else [])
    aliases = {len(inputs) - 1: 0} if gathered is not None else {}

    def emit(ins, outs, place):
        x, y, c = place
        sibling, beside_x, beside_y, g = (x, y, 1 - c), (1 - x, y, c), (x, 1 - y, c), outs[0]
        slot = lambda dev: 4 * dev[0] + 2 * dev[1] + dev[2]
        found = []
        for r0, nr in sends:
            src, dst = ins[0].at[pl.ds(r0, nr)], g.at[slot((x, y, c)), pl.ds(r0, nr)]
            found += [(src, dst, None), (src, dst, sibling), (src, dst, beside_x), (src, dst, beside_y)]

        def passed_on(block, piece, to):
            return ins[-1].at[block, pl.ds(*piece)], g.at[block, pl.ds(*piece)], to

        for piece, way in relays:
            found.append(passed_on(slot(beside_x), piece, beside_y) if way == 0 else passed_on(slot(beside_y), piece, beside_x))
        for piece in forwards:
            found += [passed_on(slot(beside_x), piece, sibling), passed_on(slot(beside_y), piece, sibling)]
        for piece in far:
            found.append(passed_on(slot((1 - x, 1 - y, c)), piece, sibling))
        return found

    return _Job(inputs, [shape], aliases, emit, 3 * len(sends) + len(relays) + 2 * len(forwards) + len(far), len(sends))


def _scatter_job(arrays, n_slots, route, prev=None, piece=None):
    shapes = [_sds((n_slots,) + v.shape[1:], v.dtype) for v in arrays]
    inputs = list(arrays) + (list(prev) if prev else [])
    aliases = {len(arrays) + i: i for i in range(len(arrays))} if prev else {}

    def emit(ins, outs, place):
        found = []
        for a in range(len(arrays)):
            for s in range(n_slots):
                block, to = route(s, *place)
                if piece is None:
                    found.append((ins[a].at[block], outs[a].at[s], to))
                else:
                    found.append((ins[a].at[block, pl.ds(*piece)], outs[a].at[s, pl.ds(*piece)], to))
        return found

    return _Job(inputs, shapes, aliases, emit, len(arrays) * n_slots)


def _to_sibling(s, x, y, c):
    return 2 * s + (1 - c), (x, y, 1 - c)


def _to_owner_chip(s, x, y, c):
    px, py = _other_chips(x, y)[s]
    return 2 * px + py, (px, py, c)


def _rstd(x):
    return lax.rsqrt(jnp.mean(x * x, axis=-1, keepdims=True) + EPS)


def _rms_bwd(xhat, r, g, dy):
    dxh = dy * g
    return r * (dxh - xhat * jnp.mean(dxh * xhat, axis=-1, keepdims=True))


def _first_half(rows):
    lane = lax.broadcasted_iota(jnp.int32, (rows, PAIR), 1)
    return (lane % HEAD_DIM) < (HEAD_DIM // 2)


def _rot_half(v, first):
    return jnp.where(first, pltpu.roll(v, PAIR - HEAD_DIM // 2, 1), pltpu.roll(v, HEAD_DIM // 2, 1))


def _rope(v, cos, sin, first):
    return v * cos + _rot_half(v, first) * sin


def _rope_t(dv, cos, sin, first):
    return dv * cos + _rot_half(dv * sin, first)


def _dot(a, b):
    return jnp.dot(a, b, preferred_element_type=F32)


def _dot_nt(a, b):
    return lax.dot_general(a, b, (((1,), (1,)), ((), ())), preferred_element_type=F32)


def _dot_tn(a, b):
    return lax.dot_general(a, b, (((0,), (0,)), ((), ())), preferred_element_type=F32)


def _sigmoid(v):
    return 1.0 / (1.0 + jnp.exp(-v))


_GELU_K = math.sqrt(2.0 / math.pi)
_GELU_C = 0.044715


def _gelu(v):
    return v * (0.5 * (1.0 + jnp.tanh(_GELU_K * (v + _GELU_C * (v * v * v)))))


def _gelu_grad(v):
    t = jnp.tanh(_GELU_K * (v + _GELU_C * (v * v * v)))
    return 0.5 * (1.0 + t) + 0.5 * v * (1.0 - t * t) * (_GELU_K * (1.0 + 3.0 * _GELU_C * (v * v)))


def _attn_qkv_fwd(x, g, w, b, cos, sin, jobs=()):
    t, d = x.shape
    n = w.shape[0]
    kd = n - d
    assert (kd // 2) % PAIR == 0
    tm = _row_tile(t)
    ch = _pick(d, (512,))

    def body(x_ref, g_ref, w_ref, b_ref, cos_ref, sin_ref, h_ref, q_ref, k_ref, v_ref):
        xv = x_ref[...]
        hb = (xv * _rstd(xv) * g_ref[...]).astype(ACT)
        h_ref[...] = hb
        cosv, sinv = cos_ref[...], sin_ref[...]
        first = _first_half(tm)
        left = _lane_halves()[0]

        def proj(lo, width):
            return _dot_nt(hb, w_ref[lo:lo + width, :]) + b_ref[:, lo:lo + width]

        def twice(ref, s, two_heads):
            swapped = pltpu.roll(two_heads, HEAD_DIM, 1)
            ref[:, 2 * s:2 * s + PAIR] = jnp.where(left, two_heads, swapped).astype(ACT)
            ref[:, 2 * s + PAIR:2 * s + 2 * PAIR] = jnp.where(left, swapped, two_heads).astype(ACT)

        for c in range(0, d, ch):
            y = proj(c, ch)
            for s in range(0, ch, PAIR):
                q_ref[:, c + s:c + s + PAIR] = (_rope(y[:, s:s + PAIR], cosv, sinv, first) * SCALE).astype(ACT)
        y = proj(d, kd)
        for s in range(0, kd // 2, PAIR):
            twice(k_ref, s, _rope(y[:, s:s + PAIR], cosv, sinv, first))
            twice(v_ref, s, y[:, kd // 2 + s:kd // 2 + s + PAIR])

    return _call(
        body, "attn_qkv_fwd", (t // tm,),
        [_rows(tm, d), _full((1, d)), _full((n, d)), _full((1, n)), _rows(tm, PAIR), _rows(tm, PAIR)],
        [_rows(tm, d), _rows(tm, d), _rows(tm, kd), _rows(tm, kd)],
        [_sds((t, d), ACT), _sds((t, d), ACT), _sds((t, kd), ACT), _sds((t, kd), ACT)],
        (x, g, w, b, cos, sin), ("parallel",), jobs=jobs)


STACK = GQA_GROUP * WINDOW


def _band_mask(has_prev):
    row = lax.broadcasted_iota(jnp.int32, (STACK, 2 * WINDOW), 0) % WINDOW
    col = lax.broadcasted_iota(jnp.int32, (STACK, 2 * WINDOW), 1)
    return ((col < WINDOW) & (col > row) & has_prev) | ((col >= WINDOW) & (col - WINDOW <= row))


def _lane_halves():
    lane = lax.broadcasted_iota(jnp.int32, (1, PAIR), 1)
    return lane < HEAD_DIM, lane >= HEAD_DIM


def _stack_heads(ref, h, halves):
    parts = []
    for p in range(2):
        pair = ref[:, (2 * h + p) * PAIR:(2 * h + p + 1) * PAIR]
        parts += [jnp.where(half, pair, jnp.zeros_like(pair)) for half in halves]
    return jnp.concatenate(parts, axis=0)


def _sink_column(sink_ref, h):
    row = lax.broadcasted_iota(jnp.int32, (STACK, 1), 0)
    col = jnp.full((STACK, 1), sink_ref[0, GQA_GROUP * h + GQA_GROUP - 1], F32)
    for j in range(GQA_GROUP - 2, -1, -1):
        col = jnp.where(row < (j + 1) * WINDOW, sink_ref[0, GQA_GROUP * h + j], col)
    return col


def _probs(scores, valid, sink):
    s = jnp.where(valid, scores, MASKED)
    m = jnp.maximum(jnp.max(s, axis=-1, keepdims=True), sink)
    p = jnp.exp(s - m)
    psink = jnp.exp(sink - m)
    inv = 1.0 / (jnp.sum(p, axis=-1, keepdims=True) + psink)
    return p * inv, psink * inv


def _attn_fwd(q, kd_, vd, sinks, jobs=()):
    t, d = q.shape
    kd = kd_.shape[1]
    cur = lambda n: (n, 0)
    prev = lambda n: (jnp.maximum(n - 1, 0), 0)

    def body(sink_ref, q_ref, kc_ref, kp_ref, vc_ref, vp_ref, o_ref):
        valid = _band_mask(pl.program_id(0) > 0)
        halves = _lane_halves()
        heads = range(kd // PAIR)
        hs = [slice(h * PAIR, (h + 1) * PAIR) for h in heads]
        scores = [_dot_nt(_stack_heads(q_ref, h, halves), jnp.concatenate([kp_ref[:, hs[h]], kc_ref[:, hs[h]]], axis=0)) for h in heads]
        probs = [_probs(scores[h], valid, _sink_column(sink_ref, h))[0].astype(ACT) for h in heads]
        for h in heads:
            v2 = jnp.concatenate([vp_ref[:, hs[h]], vc_ref[:, hs[h]]], axis=0)
            vs = jnp.concatenate([jnp.where(half, v2, jnp.zeros_like(v2)) for half in halves], axis=0)
            pr = probs[h]
            for p in range(2):
                both = jnp.concatenate([pr[2 * p * WINDOW:(2 * p + 1) * WINDOW], pr[(2 * p + 1) * WINDOW:(2 * p + 2) * WINDOW]], axis=1)
                o_ref[:, (2 * h + p) * PAIR:(2 * h + p + 1) * PAIR] = _dot(both, vs).astype(ACT)

    kv_c, kv_p = pl.BlockSpec((WINDOW, kd), cur), pl.BlockSpec((WINDOW, kd), prev)
    return _call(
        body, "attn_fwd", (t // WINDOW,),
        [pl.BlockSpec(memory_space=pltpu.SMEM), pl.BlockSpec((WINDOW, d), cur), kv_c, kv_p, kv_c, kv_p],
        [pl.BlockSpec((WINDOW, d), cur)], [_sds((t, d), ACT)],
        (sinks, q, kd_, kd_, vd, vd), ("parallel",), jobs=jobs)


def _attn_bwd(q, kd_, vd, do, sinks, cos, sin, jobs=()):
    t, d = q.shape
    kd = kd_.shape[1]
    nb = t // WINDOW
    n_out = d + kd
    assert (kd // 2) % PAIR == 0
    cur = lambda n: (jnp.minimum(n, nb - 1), 0)
    prev = lambda n: (jnp.maximum(jnp.minimum(n, nb - 1) - 1, 0), 0)
    late = lambda n: (jnp.maximum(n - 1, 0), 0)

    def body(sink_ref, q_ref, kc_ref, kp_ref, vc_ref, vp_ref, do_ref, cosc_ref, sinc_ref, cosp_ref, sinp_ref,
             out_ref, db_ref, dsink_ref, dq_keep, dk_keep, dv_keep):
        n = pl.program_id(0)

        @pl.when(n == 0)
        def _():
            for ref in (db_ref, dsink_ref, dq_keep, dk_keep, dv_keep):
                ref[...] = jnp.zeros_like(ref)

        valid = _band_mask(n > 0) & (n < nb)
        halves = _lane_halves()
        first = _first_half(WINDOW)
        slot = lax.broadcasted_iota(jnp.int32, dsink_ref.shape, 1)
        top = lax.broadcasted_iota(jnp.int32, dsink_ref.shape, 0) == 0
        dsink = jnp.zeros(dsink_ref.shape, F32)

        def emit(cols, done):
            out_ref[:, cols] = done.astype(ACT)
            db_ref[:, cols] += jnp.sum(done.astype(F32), axis=0, keepdims=True)

        heads = range(kd // PAIR)
        hs = [slice(h * PAIR, (h + 1) * PAIR) for h in heads]
        k2 = [jnp.concatenate([kp_ref[:, hs[h]], kc_ref[:, hs[h]]], axis=0) for h in heads]
        v2 = [jnp.concatenate([vp_ref[:, hs[h]], vc_ref[:, hs[h]]], axis=0) for h in heads]
        qs = [_stack_heads(q_ref, h, halves) for h in heads]
        dos = [_stack_heads(do_ref, h, halves) for h in heads]
        scores = [_dot_nt(qs[h], k2[h]) for h in heads]
        dps = [_dot_nt(dos[h], v2[h]) for h in heads]
        prs, dss = [], []
        for h in heads:
            pr, psink = _probs(scores[h], valid, _sink_column(sink_ref, h))
            delta = jnp.sum(pr * dps[h], axis=-1, keepdims=True)
            dss.append((pr * (dps[h] - delta)).astype(ACT))
            prs.append(pr.astype(ACT))
            leak = psink * delta
            for j in range(GQA_GROUP):
                share = jnp.sum(leak[j * WINDOW:(j + 1) * WINDOW], axis=0, keepdims=True)
                dsink = dsink - jnp.where(top & (slot == GQA_GROUP * h + j), share, 0.0)
        dqs = [_dot(dss[h], k2[h]) for h in heads]
        dk2 = [_dot_tn(dss[h], qs[h]) for h in heads]
        dv2 = [_dot_tn(prs[h], dos[h]) for h in heads]
        for h in heads:
            for p in range(2):
                ps = slice((2 * h + p) * PAIR, (2 * h + p + 1) * PAIR)
                dqp = jnp.where(halves[0], dqs[h][2 * p * WINDOW:(2 * p + 1) * WINDOW], dqs[h][(2 * p + 1) * WINDOW:(2 * p + 2) * WINDOW])
                emit(ps, dq_keep[:, ps])
                dq_keep[:, ps] = (_rope_t(dqp, cosc_ref[...], sinc_ref[...], first) * SCALE).astype(ACT)
        dks, dvs = [], []
        for h in heads:
            dks.append(dk_keep[:, hs[h]] + _rope_t(dk2[h][:WINDOW], cosp_ref[...], sinp_ref[...], first))
            dk_keep[:, hs[h]] = _rope_t(dk2[h][WINDOW:], cosc_ref[...], sinc_ref[...], first)
            dvs.append(dv_keep[:, hs[h]] + dv2[h][:WINDOW])
            dv_keep[:, hs[h]] = dv2[h][WINDOW:]
        both = lambda v: v + pltpu.roll(v, HEAD_DIM, 1)
        for h in range(0, len(heads), 2):
            at = h // 2 * PAIR
            emit(slice(d + at, d + at + PAIR), jnp.where(halves[0], both(dks[h]), both(dks[h + 1])))
            emit(slice(d + kd // 2 + at, d + kd // 2 + at + PAIR), jnp.where(halves[0], both(dvs[h]), both(dvs[h + 1])))
        dsink_ref[...] += dsink

    kv_c, kv_p = pl.BlockSpec((WINDOW, kd), cur), pl.BlockSpec((WINDOW, kd), prev)
    tab_c, tab_p = pl.BlockSpec((WINDOW, PAIR), cur), pl.BlockSpec((WINDOW, PAIR), prev)
    return _call(
        body, "attn_bwd", (nb + 1,),
        [pl.BlockSpec(memory_space=pltpu.SMEM), pl.BlockSpec((WINDOW, d), cur), kv_c, kv_p, kv_c, kv_p,
         pl.BlockSpec((WINDOW, d), cur), tab_c, tab_c, tab_p, tab_p],
        [pl.BlockSpec((WINDOW, n_out), late), _full((1, n_out)), _full((8, LANES))],
        [_sds((t, n_out), ACT), _sds((1, n_out), F32), _sds((8, LANES), F32)],
        (sinks, q, kd_, kd_, vd, vd, do, cos, sin, cos, sin), ("arbitrary",),
        scratch=[pltpu.VMEM((WINDOW, d), ACT), pltpu.VMEM((WINDOW, kd), F32), pltpu.VMEM((WINDOW, kd), F32)], jobs=jobs)


def _proj_res(a, w, b, g, x, name, jobs=()):
    t = a.shape[0]
    k, d = w.shape
    tm = _row_tile(t)
    has_bias = b is not None

    def body(*refs):
        a_ref, w_ref = refs[:2]
        b_ref = refs[2] if has_bias else None
        g_ref, x_ref, m_ref, xo_ref = refs[2 + has_bias:]
        m = _dot(a_ref[...], w_ref[...])
        if has_bias:
            m = m + b_ref[...]
        m_ref[...] = m.astype(ACT)
        xo_ref[...] = x_ref[...] + (m * _rstd(m)) * g_ref[...]

    ins = [a, w] + ([b] if has_bias else []) + [g, x]
    specs = [_rows(tm, k), _full((k, d))] + ([_full((1, d))] if has_bias else []) + [_full((1, d)), _rows(tm, d)]
    return _call(body, name, (t // tm,), specs, [_rows(tm, d), _rows(tm, d)], [_sds((t, d), ACT), _sds((t, d), F32)], ins,
                 ("parallel",), jobs=jobs)


def _post_bwd(dres, m, g, w, mode, name, jobs=()):
    t, d = dres.shape
    k = w.shape[0]
    tm = _row_tile(t)
    kc = _pick(k, (1408, 1024, 512))

    def body(*refs):
        d_ref, m_ref, g_ref, w_ref = refs[:4]
        outs = refs[4:]
        dm_ref, da_ref, dg_ref = outs[:3]
        i = pl.program_id(0)

        @pl.when(i == 0)
        def _():
            dg_ref[...] = jnp.zeros_like(dg_ref)
            if mode == "attn":
                outs[3][...] = jnp.zeros_like(outs[3])

        dv, mv = d_ref[...], m_ref[...].astype(F32)
        r = _rstd(mv)
        mh = mv * r
        dg_ref[...] += jnp.sum(dv * mh, axis=0, keepdims=True)
        dm = _rms_bwd(mh, r, g_ref[...], dv)
        dmb = dm.astype(ACT)
        dm_ref[...] = dmb
        if mode == "attn":
            outs[3][...] += jnp.sum(dm, axis=0, keepdims=True)
        for c in range(0, k, kc):
            da = _dot_nt(dmb, w_ref[c:c + kc, :])
            da_ref[:, c:c + kc] = da.astype(ACT)

    ins = [dres, m, g, w]
    specs = [_rows(tm, d), _rows(tm, d), _full((1, d)), _full((k, d))]
    out_specs = [_rows(tm, d), _rows(tm, k), _full((1, d))]
    out_shape = [_sds((t, d), ACT), _sds((t, k), ACT), _sds((1, d), F32)]
    if mode == "attn":
        out_specs.append(_full((1, d)))
        out_shape.append(_sds((1, d), F32))
    return _call(body, name, (t // tm,), specs, out_specs, out_shape, ins, ("arbitrary",), jobs=jobs)


def _pre_bwd(dy, w, x, g, dres, name, transposed=False, jobs=()):
    t, d = x.shape
    tm = _row_tile(t)

    def body(dy_ref, w_ref, x_ref, g_ref, dres_ref, dx_ref, dg_ref):
        @pl.when(pl.program_id(0) == 0)
        def _():
            dg_ref[...] = jnp.zeros_like(dg_ref)

        dh = jnp.zeros((tm, d), F32)
        if transposed:
            dh = dh + _dot(dy_ref[...], w_ref[...])
        elif w.ndim == 3:
            c = w.shape[2]
            for j in range(w.shape[0]):
                dh = dh + _dot_nt(dy_ref[:, j * c:(j + 1) * c], w_ref[j])
        else:
            n = w.shape[1]
            nc = _pick(n, (1408, 1024, 512))
            for c in range(0, n, nc):
                dh = dh + _dot_nt(dy_ref[:, c:c + nc], w_ref[:, c:c + nc])
        xv = x_ref[...]
        r = _rstd(xv)
        xh = xv * r
        dg_ref[...] += jnp.sum(dh * xh, axis=0, keepdims=True)
        dx_ref[...] = dres_ref[...] + _rms_bwd(xh, r, g_ref[...], dh)

    return _call(
        body, name, (t // tm,),
        [_rows(tm, dy.shape[1]), _full(w.shape), _rows(tm, d), _full((1, d)), _rows(tm, d)],
        [_rows(tm, d), _full((1, d))], [_sds((t, d), F32), _sds((1, d), F32)],
        (dy, w, x, g, dres), ("arbitrary",), jobs=jobs)


def _ffn_bwd(dres, f, g_post, w_dn, gu, w_gu, x, g_pre, name, head=None, jobs=()):
    t, d = x.shape
    n = w_dn.shape[0]
    tm = _row_tile(t, 256)

    def body(*refs):
        one_ref, two_ref, gp_ref, wd_ref, gu_ref, wu_ref, x_ref, g_ref, df_ref, dgu_ref, dx_ref, dgp_ref, dg_ref = refs[:13]

        @pl.when(pl.program_id(0) == 0)
        def _():
            for ref in (dgp_ref, dg_ref) + tuple(refs[13:]):
                ref[...] = jnp.zeros_like(ref)

        xv = x_ref[...]
        if head is None:
            dv, fv = one_ref[...], two_ref[...].astype(F32)
            r = _rstd(fv)
            fh = fv * r
        else:
            fv = _dot(one_ref[...], wd_ref[...])
            r = _rstd(fv)
            fh = fv * r
            err = xv + fh * gp_ref[...] - two_ref[...]
            dv = err * (1.0 / d)
            refs[13][...] += 0.5 * jnp.sum(jnp.mean(err * err, axis=-1, keepdims=True), axis=0, keepdims=True)
        dgp_ref[...] += jnp.sum(dv * fh, axis=0, keepdims=True)
        dfb = _rms_bwd(fh, r, gp_ref[...], dv).astype(ACT)
        df_ref[...] = dfb
        for c, size in _mxu_chunks(n):
            da = _dot_nt(dfb, wd_ref[c:c + size, :]).astype(ACT)
            gate, up = gu_ref[:, c:c + size], gu_ref[:, n + c:n + c + size]
            sg = _sigmoid(gate.astype(F32)).astype(ACT)
            gs = gate * sg
            dgu_ref[:, c:c + size] = da * up * (sg + gs - gs * sg)
            dgu_ref[:, n + c:n + c + size] = da * gs
        dh = _dot(dgu_ref[...], wu_ref[...])
        rx = _rstd(xv)
        xh = xv * rx
        dg_ref[...] += jnp.sum(dh * xh, axis=0, keepdims=True)
        dx_ref[...] = dv + _rms_bwd(xh, rx, g_ref[...], dh)

    first = [dres, f] if head is None else list(head)
    first_specs = [_rows(tm, d), _rows(tm, d)] if head is None else [_rows(tm, n), _rows(tm, d)]
    extra = ([], []) if head is None else ([_full((8, LANES))], [_sds((8, LANES), F32)])
    return _call(
        body, name, (t // tm,),
        first_specs + [_full((1, d)), _resident(w_dn.shape), _rows(tm, 2 * n), _resident(w_gu.shape), _rows(tm, d), _full((1, d))],
        [_rows(tm, d), _rows(tm, 2 * n), _rows(tm, d), _full((1, d)), _full((1, d))] + extra[0],
        [_sds((t, d), ACT), _sds((t, 2 * n), ACT), _sds((t, d), F32), _sds((1, d), F32), _sds((1, d), F32)] + extra[1],
        first + [g_post, w_dn, gu, w_gu, x, g_pre], ("arbitrary",), jobs=jobs)


def _wgrad(a, b, name, out_dtype=F32, out_block=None, transposed=False, jobs=()):
    t = a.shape[0]
    tt = _pick(t, (1024,))
    steps = t // tt
    tk = _pick(a.shape[1], (1024, 1408))
    k, n = a.shape[1], b.shape[1]
    tn = _pick(n, (1024, 1408))
    per = 0
    if transposed:
        out_spec, out_shape, acc_shape = pl.BlockSpec((tn, tk), lambda i, j, s: (j, i)), _sds((n, k), out_dtype), (tn, tk)
    elif out_block is None:
        out_spec, out_shape, acc_shape = pl.BlockSpec((tk, tn), lambda i, j, s: (i, j)), _sds((k, n), out_dtype), (tk, tn)
    else:
        per = tn // out_block
        out_spec = pl.BlockSpec((per, tk, out_block), lambda i, j, s: (j, i, 0))
        out_shape, acc_shape = _sds((n // out_block, k, out_block), out_dtype), (tk, tn)

    def body(a_ref, b_ref, o_ref, acc_ref):
        s = pl.program_id(2)

        @pl.when(s == 0)
        def _():
            acc_ref[...] = jnp.zeros_like(acc_ref)

        acc_ref[...] += _dot_tn(b_ref[...], a_ref[...]) if transposed else _dot_tn(a_ref[...], b_ref[...])

        @pl.when(s == steps - 1)
        def _():
            if per >= 1:
                for p in range(per):
                    o_ref[p] = acc_ref[:, p * out_block:(p + 1) * out_block].astype(out_dtype)
            else:
                o_ref[...] = acc_ref[...].astype(out_dtype)

    res, job_res = _call(
        body, name, (k // tk, n // tn, steps),
        [pl.BlockSpec((tt, tk), lambda i, j, s: (s, i)), pl.BlockSpec((tt, tn), lambda i, j, s: (s, j))], [out_spec], [out_shape],
        (a, b), ("parallel", "parallel", "arbitrary"), scratch=[pltpu.VMEM(acc_shape, F32)], jobs=jobs)
    return res[0], job_res


def _ffn_up_fwd(x, g, w, jobs=()):
    t, d = x.shape
    n = w.shape[0] // 2
    tm = _row_tile(t)

    def body(x_ref, g_ref, w_ref, h_ref, gu_ref, a_ref):
        xv = x_ref[...]
        hb = (xv * _rstd(xv) * g_ref[...]).astype(ACT)
        h_ref[...] = hb
        for c, size in _mxu_chunks(n):
            gate = _dot_nt(hb, w_ref[c:c + size, :])
            up = _dot_nt(hb, w_ref[n + c:n + c + size, :])
            gu_ref[:, c:c + size] = gate.astype(ACT)
            gu_ref[:, n + c:n + c + size] = up.astype(ACT)
            a_ref[:, c:c + size] = (gate * _sigmoid(gate) * up).astype(ACT)

    return _call(
        body, "ffn_up_fwd", (t // tm,),
        [_rows(tm, d), _full((1, d)), _full(w.shape)],
        [_rows(tm, d), _rows(tm, 2 * n), _rows(tm, n)],
        [_sds((t, d), ACT), _sds((t, 2 * n), ACT), _sds((t, n), ACT)],
        (x, g, w), ("parallel",), jobs=jobs)


def _causal(ws):
    row = lax.broadcasted_iota(jnp.int32, ws.shape, 0)
    col = lax.broadcasted_iota(jnp.int32, ws.shape, 1)
    return jnp.where(col <= row, ws, 0.0)


def _sgu_ln(v, lg, lb):
    mu = jnp.mean(v, axis=-1, keepdims=True)
    vc = v - mu
    rs = lax.rsqrt(jnp.mean(vc * vc, axis=-1, keepdims=True) + EPS)
    vh = vc * rs
    return vh, rs, vh * lg + lb


def _sgu_fwd(x, g, w, lg, lb, ws, bs_t, jobs=()):
    t, d = x.shape
    nb, _, c = w.shape
    n = nb * c
    groups = d // WINDOW
    tm = _row_tile(t)

    def body(x_ref, g_ref, w_ref, lg_ref, lb_ref, ws_ref, bs_ref, h_ref, z_ref, y_ref, zf_ref):
        xv = x_ref[...]
        hb = (xv * _rstd(xv) * g_ref[...]).astype(ACT)
        h_ref[...] = hb
        for j in range(nb):
            zf_ref[:, j * c:(j + 1) * c] = _dot(hb, w_ref[j])
        z_ref[...] = zf_ref[...].astype(ACT)
        gs = [slice(gi * WINDOW, (gi + 1) * WINDOW) for gi in range(groups)]
        wsg = [_causal(ws_ref[gi]).astype(ACT) for gi in range(groups)]
        for r in range(0, tm, WINDOW):
            u = _gelu(zf_ref[r:r + WINDOW, :d])
            _, _, vn = _sgu_ln(_gelu(zf_ref[r:r + WINDOW, d:]), lg_ref[...], lb_ref[...])
            mixed = [_dot(wsg[gi], vn[:, gs[gi]].astype(ACT)) for gi in range(groups)]
            for gi in range(groups):
                y_ref[r:r + WINDOW, gs[gi]] = (u[:, gs[gi]] * (mixed[gi] + bs_ref[:, gi:gi + 1])).astype(ACT)

    vec = _full((1, d))
    return _call(
        body, "sgu_fwd", (t // tm,),
        [_rows(tm, d), vec, _full((nb, d, c)), vec, vec, _full((groups, WINDOW, WINDOW)), _full((WINDOW, groups))],
        [_rows(tm, d), _rows(tm, n), _rows(tm, d)], [_sds((t, d), ACT), _sds((t, n), ACT), _sds((t, d), ACT)],
        (x, g, w, lg, lb, ws, bs_t), ("parallel",), scratch=[pltpu.VMEM((tm, n), F32)], jobs=jobs)


def _sgu_mix_bwd(z, dy, lg, lb, ws, bs_t, jobs=()):
    t, n = z.shape
    d = n // 2
    groups = d // WINDOW
    tm = _row_tile(t)

    def body(z_ref, dy_ref, lg_ref, lb_ref, ws_ref, bs_ref, dz_ref, dlg_ref, dlb_ref, dws_ref, dbs_ref):
        @pl.when(pl.program_id(0) == 0)
        def _():
            for ref in (dlg_ref, dlb_ref, dws_ref, dbs_ref):
                ref[...] = jnp.zeros_like(ref)

        lane = lax.broadcasted_iota(jnp.int32, (WINDOW, groups), 1)
        gs = [slice(gi * WINDOW, (gi + 1) * WINDOW) for gi in range(groups)]
        wsg = [_causal(ws_ref[gi]).astype(ACT) for gi in range(groups)]
        for c in range(0, tm, WINDOW):
            rows = slice(c, c + WINDOW)
            zu, zv = z_ref[rows, :d].astype(F32), z_ref[rows, d:].astype(F32)
            u = _gelu(zu)
            vh, rs, vn = _sgu_ln(_gelu(zv), lg_ref[...], lb_ref[...])
            dyv = dy_ref[rows, :].astype(F32)
            vng = [vn[:, gs[gi]].astype(ACT) for gi in range(groups)]
            dmix = dyv * u
            dmb = [dmix[:, gs[gi]].astype(ACT) for gi in range(groups)]
            mixed = [_dot(wsg[gi], vng[gi]) for gi in range(groups)]
            dvn_parts = [_dot_tn(wsg[gi], dmb[gi]) for gi in range(groups)]
            dws_parts = [_dot_nt(dmb[gi], vng[gi]) for gi in range(groups)]
            for gi in range(groups):
                dz_ref[rows, gs[gi]] = (dyv[:, gs[gi]] * (mixed[gi] + bs_ref[:, gi:gi + 1]) * _gelu_grad(zu[:, gs[gi]])).astype(ACT)
                dws_ref[:, gs[gi]] += _causal(dws_parts[gi])
                dbs_ref[...] += jnp.where(lane == gi, jnp.sum(dmix[:, gs[gi]], axis=-1, keepdims=True), 0.0)
            dvn = jnp.concatenate(dvn_parts, axis=-1)
            dlg_ref[...] += jnp.sum(dvn * vh, axis=0, keepdims=True)
            dlb_ref[...] += jnp.sum(dvn, axis=0, keepdims=True)
            dvh = dvn * lg_ref[...]
            dv = rs * (dvh - jnp.mean(dvh, axis=-1, keepdims=True) - vh * jnp.mean(dvh * vh, axis=-1, keepdims=True))
            dz_ref[rows, d:] = (dv * _gelu_grad(zv)).astype(ACT)

    vec = _full((1, d))
    return _call(
        body, "sgu_mix_bwd", (t // tm,),
        [_rows(tm, n), _rows(tm, d), vec, vec, _full((groups, WINDOW, WINDOW)), _full((WINDOW, groups))],
        [_rows(tm, n), vec, vec, _full((WINDOW, d)), _full((WINDOW, groups))],
        [_sds((t, n), ACT), _sds((1, d), F32), _sds((1, d), F32), _sds((WINDOW, d), F32), _sds((WINDOW, groups), F32)],
        (z, dy, lg, lb, ws, bs_t), ("arbitrary",), jobs=jobs)


AG_COPIES = 8


def _prepare(sources, dtypes, n_gather, name):
    n = len(sources)
    arrays, where = [], []
    for parts, layer in sources:
        found = []
        for part in parts:
            known = [i for i, v in enumerate(arrays) if v is part]
            if not known:
                arrays.append(part)
            found.append(known[0] if known else len(arrays) - 1)
        where.append((found, layer))
    shapes = [(sum(p.shape[1] for p in parts), parts[0].shape[2]) for parts, _ in sources]

    def body(*refs):
        ins, refs = refs[:len(arrays)], refs[len(arrays):]
        stage, outs = refs[:n], refs[n:n + n_gather]
        send, recv, local_sem = refs[n + n_gather:]
        x, y, c = _place()
        me, sibling, beside_x, beside_y, far = (x, y, c), (x, y, 1 - c), (1 - x, y, c), (x, 1 - y, c), (1 - x, 1 - y, c)
        slot = lambda dev: 4 * dev[0] + 2 * dev[1] + dev[2]
        other = lambda dev: (dev[0], dev[1], 1 - c)
        whole = lambda a: (0, shapes[a][0])
        cuts = [rows // 2 if rows % 32 == 0 else rows for rows, _ in shapes]
        first = lambda a: (0, cuts[a])
        rest = lambda a: (cuts[a], shapes[a][0] - cuts[a])

        def copy(a, k, block, rows, to, src=None):
            dst = outs[a].at[block, pl.ds(*rows)]
            return pltpu.make_async_remote_copy(
                src_ref=dst if src is None else src, dst_ref=dst, send_sem=send.at[a * AG_COPIES + k],
                recv_sem=recv.at[a * AG_COPIES + k], device_id=to, device_id_type=MESH)

        def cast(a):
            found, layer = where[a]
            at = 0
            for i in found:
                rows = arrays[i].shape[1]
                stage[a][at:at + rows, :] = ins[i][layer].astype(dtypes[a])
                at += rows

        for a in range(n_gather):
            cast(a)
        started = []
        for a in range(n_gather):
            own = pltpu.make_async_copy(stage[a], outs[a].at[slot(me)], local_sem.at[a])
            own.start()
            started.append(own)
        sends = []
        for a in range(n_gather):
            sends += [copy(a, k, slot(me), whole(a), to, src=stage[a]) for k, to in enumerate((sibling, beside_x, beside_y))]
        for cp in sends:
            cp.start()
        for a in range(n_gather, n):
            cast(a)
        for a in range(n_gather):
            copy(a, 1, slot(beside_x), whole(a), me).wait_recv()
            copy(a, 2, slot(beside_y), whole(a), me).wait_recv()
            passed = [copy(a, 3, slot(beside_x), first(a), beside_y), copy(a, 5, slot(beside_x), whole(a), sibling),
                      copy(a, 6, slot(beside_y), whole(a), sibling)]
            if rest(a)[1]:
                passed.append(copy(a, 4, slot(beside_y), rest(a), beside_x))
            for cp in passed:
                cp.start()
            sends += passed
        for a in range(n_gather):
            copy(a, 3, slot(far), first(a), me).wait_recv()
            if rest(a)[1]:
                copy(a, 4, slot(far), rest(a), me).wait_recv()
            passed = copy(a, 7, slot(far), whole(a), sibling)
            passed.start()
            sends.append(passed)
        for a in range(n_gather):
            for k, source in ((0, me), (5, beside_x), (6, beside_y), (7, far)):
                copy(a, k, slot(other(source)), whole(a), me).wait_recv()
        for cp in sends:
            cp.wait_send()
        for own in started:
            own.wait()

    res = pl.pallas_call(
        body, name=name,
        in_specs=[IN_VMEM] * len(arrays), out_specs=[IN_VMEM] * n + [ANY] * n_gather,
        out_shape=[_sds(s, dt) for s, dt in zip(shapes, dtypes)]
        + [_sds((N_DEV,) + s, dt) for s, dt in zip(shapes[:n_gather], dtypes)],
        scratch_shapes=[pltpu.SemaphoreType.DMA((n_gather * AG_COPIES,)), pltpu.SemaphoreType.DMA((n_gather * AG_COPIES,)),
                        pltpu.SemaphoreType.DMA((n_gather,))],
        compiler_params=pltpu.CompilerParams(vmem_limit_bytes=VMEM_LIMIT_BYTES),
    )(*arrays)
    return list(res[:n]), list(res[n:])


def _pair_sum(g, r, core, name):
    _, _, rows, cols = g.shape
    tr = _pick(rows, (512, 256, 128))

    def body(core_ref, g_ref, r_ref, p_ref):
        del core_ref
        p_ref[...] = (g_ref[...].astype(F32) + r_ref[...].astype(F32)).astype(p_ref.dtype)

    return pl.pallas_call(
        body, name=name,
        grid_spec=pltpu.PrefetchScalarGridSpec(
            num_scalar_prefetch=1, grid=(4, rows // tr),
            in_specs=[pl.BlockSpec((None, None, tr, cols), lambda q, i, core_ref: (q, core_ref[0], i, 0)),
                      pl.BlockSpec((None, tr, cols), lambda q, i, core_ref: (q, i, 0))],
            out_specs=pl.BlockSpec((None, tr, cols), lambda q, i, core_ref: (q, i, 0))),
        out_shape=_sds((4, rows, cols), g.dtype),
        compiler_params=_params("parallel", "parallel"),
    )(core, g, r)


def _adam(w, g, m, v):
    m2 = ADAM_B1 * m + (1.0 - ADAM_B1) * g
    v2 = ADAM_B2 * v + (1.0 - ADAM_B2) * (g * g)
    m_hat = m2 / (1.0 - ADAM_B1 ** ADAM_STEP)
    v_hat = v2 / (1.0 - ADAM_B2 ** ADAM_STEP)
    return -ADAM_LR * (m_hat / (jnp.sqrt(v_hat) + ADAM_EPS) + ADAM_WD * w), m2, v2


def _shard_update(own, index, received, w, m, v, layer, so_far, name):
    _, rows, cols = w.shape
    n_recv = received.shape[0]
    tr = _shard_tile(rows)

    def body(index_ref, p_ref, q_ref, w_ref, m_ref, v_ref, *rest):
        del index_ref
        g_out, d_out, m_out, v_out = rest[-4:]
        g = p_ref[...].astype(F32)
        for s in range(n_recv):
            g = g + q_ref[s].astype(F32)
        g_out[...] = g
        d_out[...], m_out[...], v_out[...] = _adam(w_ref[...], g, m_ref[...], v_ref[...])

    tile = pl.BlockSpec((None, tr, cols), lambda i, index_ref: (layer, i, 0))
    kept = list(so_far) if so_far is not None else []
    return pl.pallas_call(
        body, name=name,
        grid_spec=pltpu.PrefetchScalarGridSpec(
            num_scalar_prefetch=1, grid=(rows // tr,),
            in_specs=[pl.BlockSpec((None, tr, cols), lambda i, index_ref: (index_ref[0], i, 0)),
                      pl.BlockSpec((n_recv, tr, cols), lambda i, index_ref: (0, i, 0)), tile, tile, tile] + [ANY] * len(kept),
            out_specs=[tile] * 4),
        out_shape=[_sds(w.shape, F32)] * 4,
        input_output_aliases={6 + i: i for i in range(len(kept))},
        compiler_params=_params("parallel"),
    )(index, own, received, w, m, v, *kept)


def _natural_pieces(shape, r, c):
    if tuple(shape[-2:]) == (r, c) and all(n == 1 for n in shape[:-2]):
        return [((0,) * (len(shape) - 2), (0, c))]
    groups, width = shape[1], shape[3]
    assert len(shape) == 4 and shape[0] == 1 and shape[2] == r and groups * width == c, (shape, r, c)
    return [((0, g), (g * width, width)) for g in range(groups)]


def _replicated_update(shares, where, params, name):
    flat = [a for p in params if p is not None for a in p]

    def body(s_ref, *refs):
        ins, outs = refs[:len(flat)], refs[len(flat):]
        total = s_ref[0]
        for dev in range(1, N_DEV):
            total = total + s_ref[dev]
        i_at = o_at = 0
        for ranges, p in zip(where, params):
            chunks = [total[r0:r0 + r, :c] for r0, r, c in ranges]
            g2d = chunks[0] if len(chunks) == 1 else jnp.concatenate(chunks, axis=1)
            if p is None:
                outs[o_at][...] = g2d
                o_at += 1
                continue
            w_ref, m_ref, v_ref = ins[i_at:i_at + 3]
            for idx, (c0, cols) in _natural_pieces(p[0].shape, *g2d.shape):
                at = idx + (slice(None), slice(None))
                g = g2d[:, c0:c0 + cols]
                outs[o_at][at] = g
                outs[o_at + 1][at], outs[o_at + 2][at], outs[o_at + 3][at] = _adam(w_ref[at], g, m_ref[at], v_ref[at])
            i_at, o_at = i_at + 3, o_at + 4

    out_shape = []
    for ranges, p in zip(where, params):
        out_shape += [_sds((ranges[0][1], sum(c for _, _, c in ranges)), F32)] if p is None else [_sds(p[0].shape, F32)] * 4
    res = pl.pallas_call(
        body, name=name, out_shape=out_shape, compiler_params=pltpu.CompilerParams(vmem_limit_bytes=VMEM_LIMIT_BYTES),
    )(shares, *flat)
    out, at = [], 0
    for p in params:
        out.append(list(res[at:at + (1 if p is None else 4)]))
        at += 1 if p is None else 4
    return out


def _rope_tables(t):
    half = HEAD_DIM // 2
    inv_freq = ROPE_THETA ** (-(jnp.arange(half, dtype=F32) * 2.0) / HEAD_DIM)
    ang = jnp.arange(t, dtype=jnp.int32).astype(F32)[:, None] * inv_freq[None, :]
    cos, sin = jnp.cos(ang), jnp.sin(ang)
    return jnp.tile(jnp.concatenate([cos, cos], axis=1), (1, 2)), jnp.tile(jnp.concatenate([-sin, sin], axis=1), (1, 2))


def _rows_full(gathered):
    return gathered.reshape(-1, gathered.shape[-1])


def _rows_blocked(full):
    return full.reshape(N_DEV, full.shape[0] // N_DEV, full.shape[1]).astype(ACT)


def _pack_rows(parts, width):
    rows, where, at = [], [], 0
    for v in parts:
        r, c = v.shape
        n_rows = -(-r // 8) * 8
        chunks = []
        for c0 in range(0, c, width):
            chunk = v[:, c0:c0 + width].astype(F32)
            rows.append(jnp.pad(chunk, ((0, n_rows - r), (0, width - chunk.shape[1]))))
            chunks.append((at, r, chunk.shape[1]))
            at += n_rows
        where.append(chunks)
    return jnp.concatenate(rows, axis=0), where


def _halves(block):
    half = block.shape[0] // 2
    return (0, half), (half, half)


def _whole(block):
    return (0, block.shape[0])


EARLY = ("attn_w_qkv", "sgu_ln", "attn_w_o")
LATE = ("sgu_w_in", "sgu_w_out", "gu0", "gu1", "dn0", "dn1")
SWAPPED = ("attn_w_qkv", "ffn_w_gate_up")
BEFORE_ATTN = ("norm_mix_post", "norm_ffn_pre", "norm_ffn_post", "attn_b_o", "sgu_w_spatial", "sgu_b_spatial")
AFTER_ATTN = ("attn_b_qkv", "attn_sinks")
LAST = ("norm_mix_pre",)
WEIGHTS = ("norm_mix_pre", "norm_mix_post", "norm_ffn_pre", "norm_ffn_post", "attn_w_qkv", "attn_b_qkv", "attn_sinks", "attn_w_o",
           "attn_b_o", "sgu_w_in", "sgu_ln_g", "sgu_ln_b", "sgu_w_spatial", "sgu_b_spatial", "sgu_w_out", "ffn_w_gate_up", "ffn_w_down")


LAYER_OF = {"gu0": ("ffn_w_gate_up", 0), "gu1": ("ffn_w_gate_up", 1), "dn0": ("ffn_w_down", 0), "dn1": ("ffn_w_down", 1)}


def _source(tree, name):
    if name == "sgu_ln":
        return [tree["sgu_ln_g"][None], tree["sgu_ln_b"][None]], 0
    leaf, layer = LAYER_OF.get(name, (name, 0))
    return [tree[leaf]], layer


def kernel(x, norm_mix_pre, norm_mix_post, norm_ffn_pre, norm_ffn_post, attn_w_qkv, attn_b_qkv, attn_sinks, attn_w_o, attn_b_o, sgu_w_in, sgu_ln_g, sgu_ln_b, sgu_w_spatial, sgu_b_spatial, sgu_w_out, ffn_w_gate_up, ffn_w_down, loss_target, m_norm_mix_pre, m_norm_mix_post, m_norm_ffn_pre, m_norm_ffn_post, m_attn_w_qkv, m_attn_b_qkv, m_attn_sinks, m_attn_w_o, m_attn_b_o, m_sgu_w_in, m_sgu_ln_g, m_sgu_ln_b, m_sgu_w_spatial, m_sgu_b_spatial, m_sgu_w_out, m_ffn_w_gate_up, m_ffn_w_down, v_norm_mix_pre, v_norm_mix_post, v_norm_ffn_pre, v_norm_ffn_post, v_attn_w_qkv, v_attn_b_qkv, v_attn_sinks, v_attn_w_o, v_attn_b_o, v_sgu_w_in, v_sgu_ln_g, v_sgu_ln_b, v_sgu_w_spatial, v_sgu_b_spatial, v_sgu_w_out, v_ffn_w_gate_up, v_ffn_w_down):
    w = dict(norm_mix_pre=norm_mix_pre, norm_mix_post=norm_mix_post, norm_ffn_pre=norm_ffn_pre, norm_ffn_post=norm_ffn_post,
             attn_w_qkv=attn_w_qkv, attn_b_qkv=attn_b_qkv, attn_sinks=attn_sinks, attn_w_o=attn_w_o, attn_b_o=attn_b_o,
             sgu_w_in=sgu_w_in, sgu_ln_g=sgu_ln_g, sgu_ln_b=sgu_ln_b, sgu_w_spatial=sgu_w_spatial, sgu_b_spatial=sgu_b_spatial,
             sgu_w_out=sgu_w_out, ffn_w_gate_up=ffn_w_gate_up, ffn_w_down=ffn_w_down)
    mom = dict(norm_mix_pre=m_norm_mix_pre, norm_mix_post=m_norm_mix_post, norm_ffn_pre=m_norm_ffn_pre, norm_ffn_post=m_norm_ffn_post,
               attn_w_qkv=m_attn_w_qkv, attn_b_qkv=m_attn_b_qkv, attn_sinks=m_attn_sinks, attn_w_o=m_attn_w_o, attn_b_o=m_attn_b_o,
               sgu_w_in=m_sgu_w_in, sgu_ln_g=m_sgu_ln_g, sgu_ln_b=m_sgu_ln_b, sgu_w_spatial=m_sgu_w_spatial,
               sgu_b_spatial=m_sgu_b_spatial, sgu_w_out=m_sgu_w_out, ffn_w_gate_up=m_ffn_w_gate_up, ffn_w_down=m_ffn_w_down)
    var = dict(norm_mix_pre=v_norm_mix_pre, norm_mix_post=v_norm_mix_post, norm_ffn_pre=v_norm_ffn_pre, norm_ffn_post=v_norm_ffn_post,
               attn_w_qkv=v_attn_w_qkv, attn_b_qkv=v_attn_b_qkv, attn_sinks=v_attn_sinks, attn_w_o=v_attn_w_o, attn_b_o=v_attn_b_o,
               sgu_w_in=v_sgu_w_in, sgu_ln_g=v_sgu_ln_g, sgu_ln_b=v_sgu_ln_b, sgu_w_spatial=v_sgu_w_spatial,
               sgu_b_spatial=v_sgu_b_spatial, sgu_w_out=v_sgu_w_out, ffn_w_gate_up=v_ffn_w_gate_up, ffn_w_down=v_ffn_w_down)
    w, mom, var = [{**tree, **{n: jnp.swapaxes(tree[n], 1, 2) for n in SWAPPED}} for tree in (w, mom, var)]
    xs, target = x[0], loss_target[0]
    t, d = xs.shape
    row = lambda v: v.reshape(1, -1).astype(F32)
    core = lax.axis_index("c").astype(jnp.int32).reshape(1)
    chip = (2 * lax.axis_index("x") + lax.axis_index("y")).astype(jnp.int32).reshape(1)
    names = EARLY + LATE
    staged, early = _prepare([_source(w, n) for n in names], [F32 if n == "sgu_ln" else ACT for n in names], len(EARLY),
                             "weights_prepare")
    stage = dict(zip(names, staged))
    w_qkv = _rows_full(early[0])
    lg, lb = row(early[1][:, 0, :]), row(early[1][:, 1, :])
    w_o = _rows_full(early[2])
    b_qkv = row(attn_b_qkv)
    sinks = attn_sinks.reshape(1, -1).astype(F32)
    ws = sgu_w_spatial[0].astype(F32)
    bs_t = sgu_b_spatial[0].astype(F32).T
    cos, sin = _rope_tables(t)
    gather = lambda name, so_far, **pieces: _gather_job(stage[name], so_far, **pieces)
    a_half = lambda name: _halves(stage[name])[0]
    b_half = lambda name: _halves(stage[name])[1]
    whole = lambda name: _whole(stage[name])

    def pieces(name, n):
        rows = stage[name].shape[0] // n
        return [(i * rows, rows) for i in range(n)]

    ways = lambda parts: [(piece, i % 2) for i, piece in enumerate(parts)]
    relay = lambda name, so_far, **steps: _relay_gather_job(stage[name], so_far, **steps)
    gu0_parts, gu1_parts = pieces("gu0", 4), pieces("gu1", 4)
    dn0_parts, dn1_parts, in_parts, out_parts = pieces("dn0", 2), pieces("dn1", 2), pieces("sgu_w_in", 2), pieces("sgu_w_out", 2)
    (h0, q, kdup, vdup), ((g_gu0,),) = _attn_qkv_fwd(
        xs, row(norm_mix_pre[0]), w_qkv, b_qkv, cos, sin, jobs=[relay("gu0", None, sends=gu0_parts)])
    (o,), ((g_gu0,), (g_dn0,)) = _attn_fwd(q, kdup, vdup, sinks, jobs=[
        relay("gu0", g_gu0, relays=ways(gu0_parts), forwards=gu0_parts), relay("dn0", None, sends=dn0_parts)])
    (m0, x1), ((g_gu0,), (g_dn0,), (g_in,)) = _proj_res(o, w_o, row(attn_b_o), row(norm_mix_post[0]), xs, "attn_out_fwd", jobs=[
        relay("gu0", g_gu0, far=gu0_parts), relay("dn0", g_dn0, relays=ways(dn0_parts), forwards=dn0_parts),
        relay("sgu_w_in", None, sends=in_parts)])
    w_gu0 = _rows_full(g_gu0)
    (h1, gu0, a0), ((g_dn0,), (g_in,), (g_out,), (g_gu1,)) = _ffn_up_fwd(x1, row(norm_ffn_pre[0]), w_gu0, jobs=[
        relay("dn0", g_dn0, far=dn0_parts), relay("sgu_w_in", g_in, relays=ways(in_parts), forwards=in_parts),
        relay("sgu_w_out", None, sends=out_parts), relay("gu1", None, sends=gu1_parts[:2])])
    w_dn0 = _rows_full(g_dn0)
    (f0, x2), ((g_in,), (g_out,), (g_gu1,)) = _proj_res(a0, w_dn0, None, row(norm_ffn_post[0]), x1, "ffn_down_fwd0", jobs=[
        relay("sgu_w_in", g_in, far=in_parts), relay("sgu_w_out", g_out, relays=ways(out_parts), forwards=out_parts),
        relay("gu1", g_gu1, sends=gu1_parts[2:], relays=ways(gu1_parts[:2]), forwards=gu1_parts[:2])])
    w_in = g_in
    (h2, z, y), ((g_out,), (g_gu1,), (g_dn1,)) = _sgu_fwd(x2, row(norm_mix_pre[1]), w_in, lg, lb, ws, bs_t, jobs=[
        relay("sgu_w_out", g_out, far=out_parts),
        relay("gu1", g_gu1, relays=ways(gu1_parts[2:]), forwards=gu1_parts[2:], far=gu1_parts[:2]),
        relay("dn1", None, sends=dn1_parts)])
    w_out = _rows_full(g_out)
    (m1, x3), ((g_gu1,), (g_dn1,)) = _proj_res(y, w_out, None, row(norm_mix_post[1]), x2, "sgu_out_fwd", jobs=[
        relay("gu1", g_gu1, far=gu1_parts[2:]), relay("dn1", g_dn1, relays=ways(dn1_parts), forwards=dn1_parts)])
    w_gu1 = _rows_full(g_gu1)
    (h3, gu1, a1), ((g_dn1,),) = _ffn_up_fwd(x3, row(norm_ffn_pre[1]), w_gu1, jobs=[relay("dn1", g_dn1, far=dn1_parts)])
    w_dn1 = _rows_full(g_dn1)

    to_sibling = lambda g: _scatter_job([g], 4, _to_sibling)
    pair = lambda g, r, name: _pair_sum(g.reshape((4, 2) + g.shape[1:]), r, core, "pair_sum_" + name)
    to_chips = lambda p, prev=None, piece=None: _scatter_job([p], 3, _to_owner_chip, prev=prev, piece=piece)
    out_g, out_d, out_m, out_v = {}, {}, {}, {}

    trees = [{**tree, "sgu_ln": jnp.stack([tree["sgu_ln_g"], tree["sgu_ln_b"]], axis=1)} for tree in (w, mom, var)]
    so_far = {}

    def update(name, own, index, received):
        leaf, layer = LAYER_OF.get(name, (name, 0))
        so_far[leaf] = _shard_update(own, index, received, *[tree[leaf] for tree in trees], layer, so_far.get(leaf), "update_" + name)
        for out, val in zip((out_g, out_d, out_m, out_v), so_far[leaf]):
            out[leaf] = val

    def finish(names, n_extra, shares, where, name):
        res = _replicated_update(shares, where, [(w[n], mom[n], var[n]) for n in names] + [None] * n_extra, name)
        for n, vals in zip(names, res):
            out_g[n], out_d[n], out_m[n], out_v[n] = vals
        return [vals[0] for vals in res[len(names):]]

    first_half = lambda p: _halves(p[0])[0]
    second_half = lambda p: _halves(p[0])[1]
    (df1, dgu1, dx3, dg_ffn_post1, dg_ffn_pre1, loss_tile), _ = _ffn_bwd(
        None, None, row(norm_ffn_post[1]), w_dn1, gu1, w_gu1, x3, row(norm_ffn_pre[1]), "ffn_last_bwd1", head=(a1, target))
    b_gu1, _ = _wgrad(h3, dgu1, "ffn_up_wgrad1", ACT, transposed=True)
    b_gu1 = _rows_blocked(b_gu1)
    dw_dn1, _ = _wgrad(a1, df1, "ffn_down_wgrad1", ACT)
    b_dn1 = _rows_blocked(dw_dn1)
    (dm1, dy, dg_mix_post1), ((r_gu1,), (r_dn1,)) = _post_bwd(
        dx3, m1, row(norm_mix_post[1]), w_out, "sgu", "sgu_out_bwd", jobs=[to_sibling(b_gu1), to_sibling(b_dn1)])
    p_gu1, p_dn1 = pair(b_gu1, r_gu1, "gu1"), pair(b_dn1, r_dn1, "dn1")
    dw_out, _ = _wgrad(y, dm1, "sgu_out_wgrad", ACT)
    b_out = _rows_blocked(dw_out)
    (dz, dlg, dlb, dws, dbs_t), ((q_gu1,), (r_out,)) = _sgu_mix_bwd(z, dy, lg, lb, ws, bs_t, jobs=[
        to_chips(p_gu1, piece=first_half(p_gu1)), to_sibling(b_out)])
    p_out = pair(b_out, r_out, "sgu_w_out")
    b_in, _ = _wgrad(h2, dz, "sgu_in_wgrad", ACT, out_block=stage["sgu_w_in"].shape[1])
    b_ln = jnp.stack([dlg.reshape(N_DEV, -1), dlb.reshape(N_DEV, -1)], axis=1)
    (dx2, dg_mix_pre1), _ = _pre_bwd(dz, w_in, x2, row(norm_mix_pre[1]), dx3, "sgu_in_bwd")
    (df0, dgu0, dx1, dg_ffn_post0, dg_ffn_pre0), ((q_gu1,), (q_dn1,), (q_out,), (r_in,), (r_ln,)) = _ffn_bwd(
        dx2, f0, row(norm_ffn_post[0]), w_dn0, gu0, w_gu0, x1, row(norm_ffn_pre[0]), "ffn_bwd0",
        jobs=[to_chips(p_gu1, prev=[q_gu1], piece=second_half(p_gu1)), to_chips(p_dn1), to_chips(p_out), to_sibling(b_in),
              to_sibling(b_ln)])
    update("gu1", p_gu1, chip, q_gu1)
    update("dn1", p_dn1, chip, q_dn1)
    update("sgu_w_out", p_out, chip, q_out)
    p_in, p_ln = pair(b_in, r_in, "sgu_w_in"), pair(b_ln, r_ln, "sgu_ln")
    b_gu0, ((q_in,), (q_ln,)) = _wgrad(h1, dgu0, "ffn_up_wgrad0", ACT, transposed=True, jobs=[to_chips(p_in), to_chips(p_ln)])
    update("sgu_w_in", p_in, chip, q_in)
    update("sgu_ln", p_ln, chip, q_ln)
    b_gu0 = _rows_blocked(b_gu0)
    (dm0, do, dg_mix_post0, db_o), ((r_gu0,),) = _post_bwd(
        dx1, m0, row(norm_mix_post[0]), w_o, "attn", "attn_out_bwd", jobs=[to_sibling(b_gu0)])
    p_gu0 = pair(b_gu0, r_gu0, "gu0")
    dw_dn0, ((q_gu0,),) = _wgrad(a0, df0, "ffn_down_wgrad0", ACT, jobs=[to_chips(p_gu0, piece=first_half(p_gu0))])
    b_dn0 = _rows_blocked(dw_dn0)
    dw_o, ((r_dn0,),) = _wgrad(o, dm0, "attn_out_wgrad", ACT, jobs=[to_sibling(b_dn0)])
    p_dn0 = pair(b_dn0, r_dn0, "dn0")
    b_o = _rows_blocked(dw_o)
    grads = {
        "norm_mix_post": jnp.concatenate([dg_mix_post0, dg_mix_post1], axis=0),
        "norm_ffn_pre": jnp.concatenate([dg_ffn_pre0, dg_ffn_pre1], axis=0),
        "norm_ffn_post": jnp.concatenate([dg_ffn_post0, dg_ffn_post1], axis=0),
        "attn_b_o": db_o,
        "sgu_w_spatial": dws,
        "sgu_b_spatial": dbs_t.T,
    }
    shares1, where1 = _pack_rows([grads[name] for name in BEFORE_ATTN], d)
    (dqkv, db_qkv, dsink), ((q_gu0,), (q_dn0,), (r_o,), (g_shares1,)) = _attn_bwd(q, kdup, vdup, do, sinks, cos, sin, jobs=[
        to_chips(p_gu0, prev=[q_gu0], piece=second_half(p_gu0)), to_chips(p_dn0), to_sibling(b_o),
        _gather_job(shares1, None, sends=[_whole(shares1)])])
    update("gu0", p_gu0, chip, q_gu0)
    update("dn0", p_dn0, chip, q_dn0)
    p_o = pair(b_o, r_o, "attn_w_o")
    grads.update({"attn_b_qkv": db_qkv, "attn_sinks": dsink[:1, :d // HEAD_DIM]})
    shares2, where2 = _pack_rows([grads[name] for name in AFTER_ATTN] + [loss_tile[:1, :1]], d)
    (dx0, dg_mix_pre0), _ = _pre_bwd(dqkv, w_qkv, xs, row(norm_mix_pre[0]), dx1, "attn_qkv_bwd", True)
    grads["norm_mix_pre"] = jnp.concatenate([dg_mix_pre0, dg_mix_pre1], axis=0)
    shares3, where3 = _pack_rows([grads[name] for name in LAST], d)
    dw_qkv, ((q_o,), (g_shares1,), (g_shares2,), (g_shares3,)) = _wgrad(h0, dqkv, "attn_qkv_wgrad", ACT, transposed=True, jobs=[
        to_chips(p_o), _gather_job(shares1, g_shares1, forwards=[_whole(shares1)]),
        _gather_job(shares2, None, sends=[_whole(shares2)]), _gather_job(shares3, None, sends=[_whole(shares3)])])
    update("attn_w_o", p_o, chip, q_o)
    b_qkv_grad = _rows_blocked(dw_qkv)
    (r_qkv,), (g_shares2,), (g_shares3,) = _copies_only([
        to_sibling(b_qkv_grad), _gather_job(shares2, g_shares2, forwards=[_whole(shares2)]),
        _gather_job(shares3, g_shares3, forwards=[_whole(shares3)])], "grads_tail_to_sibling")
    p_qkv = pair(b_qkv_grad, r_qkv, "attn_w_qkv")
    (q_qkv,), = _copies_only([to_chips(p_qkv)], "grads_tail_to_owner_chip")
    update("attn_w_qkv", p_qkv, chip, q_qkv)

    finish(BEFORE_ATTN, 0, g_shares1, where1, "replicated_update_before_attn")
    loss = finish(AFTER_ATTN, 1, g_shares2, where2, "replicated_update_after_attn")[0][0, 0]
    finish(LAST, 0, g_shares3, where3, "replicated_update_last")

    for out in (out_g, out_d, out_m, out_v):
        out["sgu_ln_g"], out["sgu_ln_b"] = out["sgu_ln"][:, 0, :], out["sgu_ln"][:, 1, :]
        for n in SWAPPED:
            out[n] = jnp.swapaxes(out[n], 1, 2)

    return (loss, dx0[None], *[out_g[n] for n in WEIGHTS], *[out_d[n] for n in WEIGHTS],
            *[out_m[n] for n in WEIGHTS], *[out_v[n] for n in WEIGHTS])
```

```python
import functools
import math
import operator

import jax
import jax.numpy as jnp
from jax import lax
from jax.experimental import pallas as pl
from jax.experimental.pallas import tpu as pltpu

F32 = jnp.float32
ACT = jnp.bfloat16

EPS = 1e-6
HEAD_DIM = 64
PAIR = 2 * HEAD_DIM
GQA_GROUP = 4
WINDOW = 128
ROPE_THETA = 10000.0
SCALE = HEAD_DIM ** -0.5
MASKED = -1e30
LANES = 128
MXU_WIDTH = 256

ADAM_LR, ADAM_B1, ADAM_B2, ADAM_EPS, ADAM_WD, ADAM_STEP = 0.001, 0.9, 0.999, 1e-08, 0.01, 10

N_DEV = 8
VMEM_LIMIT_BYTES = 56 * 1024 * 1024
MESH = pl.DeviceIdType.MESH
ANY = pl.BlockSpec(memory_space=pl.ANY)
IN_VMEM = pl.BlockSpec(memory_space=pltpu.VMEM)


def _pick(n, candidates):
    for c in candidates:
        if n % c == 0:
            return c
    return n


def _row_tile(t, most=512):
    return _pick(t, (most,))


def _shard_tile(rows):
    return next((c for c in (512, 256, 176, 128, 96, 64) if rows % c == 0 and rows >= 2 * c), rows)


def _mxu_chunks(n, most=4 * MXU_WIDTH):
    assert n % MXU_WIDTH == 0 and most % MXU_WIDTH == 0
    return [(c, min(most, n - c)) for c in range(0, n, most)]


def _params(*sem):
    return pltpu.CompilerParams(dimension_semantics=sem, vmem_limit_bytes=VMEM_LIMIT_BYTES)


def _full(shape):
    return pl.BlockSpec(shape, lambda *_: (0,) * len(shape))


def _resident(shape):
    return pl.BlockSpec(shape, lambda *_: (0,) * len(shape), pipeline_mode=pl.Buffered(1))


def _rows(tm, n):
    return pl.BlockSpec((tm, n), lambda i: (i, 0))


def _sds(shape, dtype):
    return jax.ShapeDtypeStruct(shape, dtype)


def _place():
    return lax.axis_index("x"), lax.axis_index("y"), lax.axis_index("c")


def _other_chips(x, y):
    return [(1 - x, y), (x, 1 - y), (1 - x, 1 - y)]


class _Job:
    def __init__(self, inputs, out_shape, aliases, emit, n_remote, n_local=0, n_late=0):
        self.inputs, self.out_shape, self.aliases, self.emit = list(inputs), list(out_shape), dict(aliases), emit
        self.n_remote, self.n_local, self.n_late = n_remote, n_local, n_late


def _call(body, name, grid, in_specs, out_specs, out_shape, args, sem, scratch=(), jobs=()):
    n_in, n_out, n_scr = len(args), len(out_shape), len(scratch)
    j_in = [a for job in jobs for a in job.inputs]
    j_out = [s for job in jobs for s in job.out_shape]
    aliases, at_in, at_out = {}, n_in, n_out
    for job in jobs:
        aliases.update({at_in + i: at_out + o for i, o in job.aliases.items()})
        at_in, at_out = at_in + len(job.inputs), at_out + len(job.out_shape)
    n_remote = sum(job.n_remote for job in jobs)
    n_local = sum(job.n_local for job in jobs)

    def wrapped(*refs):
        ins, jin = refs[:n_in], refs[n_in:n_in + len(j_in)]
        rest = refs[n_in + len(j_in):]
        outs, jout = rest[:n_out], rest[n_out:n_out + len(j_out)]
        scr = rest[n_out + len(j_out):n_out + len(j_out) + n_scr]
        if not jobs:
            body(*ins, *outs, *scr)
            return
        send, recv, local = rest[n_out + len(j_out) + n_scr:]
        ids = [pl.program_id(a) for a in range(len(grid))]
        first = functools.reduce(operator.and_, [i == 0 for i in ids])
        last = functools.reduce(operator.and_, [i == g - 1 for i, g in zip(ids, grid)])

        def descriptors():
            place, found, i, o, r, l = _place(), ([], []), 0, 0, 0, 0
            for job in jobs:
                emitted = job.emit(jin[i:i + len(job.inputs)], jout[o:o + len(job.out_shape)], place)
                for at, (src, dst, to) in enumerate(emitted):
                    if to is None:
                        cp = pltpu.make_async_copy(src, dst, local.at[l])
                        l += 1
                    else:
                        cp = pltpu.make_async_remote_copy(src_ref=src, dst_ref=dst, send_sem=send.at[r], recv_sem=recv.at[r],
                                                          device_id=to, device_id_type=MESH)
                        r += 1
                    found[at >= len(emitted) - job.n_late].append(cp)
                i, o = i + len(job.inputs), o + len(job.out_shape)
            return found

        @pl.when(first)
        def _():
            for cp in descriptors()[0]:
                cp.start()

        body(*ins, *outs, *scr)

        @pl.when(last)
        def _():
            early, late = descriptors()
            for cp in early:
                cp.wait()
            for cp in late:
                cp.start()
            for cp in late:
                cp.wait()

    sems = [pltpu.SemaphoreType.DMA((n_remote,)), pltpu.SemaphoreType.DMA((n_remote,)),
            pltpu.SemaphoreType.DMA((max(n_local, 1),))] if jobs else []
    res = pl.pallas_call(
        wrapped, name=name, grid=grid,
        in_specs=list(in_specs) + [ANY] * len(j_in), out_specs=list(out_specs) + [ANY] * len(j_out),
        out_shape=list(out_shape) + j_out, scratch_shapes=list(scratch) + sems, input_output_aliases=aliases,
        compiler_params=_params(*(("arbitrary",) * len(grid) if jobs else sem)),
    )(*args, *j_in)
    job_res, at = [], n_out
    for job in jobs:
        job_res.append(list(res[at:at + len(job.out_shape)]))
        at += len(job.out_shape)
    return list(res[:n_out]), job_res


def _copies_only(jobs, name):
    def body(o_ref):
        o_ref[...] = jnp.zeros_like(o_ref)

    return _call(body, name, (1,), [], [_full((8, LANES))], [_sds((8, LANES), F32)], (), ("arbitrary",), jobs=jobs)[1]


def _gather_job(stage, gathered, sends=(), forwards=()):
    shape = _sds((N_DEV,) + stage.shape, stage.dtype)
    inputs = ([stage] if sends else []) + ([gathered] if gathered is not None else [])
    aliases = {len(inputs) - 1: 0} if gathered is not None else {}

    def emit(ins, outs, place):
        x, y, c = place
        sibling, chips, mine, g = (x, y, 1 - c), _other_chips(x, y), 4 * x + 2 * y + c, outs[0]
        found = []
        for r0, nr in sends:
            src, dst = ins[0].at[pl.ds(r0, nr)], g.at[mine, pl.ds(r0, nr)]
            found += [(src, dst, None), (src, dst, sibling)] + [(src, dst, (px, py, c)) for px, py in chips]
        for r0, nr in forwards:
            for px, py in chips:
                block = 4 * px + 2 * py + c
                found.append((ins[-1].at[block, pl.ds(r0, nr)], g.at[block, pl.ds(r0, nr)], sibling))
        return found

    return _Job(inputs, [shape], aliases, emit, 4 * len(sends) + 3 * len(forwards), len(sends))


def _relay_gather_job(stage, gathered, sends=(), relays=(), forwards=(), far=(), late_far=()):
    shape = _sds((N_DEV,) + stage.shape, stage.dtype)
    inputs = ([stage] if sends else []) + ([gathered] if gathered is not None else [])
    aliases = {len(inputs) - 1: 0} if gathered is not None else {}

    def emit(ins, outs, place):
        x, y, c = place
        sibling, beside_x, beside_y, g = (x, y, 1 - c), (1 - x, y, c), (x, 1 - y, c), outs[0]
        slot = lambda dev: 4 * dev[0] + 2 * dev[1] + dev[2]
        found = []
        for r0, nr in sends:
            src, dst = ins[0].at[pl.ds(r0, nr)], g.at[slot((x, y, c)), pl.ds(r0, nr)]
            found += [(src, dst, None), (src, dst, sibling), (src, dst, beside_x), (src, dst, beside_y)]

        def passed_on(block, piece, to):
            return ins[-1].at[block, pl.ds(*piece)], g.at[block, pl.ds(*piece)], to

        for piece, way in relays:
            found.append(passed_on(slot(beside_x), piece, beside_y) if way == 0 else passed_on(slot(beside_y), piece, beside_x))
        for piece in forwards:
            found += [passed_on(slot(beside_x), piece, sibling), passed_on(slot(beside_y), piece, sibling)]
        for piece in tuple(far) + tuple(late_far):
            found.append(passed_on(slot((1 - x, 1 - y, c)), piece, sibling))
        return found

    n_remote = 3 * len(sends) + len(relays) + 2 * len(forwards) + len(far) + len(late_far)
    return _Job(inputs, [shape], aliases, emit, n_remote, len(sends), len(late_far))


def _scatter_job(arrays, n_slots, route, prev=None, piece=None):
    shapes = [_sds((n_slots,) + v.shape[1:], v.dtype) for v in arrays]
    inputs = list(arrays) + (list(prev) if prev else [])
    aliases = {len(arrays) + i: i for i in range(len(arrays))} if prev else {}

    def emit(ins, outs, place):
        found = []
        for a in range(len(arrays)):
            for s in range(n_slots):
                block, to = route(s, *place)
                if piece is None:
                    found.append((ins[a].at[block], outs[a].at[s], to))
                else:
                    found.append((ins[a].at[block, pl.ds(*piece)], outs[a].at[s, pl.ds(*piece)], to))
        return found

    return _Job(inputs, shapes, aliases, emit, len(arrays) * n_slots)


def _to_sibling(s, x, y, c):
    return 2 * s + (1 - c), (x, y, 1 - c)


def _to_owner_chip(s, x, y, c):
    px, py = _other_chips(x, y)[s]
    return 2 * px + py, (px, py, c)


def _rstd(x):
    return lax.rsqrt(jnp.mean(x * x, axis=-1, keepdims=True) + EPS)


def _rms_bwd(xhat, r, g, dy):
    dxh = dy * g
    return r * (dxh - xhat * jnp.mean(dxh * xhat, axis=-1, keepdims=True))


def _first_half(rows):
    lane = lax.broadcasted_iota(jnp.int32, (rows, PAIR), 1)
    return (lane % HEAD_DIM) < (HEAD_DIM // 2)


def _rot_half(v, first):
    return jnp.where(first, pltpu.roll(v, PAIR - HEAD_DIM // 2, 1), pltpu.roll(v, HEAD_DIM // 2, 1))


def _rope(v, cos, sin, first):
    return v * cos + _rot_half(v, first) * sin


def _rope_t(dv, cos, sin, first):
    return dv * cos + _rot_half(dv * sin, first)


def _dot(a, b):
    return jnp.dot(a, b, preferred_element_type=F32)


def _dot_nt(a, b):
    return lax.dot_general(a, b, (((1,), (1,)), ((), ())), preferred_element_type=F32)


def _dot_tn(a, b):
    return lax.dot_general(a, b, (((0,), (0,)), ((), ())), preferred_element_type=F32)


def _sigmoid(v):
    return 1.0 / (1.0 + jnp.exp(-v))


_GELU_K = math.sqrt(2.0 / math.pi)
_GELU_C = 0.044715


def _gelu(v):
    return v * (0.5 * (1.0 + jnp.tanh(_GELU_K * (v + _GELU_C * (v * v * v)))))


def _gelu_grad(v):
    t = jnp.tanh(_GELU_K * (v + _GELU_C * (v * v * v)))
    return 0.5 * (1.0 + t) + 0.5 * v * (1.0 - t * t) * (_GELU_K * (1.0 + 3.0 * _GELU_C * (v * v)))


def _attn_qkv_fwd(x, g, w, b, cos, sin, jobs=()):
    t, d = x.shape
    n = w.shape[0]
    kd = n - d
    assert (kd // 2) % PAIR == 0
    tm = _row_tile(t)
    ch = _pick(d, (512,))

    def body(x_ref, g_ref, w_ref, b_ref, cos_ref, sin_ref, h_ref, q_ref, k_ref, v_ref):
        xv = x_ref[...]
        hb = (xv * _rstd(xv) * g_ref[...]).astype(ACT)
        h_ref[...] = hb
        cosv, sinv = cos_ref[...], sin_ref[...]
        first = _first_half(tm)
        left = _lane_halves()[0]

        def proj(lo, width):
            return _dot_nt(hb, w_ref[lo:lo + width, :]) + b_ref[:, lo:lo + width]

        def twice(ref, s, two_heads):
            swapped = pltpu.roll(two_heads, HEAD_DIM, 1)
            ref[:, 2 * s:2 * s + PAIR] = jnp.where(left, two_heads, swapped).astype(ACT)
            ref[:, 2 * s + PAIR:2 * s + 2 * PAIR] = jnp.where(left, swapped, two_heads).astype(ACT)

        for c in range(0, d, ch):
            y = proj(c, ch)
            for s in range(0, ch, PAIR):
                q_ref[:, c + s:c + s + PAIR] = (_rope(y[:, s:s + PAIR], cosv, sinv, first) * SCALE).astype(ACT)
        y = proj(d, kd)
        for s in range(0, kd // 2, PAIR):
            twice(k_ref, s, _rope(y[:, s:s + PAIR], cosv, sinv, first))
            twice(v_ref, s, y[:, kd // 2 + s:kd // 2 + s + PAIR])

    return _call(
        body, "attn_qkv_fwd", (t // tm,),
        [_rows(tm, d), _full((1, d)), _full((n, d)), _full((1, n)), _rows(tm, PAIR), _rows(tm, PAIR)],
        [_rows(tm, d), _rows(tm, d), _rows(tm, kd), _rows(tm, kd)],
        [_sds((t, d), ACT), _sds((t, d), ACT), _sds((t, kd), ACT), _sds((t, kd), ACT)],
        (x, g, w, b, cos, sin), ("parallel",), jobs=jobs)


STACK = GQA_GROUP * WINDOW


def _band_mask(has_prev):
    row = lax.broadcasted_iota(jnp.int32, (STACK, 2 * WINDOW), 0) % WINDOW
    col = lax.broadcasted_iota(jnp.int32, (STACK, 2 * WINDOW), 1)
    return ((col < WINDOW) & (col > row) & has_prev) | ((col >= WINDOW) & (col - WINDOW <= row))


def _lane_halves():
    lane = lax.broadcasted_iota(jnp.int32, (1, PAIR), 1)
    return lane < HEAD_DIM, lane >= HEAD_DIM


def _stack_heads(ref, h, halves):
    parts = []
    for p in range(2):
        pair = ref[:, (2 * h + p) * PAIR:(2 * h + p + 1) * PAIR]
        parts += [jnp.where(half, pair, jnp.zeros_like(pair)) for half in halves]
    return jnp.concatenate(parts, axis=0)


def _sink_column(sink_ref, h):
    row = lax.broadcasted_iota(jnp.int32, (STACK, 1), 0)
    col = jnp.full((STACK, 1), sink_ref[0, GQA_GROUP * h + GQA_GROUP - 1], F32)
    for j in range(GQA_GROUP - 2, -1, -1):
        col = jnp.where(row < (j + 1) * WINDOW, sink_ref[0, GQA_GROUP * h + j], col)
    return col


def _probs(scores, valid, sink):
    s = jnp.where(valid, scores, MASKED)
    m = jnp.maximum(jnp.max(s, axis=-1, keepdims=True), sink)
    p = jnp.exp(s - m)
    psink = jnp.exp(sink - m)
    inv = 1.0 / (jnp.sum(p, axis=-1, keepdims=True) + psink)
    return p * inv, psink * inv


def _attn_fwd(q, kd_, vd, sinks, jobs=()):
    t, d = q.shape
    kd = kd_.shape[1]
    cur = lambda n: (n, 0)
    prev = lambda n: (jnp.maximum(n - 1, 0), 0)

    def body(sink_ref, q_ref, kc_ref, kp_ref, vc_ref, vp_ref, o_ref):
        valid = _band_mask(pl.program_id(0) > 0)
        halves = _lane_halves()
        heads = range(kd // PAIR)
        hs = [slice(h * PAIR, (h + 1) * PAIR) for h in heads]
        scores = [_dot_nt(_stack_heads(q_ref, h, halves), jnp.concatenate([kp_ref[:, hs[h]], kc_ref[:, hs[h]]], axis=0)) for h in heads]
        probs = [_probs(scores[h], valid, _sink_column(sink_ref, h))[0].astype(ACT) for h in heads]
        for h in heads:
            v2 = jnp.concatenate([vp_ref[:, hs[h]], vc_ref[:, hs[h]]], axis=0)
            vs = jnp.concatenate([jnp.where(half, v2, jnp.zeros_like(v2)) for half in halves], axis=0)
            pr = probs[h]
            for p in range(2):
                both = jnp.concatenate([pr[2 * p * WINDOW:(2 * p + 1) * WINDOW], pr[(2 * p + 1) * WINDOW:(2 * p + 2) * WINDOW]], axis=1)
                o_ref[:, (2 * h + p) * PAIR:(2 * h + p + 1) * PAIR] = _dot(both, vs).astype(ACT)

    kv_c, kv_p = pl.BlockSpec((WINDOW, kd), cur), pl.BlockSpec((WINDOW, kd), prev)
    return _call(
        body, "attn_fwd", (t // WINDOW,),
        [pl.BlockSpec(memory_space=pltpu.SMEM), pl.BlockSpec((WINDOW, d), cur), kv_c, kv_p, kv_c, kv_p],
        [pl.BlockSpec((WINDOW, d), cur)], [_sds((t, d), ACT)],
        (sinks, q, kd_, kd_, vd, vd), ("parallel",), jobs=jobs)


def _attn_bwd(q, kd_, vd, do, sinks, cos, sin, jobs=()):
    t, d = q.shape
    kd = kd_.shape[1]
    nb = t // WINDOW
    n_out = d + kd
    assert (kd // 2) % PAIR == 0
    cur = lambda n: (jnp.minimum(n, nb - 1), 0)
    prev = lambda n: (jnp.maximum(jnp.minimum(n, nb - 1) - 1, 0), 0)
    late = lambda n: (jnp.maximum(n - 1, 0), 0)

    def body(sink_ref, q_ref, kc_ref, kp_ref, vc_ref, vp_ref, do_ref, cosc_ref, sinc_ref, cosp_ref, sinp_ref,
             out_ref, db_ref, dsink_ref, dq_keep, dk_keep, dv_keep):
        n = pl.program_id(0)

        @pl.when(n == 0)
        def _():
            for ref in (db_ref, dsink_ref, dq_keep, dk_keep, dv_keep):
                ref[...] = jnp.zeros_like(ref)

        valid = _band_mask(n > 0) & (n < nb)
        halves = _lane_halves()
        first = _first_half(WINDOW)
        slot = lax.broadcasted_iota(jnp.int32, dsink_ref.shape, 1)
        top = lax.broadcasted_iota(jnp.int32, dsink_ref.shape, 0) == 0
        dsink = jnp.zeros(dsink_ref.shape, F32)

        def emit(cols, done):
            out_ref[:, cols] = done.astype(ACT)
            db_ref[:, cols] += jnp.sum(done.astype(F32), axis=0, keepdims=True)

        heads = range(kd // PAIR)
        hs = [slice(h * PAIR, (h + 1) * PAIR) for h in heads]
        k2 = [jnp.concatenate([kp_ref[:, hs[h]], kc_ref[:, hs[h]]], axis=0) for h in heads]
        v2 = [jnp.concatenate([vp_ref[:, hs[h]], vc_ref[:, hs[h]]], axis=0) for h in heads]
        qs = [_stack_heads(q_ref, h, halves) for h in heads]
        dos = [_stack_heads(do_ref, h, halves) for h in heads]
        scores = [_dot_nt(qs[h], k2[h]) for h in heads]
        dps = [_dot_nt(dos[h], v2[h]) for h in heads]
        prs, dss = [], []
        for h in heads:
            pr, psink = _probs(scores[h], valid, _sink_column(sink_ref, h))
            delta = jnp.sum(pr * dps[h], axis=-1, keepdims=True)
            dss.append((pr * (dps[h] - delta)).astype(ACT))
            prs.append(pr.astype(ACT))
            leak = psink * delta
            for j in range(GQA_GROUP):
                share = jnp.sum(leak[j * WINDOW:(j + 1) * WINDOW], axis=0, keepdims=True)
                dsink = dsink - jnp.where(top & (slot == GQA_GROUP * h + j), share, 0.0)
        dqs = [_dot(dss[h], k2[h]) for h in heads]
        dk2 = [_dot_tn(dss[h], qs[h]) for h in heads]
        dv2 = [_dot_tn(prs[h], dos[h]) for h in heads]
        for h in heads:
            for p in range(2):
                ps = slice((2 * h + p) * PAIR, (2 * h + p + 1) * PAIR)
                dqp = jnp.where(halves[0], dqs[h][2 * p * WINDOW:(2 * p + 1) * WINDOW], dqs[h][(2 * p + 1) * WINDOW:(2 * p + 2) * WINDOW])
                emit(ps, dq_keep[:, ps])
                dq_keep[:, ps] = (_rope_t(dqp, cosc_ref[...], sinc_ref[...], first) * SCALE).astype(ACT)
        dks, dvs = [], []
        for h in heads:
            dks.append(dk_keep[:, hs[h]] + _rope_t(dk2[h][:WINDOW], cosp_ref[...], sinp_ref[...], first))
            dk_keep[:, hs[h]] = _rope_t(dk2[h][WINDOW:], cosc_ref[...], sinc_ref[...], first)
            dvs.append(dv_keep[:, hs[h]] + dv2[h][:WINDOW])
            dv_keep[:, hs[h]] = dv2[h][WINDOW:]
        both = lambda v: v + pltpu.roll(v, HEAD_DIM, 1)
        for h in range(0, len(heads), 2):
            at = h // 2 * PAIR
            emit(slice(d + at, d + at + PAIR), jnp.where(halves[0], both(dks[h]), both(dks[h + 1])))
            emit(slice(d + kd // 2 + at, d + kd // 2 + at + PAIR), jnp.where(halves[0], both(dvs[h]), both(dvs[h + 1])))
        dsink_ref[...] += dsink

    kv_c, kv_p = pl.BlockSpec((WINDOW, kd), cur), pl.BlockSpec((WINDOW, kd), prev)
    tab_c, tab_p = pl.BlockSpec((WINDOW, PAIR), cur), pl.BlockSpec((WINDOW, PAIR), prev)
    return _call(
        body, "attn_bwd", (nb + 1,),
        [pl.BlockSpec(memory_space=pltpu.SMEM), pl.BlockSpec((WINDOW, d), cur), kv_c, kv_p, kv_c, kv_p,
         pl.BlockSpec((WINDOW, d), cur), tab_c, tab_c, tab_p, tab_p],
        [pl.BlockSpec((WINDOW, n_out), late), _full((1, n_out)), _full((8, LANES))],
        [_sds((t, n_out), ACT), _sds((1, n_out), F32), _sds((8, LANES), F32)],
        (sinks, q, kd_, kd_, vd, vd, do, cos, sin, cos, sin), ("arbitrary",),
        scratch=[pltpu.VMEM((WINDOW, d), ACT), pltpu.VMEM((WINDOW, kd), F32), pltpu.VMEM((WINDOW, kd), F32)], jobs=jobs)


def _proj_res(a, w, b, g, x, name, jobs=()):
    t = a.shape[0]
    k, d = w.shape
    tm = _row_tile(t)
    has_bias = b is not None

    def body(*refs):
        a_ref, w_ref = refs[:2]
        b_ref = refs[2] if has_bias else None
        g_ref, x_ref, m_ref, xo_ref = refs[2 + has_bias:]
        m = _dot(a_ref[...], w_ref[...])
        if has_bias:
            m = m + b_ref[...]
        m_ref[...] = m.astype(ACT)
        xo_ref[...] = x_ref[...] + (m * _rstd(m)) * g_ref[...]

    ins = [a, w] + ([b] if has_bias else []) + [g, x]
    specs = [_rows(tm, k), _full((k, d))] + ([_full((1, d))] if has_bias else []) + [_full((1, d)), _rows(tm, d)]
    return _call(body, name, (t // tm,), specs, [_rows(tm, d), _rows(tm, d)], [_sds((t, d), ACT), _sds((t, d), F32)], ins,
                 ("parallel",), jobs=jobs)


def _post_bwd(dres, m, g, w, mode, name, jobs=()):
    t, d = dres.shape
    k = w.shape[0]
    tm = _row_tile(t)
    kc = _pick(k, (1408, 1024, 512))

    def body(*refs):
        d_ref, m_ref, g_ref, w_ref = refs[:4]
        outs = refs[4:]
        dm_ref, da_ref, dg_ref = outs[:3]
        i = pl.program_id(0)

        @pl.when(i == 0)
        def _():
            dg_ref[...] = jnp.zeros_like(dg_ref)
            if mode == "attn":
                outs[3][...] = jnp.zeros_like(outs[3])

        dv, mv = d_ref[...], m_ref[...].astype(F32)
        r = _rstd(mv)
        mh = mv * r
        dg_ref[...] += jnp.sum(dv * mh, axis=0, keepdims=True)
        dm = _rms_bwd(mh, r, g_ref[...], dv)
        dmb = dm.astype(ACT)
        dm_ref[...] = dmb
        if mode == "attn":
            outs[3][...] += jnp.sum(dm, axis=0, keepdims=True)
        for c in range(0, k, kc):
            da = _dot_nt(dmb, w_ref[c:c + kc, :])
            da_ref[:, c:c + kc] = da.astype(ACT)

    ins = [dres, m, g, w]
    specs = [_rows(tm, d), _rows(tm, d), _full((1, d)), _full((k, d))]
    out_specs = [_rows(tm, d), _rows(tm, k), _full((1, d))]
    out_shape = [_sds((t, d), ACT), _sds((t, k), ACT), _sds((1, d), F32)]
    if mode == "attn":
        out_specs.append(_full((1, d)))
        out_shape.append(_sds((1, d), F32))
    return _call(body, name, (t // tm,), specs, out_specs, out_shape, ins, ("arbitrary",), jobs=jobs)


def _pre_bwd(dy, w, x, g, dres, name, transposed=False, jobs=()):
    t, d = x.shape
    tm = _row_tile(t)

    def body(dy_ref, w_ref, x_ref, g_ref, dres_ref, dx_ref, dg_ref):
        @pl.when(pl.program_id(0) == 0)
        def _():
            dg_ref[...] = jnp.zeros_like(dg_ref)

        dh = jnp.zeros((tm, d), F32)
        if transposed:
            dh = dh + _dot(dy_ref[...], w_ref[...])
        elif w.ndim == 3:
            c = w.shape[2]
            for j in range(w.shape[0]):
                dh = dh + _dot_nt(dy_ref[:, j * c:(j + 1) * c], w_ref[j])
        else:
            n = w.shape[1]
            nc = _pick(n, (1408, 1024, 512))
            for c in range(0, n, nc):
                dh = dh + _dot_nt(dy_ref[:, c:c + nc], w_ref[:, c:c + nc])
        xv = x_ref[...]
        r = _rstd(xv)
        xh = xv * r
        dg_ref[...] += jnp.sum(dh * xh, axis=0, keepdims=True)
        dx_ref[...] = dres_ref[...] + _rms_bwd(xh, r, g_ref[...], dh)

    return _call(
        body, name, (t // tm,),
        [_rows(tm, dy.shape[1]), _full(w.shape), _rows(tm, d), _full((1, d)), _rows(tm, d)],
        [_rows(tm, d), _full((1, d))], [_sds((t, d), F32), _sds((1, d), F32)],
        (dy, w, x, g, dres), ("arbitrary",), jobs=jobs)


def _ffn_bwd(dres, f, g_post, w_dn, gu, w_gu, x, g_pre, name, target=None, jobs=()):
    t, d = x.shape
    n = w_dn.shape[0]
    tm = _row_tile(t, 256)
    whole = target is not None
    n_in = 6 if whole else 8

    def body(*refs):
        if whole:
            t_ref, gp_ref, wd_ref, wu_ref, x_ref, g_ref = refs[:n_in]
            df_ref, dgu_ref, dx_ref, dgp_ref, dg_ref, loss_ref, h_ref, a_ref, gu_ref = refs[n_in:]
        else:
            d_ref, f_ref, gp_ref, wd_ref, gu_ref, wu_ref, x_ref, g_ref = refs[:n_in]
            df_ref, dgu_ref, dx_ref, dgp_ref, dg_ref = refs[n_in:]

        @pl.when(pl.program_id(0) == 0)
        def _():
            for ref in (dgp_ref, dg_ref) + ((loss_ref,) if whole else ()):
                ref[...] = jnp.zeros_like(ref)

        xv = x_ref[...]
        rx = _rstd(xv)
        xh = xv * rx
        if whole:
            hb = (xh * g_ref[...]).astype(ACT)
            h_ref[...] = hb
            for c, size in _mxu_chunks(n):
                gate = _dot_nt(hb, wu_ref[c:c + size, :])
                up = _dot_nt(hb, wu_ref[n + c:n + c + size, :])
                gu_ref[:, c:c + size] = gate.astype(ACT)
                gu_ref[:, n + c:n + c + size] = up.astype(ACT)
                a_ref[:, c:c + size] = (gate * _sigmoid(gate) * up).astype(ACT)
            fv = _dot(a_ref[...], wd_ref[...])
            r = _rstd(fv)
            fh = fv * r
            err = xv + fh * gp_ref[...] - t_ref[...]
            dv = err * (1.0 / d)
            loss_ref[...] += 0.5 * jnp.sum(jnp.mean(err * err, axis=-1, keepdims=True), axis=0, keepdims=True)
        else:
            dv, fv = d_ref[...], f_ref[...].astype(F32)
            r = _rstd(fv)
            fh = fv * r
        dgp_ref[...] += jnp.sum(dv * fh, axis=0, keepdims=True)
        dfb = _rms_bwd(fh, r, gp_ref[...], dv).astype(ACT)
        df_ref[...] = dfb
        for c, size in _mxu_chunks(n):
            da = _dot_nt(dfb, wd_ref[c:c + size, :]).astype(ACT)
            gate, up = gu_ref[:, c:c + size], gu_ref[:, n + c:n + c + size]
            sg = _sigmoid(gate.astype(F32)).astype(ACT)
            gs = gate * sg
            dgu_ref[:, c:c + size] = da * up * (sg + gs - gs * sg)
            dgu_ref[:, n + c:n + c + size] = da * gs
        dh = _dot(dgu_ref[...], wu_ref[...])
        dg_ref[...] += jnp.sum(dh * xh, axis=0, keepdims=True)
        dx_ref[...] = dv + _rms_bwd(xh, rx, g_ref[...], dh)

    w_specs = [_resident(w_dn.shape), _resident(w_gu.shape)]
    out_specs = [_rows(tm, d), _rows(tm, 2 * n), _rows(tm, d), _full((1, d)), _full((1, d))]
    out_shape = [_sds((t, d), ACT), _sds((t, 2 * n), ACT), _sds((t, d), F32), _sds((1, d), F32), _sds((1, d), F32)]
    if whole:
        return _call(
            body, name, (t // tm,),
            [_rows(tm, d), _full((1, d))] + w_specs + [_rows(tm, d), _full((1, d))],
            out_specs + [_full((8, LANES)), _rows(tm, d), _rows(tm, n)],
            out_shape + [_sds((8, LANES), F32), _sds((t, d), ACT), _sds((t, n), ACT)],
            (target, g_post, w_dn, w_gu, x, g_pre), ("arbitrary",), scratch=[pltpu.VMEM((tm, 2 * n), ACT)], jobs=jobs)
    return _call(
        body, name, (t // tm,),
        [_rows(tm, d), _rows(tm, d), _full((1, d)), w_specs[0], _rows(tm, 2 * n), w_specs[1], _rows(tm, d), _full((1, d))],
        out_specs, out_shape, (dres, f, g_post, w_dn, gu, w_gu, x, g_pre), ("arbitrary",), jobs=jobs)


def _wgrad(a, b, name, out_dtype=F32, out_block=None, transposed=False, jobs=()):
    t = a.shape[0]
    tt = _pick(t, (1024,))
    steps = t // tt
    tk = _pick(a.shape[1], (1024, 1408))
    k, n = a.shape[1], b.shape[1]
    tn = _pick(n, (1024, 1408))
    per = 0
    if transposed:
        out_spec, out_shape, acc_shape = pl.BlockSpec((tn, tk), lambda i, j, s: (j, i)), _sds((n, k), out_dtype), (tn, tk)
    elif out_block is None:
        out_spec, out_shape, acc_shape = pl.BlockSpec((tk, tn), lambda i, j, s: (i, j)), _sds((k, n), out_dtype), (tk, tn)
    else:
        per = tn // out_block
        out_spec = pl.BlockSpec((per, tk, out_block), lambda i, j, s: (j, i, 0))
        out_shape, acc_shape = _sds((n // out_block, k, out_block), out_dtype), (tk, tn)

    def body(a_ref, b_ref, o_ref, acc_ref):
        s = pl.program_id(2)

        @pl.when(s == 0)
        def _():
            acc_ref[...] = jnp.zeros_like(acc_ref)

        acc_ref[...] += _dot_tn(b_ref[...], a_ref[...]) if transposed else _dot_tn(a_ref[...], b_ref[...])

        @pl.when(s == steps - 1)
        def _():
            if per >= 1:
                for p in range(per):
                    o_ref[p] = acc_ref[:, p * out_block:(p + 1) * out_block].astype(out_dtype)
            else:
                o_ref[...] = acc_ref[...].astype(out_dtype)

    res, job_res = _call(
        body, name, (k // tk, n // tn, steps),
        [pl.BlockSpec((tt, tk), lambda i, j, s: (s, i)), pl.BlockSpec((tt, tn), lambda i, j, s: (s, j))], [out_spec], [out_shape],
        (a, b), ("parallel", "parallel", "arbitrary"), scratch=[pltpu.VMEM(acc_shape, F32)], jobs=jobs)
    return res[0], job_res


def _ffn_up_fwd(x, g, w, jobs=()):
    t, d = x.shape
    n = w.shape[0] // 2
    tm = _row_tile(t)

    def body(x_ref, g_ref, w_ref, h_ref, gu_ref, a_ref):
        xv = x_ref[...]
        hb = (xv * _rstd(xv) * g_ref[...]).astype(ACT)
        h_ref[...] = hb
        for c, size in _mxu_chunks(n):
            gate = _dot_nt(hb, w_ref[c:c + size, :])
            up = _dot_nt(hb, w_ref[n + c:n + c + size, :])
            gu_ref[:, c:c + size] = gate.astype(ACT)
            gu_ref[:, n + c:n + c + size] = up.astype(ACT)
            a_ref[:, c:c + size] = (gate * _sigmoid(gate) * up).astype(ACT)

    return _call(
        body, "ffn_up_fwd", (t // tm,),
        [_rows(tm, d), _full((1, d)), _full(w.shape)],
        [_rows(tm, d), _rows(tm, 2 * n), _rows(tm, n)],
        [_sds((t, d), ACT), _sds((t, 2 * n), ACT), _sds((t, n), ACT)],
        (x, g, w), ("parallel",), jobs=jobs)


def _causal(ws):
    row = lax.broadcasted_iota(jnp.int32, ws.shape, 0)
    col = lax.broadcasted_iota(jnp.int32, ws.shape, 1)
    return jnp.where(col <= row, ws, 0.0)


def _sgu_ln(v, lg, lb):
    mu = jnp.mean(v, axis=-1, keepdims=True)
    vc = v - mu
    rs = lax.rsqrt(jnp.mean(vc * vc, axis=-1, keepdims=True) + EPS)
    vh = vc * rs
    return vh, rs, vh * lg + lb


def _sgu_fwd(x, g, w, lg, lb, ws, bs_t, jobs=()):
    t, d = x.shape
    nb, _, c = w.shape
    n = nb * c
    groups = d // WINDOW
    tm = _row_tile(t)

    def body(x_ref, g_ref, w_ref, lg_ref, lb_ref, ws_ref, bs_ref, h_ref, z_ref, y_ref, zf_ref):
        xv = x_ref[...]
        hb = (xv * _rstd(xv) * g_ref[...]).astype(ACT)
        h_ref[...] = hb
        for j in range(nb):
            zf_ref[:, j * c:(j + 1) * c] = _dot(hb, w_ref[j])
        z_ref[...] = zf_ref[...].astype(ACT)
        gs = [slice(gi * WINDOW, (gi + 1) * WINDOW) for gi in range(groups)]
        wsg = [_causal(ws_ref[gi]).astype(ACT) for gi in range(groups)]
        for r in range(0, tm, WINDOW):
            u = _gelu(zf_ref[r:r + WINDOW, :d])
            _, _, vn = _sgu_ln(_gelu(zf_ref[r:r + WINDOW, d:]), lg_ref[...], lb_ref[...])
            mixed = [_dot(wsg[gi], vn[:, gs[gi]].astype(ACT)) for gi in range(groups)]
            for gi in range(groups):
                y_ref[r:r + WINDOW, gs[gi]] = (u[:, gs[gi]] * (mixed[gi] + bs_ref[:, gi:gi + 1])).astype(ACT)

    vec = _full((1, d))
    return _call(
        body, "sgu_fwd", (t // tm,),
        [_rows(tm, d), vec, _full((nb, d, c)), vec, vec, _full((groups, WINDOW, WINDOW)), _full((WINDOW, groups))],
        [_rows(tm, d), _rows(tm, n), _rows(tm, d)], [_sds((t, d), ACT), _sds((t, n), ACT), _sds((t, d), ACT)],
        (x, g, w, lg, lb, ws, bs_t), ("parallel",), scratch=[pltpu.VMEM((tm, n), F32)], jobs=jobs)


def _sgu_mix_bwd(z, dy, lg, lb, ws, bs_t, jobs=()):
    t, n = z.shape
    d = n // 2
    groups = d // WINDOW
    tm = _row_tile(t)

    def body(z_ref, dy_ref, lg_ref, lb_ref, ws_ref, bs_ref, dz_ref, dlg_ref, dlb_ref, dws_ref, dbs_ref):
        @pl.when(pl.program_id(0) == 0)
        def _():
            for ref in (dlg_ref, dlb_ref, dws_ref, dbs_ref):
                ref[...] = jnp.zeros_like(ref)

        lane = lax.broadcasted_iota(jnp.int32, (WINDOW, groups), 1)
        gs = [slice(gi * WINDOW, (gi + 1) * WINDOW) for gi in range(groups)]
        wsg = [_causal(ws_ref[gi]).astype(ACT) for gi in range(groups)]
        for c in range(0, tm, WINDOW):
            rows = slice(c, c + WINDOW)
            zu, zv = z_ref[rows, :d].astype(F32), z_ref[rows, d:].astype(F32)
            u = _gelu(zu)
            vh, rs, vn = _sgu_ln(_gelu(zv), lg_ref[...], lb_ref[...])
            dyv = dy_ref[rows, :].astype(F32)
            vng = [vn[:, gs[gi]].astype(ACT) for gi in range(groups)]
            dmix = dyv * u
            dmb = [dmix[:, gs[gi]].astype(ACT) for gi in range(groups)]
            mixed = [_dot(wsg[gi], vng[gi]) for gi in range(groups)]
            dvn_parts = [_dot_tn(wsg[gi], dmb[gi]) for gi in range(groups)]
            dws_parts = [_dot_nt(dmb[gi], vng[gi]) for gi in range(groups)]
            for gi in range(groups):
                dz_ref[rows, gs[gi]] = (dyv[:, gs[gi]] * (mixed[gi] + bs_ref[:, gi:gi + 1]) * _gelu_grad(zu[:, gs[gi]])).astype(ACT)
                dws_ref[:, gs[gi]] += _causal(dws_parts[gi])
                dbs_ref[...] += jnp.where(lane == gi, jnp.sum(dmix[:, gs[gi]], axis=-1, keepdims=True), 0.0)
            dvn = jnp.concatenate(dvn_parts, axis=-1)
            dlg_ref[...] += jnp.sum(dvn * vh, axis=0, keepdims=True)
            dlb_ref[...] += jnp.sum(dvn, axis=0, keepdims=True)
            dvh = dvn * lg_ref[...]
            dv = rs * (dvh - jnp.mean(dvh, axis=-1, keepdims=True) - vh * jnp.mean(dvh * vh, axis=-1, keepdims=True))
            dz_ref[rows, d:] = (dv * _gelu_grad(zv)).astype(ACT)

    vec = _full((1, d))
    return _call(
        body, "sgu_mix_bwd", (t // tm,),
        [_rows(tm, n), _rows(tm, d), vec, vec, _full((groups, WINDOW, WINDOW)), _full((WINDOW, groups))],
        [_rows(tm, n), vec, vec, _full((WINDOW, d)), _full((WINDOW, groups))],
        [_sds((t, n), ACT), _sds((1, d), F32), _sds((1, d), F32), _sds((WINDOW, d), F32), _sds((WINDOW, groups), F32)],
        (z, dy, lg, lb, ws, bs_t), ("arbitrary",), jobs=jobs)


AG_COPIES = 8


def _prepare(sources, dtypes, n_gather, name):
    n = len(sources)
    arrays, where = [], []
    for parts, layer in sources:
        found = []
        for part in parts:
            known = [i for i, v in enumerate(arrays) if v is part]
            if not known:
                arrays.append(part)
            found.append(known[0] if known else len(arrays) - 1)
        where.append((found, layer))
    shapes = [(sum(p.shape[1] for p in parts), parts[0].shape[2]) for parts, _ in sources]

    def body(*refs):
        ins, refs = refs[:len(arrays)], refs[len(arrays):]
        stage, outs = refs[:n], refs[n:n + n_gather]
        send, recv, local_sem = refs[n + n_gather:]
        x, y, c = _place()
        me, sibling, beside_x, beside_y, far = (x, y, c), (x, y, 1 - c), (1 - x, y, c), (x, 1 - y, c), (1 - x, 1 - y, c)
        slot = lambda dev: 4 * dev[0] + 2 * dev[1] + dev[2]
        other = lambda dev: (dev[0], dev[1], 1 - c)
        whole = lambda a: (0, shapes[a][0])
        cuts = [rows // 2 if rows % 32 == 0 else rows for rows, _ in shapes]
        first = lambda a: (0, cuts[a])
        rest = lambda a: (cuts[a], shapes[a][0] - cuts[a])

        def copy(a, k, block, rows, to, src=None):
            dst = outs[a].at[block, pl.ds(*rows)]
            return pltpu.make_async_remote_copy(
                src_ref=dst if src is None else src, dst_ref=dst, send_sem=send.at[a * AG_COPIES + k],
                recv_sem=recv.at[a * AG_COPIES + k], device_id=to, device_id_type=MESH)

        def cast(a):
            found, layer = where[a]
            at = 0
            for i in found:
                rows = arrays[i].shape[1]
                stage[a][at:at + rows, :] = ins[i][layer].astype(dtypes[a])
                at += rows

        for a in range(n_gather):
            cast(a)
        started = []
        for a in range(n_gather):
            own = pltpu.make_async_copy(stage[a], outs[a].at[slot(me)], local_sem.at[a])
            own.start()
            started.append(own)
        sends = []
        for a in range(n_gather):
            sends += [copy(a, k, slot(me), whole(a), to, src=stage[a]) for k, to in enumerate((sibling, beside_x, beside_y))]
        for cp in sends:
            cp.start()
        for a in range(n_gather, n):
            cast(a)
        for a in range(n_gather):
            copy(a, 1, slot(beside_x), whole(a), me).wait_recv()
            copy(a, 2, slot(beside_y), whole(a), me).wait_recv()
            passed = [copy(a, 3, slot(beside_x), first(a), beside_y), copy(a, 5, slot(beside_x), whole(a), sibling),
                      copy(a, 6, slot(beside_y), whole(a), sibling)]
            if rest(a)[1]:
                passed.append(copy(a, 4, slot(beside_y), rest(a), beside_x))
            for cp in passed:
                cp.start()
            sends += passed
        for a in range(n_gather):
            copy(a, 3, slot(far), first(a), me).wait_recv()
            if rest(a)[1]:
                copy(a, 4, slot(far), rest(a), me).wait_recv()
            passed = copy(a, 7, slot(far), whole(a), sibling)
            passed.start()
            sends.append(passed)
        for a in range(n_gather):
            for k, source in ((0, me), (5, beside_x), (6, beside_y), (7, far)):
                copy(a, k, slot(other(source)), whole(a), me).wait_recv()
        for cp in sends:
            cp.wait_send()
        for own in started:
            own.wait()

    res = pl.pallas_call(
        body, name=name,
        in_specs=[IN_VMEM] * len(arrays), out_specs=[IN_VMEM] * n + [ANY] * n_gather,
        out_shape=[_sds(s, dt) for s, dt in zip(shapes, dtypes)]
        + [_sds((N_DEV,) + s, dt) for s, dt in zip(shapes[:n_gather], dtypes)],
        scratch_shapes=[pltpu.SemaphoreType.DMA((n_gather * AG_COPIES,)), pltpu.SemaphoreType.DMA((n_gather * AG_COPIES,)),
                        pltpu.SemaphoreType.DMA((n_gather,))],
        compiler_params=pltpu.CompilerParams(vmem_limit_bytes=VMEM_LIMIT_BYTES),
    )(*arrays)
    return list(res[:n]), list(res[n:])


def _pair_sum(g, r, core, name):
    _, _, rows, cols = g.shape
    tr = _pick(rows, (512, 256, 128))

    def body(core_ref, g_ref, r_ref, p_ref):
        del core_ref
        p_ref[...] = (g_ref[...].astype(F32) + r_ref[...].astype(F32)).astype(p_ref.dtype)

    return pl.pallas_call(
        body, name=name,
        grid_spec=pltpu.PrefetchScalarGridSpec(
            num_scalar_prefetch=1, grid=(4, rows // tr),
            in_specs=[pl.BlockSpec((None, None, tr, cols), lambda q, i, core_ref: (q, core_ref[0], i, 0)),
                      pl.BlockSpec((None, tr, cols), lambda q, i, core_ref: (q, i, 0))],
            out_specs=pl.BlockSpec((None, tr, cols), lambda q, i, core_ref: (q, i, 0))),
        out_shape=_sds((4, rows, cols), g.dtype),
        compiler_params=_params("parallel", "parallel"),
    )(core, g, r)


def _adam(w, g, m, v):
    m2 = ADAM_B1 * m + (1.0 - ADAM_B1) * g
    v2 = ADAM_B2 * v + (1.0 - ADAM_B2) * (g * g)
    m_hat = m2 / (1.0 - ADAM_B1 ** ADAM_STEP)
    v_hat = v2 / (1.0 - ADAM_B2 ** ADAM_STEP)
    return -ADAM_LR * (m_hat / (jnp.sqrt(v_hat) + ADAM_EPS) + ADAM_WD * w), m2, v2


def _shard_update(own, index, received, w, m, v, layer, so_far, name):
    _, rows, cols = w.shape
    n_recv = received.shape[0]
    tr = _shard_tile(rows)

    def body(index_ref, p_ref, q_ref, w_ref, m_ref, v_ref, *rest):
        del index_ref
        g_out, d_out, m_out, v_out = rest[-4:]
        g = p_ref[...].astype(F32)
        for s in range(n_recv):
            g = g + q_ref[s].astype(F32)
        g_out[...] = g
        d_out[...], m_out[...], v_out[...] = _adam(w_ref[...], g, m_ref[...], v_ref[...])

    tile = pl.BlockSpec((None, tr, cols), lambda i, index_ref: (layer, i, 0))
    kept = list(so_far) if so_far is not None else []
    return pl.pallas_call(
        body, name=name,
        grid_spec=pltpu.PrefetchScalarGridSpec(
            num_scalar_prefetch=1, grid=(rows // tr,),
            in_specs=[pl.BlockSpec((None, tr, cols), lambda i, index_ref: (index_ref[0], i, 0)),
                      pl.BlockSpec((n_recv, tr, cols), lambda i, index_ref: (0, i, 0)), tile, tile, tile] + [ANY] * len(kept),
            out_specs=[tile] * 4),
        out_shape=[_sds(w.shape, F32)] * 4,
        input_output_aliases={6 + i: i for i in range(len(kept))},
        compiler_params=_params("parallel"),
    )(index, own, received, w, m, v, *kept)


def _natural_pieces(shape, r, c):
    if tuple(shape[-2:]) == (r, c) and all(n == 1 for n in shape[:-2]):
        return [((0,) * (len(shape) - 2), (0, c))]
    groups, width = shape[1], shape[3]
    assert len(shape) == 4 and shape[0] == 1 and shape[2] == r and groups * width == c, (shape, r, c)
    return [((0, g), (g * width, width)) for g in range(groups)]


def _replicated_update(shares, where, params, name):
    flat = [a for p in params if p is not None for a in p]

    def body(s_ref, *refs):
        ins, outs = refs[:len(flat)], refs[len(flat):]
        total = s_ref[0]
        for dev in range(1, N_DEV):
            total = total + s_ref[dev]
        i_at = o_at = 0
        for ranges, p in zip(where, params):
            chunks = [total[r0:r0 + r, :c] for r0, r, c in ranges]
            g2d = chunks[0] if len(chunks) == 1 else jnp.concatenate(chunks, axis=1)
            if p is None:
                outs[o_at][...] = g2d
                o_at += 1
                continue
            w_ref, m_ref, v_ref = ins[i_at:i_at + 3]
            for idx, (c0, cols) in _natural_pieces(p[0].shape, *g2d.shape):
                at = idx + (slice(None), slice(None))
                g = g2d[:, c0:c0 + cols]
                outs[o_at][at] = g
                outs[o_at + 1][at], outs[o_at + 2][at], outs[o_at + 3][at] = _adam(w_ref[at], g, m_ref[at], v_ref[at])
            i_at, o_at = i_at + 3, o_at + 4

    out_shape = []
    for ranges, p in zip(where, params):
        out_shape += [_sds((ranges[0][1], sum(c for _, _, c in ranges)), F32)] if p is None else [_sds(p[0].shape, F32)] * 4
    res = pl.pallas_call(
        body, name=name, out_shape=out_shape, compiler_params=pltpu.CompilerParams(vmem_limit_bytes=VMEM_LIMIT_BYTES),
    )(shares, *flat)
    out, at = [], 0
    for p in params:
        out.append(list(res[at:at + (1 if p is None else 4)]))
        at += 1 if p is None else 4
    return out


def _rope_tables(t):
    half = HEAD_DIM // 2
    inv_freq = ROPE_THETA ** (-(jnp.arange(half, dtype=F32) * 2.0) / HEAD_DIM)
    ang = jnp.arange(t, dtype=jnp.int32).astype(F32)[:, None] * inv_freq[None, :]
    cos, sin = jnp.cos(ang), jnp.sin(ang)
    return jnp.tile(jnp.concatenate([cos, cos], axis=1), (1, 2)), jnp.tile(jnp.concatenate([-sin, sin], axis=1), (1, 2))


def _rows_full(gathered):
    return gathered.reshape(-1, gathered.shape[-1])


def _rows_blocked(full):
    return full.reshape(N_DEV, full.shape[0] // N_DEV, full.shape[1]).astype(ACT)


def _pack_rows(parts, width):
    rows, where, at = [], [], 0
    for v in parts:
        r, c = v.shape
        n_rows = -(-r // 8) * 8
        chunks = []
        for c0 in range(0, c, width):
            chunk = v[:, c0:c0 + width].astype(F32)
            rows.append(jnp.pad(chunk, ((0, n_rows - r), (0, width - chunk.shape[1]))))
            chunks.append((at, r, chunk.shape[1]))
            at += n_rows
        where.append(chunks)
    return jnp.concatenate(rows, axis=0), where


def _halves(block):
    half = block.shape[0] // 2
    return (0, half), (half, half)


def _whole(block):
    return (0, block.shape[0])


EARLY = ("attn_w_qkv", "sgu_ln", "attn_w_o")
LATE = ("sgu_w_in", "sgu_w_out", "gu0", "gu1", "dn0", "dn1")
SWAPPED = ("attn_w_qkv", "ffn_w_gate_up")
BEFORE_ATTN = ("norm_mix_post", "norm_ffn_pre", "norm_ffn_post", "attn_b_o", "sgu_w_spatial", "sgu_b_spatial")
AFTER_ATTN = ("attn_b_qkv", "attn_sinks")
LAST = ("norm_mix_pre",)
WEIGHTS = ("norm_mix_pre", "norm_mix_post", "norm_ffn_pre", "norm_ffn_post", "attn_w_qkv", "attn_b_qkv", "attn_sinks", "attn_w_o",
           "attn_b_o", "sgu_w_in", "sgu_ln_g", "sgu_ln_b", "sgu_w_spatial", "sgu_b_spatial", "sgu_w_out", "ffn_w_gate_up", "ffn_w_down")


LAYER_OF = {"gu0": ("ffn_w_gate_up", 0), "gu1": ("ffn_w_gate_up", 1), "dn0": ("ffn_w_down", 0), "dn1": ("ffn_w_down", 1)}


def _source(tree, name):
    if name == "sgu_ln":
        return [tree["sgu_ln_g"][None], tree["sgu_ln_b"][None]], 0
    leaf, layer = LAYER_OF.get(name, (name, 0))
    return [tree[leaf]], layer


def kernel(x, norm_mix_pre, norm_mix_post, norm_ffn_pre, norm_ffn_post, attn_w_qkv, attn_b_qkv, attn_sinks, attn_w_o, attn_b_o, sgu_w_in, sgu_ln_g, sgu_ln_b, sgu_w_spatial, sgu_b_spatial, sgu_w_out, ffn_w_gate_up, ffn_w_down, loss_target, m_norm_mix_pre, m_norm_mix_post, m_norm_ffn_pre, m_norm_ffn_post, m_attn_w_qkv, m_attn_b_qkv, m_attn_sinks, m_attn_w_o, m_attn_b_o, m_sgu_w_in, m_sgu_ln_g, m_sgu_ln_b, m_sgu_w_spatial, m_sgu_b_spatial, m_sgu_w_out, m_ffn_w_gate_up, m_ffn_w_down, v_norm_mix_pre, v_norm_mix_post, v_norm_ffn_pre, v_norm_ffn_post, v_attn_w_qkv, v_attn_b_qkv, v_attn_sinks, v_attn_w_o, v_attn_b_o, v_sgu_w_in, v_sgu_ln_g, v_sgu_ln_b, v_sgu_w_spatial, v_sgu_b_spatial, v_sgu_w_out, v_ffn_w_gate_up, v_ffn_w_down):
    w = dict(norm_mix_pre=norm_mix_pre, norm_mix_post=norm_mix_post, norm_ffn_pre=norm_ffn_pre, norm_ffn_post=norm_ffn_post,
             attn_w_qkv=attn_w_qkv, attn_b_qkv=attn_b_qkv, attn_sinks=attn_sinks, attn_w_o=attn_w_o, attn_b_o=attn_b_o,
             sgu_w_in=sgu_w_in, sgu_ln_g=sgu_ln_g, sgu_ln_b=sgu_ln_b, sgu_w_spatial=sgu_w_spatial, sgu_b_spatial=sgu_b_spatial,
             sgu_w_out=sgu_w_out, ffn_w_gate_up=ffn_w_gate_up, ffn_w_down=ffn_w_down)
    mom = dict(norm_mix_pre=m_norm_mix_pre, norm_mix_post=m_norm_mix_post, norm_ffn_pre=m_norm_ffn_pre, norm_ffn_post=m_norm_ffn_post,
               attn_w_qkv=m_attn_w_qkv, attn_b_qkv=m_attn_b_qkv, attn_sinks=m_attn_sinks, attn_w_o=m_attn_w_o, attn_b_o=m_attn_b_o,
               sgu_w_in=m_sgu_w_in, sgu_ln_g=m_sgu_ln_g, sgu_ln_b=m_sgu_ln_b, sgu_w_spatial=m_sgu_w_spatial,
               sgu_b_spatial=m_sgu_b_spatial, sgu_w_out=m_sgu_w_out, ffn_w_gate_up=m_ffn_w_gate_up, ffn_w_down=m_ffn_w_down)
    var = dict(norm_mix_pre=v_norm_mix_pre, norm_mix_post=v_norm_mix_post, norm_ffn_pre=v_norm_ffn_pre, norm_ffn_post=v_norm_ffn_post,
               attn_w_qkv=v_attn_w_qkv, attn_b_qkv=v_attn_b_qkv, attn_sinks=v_attn_sinks, attn_w_o=v_attn_w_o, attn_b_o=v_attn_b_o,
               sgu_w_in=v_sgu_w_in, sgu_ln_g=v_sgu_ln_g, sgu_ln_b=v_sgu_ln_b, sgu_w_spatial=v_sgu_w_spatial,
               sgu_b_spatial=v_sgu_b_spatial, sgu_w_out=v_sgu_w_out, ffn_w_gate_up=v_ffn_w_gate_up, ffn_w_down=v_ffn_w_down)
    w, mom, var = [{**tree, **{n: jnp.swapaxes(tree[n], 1, 2) for n in SWAPPED}} for tree in (w, mom, var)]
    xs, target = x[0], loss_target[0]
    t, d = xs.shape
    row = lambda v: v.reshape(1, -1).astype(F32)
    core = lax.axis_index("c").astype(jnp.int32).reshape(1)
    chip = (2 * lax.axis_index("x") + lax.axis_index("y")).astype(jnp.int32).reshape(1)
    names = EARLY + LATE
    staged, early = _prepare([_source(w, n) for n in names], [F32 if n == "sgu_ln" else ACT for n in names], len(EARLY),
                             "weights_prepare")
    stage = dict(zip(names, staged))
    w_qkv = _rows_full(early[0])
    lg, lb = row(early[1][:, 0, :]), row(early[1][:, 1, :])
    w_o = _rows_full(early[2])
    b_qkv = row(attn_b_qkv)
    sinks = attn_sinks.reshape(1, -1).astype(F32)
    ws = sgu_w_spatial[0].astype(F32)
    bs_t = sgu_b_spatial[0].astype(F32).T
    cos, sin = _rope_tables(t)
    gather = lambda name, so_far, **pieces: _gather_job(stage[name], so_far, **pieces)
    a_half = lambda name: _halves(stage[name])[0]
    b_half = lambda name: _halves(stage[name])[1]
    whole = lambda name: _whole(stage[name])

    def pieces(name, n):
        rows = stage[name].shape[0] // n
        return [(i * rows, rows) for i in range(n)]

    ways = lambda parts: [(piece, i % 2) for i, piece in enumerate(parts)]
    relay = lambda name, so_far, **steps: _relay_gather_job(stage[name], so_far, **steps)
    gu0_parts, gu1_parts = pieces("gu0", 4), pieces("gu1", 4)
    dn0_parts, dn1_parts, in_parts, out_parts = pieces("dn0", 2), pieces("dn1", 2), pieces("sgu_w_in", 2), pieces("sgu_w_out", 2)
    (h0, q, kdup, vdup), ((g_gu0,),) = _attn_qkv_fwd(
        xs, row(norm_mix_pre[0]), w_qkv, b_qkv, cos, sin, jobs=[relay("gu0", None, sends=gu0_parts)])
    (o,), ((g_gu0,), (g_dn0,)) = _attn_fwd(q, kdup, vdup, sinks, jobs=[
        relay("gu0", g_gu0, relays=ways(gu0_parts), forwards=gu0_parts), relay("dn0", None, sends=dn0_parts)])
    (m0, x1), ((g_gu0,), (g_dn0,), (g_in,)) = _proj_res(o, w_o, row(attn_b_o), row(norm_mix_post[0]), xs, "attn_out_fwd", jobs=[
        relay("gu0", g_gu0, far=gu0_parts), relay("dn0", g_dn0, relays=ways(dn0_parts), forwards=dn0_parts),
        relay("sgu_w_in", None, sends=in_parts)])
    w_gu0 = _rows_full(g_gu0)
    (h1, gu0, a0), ((g_dn0,), (g_in,), (g_out,), (g_gu1,)) = _ffn_up_fwd(x1, row(norm_ffn_pre[0]), w_gu0, jobs=[
        relay("dn0", g_dn0, far=dn0_parts), relay("sgu_w_in", g_in, relays=ways(in_parts), forwards=in_parts),
        relay("sgu_w_out", None, sends=out_parts), relay("gu1", None, sends=gu1_parts[:2])])
    w_dn0 = _rows_full(g_dn0)
    (f0, x2), ((g_in,), (g_out,), (g_gu1,)) = _proj_res(a0, w_dn0, None, row(norm_ffn_post[0]), x1, "ffn_down_fwd0", jobs=[
        relay("sgu_w_in", g_in, far=in_parts), relay("sgu_w_out", g_out, relays=ways(out_parts), forwards=out_parts),
        relay("gu1", g_gu1, sends=gu1_parts[2:], relays=ways(gu1_parts[:2]), forwards=gu1_parts[:2])])
    w_in = g_in
    (h2, z, y), ((g_out,), (g_gu1,), (g_dn1,)) = _sgu_fwd(x2, row(norm_mix_pre[1]), w_in, lg, lb, ws, bs_t, jobs=[
        relay("sgu_w_out", g_out, far=out_parts),
        relay("gu1", g_gu1, relays=ways(gu1_parts[2:]), forwards=gu1_parts[2:], far=gu1_parts[:2]),
        relay("dn1", None, sends=dn1_parts)])
    w_out = _rows_full(g_out)
    (m1, x3), ((g_gu1,), (g_dn1,)) = _proj_res(y, w_out, None, row(norm_mix_post[1]), x2, "sgu_out_fwd", jobs=[
        relay("gu1", g_gu1, far=gu1_parts[2:]),
        relay("dn1", g_dn1, relays=ways(dn1_parts), forwards=dn1_parts, late_far=dn1_parts)])
    w_gu1, w_dn1 = _rows_full(g_gu1), _rows_full(g_dn1)

    to_sibling = lambda g: _scatter_job([g], 4, _to_sibling)
    pair = lambda g, r, name: _pair_sum(g.reshape((4, 2) + g.shape[1:]), r, core, "pair_sum_" + name)
    to_chips = lambda p, prev=None, piece=None: _scatter_job([p], 3, _to_owner_chip, prev=prev, piece=piece)
    out_g, out_d, out_m, out_v = {}, {}, {}, {}

    trees = [{**tree, "sgu_ln": jnp.stack([tree["sgu_ln_g"], tree["sgu_ln_b"]], axis=1)} for tree in (w, mom, var)]
    so_far = {}

    def update(name, own, index, received):
        leaf, layer = LAYER_OF.get(name, (name, 0))
        so_far[leaf] = _shard_update(own, index, received, *[tree[leaf] for tree in trees], layer, so_far.get(leaf), "update_" + name)
        for out, val in zip((out_g, out_d, out_m, out_v), so_far[leaf]):
            out[leaf] = val

    def finish(names, n_extra, shares, where, name):
        res = _replicated_update(shares, where, [(w[n], mom[n], var[n]) for n in names] + [None] * n_extra, name)
        for n, vals in zip(names, res):
            out_g[n], out_d[n], out_m[n], out_v[n] = vals
        return [vals[0] for vals in res[len(names):]]

    first_half = lambda p: _halves(p[0])[0]
    second_half = lambda p: _halves(p[0])[1]
    (df1, dgu1, dx3, dg_ffn_post1, dg_ffn_pre1, loss_tile, h3, a1), _ = _ffn_bwd(
        None, None, row(norm_ffn_post[1]), w_dn1, None, w_gu1, x3, row(norm_ffn_pre[1]), "ffn_last", target=target)
    b_gu1, _ = _wgrad(h3, dgu1, "ffn_up_wgrad1", ACT, transposed=True)
    b_gu1 = _rows_blocked(b_gu1)
    dw_dn1, _ = _wgrad(a1, df1, "ffn_down_wgrad1", ACT)
    b_dn1 = _rows_blocked(dw_dn1)
    (dm1, dy, dg_mix_post1), ((r_gu1,), (r_dn1,)) = _post_bwd(
        dx3, m1, row(norm_mix_post[1]), w_out, "sgu", "sgu_out_bwd", jobs=[to_sibling(b_gu1), to_sibling(b_dn1)])
    p_gu1, p_dn1 = pair(b_gu1, r_gu1, "gu1"), pair(b_dn1, r_dn1, "dn1")
    dw_out, _ = _wgrad(y, dm1, "sgu_out_wgrad", ACT)
    b_out = _rows_blocked(dw_out)
    (dz, dlg, dlb, dws, dbs_t), ((q_gu1,), (r_out,)) = _sgu_mix_bwd(z, dy, lg, lb, ws, bs_t, jobs=[
        to_chips(p_gu1, piece=first_half(p_gu1)), to_sibling(b_out)])
    p_out = pair(b_out, r_out, "sgu_w_out")
    b_in, _ = _wgrad(h2, dz, "sgu_in_wgrad", ACT, out_block=stage["sgu_w_in"].shape[1])
    b_ln = jnp.stack([dlg.reshape(N_DEV, -1), dlb.reshape(N_DEV, -1)], axis=1)
    (dx2, dg_mix_pre1), _ = _pre_bwd(dz, w_in, x2, row(norm_mix_pre[1]), dx3, "sgu_in_bwd")
    (df0, dgu0, dx1, dg_ffn_post0, dg_ffn_pre0), ((q_gu1,), (q_dn1,), (q_out,), (r_in,), (r_ln,)) = _ffn_bwd(
        dx2, f0, row(norm_ffn_post[0]), w_dn0, gu0, w_gu0, x1, row(norm_ffn_pre[0]), "ffn_bwd0",
        jobs=[to_chips(p_gu1, prev=[q_gu1], piece=second_half(p_gu1)), to_chips(p_dn1), to_chips(p_out), to_sibling(b_in),
              to_sibling(b_ln)])
    update("gu1", p_gu1, chip, q_gu1)
    update("dn1", p_dn1, chip, q_dn1)
    update("sgu_w_out", p_out, chip, q_out)
    p_in, p_ln = pair(b_in, r_in, "sgu_w_in"), pair(b_ln, r_ln, "sgu_ln")
    b_gu0, ((q_in,), (q_ln,)) = _wgrad(h1, dgu0, "ffn_up_wgrad0", ACT, transposed=True, jobs=[to_chips(p_in), to_chips(p_ln)])
    update("sgu_w_in", p_in, chip, q_in)
    update("sgu_ln", p_ln, chip, q_ln)
    b_gu0 = _rows_blocked(b_gu0)
    (dm0, do, dg_mix_post0, db_o), ((r_gu0,),) = _post_bwd(
        dx1, m0, row(norm_mix_post[0]), w_o, "attn", "attn_out_bwd", jobs=[to_sibling(b_gu0)])
    p_gu0 = pair(b_gu0, r_gu0, "gu0")
    dw_dn0, ((q_gu0,),) = _wgrad(a0, df0, "ffn_down_wgrad0", ACT, jobs=[to_chips(p_gu0, piece=first_half(p_gu0))])
    b_dn0 = _rows_blocked(dw_dn0)
    dw_o, ((r_dn0,),) = _wgrad(o, dm0, "attn_out_wgrad", ACT, jobs=[to_sibling(b_dn0)])
    p_dn0 = pair(b_dn0, r_dn0, "dn0")
    b_o = _rows_blocked(dw_o)
    grads = {
        "norm_mix_post": jnp.concatenate([dg_mix_post0, dg_mix_post1], axis=0),
        "norm_ffn_pre": jnp.concatenate([dg_ffn_pre0, dg_ffn_pre1], axis=0),
        "norm_ffn_post": jnp.concatenate([dg_ffn_post0, dg_ffn_post1], axis=0),
        "attn_b_o": db_o,
        "sgu_w_spatial": dws,
        "sgu_b_spatial": dbs_t.T,
    }
    shares1, where1 = _pack_rows([grads[name] for name in BEFORE_ATTN], d)
    (dqkv, db_qkv, dsink), ((q_gu0,), (q_dn0,), (r_o,), (g_shares1,)) = _attn_bwd(q, kdup, vdup, do, sinks, cos, sin, jobs=[
        to_chips(p_gu0, prev=[q_gu0], piece=second_half(p_gu0)), to_chips(p_dn0), to_sibling(b_o),
        _gather_job(shares1, None, sends=[_whole(shares1)])])
    update("gu0", p_gu0, chip, q_gu0)
    update("dn0", p_dn0, chip, q_dn0)
    p_o = pair(b_o, r_o, "attn_w_o")
    grads.update({"attn_b_qkv": db_qkv, "attn_sinks": dsink[:1, :d // HEAD_DIM]})
    shares2, where2 = _pack_rows([grads[name] for name in AFTER_ATTN] + [loss_tile[:1, :1]], d)
    (dx0, dg_mix_pre0), _ = _pre_bwd(dqkv, w_qkv, xs, row(norm_mix_pre[0]), dx1, "attn_qkv_bwd", True)
    grads["norm_mix_pre"] = jnp.concatenate([dg_mix_pre0, dg_mix_pre1], axis=0)
    shares3, where3 = _pack_rows([grads[name] for name in LAST], d)
    dw_qkv, ((q_o,), (g_shares1,), (g_shares2,), (g_shares3,)) = _wgrad(h0, dqkv, "attn_qkv_wgrad", ACT, transposed=True, jobs=[
        to_chips(p_o), _gather_job(shares1, g_shares1, forwards=[_whole(shares1)]),
        _gather_job(shares2, None, sends=[_whole(shares2)]), _gather_job(shares3, None, sends=[_whole(shares3)])])
    update("attn_w_o", p_o, chip, q_o)
    b_qkv_grad = _rows_blocked(dw_qkv)
    (r_qkv,), (g_shares2,), (g_shares3,) = _copies_only([
        to_sibling(b_qkv_grad), _gather_job(shares2, g_shares2, forwards=[_whole(shares2)]),
        _gather_job(shares3, g_shares3, forwards=[_whole(shares3)])], "grads_tail_to_sibling")
    p_qkv = pair(b_qkv_grad, r_qkv, "attn_w_qkv")
    (q_qkv,), = _copies_only([to_chips(p_qkv)], "grads_tail_to_owner_chip")
    update("attn_w_qkv", p_qkv, chip, q_qkv)

    finish(BEFORE_ATTN, 0, g_shares1, where1, "replicated_update_before_attn")
    loss = finish(AFTER_ATTN, 1, g_shares2, where2, "replicated_update_after_attn")[0][0, 0]
    finish(LAST, 0, g_shares3, where3, "replicated_update_last")

    for out in (out_g, out_d, out_m, out_v):
        out["sgu_ln_g"], out["sgu_ln_b"] = out["sgu_ln"][:, 0, :], out["sgu_ln"][:, 1, :]
        for n in SWAPPED:
            out[n] = jnp.swapaxes(out[n], 1, 2)

    return (loss, dx0[None], *[out_g[n] for n in WEIGHTS], *[out_d[n] for n in WEIGHTS],
            *[out_m[n] for n in WEIGHTS], *[out_v[n] for n in WEIGHTS])
```

```python
import functools
import math
import operator

import jax
import jax.numpy as jnp
from jax import lax
from jax.experimental import pallas as pl
from jax.experimental.pallas import tpu as pltpu

F32 = jnp.float32
ACT = jnp.bfloat16

EPS = 1e-6
HEAD_DIM = 64
PAIR = 2 * HEAD_DIM
GQA_GROUP = 4
WINDOW = 128
ROPE_THETA = 10000.0
SCALE = HEAD_DIM ** -0.5
MASKED = -1e30
LANES = 128
MXU_WIDTH = 256

ADAM_LR, ADAM_B1, ADAM_B2, ADAM_EPS, ADAM_WD, ADAM_STEP = 0.001, 0.9, 0.999, 1e-08, 0.01, 10

N_DEV = 8
VMEM_LIMIT_BYTES = 56 * 1024 * 1024
MESH = pl.DeviceIdType.MESH
ANY = pl.BlockSpec(memory_space=pl.ANY)
IN_VMEM = pl.BlockSpec(memory_space=pltpu.VMEM)


def _pick(n, candidates):
    for c in candidates:
        if n % c == 0:
            return c
    return n


def _row_tile(t, most=512):
    return _pick(t, (most,))


def _shard_tile(rows):
    return next((c for c in (512, 256, 176, 128, 96, 64) if rows % c == 0 and rows >= 2 * c), rows)


def _mxu_chunks(n, most=4 * MXU_WIDTH):
    assert n % MXU_WIDTH == 0 and most % MXU_WIDTH == 0
    return [(c, min(most, n - c)) for c in range(0, n, most)]


def _params(*sem):
    return pltpu.CompilerParams(dimension_semantics=sem, vmem_limit_bytes=VMEM_LIMIT_BYTES)


def _full(shape):
    return pl.BlockSpec(shape, lambda *_: (0,) * len(shape))


def _resident(shape):
    return pl.BlockSpec(shape, lambda *_: (0,) * len(shape), pipeline_mode=pl.Buffered(1))


def _rows(tm, n):
    return pl.BlockSpec((tm, n), lambda i: (i, 0))


def _sds(shape, dtype):
    return jax.ShapeDtypeStruct(shape, dtype)


def _place():
    return lax.axis_index("x"), lax.axis_index("y"), lax.axis_index("c")


def _other_chips(x, y):
    return [(1 - x, y), (x, 1 - y), (1 - x, 1 - y)]


class _Job:
    def __init__(self, inputs, out_shape, aliases, emit, n_remote, n_local=0, n_late=0):
        self.inputs, self.out_shape, self.aliases, self.emit = list(inputs), list(out_shape), dict(aliases), emit
        self.n_remote, self.n_local, self.n_late = n_remote, n_local, n_late


def _call(body, name, grid, in_specs, out_specs, out_shape, args, sem, scratch=(), jobs=()):
    n_in, n_out, n_scr = len(args), len(out_shape), len(scratch)
    j_in = [a for job in jobs for a in job.inputs]
    j_out = [s for job in jobs for s in job.out_shape]
    aliases, at_in, at_out = {}, n_in, n_out
    for job in jobs:
        aliases.update({at_in + i: at_out + o for i, o in job.aliases.items()})
        at_in, at_out = at_in + len(job.inputs), at_out + len(job.out_shape)
    n_remote = sum(job.n_remote for job in jobs)
    n_local = sum(job.n_local for job in jobs)

    def wrapped(*refs):
        ins, jin = refs[:n_in], refs[n_in:n_in + len(j_in)]
        rest = refs[n_in + len(j_in):]
        outs, jout = rest[:n_out], rest[n_out:n_out + len(j_out)]
        scr = rest[n_out + len(j_out):n_out + len(j_out) + n_scr]
        if not jobs:
            body(*ins, *outs, *scr)
            return
        send, recv, local = rest[n_out + len(j_out) + n_scr:]
        ids = [pl.program_id(a) for a in range(len(grid))]
        first = functools.reduce(operator.and_, [i == 0 for i in ids])
        last = functools.reduce(operator.and_, [i == g - 1 for i, g in zip(ids, grid)])

        def descriptors():
            place, found, i, o, r, l = _place(), ([], []), 0, 0, 0, 0
            for job in jobs:
                emitted = job.emit(jin[i:i + len(job.inputs)], jout[o:o + len(job.out_shape)], place)
                for at, (src, dst, to) in enumerate(emitted):
                    if to is None:
                        cp = pltpu.make_async_copy(src, dst, local.at[l])
                        l += 1
                    else:
                        cp = pltpu.make_async_remote_copy(src_ref=src, dst_ref=dst, send_sem=send.at[r], recv_sem=recv.at[r],
                                                          device_id=to, device_id_type=MESH)
                        r += 1
                    found[at >= len(emitted) - job.n_late].append(cp)
                i, o = i + len(job.inputs), o + len(job.out_shape)
            return found

        @pl.when(first)
        def _():
            for cp in descriptors()[0]:
                cp.start()

        body(*ins, *outs, *scr)

        @pl.when(last)
        def _():
            early, late = descriptors()
            for cp in early:
                cp.wait()
            for cp in late:
                cp.start()
            for cp in late:
                cp.wait()

    sems = [pltpu.SemaphoreType.DMA((n_remote,)), pltpu.SemaphoreType.DMA((n_remote,)),
            pltpu.SemaphoreType.DMA((max(n_local, 1),))] if jobs else []
    res = pl.pallas_call(
        wrapped, name=name, grid=grid,
        in_specs=list(in_specs) + [ANY] * len(j_in), out_specs=list(out_specs) + [ANY] * len(j_out),
        out_shape=list(out_shape) + j_out, scratch_shapes=list(scratch) + sems, input_output_aliases=aliases,
        compiler_params=_params(*(("arbitrary",) * len(grid) if jobs else sem)),
    )(*args, *j_in)
    job_res, at = [], n_out
    for job in jobs:
        job_res.append(list(res[at:at + len(job.out_shape)]))
        at += len(job.out_shape)
    return list(res[:n_out]), job_res


def _copies_only(jobs, name):
    def body(o_ref):
        o_ref[...] = jnp.zeros_like(o_ref)

    return _call(body, name, (1,), [], [_full((8, LANES))], [_sds((8, LANES), F32)], (), ("arbitrary",), jobs=jobs)[1]


def _gather_job(stage, gathered, sends=(), forwards=()):
    shape = _sds((N_DEV,) + stage.shape, stage.dtype)
    inputs = ([stage] if sends else []) + ([gathered] if gathered is not None else [])
    aliases = {len(inputs) - 1: 0} if gathered is not None else {}

    def emit(ins, outs, place):
        x, y, c = place
        sibling, chips, mine, g = (x, y, 1 - c), _other_chips(x, y), 4 * x + 2 * y + c, outs[0]
        found = []
        for r0, nr in sends:
            src, dst = ins[0].at[pl.ds(r0, nr)], g.at[mine, pl.ds(r0, nr)]
            found += [(src, dst, None), (src, dst, sibling)] + [(src, dst, (px, py, c)) for px, py in chips]
        for r0, nr in forwards:
            for px, py in chips:
                block = 4 * px + 2 * py + c
                found.append((ins[-1].at[block, pl.ds(r0, nr)], g.at[block, pl.ds(r0, nr)], sibling))
        return found

    return _Job(inputs, [shape], aliases, emit, 4 * len(sends) + 3 * len(forwards), len(sends))


def _relay_gather_job(stage, gathered, sends=(), relays=(), forwards=(), far=(), late_far=()):
    shape = _sds((N_DEV,) + stage.shape, stage.dtype)
    inputs = ([stage] if sends else []) + ([gathered] if gathered is not None else [])
    aliases = {len(inputs) - 1: 0} if gathered is not None else {}

    def emit(ins, outs, place):
        x, y, c = place
        sibling, beside_x, beside_y, g = (x, y, 1 - c), (1 - x, y, c), (x, 1 - y, c), outs[0]
        slot = lambda dev: 4 * dev[0] + 2 * dev[1] + dev[2]
        found = []
        for r0, nr in sends:
            src, dst = ins[0].at[pl.ds(r0, nr)], g.at[slot((x, y, c)), pl.ds(r0, nr)]
            found += [(src, dst, None), (src, dst, sibling), (src, dst, beside_x), (src, dst, beside_y)]

        def passed_on(block, piece, to):
            return ins[-1].at[block, pl.ds(*piece)], g.at[block, pl.ds(*piece)], to

        for piece, way in relays:
            found.append(passed_on(slot(beside_x), piece, beside_y) if way == 0 else passed_on(slot(beside_y), piece, beside_x))
        for piece in forwards:
            found += [passed_on(slot(beside_x), piece, sibling), passed_on(slot(beside_y), piece, sibling)]
        for piece in tuple(far) + tuple(late_far):
            found.append(passed_on(slot((1 - x, 1 - y, c)), piece, sibling))
        return found

    n_remote = 3 * len(sends) + len(relays) + 2 * len(forwards) + len(far) + len(late_far)
    return _Job(inputs, [shape], aliases, emit, n_remote, len(sends), len(late_far))


def _scatter_job(arrays, n_slots, route, prev=None, piece=None):
    shapes = [_sds((n_slots,) + v.shape[1:], v.dtype) for v in arrays]
    inputs = list(arrays) + (list(prev) if prev else [])
    aliases = {len(arrays) + i: i for i in range(len(arrays))} if prev else {}

    def emit(ins, outs, place):
        found = []
        for a in range(len(arrays)):
            for s in range(n_slots):
                block, to = route(s, *place)
                if piece is None:
                    found.append((ins[a].at[block], outs[a].at[s], to))
                else:
                    found.append((ins[a].at[block, pl.ds(*piece)], outs[a].at[s, pl.ds(*piece)], to))
        return found

    return _Job(inputs, shapes, aliases, emit, len(arrays) * n_slots)


def _to_sibling(s, x, y, c):
    return 2 * s + (1 - c), (x, y, 1 - c)


def _to_owner_chip(s, x, y, c):
    px, py = _other_chips(x, y)[s]
    return 2 * px + py, (px, py, c)


def _rstd(x):
    return lax.rsqrt(jnp.mean(x * x, axis=-1, keepdims=True) + EPS)


def _rms_bwd(xhat, r, g, dy):
    dxh = dy * g
    return r * (dxh - xhat * jnp.mean(dxh * xhat, axis=-1, keepdims=True))


def _first_half(rows):
    lane = lax.broadcasted_iota(jnp.int32, (rows, PAIR), 1)
    return (lane % HEAD_DIM) < (HEAD_DIM // 2)


def _rot_half(v, first):
    return jnp.where(first, pltpu.roll(v, PAIR - HEAD_DIM // 2, 1), pltpu.roll(v, HEAD_DIM // 2, 1))


def _rope(v, cos, sin, first):
    return v * cos + _rot_half(v, first) * sin


def _rope_t(dv, cos, sin, first):
    return dv * cos + _rot_half(dv * sin, first)


def _dot(a, b):
    return jnp.dot(a, b, preferred_element_type=F32)


def _dot_nt(a, b):
    return lax.dot_general(a, b, (((1,), (1,)), ((), ())), preferred_element_type=F32)


def _dot_tn(a, b):
    return lax.dot_general(a, b, (((0,), (0,)), ((), ())), preferred_element_type=F32)


def _sigmoid(v):
    return 1.0 / (1.0 + jnp.exp(-v))


_GELU_K = math.sqrt(2.0 / math.pi)
_GELU_C = 0.044715


def _gelu(v):
    return v * (0.5 * (1.0 + jnp.tanh(_GELU_K * (v + _GELU_C * (v * v * v)))))


def _gelu_grad(v):
    t = jnp.tanh(_GELU_K * (v + _GELU_C * (v * v * v)))
    return 0.5 * (1.0 + t) + 0.5 * v * (1.0 - t * t) * (_GELU_K * (1.0 + 3.0 * _GELU_C * (v * v)))


def _attn_qkv_fwd(x, g, w, b, cos, sin, jobs=()):
    t, d = x.shape
    n = w.shape[0]
    kd = n - d
    assert (kd // 2) % PAIR == 0
    tm = _row_tile(t)
    ch = _pick(d, (512,))

    def body(x_ref, g_ref, w_ref, b_ref, cos_ref, sin_ref, h_ref, q_ref, k_ref, v_ref):
        xv = x_ref[...]
        hb = (xv * _rstd(xv) * g_ref[...]).astype(ACT)
        h_ref[...] = hb
        cosv, sinv = cos_ref[...], sin_ref[...]
        first = _first_half(tm)
        left = _lane_halves()[0]

        def proj(lo, width):
            return _dot_nt(hb, w_ref[lo:lo + width, :]) + b_ref[:, lo:lo + width]

        def twice(ref, s, two_heads):
            swapped = pltpu.roll(two_heads, HEAD_DIM, 1)
            ref[:, 2 * s:2 * s + PAIR] = jnp.where(left, two_heads, swapped).astype(ACT)
            ref[:, 2 * s + PAIR:2 * s + 2 * PAIR] = jnp.where(left, swapped, two_heads).astype(ACT)

        for c in range(0, d, ch):
            y = proj(c, ch)
            for s in range(0, ch, PAIR):
                q_ref[:, c + s:c + s + PAIR] = (_rope(y[:, s:s + PAIR], cosv, sinv, first) * SCALE).astype(ACT)
        y = proj(d, kd)
        for s in range(0, kd // 2, PAIR):
            twice(k_ref, s, _rope(y[:, s:s + PAIR], cosv, sinv, first))
            twice(v_ref, s, y[:, kd // 2 + s:kd // 2 + s + PAIR])

    return _call(
        body, "attn_qkv_fwd", (t // tm,),
        [_rows(tm, d), _full((1, d)), _full((n, d)), _full((1, n)), _rows(tm, PAIR), _rows(tm, PAIR)],
        [_rows(tm, d), _rows(tm, d), _rows(tm, kd), _rows(tm, kd)],
        [_sds((t, d), ACT), _sds((t, d), ACT), _sds((t, kd), ACT), _sds((t, kd), ACT)],
        (x, g, w, b, cos, sin), ("parallel",), jobs=jobs)


STACK = GQA_GROUP * WINDOW


def _band_mask(has_prev):
    row = lax.broadcasted_iota(jnp.int32, (STACK, 2 * WINDOW), 0) % WINDOW
    col = lax.broadcasted_iota(jnp.int32, (STACK, 2 * WINDOW), 1)
    return ((col < WINDOW) & (col > row) & has_prev) | ((col >= WINDOW) & (col - WINDOW <= row))


def _lane_halves():
    lane = lax.broadcasted_iota(jnp.int32, (1, PAIR), 1)
    return lane < HEAD_DIM, lane >= HEAD_DIM


def _stack_heads(ref, h, halves):
    parts = []
    for p in range(2):
        pair = ref[:, (2 * h + p) * PAIR:(2 * h + p + 1) * PAIR]
        parts += [jnp.where(half, pair, jnp.zeros_like(pair)) for half in halves]
    return jnp.concatenate(parts, axis=0)


def _sink_column(sink_ref, h):
    row = lax.broadcasted_iota(jnp.int32, (STACK, 1), 0)
    col = jnp.full((STACK, 1), sink_ref[0, GQA_GROUP * h + GQA_GROUP - 1], F32)
    for j in range(GQA_GROUP - 2, -1, -1):
        col = jnp.where(row < (j + 1) * WINDOW, sink_ref[0, GQA_GROUP * h + j], col)
    return col


def _probs(scores, valid, sink):
    s = jnp.where(valid, scores, MASKED)
    m = jnp.maximum(jnp.max(s, axis=-1, keepdims=True), sink)
    p = jnp.exp(s - m)
    psink = jnp.exp(sink - m)
    inv = 1.0 / (jnp.sum(p, axis=-1, keepdims=True) + psink)
    return p * inv, psink * inv


def _attn_fwd(q, kd_, vd, sinks, jobs=()):
    t, d = q.shape
    kd = kd_.shape[1]
    cur = lambda n: (n, 0)
    prev = lambda n: (jnp.maximum(n - 1, 0), 0)

    def body(sink_ref, q_ref, kc_ref, kp_ref, vc_ref, vp_ref, o_ref):
        valid = _band_mask(pl.program_id(0) > 0)
        halves = _lane_halves()
        heads = range(kd // PAIR)
        hs = [slice(h * PAIR, (h + 1) * PAIR) for h in heads]
        scores = [_dot_nt(_stack_heads(q_ref, h, halves), jnp.concatenate([kp_ref[:, hs[h]], kc_ref[:, hs[h]]], axis=0)) for h in heads]
        probs = [_probs(scores[h], valid, _sink_column(sink_ref, h))[0].astype(ACT) for h in heads]
        for h in heads:
            v2 = jnp.concatenate([vp_ref[:, hs[h]], vc_ref[:, hs[h]]], axis=0)
            vs = jnp.concatenate([jnp.where(half, v2, jnp.zeros_like(v2)) for half in halves], axis=0)
            pr = probs[h]
            for p in range(2):
                both = jnp.concatenate([pr[2 * p * WINDOW:(2 * p + 1) * WINDOW], pr[(2 * p + 1) * WINDOW:(2 * p + 2) * WINDOW]], axis=1)
                o_ref[:, (2 * h + p) * PAIR:(2 * h + p + 1) * PAIR] = _dot(both, vs).astype(ACT)

    kv_c, kv_p = pl.BlockSpec((WINDOW, kd), cur), pl.BlockSpec((WINDOW, kd), prev)
    return _call(
        body, "attn_fwd", (t // WINDOW,),
        [pl.BlockSpec(memory_space=pltpu.SMEM), pl.BlockSpec((WINDOW, d), cur), kv_c, kv_p, kv_c, kv_p],
        [pl.BlockSpec((WINDOW, d), cur)], [_sds((t, d), ACT)],
        (sinks, q, kd_, kd_, vd, vd), ("parallel",), jobs=jobs)


def _attn_bwd(q, kd_, vd, do, sinks, cos, sin, jobs=()):
    t, d = q.shape
    kd = kd_.shape[1]
    nb = t // WINDOW
    n_out = d + kd
    assert (kd // 2) % PAIR == 0
    cur = lambda n: (jnp.minimum(n, nb - 1), 0)
    prev = lambda n: (jnp.maximum(jnp.minimum(n, nb - 1) - 1, 0), 0)
    late = lambda n: (jnp.maximum(n - 1, 0), 0)

    def body(sink_ref, q_ref, kc_ref, kp_ref, vc_ref, vp_ref, do_ref, cosc_ref, sinc_ref, cosp_ref, sinp_ref,
             out_ref, db_ref, dsink_ref, dq_keep, dk_keep, dv_keep):
        n = pl.program_id(0)

        @pl.when(n == 0)
        def _():
            for ref in (db_ref, dsink_ref, dq_keep, dk_keep, dv_keep):
                ref[...] = jnp.zeros_like(ref)

        valid = _band_mask(n > 0) & (n < nb)
        halves = _lane_halves()
        first = _first_half(WINDOW)
        slot = lax.broadcasted_iota(jnp.int32, dsink_ref.shape, 1)
        top = lax.broadcasted_iota(jnp.int32, dsink_ref.shape, 0) == 0
        dsink = jnp.zeros(dsink_ref.shape, F32)

        def emit(cols, done):
            out_ref[:, cols] = done.astype(ACT)
            db_ref[:, cols] += jnp.sum(done.astype(F32), axis=0, keepdims=True)

        heads = range(kd // PAIR)
        hs = [slice(h * PAIR, (h + 1) * PAIR) for h in heads]
        k2 = [jnp.concatenate([kp_ref[:, hs[h]], kc_ref[:, hs[h]]], axis=0) for h in heads]
        v2 = [jnp.concatenate([vp_ref[:, hs[h]], vc_ref[:, hs[h]]], axis=0) for h in heads]
        qs = [_stack_heads(q_ref, h, halves) for h in heads]
        dos = [_stack_heads(do_ref, h, halves) for h in heads]
        scores = [_dot_nt(qs[h], k2[h]) for h in heads]
        dps = [_dot_nt(dos[h], v2[h]) for h in heads]
        prs, dss = [], []
        for h in heads:
            pr, psink = _probs(scores[h], valid, _sink_column(sink_ref, h))
            delta = jnp.sum(pr * dps[h], axis=-1, keepdims=True)
            dss.append((pr * (dps[h] - delta)).astype(ACT))
            prs.append(pr.astype(ACT))
            leak = psink * delta
            for j in range(GQA_GROUP):
                share = jnp.sum(leak[j * WINDOW:(j + 1) * WINDOW], axis=0, keepdims=True)
                dsink = dsink - jnp.where(top & (slot == GQA_GROUP * h + j), share, 0.0)
        dqs = [_dot(dss[h], k2[h]) for h in heads]
        dk2 = [_dot_tn(dss[h], qs[h]) for h in heads]
        dv2 = [_dot_tn(prs[h], dos[h]) for h in heads]
        for h in heads:
            for p in range(2):
                ps = slice((2 * h + p) * PAIR, (2 * h + p + 1) * PAIR)
                dqp = jnp.where(halves[0], dqs[h][2 * p * WINDOW:(2 * p + 1) * WINDOW], dqs[h][(2 * p + 1) * WINDOW:(2 * p + 2) * WINDOW])
                emit(ps, dq_keep[:, ps])
                dq_keep[:, ps] = (_rope_t(dqp, cosc_ref[...], sinc_ref[...], first) * SCALE).astype(ACT)
        dks, dvs = [], []
        for h in heads:
            dks.append(dk_keep[:, hs[h]] + _rope_t(dk2[h][:WINDOW], cosp_ref[...], sinp_ref[...], first))
            dk_keep[:, hs[h]] = _rope_t(dk2[h][WINDOW:], cosc_ref[...], sinc_ref[...], first)
            dvs.append(dv_keep[:, hs[h]] + dv2[h][:WINDOW])
            dv_keep[:, hs[h]] = dv2[h][WINDOW:]
        both = lambda v: v + pltpu.roll(v, HEAD_DIM, 1)
        for h in range(0, len(heads), 2):
            at = h // 2 * PAIR
            emit(slice(d + at, d + at + PAIR), jnp.where(halves[0], both(dks[h]), both(dks[h + 1])))
            emit(slice(d + kd // 2 + at, d + kd // 2 + at + PAIR), jnp.where(halves[0], both(dvs[h]), both(dvs[h + 1])))
        dsink_ref[...] += dsink

    kv_c, kv_p = pl.BlockSpec((WINDOW, kd), cur), pl.BlockSpec((WINDOW, kd), prev)
    tab_c, tab_p = pl.BlockSpec((WINDOW, PAIR), cur), pl.BlockSpec((WINDOW, PAIR), prev)
    return _call(
        body, "attn_bwd", (nb + 1,),
        [pl.BlockSpec(memory_space=pltpu.SMEM), pl.BlockSpec((WINDOW, d), cur), kv_c, kv_p, kv_c, kv_p,
         pl.BlockSpec((WINDOW, d), cur), tab_c, tab_c, tab_p, tab_p],
        [pl.BlockSpec((WINDOW, n_out), late), _full((1, n_out)), _full((8, LANES))],
        [_sds((t, n_out), ACT), _sds((1, n_out), F32), _sds((8, LANES), F32)],
        (sinks, q, kd_, kd_, vd, vd, do, cos, sin, cos, sin), ("arbitrary",),
        scratch=[pltpu.VMEM((WINDOW, d), ACT), pltpu.VMEM((WINDOW, kd), F32), pltpu.VMEM((WINDOW, kd), F32)], jobs=jobs)


def _proj_res(a, w, b, g, x, name, jobs=()):
    t = a.shape[0]
    k, d = w.shape
    tm = _row_tile(t)
    has_bias = b is not None

    def body(*refs):
        a_ref, w_ref = refs[:2]
        b_ref = refs[2] if has_bias else None
        g_ref, x_ref, m_ref, xo_ref = refs[2 + has_bias:]
        m = _dot(a_ref[...], w_ref[...])
        if has_bias:
            m = m + b_ref[...]
        m_ref[...] = m.astype(ACT)
        xo_ref[...] = x_ref[...] + (m * _rstd(m)) * g_ref[...]

    ins = [a, w] + ([b] if has_bias else []) + [g, x]
    specs = [_rows(tm, k), _full((k, d))] + ([_full((1, d))] if has_bias else []) + [_full((1, d)), _rows(tm, d)]
    return _call(body, name, (t // tm,), specs, [_rows(tm, d), _rows(tm, d)], [_sds((t, d), ACT), _sds((t, d), F32)], ins,
                 ("parallel",), jobs=jobs)


def _post_bwd(dres, m, g, w, mode, name, jobs=()):
    t, d = dres.shape
    k = w.shape[0]
    tm = _row_tile(t)
    kc = _pick(k, (1408, 1024, 512))

    def body(*refs):
        d_ref, m_ref, g_ref, w_ref = refs[:4]
        outs = refs[4:]
        dm_ref, da_ref, dg_ref = outs[:3]
        i = pl.program_id(0)

        @pl.when(i == 0)
        def _():
            dg_ref[...] = jnp.zeros_like(dg_ref)
            if mode == "attn":
                outs[3][...] = jnp.zeros_like(outs[3])

        dv, mv = d_ref[...], m_ref[...].astype(F32)
        r = _rstd(mv)
        mh = mv * r
        dg_ref[...] += jnp.sum(dv * mh, axis=0, keepdims=True)
        dm = _rms_bwd(mh, r, g_ref[...], dv)
        dmb = dm.astype(ACT)
        dm_ref[...] = dmb
        if mode == "attn":
            outs[3][...] += jnp.sum(dm, axis=0, keepdims=True)
        for c in range(0, k, kc):
            da = _dot_nt(dmb, w_ref[c:c + kc, :])
            da_ref[:, c:c + kc] = da.astype(ACT)

    ins = [dres, m, g, w]
    specs = [_rows(tm, d), _rows(tm, d), _full((1, d)), _full((k, d))]
    out_specs = [_rows(tm, d), _rows(tm, k), _full((1, d))]
    out_shape = [_sds((t, d), ACT), _sds((t, k), ACT), _sds((1, d), F32)]
    if mode == "attn":
        out_specs.append(_full((1, d)))
        out_shape.append(_sds((1, d), F32))
    return _call(body, name, (t // tm,), specs, out_specs, out_shape, ins, ("arbitrary",), jobs=jobs)


def _pre_bwd(dy, w, x, g, dres, name, transposed=False, jobs=()):
    t, d = x.shape
    tm = _row_tile(t)

    def body(dy_ref, w_ref, x_ref, g_ref, dres_ref, dx_ref, dg_ref):
        @pl.when(pl.program_id(0) == 0)
        def _():
            dg_ref[...] = jnp.zeros_like(dg_ref)

        dh = jnp.zeros((tm, d), F32)
        if transposed:
            dh = dh + _dot(dy_ref[...], w_ref[...])
        elif w.ndim == 3:
            c = w.shape[2]
            for j in range(w.shape[0]):
                dh = dh + _dot_nt(dy_ref[:, j * c:(j + 1) * c], w_ref[j])
        else:
            n = w.shape[1]
            nc = _pick(n, (1408, 1024, 512))
            for c in range(0, n, nc):
                dh = dh + _dot_nt(dy_ref[:, c:c + nc], w_ref[:, c:c + nc])
        xv = x_ref[...]
        r = _rstd(xv)
        xh = xv * r
        dg_ref[...] += jnp.sum(dh * xh, axis=0, keepdims=True)
        dx_ref[...] = dres_ref[...] + _rms_bwd(xh, r, g_ref[...], dh)

    return _call(
        body, name, (t // tm,),
        [_rows(tm, dy.shape[1]), _full(w.shape), _rows(tm, d), _full((1, d)), _rows(tm, d)],
        [_rows(tm, d), _full((1, d))], [_sds((t, d), F32), _sds((1, d), F32)],
        (dy, w, x, g, dres), ("arbitrary",), jobs=jobs)


def _ffn_bwd(dres, f, g_post, w_dn, gu, w_gu, x, g_pre, name, target=None, jobs=()):
    t, d = x.shape
    n = w_dn.shape[0]
    tm = _row_tile(t, 256)
    whole = target is not None
    n_in = 6 if whole else 8

    def body(*refs):
        if whole:
            t_ref, gp_ref, wd_ref, wu_ref, x_ref, g_ref = refs[:n_in]
            df_ref, dgu_ref, dx_ref, dgp_ref, dg_ref, loss_ref, h_ref, a_ref, gu_ref = refs[n_in:]
        else:
            d_ref, f_ref, gp_ref, wd_ref, gu_ref, wu_ref, x_ref, g_ref = refs[:n_in]
            df_ref, dgu_ref, dx_ref, dgp_ref, dg_ref = refs[n_in:]

        @pl.when(pl.program_id(0) == 0)
        def _():
            for ref in (dgp_ref, dg_ref) + ((loss_ref,) if whole else ()):
                ref[...] = jnp.zeros_like(ref)

        xv = x_ref[...]
        rx = _rstd(xv)
        xh = xv * rx
        if whole:
            hb = (xh * g_ref[...]).astype(ACT)
            h_ref[...] = hb
            for c, size in _mxu_chunks(n):
                gate = _dot_nt(hb, wu_ref[c:c + size, :])
                up = _dot_nt(hb, wu_ref[n + c:n + c + size, :])
                gu_ref[:, c:c + size] = gate.astype(ACT)
                gu_ref[:, n + c:n + c + size] = up.astype(ACT)
                a_ref[:, c:c + size] = (gate * _sigmoid(gate) * up).astype(ACT)
            fv = _dot(a_ref[...], wd_ref[...])
            r = _rstd(fv)
            fh = fv * r
            err = xv + fh * gp_ref[...] - t_ref[...]
            dv = err * (1.0 / d)
            loss_ref[...] += 0.5 * jnp.sum(jnp.mean(err * err, axis=-1, keepdims=True), axis=0, keepdims=True)
        else:
            dv, fv = d_ref[...], f_ref[...].astype(F32)
            r = _rstd(fv)
            fh = fv * r
        dgp_ref[...] += jnp.sum(dv * fh, axis=0, keepdims=True)
        dfb = _rms_bwd(fh, r, gp_ref[...], dv).astype(ACT)
        df_ref[...] = dfb
        for c, size in _mxu_chunks(n):
            da = _dot_nt(dfb, wd_ref[c:c + size, :]).astype(ACT)
            gate, up = gu_ref[:, c:c + size], gu_ref[:, n + c:n + c + size]
            sg = _sigmoid(gate.astype(F32)).astype(ACT)
            gs = gate * sg
            dgu_ref[:, c:c + size] = da * up * (sg + gs - gs * sg)
            dgu_ref[:, n + c:n + c + size] = da * gs
        dh = _dot(dgu_ref[...], wu_ref[...])
        dg_ref[...] += jnp.sum(dh * xh, axis=0, keepdims=True)
        dx_ref[...] = dv + _rms_bwd(xh, rx, g_ref[...], dh)

    w_specs = [_resident(w_dn.shape), _resident(w_gu.shape)]
    out_specs = [_rows(tm, d), _rows(tm, 2 * n), _rows(tm, d), _full((1, d)), _full((1, d))]
    out_shape = [_sds((t, d), ACT), _sds((t, 2 * n), ACT), _sds((t, d), F32), _sds((1, d), F32), _sds((1, d), F32)]
    if whole:
        return _call(
            body, name, (t // tm,),
            [_rows(tm, d), _full((1, d))] + w_specs + [_rows(tm, d), _full((1, d))],
            out_specs + [_full((8, LANES)), _rows(tm, d), _rows(tm, n)],
            out_shape + [_sds((8, LANES), F32), _sds((t, d), ACT), _sds((t, n), ACT)],
            (target, g_post, w_dn, w_gu, x, g_pre), ("arbitrary",), scratch=[pltpu.VMEM((tm, 2 * n), ACT)], jobs=jobs)
    return _call(
        body, name, (t // tm,),
        [_rows(tm, d), _rows(tm, d), _full((1, d)), w_specs[0], _rows(tm, 2 * n), w_specs[1], _rows(tm, d), _full((1, d))],
        out_specs, out_shape, (dres, f, g_post, w_dn, gu, w_gu, x, g_pre), ("arbitrary",), jobs=jobs)


def _wgrad(a, b, name, out_dtype=F32, out_block=None, transposed=False, jobs=()):
    t = a.shape[0]
    tt = _pick(t, (1024,))
    steps = t // tt
    tk = _pick(a.shape[1], (1024, 1408))
    k, n = a.shape[1], b.shape[1]
    tn = _pick(n, (1024, 1408))
    per = 0
    if transposed:
        out_spec, out_shape, acc_shape = pl.BlockSpec((tn, tk), lambda i, j, s: (j, i)), _sds((n, k), out_dtype), (tn, tk)
    elif out_block is None:
        out_spec, out_shape, acc_shape = pl.BlockSpec((tk, tn), lambda i, j, s: (i, j)), _sds((k, n), out_dtype), (tk, tn)
    else:
        per = tn // out_block
        out_spec = pl.BlockSpec((per, tk, out_block), lambda i, j, s: (j, i, 0))
        out_shape, acc_shape = _sds((n // out_block, k, out_block), out_dtype), (tk, tn)

    def body(a_ref, b_ref, o_ref, acc_ref):
        s = pl.program_id(2)

        @pl.when(s == 0)
        def _():
            acc_ref[...] = jnp.zeros_like(acc_ref)

        acc_ref[...] += _dot_tn(b_ref[...], a_ref[...]) if transposed else _dot_tn(a_ref[...], b_ref[...])

        @pl.when(s == steps - 1)
        def _():
            if per >= 1:
                for p in range(per):
                    o_ref[p] = acc_ref[:, p * out_block:(p + 1) * out_block].astype(out_dtype)
            else:
                o_ref[...] = acc_ref[...].astype(out_dtype)

    res, job_res = _call(
        body, name, (k // tk, n // tn, steps),
        [pl.BlockSpec((tt, tk), lambda i, j, s: (s, i)), pl.BlockSpec((tt, tn), lambda i, j, s: (s, j))], [out_spec], [out_shape],
        (a, b), ("parallel", "parallel", "arbitrary"), scratch=[pltpu.VMEM(acc_shape, F32)], jobs=jobs)
    return res[0], job_res


def _ffn_fwd(x, g_pre, w_gu, w_dn, g_post, jobs=()):
    t, d = x.shape
    n = w_dn.shape[0]
    tm = _row_tile(t)

    def body(x_ref, g_ref, wu_ref, wd_ref, gp_ref, h_ref, gu_ref, a_ref, f_ref, xo_ref):
        xv = x_ref[...]
        hb = (xv * _rstd(xv) * g_ref[...]).astype(ACT)
        h_ref[...] = hb
        for c, size in _mxu_chunks(n):
            gate = _dot_nt(hb, wu_ref[c:c + size, :])
            up = _dot_nt(hb, wu_ref[n + c:n + c + size, :])
            gu_ref[:, c:c + size] = gate.astype(ACT)
            gu_ref[:, n + c:n + c + size] = up.astype(ACT)
            a_ref[:, c:c + size] = (gate * _sigmoid(gate) * up).astype(ACT)
        f = _dot(a_ref[...], wd_ref[...])
        f_ref[...] = f.astype(ACT)
        xo_ref[...] = xv + (f * _rstd(f)) * gp_ref[...]

    return _call(
        body, "ffn_fwd", (t // tm,),
        [_rows(tm, d), _full((1, d)), _resident(w_gu.shape), _resident(w_dn.shape), _full((1, d))],
        [_rows(tm, d), _rows(tm, 2 * n), _rows(tm, n), _rows(tm, d), _rows(tm, d)],
        [_sds((t, d), ACT), _sds((t, 2 * n), ACT), _sds((t, n), ACT), _sds((t, d), ACT), _sds((t, d), F32)],
        (x, g_pre, w_gu, w_dn, g_post), ("parallel",), jobs=jobs)


def _causal(ws):
    row = lax.broadcasted_iota(jnp.int32, ws.shape, 0)
    col = lax.broadcasted_iota(jnp.int32, ws.shape, 1)
    return jnp.where(col <= row, ws, 0.0)


def _sgu_ln(v, lg, lb):
    mu = jnp.mean(v, axis=-1, keepdims=True)
    vc = v - mu
    rs = lax.rsqrt(jnp.mean(vc * vc, axis=-1, keepdims=True) + EPS)
    vh = vc * rs
    return vh, rs, vh * lg + lb


def _sgu_fwd(x, g, w, lg, lb, ws, bs_t, jobs=()):
    t, d = x.shape
    nb, _, c = w.shape
    n = nb * c
    groups = d // WINDOW
    tm = _row_tile(t)

    def body(x_ref, g_ref, w_ref, lg_ref, lb_ref, ws_ref, bs_ref, h_ref, z_ref, y_ref, zf_ref):
        xv = x_ref[...]
        hb = (xv * _rstd(xv) * g_ref[...]).astype(ACT)
        h_ref[...] = hb
        for j in range(nb):
            zf_ref[:, j * c:(j + 1) * c] = _dot(hb, w_ref[j])
        z_ref[...] = zf_ref[...].astype(ACT)
        gs = [slice(gi * WINDOW, (gi + 1) * WINDOW) for gi in range(groups)]
        wsg = [_causal(ws_ref[gi]).astype(ACT) for gi in range(groups)]
        for r in range(0, tm, WINDOW):
            u = _gelu(zf_ref[r:r + WINDOW, :d])
            _, _, vn = _sgu_ln(_gelu(zf_ref[r:r + WINDOW, d:]), lg_ref[...], lb_ref[...])
            mixed = [_dot(wsg[gi], vn[:, gs[gi]].astype(ACT)) for gi in range(groups)]
            for gi in range(groups):
                y_ref[r:r + WINDOW, gs[gi]] = (u[:, gs[gi]] * (mixed[gi] + bs_ref[:, gi:gi + 1])).astype(ACT)

    vec = _full((1, d))
    return _call(
        body, "sgu_fwd", (t // tm,),
        [_rows(tm, d), vec, _full((nb, d, c)), vec, vec, _full((groups, WINDOW, WINDOW)), _full((WINDOW, groups))],
        [_rows(tm, d), _rows(tm, n), _rows(tm, d)], [_sds((t, d), ACT), _sds((t, n), ACT), _sds((t, d), ACT)],
        (x, g, w, lg, lb, ws, bs_t), ("parallel",), scratch=[pltpu.VMEM((tm, n), F32)], jobs=jobs)


def _sgu_mix_bwd(z, dy, lg, lb, ws, bs_t, jobs=()):
    t, n = z.shape
    d = n // 2
    groups = d // WINDOW
    tm = _row_tile(t)

    def body(z_ref, dy_ref, lg_ref, lb_ref, ws_ref, bs_ref, dz_ref, dlg_ref, dlb_ref, dws_ref, dbs_ref):
        @pl.when(pl.program_id(0) == 0)
        def _():
            for ref in (dlg_ref, dlb_ref, dws_ref, dbs_ref):
                ref[...] = jnp.zeros_like(ref)

        lane = lax.broadcasted_iota(jnp.int32, (WINDOW, groups), 1)
        gs = [slice(gi * WINDOW, (gi + 1) * WINDOW) for gi in range(groups)]
        wsg = [_causal(ws_ref[gi]).astype(ACT) for gi in range(groups)]
        for c in range(0, tm, WINDOW):
            rows = slice(c, c + WINDOW)
            zu, zv = z_ref[rows, :d].astype(F32), z_ref[rows, d:].astype(F32)
            u = _gelu(zu)
            vh, rs, vn = _sgu_ln(_gelu(zv), lg_ref[...], lb_ref[...])
            dyv = dy_ref[rows, :].astype(F32)
            vng = [vn[:, gs[gi]].astype(ACT) for gi in range(groups)]
            dmix = dyv * u
            dmb = [dmix[:, gs[gi]].astype(ACT) for gi in range(groups)]
            mixed = [_dot(wsg[gi], vng[gi]) for gi in range(groups)]
            dvn_parts = [_dot_tn(wsg[gi], dmb[gi]) for gi in range(groups)]
            dws_parts = [_dot_nt(dmb[gi], vng[gi]) for gi in range(groups)]
            for gi in range(groups):
                dz_ref[rows, gs[gi]] = (dyv[:, gs[gi]] * (mixed[gi] + bs_ref[:, gi:gi + 1]) * _gelu_grad(zu[:, gs[gi]])).astype(ACT)
                dws_ref[:, gs[gi]] += _causal(dws_parts[gi])
                dbs_ref[...] += jnp.where(lane == gi, jnp.sum(dmix[:, gs[gi]], axis=-1, keepdims=True), 0.0)
            dvn = jnp.concatenate(dvn_parts, axis=-1)
            dlg_ref[...] += jnp.sum(dvn * vh, axis=0, keepdims=True)
            dlb_ref[...] += jnp.sum(dvn, axis=0, keepdims=True)
            dvh = dvn * lg_ref[...]
            dv = rs * (dvh - jnp.mean(dvh, axis=-1, keepdims=True) - vh * jnp.mean(dvh * vh, axis=-1, keepdims=True))
            dz_ref[rows, d:] = (dv * _gelu_grad(zv)).astype(ACT)

    vec = _full((1, d))
    return _call(
        body, "sgu_mix_bwd", (t // tm,),
        [_rows(tm, n), _rows(tm, d), vec, vec, _full((groups, WINDOW, WINDOW)), _full((WINDOW, groups))],
        [_rows(tm, n), vec, vec, _full((WINDOW, d)), _full((WINDOW, groups))],
        [_sds((t, n), ACT), _sds((1, d), F32), _sds((1, d), F32), _sds((WINDOW, d), F32), _sds((WINDOW, groups), F32)],
        (z, dy, lg, lb, ws, bs_t), ("arbitrary",), jobs=jobs)


AG_COPIES = 8


def _prepare(sources, dtypes, n_gather, name):
    n = len(sources)
    arrays, where = [], []
    for parts, layer in sources:
        found = []
        for part in parts:
            known = [i for i, v in enumerate(arrays) if v is part]
            if not known:
                arrays.append(part)
            found.append(known[0] if known else len(arrays) - 1)
        where.append((found, layer))
    shapes = [(sum(p.shape[1] for p in parts), parts[0].shape[2]) for parts, _ in sources]

    def body(*refs):
        ins, refs = refs[:len(arrays)], refs[len(arrays):]
        stage, outs = refs[:n], refs[n:n + n_gather]
        send, recv, local_sem = refs[n + n_gather:]
        x, y, c = _place()
        me, sibling, beside_x, beside_y, far = (x, y, c), (x, y, 1 - c), (1 - x, y, c), (x, 1 - y, c), (1 - x, 1 - y, c)
        slot = lambda dev: 4 * dev[0] + 2 * dev[1] + dev[2]
        other = lambda dev: (dev[0], dev[1], 1 - c)
        whole = lambda a: (0, shapes[a][0])
        cuts = [rows // 2 if rows % 32 == 0 else rows for rows, _ in shapes]
        first = lambda a: (0, cuts[a])
        rest = lambda a: (cuts[a], shapes[a][0] - cuts[a])

        def copy(a, k, block, rows, to, src=None):
            dst = outs[a].at[block, pl.ds(*rows)]
            return pltpu.make_async_remote_copy(
                src_ref=dst if src is None else src, dst_ref=dst, send_sem=send.at[a * AG_COPIES + k],
                recv_sem=recv.at[a * AG_COPIES + k], device_id=to, device_id_type=MESH)

        def cast(a):
            found, layer = where[a]
            at = 0
            for i in found:
                rows = arrays[i].shape[1]
                stage[a][at:at + rows, :] = ins[i][layer].astype(dtypes[a])
                at += rows

        for a in range(n_gather):
            cast(a)
        started = []
        for a in range(n_gather):
            own = pltpu.make_async_copy(stage[a], outs[a].at[slot(me)], local_sem.at[a])
            own.start()
            started.append(own)
        sends = []
        for a in range(n_gather):
            sends += [copy(a, k, slot(me), whole(a), to, src=stage[a]) for k, to in enumerate((sibling, beside_x, beside_y))]
        for cp in sends:
            cp.start()
        for a in range(n_gather, n):
            cast(a)
        for a in range(n_gather):
            copy(a, 1, slot(beside_x), whole(a), me).wait_recv()
            copy(a, 2, slot(beside_y), whole(a), me).wait_recv()
            passed = [copy(a, 3, slot(beside_x), first(a), beside_y), copy(a, 5, slot(beside_x), whole(a), sibling),
                      copy(a, 6, slot(beside_y), whole(a), sibling)]
            if rest(a)[1]:
                passed.append(copy(a, 4, slot(beside_y), rest(a), beside_x))
            for cp in passed:
                cp.start()
            sends += passed
        for a in range(n_gather):
            copy(a, 3, slot(far), first(a), me).wait_recv()
            if rest(a)[1]:
                copy(a, 4, slot(far), rest(a), me).wait_recv()
            passed = copy(a, 7, slot(far), whole(a), sibling)
            passed.start()
            sends.append(passed)
        for a in range(n_gather):
            for k, source in ((0, me), (5, beside_x), (6, beside_y), (7, far)):
                copy(a, k, slot(other(source)), whole(a), me).wait_recv()
        for cp in sends:
            cp.wait_send()
        for own in started:
            own.wait()

    res = pl.pallas_call(
        body, name=name,
        in_specs=[IN_VMEM] * len(arrays), out_specs=[IN_VMEM] * n + [ANY] * n_gather,
        out_shape=[_sds(s, dt) for s, dt in zip(shapes, dtypes)]
        + [_sds((N_DEV,) + s, dt) for s, dt in zip(shapes[:n_gather], dtypes)],
        scratch_shapes=[pltpu.SemaphoreType.DMA((n_gather * AG_COPIES,)), pltpu.SemaphoreType.DMA((n_gather * AG_COPIES,)),
                        pltpu.SemaphoreType.DMA((n_gather,))],
        compiler_params=pltpu.CompilerParams(vmem_limit_bytes=VMEM_LIMIT_BYTES),
    )(*arrays)
    return list(res[:n]), list(res[n:])


def _pair_sum(g, r, core, name):
    _, _, rows, cols = g.shape
    tr = _pick(rows, (512, 256, 128))

    def body(core_ref, g_ref, r_ref, p_ref):
        del core_ref
        p_ref[...] = (g_ref[...].astype(F32) + r_ref[...].astype(F32)).astype(p_ref.dtype)

    return pl.pallas_call(
        body, name=name,
        grid_spec=pltpu.PrefetchScalarGridSpec(
            num_scalar_prefetch=1, grid=(4, rows // tr),
            in_specs=[pl.BlockSpec((None, None, tr, cols), lambda q, i, core_ref: (q, core_ref[0], i, 0)),
                      pl.BlockSpec((None, tr, cols), lambda q, i, core_ref: (q, i, 0))],
            out_specs=pl.BlockSpec((None, tr, cols), lambda q, i, core_ref: (q, i, 0))),
        out_shape=_sds((4, rows, cols), g.dtype),
        compiler_params=_params("parallel", "parallel"),
    )(core, g, r)


def _adam(w, g, m, v):
    m2 = ADAM_B1 * m + (1.0 - ADAM_B1) * g
    v2 = ADAM_B2 * v + (1.0 - ADAM_B2) * (g * g)
    m_hat = m2 / (1.0 - ADAM_B1 ** ADAM_STEP)
    v_hat = v2 / (1.0 - ADAM_B2 ** ADAM_STEP)
    return -ADAM_LR * (m_hat / (jnp.sqrt(v_hat) + ADAM_EPS) + ADAM_WD * w), m2, v2


def _shard_update(own, index, received, w, m, v, layer, so_far, name):
    _, rows, cols = w.shape
    n_recv = received.shape[0]
    tr = _shard_tile(rows)

    def body(index_ref, p_ref, q_ref, w_ref, m_ref, v_ref, *rest):
        del index_ref
        g_out, d_out, m_out, v_out = rest[-4:]
        g = p_ref[...].astype(F32)
        for s in range(n_recv):
            g = g + q_ref[s].astype(F32)
        g_out[...] = g
        d_out[...], m_out[...], v_out[...] = _adam(w_ref[...], g, m_ref[...], v_ref[...])

    tile = pl.BlockSpec((None, tr, cols), lambda i, index_ref: (layer, i, 0))
    kept = list(so_far) if so_far is not None else []
    return pl.pallas_call(
        body, name=name,
        grid_spec=pltpu.PrefetchScalarGridSpec(
            num_scalar_prefetch=1, grid=(rows // tr,),
            in_specs=[pl.BlockSpec((None, tr, cols), lambda i, index_ref: (index_ref[0], i, 0)),
                      pl.BlockSpec((n_recv, tr, cols), lambda i, index_ref: (0, i, 0)), tile, tile, tile] + [ANY] * len(kept),
            out_specs=[tile] * 4),
        out_shape=[_sds(w.shape, F32)] * 4,
        input_output_aliases={6 + i: i for i in range(len(kept))},
        compiler_params=_params("parallel"),
    )(index, own, received, w, m, v, *kept)


def _natural_pieces(shape, r, c):
    if tuple(shape[-2:]) == (r, c) and all(n == 1 for n in shape[:-2]):
        return [((0,) * (len(shape) - 2), (0, c))]
    groups, width = shape[1], shape[3]
    assert len(shape) == 4 and shape[0] == 1 and shape[2] == r and groups * width == c, (shape, r, c)
    return [((0, g), (g * width, width)) for g in range(groups)]


def _replicated_update(shares, where, params, name):
    flat = [a for p in params if p is not None for a in p]

    def body(s_ref, *refs):
        ins, outs = refs[:len(flat)], refs[len(flat):]
        total = s_ref[0]
        for dev in range(1, N_DEV):
            total = total + s_ref[dev]
        i_at = o_at = 0
        for ranges, p in zip(where, params):
            chunks = [total[r0:r0 + r, :c] for r0, r, c in ranges]
            g2d = chunks[0] if len(chunks) == 1 else jnp.concatenate(chunks, axis=1)
            if p is None:
                outs[o_at][...] = g2d
                o_at += 1
                continue
            w_ref, m_ref, v_ref = ins[i_at:i_at + 3]
            for idx, (c0, cols) in _natural_pieces(p[0].shape, *g2d.shape):
                at = idx + (slice(None), slice(None))
                g = g2d[:, c0:c0 + cols]
                outs[o_at][at] = g
                outs[o_at + 1][at], outs[o_at + 2][at], outs[o_at + 3][at] = _adam(w_ref[at], g, m_ref[at], v_ref[at])
            i_at, o_at = i_at + 3, o_at + 4

    out_shape = []
    for ranges, p in zip(where, params):
        out_shape += [_sds((ranges[0][1], sum(c for _, _, c in ranges)), F32)] if p is None else [_sds(p[0].shape, F32)] * 4
    res = pl.pallas_call(
        body, name=name, out_shape=out_shape, compiler_params=pltpu.CompilerParams(vmem_limit_bytes=VMEM_LIMIT_BYTES),
    )(shares, *flat)
    out, at = [], 0
    for p in params:
        out.append(list(res[at:at + (1 if p is None else 4)]))
        at += 1 if p is None else 4
    return out


def _rope_tables(t):
    half = HEAD_DIM // 2
    inv_freq = ROPE_THETA ** (-(jnp.arange(half, dtype=F32) * 2.0) / HEAD_DIM)
    ang = jnp.arange(t, dtype=jnp.int32).astype(F32)[:, None] * inv_freq[None, :]
    cos, sin = jnp.cos(ang), jnp.sin(ang)
    return jnp.tile(jnp.concatenate([cos, cos], axis=1), (1, 2)), jnp.tile(jnp.concatenate([-sin, sin], axis=1), (1, 2))


def _rows_full(gathered):
    return gathered.reshape(-1, gathered.shape[-1])


def _rows_blocked(full):
    return full.reshape(N_DEV, full.shape[0] // N_DEV, full.shape[1]).astype(ACT)


def _pack_rows(parts, width):
    rows, where, at = [], [], 0
    for v in parts:
        r, c = v.shape
        n_rows = -(-r // 8) * 8
        chunks = []
        for c0 in range(0, c, width):
            chunk = v[:, c0:c0 + width].astype(F32)
            rows.append(jnp.pad(chunk, ((0, n_rows - r), (0, width - chunk.shape[1]))))
            chunks.append((at, r, chunk.shape[1]))
            at += n_rows
        where.append(chunks)
    return jnp.concatenate(rows, axis=0), where


def _halves(block):
    half = block.shape[0] // 2
    return (0, half), (half, half)


def _whole(block):
    return (0, block.shape[0])


EARLY = ("attn_w_qkv", "sgu_ln", "attn_w_o")
LATE = ("sgu_w_in", "sgu_w_out", "gu0", "gu1", "dn0", "dn1")
SWAPPED = ("attn_w_qkv", "ffn_w_gate_up")
BEFORE_ATTN = ("norm_mix_post", "norm_ffn_pre", "norm_ffn_post", "attn_b_o", "sgu_w_spatial", "sgu_b_spatial")
AFTER_ATTN = ("attn_b_qkv", "attn_sinks")
LAST = ("norm_mix_pre",)
WEIGHTS = ("norm_mix_pre", "norm_mix_post", "norm_ffn_pre", "norm_ffn_post", "attn_w_qkv", "attn_b_qkv", "attn_sinks", "attn_w_o",
           "attn_b_o", "sgu_w_in", "sgu_ln_g", "sgu_ln_b", "sgu_w_spatial", "sgu_b_spatial", "sgu_w_out", "ffn_w_gate_up", "ffn_w_down")


LAYER_OF = {"gu0": ("ffn_w_gate_up", 0), "gu1": ("ffn_w_gate_up", 1), "dn0": ("ffn_w_down", 0), "dn1": ("ffn_w_down", 1)}


def _source(tree, name):
    if name == "sgu_ln":
        return [tree["sgu_ln_g"][None], tree["sgu_ln_b"][None]], 0
    leaf, layer = LAYER_OF.get(name, (name, 0))
    return [tree[leaf]], layer


def kernel(x, norm_mix_pre, norm_mix_post, norm_ffn_pre, norm_ffn_post, attn_w_qkv, attn_b_qkv, attn_sinks, attn_w_o, attn_b_o, sgu_w_in, sgu_ln_g, sgu_ln_b, sgu_w_spatial, sgu_b_spatial, sgu_w_out, ffn_w_gate_up, ffn_w_down, loss_target, m_norm_mix_pre, m_norm_mix_post, m_norm_ffn_pre, m_norm_ffn_post, m_attn_w_qkv, m_attn_b_qkv, m_attn_sinks, m_attn_w_o, m_attn_b_o, m_sgu_w_in, m_sgu_ln_g, m_sgu_ln_b, m_sgu_w_spatial, m_sgu_b_spatial, m_sgu_w_out, m_ffn_w_gate_up, m_ffn_w_down, v_norm_mix_pre, v_norm_mix_post, v_norm_ffn_pre, v_norm_ffn_post, v_attn_w_qkv, v_attn_b_qkv, v_attn_sinks, v_attn_w_o, v_attn_b_o, v_sgu_w_in, v_sgu_ln_g, v_sgu_ln_b, v_sgu_w_spatial, v_sgu_b_spatial, v_sgu_w_out, v_ffn_w_gate_up, v_ffn_w_down):
    w = dict(norm_mix_pre=norm_mix_pre, norm_mix_post=norm_mix_post, norm_ffn_pre=norm_ffn_pre, norm_ffn_post=norm_ffn_post,
             attn_w_qkv=attn_w_qkv, attn_b_qkv=attn_b_qkv, attn_sinks=attn_sinks, attn_w_o=attn_w_o, attn_b_o=attn_b_o,
             sgu_w_in=sgu_w_in, sgu_ln_g=sgu_ln_g, sgu_ln_b=sgu_ln_b, sgu_w_spatial=sgu_w_spatial, sgu_b_spatial=sgu_b_spatial,
             sgu_w_out=sgu_w_out, ffn_w_gate_up=ffn_w_gate_up, ffn_w_down=ffn_w_down)
    mom = dict(norm_mix_pre=m_norm_mix_pre, norm_mix_post=m_norm_mix_post, norm_ffn_pre=m_norm_ffn_pre, norm_ffn_post=m_norm_ffn_post,
               attn_w_qkv=m_attn_w_qkv, attn_b_qkv=m_attn_b_qkv, attn_sinks=m_attn_sinks, attn_w_o=m_attn_w_o, attn_b_o=m_attn_b_o,
               sgu_w_in=m_sgu_w_in, sgu_ln_g=m_sgu_ln_g, sgu_ln_b=m_sgu_ln_b, sgu_w_spatial=m_sgu_w_spatial,
               sgu_b_spatial=m_sgu_b_spatial, sgu_w_out=m_sgu_w_out, ffn_w_gate_up=m_ffn_w_gate_up, ffn_w_down=m_ffn_w_down)
    var = dict(norm_mix_pre=v_norm_mix_pre, norm_mix_post=v_norm_mix_post, norm_ffn_pre=v_norm_ffn_pre, norm_ffn_post=v_norm_ffn_post,
               attn_w_qkv=v_attn_w_qkv, attn_b_qkv=v_attn_b_qkv, attn_sinks=v_attn_sinks, attn_w_o=v_attn_w_o, attn_b_o=v_attn_b_o,
               sgu_w_in=v_sgu_w_in, sgu_ln_g=v_sgu_ln_g, sgu_ln_b=v_sgu_ln_b, sgu_w_spatial=v_sgu_w_spatial,
               sgu_b_spatial=v_sgu_b_spatial, sgu_w_out=v_sgu_w_out, ffn_w_gate_up=v_ffn_w_gate_up, ffn_w_down=v_ffn_w_down)
    w, mom, var = [{**tree, **{n: jnp.swapaxes(tree[n], 1, 2) for n in SWAPPED}} for tree in (w, mom, var)]
    xs, target = x[0], loss_target[0]
    t, d = xs.shape
    row = lambda v: v.reshape(1, -1).astype(F32)
    core = lax.axis_index("c").astype(jnp.int32).reshape(1)
    chip = (2 * lax.axis_index("x") + lax.axis_index("y")).astype(jnp.int32).reshape(1)
    names = EARLY + LATE
    staged, early = _prepare([_source(w, n) for n in names], [F32 if n == "sgu_ln" else ACT for n in names], len(EARLY),
                             "weights_prepare")
    stage = dict(zip(names, staged))
    w_qkv = _rows_full(early[0])
    lg, lb = row(early[1][:, 0, :]), row(early[1][:, 1, :])
    w_o = _rows_full(early[2])
    b_qkv = row(attn_b_qkv)
    sinks = attn_sinks.reshape(1, -1).astype(F32)
    ws = sgu_w_spatial[0].astype(F32)
    bs_t = sgu_b_spatial[0].astype(F32).T
    cos, sin = _rope_tables(t)
    gather = lambda name, so_far, **pieces: _gather_job(stage[name], so_far, **pieces)
    a_half = lambda name: _halves(stage[name])[0]
    b_half = lambda name: _halves(stage[name])[1]
    whole = lambda name: _whole(stage[name])

    def pieces(name, n):
        rows = stage[name].shape[0] // n
        return [(i * rows, rows) for i in range(n)]

    ways = lambda parts: [(piece, i % 2) for i, piece in enumerate(parts)]
    relay = lambda name, so_far, **steps: _relay_gather_job(stage[name], so_far, **steps)
    gu0_parts, gu1_parts = pieces("gu0", 4), pieces("gu1", 4)
    dn0_parts, dn1_parts, in_parts, out_parts = pieces("dn0", 2), pieces("dn1", 2), pieces("sgu_w_in", 2), pieces("sgu_w_out", 2)
    (h0, q, kdup, vdup), ((g_gu0,),) = _attn_qkv_fwd(
        xs, row(norm_mix_pre[0]), w_qkv, b_qkv, cos, sin, jobs=[relay("gu0", None, sends=gu0_parts)])
    (o,), ((g_gu0,), (g_dn0,)) = _attn_fwd(q, kdup, vdup, sinks, jobs=[
        relay("gu0", g_gu0, relays=ways(gu0_parts), forwards=gu0_parts), relay("dn0", None, sends=dn0_parts)])
    (m0, x1), ((g_gu0,), (g_dn0,), (g_in,)) = _proj_res(o, w_o, row(attn_b_o), row(norm_mix_post[0]), xs, "attn_out_fwd", jobs=[
        relay("gu0", g_gu0, far=gu0_parts), relay("dn0", g_dn0, relays=ways(dn0_parts), forwards=dn0_parts, late_far=dn0_parts),
        relay("sgu_w_in", None, sends=in_parts)])
    w_gu0, w_dn0 = _rows_full(g_gu0), _rows_full(g_dn0)
    (h1, gu0, a0, f0, x2), ((g_in,), (g_out,), (g_gu1,), (g_dn1,)) = _ffn_fwd(
        x1, row(norm_ffn_pre[0]), w_gu0, w_dn0, row(norm_ffn_post[0]), jobs=[
            relay("sgu_w_in", g_in, relays=ways(in_parts), forwards=in_parts, late_far=in_parts),
            relay("sgu_w_out", None, sends=out_parts), relay("gu1", None, sends=gu1_parts), relay("dn1", None, sends=dn1_parts)])
    w_in = g_in
    (h2, z, y), ((g_out,), (g_gu1,), (g_dn1,)) = _sgu_fwd(x2, row(norm_mix_pre[1]), w_in, lg, lb, ws, bs_t, jobs=[
        relay("sgu_w_out", g_out, relays=ways(out_parts), forwards=out_parts, late_far=out_parts),
        relay("gu1", g_gu1, relays=ways(gu1_parts), forwards=gu1_parts),
        relay("dn1", g_dn1, relays=ways(dn1_parts), forwards=dn1_parts)])
    w_out = _rows_full(g_out)
    (m1, x3), ((g_gu1,), (g_dn1,)) = _proj_res(y, w_out, None, row(norm_mix_post[1]), x2, "sgu_out_fwd", jobs=[
        relay("gu1", g_gu1, far=gu1_parts), relay("dn1", g_dn1, far=dn1_parts)])
    w_gu1, w_dn1 = _rows_full(g_gu1), _rows_full(g_dn1)

    to_sibling = lambda g: _scatter_job([g], 4, _to_sibling)
    pair = lambda g, r, name: _pair_sum(g.reshape((4, 2) + g.shape[1:]), r, core, "pair_sum_" + name)
    to_chips = lambda p, prev=None, piece=None: _scatter_job([p], 3, _to_owner_chip, prev=prev, piece=piece)
    out_g, out_d, out_m, out_v = {}, {}, {}, {}

    trees = [{**tree, "sgu_ln": jnp.stack([tree["sgu_ln_g"], tree["sgu_ln_b"]], axis=1)} for tree in (w, mom, var)]
    so_far = {}

    def update(name, own, index, received):
        leaf, layer = LAYER_OF.get(name, (name, 0))
        so_far[leaf] = _shard_update(own, index, received, *[tree[leaf] for tree in trees], layer, so_far.get(leaf), "update_" + name)
        for out, val in zip((out_g, out_d, out_m, out_v), so_far[leaf]):
            out[leaf] = val

    def finish(names, n_extra, shares, where, name):
        res = _replicated_update(shares, where, [(w[n], mom[n], var[n]) for n in names] + [None] * n_extra, name)
        for n, vals in zip(names, res):
            out_g[n], out_d[n], out_m[n], out_v[n] = vals
        return [vals[0] for vals in res[len(names):]]

    first_half = lambda p: _halves(p[0])[0]
    second_half = lambda p: _halves(p[0])[1]
    (df1, dgu1, dx3, dg_ffn_post1, dg_ffn_pre1, loss_tile, h3, a1), _ = _ffn_bwd(
        None, None, row(norm_ffn_post[1]), w_dn1, None, w_gu1, x3, row(norm_ffn_pre[1]), "ffn_last", target=target)
    b_gu1, _ = _wgrad(h3, dgu1, "ffn_up_wgrad1", ACT, transposed=True)
    b_gu1 = _rows_blocked(b_gu1)
    dw_dn1, _ = _wgrad(a1, df1, "ffn_down_wgrad1", ACT)
    b_dn1 = _rows_blocked(dw_dn1)
    (dm1, dy, dg_mix_post1), ((r_gu1,), (r_dn1,)) = _post_bwd(
        dx3, m1, row(norm_mix_post[1]), w_out, "sgu", "sgu_out_bwd", jobs=[to_sibling(b_gu1), to_sibling(b_dn1)])
    p_gu1, p_dn1 = pair(b_gu1, r_gu1, "gu1"), pair(b_dn1, r_dn1, "dn1")
    dw_out, _ = _wgrad(y, dm1, "sgu_out_wgrad", ACT)
    b_out = _rows_blocked(dw_out)
    (dz, dlg, dlb, dws, dbs_t), ((q_gu1,), (r_out,)) = _sgu_mix_bwd(z, dy, lg, lb, ws, bs_t, jobs=[
        to_chips(p_gu1, piece=first_half(p_gu1)), to_sibling(b_out)])
    p_out = pair(b_out, r_out, "sgu_w_out")
    b_in, _ = _wgrad(h2, dz, "sgu_in_wgrad", ACT, out_block=stage["sgu_w_in"].shape[1])
    b_ln = jnp.stack([dlg.reshape(N_DEV, -1), dlb.reshape(N_DEV, -1)], axis=1)
    (dx2, dg_mix_pre1), _ = _pre_bwd(dz, w_in, x2, row(norm_mix_pre[1]), dx3, "sgu_in_bwd")
    (df0, dgu0, dx1, dg_ffn_post0, dg_ffn_pre0), ((q_gu1,), (q_dn1,), (q_out,), (r_in,), (r_ln,)) = _ffn_bwd(
        dx2, f0, row(norm_ffn_post[0]), w_dn0, gu0, w_gu0, x1, row(norm_ffn_pre[0]), "ffn_bwd0",
        jobs=[to_chips(p_gu1, prev=[q_gu1], piece=second_half(p_gu1)), to_chips(p_dn1), to_chips(p_out), to_sibling(b_in),
              to_sibling(b_ln)])
    update("gu1", p_gu1, chip, q_gu1)
    update("dn1", p_dn1, chip, q_dn1)
    update("sgu_w_out", p_out, chip, q_out)
    p_in, p_ln = pair(b_in, r_in, "sgu_w_in"), pair(b_ln, r_ln, "sgu_ln")
    b_gu0, ((q_in,), (q_ln,)) = _wgrad(h1, dgu0, "ffn_up_wgrad0", ACT, transposed=True, jobs=[to_chips(p_in), to_chips(p_ln)])
    update("sgu_w_in", p_in, chip, q_in)
    update("sgu_ln", p_ln, chip, q_ln)
    b_gu0 = _rows_blocked(b_gu0)
    (dm0, do, dg_mix_post0, db_o), ((r_gu0,),) = _post_bwd(
        dx1, m0, row(norm_mix_post[0]), w_o, "attn", "attn_out_bwd", jobs=[to_sibling(b_gu0)])
    p_gu0 = pair(b_gu0, r_gu0, "gu0")
    dw_dn0, ((q_gu0,),) = _wgrad(a0, df0, "ffn_down_wgrad0", ACT, jobs=[to_chips(p_gu0, piece=first_half(p_gu0))])
    b_dn0 = _rows_blocked(dw_dn0)
    dw_o, ((r_dn0,),) = _wgrad(o, dm0, "attn_out_wgrad", ACT, jobs=[to_sibling(b_dn0)])
    p_dn0 = pair(b_dn0, r_dn0, "dn0")
    b_o = _rows_blocked(dw_o)
    grads = {
        "norm_mix_post": jnp.concatenate([dg_mix_post0, dg_mix_post1], axis=0),
        "norm_ffn_pre": jnp.concatenate([dg_ffn_pre0, dg_ffn_pre1], axis=0),
        "norm_ffn_post": jnp.concatenate([dg_ffn_post0, dg_ffn_post1], axis=0),
        "attn_b_o": db_o,
        "sgu_w_spatial": dws,
        "sgu_b_spatial": dbs_t.T,
    }
    shares1, where1 = _pack_rows([grads[name] for name in BEFORE_ATTN], d)
    (dqkv, db_qkv, dsink), ((q_gu0,), (q_dn0,), (r_o,), (g_shares1,)) = _attn_bwd(q, kdup, vdup, do, sinks, cos, sin, jobs=[
        to_chips(p_gu0, prev=[q_gu0], piece=second_half(p_gu0)), to_chips(p_dn0), to_sibling(b_o),
        _gather_job(shares1, None, sends=[_whole(shares1)])])
    update("gu0", p_gu0, chip, q_gu0)
    update("dn0", p_dn0, chip, q_dn0)
    p_o = pair(b_o, r_o, "attn_w_o")
    grads.update({"attn_b_qkv": db_qkv, "attn_sinks": dsink[:1, :d // HEAD_DIM]})
    shares2, where2 = _pack_rows([grads[name] for name in AFTER_ATTN] + [loss_tile[:1, :1]], d)
    (dx0, dg_mix_pre0), _ = _pre_bwd(dqkv, w_qkv, xs, row(norm_mix_pre[0]), dx1, "attn_qkv_bwd", True)
    grads["norm_mix_pre"] = jnp.concatenate([dg_mix_pre0, dg_mix_pre1], axis=0)
    shares3, where3 = _pack_rows([grads[name] for name in LAST], d)
    dw_qkv, ((q_o,), (g_shares1,), (g_shares2,), (g_shares3,)) = _wgrad(h0, dqkv, "attn_qkv_wgrad", ACT, transposed=True, jobs=[
        to_chips(p_o), _gather_job(shares1, g_shares1, forwards=[_whole(shares1)]),
        _gather_job(shares2, None, sends=[_whole(shares2)]), _gather_job(shares3, None, sends=[_whole(shares3)])])
    update("attn_w_o", p_o, chip, q_o)
    b_qkv_grad = _rows_blocked(dw_qkv)
    (r_qkv,), (g_shares2,), (g_shares3,) = _copies_only([
        to_sibling(b_qkv_grad), _gather_job(shares2, g_shares2, forwards=[_whole(shares2)]),
        _gather_job(shares3, g_shares3, forwards=[_whole(shares3)])], "grads_tail_to_sibling")
    p_qkv = pair(b_qkv_grad, r_qkv, "attn_w_qkv")
    (q_qkv,), = _copies_only([to_chips(p_qkv)], "grads_tail_to_owner_chip")
    update("attn_w_qkv", p_qkv, chip, q_qkv)

    finish(BEFORE_ATTN, 0, g_shares1, where1, "replicated_update_before_attn")
    loss = finish(AFTER_ATTN, 1, g_shares2, where2, "replicated_update_after_attn")[0][0, 0]
    finish(LAST, 0, g_shares3, where3, "replicated_update_last")

    for out in (out_g, out_d, out_m, out_v):
        out["sgu_ln_g"], out["sgu_ln_b"] = out["sgu_ln"][:, 0, :], out["sgu_ln"][:, 1, :]
        for n in SWAPPED:
            out[n] = jnp.swapaxes(out[n], 1, 2)

    return (loss, dx0[None], *[out_g[n] for n in WEIGHTS], *[out_d[n] for n in WEIGHTS],
            *[out_m[n] for n in WEIGHTS], *[out_v[n] for n in WEIGHTS])
```

```python
import functools
import math
import operator

import jax
import jax.numpy as jnp
from jax import lax
from jax.experimental import pallas as pl
from jax.experimental.pallas import tpu as pltpu

F32 = jnp.float32
ACT = jnp.bfloat16

EPS = 1e-6
HEAD_DIM = 64
PAIR = 2 * HEAD_DIM
GQA_GROUP = 4
WINDOW = 128
ROPE_THETA = 10000.0
SCALE = HEAD_DIM ** -0.5
MASKED = -1e30
LANES = 128
MXU_WIDTH = 256

ADAM_LR, ADAM_B1, ADAM_B2, ADAM_EPS, ADAM_WD, ADAM_STEP = 0.001, 0.9, 0.999, 1e-08, 0.01, 10

N_DEV = 8
VMEM_LIMIT_BYTES = 56 * 1024 * 1024
MESH = pl.DeviceIdType.MESH
ANY = pl.BlockSpec(memory_space=pl.ANY)
IN_VMEM = pl.BlockSpec(memory_space=pltpu.VMEM)


def _pick(n, candidates):
    for c in candidates:
        if n % c == 0:
            return c
    return n


def _row_tile(t, most=512):
    return _pick(t, (most,))


def _shard_tile(rows):
    return next((c for c in (512, 256, 176, 128, 96, 64) if rows % c == 0 and rows >= 2 * c), rows)


def _mxu_chunks(n, most=4 * MXU_WIDTH):
    assert n % MXU_WIDTH == 0 and most % MXU_WIDTH == 0
    return [(c, min(most, n - c)) for c in range(0, n, most)]


def _params(*sem):
    return pltpu.CompilerParams(dimension_semantics=sem, vmem_limit_bytes=VMEM_LIMIT_BYTES)


def _full(shape):
    return pl.BlockSpec(shape, lambda *_: (0,) * len(shape))


def _resident(shape):
    return pl.BlockSpec(shape, lambda *_: (0,) * len(shape), pipeline_mode=pl.Buffered(1))


def _rows(tm, n):
    return pl.BlockSpec((tm, n), lambda i: (i, 0))


def _sds(shape, dtype):
    return jax.ShapeDtypeStruct(shape, dtype)


def _place():
    return lax.axis_index("x"), lax.axis_index("y"), lax.axis_index("c")


def _other_chips(x, y):
    return [(1 - x, y), (x, 1 - y), (1 - x, 1 - y)]


class _Job:
    def __init__(self, inputs, out_shape, aliases, emit, n_remote, n_local=0, n_late=0):
        self.inputs, self.out_shape, self.aliases, self.emit = list(inputs), list(out_shape), dict(aliases), emit
        self.n_remote, self.n_local, self.n_late = n_remote, n_local, n_late


def _call(body, name, grid, in_specs, out_specs, out_shape, args, sem, scratch=(), jobs=()):
    n_in, n_out, n_scr = len(args), len(out_shape), len(scratch)
    j_in = [a for job in jobs for a in job.inputs]
    j_out = [s for job in jobs for s in job.out_shape]
    aliases, at_in, at_out = {}, n_in, n_out
    for job in jobs:
        aliases.update({at_in + i: at_out + o for i, o in job.aliases.items()})
        at_in, at_out = at_in + len(job.inputs), at_out + len(job.out_shape)
    n_remote = sum(job.n_remote for job in jobs)
    n_local = sum(job.n_local for job in jobs)

    def wrapped(*refs):
        ins, jin = refs[:n_in], refs[n_in:n_in + len(j_in)]
        rest = refs[n_in + len(j_in):]
        outs, jout = rest[:n_out], rest[n_out:n_out + len(j_out)]
        scr = rest[n_out + len(j_out):n_out + len(j_out) + n_scr]
        if not jobs:
            body(*ins, *outs, *scr)
            return
        send, recv, local = rest[n_out + len(j_out) + n_scr:]
        ids = [pl.program_id(a) for a in range(len(grid))]
        first = functools.reduce(operator.and_, [i == 0 for i in ids])
        last = functools.reduce(operator.and_, [i == g - 1 for i, g in zip(ids, grid)])

        def descriptors():
            place, found, i, o, r, l = _place(), ([], []), 0, 0, 0, 0
            for job in jobs:
                emitted = job.emit(jin[i:i + len(job.inputs)], jout[o:o + len(job.out_shape)], place)
                for at, (src, dst, to) in enumerate(emitted):
                    if to is None:
                        cp = pltpu.make_async_copy(src, dst, local.at[l])
                        l += 1
                    else:
                        cp = pltpu.make_async_remote_copy(src_ref=src, dst_ref=dst, send_sem=send.at[r], recv_sem=recv.at[r],
                                                          device_id=to, device_id_type=MESH)
                        r += 1
                    found[at >= len(emitted) - job.n_late].append(cp)
                i, o = i + len(job.inputs), o + len(job.out_shape)
            return found

        @pl.when(first)
        def _():
            for cp in descriptors()[0]:
                cp.start()

        body(*ins, *outs, *scr)

        @pl.when(last)
        def _():
            early, late = descriptors()
            for cp in early:
                cp.wait()
            for cp in late:
                cp.start()
            for cp in late:
                cp.wait()

    sems = [pltpu.SemaphoreType.DMA((n_remote,)), pltpu.SemaphoreType.DMA((n_remote,)),
            pltpu.SemaphoreType.DMA((max(n_local, 1),))] if jobs else []
    res = pl.pallas_call(
        wrapped, name=name, grid=grid,
        in_specs=list(in_specs) + [ANY] * len(j_in), out_specs=list(out_specs) + [ANY] * len(j_out),
        out_shape=list(out_shape) + j_out, scratch_shapes=list(scratch) + sems, input_output_aliases=aliases,
        compiler_params=_params(*(("arbitrary",) * len(grid) if jobs else sem)),
    )(*args, *j_in)
    job_res, at = [], n_out
    for job in jobs:
        job_res.append(list(res[at:at + len(job.out_shape)]))
        at += len(job.out_shape)
    return list(res[:n_out]), job_res


def _copies_only(jobs, name):
    def body(o_ref):
        o_ref[...] = jnp.zeros_like(o_ref)

    return _call(body, name, (1,), [], [_full((8, LANES))], [_sds((8, LANES), F32)], (), ("arbitrary",), jobs=jobs)[1]


def _gather_job(stage, gathered, sends=(), forwards=()):
    shape = _sds((N_DEV,) + stage.shape, stage.dtype)
    inputs = ([stage] if sends else []) + ([gathered] if gathered is not None else [])
    aliases = {len(inputs) - 1: 0} if gathered is not None else {}

    def emit(ins, outs, place):
        x, y, c = place
        sibling, chips, mine, g = (x, y, 1 - c), _other_chips(x, y), 4 * x + 2 * y + c, outs[0]
        found = []
        for r0, nr in sends:
            src, dst = ins[0].at[pl.ds(r0, nr)], g.at[mine, pl.ds(r0, nr)]
            found += [(src, dst, None), (src, dst, sibling)] + [(src, dst, (px, py, c)) for px, py in chips]
        for r0, nr in forwards:
            for px, py in chips:
                block = 4 * px + 2 * py + c
                found.append((ins[-1].at[block, pl.ds(r0, nr)], g.at[block, pl.ds(r0, nr)], sibling))
        return found

    return _Job(inputs, [shape], aliases, emit, 4 * len(sends) + 3 * len(forwards), len(sends))


def _relay_gather_job(stage, gathered, sends=(), relays=(), forwards=(), far=(), late_far=()):
    shape = _sds((N_DEV,) + stage.shape, stage.dtype)
    inputs = ([stage] if sends else []) + ([gathered] if gathered is not None else [])
    aliases = {len(inputs) - 1: 0} if gathered is not None else {}

    def emit(ins, outs, place):
        x, y, c = place
        sibling, beside_x, beside_y, g = (x, y, 1 - c), (1 - x, y, c), (x, 1 - y, c), outs[0]
        slot = lambda dev: 4 * dev[0] + 2 * dev[1] + dev[2]
        found = []
        for r0, nr in sends:
            src, dst = ins[0].at[pl.ds(r0, nr)], g.at[slot((x, y, c)), pl.ds(r0, nr)]
            found += [(src, dst, None), (src, dst, sibling), (src, dst, beside_x), (src, dst, beside_y)]

        def passed_on(block, piece, to):
            return ins[-1].at[block, pl.ds(*piece)], g.at[block, pl.ds(*piece)], to

        for piece, way in relays:
            found.append(passed_on(slot(beside_x), piece, beside_y) if way == 0 else passed_on(slot(beside_y), piece, beside_x))
        for piece in forwards:
            found += [passed_on(slot(beside_x), piece, sibling), passed_on(slot(beside_y), piece, sibling)]
        for piece in tuple(far) + tuple(late_far):
            found.append(passed_on(slot((1 - x, 1 - y, c)), piece, sibling))
        return found

    n_remote = 3 * len(sends) + len(relays) + 2 * len(forwards) + len(far) + len(late_far)
    return _Job(inputs, [shape], aliases, emit, n_remote, len(sends), len(late_far))


def _scatter_job(arrays, n_slots, route, prev=None, piece=None):
    shapes = [_sds((n_slots,) + v.shape[1:], v.dtype) for v in arrays]
    inputs = list(arrays) + (list(prev) if prev else [])
    aliases = {len(arrays) + i: i for i in range(len(arrays))} if prev else {}

    def emit(ins, outs, place):
        found = []
        for a in range(len(arrays)):
            for s in range(n_slots):
                block, to = route(s, *place)
                if piece is None:
                    found.append((ins[a].at[block], outs[a].at[s], to))
                else:
                    found.append((ins[a].at[block, pl.ds(*piece)], outs[a].at[s, pl.ds(*piece)], to))
        return found

    return _Job(inputs, shapes, aliases, emit, len(arrays) * n_slots)


def _to_sibling(s, x, y, c):
    return 2 * s + (1 - c), (x, y, 1 - c)


def _to_owner_chip(s, x, y, c):
    px, py = _other_chips(x, y)[s]
    return 2 * px + py, (px, py, c)


def _rstd(x):
    return lax.rsqrt(jnp.mean(x * x, axis=-1, keepdims=True) + EPS)


def _rms_bwd(xhat, r, g, dy):
    dxh = dy * g
    return r * (dxh - xhat * jnp.mean(dxh * xhat, axis=-1, keepdims=True))


def _first_half(rows):
    lane = lax.broadcasted_iota(jnp.int32, (rows, PAIR), 1)
    return (lane % HEAD_DIM) < (HEAD_DIM // 2)


def _rot_half(v, first):
    return jnp.where(first, pltpu.roll(v, PAIR - HEAD_DIM // 2, 1), pltpu.roll(v, HEAD_DIM // 2, 1))


def _rope(v, cos, sin, first):
    return v * cos + _rot_half(v, first) * sin


def _rope_t(dv, cos, sin, first):
    return dv * cos + _rot_half(dv * sin, first)


def _dot(a, b):
    return jnp.dot(a, b, preferred_element_type=F32)


def _dot_nt(a, b):
    return lax.dot_general(a, b, (((1,), (1,)), ((), ())), preferred_element_type=F32)


def _dot_tn(a, b):
    return lax.dot_general(a, b, (((0,), (0,)), ((), ())), preferred_element_type=F32)


def _sigmoid(v):
    return 1.0 / (1.0 + jnp.exp(-v))


_GELU_K = math.sqrt(2.0 / math.pi)
_GELU_C = 0.044715


def _gelu(v):
    return v * (0.5 * (1.0 + jnp.tanh(_GELU_K * (v + _GELU_C * (v * v * v)))))


def _gelu_grad(v):
    t = jnp.tanh(_GELU_K * (v + _GELU_C * (v * v * v)))
    return 0.5 * (1.0 + t) + 0.5 * v * (1.0 - t * t) * (_GELU_K * (1.0 + 3.0 * _GELU_C * (v * v)))


def _attn_qkv_fwd(x, g, w, b, cos, sin, jobs=()):
    t, d = x.shape
    n = w.shape[0]
    kd = n - d
    assert (kd // 2) % PAIR == 0
    tm = _row_tile(t)
    ch = _pick(d, (512,))

    def body(x_ref, g_ref, w_ref, b_ref, cos_ref, sin_ref, h_ref, q_ref, k_ref, v_ref):
        xv = x_ref[...]
        hb = (xv * _rstd(xv) * g_ref[...]).astype(ACT)
        h_ref[...] = hb
        cosv, sinv = cos_ref[...], sin_ref[...]
        first = _first_half(tm)
        left = _lane_halves()[0]

        def proj(lo, width):
            return _dot_nt(hb, w_ref[lo:lo + width, :]) + b_ref[:, lo:lo + width]

        def twice(ref, s, two_heads):
            swapped = pltpu.roll(two_heads, HEAD_DIM, 1)
            ref[:, 2 * s:2 * s + PAIR] = jnp.where(left, two_heads, swapped).astype(ACT)
            ref[:, 2 * s + PAIR:2 * s + 2 * PAIR] = jnp.where(left, swapped, two_heads).astype(ACT)

        for c in range(0, d, ch):
            y = proj(c, ch)
            for s in range(0, ch, PAIR):
                q_ref[:, c + s:c + s + PAIR] = (_rope(y[:, s:s + PAIR], cosv, sinv, first) * SCALE).astype(ACT)
        y = proj(d, kd)
        for s in range(0, kd // 2, PAIR):
            twice(k_ref, s, _rope(y[:, s:s + PAIR], cosv, sinv, first))
            twice(v_ref, s, y[:, kd // 2 + s:kd // 2 + s + PAIR])

    return _call(
        body, "attn_qkv_fwd", (t // tm,),
        [_rows(tm, d), _full((1, d)), _full((n, d)), _full((1, n)), _rows(tm, PAIR), _rows(tm, PAIR)],
        [_rows(tm, d), _rows(tm, d), _rows(tm, kd), _rows(tm, kd)],
        [_sds((t, d), ACT), _sds((t, d), ACT), _sds((t, kd), ACT), _sds((t, kd), ACT)],
        (x, g, w, b, cos, sin), ("parallel",), jobs=jobs)


STACK = GQA_GROUP * WINDOW


def _band_mask(has_prev):
    row = lax.broadcasted_iota(jnp.int32, (STACK, 2 * WINDOW), 0) % WINDOW
    col = lax.broadcasted_iota(jnp.int32, (STACK, 2 * WINDOW), 1)
    return ((col < WINDOW) & (col > row) & has_prev) | ((col >= WINDOW) & (col - WINDOW <= row))


def _lane_halves():
    lane = lax.broadcasted_iota(jnp.int32, (1, PAIR), 1)
    return lane < HEAD_DIM, lane >= HEAD_DIM


def _stack_heads(ref, h, halves):
    parts = []
    for p in range(2):
        pair = ref[:, (2 * h + p) * PAIR:(2 * h + p + 1) * PAIR]
        parts += [jnp.where(half, pair, jnp.zeros_like(pair)) for half in halves]
    return jnp.concatenate(parts, axis=0)


def _sink_column(sink_ref, h):
    row = lax.broadcasted_iota(jnp.int32, (STACK, 1), 0)
    col = jnp.full((STACK, 1), sink_ref[0, GQA_GROUP * h + GQA_GROUP - 1], F32)
    for j in range(GQA_GROUP - 2, -1, -1):
        col = jnp.where(row < (j + 1) * WINDOW, sink_ref[0, GQA_GROUP * h + j], col)
    return col


def _probs(scores, valid, sink):
    s = jnp.where(valid, scores, MASKED)
    m = jnp.maximum(jnp.max(s, axis=-1, keepdims=True), sink)
    p = jnp.exp(s - m)
    psink = jnp.exp(sink - m)
    inv = 1.0 / (jnp.sum(p, axis=-1, keepdims=True) + psink)
    return p * inv, psink * inv


def _attn_fwd(q, kd_, vd, sinks, jobs=()):
    t, d = q.shape
    kd = kd_.shape[1]
    cur = lambda n: (n, 0)
    prev = lambda n: (jnp.maximum(n - 1, 0), 0)

    def body(sink_ref, q_ref, kc_ref, kp_ref, vc_ref, vp_ref, o_ref):
        valid = _band_mask(pl.program_id(0) > 0)
        halves = _lane_halves()
        heads = range(kd // PAIR)
        hs = [slice(h * PAIR, (h + 1) * PAIR) for h in heads]
        scores = [_dot_nt(_stack_heads(q_ref, h, halves), jnp.concatenate([kp_ref[:, hs[h]], kc_ref[:, hs[h]]], axis=0)) for h in heads]
        probs = [_probs(scores[h], valid, _sink_column(sink_ref, h))[0].astype(ACT) for h in heads]
        for h in heads:
            v2 = jnp.concatenate([vp_ref[:, hs[h]], vc_ref[:, hs[h]]], axis=0)
            vs = jnp.concatenate([jnp.where(half, v2, jnp.zeros_like(v2)) for half in halves], axis=0)
            pr = probs[h]
            for p in range(2):
                both = jnp.concatenate([pr[2 * p * WINDOW:(2 * p + 1) * WINDOW], pr[(2 * p + 1) * WINDOW:(2 * p + 2) * WINDOW]], axis=1)
                o_ref[:, (2 * h + p) * PAIR:(2 * h + p + 1) * PAIR] = _dot(both, vs).astype(ACT)

    kv_c, kv_p = pl.BlockSpec((WINDOW, kd), cur), pl.BlockSpec((WINDOW, kd), prev)
    return _call(
        body, "attn_fwd", (t // WINDOW,),
        [pl.BlockSpec(memory_space=pltpu.SMEM), pl.BlockSpec((WINDOW, d), cur), kv_c, kv_p, kv_c, kv_p],
        [pl.BlockSpec((WINDOW, d), cur)], [_sds((t, d), ACT)],
        (sinks, q, kd_, kd_, vd, vd), ("parallel",), jobs=jobs)


def _attn_bwd(q, kd_, vd, do, sinks, cos, sin, jobs=()):
    t, d = q.shape
    kd = kd_.shape[1]
    nb = t // WINDOW
    n_out = d + kd
    assert (kd // 2) % PAIR == 0
    cur = lambda n: (jnp.minimum(n, nb - 1), 0)
    prev = lambda n: (jnp.maximum(jnp.minimum(n, nb - 1) - 1, 0), 0)
    late = lambda n: (jnp.maximum(n - 1, 0), 0)

    def body(sink_ref, q_ref, kc_ref, kp_ref, vc_ref, vp_ref, do_ref, cosc_ref, sinc_ref, cosp_ref, sinp_ref,
             out_ref, db_ref, dsink_ref, dq_keep, dk_keep, dv_keep):
        n = pl.program_id(0)

        @pl.when(n == 0)
        def _():
            for ref in (db_ref, dsink_ref, dq_keep, dk_keep, dv_keep):
                ref[...] = jnp.zeros_like(ref)

        valid = _band_mask(n > 0) & (n < nb)
        halves = _lane_halves()
        first = _first_half(WINDOW)
        slot = lax.broadcasted_iota(jnp.int32, dsink_ref.shape, 1)
        top = lax.broadcasted_iota(jnp.int32, dsink_ref.shape, 0) == 0
        dsink = jnp.zeros(dsink_ref.shape, F32)

        def emit(cols, done):
            out_ref[:, cols] = done.astype(ACT)
            db_ref[:, cols] += jnp.sum(done.astype(F32), axis=0, keepdims=True)

        heads = range(kd // PAIR)
        hs = [slice(h * PAIR, (h + 1) * PAIR) for h in heads]
        k2 = [jnp.concatenate([kp_ref[:, hs[h]], kc_ref[:, hs[h]]], axis=0) for h in heads]
        v2 = [jnp.concatenate([vp_ref[:, hs[h]], vc_ref[:, hs[h]]], axis=0) for h in heads]
        qs = [_stack_heads(q_ref, h, halves) for h in heads]
        dos = [_stack_heads(do_ref, h, halves) for h in heads]
        scores = [_dot_nt(qs[h], k2[h]) for h in heads]
        dps = [_dot_nt(dos[h], v2[h]) for h in heads]
        prs, dss = [], []
        for h in heads:
            pr, psink = _probs(scores[h], valid, _sink_column(sink_ref, h))
            delta = jnp.sum(pr * dps[h], axis=-1, keepdims=True)
            dss.append((pr * (dps[h] - delta)).astype(ACT))
            prs.append(pr.astype(ACT))
            leak = psink * delta
            for j in range(GQA_GROUP):
                share = jnp.sum(leak[j * WINDOW:(j + 1) * WINDOW], axis=0, keepdims=True)
                dsink = dsink - jnp.where(top & (slot == GQA_GROUP * h + j), share, 0.0)
        dqs = [_dot(dss[h], k2[h]) for h in heads]
        dk2 = [_dot_tn(dss[h], qs[h]) for h in heads]
        dv2 = [_dot_tn(prs[h], dos[h]) for h in heads]
        for h in heads:
            for p in range(2):
                ps = slice((2 * h + p) * PAIR, (2 * h + p + 1) * PAIR)
                dqp = jnp.where(halves[0], dqs[h][2 * p * WINDOW:(2 * p + 1) * WINDOW], dqs[h][(2 * p + 1) * WINDOW:(2 * p + 2) * WINDOW])
                emit(ps, dq_keep[:, ps])
                dq_keep[:, ps] = (_rope_t(dqp, cosc_ref[...], sinc_ref[...], first) * SCALE).astype(ACT)
        dks, dvs = [], []
        for h in heads:
            dks.append(dk_keep[:, hs[h]] + _rope_t(dk2[h][:WINDOW], cosp_ref[...], sinp_ref[...], first))
            dk_keep[:, hs[h]] = _rope_t(dk2[h][WINDOW:], cosc_ref[...], sinc_ref[...], first)
            dvs.append(dv_keep[:, hs[h]] + dv2[h][:WINDOW])
            dv_keep[:, hs[h]] = dv2[h][WINDOW:]
        both = lambda v: v + pltpu.roll(v, HEAD_DIM, 1)
        for h in range(0, len(heads), 2):
            at = h // 2 * PAIR
            emit(slice(d + at, d + at + PAIR), jnp.where(halves[0], both(dks[h]), both(dks[h + 1])))
            emit(slice(d + kd // 2 + at, d + kd // 2 + at + PAIR), jnp.where(halves[0], both(dvs[h]), both(dvs[h + 1])))
        dsink_ref[...] += dsink

    kv_c, kv_p = pl.BlockSpec((WINDOW, kd), cur), pl.BlockSpec((WINDOW, kd), prev)
    tab_c, tab_p = pl.BlockSpec((WINDOW, PAIR), cur), pl.BlockSpec((WINDOW, PAIR), prev)
    return _call(
        body, "attn_bwd", (nb + 1,),
        [pl.BlockSpec(memory_space=pltpu.SMEM), pl.BlockSpec((WINDOW, d), cur), kv_c, kv_p, kv_c, kv_p,
         pl.BlockSpec((WINDOW, d), cur), tab_c, tab_c, tab_p, tab_p],
        [pl.BlockSpec((WINDOW, n_out), late), _full((1, n_out)), _full((8, LANES))],
        [_sds((t, n_out), ACT), _sds((1, n_out), F32), _sds((8, LANES), F32)],
        (sinks, q, kd_, kd_, vd, vd, do, cos, sin, cos, sin), ("arbitrary",),
        scratch=[pltpu.VMEM((WINDOW, d), ACT), pltpu.VMEM((WINDOW, kd), F32), pltpu.VMEM((WINDOW, kd), F32)], jobs=jobs)


def _proj_res(a, w, b, g, x, name, jobs=()):
    t = a.shape[0]
    k, d = w.shape
    tm = _row_tile(t)
    has_bias = b is not None

    def body(*refs):
        a_ref, w_ref = refs[:2]
        b_ref = refs[2] if has_bias else None
        g_ref, x_ref, m_ref, xo_ref = refs[2 + has_bias:]
        m = _dot(a_ref[...], w_ref[...])
        if has_bias:
            m = m + b_ref[...]
        m_ref[...] = m.astype(ACT)
        xo_ref[...] = x_ref[...] + (m * _rstd(m)) * g_ref[...]

    ins = [a, w] + ([b] if has_bias else []) + [g, x]
    specs = [_rows(tm, k), _full((k, d))] + ([_full((1, d))] if has_bias else []) + [_full((1, d)), _rows(tm, d)]
    return _call(body, name, (t // tm,), specs, [_rows(tm, d), _rows(tm, d)], [_sds((t, d), ACT), _sds((t, d), F32)], ins,
                 ("parallel",), jobs=jobs)


def _post_bwd(dres, m, g, w, a, mode, name, jobs=()):
    t, d = dres.shape
    k = w.shape[0]
    tm = _row_tile(t)
    kc = _pick(k, (1408, 1024, 512))
    steps = t // tm

    def body(*refs):
        d_ref, m_ref, g_ref, w_ref, a_ref = refs[:5]
        da_ref, dw_ref, dg_ref = refs[5:8]
        db_ref, acc_ref = (refs[8] if mode == "attn" else None), refs[-1]
        i = pl.program_id(0)

        @pl.when(i == 0)
        def _():
            for ref in (dg_ref, acc_ref) + ((db_ref,) if mode == "attn" else ()):
                ref[...] = jnp.zeros_like(ref)

        dv, mv = d_ref[...], m_ref[...].astype(F32)
        r = _rstd(mv)
        mh = mv * r
        dg_ref[...] += jnp.sum(dv * mh, axis=0, keepdims=True)
        dm = _rms_bwd(mh, r, g_ref[...], dv)
        dmb = dm.astype(ACT)
        if mode == "attn":
            db_ref[...] += jnp.sum(dm, axis=0, keepdims=True)
        for c in range(0, k, kc):
            da = _dot_nt(dmb, w_ref[c:c + kc, :])
            da_ref[:, c:c + kc] = da.astype(ACT)
        acc_ref[...] += _dot_tn(a_ref[...], dmb)

        @pl.when(i == steps - 1)
        def _():
            dw_ref[...] = acc_ref[...].astype(ACT)

    ins = [dres, m, g, w, a]
    specs = [_rows(tm, d), _rows(tm, d), _full((1, d)), _full((k, d)), _rows(tm, k)]
    out_specs = [_rows(tm, k), _full((k, d)), _full((1, d))]
    out_shape = [_sds((t, k), ACT), _sds((k, d), ACT), _sds((1, d), F32)]
    if mode == "attn":
        out_specs.append(_full((1, d)))
        out_shape.append(_sds((1, d), F32))
    return _call(body, name, (steps,), specs, out_specs, out_shape, ins, ("arbitrary",), scratch=[pltpu.VMEM((k, d), F32)], jobs=jobs)


def _pre_bwd(dy, w, x, g, dres, name, transposed=False, jobs=()):
    t, d = x.shape
    tm = _row_tile(t)

    def body(dy_ref, w_ref, x_ref, g_ref, dres_ref, dx_ref, dg_ref):
        @pl.when(pl.program_id(0) == 0)
        def _():
            dg_ref[...] = jnp.zeros_like(dg_ref)

        dh = jnp.zeros((tm, d), F32)
        if transposed:
            dh = dh + _dot(dy_ref[...], w_ref[...])
        elif w.ndim == 3:
            c = w.shape[2]
            for j in range(w.shape[0]):
                dh = dh + _dot_nt(dy_ref[:, j * c:(j + 1) * c], w_ref[j])
        else:
            n = w.shape[1]
            nc = _pick(n, (1408, 1024, 512))
            for c in range(0, n, nc):
                dh = dh + _dot_nt(dy_ref[:, c:c + nc], w_ref[:, c:c + nc])
        xv = x_ref[...]
        r = _rstd(xv)
        xh = xv * r
        dg_ref[...] += jnp.sum(dh * xh, axis=0, keepdims=True)
        dx_ref[...] = dres_ref[...] + _rms_bwd(xh, r, g_ref[...], dh)

    return _call(
        body, name, (t // tm,),
        [_rows(tm, dy.shape[1]), _full(w.shape), _rows(tm, d), _full((1, d)), _rows(tm, d)],
        [_rows(tm, d), _full((1, d))], [_sds((t, d), F32), _sds((1, d), F32)],
        (dy, w, x, g, dres), ("arbitrary",), jobs=jobs)


def _ffn_bwd(dres, f, g_post, w_dn, gu, w_gu, x, g_pre, name, target=None, jobs=()):
    t, d = x.shape
    n = w_dn.shape[0]
    tm = _row_tile(t, 256)
    whole = target is not None
    n_in = 6 if whole else 8

    def body(*refs):
        if whole:
            t_ref, gp_ref, wd_ref, wu_ref, x_ref, g_ref = refs[:n_in]
            df_ref, dgu_ref, dx_ref, dgp_ref, dg_ref, loss_ref, h_ref, a_ref, gu_ref = refs[n_in:]
        else:
            d_ref, f_ref, gp_ref, wd_ref, gu_ref, wu_ref, x_ref, g_ref = refs[:n_in]
            df_ref, dgu_ref, dx_ref, dgp_ref, dg_ref = refs[n_in:]

        @pl.when(pl.program_id(0) == 0)
        def _():
            for ref in (dgp_ref, dg_ref) + ((loss_ref,) if whole else ()):
                ref[...] = jnp.zeros_like(ref)

        xv = x_ref[...]
        rx = _rstd(xv)
        xh = xv * rx
        if whole:
            hb = (xh * g_ref[...]).astype(ACT)
            h_ref[...] = hb
            for c, size in _mxu_chunks(n):
                gate = _dot_nt(hb, wu_ref[c:c + size, :])
                up = _dot_nt(hb, wu_ref[n + c:n + c + size, :])
                gu_ref[:, c:c + size] = gate.astype(ACT)
                gu_ref[:, n + c:n + c + size] = up.astype(ACT)
                a_ref[:, c:c + size] = (gate * _sigmoid(gate) * up).astype(ACT)
            fv = _dot(a_ref[...], wd_ref[...])
            r = _rstd(fv)
            fh = fv * r
            err = xv + fh * gp_ref[...] - t_ref[...]
            dv = err * (1.0 / d)
            loss_ref[...] += 0.5 * jnp.sum(jnp.mean(err * err, axis=-1, keepdims=True), axis=0, keepdims=True)
        else:
            dv, fv = d_ref[...], f_ref[...].astype(F32)
            r = _rstd(fv)
            fh = fv * r
        dgp_ref[...] += jnp.sum(dv * fh, axis=0, keepdims=True)
        dfb = _rms_bwd(fh, r, gp_ref[...], dv).astype(ACT)
        df_ref[...] = dfb
        for c, size in _mxu_chunks(n):
            da = _dot_nt(dfb, wd_ref[c:c + size, :]).astype(ACT)
            gate, up = gu_ref[:, c:c + size], gu_ref[:, n + c:n + c + size]
            sg = _sigmoid(gate.astype(F32)).astype(ACT)
            gs = gate * sg
            dgu_ref[:, c:c + size] = da * up * (sg + gs - gs * sg)
            dgu_ref[:, n + c:n + c + size] = da * gs
        dh = _dot(dgu_ref[...], wu_ref[...])
        dg_ref[...] += jnp.sum(dh * xh, axis=0, keepdims=True)
        dx_ref[...] = dv + _rms_bwd(xh, rx, g_ref[...], dh)

    w_specs = [_resident(w_dn.shape), _resident(w_gu.shape)]
    out_specs = [_rows(tm, d), _rows(tm, 2 * n), _rows(tm, d), _full((1, d)), _full((1, d))]
    out_shape = [_sds((t, d), ACT), _sds((t, 2 * n), ACT), _sds((t, d), F32), _sds((1, d), F32), _sds((1, d), F32)]
    if whole:
        return _call(
            body, name, (t // tm,),
            [_rows(tm, d), _full((1, d))] + w_specs + [_rows(tm, d), _full((1, d))],
            out_specs + [_full((8, LANES)), _rows(tm, d), _rows(tm, n)],
            out_shape + [_sds((8, LANES), F32), _sds((t, d), ACT), _sds((t, n), ACT)],
            (target, g_post, w_dn, w_gu, x, g_pre), ("arbitrary",), scratch=[pltpu.VMEM((tm, 2 * n), ACT)], jobs=jobs)
    return _call(
        body, name, (t // tm,),
        [_rows(tm, d), _rows(tm, d), _full((1, d)), w_specs[0], _rows(tm, 2 * n), w_specs[1], _rows(tm, d), _full((1, d))],
        out_specs, out_shape, (dres, f, g_post, w_dn, gu, w_gu, x, g_pre), ("arbitrary",), jobs=jobs)


def _wgrad(a, b, name, out_dtype=F32, out_block=None, transposed=False, jobs=()):
    t = a.shape[0]
    tt = _pick(t, (1024,))
    steps = t // tt
    tk = _pick(a.shape[1], (1024, 1408))
    k, n = a.shape[1], b.shape[1]
    tn = _pick(n, (1024, 1408))
    per = 0
    if transposed:
        out_spec, out_shape, acc_shape = pl.BlockSpec((tn, tk), lambda i, j, s: (j, i)), _sds((n, k), out_dtype), (tn, tk)
    elif out_block is None:
        out_spec, out_shape, acc_shape = pl.BlockSpec((tk, tn), lambda i, j, s: (i, j)), _sds((k, n), out_dtype), (tk, tn)
    else:
        per = tn // out_block
        out_spec = pl.BlockSpec((per, tk, out_block), lambda i, j, s: (j, i, 0))
        out_shape, acc_shape = _sds((n // out_block, k, out_block), out_dtype), (tk, tn)

    def body(a_ref, b_ref, o_ref, acc_ref):
        s = pl.program_id(2)

        @pl.when(s == 0)
        def _():
            acc_ref[...] = jnp.zeros_like(acc_ref)

        acc_ref[...] += _dot_tn(b_ref[...], a_ref[...]) if transposed else _dot_tn(a_ref[...], b_ref[...])

        @pl.when(s == steps - 1)
        def _():
            if per >= 1:
                for p in range(per):
                    o_ref[p] = acc_ref[:, p * out_block:(p + 1) * out_block].astype(out_dtype)
            else:
                o_ref[...] = acc_ref[...].astype(out_dtype)

    res, job_res = _call(
        body, name, (k // tk, n // tn, steps),
        [pl.BlockSpec((tt, tk), lambda i, j, s: (s, i)), pl.BlockSpec((tt, tn), lambda i, j, s: (s, j))], [out_spec], [out_shape],
        (a, b), ("parallel", "parallel", "arbitrary"), scratch=[pltpu.VMEM(acc_shape, F32)], jobs=jobs)
    return res[0], job_res


def _ffn_fwd(x, g_pre, w_gu, w_dn, g_post, jobs=()):
    t, d = x.shape
    n = w_dn.shape[0]
    tm = _row_tile(t)

    def body(x_ref, g_ref, wu_ref, wd_ref, gp_ref, h_ref, gu_ref, a_ref, f_ref, xo_ref):
        xv = x_ref[...]
        hb = (xv * _rstd(xv) * g_ref[...]).astype(ACT)
        h_ref[...] = hb
        for c, size in _mxu_chunks(n):
            gate = _dot_nt(hb, wu_ref[c:c + size, :])
            up = _dot_nt(hb, wu_ref[n + c:n + c + size, :])
            gu_ref[:, c:c + size] = gate.astype(ACT)
            gu_ref[:, n + c:n + c + size] = up.astype(ACT)
            a_ref[:, c:c + size] = (gate * _sigmoid(gate) * up).astype(ACT)
        f = _dot(a_ref[...], wd_ref[...])
        f_ref[...] = f.astype(ACT)
        xo_ref[...] = xv + (f * _rstd(f)) * gp_ref[...]

    return _call(
        body, "ffn_fwd", (t // tm,),
        [_rows(tm, d), _full((1, d)), _resident(w_gu.shape), _resident(w_dn.shape), _full((1, d))],
        [_rows(tm, d), _rows(tm, 2 * n), _rows(tm, n), _rows(tm, d), _rows(tm, d)],
        [_sds((t, d), ACT), _sds((t, 2 * n), ACT), _sds((t, n), ACT), _sds((t, d), ACT), _sds((t, d), F32)],
        (x, g_pre, w_gu, w_dn, g_post), ("parallel",), jobs=jobs)


def _causal(ws):
    row = lax.broadcasted_iota(jnp.int32, ws.shape, 0)
    col = lax.broadcasted_iota(jnp.int32, ws.shape, 1)
    return jnp.where(col <= row, ws, 0.0)


def _sgu_ln(v, lg, lb):
    mu = jnp.mean(v, axis=-1, keepdims=True)
    vc = v - mu
    rs = lax.rsqrt(jnp.mean(vc * vc, axis=-1, keepdims=True) + EPS)
    vh = vc * rs
    return vh, rs, vh * lg + lb


def _sgu_fwd(x, g, w, lg, lb, ws, bs_t, jobs=()):
    t, d = x.shape
    nb, _, c = w.shape
    n = nb * c
    groups = d // WINDOW
    tm = _row_tile(t)

    def body(x_ref, g_ref, w_ref, lg_ref, lb_ref, ws_ref, bs_ref, h_ref, z_ref, y_ref, zf_ref):
        xv = x_ref[...]
        hb = (xv * _rstd(xv) * g_ref[...]).astype(ACT)
        h_ref[...] = hb
        for j in range(nb):
            zf_ref[:, j * c:(j + 1) * c] = _dot(hb, w_ref[j])
        z_ref[...] = zf_ref[...].astype(ACT)
        gs = [slice(gi * WINDOW, (gi + 1) * WINDOW) for gi in range(groups)]
        wsg = [_causal(ws_ref[gi]).astype(ACT) for gi in range(groups)]
        for r in range(0, tm, WINDOW):
            u = _gelu(zf_ref[r:r + WINDOW, :d])
            _, _, vn = _sgu_ln(_gelu(zf_ref[r:r + WINDOW, d:]), lg_ref[...], lb_ref[...])
            mixed = [_dot(wsg[gi], vn[:, gs[gi]].astype(ACT)) for gi in range(groups)]
            for gi in range(groups):
                y_ref[r:r + WINDOW, gs[gi]] = (u[:, gs[gi]] * (mixed[gi] + bs_ref[:, gi:gi + 1])).astype(ACT)

    vec = _full((1, d))
    return _call(
        body, "sgu_fwd", (t // tm,),
        [_rows(tm, d), vec, _full((nb, d, c)), vec, vec, _full((groups, WINDOW, WINDOW)), _full((WINDOW, groups))],
        [_rows(tm, d), _rows(tm, n), _rows(tm, d)], [_sds((t, d), ACT), _sds((t, n), ACT), _sds((t, d), ACT)],
        (x, g, w, lg, lb, ws, bs_t), ("parallel",), scratch=[pltpu.VMEM((tm, n), F32)], jobs=jobs)


def _sgu_mix_bwd(z, dy, lg, lb, ws, bs_t, jobs=()):
    t, n = z.shape
    d = n // 2
    groups = d // WINDOW
    tm = _row_tile(t)

    def body(z_ref, dy_ref, lg_ref, lb_ref, ws_ref, bs_ref, dz_ref, dlg_ref, dlb_ref, dws_ref, dbs_ref):
        @pl.when(pl.program_id(0) == 0)
        def _():
            for ref in (dlg_ref, dlb_ref, dws_ref, dbs_ref):
                ref[...] = jnp.zeros_like(ref)

        lane = lax.broadcasted_iota(jnp.int32, (WINDOW, groups), 1)
        gs = [slice(gi * WINDOW, (gi + 1) * WINDOW) for gi in range(groups)]
        wsg = [_causal(ws_ref[gi]).astype(ACT) for gi in range(groups)]
        for c in range(0, tm, WINDOW):
            rows = slice(c, c + WINDOW)
            zu, zv = z_ref[rows, :d].astype(F32), z_ref[rows, d:].astype(F32)
            u = _gelu(zu)
            vh, rs, vn = _sgu_ln(_gelu(zv), lg_ref[...], lb_ref[...])
            dyv = dy_ref[rows, :].astype(F32)
            vng = [vn[:, gs[gi]].astype(ACT) for gi in range(groups)]
            dmix = dyv * u
            dmb = [dmix[:, gs[gi]].astype(ACT) for gi in range(groups)]
            mixed = [_dot(wsg[gi], vng[gi]) for gi in range(groups)]
            dvn_parts = [_dot_tn(wsg[gi], dmb[gi]) for gi in range(groups)]
            dws_parts = [_dot_nt(dmb[gi], vng[gi]) for gi in range(groups)]
            for gi in range(groups):
                dz_ref[rows, gs[gi]] = (dyv[:, gs[gi]] * (mixed[gi] + bs_ref[:, gi:gi + 1]) * _gelu_grad(zu[:, gs[gi]])).astype(ACT)
                dws_ref[:, gs[gi]] += _causal(dws_parts[gi])
                dbs_ref[...] += jnp.where(lane == gi, jnp.sum(dmix[:, gs[gi]], axis=-1, keepdims=True), 0.0)
            dvn = jnp.concatenate(dvn_parts, axis=-1)
            dlg_ref[...] += jnp.sum(dvn * vh, axis=0, keepdims=True)
            dlb_ref[...] += jnp.sum(dvn, axis=0, keepdims=True)
            dvh = dvn * lg_ref[...]
            dv = rs * (dvh - jnp.mean(dvh, axis=-1, keepdims=True) - vh * jnp.mean(dvh * vh, axis=-1, keepdims=True))
            dz_ref[rows, d:] = (dv * _gelu_grad(zv)).astype(ACT)

    vec = _full((1, d))
    return _call(
        body, "sgu_mix_bwd", (t // tm,),
        [_rows(tm, n), _rows(tm, d), vec, vec, _full((groups, WINDOW, WINDOW)), _full((WINDOW, groups))],
        [_rows(tm, n), vec, vec, _full((WINDOW, d)), _full((WINDOW, groups))],
        [_sds((t, n), ACT), _sds((1, d), F32), _sds((1, d), F32), _sds((WINDOW, d), F32), _sds((WINDOW, groups), F32)],
        (z, dy, lg, lb, ws, bs_t), ("arbitrary",), jobs=jobs)


AG_COPIES = 8


def _prepare(sources, dtypes, n_gather, name):
    n = len(sources)
    arrays, where = [], []
    for parts, layer in sources:
        found = []
        for part in parts:
            known = [i for i, v in enumerate(arrays) if v is part]
            if not known:
                arrays.append(part)
            found.append(known[0] if known else len(arrays) - 1)
        where.append((found, layer))
    shapes = [(sum(p.shape[1] for p in parts), parts[0].shape[2]) for parts, _ in sources]

    def body(*refs):
        ins, refs = refs[:len(arrays)], refs[len(arrays):]
        stage, outs = refs[:n], refs[n:n + n_gather]
        send, recv, local_sem = refs[n + n_gather:]
        x, y, c = _place()
        me, sibling, beside_x, beside_y, far = (x, y, c), (x, y, 1 - c), (1 - x, y, c), (x, 1 - y, c), (1 - x, 1 - y, c)
        slot = lambda dev: 4 * dev[0] + 2 * dev[1] + dev[2]
        other = lambda dev: (dev[0], dev[1], 1 - c)
        whole = lambda a: (0, shapes[a][0])
        cuts = [rows // 2 if rows % 32 == 0 else rows for rows, _ in shapes]
        first = lambda a: (0, cuts[a])
        rest = lambda a: (cuts[a], shapes[a][0] - cuts[a])

        def copy(a, k, block, rows, to, src=None):
            dst = outs[a].at[block, pl.ds(*rows)]
            return pltpu.make_async_remote_copy(
                src_ref=dst if src is None else src, dst_ref=dst, send_sem=send.at[a * AG_COPIES + k],
                recv_sem=recv.at[a * AG_COPIES + k], device_id=to, device_id_type=MESH)

        def cast(a):
            found, layer = where[a]
            at = 0
            for i in found:
                rows = arrays[i].shape[1]
                stage[a][at:at + rows, :] = ins[i][layer].astype(dtypes[a])
                at += rows

        for a in range(n_gather):
            cast(a)
        started = []
        for a in range(n_gather):
            own = pltpu.make_async_copy(stage[a], outs[a].at[slot(me)], local_sem.at[a])
            own.start()
            started.append(own)
        sends = []
        for a in range(n_gather):
            sends += [copy(a, k, slot(me), whole(a), to, src=stage[a]) for k, to in enumerate((sibling, beside_x, beside_y))]
        for cp in sends:
            cp.start()
        for a in range(n_gather, n):
            cast(a)
        for a in range(n_gather):
            copy(a, 1, slot(beside_x), whole(a), me).wait_recv()
            copy(a, 2, slot(beside_y), whole(a), me).wait_recv()
            passed = [copy(a, 3, slot(beside_x), first(a), beside_y), copy(a, 5, slot(beside_x), whole(a), sibling),
                      copy(a, 6, slot(beside_y), whole(a), sibling)]
            if rest(a)[1]:
                passed.append(copy(a, 4, slot(beside_y), rest(a), beside_x))
            for cp in passed:
                cp.start()
            sends += passed
        for a in range(n_gather):
            copy(a, 3, slot(far), first(a), me).wait_recv()
            if rest(a)[1]:
                copy(a, 4, slot(far), rest(a), me).wait_recv()
            passed = copy(a, 7, slot(far), whole(a), sibling)
            passed.start()
            sends.append(passed)
        for a in range(n_gather):
            for k, source in ((0, me), (5, beside_x), (6, beside_y), (7, far)):
                copy(a, k, slot(other(source)), whole(a), me).wait_recv()
        for cp in sends:
            cp.wait_send()
        for own in started:
            own.wait()

    res = pl.pallas_call(
        body, name=name,
        in_specs=[IN_VMEM] * len(arrays), out_specs=[IN_VMEM] * n + [ANY] * n_gather,
        out_shape=[_sds(s, dt) for s, dt in zip(shapes, dtypes)]
        + [_sds((N_DEV,) + s, dt) for s, dt in zip(shapes[:n_gather], dtypes)],
        scratch_shapes=[pltpu.SemaphoreType.DMA((n_gather * AG_COPIES,)), pltpu.SemaphoreType.DMA((n_gather * AG_COPIES,)),
                        pltpu.SemaphoreType.DMA((n_gather,))],
        compiler_params=pltpu.CompilerParams(vmem_limit_bytes=VMEM_LIMIT_BYTES),
    )(*arrays)
    return list(res[:n]), list(res[n:])


def _pair_sum(g, r, core, name):
    _, _, rows, cols = g.shape
    tr = _pick(rows, (512, 256, 128))

    def body(core_ref, g_ref, r_ref, p_ref):
        del core_ref
        p_ref[...] = (g_ref[...].astype(F32) + r_ref[...].astype(F32)).astype(p_ref.dtype)

    return pl.pallas_call(
        body, name=name,
        grid_spec=pltpu.PrefetchScalarGridSpec(
            num_scalar_prefetch=1, grid=(4, rows // tr),
            in_specs=[pl.BlockSpec((None, None, tr, cols), lambda q, i, core_ref: (q, core_ref[0], i, 0)),
                      pl.BlockSpec((None, tr, cols), lambda q, i, core_ref: (q, i, 0))],
            out_specs=pl.BlockSpec((None, tr, cols), lambda q, i, core_ref: (q, i, 0))),
        out_shape=_sds((4, rows, cols), g.dtype),
        compiler_params=_params("parallel", "parallel"),
    )(core, g, r)


def _adam(w, g, m, v):
    m2 = ADAM_B1 * m + (1.0 - ADAM_B1) * g
    v2 = ADAM_B2 * v + (1.0 - ADAM_B2) * (g * g)
    m_hat = m2 / (1.0 - ADAM_B1 ** ADAM_STEP)
    v_hat = v2 / (1.0 - ADAM_B2 ** ADAM_STEP)
    return -ADAM_LR * (m_hat / (jnp.sqrt(v_hat) + ADAM_EPS) + ADAM_WD * w), m2, v2


def _shard_update(own, index, received, w, m, v, layer, so_far, name):
    _, rows, cols = w.shape
    n_recv = received.shape[0]
    tr = _shard_tile(rows)

    def body(index_ref, p_ref, q_ref, w_ref, m_ref, v_ref, *rest):
        del index_ref
        g_out, d_out, m_out, v_out = rest[-4:]
        g = p_ref[...].astype(F32)
        for s in range(n_recv):
            g = g + q_ref[s].astype(F32)
        g_out[...] = g
        d_out[...], m_out[...], v_out[...] = _adam(w_ref[...], g, m_ref[...], v_ref[...])

    tile = pl.BlockSpec((None, tr, cols), lambda i, index_ref: (layer, i, 0))
    kept = list(so_far) if so_far is not None else []
    return pl.pallas_call(
        body, name=name,
        grid_spec=pltpu.PrefetchScalarGridSpec(
            num_scalar_prefetch=1, grid=(rows // tr,),
            in_specs=[pl.BlockSpec((None, tr, cols), lambda i, index_ref: (index_ref[0], i, 0)),
                      pl.BlockSpec((n_recv, tr, cols), lambda i, index_ref: (0, i, 0)), tile, tile, tile] + [ANY] * len(kept),
            out_specs=[tile] * 4),
        out_shape=[_sds(w.shape, F32)] * 4,
        input_output_aliases={6 + i: i for i in range(len(kept))},
        compiler_params=_params("parallel"),
    )(index, own, received, w, m, v, *kept)


def _natural_pieces(shape, r, c):
    if tuple(shape[-2:]) == (r, c) and all(n == 1 for n in shape[:-2]):
        return [((0,) * (len(shape) - 2), (0, c))]
    groups, width = shape[1], shape[3]
    assert len(shape) == 4 and shape[0] == 1 and shape[2] == r and groups * width == c, (shape, r, c)
    return [((0, g), (g * width, width)) for g in range(groups)]


def _replicated_update(shares, where, params, name):
    flat = [a for p in params if p is not None for a in p]

    def body(s_ref, *refs):
        ins, outs = refs[:len(flat)], refs[len(flat):]
        total = s_ref[0]
        for dev in range(1, N_DEV):
            total = total + s_ref[dev]
        i_at = o_at = 0
        for ranges, p in zip(where, params):
            chunks = [total[r0:r0 + r, :c] for r0, r, c in ranges]
            g2d = chunks[0] if len(chunks) == 1 else jnp.concatenate(chunks, axis=1)
            if p is None:
                outs[o_at][...] = g2d
                o_at += 1
                continue
            w_ref, m_ref, v_ref = ins[i_at:i_at + 3]
            for idx, (c0, cols) in _natural_pieces(p[0].shape, *g2d.shape):
                at = idx + (slice(None), slice(None))
                g = g2d[:, c0:c0 + cols]
                outs[o_at][at] = g
                outs[o_at + 1][at], outs[o_at + 2][at], outs[o_at + 3][at] = _adam(w_ref[at], g, m_ref[at], v_ref[at])
            i_at, o_at = i_at + 3, o_at + 4

    out_shape = []
    for ranges, p in zip(where, params):
        out_shape += [_sds((ranges[0][1], sum(c for _, _, c in ranges)), F32)] if p is None else [_sds(p[0].shape, F32)] * 4
    res = pl.pallas_call(
        body, name=name, out_shape=out_shape, compiler_params=pltpu.CompilerParams(vmem_limit_bytes=VMEM_LIMIT_BYTES),
    )(shares, *flat)
    out, at = [], 0
    for p in params:
        out.append(list(res[at:at + (1 if p is None else 4)]))
        at += 1 if p is None else 4
    return out


def _rope_tables(t):
    half = HEAD_DIM // 2
    inv_freq = ROPE_THETA ** (-(jnp.arange(half, dtype=F32) * 2.0) / HEAD_DIM)
    ang = jnp.arange(t, dtype=jnp.int32).astype(F32)[:, None] * inv_freq[None, :]
    cos, sin = jnp.cos(ang), jnp.sin(ang)
    return jnp.tile(jnp.concatenate([cos, cos], axis=1), (1, 2)), jnp.tile(jnp.concatenate([-sin, sin], axis=1), (1, 2))


def _rows_full(gathered):
    return gathered.reshape(-1, gathered.shape[-1])


def _rows_blocked(full):
    return full.reshape(N_DEV, full.shape[0] // N_DEV, full.shape[1]).astype(ACT)


def _pack_rows(parts, width):
    rows, where, at = [], [], 0
    for v in parts:
        r, c = v.shape
        n_rows = -(-r // 8) * 8
        chunks = []
        for c0 in range(0, c, width):
            chunk = v[:, c0:c0 + width].astype(F32)
            rows.append(jnp.pad(chunk, ((0, n_rows - r), (0, width - chunk.shape[1]))))
            chunks.append((at, r, chunk.shape[1]))
            at += n_rows
        where.append(chunks)
    return jnp.concatenate(rows, axis=0), where


def _halves(block):
    half = block.shape[0] // 2
    return (0, half), (half, half)


def _whole(block):
    return (0, block.shape[0])


EARLY = ("attn_w_qkv", "sgu_ln", "attn_w_o")
LATE = ("sgu_w_in", "sgu_w_out", "gu0", "gu1", "dn0", "dn1")
SWAPPED = ("attn_w_qkv", "ffn_w_gate_up")
BEFORE_ATTN = ("norm_mix_post", "norm_ffn_pre", "norm_ffn_post", "attn_b_o", "sgu_w_spatial", "sgu_b_spatial")
AFTER_ATTN = ("attn_b_qkv", "attn_sinks")
LAST = ("norm_mix_pre",)
WEIGHTS = ("norm_mix_pre", "norm_mix_post", "norm_ffn_pre", "norm_ffn_post", "attn_w_qkv", "attn_b_qkv", "attn_sinks", "attn_w_o",
           "attn_b_o", "sgu_w_in", "sgu_ln_g", "sgu_ln_b", "sgu_w_spatial", "sgu_b_spatial", "sgu_w_out", "ffn_w_gate_up", "ffn_w_down")


LAYER_OF = {"gu0": ("ffn_w_gate_up", 0), "gu1": ("ffn_w_gate_up", 1), "dn0": ("ffn_w_down", 0), "dn1": ("ffn_w_down", 1)}


def _source(tree, name):
    if name == "sgu_ln":
        return [tree["sgu_ln_g"][None], tree["sgu_ln_b"][None]], 0
    leaf, layer = LAYER_OF.get(name, (name, 0))
    return [tree[leaf]], layer


def kernel(x, norm_mix_pre, norm_mix_post, norm_ffn_pre, norm_ffn_post, attn_w_qkv, attn_b_qkv, attn_sinks, attn_w_o, attn_b_o, sgu_w_in, sgu_ln_g, sgu_ln_b, sgu_w_spatial, sgu_b_spatial, sgu_w_out, ffn_w_gate_up, ffn_w_down, loss_target, m_norm_mix_pre, m_norm_mix_post, m_norm_ffn_pre, m_norm_ffn_post, m_attn_w_qkv, m_attn_b_qkv, m_attn_sinks, m_attn_w_o, m_attn_b_o, m_sgu_w_in, m_sgu_ln_g, m_sgu_ln_b, m_sgu_w_spatial, m_sgu_b_spatial, m_sgu_w_out, m_ffn_w_gate_up, m_ffn_w_down, v_norm_mix_pre, v_norm_mix_post, v_norm_ffn_pre, v_norm_ffn_post, v_attn_w_qkv, v_attn_b_qkv, v_attn_sinks, v_attn_w_o, v_attn_b_o, v_sgu_w_in, v_sgu_ln_g, v_sgu_ln_b, v_sgu_w_spatial, v_sgu_b_spatial, v_sgu_w_out, v_ffn_w_gate_up, v_ffn_w_down):
    w = dict(norm_mix_pre=norm_mix_pre, norm_mix_post=norm_mix_post, norm_ffn_pre=norm_ffn_pre, norm_ffn_post=norm_ffn_post,
             attn_w_qkv=attn_w_qkv, attn_b_qkv=attn_b_qkv, attn_sinks=attn_sinks, attn_w_o=attn_w_o, attn_b_o=attn_b_o,
             sgu_w_in=sgu_w_in, sgu_ln_g=sgu_ln_g, sgu_ln_b=sgu_ln_b, sgu_w_spatial=sgu_w_spatial, sgu_b_spatial=sgu_b_spatial,
             sgu_w_out=sgu_w_out, ffn_w_gate_up=ffn_w_gate_up, ffn_w_down=ffn_w_down)
    mom = dict(norm_mix_pre=m_norm_mix_pre, norm_mix_post=m_norm_mix_post, norm_ffn_pre=m_norm_ffn_pre, norm_ffn_post=m_norm_ffn_post,
               attn_w_qkv=m_attn_w_qkv, attn_b_qkv=m_attn_b_qkv, attn_sinks=m_attn_sinks, attn_w_o=m_attn_w_o, attn_b_o=m_attn_b_o,
               sgu_w_in=m_sgu_w_in, sgu_ln_g=m_sgu_ln_g, sgu_ln_b=m_sgu_ln_b, sgu_w_spatial=m_sgu_w_spatial,
               sgu_b_spatial=m_sgu_b_spatial, sgu_w_out=m_sgu_w_out, ffn_w_gate_up=m_ffn_w_gate_up, ffn_w_down=m_ffn_w_down)
    var = dict(norm_mix_pre=v_norm_mix_pre, norm_mix_post=v_norm_mix_post, norm_ffn_pre=v_norm_ffn_pre, norm_ffn_post=v_norm_ffn_post,
               attn_w_qkv=v_attn_w_qkv, attn_b_qkv=v_attn_b_qkv, attn_sinks=v_attn_sinks, attn_w_o=v_attn_w_o, attn_b_o=v_attn_b_o,
               sgu_w_in=v_sgu_w_in, sgu_ln_g=v_sgu_ln_g, sgu_ln_b=v_sgu_ln_b, sgu_w_spatial=v_sgu_w_spatial,
               sgu_b_spatial=v_sgu_b_spatial, sgu_w_out=v_sgu_w_out, ffn_w_gate_up=v_ffn_w_gate_up, ffn_w_down=v_ffn_w_down)
    w, mom, var = [{**tree, **{n: jnp.swapaxes(tree[n], 1, 2) for n in SWAPPED}} for tree in (w, mom, var)]
    xs, target = x[0], loss_target[0]
    t, d = xs.shape
    row = lambda v: v.reshape(1, -1).astype(F32)
    core = lax.axis_index("c").astype(jnp.int32).reshape(1)
    chip = (2 * lax.axis_index("x") + lax.axis_index("y")).astype(jnp.int32).reshape(1)
    names = EARLY + LATE
    staged, early = _prepare([_source(w, n) for n in names], [F32 if n == "sgu_ln" else ACT for n in names], len(EARLY),
                             "weights_prepare")
    stage = dict(zip(names, staged))
    w_qkv = _rows_full(early[0])
    lg, lb = row(early[1][:, 0, :]), row(early[1][:, 1, :])
    w_o = _rows_full(early[2])
    b_qkv = row(attn_b_qkv)
    sinks = attn_sinks.reshape(1, -1).astype(F32)
    ws = sgu_w_spatial[0].astype(F32)
    bs_t = sgu_b_spatial[0].astype(F32).T
    cos, sin = _rope_tables(t)
    gather = lambda name, so_far, **pieces: _gather_job(stage[name], so_far, **pieces)
    a_half = lambda name: _halves(stage[name])[0]
    b_half = lambda name: _halves(stage[name])[1]
    whole = lambda name: _whole(stage[name])

    def pieces(name, n):
        rows = stage[name].shape[0] // n
        return [(i * rows, rows) for i in range(n)]

    ways = lambda parts: [(piece, i % 2) for i, piece in enumerate(parts)]
    relay = lambda name, so_far, **steps: _relay_gather_job(stage[name], so_far, **steps)
    gu0_parts, gu1_parts = pieces("gu0", 4), pieces("gu1", 4)
    dn0_parts, dn1_parts, in_parts, out_parts = pieces("dn0", 2), pieces("dn1", 2), pieces("sgu_w_in", 2), pieces("sgu_w_out", 2)
    (h0, q, kdup, vdup), ((g_gu0,),) = _attn_qkv_fwd(
        xs, row(norm_mix_pre[0]), w_qkv, b_qkv, cos, sin, jobs=[relay("gu0", None, sends=gu0_parts)])
    (o,), ((g_gu0,), (g_dn0,)) = _attn_fwd(q, kdup, vdup, sinks, jobs=[
        relay("gu0", g_gu0, relays=ways(gu0_parts), forwards=gu0_parts), relay("dn0", None, sends=dn0_parts)])
    (m0, x1), ((g_gu0,), (g_dn0,), (g_in,)) = _proj_res(o, w_o, row(attn_b_o), row(norm_mix_post[0]), xs, "attn_out_fwd", jobs=[
        relay("gu0", g_gu0, far=gu0_parts), relay("dn0", g_dn0, relays=ways(dn0_parts), forwards=dn0_parts, late_far=dn0_parts),
        relay("sgu_w_in", None, sends=in_parts)])
    w_gu0, w_dn0 = _rows_full(g_gu0), _rows_full(g_dn0)
    (h1, gu0, a0, f0, x2), ((g_in,), (g_out,), (g_gu1,), (g_dn1,)) = _ffn_fwd(
        x1, row(norm_ffn_pre[0]), w_gu0, w_dn0, row(norm_ffn_post[0]), jobs=[
            relay("sgu_w_in", g_in, relays=ways(in_parts), forwards=in_parts, late_far=in_parts),
            relay("sgu_w_out", None, sends=out_parts), relay("gu1", None, sends=gu1_parts), relay("dn1", None, sends=dn1_parts)])
    w_in = g_in
    (h2, z, y), ((g_out,), (g_gu1,), (g_dn1,)) = _sgu_fwd(x2, row(norm_mix_pre[1]), w_in, lg, lb, ws, bs_t, jobs=[
        relay("sgu_w_out", g_out, relays=ways(out_parts), forwards=out_parts, late_far=out_parts),
        relay("gu1", g_gu1, relays=ways(gu1_parts), forwards=gu1_parts),
        relay("dn1", g_dn1, relays=ways(dn1_parts), forwards=dn1_parts)])
    w_out = _rows_full(g_out)
    (m1, x3), ((g_gu1,), (g_dn1,)) = _proj_res(y, w_out, None, row(norm_mix_post[1]), x2, "sgu_out_fwd", jobs=[
        relay("gu1", g_gu1, far=gu1_parts), relay("dn1", g_dn1, far=dn1_parts)])
    w_gu1, w_dn1 = _rows_full(g_gu1), _rows_full(g_dn1)

    to_sibling = lambda g: _scatter_job([g], 4, _to_sibling)
    pair = lambda g, r, name: _pair_sum(g.reshape((4, 2) + g.shape[1:]), r, core, "pair_sum_" + name)
    to_chips = lambda p, prev=None, piece=None: _scatter_job([p], 3, _to_owner_chip, prev=prev, piece=piece)
    out_g, out_d, out_m, out_v = {}, {}, {}, {}

    trees = [{**tree, "sgu_ln": jnp.stack([tree["sgu_ln_g"], tree["sgu_ln_b"]], axis=1)} for tree in (w, mom, var)]
    so_far = {}

    def update(name, own, index, received):
        leaf, layer = LAYER_OF.get(name, (name, 0))
        so_far[leaf] = _shard_update(own, index, received, *[tree[leaf] for tree in trees], layer, so_far.get(leaf), "update_" + name)
        for out, val in zip((out_g, out_d, out_m, out_v), so_far[leaf]):
            out[leaf] = val

    def finish(names, n_extra, shares, where, name):
        res = _replicated_update(shares, where, [(w[n], mom[n], var[n]) for n in names] + [None] * n_extra, name)
        for n, vals in zip(names, res):
            out_g[n], out_d[n], out_m[n], out_v[n] = vals
        return [vals[0] for vals in res[len(names):]]

    first_half = lambda p: _halves(p[0])[0]
    second_half = lambda p: _halves(p[0])[1]
    (df1, dgu1, dx3, dg_ffn_post1, dg_ffn_pre1, loss_tile, h3, a1), _ = _ffn_bwd(
        None, None, row(norm_ffn_post[1]), w_dn1, None, w_gu1, x3, row(norm_ffn_pre[1]), "ffn_last", target=target)
    b_gu1, _ = _wgrad(h3, dgu1, "ffn_up_wgrad1", ACT, transposed=True)
    b_gu1 = _rows_blocked(b_gu1)
    dw_dn1, _ = _wgrad(a1, df1, "ffn_down_wgrad1", ACT)
    b_dn1 = _rows_blocked(dw_dn1)
    (dy, dw_out, dg_mix_post1), ((r_gu1,), (r_dn1,)) = _post_bwd(
        dx3, m1, row(norm_mix_post[1]), w_out, y, "sgu", "sgu_out_bwd", jobs=[to_sibling(b_gu1), to_sibling(b_dn1)])
    p_gu1, p_dn1 = pair(b_gu1, r_gu1, "gu1"), pair(b_dn1, r_dn1, "dn1")
    b_out = _rows_blocked(dw_out)
    (dz, dlg, dlb, dws, dbs_t), ((q_gu1,), (r_out,)) = _sgu_mix_bwd(z, dy, lg, lb, ws, bs_t, jobs=[
        to_chips(p_gu1, piece=first_half(p_gu1)), to_sibling(b_out)])
    p_out = pair(b_out, r_out, "sgu_w_out")
    b_in, _ = _wgrad(h2, dz, "sgu_in_wgrad", ACT, out_block=stage["sgu_w_in"].shape[1])
    b_ln = jnp.stack([dlg.reshape(N_DEV, -1), dlb.reshape(N_DEV, -1)], axis=1)
    (dx2, dg_mix_pre1), _ = _pre_bwd(dz, w_in, x2, row(norm_mix_pre[1]), dx3, "sgu_in_bwd")
    (df0, dgu0, dx1, dg_ffn_post0, dg_ffn_pre0), ((q_gu1,), (q_dn1,), (q_out,), (r_in,), (r_ln,)) = _ffn_bwd(
        dx2, f0, row(norm_ffn_post[0]), w_dn0, gu0, w_gu0, x1, row(norm_ffn_pre[0]), "ffn_bwd0",
        jobs=[to_chips(p_gu1, prev=[q_gu1], piece=second_half(p_gu1)), to_chips(p_dn1), to_chips(p_out), to_sibling(b_in),
              to_sibling(b_ln)])
    update("gu1", p_gu1, chip, q_gu1)
    update("dn1", p_dn1, chip, q_dn1)
    update("sgu_w_out", p_out, chip, q_out)
    p_in, p_ln = pair(b_in, r_in, "sgu_w_in"), pair(b_ln, r_ln, "sgu_ln")
    dw_dn0, _ = _wgrad(a0, df0, "ffn_down_wgrad0", ACT)
    b_dn0 = _rows_blocked(dw_dn0)
    b_gu0, ((q_in,), (q_ln,), (r_dn0,)) = _wgrad(h1, dgu0, "ffn_up_wgrad0", ACT, transposed=True, jobs=[
        to_chips(p_in), to_chips(p_ln), to_sibling(b_dn0)])
    update("sgu_w_in", p_in, chip, q_in)
    update("sgu_ln", p_ln, chip, q_ln)
    b_gu0 = _rows_blocked(b_gu0)
    p_dn0 = pair(b_dn0, r_dn0, "dn0")
    (do, dw_o, dg_mix_post0, db_o), ((r_gu0,), (q_dn0,)) = _post_bwd(
        dx1, m0, row(norm_mix_post[0]), w_o, o, "attn", "attn_out_bwd", jobs=[to_sibling(b_gu0), to_chips(p_dn0)])
    p_gu0 = pair(b_gu0, r_gu0, "gu0")
    b_o = _rows_blocked(dw_o)
    grads = {
        "norm_mix_post": jnp.concatenate([dg_mix_post0, dg_mix_post1], axis=0),
        "norm_ffn_pre": jnp.concatenate([dg_ffn_pre0, dg_ffn_pre1], axis=0),
        "norm_ffn_post": jnp.concatenate([dg_ffn_post0, dg_ffn_post1], axis=0),
        "attn_b_o": db_o,
        "sgu_w_spatial": dws,
        "sgu_b_spatial": dbs_t.T,
    }
    shares1, where1 = _pack_rows([grads[name] for name in BEFORE_ATTN], d)
    (dqkv, db_qkv, dsink), ((q_gu0,), (r_o,), (g_shares1,)) = _attn_bwd(q, kdup, vdup, do, sinks, cos, sin, jobs=[
        to_chips(p_gu0), to_sibling(b_o),
        _gather_job(shares1, None, sends=[_whole(shares1)])])
    update("gu0", p_gu0, chip, q_gu0)
    update("dn0", p_dn0, chip, q_dn0)
    p_o = pair(b_o, r_o, "attn_w_o")
    grads.update({"attn_b_qkv": db_qkv, "attn_sinks": dsink[:1, :d // HEAD_DIM]})
    shares2, where2 = _pack_rows([grads[name] for name in AFTER_ATTN] + [loss_tile[:1, :1]], d)
    (dx0, dg_mix_pre0), _ = _pre_bwd(dqkv, w_qkv, xs, row(norm_mix_pre[0]), dx1, "attn_qkv_bwd", True)
    grads["norm_mix_pre"] = jnp.concatenate([dg_mix_pre0, dg_mix_pre1], axis=0)
    shares3, where3 = _pack_rows([grads[name] for name in LAST], d)
    dw_qkv, ((q_o,), (g_shares1,), (g_shares2,), (g_shares3,)) = _wgrad(h0, dqkv, "attn_qkv_wgrad", ACT, transposed=True, jobs=[
        to_chips(p_o), _gather_job(shares1, g_shares1, forwards=[_whole(shares1)]),
        _gather_job(shares2, None, sends=[_whole(shares2)]), _gather_job(shares3, None, sends=[_whole(shares3)])])
    update("attn_w_o", p_o, chip, q_o)
    b_qkv_grad = _rows_blocked(dw_qkv)
    (r_qkv,), (g_shares2,), (g_shares3,) = _copies_only([
        to_sibling(b_qkv_grad), _gather_job(shares2, g_shares2, forwards=[_whole(shares2)]),
        _gather_job(shares3, g_shares3, forwards=[_whole(shares3)])], "grads_tail_to_sibling")
    p_qkv = pair(b_qkv_grad, r_qkv, "attn_w_qkv")
    (q_qkv,), = _copies_only([to_chips(p_qkv)], "grads_tail_to_owner_chip")
    update("attn_w_qkv", p_qkv, chip, q_qkv)

    finish(BEFORE_ATTN, 0, g_shares1, where1, "replicated_update_before_attn")
    loss = finish(AFTER_ATTN, 1, g_shares2, where2, "replicated_update_after_attn")[0][0, 0]
    finish(LAST, 0, g_shares3, where3, "replicated_update_last")

    for out in (out_g, out_d, out_m, out_v):
        out["sgu_ln_g"], out["sgu_ln_b"] = out["sgu_ln"][:, 0, :], out["sgu_ln"][:, 1, :]
        for n in SWAPPED:
            out[n] = jnp.swapaxes(out[n], 1, 2)

    return (loss, dx0[None], *[out_g[n] for n in WEIGHTS], *[out_d[n] for n in WEIGHTS],
            *[out_m[n] for n in WEIGHTS], *[out_v[n] for n in WEIGHTS])
```

```python
import functools
import math
import operator

import jax
import jax.numpy as jnp
import numpy as np
from jax import lax
from jax.experimental import pallas as pl
from jax.experimental.pallas import tpu as pltpu

F32 = jnp.float32
ACT = jnp.bfloat16

EPS = 1e-6
HEAD_DIM = 64
PAIR = 2 * HEAD_DIM
GQA_GROUP = 4
WINDOW = 128
ROPE_THETA = 10000.0
SCALE = HEAD_DIM ** -0.5
MASKED = -1e30
LANES = 128
MXU_WIDTH = 256

ADAM_LR, ADAM_B1, ADAM_B2, ADAM_EPS, ADAM_WD, ADAM_STEP = 0.001, 0.9, 0.999, 1e-08, 0.01, 10

N_DEV = 8
VMEM_LIMIT_BYTES = 56 * 1024 * 1024
MESH = pl.DeviceIdType.MESH
ANY = pl.BlockSpec(memory_space=pl.ANY)
IN_VMEM = pl.BlockSpec(memory_space=pltpu.VMEM)


def _pick(n, candidates):
    for c in candidates:
        if n % c == 0:
            return c
    return n


def _row_tile(t, most=512):
    return _pick(t, (most,))


def _shard_tile(rows):
    return next((c for c in (512, 256, 176, 128, 96, 64) if rows % c == 0 and rows >= 2 * c), rows)


def _mxu_chunks(n, most=4 * MXU_WIDTH):
    assert n % MXU_WIDTH == 0 and most % MXU_WIDTH == 0
    return [(c, min(most, n - c)) for c in range(0, n, most)]


def _params(*sem):
    return pltpu.CompilerParams(dimension_semantics=sem, vmem_limit_bytes=VMEM_LIMIT_BYTES)


def _full(shape):
    return pl.BlockSpec(shape, lambda *_: (0,) * len(shape))


def _resident(shape):
    return pl.BlockSpec(shape, lambda *_: (0,) * len(shape), pipeline_mode=pl.Buffered(1))


def _rows(tm, n):
    return pl.BlockSpec((tm, n), lambda i: (i, 0))


def _sds(shape, dtype):
    return jax.ShapeDtypeStruct(shape, dtype)


def _place():
    return lax.axis_index("x"), lax.axis_index("y"), lax.axis_index("c")


def _other_chips(x, y):
    return [(1 - x, y), (x, 1 - y), (1 - x, 1 - y)]


class _Job:
    def __init__(self, inputs, out_shape, aliases, emit, n_remote, n_local=0, n_late=0):
        self.inputs, self.out_shape, self.aliases, self.emit = list(inputs), list(out_shape), dict(aliases), emit
        self.n_remote, self.n_local, self.n_late = n_remote, n_local, n_late


def _call(body, name, grid, in_specs, out_specs, out_shape, args, sem, scratch=(), jobs=()):
    n_in, n_out, n_scr = len(args), len(out_shape), len(scratch)
    j_in = [a for job in jobs for a in job.inputs]
    j_out = [s for job in jobs for s in job.out_shape]
    aliases, at_in, at_out = {}, n_in, n_out
    for job in jobs:
        aliases.update({at_in + i: at_out + o for i, o in job.aliases.items()})
        at_in, at_out = at_in + len(job.inputs), at_out + len(job.out_shape)
    n_remote = sum(job.n_remote for job in jobs)
    n_local = sum(job.n_local for job in jobs)

    def wrapped(*refs):
        ins, jin = refs[:n_in], refs[n_in:n_in + len(j_in)]
        rest = refs[n_in + len(j_in):]
        outs, jout = rest[:n_out], rest[n_out:n_out + len(j_out)]
        scr = rest[n_out + len(j_out):n_out + len(j_out) + n_scr]
        if not jobs:
            body(*ins, *outs, *scr)
            return
        send, recv, local = rest[n_out + len(j_out) + n_scr:]
        ids = [pl.program_id(a) for a in range(len(grid))]
        first = functools.reduce(operator.and_, [i == 0 for i in ids])
        last = functools.reduce(operator.and_, [i == g - 1 for i, g in zip(ids, grid)])

        def descriptors():
            place, found, i, o, r, l = _place(), ([], []), 0, 0, 0, 0
            for job in jobs:
                emitted = job.emit(jin[i:i + len(job.inputs)], jout[o:o + len(job.out_shape)], place)
                for at, (src, dst, to) in enumerate(emitted):
                    if to is None:
                        cp = pltpu.make_async_copy(src, dst, local.at[l])
                        l += 1
                    else:
                        cp = pltpu.make_async_remote_copy(src_ref=src, dst_ref=dst, send_sem=send.at[r], recv_sem=recv.at[r],
                                                          device_id=to, device_id_type=MESH)
                        r += 1
                    found[at >= len(emitted) - job.n_late].append(cp)
                i, o = i + len(job.inputs), o + len(job.out_shape)
            return found

        @pl.when(first)
        def _():
            for cp in descriptors()[0]:
                cp.start()

        body(*ins, *outs, *scr)

        @pl.when(last)
        def _():
            early, late = descriptors()
            for cp in early:
                cp.wait()
            for cp in late:
                cp.start()
            for cp in late:
                cp.wait()

    sems = [pltpu.SemaphoreType.DMA((n_remote,)), pltpu.SemaphoreType.DMA((n_remote,)),
            pltpu.SemaphoreType.DMA((max(n_local, 1),))] if jobs else []
    res = pl.pallas_call(
        wrapped, name=name, grid=grid,
        in_specs=list(in_specs) + [ANY] * len(j_in), out_specs=list(out_specs) + [ANY] * len(j_out),
        out_shape=list(out_shape) + j_out, scratch_shapes=list(scratch) + sems, input_output_aliases=aliases,
        compiler_params=_params(*(("arbitrary",) * len(grid) if jobs else sem)),
    )(*args, *j_in)
    job_res, at = [], n_out
    for job in jobs:
        job_res.append(list(res[at:at + len(job.out_shape)]))
        at += len(job.out_shape)
    return list(res[:n_out]), job_res


def _copies_only(jobs, name):
    def body(o_ref):
        o_ref[...] = jnp.zeros_like(o_ref)

    return _call(body, name, (1,), [], [_full((8, LANES))], [_sds((8, LANES), F32)], (), ("arbitrary",), jobs=jobs)[1]


def _gather_job(stage, gathered, sends=(), forwards=()):
    shape = _sds((N_DEV,) + stage.shape, stage.dtype)
    inputs = ([stage] if sends else []) + ([gathered] if gathered is not None else [])
    aliases = {len(inputs) - 1: 0} if gathered is not None else {}

    def emit(ins, outs, place):
        x, y, c = place
        sibling, chips, mine, g = (x, y, 1 - c), _other_chips(x, y), 4 * x + 2 * y + c, outs[0]
        found = []
        for r0, nr in sends:
            src, dst = ins[0].at[pl.ds(r0, nr)], g.at[mine, pl.ds(r0, nr)]
            found += [(src, dst, None), (src, dst, sibling)] + [(src, dst, (px, py, c)) for px, py in chips]
        for r0, nr in forwards:
            for px, py in chips:
                block = 4 * px + 2 * py + c
                found.append((ins[-1].at[block, pl.ds(r0, nr)], g.at[block, pl.ds(r0, nr)], sibling))
        return found

    return _Job(inputs, [shape], aliases, emit, 4 * len(sends) + 3 * len(forwards), len(sends))


def _relay_gather_job(stage, gathered, sends=(), relays=(), forwards=(), far=(), late_far=()):
    shape = _sds((N_DEV,) + stage.shape, stage.dtype)
    inputs = ([stage] if sends else []) + ([gathered] if gathered is not None else [])
    aliases = {len(inputs) - 1: 0} if gathered is not None else {}

    def emit(ins, outs, place):
        x, y, c = place
        sibling, beside_x, beside_y, g = (x, y, 1 - c), (1 - x, y, c), (x, 1 - y, c), outs[0]
        slot = lambda dev: 4 * dev[0] + 2 * dev[1] + dev[2]
        found = []
        for r0, nr in sends:
            src, dst = ins[0].at[pl.ds(r0, nr)], g.at[slot((x, y, c)), pl.ds(r0, nr)]
            found += [(src, dst, None), (src, dst, sibling), (src, dst, beside_x), (src, dst, beside_y)]

        def passed_on(block, piece, to):
            return ins[-1].at[block, pl.ds(*piece)], g.at[block, pl.ds(*piece)], to

        for piece, way in relays:
            found.append(passed_on(slot(beside_x), piece, beside_y) if way == 0 else passed_on(slot(beside_y), piece, beside_x))
        for piece in forwards:
            found += [passed_on(slot(beside_x), piece, sibling), passed_on(slot(beside_y), piece, sibling)]
        for piece in tuple(far) + tuple(late_far):
            found.append(passed_on(slot((1 - x, 1 - y, c)), piece, sibling))
        return found

    n_remote = 3 * len(sends) + len(relays) + 2 * len(forwards) + len(far) + len(late_far)
    return _Job(inputs, [shape], aliases, emit, n_remote, len(sends), len(late_far))


def _scatter_job(arrays, n_slots, route, prev=None, piece=None):
    shapes = [_sds((n_slots,) + v.shape[1:], v.dtype) for v in arrays]
    inputs = list(arrays) + (list(prev) if prev else [])
    aliases = {len(arrays) + i: i for i in range(len(arrays))} if prev else {}

    def emit(ins, outs, place):
        found = []
        for a in range(len(arrays)):
            for s in range(n_slots):
                block, to = route(s, *place)
                if piece is None:
                    found.append((ins[a].at[block], outs[a].at[s], to))
                else:
                    found.append((ins[a].at[block, pl.ds(*piece)], outs[a].at[s, pl.ds(*piece)], to))
        return found

    return _Job(inputs, shapes, aliases, emit, len(arrays) * n_slots)


def _to_sibling(s, x, y, c):
    return 2 * s + (1 - c), (x, y, 1 - c)


def _to_owner_chip(s, x, y, c):
    px, py = _other_chips(x, y)[s]
    return 2 * px + py, (px, py, c)


def _rstd(x):
    return lax.rsqrt(jnp.mean(x * x, axis=-1, keepdims=True) + EPS)


def _rms_bwd(xhat, r, g, dy):
    dxh = dy * g
    return r * (dxh - xhat * jnp.mean(dxh * xhat, axis=-1, keepdims=True))


def _first_half(rows):
    lane = lax.broadcasted_iota(jnp.int32, (rows, PAIR), 1)
    return (lane % HEAD_DIM) < (HEAD_DIM // 2)


def _rot_half(v, first):
    return jnp.where(first, pltpu.roll(v, PAIR - HEAD_DIM // 2, 1), pltpu.roll(v, HEAD_DIM // 2, 1))


def _rope(v, cos, sin, first):
    return v * cos + _rot_half(v, first) * sin


def _rope_t(dv, cos, sin, first):
    return dv * cos + _rot_half(dv * sin, first)


def _dot(a, b):
    return jnp.dot(a, b, preferred_element_type=F32)


def _dot_nt(a, b):
    return lax.dot_general(a, b, (((1,), (1,)), ((), ())), preferred_element_type=F32)


def _dot_tn(a, b):
    return lax.dot_general(a, b, (((0,), (0,)), ((), ())), preferred_element_type=F32)


def _sigmoid(v):
    return 1.0 / (1.0 + jnp.exp(-v))


_GELU_K = math.sqrt(2.0 / math.pi)
_GELU_C = 0.044715


def _gelu(v):
    return v * (0.5 * (1.0 + jnp.tanh(_GELU_K * (v + _GELU_C * (v * v * v)))))


def _gelu_grad(v):
    t = jnp.tanh(_GELU_K * (v + _GELU_C * (v * v * v)))
    return 0.5 * (1.0 + t) + 0.5 * v * (1.0 - t * t) * (_GELU_K * (1.0 + 3.0 * _GELU_C * (v * v)))


def _attn_qkv_fwd(x, g, w, b, cos, sin, jobs=()):
    t, d = x.shape
    n = w.shape[0]
    kd = n - d
    assert (kd // 2) % PAIR == 0
    tm = _row_tile(t)
    ch = _pick(d, (512,))

    def body(x_ref, g_ref, w_ref, b_ref, cos_ref, sin_ref, h_ref, q_ref, k_ref, v_ref):
        xv = x_ref[...]
        hb = (xv * _rstd(xv) * g_ref[...]).astype(ACT)
        h_ref[...] = hb
        cosv, sinv = cos_ref[...], sin_ref[...]
        first = _first_half(tm)
        left = _lane_halves()[0]

        def proj(lo, width):
            return _dot_nt(hb, w_ref[lo:lo + width, :]) + b_ref[:, lo:lo + width]

        def twice(ref, s, two_heads):
            swapped = pltpu.roll(two_heads, HEAD_DIM, 1)
            ref[:, 2 * s:2 * s + PAIR] = jnp.where(left, two_heads, swapped).astype(ACT)
            ref[:, 2 * s + PAIR:2 * s + 2 * PAIR] = jnp.where(left, swapped, two_heads).astype(ACT)

        for c in range(0, d, ch):
            y = proj(c, ch)
            for s in range(0, ch, PAIR):
                q_ref[:, c + s:c + s + PAIR] = (_rope(y[:, s:s + PAIR], cosv, sinv, first) * SCALE).astype(ACT)
        y = proj(d, kd)
        for s in range(0, kd // 2, PAIR):
            twice(k_ref, s, _rope(y[:, s:s + PAIR], cosv, sinv, first))
            twice(v_ref, s, y[:, kd // 2 + s:kd // 2 + s + PAIR])

    return _call(
        body, "attn_qkv_fwd", (t // tm,),
        [_rows(tm, d), _full((1, d)), _full((n, d)), _full((1, n)), _rows(tm, PAIR), _rows(tm, PAIR)],
        [_rows(tm, d), _rows(tm, d), _rows(tm, kd), _rows(tm, kd)],
        [_sds((t, d), ACT), _sds((t, d), ACT), _sds((t, kd), ACT), _sds((t, kd), ACT)],
        (x, g, w, b, cos, sin), ("parallel",), jobs=jobs)


STACK = GQA_GROUP * WINDOW


def _band_mask(has_prev):
    row = lax.broadcasted_iota(jnp.int32, (STACK, 2 * WINDOW), 0) % WINDOW
    col = lax.broadcasted_iota(jnp.int32, (STACK, 2 * WINDOW), 1)
    return ((col < WINDOW) & (col > row) & has_prev) | ((col >= WINDOW) & (col - WINDOW <= row))


def _lane_halves():
    lane = lax.broadcasted_iota(jnp.int32, (1, PAIR), 1)
    return lane < HEAD_DIM, lane >= HEAD_DIM


def _stack_heads(ref, h, halves):
    parts = []
    for p in range(2):
        pair = ref[:, (2 * h + p) * PAIR:(2 * h + p + 1) * PAIR]
        parts += [jnp.where(half, pair, jnp.zeros_like(pair)) for half in halves]
    return jnp.concatenate(parts, axis=0)


def _sink_column(sink_ref, h):
    row = lax.broadcasted_iota(jnp.int32, (STACK, 1), 0)
    col = jnp.full((STACK, 1), sink_ref[0, GQA_GROUP * h + GQA_GROUP - 1], F32)
    for j in range(GQA_GROUP - 2, -1, -1):
        col = jnp.where(row < (j + 1) * WINDOW, sink_ref[0, GQA_GROUP * h + j], col)
    return col


def _probs(scores, valid, sink):
    s = jnp.where(valid, scores, MASKED)
    m = jnp.maximum(jnp.max(s, axis=-1, keepdims=True), sink)
    p = jnp.exp(s - m)
    psink = jnp.exp(sink - m)
    inv = 1.0 / (jnp.sum(p, axis=-1, keepdims=True) + psink)
    return p * inv, psink * inv


def _attn_fwd(q, kd_, vd, sinks, jobs=()):
    t, d = q.shape
    kd = kd_.shape[1]
    cur = lambda n: (n, 0)
    prev = lambda n: (jnp.maximum(n - 1, 0), 0)

    def body(sink_ref, q_ref, kc_ref, kp_ref, vc_ref, vp_ref, o_ref):
        valid = _band_mask(pl.program_id(0) > 0)
        halves = _lane_halves()
        heads = range(kd // PAIR)
        hs = [slice(h * PAIR, (h + 1) * PAIR) for h in heads]
        scores = [_dot_nt(_stack_heads(q_ref, h, halves), jnp.concatenate([kp_ref[:, hs[h]], kc_ref[:, hs[h]]], axis=0)) for h in heads]
        probs = [_probs(scores[h], valid, _sink_column(sink_ref, h))[0].astype(ACT) for h in heads]
        for h in heads:
            v2 = jnp.concatenate([vp_ref[:, hs[h]], vc_ref[:, hs[h]]], axis=0)
            vs = jnp.concatenate([jnp.where(half, v2, jnp.zeros_like(v2)) for half in halves], axis=0)
            pr = probs[h]
            for p in range(2):
                both = jnp.concatenate([pr[2 * p * WINDOW:(2 * p + 1) * WINDOW], pr[(2 * p + 1) * WINDOW:(2 * p + 2) * WINDOW]], axis=1)
                o_ref[:, (2 * h + p) * PAIR:(2 * h + p + 1) * PAIR] = _dot(both, vs).astype(ACT)

    kv_c, kv_p = pl.BlockSpec((WINDOW, kd), cur), pl.BlockSpec((WINDOW, kd), prev)
    return _call(
        body, "attn_fwd", (t // WINDOW,),
        [pl.BlockSpec(memory_space=pltpu.SMEM), pl.BlockSpec((WINDOW, d), cur), kv_c, kv_p, kv_c, kv_p],
        [pl.BlockSpec((WINDOW, d), cur)], [_sds((t, d), ACT)],
        (sinks, q, kd_, kd_, vd, vd), ("parallel",), jobs=jobs)


def _attn_bwd(q, kd_, vd, do, sinks, cos, sin, jobs=()):
    t, d = q.shape
    kd = kd_.shape[1]
    nb = t // WINDOW
    n_out = d + kd
    assert (kd // 2) % PAIR == 0
    cur = lambda n: (jnp.minimum(n, nb - 1), 0)
    prev = lambda n: (jnp.maximum(jnp.minimum(n, nb - 1) - 1, 0), 0)
    late = lambda n: (jnp.maximum(n - 1, 0), 0)

    def body(sink_ref, q_ref, kc_ref, kp_ref, vc_ref, vp_ref, do_ref, cosc_ref, sinc_ref, cosp_ref, sinp_ref,
             out_ref, db_ref, dsink_ref, dq_keep, dk_keep, dv_keep):
        n = pl.program_id(0)

        @pl.when(n == 0)
        def _():
            for ref in (db_ref, dsink_ref, dq_keep, dk_keep, dv_keep):
                ref[...] = jnp.zeros_like(ref)

        valid = _band_mask(n > 0) & (n < nb)
        halves = _lane_halves()
        first = _first_half(WINDOW)
        slot = lax.broadcasted_iota(jnp.int32, dsink_ref.shape, 1)
        top = lax.broadcasted_iota(jnp.int32, dsink_ref.shape, 0) == 0
        dsink = jnp.zeros(dsink_ref.shape, F32)

        def emit(cols, done):
            out_ref[:, cols] = done.astype(ACT)
            db_ref[:, cols] += jnp.sum(done.astype(F32), axis=0, keepdims=True)

        heads = range(kd // PAIR)
        hs = [slice(h * PAIR, (h + 1) * PAIR) for h in heads]
        k2 = [jnp.concatenate([kp_ref[:, hs[h]], kc_ref[:, hs[h]]], axis=0) for h in heads]
        v2 = [jnp.concatenate([vp_ref[:, hs[h]], vc_ref[:, hs[h]]], axis=0) for h in heads]
        qs = [_stack_heads(q_ref, h, halves) for h in heads]
        dos = [_stack_heads(do_ref, h, halves) for h in heads]
        scores = [_dot_nt(qs[h], k2[h]) for h in heads]
        dps = [_dot_nt(dos[h], v2[h]) for h in heads]
        prs, dss = [], []
        for h in heads:
            pr, psink = _probs(scores[h], valid, _sink_column(sink_ref, h))
            delta = jnp.sum(pr * dps[h], axis=-1, keepdims=True)
            dss.append((pr * (dps[h] - delta)).astype(ACT))
            prs.append(pr.astype(ACT))
            leak = psink * delta
            for j in range(GQA_GROUP):
                share = jnp.sum(leak[j * WINDOW:(j + 1) * WINDOW], axis=0, keepdims=True)
                dsink = dsink - jnp.where(top & (slot == GQA_GROUP * h + j), share, 0.0)
        dqs = [_dot(dss[h], k2[h]) for h in heads]
        dk2 = [_dot_tn(dss[h], qs[h]) for h in heads]
        dv2 = [_dot_tn(prs[h], dos[h]) for h in heads]
        for h in heads:
            for p in range(2):
                ps = slice((2 * h + p) * PAIR, (2 * h + p + 1) * PAIR)
                dqp = jnp.where(halves[0], dqs[h][2 * p * WINDOW:(2 * p + 1) * WINDOW], dqs[h][(2 * p + 1) * WINDOW:(2 * p + 2) * WINDOW])
                emit(ps, dq_keep[:, ps])
                dq_keep[:, ps] = (_rope_t(dqp, cosc_ref[...], sinc_ref[...], first) * SCALE).astype(ACT)
        dks, dvs = [], []
        for h in heads:
            dks.append(dk_keep[:, hs[h]] + _rope_t(dk2[h][:WINDOW], cosp_ref[...], sinp_ref[...], first))
            dk_keep[:, hs[h]] = _rope_t(dk2[h][WINDOW:], cosc_ref[...], sinc_ref[...], first)
            dvs.append(dv_keep[:, hs[h]] + dv2[h][:WINDOW])
            dv_keep[:, hs[h]] = dv2[h][WINDOW:]
        both = lambda v: v + pltpu.roll(v, HEAD_DIM, 1)
        for h in range(0, len(heads), 2):
            at = h // 2 * PAIR
            emit(slice(d + at, d + at + PAIR), jnp.where(halves[0], both(dks[h]), both(dks[h + 1])))
            emit(slice(d + kd // 2 + at, d + kd // 2 + at + PAIR), jnp.where(halves[0], both(dvs[h]), both(dvs[h + 1])))
        dsink_ref[...] += dsink

    kv_c, kv_p = pl.BlockSpec((WINDOW, kd), cur), pl.BlockSpec((WINDOW, kd), prev)
    tab_c, tab_p = pl.BlockSpec((WINDOW, PAIR), cur), pl.BlockSpec((WINDOW, PAIR), prev)
    return _call(
        body, "attn_bwd", (nb + 1,),
        [pl.BlockSpec(memory_space=pltpu.SMEM), pl.BlockSpec((WINDOW, d), cur), kv_c, kv_p, kv_c, kv_p,
         pl.BlockSpec((WINDOW, d), cur), tab_c, tab_c, tab_p, tab_p],
        [pl.BlockSpec((WINDOW, n_out), late), _full((1, n_out)), _full((8, LANES))],
        [_sds((t, n_out), ACT), _sds((1, n_out), F32), _sds((8, LANES), F32)],
        (sinks, q, kd_, kd_, vd, vd, do, cos, sin, cos, sin), ("arbitrary",),
        scratch=[pltpu.VMEM((WINDOW, d), ACT), pltpu.VMEM((WINDOW, kd), F32), pltpu.VMEM((WINDOW, kd), F32)], jobs=jobs)


def _proj_res(a, w, b, g, x, name, jobs=()):
    t = a.shape[0]
    k, d = w.shape
    tm = _row_tile(t)
    has_bias = b is not None

    def body(*refs):
        a_ref, w_ref = refs[:2]
        b_ref = refs[2] if has_bias else None
        g_ref, x_ref, m_ref, xo_ref = refs[2 + has_bias:]
        m = _dot(a_ref[...], w_ref[...])
        if has_bias:
            m = m + b_ref[...]
        m_ref[...] = m.astype(ACT)
        xo_ref[...] = x_ref[...] + (m * _rstd(m)) * g_ref[...]

    ins = [a, w] + ([b] if has_bias else []) + [g, x]
    specs = [_rows(tm, k), _full((k, d))] + ([_full((1, d))] if has_bias else []) + [_full((1, d)), _rows(tm, d)]
    return _call(body, name, (t // tm,), specs, [_rows(tm, d), _rows(tm, d)], [_sds((t, d), ACT), _sds((t, d), F32)], ins,
                 ("parallel",), jobs=jobs)


def _post_bwd(dres, m, g, w, a, mode, name, jobs=()):
    t, d = dres.shape
    k = w.shape[0]
    tm = _row_tile(t)
    kc = _pick(k, (1408, 1024, 512))
    steps = t // tm

    def body(*refs):
        d_ref, m_ref, g_ref, w_ref, a_ref = refs[:5]
        da_ref, dw_ref, dg_ref = refs[5:8]
        db_ref, acc_ref = (refs[8] if mode == "attn" else None), refs[-1]
        i = pl.program_id(0)

        @pl.when(i == 0)
        def _():
            for ref in (dg_ref, acc_ref) + ((db_ref,) if mode == "attn" else ()):
                ref[...] = jnp.zeros_like(ref)

        dv, mv = d_ref[...], m_ref[...].astype(F32)
        r = _rstd(mv)
        mh = mv * r
        dg_ref[...] += jnp.sum(dv * mh, axis=0, keepdims=True)
        dm = _rms_bwd(mh, r, g_ref[...], dv)
        dmb = dm.astype(ACT)
        if mode == "attn":
            db_ref[...] += jnp.sum(dm, axis=0, keepdims=True)
        for c in range(0, k, kc):
            da = _dot_nt(dmb, w_ref[c:c + kc, :])
            da_ref[:, c:c + kc] = da.astype(ACT)
        acc_ref[...] += _dot_tn(a_ref[...], dmb)

        @pl.when(i == steps - 1)
        def _():
            dw_ref[...] = acc_ref[...].astype(ACT)

    ins = [dres, m, g, w, a]
    specs = [_rows(tm, d), _rows(tm, d), _full((1, d)), _full((k, d)), _rows(tm, k)]
    out_specs = [_rows(tm, k), _full((k, d)), _full((1, d))]
    out_shape = [_sds((t, k), ACT), _sds((k, d), ACT), _sds((1, d), F32)]
    if mode == "attn":
        out_specs.append(_full((1, d)))
        out_shape.append(_sds((1, d), F32))
    return _call(body, name, (steps,), specs, out_specs, out_shape, ins, ("arbitrary",), scratch=[pltpu.VMEM((k, d), F32)], jobs=jobs)


def _pre_bwd(dy, w, x, g, dres, name, transposed=False, jobs=()):
    t, d = x.shape
    tm = _row_tile(t)

    def body(dy_ref, w_ref, x_ref, g_ref, dres_ref, dx_ref, dg_ref):
        @pl.when(pl.program_id(0) == 0)
        def _():
            dg_ref[...] = jnp.zeros_like(dg_ref)

        dh = jnp.zeros((tm, d), F32)
        if transposed:
            dh = dh + _dot(dy_ref[...], w_ref[...])
        elif w.ndim == 3:
            c = w.shape[2]
            for j in range(w.shape[0]):
                dh = dh + _dot_nt(dy_ref[:, j * c:(j + 1) * c], w_ref[j])
        else:
            n = w.shape[1]
            nc = _pick(n, (1408, 1024, 512))
            for c in range(0, n, nc):
                dh = dh + _dot_nt(dy_ref[:, c:c + nc], w_ref[:, c:c + nc])
        xv = x_ref[...]
        r = _rstd(xv)
        xh = xv * r
        dg_ref[...] += jnp.sum(dh * xh, axis=0, keepdims=True)
        dx_ref[...] = dres_ref[...] + _rms_bwd(xh, r, g_ref[...], dh)

    return _call(
        body, name, (t // tm,),
        [_rows(tm, dy.shape[1]), _full(w.shape), _rows(tm, d), _full((1, d)), _rows(tm, d)],
        [_rows(tm, d), _full((1, d))], [_sds((t, d), F32), _sds((1, d), F32)],
        (dy, w, x, g, dres), ("arbitrary",), jobs=jobs)


def _ffn_bwd(dres, f, g_post, w_dn, gu, w_gu, x, g_pre, name, target=None, jobs=()):
    t, d = x.shape
    n = w_dn.shape[0]
    tm = _row_tile(t, 256)
    whole = target is not None
    n_in = 6 if whole else 8

    def body(*refs):
        if whole:
            t_ref, gp_ref, wd_ref, wu_ref, x_ref, g_ref = refs[:n_in]
            df_ref, dgu_ref, dx_ref, dgp_ref, dg_ref, loss_ref, h_ref, a_ref, gu_ref = refs[n_in:]
        else:
            d_ref, f_ref, gp_ref, wd_ref, gu_ref, wu_ref, x_ref, g_ref = refs[:n_in]
            df_ref, dgu_ref, dx_ref, dgp_ref, dg_ref = refs[n_in:]

        @pl.when(pl.program_id(0) == 0)
        def _():
            for ref in (dgp_ref, dg_ref) + ((loss_ref,) if whole else ()):
                ref[...] = jnp.zeros_like(ref)

        xv = x_ref[...]
        rx = _rstd(xv)
        xh = xv * rx
        if whole:
            hb = (xh * g_ref[...]).astype(ACT)
            h_ref[...] = hb
            for c, size in _mxu_chunks(n):
                gate = _dot_nt(hb, wu_ref[c:c + size, :])
                up = _dot_nt(hb, wu_ref[n + c:n + c + size, :])
                gu_ref[:, c:c + size] = gate.astype(ACT)
                gu_ref[:, n + c:n + c + size] = up.astype(ACT)
                a_ref[:, c:c + size] = (gate * _sigmoid(gate) * up).astype(ACT)
            fv = _dot(a_ref[...], wd_ref[...])
            r = _rstd(fv)
            fh = fv * r
            err = xv + fh * gp_ref[...] - t_ref[...]
            dv = err * (1.0 / d)
            loss_ref[...] += 0.5 * jnp.sum(jnp.mean(err * err, axis=-1, keepdims=True), axis=0, keepdims=True)
        else:
            dv, fv = d_ref[...], f_ref[...].astype(F32)
            r = _rstd(fv)
            fh = fv * r
        dgp_ref[...] += jnp.sum(dv * fh, axis=0, keepdims=True)
        dfb = _rms_bwd(fh, r, gp_ref[...], dv).astype(ACT)
        df_ref[...] = dfb
        for c, size in _mxu_chunks(n):
            da = _dot_nt(dfb, wd_ref[c:c + size, :]).astype(ACT)
            gate, up = gu_ref[:, c:c + size], gu_ref[:, n + c:n + c + size]
            sg = _sigmoid(gate.astype(F32)).astype(ACT)
            gs = gate * sg
            dgu_ref[:, c:c + size] = da * up * (sg + gs - gs * sg)
            dgu_ref[:, n + c:n + c + size] = da * gs
        dh = _dot(dgu_ref[...], wu_ref[...])
        dg_ref[...] += jnp.sum(dh * xh, axis=0, keepdims=True)
        dx_ref[...] = dv + _rms_bwd(xh, rx, g_ref[...], dh)

    w_specs = [_resident(w_dn.shape), _resident(w_gu.shape)]
    out_specs = [_rows(tm, d), _rows(tm, 2 * n), _rows(tm, d), _full((1, d)), _full((1, d))]
    out_shape = [_sds((t, d), ACT), _sds((t, 2 * n), ACT), _sds((t, d), F32), _sds((1, d), F32), _sds((1, d), F32)]
    if whole:
        return _call(
            body, name, (t // tm,),
            [_rows(tm, d), _full((1, d))] + w_specs + [_rows(tm, d), _full((1, d))],
            out_specs + [_full((8, LANES)), _rows(tm, d), _rows(tm, n)],
            out_shape + [_sds((8, LANES), F32), _sds((t, d), ACT), _sds((t, n), ACT)],
            (target, g_post, w_dn, w_gu, x, g_pre), ("arbitrary",), scratch=[pltpu.VMEM((tm, 2 * n), ACT)], jobs=jobs)
    return _call(
        body, name, (t // tm,),
        [_rows(tm, d), _rows(tm, d), _full((1, d)), w_specs[0], _rows(tm, 2 * n), w_specs[1], _rows(tm, d), _full((1, d))],
        out_specs, out_shape, (dres, f, g_post, w_dn, gu, w_gu, x, g_pre), ("arbitrary",), jobs=jobs)


def _wgrad(a, b, name, out_dtype=F32, out_block=None, transposed=False, jobs=()):
    t = a.shape[0]
    tt = _pick(t, (1024,))
    steps = t // tt
    tk = _pick(a.shape[1], (1024, 1408))
    k, n = a.shape[1], b.shape[1]
    tn = _pick(n, (1024, 1408))
    per = 0
    if transposed:
        out_spec, out_shape, acc_shape = pl.BlockSpec((tn, tk), lambda i, j, s: (j, i)), _sds((n, k), out_dtype), (tn, tk)
    elif out_block is None:
        out_spec, out_shape, acc_shape = pl.BlockSpec((tk, tn), lambda i, j, s: (i, j)), _sds((k, n), out_dtype), (tk, tn)
    else:
        per = tn // out_block
        out_spec = pl.BlockSpec((per, tk, out_block), lambda i, j, s: (j, i, 0))
        out_shape, acc_shape = _sds((n // out_block, k, out_block), out_dtype), (tk, tn)

    def body(a_ref, b_ref, o_ref, acc_ref):
        s = pl.program_id(2)

        @pl.when(s == 0)
        def _():
            acc_ref[...] = jnp.zeros_like(acc_ref)

        acc_ref[...] += _dot_tn(b_ref[...], a_ref[...]) if transposed else _dot_tn(a_ref[...], b_ref[...])

        @pl.when(s == steps - 1)
        def _():
            if per >= 1:
                for p in range(per):
                    o_ref[p] = acc_ref[:, p * out_block:(p + 1) * out_block].astype(out_dtype)
            else:
                o_ref[...] = acc_ref[...].astype(out_dtype)

    res, job_res = _call(
        body, name, (k // tk, n // tn, steps),
        [pl.BlockSpec((tt, tk), lambda i, j, s: (s, i)), pl.BlockSpec((tt, tn), lambda i, j, s: (s, j))], [out_spec], [out_shape],
        (a, b), ("parallel", "parallel", "arbitrary"), scratch=[pltpu.VMEM(acc_shape, F32)], jobs=jobs)
    return res[0], job_res


def _ffn_fwd(x, g_pre, w_gu, w_dn, g_post, jobs=()):
    t, d = x.shape
    n = w_dn.shape[0]
    tm = _row_tile(t)

    def body(x_ref, g_ref, wu_ref, wd_ref, gp_ref, h_ref, gu_ref, a_ref, f_ref, xo_ref):
        xv = x_ref[...]
        hb = (xv * _rstd(xv) * g_ref[...]).astype(ACT)
        h_ref[...] = hb
        for c, size in _mxu_chunks(n):
            gate = _dot_nt(hb, wu_ref[c:c + size, :])
            up = _dot_nt(hb, wu_ref[n + c:n + c + size, :])
            gu_ref[:, c:c + size] = gate.astype(ACT)
            gu_ref[:, n + c:n + c + size] = up.astype(ACT)
            a_ref[:, c:c + size] = (gate * _sigmoid(gate) * up).astype(ACT)
        f = _dot(a_ref[...], wd_ref[...])
        f_ref[...] = f.astype(ACT)
        xo_ref[...] = xv + (f * _rstd(f)) * gp_ref[...]

    return _call(
        body, "ffn_fwd", (t // tm,),
        [_rows(tm, d), _full((1, d)), _resident(w_gu.shape), _resident(w_dn.shape), _full((1, d))],
        [_rows(tm, d), _rows(tm, 2 * n), _rows(tm, n), _rows(tm, d), _rows(tm, d)],
        [_sds((t, d), ACT), _sds((t, 2 * n), ACT), _sds((t, n), ACT), _sds((t, d), ACT), _sds((t, d), F32)],
        (x, g_pre, w_gu, w_dn, g_post), ("parallel",), jobs=jobs)


def _causal(ws):
    row = lax.broadcasted_iota(jnp.int32, ws.shape, 0)
    col = lax.broadcasted_iota(jnp.int32, ws.shape, 1)
    return jnp.where(col <= row, ws, 0.0)


def _sgu_ln(v, lg, lb):
    mu = jnp.mean(v, axis=-1, keepdims=True)
    vc = v - mu
    rs = lax.rsqrt(jnp.mean(vc * vc, axis=-1, keepdims=True) + EPS)
    vh = vc * rs
    return vh, rs, vh * lg + lb


def _sgu_fwd(x, g, w, lg, lb, ws, bs_t, jobs=()):
    t, d = x.shape
    nb, _, c = w.shape
    n = nb * c
    groups = d // WINDOW
    tm = _row_tile(t)

    def body(x_ref, g_ref, w_ref, lg_ref, lb_ref, ws_ref, bs_ref, h_ref, z_ref, y_ref, zf_ref):
        xv = x_ref[...]
        hb = (xv * _rstd(xv) * g_ref[...]).astype(ACT)
        h_ref[...] = hb
        for j in range(nb):
            zf_ref[:, j * c:(j + 1) * c] = _dot(hb, w_ref[j])
        z_ref[...] = zf_ref[...].astype(ACT)
        gs = [slice(gi * WINDOW, (gi + 1) * WINDOW) for gi in range(groups)]
        wsg = [_causal(ws_ref[gi]).astype(ACT) for gi in range(groups)]
        for r in range(0, tm, WINDOW):
            u = _gelu(zf_ref[r:r + WINDOW, :d])
            _, _, vn = _sgu_ln(_gelu(zf_ref[r:r + WINDOW, d:]), lg_ref[...], lb_ref[...])
            mixed = [_dot(wsg[gi], vn[:, gs[gi]].astype(ACT)) for gi in range(groups)]
            for gi in range(groups):
                y_ref[r:r + WINDOW, gs[gi]] = (u[:, gs[gi]] * (mixed[gi] + bs_ref[:, gi:gi + 1])).astype(ACT)

    vec = _full((1, d))
    return _call(
        body, "sgu_fwd", (t // tm,),
        [_rows(tm, d), vec, _full((nb, d, c)), vec, vec, _full((groups, WINDOW, WINDOW)), _full((WINDOW, groups))],
        [_rows(tm, d), _rows(tm, n), _rows(tm, d)], [_sds((t, d), ACT), _sds((t, n), ACT), _sds((t, d), ACT)],
        (x, g, w, lg, lb, ws, bs_t), ("parallel",), scratch=[pltpu.VMEM((tm, n), F32)], jobs=jobs)


def _sgu_mix_bwd(z, dy, lg, lb, ws, bs_t, jobs=()):
    t, n = z.shape
    d = n // 2
    groups = d // WINDOW
    tm = _row_tile(t)

    def body(z_ref, dy_ref, lg_ref, lb_ref, ws_ref, bs_ref, dz_ref, dlg_ref, dlb_ref, dws_ref, dbs_ref):
        @pl.when(pl.program_id(0) == 0)
        def _():
            for ref in (dlg_ref, dlb_ref, dws_ref, dbs_ref):
                ref[...] = jnp.zeros_like(ref)

        lane = lax.broadcasted_iota(jnp.int32, (WINDOW, groups), 1)
        gs = [slice(gi * WINDOW, (gi + 1) * WINDOW) for gi in range(groups)]
        wsg = [_causal(ws_ref[gi]).astype(ACT) for gi in range(groups)]
        for c in range(0, tm, WINDOW):
            rows = slice(c, c + WINDOW)
            zu, zv = z_ref[rows, :d].astype(F32), z_ref[rows, d:].astype(F32)
            u = _gelu(zu)
            vh, rs, vn = _sgu_ln(_gelu(zv), lg_ref[...], lb_ref[...])
            dyv = dy_ref[rows, :].astype(F32)
            vng = [vn[:, gs[gi]].astype(ACT) for gi in range(groups)]
            dmix = dyv * u
            dmb = [dmix[:, gs[gi]].astype(ACT) for gi in range(groups)]
            mixed = [_dot(wsg[gi], vng[gi]) for gi in range(groups)]
            dvn_parts = [_dot_tn(wsg[gi], dmb[gi]) for gi in range(groups)]
            dws_parts = [_dot_nt(dmb[gi], vng[gi]) for gi in range(groups)]
            for gi in range(groups):
                dz_ref[rows, gs[gi]] = (dyv[:, gs[gi]] * (mixed[gi] + bs_ref[:, gi:gi + 1]) * _gelu_grad(zu[:, gs[gi]])).astype(ACT)
                dws_ref[:, gs[gi]] += _causal(dws_parts[gi])
                dbs_ref[...] += jnp.where(lane == gi, jnp.sum(dmix[:, gs[gi]], axis=-1, keepdims=True), 0.0)
            dvn = jnp.concatenate(dvn_parts, axis=-1)
            dlg_ref[...] += jnp.sum(dvn * vh, axis=0, keepdims=True)
            dlb_ref[...] += jnp.sum(dvn, axis=0, keepdims=True)
            dvh = dvn * lg_ref[...]
            dv = rs * (dvh - jnp.mean(dvh, axis=-1, keepdims=True) - vh * jnp.mean(dvh * vh, axis=-1, keepdims=True))
            dz_ref[rows, d:] = (dv * _gelu_grad(zv)).astype(ACT)

    vec = _full((1, d))
    return _call(
        body, "sgu_mix_bwd", (t // tm,),
        [_rows(tm, n), _rows(tm, d), vec, vec, _full((groups, WINDOW, WINDOW)), _full((WINDOW, groups))],
        [_rows(tm, n), vec, vec, _full((WINDOW, d)), _full((WINDOW, groups))],
        [_sds((t, n), ACT), _sds((1, d), F32), _sds((1, d), F32), _sds((WINDOW, d), F32), _sds((WINDOW, groups), F32)],
        (z, dy, lg, lb, ws, bs_t), ("arbitrary",), jobs=jobs)


AG_COPIES = 8


def _prepare(sources, dtypes, n_gather, name, ahead=None):
    n = len(sources)
    arrays, where = [], []
    for parts, layer in sources:
        found = []
        for part in parts:
            known = [i for i, v in enumerate(arrays) if v is part]
            if not known:
                arrays.append(part)
            found.append(known[0] if known else len(arrays) - 1)
        where.append((found, layer))
    shapes = [(sum(p.shape[1] for p in parts), parts[0].shape[2]) for parts, _ in sources]

    def body(*refs):
        ins, refs = refs[:len(arrays)], refs[len(arrays):]
        stage, outs = refs[:n], refs[n:n + n_gather]
        send, recv, local_sem = refs[n + n_gather + (ahead is not None):][:3]
        x, y, c = _place()
        me, sibling, beside_x, beside_y, far = (x, y, c), (x, y, 1 - c), (1 - x, y, c), (x, 1 - y, c), (1 - x, 1 - y, c)
        slot = lambda dev: 4 * dev[0] + 2 * dev[1] + dev[2]
        other = lambda dev: (dev[0], dev[1], 1 - c)
        whole = lambda a: (0, shapes[a][0])
        cuts = [rows // 2 if rows % 32 == 0 else rows for rows, _ in shapes]
        first = lambda a: (0, cuts[a])
        rest = lambda a: (cuts[a], shapes[a][0] - cuts[a])

        def copy(a, k, block, rows, to, src=None):
            dst = outs[a].at[block, pl.ds(*rows)]
            return pltpu.make_async_remote_copy(
                src_ref=dst if src is None else src, dst_ref=dst, send_sem=send.at[a * AG_COPIES + k],
                recv_sem=recv.at[a * AG_COPIES + k], device_id=to, device_id_type=MESH)

        def cast(a):
            found, layer = where[a]
            at = 0
            for i in found:
                rows = arrays[i].shape[1]
                stage[a][at:at + rows, :] = ins[i][layer].astype(dtypes[a])
                at += rows

        for a in range(n_gather):
            cast(a)
        started, ahead_sends = [], []
        for a in range(n_gather):
            own = pltpu.make_async_copy(stage[a], outs[a].at[slot(me)], local_sem.at[a])
            own.start()
            started.append(own)
        sends = []
        for a in range(n_gather):
            sends += [copy(a, k, slot(me), whole(a), to, src=stage[a]) for k, to in enumerate((sibling, beside_x, beside_y))]
        for cp in sends:
            cp.start()
        for a in range(n_gather, n):
            cast(a)
        if ahead is not None:
            block, pieces = ahead
            out, (send2, recv2, local2) = refs[n + n_gather], refs[-3:]
            for i, piece in enumerate(pieces):
                src, dst = stage[block].at[pl.ds(*piece)], out.at[slot(me), pl.ds(*piece)]
                own = pltpu.make_async_copy(src, dst, local2.at[i])
                own.start()
                started.append(own)
                for k, to in enumerate((sibling, beside_x, beside_y)):
                    cp = pltpu.make_async_remote_copy(src_ref=src, dst_ref=dst, send_sem=send2.at[3 * i + k],
                                                      recv_sem=recv2.at[3 * i + k], device_id=to, device_id_type=MESH)
                    cp.start()
                    ahead_sends.append(cp)
        for a in range(n_gather):
            copy(a, 1, slot(beside_x), whole(a), me).wait_recv()
            copy(a, 2, slot(beside_y), whole(a), me).wait_recv()
            passed = [copy(a, 3, slot(beside_x), first(a), beside_y), copy(a, 5, slot(beside_x), whole(a), sibling),
                      copy(a, 6, slot(beside_y), whole(a), sibling)]
            if rest(a)[1]:
                passed.append(copy(a, 4, slot(beside_y), rest(a), beside_x))
            for cp in passed:
                cp.start()
            sends += passed
        for a in range(n_gather):
            copy(a, 3, slot(far), first(a), me).wait_recv()
            if rest(a)[1]:
                copy(a, 4, slot(far), rest(a), me).wait_recv()
            passed = copy(a, 7, slot(far), whole(a), sibling)
            passed.start()
            sends.append(passed)
        for a in range(n_gather):
            for k, source in ((0, me), (5, beside_x), (6, beside_y), (7, far)):
                copy(a, k, slot(other(source)), whole(a), me).wait_recv()
        for cp in sends:
            cp.wait_send()
        for cp in ahead_sends:
            cp.wait()
        for own in started:
            own.wait()

    gathered = list(range(n_gather)) + ([ahead[0]] if ahead is not None else [])
    sems = [pltpu.SemaphoreType.DMA((n_gather * AG_COPIES,)), pltpu.SemaphoreType.DMA((n_gather * AG_COPIES,)),
            pltpu.SemaphoreType.DMA((n_gather,))]
    if ahead is not None:
        sems += [pltpu.SemaphoreType.DMA((3 * len(ahead[1]),)), pltpu.SemaphoreType.DMA((3 * len(ahead[1]),)),
                 pltpu.SemaphoreType.DMA((len(ahead[1]),))]
    res = pl.pallas_call(
        body, name=name,
        in_specs=[IN_VMEM] * len(arrays), out_specs=[IN_VMEM] * n + [ANY] * len(gathered),
        out_shape=[_sds(s, dt) for s, dt in zip(shapes, dtypes)] + [_sds((N_DEV,) + shapes[a], dtypes[a]) for a in gathered],
        scratch_shapes=sems,
        compiler_params=pltpu.CompilerParams(vmem_limit_bytes=VMEM_LIMIT_BYTES),
    )(*arrays)
    return list(res[:n]), list(res[n:])


def _pair_sum(g, r, core, name):
    _, _, rows, cols = g.shape
    tr = _pick(rows, (512, 256, 128))

    def body(core_ref, g_ref, r_ref, p_ref):
        del core_ref
        p_ref[...] = (g_ref[...].astype(F32) + r_ref[...].astype(F32)).astype(p_ref.dtype)

    return pl.pallas_call(
        body, name=name,
        grid_spec=pltpu.PrefetchScalarGridSpec(
            num_scalar_prefetch=1, grid=(4, rows // tr),
            in_specs=[pl.BlockSpec((None, None, tr, cols), lambda q, i, core_ref: (q, core_ref[0], i, 0)),
                      pl.BlockSpec((None, tr, cols), lambda q, i, core_ref: (q, i, 0))],
            out_specs=pl.BlockSpec((None, tr, cols), lambda q, i, core_ref: (q, i, 0))),
        out_shape=_sds((4, rows, cols), g.dtype),
        compiler_params=_params("parallel", "parallel"),
    )(core, g, r)


def _adam(w, g, m, v):
    m2 = ADAM_B1 * m + (1.0 - ADAM_B1) * g
    v2 = ADAM_B2 * v + (1.0 - ADAM_B2) * (g * g)
    m_hat = m2 / (1.0 - ADAM_B1 ** ADAM_STEP)
    v_hat = v2 / (1.0 - ADAM_B2 ** ADAM_STEP)
    return -ADAM_LR * (m_hat / (jnp.sqrt(v_hat) + ADAM_EPS) + ADAM_WD * w), m2, v2


def _shard_update(own, index, received, w, m, v, layer, so_far, name):
    _, rows, cols = w.shape
    n_recv = received.shape[0]
    tr = _shard_tile(rows)

    def body(index_ref, p_ref, q_ref, w_ref, m_ref, v_ref, *rest):
        del index_ref
        g_out, d_out, m_out, v_out = rest[-4:]
        g = p_ref[...].astype(F32)
        for s in range(n_recv):
            g = g + q_ref[s].astype(F32)
        g_out[...] = g
        d_out[...], m_out[...], v_out[...] = _adam(w_ref[...], g, m_ref[...], v_ref[...])

    tile = pl.BlockSpec((None, tr, cols), lambda i, index_ref: (layer, i, 0))
    kept = list(so_far) if so_far is not None else []
    return pl.pallas_call(
        body, name=name,
        grid_spec=pltpu.PrefetchScalarGridSpec(
            num_scalar_prefetch=1, grid=(rows // tr,),
            in_specs=[pl.BlockSpec((None, tr, cols), lambda i, index_ref: (index_ref[0], i, 0)),
                      pl.BlockSpec((n_recv, tr, cols), lambda i, index_ref: (0, i, 0)), tile, tile, tile] + [ANY] * len(kept),
            out_specs=[tile] * 4),
        out_shape=[_sds(w.shape, F32)] * 4,
        input_output_aliases={6 + i: i for i in range(len(kept))},
        compiler_params=_params("parallel"),
    )(index, own, received, w, m, v, *kept)


def _natural_pieces(shape, r, c):
    if tuple(shape[-2:]) == (r, c) and all(n == 1 for n in shape[:-2]):
        return [((0,) * (len(shape) - 2), (0, c))]
    groups, width = shape[1], shape[3]
    assert len(shape) == 4 and shape[0] == 1 and shape[2] == r and groups * width == c, (shape, r, c)
    return [((0, g), (g * width, width)) for g in range(groups)]


def _replicated_update(shares, where, params, name):
    flat = [a for p in params if p is not None for a in p]

    def body(s_ref, *refs):
        ins, outs = refs[:len(flat)], refs[len(flat):]
        total = s_ref[0]
        for dev in range(1, N_DEV):
            total = total + s_ref[dev]
        i_at = o_at = 0
        for ranges, p in zip(where, params):
            chunks = [total[r0:r0 + r, :c] for r0, r, c in ranges]
            g2d = chunks[0] if len(chunks) == 1 else jnp.concatenate(chunks, axis=1)
            if p is None:
                outs[o_at][...] = g2d
                o_at += 1
                continue
            w_ref, m_ref, v_ref = ins[i_at:i_at + 3]
            for idx, (c0, cols) in _natural_pieces(p[0].shape, *g2d.shape):
                at = idx + (slice(None), slice(None))
                g = g2d[:, c0:c0 + cols]
                outs[o_at][at] = g
                outs[o_at + 1][at], outs[o_at + 2][at], outs[o_at + 3][at] = _adam(w_ref[at], g, m_ref[at], v_ref[at])
            i_at, o_at = i_at + 3, o_at + 4

    out_shape = []
    for ranges, p in zip(where, params):
        out_shape += [_sds((ranges[0][1], sum(c for _, _, c in ranges)), F32)] if p is None else [_sds(p[0].shape, F32)] * 4
    res = pl.pallas_call(
        body, name=name, out_shape=out_shape, compiler_params=pltpu.CompilerParams(vmem_limit_bytes=VMEM_LIMIT_BYTES),
    )(shares, *flat)
    out, at = [], 0
    for p in params:
        out.append(list(res[at:at + (1 if p is None else 4)]))
        at += 1 if p is None else 4
    return out


def _rope_tables(t):
    half = HEAD_DIM // 2
    inv_freq = np.float32(ROPE_THETA) ** (-(np.arange(half, dtype=np.float32) * np.float32(2.0)) / np.float32(HEAD_DIM))
    ang = np.arange(t, dtype=np.float32)[:, None] * inv_freq[None, :].astype(np.float32)
    cos, sin = np.cos(ang).astype(np.float32), np.sin(ang).astype(np.float32)
    return (jnp.asarray(np.tile(np.concatenate([cos, cos], axis=1), (1, 2))),
            jnp.asarray(np.tile(np.concatenate([-sin, sin], axis=1), (1, 2))))


def _rows_full(gathered):
    return gathered.reshape(-1, gathered.shape[-1])


def _rows_blocked(full):
    return full.reshape(N_DEV, full.shape[0] // N_DEV, full.shape[1]).astype(ACT)


def _pack_rows(parts, width):
    rows, where, at = [], [], 0
    for v in parts:
        r, c = v.shape
        n_rows = -(-r // 8) * 8
        chunks = []
        for c0 in range(0, c, width):
            chunk = v[:, c0:c0 + width].astype(F32)
            rows.append(jnp.pad(chunk, ((0, n_rows - r), (0, width - chunk.shape[1]))))
            chunks.append((at, r, chunk.shape[1]))
            at += n_rows
        where.append(chunks)
    return jnp.concatenate(rows, axis=0), where


def _halves(block):
    half = block.shape[0] // 2
    return (0, half), (half, half)


def _whole(block):
    return (0, block.shape[0])


EARLY = ("attn_w_qkv", "sgu_ln", "attn_w_o")
LATE = ("sgu_w_in", "sgu_w_out", "gu0", "gu1", "dn0", "dn1")
SWAPPED = ("attn_w_qkv", "ffn_w_gate_up")
BEFORE_ATTN = ("norm_mix_post", "norm_ffn_pre", "norm_ffn_post", "attn_b_o", "sgu_w_spatial", "sgu_b_spatial")
AFTER_ATTN = ("attn_b_qkv", "attn_sinks")
LAST = ("norm_mix_pre",)
WEIGHTS = ("norm_mix_pre", "norm_mix_post", "norm_ffn_pre", "norm_ffn_post", "attn_w_qkv", "attn_b_qkv", "attn_sinks", "attn_w_o",
           "attn_b_o", "sgu_w_in", "sgu_ln_g", "sgu_ln_b", "sgu_w_spatial", "sgu_b_spatial", "sgu_w_out", "ffn_w_gate_up", "ffn_w_down")


LAYER_OF = {"gu0": ("ffn_w_gate_up", 0), "gu1": ("ffn_w_gate_up", 1), "dn0": ("ffn_w_down", 0), "dn1": ("ffn_w_down", 1)}


def _source(tree, name):
    if name == "sgu_ln":
        return [tree["sgu_ln_g"][None], tree["sgu_ln_b"][None]], 0
    leaf, layer = LAYER_OF.get(name, (name, 0))
    return [tree[leaf]], layer


def kernel(x, norm_mix_pre, norm_mix_post, norm_ffn_pre, norm_ffn_post, attn_w_qkv, attn_b_qkv, attn_sinks, attn_w_o, attn_b_o, sgu_w_in, sgu_ln_g, sgu_ln_b, sgu_w_spatial, sgu_b_spatial, sgu_w_out, ffn_w_gate_up, ffn_w_down, loss_target, m_norm_mix_pre, m_norm_mix_post, m_norm_ffn_pre, m_norm_ffn_post, m_attn_w_qkv, m_attn_b_qkv, m_attn_sinks, m_attn_w_o, m_attn_b_o, m_sgu_w_in, m_sgu_ln_g, m_sgu_ln_b, m_sgu_w_spatial, m_sgu_b_spatial, m_sgu_w_out, m_ffn_w_gate_up, m_ffn_w_down, v_norm_mix_pre, v_norm_mix_post, v_norm_ffn_pre, v_norm_ffn_post, v_attn_w_qkv, v_attn_b_qkv, v_attn_sinks, v_attn_w_o, v_attn_b_o, v_sgu_w_in, v_sgu_ln_g, v_sgu_ln_b, v_sgu_w_spatial, v_sgu_b_spatial, v_sgu_w_out, v_ffn_w_gate_up, v_ffn_w_down):
    w = dict(norm_mix_pre=norm_mix_pre, norm_mix_post=norm_mix_post, norm_ffn_pre=norm_ffn_pre, norm_ffn_post=norm_ffn_post,
             attn_w_qkv=attn_w_qkv, attn_b_qkv=attn_b_qkv, attn_sinks=attn_sinks, attn_w_o=attn_w_o, attn_b_o=attn_b_o,
             sgu_w_in=sgu_w_in, sgu_ln_g=sgu_ln_g, sgu_ln_b=sgu_ln_b, sgu_w_spatial=sgu_w_spatial, sgu_b_spatial=sgu_b_spatial,
             sgu_w_out=sgu_w_out, ffn_w_gate_up=ffn_w_gate_up, ffn_w_down=ffn_w_down)
    mom = dict(norm_mix_pre=m_norm_mix_pre, norm_mix_post=m_norm_mix_post, norm_ffn_pre=m_norm_ffn_pre, norm_ffn_post=m_norm_ffn_post,
               attn_w_qkv=m_attn_w_qkv, attn_b_qkv=m_attn_b_qkv, attn_sinks=m_attn_sinks, attn_w_o=m_attn_w_o, attn_b_o=m_attn_b_o,
               sgu_w_in=m_sgu_w_in, sgu_ln_g=m_sgu_ln_g, sgu_ln_b=m_sgu_ln_b, sgu_w_spatial=m_sgu_w_spatial,
               sgu_b_spatial=m_sgu_b_spatial, sgu_w_out=m_sgu_w_out, ffn_w_gate_up=m_ffn_w_gate_up, ffn_w_down=m_ffn_w_down)
    var = dict(norm_mix_pre=v_norm_mix_pre, norm_mix_post=v_norm_mix_post, norm_ffn_pre=v_norm_ffn_pre, norm_ffn_post=v_norm_ffn_post,
               attn_w_qkv=v_attn_w_qkv, attn_b_qkv=v_attn_b_qkv, attn_sinks=v_attn_sinks, attn_w_o=v_attn_w_o, attn_b_o=v_attn_b_o,
               sgu_w_in=v_sgu_w_in, sgu_ln_g=v_sgu_ln_g, sgu_ln_b=v_sgu_ln_b, sgu_w_spatial=v_sgu_w_spatial,
               sgu_b_spatial=v_sgu_b_spatial, sgu_w_out=v_sgu_w_out, ffn_w_gate_up=v_ffn_w_gate_up, ffn_w_down=v_ffn_w_down)
    w, mom, var = [{**tree, **{n: jnp.swapaxes(tree[n], 1, 2) for n in SWAPPED}} for tree in (w, mom, var)]
    xs, target = x[0], loss_target[0]
    t, d = xs.shape
    row = lambda v: v.reshape(1, -1).astype(F32)
    core = lax.axis_index("c").astype(jnp.int32).reshape(1)
    chip = (2 * lax.axis_index("x") + lax.axis_index("y")).astype(jnp.int32).reshape(1)
    names = EARLY + LATE
    gu_rows = w["ffn_w_gate_up"].shape[1] // 4
    gu0_parts = [(i * gu_rows, gu_rows) for i in range(4)]
    staged, (*early, g_gu0) = _prepare([_source(w, n) for n in names], [F32 if n == "sgu_ln" else ACT for n in names], len(EARLY),
                                       "weights_prepare", ahead=(names.index("gu0"), gu0_parts[:2]))
    stage = dict(zip(names, staged))
    w_qkv = _rows_full(early[0])
    lg, lb = row(early[1][:, 0, :]), row(early[1][:, 1, :])
    w_o = _rows_full(early[2])
    b_qkv = row(attn_b_qkv)
    sinks = attn_sinks.reshape(1, -1).astype(F32)
    ws = sgu_w_spatial[0].astype(F32)
    bs_t = sgu_b_spatial[0].astype(F32).T
    cos, sin = _rope_tables(t)
    gather = lambda name, so_far, **pieces: _gather_job(stage[name], so_far, **pieces)
    a_half = lambda name: _halves(stage[name])[0]
    b_half = lambda name: _halves(stage[name])[1]
    whole = lambda name: _whole(stage[name])

    def pieces(name, n):
        rows = stage[name].shape[0] // n
        return [(i * rows, rows) for i in range(n)]

    ways = lambda parts: [(piece, i % 2) for i, piece in enumerate(parts)]
    relay = lambda name, so_far, **steps: _relay_gather_job(stage[name], so_far, **steps)
    gu1_parts = pieces("gu1", 4)
    dn0_parts, dn1_parts, in_parts, out_parts = pieces("dn0", 2), pieces("dn1", 2), pieces("sgu_w_in", 2), pieces("sgu_w_out", 2)
    (h0, q, kdup, vdup), ((g_gu0,),) = _attn_qkv_fwd(
        xs, row(norm_mix_pre[0]), w_qkv, b_qkv, cos, sin, jobs=[relay("gu0", g_gu0, sends=gu0_parts[2:])])
    (o,), ((g_gu0,), (g_dn0,)) = _attn_fwd(q, kdup, vdup, sinks, jobs=[
        relay("gu0", g_gu0, relays=ways(gu0_parts), forwards=gu0_parts), relay("dn0", None, sends=dn0_parts)])
    (m0, x1), ((g_gu0,), (g_dn0,), (g_in,)) = _proj_res(o, w_o, row(attn_b_o), row(norm_mix_post[0]), xs, "attn_out_fwd", jobs=[
        relay("gu0", g_gu0, far=gu0_parts), relay("dn0", g_dn0, relays=ways(dn0_parts), forwards=dn0_parts, late_far=dn0_parts),
        relay("sgu_w_in", None, sends=in_parts)])
    w_gu0, w_dn0 = _rows_full(g_gu0), _rows_full(g_dn0)
    (h1, gu0, a0, f0, x2), ((g_in,), (g_out,), (g_gu1,), (g_dn1,)) = _ffn_fwd(
        x1, row(norm_ffn_pre[0]), w_gu0, w_dn0, row(norm_ffn_post[0]), jobs=[
            relay("sgu_w_in", g_in, relays=ways(in_parts), forwards=in_parts, late_far=in_parts),
            relay("sgu_w_out", None, sends=out_parts), relay("gu1", None, sends=gu1_parts), relay("dn1", None, sends=dn1_parts)])
    w_in = g_in
    (h2, z, y), ((g_out,), (g_gu1,), (g_dn1,)) = _sgu_fwd(x2, row(norm_mix_pre[1]), w_in, lg, lb, ws, bs_t, jobs=[
        relay("sgu_w_out", g_out, relays=ways(out_parts), forwards=out_parts, late_far=out_parts),
        relay("gu1", g_gu1, relays=ways(gu1_parts), forwards=gu1_parts),
        relay("dn1", g_dn1, relays=ways(dn1_parts), forwards=dn1_parts)])
    w_out = _rows_full(g_out)
    (m1, x3), ((g_gu1,), (g_dn1,)) = _proj_res(y, w_out, None, row(norm_mix_post[1]), x2, "sgu_out_fwd", jobs=[
        relay("gu1", g_gu1, far=gu1_parts), relay("dn1", g_dn1, far=dn1_parts)])
    w_gu1, w_dn1 = _rows_full(g_gu1), _rows_full(g_dn1)

    to_sibling = lambda g: _scatter_job([g], 4, _to_sibling)
    pair = lambda g, r, name: _pair_sum(g.reshape((4, 2) + g.shape[1:]), r, core, "pair_sum_" + name)
    to_chips = lambda p, prev=None, piece=None: _scatter_job([p], 3, _to_owner_chip, prev=prev, piece=piece)
    out_g, out_d, out_m, out_v = {}, {}, {}, {}

    trees = [{**tree, "sgu_ln": jnp.stack([tree["sgu_ln_g"], tree["sgu_ln_b"]], axis=1)} for tree in (w, mom, var)]
    so_far = {}

    def update(name, own, index, received):
        leaf, layer = LAYER_OF.get(name, (name, 0))
        so_far[leaf] = _shard_update(own, index, received, *[tree[leaf] for tree in trees], layer, so_far.get(leaf), "update_" + name)
        for out, val in zip((out_g, out_d, out_m, out_v), so_far[leaf]):
            out[leaf] = val

    def finish(names, n_extra, shares, where, name):
        res = _replicated_update(shares, where, [(w[n], mom[n], var[n]) for n in names] + [None] * n_extra, name)
        for n, vals in zip(names, res):
            out_g[n], out_d[n], out_m[n], out_v[n] = vals
        return [vals[0] for vals in res[len(names):]]

    first_half = lambda p: _halves(p[0])[0]
    second_half = lambda p: _halves(p[0])[1]
    (df1, dgu1, dx3, dg_ffn_post1, dg_ffn_pre1, loss_tile, h3, a1), _ = _ffn_bwd(
        None, None, row(norm_ffn_post[1]), w_dn1, None, w_gu1, x3, row(norm_ffn_pre[1]), "ffn_last", target=target)
    b_gu1, _ = _wgrad(h3, dgu1, "ffn_up_wgrad1", ACT, transposed=True)
    b_gu1 = _rows_blocked(b_gu1)
    dw_dn1, _ = _wgrad(a1, df1, "ffn_down_wgrad1", ACT)
    b_dn1 = _rows_blocked(dw_dn1)
    (dy, dw_out, dg_mix_post1), ((r_gu1,), (r_dn1,)) = _post_bwd(
        dx3, m1, row(norm_mix_post[1]), w_out, y, "sgu", "sgu_out_bwd", jobs=[to_sibling(b_gu1), to_sibling(b_dn1)])
    p_gu1, p_dn1 = pair(b_gu1, r_gu1, "gu1"), pair(b_dn1, r_dn1, "dn1")
    b_out = _rows_blocked(dw_out)
    (dz, dlg, dlb, dws, dbs_t), ((q_gu1,), (r_out,)) = _sgu_mix_bwd(z, dy, lg, lb, ws, bs_t, jobs=[
        to_chips(p_gu1, piece=first_half(p_gu1)), to_sibling(b_out)])
    p_out = pair(b_out, r_out, "sgu_w_out")
    b_in, _ = _wgrad(h2, dz, "sgu_in_wgrad", ACT, out_block=stage["sgu_w_in"].shape[1])
    b_ln = jnp.stack([dlg.reshape(N_DEV, -1), dlb.reshape(N_DEV, -1)], axis=1)
    (dx2, dg_mix_pre1), _ = _pre_bwd(dz, w_in, x2, row(norm_mix_pre[1]), dx3, "sgu_in_bwd")
    (df0, dgu0, dx1, dg_ffn_post0, dg_ffn_pre0), ((q_gu1,), (q_dn1,), (q_out,), (r_in,), (r_ln,)) = _ffn_bwd(
        dx2, f0, row(norm_ffn_post[0]), w_dn0, gu0, w_gu0, x1, row(norm_ffn_pre[0]), "ffn_bwd0",
        jobs=[to_chips(p_gu1, prev=[q_gu1], piece=second_half(p_gu1)), to_chips(p_dn1), to_chips(p_out), to_sibling(b_in),
              to_sibling(b_ln)])
    update("gu1", p_gu1, chip, q_gu1)
    update("dn1", p_dn1, chip, q_dn1)
    update("sgu_w_out", p_out, chip, q_out)
    p_in, p_ln = pair(b_in, r_in, "sgu_w_in"), pair(b_ln, r_ln, "sgu_ln")
    dw_dn0, _ = _wgrad(a0, df0, "ffn_down_wgrad0", ACT)
    b_dn0 = _rows_blocked(dw_dn0)
    b_gu0, ((q_in,), (q_ln,), (r_dn0,)) = _wgrad(h1, dgu0, "ffn_up_wgrad0", ACT, transposed=True, jobs=[
        to_chips(p_in), to_chips(p_ln), to_sibling(b_dn0)])
    update("sgu_w_in", p_in, chip, q_in)
    update("sgu_ln", p_ln, chip, q_ln)
    b_gu0 = _rows_blocked(b_gu0)
    p_dn0 = pair(b_dn0, r_dn0, "dn0")
    (do, dw_o, dg_mix_post0, db_o), ((r_gu0,), (q_dn0,)) = _post_bwd(
        dx1, m0, row(norm_mix_post[0]), w_o, o, "attn", "attn_out_bwd", jobs=[to_sibling(b_gu0), to_chips(p_dn0)])
    p_gu0 = pair(b_gu0, r_gu0, "gu0")
    b_o = _rows_blocked(dw_o)
    grads = {
        "norm_mix_post": jnp.concatenate([dg_mix_post0, dg_mix_post1], axis=0),
        "norm_ffn_pre": jnp.concatenate([dg_ffn_pre0, dg_ffn_pre1], axis=0),
        "norm_ffn_post": jnp.concatenate([dg_ffn_post0, dg_ffn_post1], axis=0),
        "attn_b_o": db_o,
        "sgu_w_spatial": dws,
        "sgu_b_spatial": dbs_t.T,
    }
    shares1, where1 = _pack_rows([grads[name] for name in BEFORE_ATTN], d)
    (dqkv, db_qkv, dsink), ((q_gu0,), (r_o,), (g_shares1,)) = _attn_bwd(q, kdup, vdup, do, sinks, cos, sin, jobs=[
        to_chips(p_gu0), to_sibling(b_o),
        _gather_job(shares1, None, sends=[_whole(shares1)])])
    update("gu0", p_gu0, chip, q_gu0)
    update("dn0", p_dn0, chip, q_dn0)
    p_o = pair(b_o, r_o, "attn_w_o")
    grads.update({"attn_b_qkv": db_qkv, "attn_sinks": dsink[:1, :d // HEAD_DIM]})
    shares2, where2 = _pack_rows([grads[name] for name in AFTER_ATTN] + [loss_tile[:1, :1]], d)
    (dx0, dg_mix_pre0), _ = _pre_bwd(dqkv, w_qkv, xs, row(norm_mix_pre[0]), dx1, "attn_qkv_bwd", True)
    grads["norm_mix_pre"] = jnp.concatenate([dg_mix_pre0, dg_mix_pre1], axis=0)
    shares3, where3 = _pack_rows([grads[name] for name in LAST], d)
    dw_qkv, ((q_o,), (g_shares1,), (g_shares2,), (g_shares3,)) = _wgrad(h0, dqkv, "attn_qkv_wgrad", ACT, transposed=True, jobs=[
        to_chips(p_o), _gather_job(shares1, g_shares1, forwards=[_whole(shares1)]),
        _gather_job(shares2, None, sends=[_whole(shares2)]), _gather_job(shares3, None, sends=[_whole(shares3)])])
    update("attn_w_o", p_o, chip, q_o)
    b_qkv_grad = _rows_blocked(dw_qkv)
    (r_qkv,), (g_shares2,), (g_shares3,) = _copies_only([
        to_sibling(b_qkv_grad), _gather_job(shares2, g_shares2, forwards=[_whole(shares2)]),
        _gather_job(shares3, g_shares3, forwards=[_whole(shares3)])], "grads_tail_to_sibling")
    p_qkv = pair(b_qkv_grad, r_qkv, "attn_w_qkv")
    (q_qkv,), = _copies_only([to_chips(p_qkv)], "grads_tail_to_owner_chip")
    update("attn_w_qkv", p_qkv, chip, q_qkv)

    finish(BEFORE_ATTN, 0, g_shares1, where1, "replicated_update_before_attn")
    loss = finish(AFTER_ATTN, 1, g_shares2, where2, "replicated_update_after_attn")[0][0, 0]
    finish(LAST, 0, g_shares3, where3, "replicated_update_last")

    for out in (out_g, out_d, out_m, out_v):
        out["sgu_ln_g"], out["sgu_ln_b"] = out["sgu_ln"][:, 0, :], out["sgu_ln"][:, 1, :]
        for n in SWAPPED:
            out[n] = jnp.swapaxes(out[n], 1, 2)

    return (loss, dx0[None], *[out_g[n] for n in WEIGHTS], *[out_d[n] for n in WEIGHTS],
            *[out_m[n] for n in WEIGHTS], *[out_v[n] for n in WEIGHTS])
```

```python
import functools
import math
import operator

import jax
import jax.numpy as jnp
import numpy as np
from jax import lax
from jax.experimental import pallas as pl
from jax.experimental.pallas import tpu as pltpu

F32 = jnp.float32
ACT = jnp.bfloat16

EPS = 1e-6
HEAD_DIM = 64
PAIR = 2 * HEAD_DIM
GQA_GROUP = 4
WINDOW = 128
ROPE_THETA = 10000.0
SCALE = HEAD_DIM ** -0.5
MASKED = -1e30
LANES = 128
MXU_WIDTH = 256

ADAM_LR, ADAM_B1, ADAM_B2, ADAM_EPS, ADAM_WD, ADAM_STEP = 0.001, 0.9, 0.999, 1e-08, 0.01, 10

N_DEV = 8
VMEM_LIMIT_BYTES = 56 * 1024 * 1024
MESH = pl.DeviceIdType.MESH
ANY = pl.BlockSpec(memory_space=pl.ANY)
IN_VMEM = pl.BlockSpec(memory_space=pltpu.VMEM)


def _pick(n, candidates):
    for c in candidates:
        if n % c == 0:
            return c
    return n


def _row_tile(t, most=512):
    return _pick(t, (most,))


def _shard_tile(rows):
    return next((c for c in (512, 256, 176, 128, 96, 64) if rows % c == 0 and rows >= 2 * c), rows)


def _mxu_chunks(n, most=4 * MXU_WIDTH):
    assert n % MXU_WIDTH == 0 and most % MXU_WIDTH == 0
    return [(c, min(most, n - c)) for c in range(0, n, most)]


def _params(*sem):
    return pltpu.CompilerParams(dimension_semantics=sem, vmem_limit_bytes=VMEM_LIMIT_BYTES)


def _full(shape):
    return pl.BlockSpec(shape, lambda *_: (0,) * len(shape))


def _resident(shape):
    return pl.BlockSpec(shape, lambda *_: (0,) * len(shape), pipeline_mode=pl.Buffered(1))


def _rows(tm, n):
    return pl.BlockSpec((tm, n), lambda i: (i, 0))


def _sds(shape, dtype):
    return jax.ShapeDtypeStruct(shape, dtype)


def _place():
    return lax.axis_index("x"), lax.axis_index("y"), lax.axis_index("c")


def _other_chips(x, y):
    return [(1 - x, y), (x, 1 - y), (1 - x, 1 - y)]


class _Job:
    def __init__(self, inputs, out_shape, aliases, emit, n_remote, n_local=0, n_late=0):
        self.inputs, self.out_shape, self.aliases, self.emit = list(inputs), list(out_shape), dict(aliases), emit
        self.n_remote, self.n_local, self.n_late = n_remote, n_local, n_late


def _call(body, name, grid, in_specs, out_specs, out_shape, args, sem, scratch=(), jobs=()):
    n_in, n_out, n_scr = len(args), len(out_shape), len(scratch)
    j_in = [a for job in jobs for a in job.inputs]
    j_out = [s for job in jobs for s in job.out_shape]
    aliases, at_in, at_out = {}, n_in, n_out
    for job in jobs:
        aliases.update({at_in + i: at_out + o for i, o in job.aliases.items()})
        at_in, at_out = at_in + len(job.inputs), at_out + len(job.out_shape)
    n_remote = sum(job.n_remote for job in jobs)
    n_local = sum(job.n_local for job in jobs)

    def wrapped(*refs):
        ins, jin = refs[:n_in], refs[n_in:n_in + len(j_in)]
        rest = refs[n_in + len(j_in):]
        outs, jout = rest[:n_out], rest[n_out:n_out + len(j_out)]
        scr = rest[n_out + len(j_out):n_out + len(j_out) + n_scr]
        if not jobs:
            body(*ins, *outs, *scr)
            return
        send, recv, local = rest[n_out + len(j_out) + n_scr:]
        ids = [pl.program_id(a) for a in range(len(grid))]
        first = functools.reduce(operator.and_, [i == 0 for i in ids])
        last = functools.reduce(operator.and_, [i == g - 1 for i, g in zip(ids, grid)])

        def descriptors():
            place, found, i, o, r, l = _place(), ([], []), 0, 0, 0, 0
            for job in jobs:
                emitted = job.emit(jin[i:i + len(job.inputs)], jout[o:o + len(job.out_shape)], place)
                for at, (src, dst, to) in enumerate(emitted):
                    if to is None:
                        cp = pltpu.make_async_copy(src, dst, local.at[l])
                        l += 1
                    else:
                        cp = pltpu.make_async_remote_copy(src_ref=src, dst_ref=dst, send_sem=send.at[r], recv_sem=recv.at[r],
                                                          device_id=to, device_id_type=MESH)
                        r += 1
                    found[at >= len(emitted) - job.n_late].append(cp)
                i, o = i + len(job.inputs), o + len(job.out_shape)
            return found

        @pl.when(first)
        def _():
            for cp in descriptors()[0]:
                cp.start()

        body(*ins, *outs, *scr)

        @pl.when(last)
        def _():
            early, late = descriptors()
            for cp in early:
                cp.wait()
            for cp in late:
                cp.start()
            for cp in late:
                cp.wait()

    sems = [pltpu.SemaphoreType.DMA((n_remote,)), pltpu.SemaphoreType.DMA((n_remote,)),
            pltpu.SemaphoreType.DMA((max(n_local, 1),))] if jobs else []
    res = pl.pallas_call(
        wrapped, name=name, grid=grid,
        in_specs=list(in_specs) + [ANY] * len(j_in), out_specs=list(out_specs) + [ANY] * len(j_out),
        out_shape=list(out_shape) + j_out, scratch_shapes=list(scratch) + sems, input_output_aliases=aliases,
        compiler_params=_params(*(("arbitrary",) * len(grid) if jobs else sem)),
    )(*args, *j_in)
    job_res, at = [], n_out
    for job in jobs:
        job_res.append(list(res[at:at + len(job.out_shape)]))
        at += len(job.out_shape)
    return list(res[:n_out]), job_res


def _copies_only(jobs, name):
    def body(o_ref):
        o_ref[...] = jnp.zeros_like(o_ref)

    return _call(body, name, (1,), [], [_full((8, LANES))], [_sds((8, LANES), F32)], (), ("arbitrary",), jobs=jobs)[1]


def _gather_job(stage, gathered, sends=(), forwards=()):
    shape = _sds((N_DEV,) + stage.shape, stage.dtype)
    inputs = ([stage] if sends else []) + ([gathered] if gathered is not None else [])
    aliases = {len(inputs) - 1: 0} if gathered is not None else {}

    def emit(ins, outs, place):
        x, y, c = place
        sibling, chips, mine, g = (x, y, 1 - c), _other_chips(x, y), 4 * x + 2 * y + c, outs[0]
        found = []
        for r0, nr in sends:
            src, dst = ins[0].at[pl.ds(r0, nr)], g.at[mine, pl.ds(r0, nr)]
            found += [(src, dst, None), (src, dst, sibling)] + [(src, dst, (px, py, c)) for px, py in chips]
        for r0, nr in forwards:
            for px, py in chips:
                block = 4 * px + 2 * py + c
                found.append((ins[-1].at[block, pl.ds(r0, nr)], g.at[block, pl.ds(r0, nr)], sibling))
        return found

    return _Job(inputs, [shape], aliases, emit, 4 * len(sends) + 3 * len(forwards), len(sends))


def _relay_gather_job(stage, gathered, sends=(), relays=(), forwards=(), far=(), late_far=()):
    shape = _sds((N_DEV,) + stage.shape, stage.dtype)
    inputs = ([stage] if sends else []) + ([gathered] if gathered is not None else [])
    aliases = {len(inputs) - 1: 0} if gathered is not None else {}

    def emit(ins, outs, place):
        x, y, c = place
        sibling, beside_x, beside_y, g = (x, y, 1 - c), (1 - x, y, c), (x, 1 - y, c), outs[0]
        slot = lambda dev: 4 * dev[0] + 2 * dev[1] + dev[2]
        found = []
        for r0, nr in sends:
            src, dst = ins[0].at[pl.ds(r0, nr)], g.at[slot((x, y, c)), pl.ds(r0, nr)]
            found += [(src, dst, None), (src, dst, sibling), (src, dst, beside_x), (src, dst, beside_y)]

        def passed_on(block, piece, to):
            return ins[-1].at[block, pl.ds(*piece)], g.at[block, pl.ds(*piece)], to

        for piece, way in relays:
            found.append(passed_on(slot(beside_x), piece, beside_y) if way == 0 else passed_on(slot(beside_y), piece, beside_x))
        for piece in forwards:
            found += [passed_on(slot(beside_x), piece, sibling), passed_on(slot(beside_y), piece, sibling)]
        for piece in tuple(far) + tuple(late_far):
            found.append(passed_on(slot((1 - x, 1 - y, c)), piece, sibling))
        return found

    n_remote = 3 * len(sends) + len(relays) + 2 * len(forwards) + len(far) + len(late_far)
    return _Job(inputs, [shape], aliases, emit, n_remote, len(sends), len(late_far))


def _scatter_job(arrays, n_slots, route, prev=None, piece=None):
    shapes = [_sds((n_slots,) + v.shape[1:], v.dtype) for v in arrays]
    inputs = list(arrays) + (list(prev) if prev else [])
    aliases = {len(arrays) + i: i for i in range(len(arrays))} if prev else {}

    def emit(ins, outs, place):
        found = []
        for a in range(len(arrays)):
            for s in range(n_slots):
                block, to = route(s, *place)
                if piece is None:
                    found.append((ins[a].at[block], outs[a].at[s], to))
                else:
                    found.append((ins[a].at[block, pl.ds(*piece)], outs[a].at[s, pl.ds(*piece)], to))
        return found

    return _Job(inputs, shapes, aliases, emit, len(arrays) * n_slots)


def _to_sibling(s, x, y, c):
    return 2 * s + (1 - c), (x, y, 1 - c)


def _to_owner_chip(s, x, y, c):
    px, py = _other_chips(x, y)[s]
    return 2 * px + py, (px, py, c)


def _rstd(x):
    return lax.rsqrt(jnp.mean(x * x, axis=-1, keepdims=True) + EPS)


def _rms_bwd(xhat, r, g, dy):
    dxh = dy * g
    return r * (dxh - xhat * jnp.mean(dxh * xhat, axis=-1, keepdims=True))


def _first_half(rows):
    lane = lax.broadcasted_iota(jnp.int32, (rows, PAIR), 1)
    return (lane % HEAD_DIM) < (HEAD_DIM // 2)


def _rot_half(v, first):
    return jnp.where(first, pltpu.roll(v, PAIR - HEAD_DIM // 2, 1), pltpu.roll(v, HEAD_DIM // 2, 1))


def _rope(v, cos, sin, first):
    return v * cos + _rot_half(v, first) * sin


def _rope_t(dv, cos, sin, first):
    return dv * cos + _rot_half(dv * sin, first)


def _dot(a, b):
    return jnp.dot(a, b, preferred_element_type=F32)


def _dot_nt(a, b):
    return lax.dot_general(a, b, (((1,), (1,)), ((), ())), preferred_element_type=F32)


def _dot_tn(a, b):
    return lax.dot_general(a, b, (((0,), (0,)), ((), ())), preferred_element_type=F32)


def _sigmoid(v):
    return 1.0 / (1.0 + jnp.exp(-v))


_GELU_K = math.sqrt(2.0 / math.pi)
_GELU_C = 0.044715


def _gelu(v):
    return v * (0.5 * (1.0 + jnp.tanh(_GELU_K * (v + _GELU_C * (v * v * v)))))


def _gelu_grad(v):
    t = jnp.tanh(_GELU_K * (v + _GELU_C * (v * v * v)))
    return 0.5 * (1.0 + t) + 0.5 * v * (1.0 - t * t) * (_GELU_K * (1.0 + 3.0 * _GELU_C * (v * v)))


def _attn_qkv_fwd(x, g, w, b, cos, sin, jobs=()):
    t, d = x.shape
    n = w.shape[0]
    kd = n - d
    assert (kd // 2) % PAIR == 0
    tm = _row_tile(t)
    ch = _pick(d, (512,))

    def body(x_ref, g_ref, w_ref, b_ref, cos_ref, sin_ref, h_ref, q_ref, k_ref, v_ref):
        xv = x_ref[...]
        hb = (xv * _rstd(xv) * g_ref[...]).astype(ACT)
        h_ref[...] = hb
        cosv, sinv = cos_ref[...], sin_ref[...]
        first = _first_half(tm)
        left = _lane_halves()[0]

        def proj(lo, width):
            return _dot_nt(hb, w_ref[lo:lo + width, :]) + b_ref[:, lo:lo + width]

        def twice(ref, s, two_heads):
            swapped = pltpu.roll(two_heads, HEAD_DIM, 1)
            ref[:, 2 * s:2 * s + PAIR] = jnp.where(left, two_heads, swapped).astype(ACT)
            ref[:, 2 * s + PAIR:2 * s + 2 * PAIR] = jnp.where(left, swapped, two_heads).astype(ACT)

        for c in range(0, d, ch):
            y = proj(c, ch)
            for s in range(0, ch, PAIR):
                q_ref[:, c + s:c + s + PAIR] = (_rope(y[:, s:s + PAIR], cosv, sinv, first) * SCALE).astype(ACT)
        y = proj(d, kd)
        for s in range(0, kd // 2, PAIR):
            twice(k_ref, s, _rope(y[:, s:s + PAIR], cosv, sinv, first))
            twice(v_ref, s, y[:, kd // 2 + s:kd // 2 + s + PAIR])

    return _call(
        body, "attn_qkv_fwd", (t // tm,),
        [_rows(tm, d), _full((1, d)), _full((n, d)), _full((1, n)), _rows(tm, PAIR), _rows(tm, PAIR)],
        [_rows(tm, d), _rows(tm, d), _rows(tm, kd), _rows(tm, kd)],
        [_sds((t, d), ACT), _sds((t, d), ACT), _sds((t, kd), ACT), _sds((t, kd), ACT)],
        (x, g, w, b, cos, sin), ("parallel",), jobs=jobs)


STACK = GQA_GROUP * WINDOW


def _band_mask(has_prev):
    row = lax.broadcasted_iota(jnp.int32, (STACK, 2 * WINDOW), 0) % WINDOW
    col = lax.broadcasted_iota(jnp.int32, (STACK, 2 * WINDOW), 1)
    return ((col < WINDOW) & (col > row) & has_prev) | ((col >= WINDOW) & (col - WINDOW <= row))


def _lane_halves():
    lane = lax.broadcasted_iota(jnp.int32, (1, PAIR), 1)
    return lane < HEAD_DIM, lane >= HEAD_DIM


def _stack_heads(ref, h, halves):
    parts = []
    for p in range(2):
        pair = ref[:, (2 * h + p) * PAIR:(2 * h + p + 1) * PAIR]
        parts += [jnp.where(half, pair, jnp.zeros_like(pair)) for half in halves]
    return jnp.concatenate(parts, axis=0)


def _sink_column(sink_ref, h):
    row = lax.broadcasted_iota(jnp.int32, (STACK, 1), 0)
    col = jnp.full((STACK, 1), sink_ref[0, GQA_GROUP * h + GQA_GROUP - 1], F32)
    for j in range(GQA_GROUP - 2, -1, -1):
        col = jnp.where(row < (j + 1) * WINDOW, sink_ref[0, GQA_GROUP * h + j], col)
    return col


def _probs(scores, valid, sink):
    s = jnp.where(valid, scores, MASKED)
    m = jnp.maximum(jnp.max(s, axis=-1, keepdims=True), sink)
    p = jnp.exp(s - m)
    psink = jnp.exp(sink - m)
    inv = 1.0 / (jnp.sum(p, axis=-1, keepdims=True) + psink)
    return p * inv, psink * inv


def _attn_fwd(q, kd_, vd, sinks, jobs=()):
    t, d = q.shape
    kd = kd_.shape[1]
    cur = lambda n: (n, 0)
    prev = lambda n: (jnp.maximum(n - 1, 0), 0)

    def body(sink_ref, q_ref, kc_ref, kp_ref, vc_ref, vp_ref, o_ref):
        valid = _band_mask(pl.program_id(0) > 0)
        halves = _lane_halves()
        heads = range(kd // PAIR)
        hs = [slice(h * PAIR, (h + 1) * PAIR) for h in heads]
        scores = [_dot_nt(_stack_heads(q_ref, h, halves), jnp.concatenate([kp_ref[:, hs[h]], kc_ref[:, hs[h]]], axis=0)) for h in heads]
        probs = [_probs(scores[h], valid, _sink_column(sink_ref, h))[0].astype(ACT) for h in heads]
        for h in heads:
            v2 = jnp.concatenate([vp_ref[:, hs[h]], vc_ref[:, hs[h]]], axis=0)
            vs = jnp.concatenate([jnp.where(half, v2, jnp.zeros_like(v2)) for half in halves], axis=0)
            pr = probs[h]
            for p in range(2):
                both = jnp.concatenate([pr[2 * p * WINDOW:(2 * p + 1) * WINDOW], pr[(2 * p + 1) * WINDOW:(2 * p + 2) * WINDOW]], axis=1)
                o_ref[:, (2 * h + p) * PAIR:(2 * h + p + 1) * PAIR] = _dot(both, vs).astype(ACT)

    kv_c, kv_p = pl.BlockSpec((WINDOW, kd), cur), pl.BlockSpec((WINDOW, kd), prev)
    return _call(
        body, "attn_fwd", (t // WINDOW,),
        [pl.BlockSpec(memory_space=pltpu.SMEM), pl.BlockSpec((WINDOW, d), cur), kv_c, kv_p, kv_c, kv_p],
        [pl.BlockSpec((WINDOW, d), cur)], [_sds((t, d), ACT)],
        (sinks, q, kd_, kd_, vd, vd), ("parallel",), jobs=jobs)


def _attn_bwd(q, kd_, vd, do, sinks, cos, sin, jobs=()):
    t, d = q.shape
    kd = kd_.shape[1]
    nb = t // WINDOW
    n_out = d + kd
    assert (kd // 2) % PAIR == 0
    cur = lambda n: (jnp.minimum(n, nb - 1), 0)
    prev = lambda n: (jnp.maximum(jnp.minimum(n, nb - 1) - 1, 0), 0)
    late = lambda n: (jnp.maximum(n - 1, 0), 0)

    def body(sink_ref, q_ref, kc_ref, kp_ref, vc_ref, vp_ref, do_ref, cosc_ref, sinc_ref, cosp_ref, sinp_ref,
             out_ref, db_ref, dsink_ref, dq_keep, dk_keep, dv_keep):
        n = pl.program_id(0)

        @pl.when(n == 0)
        def _():
            for ref in (db_ref, dsink_ref, dq_keep, dk_keep, dv_keep):
                ref[...] = jnp.zeros_like(ref)

        valid = _band_mask(n > 0) & (n < nb)
        halves = _lane_halves()
        first = _first_half(WINDOW)
        slot = lax.broadcasted_iota(jnp.int32, dsink_ref.shape, 1)
        top = lax.broadcasted_iota(jnp.int32, dsink_ref.shape, 0) == 0
        dsink = jnp.zeros(dsink_ref.shape, F32)

        def emit(cols, done):
            out_ref[:, cols] = done.astype(ACT)
            db_ref[:, cols] += jnp.sum(done.astype(F32), axis=0, keepdims=True)

        heads = range(kd // PAIR)
        hs = [slice(h * PAIR, (h + 1) * PAIR) for h in heads]
        k2 = [jnp.concatenate([kp_ref[:, hs[h]], kc_ref[:, hs[h]]], axis=0) for h in heads]
        v2 = [jnp.concatenate([vp_ref[:, hs[h]], vc_ref[:, hs[h]]], axis=0) for h in heads]
        qs = [_stack_heads(q_ref, h, halves) for h in heads]
        dos = [_stack_heads(do_ref, h, halves) for h in heads]
        scores = [_dot_nt(qs[h], k2[h]) for h in heads]
        dps = [_dot_nt(dos[h], v2[h]) for h in heads]
        prs, dss = [], []
        for h in heads:
            pr, psink = _probs(scores[h], valid, _sink_column(sink_ref, h))
            delta = jnp.sum(pr * dps[h], axis=-1, keepdims=True)
            dss.append((pr * (dps[h] - delta)).astype(ACT))
            prs.append(pr.astype(ACT))
            leak = psink * delta
            for j in range(GQA_GROUP):
                share = jnp.sum(leak[j * WINDOW:(j + 1) * WINDOW], axis=0, keepdims=True)
                dsink = dsink - jnp.where(top & (slot == GQA_GROUP * h + j), share, 0.0)
        dqs = [_dot(dss[h], k2[h]) for h in heads]
        dk2 = [_dot_tn(dss[h], qs[h]) for h in heads]
        dv2 = [_dot_tn(prs[h], dos[h]) for h in heads]
        for h in heads:
            for p in range(2):
                ps = slice((2 * h + p) * PAIR, (2 * h + p + 1) * PAIR)
                dqp = jnp.where(halves[0], dqs[h][2 * p * WINDOW:(2 * p + 1) * WINDOW], dqs[h][(2 * p + 1) * WINDOW:(2 * p + 2) * WINDOW])
                emit(ps, dq_keep[:, ps])
                dq_keep[:, ps] = (_rope_t(dqp, cosc_ref[...], sinc_ref[...], first) * SCALE).astype(ACT)
        dks, dvs = [], []
        for h in heads:
            dks.append(dk_keep[:, hs[h]] + _rope_t(dk2[h][:WINDOW], cosp_ref[...], sinp_ref[...], first))
            dk_keep[:, hs[h]] = _rope_t(dk2[h][WINDOW:], cosc_ref[...], sinc_ref[...], first)
            dvs.append(dv_keep[:, hs[h]] + dv2[h][:WINDOW])
            dv_keep[:, hs[h]] = dv2[h][WINDOW:]
        both = lambda v: v + pltpu.roll(v, HEAD_DIM, 1)
        for h in range(0, len(heads), 2):
            at = h // 2 * PAIR
            emit(slice(d + at, d + at + PAIR), jnp.where(halves[0], both(dks[h]), both(dks[h + 1])))
            emit(slice(d + kd // 2 + at, d + kd // 2 + at + PAIR), jnp.where(halves[0], both(dvs[h]), both(dvs[h + 1])))
        dsink_ref[...] += dsink

    kv_c, kv_p = pl.BlockSpec((WINDOW, kd), cur), pl.BlockSpec((WINDOW, kd), prev)
    tab_c, tab_p = pl.BlockSpec((WINDOW, PAIR), cur), pl.BlockSpec((WINDOW, PAIR), prev)
    return _call(
        body, "attn_bwd", (nb + 1,),
        [pl.BlockSpec(memory_space=pltpu.SMEM), pl.BlockSpec((WINDOW, d), cur), kv_c, kv_p, kv_c, kv_p,
         pl.BlockSpec((WINDOW, d), cur), tab_c, tab_c, tab_p, tab_p],
        [pl.BlockSpec((WINDOW, n_out), late), _full((1, n_out)), _full((8, LANES))],
        [_sds((t, n_out), ACT), _sds((1, n_out), F32), _sds((8, LANES), F32)],
        (sinks, q, kd_, kd_, vd, vd, do, cos, sin, cos, sin), ("arbitrary",),
        scratch=[pltpu.VMEM((WINDOW, d), ACT), pltpu.VMEM((WINDOW, kd), F32), pltpu.VMEM((WINDOW, kd), F32)], jobs=jobs)


def _proj_res(a, w, b, g, x, name, jobs=()):
    t = a.shape[0]
    k, d = w.shape
    tm = _row_tile(t)
    has_bias = b is not None

    def body(*refs):
        a_ref, w_ref = refs[:2]
        b_ref = refs[2] if has_bias else None
        g_ref, x_ref, m_ref, xo_ref = refs[2 + has_bias:]
        m = _dot(a_ref[...], w_ref[...])
        if has_bias:
            m = m + b_ref[...]
        m_ref[...] = m.astype(ACT)
        xo_ref[...] = x_ref[...] + (m * _rstd(m)) * g_ref[...]

    ins = [a, w] + ([b] if has_bias else []) + [g, x]
    specs = [_rows(tm, k), _full((k, d))] + ([_full((1, d))] if has_bias else []) + [_full((1, d)), _rows(tm, d)]
    return _call(body, name, (t // tm,), specs, [_rows(tm, d), _rows(tm, d)], [_sds((t, d), ACT), _sds((t, d), F32)], ins,
                 ("parallel",), jobs=jobs)


def _post_bwd(dres, m, g, w, a, mode, name, jobs=()):
    t, d = dres.shape
    k = w.shape[0]
    tm = _row_tile(t)
    kc = _pick(k, (1408, 1024, 512))
    steps = t // tm

    def body(*refs):
        d_ref, m_ref, g_ref, w_ref, a_ref = refs[:5]
        da_ref, dw_ref, dg_ref = refs[5:8]
        db_ref, acc_ref = (refs[8] if mode == "attn" else None), refs[-1]
        i = pl.program_id(0)

        @pl.when(i == 0)
        def _():
            for ref in (dg_ref, acc_ref) + ((db_ref,) if mode == "attn" else ()):
                ref[...] = jnp.zeros_like(ref)

        dv, mv = d_ref[...], m_ref[...].astype(F32)
        r = _rstd(mv)
        mh = mv * r
        dg_ref[...] += jnp.sum(dv * mh, axis=0, keepdims=True)
        dm = _rms_bwd(mh, r, g_ref[...], dv)
        dmb = dm.astype(ACT)
        if mode == "attn":
            db_ref[...] += jnp.sum(dm, axis=0, keepdims=True)
        for c in range(0, k, kc):
            da = _dot_nt(dmb, w_ref[c:c + kc, :])
            da_ref[:, c:c + kc] = da.astype(ACT)
        acc_ref[...] += _dot_tn(a_ref[...], dmb)

        @pl.when(i == steps - 1)
        def _():
            dw_ref[...] = acc_ref[...].astype(ACT)

    ins = [dres, m, g, w, a]
    specs = [_rows(tm, d), _rows(tm, d), _full((1, d)), _full((k, d)), _rows(tm, k)]
    out_specs = [_rows(tm, k), _full((k, d)), _full((1, d))]
    out_shape = [_sds((t, k), ACT), _sds((k, d), ACT), _sds((1, d), F32)]
    if mode == "attn":
        out_specs.append(_full((1, d)))
        out_shape.append(_sds((1, d), F32))
    return _call(body, name, (steps,), specs, out_specs, out_shape, ins, ("arbitrary",), scratch=[pltpu.VMEM((k, d), F32)], jobs=jobs)


def _pre_bwd(dy, w, x, g, dres, name, h=None, jobs=()):
    t, d = x.shape
    tm = _row_tile(t)
    steps = t // tm
    n = dy.shape[1]

    def body(*refs):
        dy_ref, w_ref, x_ref, g_ref, dres_ref = refs[:5]
        dx_ref, dg_ref = refs[-4:-2] if h is not None else refs[-2:]
        i = pl.program_id(0)

        @pl.when(i == 0)
        def _():
            dg_ref[...] = jnp.zeros_like(dg_ref)
            if h is not None:
                refs[-1][...] = jnp.zeros_like(refs[-1])

        if w.ndim == 3:
            c = w.shape[2]
            dh = jnp.zeros((tm, d), F32)
            for j in range(w.shape[0]):
                dh = dh + _dot_nt(dy_ref[:, j * c:(j + 1) * c], w_ref[j])
        else:
            dh = _dot(dy_ref[...], w_ref[...])
        xv = x_ref[...]
        r = _rstd(xv)
        xh = xv * r
        dg_ref[...] += jnp.sum(dh * xh, axis=0, keepdims=True)
        dx_ref[...] = dres_ref[...] + _rms_bwd(xh, r, g_ref[...], dh)
        if h is not None:
            h_ref, dw_ref, acc_ref = refs[5], refs[-2], refs[-1]
            acc_ref[...] += _dot_tn(h_ref[...], dy_ref[...])

            @pl.when(i == steps - 1)
            def _():
                for j in range(w.shape[0]):
                    dw_ref[j] = acc_ref[:, j * c:(j + 1) * c].astype(ACT)

    specs = [_rows(tm, n), _resident(w.shape), _rows(tm, d), _full((1, d)), _rows(tm, d)]
    out_specs, out_shape, scratch = [_rows(tm, d), _full((1, d))], [_sds((t, d), F32), _sds((1, d), F32)], []
    if h is not None:
        specs, out_specs, out_shape = specs + [_rows(tm, d)], out_specs + [_full(w.shape)], out_shape + [_sds(w.shape, ACT)]
        scratch = [pltpu.VMEM((d, n), F32)]
    return _call(body, name, (steps,), specs, out_specs, out_shape, (dy, w, x, g, dres) + ((h,) if h is not None else ()),
                 ("arbitrary",), scratch=scratch, jobs=jobs)


def _ffn_bwd(dres, f, g_post, w_dn, gu, w_gu, x, g_pre, name, target=None, jobs=()):
    t, d = x.shape
    n = w_dn.shape[0]
    tm = _row_tile(t, 256)
    whole = target is not None
    n_in = 6 if whole else 8

    def body(*refs):
        if whole:
            t_ref, gp_ref, wd_ref, wu_ref, x_ref, g_ref = refs[:n_in]
            df_ref, dgu_ref, dx_ref, dgp_ref, dg_ref, loss_ref, h_ref, a_ref, gu_ref = refs[n_in:]
        else:
            d_ref, f_ref, gp_ref, wd_ref, gu_ref, wu_ref, x_ref, g_ref = refs[:n_in]
            df_ref, dgu_ref, dx_ref, dgp_ref, dg_ref = refs[n_in:]

        @pl.when(pl.program_id(0) == 0)
        def _():
            for ref in (dgp_ref, dg_ref) + ((loss_ref,) if whole else ()):
                ref[...] = jnp.zeros_like(ref)

        xv = x_ref[...]
        rx = _rstd(xv)
        xh = xv * rx
        if whole:
            hb = (xh * g_ref[...]).astype(ACT)
            h_ref[...] = hb
            for c, size in _mxu_chunks(n):
                gate = _dot_nt(hb, wu_ref[c:c + size, :])
                up = _dot_nt(hb, wu_ref[n + c:n + c + size, :])
                gu_ref[:, c:c + size] = gate.astype(ACT)
                gu_ref[:, n + c:n + c + size] = up.astype(ACT)
                a_ref[:, c:c + size] = (gate * _sigmoid(gate) * up).astype(ACT)
            fv = _dot(a_ref[...], wd_ref[...])
            r = _rstd(fv)
            fh = fv * r
            err = xv + fh * gp_ref[...] - t_ref[...]
            dv = err * (1.0 / d)
            loss_ref[...] += 0.5 * jnp.sum(jnp.mean(err * err, axis=-1, keepdims=True), axis=0, keepdims=True)
        else:
            dv, fv = d_ref[...], f_ref[...].astype(F32)
            r = _rstd(fv)
            fh = fv * r
        dgp_ref[...] += jnp.sum(dv * fh, axis=0, keepdims=True)
        dfb = _rms_bwd(fh, r, gp_ref[...], dv).astype(ACT)
        df_ref[...] = dfb
        for c, size in _mxu_chunks(n):
            da = _dot_nt(dfb, wd_ref[c:c + size, :]).astype(ACT)
            gate, up = gu_ref[:, c:c + size], gu_ref[:, n + c:n + c + size]
            sg = _sigmoid(gate.astype(F32)).astype(ACT)
            gs = gate * sg
            dgu_ref[:, c:c + size] = da * up * (sg + gs - gs * sg)
            dgu_ref[:, n + c:n + c + size] = da * gs
        dh = _dot(dgu_ref[...], wu_ref[...])
        dg_ref[...] += jnp.sum(dh * xh, axis=0, keepdims=True)
        dx_ref[...] = dv + _rms_bwd(xh, rx, g_ref[...], dh)

    w_specs = [_resident(w_dn.shape), _resident(w_gu.shape)]
    out_specs = [_rows(tm, d), _rows(tm, 2 * n), _rows(tm, d), _full((1, d)), _full((1, d))]
    out_shape = [_sds((t, d), ACT), _sds((t, 2 * n), ACT), _sds((t, d), F32), _sds((1, d), F32), _sds((1, d), F32)]
    if whole:
        return _call(
            body, name, (t // tm,),
            [_rows(tm, d), _full((1, d))] + w_specs + [_rows(tm, d), _full((1, d))],
            out_specs + [_full((8, LANES)), _rows(tm, d), _rows(tm, n)],
            out_shape + [_sds((8, LANES), F32), _sds((t, d), ACT), _sds((t, n), ACT)],
            (target, g_post, w_dn, w_gu, x, g_pre), ("arbitrary",), scratch=[pltpu.VMEM((tm, 2 * n), ACT)], jobs=jobs)
    return _call(
        body, name, (t // tm,),
        [_rows(tm, d), _rows(tm, d), _full((1, d)), w_specs[0], _rows(tm, 2 * n), w_specs[1], _rows(tm, d), _full((1, d))],
        out_specs, out_shape, (dres, f, g_post, w_dn, gu, w_gu, x, g_pre), ("arbitrary",), jobs=jobs)


def _wgrad(a, b, name, out_dtype=F32, transposed=False, jobs=()):
    t = a.shape[0]
    tt = _pick(t, (1024,))
    steps = t // tt
    tk = _pick(a.shape[1], (1024, 1408))
    k, n = a.shape[1], b.shape[1]
    tn = _pick(n, (1024, 1408))
    if transposed:
        out_spec, out_shape, acc_shape = pl.BlockSpec((tn, tk), lambda i, j, s: (j, i)), _sds((n, k), out_dtype), (tn, tk)
    else:
        out_spec, out_shape, acc_shape = pl.BlockSpec((tk, tn), lambda i, j, s: (i, j)), _sds((k, n), out_dtype), (tk, tn)

    def body(a_ref, b_ref, o_ref, acc_ref):
        s = pl.program_id(2)

        @pl.when(s == 0)
        def _():
            acc_ref[...] = jnp.zeros_like(acc_ref)

        acc_ref[...] += _dot_tn(b_ref[...], a_ref[...]) if transposed else _dot_tn(a_ref[...], b_ref[...])

        @pl.when(s == steps - 1)
        def _():
            o_ref[...] = acc_ref[...].astype(out_dtype)

    res, job_res = _call(
        body, name, (k // tk, n // tn, steps),
        [pl.BlockSpec((tt, tk), lambda i, j, s: (s, i)), pl.BlockSpec((tt, tn), lambda i, j, s: (s, j))], [out_spec], [out_shape],
        (a, b), ("parallel", "parallel", "arbitrary"), scratch=[pltpu.VMEM(acc_shape, F32)], jobs=jobs)
    return res[0], job_res


def _ffn_fwd(x, g_pre, w_gu, w_dn, g_post, jobs=()):
    t, d = x.shape
    n = w_dn.shape[0]
    tm = _row_tile(t)

    def body(x_ref, g_ref, wu_ref, wd_ref, gp_ref, h_ref, gu_ref, a_ref, f_ref, xo_ref):
        xv = x_ref[...]
        hb = (xv * _rstd(xv) * g_ref[...]).astype(ACT)
        h_ref[...] = hb
        for c, size in _mxu_chunks(n):
            gate = _dot_nt(hb, wu_ref[c:c + size, :])
            up = _dot_nt(hb, wu_ref[n + c:n + c + size, :])
            gu_ref[:, c:c + size] = gate.astype(ACT)
            gu_ref[:, n + c:n + c + size] = up.astype(ACT)
            a_ref[:, c:c + size] = (gate * _sigmoid(gate) * up).astype(ACT)
        f = _dot(a_ref[...], wd_ref[...])
        f_ref[...] = f.astype(ACT)
        xo_ref[...] = xv + (f * _rstd(f)) * gp_ref[...]

    return _call(
        body, "ffn_fwd", (t // tm,),
        [_rows(tm, d), _full((1, d)), _resident(w_gu.shape), _resident(w_dn.shape), _full((1, d))],
        [_rows(tm, d), _rows(tm, 2 * n), _rows(tm, n), _rows(tm, d), _rows(tm, d)],
        [_sds((t, d), ACT), _sds((t, 2 * n), ACT), _sds((t, n), ACT), _sds((t, d), ACT), _sds((t, d), F32)],
        (x, g_pre, w_gu, w_dn, g_post), ("parallel",), jobs=jobs)


def _causal(ws):
    row = lax.broadcasted_iota(jnp.int32, ws.shape, 0)
    col = lax.broadcasted_iota(jnp.int32, ws.shape, 1)
    return jnp.where(col <= row, ws, 0.0)


def _sgu_ln(v, lg, lb):
    mu = jnp.mean(v, axis=-1, keepdims=True)
    vc = v - mu
    rs = lax.rsqrt(jnp.mean(vc * vc, axis=-1, keepdims=True) + EPS)
    vh = vc * rs
    return vh, rs, vh * lg + lb


def _sgu_fwd(x, g, w, lg, lb, ws, bs_t, jobs=()):
    t, d = x.shape
    nb, _, c = w.shape
    n = nb * c
    groups = d // WINDOW
    tm = _row_tile(t)

    def body(x_ref, g_ref, w_ref, lg_ref, lb_ref, ws_ref, bs_ref, h_ref, z_ref, y_ref, zf_ref):
        xv = x_ref[...]
        hb = (xv * _rstd(xv) * g_ref[...]).astype(ACT)
        h_ref[...] = hb
        for j in range(nb):
            zf_ref[:, j * c:(j + 1) * c] = _dot(hb, w_ref[j])
        z_ref[...] = zf_ref[...].astype(ACT)
        gs = [slice(gi * WINDOW, (gi + 1) * WINDOW) for gi in range(groups)]
        wsg = [_causal(ws_ref[gi]).astype(ACT) for gi in range(groups)]
        for r in range(0, tm, WINDOW):
            u = _gelu(zf_ref[r:r + WINDOW, :d])
            _, _, vn = _sgu_ln(_gelu(zf_ref[r:r + WINDOW, d:]), lg_ref[...], lb_ref[...])
            mixed = [_dot(wsg[gi], vn[:, gs[gi]].astype(ACT)) for gi in range(groups)]
            for gi in range(groups):
                y_ref[r:r + WINDOW, gs[gi]] = (u[:, gs[gi]] * (mixed[gi] + bs_ref[:, gi:gi + 1])).astype(ACT)

    vec = _full((1, d))
    return _call(
        body, "sgu_fwd", (t // tm,),
        [_rows(tm, d), vec, _full((nb, d, c)), vec, vec, _full((groups, WINDOW, WINDOW)), _full((WINDOW, groups))],
        [_rows(tm, d), _rows(tm, n), _rows(tm, d)], [_sds((t, d), ACT), _sds((t, n), ACT), _sds((t, d), ACT)],
        (x, g, w, lg, lb, ws, bs_t), ("parallel",), scratch=[pltpu.VMEM((tm, n), F32)], jobs=jobs)


def _sgu_mix_bwd(z, dy, lg, lb, ws, bs_t, jobs=()):
    t, n = z.shape
    d = n // 2
    groups = d // WINDOW
    tm = _row_tile(t)

    def body(z_ref, dy_ref, lg_ref, lb_ref, ws_ref, bs_ref, dz_ref, dlg_ref, dlb_ref, dws_ref, dbs_ref):
        @pl.when(pl.program_id(0) == 0)
        def _():
            for ref in (dlg_ref, dlb_ref, dws_ref, dbs_ref):
                ref[...] = jnp.zeros_like(ref)

        lane = lax.broadcasted_iota(jnp.int32, (WINDOW, groups), 1)
        gs = [slice(gi * WINDOW, (gi + 1) * WINDOW) for gi in range(groups)]
        wsg = [_causal(ws_ref[gi]).astype(ACT) for gi in range(groups)]
        for c in range(0, tm, WINDOW):
            rows = slice(c, c + WINDOW)
            zu, zv = z_ref[rows, :d].astype(F32), z_ref[rows, d:].astype(F32)
            u = _gelu(zu)
            vh, rs, vn = _sgu_ln(_gelu(zv), lg_ref[...], lb_ref[...])
            dyv = dy_ref[rows, :].astype(F32)
            vng = [vn[:, gs[gi]].astype(ACT) for gi in range(groups)]
            dmix = dyv * u
            dmb = [dmix[:, gs[gi]].astype(ACT) for gi in range(groups)]
            mixed = [_dot(wsg[gi], vng[gi]) for gi in range(groups)]
            dvn_parts = [_dot_tn(wsg[gi], dmb[gi]) for gi in range(groups)]
            dws_parts = [_dot_nt(dmb[gi], vng[gi]) for gi in range(groups)]
            for gi in range(groups):
                dz_ref[rows, gs[gi]] = (dyv[:, gs[gi]] * (mixed[gi] + bs_ref[:, gi:gi + 1]) * _gelu_grad(zu[:, gs[gi]])).astype(ACT)
                dws_ref[:, gs[gi]] += _causal(dws_parts[gi])
                dbs_ref[...] += jnp.where(lane == gi, jnp.sum(dmix[:, gs[gi]], axis=-1, keepdims=True), 0.0)
            dvn = jnp.concatenate(dvn_parts, axis=-1)
            dlg_ref[...] += jnp.sum(dvn * vh, axis=0, keepdims=True)
            dlb_ref[...] += jnp.sum(dvn, axis=0, keepdims=True)
            dvh = dvn * lg_ref[...]
            dv = rs * (dvh - jnp.mean(dvh, axis=-1, keepdims=True) - vh * jnp.mean(dvh * vh, axis=-1, keepdims=True))
            dz_ref[rows, d:] = (dv * _gelu_grad(zv)).astype(ACT)

    vec = _full((1, d))
    return _call(
        body, "sgu_mix_bwd", (t // tm,),
        [_rows(tm, n), _rows(tm, d), vec, vec, _full((groups, WINDOW, WINDOW)), _full((WINDOW, groups))],
        [_rows(tm, n), vec, vec, _full((WINDOW, d)), _full((WINDOW, groups))],
        [_sds((t, n), ACT), _sds((1, d), F32), _sds((1, d), F32), _sds((WINDOW, d), F32), _sds((WINDOW, groups), F32)],
        (z, dy, lg, lb, ws, bs_t), ("arbitrary",), jobs=jobs)


AG_COPIES = 8


def _prepare(sources, dtypes, n_gather, name, ahead=None):
    n = len(sources)
    arrays, where = [], []
    for parts, layer in sources:
        found = []
        for part in parts:
            known = [i for i, v in enumerate(arrays) if v is part]
            if not known:
                arrays.append(part)
            found.append(known[0] if known else len(arrays) - 1)
        where.append((found, layer))
    shapes = [(sum(p.shape[1] for p in parts), parts[0].shape[2]) for parts, _ in sources]

    def body(*refs):
        ins, refs = refs[:len(arrays)], refs[len(arrays):]
        stage, outs = refs[:n], refs[n:n + n_gather]
        send, recv, local_sem = refs[n + n_gather + (ahead is not None):][:3]
        x, y, c = _place()
        me, sibling, beside_x, beside_y, far = (x, y, c), (x, y, 1 - c), (1 - x, y, c), (x, 1 - y, c), (1 - x, 1 - y, c)
        slot = lambda dev: 4 * dev[0] + 2 * dev[1] + dev[2]
        other = lambda dev: (dev[0], dev[1], 1 - c)
        whole = lambda a: (0, shapes[a][0])
        cuts = [rows // 2 if rows % 32 == 0 else rows for rows, _ in shapes]
        first = lambda a: (0, cuts[a])
        rest = lambda a: (cuts[a], shapes[a][0] - cuts[a])

        def copy(a, k, block, rows, to, src=None):
            dst = outs[a].at[block, pl.ds(*rows)]
            return pltpu.make_async_remote_copy(
                src_ref=dst if src is None else src, dst_ref=dst, send_sem=send.at[a * AG_COPIES + k],
                recv_sem=recv.at[a * AG_COPIES + k], device_id=to, device_id_type=MESH)

        def cast(a):
            found, layer = where[a]
            at = 0
            for i in found:
                rows = arrays[i].shape[1]
                stage[a][at:at + rows, :] = ins[i][layer].astype(dtypes[a])
                at += rows

        for a in range(n_gather):
            cast(a)
        started, ahead_sends = [], []
        for a in range(n_gather):
            own = pltpu.make_async_copy(stage[a], outs[a].at[slot(me)], local_sem.at[a])
            own.start()
            started.append(own)
        sends = []
        for a in range(n_gather):
            sends += [copy(a, k, slot(me), whole(a), to, src=stage[a]) for k, to in enumerate((sibling, beside_x, beside_y))]
        for cp in sends:
            cp.start()
        for a in range(n_gather, n):
            cast(a)
        if ahead is not None:
            block, pieces = ahead
            out, (send2, recv2, local2) = refs[n + n_gather], refs[-3:]
            for i, piece in enumerate(pieces):
                src, dst = stage[block].at[pl.ds(*piece)], out.at[slot(me), pl.ds(*piece)]
                own = pltpu.make_async_copy(src, dst, local2.at[i])
                own.start()
                started.append(own)
                for k, to in enumerate((sibling, beside_x, beside_y)):
                    cp = pltpu.make_async_remote_copy(src_ref=src, dst_ref=dst, send_sem=send2.at[3 * i + k],
                                                      recv_sem=recv2.at[3 * i + k], device_id=to, device_id_type=MESH)
                    cp.start()
                    ahead_sends.append(cp)
        for a in range(n_gather):
            copy(a, 1, slot(beside_x), whole(a), me).wait_recv()
            copy(a, 2, slot(beside_y), whole(a), me).wait_recv()
            passed = [copy(a, 3, slot(beside_x), first(a), beside_y), copy(a, 5, slot(beside_x), whole(a), sibling),
                      copy(a, 6, slot(beside_y), whole(a), sibling)]
            if rest(a)[1]:
                passed.append(copy(a, 4, slot(beside_y), rest(a), beside_x))
            for cp in passed:
                cp.start()
            sends += passed
        for a in range(n_gather):
            copy(a, 3, slot(far), first(a), me).wait_recv()
            if rest(a)[1]:
                copy(a, 4, slot(far), rest(a), me).wait_recv()
            passed = copy(a, 7, slot(far), whole(a), sibling)
            passed.start()
            sends.append(passed)
        for a in range(n_gather):
            for k, source in ((0, me), (5, beside_x), (6, beside_y), (7, far)):
                copy(a, k, slot(other(source)), whole(a), me).wait_recv()
        for cp in sends:
            cp.wait_send()
        for cp in ahead_sends:
            cp.wait()
        for own in started:
            own.wait()

    gathered = list(range(n_gather)) + ([ahead[0]] if ahead is not None else [])
    sems = [pltpu.SemaphoreType.DMA((n_gather * AG_COPIES,)), pltpu.SemaphoreType.DMA((n_gather * AG_COPIES,)),
            pltpu.SemaphoreType.DMA((n_gather,))]
    if ahead is not None:
        sems += [pltpu.SemaphoreType.DMA((3 * len(ahead[1]),)), pltpu.SemaphoreType.DMA((3 * len(ahead[1]),)),
                 pltpu.SemaphoreType.DMA((len(ahead[1]),))]
    res = pl.pallas_call(
        body, name=name,
        in_specs=[IN_VMEM] * len(arrays), out_specs=[IN_VMEM] * n + [ANY] * len(gathered),
        out_shape=[_sds(s, dt) for s, dt in zip(shapes, dtypes)] + [_sds((N_DEV,) + shapes[a], dtypes[a]) for a in gathered],
        scratch_shapes=sems,
        compiler_params=pltpu.CompilerParams(vmem_limit_bytes=VMEM_LIMIT_BYTES),
    )(*arrays)
    return list(res[:n]), list(res[n:])


def _pair_sum(g, r, core, name):
    _, _, rows, cols = g.shape
    tr = _pick(rows, (512, 256, 128))

    def body(core_ref, g_ref, r_ref, p_ref):
        del core_ref
        p_ref[...] = (g_ref[...].astype(F32) + r_ref[...].astype(F32)).astype(p_ref.dtype)

    return pl.pallas_call(
        body, name=name,
        grid_spec=pltpu.PrefetchScalarGridSpec(
            num_scalar_prefetch=1, grid=(4, rows // tr),
            in_specs=[pl.BlockSpec((None, None, tr, cols), lambda q, i, core_ref: (q, core_ref[0], i, 0)),
                      pl.BlockSpec((None, tr, cols), lambda q, i, core_ref: (q, i, 0))],
            out_specs=pl.BlockSpec((None, tr, cols), lambda q, i, core_ref: (q, i, 0))),
        out_shape=_sds((4, rows, cols), g.dtype),
        compiler_params=_params("parallel", "parallel"),
    )(core, g, r)


def _adam(w, g, m, v):
    m2 = ADAM_B1 * m + (1.0 - ADAM_B1) * g
    v2 = ADAM_B2 * v + (1.0 - ADAM_B2) * (g * g)
    m_hat = m2 / (1.0 - ADAM_B1 ** ADAM_STEP)
    v_hat = v2 / (1.0 - ADAM_B2 ** ADAM_STEP)
    return -ADAM_LR * (m_hat / (jnp.sqrt(v_hat) + ADAM_EPS) + ADAM_WD * w), m2, v2


def _shard_update(own, index, received, w, m, v, layer, so_far, name):
    _, rows, cols = w.shape
    n_recv = received.shape[0]
    tr = _shard_tile(rows)

    def body(index_ref, p_ref, q_ref, w_ref, m_ref, v_ref, *rest):
        del index_ref
        g_out, d_out, m_out, v_out = rest[-4:]
        g = p_ref[...].astype(F32)
        for s in range(n_recv):
            g = g + q_ref[s].astype(F32)
        g_out[...] = g
        d_out[...], m_out[...], v_out[...] = _adam(w_ref[...], g, m_ref[...], v_ref[...])

    tile = pl.BlockSpec((None, tr, cols), lambda i, index_ref: (layer, i, 0))
    kept = list(so_far) if so_far is not None else []
    return pl.pallas_call(
        body, name=name,
        grid_spec=pltpu.PrefetchScalarGridSpec(
            num_scalar_prefetch=1, grid=(rows // tr,),
            in_specs=[pl.BlockSpec((None, tr, cols), lambda i, index_ref: (index_ref[0], i, 0)),
                      pl.BlockSpec((n_recv, tr, cols), lambda i, index_ref: (0, i, 0)), tile, tile, tile] + [ANY] * len(kept),
            out_specs=[tile] * 4),
        out_shape=[_sds(w.shape, F32)] * 4,
        input_output_aliases={6 + i: i for i in range(len(kept))},
        compiler_params=_params("parallel"),
    )(index, own, received, w, m, v, *kept)


def _natural_pieces(shape, r, c):
    if tuple(shape[-2:]) == (r, c) and all(n == 1 for n in shape[:-2]):
        return [((0,) * (len(shape) - 2), (0, c))]
    groups, width = shape[1], shape[3]
    assert len(shape) == 4 and shape[0] == 1 and shape[2] == r and groups * width == c, (shape, r, c)
    return [((0, g), (g * width, width)) for g in range(groups)]


def _replicated_update(shares, where, params, name):
    flat = [a for p in params if p is not None for a in p]

    def body(s_ref, *refs):
        ins, outs = refs[:len(flat)], refs[len(flat):]
        total = s_ref[0]
        for dev in range(1, N_DEV):
            total = total + s_ref[dev]
        i_at = o_at = 0
        for ranges, p in zip(where, params):
            chunks = [total[r0:r0 + r, :c] for r0, r, c in ranges]
            g2d = chunks[0] if len(chunks) == 1 else jnp.concatenate(chunks, axis=1)
            if p is None:
                outs[o_at][...] = g2d
                o_at += 1
                continue
            w_ref, m_ref, v_ref = ins[i_at:i_at + 3]
            for idx, (c0, cols) in _natural_pieces(p[0].shape, *g2d.shape):
                at = idx + (slice(None), slice(None))
                g = g2d[:, c0:c0 + cols]
                outs[o_at][at] = g
                outs[o_at + 1][at], outs[o_at + 2][at], outs[o_at + 3][at] = _adam(w_ref[at], g, m_ref[at], v_ref[at])
            i_at, o_at = i_at + 3, o_at + 4

    out_shape = []
    for ranges, p in zip(where, params):
        out_shape += [_sds((ranges[0][1], sum(c for _, _, c in ranges)), F32)] if p is None else [_sds(p[0].shape, F32)] * 4
    res = pl.pallas_call(
        body, name=name, out_shape=out_shape, compiler_params=pltpu.CompilerParams(vmem_limit_bytes=VMEM_LIMIT_BYTES),
    )(shares, *flat)
    out, at = [], 0
    for p in params:
        out.append(list(res[at:at + (1 if p is None else 4)]))
        at += 1 if p is None else 4
    return out


def _rope_tables(t):
    half = HEAD_DIM // 2
    inv_freq = np.float32(ROPE_THETA) ** (-(np.arange(half, dtype=np.float32) * np.float32(2.0)) / np.float32(HEAD_DIM))
    ang = np.arange(t, dtype=np.float32)[:, None] * inv_freq[None, :].astype(np.float32)
    cos, sin = np.cos(ang).astype(np.float32), np.sin(ang).astype(np.float32)
    return (jnp.asarray(np.tile(np.concatenate([cos, cos], axis=1), (1, 2))),
            jnp.asarray(np.tile(np.concatenate([-sin, sin], axis=1), (1, 2))))


def _rows_full(gathered):
    return gathered.reshape(-1, gathered.shape[-1])


def _rows_blocked(full):
    return full.reshape(N_DEV, full.shape[0] // N_DEV, full.shape[1]).astype(ACT)


def _pack_rows(parts, width):
    rows, where, at = [], [], 0
    for v in parts:
        r, c = v.shape
        n_rows = -(-r // 8) * 8
        chunks = []
        for c0 in range(0, c, width):
            chunk = v[:, c0:c0 + width].astype(F32)
            rows.append(jnp.pad(chunk, ((0, n_rows - r), (0, width - chunk.shape[1]))))
            chunks.append((at, r, chunk.shape[1]))
            at += n_rows
        where.append(chunks)
    return jnp.concatenate(rows, axis=0), where


def _halves(block):
    half = block.shape[0] // 2
    return (0, half), (half, half)


def _whole(block):
    return (0, block.shape[0])


EARLY = ("attn_w_qkv", "sgu_ln", "attn_w_o")
LATE = ("sgu_w_in", "sgu_w_out", "gu0", "gu1", "dn0", "dn1")
SWAPPED = ("attn_w_qkv", "ffn_w_gate_up")
BEFORE_ATTN = ("norm_mix_post", "norm_ffn_pre", "norm_ffn_post", "attn_b_o", "sgu_w_spatial", "sgu_b_spatial")
AFTER_ATTN = ("attn_b_qkv", "attn_sinks")
LAST = ("norm_mix_pre",)
WEIGHTS = ("norm_mix_pre", "norm_mix_post", "norm_ffn_pre", "norm_ffn_post", "attn_w_qkv", "attn_b_qkv", "attn_sinks", "attn_w_o",
           "attn_b_o", "sgu_w_in", "sgu_ln_g", "sgu_ln_b", "sgu_w_spatial", "sgu_b_spatial", "sgu_w_out", "ffn_w_gate_up", "ffn_w_down")


LAYER_OF = {"gu0": ("ffn_w_gate_up", 0), "gu1": ("ffn_w_gate_up", 1), "dn0": ("ffn_w_down", 0), "dn1": ("ffn_w_down", 1)}


def _source(tree, name):
    if name == "sgu_ln":
        return [tree["sgu_ln_g"][None], tree["sgu_ln_b"][None]], 0
    leaf, layer = LAYER_OF.get(name, (name, 0))
    return [tree[leaf]], layer


def kernel(x, norm_mix_pre, norm_mix_post, norm_ffn_pre, norm_ffn_post, attn_w_qkv, attn_b_qkv, attn_sinks, attn_w_o, attn_b_o, sgu_w_in, sgu_ln_g, sgu_ln_b, sgu_w_spatial, sgu_b_spatial, sgu_w_out, ffn_w_gate_up, ffn_w_down, loss_target, m_norm_mix_pre, m_norm_mix_post, m_norm_ffn_pre, m_norm_ffn_post, m_attn_w_qkv, m_attn_b_qkv, m_attn_sinks, m_attn_w_o, m_attn_b_o, m_sgu_w_in, m_sgu_ln_g, m_sgu_ln_b, m_sgu_w_spatial, m_sgu_b_spatial, m_sgu_w_out, m_ffn_w_gate_up, m_ffn_w_down, v_norm_mix_pre, v_norm_mix_post, v_norm_ffn_pre, v_norm_ffn_post, v_attn_w_qkv, v_attn_b_qkv, v_attn_sinks, v_attn_w_o, v_attn_b_o, v_sgu_w_in, v_sgu_ln_g, v_sgu_ln_b, v_sgu_w_spatial, v_sgu_b_spatial, v_sgu_w_out, v_ffn_w_gate_up, v_ffn_w_down):
    w = dict(norm_mix_pre=norm_mix_pre, norm_mix_post=norm_mix_post, norm_ffn_pre=norm_ffn_pre, norm_ffn_post=norm_ffn_post,
             attn_w_qkv=attn_w_qkv, attn_b_qkv=attn_b_qkv, attn_sinks=attn_sinks, attn_w_o=attn_w_o, attn_b_o=attn_b_o,
             sgu_w_in=sgu_w_in, sgu_ln_g=sgu_ln_g, sgu_ln_b=sgu_ln_b, sgu_w_spatial=sgu_w_spatial, sgu_b_spatial=sgu_b_spatial,
             sgu_w_out=sgu_w_out, ffn_w_gate_up=ffn_w_gate_up, ffn_w_down=ffn_w_down)
    mom = dict(norm_mix_pre=m_norm_mix_pre, norm_mix_post=m_norm_mix_post, norm_ffn_pre=m_norm_ffn_pre, norm_ffn_post=m_norm_ffn_post,
               attn_w_qkv=m_attn_w_qkv, attn_b_qkv=m_attn_b_qkv, attn_sinks=m_attn_sinks, attn_w_o=m_attn_w_o, attn_b_o=m_attn_b_o,
               sgu_w_in=m_sgu_w_in, sgu_ln_g=m_sgu_ln_g, sgu_ln_b=m_sgu_ln_b, sgu_w_spatial=m_sgu_w_spatial,
               sgu_b_spatial=m_sgu_b_spatial, sgu_w_out=m_sgu_w_out, ffn_w_gate_up=m_ffn_w_gate_up, ffn_w_down=m_ffn_w_down)
    var = dict(norm_mix_pre=v_norm_mix_pre, norm_mix_post=v_norm_mix_post, norm_ffn_pre=v_norm_ffn_pre, norm_ffn_post=v_norm_ffn_post,
               attn_w_qkv=v_attn_w_qkv, attn_b_qkv=v_attn_b_qkv, attn_sinks=v_attn_sinks, attn_w_o=v_attn_w_o, attn_b_o=v_attn_b_o,
               sgu_w_in=v_sgu_w_in, sgu_ln_g=v_sgu_ln_g, sgu_ln_b=v_sgu_ln_b, sgu_w_spatial=v_sgu_w_spatial,
               sgu_b_spatial=v_sgu_b_spatial, sgu_w_out=v_sgu_w_out, ffn_w_gate_up=v_ffn_w_gate_up, ffn_w_down=v_ffn_w_down)
    w, mom, var = [{**tree, **{n: jnp.swapaxes(tree[n], 1, 2) for n in SWAPPED}} for tree in (w, mom, var)]
    xs, target = x[0], loss_target[0]
    t, d = xs.shape
    row = lambda v: v.reshape(1, -1).astype(F32)
    core = lax.axis_index("c").astype(jnp.int32).reshape(1)
    chip = (2 * lax.axis_index("x") + lax.axis_index("y")).astype(jnp.int32).reshape(1)
    names = EARLY + LATE
    gu_rows = w["ffn_w_gate_up"].shape[1] // 4
    gu0_parts = [(i * gu_rows, gu_rows) for i in range(4)]
    staged, (*early, g_gu0) = _prepare([_source(w, n) for n in names], [F32 if n == "sgu_ln" else ACT for n in names], len(EARLY),
                                       "weights_prepare", ahead=(names.index("gu0"), gu0_parts[:2]))
    stage = dict(zip(names, staged))
    w_qkv = _rows_full(early[0])
    lg, lb = row(early[1][:, 0, :]), row(early[1][:, 1, :])
    w_o = _rows_full(early[2])
    b_qkv = row(attn_b_qkv)
    sinks = attn_sinks.reshape(1, -1).astype(F32)
    ws = sgu_w_spatial[0].astype(F32)
    bs_t = sgu_b_spatial[0].astype(F32).T
    cos, sin = _rope_tables(t)
    gather = lambda name, so_far, **pieces: _gather_job(stage[name], so_far, **pieces)
    a_half = lambda name: _halves(stage[name])[0]
    b_half = lambda name: _halves(stage[name])[1]
    whole = lambda name: _whole(stage[name])

    def pieces(name, n):
        rows = stage[name].shape[0] // n
        return [(i * rows, rows) for i in range(n)]

    ways = lambda parts: [(piece, i % 2) for i, piece in enumerate(parts)]
    relay = lambda name, so_far, **steps: _relay_gather_job(stage[name], so_far, **steps)
    gu1_parts = pieces("gu1", 4)
    dn0_parts, dn1_parts, in_parts, out_parts = pieces("dn0", 2), pieces("dn1", 2), pieces("sgu_w_in", 2), pieces("sgu_w_out", 2)
    (h0, q, kdup, vdup), ((g_gu0,),) = _attn_qkv_fwd(
        xs, row(norm_mix_pre[0]), w_qkv, b_qkv, cos, sin, jobs=[relay("gu0", g_gu0, sends=gu0_parts[2:])])
    (o,), ((g_gu0,), (g_dn0,)) = _attn_fwd(q, kdup, vdup, sinks, jobs=[
        relay("gu0", g_gu0, relays=ways(gu0_parts), forwards=gu0_parts), relay("dn0", None, sends=dn0_parts)])
    (m0, x1), ((g_gu0,), (g_dn0,), (g_in,)) = _proj_res(o, w_o, row(attn_b_o), row(norm_mix_post[0]), xs, "attn_out_fwd", jobs=[
        relay("gu0", g_gu0, far=gu0_parts), relay("dn0", g_dn0, relays=ways(dn0_parts), forwards=dn0_parts, late_far=dn0_parts),
        relay("sgu_w_in", None, sends=in_parts)])
    w_gu0, w_dn0 = _rows_full(g_gu0), _rows_full(g_dn0)
    (h1, gu0, a0, f0, x2), ((g_in,), (g_out,), (g_gu1,), (g_dn1,)) = _ffn_fwd(
        x1, row(norm_ffn_pre[0]), w_gu0, w_dn0, row(norm_ffn_post[0]), jobs=[
            relay("sgu_w_in", g_in, relays=ways(in_parts), forwards=in_parts, late_far=in_parts),
            relay("sgu_w_out", None, sends=out_parts), relay("gu1", None, sends=gu1_parts), relay("dn1", None, sends=dn1_parts)])
    w_in = g_in
    (h2, z, y), ((g_out,), (g_gu1,), (g_dn1,)) = _sgu_fwd(x2, row(norm_mix_pre[1]), w_in, lg, lb, ws, bs_t, jobs=[
        relay("sgu_w_out", g_out, relays=ways(out_parts), forwards=out_parts, late_far=out_parts),
        relay("gu1", g_gu1, relays=ways(gu1_parts), forwards=gu1_parts),
        relay("dn1", g_dn1, relays=ways(dn1_parts), forwards=dn1_parts)])
    w_out = _rows_full(g_out)
    (m1, x3), ((g_gu1,), (g_dn1,)) = _proj_res(y, w_out, None, row(norm_mix_post[1]), x2, "sgu_out_fwd", jobs=[
        relay("gu1", g_gu1, far=gu1_parts), relay("dn1", g_dn1, far=dn1_parts)])
    w_gu1, w_dn1 = _rows_full(g_gu1), _rows_full(g_dn1)

    to_sibling = lambda g: _scatter_job([g], 4, _to_sibling)
    pair = lambda g, r, name: _pair_sum(g.reshape((4, 2) + g.shape[1:]), r, core, "pair_sum_" + name)
    to_chips = lambda p, prev=None, piece=None: _scatter_job([p], 3, _to_owner_chip, prev=prev, piece=piece)
    out_g, out_d, out_m, out_v = {}, {}, {}, {}

    trees = [{**tree, "sgu_ln": jnp.stack([tree["sgu_ln_g"], tree["sgu_ln_b"]], axis=1)} for tree in (w, mom, var)]
    so_far = {}

    def update(name, own, index, received):
        leaf, layer = LAYER_OF.get(name, (name, 0))
        so_far[leaf] = _shard_update(own, index, received, *[tree[leaf] for tree in trees], layer, so_far.get(leaf), "update_" + name)
        for out, val in zip((out_g, out_d, out_m, out_v), so_far[leaf]):
            out[leaf] = val

    def finish(names, n_extra, shares, where, name):
        res = _replicated_update(shares, where, [(w[n], mom[n], var[n]) for n in names] + [None] * n_extra, name)
        for n, vals in zip(names, res):
            out_g[n], out_d[n], out_m[n], out_v[n] = vals
        return [vals[0] for vals in res[len(names):]]

    first_half = lambda p: _halves(p[0])[0]
    second_half = lambda p: _halves(p[0])[1]
    (df1, dgu1, dx3, dg_ffn_post1, dg_ffn_pre1, loss_tile, h3, a1), _ = _ffn_bwd(
        None, None, row(norm_ffn_post[1]), w_dn1, None, w_gu1, x3, row(norm_ffn_pre[1]), "ffn_last", target=target)
    b_gu1, _ = _wgrad(h3, dgu1, "ffn_up_wgrad1", ACT, transposed=True)
    b_gu1 = _rows_blocked(b_gu1)
    dw_dn1, _ = _wgrad(a1, df1, "ffn_down_wgrad1", ACT)
    b_dn1 = _rows_blocked(dw_dn1)
    (dy, dw_out, dg_mix_post1), ((r_gu1,), (r_dn1,)) = _post_bwd(
        dx3, m1, row(norm_mix_post[1]), w_out, y, "sgu", "sgu_out_bwd", jobs=[to_sibling(b_gu1), to_sibling(b_dn1)])
    p_gu1, p_dn1 = pair(b_gu1, r_gu1, "gu1"), pair(b_dn1, r_dn1, "dn1")
    b_out = _rows_blocked(dw_out)
    (dz, dlg, dlb, dws, dbs_t), ((q_gu1,), (r_out,)) = _sgu_mix_bwd(z, dy, lg, lb, ws, bs_t, jobs=[
        to_chips(p_gu1, piece=first_half(p_gu1)), to_sibling(b_out)])
    p_out = pair(b_out, r_out, "sgu_w_out")
    b_ln = jnp.stack([dlg.reshape(N_DEV, -1), dlb.reshape(N_DEV, -1)], axis=1)
    (dx2, dg_mix_pre1, b_in), _ = _pre_bwd(dz, w_in, x2, row(norm_mix_pre[1]), dx3, "sgu_in_bwd", h=h2)
    (df0, dgu0, dx1, dg_ffn_post0, dg_ffn_pre0), ((q_gu1,), (q_dn1,), (q_out,), (r_in,), (r_ln,)) = _ffn_bwd(
        dx2, f0, row(norm_ffn_post[0]), w_dn0, gu0, w_gu0, x1, row(norm_ffn_pre[0]), "ffn_bwd0",
        jobs=[to_chips(p_gu1, prev=[q_gu1], piece=second_half(p_gu1)), to_chips(p_dn1), to_chips(p_out), to_sibling(b_in),
              to_sibling(b_ln)])
    update("gu1", p_gu1, chip, q_gu1)
    update("dn1", p_dn1, chip, q_dn1)
    update("sgu_w_out", p_out, chip, q_out)
    p_in, p_ln = pair(b_in, r_in, "sgu_w_in"), pair(b_ln, r_ln, "sgu_ln")
    dw_dn0, _ = _wgrad(a0, df0, "ffn_down_wgrad0", ACT)
    b_dn0 = _rows_blocked(dw_dn0)
    b_gu0, ((q_in,), (q_ln,), (r_dn0,)) = _wgrad(h1, dgu0, "ffn_up_wgrad0", ACT, transposed=True, jobs=[
        to_chips(p_in), to_chips(p_ln), to_sibling(b_dn0)])
    update("sgu_w_in", p_in, chip, q_in)
    update("sgu_ln", p_ln, chip, q_ln)
    b_gu0 = _rows_blocked(b_gu0)
    p_dn0 = pair(b_dn0, r_dn0, "dn0")
    (do, dw_o, dg_mix_post0, db_o), ((r_gu0,), (q_dn0,)) = _post_bwd(
        dx1, m0, row(norm_mix_post[0]), w_o, o, "attn", "attn_out_bwd", jobs=[to_sibling(b_gu0), to_chips(p_dn0, piece=first_half(p_dn0))])
    p_gu0 = pair(b_gu0, r_gu0, "gu0")
    b_o = _rows_blocked(dw_o)
    grads = {
        "norm_mix_post": jnp.concatenate([dg_mix_post0, dg_mix_post1], axis=0),
        "norm_ffn_pre": jnp.concatenate([dg_ffn_pre0, dg_ffn_pre1], axis=0),
        "norm_ffn_post": jnp.concatenate([dg_ffn_post0, dg_ffn_post1], axis=0),
        "attn_b_o": db_o,
        "sgu_w_spatial": dws,
        "sgu_b_spatial": dbs_t.T,
    }
    shares1, where1 = _pack_rows([grads[name] for name in BEFORE_ATTN], d)
    (dqkv, db_qkv, dsink), ((q_gu0,), (q_dn0,), (r_o,), (g_shares1,)) = _attn_bwd(q, kdup, vdup, do, sinks, cos, sin, jobs=[
        to_chips(p_gu0), to_chips(p_dn0, prev=[q_dn0], piece=second_half(p_dn0)), to_sibling(b_o),
        _gather_job(shares1, None, sends=[_whole(shares1)])])
    update("gu0", p_gu0, chip, q_gu0)
    update("dn0", p_dn0, chip, q_dn0)
    p_o = pair(b_o, r_o, "attn_w_o")
    grads.update({"attn_b_qkv": db_qkv, "attn_sinks": dsink[:1, :d // HEAD_DIM]})
    shares2, where2 = _pack_rows([grads[name] for name in AFTER_ATTN] + [loss_tile[:1, :1]], d)
    (dx0, dg_mix_pre0), _ = _pre_bwd(dqkv, w_qkv, xs, row(norm_mix_pre[0]), dx1, "attn_qkv_bwd")
    grads["norm_mix_pre"] = jnp.concatenate([dg_mix_pre0, dg_mix_pre1], axis=0)
    shares3, where3 = _pack_rows([grads[name] for name in LAST], d)
    dw_qkv, ((q_o,), (g_shares1,), (g_shares2,), (g_shares3,)) = _wgrad(h0, dqkv, "attn_qkv_wgrad", ACT, transposed=True, jobs=[
        to_chips(p_o), _gather_job(shares1, g_shares1, forwards=[_whole(shares1)]),
        _gather_job(shares2, None, sends=[_whole(shares2)]), _gather_job(shares3, None, sends=[_whole(shares3)])])
    update("attn_w_o", p_o, chip, q_o)
    b_qkv_grad = _rows_blocked(dw_qkv)
    (r_qkv,), (g_shares2,), (g_shares3,) = _copies_only([
        to_sibling(b_qkv_grad), _gather_job(shares2, g_shares2, forwards=[_whole(shares2)]),
        _gather_job(shares3, g_shares3, forwards=[_whole(shares3)])], "grads_tail_to_sibling")
    p_qkv = pair(b_qkv_grad, r_qkv, "attn_w_qkv")
    (q_qkv,), = _copies_only([to_chips(p_qkv)], "grads_tail_to_owner_chip")
    update("attn_w_qkv", p_qkv, chip, q_qkv)

    finish(BEFORE_ATTN, 0, g_shares1, where1, "replicated_update_before_attn")
    loss = finish(AFTER_ATTN, 1, g_shares2, where2, "replicated_update_after_attn")[0][0, 0]
    finish(LAST, 0, g_shares3, where3, "replicated_update_last")

    for out in (out_g, out_d, out_m, out_v):
        out["sgu_ln_g"], out["sgu_ln_b"] = out["sgu_ln"][:, 0, :], out["sgu_ln"][:, 1, :]
        for n in SWAPPED:
            out[n] = jnp.swapaxes(out[n], 1, 2)

    return (loss, dx0[None], *[out_g[n] for n in WEIGHTS], *[out_d[n] for n in WEIGHTS],
            *[out_m[n] for n in WEIGHTS], *[out_v[n] for n in WEIGHTS])
```

```python
import functools
import math
import operator

import jax
import jax.numpy as jnp
import numpy as np
from jax import lax
from jax.experimental import pallas as pl
from jax.experimental.pallas import tpu as pltpu

F32 = jnp.float32
ACT = jnp.bfloat16

EPS = 1e-6
HEAD_DIM = 64
PAIR = 2 * HEAD_DIM
GQA_GROUP = 4
WINDOW = 128
ROPE_THETA = 10000.0
SCALE = HEAD_DIM ** -0.5
MASKED = -1e30
LANES = 128
MXU_WIDTH = 256

ADAM_LR, ADAM_B1, ADAM_B2, ADAM_EPS, ADAM_WD, ADAM_STEP = 0.001, 0.9, 0.999, 1e-08, 0.01, 10

N_DEV = 8
VMEM_LIMIT_BYTES = 56 * 1024 * 1024
MESH = pl.DeviceIdType.MESH
ANY = pl.BlockSpec(memory_space=pl.ANY)
IN_VMEM = pl.BlockSpec(memory_space=pltpu.VMEM)


def _pick(n, candidates):
    for c in candidates:
        if n % c == 0:
            return c
    return n


def _row_tile(t, most=512):
    return _pick(t, (most,))


def _shard_tile(rows):
    return next((c for c in (512, 256, 176, 128, 96, 64) if rows % c == 0 and rows >= 2 * c), rows)


def _mxu_chunks(n, most=4 * MXU_WIDTH):
    assert n % MXU_WIDTH == 0 and most % MXU_WIDTH == 0
    return [(c, min(most, n - c)) for c in range(0, n, most)]


def _params(*sem):
    return pltpu.CompilerParams(dimension_semantics=sem, vmem_limit_bytes=VMEM_LIMIT_BYTES)


def _full(shape):
    return pl.BlockSpec(shape, lambda *_: (0,) * len(shape))


def _resident(shape):
    return pl.BlockSpec(shape, lambda *_: (0,) * len(shape), pipeline_mode=pl.Buffered(1))


def _rows(tm, n):
    return pl.BlockSpec((tm, n), lambda i: (i, 0))


def _sds(shape, dtype):
    return jax.ShapeDtypeStruct(shape, dtype)


def _place():
    return lax.axis_index("x"), lax.axis_index("y"), lax.axis_index("c")


def _other_chips(x, y):
    return [(1 - x, y), (x, 1 - y), (1 - x, 1 - y)]


class _Job:
    def __init__(self, inputs, out_shape, aliases, emit, n_remote, n_local=0, n_late=0):
        self.inputs, self.out_shape, self.aliases, self.emit = list(inputs), list(out_shape), dict(aliases), emit
        self.n_remote, self.n_local, self.n_late = n_remote, n_local, n_late


def _call(body, name, grid, in_specs, out_specs, out_shape, args, sem, scratch=(), jobs=()):
    n_in, n_out, n_scr = len(args), len(out_shape), len(scratch)
    j_in = [a for job in jobs for a in job.inputs]
    j_out = [s for job in jobs for s in job.out_shape]
    aliases, at_in, at_out = {}, n_in, n_out
    for job in jobs:
        aliases.update({at_in + i: at_out + o for i, o in job.aliases.items()})
        at_in, at_out = at_in + len(job.inputs), at_out + len(job.out_shape)
    n_remote = sum(job.n_remote for job in jobs)
    n_local = sum(job.n_local for job in jobs)

    def wrapped(*refs):
        ins, jin = refs[:n_in], refs[n_in:n_in + len(j_in)]
        rest = refs[n_in + len(j_in):]
        outs, jout = rest[:n_out], rest[n_out:n_out + len(j_out)]
        scr = rest[n_out + len(j_out):n_out + len(j_out) + n_scr]
        if not jobs:
            body(*ins, *outs, *scr)
            return
        send, recv, local = rest[n_out + len(j_out) + n_scr:]
        ids = [pl.program_id(a) for a in range(len(grid))]
        first = functools.reduce(operator.and_, [i == 0 for i in ids])
        last = functools.reduce(operator.and_, [i == g - 1 for i, g in zip(ids, grid)])

        def descriptors():
            place, found, i, o, r, l = _place(), ([], []), 0, 0, 0, 0
            for job in jobs:
                emitted = job.emit(jin[i:i + len(job.inputs)], jout[o:o + len(job.out_shape)], place)
                for at, (src, dst, to) in enumerate(emitted):
                    if to is None:
                        cp = pltpu.make_async_copy(src, dst, local.at[l])
                        l += 1
                    else:
                        cp = pltpu.make_async_remote_copy(src_ref=src, dst_ref=dst, send_sem=send.at[r], recv_sem=recv.at[r],
                                                          device_id=to, device_id_type=MESH)
                        r += 1
                    found[at >= len(emitted) - job.n_late].append(cp)
                i, o = i + len(job.inputs), o + len(job.out_shape)
            return found

        @pl.when(first)
        def _():
            for cp in descriptors()[0]:
                cp.start()

        body(*ins, *outs, *scr)

        @pl.when(last)
        def _():
            early, late = descriptors()
            for cp in early:
                cp.wait()
            for cp in late:
                cp.start()
            for cp in late:
                cp.wait()

    sems = [pltpu.SemaphoreType.DMA((n_remote,)), pltpu.SemaphoreType.DMA((n_remote,)),
            pltpu.SemaphoreType.DMA((max(n_local, 1),))] if jobs else []
    res = pl.pallas_call(
        wrapped, name=name, grid=grid,
        in_specs=list(in_specs) + [ANY] * len(j_in), out_specs=list(out_specs) + [ANY] * len(j_out),
        out_shape=list(out_shape) + j_out, scratch_shapes=list(scratch) + sems, input_output_aliases=aliases,
        compiler_params=_params(*(("arbitrary",) * len(grid) if jobs else sem)),
    )(*args, *j_in)
    job_res, at = [], n_out
    for job in jobs:
        job_res.append(list(res[at:at + len(job.out_shape)]))
        at += len(job.out_shape)
    return list(res[:n_out]), job_res


def _copies_only(jobs, name):
    def body(o_ref):
        o_ref[...] = jnp.zeros_like(o_ref)

    return _call(body, name, (1,), [], [_full((8, LANES))], [_sds((8, LANES), F32)], (), ("arbitrary",), jobs=jobs)[1]


def _gather_job(stage, gathered, sends=(), forwards=()):
    shape = _sds((N_DEV,) + stage.shape, stage.dtype)
    inputs = ([stage] if sends else []) + ([gathered] if gathered is not None else [])
    aliases = {len(inputs) - 1: 0} if gathered is not None else {}

    def emit(ins, outs, place):
        x, y, c = place
        sibling, chips, mine, g = (x, y, 1 - c), _other_chips(x, y), 4 * x + 2 * y + c, outs[0]
        found = []
        for r0, nr in sends:
            src, dst = ins[0].at[pl.ds(r0, nr)], g.at[mine, pl.ds(r0, nr)]
            found += [(src, dst, None), (src, dst, sibling)] + [(src, dst, (px, py, c)) for px, py in chips]
        for r0, nr in forwards:
            for px, py in chips:
                block = 4 * px + 2 * py + c
                found.append((ins[-1].at[block, pl.ds(r0, nr)], g.at[block, pl.ds(r0, nr)], sibling))
        return found

    return _Job(inputs, [shape], aliases, emit, 4 * len(sends) + 3 * len(forwards), len(sends))


def _relay_gather_job(stage, gathered, sends=(), relays=(), forwards=(), far=(), late_far=()):
    shape = _sds((N_DEV,) + stage.shape, stage.dtype)
    inputs = ([stage] if sends else []) + ([gathered] if gathered is not None else [])
    aliases = {len(inputs) - 1: 0} if gathered is not None else {}

    def emit(ins, outs, place):
        x, y, c = place
        sibling, beside_x, beside_y, g = (x, y, 1 - c), (1 - x, y, c), (x, 1 - y, c), outs[0]
        slot = lambda dev: 4 * dev[0] + 2 * dev[1] + dev[2]
        found = []
        for r0, nr in sends:
            src, dst = ins[0].at[pl.ds(r0, nr)], g.at[slot((x, y, c)), pl.ds(r0, nr)]
            found += [(src, dst, None), (src, dst, sibling), (src, dst, beside_x), (src, dst, beside_y)]

        def passed_on(block, piece, to):
            return ins[-1].at[block, pl.ds(*piece)], g.at[block, pl.ds(*piece)], to

        for piece, way in relays:
            found.append(passed_on(slot(beside_x), piece, beside_y) if way == 0 else passed_on(slot(beside_y), piece, beside_x))
        for piece in forwards:
            found += [passed_on(slot(beside_x), piece, sibling), passed_on(slot(beside_y), piece, sibling)]
        for piece in tuple(far) + tuple(late_far):
            found.append(passed_on(slot((1 - x, 1 - y, c)), piece, sibling))
        return found

    n_remote = 3 * len(sends) + len(relays) + 2 * len(forwards) + len(far) + len(late_far)
    return _Job(inputs, [shape], aliases, emit, n_remote, len(sends), len(late_far))


def _scatter_job(arrays, n_slots, route, prev=None, piece=None):
    shapes = [_sds((n_slots,) + v.shape[1:], v.dtype) for v in arrays]
    inputs = list(arrays) + (list(prev) if prev else [])
    aliases = {len(arrays) + i: i for i in range(len(arrays))} if prev else {}

    def emit(ins, outs, place):
        found = []
        for a in range(len(arrays)):
            for s in range(n_slots):
                block, to = route(s, *place)
                if piece is None:
                    found.append((ins[a].at[block], outs[a].at[s], to))
                else:
                    found.append((ins[a].at[block, pl.ds(*piece)], outs[a].at[s, pl.ds(*piece)], to))
        return found

    return _Job(inputs, shapes, aliases, emit, len(arrays) * n_slots)


def _to_sibling(s, x, y, c):
    return 2 * s + (1 - c), (x, y, 1 - c)


def _to_owner_chip(s, x, y, c):
    px, py = _other_chips(x, y)[s]
    return 2 * px + py, (px, py, c)


def _rstd(x):
    return lax.rsqrt(jnp.mean(x * x, axis=-1, keepdims=True) + EPS)


def _rms_bwd(xhat, r, g, dy):
    dxh = dy * g
    return r * (dxh - xhat * jnp.mean(dxh * xhat, axis=-1, keepdims=True))


def _first_half(rows):
    lane = lax.broadcasted_iota(jnp.int32, (rows, PAIR), 1)
    return (lane % HEAD_DIM) < (HEAD_DIM // 2)


def _rot_half(v, first):
    return jnp.where(first, pltpu.roll(v, PAIR - HEAD_DIM // 2, 1), pltpu.roll(v, HEAD_DIM // 2, 1))


def _rope(v, cos, sin, first):
    return v * cos + _rot_half(v, first) * sin


def _rope_t(dv, cos, sin, first):
    return dv * cos + _rot_half(dv * sin, first)


def _dot(a, b):
    return jnp.dot(a, b, preferred_element_type=F32)


def _dot_nt(a, b):
    return lax.dot_general(a, b, (((1,), (1,)), ((), ())), preferred_element_type=F32)


def _dot_tn(a, b):
    return lax.dot_general(a, b, (((0,), (0,)), ((), ())), preferred_element_type=F32)


def _sigmoid(v):
    return 1.0 / (1.0 + jnp.exp(-v))


_GELU_K = math.sqrt(2.0 / math.pi)
_GELU_C = 0.044715


def _gelu(v):
    return v * (0.5 * (1.0 + jnp.tanh(_GELU_K * (v + _GELU_C * (v * v * v)))))


def _gelu_grad(v):
    t = jnp.tanh(_GELU_K * (v + _GELU_C * (v * v * v)))
    return 0.5 * (1.0 + t) + 0.5 * v * (1.0 - t * t) * (_GELU_K * (1.0 + 3.0 * _GELU_C * (v * v)))


def _attn_qkv_fwd(x, g, w, b, cos, sin, jobs=()):
    t, d = x.shape
    n = w.shape[0]
    kd = n - d
    assert (kd // 2) % PAIR == 0
    tm = _row_tile(t)
    ch = _pick(d, (512,))

    def body(x_ref, g_ref, w_ref, b_ref, cos_ref, sin_ref, h_ref, q_ref, k_ref, v_ref):
        xv = x_ref[...]
        hb = (xv * _rstd(xv) * g_ref[...]).astype(ACT)
        h_ref[...] = hb
        cosv, sinv = cos_ref[...], sin_ref[...]
        first = _first_half(tm)
        left = _lane_halves()[0]

        def proj(lo, width):
            return _dot_nt(hb, w_ref[lo:lo + width, :]) + b_ref[:, lo:lo + width]

        def twice(ref, s, two_heads):
            swapped = pltpu.roll(two_heads, HEAD_DIM, 1)
            ref[:, 2 * s:2 * s + PAIR] = jnp.where(left, two_heads, swapped).astype(ACT)
            ref[:, 2 * s + PAIR:2 * s + 2 * PAIR] = jnp.where(left, swapped, two_heads).astype(ACT)

        for c in range(0, d, ch):
            y = proj(c, ch)
            for s in range(0, ch, PAIR):
                q_ref[:, c + s:c + s + PAIR] = (_rope(y[:, s:s + PAIR], cosv, sinv, first) * SCALE).astype(ACT)
        y = proj(d, kd)
        for s in range(0, kd // 2, PAIR):
            twice(k_ref, s, _rope(y[:, s:s + PAIR], cosv, sinv, first))
            twice(v_ref, s, y[:, kd // 2 + s:kd // 2 + s + PAIR])

    return _call(
        body, "attn_qkv_fwd", (t // tm,),
        [_rows(tm, d), _full((1, d)), _full((n, d)), _full((1, n)), _rows(tm, PAIR), _rows(tm, PAIR)],
        [_rows(tm, d), _rows(tm, d), _rows(tm, kd), _rows(tm, kd)],
        [_sds((t, d), ACT), _sds((t, d), ACT), _sds((t, kd), ACT), _sds((t, kd), ACT)],
        (x, g, w, b, cos, sin), ("parallel",), jobs=jobs)


STACK = GQA_GROUP * WINDOW


def _band_mask(has_prev):
    row = lax.broadcasted_iota(jnp.int32, (STACK, 2 * WINDOW), 0) % WINDOW
    col = lax.broadcasted_iota(jnp.int32, (STACK, 2 * WINDOW), 1)
    return ((col < WINDOW) & (col > row) & has_prev) | ((col >= WINDOW) & (col - WINDOW <= row))


def _lane_halves():
    lane = lax.broadcasted_iota(jnp.int32, (1, PAIR), 1)
    return lane < HEAD_DIM, lane >= HEAD_DIM


def _stack_heads(ref, h, halves):
    parts = []
    for p in range(2):
        pair = ref[:, (2 * h + p) * PAIR:(2 * h + p + 1) * PAIR]
        parts += [jnp.where(half, pair, jnp.zeros_like(pair)) for half in halves]
    return jnp.concatenate(parts, axis=0)


def _sink_column(sink_ref, h):
    row = lax.broadcasted_iota(jnp.int32, (STACK, 1), 0)
    col = jnp.full((STACK, 1), sink_ref[0, GQA_GROUP * h + GQA_GROUP - 1], F32)
    for j in range(GQA_GROUP - 2, -1, -1):
        col = jnp.where(row < (j + 1) * WINDOW, sink_ref[0, GQA_GROUP * h + j], col)
    return col


def _probs(scores, valid, sink):
    s = jnp.where(valid, scores, MASKED)
    m = jnp.maximum(jnp.max(s, axis=-1, keepdims=True), sink)
    p = jnp.exp(s - m)
    psink = jnp.exp(sink - m)
    inv = 1.0 / (jnp.sum(p, axis=-1, keepdims=True) + psink)
    return p * inv, psink * inv


def _attn_fwd(q, kd_, vd, sinks, jobs=()):
    t, d = q.shape
    kd = kd_.shape[1]
    cur = lambda n: (n, 0)
    prev = lambda n: (jnp.maximum(n - 1, 0), 0)

    def body(sink_ref, q_ref, kc_ref, kp_ref, vc_ref, vp_ref, o_ref):
        valid = _band_mask(pl.program_id(0) > 0)
        halves = _lane_halves()
        heads = range(kd // PAIR)
        hs = [slice(h * PAIR, (h + 1) * PAIR) for h in heads]
        scores = [_dot_nt(_stack_heads(q_ref, h, halves), jnp.concatenate([kp_ref[:, hs[h]], kc_ref[:, hs[h]]], axis=0)) for h in heads]
        probs = [_probs(scores[h], valid, _sink_column(sink_ref, h))[0].astype(ACT) for h in heads]
        for h in heads:
            v2 = jnp.concatenate([vp_ref[:, hs[h]], vc_ref[:, hs[h]]], axis=0)
            vs = jnp.concatenate([jnp.where(half, v2, jnp.zeros_like(v2)) for half in halves], axis=0)
            pr = probs[h]
            for p in range(2):
                both = jnp.concatenate([pr[2 * p * WINDOW:(2 * p + 1) * WINDOW], pr[(2 * p + 1) * WINDOW:(2 * p + 2) * WINDOW]], axis=1)
                o_ref[:, (2 * h + p) * PAIR:(2 * h + p + 1) * PAIR] = _dot(both, vs).astype(ACT)

    kv_c, kv_p = pl.BlockSpec((WINDOW, kd), cur), pl.BlockSpec((WINDOW, kd), prev)
    return _call(
        body, "attn_fwd", (t // WINDOW,),
        [pl.BlockSpec(memory_space=pltpu.SMEM), pl.BlockSpec((WINDOW, d), cur), kv_c, kv_p, kv_c, kv_p],
        [pl.BlockSpec((WINDOW, d), cur)], [_sds((t, d), ACT)],
        (sinks, q, kd_, kd_, vd, vd), ("parallel",), jobs=jobs)


def _attn_bwd(q, kd_, vd, do, sinks, cos, sin, jobs=()):
    t, d = q.shape
    kd = kd_.shape[1]
    nb = t // WINDOW
    n_out = d + kd
    assert (kd // 2) % PAIR == 0
    cur = lambda n: (jnp.minimum(n, nb - 1), 0)
    prev = lambda n: (jnp.maximum(jnp.minimum(n, nb - 1) - 1, 0), 0)
    late = lambda n: (jnp.maximum(n - 1, 0), 0)

    def body(sink_ref, q_ref, kc_ref, kp_ref, vc_ref, vp_ref, do_ref, cosc_ref, sinc_ref, cosp_ref, sinp_ref,
             out_ref, db_ref, dsink_ref, dq_keep, dk_keep, dv_keep):
        n = pl.program_id(0)

        @pl.when(n == 0)
        def _():
            for ref in (db_ref, dsink_ref, dq_keep, dk_keep, dv_keep):
                ref[...] = jnp.zeros_like(ref)

        valid = _band_mask(n > 0) & (n < nb)
        halves = _lane_halves()
        first = _first_half(WINDOW)
        slot = lax.broadcasted_iota(jnp.int32, dsink_ref.shape, 1)
        top = lax.broadcasted_iota(jnp.int32, dsink_ref.shape, 0) == 0
        dsink = jnp.zeros(dsink_ref.shape, F32)

        def emit(cols, done):
            out_ref[:, cols] = done.astype(ACT)
            db_ref[:, cols] += jnp.sum(done.astype(F32), axis=0, keepdims=True)

        heads = range(kd // PAIR)
        hs = [slice(h * PAIR, (h + 1) * PAIR) for h in heads]
        k2 = [jnp.concatenate([kp_ref[:, hs[h]], kc_ref[:, hs[h]]], axis=0) for h in heads]
        v2 = [jnp.concatenate([vp_ref[:, hs[h]], vc_ref[:, hs[h]]], axis=0) for h in heads]
        qs = [_stack_heads(q_ref, h, halves) for h in heads]
        dos = [_stack_heads(do_ref, h, halves) for h in heads]
        scores = [_dot_nt(qs[h], k2[h]) for h in heads]
        dps = [_dot_nt(dos[h], v2[h]) for h in heads]
        prs, dss = [], []
        for h in heads:
            pr, psink = _probs(scores[h], valid, _sink_column(sink_ref, h))
            delta = jnp.sum(pr * dps[h], axis=-1, keepdims=True)
            dss.append((pr * (dps[h] - delta)).astype(ACT))
            prs.append(pr.astype(ACT))
            leak = psink * delta
            for j in range(GQA_GROUP):
                share = jnp.sum(leak[j * WINDOW:(j + 1) * WINDOW], axis=0, keepdims=True)
                dsink = dsink - jnp.where(top & (slot == GQA_GROUP * h + j), share, 0.0)
        dqs = [_dot(dss[h], k2[h]) for h in heads]
        dk2 = [_dot_tn(dss[h], qs[h]) for h in heads]
        dv2 = [_dot_tn(prs[h], dos[h]) for h in heads]
        for h in heads:
            for p in range(2):
                ps = slice((2 * h + p) * PAIR, (2 * h + p + 1) * PAIR)
                dqp = jnp.where(halves[0], dqs[h][2 * p * WINDOW:(2 * p + 1) * WINDOW], dqs[h][(2 * p + 1) * WINDOW:(2 * p + 2) * WINDOW])
                emit(ps, dq_keep[:, ps])
                dq_keep[:, ps] = (_rope_t(dqp, cosc_ref[...], sinc_ref[...], first) * SCALE).astype(ACT)
        dks, dvs = [], []
        for h in heads:
            dks.append(dk_keep[:, hs[h]] + _rope_t(dk2[h][:WINDOW], cosp_ref[...], sinp_ref[...], first))
            dk_keep[:, hs[h]] = _rope_t(dk2[h][WINDOW:], cosc_ref[...], sinc_ref[...], first)
            dvs.append(dv_keep[:, hs[h]] + dv2[h][:WINDOW])
            dv_keep[:, hs[h]] = dv2[h][WINDOW:]
        both = lambda v: v + pltpu.roll(v, HEAD_DIM, 1)
        for h in range(0, len(heads), 2):
            at = h // 2 * PAIR
            emit(slice(d + at, d + at + PAIR), jnp.where(halves[0], both(dks[h]), both(dks[h + 1])))
            emit(slice(d + kd // 2 + at, d + kd // 2 + at + PAIR), jnp.where(halves[0], both(dvs[h]), both(dvs[h + 1])))
        dsink_ref[...] += dsink

    kv_c, kv_p = pl.BlockSpec((WINDOW, kd), cur), pl.BlockSpec((WINDOW, kd), prev)
    tab_c, tab_p = pl.BlockSpec((WINDOW, PAIR), cur), pl.BlockSpec((WINDOW, PAIR), prev)
    return _call(
        body, "attn_bwd", (nb + 1,),
        [pl.BlockSpec(memory_space=pltpu.SMEM), pl.BlockSpec((WINDOW, d), cur), kv_c, kv_p, kv_c, kv_p,
         pl.BlockSpec((WINDOW, d), cur), tab_c, tab_c, tab_p, tab_p],
        [pl.BlockSpec((WINDOW, n_out), late), _full((1, n_out)), _full((8, LANES))],
        [_sds((t, n_out), ACT), _sds((1, n_out), F32), _sds((8, LANES), F32)],
        (sinks, q, kd_, kd_, vd, vd, do, cos, sin, cos, sin), ("arbitrary",),
        scratch=[pltpu.VMEM((WINDOW, d), ACT), pltpu.VMEM((WINDOW, kd), F32), pltpu.VMEM((WINDOW, kd), F32)], jobs=jobs)


def _proj_res(a, w, b, g, x, name, jobs=()):
    t = a.shape[0]
    k, d = w.shape
    tm = _row_tile(t)
    has_bias = b is not None

    def body(*refs):
        a_ref, w_ref = refs[:2]
        b_ref = refs[2] if has_bias else None
        g_ref, x_ref, m_ref, xo_ref = refs[2 + has_bias:]
        m = _dot(a_ref[...], w_ref[...])
        if has_bias:
            m = m + b_ref[...]
        m_ref[...] = m.astype(ACT)
        xo_ref[...] = x_ref[...] + (m * _rstd(m)) * g_ref[...]

    ins = [a, w] + ([b] if has_bias else []) + [g, x]
    specs = [_rows(tm, k), _full((k, d))] + ([_full((1, d))] if has_bias else []) + [_full((1, d)), _rows(tm, d)]
    return _call(body, name, (t // tm,), specs, [_rows(tm, d), _rows(tm, d)], [_sds((t, d), ACT), _sds((t, d), F32)], ins,
                 ("parallel",), jobs=jobs)


def _post_bwd(dres, m, g, w, a, mode, name, jobs=()):
    t, d = dres.shape
    k = w.shape[0]
    tm = _row_tile(t)
    kc = _pick(k, (1408, 1024, 512))
    steps = t // tm

    def body(*refs):
        d_ref, m_ref, g_ref, w_ref, a_ref = refs[:5]
        da_ref, dw_ref, dg_ref = refs[5:8]
        db_ref, acc_ref = (refs[8] if mode == "attn" else None), refs[-1]
        i = pl.program_id(0)

        @pl.when(i == 0)
        def _():
            for ref in (dg_ref, acc_ref) + ((db_ref,) if mode == "attn" else ()):
                ref[...] = jnp.zeros_like(ref)

        dv, mv = d_ref[...], m_ref[...].astype(F32)
        r = _rstd(mv)
        mh = mv * r
        dg_ref[...] += jnp.sum(dv * mh, axis=0, keepdims=True)
        dm = _rms_bwd(mh, r, g_ref[...], dv)
        dmb = dm.astype(ACT)
        if mode == "attn":
            db_ref[...] += jnp.sum(dm, axis=0, keepdims=True)
        for c in range(0, k, kc):
            da = _dot_nt(dmb, w_ref[c:c + kc, :])
            da_ref[:, c:c + kc] = da.astype(ACT)
        acc_ref[...] += _dot_tn(a_ref[...], dmb)

        @pl.when(i == steps - 1)
        def _():
            dw_ref[...] = acc_ref[...].astype(ACT)

    ins = [dres, m, g, w, a]
    specs = [_rows(tm, d), _rows(tm, d), _full((1, d)), _full((k, d)), _rows(tm, k)]
    out_specs = [_rows(tm, k), _full((k, d)), _full((1, d))]
    out_shape = [_sds((t, k), ACT), _sds((k, d), ACT), _sds((1, d), F32)]
    if mode == "attn":
        out_specs.append(_full((1, d)))
        out_shape.append(_sds((1, d), F32))
    return _call(body, name, (steps,), specs, out_specs, out_shape, ins, ("arbitrary",), scratch=[pltpu.VMEM((k, d), F32)], jobs=jobs)


def _pre_bwd(dy, w, x, g, dres, name, h=None, jobs=()):
    t, d = x.shape
    tm = _row_tile(t)
    steps = t // tm
    n = dy.shape[1]

    def body(*refs):
        dy_ref, w_ref, x_ref, g_ref, dres_ref = refs[:5]
        dx_ref, dg_ref = refs[-4:-2] if h is not None else refs[-2:]
        i = pl.program_id(0)

        @pl.when(i == 0)
        def _():
            dg_ref[...] = jnp.zeros_like(dg_ref)
            if h is not None:
                refs[-1][...] = jnp.zeros_like(refs[-1])

        if w.ndim == 3:
            c = w.shape[2]
            dh = jnp.zeros((tm, d), F32)
            for j in range(w.shape[0]):
                dh = dh + _dot_nt(dy_ref[:, j * c:(j + 1) * c], w_ref[j])
        else:
            dh = _dot(dy_ref[...], w_ref[...])
        xv = x_ref[...]
        r = _rstd(xv)
        xh = xv * r
        dg_ref[...] += jnp.sum(dh * xh, axis=0, keepdims=True)
        dx_ref[...] = dres_ref[...] + _rms_bwd(xh, r, g_ref[...], dh)
        if h is not None:
            h_ref, dw_ref, acc_ref = refs[5], refs[-2], refs[-1]
            acc_ref[...] += _dot_tn(h_ref[...], dy_ref[...])

            @pl.when(i == steps - 1)
            def _():
                for j in range(w.shape[0]):
                    dw_ref[j] = acc_ref[:, j * c:(j + 1) * c].astype(ACT)

    specs = [_rows(tm, n), _resident(w.shape), _rows(tm, d), _full((1, d)), _rows(tm, d)]
    out_specs, out_shape, scratch = [_rows(tm, d), _full((1, d))], [_sds((t, d), F32), _sds((1, d), F32)], []
    if h is not None:
        specs, out_specs, out_shape = specs + [_rows(tm, d)], out_specs + [_full(w.shape)], out_shape + [_sds(w.shape, ACT)]
        scratch = [pltpu.VMEM((d, n), F32)]
    return _call(body, name, (steps,), specs, out_specs, out_shape, (dy, w, x, g, dres) + ((h,) if h is not None else ()),
                 ("arbitrary",), scratch=scratch, jobs=jobs)


def _ffn_bwd(dres, f, g_post, w_dn, gu, w_gu, x, g_pre, name, target=None, jobs=()):
    t, d = x.shape
    n = w_dn.shape[0]
    tm = _row_tile(t, 256)
    whole = target is not None
    n_in = 6 if whole else 8

    def body(*refs):
        if whole:
            t_ref, gp_ref, wd_ref, wu_ref, x_ref, g_ref = refs[:n_in]
            df_ref, dgu_ref, dx_ref, dgp_ref, dg_ref, loss_ref, h_ref, a_ref, gu_ref = refs[n_in:]
        else:
            d_ref, f_ref, gp_ref, wd_ref, gu_ref, wu_ref, x_ref, g_ref = refs[:n_in]
            df_ref, dgu_ref, dx_ref, dgp_ref, dg_ref = refs[n_in:]

        @pl.when(pl.program_id(0) == 0)
        def _():
            for ref in (dgp_ref, dg_ref) + ((loss_ref,) if whole else ()):
                ref[...] = jnp.zeros_like(ref)

        xv = x_ref[...]
        rx = _rstd(xv)
        xh = xv * rx
        if whole:
            hb = (xh * g_ref[...]).astype(ACT)
            h_ref[...] = hb
            for c, size in _mxu_chunks(n):
                gate = _dot_nt(hb, wu_ref[c:c + size, :])
                up = _dot_nt(hb, wu_ref[n + c:n + c + size, :])
                gu_ref[:, c:c + size] = gate.astype(ACT)
                gu_ref[:, n + c:n + c + size] = up.astype(ACT)
                a_ref[:, c:c + size] = (gate * _sigmoid(gate) * up).astype(ACT)
            fv = _dot(a_ref[...], wd_ref[...])
            r = _rstd(fv)
            fh = fv * r
            err = xv + fh * gp_ref[...] - t_ref[...]
            dv = err * (1.0 / d)
            loss_ref[...] += 0.5 * jnp.sum(jnp.mean(err * err, axis=-1, keepdims=True), axis=0, keepdims=True)
        else:
            dv, fv = d_ref[...], f_ref[...].astype(F32)
            r = _rstd(fv)
            fh = fv * r
        dgp_ref[...] += jnp.sum(dv * fh, axis=0, keepdims=True)
        dfb = _rms_bwd(fh, r, gp_ref[...], dv).astype(ACT)
        df_ref[...] = dfb
        for c, size in _mxu_chunks(n):
            da = _dot_nt(dfb, wd_ref[c:c + size, :]).astype(ACT)
            gate, up = gu_ref[:, c:c + size], gu_ref[:, n + c:n + c + size]
            sg = _sigmoid(gate.astype(F32)).astype(ACT)
            gs = gate * sg
            dgu_ref[:, c:c + size] = da * up * (sg + gs - gs * sg)
            dgu_ref[:, n + c:n + c + size] = da * gs
        dh = _dot(dgu_ref[...], wu_ref[...])
        dg_ref[...] += jnp.sum(dh * xh, axis=0, keepdims=True)
        dx_ref[...] = dv + _rms_bwd(xh, rx, g_ref[...], dh)

    w_specs = [_resident(w_dn.shape), _resident(w_gu.shape)]
    out_specs = [_rows(tm, d), _rows(tm, 2 * n), _rows(tm, d), _full((1, d)), _full((1, d))]
    out_shape = [_sds((t, d), ACT), _sds((t, 2 * n), ACT), _sds((t, d), F32), _sds((1, d), F32), _sds((1, d), F32)]
    if whole:
        return _call(
            body, name, (t // tm,),
            [_rows(tm, d), _full((1, d))] + w_specs + [_rows(tm, d), _full((1, d))],
            out_specs + [_full((8, LANES)), _rows(tm, d), _rows(tm, n)],
            out_shape + [_sds((8, LANES), F32), _sds((t, d), ACT), _sds((t, n), ACT)],
            (target, g_post, w_dn, w_gu, x, g_pre), ("arbitrary",), scratch=[pltpu.VMEM((tm, 2 * n), ACT)], jobs=jobs)
    return _call(
        body, name, (t // tm,),
        [_rows(tm, d), _rows(tm, d), _full((1, d)), w_specs[0], _rows(tm, 2 * n), w_specs[1], _rows(tm, d), _full((1, d))],
        out_specs, out_shape, (dres, f, g_post, w_dn, gu, w_gu, x, g_pre), ("arbitrary",), jobs=jobs)


def _wgrad(a, b, name, out_dtype=F32, transposed=False, jobs=()):
    t = a.shape[0]
    tt = _pick(t, (1024,))
    steps = t // tt
    tk = _pick(a.shape[1], (1024, 1408))
    k, n = a.shape[1], b.shape[1]
    tn = _pick(n, (1024, 1408))
    if transposed:
        out_spec, out_shape, acc_shape = pl.BlockSpec((tn, tk), lambda i, j, s: (j, i)), _sds((n, k), out_dtype), (tn, tk)
    else:
        out_spec, out_shape, acc_shape = pl.BlockSpec((tk, tn), lambda i, j, s: (i, j)), _sds((k, n), out_dtype), (tk, tn)

    def body(a_ref, b_ref, o_ref, acc_ref):
        s = pl.program_id(2)

        @pl.when(s == 0)
        def _():
            acc_ref[...] = jnp.zeros_like(acc_ref)

        acc_ref[...] += _dot_tn(b_ref[...], a_ref[...]) if transposed else _dot_tn(a_ref[...], b_ref[...])

        @pl.when(s == steps - 1)
        def _():
            o_ref[...] = acc_ref[...].astype(out_dtype)

    res, job_res = _call(
        body, name, (k // tk, n // tn, steps),
        [pl.BlockSpec((tt, tk), lambda i, j, s: (s, i)), pl.BlockSpec((tt, tn), lambda i, j, s: (s, j))], [out_spec], [out_shape],
        (a, b), ("parallel", "parallel", "arbitrary"), scratch=[pltpu.VMEM(acc_shape, F32)], jobs=jobs)
    return res[0], job_res


def _ffn_fwd(x, g_pre, w_gu, w_dn, g_post, jobs=()):
    t, d = x.shape
    n = w_dn.shape[0]
    tm = _row_tile(t)

    def body(x_ref, g_ref, wu_ref, wd_ref, gp_ref, h_ref, gu_ref, a_ref, f_ref, xo_ref):
        xv = x_ref[...]
        hb = (xv * _rstd(xv) * g_ref[...]).astype(ACT)
        h_ref[...] = hb
        for c, size in _mxu_chunks(n):
            gate = _dot_nt(hb, wu_ref[c:c + size, :])
            up = _dot_nt(hb, wu_ref[n + c:n + c + size, :])
            gu_ref[:, c:c + size] = gate.astype(ACT)
            gu_ref[:, n + c:n + c + size] = up.astype(ACT)
            a_ref[:, c:c + size] = (gate * _sigmoid(gate) * up).astype(ACT)
        f = _dot(a_ref[...], wd_ref[...])
        f_ref[...] = f.astype(ACT)
        xo_ref[...] = xv + (f * _rstd(f)) * gp_ref[...]

    return _call(
        body, "ffn_fwd", (t // tm,),
        [_rows(tm, d), _full((1, d)), _resident(w_gu.shape), _resident(w_dn.shape), _full((1, d))],
        [_rows(tm, d), _rows(tm, 2 * n), _rows(tm, n), _rows(tm, d), _rows(tm, d)],
        [_sds((t, d), ACT), _sds((t, 2 * n), ACT), _sds((t, n), ACT), _sds((t, d), ACT), _sds((t, d), F32)],
        (x, g_pre, w_gu, w_dn, g_post), ("parallel",), jobs=jobs)


def _causal(ws):
    row = lax.broadcasted_iota(jnp.int32, ws.shape, 0)
    col = lax.broadcasted_iota(jnp.int32, ws.shape, 1)
    return jnp.where(col <= row, ws, 0.0)


def _sgu_ln(v, lg, lb):
    mu = jnp.mean(v, axis=-1, keepdims=True)
    vc = v - mu
    rs = lax.rsqrt(jnp.mean(vc * vc, axis=-1, keepdims=True) + EPS)
    vh = vc * rs
    return vh, rs, vh * lg + lb


def _sgu_fwd(x, g, w, lg, lb, ws, bs_t, jobs=()):
    t, d = x.shape
    nb, _, c = w.shape
    n = nb * c
    groups = d // WINDOW
    tm = _row_tile(t)

    def body(x_ref, g_ref, w_ref, lg_ref, lb_ref, ws_ref, bs_ref, h_ref, z_ref, y_ref, zf_ref):
        xv = x_ref[...]
        hb = (xv * _rstd(xv) * g_ref[...]).astype(ACT)
        h_ref[...] = hb
        for j in range(nb):
            zf_ref[:, j * c:(j + 1) * c] = _dot(hb, w_ref[j])
        z_ref[...] = zf_ref[...].astype(ACT)
        gs = [slice(gi * WINDOW, (gi + 1) * WINDOW) for gi in range(groups)]
        wsg = [_causal(ws_ref[gi]).astype(ACT) for gi in range(groups)]
        for r in range(0, tm, WINDOW):
            u = _gelu(zf_ref[r:r + WINDOW, :d])
            _, _, vn = _sgu_ln(_gelu(zf_ref[r:r + WINDOW, d:]), lg_ref[...], lb_ref[...])
            mixed = [_dot(wsg[gi], vn[:, gs[gi]].astype(ACT)) for gi in range(groups)]
            for gi in range(groups):
                y_ref[r:r + WINDOW, gs[gi]] = (u[:, gs[gi]] * (mixed[gi] + bs_ref[:, gi:gi + 1])).astype(ACT)

    vec = _full((1, d))
    return _call(
        body, "sgu_fwd", (t // tm,),
        [_rows(tm, d), vec, _full((nb, d, c)), vec, vec, _full((groups, WINDOW, WINDOW)), _full((WINDOW, groups))],
        [_rows(tm, d), _rows(tm, n), _rows(tm, d)], [_sds((t, d), ACT), _sds((t, n), ACT), _sds((t, d), ACT)],
        (x, g, w, lg, lb, ws, bs_t), ("parallel",), scratch=[pltpu.VMEM((tm, n), F32)], jobs=jobs)


def _sgu_mix_bwd(z, dy, lg, lb, ws, bs_t, jobs=()):
    t, n = z.shape
    d = n // 2
    groups = d // WINDOW
    tm = _row_tile(t)

    def body(z_ref, dy_ref, lg_ref, lb_ref, ws_ref, bs_ref, dz_ref, dlg_ref, dlb_ref, dws_ref, dbs_ref):
        @pl.when(pl.program_id(0) == 0)
        def _():
            for ref in (dlg_ref, dlb_ref, dws_ref, dbs_ref):
                ref[...] = jnp.zeros_like(ref)

        lane = lax.broadcasted_iota(jnp.int32, (WINDOW, groups), 1)
        gs = [slice(gi * WINDOW, (gi + 1) * WINDOW) for gi in range(groups)]
        wsg = [_causal(ws_ref[gi]).astype(ACT) for gi in range(groups)]
        for c in range(0, tm, WINDOW):
            rows = slice(c, c + WINDOW)
            zu, zv = z_ref[rows, :d].astype(F32), z_ref[rows, d:].astype(F32)
            u = _gelu(zu)
            vh, rs, vn = _sgu_ln(_gelu(zv), lg_ref[...], lb_ref[...])
            dyv = dy_ref[rows, :].astype(F32)
            vng = [vn[:, gs[gi]].astype(ACT) for gi in range(groups)]
            dmix = dyv * u
            dmb = [dmix[:, gs[gi]].astype(ACT) for gi in range(groups)]
            mixed = [_dot(wsg[gi], vng[gi]) for gi in range(groups)]
            dvn_parts = [_dot_tn(wsg[gi], dmb[gi]) for gi in range(groups)]
            dws_parts = [_dot_nt(dmb[gi], vng[gi]) for gi in range(groups)]
            for gi in range(groups):
                dz_ref[rows, gs[gi]] = (dyv[:, gs[gi]] * (mixed[gi] + bs_ref[:, gi:gi + 1]) * _gelu_grad(zu[:, gs[gi]])).astype(ACT)
                dws_ref[:, gs[gi]] += _causal(dws_parts[gi])
                dbs_ref[...] += jnp.where(lane == gi, jnp.sum(dmix[:, gs[gi]], axis=-1, keepdims=True), 0.0)
            dvn = jnp.concatenate(dvn_parts, axis=-1)
            dlg_ref[...] += jnp.sum(dvn * vh, axis=0, keepdims=True)
            dlb_ref[...] += jnp.sum(dvn, axis=0, keepdims=True)
            dvh = dvn * lg_ref[...]
            dv = rs * (dvh - jnp.mean(dvh, axis=-1, keepdims=True) - vh * jnp.mean(dvh * vh, axis=-1, keepdims=True))
            dz_ref[rows, d:] = (dv * _gelu_grad(zv)).astype(ACT)

    vec = _full((1, d))
    return _call(
        body, "sgu_mix_bwd", (t // tm,),
        [_rows(tm, n), _rows(tm, d), vec, vec, _full((groups, WINDOW, WINDOW)), _full((WINDOW, groups))],
        [_rows(tm, n), vec, vec, _full((WINDOW, d)), _full((WINDOW, groups))],
        [_sds((t, n), ACT), _sds((1, d), F32), _sds((1, d), F32), _sds((WINDOW, d), F32), _sds((WINDOW, groups), F32)],
        (z, dy, lg, lb, ws, bs_t), ("arbitrary",), jobs=jobs)


AG_COPIES = 8


def _prepare(sources, dtypes, n_gather, name, ahead=None):
    n = len(sources)
    arrays, where = [], []
    for parts, layer in sources:
        found = []
        for part in parts:
            known = [i for i, v in enumerate(arrays) if v is part]
            if not known:
                arrays.append(part)
            found.append(known[0] if known else len(arrays) - 1)
        where.append((found, layer))
    shapes = [(sum(p.shape[1] for p in parts), parts[0].shape[2]) for parts, _ in sources]
    later = [i for i in range(len(arrays)) if all(i not in found for found, _ in where[:n_gather])]
    n_sems = 6 if ahead is not None else 3

    def body(*refs):
        ins, refs = list(refs[:len(arrays)]), refs[len(arrays):]
        stage, outs = refs[:n], refs[n:n + n_gather]
        sems = refs[len(refs) - n_sems - len(later) - 1:len(refs) - len(later) - 1]
        send, recv, local_sem = sems[:3]
        fetches = []
        for j, i in enumerate(later):
            held = refs[len(refs) - len(later) - 1 + j]
            fetches.append(pltpu.make_async_copy(ins[i], held, refs[-1].at[j]))
            fetches[-1].start()
            ins[i] = held
        x, y, c = _place()
        me, sibling, beside_x, beside_y, far = (x, y, c), (x, y, 1 - c), (1 - x, y, c), (x, 1 - y, c), (1 - x, 1 - y, c)
        slot = lambda dev: 4 * dev[0] + 2 * dev[1] + dev[2]
        other = lambda dev: (dev[0], dev[1], 1 - c)
        whole = lambda a: (0, shapes[a][0])
        cuts = [rows // 2 if rows % 32 == 0 else rows for rows, _ in shapes]
        first = lambda a: (0, cuts[a])
        rest = lambda a: (cuts[a], shapes[a][0] - cuts[a])

        def copy(a, k, block, rows, to, src=None):
            dst = outs[a].at[block, pl.ds(*rows)]
            return pltpu.make_async_remote_copy(
                src_ref=dst if src is None else src, dst_ref=dst, send_sem=send.at[a * AG_COPIES + k],
                recv_sem=recv.at[a * AG_COPIES + k], device_id=to, device_id_type=MESH)

        def cast(a):
            found, layer = where[a]
            at = 0
            for i in found:
                rows = arrays[i].shape[1]
                stage[a][at:at + rows, :] = ins[i][layer].astype(dtypes[a])
                at += rows

        for a in range(n_gather):
            cast(a)
        started, ahead_sends = [], []
        for a in range(n_gather):
            own = pltpu.make_async_copy(stage[a], outs[a].at[slot(me)], local_sem.at[a])
            own.start()
            started.append(own)
        sends = []
        for a in range(n_gather):
            sends += [copy(a, k, slot(me), whole(a), to, src=stage[a]) for k, to in enumerate((sibling, beside_x, beside_y))]
        for cp in sends:
            cp.start()
        for cp in fetches:
            cp.wait()
        for a in range(n_gather, n):
            cast(a)
        if ahead is not None:
            block, pieces = ahead
            out, (send2, recv2, local2) = refs[n + n_gather], sems[3:]
            for i, piece in enumerate(pieces):
                src, dst = stage[block].at[pl.ds(*piece)], out.at[slot(me), pl.ds(*piece)]
                own = pltpu.make_async_copy(src, dst, local2.at[i])
                own.start()
                started.append(own)
                for k, to in enumerate((sibling, beside_x, beside_y)):
                    cp = pltpu.make_async_remote_copy(src_ref=src, dst_ref=dst, send_sem=send2.at[3 * i + k],
                                                      recv_sem=recv2.at[3 * i + k], device_id=to, device_id_type=MESH)
                    cp.start()
                    ahead_sends.append(cp)
        for a in range(n_gather):
            copy(a, 1, slot(beside_x), whole(a), me).wait_recv()
            copy(a, 2, slot(beside_y), whole(a), me).wait_recv()
            passed = [copy(a, 3, slot(beside_x), first(a), beside_y), copy(a, 5, slot(beside_x), whole(a), sibling),
                      copy(a, 6, slot(beside_y), whole(a), sibling)]
            if rest(a)[1]:
                passed.append(copy(a, 4, slot(beside_y), rest(a), beside_x))
            for cp in passed:
                cp.start()
            sends += passed
        for a in range(n_gather):
            copy(a, 3, slot(far), first(a), me).wait_recv()
            if rest(a)[1]:
                copy(a, 4, slot(far), rest(a), me).wait_recv()
            passed = copy(a, 7, slot(far), whole(a), sibling)
            passed.start()
            sends.append(passed)
        for a in range(n_gather):
            for k, source in ((0, me), (5, beside_x), (6, beside_y), (7, far)):
                copy(a, k, slot(other(source)), whole(a), me).wait_recv()
        for cp in sends:
            cp.wait_send()
        for cp in ahead_sends:
            cp.wait()
        for own in started:
            own.wait()

    gathered = list(range(n_gather)) + ([ahead[0]] if ahead is not None else [])
    sems = [pltpu.SemaphoreType.DMA((n_gather * AG_COPIES,)), pltpu.SemaphoreType.DMA((n_gather * AG_COPIES,)),
            pltpu.SemaphoreType.DMA((n_gather,))]
    if ahead is not None:
        sems += [pltpu.SemaphoreType.DMA((3 * len(ahead[1]),)), pltpu.SemaphoreType.DMA((3 * len(ahead[1]),)),
                 pltpu.SemaphoreType.DMA((len(ahead[1]),))]
    res = pl.pallas_call(
        body, name=name,
        in_specs=[ANY if i in later else IN_VMEM for i in range(len(arrays))], out_specs=[IN_VMEM] * n + [ANY] * len(gathered),
        out_shape=[_sds(s, dt) for s, dt in zip(shapes, dtypes)] + [_sds((N_DEV,) + shapes[a], dtypes[a]) for a in gathered],
        scratch_shapes=sems + [pltpu.VMEM(arrays[i].shape, arrays[i].dtype) for i in later] + [pltpu.SemaphoreType.DMA((len(later),))],
        compiler_params=pltpu.CompilerParams(vmem_limit_bytes=VMEM_LIMIT_BYTES),
    )(*arrays)
    return list(res[:n]), list(res[n:])


def _pair_sum(g, r, core, name):
    _, _, rows, cols = g.shape
    tr = _pick(rows, (512, 256, 128))

    def body(core_ref, g_ref, r_ref, p_ref):
        del core_ref
        p_ref[...] = (g_ref[...].astype(F32) + r_ref[...].astype(F32)).astype(p_ref.dtype)

    return pl.pallas_call(
        body, name=name,
        grid_spec=pltpu.PrefetchScalarGridSpec(
            num_scalar_prefetch=1, grid=(4, rows // tr),
            in_specs=[pl.BlockSpec((None, None, tr, cols), lambda q, i, core_ref: (q, core_ref[0], i, 0)),
                      pl.BlockSpec((None, tr, cols), lambda q, i, core_ref: (q, i, 0))],
            out_specs=pl.BlockSpec((None, tr, cols), lambda q, i, core_ref: (q, i, 0))),
        out_shape=_sds((4, rows, cols), g.dtype),
        compiler_params=_params("parallel", "parallel"),
    )(core, g, r)


def _adam(w, g, m, v):
    m2 = ADAM_B1 * m + (1.0 - ADAM_B1) * g
    v2 = ADAM_B2 * v + (1.0 - ADAM_B2) * (g * g)
    m_hat = m2 / (1.0 - ADAM_B1 ** ADAM_STEP)
    v_hat = v2 / (1.0 - ADAM_B2 ** ADAM_STEP)
    return -ADAM_LR * (m_hat / (jnp.sqrt(v_hat) + ADAM_EPS) + ADAM_WD * w), m2, v2


def _shard_update(own, index, received, w, m, v, layer, so_far, name):
    _, rows, cols = w.shape
    n_recv = received.shape[0]
    tr = _shard_tile(rows)

    def body(index_ref, p_ref, q_ref, w_ref, m_ref, v_ref, *rest):
        del index_ref
        g_out, d_out, m_out, v_out = rest[-4:]
        g = p_ref[...].astype(F32)
        for s in range(n_recv):
            g = g + q_ref[s].astype(F32)
        g_out[...] = g
        d_out[...], m_out[...], v_out[...] = _adam(w_ref[...], g, m_ref[...], v_ref[...])

    tile = pl.BlockSpec((None, tr, cols), lambda i, index_ref: (layer, i, 0))
    kept = list(so_far) if so_far is not None else []
    return pl.pallas_call(
        body, name=name,
        grid_spec=pltpu.PrefetchScalarGridSpec(
            num_scalar_prefetch=1, grid=(rows // tr,),
            in_specs=[pl.BlockSpec((None, tr, cols), lambda i, index_ref: (index_ref[0], i, 0)),
                      pl.BlockSpec((n_recv, tr, cols), lambda i, index_ref: (0, i, 0)), tile, tile, tile] + [ANY] * len(kept),
            out_specs=[tile] * 4),
        out_shape=[_sds(w.shape, F32)] * 4,
        input_output_aliases={6 + i: i for i in range(len(kept))},
        compiler_params=_params("parallel"),
    )(index, own, received, w, m, v, *kept)


def _natural_pieces(shape, r, c):
    if tuple(shape[-2:]) == (r, c) and all(n == 1 for n in shape[:-2]):
        return [((0,) * (len(shape) - 2), (0, c))]
    groups, width = shape[1], shape[3]
    assert len(shape) == 4 and shape[0] == 1 and shape[2] == r and groups * width == c, (shape, r, c)
    return [((0, g), (g * width, width)) for g in range(groups)]


def _replicated_update(shares, where, params, name):
    flat = [a for p in params if p is not None for a in p]

    def body(s_ref, *refs):
        ins, outs = refs[:len(flat)], refs[len(flat):]
        total = s_ref[0]
        for dev in range(1, N_DEV):
            total = total + s_ref[dev]
        i_at = o_at = 0
        for ranges, p in zip(where, params):
            chunks = [total[r0:r0 + r, :c] for r0, r, c in ranges]
            g2d = chunks[0] if len(chunks) == 1 else jnp.concatenate(chunks, axis=1)
            if p is None:
                outs[o_at][...] = g2d
                o_at += 1
                continue
            w_ref, m_ref, v_ref = ins[i_at:i_at + 3]
            for idx, (c0, cols) in _natural_pieces(p[0].shape, *g2d.shape):
                at = idx + (slice(None), slice(None))
                g = g2d[:, c0:c0 + cols]
                outs[o_at][at] = g
                outs[o_at + 1][at], outs[o_at + 2][at], outs[o_at + 3][at] = _adam(w_ref[at], g, m_ref[at], v_ref[at])
            i_at, o_at = i_at + 3, o_at + 4

    out_shape = []
    for ranges, p in zip(where, params):
        out_shape += [_sds((ranges[0][1], sum(c for _, _, c in ranges)), F32)] if p is None else [_sds(p[0].shape, F32)] * 4
    res = pl.pallas_call(
        body, name=name, out_shape=out_shape, compiler_params=pltpu.CompilerParams(vmem_limit_bytes=VMEM_LIMIT_BYTES),
    )(shares, *flat)
    out, at = [], 0
    for p in params:
        out.append(list(res[at:at + (1 if p is None else 4)]))
        at += 1 if p is None else 4
    return out


def _rope_tables(t):
    half = HEAD_DIM // 2
    inv_freq = np.float32(ROPE_THETA) ** (-(np.arange(half, dtype=np.float32) * np.float32(2.0)) / np.float32(HEAD_DIM))
    ang = np.arange(t, dtype=np.float32)[:, None] * inv_freq[None, :].astype(np.float32)
    cos, sin = np.cos(ang).astype(np.float32), np.sin(ang).astype(np.float32)
    return (jnp.asarray(np.tile(np.concatenate([cos, cos], axis=1), (1, 2))),
            jnp.asarray(np.tile(np.concatenate([-sin, sin], axis=1), (1, 2))))


def _rows_full(gathered):
    return gathered.reshape(-1, gathered.shape[-1])


def _rows_blocked(full):
    return full.reshape(N_DEV, full.shape[0] // N_DEV, full.shape[1]).astype(ACT)


def _pack_rows(parts, width):
    rows, where, at = [], [], 0
    for v in parts:
        r, c = v.shape
        n_rows = -(-r // 8) * 8
        chunks = []
        for c0 in range(0, c, width):
            chunk = v[:, c0:c0 + width].astype(F32)
            rows.append(jnp.pad(chunk, ((0, n_rows - r), (0, width - chunk.shape[1]))))
            chunks.append((at, r, chunk.shape[1]))
            at += n_rows
        where.append(chunks)
    return jnp.concatenate(rows, axis=0), where


def _halves(block):
    half = block.shape[0] // 2
    return (0, half), (half, half)


def _whole(block):
    return (0, block.shape[0])


EARLY = ("attn_w_qkv", "sgu_ln", "attn_w_o")
LATE = ("sgu_w_in", "sgu_w_out", "gu0", "gu1", "dn0", "dn1")
SWAPPED = ("attn_w_qkv", "ffn_w_gate_up")
BEFORE_ATTN = ("norm_mix_post", "norm_ffn_pre", "norm_ffn_post", "attn_b_o", "sgu_w_spatial", "sgu_b_spatial")
AFTER_ATTN = ("attn_b_qkv", "attn_sinks")
LAST = ("norm_mix_pre",)
WEIGHTS = ("norm_mix_pre", "norm_mix_post", "norm_ffn_pre", "norm_ffn_post", "attn_w_qkv", "attn_b_qkv", "attn_sinks", "attn_w_o",
           "attn_b_o", "sgu_w_in", "sgu_ln_g", "sgu_ln_b", "sgu_w_spatial", "sgu_b_spatial", "sgu_w_out", "ffn_w_gate_up", "ffn_w_down")


LAYER_OF = {"gu0": ("ffn_w_gate_up", 0), "gu1": ("ffn_w_gate_up", 1), "dn0": ("ffn_w_down", 0), "dn1": ("ffn_w_down", 1)}


def _source(tree, name):
    if name == "sgu_ln":
        return [tree["sgu_ln_g"][None], tree["sgu_ln_b"][None]], 0
    leaf, layer = LAYER_OF.get(name, (name, 0))
    return [tree[leaf]], layer


def kernel(x, norm_mix_pre, norm_mix_post, norm_ffn_pre, norm_ffn_post, attn_w_qkv, attn_b_qkv, attn_sinks, attn_w_o, attn_b_o, sgu_w_in, sgu_ln_g, sgu_ln_b, sgu_w_spatial, sgu_b_spatial, sgu_w_out, ffn_w_gate_up, ffn_w_down, loss_target, m_norm_mix_pre, m_norm_mix_post, m_norm_ffn_pre, m_norm_ffn_post, m_attn_w_qkv, m_attn_b_qkv, m_attn_sinks, m_attn_w_o, m_attn_b_o, m_sgu_w_in, m_sgu_ln_g, m_sgu_ln_b, m_sgu_w_spatial, m_sgu_b_spatial, m_sgu_w_out, m_ffn_w_gate_up, m_ffn_w_down, v_norm_mix_pre, v_norm_mix_post, v_norm_ffn_pre, v_norm_ffn_post, v_attn_w_qkv, v_attn_b_qkv, v_attn_sinks, v_attn_w_o, v_attn_b_o, v_sgu_w_in, v_sgu_ln_g, v_sgu_ln_b, v_sgu_w_spatial, v_sgu_b_spatial, v_sgu_w_out, v_ffn_w_gate_up, v_ffn_w_down):
    w = dict(norm_mix_pre=norm_mix_pre, norm_mix_post=norm_mix_post, norm_ffn_pre=norm_ffn_pre, norm_ffn_post=norm_ffn_post,
             attn_w_qkv=attn_w_qkv, attn_b_qkv=attn_b_qkv, attn_sinks=attn_sinks, attn_w_o=attn_w_o, attn_b_o=attn_b_o,
             sgu_w_in=sgu_w_in, sgu_ln_g=sgu_ln_g, sgu_ln_b=sgu_ln_b, sgu_w_spatial=sgu_w_spatial, sgu_b_spatial=sgu_b_spatial,
             sgu_w_out=sgu_w_out, ffn_w_gate_up=ffn_w_gate_up, ffn_w_down=ffn_w_down)
    mom = dict(norm_mix_pre=m_norm_mix_pre, norm_mix_post=m_norm_mix_post, norm_ffn_pre=m_norm_ffn_pre, norm_ffn_post=m_norm_ffn_post,
               attn_w_qkv=m_attn_w_qkv, attn_b_qkv=m_attn_b_qkv, attn_sinks=m_attn_sinks, attn_w_o=m_attn_w_o, attn_b_o=m_attn_b_o,
               sgu_w_in=m_sgu_w_in, sgu_ln_g=m_sgu_ln_g, sgu_ln_b=m_sgu_ln_b, sgu_w_spatial=m_sgu_w_spatial,
               sgu_b_spatial=m_sgu_b_spatial, sgu_w_out=m_sgu_w_out, ffn_w_gate_up=m_ffn_w_gate_up, ffn_w_down=m_ffn_w_down)
    var = dict(norm_mix_pre=v_norm_mix_pre, norm_mix_post=v_norm_mix_post, norm_ffn_pre=v_norm_ffn_pre, norm_ffn_post=v_norm_ffn_post,
               attn_w_qkv=v_attn_w_qkv, attn_b_qkv=v_attn_b_qkv, attn_sinks=v_attn_sinks, attn_w_o=v_attn_w_o, attn_b_o=v_attn_b_o,
               sgu_w_in=v_sgu_w_in, sgu_ln_g=v_sgu_ln_g, sgu_ln_b=v_sgu_ln_b, sgu_w_spatial=v_sgu_w_spatial,
               sgu_b_spatial=v_sgu_b_spatial, sgu_w_out=v_sgu_w_out, ffn_w_gate_up=v_ffn_w_gate_up, ffn_w_down=v_ffn_w_down)
    w, mom, var = [{**tree, **{n: jnp.swapaxes(tree[n], 1, 2) for n in SWAPPED}} for tree in (w, mom, var)]
    xs, target = x[0], loss_target[0]
    t, d = xs.shape
    row = lambda v: v.reshape(1, -1).astype(F32)
    core = lax.axis_index("c").astype(jnp.int32).reshape(1)
    chip = (2 * lax.axis_index("x") + lax.axis_index("y")).astype(jnp.int32).reshape(1)
    names = EARLY + LATE
    gu_rows = w["ffn_w_gate_up"].shape[1] // 4
    gu0_parts = [(i * gu_rows, gu_rows) for i in range(4)]
    staged, (*early, g_gu0) = _prepare([_source(w, n) for n in names], [F32 if n == "sgu_ln" else ACT for n in names], len(EARLY),
                                       "weights_prepare", ahead=(names.index("gu0"), gu0_parts[:2]))
    stage = dict(zip(names, staged))
    w_qkv = _rows_full(early[0])
    lg, lb = row(early[1][:, 0, :]), row(early[1][:, 1, :])
    w_o = _rows_full(early[2])
    b_qkv = row(attn_b_qkv)
    sinks = attn_sinks.reshape(1, -1).astype(F32)
    ws = sgu_w_spatial[0].astype(F32)
    bs_t = sgu_b_spatial[0].astype(F32).T
    cos, sin = _rope_tables(t)
    gather = lambda name, so_far, **pieces: _gather_job(stage[name], so_far, **pieces)
    a_half = lambda name: _halves(stage[name])[0]
    b_half = lambda name: _halves(stage[name])[1]
    whole = lambda name: _whole(stage[name])

    def pieces(name, n):
        rows = stage[name].shape[0] // n
        return [(i * rows, rows) for i in range(n)]

    ways = lambda parts: [(piece, i % 2) for i, piece in enumerate(parts)]
    relay = lambda name, so_far, **steps: _relay_gather_job(stage[name], so_far, **steps)
    gu1_parts = pieces("gu1", 4)
    dn0_parts, dn1_parts, in_parts, out_parts = pieces("dn0", 2), pieces("dn1", 2), pieces("sgu_w_in", 2), pieces("sgu_w_out", 2)
    (h0, q, kdup, vdup), ((g_gu0,),) = _attn_qkv_fwd(
        xs, row(norm_mix_pre[0]), w_qkv, b_qkv, cos, sin, jobs=[relay("gu0", g_gu0, sends=gu0_parts[2:])])
    (o,), ((g_gu0,), (g_dn0,)) = _attn_fwd(q, kdup, vdup, sinks, jobs=[
        relay("gu0", g_gu0, relays=ways(gu0_parts), forwards=gu0_parts), relay("dn0", None, sends=dn0_parts)])
    (m0, x1), ((g_gu0,), (g_dn0,), (g_in,)) = _proj_res(o, w_o, row(attn_b_o), row(norm_mix_post[0]), xs, "attn_out_fwd", jobs=[
        relay("gu0", g_gu0, far=gu0_parts), relay("dn0", g_dn0, relays=ways(dn0_parts), forwards=dn0_parts, late_far=dn0_parts),
        relay("sgu_w_in", None, sends=in_parts)])
    w_gu0, w_dn0 = _rows_full(g_gu0), _rows_full(g_dn0)
    (h1, gu0, a0, f0, x2), ((g_in,), (g_out,), (g_gu1,), (g_dn1,)) = _ffn_fwd(
        x1, row(norm_ffn_pre[0]), w_gu0, w_dn0, row(norm_ffn_post[0]), jobs=[
            relay("sgu_w_in", g_in, relays=ways(in_parts), forwards=in_parts, late_far=in_parts),
            relay("sgu_w_out", None, sends=out_parts), relay("gu1", None, sends=gu1_parts), relay("dn1", None, sends=dn1_parts)])
    w_in = g_in
    (h2, z, y), ((g_out,), (g_gu1,), (g_dn1,)) = _sgu_fwd(x2, row(norm_mix_pre[1]), w_in, lg, lb, ws, bs_t, jobs=[
        relay("sgu_w_out", g_out, relays=ways(out_parts), forwards=out_parts, late_far=out_parts),
        relay("gu1", g_gu1, relays=ways(gu1_parts), forwards=gu1_parts),
        relay("dn1", g_dn1, relays=ways(dn1_parts), forwards=dn1_parts)])
    w_out = _rows_full(g_out)
    (m1, x3), ((g_gu1,), (g_dn1,)) = _proj_res(y, w_out, None, row(norm_mix_post[1]), x2, "sgu_out_fwd", jobs=[
        relay("gu1", g_gu1, far=gu1_parts), relay("dn1", g_dn1, far=dn1_parts)])
    w_gu1, w_dn1 = _rows_full(g_gu1), _rows_full(g_dn1)

    to_sibling = lambda g: _scatter_job([g], 4, _to_sibling)
    pair = lambda g, r, name: _pair_sum(g.reshape((4, 2) + g.shape[1:]), r, core, "pair_sum_" + name)
    to_chips = lambda p, prev=None, piece=None: _scatter_job([p], 3, _to_owner_chip, prev=prev, piece=piece)
    out_g, out_d, out_m, out_v = {}, {}, {}, {}

    trees = [{**tree, "sgu_ln": jnp.stack([tree["sgu_ln_g"], tree["sgu_ln_b"]], axis=1)} for tree in (w, mom, var)]
    so_far = {}

    def update(name, own, index, received):
        leaf, layer = LAYER_OF.get(name, (name, 0))
        so_far[leaf] = _shard_update(own, index, received, *[tree[leaf] for tree in trees], layer, so_far.get(leaf), "update_" + name)
        for out, val in zip((out_g, out_d, out_m, out_v), so_far[leaf]):
            out[leaf] = val

    def finish(names, n_extra, shares, where, name):
        res = _replicated_update(shares, where, [(w[n], mom[n], var[n]) for n in names] + [None] * n_extra, name)
        for n, vals in zip(names, res):
            out_g[n], out_d[n], out_m[n], out_v[n] = vals
        return [vals[0] for vals in res[len(names):]]

    first_half = lambda p: _halves(p[0])[0]
    second_half = lambda p: _halves(p[0])[1]
    (df1, dgu1, dx3, dg_ffn_post1, dg_ffn_pre1, loss_tile, h3, a1), _ = _ffn_bwd(
        None, None, row(norm_ffn_post[1]), w_dn1, None, w_gu1, x3, row(norm_ffn_pre[1]), "ffn_last", target=target)
    b_gu1, _ = _wgrad(h3, dgu1, "ffn_up_wgrad1", ACT, transposed=True)
    b_gu1 = _rows_blocked(b_gu1)
    dw_dn1, _ = _wgrad(a1, df1, "ffn_down_wgrad1", ACT)
    b_dn1 = _rows_blocked(dw_dn1)
    (dy, dw_out, dg_mix_post1), ((r_gu1,), (r_dn1,)) = _post_bwd(
        dx3, m1, row(norm_mix_post[1]), w_out, y, "sgu", "sgu_out_bwd", jobs=[to_sibling(b_gu1), to_sibling(b_dn1)])
    p_gu1, p_dn1 = pair(b_gu1, r_gu1, "gu1"), pair(b_dn1, r_dn1, "dn1")
    b_out = _rows_blocked(dw_out)
    (dz, dlg, dlb, dws, dbs_t), ((q_gu1,), (r_out,)) = _sgu_mix_bwd(z, dy, lg, lb, ws, bs_t, jobs=[
        to_chips(p_gu1, piece=first_half(p_gu1)), to_sibling(b_out)])
    p_out = pair(b_out, r_out, "sgu_w_out")
    b_ln = jnp.stack([dlg.reshape(N_DEV, -1), dlb.reshape(N_DEV, -1)], axis=1)
    (dx2, dg_mix_pre1, b_in), _ = _pre_bwd(dz, w_in, x2, row(norm_mix_pre[1]), dx3, "sgu_in_bwd", h=h2)
    (df0, dgu0, dx1, dg_ffn_post0, dg_ffn_pre0), ((q_gu1,), (q_dn1,), (q_out,), (r_in,), (r_ln,)) = _ffn_bwd(
        dx2, f0, row(norm_ffn_post[0]), w_dn0, gu0, w_gu0, x1, row(norm_ffn_pre[0]), "ffn_bwd0",
        jobs=[to_chips(p_gu1, prev=[q_gu1], piece=second_half(p_gu1)), to_chips(p_dn1), to_chips(p_out), to_sibling(b_in),
              to_sibling(b_ln)])
    update("gu1", p_gu1, chip, q_gu1)
    update("dn1", p_dn1, chip, q_dn1)
    update("sgu_w_out", p_out, chip, q_out)
    p_in, p_ln = pair(b_in, r_in, "sgu_w_in"), pair(b_ln, r_ln, "sgu_ln")
    dw_dn0, _ = _wgrad(a0, df0, "ffn_down_wgrad0", ACT)
    b_dn0 = _rows_blocked(dw_dn0)
    b_gu0, ((q_in,), (q_ln,), (r_dn0,)) = _wgrad(h1, dgu0, "ffn_up_wgrad0", ACT, transposed=True, jobs=[
        to_chips(p_in), to_chips(p_ln), to_sibling(b_dn0)])
    update("sgu_w_in", p_in, chip, q_in)
    update("sgu_ln", p_ln, chip, q_ln)
    b_gu0 = _rows_blocked(b_gu0)
    p_dn0 = pair(b_dn0, r_dn0, "dn0")
    (do, dw_o, dg_mix_post0, db_o), ((r_gu0,), (q_dn0,)) = _post_bwd(
        dx1, m0, row(norm_mix_post[0]), w_o, o, "attn", "attn_out_bwd", jobs=[to_sibling(b_gu0), to_chips(p_dn0, piece=first_half(p_dn0))])
    p_gu0 = pair(b_gu0, r_gu0, "gu0")
    b_o = _rows_blocked(dw_o)
    grads = {
        "norm_mix_post": jnp.concatenate([dg_mix_post0, dg_mix_post1], axis=0),
        "norm_ffn_pre": jnp.concatenate([dg_ffn_pre0, dg_ffn_pre1], axis=0),
        "norm_ffn_post": jnp.concatenate([dg_ffn_post0, dg_ffn_post1], axis=0),
        "attn_b_o": db_o,
        "sgu_w_spatial": dws,
        "sgu_b_spatial": dbs_t.T,
    }
    shares1, where1 = _pack_rows([grads[name] for name in BEFORE_ATTN], d)
    (dqkv, db_qkv, dsink), ((q_gu0,), (q_dn0,), (r_o,), (g_shares1,)) = _attn_bwd(q, kdup, vdup, do, sinks, cos, sin, jobs=[
        to_chips(p_gu0), to_chips(p_dn0, prev=[q_dn0], piece=second_half(p_dn0)), to_sibling(b_o),
        _gather_job(shares1, None, sends=[_whole(shares1)])])
    update("gu0", p_gu0, chip, q_gu0)
    update("dn0", p_dn0, chip, q_dn0)
    p_o = pair(b_o, r_o, "attn_w_o")
    grads.update({"attn_b_qkv": db_qkv, "attn_sinks": dsink[:1, :d // HEAD_DIM]})
    shares2, where2 = _pack_rows([grads[name] for name in AFTER_ATTN] + [loss_tile[:1, :1]], d)
    (dx0, dg_mix_pre0), _ = _pre_bwd(dqkv, w_qkv, xs, row(norm_mix_pre[0]), dx1, "attn_qkv_bwd")
    grads["norm_mix_pre"] = jnp.concatenate([dg_mix_pre0, dg_mix_pre1], axis=0)
    shares3, where3 = _pack_rows([grads[name] for name in LAST], d)
    dw_qkv, ((q_o,), (g_shares1,), (g_shares2,), (g_shares3,)) = _wgrad(h0, dqkv, "attn_qkv_wgrad", ACT, transposed=True, jobs=[
        to_chips(p_o), _gather_job(shares1, g_shares1, forwards=[_whole(shares1)]),
        _gather_job(shares2, None, sends=[_whole(shares2)]), _gather_job(shares3, None, sends=[_whole(shares3)])])
    update("attn_w_o", p_o, chip, q_o)
    b_qkv_grad = _rows_blocked(dw_qkv)
    (r_qkv,), (g_shares2,), (g_shares3,) = _copies_only([
        to_sibling(b_qkv_grad), _gather_job(shares2, g_shares2, forwards=[_whole(shares2)]),
        _gather_job(shares3, g_shares3, forwards=[_whole(shares3)])], "grads_tail_to_sibling")
    p_qkv = pair(b_qkv_grad, r_qkv, "attn_w_qkv")
    (q_qkv,), = _copies_only([to_chips(p_qkv)], "grads_tail_to_owner_chip")
    update("attn_w_qkv", p_qkv, chip, q_qkv)

    finish(BEFORE_ATTN, 0, g_shares1, where1, "replicated_update_before_attn")
    loss = finish(AFTER_ATTN, 1, g_shares2, where2, "replicated_update_after_attn")[0][0, 0]
    finish(LAST, 0, g_shares3, where3, "replicated_update_last")

    for out in (out_g, out_d, out_m, out_v):
        out["sgu_ln_g"], out["sgu_ln_b"] = out["sgu_ln"][:, 0, :], out["sgu_ln"][:, 1, :]
        for n in SWAPPED:
            out[n] = jnp.swapaxes(out[n], 1, 2)

    return (loss, dx0[None], *[out_g[n] for n in WEIGHTS], *[out_d[n] for n in WEIGHTS],
            *[out_m[n] for n in WEIGHTS], *[out_v[n] for n in WEIGHTS])
```

```python
import functools
import math
import operator

import jax
import jax.numpy as jnp
import numpy as np
from jax import lax
from jax.experimental import pallas as pl
from jax.experimental.pallas import tpu as pltpu

F32 = jnp.float32
ACT = jnp.bfloat16

EPS = 1e-6
HEAD_DIM = 64
PAIR = 2 * HEAD_DIM
GQA_GROUP = 4
WINDOW = 128
ROPE_THETA = 10000.0
SCALE = HEAD_DIM ** -0.5
MASKED = -1e30
LANES = 128
MXU_WIDTH = 256

ADAM_LR, ADAM_B1, ADAM_B2, ADAM_EPS, ADAM_WD, ADAM_STEP = 0.001, 0.9, 0.999, 1e-08, 0.01, 10

N_DEV = 8
VMEM_LIMIT_BYTES = 56 * 1024 * 1024
MESH = pl.DeviceIdType.MESH
ANY = pl.BlockSpec(memory_space=pl.ANY)
IN_VMEM = pl.BlockSpec(memory_space=pltpu.VMEM)


def _pick(n, candidates):
    for c in candidates:
        if n % c == 0:
            return c
    return n


def _row_tile(t, most=512):
    return _pick(t, (most,))


def _shard_tile(rows):
    return next((c for c in (512, 256, 176, 128, 96, 64) if rows % c == 0 and rows >= 2 * c), rows)


def _mxu_chunks(n, most=4 * MXU_WIDTH):
    assert n % MXU_WIDTH == 0 and most % MXU_WIDTH == 0
    return [(c, min(most, n - c)) for c in range(0, n, most)]


def _params(*sem):
    return pltpu.CompilerParams(dimension_semantics=sem, vmem_limit_bytes=VMEM_LIMIT_BYTES)


def _full(shape):
    return pl.BlockSpec(shape, lambda *_: (0,) * len(shape))


def _resident(shape):
    return pl.BlockSpec(shape, lambda *_: (0,) * len(shape), pipeline_mode=pl.Buffered(1))


def _rows(tm, n):
    return pl.BlockSpec((tm, n), lambda i: (i, 0))


def _sds(shape, dtype):
    return jax.ShapeDtypeStruct(shape, dtype)


def _place():
    return lax.axis_index("x"), lax.axis_index("y"), lax.axis_index("c")


def _other_chips(x, y):
    return [(1 - x, y), (x, 1 - y), (1 - x, 1 - y)]


class _Job:
    def __init__(self, inputs, out_shape, aliases, emit, n_remote, n_local=0, n_late=0):
        self.inputs, self.out_shape, self.aliases, self.emit = list(inputs), list(out_shape), dict(aliases), emit
        self.n_remote, self.n_local, self.n_late = n_remote, n_local, n_late


def _call(body, name, grid, in_specs, out_specs, out_shape, args, sem, scratch=(), jobs=()):
    n_in, n_out, n_scr = len(args), len(out_shape), len(scratch)
    j_in = [a for job in jobs for a in job.inputs]
    j_out = [s for job in jobs for s in job.out_shape]
    aliases, at_in, at_out = {}, n_in, n_out
    for job in jobs:
        aliases.update({at_in + i: at_out + o for i, o in job.aliases.items()})
        at_in, at_out = at_in + len(job.inputs), at_out + len(job.out_shape)
    n_remote = sum(job.n_remote for job in jobs)
    n_local = sum(job.n_local for job in jobs)

    def wrapped(*refs):
        ins, jin = refs[:n_in], refs[n_in:n_in + len(j_in)]
        rest = refs[n_in + len(j_in):]
        outs, jout = rest[:n_out], rest[n_out:n_out + len(j_out)]
        scr = rest[n_out + len(j_out):n_out + len(j_out) + n_scr]
        if not jobs:
            body(*ins, *outs, *scr)
            return
        send, recv, local = rest[n_out + len(j_out) + n_scr:]
        ids = [pl.program_id(a) for a in range(len(grid))]
        first = functools.reduce(operator.and_, [i == 0 for i in ids])
        last = functools.reduce(operator.and_, [i == g - 1 for i, g in zip(ids, grid)])

        def descriptors():
            place, found, i, o, r, l = _place(), ([], []), 0, 0, 0, 0
            for job in jobs:
                emitted = job.emit(jin[i:i + len(job.inputs)], jout[o:o + len(job.out_shape)], place)
                for at, (src, dst, to) in enumerate(emitted):
                    if to is None:
                        cp = pltpu.make_async_copy(src, dst, local.at[l])
                        l += 1
                    else:
                        cp = pltpu.make_async_remote_copy(src_ref=src, dst_ref=dst, send_sem=send.at[r], recv_sem=recv.at[r],
                                                          device_id=to, device_id_type=MESH)
                        r += 1
                    found[at >= len(emitted) - job.n_late].append(cp)
                i, o = i + len(job.inputs), o + len(job.out_shape)
            return found

        @pl.when(first)
        def _():
            for cp in descriptors()[0]:
                cp.start()

        body(*ins, *outs, *scr)

        @pl.when(last)
        def _():
            early, late = descriptors()
            for cp in early:
                cp.wait()
            for cp in late:
                cp.start()
            for cp in late:
                cp.wait()

    sems = [pltpu.SemaphoreType.DMA((n_remote,)), pltpu.SemaphoreType.DMA((n_remote,)),
            pltpu.SemaphoreType.DMA((max(n_local, 1),))] if jobs else []
    res = pl.pallas_call(
        wrapped, name=name, grid=grid,
        in_specs=list(in_specs) + [ANY] * len(j_in), out_specs=list(out_specs) + [ANY] * len(j_out),
        out_shape=list(out_shape) + j_out, scratch_shapes=list(scratch) + sems, input_output_aliases=aliases,
        compiler_params=_params(*(("arbitrary",) * len(grid) if jobs else sem)),
    )(*args, *j_in)
    job_res, at = [], n_out
    for job in jobs:
        job_res.append(list(res[at:at + len(job.out_shape)]))
        at += len(job.out_shape)
    return list(res[:n_out]), job_res


def _copies_only(jobs, name):
    def body(o_ref):
        o_ref[...] = jnp.zeros_like(o_ref)

    return _call(body, name, (1,), [], [_full((8, LANES))], [_sds((8, LANES), F32)], (), ("arbitrary",), jobs=jobs)[1]


def _gather_job(stage, gathered, sends=(), forwards=()):
    shape = _sds((N_DEV,) + stage.shape, stage.dtype)
    inputs = ([stage] if sends else []) + ([gathered] if gathered is not None else [])
    aliases = {len(inputs) - 1: 0} if gathered is not None else {}

    def emit(ins, outs, place):
        x, y, c = place
        sibling, chips, mine, g = (x, y, 1 - c), _other_chips(x, y), 4 * x + 2 * y + c, outs[0]
        found = []
        for r0, nr in sends:
            src, dst = ins[0].at[pl.ds(r0, nr)], g.at[mine, pl.ds(r0, nr)]
            found += [(src, dst, None), (src, dst, sibling)] + [(src, dst, (px, py, c)) for px, py in chips]
        for r0, nr in forwards:
            for px, py in chips:
                block = 4 * px + 2 * py + c
                found.append((ins[-1].at[block, pl.ds(r0, nr)], g.at[block, pl.ds(r0, nr)], sibling))
        return found

    return _Job(inputs, [shape], aliases, emit, 4 * len(sends) + 3 * len(forwards), len(sends))


def _relay_gather_job(stage, gathered, sends=(), relays=(), forwards=(), far=(), late_far=()):
    shape = _sds((N_DEV,) + stage.shape, stage.dtype)
    inputs = ([stage] if sends else []) + ([gathered] if gathered is not None else [])
    aliases = {len(inputs) - 1: 0} if gathered is not None else {}

    def emit(ins, outs, place):
        x, y, c = place
        sibling, beside_x, beside_y, g = (x, y, 1 - c), (1 - x, y, c), (x, 1 - y, c), outs[0]
        slot = lambda dev: 4 * dev[0] + 2 * dev[1] + dev[2]
        found = []
        for r0, nr in sends:
            src, dst = ins[0].at[pl.ds(r0, nr)], g.at[slot((x, y, c)), pl.ds(r0, nr)]
            found += [(src, dst, None), (src, dst, sibling), (src, dst, beside_x), (src, dst, beside_y)]

        def passed_on(block, piece, to):
            return ins[-1].at[block, pl.ds(*piece)], g.at[block, pl.ds(*piece)], to

        for piece, way in relays:
            found.append(passed_on(slot(beside_x), piece, beside_y) if way == 0 else passed_on(slot(beside_y), piece, beside_x))
        for piece in forwards:
            found += [passed_on(slot(beside_x), piece, sibling), passed_on(slot(beside_y), piece, sibling)]
        for piece in tuple(far) + tuple(late_far):
            found.append(passed_on(slot((1 - x, 1 - y, c)), piece, sibling))
        return found

    n_remote = 3 * len(sends) + len(relays) + 2 * len(forwards) + len(far) + len(late_far)
    return _Job(inputs, [shape], aliases, emit, n_remote, len(sends), len(late_far))


def _scatter_job(arrays, n_slots, route, prev=None, piece=None):
    shapes = [_sds((n_slots,) + v.shape[1:], v.dtype) for v in arrays]
    inputs = list(arrays) + (list(prev) if prev else [])
    aliases = {len(arrays) + i: i for i in range(len(arrays))} if prev else {}

    def emit(ins, outs, place):
        found = []
        for a in range(len(arrays)):
            for s in range(n_slots):
                block, to = route(s, *place)
                if piece is None:
                    found.append((ins[a].at[block], outs[a].at[s], to))
                else:
                    found.append((ins[a].at[block, pl.ds(*piece)], outs[a].at[s, pl.ds(*piece)], to))
        return found

    return _Job(inputs, shapes, aliases, emit, len(arrays) * n_slots)


def _to_sibling(s, x, y, c):
    return 2 * s + (1 - c), (x, y, 1 - c)


def _to_owner_chip(s, x, y, c):
    px, py = _other_chips(x, y)[s]
    return 2 * px + py, (px, py, c)


def _rstd(x):
    return lax.rsqrt(jnp.mean(x * x, axis=-1, keepdims=True) + EPS)


def _rms_bwd(xhat, r, g, dy):
    dxh = dy * g
    return r * (dxh - xhat * jnp.mean(dxh * xhat, axis=-1, keepdims=True))


def _first_half(rows):
    lane = lax.broadcasted_iota(jnp.int32, (rows, PAIR), 1)
    return (lane % HEAD_DIM) < (HEAD_DIM // 2)


def _rot_half(v, first):
    return jnp.where(first, pltpu.roll(v, PAIR - HEAD_DIM // 2, 1), pltpu.roll(v, HEAD_DIM // 2, 1))


def _rope(v, cos, sin, first):
    return v * cos + _rot_half(v, first) * sin


def _rope_t(dv, cos, sin, first):
    return dv * cos + _rot_half(dv * sin, first)


def _dot(a, b):
    return jnp.dot(a, b, preferred_element_type=F32)


def _dot_nt(a, b):
    return lax.dot_general(a, b, (((1,), (1,)), ((), ())), preferred_element_type=F32)


def _dot_tn(a, b):
    return lax.dot_general(a, b, (((0,), (0,)), ((), ())), preferred_element_type=F32)


def _sigmoid(v):
    return 1.0 / (1.0 + jnp.exp(-v))


_GELU_K = math.sqrt(2.0 / math.pi)
_GELU_C = 0.044715


def _gelu(v):
    return v * (0.5 * (1.0 + jnp.tanh(_GELU_K * (v + _GELU_C * (v * v * v)))))


def _gelu_grad(v):
    t = jnp.tanh(_GELU_K * (v + _GELU_C * (v * v * v)))
    return 0.5 * (1.0 + t) + 0.5 * v * (1.0 - t * t) * (_GELU_K * (1.0 + 3.0 * _GELU_C * (v * v)))


def _attn_qkv_fwd(x, g, w, b, cos, sin, jobs=()):
    t, d = x.shape
    n = w.shape[0]
    kd = n - d
    assert (kd // 2) % PAIR == 0
    tm = _row_tile(t)
    ch = _pick(d, (512,))

    def body(x_ref, g_ref, w_ref, b_ref, cos_ref, sin_ref, h_ref, q_ref, k_ref, v_ref):
        xv = x_ref[...]
        hb = (xv * _rstd(xv) * g_ref[...]).astype(ACT)
        h_ref[...] = hb
        cosv, sinv = cos_ref[...], sin_ref[...]
        first = _first_half(tm)
        left = _lane_halves()[0]

        def proj(lo, width):
            return _dot_nt(hb, w_ref[lo:lo + width, :]) + b_ref[:, lo:lo + width]

        def twice(ref, s, two_heads):
            swapped = pltpu.roll(two_heads, HEAD_DIM, 1)
            ref[:, 2 * s:2 * s + PAIR] = jnp.where(left, two_heads, swapped).astype(ACT)
            ref[:, 2 * s + PAIR:2 * s + 2 * PAIR] = jnp.where(left, swapped, two_heads).astype(ACT)

        for c in range(0, d, ch):
            y = proj(c, ch)
            for s in range(0, ch, PAIR):
                q_ref[:, c + s:c + s + PAIR] = (_rope(y[:, s:s + PAIR], cosv, sinv, first) * SCALE).astype(ACT)
        y = proj(d, kd)
        for s in range(0, kd // 2, PAIR):
            twice(k_ref, s, _rope(y[:, s:s + PAIR], cosv, sinv, first))
            twice(v_ref, s, y[:, kd // 2 + s:kd // 2 + s + PAIR])

    return _call(
        body, "attn_qkv_fwd", (t // tm,),
        [_rows(tm, d), _full((1, d)), _full((n, d)), _full((1, n)), _rows(tm, PAIR), _rows(tm, PAIR)],
        [_rows(tm, d), _rows(tm, d), _rows(tm, kd), _rows(tm, kd)],
        [_sds((t, d), ACT), _sds((t, d), ACT), _sds((t, kd), ACT), _sds((t, kd), ACT)],
        (x, g, w, b, cos, sin), ("parallel",), jobs=jobs)


STACK = GQA_GROUP * WINDOW


def _band_mask(has_prev):
    row = lax.broadcasted_iota(jnp.int32, (STACK, 2 * WINDOW), 0) % WINDOW
    col = lax.broadcasted_iota(jnp.int32, (STACK, 2 * WINDOW), 1)
    return ((col < WINDOW) & (col > row) & has_prev) | ((col >= WINDOW) & (col - WINDOW <= row))


def _lane_halves():
    lane = lax.broadcasted_iota(jnp.int32, (1, PAIR), 1)
    return lane < HEAD_DIM, lane >= HEAD_DIM


def _stack_heads(ref, h, halves):
    parts = []
    for p in range(2):
        pair = ref[:, (2 * h + p) * PAIR:(2 * h + p + 1) * PAIR]
        parts += [jnp.where(half, pair, jnp.zeros_like(pair)) for half in halves]
    return jnp.concatenate(parts, axis=0)


def _sink_column(sink_ref, h):
    row = lax.broadcasted_iota(jnp.int32, (STACK, 1), 0)
    col = jnp.full((STACK, 1), sink_ref[0, GQA_GROUP * h + GQA_GROUP - 1], F32)
    for j in range(GQA_GROUP - 2, -1, -1):
        col = jnp.where(row < (j + 1) * WINDOW, sink_ref[0, GQA_GROUP * h + j], col)
    return col


def _probs(scores, valid, sink):
    s = jnp.where(valid, scores, MASKED)
    m = jnp.maximum(jnp.max(s, axis=-1, keepdims=True), sink)
    p = jnp.exp(s - m)
    psink = jnp.exp(sink - m)
    inv = 1.0 / (jnp.sum(p, axis=-1, keepdims=True) + psink)
    return p * inv, psink * inv


def _attn_fwd(q, kd_, vd, sinks, jobs=()):
    t, d = q.shape
    kd = kd_.shape[1]
    cur = lambda n: (n, 0)
    prev = lambda n: (jnp.maximum(n - 1, 0), 0)

    def body(sink_ref, q_ref, kc_ref, kp_ref, vc_ref, vp_ref, o_ref):
        valid = _band_mask(pl.program_id(0) > 0)
        halves = _lane_halves()
        heads = range(kd // PAIR)
        hs = [slice(h * PAIR, (h + 1) * PAIR) for h in heads]
        scores = [_dot_nt(_stack_heads(q_ref, h, halves), jnp.concatenate([kp_ref[:, hs[h]], kc_ref[:, hs[h]]], axis=0)) for h in heads]
        probs = [_probs(scores[h], valid, _sink_column(sink_ref, h))[0].astype(ACT) for h in heads]
        for h in heads:
            v2 = jnp.concatenate([vp_ref[:, hs[h]], vc_ref[:, hs[h]]], axis=0)
            vs = jnp.concatenate([jnp.where(half, v2, jnp.zeros_like(v2)) for half in halves], axis=0)
            pr = probs[h]
            for p in range(2):
                both = jnp.concatenate([pr[2 * p * WINDOW:(2 * p + 1) * WINDOW], pr[(2 * p + 1) * WINDOW:(2 * p + 2) * WINDOW]], axis=1)
                o_ref[:, (2 * h + p) * PAIR:(2 * h + p + 1) * PAIR] = _dot(both, vs).astype(ACT)

    kv_c, kv_p = pl.BlockSpec((WINDOW, kd), cur), pl.BlockSpec((WINDOW, kd), prev)
    return _call(
        body, "attn_fwd", (t // WINDOW,),
        [pl.BlockSpec(memory_space=pltpu.SMEM), pl.BlockSpec((WINDOW, d), cur), kv_c, kv_p, kv_c, kv_p],
        [pl.BlockSpec((WINDOW, d), cur)], [_sds((t, d), ACT)],
        (sinks, q, kd_, kd_, vd, vd), ("parallel",), jobs=jobs)


def _attn_bwd(q, kd_, vd, do, sinks, cos, sin, jobs=()):
    t, d = q.shape
    kd = kd_.shape[1]
    nb = t // WINDOW
    n_out = d + kd
    assert (kd // 2) % PAIR == 0
    cur = lambda n: (jnp.minimum(n, nb - 1), 0)
    prev = lambda n: (jnp.maximum(jnp.minimum(n, nb - 1) - 1, 0), 0)
    late = lambda n: (jnp.maximum(n - 1, 0), 0)

    def body(sink_ref, q_ref, kc_ref, kp_ref, vc_ref, vp_ref, do_ref, cosc_ref, sinc_ref, cosp_ref, sinp_ref,
             out_ref, db_ref, dsink_ref, dq_keep, dk_keep, dv_keep):
        n = pl.program_id(0)

        @pl.when(n == 0)
        def _():
            for ref in (db_ref, dsink_ref, dq_keep, dk_keep, dv_keep):
                ref[...] = jnp.zeros_like(ref)

        valid = _band_mask(n > 0) & (n < nb)
        halves = _lane_halves()
        first = _first_half(WINDOW)
        slot = lax.broadcasted_iota(jnp.int32, dsink_ref.shape, 1)
        top = lax.broadcasted_iota(jnp.int32, dsink_ref.shape, 0) == 0
        dsink = jnp.zeros(dsink_ref.shape, F32)

        def emit(cols, done):
            out_ref[:, cols] = done.astype(ACT)
            db_ref[:, cols] += jnp.sum(done.astype(F32), axis=0, keepdims=True)

        heads = range(kd // PAIR)
        hs = [slice(h * PAIR, (h + 1) * PAIR) for h in heads]
        k2 = [jnp.concatenate([kp_ref[:, hs[h]], kc_ref[:, hs[h]]], axis=0) for h in heads]
        v2 = [jnp.concatenate([vp_ref[:, hs[h]], vc_ref[:, hs[h]]], axis=0) for h in heads]
        qs = [_stack_heads(q_ref, h, halves) for h in heads]
        dos = [_stack_heads(do_ref, h, halves) for h in heads]
        scores = [_dot_nt(qs[h], k2[h]) for h in heads]
        dps = [_dot_nt(dos[h], v2[h]) for h in heads]
        prs, dss = [], []
        for h in heads:
            pr, psink = _probs(scores[h], valid, _sink_column(sink_ref, h))
            delta = jnp.sum(pr * dps[h], axis=-1, keepdims=True)
            dss.append((pr * (dps[h] - delta)).astype(ACT))
            prs.append(pr.astype(ACT))
            leak = psink * delta
            for j in range(GQA_GROUP):
                share = jnp.sum(leak[j * WINDOW:(j + 1) * WINDOW], axis=0, keepdims=True)
                dsink = dsink - jnp.where(top & (slot == GQA_GROUP * h + j), share, 0.0)
        dqs = [_dot(dss[h], k2[h]) for h in heads]
        dk2 = [_dot_tn(dss[h], qs[h]) for h in heads]
        dv2 = [_dot_tn(prs[h], dos[h]) for h in heads]
        for h in heads:
            for p in range(2):
                ps = slice((2 * h + p) * PAIR, (2 * h + p + 1) * PAIR)
                dqp = jnp.where(halves[0], dqs[h][2 * p * WINDOW:(2 * p + 1) * WINDOW], dqs[h][(2 * p + 1) * WINDOW:(2 * p + 2) * WINDOW])
                emit(ps, dq_keep[:, ps])
                dq_keep[:, ps] = (_rope_t(dqp, cosc_ref[...], sinc_ref[...], first) * SCALE).astype(ACT)
        dks, dvs = [], []
        for h in heads:
            dks.append(dk_keep[:, hs[h]] + _rope_t(dk2[h][:WINDOW], cosp_ref[...], sinp_ref[...], first))
            dk_keep[:, hs[h]] = _rope_t(dk2[h][WINDOW:], cosc_ref[...], sinc_ref[...], first)
            dvs.append(dv_keep[:, hs[h]] + dv2[h][:WINDOW])
            dv_keep[:, hs[h]] = dv2[h][WINDOW:]
        both = lambda v: v + pltpu.roll(v, HEAD_DIM, 1)
        for h in range(0, len(heads), 2):
            at = h // 2 * PAIR
            emit(slice(d + at, d + at + PAIR), jnp.where(halves[0], both(dks[h]), both(dks[h + 1])))
            emit(slice(d + kd // 2 + at, d + kd // 2 + at + PAIR), jnp.where(halves[0], both(dvs[h]), both(dvs[h + 1])))
        dsink_ref[...] += dsink

    kv_c, kv_p = pl.BlockSpec((WINDOW, kd), cur), pl.BlockSpec((WINDOW, kd), prev)
    tab_c, tab_p = pl.BlockSpec((WINDOW, PAIR), cur), pl.BlockSpec((WINDOW, PAIR), prev)
    return _call(
        body, "attn_bwd", (nb + 1,),
        [pl.BlockSpec(memory_space=pltpu.SMEM), pl.BlockSpec((WINDOW, d), cur), kv_c, kv_p, kv_c, kv_p,
         pl.BlockSpec((WINDOW, d), cur), tab_c, tab_c, tab_p, tab_p],
        [pl.BlockSpec((WINDOW, n_out), late), _full((1, n_out)), _full((8, LANES))],
        [_sds((t, n_out), ACT), _sds((1, n_out), F32), _sds((8, LANES), F32)],
        (sinks, q, kd_, kd_, vd, vd, do, cos, sin, cos, sin), ("arbitrary",),
        scratch=[pltpu.VMEM((WINDOW, d), ACT), pltpu.VMEM((WINDOW, kd), F32), pltpu.VMEM((WINDOW, kd), F32)], jobs=jobs)


def _proj_res(a, w, b, g, x, name, jobs=()):
    t = a.shape[0]
    k, d = w.shape
    tm = _row_tile(t)
    has_bias = b is not None

    def body(*refs):
        a_ref, w_ref = refs[:2]
        b_ref = refs[2] if has_bias else None
        g_ref, x_ref, m_ref, xo_ref = refs[2 + has_bias:]
        m = _dot(a_ref[...], w_ref[...])
        if has_bias:
            m = m + b_ref[...]
        m_ref[...] = m.astype(ACT)
        xo_ref[...] = x_ref[...] + (m * _rstd(m)) * g_ref[...]

    ins = [a, w] + ([b] if has_bias else []) + [g, x]
    specs = [_rows(tm, k), _full((k, d))] + ([_full((1, d))] if has_bias else []) + [_full((1, d)), _rows(tm, d)]
    return _call(body, name, (t // tm,), specs, [_rows(tm, d), _rows(tm, d)], [_sds((t, d), ACT), _sds((t, d), F32)], ins,
                 ("parallel",), jobs=jobs)


def _post_bwd(dres, m, g, w, a, mode, name, jobs=()):
    t, d = dres.shape
    k = w.shape[0]
    tm = _row_tile(t)
    kc = _pick(k, (1408, 1024, 512))
    steps = t // tm

    def body(*refs):
        d_ref, m_ref, g_ref, w_ref, a_ref = refs[:5]
        da_ref, dw_ref, dg_ref = refs[5:8]
        db_ref, acc_ref = (refs[8] if mode == "attn" else None), refs[-1]
        i = pl.program_id(0)

        @pl.when(i == 0)
        def _():
            for ref in (dg_ref, acc_ref) + ((db_ref,) if mode == "attn" else ()):
                ref[...] = jnp.zeros_like(ref)

        dv, mv = d_ref[...], m_ref[...].astype(F32)
        r = _rstd(mv)
        mh = mv * r
        dg_ref[...] += jnp.sum(dv * mh, axis=0, keepdims=True)
        dm = _rms_bwd(mh, r, g_ref[...], dv)
        dmb = dm.astype(ACT)
        if mode == "attn":
            db_ref[...] += jnp.sum(dm, axis=0, keepdims=True)
        for c in range(0, k, kc):
            da = _dot_nt(dmb, w_ref[c:c + kc, :])
            da_ref[:, c:c + kc] = da.astype(ACT)
        acc_ref[...] += _dot_tn(a_ref[...], dmb)

        @pl.when(i == steps - 1)
        def _():
            dw_ref[...] = acc_ref[...].astype(ACT)

    ins = [dres, m, g, w, a]
    specs = [_rows(tm, d), _rows(tm, d), _full((1, d)), _full((k, d)), _rows(tm, k)]
    out_specs = [_rows(tm, k), _full((k, d)), _full((1, d))]
    out_shape = [_sds((t, k), ACT), _sds((k, d), ACT), _sds((1, d), F32)]
    if mode == "attn":
        out_specs.append(_full((1, d)))
        out_shape.append(_sds((1, d), F32))
    return _call(body, name, (steps,), specs, out_specs, out_shape, ins, ("arbitrary",), scratch=[pltpu.VMEM((k, d), F32)], jobs=jobs)


def _pre_bwd(dy, w, x, g, dres, name, h=None, jobs=()):
    t, d = x.shape
    tm = _row_tile(t)
    steps = t // tm
    n = dy.shape[1]

    def body(*refs):
        dy_ref, w_ref, x_ref, g_ref, dres_ref = refs[:5]
        dx_ref, dg_ref = refs[-4:-2] if h is not None else refs[-2:]
        i = pl.program_id(0)

        @pl.when(i == 0)
        def _():
            dg_ref[...] = jnp.zeros_like(dg_ref)
            if h is not None:
                refs[-1][...] = jnp.zeros_like(refs[-1])

        if w.ndim == 3:
            c = w.shape[2]
            dh = jnp.zeros((tm, d), F32)
            for j in range(w.shape[0]):
                dh = dh + _dot_nt(dy_ref[:, j * c:(j + 1) * c], w_ref[j])
        else:
            dh = _dot(dy_ref[...], w_ref[...])
        xv = x_ref[...]
        r = _rstd(xv)
        xh = xv * r
        dg_ref[...] += jnp.sum(dh * xh, axis=0, keepdims=True)
        dx_ref[...] = dres_ref[...] + _rms_bwd(xh, r, g_ref[...], dh)
        if h is not None:
            h_ref, dw_ref, acc_ref = refs[5], refs[-2], refs[-1]
            acc_ref[...] += _dot_tn(h_ref[...], dy_ref[...])

            @pl.when(i == steps - 1)
            def _():
                for j in range(w.shape[0]):
                    dw_ref[j] = acc_ref[:, j * c:(j + 1) * c].astype(ACT)

    specs = [_rows(tm, n), _resident(w.shape), _rows(tm, d), _full((1, d)), _rows(tm, d)]
    out_specs, out_shape, scratch = [_rows(tm, d), _full((1, d))], [_sds((t, d), F32), _sds((1, d), F32)], []
    if h is not None:
        specs, out_specs, out_shape = specs + [_rows(tm, d)], out_specs + [_full(w.shape)], out_shape + [_sds(w.shape, ACT)]
        scratch = [pltpu.VMEM((d, n), F32)]
    return _call(body, name, (steps,), specs, out_specs, out_shape, (dy, w, x, g, dres) + ((h,) if h is not None else ()),
                 ("arbitrary",), scratch=scratch, jobs=jobs)


def _ffn_bwd(dres, f, g_post, w_dn, gu, w_gu, x, g_pre, name, target=None, jobs=()):
    t, d = x.shape
    n = w_dn.shape[0]
    tm = _row_tile(t, 256)
    whole = target is not None
    n_in = 6 if whole else 8

    def body(*refs):
        if whole:
            t_ref, gp_ref, wd_ref, wu_ref, x_ref, g_ref = refs[:n_in]
            df_ref, dgu_ref, dx_ref, dgp_ref, dg_ref, loss_ref, h_ref, a_ref, gu_ref = refs[n_in:]
        else:
            d_ref, f_ref, gp_ref, wd_ref, gu_ref, wu_ref, x_ref, g_ref = refs[:n_in]
            df_ref, dgu_ref, dx_ref, dgp_ref, dg_ref = refs[n_in:]

        @pl.when(pl.program_id(0) == 0)
        def _():
            for ref in (dgp_ref, dg_ref) + ((loss_ref,) if whole else ()):
                ref[...] = jnp.zeros_like(ref)

        xv = x_ref[...]
        rx = _rstd(xv)
        xh = xv * rx
        if whole:
            hb = (xh * g_ref[...]).astype(ACT)
            h_ref[...] = hb
            for c, size in _mxu_chunks(n):
                gate = _dot_nt(hb, wu_ref[c:c + size, :])
                up = _dot_nt(hb, wu_ref[n + c:n + c + size, :])
                gu_ref[:, c:c + size] = gate.astype(ACT)
                gu_ref[:, n + c:n + c + size] = up.astype(ACT)
                a_ref[:, c:c + size] = (gate * _sigmoid(gate) * up).astype(ACT)
            fv = _dot(a_ref[...], wd_ref[...])
            r = _rstd(fv)
            fh = fv * r
            err = xv + fh * gp_ref[...] - t_ref[...]
            dv = err * (1.0 / d)
            loss_ref[...] += 0.5 * jnp.sum(jnp.mean(err * err, axis=-1, keepdims=True), axis=0, keepdims=True)
        else:
            dv, fv = d_ref[...], f_ref[...].astype(F32)
            r = _rstd(fv)
            fh = fv * r
        dgp_ref[...] += jnp.sum(dv * fh, axis=0, keepdims=True)
        dfb = _rms_bwd(fh, r, gp_ref[...], dv).astype(ACT)
        df_ref[...] = dfb
        for c, size in _mxu_chunks(n):
            da = _dot_nt(dfb, wd_ref[c:c + size, :]).astype(ACT)
            gate, up = gu_ref[:, c:c + size], gu_ref[:, n + c:n + c + size]
            sg = _sigmoid(gate.astype(F32)).astype(ACT)
            gs = gate * sg
            dgu_ref[:, c:c + size] = da * up * (sg + gs - gs * sg)
            dgu_ref[:, n + c:n + c + size] = da * gs
        dh = _dot(dgu_ref[...], wu_ref[...])
        dg_ref[...] += jnp.sum(dh * xh, axis=0, keepdims=True)
        dx_ref[...] = dv + _rms_bwd(xh, rx, g_ref[...], dh)

    w_specs = [_resident(w_dn.shape), _resident(w_gu.shape)]
    out_specs = [_rows(tm, d), _rows(tm, 2 * n), _rows(tm, d), _full((1, d)), _full((1, d))]
    out_shape = [_sds((t, d), ACT), _sds((t, 2 * n), ACT), _sds((t, d), F32), _sds((1, d), F32), _sds((1, d), F32)]
    if whole:
        return _call(
            body, name, (t // tm,),
            [_rows(tm, d), _full((1, d))] + w_specs + [_rows(tm, d), _full((1, d))],
            out_specs + [_full((8, LANES)), _rows(tm, d), _rows(tm, n)],
            out_shape + [_sds((8, LANES), F32), _sds((t, d), ACT), _sds((t, n), ACT)],
            (target, g_post, w_dn, w_gu, x, g_pre), ("arbitrary",), scratch=[pltpu.VMEM((tm, 2 * n), ACT)], jobs=jobs)
    return _call(
        body, name, (t // tm,),
        [_rows(tm, d), _rows(tm, d), _full((1, d)), w_specs[0], _rows(tm, 2 * n), w_specs[1], _rows(tm, d), _full((1, d))],
        out_specs, out_shape, (dres, f, g_post, w_dn, gu, w_gu, x, g_pre), ("arbitrary",), jobs=jobs)


def _wgrad(a, b, name, out_dtype=F32, transposed=False, jobs=()):
    t = a.shape[0]
    tt = _pick(t, (1024,))
    steps = t // tt
    tk = _pick(a.shape[1], (1024, 1408))
    k, n = a.shape[1], b.shape[1]
    tn = _pick(n, (2816, 1024, 1408))
    if transposed:
        out_spec, out_shape, acc_shape = pl.BlockSpec((tn, tk), lambda i, j, s: (j, i)), _sds((n, k), out_dtype), (tn, tk)
    else:
        out_spec, out_shape, acc_shape = pl.BlockSpec((tk, tn), lambda i, j, s: (i, j)), _sds((k, n), out_dtype), (tk, tn)

    def body(a_ref, b_ref, o_ref, acc_ref):
        s = pl.program_id(2)

        @pl.when(s == 0)
        def _():
            acc_ref[...] = jnp.zeros_like(acc_ref)

        acc_ref[...] += _dot_tn(b_ref[...], a_ref[...]) if transposed else _dot_tn(a_ref[...], b_ref[...])

        @pl.when(s == steps - 1)
        def _():
            o_ref[...] = acc_ref[...].astype(out_dtype)

    res, job_res = _call(
        body, name, (k // tk, n // tn, steps),
        [pl.BlockSpec((tt, tk), lambda i, j, s: (s, i)), pl.BlockSpec((tt, tn), lambda i, j, s: (s, j))], [out_spec], [out_shape],
        (a, b), ("parallel", "parallel", "arbitrary"), scratch=[pltpu.VMEM(acc_shape, F32)], jobs=jobs)
    return res[0], job_res


def _ffn_fwd(x, g_pre, w_gu, w_dn, g_post, jobs=()):
    t, d = x.shape
    n = w_dn.shape[0]
    tm = _row_tile(t)

    def body(x_ref, g_ref, wu_ref, wd_ref, gp_ref, h_ref, gu_ref, a_ref, f_ref, xo_ref):
        xv = x_ref[...]
        hb = (xv * _rstd(xv) * g_ref[...]).astype(ACT)
        h_ref[...] = hb
        for c, size in _mxu_chunks(n):
            gate = _dot_nt(hb, wu_ref[c:c + size, :])
            up = _dot_nt(hb, wu_ref[n + c:n + c + size, :])
            gu_ref[:, c:c + size] = gate.astype(ACT)
            gu_ref[:, n + c:n + c + size] = up.astype(ACT)
            a_ref[:, c:c + size] = (gate * _sigmoid(gate) * up).astype(ACT)
        f = _dot(a_ref[...], wd_ref[...])
        f_ref[...] = f.astype(ACT)
        xo_ref[...] = xv + (f * _rstd(f)) * gp_ref[...]

    return _call(
        body, "ffn_fwd", (t // tm,),
        [_rows(tm, d), _full((1, d)), _resident(w_gu.shape), _resident(w_dn.shape), _full((1, d))],
        [_rows(tm, d), _rows(tm, 2 * n), _rows(tm, n), _rows(tm, d), _rows(tm, d)],
        [_sds((t, d), ACT), _sds((t, 2 * n), ACT), _sds((t, n), ACT), _sds((t, d), ACT), _sds((t, d), F32)],
        (x, g_pre, w_gu, w_dn, g_post), ("parallel",), jobs=jobs)


def _causal(ws):
    row = lax.broadcasted_iota(jnp.int32, ws.shape, 0)
    col = lax.broadcasted_iota(jnp.int32, ws.shape, 1)
    return jnp.where(col <= row, ws, 0.0)


def _sgu_ln(v, lg, lb):
    mu = jnp.mean(v, axis=-1, keepdims=True)
    vc = v - mu
    rs = lax.rsqrt(jnp.mean(vc * vc, axis=-1, keepdims=True) + EPS)
    vh = vc * rs
    return vh, rs, vh * lg + lb


def _sgu_fwd(x, g, w, lg, lb, ws, bs_t, jobs=()):
    t, d = x.shape
    nb, _, c = w.shape
    n = nb * c
    groups = d // WINDOW
    tm = _row_tile(t)

    def body(x_ref, g_ref, w_ref, lg_ref, lb_ref, ws_ref, bs_ref, h_ref, z_ref, y_ref, zf_ref):
        xv = x_ref[...]
        hb = (xv * _rstd(xv) * g_ref[...]).astype(ACT)
        h_ref[...] = hb
        for j in range(nb):
            zf_ref[:, j * c:(j + 1) * c] = _dot(hb, w_ref[j])
        z_ref[...] = zf_ref[...].astype(ACT)
        gs = [slice(gi * WINDOW, (gi + 1) * WINDOW) for gi in range(groups)]
        wsg = [_causal(ws_ref[gi]).astype(ACT) for gi in range(groups)]
        for r in range(0, tm, WINDOW):
            u = _gelu(zf_ref[r:r + WINDOW, :d])
            _, _, vn = _sgu_ln(_gelu(zf_ref[r:r + WINDOW, d:]), lg_ref[...], lb_ref[...])
            mixed = [_dot(wsg[gi], vn[:, gs[gi]].astype(ACT)) for gi in range(groups)]
            for gi in range(groups):
                y_ref[r:r + WINDOW, gs[gi]] = (u[:, gs[gi]] * (mixed[gi] + bs_ref[:, gi:gi + 1])).astype(ACT)

    vec = _full((1, d))
    return _call(
        body, "sgu_fwd", (t // tm,),
        [_rows(tm, d), vec, _full((nb, d, c)), vec, vec, _full((groups, WINDOW, WINDOW)), _full((WINDOW, groups))],
        [_rows(tm, d), _rows(tm, n), _rows(tm, d)], [_sds((t, d), ACT), _sds((t, n), ACT), _sds((t, d), ACT)],
        (x, g, w, lg, lb, ws, bs_t), ("parallel",), scratch=[pltpu.VMEM((tm, n), F32)], jobs=jobs)


def _sgu_mix_bwd(z, dy, lg, lb, ws, bs_t, jobs=()):
    t, n = z.shape
    d = n // 2
    groups = d // WINDOW
    tm = _row_tile(t)

    def body(z_ref, dy_ref, lg_ref, lb_ref, ws_ref, bs_ref, dz_ref, dlg_ref, dlb_ref, dws_ref, dbs_ref):
        @pl.when(pl.program_id(0) == 0)
        def _():
            for ref in (dlg_ref, dlb_ref, dws_ref, dbs_ref):
                ref[...] = jnp.zeros_like(ref)

        lane = lax.broadcasted_iota(jnp.int32, (WINDOW, groups), 1)
        gs = [slice(gi * WINDOW, (gi + 1) * WINDOW) for gi in range(groups)]
        wsg = [_causal(ws_ref[gi]).astype(ACT) for gi in range(groups)]
        for c in range(0, tm, WINDOW):
            rows = slice(c, c + WINDOW)
            zu, zv = z_ref[rows, :d].astype(F32), z_ref[rows, d:].astype(F32)
            u = _gelu(zu)
            vh, rs, vn = _sgu_ln(_gelu(zv), lg_ref[...], lb_ref[...])
            dyv = dy_ref[rows, :].astype(F32)
            vng = [vn[:, gs[gi]].astype(ACT) for gi in range(groups)]
            dmix = dyv * u
            dmb = [dmix[:, gs[gi]].astype(ACT) for gi in range(groups)]
            mixed = [_dot(wsg[gi], vng[gi]) for gi in range(groups)]
            dvn_parts = [_dot_tn(wsg[gi], dmb[gi]) for gi in range(groups)]
            dws_parts = [_dot_nt(dmb[gi], vng[gi]) for gi in range(groups)]
            for gi in range(groups):
                dz_ref[rows, gs[gi]] = (dyv[:, gs[gi]] * (mixed[gi] + bs_ref[:, gi:gi + 1]) * _gelu_grad(zu[:, gs[gi]])).astype(ACT)
                dws_ref[:, gs[gi]] += _causal(dws_parts[gi])
                dbs_ref[...] += jnp.where(lane == gi, jnp.sum(dmix[:, gs[gi]], axis=-1, keepdims=True), 0.0)
            dvn = jnp.concatenate(dvn_parts, axis=-1)
            dlg_ref[...] += jnp.sum(dvn * vh, axis=0, keepdims=True)
            dlb_ref[...] += jnp.sum(dvn, axis=0, keepdims=True)
            dvh = dvn * lg_ref[...]
            dv = rs * (dvh - jnp.mean(dvh, axis=-1, keepdims=True) - vh * jnp.mean(dvh * vh, axis=-1, keepdims=True))
            dz_ref[rows, d:] = (dv * _gelu_grad(zv)).astype(ACT)

    vec = _full((1, d))
    return _call(
        body, "sgu_mix_bwd", (t // tm,),
        [_rows(tm, n), _rows(tm, d), vec, vec, _full((groups, WINDOW, WINDOW)), _full((WINDOW, groups))],
        [_rows(tm, n), vec, vec, _full((WINDOW, d)), _full((WINDOW, groups))],
        [_sds((t, n), ACT), _sds((1, d), F32), _sds((1, d), F32), _sds((WINDOW, d), F32), _sds((WINDOW, groups), F32)],
        (z, dy, lg, lb, ws, bs_t), ("arbitrary",), jobs=jobs)


AG_COPIES = 8


def _prepare(sources, dtypes, n_gather, name, ahead=None):
    n = len(sources)
    arrays, where = [], []
    for parts, layer in sources:
        found = []
        for part in parts:
            known = [i for i, v in enumerate(arrays) if v is part]
            if not known:
                arrays.append(part)
            found.append(known[0] if known else len(arrays) - 1)
        where.append((found, layer))
    shapes = [(sum(p.shape[1] for p in parts), parts[0].shape[2]) for parts, _ in sources]
    later = [i for i in range(len(arrays)) if all(i not in found for found, _ in where[:n_gather])]
    n_sems = 6 if ahead is not None else 3

    def body(*refs):
        ins, refs = list(refs[:len(arrays)]), refs[len(arrays):]
        stage, outs = refs[:n], refs[n:n + n_gather]
        sems = refs[len(refs) - n_sems - len(later) - 1:len(refs) - len(later) - 1]
        send, recv, local_sem = sems[:3]
        fetches = []
        for j, i in enumerate(later):
            held = refs[len(refs) - len(later) - 1 + j]
            fetches.append(pltpu.make_async_copy(ins[i], held, refs[-1].at[j]))
            fetches[-1].start()
            ins[i] = held
        x, y, c = _place()
        me, sibling, beside_x, beside_y, far = (x, y, c), (x, y, 1 - c), (1 - x, y, c), (x, 1 - y, c), (1 - x, 1 - y, c)
        slot = lambda dev: 4 * dev[0] + 2 * dev[1] + dev[2]
        other = lambda dev: (dev[0], dev[1], 1 - c)
        whole = lambda a: (0, shapes[a][0])
        cuts = [rows // 2 if rows % 32 == 0 else rows for rows, _ in shapes]
        first = lambda a: (0, cuts[a])
        rest = lambda a: (cuts[a], shapes[a][0] - cuts[a])

        def copy(a, k, block, rows, to, src=None):
            dst = outs[a].at[block, pl.ds(*rows)]
            return pltpu.make_async_remote_copy(
                src_ref=dst if src is None else src, dst_ref=dst, send_sem=send.at[a * AG_COPIES + k],
                recv_sem=recv.at[a * AG_COPIES + k], device_id=to, device_id_type=MESH)

        def cast(a):
            found, layer = where[a]
            at = 0
            for i in found:
                rows = arrays[i].shape[1]
                stage[a][at:at + rows, :] = ins[i][layer].astype(dtypes[a])
                at += rows

        for a in range(n_gather):
            cast(a)
        started, ahead_sends = [], []
        for a in range(n_gather):
            own = pltpu.make_async_copy(stage[a], outs[a].at[slot(me)], local_sem.at[a])
            own.start()
            started.append(own)
        sends = []
        for a in range(n_gather):
            sends += [copy(a, k, slot(me), whole(a), to, src=stage[a]) for k, to in enumerate((sibling, beside_x, beside_y))]
        for cp in sends:
            cp.start()
        for cp in fetches:
            cp.wait()
        for a in range(n_gather, n):
            cast(a)
        if ahead is not None:
            block, pieces = ahead
            out, (send2, recv2, local2) = refs[n + n_gather], sems[3:]
            for i, piece in enumerate(pieces):
                src, dst = stage[block].at[pl.ds(*piece)], out.at[slot(me), pl.ds(*piece)]
                own = pltpu.make_async_copy(src, dst, local2.at[i])
                own.start()
                started.append(own)
                for k, to in enumerate((sibling, beside_x, beside_y)):
                    cp = pltpu.make_async_remote_copy(src_ref=src, dst_ref=dst, send_sem=send2.at[3 * i + k],
                                                      recv_sem=recv2.at[3 * i + k], device_id=to, device_id_type=MESH)
                    cp.start()
                    ahead_sends.append(cp)
        for a in range(n_gather):
            copy(a, 1, slot(beside_x), whole(a), me).wait_recv()
            copy(a, 2, slot(beside_y), whole(a), me).wait_recv()
            passed = [copy(a, 3, slot(beside_x), first(a), beside_y), copy(a, 5, slot(beside_x), whole(a), sibling),
                      copy(a, 6, slot(beside_y), whole(a), sibling)]
            if rest(a)[1]:
                passed.append(copy(a, 4, slot(beside_y), rest(a), beside_x))
            for cp in passed:
                cp.start()
            sends += passed
        for a in range(n_gather):
            copy(a, 3, slot(far), first(a), me).wait_recv()
            if rest(a)[1]:
                copy(a, 4, slot(far), rest(a), me).wait_recv()
            passed = copy(a, 7, slot(far), whole(a), sibling)
            passed.start()
            sends.append(passed)
        for a in range(n_gather):
            for k, source in ((0, me), (5, beside_x), (6, beside_y), (7, far)):
                copy(a, k, slot(other(source)), whole(a), me).wait_recv()
        for cp in sends:
            cp.wait_send()
        for cp in ahead_sends:
            cp.wait()
        for own in started:
            own.wait()

    gathered = list(range(n_gather)) + ([ahead[0]] if ahead is not None else [])
    sems = [pltpu.SemaphoreType.DMA((n_gather * AG_COPIES,)), pltpu.SemaphoreType.DMA((n_gather * AG_COPIES,)),
            pltpu.SemaphoreType.DMA((n_gather,))]
    if ahead is not None:
        sems += [pltpu.SemaphoreType.DMA((3 * len(ahead[1]),)), pltpu.SemaphoreType.DMA((3 * len(ahead[1]),)),
                 pltpu.SemaphoreType.DMA((len(ahead[1]),))]
    res = pl.pallas_call(
        body, name=name,
        in_specs=[ANY if i in later else IN_VMEM for i in range(len(arrays))], out_specs=[IN_VMEM] * n + [ANY] * len(gathered),
        out_shape=[_sds(s, dt) for s, dt in zip(shapes, dtypes)] + [_sds((N_DEV,) + shapes[a], dtypes[a]) for a in gathered],
        scratch_shapes=sems + [pltpu.VMEM(arrays[i].shape, arrays[i].dtype) for i in later] + [pltpu.SemaphoreType.DMA((len(later),))],
        compiler_params=pltpu.CompilerParams(vmem_limit_bytes=VMEM_LIMIT_BYTES),
    )(*arrays)
    return list(res[:n]), list(res[n:])


def _pair_sum(g, r, core, name):
    _, _, rows, cols = g.shape
    tr = _pick(rows, (512, 256, 128))

    def body(core_ref, g_ref, r_ref, p_ref):
        del core_ref
        p_ref[...] = (g_ref[...].astype(F32) + r_ref[...].astype(F32)).astype(p_ref.dtype)

    return pl.pallas_call(
        body, name=name,
        grid_spec=pltpu.PrefetchScalarGridSpec(
            num_scalar_prefetch=1, grid=(4, rows // tr),
            in_specs=[pl.BlockSpec((None, None, tr, cols), lambda q, i, core_ref: (q, core_ref[0], i, 0)),
                      pl.BlockSpec((None, tr, cols), lambda q, i, core_ref: (q, i, 0))],
            out_specs=pl.BlockSpec((None, tr, cols), lambda q, i, core_ref: (q, i, 0))),
        out_shape=_sds((4, rows, cols), g.dtype),
        compiler_params=_params("parallel", "parallel"),
    )(core, g, r)


def _adam(w, g, m, v):
    m2 = ADAM_B1 * m + (1.0 - ADAM_B1) * g
    v2 = ADAM_B2 * v + (1.0 - ADAM_B2) * (g * g)
    m_hat = m2 / (1.0 - ADAM_B1 ** ADAM_STEP)
    v_hat = v2 / (1.0 - ADAM_B2 ** ADAM_STEP)
    return -ADAM_LR * (m_hat / (jnp.sqrt(v_hat) + ADAM_EPS) + ADAM_WD * w), m2, v2


def _shard_update(own, index, received, w, m, v, layer, so_far, name):
    _, rows, cols = w.shape
    n_recv = received.shape[0]
    tr = _shard_tile(rows)

    def body(index_ref, p_ref, q_ref, w_ref, m_ref, v_ref, *rest):
        del index_ref
        g_out, d_out, m_out, v_out = rest[-4:]
        g = p_ref[...].astype(F32)
        for s in range(n_recv):
            g = g + q_ref[s].astype(F32)
        g_out[...] = g
        d_out[...], m_out[...], v_out[...] = _adam(w_ref[...], g, m_ref[...], v_ref[...])

    tile = pl.BlockSpec((None, tr, cols), lambda i, index_ref: (layer, i, 0))
    kept = list(so_far) if so_far is not None else []
    return pl.pallas_call(
        body, name=name,
        grid_spec=pltpu.PrefetchScalarGridSpec(
            num_scalar_prefetch=1, grid=(rows // tr,),
            in_specs=[pl.BlockSpec((None, tr, cols), lambda i, index_ref: (index_ref[0], i, 0)),
                      pl.BlockSpec((n_recv, tr, cols), lambda i, index_ref: (0, i, 0)), tile, tile, tile] + [ANY] * len(kept),
            out_specs=[tile] * 4),
        out_shape=[_sds(w.shape, F32)] * 4,
        input_output_aliases={6 + i: i for i in range(len(kept))},
        compiler_params=_params("parallel"),
    )(index, own, received, w, m, v, *kept)


def _natural_pieces(shape, r, c):
    if tuple(shape[-2:]) == (r, c) and all(n == 1 for n in shape[:-2]):
        return [((0,) * (len(shape) - 2), (0, c))]
    groups, width = shape[1], shape[3]
    assert len(shape) == 4 and shape[0] == 1 and shape[2] == r and groups * width == c, (shape, r, c)
    return [((0, g), (g * width, width)) for g in range(groups)]


def _replicated_update(shares, where, params, name):
    flat = [a for p in params if p is not None for a in p]

    def body(s_ref, *refs):
        ins, outs = refs[:len(flat)], refs[len(flat):]
        total = s_ref[0]
        for dev in range(1, N_DEV):
            total = total + s_ref[dev]
        i_at = o_at = 0
        for ranges, p in zip(where, params):
            chunks = [total[r0:r0 + r, :c] for r0, r, c in ranges]
            g2d = chunks[0] if len(chunks) == 1 else jnp.concatenate(chunks, axis=1)
            if p is None:
                outs[o_at][...] = g2d
                o_at += 1
                continue
            w_ref, m_ref, v_ref = ins[i_at:i_at + 3]
            for idx, (c0, cols) in _natural_pieces(p[0].shape, *g2d.shape):
                at = idx + (slice(None), slice(None))
                g = g2d[:, c0:c0 + cols]
                outs[o_at][at] = g
                outs[o_at + 1][at], outs[o_at + 2][at], outs[o_at + 3][at] = _adam(w_ref[at], g, m_ref[at], v_ref[at])
            i_at, o_at = i_at + 3, o_at + 4

    out_shape = []
    for ranges, p in zip(where, params):
        out_shape += [_sds((ranges[0][1], sum(c for _, _, c in ranges)), F32)] if p is None else [_sds(p[0].shape, F32)] * 4
    res = pl.pallas_call(
        body, name=name, out_shape=out_shape, compiler_params=pltpu.CompilerParams(vmem_limit_bytes=VMEM_LIMIT_BYTES),
    )(shares, *flat)
    out, at = [], 0
    for p in params:
        out.append(list(res[at:at + (1 if p is None else 4)]))
        at += 1 if p is None else 4
    return out


def _rope_tables(t):
    half = HEAD_DIM // 2
    inv_freq = np.float32(ROPE_THETA) ** (-(np.arange(half, dtype=np.float32) * np.float32(2.0)) / np.float32(HEAD_DIM))
    ang = np.arange(t, dtype=np.float32)[:, None] * inv_freq[None, :].astype(np.float32)
    cos, sin = np.cos(ang).astype(np.float32), np.sin(ang).astype(np.float32)
    return (jnp.asarray(np.tile(np.concatenate([cos, cos], axis=1), (1, 2))),
            jnp.asarray(np.tile(np.concatenate([-sin, sin], axis=1), (1, 2))))


def _rows_full(gathered):
    return gathered.reshape(-1, gathered.shape[-1])


def _rows_blocked(full):
    return full.reshape(N_DEV, full.shape[0] // N_DEV, full.shape[1]).astype(ACT)


def _pack_rows(parts, width):
    rows, where, at = [], [], 0
    for v in parts:
        r, c = v.shape
        n_rows = -(-r // 8) * 8
        chunks = []
        for c0 in range(0, c, width):
            chunk = v[:, c0:c0 + width].astype(F32)
            rows.append(jnp.pad(chunk, ((0, n_rows - r), (0, width - chunk.shape[1]))))
            chunks.append((at, r, chunk.shape[1]))
            at += n_rows
        where.append(chunks)
    return jnp.concatenate(rows, axis=0), where


def _halves(block):
    half = block.shape[0] // 2
    return (0, half), (half, half)


def _whole(block):
    return (0, block.shape[0])


EARLY = ("attn_w_qkv", "sgu_ln", "attn_w_o")
LATE = ("sgu_w_in", "sgu_w_out", "gu0", "gu1", "dn0", "dn1")
SWAPPED = ("attn_w_qkv", "ffn_w_gate_up")
BEFORE_ATTN = ("norm_mix_post", "norm_ffn_pre", "norm_ffn_post", "attn_b_o", "sgu_w_spatial", "sgu_b_spatial")
AFTER_ATTN = ("attn_b_qkv", "attn_sinks")
LAST = ("norm_mix_pre",)
WEIGHTS = ("norm_mix_pre", "norm_mix_post", "norm_ffn_pre", "norm_ffn_post", "attn_w_qkv", "attn_b_qkv", "attn_sinks", "attn_w_o",
           "attn_b_o", "sgu_w_in", "sgu_ln_g", "sgu_ln_b", "sgu_w_spatial", "sgu_b_spatial", "sgu_w_out", "ffn_w_gate_up", "ffn_w_down")


LAYER_OF = {"gu0": ("ffn_w_gate_up", 0), "gu1": ("ffn_w_gate_up", 1), "dn0": ("ffn_w_down", 0), "dn1": ("ffn_w_down", 1)}


def _source(tree, name):
    if name == "sgu_ln":
        return [tree["sgu_ln_g"][None], tree["sgu_ln_b"][None]], 0
    leaf, layer = LAYER_OF.get(name, (name, 0))
    return [tree[leaf]], layer


def kernel(x, norm_mix_pre, norm_mix_post, norm_ffn_pre, norm_ffn_post, attn_w_qkv, attn_b_qkv, attn_sinks, attn_w_o, attn_b_o, sgu_w_in, sgu_ln_g, sgu_ln_b, sgu_w_spatial, sgu_b_spatial, sgu_w_out, ffn_w_gate_up, ffn_w_down, loss_target, m_norm_mix_pre, m_norm_mix_post, m_norm_ffn_pre, m_norm_ffn_post, m_attn_w_qkv, m_attn_b_qkv, m_attn_sinks, m_attn_w_o, m_attn_b_o, m_sgu_w_in, m_sgu_ln_g, m_sgu_ln_b, m_sgu_w_spatial, m_sgu_b_spatial, m_sgu_w_out, m_ffn_w_gate_up, m_ffn_w_down, v_norm_mix_pre, v_norm_mix_post, v_norm_ffn_pre, v_norm_ffn_post, v_attn_w_qkv, v_attn_b_qkv, v_attn_sinks, v_attn_w_o, v_attn_b_o, v_sgu_w_in, v_sgu_ln_g, v_sgu_ln_b, v_sgu_w_spatial, v_sgu_b_spatial, v_sgu_w_out, v_ffn_w_gate_up, v_ffn_w_down):
    w = dict(norm_mix_pre=norm_mix_pre, norm_mix_post=norm_mix_post, norm_ffn_pre=norm_ffn_pre, norm_ffn_post=norm_ffn_post,
             attn_w_qkv=attn_w_qkv, attn_b_qkv=attn_b_qkv, attn_sinks=attn_sinks, attn_w_o=attn_w_o, attn_b_o=attn_b_o,
             sgu_w_in=sgu_w_in, sgu_ln_g=sgu_ln_g, sgu_ln_b=sgu_ln_b, sgu_w_spatial=sgu_w_spatial, sgu_b_spatial=sgu_b_spatial,
             sgu_w_out=sgu_w_out, ffn_w_gate_up=ffn_w_gate_up, ffn_w_down=ffn_w_down)
    mom = dict(norm_mix_pre=m_norm_mix_pre, norm_mix_post=m_norm_mix_post, norm_ffn_pre=m_norm_ffn_pre, norm_ffn_post=m_norm_ffn_post,
               attn_w_qkv=m_attn_w_qkv, attn_b_qkv=m_attn_b_qkv, attn_sinks=m_attn_sinks, attn_w_o=m_attn_w_o, attn_b_o=m_attn_b_o,
               sgu_w_in=m_sgu_w_in, sgu_ln_g=m_sgu_ln_g, sgu_ln_b=m_sgu_ln_b, sgu_w_spatial=m_sgu_w_spatial,
               sgu_b_spatial=m_sgu_b_spatial, sgu_w_out=m_sgu_w_out, ffn_w_gate_up=m_ffn_w_gate_up, ffn_w_down=m_ffn_w_down)
    var = dict(norm_mix_pre=v_norm_mix_pre, norm_mix_post=v_norm_mix_post, norm_ffn_pre=v_norm_ffn_pre, norm_ffn_post=v_norm_ffn_post,
               attn_w_qkv=v_attn_w_qkv, attn_b_qkv=v_attn_b_qkv, attn_sinks=v_attn_sinks, attn_w_o=v_attn_w_o, attn_b_o=v_attn_b_o,
               sgu_w_in=v_sgu_w_in, sgu_ln_g=v_sgu_ln_g, sgu_ln_b=v_sgu_ln_b, sgu_w_spatial=v_sgu_w_spatial,
               sgu_b_spatial=v_sgu_b_spatial, sgu_w_out=v_sgu_w_out, ffn_w_gate_up=v_ffn_w_gate_up, ffn_w_down=v_ffn_w_down)
    w, mom, var = [{**tree, **{n: jnp.swapaxes(tree[n], 1, 2) for n in SWAPPED}} for tree in (w, mom, var)]
    xs, target = x[0], loss_target[0]
    t, d = xs.shape
    row = lambda v: v.reshape(1, -1).astype(F32)
    core = lax.axis_index("c").astype(jnp.int32).reshape(1)
    chip = (2 * lax.axis_index("x") + lax.axis_index("y")).astype(jnp.int32).reshape(1)
    names = EARLY + LATE
    gu_rows = w["ffn_w_gate_up"].shape[1] // 4
    gu0_parts = [(i * gu_rows, gu_rows) for i in range(4)]
    staged, (*early, g_gu0) = _prepare([_source(w, n) for n in names], [F32 if n == "sgu_ln" else ACT for n in names], len(EARLY),
                                       "weights_prepare", ahead=(names.index("gu0"), gu0_parts[:2]))
    stage = dict(zip(names, staged))
    w_qkv = _rows_full(early[0])
    lg, lb = row(early[1][:, 0, :]), row(early[1][:, 1, :])
    w_o = _rows_full(early[2])
    b_qkv = row(attn_b_qkv)
    sinks = attn_sinks.reshape(1, -1).astype(F32)
    ws = sgu_w_spatial[0].astype(F32)
    bs_t = sgu_b_spatial[0].astype(F32).T
    cos, sin = _rope_tables(t)
    gather = lambda name, so_far, **pieces: _gather_job(stage[name], so_far, **pieces)
    a_half = lambda name: _halves(stage[name])[0]
    b_half = lambda name: _halves(stage[name])[1]
    whole = lambda name: _whole(stage[name])

    def pieces(name, n):
        rows = stage[name].shape[0] // n
        return [(i * rows, rows) for i in range(n)]

    ways = lambda parts: [(piece, i % 2) for i, piece in enumerate(parts)]
    relay = lambda name, so_far, **steps: _relay_gather_job(stage[name], so_far, **steps)
    gu1_parts = pieces("gu1", 4)
    dn0_parts, dn1_parts, in_parts, out_parts = pieces("dn0", 2), pieces("dn1", 2), pieces("sgu_w_in", 2), pieces("sgu_w_out", 2)
    (h0, q, kdup, vdup), ((g_gu0,),) = _attn_qkv_fwd(
        xs, row(norm_mix_pre[0]), w_qkv, b_qkv, cos, sin, jobs=[relay("gu0", g_gu0, sends=gu0_parts[2:])])
    (o,), ((g_gu0,), (g_dn0,)) = _attn_fwd(q, kdup, vdup, sinks, jobs=[
        relay("gu0", g_gu0, relays=ways(gu0_parts), forwards=gu0_parts), relay("dn0", None, sends=dn0_parts)])
    (m0, x1), ((g_gu0,), (g_dn0,), (g_in,)) = _proj_res(o, w_o, row(attn_b_o), row(norm_mix_post[0]), xs, "attn_out_fwd", jobs=[
        relay("gu0", g_gu0, far=gu0_parts), relay("dn0", g_dn0, relays=ways(dn0_parts), forwards=dn0_parts, late_far=dn0_parts),
        relay("sgu_w_in", None, sends=in_parts)])
    w_gu0, w_dn0 = _rows_full(g_gu0), _rows_full(g_dn0)
    (h1, gu0, a0, f0, x2), ((g_in,), (g_out,), (g_gu1,), (g_dn1,)) = _ffn_fwd(
        x1, row(norm_ffn_pre[0]), w_gu0, w_dn0, row(norm_ffn_post[0]), jobs=[
            relay("sgu_w_in", g_in, relays=ways(in_parts), forwards=in_parts, late_far=in_parts),
            relay("sgu_w_out", None, sends=out_parts), relay("gu1", None, sends=gu1_parts), relay("dn1", None, sends=dn1_parts)])
    w_in = g_in
    (h2, z, y), ((g_out,), (g_gu1,), (g_dn1,)) = _sgu_fwd(x2, row(norm_mix_pre[1]), w_in, lg, lb, ws, bs_t, jobs=[
        relay("sgu_w_out", g_out, relays=ways(out_parts), forwards=out_parts, late_far=out_parts),
        relay("gu1", g_gu1, relays=ways(gu1_parts), forwards=gu1_parts),
        relay("dn1", g_dn1, relays=ways(dn1_parts), forwards=dn1_parts)])
    w_out = _rows_full(g_out)
    (m1, x3), ((g_gu1,), (g_dn1,)) = _proj_res(y, w_out, None, row(norm_mix_post[1]), x2, "sgu_out_fwd", jobs=[
        relay("gu1", g_gu1, far=gu1_parts), relay("dn1", g_dn1, far=dn1_parts)])
    w_gu1, w_dn1 = _rows_full(g_gu1), _rows_full(g_dn1)

    to_sibling = lambda g: _scatter_job([g], 4, _to_sibling)
    pair = lambda g, r, name: _pair_sum(g.reshape((4, 2) + g.shape[1:]), r, core, "pair_sum_" + name)
    to_chips = lambda p, prev=None, piece=None: _scatter_job([p], 3, _to_owner_chip, prev=prev, piece=piece)
    out_g, out_d, out_m, out_v = {}, {}, {}, {}

    trees = [{**tree, "sgu_ln": jnp.stack([tree["sgu_ln_g"], tree["sgu_ln_b"]], axis=1)} for tree in (w, mom, var)]
    so_far = {}

    def update(name, own, index, received):
        leaf, layer = LAYER_OF.get(name, (name, 0))
        so_far[leaf] = _shard_update(own, index, received, *[tree[leaf] for tree in trees], layer, so_far.get(leaf), "update_" + name)
        for out, val in zip((out_g, out_d, out_m, out_v), so_far[leaf]):
            out[leaf] = val

    def finish(names, n_extra, shares, where, name):
        res = _replicated_update(shares, where, [(w[n], mom[n], var[n]) for n in names] + [None] * n_extra, name)
        for n, vals in zip(names, res):
            out_g[n], out_d[n], out_m[n], out_v[n] = vals
        return [vals[0] for vals in res[len(names):]]

    first_half = lambda p: _halves(p[0])[0]
    second_half = lambda p: _halves(p[0])[1]
    (df1, dgu1, dx3, dg_ffn_post1, dg_ffn_pre1, loss_tile, h3, a1), _ = _ffn_bwd(
        None, None, row(norm_ffn_post[1]), w_dn1, None, w_gu1, x3, row(norm_ffn_pre[1]), "ffn_last", target=target)
    b_gu1, _ = _wgrad(h3, dgu1, "ffn_up_wgrad1", ACT, transposed=True)
    b_gu1 = _rows_blocked(b_gu1)
    dw_dn1, _ = _wgrad(a1, df1, "ffn_down_wgrad1", ACT)
    b_dn1 = _rows_blocked(dw_dn1)
    (dy, dw_out, dg_mix_post1), ((r_gu1,), (r_dn1,)) = _post_bwd(
        dx3, m1, row(norm_mix_post[1]), w_out, y, "sgu", "sgu_out_bwd", jobs=[to_sibling(b_gu1), to_sibling(b_dn1)])
    p_gu1, p_dn1 = pair(b_gu1, r_gu1, "gu1"), pair(b_dn1, r_dn1, "dn1")
    b_out = _rows_blocked(dw_out)
    (dz, dlg, dlb, dws, dbs_t), ((q_gu1,), (r_out,)) = _sgu_mix_bwd(z, dy, lg, lb, ws, bs_t, jobs=[
        to_chips(p_gu1, piece=first_half(p_gu1)), to_sibling(b_out)])
    p_out = pair(b_out, r_out, "sgu_w_out")
    b_ln = jnp.stack([dlg.reshape(N_DEV, -1), dlb.reshape(N_DEV, -1)], axis=1)
    (dx2, dg_mix_pre1, b_in), _ = _pre_bwd(dz, w_in, x2, row(norm_mix_pre[1]), dx3, "sgu_in_bwd", h=h2)
    (df0, dgu0, dx1, dg_ffn_post0, dg_ffn_pre0), ((q_gu1,), (q_dn1,), (q_out,), (r_in,), (r_ln,)) = _ffn_bwd(
        dx2, f0, row(norm_ffn_post[0]), w_dn0, gu0, w_gu0, x1, row(norm_ffn_pre[0]), "ffn_bwd0",
        jobs=[to_chips(p_gu1, prev=[q_gu1], piece=second_half(p_gu1)), to_chips(p_dn1), to_chips(p_out), to_sibling(b_in),
              to_sibling(b_ln)])
    update("gu1", p_gu1, chip, q_gu1)
    update("dn1", p_dn1, chip, q_dn1)
    update("sgu_w_out", p_out, chip, q_out)
    p_in, p_ln = pair(b_in, r_in, "sgu_w_in"), pair(b_ln, r_ln, "sgu_ln")
    dw_dn0, _ = _wgrad(a0, df0, "ffn_down_wgrad0", ACT)
    b_dn0 = _rows_blocked(dw_dn0)
    b_gu0, ((q_in,), (q_ln,), (r_dn0,)) = _wgrad(h1, dgu0, "ffn_up_wgrad0", ACT, transposed=True, jobs=[
        to_chips(p_in), to_chips(p_ln), to_sibling(b_dn0)])
    update("sgu_w_in", p_in, chip, q_in)
    update("sgu_ln", p_ln, chip, q_ln)
    b_gu0 = _rows_blocked(b_gu0)
    p_dn0 = pair(b_dn0, r_dn0, "dn0")
    (do, dw_o, dg_mix_post0, db_o), ((r_gu0,), (q_dn0,)) = _post_bwd(
        dx1, m0, row(norm_mix_post[0]), w_o, o, "attn", "attn_out_bwd", jobs=[to_sibling(b_gu0), to_chips(p_dn0, piece=first_half(p_dn0))])
    p_gu0 = pair(b_gu0, r_gu0, "gu0")
    b_o = _rows_blocked(dw_o)
    grads = {
        "norm_mix_post": jnp.concatenate([dg_mix_post0, dg_mix_post1], axis=0),
        "norm_ffn_pre": jnp.concatenate([dg_ffn_pre0, dg_ffn_pre1], axis=0),
        "norm_ffn_post": jnp.concatenate([dg_ffn_post0, dg_ffn_post1], axis=0),
        "attn_b_o": db_o,
        "sgu_w_spatial": dws,
        "sgu_b_spatial": dbs_t.T,
    }
    shares1, where1 = _pack_rows([grads[name] for name in BEFORE_ATTN], d)
    (dqkv, db_qkv, dsink), ((q_gu0,), (q_dn0,), (r_o,), (g_shares1,)) = _attn_bwd(q, kdup, vdup, do, sinks, cos, sin, jobs=[
        to_chips(p_gu0), to_chips(p_dn0, prev=[q_dn0], piece=second_half(p_dn0)), to_sibling(b_o),
        _gather_job(shares1, None, sends=[_whole(shares1)])])
    update("gu0", p_gu0, chip, q_gu0)
    update("dn0", p_dn0, chip, q_dn0)
    p_o = pair(b_o, r_o, "attn_w_o")
    grads.update({"attn_b_qkv": db_qkv, "attn_sinks": dsink[:1, :d // HEAD_DIM]})
    shares2, where2 = _pack_rows([grads[name] for name in AFTER_ATTN] + [loss_tile[:1, :1]], d)
    (dx0, dg_mix_pre0), _ = _pre_bwd(dqkv, w_qkv, xs, row(norm_mix_pre[0]), dx1, "attn_qkv_bwd")
    grads["norm_mix_pre"] = jnp.concatenate([dg_mix_pre0, dg_mix_pre1], axis=0)
    shares3, where3 = _pack_rows([grads[name] for name in LAST], d)
    dw_qkv, ((q_o,), (g_shares1,), (g_shares2,), (g_shares3,)) = _wgrad(h0, dqkv, "attn_qkv_wgrad", ACT, transposed=True, jobs=[
        to_chips(p_o), _gather_job(shares1, g_shares1, forwards=[_whole(shares1)]),
        _gather_job(shares2, None, sends=[_whole(shares2)]), _gather_job(shares3, None, sends=[_whole(shares3)])])
    update("attn_w_o", p_o, chip, q_o)
    b_qkv_grad = _rows_blocked(dw_qkv)
    (r_qkv,), (g_shares2,), (g_shares3,) = _copies_only([
        to_sibling(b_qkv_grad), _gather_job(shares2, g_shares2, forwards=[_whole(shares2)]),
        _gather_job(shares3, g_shares3, forwards=[_whole(shares3)])], "grads_tail_to_sibling")
    p_qkv = pair(b_qkv_grad, r_qkv, "attn_w_qkv")
    (q_qkv,), = _copies_only([to_chips(p_qkv)], "grads_tail_to_owner_chip")
    update("attn_w_qkv", p_qkv, chip, q_qkv)

    finish(BEFORE_ATTN, 0, g_shares1, where1, "replicated_update_before_attn")
    loss = finish(AFTER_ATTN, 1, g_shares2, where2, "replicated_update_after_attn")[0][0, 0]
    finish(LAST, 0, g_shares3, where3, "replicated_update_last")

    for out in (out_g, out_d, out_m, out_v):
        out["sgu_ln_g"], out["sgu_ln_b"] = out["sgu_ln"][:, 0, :], out["sgu_ln"][:, 1, :]
        for n in SWAPPED:
            out[n] = jnp.swapaxes(out[n], 1, 2)

    return (loss, dx0[None], *[out_g[n] for n in WEIGHTS], *[out_d[n] for n in WEIGHTS],
            *[out_m[n] for n in WEIGHTS], *[out_v[n] for n in WEIGHTS])
```

```python
import functools
import math
import operator

import jax
import jax.numpy as jnp
import numpy as np
from jax import lax
from jax.experimental import pallas as pl
from jax.experimental.pallas import tpu as pltpu

F32 = jnp.float32
ACT = jnp.bfloat16

EPS = 1e-6
HEAD_DIM = 64
PAIR = 2 * HEAD_DIM
GQA_GROUP = 4
WINDOW = 128
ROPE_THETA = 10000.0
SCALE = HEAD_DIM ** -0.5
MASKED = -1e30
LANES = 128
MXU_WIDTH = 256

ADAM_LR, ADAM_B1, ADAM_B2, ADAM_EPS, ADAM_WD, ADAM_STEP = 0.001, 0.9, 0.999, 1e-08, 0.01, 10

N_DEV = 8
VMEM_LIMIT_BYTES = 56 * 1024 * 1024
MESH = pl.DeviceIdType.MESH
ANY = pl.BlockSpec(memory_space=pl.ANY)
IN_VMEM = pl.BlockSpec(memory_space=pltpu.VMEM)


def _pick(n, candidates):
    for c in candidates:
        if n % c == 0:
            return c
    return n


def _row_tile(t, most=512):
    return _pick(t, (most,))


def _shard_tile(rows):
    return next((c for c in (512, 256, 176, 128, 96, 64) if rows % c == 0 and rows >= 2 * c), rows)


def _mxu_chunks(n, most=4 * MXU_WIDTH):
    assert n % MXU_WIDTH == 0 and most % MXU_WIDTH == 0
    return [(c, min(most, n - c)) for c in range(0, n, most)]


def _params(*sem):
    return pltpu.CompilerParams(dimension_semantics=sem, vmem_limit_bytes=VMEM_LIMIT_BYTES)


def _full(shape):
    return pl.BlockSpec(shape, lambda *_: (0,) * len(shape))


def _resident(shape):
    return pl.BlockSpec(shape, lambda *_: (0,) * len(shape), pipeline_mode=pl.Buffered(1))


def _rows(tm, n):
    return pl.BlockSpec((tm, n), lambda i: (i, 0))


def _sds(shape, dtype):
    return jax.ShapeDtypeStruct(shape, dtype)


def _place():
    return lax.axis_index("x"), lax.axis_index("y"), lax.axis_index("c")


def _other_chips(x, y):
    return [(1 - x, y), (x, 1 - y), (1 - x, 1 - y)]


class _Job:
    def __init__(self, inputs, out_shape, aliases, emit, n_remote, n_local=0, n_late=0):
        self.inputs, self.out_shape, self.aliases, self.emit = list(inputs), list(out_shape), dict(aliases), emit
        self.n_remote, self.n_local, self.n_late = n_remote, n_local, n_late


def _call(body, name, grid, in_specs, out_specs, out_shape, args, sem, scratch=(), jobs=()):
    n_in, n_out, n_scr = len(args), len(out_shape), len(scratch)
    j_in = [a for job in jobs for a in job.inputs]
    j_out = [s for job in jobs for s in job.out_shape]
    aliases, at_in, at_out = {}, n_in, n_out
    for job in jobs:
        aliases.update({at_in + i: at_out + o for i, o in job.aliases.items()})
        at_in, at_out = at_in + len(job.inputs), at_out + len(job.out_shape)
    n_remote = sum(job.n_remote for job in jobs)
    n_local = sum(job.n_local for job in jobs)

    def wrapped(*refs):
        ins, jin = refs[:n_in], refs[n_in:n_in + len(j_in)]
        rest = refs[n_in + len(j_in):]
        outs, jout = rest[:n_out], rest[n_out:n_out + len(j_out)]
        scr = rest[n_out + len(j_out):n_out + len(j_out) + n_scr]
        if not jobs:
            body(*ins, *outs, *scr)
            return
        send, recv, local = rest[n_out + len(j_out) + n_scr:]
        ids = [pl.program_id(a) for a in range(len(grid))]
        first = functools.reduce(operator.and_, [i == 0 for i in ids])
        last = functools.reduce(operator.and_, [i == g - 1 for i, g in zip(ids, grid)])

        def descriptors():
            place, found, i, o, r, l = _place(), ([], []), 0, 0, 0, 0
            for job in jobs:
                emitted = job.emit(jin[i:i + len(job.inputs)], jout[o:o + len(job.out_shape)], place)
                for at, (src, dst, to) in enumerate(emitted):
                    if to is None:
                        cp = pltpu.make_async_copy(src, dst, local.at[l])
                        l += 1
                    else:
                        cp = pltpu.make_async_remote_copy(src_ref=src, dst_ref=dst, send_sem=send.at[r], recv_sem=recv.at[r],
                                                          device_id=to, device_id_type=MESH)
                        r += 1
                    found[at >= len(emitted) - job.n_late].append(cp)
                i, o = i + len(job.inputs), o + len(job.out_shape)
            return found

        @pl.when(first)
        def _():
            for cp in descriptors()[0]:
                cp.start()

        body(*ins, *outs, *scr)

        @pl.when(last)
        def _():
            early, late = descriptors()
            for cp in early:
                cp.wait()
            for cp in late:
                cp.start()
            for cp in late:
                cp.wait()

    sems = [pltpu.SemaphoreType.DMA((n_remote,)), pltpu.SemaphoreType.DMA((n_remote,)),
            pltpu.SemaphoreType.DMA((max(n_local, 1),))] if jobs else []
    res = pl.pallas_call(
        wrapped, name=name, grid=grid,
        in_specs=list(in_specs) + [ANY] * len(j_in), out_specs=list(out_specs) + [ANY] * len(j_out),
        out_shape=list(out_shape) + j_out, scratch_shapes=list(scratch) + sems, input_output_aliases=aliases,
        compiler_params=_params(*(("arbitrary",) * len(grid) if jobs else sem)),
    )(*args, *j_in)
    job_res, at = [], n_out
    for job in jobs:
        job_res.append(list(res[at:at + len(job.out_shape)]))
        at += len(job.out_shape)
    return list(res[:n_out]), job_res


def _copies_only(jobs, name):
    def body(o_ref):
        o_ref[...] = jnp.zeros_like(o_ref)

    return _call(body, name, (1,), [], [_full((8, LANES))], [_sds((8, LANES), F32)], (), ("arbitrary",), jobs=jobs)[1]


def _gather_job(stage, gathered, sends=(), forwards=()):
    shape = _sds((N_DEV,) + stage.shape, stage.dtype)
    inputs = ([stage] if sends else []) + ([gathered] if gathered is not None else [])
    aliases = {len(inputs) - 1: 0} if gathered is not None else {}

    def emit(ins, outs, place):
        x, y, c = place
        sibling, chips, mine, g = (x, y, 1 - c), _other_chips(x, y), 4 * x + 2 * y + c, outs[0]
        found = []
        for r0, nr in sends:
            src, dst = ins[0].at[pl.ds(r0, nr)], g.at[mine, pl.ds(r0, nr)]
            found += [(src, dst, None), (src, dst, sibling)] + [(src, dst, (px, py, c)) for px, py in chips]
        for r0, nr in forwards:
            for px, py in chips:
                block = 4 * px + 2 * py + c
                found.append((ins[-1].at[block, pl.ds(r0, nr)], g.at[block, pl.ds(r0, nr)], sibling))
        return found

    return _Job(inputs, [shape], aliases, emit, 4 * len(sends) + 3 * len(forwards), len(sends))


def _relay_gather_job(stage, gathered, sends=(), relays=(), forwards=(), far=(), late_far=()):
    shape = _sds((N_DEV,) + stage.shape, stage.dtype)
    inputs = ([stage] if sends else []) + ([gathered] if gathered is not None else [])
    aliases = {len(inputs) - 1: 0} if gathered is not None else {}

    def emit(ins, outs, place):
        x, y, c = place
        sibling, beside_x, beside_y, g = (x, y, 1 - c), (1 - x, y, c), (x, 1 - y, c), outs[0]
        slot = lambda dev: 4 * dev[0] + 2 * dev[1] + dev[2]
        found = []
        for r0, nr in sends:
            src, dst = ins[0].at[pl.ds(r0, nr)], g.at[slot((x, y, c)), pl.ds(r0, nr)]
            found += [(src, dst, None), (src, dst, sibling), (src, dst, beside_x), (src, dst, beside_y)]

        def passed_on(block, piece, to):
            return ins[-1].at[block, pl.ds(*piece)], g.at[block, pl.ds(*piece)], to

        for piece, way in relays:
            found.append(passed_on(slot(beside_x), piece, beside_y) if way == 0 else passed_on(slot(beside_y), piece, beside_x))
        for piece in forwards:
            found += [passed_on(slot(beside_x), piece, sibling), passed_on(slot(beside_y), piece, sibling)]
        for piece in tuple(far) + tuple(late_far):
            found.append(passed_on(slot((1 - x, 1 - y, c)), piece, sibling))
        return found

    n_remote = 3 * len(sends) + len(relays) + 2 * len(forwards) + len(far) + len(late_far)
    return _Job(inputs, [shape], aliases, emit, n_remote, len(sends), len(late_far))


def _scatter_job(arrays, n_slots, route, prev=None, piece=None):
    shapes = [_sds((n_slots,) + v.shape[1:], v.dtype) for v in arrays]
    inputs = list(arrays) + (list(prev) if prev else [])
    aliases = {len(arrays) + i: i for i in range(len(arrays))} if prev else {}

    def emit(ins, outs, place):
        found = []
        for a in range(len(arrays)):
            for s in range(n_slots):
                block, to = route(s, *place)
                if piece is None:
                    found.append((ins[a].at[block], outs[a].at[s], to))
                else:
                    found.append((ins[a].at[block, pl.ds(*piece)], outs[a].at[s, pl.ds(*piece)], to))
        return found

    return _Job(inputs, shapes, aliases, emit, len(arrays) * n_slots)


def _to_sibling(s, x, y, c):
    return 2 * s + (1 - c), (x, y, 1 - c)


def _to_owner_chip(s, x, y, c):
    px, py = _other_chips(x, y)[s]
    return 2 * px + py, (px, py, c)


def _rstd(x):
    return lax.rsqrt(jnp.mean(x * x, axis=-1, keepdims=True) + EPS)


def _rms_bwd(xhat, r, g, dy):
    dxh = dy * g
    return r * (dxh - xhat * jnp.mean(dxh * xhat, axis=-1, keepdims=True))


def _first_half(rows):
    lane = lax.broadcasted_iota(jnp.int32, (rows, PAIR), 1)
    return (lane % HEAD_DIM) < (HEAD_DIM // 2)


def _rot_half(v, first):
    return jnp.where(first, pltpu.roll(v, PAIR - HEAD_DIM // 2, 1), pltpu.roll(v, HEAD_DIM // 2, 1))


def _rope(v, cos, sin, first):
    return v * cos + _rot_half(v, first) * sin


def _rope_t(dv, cos, sin, first):
    return dv * cos + _rot_half(dv * sin, first)


def _dot(a, b):
    return jnp.dot(a, b, preferred_element_type=F32)


def _dot_nt(a, b):
    return lax.dot_general(a, b, (((1,), (1,)), ((), ())), preferred_element_type=F32)


def _dot_tn(a, b):
    return lax.dot_general(a, b, (((0,), (0,)), ((), ())), preferred_element_type=F32)


def _sigmoid(v):
    return 1.0 / (1.0 + jnp.exp(-v))


_GELU_K = math.sqrt(2.0 / math.pi)
_GELU_C = 0.044715


def _gelu(v):
    return v * (0.5 * (1.0 + jnp.tanh(_GELU_K * (v + _GELU_C * (v * v * v)))))


def _gelu_grad(v):
    t = jnp.tanh(_GELU_K * (v + _GELU_C * (v * v * v)))
    return 0.5 * (1.0 + t) + 0.5 * v * (1.0 - t * t) * (_GELU_K * (1.0 + 3.0 * _GELU_C * (v * v)))


def _attn_qkv_fwd(x, g, w, b, cos, sin, jobs=()):
    t, d = x.shape
    n = w.shape[0]
    kd = n - d
    assert (kd // 2) % PAIR == 0
    tm = _row_tile(t)
    ch = _pick(d, (512,))

    def body(x_ref, g_ref, w_ref, b_ref, cos_ref, sin_ref, h_ref, q_ref, k_ref, v_ref):
        xv = x_ref[...]
        hb = (xv * _rstd(xv) * g_ref[...]).astype(ACT)
        h_ref[...] = hb
        cosv, sinv = cos_ref[...], sin_ref[...]
        first = _first_half(tm)
        left = _lane_halves()[0]

        def proj(lo, width):
            return _dot_nt(hb, w_ref[lo:lo + width, :]) + b_ref[:, lo:lo + width]

        def twice(ref, s, two_heads):
            swapped = pltpu.roll(two_heads, HEAD_DIM, 1)
            ref[:, 2 * s:2 * s + PAIR] = jnp.where(left, two_heads, swapped).astype(ACT)
            ref[:, 2 * s + PAIR:2 * s + 2 * PAIR] = jnp.where(left, swapped, two_heads).astype(ACT)

        for c in range(0, d, ch):
            y = proj(c, ch)
            for s in range(0, ch, PAIR):
                q_ref[:, c + s:c + s + PAIR] = (_rope(y[:, s:s + PAIR], cosv, sinv, first) * SCALE).astype(ACT)
        y = proj(d, kd)
        for s in range(0, kd // 2, PAIR):
            twice(k_ref, s, _rope(y[:, s:s + PAIR], cosv, sinv, first))
            twice(v_ref, s, y[:, kd // 2 + s:kd // 2 + s + PAIR])

    return _call(
        body, "attn_qkv_fwd", (t // tm,),
        [_rows(tm, d), _full((1, d)), _full((n, d)), _full((1, n)), _rows(tm, PAIR), _rows(tm, PAIR)],
        [_rows(tm, d), _rows(tm, d), _rows(tm, kd), _rows(tm, kd)],
        [_sds((t, d), ACT), _sds((t, d), ACT), _sds((t, kd), ACT), _sds((t, kd), ACT)],
        (x, g, w, b, cos, sin), ("parallel",), jobs=jobs)


STACK = GQA_GROUP * WINDOW


def _band_mask(has_prev):
    row = lax.broadcasted_iota(jnp.int32, (STACK, 2 * WINDOW), 0) % WINDOW
    col = lax.broadcasted_iota(jnp.int32, (STACK, 2 * WINDOW), 1)
    return ((col < WINDOW) & (col > row) & has_prev) | ((col >= WINDOW) & (col - WINDOW <= row))


def _lane_halves():
    lane = lax.broadcasted_iota(jnp.int32, (1, PAIR), 1)
    return lane < HEAD_DIM, lane >= HEAD_DIM


def _stack_heads(ref, h, halves):
    parts = []
    for p in range(2):
        pair = ref[:, (2 * h + p) * PAIR:(2 * h + p + 1) * PAIR]
        parts += [jnp.where(half, pair, jnp.zeros_like(pair)) for half in halves]
    return jnp.concatenate(parts, axis=0)


def _sink_column(sink_ref, h):
    row = lax.broadcasted_iota(jnp.int32, (STACK, 1), 0)
    col = jnp.full((STACK, 1), sink_ref[0, GQA_GROUP * h + GQA_GROUP - 1], F32)
    for j in range(GQA_GROUP - 2, -1, -1):
        col = jnp.where(row < (j + 1) * WINDOW, sink_ref[0, GQA_GROUP * h + j], col)
    return col


def _probs(scores, valid, sink):
    s = jnp.where(valid, scores, MASKED)
    m = jnp.maximum(jnp.max(s, axis=-1, keepdims=True), sink)
    p = jnp.exp(s - m)
    psink = jnp.exp(sink - m)
    inv = 1.0 / (jnp.sum(p, axis=-1, keepdims=True) + psink)
    return p * inv, psink * inv


def _attn_fwd(q, kd_, vd, sinks, jobs=()):
    t, d = q.shape
    kd = kd_.shape[1]
    cur = lambda n: (n, 0)
    prev = lambda n: (jnp.maximum(n - 1, 0), 0)

    def body(sink_ref, q_ref, kc_ref, kp_ref, vc_ref, vp_ref, o_ref):
        valid = _band_mask(pl.program_id(0) > 0)
        halves = _lane_halves()
        heads = range(kd // PAIR)
        hs = [slice(h * PAIR, (h + 1) * PAIR) for h in heads]
        scores = [_dot_nt(_stack_heads(q_ref, h, halves), jnp.concatenate([kp_ref[:, hs[h]], kc_ref[:, hs[h]]], axis=0)) for h in heads]
        probs = [_probs(scores[h], valid, _sink_column(sink_ref, h))[0].astype(ACT) for h in heads]
        for h in heads:
            v2 = jnp.concatenate([vp_ref[:, hs[h]], vc_ref[:, hs[h]]], axis=0)
            vs = jnp.concatenate([jnp.where(half, v2, jnp.zeros_like(v2)) for half in halves], axis=0)
            pr = probs[h]
            for p in range(2):
                both = jnp.concatenate([pr[2 * p * WINDOW:(2 * p + 1) * WINDOW], pr[(2 * p + 1) * WINDOW:(2 * p + 2) * WINDOW]], axis=1)
                o_ref[:, (2 * h + p) * PAIR:(2 * h + p + 1) * PAIR] = _dot(both, vs).astype(ACT)

    kv_c, kv_p = pl.BlockSpec((WINDOW, kd), cur), pl.BlockSpec((WINDOW, kd), prev)
    return _call(
        body, "attn_fwd", (t // WINDOW,),
        [pl.BlockSpec(memory_space=pltpu.SMEM), pl.BlockSpec((WINDOW, d), cur), kv_c, kv_p, kv_c, kv_p],
        [pl.BlockSpec((WINDOW, d), cur)], [_sds((t, d), ACT)],
        (sinks, q, kd_, kd_, vd, vd), ("parallel",), jobs=jobs)


def _attn_bwd(q, kd_, vd, do, sinks, cos, sin, jobs=()):
    t, d = q.shape
    kd = kd_.shape[1]
    nb = t // WINDOW
    n_out = d + kd
    assert (kd // 2) % PAIR == 0
    cur = lambda n: (jnp.minimum(n, nb - 1), 0)
    prev = lambda n: (jnp.maximum(jnp.minimum(n, nb - 1) - 1, 0), 0)
    late = lambda n: (jnp.maximum(n - 1, 0), 0)

    def body(sink_ref, q_ref, kc_ref, kp_ref, vc_ref, vp_ref, do_ref, cosc_ref, sinc_ref, cosp_ref, sinp_ref,
             out_ref, db_ref, dsink_ref, dq_keep, dk_keep, dv_keep):
        n = pl.program_id(0)

        @pl.when(n == 0)
        def _():
            for ref in (db_ref, dsink_ref, dq_keep, dk_keep, dv_keep):
                ref[...] = jnp.zeros_like(ref)

        valid = _band_mask(n > 0) & (n < nb)
        halves = _lane_halves()
        first = _first_half(WINDOW)
        slot = lax.broadcasted_iota(jnp.int32, dsink_ref.shape, 1)
        top = lax.broadcasted_iota(jnp.int32, dsink_ref.shape, 0) == 0
        dsink = jnp.zeros(dsink_ref.shape, F32)

        def emit(cols, done):
            out_ref[:, cols] = done.astype(ACT)
            db_ref[:, cols] += jnp.sum(done.astype(F32), axis=0, keepdims=True)

        heads = range(kd // PAIR)
        hs = [slice(h * PAIR, (h + 1) * PAIR) for h in heads]
        k2 = [jnp.concatenate([kp_ref[:, hs[h]], kc_ref[:, hs[h]]], axis=0) for h in heads]
        v2 = [jnp.concatenate([vp_ref[:, hs[h]], vc_ref[:, hs[h]]], axis=0) for h in heads]
        qs = [_stack_heads(q_ref, h, halves) for h in heads]
        dos = [_stack_heads(do_ref, h, halves) for h in heads]
        scores = [_dot_nt(qs[h], k2[h]) for h in heads]
        dps = [_dot_nt(dos[h], v2[h]) for h in heads]
        prs, dss = [], []
        for h in heads:
            pr, psink = _probs(scores[h], valid, _sink_column(sink_ref, h))
            delta = jnp.sum(pr * dps[h], axis=-1, keepdims=True)
            dss.append((pr * (dps[h] - delta)).astype(ACT))
            prs.append(pr.astype(ACT))
            leak = psink * delta
            for j in range(GQA_GROUP):
                share = jnp.sum(leak[j * WINDOW:(j + 1) * WINDOW], axis=0, keepdims=True)
                dsink = dsink - jnp.where(top & (slot == GQA_GROUP * h + j), share, 0.0)
        dqs = [_dot(dss[h], k2[h]) for h in heads]
        dk2 = [_dot_tn(dss[h], qs[h]) for h in heads]
        dv2 = [_dot_tn(prs[h], dos[h]) for h in heads]
        for h in heads:
            for p in range(2):
                ps = slice((2 * h + p) * PAIR, (2 * h + p + 1) * PAIR)
                dqp = jnp.where(halves[0], dqs[h][2 * p * WINDOW:(2 * p + 1) * WINDOW], dqs[h][(2 * p + 1) * WINDOW:(2 * p + 2) * WINDOW])
                emit(ps, dq_keep[:, ps])
                dq_keep[:, ps] = (_rope_t(dqp, cosc_ref[...], sinc_ref[...], first) * SCALE).astype(ACT)
        dks, dvs = [], []
        for h in heads:
            dks.append(dk_keep[:, hs[h]] + _rope_t(dk2[h][:WINDOW], cosp_ref[...], sinp_ref[...], first))
            dk_keep[:, hs[h]] = _rope_t(dk2[h][WINDOW:], cosc_ref[...], sinc_ref[...], first)
            dvs.append(dv_keep[:, hs[h]] + dv2[h][:WINDOW])
            dv_keep[:, hs[h]] = dv2[h][WINDOW:]
        both = lambda v: v + pltpu.roll(v, HEAD_DIM, 1)
        for h in range(0, len(heads), 2):
            at = h // 2 * PAIR
            emit(slice(d + at, d + at + PAIR), jnp.where(halves[0], both(dks[h]), both(dks[h + 1])))
            emit(slice(d + kd // 2 + at, d + kd // 2 + at + PAIR), jnp.where(halves[0], both(dvs[h]), both(dvs[h + 1])))
        dsink_ref[...] += dsink

    kv_c, kv_p = pl.BlockSpec((WINDOW, kd), cur), pl.BlockSpec((WINDOW, kd), prev)
    tab_c, tab_p = pl.BlockSpec((WINDOW, PAIR), cur), pl.BlockSpec((WINDOW, PAIR), prev)
    return _call(
        body, "attn_bwd", (nb + 1,),
        [pl.BlockSpec(memory_space=pltpu.SMEM), pl.BlockSpec((WINDOW, d), cur), kv_c, kv_p, kv_c, kv_p,
         pl.BlockSpec((WINDOW, d), cur), tab_c, tab_c, tab_p, tab_p],
        [pl.BlockSpec((WINDOW, n_out), late), _full((1, n_out)), _full((8, LANES))],
        [_sds((t, n_out), ACT), _sds((1, n_out), F32), _sds((8, LANES), F32)],
        (sinks, q, kd_, kd_, vd, vd, do, cos, sin, cos, sin), ("arbitrary",),
        scratch=[pltpu.VMEM((WINDOW, d), ACT), pltpu.VMEM((WINDOW, kd), F32), pltpu.VMEM((WINDOW, kd), F32)], jobs=jobs)


def _proj_res(a, w, b, g, x, name, jobs=()):
    t = a.shape[0]
    k, d = w.shape
    tm = _row_tile(t)
    has_bias = b is not None

    def body(*refs):
        a_ref, w_ref = refs[:2]
        b_ref = refs[2] if has_bias else None
        g_ref, x_ref, m_ref, xo_ref = refs[2 + has_bias:]
        m = _dot(a_ref[...], w_ref[...])
        if has_bias:
            m = m + b_ref[...]
        m_ref[...] = m.astype(ACT)
        xo_ref[...] = x_ref[...] + (m * _rstd(m)) * g_ref[...]

    ins = [a, w] + ([b] if has_bias else []) + [g, x]
    specs = [_rows(tm, k), _full((k, d))] + ([_full((1, d))] if has_bias else []) + [_full((1, d)), _rows(tm, d)]
    return _call(body, name, (t // tm,), specs, [_rows(tm, d), _rows(tm, d)], [_sds((t, d), ACT), _sds((t, d), F32)], ins,
                 ("parallel",), jobs=jobs)


def _post_bwd(dres, m, g, w, a, mode, name, jobs=()):
    t, d = dres.shape
    k = w.shape[0]
    tm = _row_tile(t)
    kc = _pick(k, (1408, 1024, 512))
    steps = t // tm

    def body(*refs):
        d_ref, m_ref, g_ref, w_ref, a_ref = refs[:5]
        da_ref, dw_ref, dg_ref = refs[5:8]
        db_ref, acc_ref = (refs[8] if mode == "attn" else None), refs[-1]
        i = pl.program_id(0)

        @pl.when(i == 0)
        def _():
            for ref in (dg_ref, acc_ref) + ((db_ref,) if mode == "attn" else ()):
                ref[...] = jnp.zeros_like(ref)

        dv, mv = d_ref[...], m_ref[...].astype(F32)
        r = _rstd(mv)
        mh = mv * r
        dg_ref[...] += jnp.sum(dv * mh, axis=0, keepdims=True)
        dm = _rms_bwd(mh, r, g_ref[...], dv)
        dmb = dm.astype(ACT)
        if mode == "attn":
            db_ref[...] += jnp.sum(dm, axis=0, keepdims=True)
        for c in range(0, k, kc):
            da = _dot_nt(dmb, w_ref[c:c + kc, :])
            da_ref[:, c:c + kc] = da.astype(ACT)
        acc_ref[...] += _dot_tn(a_ref[...], dmb)

        @pl.when(i == steps - 1)
        def _():
            dw_ref[...] = acc_ref[...].astype(ACT)

    ins = [dres, m, g, w, a]
    specs = [_rows(tm, d), _rows(tm, d), _full((1, d)), _full((k, d)), _rows(tm, k)]
    out_specs = [_rows(tm, k), _full((k, d)), _full((1, d))]
    out_shape = [_sds((t, k), ACT), _sds((k, d), ACT), _sds((1, d), F32)]
    if mode == "attn":
        out_specs.append(_full((1, d)))
        out_shape.append(_sds((1, d), F32))
    return _call(body, name, (steps,), specs, out_specs, out_shape, ins, ("arbitrary",), scratch=[pltpu.VMEM((k, d), F32)], jobs=jobs)


def _pre_bwd(dy, w, x, g, dres, name, h=None, jobs=()):
    t, d = x.shape
    tm = _row_tile(t)
    steps = t // tm
    n = dy.shape[1]

    def body(*refs):
        dy_ref, w_ref, x_ref, g_ref, dres_ref = refs[:5]
        dx_ref, dg_ref = refs[-4:-2] if h is not None else refs[-2:]
        i = pl.program_id(0)

        @pl.when(i == 0)
        def _():
            dg_ref[...] = jnp.zeros_like(dg_ref)
            if h is not None:
                refs[-1][...] = jnp.zeros_like(refs[-1])

        if w.ndim == 3:
            c = w.shape[2]
            dh = jnp.zeros((tm, d), F32)
            for j in range(w.shape[0]):
                dh = dh + _dot_nt(dy_ref[:, j * c:(j + 1) * c], w_ref[j])
        else:
            dh = _dot(dy_ref[...], w_ref[...])
        xv = x_ref[...]
        r = _rstd(xv)
        xh = xv * r
        dg_ref[...] += jnp.sum(dh * xh, axis=0, keepdims=True)
        dx_ref[...] = dres_ref[...] + _rms_bwd(xh, r, g_ref[...], dh)
        if h is not None:
            h_ref, dw_ref, acc_ref = refs[5], refs[-2], refs[-1]
            acc_ref[...] += _dot_tn(h_ref[...], dy_ref[...])

            @pl.when(i == steps - 1)
            def _():
                for j in range(w.shape[0]):
                    dw_ref[j] = acc_ref[:, j * c:(j + 1) * c].astype(ACT)

    specs = [_rows(tm, n), _resident(w.shape), _rows(tm, d), _full((1, d)), _rows(tm, d)]
    out_specs, out_shape, scratch = [_rows(tm, d), _full((1, d))], [_sds((t, d), F32), _sds((1, d), F32)], []
    if h is not None:
        specs, out_specs, out_shape = specs + [_rows(tm, d)], out_specs + [_full(w.shape)], out_shape + [_sds(w.shape, ACT)]
        scratch = [pltpu.VMEM((d, n), F32)]
    return _call(body, name, (steps,), specs, out_specs, out_shape, (dy, w, x, g, dres) + ((h,) if h is not None else ()),
                 ("arbitrary",), scratch=scratch, jobs=jobs)


def _ffn_bwd(dres, f, g_post, w_dn, gu, w_gu, x, g_pre, name, target=None, jobs=()):
    t, d = x.shape
    n = w_dn.shape[0]
    tm = _row_tile(t, 256)
    whole = target is not None
    n_in = 6 if whole else 8

    def body(*refs):
        if whole:
            t_ref, gp_ref, wd_ref, wu_ref, x_ref, g_ref = refs[:n_in]
            df_ref, dgu_ref, dx_ref, dgp_ref, dg_ref, loss_ref, h_ref, a_ref, gu_ref = refs[n_in:]
        else:
            d_ref, f_ref, gp_ref, wd_ref, gu_ref, wu_ref, x_ref, g_ref = refs[:n_in]
            df_ref, dgu_ref, dx_ref, dgp_ref, dg_ref = refs[n_in:]

        @pl.when(pl.program_id(0) == 0)
        def _():
            for ref in (dgp_ref, dg_ref) + ((loss_ref,) if whole else ()):
                ref[...] = jnp.zeros_like(ref)

        xv = x_ref[...]
        rx = _rstd(xv)
        xh = xv * rx
        if whole:
            hb = (xh * g_ref[...]).astype(ACT)
            h_ref[...] = hb
            for c, size in _mxu_chunks(n):
                gate = _dot_nt(hb, wu_ref[c:c + size, :])
                up = _dot_nt(hb, wu_ref[n + c:n + c + size, :])
                gu_ref[:, c:c + size] = gate.astype(ACT)
                gu_ref[:, n + c:n + c + size] = up.astype(ACT)
                a_ref[:, c:c + size] = (gate * _sigmoid(gate) * up).astype(ACT)
            fv = _dot(a_ref[...], wd_ref[...])
            r = _rstd(fv)
            fh = fv * r
            err = xv + fh * gp_ref[...] - t_ref[...]
            dv = err * (1.0 / d)
            loss_ref[...] += 0.5 * jnp.sum(jnp.mean(err * err, axis=-1, keepdims=True), axis=0, keepdims=True)
        else:
            dv, fv = d_ref[...], f_ref[...].astype(F32)
            r = _rstd(fv)
            fh = fv * r
        dgp_ref[...] += jnp.sum(dv * fh, axis=0, keepdims=True)
        dfb = _rms_bwd(fh, r, gp_ref[...], dv).astype(ACT)
        df_ref[...] = dfb
        for c, size in _mxu_chunks(n):
            da = _dot_nt(dfb, wd_ref[c:c + size, :]).astype(ACT)
            gate, up = gu_ref[:, c:c + size], gu_ref[:, n + c:n + c + size]
            sg = _sigmoid(gate.astype(F32)).astype(ACT)
            gs = gate * sg
            dgu_ref[:, c:c + size] = da * up * (sg + gs - gs * sg)
            dgu_ref[:, n + c:n + c + size] = da * gs
        dh = _dot(dgu_ref[...], wu_ref[...])
        dg_ref[...] += jnp.sum(dh * xh, axis=0, keepdims=True)
        dx_ref[...] = dv + _rms_bwd(xh, rx, g_ref[...], dh)

    w_specs = [_resident(w_dn.shape), _resident(w_gu.shape)]
    out_specs = [_rows(tm, d), _rows(tm, 2 * n), _rows(tm, d), _full((1, d)), _full((1, d))]
    out_shape = [_sds((t, d), ACT), _sds((t, 2 * n), ACT), _sds((t, d), F32), _sds((1, d), F32), _sds((1, d), F32)]
    if whole:
        return _call(
            body, name, (t // tm,),
            [_rows(tm, d), _full((1, d))] + w_specs + [_rows(tm, d), _full((1, d))],
            out_specs + [_full((8, LANES)), _rows(tm, d), _rows(tm, n)],
            out_shape + [_sds((8, LANES), F32), _sds((t, d), ACT), _sds((t, n), ACT)],
            (target, g_post, w_dn, w_gu, x, g_pre), ("arbitrary",), scratch=[pltpu.VMEM((tm, 2 * n), ACT)], jobs=jobs)
    return _call(
        body, name, (t // tm,),
        [_rows(tm, d), _rows(tm, d), _full((1, d)), w_specs[0], _rows(tm, 2 * n), w_specs[1], _rows(tm, d), _full((1, d))],
        out_specs, out_shape, (dres, f, g_post, w_dn, gu, w_gu, x, g_pre), ("arbitrary",), jobs=jobs)


def _wgrad(a, b, name, out_dtype=F32, transposed=False, jobs=()):
    t = a.shape[0]
    tt = _pick(t, (1024,))
    steps = t // tt
    tk = _pick(a.shape[1], (2816, 1024, 1408))
    k, n = a.shape[1], b.shape[1]
    tn = _pick(n, (2816, 1024, 1408))
    if transposed:
        out_spec, out_shape, acc_shape = pl.BlockSpec((tn, tk), lambda i, j, s: (j, i)), _sds((n, k), out_dtype), (tn, tk)
    else:
        out_spec, out_shape, acc_shape = pl.BlockSpec((tk, tn), lambda i, j, s: (i, j)), _sds((k, n), out_dtype), (tk, tn)

    def body(a_ref, b_ref, o_ref, acc_ref):
        s = pl.program_id(2)

        @pl.when(s == 0)
        def _():
            acc_ref[...] = jnp.zeros_like(acc_ref)

        acc_ref[...] += _dot_tn(b_ref[...], a_ref[...]) if transposed else _dot_tn(a_ref[...], b_ref[...])

        @pl.when(s == steps - 1)
        def _():
            o_ref[...] = acc_ref[...].astype(out_dtype)

    res, job_res = _call(
        body, name, (k // tk, n // tn, steps),
        [pl.BlockSpec((tt, tk), lambda i, j, s: (s, i)), pl.BlockSpec((tt, tn), lambda i, j, s: (s, j))], [out_spec], [out_shape],
        (a, b), ("parallel", "parallel", "arbitrary"), scratch=[pltpu.VMEM(acc_shape, F32)], jobs=jobs)
    return res[0], job_res


def _ffn_fwd(x, g_pre, w_gu, w_dn, g_post, jobs=()):
    t, d = x.shape
    n = w_dn.shape[0]
    tm = _row_tile(t)

    def body(x_ref, g_ref, wu_ref, wd_ref, gp_ref, h_ref, gu_ref, a_ref, f_ref, xo_ref):
        xv = x_ref[...]
        hb = (xv * _rstd(xv) * g_ref[...]).astype(ACT)
        h_ref[...] = hb
        for c, size in _mxu_chunks(n):
            gate = _dot_nt(hb, wu_ref[c:c + size, :])
            up = _dot_nt(hb, wu_ref[n + c:n + c + size, :])
            gu_ref[:, c:c + size] = gate.astype(ACT)
            gu_ref[:, n + c:n + c + size] = up.astype(ACT)
            a_ref[:, c:c + size] = (gate * _sigmoid(gate) * up).astype(ACT)
        f = _dot(a_ref[...], wd_ref[...])
        f_ref[...] = f.astype(ACT)
        xo_ref[...] = xv + (f * _rstd(f)) * gp_ref[...]

    return _call(
        body, "ffn_fwd", (t // tm,),
        [_rows(tm, d), _full((1, d)), _resident(w_gu.shape), _resident(w_dn.shape), _full((1, d))],
        [_rows(tm, d), _rows(tm, 2 * n), _rows(tm, n), _rows(tm, d), _rows(tm, d)],
        [_sds((t, d), ACT), _sds((t, 2 * n), ACT), _sds((t, n), ACT), _sds((t, d), ACT), _sds((t, d), F32)],
        (x, g_pre, w_gu, w_dn, g_post), ("parallel",), jobs=jobs)


def _causal(ws):
    row = lax.broadcasted_iota(jnp.int32, ws.shape, 0)
    col = lax.broadcasted_iota(jnp.int32, ws.shape, 1)
    return jnp.where(col <= row, ws, 0.0)


def _sgu_ln(v, lg, lb):
    mu = jnp.mean(v, axis=-1, keepdims=True)
    vc = v - mu
    rs = lax.rsqrt(jnp.mean(vc * vc, axis=-1, keepdims=True) + EPS)
    vh = vc * rs
    return vh, rs, vh * lg + lb


def _sgu_fwd(x, g, w, lg, lb, ws, bs_t, jobs=()):
    t, d = x.shape
    nb, _, c = w.shape
    n = nb * c
    groups = d // WINDOW
    tm = _row_tile(t)

    def body(x_ref, g_ref, w_ref, lg_ref, lb_ref, ws_ref, bs_ref, h_ref, z_ref, y_ref, zf_ref):
        xv = x_ref[...]
        hb = (xv * _rstd(xv) * g_ref[...]).astype(ACT)
        h_ref[...] = hb
        for j in range(nb):
            zf_ref[:, j * c:(j + 1) * c] = _dot(hb, w_ref[j])
        z_ref[...] = zf_ref[...].astype(ACT)
        gs = [slice(gi * WINDOW, (gi + 1) * WINDOW) for gi in range(groups)]
        wsg = [_causal(ws_ref[gi]).astype(ACT) for gi in range(groups)]
        for r in range(0, tm, WINDOW):
            u = _gelu(zf_ref[r:r + WINDOW, :d])
            _, _, vn = _sgu_ln(_gelu(zf_ref[r:r + WINDOW, d:]), lg_ref[...], lb_ref[...])
            mixed = [_dot(wsg[gi], vn[:, gs[gi]].astype(ACT)) for gi in range(groups)]
            for gi in range(groups):
                y_ref[r:r + WINDOW, gs[gi]] = (u[:, gs[gi]] * (mixed[gi] + bs_ref[:, gi:gi + 1])).astype(ACT)

    vec = _full((1, d))
    return _call(
        body, "sgu_fwd", (t // tm,),
        [_rows(tm, d), vec, _full((nb, d, c)), vec, vec, _full((groups, WINDOW, WINDOW)), _full((WINDOW, groups))],
        [_rows(tm, d), _rows(tm, n), _rows(tm, d)], [_sds((t, d), ACT), _sds((t, n), ACT), _sds((t, d), ACT)],
        (x, g, w, lg, lb, ws, bs_t), ("parallel",), scratch=[pltpu.VMEM((tm, n), F32)], jobs=jobs)


def _sgu_mix_bwd(z, dy, lg, lb, ws, bs_t, jobs=()):
    t, n = z.shape
    d = n // 2
    groups = d // WINDOW
    tm = _row_tile(t)

    def body(z_ref, dy_ref, lg_ref, lb_ref, ws_ref, bs_ref, dz_ref, dlg_ref, dlb_ref, dws_ref, dbs_ref):
        @pl.when(pl.program_id(0) == 0)
        def _():
            for ref in (dlg_ref, dlb_ref, dws_ref, dbs_ref):
                ref[...] = jnp.zeros_like(ref)

        lane = lax.broadcasted_iota(jnp.int32, (WINDOW, groups), 1)
        gs = [slice(gi * WINDOW, (gi + 1) * WINDOW) for gi in range(groups)]
        wsg = [_causal(ws_ref[gi]).astype(ACT) for gi in range(groups)]
        for c in range(0, tm, WINDOW):
            rows = slice(c, c + WINDOW)
            zu, zv = z_ref[rows, :d].astype(F32), z_ref[rows, d:].astype(F32)
            u = _gelu(zu)
            vh, rs, vn = _sgu_ln(_gelu(zv), lg_ref[...], lb_ref[...])
            dyv = dy_ref[rows, :].astype(F32)
            vng = [vn[:, gs[gi]].astype(ACT) for gi in range(groups)]
            dmix = dyv * u
            dmb = [dmix[:, gs[gi]].astype(ACT) for gi in range(groups)]
            mixed = [_dot(wsg[gi], vng[gi]) for gi in range(groups)]
            dvn_parts = [_dot_tn(wsg[gi], dmb[gi]) for gi in range(groups)]
            dws_parts = [_dot_nt(dmb[gi], vng[gi]) for gi in range(groups)]
            for gi in range(groups):
                dz_ref[rows, gs[gi]] = (dyv[:, gs[gi]] * (mixed[gi] + bs_ref[:, gi:gi + 1]) * _gelu_grad(zu[:, gs[gi]])).astype(ACT)
                dws_ref[:, gs[gi]] += _causal(dws_parts[gi])
                dbs_ref[...] += jnp.where(lane == gi, jnp.sum(dmix[:, gs[gi]], axis=-1, keepdims=True), 0.0)
            dvn = jnp.concatenate(dvn_parts, axis=-1)
            dlg_ref[...] += jnp.sum(dvn * vh, axis=0, keepdims=True)
            dlb_ref[...] += jnp.sum(dvn, axis=0, keepdims=True)
            dvh = dvn * lg_ref[...]
            dv = rs * (dvh - jnp.mean(dvh, axis=-1, keepdims=True) - vh * jnp.mean(dvh * vh, axis=-1, keepdims=True))
            dz_ref[rows, d:] = (dv * _gelu_grad(zv)).astype(ACT)

    vec = _full((1, d))
    return _call(
        body, "sgu_mix_bwd", (t // tm,),
        [_rows(tm, n), _rows(tm, d), vec, vec, _full((groups, WINDOW, WINDOW)), _full((WINDOW, groups))],
        [_rows(tm, n), vec, vec, _full((WINDOW, d)), _full((WINDOW, groups))],
        [_sds((t, n), ACT), _sds((1, d), F32), _sds((1, d), F32), _sds((WINDOW, d), F32), _sds((WINDOW, groups), F32)],
        (z, dy, lg, lb, ws, bs_t), ("arbitrary",), jobs=jobs)


AG_COPIES = 8


def _prepare(sources, dtypes, n_gather, name, ahead=None):
    n = len(sources)
    arrays, where = [], []
    for parts, layer in sources:
        found = []
        for part in parts:
            known = [i for i, v in enumerate(arrays) if v is part]
            if not known:
                arrays.append(part)
            found.append(known[0] if known else len(arrays) - 1)
        where.append((found, layer))
    shapes = [(sum(p.shape[1] for p in parts), parts[0].shape[2]) for parts, _ in sources]
    later = [i for i in range(len(arrays)) if all(i not in found for found, _ in where[:n_gather])]
    n_sems = 6 if ahead is not None else 3

    def body(*refs):
        ins, refs = list(refs[:len(arrays)]), refs[len(arrays):]
        stage, outs = refs[:n], refs[n:n + n_gather]
        sems = refs[len(refs) - n_sems - len(later) - 1:len(refs) - len(later) - 1]
        send, recv, local_sem = sems[:3]
        fetches = []
        for j, i in enumerate(later):
            held = refs[len(refs) - len(later) - 1 + j]
            fetches.append(pltpu.make_async_copy(ins[i], held, refs[-1].at[j]))
            fetches[-1].start()
            ins[i] = held
        x, y, c = _place()
        me, sibling, beside_x, beside_y, far = (x, y, c), (x, y, 1 - c), (1 - x, y, c), (x, 1 - y, c), (1 - x, 1 - y, c)
        slot = lambda dev: 4 * dev[0] + 2 * dev[1] + dev[2]
        other = lambda dev: (dev[0], dev[1], 1 - c)
        whole = lambda a: (0, shapes[a][0])
        cuts = [rows // 2 if rows % 32 == 0 else rows for rows, _ in shapes]
        first = lambda a: (0, cuts[a])
        rest = lambda a: (cuts[a], shapes[a][0] - cuts[a])

        def copy(a, k, block, rows, to, src=None):
            dst = outs[a].at[block, pl.ds(*rows)]
            return pltpu.make_async_remote_copy(
                src_ref=dst if src is None else src, dst_ref=dst, send_sem=send.at[a * AG_COPIES + k],
                recv_sem=recv.at[a * AG_COPIES + k], device_id=to, device_id_type=MESH)

        def cast(a):
            found, layer = where[a]
            at = 0
            for i in found:
                rows = arrays[i].shape[1]
                stage[a][at:at + rows, :] = ins[i][layer].astype(dtypes[a])
                at += rows

        for a in range(n_gather):
            cast(a)
        started, ahead_sends = [], []
        for a in range(n_gather):
            own = pltpu.make_async_copy(stage[a], outs[a].at[slot(me)], local_sem.at[a])
            own.start()
            started.append(own)
        sends = []
        for a in range(n_gather):
            sends += [copy(a, k, slot(me), whole(a), to, src=stage[a]) for k, to in enumerate((sibling, beside_x, beside_y))]
        for cp in sends:
            cp.start()
        for cp in fetches:
            cp.wait()
        for a in range(n_gather, n):
            cast(a)
        if ahead is not None:
            block, pieces = ahead
            out, (send2, recv2, local2) = refs[n + n_gather], sems[3:]
            for i, piece in enumerate(pieces):
                src, dst = stage[block].at[pl.ds(*piece)], out.at[slot(me), pl.ds(*piece)]
                own = pltpu.make_async_copy(src, dst, local2.at[i])
                own.start()
                started.append(own)
                for k, to in enumerate((sibling, beside_x, beside_y)):
                    cp = pltpu.make_async_remote_copy(src_ref=src, dst_ref=dst, send_sem=send2.at[3 * i + k],
                                                      recv_sem=recv2.at[3 * i + k], device_id=to, device_id_type=MESH)
                    cp.start()
                    ahead_sends.append(cp)
        for a in range(n_gather):
            copy(a, 1, slot(beside_x), whole(a), me).wait_recv()
            copy(a, 2, slot(beside_y), whole(a), me).wait_recv()
            passed = [copy(a, 3, slot(beside_x), first(a), beside_y), copy(a, 5, slot(beside_x), whole(a), sibling),
                      copy(a, 6, slot(beside_y), whole(a), sibling)]
            if rest(a)[1]:
                passed.append(copy(a, 4, slot(beside_y), rest(a), beside_x))
            for cp in passed:
                cp.start()
            sends += passed
        for a in range(n_gather):
            copy(a, 3, slot(far), first(a), me).wait_recv()
            if rest(a)[1]:
                copy(a, 4, slot(far), rest(a), me).wait_recv()
            passed = copy(a, 7, slot(far), whole(a), sibling)
            passed.start()
            sends.append(passed)
        for a in range(n_gather):
            for k, source in ((0, me), (5, beside_x), (6, beside_y), (7, far)):
                copy(a, k, slot(other(source)), whole(a), me).wait_recv()
        for cp in sends:
            cp.wait_send()
        for cp in ahead_sends:
            cp.wait()
        for own in started:
            own.wait()

    gathered = list(range(n_gather)) + ([ahead[0]] if ahead is not None else [])
    sems = [pltpu.SemaphoreType.DMA((n_gather * AG_COPIES,)), pltpu.SemaphoreType.DMA((n_gather * AG_COPIES,)),
            pltpu.SemaphoreType.DMA((n_gather,))]
    if ahead is not None:
        sems += [pltpu.SemaphoreType.DMA((3 * len(ahead[1]),)), pltpu.SemaphoreType.DMA((3 * len(ahead[1]),)),
                 pltpu.SemaphoreType.DMA((len(ahead[1]),))]
    res = pl.pallas_call(
        body, name=name,
        in_specs=[ANY if i in later else IN_VMEM for i in range(len(arrays))], out_specs=[IN_VMEM] * n + [ANY] * len(gathered),
        out_shape=[_sds(s, dt) for s, dt in zip(shapes, dtypes)] + [_sds((N_DEV,) + shapes[a], dtypes[a]) for a in gathered],
        scratch_shapes=sems + [pltpu.VMEM(arrays[i].shape, arrays[i].dtype) for i in later] + [pltpu.SemaphoreType.DMA((len(later),))],
        compiler_params=pltpu.CompilerParams(vmem_limit_bytes=VMEM_LIMIT_BYTES),
    )(*arrays)
    return list(res[:n]), list(res[n:])


def _pair_sum(g, r, core, name):
    _, _, rows, cols = g.shape
    tr = _pick(rows, (512, 256, 128))

    def body(core_ref, g_ref, r_ref, p_ref):
        del core_ref
        p_ref[...] = (g_ref[...].astype(F32) + r_ref[...].astype(F32)).astype(p_ref.dtype)

    return pl.pallas_call(
        body, name=name,
        grid_spec=pltpu.PrefetchScalarGridSpec(
            num_scalar_prefetch=1, grid=(4, rows // tr),
            in_specs=[pl.BlockSpec((None, None, tr, cols), lambda q, i, core_ref: (q, core_ref[0], i, 0)),
                      pl.BlockSpec((None, tr, cols), lambda q, i, core_ref: (q, i, 0))],
            out_specs=pl.BlockSpec((None, tr, cols), lambda q, i, core_ref: (q, i, 0))),
        out_shape=_sds((4, rows, cols), g.dtype),
        compiler_params=_params("parallel", "parallel"),
    )(core, g, r)


def _adam(w, g, m, v):
    m2 = ADAM_B1 * m + (1.0 - ADAM_B1) * g
    v2 = ADAM_B2 * v + (1.0 - ADAM_B2) * (g * g)
    m_hat = m2 / (1.0 - ADAM_B1 ** ADAM_STEP)
    v_hat = v2 / (1.0 - ADAM_B2 ** ADAM_STEP)
    return -ADAM_LR * (m_hat / (jnp.sqrt(v_hat) + ADAM_EPS) + ADAM_WD * w), m2, v2


def _shard_update(own, index, received, w, m, v, layer, so_far, name):
    _, rows, cols = w.shape
    n_recv = received.shape[0]
    tr = _shard_tile(rows)

    def body(index_ref, p_ref, q_ref, w_ref, m_ref, v_ref, *rest):
        del index_ref
        g_out, d_out, m_out, v_out = rest[-4:]
        g = p_ref[...].astype(F32)
        for s in range(n_recv):
            g = g + q_ref[s].astype(F32)
        g_out[...] = g
        d_out[...], m_out[...], v_out[...] = _adam(w_ref[...], g, m_ref[...], v_ref[...])

    tile = pl.BlockSpec((None, tr, cols), lambda i, index_ref: (layer, i, 0))
    kept = list(so_far) if so_far is not None else []
    return pl.pallas_call(
        body, name=name,
        grid_spec=pltpu.PrefetchScalarGridSpec(
            num_scalar_prefetch=1, grid=(rows // tr,),
            in_specs=[pl.BlockSpec((None, tr, cols), lambda i, index_ref: (index_ref[0], i, 0)),
                      pl.BlockSpec((n_recv, tr, cols), lambda i, index_ref: (0, i, 0)), tile, tile, tile] + [ANY] * len(kept),
            out_specs=[tile] * 4),
        out_shape=[_sds(w.shape, F32)] * 4,
        input_output_aliases={6 + i: i for i in range(len(kept))},
        compiler_params=_params("parallel"),
    )(index, own, received, w, m, v, *kept)


def _natural_pieces(shape, r, c):
    if tuple(shape[-2:]) == (r, c) and all(n == 1 for n in shape[:-2]):
        return [((0,) * (len(shape) - 2), (0, c))]
    groups, width = shape[1], shape[3]
    assert len(shape) == 4 and shape[0] == 1 and shape[2] == r and groups * width == c, (shape, r, c)
    return [((0, g), (g * width, width)) for g in range(groups)]


def _replicated_update(shares, where, params, name):
    flat = [a for p in params if p is not None for a in p]

    def body(s_ref, *refs):
        ins, outs = refs[:len(flat)], refs[len(flat):]
        total = s_ref[0]
        for dev in range(1, N_DEV):
            total = total + s_ref[dev]
        i_at = o_at = 0
        for ranges, p in zip(where, params):
            chunks = [total[r0:r0 + r, :c] for r0, r, c in ranges]
            g2d = chunks[0] if len(chunks) == 1 else jnp.concatenate(chunks, axis=1)
            if p is None:
                outs[o_at][...] = g2d
                o_at += 1
                continue
            w_ref, m_ref, v_ref = ins[i_at:i_at + 3]
            for idx, (c0, cols) in _natural_pieces(p[0].shape, *g2d.shape):
                at = idx + (slice(None), slice(None))
                g = g2d[:, c0:c0 + cols]
                outs[o_at][at] = g
                outs[o_at + 1][at], outs[o_at + 2][at], outs[o_at + 3][at] = _adam(w_ref[at], g, m_ref[at], v_ref[at])
            i_at, o_at = i_at + 3, o_at + 4

    out_shape = []
    for ranges, p in zip(where, params):
        out_shape += [_sds((ranges[0][1], sum(c for _, _, c in ranges)), F32)] if p is None else [_sds(p[0].shape, F32)] * 4
    res = pl.pallas_call(
        body, name=name, out_shape=out_shape, compiler_params=pltpu.CompilerParams(vmem_limit_bytes=VMEM_LIMIT_BYTES),
    )(shares, *flat)
    out, at = [], 0
    for p in params:
        out.append(list(res[at:at + (1 if p is None else 4)]))
        at += 1 if p is None else 4
    return out


def _rope_tables(t):
    half = HEAD_DIM // 2
    inv_freq = np.float32(ROPE_THETA) ** (-(np.arange(half, dtype=np.float32) * np.float32(2.0)) / np.float32(HEAD_DIM))
    ang = np.arange(t, dtype=np.float32)[:, None] * inv_freq[None, :].astype(np.float32)
    cos, sin = np.cos(ang).astype(np.float32), np.sin(ang).astype(np.float32)
    return (jnp.asarray(np.tile(np.concatenate([cos, cos], axis=1), (1, 2))),
            jnp.asarray(np.tile(np.concatenate([-sin, sin], axis=1), (1, 2))))


def _rows_full(gathered):
    return gathered.reshape(-1, gathered.shape[-1])


def _rows_blocked(full):
    return full.reshape(N_DEV, full.shape[0] // N_DEV, full.shape[1]).astype(ACT)


def _pack_rows(parts, width):
    rows, where, at = [], [], 0
    for v in parts:
        r, c = v.shape
        n_rows = -(-r // 8) * 8
        chunks = []
        for c0 in range(0, c, width):
            chunk = v[:, c0:c0 + width].astype(F32)
            rows.append(jnp.pad(chunk, ((0, n_rows - r), (0, width - chunk.shape[1]))))
            chunks.append((at, r, chunk.shape[1]))
            at += n_rows
        where.append(chunks)
    return jnp.concatenate(rows, axis=0), where


def _halves(block):
    half = block.shape[0] // 2
    return (0, half), (half, half)


def _whole(block):
    return (0, block.shape[0])


EARLY = ("attn_w_qkv", "sgu_ln", "attn_w_o")
LATE = ("sgu_w_in", "sgu_w_out", "gu0", "gu1", "dn0", "dn1")
SWAPPED = ("attn_w_qkv", "ffn_w_gate_up")
BEFORE_ATTN = ("norm_mix_post", "norm_ffn_pre", "norm_ffn_post", "attn_b_o", "sgu_w_spatial", "sgu_b_spatial")
AFTER_ATTN = ("attn_b_qkv", "attn_sinks")
LAST = ("norm_mix_pre",)
WEIGHTS = ("norm_mix_pre", "norm_mix_post", "norm_ffn_pre", "norm_ffn_post", "attn_w_qkv", "attn_b_qkv", "attn_sinks", "attn_w_o",
           "attn_b_o", "sgu_w_in", "sgu_ln_g", "sgu_ln_b", "sgu_w_spatial", "sgu_b_spatial", "sgu_w_out", "ffn_w_gate_up", "ffn_w_down")


LAYER_OF = {"gu0": ("ffn_w_gate_up", 0), "gu1": ("ffn_w_gate_up", 1), "dn0": ("ffn_w_down", 0), "dn1": ("ffn_w_down", 1)}


def _source(tree, name):
    if name == "sgu_ln":
        return [tree["sgu_ln_g"][None], tree["sgu_ln_b"][None]], 0
    leaf, layer = LAYER_OF.get(name, (name, 0))
    return [tree[leaf]], layer


def kernel(x, norm_mix_pre, norm_mix_post, norm_ffn_pre, norm_ffn_post, attn_w_qkv, attn_b_qkv, attn_sinks, attn_w_o, attn_b_o, sgu_w_in, sgu_ln_g, sgu_ln_b, sgu_w_spatial, sgu_b_spatial, sgu_w_out, ffn_w_gate_up, ffn_w_down, loss_target, m_norm_mix_pre, m_norm_mix_post, m_norm_ffn_pre, m_norm_ffn_post, m_attn_w_qkv, m_attn_b_qkv, m_attn_sinks, m_attn_w_o, m_attn_b_o, m_sgu_w_in, m_sgu_ln_g, m_sgu_ln_b, m_sgu_w_spatial, m_sgu_b_spatial, m_sgu_w_out, m_ffn_w_gate_up, m_ffn_w_down, v_norm_mix_pre, v_norm_mix_post, v_norm_ffn_pre, v_norm_ffn_post, v_attn_w_qkv, v_attn_b_qkv, v_attn_sinks, v_attn_w_o, v_attn_b_o, v_sgu_w_in, v_sgu_ln_g, v_sgu_ln_b, v_sgu_w_spatial, v_sgu_b_spatial, v_sgu_w_out, v_ffn_w_gate_up, v_ffn_w_down):
    w = dict(norm_mix_pre=norm_mix_pre, norm_mix_post=norm_mix_post, norm_ffn_pre=norm_ffn_pre, norm_ffn_post=norm_ffn_post,
             attn_w_qkv=attn_w_qkv, attn_b_qkv=attn_b_qkv, attn_sinks=attn_sinks, attn_w_o=attn_w_o, attn_b_o=attn_b_o,
             sgu_w_in=sgu_w_in, sgu_ln_g=sgu_ln_g, sgu_ln_b=sgu_ln_b, sgu_w_spatial=sgu_w_spatial, sgu_b_spatial=sgu_b_spatial,
             sgu_w_out=sgu_w_out, ffn_w_gate_up=ffn_w_gate_up, ffn_w_down=ffn_w_down)
    mom = dict(norm_mix_pre=m_norm_mix_pre, norm_mix_post=m_norm_mix_post, norm_ffn_pre=m_norm_ffn_pre, norm_ffn_post=m_norm_ffn_post,
               attn_w_qkv=m_attn_w_qkv, attn_b_qkv=m_attn_b_qkv, attn_sinks=m_attn_sinks, attn_w_o=m_attn_w_o, attn_b_o=m_attn_b_o,
               sgu_w_in=m_sgu_w_in, sgu_ln_g=m_sgu_ln_g, sgu_ln_b=m_sgu_ln_b, sgu_w_spatial=m_sgu_w_spatial,
               sgu_b_spatial=m_sgu_b_spatial, sgu_w_out=m_sgu_w_out, ffn_w_gate_up=m_ffn_w_gate_up, ffn_w_down=m_ffn_w_down)
    var = dict(norm_mix_pre=v_norm_mix_pre, norm_mix_post=v_norm_mix_post, norm_ffn_pre=v_norm_ffn_pre, norm_ffn_post=v_norm_ffn_post,
               attn_w_qkv=v_attn_w_qkv, attn_b_qkv=v_attn_b_qkv, attn_sinks=v_attn_sinks, attn_w_o=v_attn_w_o, attn_b_o=v_attn_b_o,
               sgu_w_in=v_sgu_w_in, sgu_ln_g=v_sgu_ln_g, sgu_ln_b=v_sgu_ln_b, sgu_w_spatial=v_sgu_w_spatial,
               sgu_b_spatial=v_sgu_b_spatial, sgu_w_out=v_sgu_w_out, ffn_w_gate_up=v_ffn_w_gate_up, ffn_w_down=v_ffn_w_down)
    w, mom, var = [{**tree, **{n: jnp.swapaxes(tree[n], 1, 2) for n in SWAPPED}} for tree in (w, mom, var)]
    xs, target = x[0], loss_target[0]
    t, d = xs.shape
    row = lambda v: v.reshape(1, -1).astype(F32)
    core = lax.axis_index("c").astype(jnp.int32).reshape(1)
    chip = (2 * lax.axis_index("x") + lax.axis_index("y")).astype(jnp.int32).reshape(1)
    names = EARLY + LATE
    gu_rows = w["ffn_w_gate_up"].shape[1] // 4
    gu0_parts = [(i * gu_rows, gu_rows) for i in range(4)]
    staged, (*early, g_gu0) = _prepare([_source(w, n) for n in names], [F32 if n == "sgu_ln" else ACT for n in names], len(EARLY),
                                       "weights_prepare", ahead=(names.index("gu0"), gu0_parts[:2]))
    stage = dict(zip(names, staged))
    w_qkv = _rows_full(early[0])
    lg, lb = row(early[1][:, 0, :]), row(early[1][:, 1, :])
    w_o = _rows_full(early[2])
    b_qkv = row(attn_b_qkv)
    sinks = attn_sinks.reshape(1, -1).astype(F32)
    ws = sgu_w_spatial[0].astype(F32)
    bs_t = sgu_b_spatial[0].astype(F32).T
    cos, sin = _rope_tables(t)
    gather = lambda name, so_far, **pieces: _gather_job(stage[name], so_far, **pieces)
    a_half = lambda name: _halves(stage[name])[0]
    b_half = lambda name: _halves(stage[name])[1]
    whole = lambda name: _whole(stage[name])

    def pieces(name, n):
        rows = stage[name].shape[0] // n
        return [(i * rows, rows) for i in range(n)]

    ways = lambda parts: [(piece, i % 2) for i, piece in enumerate(parts)]
    relay = lambda name, so_far, **steps: _relay_gather_job(stage[name], so_far, **steps)
    gu1_parts = pieces("gu1", 4)
    dn0_parts, dn1_parts, in_parts, out_parts = pieces("dn0", 2), pieces("dn1", 2), pieces("sgu_w_in", 2), pieces("sgu_w_out", 2)
    (h0, q, kdup, vdup), ((g_gu0,),) = _attn_qkv_fwd(
        xs, row(norm_mix_pre[0]), w_qkv, b_qkv, cos, sin, jobs=[relay("gu0", g_gu0, sends=gu0_parts[2:])])
    (o,), ((g_gu0,), (g_dn0,)) = _attn_fwd(q, kdup, vdup, sinks, jobs=[
        relay("gu0", g_gu0, relays=ways(gu0_parts), forwards=gu0_parts), relay("dn0", None, sends=dn0_parts)])
    (m0, x1), ((g_gu0,), (g_dn0,), (g_in,)) = _proj_res(o, w_o, row(attn_b_o), row(norm_mix_post[0]), xs, "attn_out_fwd", jobs=[
        relay("gu0", g_gu0, far=gu0_parts), relay("dn0", g_dn0, relays=ways(dn0_parts), forwards=dn0_parts, late_far=dn0_parts),
        relay("sgu_w_in", None, sends=in_parts)])
    w_gu0, w_dn0 = _rows_full(g_gu0), _rows_full(g_dn0)
    (h1, gu0, a0, f0, x2), ((g_in,), (g_out,), (g_gu1,), (g_dn1,)) = _ffn_fwd(
        x1, row(norm_ffn_pre[0]), w_gu0, w_dn0, row(norm_ffn_post[0]), jobs=[
            relay("sgu_w_in", g_in, relays=ways(in_parts), forwards=in_parts, late_far=in_parts),
            relay("sgu_w_out", None, sends=out_parts), relay("gu1", None, sends=gu1_parts), relay("dn1", None, sends=dn1_parts)])
    w_in = g_in
    (h2, z, y), ((g_out,), (g_gu1,), (g_dn1,)) = _sgu_fwd(x2, row(norm_mix_pre[1]), w_in, lg, lb, ws, bs_t, jobs=[
        relay("sgu_w_out", g_out, relays=ways(out_parts), forwards=out_parts, late_far=out_parts),
        relay("gu1", g_gu1, relays=ways(gu1_parts), forwards=gu1_parts),
        relay("dn1", g_dn1, relays=ways(dn1_parts), forwards=dn1_parts)])
    w_out = _rows_full(g_out)
    (m1, x3), ((g_gu1,), (g_dn1,)) = _proj_res(y, w_out, None, row(norm_mix_post[1]), x2, "sgu_out_fwd", jobs=[
        relay("gu1", g_gu1, far=gu1_parts), relay("dn1", g_dn1, far=dn1_parts)])
    w_gu1, w_dn1 = _rows_full(g_gu1), _rows_full(g_dn1)

    to_sibling = lambda g: _scatter_job([g], 4, _to_sibling)
    pair = lambda g, r, name: _pair_sum(g.reshape((4, 2) + g.shape[1:]), r, core, "pair_sum_" + name)
    to_chips = lambda p, prev=None, piece=None: _scatter_job([p], 3, _to_owner_chip, prev=prev, piece=piece)
    out_g, out_d, out_m, out_v = {}, {}, {}, {}

    trees = [{**tree, "sgu_ln": jnp.stack([tree["sgu_ln_g"], tree["sgu_ln_b"]], axis=1)} for tree in (w, mom, var)]
    so_far = {}

    def update(name, own, index, received):
        leaf, layer = LAYER_OF.get(name, (name, 0))
        so_far[leaf] = _shard_update(own, index, received, *[tree[leaf] for tree in trees], layer, so_far.get(leaf), "update_" + name)
        for out, val in zip((out_g, out_d, out_m, out_v), so_far[leaf]):
            out[leaf] = val

    def finish(names, n_extra, shares, where, name):
        res = _replicated_update(shares, where, [(w[n], mom[n], var[n]) for n in names] + [None] * n_extra, name)
        for n, vals in zip(names, res):
            out_g[n], out_d[n], out_m[n], out_v[n] = vals
        return [vals[0] for vals in res[len(names):]]

    first_half = lambda p: _halves(p[0])[0]
    second_half = lambda p: _halves(p[0])[1]
    (df1, dgu1, dx3, dg_ffn_post1, dg_ffn_pre1, loss_tile, h3, a1), _ = _ffn_bwd(
        None, None, row(norm_ffn_post[1]), w_dn1, None, w_gu1, x3, row(norm_ffn_pre[1]), "ffn_last", target=target)
    b_gu1, _ = _wgrad(h3, dgu1, "ffn_up_wgrad1", ACT, transposed=True)
    b_gu1 = _rows_blocked(b_gu1)
    dw_dn1, _ = _wgrad(a1, df1, "ffn_down_wgrad1", ACT)
    b_dn1 = _rows_blocked(dw_dn1)
    (dy, dw_out, dg_mix_post1), ((r_gu1,), (r_dn1,)) = _post_bwd(
        dx3, m1, row(norm_mix_post[1]), w_out, y, "sgu", "sgu_out_bwd", jobs=[to_sibling(b_gu1), to_sibling(b_dn1)])
    p_gu1, p_dn1 = pair(b_gu1, r_gu1, "gu1"), pair(b_dn1, r_dn1, "dn1")
    b_out = _rows_blocked(dw_out)
    (dz, dlg, dlb, dws, dbs_t), ((q_gu1,), (r_out,)) = _sgu_mix_bwd(z, dy, lg, lb, ws, bs_t, jobs=[
        to_chips(p_gu1, piece=first_half(p_gu1)), to_sibling(b_out)])
    p_out = pair(b_out, r_out, "sgu_w_out")
    b_ln = jnp.stack([dlg.reshape(N_DEV, -1), dlb.reshape(N_DEV, -1)], axis=1)
    (dx2, dg_mix_pre1, b_in), _ = _pre_bwd(dz, w_in, x2, row(norm_mix_pre[1]), dx3, "sgu_in_bwd", h=h2)
    (df0, dgu0, dx1, dg_ffn_post0, dg_ffn_pre0), ((q_gu1,), (q_dn1,), (q_out,), (r_in,), (r_ln,)) = _ffn_bwd(
        dx2, f0, row(norm_ffn_post[0]), w_dn0, gu0, w_gu0, x1, row(norm_ffn_pre[0]), "ffn_bwd0",
        jobs=[to_chips(p_gu1, prev=[q_gu1], piece=second_half(p_gu1)), to_chips(p_dn1), to_chips(p_out), to_sibling(b_in),
              to_sibling(b_ln)])
    update("gu1", p_gu1, chip, q_gu1)
    update("dn1", p_dn1, chip, q_dn1)
    update("sgu_w_out", p_out, chip, q_out)
    p_in, p_ln = pair(b_in, r_in, "sgu_w_in"), pair(b_ln, r_ln, "sgu_ln")
    dw_dn0, _ = _wgrad(a0, df0, "ffn_down_wgrad0", ACT)
    b_dn0 = _rows_blocked(dw_dn0)
    b_gu0, ((q_in,), (q_ln,), (r_dn0,)) = _wgrad(h1, dgu0, "ffn_up_wgrad0", ACT, transposed=True, jobs=[
        to_chips(p_in), to_chips(p_ln), to_sibling(b_dn0)])
    update("sgu_w_in", p_in, chip, q_in)
    update("sgu_ln", p_ln, chip, q_ln)
    b_gu0 = _rows_blocked(b_gu0)
    p_dn0 = pair(b_dn0, r_dn0, "dn0")
    (do, dw_o, dg_mix_post0, db_o), ((r_gu0,), (q_dn0,)) = _post_bwd(
        dx1, m0, row(norm_mix_post[0]), w_o, o, "attn", "attn_out_bwd", jobs=[to_sibling(b_gu0), to_chips(p_dn0, piece=first_half(p_dn0))])
    p_gu0 = pair(b_gu0, r_gu0, "gu0")
    b_o = _rows_blocked(dw_o)
    grads = {
        "norm_mix_post": jnp.concatenate([dg_mix_post0, dg_mix_post1], axis=0),
        "norm_ffn_pre": jnp.concatenate([dg_ffn_pre0, dg_ffn_pre1], axis=0),
        "norm_ffn_post": jnp.concatenate([dg_ffn_post0, dg_ffn_post1], axis=0),
        "attn_b_o": db_o,
        "sgu_w_spatial": dws,
        "sgu_b_spatial": dbs_t.T,
    }
    shares1, where1 = _pack_rows([grads[name] for name in BEFORE_ATTN], d)
    (dqkv, db_qkv, dsink), ((q_gu0,), (q_dn0,), (r_o,), (g_shares1,)) = _attn_bwd(q, kdup, vdup, do, sinks, cos, sin, jobs=[
        to_chips(p_gu0), to_chips(p_dn0, prev=[q_dn0], piece=second_half(p_dn0)), to_sibling(b_o),
        _gather_job(shares1, None, sends=[_whole(shares1)])])
    update("gu0", p_gu0, chip, q_gu0)
    update("dn0", p_dn0, chip, q_dn0)
    p_o = pair(b_o, r_o, "attn_w_o")
    grads.update({"attn_b_qkv": db_qkv, "attn_sinks": dsink[:1, :d // HEAD_DIM]})
    shares2, where2 = _pack_rows([grads[name] for name in AFTER_ATTN] + [loss_tile[:1, :1]], d)
    (dx0, dg_mix_pre0), _ = _pre_bwd(dqkv, w_qkv, xs, row(norm_mix_pre[0]), dx1, "attn_qkv_bwd")
    grads["norm_mix_pre"] = jnp.concatenate([dg_mix_pre0, dg_mix_pre1], axis=0)
    shares3, where3 = _pack_rows([grads[name] for name in LAST], d)
    dw_qkv, ((q_o,), (g_shares1,), (g_shares2,), (g_shares3,)) = _wgrad(h0, dqkv, "attn_qkv_wgrad", ACT, transposed=True, jobs=[
        to_chips(p_o), _gather_job(shares1, g_shares1, forwards=[_whole(shares1)]),
        _gather_job(shares2, None, sends=[_whole(shares2)]), _gather_job(shares3, None, sends=[_whole(shares3)])])
    update("attn_w_o", p_o, chip, q_o)
    b_qkv_grad = _rows_blocked(dw_qkv)
    (r_qkv,), (g_shares2,), (g_shares3,) = _copies_only([
        to_sibling(b_qkv_grad), _gather_job(shares2, g_shares2, forwards=[_whole(shares2)]),
        _gather_job(shares3, g_shares3, forwards=[_whole(shares3)])], "grads_tail_to_sibling")
    p_qkv = pair(b_qkv_grad, r_qkv, "attn_w_qkv")
    (q_qkv,), = _copies_only([to_chips(p_qkv)], "grads_tail_to_owner_chip")
    update("attn_w_qkv", p_qkv, chip, q_qkv)

    finish(BEFORE_ATTN, 0, g_shares1, where1, "replicated_update_before_attn")
    loss = finish(AFTER_ATTN, 1, g_shares2, where2, "replicated_update_after_attn")[0][0, 0]
    finish(LAST, 0, g_shares3, where3, "replicated_update_last")

    for out in (out_g, out_d, out_m, out_v):
        out["sgu_ln_g"], out["sgu_ln_b"] = out["sgu_ln"][:, 0, :], out["sgu_ln"][:, 1, :]
        for n in SWAPPED:
            out[n] = jnp.swapaxes(out[n], 1, 2)

    return (loss, dx0[None], *[out_g[n] for n in WEIGHTS], *[out_d[n] for n in WEIGHTS],
            *[out_m[n] for n in WEIGHTS], *[out_v[n] for n in WEIGHTS])
```

```python
import functools
import math
import operator

import jax
import jax.numpy as jnp
import numpy as np
from jax import lax
from jax.experimental import pallas as pl
from jax.experimental.pallas import tpu as pltpu

F32 = jnp.float32
ACT = jnp.bfloat16

EPS = 1e-6
HEAD_DIM = 64
PAIR = 2 * HEAD_DIM
GQA_GROUP = 4
WINDOW = 128
ROPE_THETA = 10000.0
SCALE = HEAD_DIM ** -0.5
MASKED = -1e30
LANES = 128
MXU_WIDTH = 256

ADAM_LR, ADAM_B1, ADAM_B2, ADAM_EPS, ADAM_WD, ADAM_STEP = 0.001, 0.9, 0.999, 1e-08, 0.01, 10

N_DEV = 8
VMEM_LIMIT_BYTES = 56 * 1024 * 1024
MESH = pl.DeviceIdType.MESH
ANY = pl.BlockSpec(memory_space=pl.ANY)
IN_VMEM = pl.BlockSpec(memory_space=pltpu.VMEM)


def _pick(n, candidates):
    for c in candidates:
        if n % c == 0:
            return c
    return n


def _row_tile(t, most=512):
    return _pick(t, (most,))


def _shard_tile(rows):
    return next((c for c in (512, 256, 176, 128, 96, 64) if rows % c == 0 and rows >= 2 * c), rows)


def _mxu_chunks(n, most=4 * MXU_WIDTH):
    assert n % MXU_WIDTH == 0 and most % MXU_WIDTH == 0
    return [(c, min(most, n - c)) for c in range(0, n, most)]


def _params(*sem):
    return pltpu.CompilerParams(dimension_semantics=sem, vmem_limit_bytes=VMEM_LIMIT_BYTES)


def _full(shape):
    return pl.BlockSpec(shape, lambda *_: (0,) * len(shape))


def _resident(shape):
    return pl.BlockSpec(shape, lambda *_: (0,) * len(shape), pipeline_mode=pl.Buffered(1))


def _rows(tm, n):
    return pl.BlockSpec((tm, n), lambda i: (i, 0))


def _sds(shape, dtype):
    return jax.ShapeDtypeStruct(shape, dtype)


def _place():
    return lax.axis_index("x"), lax.axis_index("y"), lax.axis_index("c")


def _other_chips(x, y):
    return [(1 - x, y), (x, 1 - y), (1 - x, 1 - y)]


class _Job:
    def __init__(self, inputs, out_shape, aliases, emit, n_remote, n_local=0, n_late=0):
        self.inputs, self.out_shape, self.aliases, self.emit = list(inputs), list(out_shape), dict(aliases), emit
        self.n_remote, self.n_local, self.n_late = n_remote, n_local, n_late


def _call(body, name, grid, in_specs, out_specs, out_shape, args, sem, scratch=(), jobs=()):
    n_in, n_out, n_scr = len(args), len(out_shape), len(scratch)
    j_in = [a for job in jobs for a in job.inputs]
    j_out = [s for job in jobs for s in job.out_shape]
    aliases, at_in, at_out = {}, n_in, n_out
    for job in jobs:
        aliases.update({at_in + i: at_out + o for i, o in job.aliases.items()})
        at_in, at_out = at_in + len(job.inputs), at_out + len(job.out_shape)
    n_remote = sum(job.n_remote for job in jobs)
    n_local = sum(job.n_local for job in jobs)

    def wrapped(*refs):
        ins, jin = refs[:n_in], refs[n_in:n_in + len(j_in)]
        rest = refs[n_in + len(j_in):]
        outs, jout = rest[:n_out], rest[n_out:n_out + len(j_out)]
        scr = rest[n_out + len(j_out):n_out + len(j_out) + n_scr]
        if not jobs:
            body(*ins, *outs, *scr)
            return
        send, recv, local = rest[n_out + len(j_out) + n_scr:]
        ids = [pl.program_id(a) for a in range(len(grid))]
        first = functools.reduce(operator.and_, [i == 0 for i in ids])
        last = functools.reduce(operator.and_, [i == g - 1 for i, g in zip(ids, grid)])

        def descriptors():
            place, found, i, o, r, l = _place(), ([], []), 0, 0, 0, 0
            for job in jobs:
                emitted = job.emit(jin[i:i + len(job.inputs)], jout[o:o + len(job.out_shape)], place)
                for at, (src, dst, to) in enumerate(emitted):
                    if to is None:
                        cp = pltpu.make_async_copy(src, dst, local.at[l])
                        l += 1
                    else:
                        cp = pltpu.make_async_remote_copy(src_ref=src, dst_ref=dst, send_sem=send.at[r], recv_sem=recv.at[r],
                                                          device_id=to, device_id_type=MESH)
                        r += 1
                    found[at >= len(emitted) - job.n_late].append(cp)
                i, o = i + len(job.inputs), o + len(job.out_shape)
            return found

        @pl.when(first)
        def _():
            for cp in descriptors()[0]:
                cp.start()

        body(*ins, *outs, *scr)

        @pl.when(last)
        def _():
            early, late = descriptors()
            for cp in early:
                cp.wait()
            for cp in late:
                cp.start()
            for cp in late:
                cp.wait()

    sems = [pltpu.SemaphoreType.DMA((n_remote,)), pltpu.SemaphoreType.DMA((n_remote,)),
            pltpu.SemaphoreType.DMA((max(n_local, 1),))] if jobs else []
    res = pl.pallas_call(
        wrapped, name=name, grid=grid,
        in_specs=list(in_specs) + [ANY] * len(j_in), out_specs=list(out_specs) + [ANY] * len(j_out),
        out_shape=list(out_shape) + j_out, scratch_shapes=list(scratch) + sems, input_output_aliases=aliases,
        compiler_params=_params(*(("arbitrary",) * len(grid) if jobs else sem)),
    )(*args, *j_in)
    job_res, at = [], n_out
    for job in jobs:
        job_res.append(list(res[at:at + len(job.out_shape)]))
        at += len(job.out_shape)
    return list(res[:n_out]), job_res


def _copies_only(jobs, name):
    def body(o_ref):
        o_ref[...] = jnp.zeros_like(o_ref)

    return _call(body, name, (1,), [], [_full((8, LANES))], [_sds((8, LANES), F32)], (), ("arbitrary",), jobs=jobs)[1]


def _gather_job(stage, gathered, sends=(), forwards=()):
    shape = _sds((N_DEV,) + stage.shape, stage.dtype)
    inputs = ([stage] if sends else []) + ([gathered] if gathered is not None else [])
    aliases = {len(inputs) - 1: 0} if gathered is not None else {}

    def emit(ins, outs, place):
        x, y, c = place
        sibling, chips, mine, g = (x, y, 1 - c), _other_chips(x, y), 4 * x + 2 * y + c, outs[0]
        found = []
        for r0, nr in sends:
            src, dst = ins[0].at[pl.ds(r0, nr)], g.at[mine, pl.ds(r0, nr)]
            found += [(src, dst, None), (src, dst, sibling)] + [(src, dst, (px, py, c)) for px, py in chips]
        for r0, nr in forwards:
            for px, py in chips:
                block = 4 * px + 2 * py + c
                found.append((ins[-1].at[block, pl.ds(r0, nr)], g.at[block, pl.ds(r0, nr)], sibling))
        return found

    return _Job(inputs, [shape], aliases, emit, 4 * len(sends) + 3 * len(forwards), len(sends))


def _relay_gather_job(stage, gathered, sends=(), relays=(), forwards=(), far=(), late_far=()):
    shape = _sds((N_DEV,) + stage.shape, stage.dtype)
    inputs = ([stage] if sends else []) + ([gathered] if gathered is not None else [])
    aliases = {len(inputs) - 1: 0} if gathered is not None else {}

    def emit(ins, outs, place):
        x, y, c = place
        sibling, beside_x, beside_y, g = (x, y, 1 - c), (1 - x, y, c), (x, 1 - y, c), outs[0]
        slot = lambda dev: 4 * dev[0] + 2 * dev[1] + dev[2]
        found = []
        for r0, nr in sends:
            src, dst = ins[0].at[pl.ds(r0, nr)], g.at[slot((x, y, c)), pl.ds(r0, nr)]
            found += [(src, dst, None), (src, dst, sibling), (src, dst, beside_x), (src, dst, beside_y)]

        def passed_on(block, piece, to):
            return ins[-1].at[block, pl.ds(*piece)], g.at[block, pl.ds(*piece)], to

        for piece, way in relays:
            found.append(passed_on(slot(beside_x), piece, beside_y) if way == 0 else passed_on(slot(beside_y), piece, beside_x))
        for piece in forwards:
            found += [passed_on(slot(beside_x), piece, sibling), passed_on(slot(beside_y), piece, sibling)]
        for piece in tuple(far) + tuple(late_far):
            found.append(passed_on(slot((1 - x, 1 - y, c)), piece, sibling))
        return found

    n_remote = 3 * len(sends) + len(relays) + 2 * len(forwards) + len(far) + len(late_far)
    return _Job(inputs, [shape], aliases, emit, n_remote, len(sends), len(late_far))


def _scatter_job(arrays, n_slots, route, prev=None, piece=None):
    shapes = [_sds((n_slots,) + v.shape[1:], v.dtype) for v in arrays]
    inputs = list(arrays) + (list(prev) if prev else [])
    aliases = {len(arrays) + i: i for i in range(len(arrays))} if prev else {}

    def emit(ins, outs, place):
        found = []
        for a in range(len(arrays)):
            for s in range(n_slots):
                block, to = route(s, *place)
                if piece is None:
                    found.append((ins[a].at[block], outs[a].at[s], to))
                else:
                    found.append((ins[a].at[block, pl.ds(*piece)], outs[a].at[s, pl.ds(*piece)], to))
        return found

    return _Job(inputs, shapes, aliases, emit, len(arrays) * n_slots)


def _to_sibling(s, x, y, c):
    return 2 * s + (1 - c), (x, y, 1 - c)


def _to_owner_chip(s, x, y, c):
    px, py = _other_chips(x, y)[s]
    return 2 * px + py, (px, py, c)


def _rstd(x):
    return lax.rsqrt(jnp.mean(x * x, axis=-1, keepdims=True) + EPS)


def _rms_bwd(xhat, r, g, dy):
    dxh = dy * g
    return r * (dxh - xhat * jnp.mean(dxh * xhat, axis=-1, keepdims=True))


def _first_half(rows):
    lane = lax.broadcasted_iota(jnp.int32, (rows, PAIR), 1)
    return (lane % HEAD_DIM) < (HEAD_DIM // 2)


def _rot_half(v, first):
    return jnp.where(first, pltpu.roll(v, PAIR - HEAD_DIM // 2, 1), pltpu.roll(v, HEAD_DIM // 2, 1))


def _rope(v, cos, sin, first):
    return v * cos + _rot_half(v, first) * sin


def _rope_t(dv, cos, sin, first):
    return dv * cos + _rot_half(dv * sin, first)


def _dot(a, b):
    return jnp.dot(a, b, preferred_element_type=F32)


def _dot_nt(a, b):
    return lax.dot_general(a, b, (((1,), (1,)), ((), ())), preferred_element_type=F32)


def _dot_tn(a, b):
    return lax.dot_general(a, b, (((0,), (0,)), ((), ())), preferred_element_type=F32)


def _sigmoid(v):
    return 1.0 / (1.0 + jnp.exp(-v))


_GELU_K = math.sqrt(2.0 / math.pi)
_GELU_C = 0.044715


def _gelu(v):
    return v * (0.5 * (1.0 + jnp.tanh(_GELU_K * (v + _GELU_C * (v * v * v)))))


def _gelu_grad(v):
    t = jnp.tanh(_GELU_K * (v + _GELU_C * (v * v * v)))
    return 0.5 * (1.0 + t) + 0.5 * v * (1.0 - t * t) * (_GELU_K * (1.0 + 3.0 * _GELU_C * (v * v)))


def _attn_qkv_fwd(x, g, w, b, cos, sin, jobs=()):
    t, d = x.shape
    n = w.shape[0]
    kd = n - d
    assert (kd // 2) % PAIR == 0
    tm = _row_tile(t)
    ch = _pick(d, (512,))

    def body(x_ref, g_ref, w_ref, b_ref, cos_ref, sin_ref, h_ref, q_ref, k_ref, v_ref):
        xv = x_ref[...]
        hb = (xv * _rstd(xv) * g_ref[...]).astype(ACT)
        h_ref[...] = hb
        cosv, sinv = cos_ref[...], sin_ref[...]
        first = _first_half(tm)
        left = _lane_halves()[0]

        def proj(lo, width):
            return _dot_nt(hb, w_ref[lo:lo + width, :]) + b_ref[:, lo:lo + width]

        def twice(ref, s, two_heads):
            swapped = pltpu.roll(two_heads, HEAD_DIM, 1)
            ref[:, 2 * s:2 * s + PAIR] = jnp.where(left, two_heads, swapped).astype(ACT)
            ref[:, 2 * s + PAIR:2 * s + 2 * PAIR] = jnp.where(left, swapped, two_heads).astype(ACT)

        for c in range(0, d, ch):
            y = proj(c, ch)
            for s in range(0, ch, PAIR):
                q_ref[:, c + s:c + s + PAIR] = (_rope(y[:, s:s + PAIR], cosv, sinv, first) * SCALE).astype(ACT)
        y = proj(d, kd)
        for s in range(0, kd // 2, PAIR):
            twice(k_ref, s, _rope(y[:, s:s + PAIR], cosv, sinv, first))
            twice(v_ref, s, y[:, kd // 2 + s:kd // 2 + s + PAIR])

    return _call(
        body, "attn_qkv_fwd", (t // tm,),
        [_rows(tm, d), _full((1, d)), _full((n, d)), _full((1, n)), _rows(tm, PAIR), _rows(tm, PAIR)],
        [_rows(tm, d), _rows(tm, d), _rows(tm, kd), _rows(tm, kd)],
        [_sds((t, d), ACT), _sds((t, d), ACT), _sds((t, kd), ACT), _sds((t, kd), ACT)],
        (x, g, w, b, cos, sin), ("parallel",), jobs=jobs)


STACK = GQA_GROUP * WINDOW


def _band_mask(has_prev):
    row = lax.broadcasted_iota(jnp.int32, (STACK, 2 * WINDOW), 0) % WINDOW
    col = lax.broadcasted_iota(jnp.int32, (STACK, 2 * WINDOW), 1)
    return ((col < WINDOW) & (col > row) & has_prev) | ((col >= WINDOW) & (col - WINDOW <= row))


def _lane_halves():
    lane = lax.broadcasted_iota(jnp.int32, (1, PAIR), 1)
    return lane < HEAD_DIM, lane >= HEAD_DIM


def _stack_heads(ref, h, halves):
    parts = []
    for p in range(2):
        pair = ref[:, (2 * h + p) * PAIR:(2 * h + p + 1) * PAIR]
        parts += [jnp.where(half, pair, jnp.zeros_like(pair)) for half in halves]
    return jnp.concatenate(parts, axis=0)


def _sink_column(sink_ref, h):
    row = lax.broadcasted_iota(jnp.int32, (STACK, 1), 0)
    col = jnp.full((STACK, 1), sink_ref[0, GQA_GROUP * h + GQA_GROUP - 1], F32)
    for j in range(GQA_GROUP - 2, -1, -1):
        col = jnp.where(row < (j + 1) * WINDOW, sink_ref[0, GQA_GROUP * h + j], col)
    return col


def _probs(scores, valid, sink):
    s = jnp.where(valid, scores, MASKED)
    m = jnp.maximum(jnp.max(s, axis=-1, keepdims=True), sink)
    p = jnp.exp(s - m)
    psink = jnp.exp(sink - m)
    inv = 1.0 / (jnp.sum(p, axis=-1, keepdims=True) + psink)
    return p * inv, psink * inv


def _attn_fwd(q, kd_, vd, sinks, jobs=()):
    t, d = q.shape
    kd = kd_.shape[1]
    cur = lambda n: (n, 0)
    prev = lambda n: (jnp.maximum(n - 1, 0), 0)

    def body(sink_ref, q_ref, kc_ref, kp_ref, vc_ref, vp_ref, o_ref):
        valid = _band_mask(pl.program_id(0) > 0)
        halves = _lane_halves()
        heads = range(kd // PAIR)
        hs = [slice(h * PAIR, (h + 1) * PAIR) for h in heads]
        scores = [_dot_nt(_stack_heads(q_ref, h, halves), jnp.concatenate([kp_ref[:, hs[h]], kc_ref[:, hs[h]]], axis=0)) for h in heads]
        probs = [_probs(scores[h], valid, _sink_column(sink_ref, h))[0].astype(ACT) for h in heads]
        for h in heads:
            v2 = jnp.concatenate([vp_ref[:, hs[h]], vc_ref[:, hs[h]]], axis=0)
            vs = jnp.concatenate([jnp.where(half, v2, jnp.zeros_like(v2)) for half in halves], axis=0)
            pr = probs[h]
            for p in range(2):
                both = jnp.concatenate([pr[2 * p * WINDOW:(2 * p + 1) * WINDOW], pr[(2 * p + 1) * WINDOW:(2 * p + 2) * WINDOW]], axis=1)
                o_ref[:, (2 * h + p) * PAIR:(2 * h + p + 1) * PAIR] = _dot(both, vs).astype(ACT)

    kv_c, kv_p = pl.BlockSpec((WINDOW, kd), cur), pl.BlockSpec((WINDOW, kd), prev)
    return _call(
        body, "attn_fwd", (t // WINDOW,),
        [pl.BlockSpec(memory_space=pltpu.SMEM), pl.BlockSpec((WINDOW, d), cur), kv_c, kv_p, kv_c, kv_p],
        [pl.BlockSpec((WINDOW, d), cur)], [_sds((t, d), ACT)],
        (sinks, q, kd_, kd_, vd, vd), ("parallel",), jobs=jobs)


def _attn_bwd(q, kd_, vd, do, sinks, cos, sin, jobs=()):
    t, d = q.shape
    kd = kd_.shape[1]
    nb = t // WINDOW
    n_out = d + kd
    assert (kd // 2) % PAIR == 0
    cur = lambda n: (jnp.minimum(n, nb - 1), 0)
    prev = lambda n: (jnp.maximum(jnp.minimum(n, nb - 1) - 1, 0), 0)
    late = lambda n: (jnp.maximum(n - 1, 0), 0)

    def body(sink_ref, q_ref, kc_ref, kp_ref, vc_ref, vp_ref, do_ref, cosc_ref, sinc_ref, cosp_ref, sinp_ref,
             out_ref, db_ref, dsink_ref, dq_keep, dk_keep, dv_keep):
        n = pl.program_id(0)

        @pl.when(n == 0)
        def _():
            for ref in (db_ref, dsink_ref, dq_keep, dk_keep, dv_keep):
                ref[...] = jnp.zeros_like(ref)

        valid = _band_mask(n > 0) & (n < nb)
        halves = _lane_halves()
        first = _first_half(WINDOW)
        slot = lax.broadcasted_iota(jnp.int32, dsink_ref.shape, 1)
        top = lax.broadcasted_iota(jnp.int32, dsink_ref.shape, 0) == 0
        dsink = jnp.zeros(dsink_ref.shape, F32)

        def emit(cols, done):
            out_ref[:, cols] = done.astype(ACT)
            db_ref[:, cols] += jnp.sum(done.astype(F32), axis=0, keepdims=True)

        heads = range(kd // PAIR)
        hs = [slice(h * PAIR, (h + 1) * PAIR) for h in heads]
        k2 = [jnp.concatenate([kp_ref[:, hs[h]], kc_ref[:, hs[h]]], axis=0) for h in heads]
        v2 = [jnp.concatenate([vp_ref[:, hs[h]], vc_ref[:, hs[h]]], axis=0) for h in heads]
        qs = [_stack_heads(q_ref, h, halves) for h in heads]
        dos = [_stack_heads(do_ref, h, halves) for h in heads]
        scores = [_dot_nt(qs[h], k2[h]) for h in heads]
        dps = [_dot_nt(dos[h], v2[h]) for h in heads]
        prs, dss = [], []
        for h in heads:
            pr, psink = _probs(scores[h], valid, _sink_column(sink_ref, h))
            delta = jnp.sum(pr * dps[h], axis=-1, keepdims=True)
            dss.append((pr * (dps[h] - delta)).astype(ACT))
            prs.append(pr.astype(ACT))
            leak = psink * delta
            for j in range(GQA_GROUP):
                share = jnp.sum(leak[j * WINDOW:(j + 1) * WINDOW], axis=0, keepdims=True)
                dsink = dsink - jnp.where(top & (slot == GQA_GROUP * h + j), share, 0.0)
        dqs = [_dot(dss[h], k2[h]) for h in heads]
        dk2 = [_dot_tn(dss[h], qs[h]) for h in heads]
        dv2 = [_dot_tn(prs[h], dos[h]) for h in heads]
        for h in heads:
            for p in range(2):
                ps = slice((2 * h + p) * PAIR, (2 * h + p + 1) * PAIR)
                dqp = jnp.where(halves[0], dqs[h][2 * p * WINDOW:(2 * p + 1) * WINDOW], dqs[h][(2 * p + 1) * WINDOW:(2 * p + 2) * WINDOW])
                emit(ps, dq_keep[:, ps])
                dq_keep[:, ps] = (_rope_t(dqp, cosc_ref[...], sinc_ref[...], first) * SCALE).astype(ACT)
        dks, dvs = [], []
        for h in heads:
            dks.append(dk_keep[:, hs[h]] + _rope_t(dk2[h][:WINDOW], cosp_ref[...], sinp_ref[...], first))
            dk_keep[:, hs[h]] = _rope_t(dk2[h][WINDOW:], cosc_ref[...], sinc_ref[...], first)
            dvs.append(dv_keep[:, hs[h]] + dv2[h][:WINDOW])
            dv_keep[:, hs[h]] = dv2[h][WINDOW:]
        both = lambda v: v + pltpu.roll(v, HEAD_DIM, 1)
        for h in range(0, len(heads), 2):
            at = h // 2 * PAIR
            emit(slice(d + at, d + at + PAIR), jnp.where(halves[0], both(dks[h]), both(dks[h + 1])))
            emit(slice(d + kd // 2 + at, d + kd // 2 + at + PAIR), jnp.where(halves[0], both(dvs[h]), both(dvs[h + 1])))
        dsink_ref[...] += dsink

    kv_c, kv_p = pl.BlockSpec((WINDOW, kd), cur), pl.BlockSpec((WINDOW, kd), prev)
    tab_c, tab_p = pl.BlockSpec((WINDOW, PAIR), cur), pl.BlockSpec((WINDOW, PAIR), prev)
    return _call(
        body, "attn_bwd", (nb + 1,),
        [pl.BlockSpec(memory_space=pltpu.SMEM), pl.BlockSpec((WINDOW, d), cur), kv_c, kv_p, kv_c, kv_p,
         pl.BlockSpec((WINDOW, d), cur), tab_c, tab_c, tab_p, tab_p],
        [pl.BlockSpec((WINDOW, n_out), late), _full((1, n_out)), _full((8, LANES))],
        [_sds((t, n_out), ACT), _sds((1, n_out), F32), _sds((8, LANES), F32)],
        (sinks, q, kd_, kd_, vd, vd, do, cos, sin, cos, sin), ("arbitrary",),
        scratch=[pltpu.VMEM((WINDOW, d), ACT), pltpu.VMEM((WINDOW, kd), F32), pltpu.VMEM((WINDOW, kd), F32)], jobs=jobs)


def _proj_res(a, w, b, g, x, name, jobs=()):
    t = a.shape[0]
    k, d = w.shape
    tm = _row_tile(t)
    has_bias = b is not None

    def body(*refs):
        a_ref, w_ref = refs[:2]
        b_ref = refs[2] if has_bias else None
        g_ref, x_ref, m_ref, xo_ref = refs[2 + has_bias:]
        m = _dot(a_ref[...], w_ref[...])
        if has_bias:
            m = m + b_ref[...]
        m_ref[...] = m.astype(ACT)
        xo_ref[...] = x_ref[...] + (m * _rstd(m)) * g_ref[...]

    ins = [a, w] + ([b] if has_bias else []) + [g, x]
    specs = [_rows(tm, k), _full((k, d))] + ([_full((1, d))] if has_bias else []) + [_full((1, d)), _rows(tm, d)]
    return _call(body, name, (t // tm,), specs, [_rows(tm, d), _rows(tm, d)], [_sds((t, d), ACT), _sds((t, d), F32)], ins,
                 ("parallel",), jobs=jobs)


def _post_bwd(dres, m, g, w, a, mode, name, jobs=()):
    t, d = dres.shape
    k = w.shape[0]
    tm = _row_tile(t)
    kc = _pick(k, (1408, 1024, 512))
    steps = t // tm

    def body(*refs):
        d_ref, m_ref, g_ref, w_ref, a_ref = refs[:5]
        da_ref, dw_ref, dg_ref = refs[5:8]
        db_ref, acc_ref = (refs[8] if mode == "attn" else None), refs[-1]
        i = pl.program_id(0)

        @pl.when(i == 0)
        def _():
            for ref in (dg_ref, acc_ref) + ((db_ref,) if mode == "attn" else ()):
                ref[...] = jnp.zeros_like(ref)

        dv, mv = d_ref[...], m_ref[...].astype(F32)
        r = _rstd(mv)
        mh = mv * r
        dg_ref[...] += jnp.sum(dv * mh, axis=0, keepdims=True)
        dm = _rms_bwd(mh, r, g_ref[...], dv)
        dmb = dm.astype(ACT)
        if mode == "attn":
            db_ref[...] += jnp.sum(dm, axis=0, keepdims=True)
        for c in range(0, k, kc):
            da = _dot_nt(dmb, w_ref[c:c + kc, :])
            da_ref[:, c:c + kc] = da.astype(ACT)
        acc_ref[...] += _dot_tn(a_ref[...], dmb)

        @pl.when(i == steps - 1)
        def _():
            dw_ref[...] = acc_ref[...].astype(ACT)

    ins = [dres, m, g, w, a]
    specs = [_rows(tm, d), _rows(tm, d), _full((1, d)), _full((k, d)), _rows(tm, k)]
    out_specs = [_rows(tm, k), _full((k, d)), _full((1, d))]
    out_shape = [_sds((t, k), ACT), _sds((k, d), ACT), _sds((1, d), F32)]
    if mode == "attn":
        out_specs.append(_full((1, d)))
        out_shape.append(_sds((1, d), F32))
    return _call(body, name, (steps,), specs, out_specs, out_shape, ins, ("arbitrary",), scratch=[pltpu.VMEM((k, d), F32)], jobs=jobs)


def _pre_bwd(dy, w, x, g, dres, name, h=None, jobs=()):
    t, d = x.shape
    tm = _row_tile(t)
    steps = t // tm
    n = dy.shape[1]

    def body(*refs):
        dy_ref, w_ref, x_ref, g_ref, dres_ref = refs[:5]
        dx_ref, dg_ref = refs[-4:-2] if h is not None else refs[-2:]
        i = pl.program_id(0)

        @pl.when(i == 0)
        def _():
            dg_ref[...] = jnp.zeros_like(dg_ref)
            if h is not None:
                refs[-1][...] = jnp.zeros_like(refs[-1])

        if w.ndim == 3:
            c = w.shape[2]
            dh = jnp.zeros((tm, d), F32)
            for j in range(w.shape[0]):
                dh = dh + _dot_nt(dy_ref[:, j * c:(j + 1) * c], w_ref[j])
        else:
            dh = _dot(dy_ref[...], w_ref[...])
        xv = x_ref[...]
        r = _rstd(xv)
        xh = xv * r
        dg_ref[...] += jnp.sum(dh * xh, axis=0, keepdims=True)
        dx_ref[...] = dres_ref[...] + _rms_bwd(xh, r, g_ref[...], dh)
        if h is not None:
            h_ref, dw_ref, acc_ref = refs[5], refs[-2], refs[-1]
            acc_ref[...] += _dot_tn(h_ref[...], dy_ref[...])

            @pl.when(i == steps - 1)
            def _():
                for j in range(w.shape[0]):
                    dw_ref[j] = acc_ref[:, j * c:(j + 1) * c].astype(ACT)

    specs = [_rows(tm, n), _resident(w.shape), _rows(tm, d), _full((1, d)), _rows(tm, d)]
    out_specs, out_shape, scratch = [_rows(tm, d), _full((1, d))], [_sds((t, d), F32), _sds((1, d), F32)], []
    if h is not None:
        specs, out_specs, out_shape = specs + [_rows(tm, d)], out_specs + [_full(w.shape)], out_shape + [_sds(w.shape, ACT)]
        scratch = [pltpu.VMEM((d, n), F32)]
    return _call(body, name, (steps,), specs, out_specs, out_shape, (dy, w, x, g, dres) + ((h,) if h is not None else ()),
                 ("arbitrary",), scratch=scratch, jobs=jobs)


def _ffn_bwd(dres, f, g_post, w_dn, gu, w_gu, x, g_pre, name, target=None, jobs=()):
    t, d = x.shape
    n = w_dn.shape[0]
    tm = _row_tile(t, 256)
    whole = target is not None
    n_in = 6 if whole else 8

    def body(*refs):
        if whole:
            t_ref, gp_ref, wd_ref, wu_ref, x_ref, g_ref = refs[:n_in]
            df_ref, dgu_ref, dx_ref, dgp_ref, dg_ref, loss_ref, h_ref, a_ref, gu_ref = refs[n_in:]
        else:
            d_ref, f_ref, gp_ref, wd_ref, gu_ref, wu_ref, x_ref, g_ref = refs[:n_in]
            df_ref, dgu_ref, dx_ref, dgp_ref, dg_ref = refs[n_in:]

        @pl.when(pl.program_id(0) == 0)
        def _():
            for ref in (dgp_ref, dg_ref) + ((loss_ref,) if whole else ()):
                ref[...] = jnp.zeros_like(ref)

        xv = x_ref[...]
        rx = _rstd(xv)
        xh = xv * rx
        if whole:
            hb = (xh * g_ref[...]).astype(ACT)
            h_ref[...] = hb
            for c, size in _mxu_chunks(n):
                gate = _dot_nt(hb, wu_ref[c:c + size, :])
                up = _dot_nt(hb, wu_ref[n + c:n + c + size, :])
                gu_ref[:, c:c + size] = gate.astype(ACT)
                gu_ref[:, n + c:n + c + size] = up.astype(ACT)
                a_ref[:, c:c + size] = (gate * _sigmoid(gate) * up).astype(ACT)
            fv = _dot(a_ref[...], wd_ref[...])
            r = _rstd(fv)
            fh = fv * r
            err = xv + fh * gp_ref[...] - t_ref[...]
            dv = err * (1.0 / d)
            loss_ref[...] += 0.5 * jnp.sum(jnp.mean(err * err, axis=-1, keepdims=True), axis=0, keepdims=True)
        else:
            dv, fv = d_ref[...], f_ref[...].astype(F32)
            r = _rstd(fv)
            fh = fv * r
        dgp_ref[...] += jnp.sum(dv * fh, axis=0, keepdims=True)
        dfb = _rms_bwd(fh, r, gp_ref[...], dv).astype(ACT)
        df_ref[...] = dfb
        for c, size in _mxu_chunks(n):
            da = _dot_nt(dfb, wd_ref[c:c + size, :]).astype(ACT)
            gate, up = gu_ref[:, c:c + size], gu_ref[:, n + c:n + c + size]
            sg = _sigmoid(gate.astype(F32)).astype(ACT)
            gs = gate * sg
            dgu_ref[:, c:c + size] = da * up * (sg + gs - gs * sg)
            dgu_ref[:, n + c:n + c + size] = da * gs
        dh = _dot(dgu_ref[...], wu_ref[...])
        dg_ref[...] += jnp.sum(dh * xh, axis=0, keepdims=True)
        dx_ref[...] = dv + _rms_bwd(xh, rx, g_ref[...], dh)

    w_specs = [_resident(w_dn.shape), _resident(w_gu.shape)]
    out_specs = [_rows(tm, d), _rows(tm, 2 * n), _rows(tm, d), _full((1, d)), _full((1, d))]
    out_shape = [_sds((t, d), ACT), _sds((t, 2 * n), ACT), _sds((t, d), F32), _sds((1, d), F32), _sds((1, d), F32)]
    if whole:
        return _call(
            body, name, (t // tm,),
            [_rows(tm, d), _full((1, d))] + w_specs + [_rows(tm, d), _full((1, d))],
            out_specs + [_full((8, LANES)), _rows(tm, d), _rows(tm, n)],
            out_shape + [_sds((8, LANES), F32), _sds((t, d), ACT), _sds((t, n), ACT)],
            (target, g_post, w_dn, w_gu, x, g_pre), ("arbitrary",), scratch=[pltpu.VMEM((tm, 2 * n), ACT)], jobs=jobs)
    return _call(
        body, name, (t // tm,),
        [_rows(tm, d), _rows(tm, d), _full((1, d)), w_specs[0], _rows(tm, 2 * n), w_specs[1], _rows(tm, d), _full((1, d))],
        out_specs, out_shape, (dres, f, g_post, w_dn, gu, w_gu, x, g_pre), ("arbitrary",), jobs=jobs)


def _wgrad(a, b, name, out_dtype=F32, transposed=False, jobs=()):
    t = a.shape[0]
    tt = _pick(t, (1024,))
    steps = t // tt
    tk = _pick(a.shape[1], (1024, 1408))
    k, n = a.shape[1], b.shape[1]
    tn = _pick(n, (2816, 1024, 1408))
    if transposed:
        out_spec, out_shape, acc_shape = pl.BlockSpec((tn, tk), lambda i, j, s: (j, i)), _sds((n, k), out_dtype), (tn, tk)
    else:
        out_spec, out_shape, acc_shape = pl.BlockSpec((tk, tn), lambda i, j, s: (i, j)), _sds((k, n), out_dtype), (tk, tn)

    def body(a_ref, b_ref, o_ref, acc_ref):
        s = pl.program_id(2)

        @pl.when(s == 0)
        def _():
            acc_ref[...] = jnp.zeros_like(acc_ref)

        acc_ref[...] += _dot_tn(b_ref[...], a_ref[...]) if transposed else _dot_tn(a_ref[...], b_ref[...])

        @pl.when(s == steps - 1)
        def _():
            o_ref[...] = acc_ref[...].astype(out_dtype)

    res, job_res = _call(
        body, name, (k // tk, n // tn, steps),
        [pl.BlockSpec((tt, tk), lambda i, j, s: (s, i)), pl.BlockSpec((tt, tn), lambda i, j, s: (s, j))], [out_spec], [out_shape],
        (a, b), ("parallel", "parallel", "arbitrary"), scratch=[pltpu.VMEM(acc_shape, F32)], jobs=jobs)
    return res[0], job_res


def _ffn_fwd(x, g_pre, w_gu, w_dn, g_post, jobs=()):
    t, d = x.shape
    n = w_dn.shape[0]
    tm = _row_tile(t)

    def body(x_ref, g_ref, wu_ref, wd_ref, gp_ref, h_ref, gu_ref, a_ref, f_ref, xo_ref):
        xv = x_ref[...]
        hb = (xv * _rstd(xv) * g_ref[...]).astype(ACT)
        h_ref[...] = hb
        for c, size in _mxu_chunks(n):
            gate = _dot_nt(hb, wu_ref[c:c + size, :])
            up = _dot_nt(hb, wu_ref[n + c:n + c + size, :])
            gu_ref[:, c:c + size] = gate.astype(ACT)
            gu_ref[:, n + c:n + c + size] = up.astype(ACT)
            a_ref[:, c:c + size] = (gate * _sigmoid(gate) * up).astype(ACT)
        f = _dot(a_ref[...], wd_ref[...])
        f_ref[...] = f.astype(ACT)
        xo_ref[...] = xv + (f * _rstd(f)) * gp_ref[...]

    return _call(
        body, "ffn_fwd", (t // tm,),
        [_rows(tm, d), _full((1, d)), _resident(w_gu.shape), _resident(w_dn.shape), _full((1, d))],
        [_rows(tm, d), _rows(tm, 2 * n), _rows(tm, n), _rows(tm, d), _rows(tm, d)],
        [_sds((t, d), ACT), _sds((t, 2 * n), ACT), _sds((t, n), ACT), _sds((t, d), ACT), _sds((t, d), F32)],
        (x, g_pre, w_gu, w_dn, g_post), ("parallel",), jobs=jobs)


def _causal(ws):
    row = lax.broadcasted_iota(jnp.int32, ws.shape, 0)
    col = lax.broadcasted_iota(jnp.int32, ws.shape, 1)
    return jnp.where(col <= row, ws, 0.0)


def _sgu_ln(v, lg, lb):
    mu = jnp.mean(v, axis=-1, keepdims=True)
    vc = v - mu
    rs = lax.rsqrt(jnp.mean(vc * vc, axis=-1, keepdims=True) + EPS)
    vh = vc * rs
    return vh, rs, vh * lg + lb


def _sgu_fwd(x, g, w, lg, lb, ws, bs_t, jobs=()):
    t, d = x.shape
    nb, _, c = w.shape
    n = nb * c
    groups = d // WINDOW
    tm = _row_tile(t)

    def body(x_ref, g_ref, w_ref, lg_ref, lb_ref, ws_ref, bs_ref, h_ref, z_ref, y_ref, zf_ref):
        xv = x_ref[...]
        hb = (xv * _rstd(xv) * g_ref[...]).astype(ACT)
        h_ref[...] = hb
        for j in range(nb):
            zf_ref[:, j * c:(j + 1) * c] = _dot(hb, w_ref[j])
        z_ref[...] = zf_ref[...].astype(ACT)
        gs = [slice(gi * WINDOW, (gi + 1) * WINDOW) for gi in range(groups)]
        wsg = [_causal(ws_ref[gi]).astype(ACT) for gi in range(groups)]
        for r in range(0, tm, WINDOW):
            u = _gelu(zf_ref[r:r + WINDOW, :d])
            _, _, vn = _sgu_ln(_gelu(zf_ref[r:r + WINDOW, d:]), lg_ref[...], lb_ref[...])
            mixed = [_dot(wsg[gi], vn[:, gs[gi]].astype(ACT)) for gi in range(groups)]
            for gi in range(groups):
                y_ref[r:r + WINDOW, gs[gi]] = (u[:, gs[gi]] * (mixed[gi] + bs_ref[:, gi:gi + 1])).astype(ACT)

    vec = _full((1, d))
    return _call(
        body, "sgu_fwd", (t // tm,),
        [_rows(tm, d), vec, _full((nb, d, c)), vec, vec, _full((groups, WINDOW, WINDOW)), _full((WINDOW, groups))],
        [_rows(tm, d), _rows(tm, n), _rows(tm, d)], [_sds((t, d), ACT), _sds((t, n), ACT), _sds((t, d), ACT)],
        (x, g, w, lg, lb, ws, bs_t), ("parallel",), scratch=[pltpu.VMEM((tm, n), F32)], jobs=jobs)


def _sgu_mix_bwd(z, dy, lg, lb, ws, bs_t, jobs=()):
    t, n = z.shape
    d = n // 2
    groups = d // WINDOW
    tm = _row_tile(t)

    def body(z_ref, dy_ref, lg_ref, lb_ref, ws_ref, bs_ref, dz_ref, dlg_ref, dlb_ref, dws_ref, dbs_ref):
        @pl.when(pl.program_id(0) == 0)
        def _():
            for ref in (dlg_ref, dlb_ref, dws_ref, dbs_ref):
                ref[...] = jnp.zeros_like(ref)

        lane = lax.broadcasted_iota(jnp.int32, (WINDOW, groups), 1)
        gs = [slice(gi * WINDOW, (gi + 1) * WINDOW) for gi in range(groups)]
        wsg = [_causal(ws_ref[gi]).astype(ACT) for gi in range(groups)]
        for c in range(0, tm, WINDOW):
            rows = slice(c, c + WINDOW)
            zu, zv = z_ref[rows, :d].astype(F32), z_ref[rows, d:].astype(F32)
            u = _gelu(zu)
            vh, rs, vn = _sgu_ln(_gelu(zv), lg_ref[...], lb_ref[...])
            dyv = dy_ref[rows, :].astype(F32)
            vng = [vn[:, gs[gi]].astype(ACT) for gi in range(groups)]
            dmix = dyv * u
            dmb = [dmix[:, gs[gi]].astype(ACT) for gi in range(groups)]
            mixed = [_dot(wsg[gi], vng[gi]) for gi in range(groups)]
            dvn_parts = [_dot_tn(wsg[gi], dmb[gi]) for gi in range(groups)]
            dws_parts = [_dot_nt(dmb[gi], vng[gi]) for gi in range(groups)]
            for gi in range(groups):
                dz_ref[rows, gs[gi]] = (dyv[:, gs[gi]] * (mixed[gi] + bs_ref[:, gi:gi + 1]) * _gelu_grad(zu[:, gs[gi]])).astype(ACT)
                dws_ref[:, gs[gi]] += _causal(dws_parts[gi])
                dbs_ref[...] += jnp.where(lane == gi, jnp.sum(dmix[:, gs[gi]], axis=-1, keepdims=True), 0.0)
            dvn = jnp.concatenate(dvn_parts, axis=-1)
            dlg_ref[...] += jnp.sum(dvn * vh, axis=0, keepdims=True)
            dlb_ref[...] += jnp.sum(dvn, axis=0, keepdims=True)
            dvh = dvn * lg_ref[...]
            dv = rs * (dvh - jnp.mean(dvh, axis=-1, keepdims=True) - vh * jnp.mean(dvh * vh, axis=-1, keepdims=True))
            dz_ref[rows, d:] = (dv * _gelu_grad(zv)).astype(ACT)

    vec = _full((1, d))
    return _call(
        body, "sgu_mix_bwd", (t // tm,),
        [_rows(tm, n), _rows(tm, d), vec, vec, _full((groups, WINDOW, WINDOW)), _full((WINDOW, groups))],
        [_rows(tm, n), vec, vec, _full((WINDOW, d)), _full((WINDOW, groups))],
        [_sds((t, n), ACT), _sds((1, d), F32), _sds((1, d), F32), _sds((WINDOW, d), F32), _sds((WINDOW, groups), F32)],
        (z, dy, lg, lb, ws, bs_t), ("arbitrary",), jobs=jobs)


AG_COPIES = 8


def _prepare(sources, dtypes, n_gather, name, ahead=None):
    n = len(sources)
    arrays, where = [], []
    for parts, layer in sources:
        found = []
        for part in parts:
            known = [i for i, v in enumerate(arrays) if v is part]
            if not known:
                arrays.append(part)
            found.append(known[0] if known else len(arrays) - 1)
        where.append((found, layer))
    shapes = [(sum(p.shape[1] for p in parts), parts[0].shape[2]) for parts, _ in sources]
    later = [i for i in range(len(arrays)) if all(i not in found for found, _ in where[:n_gather])]
    n_sems = 6 if ahead is not None else 3

    def body(*refs):
        ins, refs = list(refs[:len(arrays)]), refs[len(arrays):]
        stage, outs = refs[:n], refs[n:n + n_gather]
        sems = refs[len(refs) - n_sems - len(later) - 1:len(refs) - len(later) - 1]
        send, recv, local_sem = sems[:3]
        fetches = []
        for j, i in enumerate(later):
            held = refs[len(refs) - len(later) - 1 + j]
            fetches.append(pltpu.make_async_copy(ins[i], held, refs[-1].at[j]))
            fetches[-1].start()
            ins[i] = held
        x, y, c = _place()
        me, sibling, beside_x, beside_y, far = (x, y, c), (x, y, 1 - c), (1 - x, y, c), (x, 1 - y, c), (1 - x, 1 - y, c)
        slot = lambda dev: 4 * dev[0] + 2 * dev[1] + dev[2]
        other = lambda dev: (dev[0], dev[1], 1 - c)
        whole = lambda a: (0, shapes[a][0])
        cuts = [rows // 2 if rows % 32 == 0 else rows for rows, _ in shapes]
        first = lambda a: (0, cuts[a])
        rest = lambda a: (cuts[a], shapes[a][0] - cuts[a])

        def copy(a, k, block, rows, to, src=None):
            dst = outs[a].at[block, pl.ds(*rows)]
            return pltpu.make_async_remote_copy(
                src_ref=dst if src is None else src, dst_ref=dst, send_sem=send.at[a * AG_COPIES + k],
                recv_sem=recv.at[a * AG_COPIES + k], device_id=to, device_id_type=MESH)

        def cast(a):
            found, layer = where[a]
            at = 0
            for i in found:
                rows = arrays[i].shape[1]
                stage[a][at:at + rows, :] = ins[i][layer].astype(dtypes[a])
                at += rows

        for a in range(n_gather):
            cast(a)
        started, ahead_sends = [], []
        for a in range(n_gather):
            own = pltpu.make_async_copy(stage[a], outs[a].at[slot(me)], local_sem.at[a])
            own.start()
            started.append(own)
        sends = []
        for a in range(n_gather):
            sends += [copy(a, k, slot(me), whole(a), to, src=stage[a]) for k, to in enumerate((sibling, beside_x, beside_y))]
        for cp in sends:
            cp.start()
        for cp in fetches:
            cp.wait()
        for a in range(n_gather, n):
            cast(a)
        if ahead is not None:
            block, pieces = ahead
            out, (send2, recv2, local2) = refs[n + n_gather], sems[3:]
            for i, piece in enumerate(pieces):
                src, dst = stage[block].at[pl.ds(*piece)], out.at[slot(me), pl.ds(*piece)]
                own = pltpu.make_async_copy(src, dst, local2.at[i])
                own.start()
                started.append(own)
                for k, to in enumerate((sibling, beside_x, beside_y)):
                    cp = pltpu.make_async_remote_copy(src_ref=src, dst_ref=dst, send_sem=send2.at[3 * i + k],
                                                      recv_sem=recv2.at[3 * i + k], device_id=to, device_id_type=MESH)
                    cp.start()
                    ahead_sends.append(cp)
        for a in range(n_gather):
            copy(a, 1, slot(beside_x), whole(a), me).wait_recv()
            copy(a, 2, slot(beside_y), whole(a), me).wait_recv()
            passed = [copy(a, 3, slot(beside_x), first(a), beside_y), copy(a, 5, slot(beside_x), whole(a), sibling),
                      copy(a, 6, slot(beside_y), whole(a), sibling)]
            if rest(a)[1]:
                passed.append(copy(a, 4, slot(beside_y), rest(a), beside_x))
            for cp in passed:
                cp.start()
            sends += passed
        for a in range(n_gather):
            copy(a, 3, slot(far), first(a), me).wait_recv()
            if rest(a)[1]:
                copy(a, 4, slot(far), rest(a), me).wait_recv()
            passed = copy(a, 7, slot(far), whole(a), sibling)
            passed.start()
            sends.append(passed)
        for a in range(n_gather):
            for k, source in ((0, me), (5, beside_x), (6, beside_y), (7, far)):
                copy(a, k, slot(other(source)), whole(a), me).wait_recv()
        for cp in sends:
            cp.wait_send()
        for cp in ahead_sends:
            cp.wait()
        for own in started:
            own.wait()

    gathered = list(range(n_gather)) + ([ahead[0]] if ahead is not None else [])
    sems = [pltpu.SemaphoreType.DMA((n_gather * AG_COPIES,)), pltpu.SemaphoreType.DMA((n_gather * AG_COPIES,)),
            pltpu.SemaphoreType.DMA((n_gather,))]
    if ahead is not None:
        sems += [pltpu.SemaphoreType.DMA((3 * len(ahead[1]),)), pltpu.SemaphoreType.DMA((3 * len(ahead[1]),)),
                 pltpu.SemaphoreType.DMA((len(ahead[1]),))]
    res = pl.pallas_call(
        body, name=name,
        in_specs=[ANY if i in later else IN_VMEM for i in range(len(arrays))], out_specs=[IN_VMEM] * n + [ANY] * len(gathered),
        out_shape=[_sds(s, dt) for s, dt in zip(shapes, dtypes)] + [_sds((N_DEV,) + shapes[a], dtypes[a]) for a in gathered],
        scratch_shapes=sems + [pltpu.VMEM(arrays[i].shape, arrays[i].dtype) for i in later] + [pltpu.SemaphoreType.DMA((len(later),))],
        compiler_params=pltpu.CompilerParams(vmem_limit_bytes=VMEM_LIMIT_BYTES),
    )(*arrays)
    return list(res[:n]), list(res[n:])


def _pair_sum(g, r, core, name):
    _, _, rows, cols = g.shape
    tr = _pick(rows, (512, 256, 128))

    def body(core_ref, g_ref, r_ref, p_ref):
        del core_ref
        p_ref[...] = (g_ref[...].astype(F32) + r_ref[...].astype(F32)).astype(p_ref.dtype)

    return pl.pallas_call(
        body, name=name,
        grid_spec=pltpu.PrefetchScalarGridSpec(
            num_scalar_prefetch=1, grid=(4, rows // tr),
            in_specs=[pl.BlockSpec((None, None, tr, cols), lambda q, i, core_ref: (q, core_ref[0], i, 0)),
                      pl.BlockSpec((None, tr, cols), lambda q, i, core_ref: (q, i, 0))],
            out_specs=pl.BlockSpec((None, tr, cols), lambda q, i, core_ref: (q, i, 0))),
        out_shape=_sds((4, rows, cols), g.dtype),
        compiler_params=_params("parallel", "parallel"),
    )(core, g, r)


def _adam(w, g, m, v):
    m2 = ADAM_B1 * m + (1.0 - ADAM_B1) * g
    v2 = ADAM_B2 * v + (1.0 - ADAM_B2) * (g * g)
    m_hat = m2 / (1.0 - ADAM_B1 ** ADAM_STEP)
    v_hat = v2 / (1.0 - ADAM_B2 ** ADAM_STEP)
    return -ADAM_LR * (m_hat / (jnp.sqrt(v_hat) + ADAM_EPS) + ADAM_WD * w), m2, v2


def _shard_update(own, index, received, w, m, v, layer, so_far, name):
    _, rows, cols = w.shape
    n_recv = received.shape[0]
    tr = _shard_tile(rows)

    def body(index_ref, p_ref, q_ref, w_ref, m_ref, v_ref, *rest):
        del index_ref
        g_out, d_out, m_out, v_out = rest[-4:]
        g = p_ref[...].astype(F32)
        for s in range(n_recv):
            g = g + q_ref[s].astype(F32)
        g_out[...] = g
        d_out[...], m_out[...], v_out[...] = _adam(w_ref[...], g, m_ref[...], v_ref[...])

    tile = pl.BlockSpec((None, tr, cols), lambda i, index_ref: (layer, i, 0))
    kept = list(so_far) if so_far is not None else []
    return pl.pallas_call(
        body, name=name,
        grid_spec=pltpu.PrefetchScalarGridSpec(
            num_scalar_prefetch=1, grid=(rows // tr,),
            in_specs=[pl.BlockSpec((None, tr, cols), lambda i, index_ref: (index_ref[0], i, 0)),
                      pl.BlockSpec((n_recv, tr, cols), lambda i, index_ref: (0, i, 0)), tile, tile, tile] + [ANY] * len(kept),
            out_specs=[tile] * 4),
        out_shape=[_sds(w.shape, F32)] * 4,
        input_output_aliases={6 + i: i for i in range(len(kept))},
        compiler_params=_params("parallel"),
    )(index, own, received, w, m, v, *kept)


def _natural_pieces(shape, r, c):
    if tuple(shape[-2:]) == (r, c) and all(n == 1 for n in shape[:-2]):
        return [((0,) * (len(shape) - 2), (0, c))]
    groups, width = shape[1], shape[3]
    assert len(shape) == 4 and shape[0] == 1 and shape[2] == r and groups * width == c, (shape, r, c)
    return [((0, g), (g * width, width)) for g in range(groups)]


def _replicated_update(shares, where, params, name):
    flat = [a for p in params if p is not None for a in p]

    def body(s_ref, *refs):
        ins, outs = refs[:len(flat)], refs[len(flat):]
        total = s_ref[0]
        for dev in range(1, N_DEV):
            total = total + s_ref[dev]
        i_at = o_at = 0
        for ranges, p in zip(where, params):
            chunks = [total[r0:r0 + r, :c] for r0, r, c in ranges]
            g2d = chunks[0] if len(chunks) == 1 else jnp.concatenate(chunks, axis=1)
            if p is None:
                outs[o_at][...] = g2d
                o_at += 1
                continue
            w_ref, m_ref, v_ref = ins[i_at:i_at + 3]
            for idx, (c0, cols) in _natural_pieces(p[0].shape, *g2d.shape):
                at = idx + (slice(None), slice(None))
                g = g2d[:, c0:c0 + cols]
                outs[o_at][at] = g
                outs[o_at + 1][at], outs[o_at + 2][at], outs[o_at + 3][at] = _adam(w_ref[at], g, m_ref[at], v_ref[at])
            i_at, o_at = i_at + 3, o_at + 4

    out_shape = []
    for ranges, p in zip(where, params):
        out_shape += [_sds((ranges[0][1], sum(c for _, _, c in ranges)), F32)] if p is None else [_sds(p[0].shape, F32)] * 4
    res = pl.pallas_call(
        body, name=name, out_shape=out_shape, compiler_params=pltpu.CompilerParams(vmem_limit_bytes=VMEM_LIMIT_BYTES),
    )(shares, *flat)
    out, at = [], 0
    for p in params:
        out.append(list(res[at:at + (1 if p is None else 4)]))
        at += 1 if p is None else 4
    return out


def _rope_tables(t):
    half = HEAD_DIM // 2
    inv_freq = np.float32(ROPE_THETA) ** (-(np.arange(half, dtype=np.float32) * np.float32(2.0)) / np.float32(HEAD_DIM))
    ang = np.arange(t, dtype=np.float32)[:, None] * inv_freq[None, :].astype(np.float32)
    cos, sin = np.cos(ang).astype(np.float32), np.sin(ang).astype(np.float32)
    return (jnp.asarray(np.tile(np.concatenate([cos, cos], axis=1), (1, 2))),
            jnp.asarray(np.tile(np.concatenate([-sin, sin], axis=1), (1, 2))))


def _rows_full(gathered):
    return gathered.reshape(-1, gathered.shape[-1])


def _rows_blocked(full):
    return full.reshape(N_DEV, full.shape[0] // N_DEV, full.shape[1]).astype(ACT)


def _pack_rows(parts, width):
    rows, where, at = [], [], 0
    for v in parts:
        r, c = v.shape
        n_rows = -(-r // 8) * 8
        chunks = []
        for c0 in range(0, c, width):
            chunk = v[:, c0:c0 + width].astype(F32)
            rows.append(jnp.pad(chunk, ((0, n_rows - r), (0, width - chunk.shape[1]))))
            chunks.append((at, r, chunk.shape[1]))
            at += n_rows
        where.append(chunks)
    return jnp.concatenate(rows, axis=0), where


def _halves(block):
    half = block.shape[0] // 2
    return (0, half), (half, half)


def _whole(block):
    return (0, block.shape[0])


EARLY = ("attn_w_qkv", "sgu_ln", "attn_w_o")
LATE = ("sgu_w_in", "sgu_w_out", "gu0", "gu1", "dn0", "dn1")
SWAPPED = ("attn_w_qkv", "ffn_w_gate_up")
BEFORE_ATTN = ("norm_mix_post", "norm_ffn_pre", "norm_ffn_post", "attn_b_o", "sgu_w_spatial", "sgu_b_spatial")
AFTER_ATTN = ("attn_b_qkv", "attn_sinks")
LAST = ("norm_mix_pre",)
WEIGHTS = ("norm_mix_pre", "norm_mix_post", "norm_ffn_pre", "norm_ffn_post", "attn_w_qkv", "attn_b_qkv", "attn_sinks", "attn_w_o",
           "attn_b_o", "sgu_w_in", "sgu_ln_g", "sgu_ln_b", "sgu_w_spatial", "sgu_b_spatial", "sgu_w_out", "ffn_w_gate_up", "ffn_w_down")


LAYER_OF = {"gu0": ("ffn_w_gate_up", 0), "gu1": ("ffn_w_gate_up", 1), "dn0": ("ffn_w_down", 0), "dn1": ("ffn_w_down", 1)}


def _source(tree, name):
    if name == "sgu_ln":
        return [tree["sgu_ln_g"][None], tree["sgu_ln_b"][None]], 0
    leaf, layer = LAYER_OF.get(name, (name, 0))
    return [tree[leaf]], layer


def kernel(x, norm_mix_pre, norm_mix_post, norm_ffn_pre, norm_ffn_post, attn_w_qkv, attn_b_qkv, attn_sinks, attn_w_o, attn_b_o, sgu_w_in, sgu_ln_g, sgu_ln_b, sgu_w_spatial, sgu_b_spatial, sgu_w_out, ffn_w_gate_up, ffn_w_down, loss_target, m_norm_mix_pre, m_norm_mix_post, m_norm_ffn_pre, m_norm_ffn_post, m_attn_w_qkv, m_attn_b_qkv, m_attn_sinks, m_attn_w_o, m_attn_b_o, m_sgu_w_in, m_sgu_ln_g, m_sgu_ln_b, m_sgu_w_spatial, m_sgu_b_spatial, m_sgu_w_out, m_ffn_w_gate_up, m_ffn_w_down, v_norm_mix_pre, v_norm_mix_post, v_norm_ffn_pre, v_norm_ffn_post, v_attn_w_qkv, v_attn_b_qkv, v_attn_sinks, v_attn_w_o, v_attn_b_o, v_sgu_w_in, v_sgu_ln_g, v_sgu_ln_b, v_sgu_w_spatial, v_sgu_b_spatial, v_sgu_w_out, v_ffn_w_gate_up, v_ffn_w_down):
    w = dict(norm_mix_pre=norm_mix_pre, norm_mix_post=norm_mix_post, norm_ffn_pre=norm_ffn_pre, norm_ffn_post=norm_ffn_post,
             attn_w_qkv=attn_w_qkv, attn_b_qkv=attn_b_qkv, attn_sinks=attn_sinks, attn_w_o=attn_w_o, attn_b_o=attn_b_o,
             sgu_w_in=sgu_w_in, sgu_ln_g=sgu_ln_g, sgu_ln_b=sgu_ln_b, sgu_w_spatial=sgu_w_spatial, sgu_b_spatial=sgu_b_spatial,
             sgu_w_out=sgu_w_out, ffn_w_gate_up=ffn_w_gate_up, ffn_w_down=ffn_w_down)
    mom = dict(norm_mix_pre=m_norm_mix_pre, norm_mix_post=m_norm_mix_post, norm_ffn_pre=m_norm_ffn_pre, norm_ffn_post=m_norm_ffn_post,
               attn_w_qkv=m_attn_w_qkv, attn_b_qkv=m_attn_b_qkv, attn_sinks=m_attn_sinks, attn_w_o=m_attn_w_o, attn_b_o=m_attn_b_o,
               sgu_w_in=m_sgu_w_in, sgu_ln_g=m_sgu_ln_g, sgu_ln_b=m_sgu_ln_b, sgu_w_spatial=m_sgu_w_spatial,
               sgu_b_spatial=m_sgu_b_spatial, sgu_w_out=m_sgu_w_out, ffn_w_gate_up=m_ffn_w_gate_up, ffn_w_down=m_ffn_w_down)
    var = dict(norm_mix_pre=v_norm_mix_pre, norm_mix_post=v_norm_mix_post, norm_ffn_pre=v_norm_ffn_pre, norm_ffn_post=v_norm_ffn_post,
               attn_w_qkv=v_attn_w_qkv, attn_b_qkv=v_attn_b_qkv, attn_sinks=v_attn_sinks, attn_w_o=v_attn_w_o, attn_b_o=v_attn_b_o,
               sgu_w_in=v_sgu_w_in, sgu_ln_g=v_sgu_ln_g, sgu_ln_b=v_sgu_ln_b, sgu_w_spatial=v_sgu_w_spatial,
               sgu_b_spatial=v_sgu_b_spatial, sgu_w_out=v_sgu_w_out, ffn_w_gate_up=v_ffn_w_gate_up, ffn_w_down=v_ffn_w_down)
    w, mom, var = [{**tree, **{n: jnp.swapaxes(tree[n], 1, 2) for n in SWAPPED}} for tree in (w, mom, var)]
    xs, target = x[0], loss_target[0]
    t, d = xs.shape
    row = lambda v: v.reshape(1, -1).astype(F32)
    core = lax.axis_index("c").astype(jnp.int32).reshape(1)
    chip = (2 * lax.axis_index("x") + lax.axis_index("y")).astype(jnp.int32).reshape(1)
    names = EARLY + LATE
    gu_rows = w["ffn_w_gate_up"].shape[1] // 4
    gu0_parts = [(i * gu_rows, gu_rows) for i in range(4)]
    staged, (*early, g_gu0) = _prepare([_source(w, n) for n in names], [F32 if n == "sgu_ln" else ACT for n in names], len(EARLY),
                                       "weights_prepare", ahead=(names.index("gu0"), gu0_parts[:2]))
    stage = dict(zip(names, staged))
    w_qkv = _rows_full(early[0])
    lg, lb = row(early[1][:, 0, :]), row(early[1][:, 1, :])
    w_o = _rows_full(early[2])
    b_qkv = row(attn_b_qkv)
    sinks = attn_sinks.reshape(1, -1).astype(F32)
    ws = sgu_w_spatial[0].astype(F32)
    bs_t = sgu_b_spatial[0].astype(F32).T
    cos, sin = _rope_tables(t)
    gather = lambda name, so_far, **pieces: _gather_job(stage[name], so_far, **pieces)
    a_half = lambda name: _halves(stage[name])[0]
    b_half = lambda name: _halves(stage[name])[1]
    whole = lambda name: _whole(stage[name])

    def pieces(name, n):
        rows = stage[name].shape[0] // n
        return [(i * rows, rows) for i in range(n)]

    ways = lambda parts: [(piece, i % 2) for i, piece in enumerate(parts)]
    relay = lambda name, so_far, **steps: _relay_gather_job(stage[name], so_far, **steps)
    gu1_parts = pieces("gu1", 4)
    dn0_parts, dn1_parts, in_parts, out_parts = pieces("dn0", 2), pieces("dn1", 2), pieces("sgu_w_in", 2), pieces("sgu_w_out", 2)
    (h0, q, kdup, vdup), ((g_gu0,),) = _attn_qkv_fwd(
        xs, row(norm_mix_pre[0]), w_qkv, b_qkv, cos, sin, jobs=[relay("gu0", g_gu0, sends=gu0_parts[2:])])
    (o,), ((g_gu0,), (g_dn0,)) = _attn_fwd(q, kdup, vdup, sinks, jobs=[
        relay("gu0", g_gu0, relays=ways(gu0_parts), forwards=gu0_parts), relay("dn0", None, sends=dn0_parts)])
    (m0, x1), ((g_gu0,), (g_dn0,), (g_in,)) = _proj_res(o, w_o, row(attn_b_o), row(norm_mix_post[0]), xs, "attn_out_fwd", jobs=[
        relay("gu0", g_gu0, far=gu0_parts), relay("dn0", g_dn0, relays=ways(dn0_parts), forwards=dn0_parts, late_far=dn0_parts),
        relay("sgu_w_in", None, sends=in_parts)])
    w_gu0, w_dn0 = _rows_full(g_gu0), _rows_full(g_dn0)
    (h1, gu0, a0, f0, x2), ((g_in,), (g_out,), (g_gu1,), (g_dn1,)) = _ffn_fwd(
        x1, row(norm_ffn_pre[0]), w_gu0, w_dn0, row(norm_ffn_post[0]), jobs=[
            relay("sgu_w_in", g_in, relays=ways(in_parts), forwards=in_parts, late_far=in_parts),
            relay("sgu_w_out", None, sends=out_parts), relay("gu1", None, sends=gu1_parts), relay("dn1", None, sends=dn1_parts)])
    w_in = g_in
    (h2, z, y), ((g_out,), (g_gu1,)) = _sgu_fwd(x2, row(norm_mix_pre[1]), w_in, lg, lb, ws, bs_t, jobs=[
        relay("sgu_w_out", g_out, relays=ways(out_parts), forwards=out_parts, late_far=out_parts),
        relay("gu1", g_gu1, relays=ways(gu1_parts), forwards=gu1_parts)])
    w_out = _rows_full(g_out)
    (m1, x3), ((g_gu1,), (g_dn1,)) = _proj_res(y, w_out, None, row(norm_mix_post[1]), x2, "sgu_out_fwd", jobs=[
        relay("gu1", g_gu1, far=gu1_parts),
        relay("dn1", g_dn1, relays=ways(dn1_parts), forwards=dn1_parts, late_far=dn1_parts)])
    w_gu1, w_dn1 = _rows_full(g_gu1), _rows_full(g_dn1)

    to_sibling = lambda g: _scatter_job([g], 4, _to_sibling)
    pair = lambda g, r, name: _pair_sum(g.reshape((4, 2) + g.shape[1:]), r, core, "pair_sum_" + name)
    to_chips = lambda p, prev=None, piece=None: _scatter_job([p], 3, _to_owner_chip, prev=prev, piece=piece)
    out_g, out_d, out_m, out_v = {}, {}, {}, {}

    trees = [{**tree, "sgu_ln": jnp.stack([tree["sgu_ln_g"], tree["sgu_ln_b"]], axis=1)} for tree in (w, mom, var)]
    so_far = {}

    def update(name, own, index, received):
        leaf, layer = LAYER_OF.get(name, (name, 0))
        so_far[leaf] = _shard_update(own, index, received, *[tree[leaf] for tree in trees], layer, so_far.get(leaf), "update_" + name)
        for out, val in zip((out_g, out_d, out_m, out_v), so_far[leaf]):
            out[leaf] = val

    def finish(names, n_extra, shares, where, name):
        res = _replicated_update(shares, where, [(w[n], mom[n], var[n]) for n in names] + [None] * n_extra, name)
        for n, vals in zip(names, res):
            out_g[n], out_d[n], out_m[n], out_v[n] = vals
        return [vals[0] for vals in res[len(names):]]

    first_half = lambda p: _halves(p[0])[0]
    second_half = lambda p: _halves(p[0])[1]
    (df1, dgu1, dx3, dg_ffn_post1, dg_ffn_pre1, loss_tile, h3, a1), _ = _ffn_bwd(
        None, None, row(norm_ffn_post[1]), w_dn1, None, w_gu1, x3, row(norm_ffn_pre[1]), "ffn_last", target=target)
    b_gu1, _ = _wgrad(h3, dgu1, "ffn_up_wgrad1", ACT, transposed=True)
    b_gu1 = _rows_blocked(b_gu1)
    dw_dn1, _ = _wgrad(a1, df1, "ffn_down_wgrad1", ACT)
    b_dn1 = _rows_blocked(dw_dn1)
    (dy, dw_out, dg_mix_post1), ((r_gu1,), (r_dn1,)) = _post_bwd(
        dx3, m1, row(norm_mix_post[1]), w_out, y, "sgu", "sgu_out_bwd", jobs=[to_sibling(b_gu1), to_sibling(b_dn1)])
    p_gu1, p_dn1 = pair(b_gu1, r_gu1, "gu1"), pair(b_dn1, r_dn1, "dn1")
    b_out = _rows_blocked(dw_out)
    (dz, dlg, dlb, dws, dbs_t), ((q_gu1,), (r_out,)) = _sgu_mix_bwd(z, dy, lg, lb, ws, bs_t, jobs=[
        to_chips(p_gu1, piece=first_half(p_gu1)), to_sibling(b_out)])
    p_out = pair(b_out, r_out, "sgu_w_out")
    b_ln = jnp.stack([dlg.reshape(N_DEV, -1), dlb.reshape(N_DEV, -1)], axis=1)
    (dx2, dg_mix_pre1, b_in), _ = _pre_bwd(dz, w_in, x2, row(norm_mix_pre[1]), dx3, "sgu_in_bwd", h=h2)
    (df0, dgu0, dx1, dg_ffn_post0, dg_ffn_pre0), ((q_gu1,), (q_dn1,), (q_out,), (r_in,), (r_ln,)) = _ffn_bwd(
        dx2, f0, row(norm_ffn_post[0]), w_dn0, gu0, w_gu0, x1, row(norm_ffn_pre[0]), "ffn_bwd0",
        jobs=[to_chips(p_gu1, prev=[q_gu1], piece=second_half(p_gu1)), to_chips(p_dn1), to_chips(p_out), to_sibling(b_in),
              to_sibling(b_ln)])
    update("gu1", p_gu1, chip, q_gu1)
    update("dn1", p_dn1, chip, q_dn1)
    update("sgu_w_out", p_out, chip, q_out)
    p_in, p_ln = pair(b_in, r_in, "sgu_w_in"), pair(b_ln, r_ln, "sgu_ln")
    dw_dn0, _ = _wgrad(a0, df0, "ffn_down_wgrad0", ACT)
    b_dn0 = _rows_blocked(dw_dn0)
    b_gu0, ((q_in,), (q_ln,), (r_dn0,)) = _wgrad(h1, dgu0, "ffn_up_wgrad0", ACT, transposed=True, jobs=[
        to_chips(p_in), to_chips(p_ln), to_sibling(b_dn0)])
    update("sgu_w_in", p_in, chip, q_in)
    update("sgu_ln", p_ln, chip, q_ln)
    b_gu0 = _rows_blocked(b_gu0)
    p_dn0 = pair(b_dn0, r_dn0, "dn0")
    (do, dw_o, dg_mix_post0, db_o), ((r_gu0,), (q_dn0,)) = _post_bwd(
        dx1, m0, row(norm_mix_post[0]), w_o, o, "attn", "attn_out_bwd", jobs=[to_sibling(b_gu0), to_chips(p_dn0, piece=first_half(p_dn0))])
    p_gu0 = pair(b_gu0, r_gu0, "gu0")
    b_o = _rows_blocked(dw_o)
    grads = {
        "norm_mix_post": jnp.concatenate([dg_mix_post0, dg_mix_post1], axis=0),
        "norm_ffn_pre": jnp.concatenate([dg_ffn_pre0, dg_ffn_pre1], axis=0),
        "norm_ffn_post": jnp.concatenate([dg_ffn_post0, dg_ffn_post1], axis=0),
        "attn_b_o": db_o,
        "sgu_w_spatial": dws,
        "sgu_b_spatial": dbs_t.T,
    }
    shares1, where1 = _pack_rows([grads[name] for name in BEFORE_ATTN], d)
    (dqkv, db_qkv, dsink), ((q_gu0,), (q_dn0,), (r_o,), (g_shares1,)) = _attn_bwd(q, kdup, vdup, do, sinks, cos, sin, jobs=[
        to_chips(p_gu0), to_chips(p_dn0, prev=[q_dn0], piece=second_half(p_dn0)), to_sibling(b_o),
        _gather_job(shares1, None, sends=[_whole(shares1)])])
    update("gu0", p_gu0, chip, q_gu0)
    update("dn0", p_dn0, chip, q_dn0)
    p_o = pair(b_o, r_o, "attn_w_o")
    grads.update({"attn_b_qkv": db_qkv, "attn_sinks": dsink[:1, :d // HEAD_DIM]})
    shares2, where2 = _pack_rows([grads[name] for name in AFTER_ATTN] + [loss_tile[:1, :1]], d)
    (dx0, dg_mix_pre0), _ = _pre_bwd(dqkv, w_qkv, xs, row(norm_mix_pre[0]), dx1, "attn_qkv_bwd")
    grads["norm_mix_pre"] = jnp.concatenate([dg_mix_pre0, dg_mix_pre1], axis=0)
    shares3, where3 = _pack_rows([grads[name] for name in LAST], d)
    dw_qkv, ((q_o,), (g_shares1,), (g_shares2,), (g_shares3,)) = _wgrad(h0, dqkv, "attn_qkv_wgrad", ACT, transposed=True, jobs=[
        to_chips(p_o), _gather_job(shares1, g_shares1, forwards=[_whole(shares1)]),
        _gather_job(shares2, None, sends=[_whole(shares2)]), _gather_job(shares3, None, sends=[_whole(shares3)])])
    update("attn_w_o", p_o, chip, q_o)
    b_qkv_grad = _rows_blocked(dw_qkv)
    (r_qkv,), (g_shares2,), (g_shares3,) = _copies_only([
        to_sibling(b_qkv_grad), _gather_job(shares2, g_shares2, forwards=[_whole(shares2)]),
        _gather_job(shares3, g_shares3, forwards=[_whole(shares3)])], "grads_tail_to_sibling")
    p_qkv = pair(b_qkv_grad, r_qkv, "attn_w_qkv")
    (q_qkv,), = _copies_only([to_chips(p_qkv)], "grads_tail_to_owner_chip")
    update("attn_w_qkv", p_qkv, chip, q_qkv)

    finish(BEFORE_ATTN, 0, g_shares1, where1, "replicated_update_before_attn")
    loss = finish(AFTER_ATTN, 1, g_shares2, where2, "replicated_update_after_attn")[0][0, 0]
    finish(LAST, 0, g_shares3, where3, "replicated_update_last")

    for out in (out_g, out_d, out_m, out_v):
        out["sgu_ln_g"], out["sgu_ln_b"] = out["sgu_ln"][:, 0, :], out["sgu_ln"][:, 1, :]
        for n in SWAPPED:
            out[n] = jnp.swapaxes(out[n], 1, 2)

    return (loss, dx0[None], *[out_g[n] for n in WEIGHTS], *[out_d[n] for n in WEIGHTS],
            *[out_m[n] for n in WEIGHTS], *[out_v[n] for n in WEIGHTS])
```

```python
import functools
import math
import operator

import jax
import jax.numpy as jnp
import numpy as np
from jax import lax
from jax.experimental import pallas as pl
from jax.experimental.pallas import tpu as pltpu

F32 = jnp.float32
ACT = jnp.bfloat16

EPS = 1e-6
HEAD_DIM = 64
PAIR = 2 * HEAD_DIM
GQA_GROUP = 4
WINDOW = 128
ROPE_THETA = 10000.0
SCALE = HEAD_DIM ** -0.5
MASKED = -1e30
LANES = 128
MXU_WIDTH = 256

ADAM_LR, ADAM_B1, ADAM_B2, ADAM_EPS, ADAM_WD, ADAM_STEP = 0.001, 0.9, 0.999, 1e-08, 0.01, 10

N_DEV = 8
VMEM_LIMIT_BYTES = 56 * 1024 * 1024
MESH = pl.DeviceIdType.MESH
ANY = pl.BlockSpec(memory_space=pl.ANY)
IN_VMEM = pl.BlockSpec(memory_space=pltpu.VMEM)


def _pick(n, candidates):
    for c in candidates:
        if n % c == 0:
            return c
    return n


def _row_tile(t, most=512):
    return _pick(t, (most,))


def _shard_tile(rows):
    return next((c for c in (512, 256, 176, 128, 96, 64) if rows % c == 0 and rows >= 2 * c), rows)


def _mxu_chunks(n, most=4 * MXU_WIDTH):
    assert n % MXU_WIDTH == 0 and most % MXU_WIDTH == 0
    return [(c, min(most, n - c)) for c in range(0, n, most)]


def _params(*sem):
    return pltpu.CompilerParams(dimension_semantics=sem, vmem_limit_bytes=VMEM_LIMIT_BYTES)


def _full(shape):
    return pl.BlockSpec(shape, lambda *_: (0,) * len(shape))


def _resident(shape):
    return pl.BlockSpec(shape, lambda *_: (0,) * len(shape), pipeline_mode=pl.Buffered(1))


def _rows(tm, n):
    return pl.BlockSpec((tm, n), lambda i: (i, 0))


def _sds(shape, dtype):
    return jax.ShapeDtypeStruct(shape, dtype)


def _place():
    return lax.axis_index("x"), lax.axis_index("y"), lax.axis_index("c")


def _other_chips(x, y):
    return [(1 - x, y), (x, 1 - y), (1 - x, 1 - y)]


class _Job:
    def __init__(self, inputs, out_shape, aliases, emit, n_remote, n_local=0, n_late=0):
        self.inputs, self.out_shape, self.aliases, self.emit = list(inputs), list(out_shape), dict(aliases), emit
        self.n_remote, self.n_local, self.n_late = n_remote, n_local, n_late


def _call(body, name, grid, in_specs, out_specs, out_shape, args, sem, scratch=(), jobs=()):
    n_in, n_out, n_scr = len(args), len(out_shape), len(scratch)
    j_in = [a for job in jobs for a in job.inputs]
    j_out = [s for job in jobs for s in job.out_shape]
    aliases, at_in, at_out = {}, n_in, n_out
    for job in jobs:
        aliases.update({at_in + i: at_out + o for i, o in job.aliases.items()})
        at_in, at_out = at_in + len(job.inputs), at_out + len(job.out_shape)
    n_remote = sum(job.n_remote for job in jobs)
    n_local = sum(job.n_local for job in jobs)

    def wrapped(*refs):
        ins, jin = refs[:n_in], refs[n_in:n_in + len(j_in)]
        rest = refs[n_in + len(j_in):]
        outs, jout = rest[:n_out], rest[n_out:n_out + len(j_out)]
        scr = rest[n_out + len(j_out):n_out + len(j_out) + n_scr]
        if not jobs:
            body(*ins, *outs, *scr)
            return
        send, recv, local = rest[n_out + len(j_out) + n_scr:]
        ids = [pl.program_id(a) for a in range(len(grid))]
        first = functools.reduce(operator.and_, [i == 0 for i in ids])
        last = functools.reduce(operator.and_, [i == g - 1 for i, g in zip(ids, grid)])

        def descriptors():
            place, found, i, o, r, l = _place(), ([], []), 0, 0, 0, 0
            for job in jobs:
                emitted = job.emit(jin[i:i + len(job.inputs)], jout[o:o + len(job.out_shape)], place)
                for at, (src, dst, to) in enumerate(emitted):
                    if to is None:
                        cp = pltpu.make_async_copy(src, dst, local.at[l])
                        l += 1
                    else:
                        cp = pltpu.make_async_remote_copy(src_ref=src, dst_ref=dst, send_sem=send.at[r], recv_sem=recv.at[r],
                                                          device_id=to, device_id_type=MESH)
                        r += 1
                    found[at >= len(emitted) - job.n_late].append(cp)
                i, o = i + len(job.inputs), o + len(job.out_shape)
            return found

        @pl.when(first)
        def _():
            for cp in descriptors()[0]:
                cp.start()

        body(*ins, *outs, *scr)

        @pl.when(last)
        def _():
            early, late = descriptors()
            for cp in early:
                cp.wait()
            for cp in late:
                cp.start()
            for cp in late:
                cp.wait()

    sems = [pltpu.SemaphoreType.DMA((n_remote,)), pltpu.SemaphoreType.DMA((n_remote,)),
            pltpu.SemaphoreType.DMA((max(n_local, 1),))] if jobs else []
    res = pl.pallas_call(
        wrapped, name=name, grid=grid,
        in_specs=list(in_specs) + [ANY] * len(j_in), out_specs=list(out_specs) + [ANY] * len(j_out),
        out_shape=list(out_shape) + j_out, scratch_shapes=list(scratch) + sems, input_output_aliases=aliases,
        compiler_params=_params(*(("arbitrary",) * len(grid) if jobs else sem)),
    )(*args, *j_in)
    job_res, at = [], n_out
    for job in jobs:
        job_res.append(list(res[at:at + len(job.out_shape)]))
        at += len(job.out_shape)
    return list(res[:n_out]), job_res


def _copies_only(jobs, name):
    def body(o_ref):
        o_ref[...] = jnp.zeros_like(o_ref)

    return _call(body, name, (1,), [], [_full((8, LANES))], [_sds((8, LANES), F32)], (), ("arbitrary",), jobs=jobs)[1]


def _gather_job(stage, gathered, sends=(), forwards=()):
    shape = _sds((N_DEV,) + stage.shape, stage.dtype)
    inputs = ([stage] if sends else []) + ([gathered] if gathered is not None else [])
    aliases = {len(inputs) - 1: 0} if gathered is not None else {}

    def emit(ins, outs, place):
        x, y, c = place
        sibling, chips, mine, g = (x, y, 1 - c), _other_chips(x, y), 4 * x + 2 * y + c, outs[0]
        found = []
        for r0, nr in sends:
            src, dst = ins[0].at[pl.ds(r0, nr)], g.at[mine, pl.ds(r0, nr)]
            found += [(src, dst, None), (src, dst, sibling)] + [(src, dst, (px, py, c)) for px, py in chips]
        for r0, nr in forwards:
            for px, py in chips:
                block = 4 * px + 2 * py + c
                found.append((ins[-1].at[block, pl.ds(r0, nr)], g.at[block, pl.ds(r0, nr)], sibling))
        return found

    return _Job(inputs, [shape], aliases, emit, 4 * len(sends) + 3 * len(forwards), len(sends))


def _relay_gather_job(stage, gathered, sends=(), relays=(), forwards=(), far=(), late_far=()):
    shape = _sds((N_DEV,) + stage.shape, stage.dtype)
    inputs = ([stage] if sends else []) + ([gathered] if gathered is not None else [])
    aliases = {len(inputs) - 1: 0} if gathered is not None else {}

    def emit(ins, outs, place):
        x, y, c = place
        sibling, beside_x, beside_y, g = (x, y, 1 - c), (1 - x, y, c), (x, 1 - y, c), outs[0]
        slot = lambda dev: 4 * dev[0] + 2 * dev[1] + dev[2]
        found = []
        for r0, nr in sends:
            src, dst = ins[0].at[pl.ds(r0, nr)], g.at[slot((x, y, c)), pl.ds(r0, nr)]
            found += [(src, dst, None), (src, dst, sibling), (src, dst, beside_x), (src, dst, beside_y)]

        def passed_on(block, piece, to):
            return ins[-1].at[block, pl.ds(*piece)], g.at[block, pl.ds(*piece)], to

        for piece, way in relays:
            found.append(passed_on(slot(beside_x), piece, beside_y) if way == 0 else passed_on(slot(beside_y), piece, beside_x))
        for piece in forwards:
            found += [passed_on(slot(beside_x), piece, sibling), passed_on(slot(beside_y), piece, sibling)]
        for piece in tuple(far) + tuple(late_far):
            found.append(passed_on(slot((1 - x, 1 - y, c)), piece, sibling))
        return found

    n_remote = 3 * len(sends) + len(relays) + 2 * len(forwards) + len(far) + len(late_far)
    return _Job(inputs, [shape], aliases, emit, n_remote, len(sends), len(late_far))


def _scatter_job(arrays, n_slots, route, prev=None, piece=None):
    shapes = [_sds((n_slots,) + v.shape[1:], v.dtype) for v in arrays]
    inputs = list(arrays) + (list(prev) if prev else [])
    aliases = {len(arrays) + i: i for i in range(len(arrays))} if prev else {}

    def emit(ins, outs, place):
        found = []
        for a in range(len(arrays)):
            for s in range(n_slots):
                block, to = route(s, *place)
                if piece is None:
                    found.append((ins[a].at[block], outs[a].at[s], to))
                else:
                    found.append((ins[a].at[block, pl.ds(*piece)], outs[a].at[s, pl.ds(*piece)], to))
        return found

    return _Job(inputs, shapes, aliases, emit, len(arrays) * n_slots)


def _to_sibling(s, x, y, c):
    return 2 * s + (1 - c), (x, y, 1 - c)


def _to_owner_chip(s, x, y, c):
    px, py = _other_chips(x, y)[s]
    return 2 * px + py, (px, py, c)


def _rstd(x):
    return lax.rsqrt(jnp.mean(x * x, axis=-1, keepdims=True) + EPS)


def _rms_bwd(xhat, r, g, dy):
    dxh = dy * g
    return r * (dxh - xhat * jnp.mean(dxh * xhat, axis=-1, keepdims=True))


def _first_half(rows):
    lane = lax.broadcasted_iota(jnp.int32, (rows, PAIR), 1)
    return (lane % HEAD_DIM) < (HEAD_DIM // 2)


def _rot_half(v, first):
    return jnp.where(first, pltpu.roll(v, PAIR - HEAD_DIM // 2, 1), pltpu.roll(v, HEAD_DIM // 2, 1))


def _rope(v, cos, sin, first):
    return v * cos + _rot_half(v, first) * sin


def _rope_t(dv, cos, sin, first):
    return dv * cos + _rot_half(dv * sin, first)


def _dot(a, b):
    return jnp.dot(a, b, preferred_element_type=F32)


def _dot_nt(a, b):
    return lax.dot_general(a, b, (((1,), (1,)), ((), ())), preferred_element_type=F32)


def _dot_tn(a, b):
    return lax.dot_general(a, b, (((0,), (0,)), ((), ())), preferred_element_type=F32)


def _sigmoid(v):
    return 1.0 / (1.0 + jnp.exp(-v))


_GELU_K = math.sqrt(2.0 / math.pi)
_GELU_C = 0.044715


def _gelu(v):
    return v * (0.5 * (1.0 + jnp.tanh(_GELU_K * (v + _GELU_C * (v * v * v)))))


def _gelu_grad(v):
    t = jnp.tanh(_GELU_K * (v + _GELU_C * (v * v * v)))
    return 0.5 * (1.0 + t) + 0.5 * v * (1.0 - t * t) * (_GELU_K * (1.0 + 3.0 * _GELU_C * (v * v)))


def _attn_qkv_fwd(x, g, w, b, cos, sin, jobs=()):
    t, d = x.shape
    n = w.shape[0]
    kd = n - d
    assert (kd // 2) % PAIR == 0
    tm = _row_tile(t)
    ch = _pick(d, (512,))

    def body(x_ref, g_ref, w_ref, b_ref, cos_ref, sin_ref, h_ref, q_ref, k_ref, v_ref):
        xv = x_ref[...]
        hb = (xv * _rstd(xv) * g_ref[...]).astype(ACT)
        h_ref[...] = hb
        cosv, sinv = cos_ref[...], sin_ref[...]
        first = _first_half(tm)
        left = _lane_halves()[0]

        def proj(lo, width):
            return _dot_nt(hb, w_ref[lo:lo + width, :]) + b_ref[:, lo:lo + width]

        def twice(ref, s, two_heads):
            swapped = pltpu.roll(two_heads, HEAD_DIM, 1)
            ref[:, 2 * s:2 * s + PAIR] = jnp.where(left, two_heads, swapped).astype(ACT)
            ref[:, 2 * s + PAIR:2 * s + 2 * PAIR] = jnp.where(left, swapped, two_heads).astype(ACT)

        for c in range(0, d, ch):
            y = proj(c, ch)
            for s in range(0, ch, PAIR):
                q_ref[:, c + s:c + s + PAIR] = (_rope(y[:, s:s + PAIR], cosv, sinv, first) * SCALE).astype(ACT)
        y = proj(d, kd)
        for s in range(0, kd // 2, PAIR):
            twice(k_ref, s, _rope(y[:, s:s + PAIR], cosv, sinv, first))
            twice(v_ref, s, y[:, kd // 2 + s:kd // 2 + s + PAIR])

    return _call(
        body, "attn_qkv_fwd", (t // tm,),
        [_rows(tm, d), _full((1, d)), _full((n, d)), _full((1, n)), _rows(tm, PAIR), _rows(tm, PAIR)],
        [_rows(tm, d), _rows(tm, d), _rows(tm, kd), _rows(tm, kd)],
        [_sds((t, d), ACT), _sds((t, d), ACT), _sds((t, kd), ACT), _sds((t, kd), ACT)],
        (x, g, w, b, cos, sin), ("parallel",), jobs=jobs)


STACK = GQA_GROUP * WINDOW


def _band_mask(has_prev):
    row = lax.broadcasted_iota(jnp.int32, (STACK, 2 * WINDOW), 0) % WINDOW
    col = lax.broadcasted_iota(jnp.int32, (STACK, 2 * WINDOW), 1)
    return ((col < WINDOW) & (col > row) & has_prev) | ((col >= WINDOW) & (col - WINDOW <= row))


def _lane_halves():
    lane = lax.broadcasted_iota(jnp.int32, (1, PAIR), 1)
    return lane < HEAD_DIM, lane >= HEAD_DIM


def _stack_heads(ref, h, halves):
    parts = []
    for p in range(2):
        pair = ref[:, (2 * h + p) * PAIR:(2 * h + p + 1) * PAIR]
        parts += [jnp.where(half, pair, jnp.zeros_like(pair)) for half in halves]
    return jnp.concatenate(parts, axis=0)


def _sink_column(sink_ref, h):
    row = lax.broadcasted_iota(jnp.int32, (STACK, 1), 0)
    col = jnp.full((STACK, 1), sink_ref[0, GQA_GROUP * h + GQA_GROUP - 1], F32)
    for j in range(GQA_GROUP - 2, -1, -1):
        col = jnp.where(row < (j + 1) * WINDOW, sink_ref[0, GQA_GROUP * h + j], col)
    return col


def _probs(scores, valid, sink):
    s = jnp.where(valid, scores, MASKED)
    m = jnp.maximum(jnp.max(s, axis=-1, keepdims=True), sink)
    p = jnp.exp(s - m)
    psink = jnp.exp(sink - m)
    inv = 1.0 / (jnp.sum(p, axis=-1, keepdims=True) + psink)
    return p * inv, psink * inv


def _attn_fwd(q, kd_, vd, sinks, jobs=()):
    t, d = q.shape
    kd = kd_.shape[1]
    cur = lambda n: (n, 0)
    prev = lambda n: (jnp.maximum(n - 1, 0), 0)

    def body(sink_ref, q_ref, kc_ref, kp_ref, vc_ref, vp_ref, o_ref):
        valid = _band_mask(pl.program_id(0) > 0)
        halves = _lane_halves()
        heads = range(kd // PAIR)
        hs = [slice(h * PAIR, (h + 1) * PAIR) for h in heads]
        scores = [_dot_nt(_stack_heads(q_ref, h, halves), jnp.concatenate([kp_ref[:, hs[h]], kc_ref[:, hs[h]]], axis=0)) for h in heads]
        probs = [_probs(scores[h], valid, _sink_column(sink_ref, h))[0].astype(ACT) for h in heads]
        for h in heads:
            v2 = jnp.concatenate([vp_ref[:, hs[h]], vc_ref[:, hs[h]]], axis=0)
            vs = jnp.concatenate([jnp.where(half, v2, jnp.zeros_like(v2)) for half in halves], axis=0)
            pr = probs[h]
            for p in range(2):
                both = jnp.concatenate([pr[2 * p * WINDOW:(2 * p + 1) * WINDOW], pr[(2 * p + 1) * WINDOW:(2 * p + 2) * WINDOW]], axis=1)
                o_ref[:, (2 * h + p) * PAIR:(2 * h + p + 1) * PAIR] = _dot(both, vs).astype(ACT)

    kv_c, kv_p = pl.BlockSpec((WINDOW, kd), cur), pl.BlockSpec((WINDOW, kd), prev)
    return _call(
        body, "attn_fwd", (t // WINDOW,),
        [pl.BlockSpec(memory_space=pltpu.SMEM), pl.BlockSpec((WINDOW, d), cur), kv_c, kv_p, kv_c, kv_p],
        [pl.BlockSpec((WINDOW, d), cur)], [_sds((t, d), ACT)],
        (sinks, q, kd_, kd_, vd, vd), ("parallel",), jobs=jobs)


def _attn_bwd(q, kd_, vd, do, sinks, cos, sin, jobs=()):
    t, d = q.shape
    kd = kd_.shape[1]
    nb = t // WINDOW
    n_out = d + kd
    assert (kd // 2) % PAIR == 0
    cur = lambda n: (jnp.minimum(n, nb - 1), 0)
    late = lambda n: (jnp.maximum(n - 1, 0), 0)

    def body(sink_ref, q_ref, kc_ref, vc_ref, do_ref, cosc_ref, sinc_ref,
             out_ref, db_ref, dsink_ref, dq_keep, dk_keep, dv_keep, kp_ref, vp_ref, cosp_ref, sinp_ref):
        n = pl.program_id(0)

        @pl.when(n == 0)
        def _():
            for ref in (db_ref, dsink_ref, dq_keep, dk_keep, dv_keep, kp_ref, vp_ref, cosp_ref, sinp_ref):
                ref[...] = jnp.zeros_like(ref)

        valid = _band_mask(n > 0) & (n < nb)
        halves = _lane_halves()
        first = _first_half(WINDOW)
        slot = lax.broadcasted_iota(jnp.int32, dsink_ref.shape, 1)
        top = lax.broadcasted_iota(jnp.int32, dsink_ref.shape, 0) == 0
        dsink = jnp.zeros(dsink_ref.shape, F32)

        def emit(cols, done):
            out_ref[:, cols] = done.astype(ACT)
            db_ref[:, cols] += jnp.sum(done.astype(F32), axis=0, keepdims=True)

        heads = range(kd // PAIR)
        hs = [slice(h * PAIR, (h + 1) * PAIR) for h in heads]
        k2 = [jnp.concatenate([kp_ref[:, hs[h]], kc_ref[:, hs[h]]], axis=0) for h in heads]
        v2 = [jnp.concatenate([vp_ref[:, hs[h]], vc_ref[:, hs[h]]], axis=0) for h in heads]
        qs = [_stack_heads(q_ref, h, halves) for h in heads]
        dos = [_stack_heads(do_ref, h, halves) for h in heads]
        scores = [_dot_nt(qs[h], k2[h]) for h in heads]
        dps = [_dot_nt(dos[h], v2[h]) for h in heads]
        prs, dss = [], []
        for h in heads:
            pr, psink = _probs(scores[h], valid, _sink_column(sink_ref, h))
            delta = jnp.sum(pr * dps[h], axis=-1, keepdims=True)
            dss.append((pr * (dps[h] - delta)).astype(ACT))
            prs.append(pr.astype(ACT))
            leak = psink * delta
            for j in range(GQA_GROUP):
                share = jnp.sum(leak[j * WINDOW:(j + 1) * WINDOW], axis=0, keepdims=True)
                dsink = dsink - jnp.where(top & (slot == GQA_GROUP * h + j), share, 0.0)
        dqs = [_dot(dss[h], k2[h]) for h in heads]
        dk2 = [_dot_tn(dss[h], qs[h]) for h in heads]
        dv2 = [_dot_tn(prs[h], dos[h]) for h in heads]
        for h in heads:
            for p in range(2):
                ps = slice((2 * h + p) * PAIR, (2 * h + p + 1) * PAIR)
                dqp = jnp.where(halves[0], dqs[h][2 * p * WINDOW:(2 * p + 1) * WINDOW], dqs[h][(2 * p + 1) * WINDOW:(2 * p + 2) * WINDOW])
                emit(ps, dq_keep[:, ps])
                dq_keep[:, ps] = (_rope_t(dqp, cosc_ref[...], sinc_ref[...], first) * SCALE).astype(ACT)
        dks, dvs = [], []
        for h in heads:
            dks.append(dk_keep[:, hs[h]] + _rope_t(dk2[h][:WINDOW], cosp_ref[...], sinp_ref[...], first))
            dk_keep[:, hs[h]] = _rope_t(dk2[h][WINDOW:], cosc_ref[...], sinc_ref[...], first)
            dvs.append(dv_keep[:, hs[h]] + dv2[h][:WINDOW])
            dv_keep[:, hs[h]] = dv2[h][WINDOW:]
        both = lambda v: v + pltpu.roll(v, HEAD_DIM, 1)
        for h in range(0, len(heads), 2):
            at = h // 2 * PAIR
            emit(slice(d + at, d + at + PAIR), jnp.where(halves[0], both(dks[h]), both(dks[h + 1])))
            emit(slice(d + kd // 2 + at, d + kd // 2 + at + PAIR), jnp.where(halves[0], both(dvs[h]), both(dvs[h + 1])))
        dsink_ref[...] += dsink
        kp_ref[...], vp_ref[...], cosp_ref[...], sinp_ref[...] = kc_ref[...], vc_ref[...], cosc_ref[...], sinc_ref[...]

    kv_c, tab_c = pl.BlockSpec((WINDOW, kd), cur), pl.BlockSpec((WINDOW, PAIR), cur)
    return _call(
        body, "attn_bwd", (nb + 1,),
        [pl.BlockSpec(memory_space=pltpu.SMEM), pl.BlockSpec((WINDOW, d), cur), kv_c, kv_c,
         pl.BlockSpec((WINDOW, d), cur), tab_c, tab_c],
        [pl.BlockSpec((WINDOW, n_out), late), _full((1, n_out)), _full((8, LANES))],
        [_sds((t, n_out), ACT), _sds((1, n_out), F32), _sds((8, LANES), F32)],
        (sinks, q, kd_, vd, do, cos, sin), ("arbitrary",),
        scratch=[pltpu.VMEM((WINDOW, d), ACT), pltpu.VMEM((WINDOW, kd), F32), pltpu.VMEM((WINDOW, kd), F32),
                 pltpu.VMEM((WINDOW, kd), ACT), pltpu.VMEM((WINDOW, kd), ACT), pltpu.VMEM((WINDOW, PAIR), F32),
                 pltpu.VMEM((WINDOW, PAIR), F32)], jobs=jobs)


def _proj_res(a, w, b, g, x, name, jobs=()):
    t = a.shape[0]
    k, d = w.shape
    tm = _row_tile(t)
    has_bias = b is not None

    def body(*refs):
        a_ref, w_ref = refs[:2]
        b_ref = refs[2] if has_bias else None
        g_ref, x_ref, m_ref, xo_ref = refs[2 + has_bias:]
        m = _dot(a_ref[...], w_ref[...])
        if has_bias:
            m = m + b_ref[...]
        m_ref[...] = m.astype(ACT)
        xo_ref[...] = x_ref[...] + (m * _rstd(m)) * g_ref[...]

    ins = [a, w] + ([b] if has_bias else []) + [g, x]
    specs = [_rows(tm, k), _full((k, d))] + ([_full((1, d))] if has_bias else []) + [_full((1, d)), _rows(tm, d)]
    return _call(body, name, (t // tm,), specs, [_rows(tm, d), _rows(tm, d)], [_sds((t, d), ACT), _sds((t, d), F32)], ins,
                 ("parallel",), jobs=jobs)


def _post_bwd(dres, m, g, w, a, mode, name, jobs=()):
    t, d = dres.shape
    k = w.shape[0]
    tm = _row_tile(t)
    kc = _pick(k, (1408, 1024, 512))
    steps = t // tm

    def body(*refs):
        d_ref, m_ref, g_ref, w_ref, a_ref = refs[:5]
        da_ref, dw_ref, dg_ref = refs[5:8]
        db_ref, acc_ref = (refs[8] if mode == "attn" else None), refs[-1]
        i = pl.program_id(0)

        @pl.when(i == 0)
        def _():
            for ref in (dg_ref, acc_ref) + ((db_ref,) if mode == "attn" else ()):
                ref[...] = jnp.zeros_like(ref)

        dv, mv = d_ref[...], m_ref[...].astype(F32)
        r = _rstd(mv)
        mh = mv * r
        dg_ref[...] += jnp.sum(dv * mh, axis=0, keepdims=True)
        dm = _rms_bwd(mh, r, g_ref[...], dv)
        dmb = dm.astype(ACT)
        if mode == "attn":
            db_ref[...] += jnp.sum(dm, axis=0, keepdims=True)
        for c in range(0, k, kc):
            da = _dot_nt(dmb, w_ref[c:c + kc, :])
            da_ref[:, c:c + kc] = da.astype(ACT)
        acc_ref[...] += _dot_tn(a_ref[...], dmb)

        @pl.when(i == steps - 1)
        def _():
            dw_ref[...] = acc_ref[...].astype(ACT)

    ins = [dres, m, g, w, a]
    specs = [_rows(tm, d), _rows(tm, d), _full((1, d)), _full((k, d)), _rows(tm, k)]
    out_specs = [_rows(tm, k), _full((k, d)), _full((1, d))]
    out_shape = [_sds((t, k), ACT), _sds((k, d), ACT), _sds((1, d), F32)]
    if mode == "attn":
        out_specs.append(_full((1, d)))
        out_shape.append(_sds((1, d), F32))
    return _call(body, name, (steps,), specs, out_specs, out_shape, ins, ("arbitrary",), scratch=[pltpu.VMEM((k, d), F32)], jobs=jobs)


def _pre_bwd(dy, w, x, g, dres, name, h=None, jobs=()):
    t, d = x.shape
    tm = _row_tile(t)
    steps = t // tm
    n = dy.shape[1]

    def body(*refs):
        dy_ref, w_ref, x_ref, g_ref, dres_ref = refs[:5]
        dx_ref, dg_ref = refs[-4:-2] if h is not None else refs[-2:]
        i = pl.program_id(0)

        @pl.when(i == 0)
        def _():
            dg_ref[...] = jnp.zeros_like(dg_ref)
            if h is not None:
                refs[-1][...] = jnp.zeros_like(refs[-1])

        if w.ndim == 3:
            c = w.shape[2]
            dh = jnp.zeros((tm, d), F32)
            for j in range(w.shape[0]):
                dh = dh + _dot_nt(dy_ref[:, j * c:(j + 1) * c], w_ref[j])
        else:
            dh = _dot(dy_ref[...], w_ref[...])
        xv = x_ref[...]
        r = _rstd(xv)
        xh = xv * r
        dg_ref[...] += jnp.sum(dh * xh, axis=0, keepdims=True)
        dx_ref[...] = dres_ref[...] + _rms_bwd(xh, r, g_ref[...], dh)
        if h is not None:
            h_ref, dw_ref, acc_ref = refs[5], refs[-2], refs[-1]
            acc_ref[...] += _dot_tn(h_ref[...], dy_ref[...])

            @pl.when(i == steps - 1)
            def _():
                for j in range(w.shape[0]):
                    dw_ref[j] = acc_ref[:, j * c:(j + 1) * c].astype(ACT)

    specs = [_rows(tm, n), _resident(w.shape), _rows(tm, d), _full((1, d)), _rows(tm, d)]
    out_specs, out_shape, scratch = [_rows(tm, d), _full((1, d))], [_sds((t, d), F32), _sds((1, d), F32)], []
    if h is not None:
        specs, out_specs, out_shape = specs + [_rows(tm, d)], out_specs + [_full(w.shape)], out_shape + [_sds(w.shape, ACT)]
        scratch = [pltpu.VMEM((d, n), F32)]
    return _call(body, name, (steps,), specs, out_specs, out_shape, (dy, w, x, g, dres) + ((h,) if h is not None else ()),
                 ("arbitrary",), scratch=scratch, jobs=jobs)


def _ffn_bwd(dres, f, g_post, w_dn, gu, w_gu, x, g_pre, name, target=None, jobs=()):
    t, d = x.shape
    n = w_dn.shape[0]
    tm = _row_tile(t, 256)
    whole = target is not None
    n_in = 6 if whole else 8

    def body(*refs):
        if whole:
            t_ref, gp_ref, wd_ref, wu_ref, x_ref, g_ref = refs[:n_in]
            df_ref, dgu_ref, dx_ref, dgp_ref, dg_ref, loss_ref, h_ref, a_ref, gu_ref = refs[n_in:]
        else:
            d_ref, f_ref, gp_ref, wd_ref, gu_ref, wu_ref, x_ref, g_ref = refs[:n_in]
            df_ref, dgu_ref, dx_ref, dgp_ref, dg_ref = refs[n_in:]

        @pl.when(pl.program_id(0) == 0)
        def _():
            for ref in (dgp_ref, dg_ref) + ((loss_ref,) if whole else ()):
                ref[...] = jnp.zeros_like(ref)

        xv = x_ref[...]
        rx = _rstd(xv)
        xh = xv * rx
        if whole:
            hb = (xh * g_ref[...]).astype(ACT)
            h_ref[...] = hb
            for c, size in _mxu_chunks(n):
                gate = _dot_nt(hb, wu_ref[c:c + size, :])
                up = _dot_nt(hb, wu_ref[n + c:n + c + size, :])
                gu_ref[:, c:c + size] = gate.astype(ACT)
                gu_ref[:, n + c:n + c + size] = up.astype(ACT)
                a_ref[:, c:c + size] = (gate * _sigmoid(gate) * up).astype(ACT)
            fv = _dot(a_ref[...], wd_ref[...])
            r = _rstd(fv)
            fh = fv * r
            err = xv + fh * gp_ref[...] - t_ref[...]
            dv = err * (1.0 / d)
            loss_ref[...] += 0.5 * jnp.sum(jnp.mean(err * err, axis=-1, keepdims=True), axis=0, keepdims=True)
        else:
            dv, fv = d_ref[...], f_ref[...].astype(F32)
            r = _rstd(fv)
            fh = fv * r
        dgp_ref[...] += jnp.sum(dv * fh, axis=0, keepdims=True)
        dfb = _rms_bwd(fh, r, gp_ref[...], dv).astype(ACT)
        df_ref[...] = dfb
        for c, size in _mxu_chunks(n):
            da = _dot_nt(dfb, wd_ref[c:c + size, :]).astype(ACT)
            gate, up = gu_ref[:, c:c + size], gu_ref[:, n + c:n + c + size]
            sg = _sigmoid(gate.astype(F32)).astype(ACT)
            gs = gate * sg
            dgu_ref[:, c:c + size] = da * up * (sg + gs - gs * sg)
            dgu_ref[:, n + c:n + c + size] = da * gs
        dh = _dot(dgu_ref[...], wu_ref[...])
        dg_ref[...] += jnp.sum(dh * xh, axis=0, keepdims=True)
        dx_ref[...] = dv + _rms_bwd(xh, rx, g_ref[...], dh)

    w_specs = [_resident(w_dn.shape), _resident(w_gu.shape)]
    out_specs = [_rows(tm, d), _rows(tm, 2 * n), _rows(tm, d), _full((1, d)), _full((1, d))]
    out_shape = [_sds((t, d), ACT), _sds((t, 2 * n), ACT), _sds((t, d), F32), _sds((1, d), F32), _sds((1, d), F32)]
    if whole:
        return _call(
            body, name, (t // tm,),
            [_rows(tm, d), _full((1, d))] + w_specs + [_rows(tm, d), _full((1, d))],
            out_specs + [_full((8, LANES)), _rows(tm, d), _rows(tm, n)],
            out_shape + [_sds((8, LANES), F32), _sds((t, d), ACT), _sds((t, n), ACT)],
            (target, g_post, w_dn, w_gu, x, g_pre), ("arbitrary",), scratch=[pltpu.VMEM((tm, 2 * n), ACT)], jobs=jobs)
    return _call(
        body, name, (t // tm,),
        [_rows(tm, d), _rows(tm, d), _full((1, d)), w_specs[0], _rows(tm, 2 * n), w_specs[1], _rows(tm, d), _full((1, d))],
        out_specs, out_shape, (dres, f, g_post, w_dn, gu, w_gu, x, g_pre), ("arbitrary",), jobs=jobs)


def _wgrad(a, b, name, out_dtype=F32, transposed=False, jobs=()):
    t = a.shape[0]
    tt = _pick(t, (1024,))
    steps = t // tt
    tk = _pick(a.shape[1], (1024, 1408))
    k, n = a.shape[1], b.shape[1]
    tn = _pick(n, (2816, 1024, 1408))
    if transposed:
        out_spec, out_shape, acc_shape = pl.BlockSpec((tn, tk), lambda i, j, s: (j, i)), _sds((n, k), out_dtype), (tn, tk)
    else:
        out_spec, out_shape, acc_shape = pl.BlockSpec((tk, tn), lambda i, j, s: (i, j)), _sds((k, n), out_dtype), (tk, tn)

    def body(a_ref, b_ref, o_ref, acc_ref):
        s = pl.program_id(2)

        @pl.when(s == 0)
        def _():
            acc_ref[...] = jnp.zeros_like(acc_ref)

        acc_ref[...] += _dot_tn(b_ref[...], a_ref[...]) if transposed else _dot_tn(a_ref[...], b_ref[...])

        @pl.when(s == steps - 1)
        def _():
            o_ref[...] = acc_ref[...].astype(out_dtype)

    res, job_res = _call(
        body, name, (k // tk, n // tn, steps),
        [pl.BlockSpec((tt, tk), lambda i, j, s: (s, i)), pl.BlockSpec((tt, tn), lambda i, j, s: (s, j))], [out_spec], [out_shape],
        (a, b), ("parallel", "parallel", "arbitrary"), scratch=[pltpu.VMEM(acc_shape, F32)], jobs=jobs)
    return res[0], job_res


def _ffn_fwd(x, g_pre, w_gu, w_dn, g_post, jobs=()):
    t, d = x.shape
    n = w_dn.shape[0]
    tm = _row_tile(t)

    def body(x_ref, g_ref, wu_ref, wd_ref, gp_ref, h_ref, gu_ref, a_ref, f_ref, xo_ref):
        xv = x_ref[...]
        hb = (xv * _rstd(xv) * g_ref[...]).astype(ACT)
        h_ref[...] = hb
        for c, size in _mxu_chunks(n):
            gate = _dot_nt(hb, wu_ref[c:c + size, :])
            up = _dot_nt(hb, wu_ref[n + c:n + c + size, :])
            gu_ref[:, c:c + size] = gate.astype(ACT)
            gu_ref[:, n + c:n + c + size] = up.astype(ACT)
            a_ref[:, c:c + size] = (gate * _sigmoid(gate) * up).astype(ACT)
        f = _dot(a_ref[...], wd_ref[...])
        f_ref[...] = f.astype(ACT)
        xo_ref[...] = xv + (f * _rstd(f)) * gp_ref[...]

    return _call(
        body, "ffn_fwd", (t // tm,),
        [_rows(tm, d), _full((1, d)), _resident(w_gu.shape), _resident(w_dn.shape), _full((1, d))],
        [_rows(tm, d), _rows(tm, 2 * n), _rows(tm, n), _rows(tm, d), _rows(tm, d)],
        [_sds((t, d), ACT), _sds((t, 2 * n), ACT), _sds((t, n), ACT), _sds((t, d), ACT), _sds((t, d), F32)],
        (x, g_pre, w_gu, w_dn, g_post), ("parallel",), jobs=jobs)


def _causal(ws):
    row = lax.broadcasted_iota(jnp.int32, ws.shape, 0)
    col = lax.broadcasted_iota(jnp.int32, ws.shape, 1)
    return jnp.where(col <= row, ws, 0.0)


def _sgu_ln(v, lg, lb):
    mu = jnp.mean(v, axis=-1, keepdims=True)
    vc = v - mu
    rs = lax.rsqrt(jnp.mean(vc * vc, axis=-1, keepdims=True) + EPS)
    vh = vc * rs
    return vh, rs, vh * lg + lb


def _sgu_fwd(x, g, w, lg, lb, ws, bs_t, jobs=()):
    t, d = x.shape
    nb, _, c = w.shape
    n = nb * c
    groups = d // WINDOW
    tm = _row_tile(t)

    def body(x_ref, g_ref, w_ref, lg_ref, lb_ref, ws_ref, bs_ref, h_ref, z_ref, y_ref, zf_ref):
        xv = x_ref[...]
        hb = (xv * _rstd(xv) * g_ref[...]).astype(ACT)
        h_ref[...] = hb
        for j in range(nb):
            zf_ref[:, j * c:(j + 1) * c] = _dot(hb, w_ref[j])
        z_ref[...] = zf_ref[...].astype(ACT)
        gs = [slice(gi * WINDOW, (gi + 1) * WINDOW) for gi in range(groups)]
        wsg = [_causal(ws_ref[gi]).astype(ACT) for gi in range(groups)]
        for r in range(0, tm, WINDOW):
            u = _gelu(zf_ref[r:r + WINDOW, :d])
            _, _, vn = _sgu_ln(_gelu(zf_ref[r:r + WINDOW, d:]), lg_ref[...], lb_ref[...])
            mixed = [_dot(wsg[gi], vn[:, gs[gi]].astype(ACT)) for gi in range(groups)]
            for gi in range(groups):
                y_ref[r:r + WINDOW, gs[gi]] = (u[:, gs[gi]] * (mixed[gi] + bs_ref[:, gi:gi + 1])).astype(ACT)

    vec = _full((1, d))
    return _call(
        body, "sgu_fwd", (t // tm,),
        [_rows(tm, d), vec, _full((nb, d, c)), vec, vec, _full((groups, WINDOW, WINDOW)), _full((WINDOW, groups))],
        [_rows(tm, d), _rows(tm, n), _rows(tm, d)], [_sds((t, d), ACT), _sds((t, n), ACT), _sds((t, d), ACT)],
        (x, g, w, lg, lb, ws, bs_t), ("parallel",), scratch=[pltpu.VMEM((tm, n), F32)], jobs=jobs)


def _sgu_mix_bwd(z, dy, lg, lb, ws, bs_t, jobs=()):
    t, n = z.shape
    d = n // 2
    groups = d // WINDOW
    tm = _row_tile(t)

    def body(z_ref, dy_ref, lg_ref, lb_ref, ws_ref, bs_ref, dz_ref, dlg_ref, dlb_ref, dws_ref, dbs_ref):
        @pl.when(pl.program_id(0) == 0)
        def _():
            for ref in (dlg_ref, dlb_ref, dws_ref, dbs_ref):
                ref[...] = jnp.zeros_like(ref)

        lane = lax.broadcasted_iota(jnp.int32, (WINDOW, groups), 1)
        gs = [slice(gi * WINDOW, (gi + 1) * WINDOW) for gi in range(groups)]
        wsg = [_causal(ws_ref[gi]).astype(ACT) for gi in range(groups)]
        for c in range(0, tm, WINDOW):
            rows = slice(c, c + WINDOW)
            zu, zv = z_ref[rows, :d].astype(F32), z_ref[rows, d:].astype(F32)
            u = _gelu(zu)
            vh, rs, vn = _sgu_ln(_gelu(zv), lg_ref[...], lb_ref[...])
            dyv = dy_ref[rows, :].astype(F32)
            vng = [vn[:, gs[gi]].astype(ACT) for gi in range(groups)]
            dmix = dyv * u
            dmb = [dmix[:, gs[gi]].astype(ACT) for gi in range(groups)]
            mixed = [_dot(wsg[gi], vng[gi]) for gi in range(groups)]
            dvn_parts = [_dot_tn(wsg[gi], dmb[gi]) for gi in range(groups)]
            dws_parts = [_dot_nt(dmb[gi], vng[gi]) for gi in range(groups)]
            for gi in range(groups):
                dz_ref[rows, gs[gi]] = (dyv[:, gs[gi]] * (mixed[gi] + bs_ref[:, gi:gi + 1]) * _gelu_grad(zu[:, gs[gi]])).astype(ACT)
                dws_ref[:, gs[gi]] += _causal(dws_parts[gi])
                dbs_ref[...] += jnp.where(lane == gi, jnp.sum(dmix[:, gs[gi]], axis=-1, keepdims=True), 0.0)
            dvn = jnp.concatenate(dvn_parts, axis=-1)
            dlg_ref[...] += jnp.sum(dvn * vh, axis=0, keepdims=True)
            dlb_ref[...] += jnp.sum(dvn, axis=0, keepdims=True)
            dvh = dvn * lg_ref[...]
            dv = rs * (dvh - jnp.mean(dvh, axis=-1, keepdims=True) - vh * jnp.mean(dvh * vh, axis=-1, keepdims=True))
            dz_ref[rows, d:] = (dv * _gelu_grad(zv)).astype(ACT)

    vec = _full((1, d))
    return _call(
        body, "sgu_mix_bwd", (t // tm,),
        [_rows(tm, n), _rows(tm, d), vec, vec, _full((groups, WINDOW, WINDOW)), _full((WINDOW, groups))],
        [_rows(tm, n), vec, vec, _full((WINDOW, d)), _full((WINDOW, groups))],
        [_sds((t, n), ACT), _sds((1, d), F32), _sds((1, d), F32), _sds((WINDOW, d), F32), _sds((WINDOW, groups), F32)],
        (z, dy, lg, lb, ws, bs_t), ("arbitrary",), jobs=jobs)


AG_COPIES = 8


def _prepare(sources, dtypes, n_gather, name, ahead=None):
    n = len(sources)
    arrays, where = [], []
    for parts, layer in sources:
        found = []
        for part in parts:
            known = [i for i, v in enumerate(arrays) if v is part]
            if not known:
                arrays.append(part)
            found.append(known[0] if known else len(arrays) - 1)
        where.append((found, layer))
    shapes = [(sum(p.shape[1] for p in parts), parts[0].shape[2]) for parts, _ in sources]
    later = [i for i in range(len(arrays)) if all(i not in found for found, _ in where[:n_gather])]
    n_sems = 6 if ahead is not None else 3

    def body(*refs):
        ins, refs = list(refs[:len(arrays)]), refs[len(arrays):]
        stage, outs = refs[:n], refs[n:n + n_gather]
        sems = refs[len(refs) - n_sems - len(later) - 1:len(refs) - len(later) - 1]
        send, recv, local_sem = sems[:3]
        fetches = []
        for j, i in enumerate(later):
            held = refs[len(refs) - len(later) - 1 + j]
            fetches.append(pltpu.make_async_copy(ins[i], held, refs[-1].at[j]))
            fetches[-1].start()
            ins[i] = held
        x, y, c = _place()
        me, sibling, beside_x, beside_y, far = (x, y, c), (x, y, 1 - c), (1 - x, y, c), (x, 1 - y, c), (1 - x, 1 - y, c)
        slot = lambda dev: 4 * dev[0] + 2 * dev[1] + dev[2]
        other = lambda dev: (dev[0], dev[1], 1 - c)
        whole = lambda a: (0, shapes[a][0])
        cuts = [rows // 2 if rows % 32 == 0 else rows for rows, _ in shapes]
        first = lambda a: (0, cuts[a])
        rest = lambda a: (cuts[a], shapes[a][0] - cuts[a])

        def copy(a, k, block, rows, to, src=None):
            dst = outs[a].at[block, pl.ds(*rows)]
            return pltpu.make_async_remote_copy(
                src_ref=dst if src is None else src, dst_ref=dst, send_sem=send.at[a * AG_COPIES + k],
                recv_sem=recv.at[a * AG_COPIES + k], device_id=to, device_id_type=MESH)

        def cast(a):
            found, layer = where[a]
            at = 0
            for i in found:
                rows = arrays[i].shape[1]
                stage[a][at:at + rows, :] = ins[i][layer].astype(dtypes[a])
                at += rows

        for a in range(n_gather):
            cast(a)
        started, ahead_sends = [], []
        for a in range(n_gather):
            own = pltpu.make_async_copy(stage[a], outs[a].at[slot(me)], local_sem.at[a])
            own.start()
            started.append(own)
        sends = []
        for a in range(n_gather):
            sends += [copy(a, k, slot(me), whole(a), to, src=stage[a]) for k, to in enumerate((sibling, beside_x, beside_y))]
        for cp in sends:
            cp.start()
        for cp in fetches:
            cp.wait()
        for a in range(n_gather, n):
            cast(a)
        if ahead is not None:
            block, pieces = ahead
            out, (send2, recv2, local2) = refs[n + n_gather], sems[3:]
            for i, piece in enumerate(pieces):
                src, dst = stage[block].at[pl.ds(*piece)], out.at[slot(me), pl.ds(*piece)]
                own = pltpu.make_async_copy(src, dst, local2.at[i])
                own.start()
                started.append(own)
                for k, to in enumerate((sibling, beside_x, beside_y)):
                    cp = pltpu.make_async_remote_copy(src_ref=src, dst_ref=dst, send_sem=send2.at[3 * i + k],
                                                      recv_sem=recv2.at[3 * i + k], device_id=to, device_id_type=MESH)
                    cp.start()
                    ahead_sends.append(cp)
        for a in range(n_gather):
            copy(a, 1, slot(beside_x), whole(a), me).wait_recv()
            copy(a, 2, slot(beside_y), whole(a), me).wait_recv()
            passed = [copy(a, 3, slot(beside_x), first(a), beside_y), copy(a, 5, slot(beside_x), whole(a), sibling),
                      copy(a, 6, slot(beside_y), whole(a), sibling)]
            if rest(a)[1]:
                passed.append(copy(a, 4, slot(beside_y), rest(a), beside_x))
            for cp in passed:
                cp.start()
            sends += passed
        for a in range(n_gather):
            copy(a, 3, slot(far), first(a), me).wait_recv()
            if rest(a)[1]:
                copy(a, 4, slot(far), rest(a), me).wait_recv()
            passed = copy(a, 7, slot(far), whole(a), sibling)
            passed.start()
            sends.append(passed)
        for a in range(n_gather):
            for k, source in ((0, me), (5, beside_x), (6, beside_y), (7, far)):
                copy(a, k, slot(other(source)), whole(a), me).wait_recv()
        for cp in sends:
            cp.wait_send()
        for cp in ahead_sends:
            cp.wait()
        for own in started:
            own.wait()

    gathered = list(range(n_gather)) + ([ahead[0]] if ahead is not None else [])
    sems = [pltpu.SemaphoreType.DMA((n_gather * AG_COPIES,)), pltpu.SemaphoreType.DMA((n_gather * AG_COPIES,)),
            pltpu.SemaphoreType.DMA((n_gather,))]
    if ahead is not None:
        sems += [pltpu.SemaphoreType.DMA((3 * len(ahead[1]),)), pltpu.SemaphoreType.DMA((3 * len(ahead[1]),)),
                 pltpu.SemaphoreType.DMA((len(ahead[1]),))]
    res = pl.pallas_call(
        body, name=name,
        in_specs=[ANY if i in later else IN_VMEM for i in range(len(arrays))], out_specs=[IN_VMEM] * n + [ANY] * len(gathered),
        out_shape=[_sds(s, dt) for s, dt in zip(shapes, dtypes)] + [_sds((N_DEV,) + shapes[a], dtypes[a]) for a in gathered],
        scratch_shapes=sems + [pltpu.VMEM(arrays[i].shape, arrays[i].dtype) for i in later] + [pltpu.SemaphoreType.DMA((len(later),))],
        compiler_params=pltpu.CompilerParams(vmem_limit_bytes=VMEM_LIMIT_BYTES),
    )(*arrays)
    return list(res[:n]), list(res[n:])


def _pair_sum(g, r, core, name):
    _, _, rows, cols = g.shape
    tr = _pick(rows, (512, 256, 128))

    def body(core_ref, g_ref, r_ref, p_ref):
        del core_ref
        p_ref[...] = (g_ref[...].astype(F32) + r_ref[...].astype(F32)).astype(p_ref.dtype)

    return pl.pallas_call(
        body, name=name,
        grid_spec=pltpu.PrefetchScalarGridSpec(
            num_scalar_prefetch=1, grid=(4, rows // tr),
            in_specs=[pl.BlockSpec((None, None, tr, cols), lambda q, i, core_ref: (q, core_ref[0], i, 0)),
                      pl.BlockSpec((None, tr, cols), lambda q, i, core_ref: (q, i, 0))],
            out_specs=pl.BlockSpec((None, tr, cols), lambda q, i, core_ref: (q, i, 0))),
        out_shape=_sds((4, rows, cols), g.dtype),
        compiler_params=_params("parallel", "parallel"),
    )(core, g, r)


def _adam(w, g, m, v):
    m2 = ADAM_B1 * m + (1.0 - ADAM_B1) * g
    v2 = ADAM_B2 * v + (1.0 - ADAM_B2) * (g * g)
    m_hat = m2 / (1.0 - ADAM_B1 ** ADAM_STEP)
    v_hat = v2 / (1.0 - ADAM_B2 ** ADAM_STEP)
    return -ADAM_LR * (m_hat / (jnp.sqrt(v_hat) + ADAM_EPS) + ADAM_WD * w), m2, v2


def _shard_update(own, index, received, w, m, v, layer, so_far, name):
    _, rows, cols = w.shape
    n_recv = received.shape[0]
    tr = _shard_tile(rows)

    def body(index_ref, p_ref, q_ref, w_ref, m_ref, v_ref, *rest):
        del index_ref
        g_out, d_out, m_out, v_out = rest[-4:]
        g = p_ref[...].astype(F32)
        for s in range(n_recv):
            g = g + q_ref[s].astype(F32)
        g_out[...] = g
        d_out[...], m_out[...], v_out[...] = _adam(w_ref[...], g, m_ref[...], v_ref[...])

    tile = pl.BlockSpec((None, tr, cols), lambda i, index_ref: (layer, i, 0))
    kept = list(so_far) if so_far is not None else []
    return pl.pallas_call(
        body, name=name,
        grid_spec=pltpu.PrefetchScalarGridSpec(
            num_scalar_prefetch=1, grid=(rows // tr,),
            in_specs=[pl.BlockSpec((None, tr, cols), lambda i, index_ref: (index_ref[0], i, 0)),
                      pl.BlockSpec((n_recv, tr, cols), lambda i, index_ref: (0, i, 0)), tile, tile, tile] + [ANY] * len(kept),
            out_specs=[tile] * 4),
        out_shape=[_sds(w.shape, F32)] * 4,
        input_output_aliases={6 + i: i for i in range(len(kept))},
        compiler_params=_params("parallel"),
    )(index, own, received, w, m, v, *kept)


def _natural_pieces(shape, r, c):
    if tuple(shape[-2:]) == (r, c) and all(n == 1 for n in shape[:-2]):
        return [((0,) * (len(shape) - 2), (0, c))]
    groups, width = shape[1], shape[3]
    assert len(shape) == 4 and shape[0] == 1 and shape[2] == r and groups * width == c, (shape, r, c)
    return [((0, g), (g * width, width)) for g in range(groups)]


def _replicated_update(shares, where, params, name):
    flat = [a for p in params if p is not None for a in p]

    def body(s_ref, *refs):
        ins, outs = refs[:len(flat)], refs[len(flat):]
        total = s_ref[0]
        for dev in range(1, N_DEV):
            total = total + s_ref[dev]
        i_at = o_at = 0
        for ranges, p in zip(where, params):
            chunks = [total[r0:r0 + r, :c] for r0, r, c in ranges]
            g2d = chunks[0] if len(chunks) == 1 else jnp.concatenate(chunks, axis=1)
            if p is None:
                outs[o_at][...] = g2d
                o_at += 1
                continue
            w_ref, m_ref, v_ref = ins[i_at:i_at + 3]
            for idx, (c0, cols) in _natural_pieces(p[0].shape, *g2d.shape):
                at = idx + (slice(None), slice(None))
                g = g2d[:, c0:c0 + cols]
                outs[o_at][at] = g
                outs[o_at + 1][at], outs[o_at + 2][at], outs[o_at + 3][at] = _adam(w_ref[at], g, m_ref[at], v_ref[at])
            i_at, o_at = i_at + 3, o_at + 4

    out_shape = []
    for ranges, p in zip(where, params):
        out_shape += [_sds((ranges[0][1], sum(c for _, _, c in ranges)), F32)] if p is None else [_sds(p[0].shape, F32)] * 4
    res = pl.pallas_call(
        body, name=name, out_shape=out_shape, compiler_params=pltpu.CompilerParams(vmem_limit_bytes=VMEM_LIMIT_BYTES),
    )(shares, *flat)
    out, at = [], 0
    for p in params:
        out.append(list(res[at:at + (1 if p is None else 4)]))
        at += 1 if p is None else 4
    return out


def _rope_tables(t):
    half = HEAD_DIM // 2
    inv_freq = np.float32(ROPE_THETA) ** (-(np.arange(half, dtype=np.float32) * np.float32(2.0)) / np.float32(HEAD_DIM))
    ang = np.arange(t, dtype=np.float32)[:, None] * inv_freq[None, :].astype(np.float32)
    cos, sin = np.cos(ang).astype(np.float32), np.sin(ang).astype(np.float32)
    return (jnp.asarray(np.tile(np.concatenate([cos, cos], axis=1), (1, 2))),
            jnp.asarray(np.tile(np.concatenate([-sin, sin], axis=1), (1, 2))))


def _rows_full(gathered):
    return gathered.reshape(-1, gathered.shape[-1])


def _rows_blocked(full):
    return full.reshape(N_DEV, full.shape[0] // N_DEV, full.shape[1]).astype(ACT)


def _pack_rows(parts, width):
    rows, where, at = [], [], 0
    for v in parts:
        r, c = v.shape
        n_rows = -(-r // 8) * 8
        chunks = []
        for c0 in range(0, c, width):
            chunk = v[:, c0:c0 + width].astype(F32)
            rows.append(jnp.pad(chunk, ((0, n_rows - r), (0, width - chunk.shape[1]))))
            chunks.append((at, r, chunk.shape[1]))
            at += n_rows
        where.append(chunks)
    return jnp.concatenate(rows, axis=0), where


def _halves(block):
    half = block.shape[0] // 2
    return (0, half), (half, half)


def _whole(block):
    return (0, block.shape[0])


EARLY = ("attn_w_qkv", "sgu_ln", "attn_w_o")
LATE = ("sgu_w_in", "sgu_w_out", "gu0", "gu1", "dn0", "dn1")
SWAPPED = ("attn_w_qkv", "ffn_w_gate_up")
BEFORE_ATTN = ("norm_mix_post", "norm_ffn_pre", "norm_ffn_post", "attn_b_o", "sgu_w_spatial", "sgu_b_spatial")
AFTER_ATTN = ("attn_b_qkv", "attn_sinks")
LAST = ("norm_mix_pre",)
WEIGHTS = ("norm_mix_pre", "norm_mix_post", "norm_ffn_pre", "norm_ffn_post", "attn_w_qkv", "attn_b_qkv", "attn_sinks", "attn_w_o",
           "attn_b_o", "sgu_w_in", "sgu_ln_g", "sgu_ln_b", "sgu_w_spatial", "sgu_b_spatial", "sgu_w_out", "ffn_w_gate_up", "ffn_w_down")


LAYER_OF = {"gu0": ("ffn_w_gate_up", 0), "gu1": ("ffn_w_gate_up", 1), "dn0": ("ffn_w_down", 0), "dn1": ("ffn_w_down", 1)}


def _source(tree, name):
    if name == "sgu_ln":
        return [tree["sgu_ln_g"][None], tree["sgu_ln_b"][None]], 0
    leaf, layer = LAYER_OF.get(name, (name, 0))
    return [tree[leaf]], layer


def kernel(x, norm_mix_pre, norm_mix_post, norm_ffn_pre, norm_ffn_post, attn_w_qkv, attn_b_qkv, attn_sinks, attn_w_o, attn_b_o, sgu_w_in, sgu_ln_g, sgu_ln_b, sgu_w_spatial, sgu_b_spatial, sgu_w_out, ffn_w_gate_up, ffn_w_down, loss_target, m_norm_mix_pre, m_norm_mix_post, m_norm_ffn_pre, m_norm_ffn_post, m_attn_w_qkv, m_attn_b_qkv, m_attn_sinks, m_attn_w_o, m_attn_b_o, m_sgu_w_in, m_sgu_ln_g, m_sgu_ln_b, m_sgu_w_spatial, m_sgu_b_spatial, m_sgu_w_out, m_ffn_w_gate_up, m_ffn_w_down, v_norm_mix_pre, v_norm_mix_post, v_norm_ffn_pre, v_norm_ffn_post, v_attn_w_qkv, v_attn_b_qkv, v_attn_sinks, v_attn_w_o, v_attn_b_o, v_sgu_w_in, v_sgu_ln_g, v_sgu_ln_b, v_sgu_w_spatial, v_sgu_b_spatial, v_sgu_w_out, v_ffn_w_gate_up, v_ffn_w_down):
    w = dict(norm_mix_pre=norm_mix_pre, norm_mix_post=norm_mix_post, norm_ffn_pre=norm_ffn_pre, norm_ffn_post=norm_ffn_post,
             attn_w_qkv=attn_w_qkv, attn_b_qkv=attn_b_qkv, attn_sinks=attn_sinks, attn_w_o=attn_w_o, attn_b_o=attn_b_o,
             sgu_w_in=sgu_w_in, sgu_ln_g=sgu_ln_g, sgu_ln_b=sgu_ln_b, sgu_w_spatial=sgu_w_spatial, sgu_b_spatial=sgu_b_spatial,
             sgu_w_out=sgu_w_out, ffn_w_gate_up=ffn_w_gate_up, ffn_w_down=ffn_w_down)
    mom = dict(norm_mix_pre=m_norm_mix_pre, norm_mix_post=m_norm_mix_post, norm_ffn_pre=m_norm_ffn_pre, norm_ffn_post=m_norm_ffn_post,
               attn_w_qkv=m_attn_w_qkv, attn_b_qkv=m_attn_b_qkv, attn_sinks=m_attn_sinks, attn_w_o=m_attn_w_o, attn_b_o=m_attn_b_o,
               sgu_w_in=m_sgu_w_in, sgu_ln_g=m_sgu_ln_g, sgu_ln_b=m_sgu_ln_b, sgu_w_spatial=m_sgu_w_spatial,
               sgu_b_spatial=m_sgu_b_spatial, sgu_w_out=m_sgu_w_out, ffn_w_gate_up=m_ffn_w_gate_up, ffn_w_down=m_ffn_w_down)
    var = dict(norm_mix_pre=v_norm_mix_pre, norm_mix_post=v_norm_mix_post, norm_ffn_pre=v_norm_ffn_pre, norm_ffn_post=v_norm_ffn_post,
               attn_w_qkv=v_attn_w_qkv, attn_b_qkv=v_attn_b_qkv, attn_sinks=v_attn_sinks, attn_w_o=v_attn_w_o, attn_b_o=v_attn_b_o,
               sgu_w_in=v_sgu_w_in, sgu_ln_g=v_sgu_ln_g, sgu_ln_b=v_sgu_ln_b, sgu_w_spatial=v_sgu_w_spatial,
               sgu_b_spatial=v_sgu_b_spatial, sgu_w_out=v_sgu_w_out, ffn_w_gate_up=v_ffn_w_gate_up, ffn_w_down=v_ffn_w_down)
    w, mom, var = [{**tree, **{n: jnp.swapaxes(tree[n], 1, 2) for n in SWAPPED}} for tree in (w, mom, var)]
    xs, target = x[0], loss_target[0]
    t, d = xs.shape
    row = lambda v: v.reshape(1, -1).astype(F32)
    core = lax.axis_index("c").astype(jnp.int32).reshape(1)
    chip = (2 * lax.axis_index("x") + lax.axis_index("y")).astype(jnp.int32).reshape(1)
    names = EARLY + LATE
    gu_rows = w["ffn_w_gate_up"].shape[1] // 4
    gu0_parts = [(i * gu_rows, gu_rows) for i in range(4)]
    staged, (*early, g_gu0) = _prepare([_source(w, n) for n in names], [F32 if n == "sgu_ln" else ACT for n in names], len(EARLY),
                                       "weights_prepare", ahead=(names.index("gu0"), gu0_parts[:2]))
    stage = dict(zip(names, staged))
    w_qkv = _rows_full(early[0])
    lg, lb = row(early[1][:, 0, :]), row(early[1][:, 1, :])
    w_o = _rows_full(early[2])
    b_qkv = row(attn_b_qkv)
    sinks = attn_sinks.reshape(1, -1).astype(F32)
    ws = sgu_w_spatial[0].astype(F32)
    bs_t = sgu_b_spatial[0].astype(F32).T
    cos, sin = _rope_tables(t)
    gather = lambda name, so_far, **pieces: _gather_job(stage[name], so_far, **pieces)
    a_half = lambda name: _halves(stage[name])[0]
    b_half = lambda name: _halves(stage[name])[1]
    whole = lambda name: _whole(stage[name])

    def pieces(name, n):
        rows = stage[name].shape[0] // n
        return [(i * rows, rows) for i in range(n)]

    ways = lambda parts: [(piece, i % 2) for i, piece in enumerate(parts)]
    relay = lambda name, so_far, **steps: _relay_gather_job(stage[name], so_far, **steps)
    gu1_parts = pieces("gu1", 4)
    dn0_parts, dn1_parts, in_parts, out_parts = pieces("dn0", 2), pieces("dn1", 2), pieces("sgu_w_in", 2), pieces("sgu_w_out", 2)
    (h0, q, kdup, vdup), ((g_gu0,),) = _attn_qkv_fwd(
        xs, row(norm_mix_pre[0]), w_qkv, b_qkv, cos, sin, jobs=[relay("gu0", g_gu0, sends=gu0_parts[2:])])
    (o,), ((g_gu0,), (g_dn0,)) = _attn_fwd(q, kdup, vdup, sinks, jobs=[
        relay("gu0", g_gu0, relays=ways(gu0_parts), forwards=gu0_parts), relay("dn0", None, sends=dn0_parts)])
    (m0, x1), ((g_gu0,), (g_dn0,), (g_in,)) = _proj_res(o, w_o, row(attn_b_o), row(norm_mix_post[0]), xs, "attn_out_fwd", jobs=[
        relay("gu0", g_gu0, far=gu0_parts), relay("dn0", g_dn0, relays=ways(dn0_parts), forwards=dn0_parts, late_far=dn0_parts),
        relay("sgu_w_in", None, sends=in_parts)])
    w_gu0, w_dn0 = _rows_full(g_gu0), _rows_full(g_dn0)
    (h1, gu0, a0, f0, x2), ((g_in,), (g_out,), (g_gu1,), (g_dn1,)) = _ffn_fwd(
        x1, row(norm_ffn_pre[0]), w_gu0, w_dn0, row(norm_ffn_post[0]), jobs=[
            relay("sgu_w_in", g_in, relays=ways(in_parts), forwards=in_parts, late_far=in_parts),
            relay("sgu_w_out", None, sends=out_parts), relay("gu1", None, sends=gu1_parts), relay("dn1", None, sends=dn1_parts)])
    w_in = g_in
    (h2, z, y), ((g_out,), (g_gu1,), (g_dn1,)) = _sgu_fwd(x2, row(norm_mix_pre[1]), w_in, lg, lb, ws, bs_t, jobs=[
        relay("sgu_w_out", g_out, relays=ways(out_parts), forwards=out_parts, late_far=out_parts),
        relay("gu1", g_gu1, relays=ways(gu1_parts), forwards=gu1_parts),
        relay("dn1", g_dn1, relays=ways(dn1_parts), forwards=dn1_parts)])
    w_out = _rows_full(g_out)
    (m1, x3), ((g_gu1,), (g_dn1,)) = _proj_res(y, w_out, None, row(norm_mix_post[1]), x2, "sgu_out_fwd", jobs=[
        relay("gu1", g_gu1, far=gu1_parts), relay("dn1", g_dn1, far=dn1_parts)])
    w_gu1, w_dn1 = _rows_full(g_gu1), _rows_full(g_dn1)

    to_sibling = lambda g: _scatter_job([g], 4, _to_sibling)
    pair = lambda g, r, name: _pair_sum(g.reshape((4, 2) + g.shape[1:]), r, core, "pair_sum_" + name)
    to_chips = lambda p, prev=None, piece=None: _scatter_job([p], 3, _to_owner_chip, prev=prev, piece=piece)
    out_g, out_d, out_m, out_v = {}, {}, {}, {}

    trees = [{**tree, "sgu_ln": jnp.stack([tree["sgu_ln_g"], tree["sgu_ln_b"]], axis=1)} for tree in (w, mom, var)]
    so_far = {}

    def update(name, own, index, received):
        leaf, layer = LAYER_OF.get(name, (name, 0))
        so_far[leaf] = _shard_update(own, index, received, *[tree[leaf] for tree in trees], layer, so_far.get(leaf), "update_" + name)
        for out, val in zip((out_g, out_d, out_m, out_v), so_far[leaf]):
            out[leaf] = val

    def finish(names, n_extra, shares, where, name):
        res = _replicated_update(shares, where, [(w[n], mom[n], var[n]) for n in names] + [None] * n_extra, name)
        for n, vals in zip(names, res):
            out_g[n], out_d[n], out_m[n], out_v[n] = vals
        return [vals[0] for vals in res[len(names):]]

    first_half = lambda p: _halves(p[0])[0]
    second_half = lambda p: _halves(p[0])[1]
    (df1, dgu1, dx3, dg_ffn_post1, dg_ffn_pre1, loss_tile, h3, a1), _ = _ffn_bwd(
        None, None, row(norm_ffn_post[1]), w_dn1, None, w_gu1, x3, row(norm_ffn_pre[1]), "ffn_last", target=target)
    b_gu1, _ = _wgrad(h3, dgu1, "ffn_up_wgrad1", ACT, transposed=True)
    b_gu1 = _rows_blocked(b_gu1)
    dw_dn1, _ = _wgrad(a1, df1, "ffn_down_wgrad1", ACT)
    b_dn1 = _rows_blocked(dw_dn1)
    (dy, dw_out, dg_mix_post1), ((r_gu1,), (r_dn1,)) = _post_bwd(
        dx3, m1, row(norm_mix_post[1]), w_out, y, "sgu", "sgu_out_bwd", jobs=[to_sibling(b_gu1), to_sibling(b_dn1)])
    p_gu1, p_dn1 = pair(b_gu1, r_gu1, "gu1"), pair(b_dn1, r_dn1, "dn1")
    b_out = _rows_blocked(dw_out)
    (dz, dlg, dlb, dws, dbs_t), ((q_gu1,), (r_out,)) = _sgu_mix_bwd(z, dy, lg, lb, ws, bs_t, jobs=[
        to_chips(p_gu1, piece=first_half(p_gu1)), to_sibling(b_out)])
    p_out = pair(b_out, r_out, "sgu_w_out")
    b_ln = jnp.stack([dlg.reshape(N_DEV, -1), dlb.reshape(N_DEV, -1)], axis=1)
    (dx2, dg_mix_pre1, b_in), _ = _pre_bwd(dz, w_in, x2, row(norm_mix_pre[1]), dx3, "sgu_in_bwd", h=h2)
    (df0, dgu0, dx1, dg_ffn_post0, dg_ffn_pre0), ((q_gu1,), (q_dn1,), (q_out,), (r_in,), (r_ln,)) = _ffn_bwd(
        dx2, f0, row(norm_ffn_post[0]), w_dn0, gu0, w_gu0, x1, row(norm_ffn_pre[0]), "ffn_bwd0",
        jobs=[to_chips(p_gu1, prev=[q_gu1], piece=second_half(p_gu1)), to_chips(p_dn1), to_chips(p_out), to_sibling(b_in),
              to_sibling(b_ln)])
    update("gu1", p_gu1, chip, q_gu1)
    update("dn1", p_dn1, chip, q_dn1)
    update("sgu_w_out", p_out, chip, q_out)
    p_in, p_ln = pair(b_in, r_in, "sgu_w_in"), pair(b_ln, r_ln, "sgu_ln")
    dw_dn0, _ = _wgrad(a0, df0, "ffn_down_wgrad0", ACT)
    b_dn0 = _rows_blocked(dw_dn0)
    b_gu0, ((q_in,), (q_ln,), (r_dn0,)) = _wgrad(h1, dgu0, "ffn_up_wgrad0", ACT, transposed=True, jobs=[
        to_chips(p_in), to_chips(p_ln), to_sibling(b_dn0)])
    update("sgu_w_in", p_in, chip, q_in)
    update("sgu_ln", p_ln, chip, q_ln)
    b_gu0 = _rows_blocked(b_gu0)
    p_dn0 = pair(b_dn0, r_dn0, "dn0")
    (do, dw_o, dg_mix_post0, db_o), ((r_gu0,), (q_dn0,)) = _post_bwd(
        dx1, m0, row(norm_mix_post[0]), w_o, o, "attn", "attn_out_bwd", jobs=[to_sibling(b_gu0), to_chips(p_dn0, piece=first_half(p_dn0))])
    p_gu0 = pair(b_gu0, r_gu0, "gu0")
    b_o = _rows_blocked(dw_o)
    grads = {
        "norm_mix_post": jnp.concatenate([dg_mix_post0, dg_mix_post1], axis=0),
        "norm_ffn_pre": jnp.concatenate([dg_ffn_pre0, dg_ffn_pre1], axis=0),
        "norm_ffn_post": jnp.concatenate([dg_ffn_post0, dg_ffn_post1], axis=0),
        "attn_b_o": db_o,
        "sgu_w_spatial": dws,
        "sgu_b_spatial": dbs_t.T,
    }
    shares1, where1 = _pack_rows([grads[name] for name in BEFORE_ATTN], d)
    (dqkv, db_qkv, dsink), ((q_gu0,), (q_dn0,), (r_o,), (g_shares1,)) = _attn_bwd(q, kdup, vdup, do, sinks, cos, sin, jobs=[
        to_chips(p_gu0), to_chips(p_dn0, prev=[q_dn0], piece=second_half(p_dn0)), to_sibling(b_o),
        _gather_job(shares1, None, sends=[_whole(shares1)])])
    update("gu0", p_gu0, chip, q_gu0)
    update("dn0", p_dn0, chip, q_dn0)
    p_o = pair(b_o, r_o, "attn_w_o")
    grads.update({"attn_b_qkv": db_qkv, "attn_sinks": dsink[:1, :d // HEAD_DIM]})
    shares2, where2 = _pack_rows([grads[name] for name in AFTER_ATTN] + [loss_tile[:1, :1]], d)
    (dx0, dg_mix_pre0), _ = _pre_bwd(dqkv, w_qkv, xs, row(norm_mix_pre[0]), dx1, "attn_qkv_bwd")
    grads["norm_mix_pre"] = jnp.concatenate([dg_mix_pre0, dg_mix_pre1], axis=0)
    shares3, where3 = _pack_rows([grads[name] for name in LAST], d)
    dw_qkv, ((q_o,), (g_shares1,), (g_shares2,), (g_shares3,)) = _wgrad(h0, dqkv, "attn_qkv_wgrad", ACT, transposed=True, jobs=[
        to_chips(p_o), _gather_job(shares1, g_shares1, forwards=[_whole(shares1)]),
        _gather_job(shares2, None, sends=[_whole(shares2)]), _gather_job(shares3, None, sends=[_whole(shares3)])])
    update("attn_w_o", p_o, chip, q_o)
    b_qkv_grad = _rows_blocked(dw_qkv)
    (r_qkv,), (g_shares2,), (g_shares3,) = _copies_only([
        to_sibling(b_qkv_grad), _gather_job(shares2, g_shares2, forwards=[_whole(shares2)]),
        _gather_job(shares3, g_shares3, forwards=[_whole(shares3)])], "grads_tail_to_sibling")
    p_qkv = pair(b_qkv_grad, r_qkv, "attn_w_qkv")
    (q_qkv,), = _copies_only([to_chips(p_qkv)], "grads_tail_to_owner_chip")
    update("attn_w_qkv", p_qkv, chip, q_qkv)

    finish(BEFORE_ATTN, 0, g_shares1, where1, "replicated_update_before_attn")
    loss = finish(AFTER_ATTN, 1, g_shares2, where2, "replicated_update_after_attn")[0][0, 0]
    finish(LAST, 0, g_shares3, where3, "replicated_update_last")

    for out in (out_g, out_d, out_m, out_v):
        out["sgu_ln_g"], out["sgu_ln_b"] = out["sgu_ln"][:, 0, :], out["sgu_ln"][:, 1, :]
        for n in SWAPPED:
            out[n] = jnp.swapaxes(out[n], 1, 2)

    return (loss, dx0[None], *[out_g[n] for n in WEIGHTS], *[out_d[n] for n in WEIGHTS],
            *[out_m[n] for n in WEIGHTS], *[out_v[n] for n in WEIGHTS])
```
